```python
import math
import jax, jax.numpy as jnp
from jax import lax
import numpy as np

D_MODEL = 1024
BATCH = 8
SEQ = 4096
DEPTH = 4

D_CONV = D_MODEL // 2
CONV_WIDTH = 3
D_SSM = D_MODEL // 2
SSM_GROUP = 16
N_SSM_GROUPS = D_SSM // SSM_GROUP
SSM_STATE = 64
D_POOL = D_MODEL // 2
POOL_WINDOWS = (2, 4, 8, 16)
POOL_GROUP = D_POOL // len(POOL_WINDOWS)
D_SGU = D_MODEL // 2
SGU_HEADS = 4
SGU_HEAD_DIM = D_SGU // SGU_HEADS
CHUNK = 128
D_FF = ((8 * D_MODEL // 3 + 127) // 128) * 128
N_EVEN = (DEPTH + 1) // 2
N_ODD = DEPTH // 2
EPS = 1e-6

kernel_name = "hybrid_conv_s5_pool_sgu_trunk"


def rmsnorm(x, g):
    xf = x.astype(jnp.float32)
    y = xf * lax.rsqrt(jnp.mean(xf * xf, axis=-1, keepdims=True) + EPS)
    return (y * g.astype(jnp.float32)).astype(x.dtype)


def causal_dwconv(x, w):
    L = x.shape[1]
    K = w.shape[0]
    xp = jnp.pad(x, ((0, 0), (K - 1, 0), (0, 0)))
    y = xp[:, 0:L] * w[0]
    for k in range(1, K):
        y = y + xp[:, k:k + L] * w[k]
    return y


def short_conv_mixer(xa, ba, ca, conv_w):
    return ba * causal_dwconv(ca * xa, conv_w)


def s5_mixer(u, log_step, a_re, a_im, b_re, b_im, c_re, c_im, d_skip, glu_w, glu_b):
    f32 = jnp.float32
    Bsz, L, _ = u.shape
    uf = u.astype(f32).reshape(Bsz, L, N_SSM_GROUPS, SSM_GROUP)
    lam = lax.complex(a_re.astype(f32), a_im.astype(f32))
    step = jnp.exp(log_step.astype(f32))[:, None]
    lam_bar = jnp.exp(lam * step)
    b_tilde = lax.complex(b_re.astype(f32), b_im.astype(f32))
    b_bar = ((lam_bar - 1.0) / lam)[..., None] * b_tilde
    bu = jnp.einsum('gph,blgh->blgp', b_bar, uf.astype(jnp.complex64))
    a_elems = jnp.broadcast_to(lam_bar, bu.shape)

    def combine(left, right):
        a_l, b_l = left
        a_r, b_r = right
        return a_r * a_l, a_r * b_l + b_r

    _, states = lax.associative_scan(combine, (a_elems, bu), axis=1)
    c_tilde = lax.complex(c_re.astype(f32), c_im.astype(f32))
    y = jnp.real(jnp.einsum('ghp,blgp->blgh', c_tilde, states))
    y = y + d_skip.astype(f32).reshape(N_SSM_GROUPS, SSM_GROUP) * uf
    y = jax.nn.gelu(y.reshape(Bsz, L, D_SSM))
    y = y * jax.nn.sigmoid(y @ glu_w.astype(f32) + glu_b.astype(f32))
    return y.astype(u.dtype)


def pool_mixer(z, pool_w, pool_scale):
    f32 = jnp.float32
    Bsz, L, _ = z.shape
    zf = z.astype(f32).reshape(Bsz, L, len(POOL_WINDOWS), POOL_GROUP)
    csum = lax.cumsum(zf, axis=1)
    count = jnp.arange(1, L + 1, dtype=f32)[None, :, None]
    outs = []
    for g, w in enumerate(POOL_WINDOWS):
        s = csum[:, :, g]
        lower = jnp.pad(s, ((0, 0), (w, 0), (0, 0)))[:, :L]
        mean = (s - lower) / jnp.minimum(count, w)
        outs.append(mean - zf[:, :, g])
    pooled = jnp.stack(outs, axis=2)
    y = jnp.einsum('blgc,gcd->blgd', pooled, pool_w.astype(f32)).reshape(Bsz, L, D_POOL)
    return (y * pool_scale.astype(f32)).astype(z.dtype)


def sgu_mixer(su, sv, norm_g, sgu_w, sgu_b):
    Bsz, L, _ = su.shape
    v = rmsnorm(sv, norm_g)
    vr = v.reshape(Bsz, L // CHUNK, CHUNK, SGU_HEADS, SGU_HEAD_DIM)
    mask = jnp.tril(jnp.ones((CHUNK, CHUNK), dtype=bool))
    w_s = jnp.where(mask, sgu_w, 0)
    mixed = jnp.einsum('hts,bnshd->bnthd', w_s, vr) + jnp.swapaxes(sgu_b, 0, 1)[:, :, None]
    return su * mixed.reshape(Bsz, L, D_SGU)


def conv_ffn(h, w_up, conv_w, conv_b, w_down):
    up = causal_dwconv(h @ w_up, conv_w) + conv_b
    g, v = jnp.split(up, 2, axis=-1)
    return (jax.nn.silu(g) * v) @ w_down


def _fwd_setup_inputs(seed: int = 0) -> dict:
    key = jax.random.key(seed)
    ks = jax.random.split(key, 32)
    f32 = jnp.float32

    def nrm(k, shape, scale):
        return jax.random.normal(k, shape, f32) * scale

    G, P, Hg = N_SSM_GROUPS, SSM_STATE, SSM_GROUP
    d_even_in = 3 * D_CONV + D_SSM
    d_odd_in = D_POOL + 2 * D_SGU
    a_im_base = jnp.pi * jnp.arange(P, dtype=f32)
    return {
        "x": nrm(ks[0], (BATCH, SEQ, D_MODEL), 1.0),
        "norm_mix_g": 1.0 + nrm(ks[1], (DEPTH, D_MODEL), 0.05),
        "even_w_in": nrm(ks[2], (N_EVEN, D_MODEL, d_even_in), D_MODEL ** -0.5),
        "even_conv_w": nrm(ks[3], (N_EVEN, CONV_WIDTH, D_CONV), CONV_WIDTH ** -0.5),
        "ssm_log_step": jax.random.uniform(ks[4], (N_EVEN, G), f32, math.log(1e-3), math.log(1e-1)),
        "ssm_a_re": -0.5 * (1.0 + nrm(ks[5], (N_EVEN, G, P), 0.01)),
        "ssm_a_im": a_im_base + nrm(ks[6], (N_EVEN, G, P), 0.01),
        "ssm_b_re": nrm(ks[7], (N_EVEN, G, P, Hg), (2 * Hg) ** -0.5),
        "ssm_b_im": nrm(ks[8], (N_EVEN, G, P, Hg), (2 * Hg) ** -0.5),
        "ssm_c_re": nrm(ks[9], (N_EVEN, G, Hg, P), (2 * P) ** -0.5),
        "ssm_c_im": nrm(ks[10], (N_EVEN, G, Hg, P), (2 * P) ** -0.5),
        "ssm_d": nrm(ks[11], (N_EVEN, D_SSM), 1.0),
        "ssm_glu_w": nrm(ks[12], (N_EVEN, D_SSM, D_SSM), D_SSM ** -0.5),
        "ssm_glu_b": nrm(ks[13], (N_EVEN, D_SSM), 0.02),
        "even_w_out": nrm(ks[14], (N_EVEN, D_CONV + D_SSM, D_MODEL), (D_CONV + D_SSM) ** -0.5),
        "odd_w_in": nrm(ks[15], (N_ODD, D_MODEL, d_odd_in), D_MODEL ** -0.5),
        "pool_w": nrm(ks[16], (N_ODD, len(POOL_WINDOWS), POOL_GROUP, POOL_GROUP), POOL_GROUP ** -0.5),
        "pool_scale": 1.0 + nrm(ks[17], (N_ODD, D_POOL), 0.1),
        "sgu_norm_g": 1.0 + nrm(ks[18], (N_ODD, D_SGU), 0.05),
        "sgu_w": nrm(ks[19], (N_ODD, SGU_HEADS, CHUNK, CHUNK), CHUNK ** -0.5),
        "sgu_b": 1.0 + nrm(ks[20], (N_ODD, SGU_HEADS, CHUNK), 0.1),
        "odd_w_out": nrm(ks[21], (N_ODD, D_POOL + D_SGU, D_MODEL), (D_POOL + D_SGU) ** -0.5),
        "norm_ffn_g": 1.0 + nrm(ks[22], (DEPTH, D_MODEL), 0.05),
        "ffn_w_up": nrm(ks[23], (DEPTH, D_MODEL, 2 * D_FF), D_MODEL ** -0.5),
        "ffn_conv_w": nrm(ks[24], (DEPTH, CONV_WIDTH, 2 * D_FF), CONV_WIDTH ** -0.5),
        "ffn_conv_b": nrm(ks[25], (DEPTH, 2 * D_FF), 0.02),
        "ffn_w_down": nrm(ks[26], (DEPTH, D_FF, D_MODEL), D_FF ** -0.5),
        "norm_final_g": 1.0 + nrm(ks[27], (D_MODEL,), 0.05),
    }


def _fwd_reference(x, norm_mix_g, even_w_in, even_conv_w, ssm_log_step, ssm_a_re, ssm_a_im,
              ssm_b_re, ssm_b_im, ssm_c_re, ssm_c_im, ssm_d, ssm_glu_w, ssm_glu_b,
              even_w_out, odd_w_in, pool_w, pool_scale, sgu_norm_g, sgu_w, sgu_b,
              odd_w_out, norm_ffn_g, ffn_w_up, ffn_conv_w, ffn_conv_b, ffn_w_down,
              norm_final_g):
    for i in range(DEPTH):
        h = rmsnorm(x, norm_mix_g[i])
        j = i // 2
        if i % 2 == 0:
            proj = h @ even_w_in[j]
            xa = proj[..., :D_CONV]
            ba = proj[..., D_CONV:2 * D_CONV]
            ca = proj[..., 2 * D_CONV:3 * D_CONV]
            u = proj[..., 3 * D_CONV:]
            ya = short_conv_mixer(xa, ba, ca, even_conv_w[j])
            yb = s5_mixer(u, ssm_log_step[j], ssm_a_re[j], ssm_a_im[j], ssm_b_re[j], ssm_b_im[j],
                          ssm_c_re[j], ssm_c_im[j], ssm_d[j], ssm_glu_w[j], ssm_glu_b[j])
            mix = jnp.concatenate([ya, yb], axis=-1) @ even_w_out[j]
        else:
            proj = h @ odd_w_in[j]
            z = proj[..., :D_POOL]
            uv = jax.nn.gelu(proj[..., D_POOL:])
            su = uv[..., :D_SGU]
            sv = uv[..., D_SGU:]
            yc = pool_mixer(z, pool_w[j], pool_scale[j])
            yd = sgu_mixer(su, sv, sgu_norm_g[j], sgu_w[j], sgu_b[j])
            mix = jnp.concatenate([yc, yd], axis=-1) @ odd_w_out[j]
        x = x + mix
        x = x + conv_ffn(rmsnorm(x, norm_ffn_g[i]), ffn_w_up[i], ffn_conv_w[i], ffn_conv_b[i], ffn_w_down[i])
    return rmsnorm(x, norm_final_g)


import jax as _jax
import jax.numpy as _jnp

TWIN_FORMAT = 'train_step'
FWD_PARAMS = ['x', 'norm_mix_g', 'even_w_in', 'even_conv_w', 'ssm_log_step', 'ssm_a_re', 'ssm_a_im', 'ssm_b_re', 'ssm_b_im', 'ssm_c_re', 'ssm_c_im', 'ssm_d', 'ssm_glu_w', 'ssm_glu_b', 'even_w_out', 'odd_w_in', 'pool_w', 'pool_scale', 'sgu_norm_g', 'sgu_w', 'sgu_b', 'odd_w_out', 'norm_ffn_g', 'ffn_w_up', 'ffn_conv_w', 'ffn_conv_b', 'ffn_w_down', 'norm_final_g']
TWIN_WEIGHTS = ['norm_mix_g', 'even_w_in', 'even_conv_w', 'ssm_log_step', 'ssm_a_re', 'ssm_a_im', 'ssm_b_re', 'ssm_b_im', 'ssm_c_re', 'ssm_c_im', 'ssm_d', 'ssm_glu_w', 'ssm_glu_b', 'even_w_out', 'odd_w_in', 'pool_w', 'pool_scale', 'sgu_norm_g', 'sgu_w', 'sgu_b', 'odd_w_out', 'norm_ffn_g', 'ffn_w_up', 'ffn_conv_w', 'ffn_conv_b', 'ffn_w_down', 'norm_final_g']
TWIN_DIFF_INPUT = 'x'
TWIN_INPUTS = ['x', 'norm_mix_g', 'even_w_in', 'even_conv_w', 'ssm_log_step', 'ssm_a_re', 'ssm_a_im', 'ssm_b_re', 'ssm_b_im', 'ssm_c_re', 'ssm_c_im', 'ssm_d', 'ssm_glu_w', 'ssm_glu_b', 'even_w_out', 'odd_w_in', 'pool_w', 'pool_scale', 'sgu_norm_g', 'sgu_w', 'sgu_b', 'odd_w_out', 'norm_ffn_g', 'ffn_w_up', 'ffn_conv_w', 'ffn_conv_b', 'ffn_w_down', 'norm_final_g', 'loss_target', 'm_norm_mix_g', 'm_even_w_in', 'm_even_conv_w', 'm_ssm_log_step', 'm_ssm_a_re', 'm_ssm_a_im', 'm_ssm_b_re', 'm_ssm_b_im', 'm_ssm_c_re', 'm_ssm_c_im', 'm_ssm_d', 'm_ssm_glu_w', 'm_ssm_glu_b', 'm_even_w_out', 'm_odd_w_in', 'm_pool_w', 'm_pool_scale', 'm_sgu_norm_g', 'm_sgu_w', 'm_sgu_b', 'm_odd_w_out', 'm_norm_ffn_g', 'm_ffn_w_up', 'm_ffn_conv_w', 'm_ffn_conv_b', 'm_ffn_w_down', 'm_norm_final_g', 'v_norm_mix_g', 'v_even_w_in', 'v_even_conv_w', 'v_ssm_log_step', 'v_ssm_a_re', 'v_ssm_a_im', 'v_ssm_b_re', 'v_ssm_b_im', 'v_ssm_c_re', 'v_ssm_c_im', 'v_ssm_d', 'v_ssm_glu_w', 'v_ssm_glu_b', 'v_even_w_out', 'v_odd_w_in', 'v_pool_w', 'v_pool_scale', 'v_sgu_norm_g', 'v_sgu_w', 'v_sgu_b', 'v_odd_w_out', 'v_norm_ffn_g', 'v_ffn_w_up', 'v_ffn_conv_w', 'v_ffn_conv_b', 'v_ffn_w_down', 'v_norm_final_g']
TWIN_OUTPUTS = ['loss', 'grad_x', 'grad_norm_mix_g', 'grad_even_w_in', 'grad_even_conv_w', 'grad_ssm_log_step', 'grad_ssm_a_re', 'grad_ssm_a_im', 'grad_ssm_b_re', 'grad_ssm_b_im', 'grad_ssm_c_re', 'grad_ssm_c_im', 'grad_ssm_d', 'grad_ssm_glu_w', 'grad_ssm_glu_b', 'grad_even_w_out', 'grad_odd_w_in', 'grad_pool_w', 'grad_pool_scale', 'grad_sgu_norm_g', 'grad_sgu_w', 'grad_sgu_b', 'grad_odd_w_out', 'grad_norm_ffn_g', 'grad_ffn_w_up', 'grad_ffn_conv_w', 'grad_ffn_conv_b', 'grad_ffn_w_down', 'grad_norm_final_g', 'delta_norm_mix_g', 'delta_even_w_in', 'delta_even_conv_w', 'delta_ssm_log_step', 'delta_ssm_a_re', 'delta_ssm_a_im', 'delta_ssm_b_re', 'delta_ssm_b_im', 'delta_ssm_c_re', 'delta_ssm_c_im', 'delta_ssm_d', 'delta_ssm_glu_w', 'delta_ssm_glu_b', 'delta_even_w_out', 'delta_odd_w_in', 'delta_pool_w', 'delta_pool_scale', 'delta_sgu_norm_g', 'delta_sgu_w', 'delta_sgu_b', 'delta_odd_w_out', 'delta_norm_ffn_g', 'delta_ffn_w_up', 'delta_ffn_conv_w', 'delta_ffn_conv_b', 'delta_ffn_w_down', 'delta_norm_final_g', 'new_m_norm_mix_g', 'new_m_even_w_in', 'new_m_even_conv_w', 'new_m_ssm_log_step', 'new_m_ssm_a_re', 'new_m_ssm_a_im', 'new_m_ssm_b_re', 'new_m_ssm_b_im', 'new_m_ssm_c_re', 'new_m_ssm_c_im', 'new_m_ssm_d', 'new_m_ssm_glu_w', 'new_m_ssm_glu_b', 'new_m_even_w_out', 'new_m_odd_w_in', 'new_m_pool_w', 'new_m_pool_scale', 'new_m_sgu_norm_g', 'new_m_sgu_w', 'new_m_sgu_b', 'new_m_odd_w_out', 'new_m_norm_ffn_g', 'new_m_ffn_w_up', 'new_m_ffn_conv_w', 'new_m_ffn_conv_b', 'new_m_ffn_w_down', 'new_m_norm_final_g', 'new_v_norm_mix_g', 'new_v_even_w_in', 'new_v_even_conv_w', 'new_v_ssm_log_step', 'new_v_ssm_a_re', 'new_v_ssm_a_im', 'new_v_ssm_b_re', 'new_v_ssm_b_im', 'new_v_ssm_c_re', 'new_v_ssm_c_im', 'new_v_ssm_d', 'new_v_ssm_glu_w', 'new_v_ssm_glu_b', 'new_v_even_w_out', 'new_v_odd_w_in', 'new_v_pool_w', 'new_v_pool_scale', 'new_v_sgu_norm_g', 'new_v_sgu_w', 'new_v_sgu_b', 'new_v_odd_w_out', 'new_v_norm_ffn_g', 'new_v_ffn_w_up', 'new_v_ffn_conv_w', 'new_v_ffn_conv_b', 'new_v_ffn_w_down', 'new_v_norm_final_g']
TWIN_LEAF_KINDS = {'loss': 'loss', 'grad_x': 'grad_x', 'grad_norm_mix_g': 'grad_w', 'grad_even_w_in': 'grad_w', 'grad_even_conv_w': 'grad_w', 'grad_ssm_log_step': 'grad_w', 'grad_ssm_a_re': 'grad_w', 'grad_ssm_a_im': 'grad_w', 'grad_ssm_b_re': 'grad_w', 'grad_ssm_b_im': 'grad_w', 'grad_ssm_c_re': 'grad_w', 'grad_ssm_c_im': 'grad_w', 'grad_ssm_d': 'grad_w', 'grad_ssm_glu_w': 'grad_w', 'grad_ssm_glu_b': 'grad_w', 'grad_even_w_out': 'grad_w', 'grad_odd_w_in': 'grad_w', 'grad_pool_w': 'grad_w', 'grad_pool_scale': 'grad_w', 'grad_sgu_norm_g': 'grad_w', 'grad_sgu_w': 'grad_w', 'grad_sgu_b': 'grad_w', 'grad_odd_w_out': 'grad_w', 'grad_norm_ffn_g': 'grad_w', 'grad_ffn_w_up': 'grad_w', 'grad_ffn_conv_w': 'grad_w', 'grad_ffn_conv_b': 'grad_w', 'grad_ffn_w_down': 'grad_w', 'grad_norm_final_g': 'grad_w', 'delta_norm_mix_g': 'delta_w', 'delta_even_w_in': 'delta_w', 'delta_even_conv_w': 'delta_w', 'delta_ssm_log_step': 'delta_w', 'delta_ssm_a_re': 'delta_w', 'delta_ssm_a_im': 'delta_w', 'delta_ssm_b_re': 'delta_w', 'delta_ssm_b_im': 'delta_w', 'delta_ssm_c_re': 'delta_w', 'delta_ssm_c_im': 'delta_w', 'delta_ssm_d': 'delta_w', 'delta_ssm_glu_w': 'delta_w', 'delta_ssm_glu_b': 'delta_w', 'delta_even_w_out': 'delta_w', 'delta_odd_w_in': 'delta_w', 'delta_pool_w': 'delta_w', 'delta_pool_scale': 'delta_w', 'delta_sgu_norm_g': 'delta_w', 'delta_sgu_w': 'delta_w', 'delta_sgu_b': 'delta_w', 'delta_odd_w_out': 'delta_w', 'delta_norm_ffn_g': 'delta_w', 'delta_ffn_w_up': 'delta_w', 'delta_ffn_conv_w': 'delta_w', 'delta_ffn_conv_b': 'delta_w', 'delta_ffn_w_down': 'delta_w', 'delta_norm_final_g': 'delta_w', 'new_m_norm_mix_g': 'new_m', 'new_m_even_w_in': 'new_m', 'new_m_even_conv_w': 'new_m', 'new_m_ssm_log_step': 'new_m', 'new_m_ssm_a_re': 'new_m', 'new_m_ssm_a_im': 'new_m', 'new_m_ssm_b_re': 'new_m', 'new_m_ssm_b_im': 'new_m', 'new_m_ssm_c_re': 'new_m', 'new_m_ssm_c_im': 'new_m', 'new_m_ssm_d': 'new_m', 'new_m_ssm_glu_w': 'new_m', 'new_m_ssm_glu_b': 'new_m', 'new_m_even_w_out': 'new_m', 'new_m_odd_w_in': 'new_m', 'new_m_pool_w': 'new_m', 'new_m_pool_scale': 'new_m', 'new_m_sgu_norm_g': 'new_m', 'new_m_sgu_w': 'new_m', 'new_m_sgu_b': 'new_m', 'new_m_odd_w_out': 'new_m', 'new_m_norm_ffn_g': 'new_m', 'new_m_ffn_w_up': 'new_m', 'new_m_ffn_conv_w': 'new_m', 'new_m_ffn_conv_b': 'new_m', 'new_m_ffn_w_down': 'new_m', 'new_m_norm_final_g': 'new_m', 'new_v_norm_mix_g': 'new_v', 'new_v_even_w_in': 'new_v', 'new_v_even_conv_w': 'new_v', 'new_v_ssm_log_step': 'new_v', 'new_v_ssm_a_re': 'new_v', 'new_v_ssm_a_im': 'new_v', 'new_v_ssm_b_re': 'new_v', 'new_v_ssm_b_im': 'new_v', 'new_v_ssm_c_re': 'new_v', 'new_v_ssm_c_im': 'new_v', 'new_v_ssm_d': 'new_v', 'new_v_ssm_glu_w': 'new_v', 'new_v_ssm_glu_b': 'new_v', 'new_v_even_w_out': 'new_v', 'new_v_odd_w_in': 'new_v', 'new_v_pool_w': 'new_v', 'new_v_pool_scale': 'new_v', 'new_v_sgu_norm_g': 'new_v', 'new_v_sgu_w': 'new_v', 'new_v_sgu_b': 'new_v', 'new_v_odd_w_out': 'new_v', 'new_v_norm_ffn_g': 'new_v', 'new_v_ffn_w_up': 'new_v', 'new_v_ffn_conv_w': 'new_v', 'new_v_ffn_conv_b': 'new_v', 'new_v_ffn_w_down': 'new_v', 'new_v_norm_final_g': 'new_v'}


def _forward(args):
    return _fwd_reference(*[args[k] for k in FWD_PARAMS])


def _output_shape():
    out = _jax.eval_shape(lambda: _forward(_fwd_setup_inputs(0)))
    return out.shape, out.dtype

N_MICROBATCH = 1
ADAM_LR = 0.001
ADAM_B1 = 0.9
ADAM_B2 = 0.999
ADAM_EPS = 1e-08
ADAM_WD = 0.01
ADAM_STEP = 10
PER_EXAMPLE_BATCH_AXIS = {'x': 0, 'loss_target': 0}
SHARED_INPUTS = []
_WEIGHT_DTYPES = {'norm_mix_g': _jnp.float32, 'even_w_in': _jnp.float32, 'even_conv_w': _jnp.float32, 'ssm_log_step': _jnp.float32, 'ssm_a_re': _jnp.float32, 'ssm_a_im': _jnp.float32, 'ssm_b_re': _jnp.float32, 'ssm_b_im': _jnp.float32, 'ssm_c_re': _jnp.float32, 'ssm_c_im': _jnp.float32, 'ssm_d': _jnp.float32, 'ssm_glu_w': _jnp.float32, 'ssm_glu_b': _jnp.float32, 'even_w_out': _jnp.float32, 'odd_w_in': _jnp.float32, 'pool_w': _jnp.float32, 'pool_scale': _jnp.float32, 'sgu_norm_g': _jnp.float32, 'sgu_w': _jnp.float32, 'sgu_b': _jnp.float32, 'odd_w_out': _jnp.float32, 'norm_ffn_g': _jnp.float32, 'ffn_w_up': _jnp.float32, 'ffn_conv_w': _jnp.float32, 'ffn_conv_b': _jnp.float32, 'ffn_w_down': _jnp.float32, 'norm_final_g': _jnp.float32}
MOMENT_SCALE = {'norm_mix_g': 1.649402e-01, 'even_w_in': 1.481058e-01, 'even_conv_w': 1.724439e-01, 'ssm_log_step': 3.676624e+00, 'ssm_a_re': 3.389794e-03, 'ssm_a_im': 3.528689e-03, 'ssm_b_re': 2.128420e-03, 'ssm_b_im': 2.097336e-03, 'ssm_c_re': 4.302892e-03, 'ssm_c_im': 4.299436e-03, 'ssm_d': 7.480461e-02, 'ssm_glu_w': 1.865660e-02, 'ssm_glu_b': 2.982143e-02, 'even_w_out': 1.270576e-01, 'odd_w_in': 9.162134e-02, 'pool_w': 1.094666e-01, 'pool_scale': 1.194193e-01, 'sgu_norm_g': 5.503268e-02, 'sgu_w': 5.633510e-02, 'sgu_b': 8.035076e-02, 'odd_w_out': 1.051857e-01, 'norm_ffn_g': 1.134057e-01, 'ffn_w_up': 4.771096e-02, 'ffn_conv_w': 4.745788e-02, 'ffn_conv_b': 4.758466e-02, 'ffn_w_down': 7.804445e-02, 'norm_final_g': 3.199318e+01}


def _to_microbatches(a, axis):
    t = _jnp.moveaxis(a, axis, 0)
    t = t.reshape((N_MICROBATCH, t.shape[0] // N_MICROBATCH) + t.shape[1:])
    return _jnp.moveaxis(t, 1, axis + 1)


def setup_inputs(seed: int = 0) -> dict:
    inp = _fwd_setup_inputs(seed)
    key = _jax.random.fold_in(_jax.random.key(seed), 7919)
    shape, _ = _output_shape()
    out = dict(inp)
    out["loss_target"] = _jax.random.normal(_jax.random.fold_in(key, 0), shape, _jnp.float32)
    for i, name in enumerate(TWIN_WEIGHTS):
        w = inp[name].astype(_jnp.float32)
        if MOMENT_SCALE is None:
            s = _jnp.sqrt(_jnp.mean(_jnp.square(w)) + 1e-30)
        else:
            s = MOMENT_SCALE[name]
        km, kv = _jax.random.split(_jax.random.fold_in(key, i + 1))
        out[name] = w
        out["m_" + name] = s * _jax.random.normal(km, w.shape, _jnp.float32)
        out["v_" + name] = (s * s) * _jax.random.uniform(kv, w.shape, _jnp.float32, 0.5, 1.5)
    if N_MICROBATCH > 1:
        for name, axis in PER_EXAMPLE_BATCH_AXIS.items():
            out[name] = _to_microbatches(out[name], axis)
    return {'x': out['x'], 'norm_mix_g': out['norm_mix_g'], 'even_w_in': out['even_w_in'], 'even_conv_w': out['even_conv_w'], 'ssm_log_step': out['ssm_log_step'], 'ssm_a_re': out['ssm_a_re'], 'ssm_a_im': out['ssm_a_im'], 'ssm_b_re': out['ssm_b_re'], 'ssm_b_im': out['ssm_b_im'], 'ssm_c_re': out['ssm_c_re'], 'ssm_c_im': out['ssm_c_im'], 'ssm_d': out['ssm_d'], 'ssm_glu_w': out['ssm_glu_w'], 'ssm_glu_b': out['ssm_glu_b'], 'even_w_out': out['even_w_out'], 'odd_w_in': out['odd_w_in'], 'pool_w': out['pool_w'], 'pool_scale': out['pool_scale'], 'sgu_norm_g': out['sgu_norm_g'], 'sgu_w': out['sgu_w'], 'sgu_b': out['sgu_b'], 'odd_w_out': out['odd_w_out'], 'norm_ffn_g': out['norm_ffn_g'], 'ffn_w_up': out['ffn_w_up'], 'ffn_conv_w': out['ffn_conv_w'], 'ffn_conv_b': out['ffn_conv_b'], 'ffn_w_down': out['ffn_w_down'], 'norm_final_g': out['norm_final_g'], 'loss_target': out['loss_target'], 'm_norm_mix_g': out['m_norm_mix_g'], 'm_even_w_in': out['m_even_w_in'], 'm_even_conv_w': out['m_even_conv_w'], 'm_ssm_log_step': out['m_ssm_log_step'], 'm_ssm_a_re': out['m_ssm_a_re'], 'm_ssm_a_im': out['m_ssm_a_im'], 'm_ssm_b_re': out['m_ssm_b_re'], 'm_ssm_b_im': out['m_ssm_b_im'], 'm_ssm_c_re': out['m_ssm_c_re'], 'm_ssm_c_im': out['m_ssm_c_im'], 'm_ssm_d': out['m_ssm_d'], 'm_ssm_glu_w': out['m_ssm_glu_w'], 'm_ssm_glu_b': out['m_ssm_glu_b'], 'm_even_w_out': out['m_even_w_out'], 'm_odd_w_in': out['m_odd_w_in'], 'm_pool_w': out['m_pool_w'], 'm_pool_scale': out['m_pool_scale'], 'm_sgu_norm_g': out['m_sgu_norm_g'], 'm_sgu_w': out['m_sgu_w'], 'm_sgu_b': out['m_sgu_b'], 'm_odd_w_out': out['m_odd_w_out'], 'm_norm_ffn_g': out['m_norm_ffn_g'], 'm_ffn_w_up': out['m_ffn_w_up'], 'm_ffn_conv_w': out['m_ffn_conv_w'], 'm_ffn_conv_b': out['m_ffn_conv_b'], 'm_ffn_w_down': out['m_ffn_w_down'], 'm_norm_final_g': out['m_norm_final_g'], 'v_norm_mix_g': out['v_norm_mix_g'], 'v_even_w_in': out['v_even_w_in'], 'v_even_conv_w': out['v_even_conv_w'], 'v_ssm_log_step': out['v_ssm_log_step'], 'v_ssm_a_re': out['v_ssm_a_re'], 'v_ssm_a_im': out['v_ssm_a_im'], 'v_ssm_b_re': out['v_ssm_b_re'], 'v_ssm_b_im': out['v_ssm_b_im'], 'v_ssm_c_re': out['v_ssm_c_re'], 'v_ssm_c_im': out['v_ssm_c_im'], 'v_ssm_d': out['v_ssm_d'], 'v_ssm_glu_w': out['v_ssm_glu_w'], 'v_ssm_glu_b': out['v_ssm_glu_b'], 'v_even_w_out': out['v_even_w_out'], 'v_odd_w_in': out['v_odd_w_in'], 'v_pool_w': out['v_pool_w'], 'v_pool_scale': out['v_pool_scale'], 'v_sgu_norm_g': out['v_sgu_norm_g'], 'v_sgu_w': out['v_sgu_w'], 'v_sgu_b': out['v_sgu_b'], 'v_odd_w_out': out['v_odd_w_out'], 'v_norm_ffn_g': out['v_norm_ffn_g'], 'v_ffn_w_up': out['v_ffn_w_up'], 'v_ffn_conv_w': out['v_ffn_conv_w'], 'v_ffn_conv_b': out['v_ffn_conv_b'], 'v_ffn_w_down': out['v_ffn_w_down'], 'v_norm_final_g': out['v_norm_final_g']}


def _loss(weights, diff, rest, loss_target):
    with _jax.named_scope("forward"):
        args = {**rest, TWIN_DIFF_INPUT: diff, **{k: w.astype(_WEIGHT_DTYPES[k]) for k, w in weights.items()}}
        y = _forward(args)
    with _jax.named_scope("loss_head"):
        err = _jnp.square(y.astype(_jnp.float32) - loss_target)
        return 0.5 * _jnp.sum(_jnp.mean(err, axis=-1)) if err.ndim else 0.5 * err


def _adamw(w, g, m, v):
    m = ADAM_B1 * m + (1.0 - ADAM_B1) * g
    v = ADAM_B2 * v + (1.0 - ADAM_B2) * _jnp.square(g)
    m_hat = m / (1.0 - ADAM_B1 ** ADAM_STEP)
    v_hat = v / (1.0 - ADAM_B2 ** ADAM_STEP)
    delta = -ADAM_LR * (m_hat / (_jnp.sqrt(v_hat) + ADAM_EPS) + ADAM_WD * w)
    return delta, m, v


def reference(x, norm_mix_g, even_w_in, even_conv_w, ssm_log_step, ssm_a_re, ssm_a_im, ssm_b_re, ssm_b_im, ssm_c_re, ssm_c_im, ssm_d, ssm_glu_w, ssm_glu_b, even_w_out, odd_w_in, pool_w, pool_scale, sgu_norm_g, sgu_w, sgu_b, odd_w_out, norm_ffn_g, ffn_w_up, ffn_conv_w, ffn_conv_b, ffn_w_down, norm_final_g, loss_target, m_norm_mix_g, m_even_w_in, m_even_conv_w, m_ssm_log_step, m_ssm_a_re, m_ssm_a_im, m_ssm_b_re, m_ssm_b_im, m_ssm_c_re, m_ssm_c_im, m_ssm_d, m_ssm_glu_w, m_ssm_glu_b, m_even_w_out, m_odd_w_in, m_pool_w, m_pool_scale, m_sgu_norm_g, m_sgu_w, m_sgu_b, m_odd_w_out, m_norm_ffn_g, m_ffn_w_up, m_ffn_conv_w, m_ffn_conv_b, m_ffn_w_down, m_norm_final_g, v_norm_mix_g, v_even_w_in, v_even_conv_w, v_ssm_log_step, v_ssm_a_re, v_ssm_a_im, v_ssm_b_re, v_ssm_b_im, v_ssm_c_re, v_ssm_c_im, v_ssm_d, v_ssm_glu_w, v_ssm_glu_b, v_even_w_out, v_odd_w_in, v_pool_w, v_pool_scale, v_sgu_norm_g, v_sgu_w, v_sgu_b, v_odd_w_out, v_norm_ffn_g, v_ffn_w_up, v_ffn_conv_w, v_ffn_conv_b, v_ffn_w_down, v_norm_final_g):
    given = dict(x=x, norm_mix_g=norm_mix_g, even_w_in=even_w_in, even_conv_w=even_conv_w, ssm_log_step=ssm_log_step, ssm_a_re=ssm_a_re, ssm_a_im=ssm_a_im, ssm_b_re=ssm_b_re, ssm_b_im=ssm_b_im, ssm_c_re=ssm_c_re, ssm_c_im=ssm_c_im, ssm_d=ssm_d, ssm_glu_w=ssm_glu_w, ssm_glu_b=ssm_glu_b, even_w_out=even_w_out, odd_w_in=odd_w_in, pool_w=pool_w, pool_scale=pool_scale, sgu_norm_g=sgu_norm_g, sgu_w=sgu_w, sgu_b=sgu_b, odd_w_out=odd_w_out, norm_ffn_g=norm_ffn_g, ffn_w_up=ffn_w_up, ffn_conv_w=ffn_conv_w, ffn_conv_b=ffn_conv_b, ffn_w_down=ffn_w_down, norm_final_g=norm_final_g, loss_target=loss_target, m_norm_mix_g=m_norm_mix_g, m_even_w_in=m_even_w_in, m_even_conv_w=m_even_conv_w, m_ssm_log_step=m_ssm_log_step, m_ssm_a_re=m_ssm_a_re, m_ssm_a_im=m_ssm_a_im, m_ssm_b_re=m_ssm_b_re, m_ssm_b_im=m_ssm_b_im, m_ssm_c_re=m_ssm_c_re, m_ssm_c_im=m_ssm_c_im, m_ssm_d=m_ssm_d, m_ssm_glu_w=m_ssm_glu_w, m_ssm_glu_b=m_ssm_glu_b, m_even_w_out=m_even_w_out, m_odd_w_in=m_odd_w_in, m_pool_w=m_pool_w, m_pool_scale=m_pool_scale, m_sgu_norm_g=m_sgu_norm_g, m_sgu_w=m_sgu_w, m_sgu_b=m_sgu_b, m_odd_w_out=m_odd_w_out, m_norm_ffn_g=m_norm_ffn_g, m_ffn_w_up=m_ffn_w_up, m_ffn_conv_w=m_ffn_conv_w, m_ffn_conv_b=m_ffn_conv_b, m_ffn_w_down=m_ffn_w_down, m_norm_final_g=m_norm_final_g, v_norm_mix_g=v_norm_mix_g, v_even_w_in=v_even_w_in, v_even_conv_w=v_even_conv_w, v_ssm_log_step=v_ssm_log_step, v_ssm_a_re=v_ssm_a_re, v_ssm_a_im=v_ssm_a_im, v_ssm_b_re=v_ssm_b_re, v_ssm_b_im=v_ssm_b_im, v_ssm_c_re=v_ssm_c_re, v_ssm_c_im=v_ssm_c_im, v_ssm_d=v_ssm_d, v_ssm_glu_w=v_ssm_glu_w, v_ssm_glu_b=v_ssm_glu_b, v_even_w_out=v_even_w_out, v_odd_w_in=v_odd_w_in, v_pool_w=v_pool_w, v_pool_scale=v_pool_scale, v_sgu_norm_g=v_sgu_norm_g, v_sgu_w=v_sgu_w, v_sgu_b=v_sgu_b, v_odd_w_out=v_odd_w_out, v_norm_ffn_g=v_norm_ffn_g, v_ffn_w_up=v_ffn_w_up, v_ffn_conv_w=v_ffn_conv_w, v_ffn_conv_b=v_ffn_conv_b, v_ffn_w_down=v_ffn_w_down, v_norm_final_g=v_norm_final_g)
    weights = {n: given[n] for n in TWIN_WEIGHTS}
    shared = {n: given[n] for n in SHARED_INPUTS}
    per_example = {n: given[n] for n in ['x']}
    grad_fn = _jax.value_and_grad(_loss, argnums=(0, 1))

    def one_microbatch(ex, loss_target):
        ex = dict(ex)
        diff = ex.pop(TWIN_DIFF_INPUT)
        return grad_fn(weights, diff, {**shared, **ex}, loss_target)

    if N_MICROBATCH == 1:
        loss, (grad_w, grad_x) = one_microbatch(per_example, given["loss_target"])
    else:
        def body(carry, xs):
            loss_sum, grad_sum = carry
            l_k, (gw_k, gx_k) = one_microbatch(xs[0], xs[1])
            with _jax.named_scope("update"):
                return (loss_sum + l_k, _jax.tree.map(_jnp.add, grad_sum, gw_k)), gx_k

        init = (_jnp.zeros((), _jnp.float32), _jax.tree.map(_jnp.zeros_like, weights))
        (loss, grad_w), grad_x = _jax.lax.scan(body, init, (per_example, given["loss_target"]))
    with _jax.named_scope("update"):
        delta_w, new_m, new_v = {}, {}, {}
        for n in TWIN_WEIGHTS:
            delta_w[n], new_m[n], new_v[n] = _adamw(weights[n], grad_w[n], given["m_" + n], given["v_" + n])
    return (loss, grad_x, *[grad_w[n] for n in TWIN_WEIGHTS], *[delta_w[n] for n in TWIN_WEIGHTS],
            *[new_m[n] for n in TWIN_WEIGHTS], *[new_v[n] for n in TWIN_WEIGHTS])
```

```python
import functools
import math

import numpy as np
import jax
import jax.numpy as jnp
from jax import lax
from jax.experimental import pallas as pl
from jax.experimental.pallas import tpu as pltpu

F32 = jnp.float32
BF16 = jnp.bfloat16
MESH = pl.DeviceIdType.MESH

EPS = 1e-6
CHUNK = 128
POOL_WINDOWS = (2, 4, 8, 16)
LANES = 128
SUBLANES = 8
SCAN_CHUNKS = SUBLANES
S5_GROUPS_PER_STEP = 4
VMEM_LIMIT = 48 * 1024 * 1024
VMEM_LIMIT_S5 = 56 * 1024 * 1024

ADAM_LR, ADAM_B1, ADAM_B2, ADAM_EPS, ADAM_WD, ADAM_STEP = 0.001, 0.9, 0.999, 1e-08, 0.01, 10

WEIGHTS = ['norm_mix_g', 'even_w_in', 'even_conv_w', 'ssm_log_step', 'ssm_a_re', 'ssm_a_im', 'ssm_b_re',
           'ssm_b_im', 'ssm_c_re', 'ssm_c_im', 'ssm_d', 'ssm_glu_w', 'ssm_glu_b', 'even_w_out', 'odd_w_in',
           'pool_w', 'pool_scale', 'sgu_norm_g', 'sgu_w', 'sgu_b', 'odd_w_out', 'norm_ffn_g', 'ffn_w_up',
           'ffn_conv_w', 'ffn_conv_b', 'ffn_w_down', 'norm_final_g']
BIG = {'even_w_in': 2, 'ssm_glu_w': 1, 'even_w_out': 1, 'odd_w_in': 2, 'odd_w_out': 1, 'ffn_w_up': 2,
       'ffn_w_down': 1}
SMALL_SHARDED = ('even_conv_w', 'pool_scale', 'sgu_norm_g', 'ffn_conv_w')
SMALL = [n for n in WEIGHTS if n not in BIG]
N_CHIPS = 4
N_DEV = 8


def _cparams(sem=None, vmem=VMEM_LIMIT):
    kw = dict(vmem_limit_bytes=vmem)
    if sem is not None:
        kw['dimension_semantics'] = sem
    return pltpu.CompilerParams(**kw)


def _pick(n, segs=(), prefs=(1024, 512, 256, 128)):
    for t in prefs:
        if n % t == 0 and all(s % t == 0 for s in segs if s):
            return t
    return n


def _ldims(arr, kind):
    if kind is None:
        return arr.shape
    if kind[0] == 'lead':
        return arr.shape[1:]
    return (arr.shape[1], arr.shape[0] * arr.shape[2])


def _segw(arr, kind):
    return arr.shape[2] if (kind is not None and kind[0] == 'seg') else None


def _opspec(arr, kind, br, bc, rfn, cfn):
    if kind is None:
        return pl.BlockSpec((br, bc), lambda i, j, k: (rfn(i, j, k), cfn(i, j, k)))
    if kind[0] == 'lead':
        lead = kind[1]
        return pl.BlockSpec((None, br, bc), lambda i, j, k: (lead, rfn(i, j, k), cfn(i, j, k)))
    per = arr.shape[2] // bc
    return pl.BlockSpec((None, br, bc), lambda i, j, k: (cfn(i, j, k) // per, rfn(i, j, k), cfn(i, j, k) % per))


def _mm(a, b, mode, out_dtype, name, ak=None, bk=None, ok=None, res=None):
    ar, ac = _ldims(a, ak)
    br_, bc_ = _ldims(b, bk)
    if mode == 'nn':
        M, K, N = ar, ac, bc_
        assert br_ == K
    elif mode == 'nt':
        M, K, N = ar, ac, br_
        assert bc_ == K
    else:
        K, M, N = ar, ac, bc_
        assert br_ == K
    sa, sb = _segw(a, ak), _segw(b, bk)
    so = (N // ok[1]) if ok is not None else None
    tm = _pick(M, [sa if mode == 'tn' else None])
    tn = _pick(N, [sb if mode in ('nn', 'tn') else None, so])
    tk = _pick(K, [sa if mode in ('nn', 'nt') else None, sb if mode == 'nt' else None], prefs=(512, 256, 128))
    nk = K // tk
    I = lambda i, j, k: i
    J = lambda i, j, k: j
    Kk = lambda i, j, k: k
    if mode == 'nn':
        a_spec = _opspec(a, ak, tm, tk, I, Kk)
        b_spec = _opspec(b, bk, tk, tn, Kk, J)
        dims = (((1,), (0,)), ((), ()))
    elif mode == 'nt':
        a_spec = _opspec(a, ak, tm, tk, I, Kk)
        b_spec = _opspec(b, bk, tn, tk, J, Kk)
        dims = (((1,), (1,)), ((), ()))
    else:
        a_spec = _opspec(a, ak, tk, tm, Kk, I)
        b_spec = _opspec(b, bk, tk, tn, Kk, J)
        dims = (((0,), (0,)), ((), ()))
    if ok is None:
        out_shape = jax.ShapeDtypeStruct((M, N), out_dtype)
        o_spec = pl.BlockSpec((tm, tn), lambda i, j, k: (i, j))
    else:
        out_shape = jax.ShapeDtypeStruct((ok[1], M, N // ok[1]), out_dtype)
        per = (N // ok[1]) // tn
        o_spec = pl.BlockSpec((None, tm, tn), lambda i, j, k: (j // per, i, j % per))
    has_res = res is not None

    def body(*refs):
        if has_res:
            a_ref, b_ref, r_ref, o_ref, acc = refs
        else:
            a_ref, b_ref, o_ref, acc = refs
        k = pl.program_id(2)

        @pl.when(k == 0)
        def _():
            acc[...] = jnp.zeros_like(acc)

        acc[...] += lax.dot_general(a_ref[...].astype(BF16), b_ref[...].astype(BF16), dims,
                                    preferred_element_type=F32)

        @pl.when(k == nk - 1)
        def _():
            o = acc[...]
            if has_res:
                o = o + r_ref[...]
            o_ref[...] = o.astype(out_dtype)

    in_specs = [a_spec, b_spec]
    args = [a, b]
    if has_res:
        in_specs.append(pl.BlockSpec((tm, tn), lambda i, j, k: (i, j)))
        args.append(res)
    return pl.pallas_call(
        body, name=name, out_shape=out_shape, grid=(M // tm, N // tn, nk), in_specs=in_specs, out_specs=o_spec,
        scratch_shapes=[pltpu.VMEM((tm, tn), F32)],
        compiler_params=_cparams(("parallel", "parallel", "arbitrary")),
    )(*args)


_G0 = math.sqrt(2.0 / math.pi)
_G1 = 0.044715


def _gelu(x):
    return 0.5 * x * (1.0 + jnp.tanh(_G0 * (x + _G1 * x * x * x)))


def _gelu_grad(x):
    x2 = x * x
    t = jnp.tanh(_G0 * (x + _G1 * x * x2))
    return 0.5 * (1.0 + t) + 0.5 * x * (1.0 - t * t) * (_G0 * (1.0 + 3.0 * _G1 * x2))


def _sigmoid(x):
    return 1.0 / (1.0 + jnp.exp(-x))


def _down(v, k):
    row = lax.broadcasted_iota(jnp.int32, v.shape, 0)
    return jnp.where(row >= k, pltpu.roll(v, k, axis=0), 0.0)


def _up(v, k):
    n = v.shape[0]
    row = lax.broadcasted_iota(jnp.int32, v.shape, 0)
    return jnp.where(row < n - k, pltpu.roll(v, n - k, axis=0), 0.0)


def _conv3(v, w):
    return w[0:1, :] * _down(v, 2) + w[1:2, :] * _down(v, 1) + w[2:3, :] * v


def _conv3_t(dv, w):
    return w[2:3, :] * dv + w[1:2, :] * _up(dv, 1) + w[0:1, :] * _up(dv, 2)


def _conv3_dw(dv, v):
    return (jnp.sum(dv * _down(v, 2), axis=0, keepdims=True),
            jnp.sum(dv * _down(v, 1), axis=0, keepdims=True),
            jnp.sum(dv * v, axis=0, keepdims=True))


def _cmul(ar, ai, br, bi):
    return ar * br - ai * bi, ar * bi + ai * br


def _cpow(lr, li, n):
    rr = ri = None
    br, bi = lr, li
    while n:
        if n & 1:
            rr, ri = (br, bi) if rr is None else _cmul(rr, ri, br, bi)
        n >>= 1
        if n:
            br, bi = _cmul(br, bi, br, bi)
    return rr, ri


def _rms_fwd(x, g, name):
    L, D = x.shape
    tr = _pick(L, prefs=(512, 256, 128))

    def body(x_ref, g_ref, h_ref):
        xv = x_ref[...]
        r = lax.rsqrt(jnp.mean(xv * xv, axis=-1, keepdims=True) + EPS)
        h_ref[...] = (xv * r * g_ref[...]).astype(BF16)

    return pl.pallas_call(
        body, name=name, out_shape=jax.ShapeDtypeStruct((L, D), BF16), grid=(L // tr,),
        in_specs=[pl.BlockSpec((tr, D), lambda i: (i, 0)), pl.BlockSpec((1, D), lambda i: (0, 0))],
        out_specs=pl.BlockSpec((tr, D), lambda i: (i, 0)), compiler_params=_cparams(("parallel",)),
    )(x, g.reshape(1, D))


def _rms_bwd(x, g, dh, dres, name):
    L, D = x.shape
    tr = _pick(L, prefs=(512, 256, 128))
    nsteps = L // tr

    def body(x_ref, g_ref, dh_ref, dres_ref, dx_ref, dg_ref, acc):
        i = pl.program_id(0)

        @pl.when(i == 0)
        def _():
            acc[...] = jnp.zeros_like(acc)

        xv = x_ref[...]
        r = lax.rsqrt(jnp.mean(xv * xv, axis=-1, keepdims=True) + EPS)
        xh = xv * r
        dhv = dh_ref[...].astype(F32)
        acc[...] += jnp.sum((dhv * xh).reshape(tr // SUBLANES, SUBLANES, D), axis=0)
        dxh = dhv * g_ref[...]
        dx_ref[...] = dres_ref[...] + r * (dxh - xh * jnp.mean(dxh * xh, axis=-1, keepdims=True))

        @pl.when(i == nsteps - 1)
        def _():
            dg_ref[...] = jnp.sum(acc[...], axis=0, keepdims=True)

    row = pl.BlockSpec((tr, D), lambda i: (i, 0))
    vec = pl.BlockSpec((1, D), lambda i: (0, 0))
    return pl.pallas_call(
        body, name=name, out_shape=(jax.ShapeDtypeStruct((L, D), F32), jax.ShapeDtypeStruct((1, D), F32)),
        grid=(nsteps,), in_specs=[row, vec, row, row], out_specs=(row, vec),
        scratch_shapes=[pltpu.VMEM((SUBLANES, D), F32)], compiler_params=_cparams(("arbitrary",)),
    )(x, g.reshape(1, D), dh, dres)


def _loss_head(x, g, tgt):
    L, D = x.shape
    tr = _pick(L, prefs=(512, 256, 128))
    nsteps = L // tr

    def body(x_ref, g_ref, t_ref, loss_ref, dx_ref, dg_ref, acc_g, acc_l):
        i = pl.program_id(0)

        @pl.when(i == 0)
        def _():
            acc_g[...] = jnp.zeros_like(acc_g)
            acc_l[...] = jnp.zeros_like(acc_l)

        xv = x_ref[...]
        gv = g_ref[...]
        r = lax.rsqrt(jnp.mean(xv * xv, axis=-1, keepdims=True) + EPS)
        xh = xv * r
        e = xh * gv - t_ref[...]
        acc_l[...] += jnp.sum((e * e).reshape(tr // SUBLANES, SUBLANES, D), axis=0)
        dy = e * (1.0 / D)
        acc_g[...] += jnp.sum((dy * xh).reshape(tr // SUBLANES, SUBLANES, D), axis=0)
        dxh = dy * gv
        dx_ref[...] = r * (dxh - xh * jnp.mean(dxh * xh, axis=-1, keepdims=True))

        @pl.when(i == nsteps - 1)
        def _():
            dg_ref[...] = jnp.sum(acc_g[...], axis=0, keepdims=True)
            tot = jnp.sum(jnp.sum(acc_l[...], axis=0, keepdims=True), axis=1, keepdims=True) * (0.5 / D)
            loss_ref[...] = jnp.broadcast_to(tot, (SUBLANES, LANES))

    row = pl.BlockSpec((tr, D), lambda i: (i, 0))
    vec = pl.BlockSpec((1, D), lambda i: (0, 0))
    return pl.pallas_call(
        body, name="loss_head",
        out_shape=(jax.ShapeDtypeStruct((SUBLANES, LANES), F32), jax.ShapeDtypeStruct((L, D), F32),
                   jax.ShapeDtypeStruct((1, D), F32)),
        grid=(nsteps,), in_specs=[row, vec, row],
        out_specs=(pl.BlockSpec((SUBLANES, LANES), lambda i: (0, 0)), row, vec),
        scratch_shapes=[pltpu.VMEM((SUBLANES, D), F32), pltpu.VMEM((SUBLANES, D), F32)],
        compiler_params=_cparams(("arbitrary",)),
    )(x, g.reshape(1, D), tgt)


def _sconv_fwd(proj4, conv_w, name):
    _, L, C = proj4.shape
    cb = LANES

    def body(p_ref, w_ref, o_ref):
        xa, ba, ca = p_ref[0], p_ref[1], p_ref[2]
        o_ref[...] = (ba * _conv3(ca * xa, w_ref[...])).astype(BF16)

    return pl.pallas_call(
        body, name=name, out_shape=jax.ShapeDtypeStruct((L, C), BF16), grid=(C // cb,),
        in_specs=[pl.BlockSpec((3, L, cb), lambda j: (0, 0, j)), pl.BlockSpec((3, cb), lambda j: (0, j))],
        out_specs=pl.BlockSpec((L, cb), lambda j: (0, j)), compiler_params=_cparams(("parallel",)),
    )(proj4, conv_w)


def _sconv_bwd(proj4, dmix, conv_w, name):
    _, L, C = proj4.shape
    cb = LANES

    def body(p_ref, d_ref, w_ref, o_ref, dw_ref):
        xa, ba, ca = p_ref[0], p_ref[1], p_ref[2]
        w = w_ref[...]
        dya = d_ref[...]
        q = ca * xa
        cq = _conv3(q, w)
        dcq = dya * ba
        dq = _conv3_t(dcq, w)
        for tap, dwt in enumerate(_conv3_dw(dcq, q)):
            dw_ref[tap:tap + 1, :] = dwt
        o_ref[0] = (dq * ca).astype(BF16)
        o_ref[1] = (dya * cq).astype(BF16)
        o_ref[2] = (dq * xa).astype(BF16)

    return pl.pallas_call(
        body, name=name,
        out_shape=(jax.ShapeDtypeStruct((3, L, C), BF16), jax.ShapeDtypeStruct((3, C), F32)), grid=(C // cb,),
        in_specs=[pl.BlockSpec((3, L, cb), lambda j: (0, 0, j)), pl.BlockSpec((L, cb), lambda j: (0, j)),
                  pl.BlockSpec((3, cb), lambda j: (0, j))],
        out_specs=(pl.BlockSpec((3, L, cb), lambda j: (0, 0, j)), pl.BlockSpec((3, cb), lambda j: (0, j))),
        compiler_params=_cparams(("parallel",)),
    )(proj4, dmix, conv_w)


def _to_scan_order(v):
    L, C = v.shape
    return v.reshape(SCAN_CHUNKS, L // SCAN_CHUNKS, C).transpose(1, 0, 2).reshape(L, C)


def _from_scan_order(v):
    L, C = v.shape
    return v.reshape(L // SCAN_CHUNKS, SCAN_CHUNKS, C).transpose(1, 0, 2).reshape(L, C)


def _s5_prep(log_step, a_re, a_im, b_re, b_im, c_re, c_im):
    G, P = a_re.shape
    H = b_re.shape[-1]
    gs = S5_GROUPS_PER_STEP
    ns = G // gs
    gu = LANES // H
    lam = lax.complex(a_re, a_im)
    step = jnp.exp(log_step)[:, None]
    lam_bar = jnp.exp(lam * step)
    b_bar = ((lam_bar - 1.0) / lam)[..., None] * lax.complex(b_re, b_im)
    lr = jnp.real(lam_bar).reshape(ns, 1, gs * P)
    li = jnp.imag(lam_bar).reshape(ns, 1, gs * P)
    k = np.arange(ns)[:, None, None]
    oh = jnp.asarray((np.arange(gu)[None, :, None] == gs * (k % (gu // gs)) + np.arange(gs)[None, None, :]),
                     F32)
    bre = jnp.einsum('kgl,klph->kghlp', oh, jnp.real(b_bar).reshape(ns, gs, P, H)).reshape(ns, gu * H, gs * P)
    bim = jnp.einsum('kgl,klph->kghlp', oh, jnp.imag(b_bar).reshape(ns, gs, P, H)).reshape(ns, gu * H, gs * P)
    cre = jnp.einsum('kgl,klhp->klpgh', oh, c_re.reshape(ns, gs, H, P)).reshape(ns, gs * P, gu * H)
    cim = jnp.einsum('kgl,klhp->klpgh', oh, c_im.reshape(ns, gs, H, P)).reshape(ns, gs * P, gu * H)
    return lr, li, jnp.concatenate([bre, bim], axis=2), jnp.concatenate([cre, -cim], axis=1)


def _carry_tile(fr, fi, pr, pi, reverse):
    row = lax.broadcasted_iota(jnp.int32, fr.shape, 0)
    cr = jnp.zeros_like(fr)
    ci = jnp.zeros_like(fi)
    sr = jnp.zeros_like(fr[0:1])
    si = jnp.zeros_like(sr)
    order = range(SCAN_CHUNKS - 1, 0, -1) if reverse else range(0, SCAN_CHUNKS - 1)
    for c in order:
        fcr = jnp.sum(jnp.where(row == c, fr, 0.0), axis=0, keepdims=True)
        fci = jnp.sum(jnp.where(row == c, fi, 0.0), axis=0, keepdims=True)
        mr, mi = _cmul(pr, pi, sr, si)
        sr, si = mr + fcr, mi + fci
        nxt = c - 1 if reverse else c + 1
        cr = jnp.where(row == nxt, sr, cr)
        ci = jnp.where(row == nxt, si, ci)
    return cr, ci


def _s5_fwd(u, lr, li, bmat, cmat, d, name):
    L, Du = u.shape
    ns, _, W2 = bmat.shape
    W = W2 // 2
    T = L // SCAN_CHUNKS
    rb = _pick(L, prefs=(512, 256, 128))
    per = (ns * LANES) // Du

    def body(u_ref, lr_ref, li_ref, b_ref, c_ref, d_ref, y_ref, sr_ref, si_ref):
        k = pl.program_id(0)
        for r in range(L // rb):
            rows = pl.ds(r * rb, rb)
            bu = jnp.dot(u_ref[rows, :].astype(BF16), b_ref[...], preferred_element_type=F32)
            sr_ref[rows, :] = bu[:, :W]
            si_ref[rows, :] = bu[:, W:]
        lam_r = jnp.broadcast_to(lr_ref[...], (SUBLANES, W))
        lam_i = jnp.broadcast_to(li_ref[...], (SUBLANES, W))

        def local(t, carry):
            sr, si = carry
            rows = pl.ds(pl.multiple_of(t * SUBLANES, SUBLANES), SUBLANES)
            mr, mi = _cmul(lam_r, lam_i, sr, si)
            sr = mr + sr_ref[rows, :]
            si = mi + si_ref[rows, :]
            sr_ref[rows, :] = sr
            si_ref[rows, :] = si
            return sr, si

        z = jnp.zeros((SUBLANES, W), F32)
        fr, fi = lax.fori_loop(0, T, local, (z, z))
        pr, pi = _cpow(lam_r, lam_i, T)
        cr, ci = _carry_tile(fr, fi, pr[0:1], pi[0:1], reverse=False)

        def fix(t, carry):
            wr, wi = carry
            rows = pl.ds(pl.multiple_of(t * SUBLANES, SUBLANES), SUBLANES)
            ar, ai = _cmul(wr, wi, cr, ci)
            sr_ref[rows, :] += ar
            si_ref[rows, :] += ai
            return _cmul(wr, wi, lam_r, lam_i)

        lax.fori_loop(0, T, fix, (lam_r, lam_i))
        first = (k % per) == 0
        for r in range(L // rb):
            rows = pl.ds(r * rb, rb)
            s = jnp.concatenate([sr_ref[rows, :], si_ref[rows, :]], axis=1).astype(BF16)
            y = jnp.dot(s, c_ref[...], preferred_element_type=F32)

            @pl.when(first)
            def _():
                y_ref[rows, :] = y + d_ref[...] * u_ref[rows, :]

            @pl.when(jnp.logical_not(first))
            def _():
                y_ref[rows, :] += y

    ublk = pl.BlockSpec((L, LANES), lambda k: (0, k // per))
    sblk = pl.BlockSpec((L, W), lambda k: (0, k))
    lam = pl.BlockSpec((None, 1, W), lambda k: (k, 0, 0))
    return pl.pallas_call(
        body, name=name,
        out_shape=(jax.ShapeDtypeStruct((L, Du), F32), jax.ShapeDtypeStruct((L, ns * W), F32),
                   jax.ShapeDtypeStruct((L, ns * W), F32)),
        grid=(ns,),
        in_specs=[ublk, lam, lam, pl.BlockSpec((None, LANES, 2 * W), lambda k: (k, 0, 0)),
                  pl.BlockSpec((None, 2 * W, LANES), lambda k: (k, 0, 0)),
                  pl.BlockSpec((1, LANES), lambda k: (0, k // per))],
        out_specs=(ublk, sblk, sblk), compiler_params=_cparams(("arbitrary",), VMEM_LIMIT_S5),
    )(u, lr, li, bmat.astype(BF16), cmat.astype(BF16), d.reshape(1, Du))


def _s5_bwd(dy, u, s_re, s_im, lr, li, bmat, cmat, d, name):
    L, Du = u.shape
    ns, _, W2 = bmat.shape
    W = W2 // 2
    T = L // SCAN_CHUNKS
    rb = _pick(L, prefs=(512, 256, 128))
    per = (ns * LANES) // Du
    NT = (((1,), (1,)), ((), ()))
    TN = (((0,), (0,)), ((), ()))

    def body(dy_ref, u_ref, sr_ref, si_ref, lr_ref, li_ref, b_ref, c_ref, d_ref,
             du_ref, db_ref, dc_ref, dl_ref, dd_ref, gr_ref, gi_ref):
        k = pl.program_id(0)
        for r in range(L // rb):
            rows = pl.ds(r * rb, rb)
            g = lax.dot_general(dy_ref[rows, :].astype(BF16), c_ref[...], NT, preferred_element_type=F32)
            gr_ref[rows, :] = g[:, :W]
            gi_ref[rows, :] = g[:, W:]
        lam_r = jnp.broadcast_to(lr_ref[...], (SUBLANES, W))
        lam_i = -jnp.broadcast_to(li_ref[...], (SUBLANES, W))

        def local(i, carry):
            gr, gi = carry
            rows = pl.ds(pl.multiple_of((T - 1 - i) * SUBLANES, SUBLANES), SUBLANES)
            mr, mi = _cmul(lam_r, lam_i, gr, gi)
            gr = mr + gr_ref[rows, :]
            gi = mi + gi_ref[rows, :]
            gr_ref[rows, :] = gr
            gi_ref[rows, :] = gi
            return gr, gi

        z = jnp.zeros((SUBLANES, W), F32)
        fr, fi = lax.fori_loop(0, T, local, (z, z))
        pr, pi = _cpow(lam_r, lam_i, T)
        cr, ci = _carry_tile(fr, fi, pr[0:1], pi[0:1], reverse=True)

        def true_g(rows, wr, wi):
            ar, ai = _cmul(wr, wi, cr, ci)
            gr = gr_ref[rows, :] + ar
            gi = gi_ref[rows, :] + ai
            gr_ref[rows, :] = gr
            gi_ref[rows, :] = gi
            return gr, gi

        def fix(i, carry):
            wr, wi, ar_, ai_ = carry
            t = T - 1 - i
            rows = pl.ds(pl.multiple_of(t * SUBLANES, SUBLANES), SUBLANES)
            prev = pl.ds(pl.multiple_of((t - 1) * SUBLANES, SUBLANES), SUBLANES)
            gr, gi = true_g(rows, wr, wi)
            qr, qi = sr_ref[prev, :], si_ref[prev, :]
            ar_ = ar_ + gr * qr + gi * qi
            ai_ = ai_ + gi * qr - gr * qi
            wr, wi = _cmul(wr, wi, lam_r, lam_i)
            return wr, wi, ar_, ai_

        wr, wi, acc_r, acc_i = lax.fori_loop(0, T - 1, fix, (lam_r, lam_i, z, z))
        gr, gi = true_g(pl.ds(0, SUBLANES), wr, wi)
        last = pl.ds((T - 1) * SUBLANES, SUBLANES)
        row = lax.broadcasted_iota(jnp.int32, (SUBLANES, W), 0)
        qr = jnp.where(row >= 1, pltpu.roll(sr_ref[last, :], 1, axis=0), 0.0)
        qi = jnp.where(row >= 1, pltpu.roll(si_ref[last, :], 1, axis=0), 0.0)
        acc_r = acc_r + gr * qr + gi * qi
        acc_i = acc_i + gi * qr - gr * qi
        dl_ref[0:1, :] = jnp.sum(acc_r, axis=0, keepdims=True)
        dl_ref[1:2, :] = jnp.sum(acc_i, axis=0, keepdims=True)

        first = (k % per) == 0
        db = jnp.zeros((LANES, 2 * W), F32)
        dc = jnp.zeros((LANES, 2 * W), F32)
        dd = jnp.zeros((1, LANES), F32)
        for r in range(L // rb):
            rows = pl.ds(r * rb, rb)
            gb = jnp.concatenate([gr_ref[rows, :], gi_ref[rows, :]], axis=1).astype(BF16)
            sb = jnp.concatenate([sr_ref[rows, :], si_ref[rows, :]], axis=1).astype(BF16)
            dyv = dy_ref[rows, :]
            uv = u_ref[rows, :]
            du = lax.dot_general(gb, b_ref[...], NT, preferred_element_type=F32)
            db = db + lax.dot_general(uv.astype(BF16), gb, TN, preferred_element_type=F32)
            dc = dc + lax.dot_general(dyv.astype(BF16), sb, TN, preferred_element_type=F32)
            dd = dd + jnp.sum(dyv * uv, axis=0, keepdims=True)

            @pl.when(first)
            def _():
                du_ref[rows, :] = du + d_ref[...] * dyv

            @pl.when(jnp.logical_not(first))
            def _():
                du_ref[rows, :] += du

        db_ref[...] = db
        dc_ref[...] = dc

        @pl.when(first)
        def _():
            dd_ref[...] = dd

    ublk = pl.BlockSpec((L, LANES), lambda k: (0, k // per))
    sblk = pl.BlockSpec((L, W), lambda k: (0, k))
    lam = pl.BlockSpec((None, 1, W), lambda k: (k, 0, 0))
    vec = pl.BlockSpec((1, LANES), lambda k: (0, k // per))
    mat = pl.BlockSpec((None, LANES, 2 * W), lambda k: (k, 0, 0))
    return pl.pallas_call(
        body, name=name,
        out_shape=(jax.ShapeDtypeStruct((L, Du), F32), jax.ShapeDtypeStruct((ns, LANES, 2 * W), F32),
                   jax.ShapeDtypeStruct((ns, LANES, 2 * W), F32), jax.ShapeDtypeStruct((ns, 2, W), F32),
                   jax.ShapeDtypeStruct((1, Du), F32)),
        grid=(ns,),
        in_specs=[ublk, ublk, sblk, sblk, lam, lam, mat,
                  pl.BlockSpec((None, 2 * W, LANES), lambda k: (k, 0, 0)), vec],
        out_specs=(ublk, mat, mat, pl.BlockSpec((None, 2, W), lambda k: (k, 0, 0)), vec),
        scratch_shapes=[pltpu.VMEM((L, W), F32), pltpu.VMEM((L, W), F32)],
        compiler_params=_cparams(("arbitrary",), VMEM_LIMIT_S5),
    )(dy, u, s_re, s_im, lr, li, bmat.astype(BF16), cmat.astype(BF16), d.reshape(1, Du))


def _glu_fwd(yraw, w3, layer, bias, name):
    L, C = yraw.shape
    tr = _pick(L, prefs=(512, 256, 128))

    def body(y_ref, w_ref, b_ref, o_ref):
        yg = _gelu(y_ref[...])
        zz = jnp.dot(yg.astype(BF16), w_ref[...], preferred_element_type=F32) + b_ref[...]
        o_ref[...] = (yg * _sigmoid(zz)).astype(BF16)

    return pl.pallas_call(
        body, name=name, out_shape=jax.ShapeDtypeStruct((L, C), BF16), grid=(L // tr,),
        in_specs=[pl.BlockSpec((tr, C), lambda i: (i, 0)), pl.BlockSpec((None, C, C), lambda i: (layer, 0, 0)),
                  pl.BlockSpec((1, C), lambda i: (0, 0))],
        out_specs=pl.BlockSpec((tr, C), lambda i: (i, 0)), compiler_params=_cparams(("parallel",)),
    )(yraw, w3, bias.reshape(1, C))


def _glu_bwd(yraw, dyb, w3, layer, bias, name):
    L, C = yraw.shape
    tr = _pick(L, prefs=(512, 256, 128))
    nsteps = L // tr

    def body(y_ref, d_ref, w_ref, b_ref, dy_ref, dw_ref, db_ref, acc_b):
        i = pl.program_id(0)

        @pl.when(i == 0)
        def _():
            dw_ref[...] = jnp.zeros_like(dw_ref)
            acc_b[...] = jnp.zeros_like(acc_b)

        yr = y_ref[...]
        yg = _gelu(yr)
        ygb = yg.astype(BF16)
        sg = _sigmoid(jnp.dot(ygb, w_ref[...], preferred_element_type=F32) + b_ref[...])
        dyb_ = d_ref[...]
        dz = dyb_ * yg * sg * (1.0 - sg)
        dzb = dz.astype(BF16)
        dyg = dyb_ * sg + lax.dot_general(dzb, w_ref[...], (((1,), (1,)), ((), ())), preferred_element_type=F32)
        dw_ref[...] += lax.dot_general(ygb, dzb, (((0,), (0,)), ((), ())), preferred_element_type=F32)
        acc_b[...] += jnp.sum(dz.reshape(tr // SUBLANES, SUBLANES, C), axis=0)
        dy_ref[...] = dyg * _gelu_grad(yr)

        @pl.when(i == nsteps - 1)
        def _():
            db_ref[...] = jnp.sum(acc_b[...], axis=0, keepdims=True)

    row = pl.BlockSpec((tr, C), lambda i: (i, 0))
    return pl.pallas_call(
        body, name=name,
        out_shape=(jax.ShapeDtypeStruct((L, C), F32), jax.ShapeDtypeStruct((C, C), F32),
                   jax.ShapeDtypeStruct((1, C), F32)),
        grid=(nsteps,),
        in_specs=[row, row, pl.BlockSpec((None, C, C), lambda i: (layer, 0, 0)), pl.BlockSpec((1, C), lambda i: (0, 0))],
        out_specs=(row, pl.BlockSpec((C, C), lambda i: (0, 0)), pl.BlockSpec((1, C), lambda i: (0, 0))),
        scratch_shapes=[pltpu.VMEM((SUBLANES, C), F32)], compiler_params=_cparams(("arbitrary",)),
    )(yraw, dyb, w3, bias.reshape(1, C))


def _pool_counts(L, g):
    t = lax.broadcasted_iota(jnp.int32, (L, LANES), 0).astype(F32) + 1.0
    w = jnp.where(g == 0, 2.0, jnp.where(g == 1, 4.0, jnp.where(g == 2, 8.0, 16.0)))
    return 1.0 / jnp.minimum(t, w)


def _select_window(g, a2, a4, a8, a16):
    return jnp.where(g == 0, a2, jnp.where(g == 1, a4, jnp.where(g == 2, a8, a16)))


def _pooled(z, g):
    a2 = z + _down(z, 1)
    a4 = a2 + _down(a2, 2)
    a8 = a4 + _down(a4, 4)
    a16 = a8 + _down(a8, 8)
    return _select_window(g, a2, a4, a8, a16) * _pool_counts(z.shape[0], g) - z


def _pool_fwd(proj3, pool_w, scale, name):
    _, L, C = proj3.shape
    ng = len(POOL_WINDOWS)
    pg = C // ng
    assert pg == LANES

    def body(z_ref, w_ref, s_ref, o_ref):
        g = pl.program_id(0)
        p = _pooled(z_ref[...], g)
        y = jnp.dot(p.astype(BF16), w_ref[...].astype(BF16), preferred_element_type=F32)
        o_ref[...] = (y * s_ref[...]).astype(BF16)

    return pl.pallas_call(
        body, name=name, out_shape=jax.ShapeDtypeStruct((L, C), BF16), grid=(ng,),
        in_specs=[pl.BlockSpec((None, L, pg), lambda g: (0, 0, g)), pl.BlockSpec((None, pg, pg), lambda g: (g, 0, 0)),
                  pl.BlockSpec((1, pg), lambda g: (0, g))],
        out_specs=pl.BlockSpec((L, pg), lambda g: (0, g)), compiler_params=_cparams(("parallel",)),
    )(proj3, pool_w, scale.reshape(1, C))


def _pool_bwd(proj3, dmix, pool_w, scale, name):
    _, L, C = proj3.shape
    ng = len(POOL_WINDOWS)
    pg = C // ng

    def body(z_ref, d_ref, w_ref, s_ref, dz_ref, dw_ref, ds_ref):
        g = pl.program_id(0)
        p = _pooled(z_ref[...], g)
        pb = p.astype(BF16)
        wb = w_ref[...].astype(BF16)
        pre = jnp.dot(pb, wb, preferred_element_type=F32)
        dyc = d_ref[...]
        ds_ref[...] = jnp.sum(dyc * pre, axis=0, keepdims=True)
        dpre = (dyc * s_ref[...]).astype(BF16)
        dw_ref[...] = lax.dot_general(pb, dpre, (((0,), (0,)), ((), ())), preferred_element_type=F32)
        dp = lax.dot_general(dpre, wb, (((1,), (1,)), ((), ())), preferred_element_type=F32)
        v = dp * _pool_counts(L, g)
        a2 = v + _up(v, 1)
        a4 = a2 + _up(a2, 2)
        a8 = a4 + _up(a4, 4)
        a16 = a8 + _up(a8, 8)
        dz_ref[...] = (_select_window(g, a2, a4, a8, a16) - dp).astype(BF16)

    return pl.pallas_call(
        body, name=name,
        out_shape=(jax.ShapeDtypeStruct((L, C), BF16), jax.ShapeDtypeStruct((ng, pg, pg), F32),
                   jax.ShapeDtypeStruct((1, C), F32)),
        grid=(ng,),
        in_specs=[pl.BlockSpec((None, L, pg), lambda g: (0, 0, g)), pl.BlockSpec((L, pg), lambda g: (0, g)),
                  pl.BlockSpec((None, pg, pg), lambda g: (g, 0, 0)), pl.BlockSpec((1, pg), lambda g: (0, g))],
        out_specs=(pl.BlockSpec((L, pg), lambda g: (0, g)), pl.BlockSpec((None, pg, pg), lambda g: (g, 0, 0)),
                   pl.BlockSpec((1, pg), lambda g: (0, g))),
        compiler_params=_cparams(("parallel",)),
    )(proj3, dmix, pool_w, scale.reshape(1, C))


def _tril_w(w_ref, h):
    r = lax.broadcasted_iota(jnp.int32, (CHUNK, CHUNK), 0)
    c = lax.broadcasted_iota(jnp.int32, (CHUNK, CHUNK), 1)
    return jnp.where(r >= c, w_ref[h], 0.0)


def _sgu_fwd(proj3, norm_g, w, b, name):
    _, L, C = proj3.shape
    nh = w.shape[0]
    dh = C // nh
    assert dh == LANES and w.shape[1] == CHUNK
    tr = _pick(L, prefs=(512, 256, 128))
    bfull = jnp.broadcast_to(b[:, :, None], (nh, CHUNK, dh))

    def body(su_ref, sv_ref, g_ref, w_ref, b_ref, o_ref):
        sv = _gelu(sv_ref[...])
        r = lax.rsqrt(jnp.mean(sv * sv, axis=-1, keepdims=True) + EPS)
        v = (sv * r * g_ref[...]).astype(BF16)
        for h in range(nh):
            wm = _tril_w(w_ref, h).astype(BF16)
            cols = slice(h * dh, (h + 1) * dh)
            for n in range(tr // CHUNK):
                rows = slice(n * CHUNK, (n + 1) * CHUNK)
                mixed = jnp.dot(wm, v[rows, cols], preferred_element_type=F32) + b_ref[h]
                o_ref[rows, cols] = (_gelu(su_ref[rows, cols]) * mixed).astype(BF16)

    full = lambda shp: pl.BlockSpec(shp, lambda i: (0,) * len(shp))
    return pl.pallas_call(
        body, name=name, out_shape=jax.ShapeDtypeStruct((L, C), BF16), grid=(L // tr,),
        in_specs=[pl.BlockSpec((None, tr, C), lambda i: (1, i, 0)), pl.BlockSpec((None, tr, C), lambda i: (2, i, 0)),
                  full((1, C)), full((nh, CHUNK, CHUNK)), full((nh, CHUNK, dh))],
        out_specs=pl.BlockSpec((tr, C), lambda i: (i, 0)), compiler_params=_cparams(("parallel",)),
    )(proj3, proj3, norm_g.reshape(1, C), w, bfull)


def _sgu_bwd(proj3, dmix, norm_g, w, b, name):
    _, L, C = proj3.shape
    nh = w.shape[0]
    dh = C // nh
    tr = _pick(L, prefs=(512, 256, 128))
    nsteps = L // tr
    bfull = jnp.broadcast_to(b[:, :, None], (nh, CHUNK, dh))

    def body(su_ref, sv_ref, d_ref, g_ref, w_ref, b_ref, o_ref, dw_ref, db_ref, dg_ref, dv_ref, acc_g):
        i = pl.program_id(0)

        @pl.when(i == 0)
        def _():
            dw_ref[...] = jnp.zeros_like(dw_ref)
            db_ref[...] = jnp.zeros_like(db_ref)
            acc_g[...] = jnp.zeros_like(acc_g)

        svp = sv_ref[...]
        sv = _gelu(svp)
        r = lax.rsqrt(jnp.mean(sv * sv, axis=-1, keepdims=True) + EPS)
        vh = sv * r
        gv = g_ref[...]
        v = (vh * gv).astype(BF16)
        tri_r = lax.broadcasted_iota(jnp.int32, (CHUNK, CHUNK), 0)
        tri_c = lax.broadcasted_iota(jnp.int32, (CHUNK, CHUNK), 1)
        for h in range(nh):
            wm = _tril_w(w_ref, h).astype(BF16)
            cols = slice(h * dh, (h + 1) * dh)
            dwh = jnp.zeros((CHUNK, CHUNK), F32)
            dbh = jnp.zeros((CHUNK, dh), F32)
            for n in range(tr // CHUNK):
                rows = slice(n * CHUNK, (n + 1) * CHUNK)
                vb = v[rows, cols]
                mixed = jnp.dot(wm, vb, preferred_element_type=F32) + b_ref[h]
                sup = su_ref[rows, cols]
                dyd = d_ref[rows, cols]
                dmx = dyd * _gelu(sup)
                o_ref[0, rows, cols] = (dyd * mixed * _gelu_grad(sup)).astype(BF16)
                dmb = dmx.astype(BF16)
                dwh = dwh + lax.dot_general(dmb, vb, (((1,), (1,)), ((), ())), preferred_element_type=F32)
                dbh = dbh + dmx
                dv_ref[rows, cols] = lax.dot_general(wm, dmb, (((0,), (0,)), ((), ())), preferred_element_type=F32)
            dw_ref[h] += jnp.where(tri_r >= tri_c, dwh, 0.0)
            db_ref[h] += dbh
        dv = dv_ref[...]
        acc_g[...] += jnp.sum((dv * vh).reshape(tr // SUBLANES, SUBLANES, C), axis=0)
        dvg = dv * gv
        dsv = r * (dvg - vh * jnp.mean(dvg * vh, axis=-1, keepdims=True))
        o_ref[1] = (dsv * _gelu_grad(svp)).astype(BF16)

        @pl.when(i == nsteps - 1)
        def _():
            dg_ref[...] = jnp.sum(acc_g[...], axis=0, keepdims=True)

    full = lambda shp: pl.BlockSpec(shp, lambda i: (0,) * len(shp))
    return pl.pallas_call(
        body, name=name,
        out_shape=(jax.ShapeDtypeStruct((2, L, C), BF16), jax.ShapeDtypeStruct((nh, CHUNK, CHUNK), F32),
                   jax.ShapeDtypeStruct((nh, CHUNK, dh), F32), jax.ShapeDtypeStruct((1, C), F32)),
        grid=(nsteps,),
        in_specs=[pl.BlockSpec((None, tr, C), lambda i: (1, i, 0)), pl.BlockSpec((None, tr, C), lambda i: (2, i, 0)),
                  pl.BlockSpec((tr, C), lambda i: (i, 1)), full((1, C)), full((nh, CHUNK, CHUNK)),
                  full((nh, CHUNK, dh))],
        out_specs=(pl.BlockSpec((2, tr, C), lambda i: (0, i, 0)), full((nh, CHUNK, CHUNK)), full((nh, CHUNK, dh)),
                   full((1, C))),
        scratch_shapes=[pltpu.VMEM((tr, C), F32), pltpu.VMEM((SUBLANES, C), F32)],
        compiler_params=_cparams(("arbitrary",)),
    )(proj3, proj3, dmix, norm_g.reshape(1, C), w, bfull)


def _ffn_act_fwd(up3, conv_w, conv_b, name):
    _, L, Fh = up3.shape
    cb = LANES
    w2 = conv_w.reshape(3, 2, Fh).transpose(1, 0, 2)
    b2 = conv_b.reshape(2, 1, Fh)

    def body(u_ref, w_ref, b_ref, o_ref):
        g = _conv3(u_ref[0], w_ref[0]) + b_ref[0]
        v = _conv3(u_ref[1], w_ref[1]) + b_ref[1]
        o_ref[...] = (g * _sigmoid(g) * v).astype(BF16)

    return pl.pallas_call(
        body, name=name, out_shape=jax.ShapeDtypeStruct((L, Fh), BF16), grid=(Fh // cb,),
        in_specs=[pl.BlockSpec((2, L, cb), lambda j: (0, 0, j)), pl.BlockSpec((2, 3, cb), lambda j: (0, 0, j)),
                  pl.BlockSpec((2, 1, cb), lambda j: (0, 0, j))],
        out_specs=pl.BlockSpec((L, cb), lambda j: (0, j)), compiler_params=_cparams(("parallel",)),
    )(up3, w2, b2)


def _ffn_act_bwd(up3, da, conv_w, conv_b, name):
    _, L, Fh = up3.shape
    cb = LANES
    w2 = conv_w.reshape(3, 2, Fh).transpose(1, 0, 2)
    b2 = conv_b.reshape(2, 1, Fh)

    def body(u_ref, d_ref, w_ref, b_ref, o_ref, dw_ref, db_ref):
        ug, uv = u_ref[0], u_ref[1]
        wg, wv = w_ref[0], w_ref[1]
        g = _conv3(ug, wg) + b_ref[0]
        v = _conv3(uv, wv) + b_ref[1]
        sg = _sigmoid(g)
        dav = d_ref[...]
        dg = dav * v * (sg * (1.0 + g * (1.0 - sg)))
        dv = dav * (g * sg)
        o_ref[0] = _conv3_t(dg, wg).astype(BF16)
        o_ref[1] = _conv3_t(dv, wv).astype(BF16)
        for tap, (dwg, dwv) in enumerate(zip(_conv3_dw(dg, ug), _conv3_dw(dv, uv))):
            dw_ref[0, tap:tap + 1, :] = dwg
            dw_ref[1, tap:tap + 1, :] = dwv
        db_ref[0] = jnp.sum(dg, axis=0, keepdims=True)
        db_ref[1] = jnp.sum(dv, axis=0, keepdims=True)

    dup, dw2, db2 = pl.pallas_call(
        body, name=name,
        out_shape=(jax.ShapeDtypeStruct((2, L, Fh), BF16), jax.ShapeDtypeStruct((2, 3, Fh), F32),
                   jax.ShapeDtypeStruct((2, 1, Fh), F32)),
        grid=(Fh // cb,),
        in_specs=[pl.BlockSpec((2, L, cb), lambda j: (0, 0, j)), pl.BlockSpec((L, cb), lambda j: (0, j)),
                  pl.BlockSpec((2, 3, cb), lambda j: (0, 0, j)), pl.BlockSpec((2, 1, cb), lambda j: (0, 0, j))],
        out_specs=(pl.BlockSpec((2, L, cb), lambda j: (0, 0, j)), pl.BlockSpec((2, 3, cb), lambda j: (0, 0, j)),
                   pl.BlockSpec((2, 1, cb), lambda j: (0, 0, j))),
        compiler_params=_cparams(("parallel",)),
    )(up3, da, w2, b2)
    return dup, dw2.transpose(1, 0, 2).reshape(3, 2 * Fh), db2.reshape(2 * Fh)


def _local_step(x, tgt, w, wbig):
    L, D = x.shape
    depth = w['norm_mix_g'].shape[0]
    saved = []
    for i in range(depth):
        j = i // 2
        s = {'x': x}
        h = _rms_fwd(x, w['norm_mix_g'][i], "mix_norm_fwd")
        s['h'] = h
        if i % 2 == 0:
            proj4 = _mm(h, wbig['even_w_in'], 'nn', F32, "even_in_fwd", bk=('lead', j), ok=('seg', 4))
            s['proj'] = proj4
            ya = _sconv_fwd(proj4, w['even_conv_w'][j], "sconv_fwd")
            prm = (w['ssm_log_step'][j], w['ssm_a_re'][j], w['ssm_a_im'][j], w['ssm_b_re'][j], w['ssm_b_im'][j],
                   w['ssm_c_re'][j], w['ssm_c_im'][j])
            (lr, li, bmat, cmat), prep_vjp = jax.vjp(_s5_prep, *prm)
            u = _to_scan_order(proj4[3])
            yraw, s_re, s_im = _s5_fwd(u, lr, li, bmat, cmat, w['ssm_d'][j], "s5_fwd")
            yb = _glu_fwd(yraw, wbig['ssm_glu_w'], j, w['ssm_glu_b'][j], "glu_fwd")
            s.update(u=u, yraw=yraw, s_re=s_re, s_im=s_im, s5=(lr, li, bmat, cmat), prep_vjp=prep_vjp)
            mixin = jnp.concatenate([ya, _from_scan_order(yb)], axis=1)
            x = _mm(mixin, wbig['even_w_out'], 'nn', F32, "even_out_fwd", bk=('lead', j), res=x)
        else:
            proj3 = _mm(h, wbig['odd_w_in'], 'nn', F32, "odd_in_fwd", bk=('lead', j), ok=('seg', 3))
            s['proj'] = proj3
            yc = _pool_fwd(proj3, w['pool_w'][j], w['pool_scale'][j], "pool_fwd")
            yd = _sgu_fwd(proj3, w['sgu_norm_g'][j], w['sgu_w'][j], w['sgu_b'][j], "sgu_fwd")
            mixin = jnp.concatenate([yc, yd], axis=1)
            x = _mm(mixin, wbig['odd_w_out'], 'nn', F32, "odd_out_fwd", bk=('lead', j), res=x)
        s['mixin'] = mixin
        s['x1'] = x
        h2 = _rms_fwd(x, w['norm_ffn_g'][i], "ffn_norm_fwd")
        up3 = _mm(h2, wbig['ffn_w_up'], 'nn', F32, "ffn_up_fwd", bk=('lead', i), ok=('seg', 2))
        a = _ffn_act_fwd(up3, w['ffn_conv_w'][i], w['ffn_conv_b'][i], "ffn_act_fwd")
        x = _mm(a, wbig['ffn_w_down'], 'nn', F32, "ffn_down_fwd", bk=('lead', i), res=x)
        s.update(h2=h2, up3=up3, a=a)
        saved.append(s)

    loss8, dx, dg_final = _loss_head(x, w['norm_final_g'], tgt)
    gs = {n: [None] * w[n].shape[0] for n in SMALL if n != 'norm_final_g'}
    gs['norm_final_g'] = dg_final.reshape(D)
    gb = {n: [None] * wbig[n].shape[0] for n in BIG}

    for i in reversed(range(depth)):
        j = i // 2
        s = saved[i]
        da = _mm(dx, wbig['ffn_w_down'], 'nt', F32, "ffn_down_dgrad", bk=('lead', i))
        gb['ffn_w_down'][i] = _mm(s['a'], dx, 'tn', BF16, "ffn_down_wgrad")
        dup3, dcw, dcb = _ffn_act_bwd(s['up3'], da, w['ffn_conv_w'][i], w['ffn_conv_b'][i], "ffn_act_bwd")
        gs['ffn_conv_w'][i], gs['ffn_conv_b'][i] = dcw, dcb
        gb['ffn_w_up'][i] = _mm(s['h2'], dup3, 'tn', BF16, "ffn_up_wgrad", bk=('seg', 2))
        dh2 = _mm(dup3, wbig['ffn_w_up'], 'nt', F32, "ffn_up_dgrad", ak=('seg', 2), bk=('lead', i))
        dx, dg = _rms_bwd(s['x1'], w['norm_ffn_g'][i], dh2, dx, "ffn_norm_bwd")
        gs['norm_ffn_g'][i] = dg.reshape(D)
        if i % 2 == 0:
            dmix = _mm(dx, wbig['even_w_out'], 'nt', F32, "even_out_dgrad", bk=('lead', j))
            gb['even_w_out'][j] = _mm(s['mixin'], dx, 'tn', BF16, "even_out_wgrad")
            dpc, dcw = _sconv_bwd(s['proj'], dmix, w['even_conv_w'][j], "sconv_bwd")
            gs['even_conv_w'][j] = dcw
            dyb = _to_scan_order(dmix[:, D // 2:])
            dyraw, dglu_w, dglu_b = _glu_bwd(s['yraw'], dyb, wbig['ssm_glu_w'], j, w['ssm_glu_b'][j], "glu_bwd")
            gb['ssm_glu_w'][j] = dglu_w.astype(BF16)
            gs['ssm_glu_b'][j] = dglu_b.reshape(-1)
            lr, li, bmat, cmat = s['s5']
            du, dbm, dcm, dlam, dd = _s5_bwd(dyraw, s['u'], s['s_re'], s['s_im'], lr, li, bmat, cmat,
                                            w['ssm_d'][j], "s5_bwd")
            gs['ssm_d'][j] = dd.reshape(-1)
            dcm = jnp.swapaxes(dcm, 1, 2)
            dprm = s['prep_vjp']((dlam[:, 0:1, :], dlam[:, 1:2, :], dbm, dcm))
            for n, gval in zip(('ssm_log_step', 'ssm_a_re', 'ssm_a_im', 'ssm_b_re', 'ssm_b_im', 'ssm_c_re',
                                'ssm_c_im'), dprm):
                gs[n][j] = gval
            dproj = jnp.concatenate([dpc, _from_scan_order(du).astype(BF16)[None]], axis=0)
            gb['even_w_in'][j] = _mm(s['h'], dproj, 'tn', BF16, "even_in_wgrad", bk=('seg', 4))
            dh = _mm(dproj, wbig['even_w_in'], 'nt', F32, "even_in_dgrad", ak=('seg', 4), bk=('lead', j))
        else:
            dmix = _mm(dx, wbig['odd_w_out'], 'nt', F32, "odd_out_dgrad", bk=('lead', j))
            gb['odd_w_out'][j] = _mm(s['mixin'], dx, 'tn', BF16, "odd_out_wgrad")
            dz, dpw, dps = _pool_bwd(s['proj'], dmix, w['pool_w'][j], w['pool_scale'][j], "pool_bwd")
            gs['pool_w'][j], gs['pool_scale'][j] = dpw, dps.reshape(-1)
            dsuv, dsw, dsb, dsg = _sgu_bwd(s['proj'], dmix, w['sgu_norm_g'][j], w['sgu_w'][j], w['sgu_b'][j],
                                           "sgu_bwd")
            gs['sgu_w'][j], gs['sgu_b'][j], gs['sgu_norm_g'][j] = dsw, jnp.sum(dsb, axis=-1), dsg.reshape(-1)
            dproj = jnp.concatenate([dz[None], dsuv], axis=0)
            gb['odd_w_in'][j] = _mm(s['h'], dproj, 'tn', BF16, "odd_in_wgrad", bk=('seg', 3))
            dh = _mm(dproj, wbig['odd_w_in'], 'nt', F32, "odd_in_dgrad", ak=('seg', 3), bk=('lead', j))
        dx, dg = _rms_bwd(s['x'], w['norm_mix_g'][i], dh, dx, "mix_norm_bwd")
        gs['norm_mix_g'][i] = dg.reshape(D)

    gsmall = {n: (v if n == 'norm_final_g' else jnp.stack(v)) for n, v in gs.items()}
    return loss8[0, 0], dx, gsmall, gb


_HBM = pl.BlockSpec(memory_space=pltpu.HBM)
_CHIP_FLIPS = ((0, 0), (1, 0), (0, 1), (1, 1))


def _coords():
    return lax.axis_index("x"), lax.axis_index("y"), lax.axis_index("c")


def _flip(v, f):
    return 1 - v if f else v


def _shard_of(ref, axis, s, width):
    start = pl.multiple_of(s * width, LANES if axis == ref.ndim - 1 else 16) if width % 16 == 0 else s * width
    idx = [slice(None)] * ref.ndim
    idx[axis] = pl.ds(start, width)
    return ref.at[tuple(idx)]


def _gather_over_chips(shards, axes):
    n = len(shards)

    def body(*refs):
        ins, outs = refs[:n], refs[n:2 * n]
        send_sems, recv_sems, loc_sems = refs[2 * n:]
        x, y, c = _coords()

        def place(t, px, py):
            return _shard_of(outs[t], axes[t], 2 * px + py, ins[t].shape[axes[t]])

        def remote(t, f, dst_chip):
            fx, fy = _CHIP_FLIPS[f]
            return pltpu.make_async_remote_copy(
                src_ref=ins[t], dst_ref=place(t, *dst_chip), send_sem=send_sems.at[3 * t + f - 1],
                recv_sem=recv_sems.at[3 * t + f - 1], device_id=(_flip(x, fx), _flip(y, fy), c), device_id_type=MESH)

        local = [pltpu.make_async_copy(ins[t], place(t, x, y), loc_sems.at[t]) for t in range(n)]
        sends = [remote(t, f, (x, y)) for t in range(n) for f in (1, 2, 3)]
        for cp in local + sends:
            cp.start()
        for t in range(n):
            for f in (1, 2, 3):
                fx, fy = _CHIP_FLIPS[f]
                remote(t, f, (_flip(x, fx), _flip(y, fy))).wait_recv()
        for cp in sends:
            cp.wait_send()
        for cp in local:
            cp.wait()

    out_shape = []
    for a, ax in zip(shards, axes):
        shp = list(a.shape)
        shp[ax] *= N_CHIPS
        out_shape.append(jax.ShapeDtypeStruct(tuple(shp), a.dtype))
    return pl.pallas_call(
        body, name="gather_weights", out_shape=tuple(out_shape), in_specs=[_HBM] * n, out_specs=tuple([_HBM] * n),
        scratch_shapes=[pltpu.SemaphoreType.DMA((3 * n,)), pltpu.SemaphoreType.DMA((3 * n,)),
                        pltpu.SemaphoreType.DMA((n,))],
    )(*shards)


def _scatter_grads(names, grads, axes):
    items = [(t, l) for t in range(len(names)) for l in range(len(grads[t]))]
    arrays = [grads[t][l] for t, l in items]
    ni, nt = len(items), len(names)
    halves = [len(g) // 2 for g in grads]
    base = np.concatenate([[0], np.cumsum(halves)]).astype(int)

    def quarter_shape(t):
        r, c = grads[t][0].shape
        return (r, c // N_CHIPS) if axes[t] == 1 else (r // N_CHIPS, c)

    def body(*refs):
        ins, outs = refs[:ni], refs[ni:ni + nt]
        send_sems, recv_sems, loc_sems = refs[ni + nt:]
        x, y, c = _coords()
        started = []
        for idx, (t, l) in enumerate(items):
            owner, q = l // halves[t], l % halves[t]
            other = c if owner == 0 else 1 - c
            width = quarter_shape(t)[axes[t]]
            for f, (fx, fy) in enumerate(_CHIP_FLIPS):
                px, py = _flip(x, fx), _flip(y, fy)
                src = _shard_of(ins[idx], axes[t], 2 * px + py, width)
                slot = f + 4 * other
                dst = outs[t].at[q, slot]
                rem = pltpu.make_async_remote_copy(
                    src_ref=src, dst_ref=dst, send_sem=send_sems.at[4 * idx + f],
                    recv_sem=recv_sems.at[(int(base[t]) + q) * N_DEV + slot], device_id=(px, py, owner),
                    device_id_type=MESH)
                if f == 0:
                    loc = pltpu.make_async_copy(src, dst, loc_sems.at[idx])
                    pl.when(other == 0)(loc.start)
                    pl.when(other == 1)(rem.start)
                    started.append((other, loc, rem))
                else:
                    rem.start()
                    started.append((None, None, rem))
        for t in range(nt):
            for q in range(halves[t]):
                for slot in range(1, N_DEV):
                    land = outs[t].at[q, slot]
                    pltpu.make_async_remote_copy(
                        src_ref=land, dst_ref=land, send_sem=send_sems.at[0],
                        recv_sem=recv_sems.at[(int(base[t]) + q) * N_DEV + slot], device_id=(x, y, c),
                        device_id_type=MESH).wait_recv()
        for other, loc, rem in started:
            if loc is None:
                rem.wait_send()
            else:
                pl.when(other == 0)(loc.wait)
                pl.when(other == 1)(rem.wait_send)

    out_shape = tuple(jax.ShapeDtypeStruct((halves[t], N_DEV) + quarter_shape(t), BF16) for t in range(nt))
    return pl.pallas_call(
        body, name="scatter_grads", out_shape=out_shape, in_specs=[_HBM] * ni, out_specs=tuple([_HBM] * nt),
        scratch_shapes=[pltpu.SemaphoreType.DMA((4 * ni,)), pltpu.SemaphoreType.DMA((int(base[-1]) * N_DEV,)),
                        pltpu.SemaphoreType.DMA((ni,))],
    )(*arrays)


def _sum_slots(recv, name):
    n, ns, r, c = recv.shape
    tr = _pick(r, prefs=(256, 128, 64, 32, 16))

    def body(i_ref, o_ref):
        acc = i_ref[0].astype(F32)
        for s in range(1, ns):
            acc = acc + i_ref[s].astype(F32)
        o_ref[...] = acc

    return pl.pallas_call(
        body, name=name, out_shape=jax.ShapeDtypeStruct((n, r, c), F32), grid=(n, r // tr),
        in_specs=[pl.BlockSpec((None, ns, tr, c), lambda h, i: (h, 0, i, 0))],
        out_specs=pl.BlockSpec((None, tr, c), lambda h, i: (h, i, 0)),
        compiler_params=_cparams(("parallel", "parallel")),
    )(recv)


def _exchange_with_sibling(parts):
    nt = len(parts)

    def body(*refs):
        ins, outs = refs[:nt], refs[nt:2 * nt]
        send_sems, recv_sems, loc_sems = refs[2 * nt:]
        x, y, c = _coords()

        def half(t, core):
            n = ins[t].shape[0]
            return outs[t].at[pl.ds(core * n, n)]

        local = [pltpu.make_async_copy(ins[t], half(t, c), loc_sems.at[t]) for t in range(nt)]
        sends = [pltpu.make_async_remote_copy(
            src_ref=ins[t], dst_ref=half(t, c), send_sem=send_sems.at[t], recv_sem=recv_sems.at[t],
            device_id=(x, y, 1 - c), device_id_type=MESH) for t in range(nt)]
        for cp in local + sends:
            cp.start()
        for t in range(nt):
            pltpu.make_async_remote_copy(
                src_ref=ins[t], dst_ref=half(t, 1 - c), send_sem=send_sems.at[t], recv_sem=recv_sems.at[t],
                device_id=(x, y, 1 - c), device_id_type=MESH).wait_recv()
        for cp in sends:
            cp.wait_send()
        for cp in local:
            cp.wait()

    out_shape = tuple(jax.ShapeDtypeStruct((2 * p.shape[0],) + p.shape[1:], p.dtype) for p in parts)
    return pl.pallas_call(
        body, name="exchange_with_sibling", out_shape=out_shape, in_specs=[_HBM] * nt, out_specs=tuple([_HBM] * nt),
        scratch_shapes=[pltpu.SemaphoreType.DMA((nt,)), pltpu.SemaphoreType.DMA((nt,)), pltpu.SemaphoreType.DMA((nt,))],
    )(*parts)


def _gather_over_devices(pack):
    def body(i_ref, o_ref, send_sems, recv_sems, loc_sem):
        x, y, c = _coords()
        me = 4 * x + 2 * y + c
        masks = [(m >> 2 & 1, m >> 1 & 1, m & 1) for m in range(1, N_DEV)]
        local = pltpu.make_async_copy(i_ref, o_ref.at[me], loc_sem)
        sends = [pltpu.make_async_remote_copy(
            src_ref=i_ref, dst_ref=o_ref.at[me], send_sem=send_sems.at[k], recv_sem=recv_sems.at[me],
            device_id=(_flip(x, fx), _flip(y, fy), _flip(c, fc)), device_id_type=MESH)
            for k, (fx, fy, fc) in enumerate(masks)]
        for cp in [local] + sends:
            cp.start()
        for fx, fy, fc in masks:
            px, py, pc = _flip(x, fx), _flip(y, fy), _flip(c, fc)
            peer = 4 * px + 2 * py + pc
            pltpu.make_async_remote_copy(
                src_ref=i_ref, dst_ref=o_ref.at[peer], send_sem=send_sems.at[0], recv_sem=recv_sems.at[peer],
                device_id=(px, py, pc), device_id_type=MESH).wait_recv()
        for cp in sends:
            cp.wait_send()
        local.wait()

    return pl.pallas_call(
        body, name="gather_small_grads", out_shape=jax.ShapeDtypeStruct((N_DEV,) + pack.shape, pack.dtype),
        in_specs=[_HBM], out_specs=_HBM,
        scratch_shapes=[pltpu.SemaphoreType.DMA((N_DEV - 1,)), pltpu.SemaphoreType.DMA((N_DEV,)),
                        pltpu.SemaphoreType.DMA],
    )(pack)


def _adamw(w, g, m, v, name):
    R, C = w.shape
    tr = _pick(R, prefs=(256, 128, 64, 32, 16, 8))
    bc1 = 1.0 - ADAM_B1 ** ADAM_STEP
    bc2 = 1.0 - ADAM_B2 ** ADAM_STEP

    def body(w_ref, g_ref, m_ref, v_ref, d_ref, mo_ref, vo_ref):
        gv = g_ref[...]
        mn = ADAM_B1 * m_ref[...] + (1.0 - ADAM_B1) * gv
        vn = ADAM_B2 * v_ref[...] + (1.0 - ADAM_B2) * (gv * gv)
        d_ref[...] = -ADAM_LR * ((mn / bc1) / (jnp.sqrt(vn / bc2) + ADAM_EPS) + ADAM_WD * w_ref[...])
        mo_ref[...] = mn
        vo_ref[...] = vn

    blk = pl.BlockSpec((tr, C), lambda i: (i, 0))
    sds = jax.ShapeDtypeStruct((R, C), F32)
    return pl.pallas_call(
        body, name=name, out_shape=(sds, sds, sds), grid=(R // tr,), in_specs=[blk] * 4, out_specs=(blk,) * 3,
        compiler_params=_cparams(("parallel",)),
    )(w, g, m, v)


_PACK_QUANTUM = 256 * LANES


def _pack(arrs):
    flat = jnp.concatenate([a.reshape(-1).astype(F32) for a in arrs])
    flat = jnp.pad(flat, (0, (-flat.shape[0]) % _PACK_QUANTUM))
    return flat.reshape(-1, LANES)


def _unpack(p, shapes):
    flat = p.reshape(-1)
    out, off = [], 0
    for s in shapes:
        n = int(np.prod(s))
        out.append(flat[off:off + n].reshape(s))
        off += n
    return out


def kernel(*args):
    nw = len(WEIGHTS)
    x, tgt = args[0], args[1 + nw]
    w = dict(zip(WEIGHTS, args[1:1 + nw]))
    m = dict(zip(WEIGHTS, args[2 + nw:2 + 2 * nw]))
    v = dict(zip(WEIGHTS, args[2 + 2 * nw:2 + 3 * nw]))
    _, L, D = x.shape
    chip = 2 * lax.axis_index("x") + lax.axis_index("y")

    big = list(BIG)
    small_sh_shapes = [w[n].shape for n in SMALL_SHARDED]
    gathered = _gather_over_chips([w[n].astype(BF16) for n in big] + [_pack([w[n] for n in SMALL_SHARDED])[None]],
                                  [BIG[n] for n in big] + [0])
    wbig = dict(zip(big, gathered[:-1]))
    per_chip = [_unpack(gathered[-1][s], small_sh_shapes) for s in range(N_CHIPS)]
    wl = dict(w)
    for k, n in enumerate(SMALL_SHARDED):
        wl[n] = jnp.concatenate([per_chip[s][k] for s in range(N_CHIPS)], axis=-1)

    loss, dx, gsmall, gbig = _local_step(x.reshape(L, D), tgt.reshape(L, D), wl, wbig)
    loss = lax.psum(loss, ("x", "y", "c"))

    landed = _scatter_grads(big, [gbig[n] for n in big], [BIG[n] - 1 for n in big])
    summed = [_sum_slots(r, "sum_big_grads") for r in landed]
    gshard = dict(zip(big, _exchange_with_sibling(summed)))

    small_shapes = [gsmall[n].shape for n in SMALL]
    gpack = _sum_slots(_gather_over_devices(_pack([gsmall[n] for n in SMALL]))[None], "sum_small_grads")[0]
    gs = dict(zip(SMALL, _unpack(gpack, small_shapes)))
    for n in SMALL_SHARDED:
        width = w[n].shape[-1]
        gs[n] = lax.dynamic_slice_in_dim(gs[n], chip * width, width, axis=gs[n].ndim - 1)

    grads, delta, new_m, new_v = {}, {}, {}, {}
    for n in big:
        shp = w[n].shape
        flat = lambda a: a.reshape(shp[0] * shp[1], shp[2])
        g = gshard[n]
        grads[n] = g
        d_, m_, v_ = _adamw(flat(w[n]), flat(g), flat(m[n]), flat(v[n]), "adamw_" + n)
        delta[n], new_m[n], new_v[n] = d_.reshape(shp), m_.reshape(shp), v_.reshape(shp)
    loc_shapes = [w[n].shape for n in SMALL]
    d_, m_, v_ = _adamw(_pack([w[n] for n in SMALL]), _pack([gs[n] for n in SMALL]), _pack([m[n] for n in SMALL]),
                        _pack([v[n] for n in SMALL]), "adamw_small")
    for n, dn, mn, vn in zip(SMALL, _unpack(d_, loc_shapes), _unpack(m_, loc_shapes), _unpack(v_, loc_shapes)):
        grads[n], delta[n], new_m[n], new_v[n] = gs[n], dn, mn, vn

    return (loss, dx.reshape(1, L, D), *[grads[n] for n in WEIGHTS], *[delta[n] for n in WEIGHTS],
            *[new_m[n] for n in WEIGHTS], *[new_v[n] for n in WEIGHTS])
```

```python
import functools
import math

import numpy as np
import jax
import jax.numpy as jnp
from jax import lax
from jax.experimental import pallas as pl
from jax.experimental.pallas import tpu as pltpu

F32 = jnp.float32
BF16 = jnp.bfloat16
MESH = pl.DeviceIdType.MESH

EPS = 1e-6
CHUNK = 128
POOL_WINDOWS = (2, 4, 8, 16)
LANES = 128
SUBLANES = 8
SCAN_CHUNKS = SUBLANES
S5_GROUPS_PER_STEP = 4
MM_TM_CAP, MM_TN_CAP, MM_TK_CAP = 1024, 1408, 1408
MM_TK_WHOLE = 2048
VMEM_LIMIT = 48 * 1024 * 1024
VMEM_LIMIT_S5 = 56 * 1024 * 1024

ADAM_LR, ADAM_B1, ADAM_B2, ADAM_EPS, ADAM_WD, ADAM_STEP = 0.001, 0.9, 0.999, 1e-08, 0.01, 10

WEIGHTS = ['norm_mix_g', 'even_w_in', 'even_conv_w', 'ssm_log_step', 'ssm_a_re', 'ssm_a_im', 'ssm_b_re',
           'ssm_b_im', 'ssm_c_re', 'ssm_c_im', 'ssm_d', 'ssm_glu_w', 'ssm_glu_b', 'even_w_out', 'odd_w_in',
           'pool_w', 'pool_scale', 'sgu_norm_g', 'sgu_w', 'sgu_b', 'odd_w_out', 'norm_ffn_g', 'ffn_w_up',
           'ffn_conv_w', 'ffn_conv_b', 'ffn_w_down', 'norm_final_g']
BIG = {'even_w_in': 2, 'ssm_glu_w': 1, 'even_w_out': 1, 'odd_w_in': 2, 'odd_w_out': 1, 'ffn_w_up': 2,
       'ffn_w_down': 1}
SMALL_SHARDED = ('even_conv_w', 'pool_scale', 'sgu_norm_g', 'ffn_conv_w')
SMALL = [n for n in WEIGHTS if n not in BIG]
N_CHIPS = 4
N_DEV = 8


def _cparams(sem=None, vmem=VMEM_LIMIT):
    kw = dict(vmem_limit_bytes=vmem)
    if sem is not None:
        kw['dimension_semantics'] = sem
    return pltpu.CompilerParams(**kw)


def _pick(n, segs=(), prefs=(1024, 512, 256, 128)):
    for t in prefs:
        if n % t == 0 and all(s % t == 0 for s in segs if s):
            return t
    return n


def _largest_tile(n, segs, cap):
    best = None
    for t in range(LANES, min(n, cap) + 1, LANES):
        if n % t == 0 and all(s % t == 0 for s in segs if s):
            best = t
    return best if best is not None else n


def _ldims(arr, kind):
    if kind is None:
        return arr.shape
    if kind[0] == 'lead':
        return arr.shape[1:]
    return (arr.shape[1], arr.shape[0] * arr.shape[2])


def _segw(arr, kind):
    return arr.shape[2] if (kind is not None and kind[0] == 'seg') else None


def _opspec(arr, kind, br, bc, rfn, cfn):
    if kind is None:
        return pl.BlockSpec((br, bc), lambda i, j, k: (rfn(i, j, k), cfn(i, j, k)))
    if kind[0] == 'lead':
        lead = kind[1]
        return pl.BlockSpec((None, br, bc), lambda i, j, k: (lead, rfn(i, j, k), cfn(i, j, k)))
    per = arr.shape[2] // bc
    return pl.BlockSpec((None, br, bc), lambda i, j, k: (cfn(i, j, k) // per, rfn(i, j, k), cfn(i, j, k) % per))


def _mm(a, b, mode, out_dtype, name, ak=None, bk=None, ok=None, res=None):
    ar, ac = _ldims(a, ak)
    br_, bc_ = _ldims(b, bk)
    if mode == 'nn':
        M, K, N = ar, ac, bc_
        assert br_ == K
    else:
        M, K, N = ar, ac, br_
        assert bc_ == K
    sa, sb = _segw(a, ak), _segw(b, bk)
    so = (N // ok[1]) if ok is not None else None
    tm = _largest_tile(M, [], MM_TM_CAP)
    tn = _largest_tile(N, [sb if mode == 'nn' else None, so], MM_TN_CAP)
    ksegs = [sa, sb if mode == 'nt' else None]
    tk = K if (K <= MM_TK_WHOLE and not any(ksegs)) else _largest_tile(K, ksegs, MM_TK_CAP)
    nk = K // tk
    I = lambda i, j, k: i
    J = lambda i, j, k: j
    Kk = lambda i, j, k: k
    a_spec = _opspec(a, ak, tm, tk, I, Kk)
    if mode == 'nn':
        b_spec = _opspec(b, bk, tk, tn, Kk, J)
        dims = (((1,), (0,)), ((), ()))
    else:
        b_spec = _opspec(b, bk, tn, tk, J, Kk)
        dims = (((1,), (1,)), ((), ()))
    if ok is None:
        out_shape = jax.ShapeDtypeStruct((M, N), out_dtype)
        o_spec = pl.BlockSpec((tm, tn), lambda i, j, k: (i, j))
    else:
        out_shape = jax.ShapeDtypeStruct((ok[1], M, N // ok[1]), out_dtype)
        per = (N // ok[1]) // tn
        o_spec = pl.BlockSpec((None, tm, tn), lambda i, j, k: (j // per, i, j % per))
    has_res = res is not None

    def body(*refs):
        a_ref, b_ref = refs[0], refs[1]
        r_ref = refs[2] if has_res else None
        o_ref = refs[3] if has_res else refs[2]
        prod = lax.dot_general(a_ref[...].astype(BF16), b_ref[...].astype(BF16), dims, preferred_element_type=F32)
        if nk == 1:
            o_ref[...] = (prod + r_ref[...] if has_res else prod).astype(out_dtype)
            return
        acc = refs[-1]
        k = pl.program_id(2)

        @pl.when(k == 0)
        def _():
            acc[...] = prod

        @pl.when(k > 0)
        def _():
            acc[...] += prod

        @pl.when(k == nk - 1)
        def _():
            o = acc[...]
            if has_res:
                o = o + r_ref[...]
            o_ref[...] = o.astype(out_dtype)

    in_specs = [a_spec, b_spec]
    args = [a, b]
    if has_res:
        in_specs.append(pl.BlockSpec((tm, tn), lambda i, j, k: (i, j)))
        args.append(res)
    return pl.pallas_call(
        body, name=name, out_shape=out_shape, grid=(M // tm, N // tn, nk), in_specs=in_specs, out_specs=o_spec,
        scratch_shapes=[pltpu.VMEM((tm, tn), F32)] if nk > 1 else [],
        compiler_params=_cparams(("parallel", "parallel", "arbitrary")),
    )(*args)


_G0 = math.sqrt(2.0 / math.pi)
_G1 = 0.044715


def _gelu(x):
    return 0.5 * x * (1.0 + jnp.tanh(_G0 * (x + _G1 * x * x * x)))


def _gelu_grad(x):
    x2 = x * x
    t = jnp.tanh(_G0 * (x + _G1 * x * x2))
    return 0.5 * (1.0 + t) + 0.5 * x * (1.0 - t * t) * (_G0 * (1.0 + 3.0 * _G1 * x2))


def _sigmoid(x):
    return 1.0 / (1.0 + jnp.exp(-x))


def _down(v, k):
    row = lax.broadcasted_iota(jnp.int32, v.shape, 0)
    return jnp.where(row >= k, pltpu.roll(v, k, axis=0), 0.0)


def _up(v, k):
    n = v.shape[0]
    row = lax.broadcasted_iota(jnp.int32, v.shape, 0)
    return jnp.where(row < n - k, pltpu.roll(v, n - k, axis=0), 0.0)


def _conv3(v, w):
    return w[0:1, :] * _down(v, 2) + w[1:2, :] * _down(v, 1) + w[2:3, :] * v


def _conv3_t(dv, w):
    return w[2:3, :] * dv + w[1:2, :] * _up(dv, 1) + w[0:1, :] * _up(dv, 2)


def _conv3_dw(dv, v):
    return (jnp.sum(dv * _down(v, 2), axis=0, keepdims=True),
            jnp.sum(dv * _down(v, 1), axis=0, keepdims=True),
            jnp.sum(dv * v, axis=0, keepdims=True))


def _cmul(ar, ai, br, bi):
    return ar * br - ai * bi, ar * bi + ai * br


def _cpow(lr, li, n):
    rr = ri = None
    br, bi = lr, li
    while n:
        if n & 1:
            rr, ri = (br, bi) if rr is None else _cmul(rr, ri, br, bi)
        n >>= 1
        if n:
            br, bi = _cmul(br, bi, br, bi)
    return rr, ri


def _rms_fwd(x, g, name):
    L, D = x.shape
    tr = _pick(L, prefs=(512, 256, 128))

    def body(x_ref, g_ref, h_ref):
        xv = x_ref[...]
        r = lax.rsqrt(jnp.mean(xv * xv, axis=-1, keepdims=True) + EPS)
        h_ref[...] = (xv * r * g_ref[...]).astype(BF16)

    return pl.pallas_call(
        body, name=name, out_shape=jax.ShapeDtypeStruct((L, D), BF16), grid=(L // tr,),
        in_specs=[pl.BlockSpec((tr, D), lambda i: (i, 0)), pl.BlockSpec((1, D), lambda i: (0, 0))],
        out_specs=pl.BlockSpec((tr, D), lambda i: (i, 0)), compiler_params=_cparams(("parallel",)),
    )(x, g.reshape(1, D))


def _rms_bwd(x, g, dh, dres, name):
    L, D = x.shape
    tr = _pick(L, prefs=(512, 256, 128))
    nsteps = L // tr

    def body(x_ref, g_ref, dh_ref, dres_ref, dx_ref, dg_ref, acc):
        i = pl.program_id(0)

        @pl.when(i == 0)
        def _():
            acc[...] = jnp.zeros_like(acc)

        xv = x_ref[...]
        r = lax.rsqrt(jnp.mean(xv * xv, axis=-1, keepdims=True) + EPS)
        xh = xv * r
        dhv = dh_ref[...].astype(F32)
        acc[...] += jnp.sum((dhv * xh).reshape(tr // SUBLANES, SUBLANES, D), axis=0)
        dxh = dhv * g_ref[...]
        dx_ref[...] = dres_ref[...] + r * (dxh - xh * jnp.mean(dxh * xh, axis=-1, keepdims=True))

        @pl.when(i == nsteps - 1)
        def _():
            dg_ref[...] = jnp.sum(acc[...], axis=0, keepdims=True)

    row = pl.BlockSpec((tr, D), lambda i: (i, 0))
    vec = pl.BlockSpec((1, D), lambda i: (0, 0))
    return pl.pallas_call(
        body, name=name, out_shape=(jax.ShapeDtypeStruct((L, D), F32), jax.ShapeDtypeStruct((1, D), F32)),
        grid=(nsteps,), in_specs=[row, vec, row, row], out_specs=(row, vec),
        scratch_shapes=[pltpu.VMEM((SUBLANES, D), F32)], compiler_params=_cparams(("arbitrary",)),
    )(x, g.reshape(1, D), dh, dres)


def _loss_head(x, g, tgt):
    L, D = x.shape
    tr = _pick(L, prefs=(512, 256, 128))
    nsteps = L // tr

    def body(x_ref, g_ref, t_ref, loss_ref, dx_ref, dg_ref, acc_g, acc_l):
        i = pl.program_id(0)

        @pl.when(i == 0)
        def _():
            acc_g[...] = jnp.zeros_like(acc_g)
            acc_l[...] = jnp.zeros_like(acc_l)

        xv = x_ref[...]
        gv = g_ref[...]
        r = lax.rsqrt(jnp.mean(xv * xv, axis=-1, keepdims=True) + EPS)
        xh = xv * r
        e = xh * gv - t_ref[...]
        acc_l[...] += jnp.sum((e * e).reshape(tr // SUBLANES, SUBLANES, D), axis=0)
        dy = e * (1.0 / D)
        acc_g[...] += jnp.sum((dy * xh).reshape(tr // SUBLANES, SUBLANES, D), axis=0)
        dxh = dy * gv
        dx_ref[...] = r * (dxh - xh * jnp.mean(dxh * xh, axis=-1, keepdims=True))

        @pl.when(i == nsteps - 1)
        def _():
            dg_ref[...] = jnp.sum(acc_g[...], axis=0, keepdims=True)
            tot = jnp.sum(jnp.sum(acc_l[...], axis=0, keepdims=True), axis=1, keepdims=True) * (0.5 / D)
            loss_ref[...] = jnp.broadcast_to(tot, (SUBLANES, LANES))

    row = pl.BlockSpec((tr, D), lambda i: (i, 0))
    vec = pl.BlockSpec((1, D), lambda i: (0, 0))
    return pl.pallas_call(
        body, name="loss_head",
        out_shape=(jax.ShapeDtypeStruct((SUBLANES, LANES), F32), jax.ShapeDtypeStruct((L, D), F32),
                   jax.ShapeDtypeStruct((1, D), F32)),
        grid=(nsteps,), in_specs=[row, vec, row],
        out_specs=(pl.BlockSpec((SUBLANES, LANES), lambda i: (0, 0)), row, vec),
        scratch_shapes=[pltpu.VMEM((SUBLANES, D), F32), pltpu.VMEM((SUBLANES, D), F32)],
        compiler_params=_cparams(("arbitrary",)),
    )(x, g.reshape(1, D), tgt)


def _sconv_fwd(proj4, conv_w, name):
    _, L, C = proj4.shape
    cb = LANES

    def body(p_ref, w_ref, o_ref):
        xa, ba, ca = p_ref[0], p_ref[1], p_ref[2]
        o_ref[...] = (ba * _conv3(ca * xa, w_ref[...])).astype(BF16)

    return pl.pallas_call(
        body, name=name, out_shape=jax.ShapeDtypeStruct((L, C), BF16), grid=(C // cb,),
        in_specs=[pl.BlockSpec((3, L, cb), lambda j: (0, 0, j)), pl.BlockSpec((3, cb), lambda j: (0, j))],
        out_specs=pl.BlockSpec((L, cb), lambda j: (0, j)), compiler_params=_cparams(("parallel",)),
    )(proj4, conv_w)


def _sconv_bwd(proj4, dmix, conv_w, name):
    _, L, C = proj4.shape
    cb = LANES

    def body(p_ref, d_ref, w_ref, o_ref, dw_ref):
        xa, ba, ca = p_ref[0], p_ref[1], p_ref[2]
        w = w_ref[...]
        dya = d_ref[...]
        q = ca * xa
        cq = _conv3(q, w)
        dcq = dya * ba
        dq = _conv3_t(dcq, w)
        for tap, dwt in enumerate(_conv3_dw(dcq, q)):
            dw_ref[tap:tap + 1, :] = dwt
        o_ref[0] = (dq * ca).astype(BF16)
        o_ref[1] = (dya * cq).astype(BF16)
        o_ref[2] = (dq * xa).astype(BF16)

    return pl.pallas_call(
        body, name=name,
        out_shape=(jax.ShapeDtypeStruct((3, L, C), BF16), jax.ShapeDtypeStruct((3, C), F32)), grid=(C // cb,),
        in_specs=[pl.BlockSpec((3, L, cb), lambda j: (0, 0, j)), pl.BlockSpec((L, cb), lambda j: (0, j)),
                  pl.BlockSpec((3, cb), lambda j: (0, j))],
        out_specs=(pl.BlockSpec((3, L, cb), lambda j: (0, 0, j)), pl.BlockSpec((3, cb), lambda j: (0, j))),
        compiler_params=_cparams(("parallel",)),
    )(proj4, dmix, conv_w)


def _to_scan_order(v):
    L, C = v.shape
    return v.reshape(SCAN_CHUNKS, L // SCAN_CHUNKS, C).transpose(1, 0, 2).reshape(L, C)


def _from_scan_order(v):
    L, C = v.shape
    return v.reshape(L // SCAN_CHUNKS, SCAN_CHUNKS, C).transpose(1, 0, 2).reshape(L, C)


def _s5_prep(log_step, a_re, a_im, b_re, b_im, c_re, c_im):
    G, P = a_re.shape
    H = b_re.shape[-1]
    gs = S5_GROUPS_PER_STEP
    ns = G // gs
    gu = LANES // H
    lam = lax.complex(a_re, a_im)
    step = jnp.exp(log_step)[:, None]
    lam_bar = jnp.exp(lam * step)
    b_bar = ((lam_bar - 1.0) / lam)[..., None] * lax.complex(b_re, b_im)
    lr = jnp.real(lam_bar).reshape(ns, 1, gs * P)
    li = jnp.imag(lam_bar).reshape(ns, 1, gs * P)
    k = np.arange(ns)[:, None, None]
    oh = jnp.asarray((np.arange(gu)[None, :, None] == gs * (k % (gu // gs)) + np.arange(gs)[None, None, :]),
                     F32)
    bre = jnp.einsum('kgl,klph->kghlp', oh, jnp.real(b_bar).reshape(ns, gs, P, H)).reshape(ns, gu * H, gs * P)
    bim = jnp.einsum('kgl,klph->kghlp', oh, jnp.imag(b_bar).reshape(ns, gs, P, H)).reshape(ns, gu * H, gs * P)
    cre = jnp.einsum('kgl,klhp->klpgh', oh, c_re.reshape(ns, gs, H, P)).reshape(ns, gs * P, gu * H)
    cim = jnp.einsum('kgl,klhp->klpgh', oh, c_im.reshape(ns, gs, H, P)).reshape(ns, gs * P, gu * H)
    return lr, li, jnp.concatenate([bre, bim], axis=2), jnp.concatenate([cre, -cim], axis=1)


def _carry_tile(fr, fi, pr, pi, reverse):
    row = lax.broadcasted_iota(jnp.int32, fr.shape, 0)
    cr = jnp.zeros_like(fr)
    ci = jnp.zeros_like(fi)
    sr = jnp.zeros_like(fr[0:1])
    si = jnp.zeros_like(sr)
    order = range(SCAN_CHUNKS - 1, 0, -1) if reverse else range(0, SCAN_CHUNKS - 1)
    for c in order:
        fcr = jnp.sum(jnp.where(row == c, fr, 0.0), axis=0, keepdims=True)
        fci = jnp.sum(jnp.where(row == c, fi, 0.0), axis=0, keepdims=True)
        mr, mi = _cmul(pr, pi, sr, si)
        sr, si = mr + fcr, mi + fci
        nxt = c - 1 if reverse else c + 1
        cr = jnp.where(row == nxt, sr, cr)
        ci = jnp.where(row == nxt, si, ci)
    return cr, ci


def _s5_fwd(u, lr, li, bmat, cmat, d, name):
    L, Du = u.shape
    ns, _, W2 = bmat.shape
    W = W2 // 2
    T = L // SCAN_CHUNKS
    rb = _pick(L, prefs=(512, 256, 128))
    per = (ns * LANES) // Du

    def body(u_ref, lr_ref, li_ref, b_ref, c_ref, d_ref, y_ref, sr_ref, si_ref):
        k = pl.program_id(0)
        for r in range(L // rb):
            rows = pl.ds(r * rb, rb)
            bu = jnp.dot(u_ref[rows, :].astype(BF16), b_ref[...], preferred_element_type=F32)
            sr_ref[rows, :] = bu[:, :W]
            si_ref[rows, :] = bu[:, W:]
        lam_r = jnp.broadcast_to(lr_ref[...], (SUBLANES, W))
        lam_i = jnp.broadcast_to(li_ref[...], (SUBLANES, W))

        def local(t, carry):
            sr, si = carry
            rows = pl.ds(pl.multiple_of(t * SUBLANES, SUBLANES), SUBLANES)
            mr, mi = _cmul(lam_r, lam_i, sr, si)
            sr = mr + sr_ref[rows, :]
            si = mi + si_ref[rows, :]
            sr_ref[rows, :] = sr
            si_ref[rows, :] = si
            return sr, si

        z = jnp.zeros((SUBLANES, W), F32)
        fr, fi = lax.fori_loop(0, T, local, (z, z))
        pr, pi = _cpow(lam_r, lam_i, T)
        cr, ci = _carry_tile(fr, fi, pr[0:1], pi[0:1], reverse=False)

        def fix(t, carry):
            wr, wi = carry
            rows = pl.ds(pl.multiple_of(t * SUBLANES, SUBLANES), SUBLANES)
            ar, ai = _cmul(wr, wi, cr, ci)
            sr_ref[rows, :] += ar
            si_ref[rows, :] += ai
            return _cmul(wr, wi, lam_r, lam_i)

        lax.fori_loop(0, T, fix, (lam_r, lam_i))
        first = (k % per) == 0
        for r in range(L // rb):
            rows = pl.ds(r * rb, rb)
            s = jnp.concatenate([sr_ref[rows, :], si_ref[rows, :]], axis=1).astype(BF16)
            y = jnp.dot(s, c_ref[...], preferred_element_type=F32)

            @pl.when(first)
            def _():
                y_ref[rows, :] = y + d_ref[...] * u_ref[rows, :]

            @pl.when(jnp.logical_not(first))
            def _():
                y_ref[rows, :] += y

    ublk = pl.BlockSpec((L, LANES), lambda k: (0, k // per))
    sblk = pl.BlockSpec((L, W), lambda k: (0, k))
    lam = pl.BlockSpec((None, 1, W), lambda k: (k, 0, 0))
    return pl.pallas_call(
        body, name=name,
        out_shape=(jax.ShapeDtypeStruct((L, Du), F32), jax.ShapeDtypeStruct((L, ns * W), F32),
                   jax.ShapeDtypeStruct((L, ns * W), F32)),
        grid=(ns,),
        in_specs=[ublk, lam, lam, pl.BlockSpec((None, LANES, 2 * W), lambda k: (k, 0, 0)),
                  pl.BlockSpec((None, 2 * W, LANES), lambda k: (k, 0, 0)),
                  pl.BlockSpec((1, LANES), lambda k: (0, k // per))],
        out_specs=(ublk, sblk, sblk), compiler_params=_cparams(("arbitrary",), VMEM_LIMIT_S5),
    )(u, lr, li, bmat.astype(BF16), cmat.astype(BF16), d.reshape(1, Du))


def _s5_bwd(dy, u, s_re, s_im, lr, li, bmat, cmat, d, name):
    L, Du = u.shape
    ns, _, W2 = bmat.shape
    W = W2 // 2
    T = L // SCAN_CHUNKS
    rb = _pick(L, prefs=(512, 256, 128))
    per = (ns * LANES) // Du
    NT = (((1,), (1,)), ((), ()))
    TN = (((0,), (0,)), ((), ()))

    def body(dy_ref, u_ref, sr_ref, si_ref, lr_ref, li_ref, b_ref, c_ref, d_ref,
             du_ref, db_ref, dc_ref, dl_ref, dd_ref, gr_ref, gi_ref):
        k = pl.program_id(0)
        for r in range(L // rb):
            rows = pl.ds(r * rb, rb)
            g = lax.dot_general(dy_ref[rows, :].astype(BF16), c_ref[...], NT, preferred_element_type=F32)
            gr_ref[rows, :] = g[:, :W]
            gi_ref[rows, :] = g[:, W:]
        lam_r = jnp.broadcast_to(lr_ref[...], (SUBLANES, W))
        lam_i = -jnp.broadcast_to(li_ref[...], (SUBLANES, W))

        def local(i, carry):
            gr, gi = carry
            rows = pl.ds(pl.multiple_of((T - 1 - i) * SUBLANES, SUBLANES), SUBLANES)
            mr, mi = _cmul(lam_r, lam_i, gr, gi)
            gr = mr + gr_ref[rows, :]
            gi = mi + gi_ref[rows, :]
            gr_ref[rows, :] = gr
            gi_ref[rows, :] = gi
            return gr, gi

        z = jnp.zeros((SUBLANES, W), F32)
        fr, fi = lax.fori_loop(0, T, local, (z, z))
        pr, pi = _cpow(lam_r, lam_i, T)
        cr, ci = _carry_tile(fr, fi, pr[0:1], pi[0:1], reverse=True)

        def true_g(rows, wr, wi):
            ar, ai = _cmul(wr, wi, cr, ci)
            gr = gr_ref[rows, :] + ar
            gi = gi_ref[rows, :] + ai
            gr_ref[rows, :] = gr
            gi_ref[rows, :] = gi
            return gr, gi

        def fix(i, carry):
            wr, wi, ar_, ai_ = carry
            t = T - 1 - i
            rows = pl.ds(pl.multiple_of(t * SUBLANES, SUBLANES), SUBLANES)
            prev = pl.ds(pl.multiple_of((t - 1) * SUBLANES, SUBLANES), SUBLANES)
            gr, gi = true_g(rows, wr, wi)
            qr, qi = sr_ref[prev, :], si_ref[prev, :]
            ar_ = ar_ + gr * qr + gi * qi
            ai_ = ai_ + gi * qr - gr * qi
            wr, wi = _cmul(wr, wi, lam_r, lam_i)
            return wr, wi, ar_, ai_

        wr, wi, acc_r, acc_i = lax.fori_loop(0, T - 1, fix, (lam_r, lam_i, z, z))
        gr, gi = true_g(pl.ds(0, SUBLANES), wr, wi)
        last = pl.ds((T - 1) * SUBLANES, SUBLANES)
        row = lax.broadcasted_iota(jnp.int32, (SUBLANES, W), 0)
        qr = jnp.where(row >= 1, pltpu.roll(sr_ref[last, :], 1, axis=0), 0.0)
        qi = jnp.where(row >= 1, pltpu.roll(si_ref[last, :], 1, axis=0), 0.0)
        acc_r = acc_r + gr * qr + gi * qi
        acc_i = acc_i + gi * qr - gr * qi
        dl_ref[0:1, :] = jnp.sum(acc_r, axis=0, keepdims=True)
        dl_ref[1:2, :] = jnp.sum(acc_i, axis=0, keepdims=True)

        first = (k % per) == 0
        db = jnp.zeros((LANES, 2 * W), F32)
        dc = jnp.zeros((LANES, 2 * W), F32)
        dd = jnp.zeros((1, LANES), F32)
        for r in range(L // rb):
            rows = pl.ds(r * rb, rb)
            gb = jnp.concatenate([gr_ref[rows, :], gi_ref[rows, :]], axis=1).astype(BF16)
            sb = jnp.concatenate([sr_ref[rows, :], si_ref[rows, :]], axis=1).astype(BF16)
            dyv = dy_ref[rows, :]
            uv = u_ref[rows, :]
            du = lax.dot_general(gb, b_ref[...], NT, preferred_element_type=F32)
            db = db + lax.dot_general(uv.astype(BF16), gb, TN, preferred_element_type=F32)
            dc = dc + lax.dot_general(dyv.astype(BF16), sb, TN, preferred_element_type=F32)
            dd = dd + jnp.sum(dyv * uv, axis=0, keepdims=True)

            @pl.when(first)
            def _():
                du_ref[rows, :] = du + d_ref[...] * dyv

            @pl.when(jnp.logical_not(first))
            def _():
                du_ref[rows, :] += du

        db_ref[...] = db
        dc_ref[...] = dc

        @pl.when(first)
        def _():
            dd_ref[...] = dd

    ublk = pl.BlockSpec((L, LANES), lambda k: (0, k // per))
    sblk = pl.BlockSpec((L, W), lambda k: (0, k))
    lam = pl.BlockSpec((None, 1, W), lambda k: (k, 0, 0))
    vec = pl.BlockSpec((1, LANES), lambda k: (0, k // per))
    mat = pl.BlockSpec((None, LANES, 2 * W), lambda k: (k, 0, 0))
    return pl.pallas_call(
        body, name=name,
        out_shape=(jax.ShapeDtypeStruct((L, Du), F32), jax.ShapeDtypeStruct((ns, LANES, 2 * W), F32),
                   jax.ShapeDtypeStruct((ns, LANES, 2 * W), F32), jax.ShapeDtypeStruct((ns, 2, W), F32),
                   jax.ShapeDtypeStruct((1, Du), F32)),
        grid=(ns,),
        in_specs=[ublk, ublk, sblk, sblk, lam, lam, mat,
                  pl.BlockSpec((None, 2 * W, LANES), lambda k: (k, 0, 0)), vec],
        out_specs=(ublk, mat, mat, pl.BlockSpec((None, 2, W), lambda k: (k, 0, 0)), vec),
        scratch_shapes=[pltpu.VMEM((L, W), F32), pltpu.VMEM((L, W), F32)],
        compiler_params=_cparams(("arbitrary",), VMEM_LIMIT_S5),
    )(dy, u, s_re, s_im, lr, li, bmat.astype(BF16), cmat.astype(BF16), d.reshape(1, Du))


def _glu_fwd(yraw, w3, layer, bias, name):
    L, C = yraw.shape
    tr = _pick(L, prefs=(512, 256, 128))

    def body(y_ref, w_ref, b_ref, o_ref):
        yg = _gelu(y_ref[...])
        zz = jnp.dot(yg.astype(BF16), w_ref[...], preferred_element_type=F32) + b_ref[...]
        o_ref[...] = (yg * _sigmoid(zz)).astype(BF16)

    return pl.pallas_call(
        body, name=name, out_shape=jax.ShapeDtypeStruct((L, C), BF16), grid=(L // tr,),
        in_specs=[pl.BlockSpec((tr, C), lambda i: (i, 0)), pl.BlockSpec((None, C, C), lambda i: (layer, 0, 0)),
                  pl.BlockSpec((1, C), lambda i: (0, 0))],
        out_specs=pl.BlockSpec((tr, C), lambda i: (i, 0)), compiler_params=_cparams(("parallel",)),
    )(yraw, w3, bias.reshape(1, C))


def _glu_bwd(yraw, dyb, w3, layer, bias, name):
    L, C = yraw.shape
    tr = _pick(L, prefs=(512, 256, 128))
    nsteps = L // tr

    def body(y_ref, d_ref, w_ref, b_ref, dy_ref, dw_ref, db_ref, acc_b):
        i = pl.program_id(0)

        @pl.when(i == 0)
        def _():
            dw_ref[...] = jnp.zeros_like(dw_ref)
            acc_b[...] = jnp.zeros_like(acc_b)

        yr = y_ref[...]
        yg = _gelu(yr)
        ygb = yg.astype(BF16)
        sg = _sigmoid(jnp.dot(ygb, w_ref[...], preferred_element_type=F32) + b_ref[...])
        dyb_ = d_ref[...]
        dz = dyb_ * yg * sg * (1.0 - sg)
        dzb = dz.astype(BF16)
        dyg = dyb_ * sg + lax.dot_general(dzb, w_ref[...], (((1,), (1,)), ((), ())), preferred_element_type=F32)
        dw_ref[...] += lax.dot_general(ygb, dzb, (((0,), (0,)), ((), ())), preferred_element_type=F32)
        acc_b[...] += jnp.sum(dz.reshape(tr // SUBLANES, SUBLANES, C), axis=0)
        dy_ref[...] = dyg * _gelu_grad(yr)

        @pl.when(i == nsteps - 1)
        def _():
            db_ref[...] = jnp.sum(acc_b[...], axis=0, keepdims=True)

    row = pl.BlockSpec((tr, C), lambda i: (i, 0))
    return pl.pallas_call(
        body, name=name,
        out_shape=(jax.ShapeDtypeStruct((L, C), F32), jax.ShapeDtypeStruct((C, C), F32),
                   jax.ShapeDtypeStruct((1, C), F32)),
        grid=(nsteps,),
        in_specs=[row, row, pl.BlockSpec((None, C, C), lambda i: (layer, 0, 0)), pl.BlockSpec((1, C), lambda i: (0, 0))],
        out_specs=(row, pl.BlockSpec((C, C), lambda i: (0, 0)), pl.BlockSpec((1, C), lambda i: (0, 0))),
        scratch_shapes=[pltpu.VMEM((SUBLANES, C), F32)], compiler_params=_cparams(("arbitrary",)),
    )(yraw, dyb, w3, bias.reshape(1, C))


def _pool_counts(L, g):
    t = lax.broadcasted_iota(jnp.int32, (L, LANES), 0).astype(F32) + 1.0
    w = jnp.where(g == 0, 2.0, jnp.where(g == 1, 4.0, jnp.where(g == 2, 8.0, 16.0)))
    return 1.0 / jnp.minimum(t, w)


def _select_window(g, a2, a4, a8, a16):
    return jnp.where(g == 0, a2, jnp.where(g == 1, a4, jnp.where(g == 2, a8, a16)))


def _pooled(z, g):
    a2 = z + _down(z, 1)
    a4 = a2 + _down(a2, 2)
    a8 = a4 + _down(a4, 4)
    a16 = a8 + _down(a8, 8)
    return _select_window(g, a2, a4, a8, a16) * _pool_counts(z.shape[0], g) - z


def _pool_fwd(proj3, pool_w, scale, name):
    _, L, C = proj3.shape
    ng = len(POOL_WINDOWS)
    pg = C // ng
    assert pg == LANES

    def body(z_ref, w_ref, s_ref, o_ref):
        g = pl.program_id(0)
        p = _pooled(z_ref[...], g)
        y = jnp.dot(p.astype(BF16), w_ref[...].astype(BF16), preferred_element_type=F32)
        o_ref[...] = (y * s_ref[...]).astype(BF16)

    return pl.pallas_call(
        body, name=name, out_shape=jax.ShapeDtypeStruct((L, C), BF16), grid=(ng,),
        in_specs=[pl.BlockSpec((None, L, pg), lambda g: (0, 0, g)), pl.BlockSpec((None, pg, pg), lambda g: (g, 0, 0)),
                  pl.BlockSpec((1, pg), lambda g: (0, g))],
        out_specs=pl.BlockSpec((L, pg), lambda g: (0, g)), compiler_params=_cparams(("parallel",)),
    )(proj3, pool_w, scale.reshape(1, C))


def _pool_bwd(proj3, dmix, pool_w, scale, name):
    _, L, C = proj3.shape
    ng = len(POOL_WINDOWS)
    pg = C // ng

    def body(z_ref, d_ref, w_ref, s_ref, dz_ref, dw_ref, ds_ref):
        g = pl.program_id(0)
        p = _pooled(z_ref[...], g)
        pb = p.astype(BF16)
        wb = w_ref[...].astype(BF16)
        pre = jnp.dot(pb, wb, preferred_element_type=F32)
        dyc = d_ref[...]
        ds_ref[...] = jnp.sum(dyc * pre, axis=0, keepdims=True)
        dpre = (dyc * s_ref[...]).astype(BF16)
        dw_ref[...] = lax.dot_general(pb, dpre, (((0,), (0,)), ((), ())), preferred_element_type=F32)
        dp = lax.dot_general(dpre, wb, (((1,), (1,)), ((), ())), preferred_element_type=F32)
        v = dp * _pool_counts(L, g)
        a2 = v + _up(v, 1)
        a4 = a2 + _up(a2, 2)
        a8 = a4 + _up(a4, 4)
        a16 = a8 + _up(a8, 8)
        dz_ref[...] = (_select_window(g, a2, a4, a8, a16) - dp).astype(BF16)

    return pl.pallas_call(
        body, name=name,
        out_shape=(jax.ShapeDtypeStruct((L, C), BF16), jax.ShapeDtypeStruct((ng, pg, pg), F32),
                   jax.ShapeDtypeStruct((1, C), F32)),
        grid=(ng,),
        in_specs=[pl.BlockSpec((None, L, pg), lambda g: (0, 0, g)), pl.BlockSpec((L, pg), lambda g: (0, g)),
                  pl.BlockSpec((None, pg, pg), lambda g: (g, 0, 0)), pl.BlockSpec((1, pg), lambda g: (0, g))],
        out_specs=(pl.BlockSpec((L, pg), lambda g: (0, g)), pl.BlockSpec((None, pg, pg), lambda g: (g, 0, 0)),
                   pl.BlockSpec((1, pg), lambda g: (0, g))),
        compiler_params=_cparams(("parallel",)),
    )(proj3, dmix, pool_w, scale.reshape(1, C))


def _tril_w(w_ref, h):
    r = lax.broadcasted_iota(jnp.int32, (CHUNK, CHUNK), 0)
    c = lax.broadcasted_iota(jnp.int32, (CHUNK, CHUNK), 1)
    return jnp.where(r >= c, w_ref[h], 0.0)


def _sgu_fwd(proj3, norm_g, w, b, name):
    _, L, C = proj3.shape
    nh = w.shape[0]
    dh = C // nh
    assert dh == LANES and w.shape[1] == CHUNK
    tr = _pick(L, prefs=(512, 256, 128))
    bfull = jnp.broadcast_to(b[:, :, None], (nh, CHUNK, dh))

    def body(su_ref, sv_ref, g_ref, w_ref, b_ref, o_ref):
        sv = _gelu(sv_ref[...])
        r = lax.rsqrt(jnp.mean(sv * sv, axis=-1, keepdims=True) + EPS)
        v = (sv * r * g_ref[...]).astype(BF16)
        for h in range(nh):
            wm = _tril_w(w_ref, h).astype(BF16)
            cols = slice(h * dh, (h + 1) * dh)
            for n in range(tr // CHUNK):
                rows = slice(n * CHUNK, (n + 1) * CHUNK)
                mixed = jnp.dot(wm, v[rows, cols], preferred_element_type=F32) + b_ref[h]
                o_ref[rows, cols] = (_gelu(su_ref[rows, cols]) * mixed).astype(BF16)

    full = lambda shp: pl.BlockSpec(shp, lambda i: (0,) * len(shp))
    return pl.pallas_call(
        body, name=name, out_shape=jax.ShapeDtypeStruct((L, C), BF16), grid=(L // tr,),
        in_specs=[pl.BlockSpec((None, tr, C), lambda i: (1, i, 0)), pl.BlockSpec((None, tr, C), lambda i: (2, i, 0)),
                  full((1, C)), full((nh, CHUNK, CHUNK)), full((nh, CHUNK, dh))],
        out_specs=pl.BlockSpec((tr, C), lambda i: (i, 0)), compiler_params=_cparams(("parallel",)),
    )(proj3, proj3, norm_g.reshape(1, C), w, bfull)


def _sgu_bwd(proj3, dmix, norm_g, w, b, name):
    _, L, C = proj3.shape
    nh = w.shape[0]
    dh = C // nh
    tr = _pick(L, prefs=(512, 256, 128))
    nsteps = L // tr
    bfull = jnp.broadcast_to(b[:, :, None], (nh, CHUNK, dh))

    def body(su_ref, sv_ref, d_ref, g_ref, w_ref, b_ref, o_ref, dw_ref, db_ref, dg_ref, dv_ref, acc_g):
        i = pl.program_id(0)

        @pl.when(i == 0)
        def _():
            dw_ref[...] = jnp.zeros_like(dw_ref)
            db_ref[...] = jnp.zeros_like(db_ref)
            acc_g[...] = jnp.zeros_like(acc_g)

        svp = sv_ref[...]
        sv = _gelu(svp)
        r = lax.rsqrt(jnp.mean(sv * sv, axis=-1, keepdims=True) + EPS)
        vh = sv * r
        gv = g_ref[...]
        v = (vh * gv).astype(BF16)
        tri_r = lax.broadcasted_iota(jnp.int32, (CHUNK, CHUNK), 0)
        tri_c = lax.broadcasted_iota(jnp.int32, (CHUNK, CHUNK), 1)
        for h in range(nh):
            wm = _tril_w(w_ref, h).astype(BF16)
            cols = slice(h * dh, (h + 1) * dh)
            dwh = jnp.zeros((CHUNK, CHUNK), F32)
            dbh = jnp.zeros((CHUNK, dh), F32)
            for n in range(tr // CHUNK):
                rows = slice(n * CHUNK, (n + 1) * CHUNK)
                vb = v[rows, cols]
                mixed = jnp.dot(wm, vb, preferred_element_type=F32) + b_ref[h]
                sup = su_ref[rows, cols]
                dyd = d_ref[rows, cols]
                dmx = dyd * _gelu(sup)
                o_ref[0, rows, cols] = (dyd * mixed * _gelu_grad(sup)).astype(BF16)
                dmb = dmx.astype(BF16)
                dwh = dwh + lax.dot_general(dmb, vb, (((1,), (1,)), ((), ())), preferred_element_type=F32)
                dbh = dbh + dmx
                dv_ref[rows, cols] = lax.dot_general(wm, dmb, (((0,), (0,)), ((), ())), preferred_element_type=F32)
            dw_ref[h] += jnp.where(tri_r >= tri_c, dwh, 0.0)
            db_ref[h] += dbh
        dv = dv_ref[...]
        acc_g[...] += jnp.sum((dv * vh).reshape(tr // SUBLANES, SUBLANES, C), axis=0)
        dvg = dv * gv
        dsv = r * (dvg - vh * jnp.mean(dvg * vh, axis=-1, keepdims=True))
        o_ref[1] = (dsv * _gelu_grad(svp)).astype(BF16)

        @pl.when(i == nsteps - 1)
        def _():
            dg_ref[...] = jnp.sum(acc_g[...], axis=0, keepdims=True)

    full = lambda shp: pl.BlockSpec(shp, lambda i: (0,) * len(shp))
    return pl.pallas_call(
        body, name=name,
        out_shape=(jax.ShapeDtypeStruct((2, L, C), BF16), jax.ShapeDtypeStruct((nh, CHUNK, CHUNK), F32),
                   jax.ShapeDtypeStruct((nh, CHUNK, dh), F32), jax.ShapeDtypeStruct((1, C), F32)),
        grid=(nsteps,),
        in_specs=[pl.BlockSpec((None, tr, C), lambda i: (1, i, 0)), pl.BlockSpec((None, tr, C), lambda i: (2, i, 0)),
                  pl.BlockSpec((tr, C), lambda i: (i, 1)), full((1, C)), full((nh, CHUNK, CHUNK)),
                  full((nh, CHUNK, dh))],
        out_specs=(pl.BlockSpec((2, tr, C), lambda i: (0, i, 0)), full((nh, CHUNK, CHUNK)), full((nh, CHUNK, dh)),
                   full((1, C))),
        scratch_shapes=[pltpu.VMEM((tr, C), F32), pltpu.VMEM((SUBLANES, C), F32)],
        compiler_params=_cparams(("arbitrary",)),
    )(proj3, proj3, dmix, norm_g.reshape(1, C), w, bfull)


def _ffn_act_fwd(up3, conv_w, conv_b, name):
    _, L, Fh = up3.shape
    cb = LANES
    w2 = conv_w.reshape(3, 2, Fh).transpose(1, 0, 2)
    b2 = conv_b.reshape(2, 1, Fh)

    def body(u_ref, w_ref, b_ref, o_ref):
        g = _conv3(u_ref[0].astype(F32), w_ref[0]) + b_ref[0]
        v = _conv3(u_ref[1].astype(F32), w_ref[1]) + b_ref[1]
        o_ref[...] = (g * _sigmoid(g) * v).astype(BF16)

    return pl.pallas_call(
        body, name=name, out_shape=jax.ShapeDtypeStruct((L, Fh), BF16), grid=(Fh // cb,),
        in_specs=[pl.BlockSpec((2, L, cb), lambda j: (0, 0, j)), pl.BlockSpec((2, 3, cb), lambda j: (0, 0, j)),
                  pl.BlockSpec((2, 1, cb), lambda j: (0, 0, j))],
        out_specs=pl.BlockSpec((L, cb), lambda j: (0, j)), compiler_params=_cparams(("parallel",)),
    )(up3, w2, b2)


def _ffn_act_bwd(up3, da, conv_w, conv_b, name):
    _, L, Fh = up3.shape
    cb = LANES
    w2 = conv_w.reshape(3, 2, Fh).transpose(1, 0, 2)
    b2 = conv_b.reshape(2, 1, Fh)

    def body(u_ref, d_ref, w_ref, b_ref, o_ref, dw_ref, db_ref):
        ug, uv = u_ref[0].astype(F32), u_ref[1].astype(F32)
        wg, wv = w_ref[0], w_ref[1]
        g = _conv3(ug, wg) + b_ref[0]
        v = _conv3(uv, wv) + b_ref[1]
        sg = _sigmoid(g)
        dav = d_ref[...].astype(F32)
        dg = dav * v * (sg * (1.0 + g * (1.0 - sg)))
        dv = dav * (g * sg)
        o_ref[0] = _conv3_t(dg, wg).astype(BF16)
        o_ref[1] = _conv3_t(dv, wv).astype(BF16)
        for tap, (dwg, dwv) in enumerate(zip(_conv3_dw(dg, ug), _conv3_dw(dv, uv))):
            dw_ref[0, tap:tap + 1, :] = dwg
            dw_ref[1, tap:tap + 1, :] = dwv
        db_ref[0] = jnp.sum(dg, axis=0, keepdims=True)
        db_ref[1] = jnp.sum(dv, axis=0, keepdims=True)

    dup, dw2, db2 = pl.pallas_call(
        body, name=name,
        out_shape=(jax.ShapeDtypeStruct((2, L, Fh), BF16), jax.ShapeDtypeStruct((2, 3, Fh), F32),
                   jax.ShapeDtypeStruct((2, 1, Fh), F32)),
        grid=(Fh // cb,),
        in_specs=[pl.BlockSpec((2, L, cb), lambda j: (0, 0, j)), pl.BlockSpec((L, cb), lambda j: (0, j)),
                  pl.BlockSpec((2, 3, cb), lambda j: (0, 0, j)), pl.BlockSpec((2, 1, cb), lambda j: (0, 0, j))],
        out_specs=(pl.BlockSpec((2, L, cb), lambda j: (0, 0, j)), pl.BlockSpec((2, 3, cb), lambda j: (0, 0, j)),
                   pl.BlockSpec((2, 1, cb), lambda j: (0, 0, j))),
        compiler_params=_cparams(("parallel",)),
    )(up3, da, w2, b2)
    return dup, dw2.transpose(1, 0, 2).reshape(3, 2 * Fh), db2.reshape(2 * Fh)


def _local_step(x, tgt, w, wbig):
    L, D = x.shape
    depth = w['norm_mix_g'].shape[0]
    saved = []
    for i in range(depth):
        j = i // 2
        s = {'x': x}
        h = _rms_fwd(x, w['norm_mix_g'][i], "mix_norm_fwd")
        s['h'] = h
        if i % 2 == 0:
            proj4 = _mm(h, wbig['even_w_in'], 'nn', F32, "even_in_fwd", bk=('lead', j), ok=('seg', 4))
            s['proj'] = proj4
            ya = _sconv_fwd(proj4, w['even_conv_w'][j], "sconv_fwd")
            prm = (w['ssm_log_step'][j], w['ssm_a_re'][j], w['ssm_a_im'][j], w['ssm_b_re'][j], w['ssm_b_im'][j],
                   w['ssm_c_re'][j], w['ssm_c_im'][j])
            (lr, li, bmat, cmat), prep_vjp = jax.vjp(_s5_prep, *prm)
            u = _to_scan_order(proj4[3])
            yraw, s_re, s_im = _s5_fwd(u, lr, li, bmat, cmat, w['ssm_d'][j], "s5_fwd")
            yb = _glu_fwd(yraw, wbig['ssm_glu_w'], j, w['ssm_glu_b'][j], "glu_fwd")
            s.update(u=u, yraw=yraw, s_re=s_re, s_im=s_im, s5=(lr, li, bmat, cmat), prep_vjp=prep_vjp)
            mixin = jnp.concatenate([ya, _from_scan_order(yb)], axis=1)
            x = _mm(mixin, wbig['even_w_out'], 'nn', F32, "even_out_fwd", bk=('lead', j), res=x)
        else:
            proj3 = _mm(h, wbig['odd_w_in'], 'nn', F32, "odd_in_fwd", bk=('lead', j), ok=('seg', 3))
            s['proj'] = proj3
            yc = _pool_fwd(proj3, w['pool_w'][j], w['pool_scale'][j], "pool_fwd")
            yd = _sgu_fwd(proj3, w['sgu_norm_g'][j], w['sgu_w'][j], w['sgu_b'][j], "sgu_fwd")
            mixin = jnp.concatenate([yc, yd], axis=1)
            x = _mm(mixin, wbig['odd_w_out'], 'nn', F32, "odd_out_fwd", bk=('lead', j), res=x)
        s['mixin'] = mixin
        s['x1'] = x
        h2 = _rms_fwd(x, w['norm_ffn_g'][i], "ffn_norm_fwd")
        up3 = _mm(h2, wbig['ffn_w_up'], 'nn', BF16, "ffn_up_fwd", bk=('lead', i), ok=('seg', 2))
        a = _ffn_act_fwd(up3, w['ffn_conv_w'][i], w['ffn_conv_b'][i], "ffn_act_fwd")
        x = _mm(a, wbig['ffn_w_down'], 'nn', F32, "ffn_down_fwd", bk=('lead', i), res=x)
        s.update(h2=h2, up3=up3, a=a)
        saved.append(s)

    loss8, dx, dg_final = _loss_head(x, w['norm_final_g'], tgt)
    gs = {n: [None] * w[n].shape[0] for n in SMALL if n != 'norm_final_g'}
    gs['norm_final_g'] = dg_final.reshape(D)
    gb = {n: [None] * wbig[n].shape[0] for n in BIG}

    for i in reversed(range(depth)):
        j = i // 2
        s = saved[i]
        da = _mm(dx, wbig['ffn_w_down'], 'nt', BF16, "ffn_down_dgrad", bk=('lead', i))
        gb['ffn_w_down'][i] = _mm(s['a'].T, dx, 'nn', BF16, "ffn_down_wgrad")
        dup3, dcw, dcb = _ffn_act_bwd(s['up3'], da, w['ffn_conv_w'][i], w['ffn_conv_b'][i], "ffn_act_bwd")
        gs['ffn_conv_w'][i], gs['ffn_conv_b'][i] = dcw, dcb
        gb['ffn_w_up'][i] = _mm(s['h2'].T, dup3, 'nn', BF16, "ffn_up_wgrad", bk=('seg', 2))
        dh2 = _mm(dup3, wbig['ffn_w_up'], 'nt', F32, "ffn_up_dgrad", ak=('seg', 2), bk=('lead', i))
        dx, dg = _rms_bwd(s['x1'], w['norm_ffn_g'][i], dh2, dx, "ffn_norm_bwd")
        gs['norm_ffn_g'][i] = dg.reshape(D)
        if i % 2 == 0:
            dmix = _mm(dx, wbig['even_w_out'], 'nt', F32, "even_out_dgrad", bk=('lead', j))
            gb['even_w_out'][j] = _mm(s['mixin'].T, dx, 'nn', BF16, "even_out_wgrad")
            dpc, dcw = _sconv_bwd(s['proj'], dmix, w['even_conv_w'][j], "sconv_bwd")
            gs['even_conv_w'][j] = dcw
            dyb = _to_scan_order(dmix[:, D // 2:])
            dyraw, dglu_w, dglu_b = _glu_bwd(s['yraw'], dyb, wbig['ssm_glu_w'], j, w['ssm_glu_b'][j], "glu_bwd")
            gb['ssm_glu_w'][j] = dglu_w.astype(BF16)
            gs['ssm_glu_b'][j] = dglu_b.reshape(-1)
            lr, li, bmat, cmat = s['s5']
            du, dbm, dcm, dlam, dd = _s5_bwd(dyraw, s['u'], s['s_re'], s['s_im'], lr, li, bmat, cmat,
                                            w['ssm_d'][j], "s5_bwd")
            gs['ssm_d'][j] = dd.reshape(-1)
            dcm = jnp.swapaxes(dcm, 1, 2)
            dprm = s['prep_vjp']((dlam[:, 0:1, :], dlam[:, 1:2, :], dbm, dcm))
            for n, gval in zip(('ssm_log_step', 'ssm_a_re', 'ssm_a_im', 'ssm_b_re', 'ssm_b_im', 'ssm_c_re',
                                'ssm_c_im'), dprm):
                gs[n][j] = gval
            dproj = jnp.concatenate([dpc, _from_scan_order(du).astype(BF16)[None]], axis=0)
            gb['even_w_in'][j] = _mm(s['h'].T, dproj, 'nn', BF16, "even_in_wgrad", bk=('seg', 4))
            dh = _mm(dproj, wbig['even_w_in'], 'nt', F32, "even_in_dgrad", ak=('seg', 4), bk=('lead', j))
        else:
            dmix = _mm(dx, wbig['odd_w_out'], 'nt', F32, "odd_out_dgrad", bk=('lead', j))
            gb['odd_w_out'][j] = _mm(s['mixin'].T, dx, 'nn', BF16, "odd_out_wgrad")
            dz, dpw, dps = _pool_bwd(s['proj'], dmix, w['pool_w'][j], w['pool_scale'][j], "pool_bwd")
            gs['pool_w'][j], gs['pool_scale'][j] = dpw, dps.reshape(-1)
            dsuv, dsw, dsb, dsg = _sgu_bwd(s['proj'], dmix, w['sgu_norm_g'][j], w['sgu_w'][j], w['sgu_b'][j],
                                           "sgu_bwd")
            gs['sgu_w'][j], gs['sgu_b'][j], gs['sgu_norm_g'][j] = dsw, jnp.sum(dsb, axis=-1), dsg.reshape(-1)
            dproj = jnp.concatenate([dz[None], dsuv], axis=0)
            gb['odd_w_in'][j] = _mm(s['h'].T, dproj, 'nn', BF16, "odd_in_wgrad", bk=('seg', 3))
            dh = _mm(dproj, wbig['odd_w_in'], 'nt', F32, "odd_in_dgrad", ak=('seg', 3), bk=('lead', j))
        dx, dg = _rms_bwd(s['x'], w['norm_mix_g'][i], dh, dx, "mix_norm_bwd")
        gs['norm_mix_g'][i] = dg.reshape(D)

    gsmall = {n: (v if n == 'norm_final_g' else jnp.stack(v)) for n, v in gs.items()}
    return loss8[0, 0], dx, gsmall, gb


_HBM = pl.BlockSpec(memory_space=pltpu.HBM)
_CHIP_FLIPS = ((0, 0), (1, 0), (0, 1), (1, 1))


def _coords():
    return lax.axis_index("x"), lax.axis_index("y"), lax.axis_index("c")


def _flip(v, f):
    return 1 - v if f else v


def _shard_of(ref, axis, s, width):
    start = pl.multiple_of(s * width, LANES if axis == ref.ndim - 1 else 16) if width % 16 == 0 else s * width
    idx = [slice(None)] * ref.ndim
    idx[axis] = pl.ds(start, width)
    return ref.at[tuple(idx)]


def _gather_over_chips(shards, axes):
    n = len(shards)

    def body(*refs):
        ins, outs = refs[:n], refs[n:2 * n]
        send_sems, recv_sems, loc_sems = refs[2 * n:]
        x, y, c = _coords()

        def place(t, px, py):
            return _shard_of(outs[t], axes[t], 2 * px + py, ins[t].shape[axes[t]])

        def remote(t, f, dst_chip):
            fx, fy = _CHIP_FLIPS[f]
            return pltpu.make_async_remote_copy(
                src_ref=ins[t], dst_ref=place(t, *dst_chip), send_sem=send_sems.at[3 * t + f - 1],
                recv_sem=recv_sems.at[3 * t + f - 1], device_id=(_flip(x, fx), _flip(y, fy), c), device_id_type=MESH)

        local = [pltpu.make_async_copy(ins[t], place(t, x, y), loc_sems.at[t]) for t in range(n)]
        sends = [remote(t, f, (x, y)) for t in range(n) for f in (1, 2, 3)]
        for cp in local + sends:
            cp.start()
        for t in range(n):
            for f in (1, 2, 3):
                fx, fy = _CHIP_FLIPS[f]
                remote(t, f, (_flip(x, fx), _flip(y, fy))).wait_recv()
        for cp in sends:
            cp.wait_send()
        for cp in local:
            cp.wait()

    out_shape = []
    for a, ax in zip(shards, axes):
        shp = list(a.shape)
        shp[ax] *= N_CHIPS
        out_shape.append(jax.ShapeDtypeStruct(tuple(shp), a.dtype))
    return pl.pallas_call(
        body, name="gather_weights", out_shape=tuple(out_shape), in_specs=[_HBM] * n, out_specs=tuple([_HBM] * n),
        scratch_shapes=[pltpu.SemaphoreType.DMA((3 * n,)), pltpu.SemaphoreType.DMA((3 * n,)),
                        pltpu.SemaphoreType.DMA((n,))],
    )(*shards)


def _scatter_grads(names, grads, axes):
    items = [(t, l) for t in range(len(names)) for l in range(len(grads[t]))]
    arrays = [grads[t][l] for t, l in items]
    ni, nt = len(items), len(names)
    halves = [len(g) // 2 for g in grads]
    base = np.concatenate([[0], np.cumsum(halves)]).astype(int)

    def quarter_shape(t):
        r, c = grads[t][0].shape
        return (r, c // N_CHIPS) if axes[t] == 1 else (r // N_CHIPS, c)

    def body(*refs):
        ins, outs = refs[:ni], refs[ni:ni + nt]
        send_sems, recv_sems, loc_sems = refs[ni + nt:]
        x, y, c = _coords()
        started = []
        for idx, (t, l) in enumerate(items):
            owner, q = l // halves[t], l % halves[t]
            other = c if owner == 0 else 1 - c
            width = quarter_shape(t)[axes[t]]
            for f, (fx, fy) in enumerate(_CHIP_FLIPS):
                px, py = _flip(x, fx), _flip(y, fy)
                src = _shard_of(ins[idx], axes[t], 2 * px + py, width)
                slot = f + 4 * other
                dst = outs[t].at[q, slot]
                rem = pltpu.make_async_remote_copy(
                    src_ref=src, dst_ref=dst, send_sem=send_sems.at[4 * idx + f],
                    recv_sem=recv_sems.at[(int(base[t]) + q) * N_DEV + slot], device_id=(px, py, owner),
                    device_id_type=MESH)
                if f == 0:
                    loc = pltpu.make_async_copy(src, dst, loc_sems.at[idx])
                    pl.when(other == 0)(loc.start)
                    pl.when(other == 1)(rem.start)
                    started.append((other, loc, rem))
                else:
                    rem.start()
                    started.append((None, None, rem))
        for t in range(nt):
            for q in range(halves[t]):
                for slot in range(1, N_DEV):
                    land = outs[t].at[q, slot]
                    pltpu.make_async_remote_copy(
                        src_ref=land, dst_ref=land, send_sem=send_sems.at[0],
                        recv_sem=recv_sems.at[(int(base[t]) + q) * N_DEV + slot], device_id=(x, y, c),
                        device_id_type=MESH).wait_recv()
        for other, loc, rem in started:
            if loc is None:
                rem.wait_send()
            else:
                pl.when(other == 0)(loc.wait)
                pl.when(other == 1)(rem.wait_send)

    out_shape = tuple(jax.ShapeDtypeStruct((halves[t], N_DEV) + quarter_shape(t), BF16) for t in range(nt))
    return pl.pallas_call(
        body, name="scatter_grads", out_shape=out_shape, in_specs=[_HBM] * ni, out_specs=tuple([_HBM] * nt),
        scratch_shapes=[pltpu.SemaphoreType.DMA((4 * ni,)), pltpu.SemaphoreType.DMA((int(base[-1]) * N_DEV,)),
                        pltpu.SemaphoreType.DMA((ni,))],
    )(*arrays)


def _sum_slots(recv, name):
    n, ns, r, c = recv.shape
    tr = _pick(r, prefs=(256, 128, 64, 32, 16))

    def body(i_ref, o_ref):
        acc = i_ref[0].astype(F32)
        for s in range(1, ns):
            acc = acc + i_ref[s].astype(F32)
        o_ref[...] = acc

    return pl.pallas_call(
        body, name=name, out_shape=jax.ShapeDtypeStruct((n, r, c), F32), grid=(n, r // tr),
        in_specs=[pl.BlockSpec((None, ns, tr, c), lambda h, i: (h, 0, i, 0))],
        out_specs=pl.BlockSpec((None, tr, c), lambda h, i: (h, i, 0)),
        compiler_params=_cparams(("parallel", "parallel")),
    )(recv)


def _sum_and_share(recv, name):
    n, ns, r, c = recv.shape
    tr = _pick(r, prefs=(256, 128, 64, 32, 16))
    nr = r // tr
    nsteps = n * nr

    def body(i_ref, o_ref, buf, loc_sems, send_sems, recv_sems):
        h, i = pl.program_id(0), pl.program_id(1)
        step = h * nr + i
        slot = step % 2
        x, y, core = _coords()

        def copies(sl):
            dst = o_ref.at[core * n + h, pl.ds(pl.multiple_of(i * tr, tr), tr), :]
            loc = pltpu.make_async_copy(buf.at[sl], dst, loc_sems.at[sl])
            rem = pltpu.make_async_remote_copy(
                src_ref=buf.at[sl], dst_ref=dst, send_sem=send_sems.at[sl], recv_sem=recv_sems.at[step],
                device_id=(x, y, 1 - core), device_id_type=MESH)
            return loc, rem

        def drain(sl):
            loc, rem = copies(sl)
            loc.wait()
            rem.wait_send()

        pl.when(step >= 2)(lambda: drain(slot))
        acc = i_ref[0].astype(F32)
        for s in range(1, ns):
            acc = acc + i_ref[s].astype(F32)
        buf[slot] = acc
        loc, rem = copies(slot)
        loc.start()
        rem.start()

        @pl.when(step == nsteps - 1)
        def _():
            drain(slot)
            if nsteps > 1:
                drain(1 - slot)
            for hh in range(n):
                for ii in range(nr):
                    land = o_ref.at[(1 - core) * n + hh, pl.ds(ii * tr, tr), :]
                    pltpu.make_async_remote_copy(
                        src_ref=buf.at[0], dst_ref=land, send_sem=send_sems.at[0], recv_sem=recv_sems.at[hh * nr + ii],
                        device_id=(x, y, 1 - core), device_id_type=MESH).wait_recv()

    return pl.pallas_call(
        body, name=name, out_shape=jax.ShapeDtypeStruct((2 * n, r, c), F32), grid=(n, nr),
        in_specs=[pl.BlockSpec((None, ns, tr, c), lambda h, i: (h, 0, i, 0))], out_specs=_HBM,
        scratch_shapes=[pltpu.VMEM((2, tr, c), F32), pltpu.SemaphoreType.DMA((2,)), pltpu.SemaphoreType.DMA((2,)),
                        pltpu.SemaphoreType.DMA((nsteps,))],
        compiler_params=_cparams(("arbitrary", "arbitrary")),
    )(recv)


def _gather_over_devices(pack):
    def body(i_ref, o_ref, send_sems, recv_sems, loc_sem):
        x, y, c = _coords()
        me = 4 * x + 2 * y + c
        masks = [(m >> 2 & 1, m >> 1 & 1, m & 1) for m in range(1, N_DEV)]
        local = pltpu.make_async_copy(i_ref, o_ref.at[me], loc_sem)
        sends = [pltpu.make_async_remote_copy(
            src_ref=i_ref, dst_ref=o_ref.at[me], send_sem=send_sems.at[k], recv_sem=recv_sems.at[me],
            device_id=(_flip(x, fx), _flip(y, fy), _flip(c, fc)), device_id_type=MESH)
            for k, (fx, fy, fc) in enumerate(masks)]
        for cp in [local] + sends:
            cp.start()
        for fx, fy, fc in masks:
            px, py, pc = _flip(x, fx), _flip(y, fy), _flip(c, fc)
            peer = 4 * px + 2 * py + pc
            pltpu.make_async_remote_copy(
                src_ref=i_ref, dst_ref=o_ref.at[peer], send_sem=send_sems.at[0], recv_sem=recv_sems.at[peer],
                device_id=(px, py, pc), device_id_type=MESH).wait_recv()
        for cp in sends:
            cp.wait_send()
        local.wait()

    return pl.pallas_call(
        body, name="gather_small_grads", out_shape=jax.ShapeDtypeStruct((N_DEV,) + pack.shape, pack.dtype),
        in_specs=[_HBM], out_specs=_HBM,
        scratch_shapes=[pltpu.SemaphoreType.DMA((N_DEV - 1,)), pltpu.SemaphoreType.DMA((N_DEV,)),
                        pltpu.SemaphoreType.DMA],
    )(pack)


def _adamw(w, g, m, v, name):
    R, C = w.shape
    tr = _pick(R, prefs=(256, 128, 64, 32, 16, 8))
    bc1 = 1.0 - ADAM_B1 ** ADAM_STEP
    bc2 = 1.0 - ADAM_B2 ** ADAM_STEP

    def body(w_ref, g_ref, m_ref, v_ref, d_ref, mo_ref, vo_ref):
        gv = g_ref[...]
        mn = ADAM_B1 * m_ref[...] + (1.0 - ADAM_B1) * gv
        vn = ADAM_B2 * v_ref[...] + (1.0 - ADAM_B2) * (gv * gv)
        d_ref[...] = -ADAM_LR * ((mn / bc1) / (jnp.sqrt(vn / bc2) + ADAM_EPS) + ADAM_WD * w_ref[...])
        mo_ref[...] = mn
        vo_ref[...] = vn

    blk = pl.BlockSpec((tr, C), lambda i: (i, 0))
    sds = jax.ShapeDtypeStruct((R, C), F32)
    return pl.pallas_call(
        body, name=name, out_shape=(sds, sds, sds), grid=(R // tr,), in_specs=[blk] * 4, out_specs=(blk,) * 3,
        compiler_params=_cparams(("parallel",)),
    )(w, g, m, v)


_PACK_QUANTUM = 256 * LANES


def _pack(arrs):
    flat = jnp.concatenate([a.reshape(-1).astype(F32) for a in arrs])
    flat = jnp.pad(flat, (0, (-flat.shape[0]) % _PACK_QUANTUM))
    return flat.reshape(-1, LANES)


def _unpack(p, shapes):
    flat = p.reshape(-1)
    out, off = [], 0
    for s in shapes:
        n = int(np.prod(s))
        out.append(flat[off:off + n].reshape(s))
        off += n
    return out


def kernel(*args):
    nw = len(WEIGHTS)
    x, tgt = args[0], args[1 + nw]
    w = dict(zip(WEIGHTS, args[1:1 + nw]))
    m = dict(zip(WEIGHTS, args[2 + nw:2 + 2 * nw]))
    v = dict(zip(WEIGHTS, args[2 + 2 * nw:2 + 3 * nw]))
    _, L, D = x.shape
    chip = 2 * lax.axis_index("x") + lax.axis_index("y")

    big = list(BIG)
    small_sh_shapes = [w[n].shape for n in SMALL_SHARDED]
    gathered = _gather_over_chips([w[n].astype(BF16) for n in big] + [_pack([w[n] for n in SMALL_SHARDED])[None]],
                                  [BIG[n] for n in big] + [0])
    wbig = dict(zip(big, gathered[:-1]))
    per_chip = [_unpack(gathered[-1][s], small_sh_shapes) for s in range(N_CHIPS)]
    wl = dict(w)
    for k, n in enumerate(SMALL_SHARDED):
        wl[n] = jnp.concatenate([per_chip[s][k] for s in range(N_CHIPS)], axis=-1)

    loss, dx, gsmall, gbig = _local_step(x.reshape(L, D), tgt.reshape(L, D), wl, wbig)
    loss = lax.psum(loss, ("x", "y", "c"))

    landed = _scatter_grads(big, [gbig[n] for n in big], [BIG[n] - 1 for n in big])
    gshard = {n: _sum_and_share(r, "sum_share_" + n) for n, r in zip(big, landed)}

    small_shapes = [gsmall[n].shape for n in SMALL]
    gpack = _sum_slots(_gather_over_devices(_pack([gsmall[n] for n in SMALL]))[None], "sum_small_grads")[0]
    gs = dict(zip(SMALL, _unpack(gpack, small_shapes)))
    for n in SMALL_SHARDED:
        width = w[n].shape[-1]
        gs[n] = lax.dynamic_slice_in_dim(gs[n], chip * width, width, axis=gs[n].ndim - 1)

    grads, delta, new_m, new_v = {}, {}, {}, {}
    for n in big:
        shp = w[n].shape
        flat = lambda a: a.reshape(shp[0] * shp[1], shp[2])
        g = gshard[n]
        grads[n] = g
        d_, m_, v_ = _adamw(flat(w[n]), flat(g), flat(m[n]), flat(v[n]), "adamw_" + n)
        delta[n], new_m[n], new_v[n] = d_.reshape(shp), m_.reshape(shp), v_.reshape(shp)
    loc_shapes = [w[n].shape for n in SMALL]
    d_, m_, v_ = _adamw(_pack([w[n] for n in SMALL]), _pack([gs[n] for n in SMALL]), _pack([m[n] for n in SMALL]),
                        _pack([v[n] for n in SMALL]), "adamw_small")
    for n, dn, mn, vn in zip(SMALL, _unpack(d_, loc_shapes), _unpack(m_, loc_shapes), _unpack(v_, loc_shapes)):
        grads[n], delta[n], new_m[n], new_v[n] = gs[n], dn, mn, vn

    return (loss, dx.reshape(1, L, D), *[grads[n] for n in WEIGHTS], *[delta[n] for n in WEIGHTS],
            *[new_m[n] for n in WEIGHTS], *[new_v[n] for n in WEIGHTS])
```

```python
import functools
import math

import numpy as np
import jax
import jax.numpy as jnp
from jax import lax
from jax.experimental import pallas as pl
from jax.experimental.pallas import tpu as pltpu

F32 = jnp.float32
BF16 = jnp.bfloat16
MESH = pl.DeviceIdType.MESH

EPS = 1e-6
CHUNK = 128
POOL_WINDOWS = (2, 4, 8, 16)
LANES = 128
SUBLANES = 8
SCAN_CHUNKS = SUBLANES
S5_GROUPS_PER_STEP = 4
MM_TM_CAP, MM_TN_CAP, MM_TK_CAP = 1024, 1408, 1408
MM_TK_WHOLE = 2048
VMEM_LIMIT = 48 * 1024 * 1024
VMEM_LIMIT_S5 = 56 * 1024 * 1024

ADAM_LR, ADAM_B1, ADAM_B2, ADAM_EPS, ADAM_WD, ADAM_STEP = 0.001, 0.9, 0.999, 1e-08, 0.01, 10

WEIGHTS = ['norm_mix_g', 'even_w_in', 'even_conv_w', 'ssm_log_step', 'ssm_a_re', 'ssm_a_im', 'ssm_b_re',
           'ssm_b_im', 'ssm_c_re', 'ssm_c_im', 'ssm_d', 'ssm_glu_w', 'ssm_glu_b', 'even_w_out', 'odd_w_in',
           'pool_w', 'pool_scale', 'sgu_norm_g', 'sgu_w', 'sgu_b', 'odd_w_out', 'norm_ffn_g', 'ffn_w_up',
           'ffn_conv_w', 'ffn_conv_b', 'ffn_w_down', 'norm_final_g']
BIG = {'even_w_in': 2, 'ssm_glu_w': 1, 'even_w_out': 1, 'odd_w_in': 2, 'odd_w_out': 1, 'ffn_w_up': 2,
       'ffn_w_down': 1}
SMALL_SHARDED = ('even_conv_w', 'pool_scale', 'sgu_norm_g', 'ffn_conv_w')
SMALL = [n for n in WEIGHTS if n not in BIG]
N_CHIPS = 4
N_DEV = 8


def _cparams(sem=None, vmem=VMEM_LIMIT):
    kw = dict(vmem_limit_bytes=vmem)
    if sem is not None:
        kw['dimension_semantics'] = sem
    return pltpu.CompilerParams(**kw)


def _pick(n, segs=(), prefs=(1024, 512, 256, 128)):
    for t in prefs:
        if n % t == 0 and all(s % t == 0 for s in segs if s):
            return t
    return n


def _largest_tile(n, segs, cap):
    best = None
    for t in range(LANES, min(n, cap) + 1, LANES):
        if n % t == 0 and all(s % t == 0 for s in segs if s):
            best = t
    return best if best is not None else n


def _ldims(arr, kind):
    if kind is None:
        return arr.shape
    if kind[0] == 'lead':
        return arr.shape[1:]
    return (arr.shape[1], arr.shape[0] * arr.shape[2])


def _segw(arr, kind):
    return arr.shape[2] if (kind is not None and kind[0] == 'seg') else None


def _opspec(arr, kind, br, bc, rfn, cfn):
    if kind is None:
        return pl.BlockSpec((br, bc), lambda i, j, k: (rfn(i, j, k), cfn(i, j, k)))
    if kind[0] == 'lead':
        lead = kind[1]
        return pl.BlockSpec((None, br, bc), lambda i, j, k: (lead, rfn(i, j, k), cfn(i, j, k)))
    per = arr.shape[2] // bc
    return pl.BlockSpec((None, br, bc), lambda i, j, k: (cfn(i, j, k) // per, rfn(i, j, k), cfn(i, j, k) % per))


def _mm(a, b, mode, out_dtype, name, ak=None, bk=None, ok=None, res=None):
    ar, ac = _ldims(a, ak)
    br_, bc_ = _ldims(b, bk)
    if mode == 'nn':
        M, K, N = ar, ac, bc_
        assert br_ == K
    else:
        M, K, N = ar, ac, br_
        assert bc_ == K
    sa, sb = _segw(a, ak), _segw(b, bk)
    so = (N // ok[1]) if ok is not None else None
    tm = _largest_tile(M, [], MM_TM_CAP)
    tn = _largest_tile(N, [sb if mode == 'nn' else None, so], MM_TN_CAP)
    ksegs = [sa, sb if mode == 'nt' else None]
    tk = K if (K <= MM_TK_WHOLE and not any(ksegs)) else _largest_tile(K, ksegs, MM_TK_CAP)
    nk = K // tk
    I = lambda i, j, k: i
    J = lambda i, j, k: j
    Kk = lambda i, j, k: k
    a_spec = _opspec(a, ak, tm, tk, I, Kk)
    if mode == 'nn':
        b_spec = _opspec(b, bk, tk, tn, Kk, J)
        dims = (((1,), (0,)), ((), ()))
    else:
        b_spec = _opspec(b, bk, tn, tk, J, Kk)
        dims = (((1,), (1,)), ((), ()))
    if ok is None:
        out_shape = jax.ShapeDtypeStruct((M, N), out_dtype)
        o_spec = pl.BlockSpec((tm, tn), lambda i, j, k: (i, j))
    else:
        out_shape = jax.ShapeDtypeStruct((ok[1], M, N // ok[1]), out_dtype)
        per = (N // ok[1]) // tn
        o_spec = pl.BlockSpec((None, tm, tn), lambda i, j, k: (j // per, i, j % per))
    has_res = res is not None

    def body(*refs):
        a_ref, b_ref = refs[0], refs[1]
        r_ref = refs[2] if has_res else None
        o_ref = refs[3] if has_res else refs[2]
        prod = lax.dot_general(a_ref[...].astype(BF16), b_ref[...].astype(BF16), dims, preferred_element_type=F32)
        if nk == 1:
            o_ref[...] = (prod + r_ref[...] if has_res else prod).astype(out_dtype)
            return
        acc = refs[-1]
        k = pl.program_id(2)

        @pl.when(k == 0)
        def _():
            acc[...] = prod

        @pl.when(k > 0)
        def _():
            acc[...] += prod

        @pl.when(k == nk - 1)
        def _():
            o = acc[...]
            if has_res:
                o = o + r_ref[...]
            o_ref[...] = o.astype(out_dtype)

    in_specs = [a_spec, b_spec]
    args = [a, b]
    if has_res:
        in_specs.append(pl.BlockSpec((tm, tn), lambda i, j, k: (i, j)))
        args.append(res)
    return pl.pallas_call(
        body, name=name, out_shape=out_shape, grid=(M // tm, N // tn, nk), in_specs=in_specs, out_specs=o_spec,
        scratch_shapes=[pltpu.VMEM((tm, tn), F32)] if nk > 1 else [],
        compiler_params=_cparams(("parallel", "parallel", "arbitrary")),
    )(*args)


_G0 = math.sqrt(2.0 / math.pi)
_G1 = 0.044715


def _gelu(x):
    return 0.5 * x * (1.0 + jnp.tanh(_G0 * (x + _G1 * x * x * x)))


def _gelu_grad(x):
    x2 = x * x
    t = jnp.tanh(_G0 * (x + _G1 * x * x2))
    return 0.5 * (1.0 + t) + 0.5 * x * (1.0 - t * t) * (_G0 * (1.0 + 3.0 * _G1 * x2))


def _sigmoid(x):
    return 1.0 / (1.0 + jnp.exp(-x))


def _down(v, k):
    row = lax.broadcasted_iota(jnp.int32, v.shape, 0)
    return jnp.where(row >= k, pltpu.roll(v, k, axis=0), 0.0)


def _up(v, k):
    n = v.shape[0]
    row = lax.broadcasted_iota(jnp.int32, v.shape, 0)
    return jnp.where(row < n - k, pltpu.roll(v, n - k, axis=0), 0.0)


def _conv3(v, w):
    return w[0:1, :] * _down(v, 2) + w[1:2, :] * _down(v, 1) + w[2:3, :] * v


def _conv3_t(dv, w):
    return w[2:3, :] * dv + w[1:2, :] * _up(dv, 1) + w[0:1, :] * _up(dv, 2)


def _conv3_dw(dv, v):
    return (jnp.sum(dv * _down(v, 2), axis=0, keepdims=True),
            jnp.sum(dv * _down(v, 1), axis=0, keepdims=True),
            jnp.sum(dv * v, axis=0, keepdims=True))


def _cmul(ar, ai, br, bi):
    return ar * br - ai * bi, ar * bi + ai * br


def _cpow(lr, li, n):
    rr = ri = None
    br, bi = lr, li
    while n:
        if n & 1:
            rr, ri = (br, bi) if rr is None else _cmul(rr, ri, br, bi)
        n >>= 1
        if n:
            br, bi = _cmul(br, bi, br, bi)
    return rr, ri


def _rms_fwd(x, g, name):
    L, D = x.shape
    tr = _pick(L, prefs=(512, 256, 128))

    def body(x_ref, g_ref, h_ref):
        xv = x_ref[...]
        r = lax.rsqrt(jnp.mean(xv * xv, axis=-1, keepdims=True) + EPS)
        h_ref[...] = (xv * r * g_ref[...]).astype(BF16)

    return pl.pallas_call(
        body, name=name, out_shape=jax.ShapeDtypeStruct((L, D), BF16), grid=(L // tr,),
        in_specs=[pl.BlockSpec((tr, D), lambda i: (i, 0)), pl.BlockSpec((1, D), lambda i: (0, 0))],
        out_specs=pl.BlockSpec((tr, D), lambda i: (i, 0)), compiler_params=_cparams(("parallel",)),
    )(x, g.reshape(1, D))


def _rms_bwd(x, g, dh, dres, name):
    L, D = x.shape
    tr = _pick(L, prefs=(512, 256, 128))
    nsteps = L // tr

    def body(x_ref, g_ref, dh_ref, dres_ref, dx_ref, dg_ref, acc):
        i = pl.program_id(0)

        @pl.when(i == 0)
        def _():
            acc[...] = jnp.zeros_like(acc)

        xv = x_ref[...]
        r = lax.rsqrt(jnp.mean(xv * xv, axis=-1, keepdims=True) + EPS)
        xh = xv * r
        dhv = dh_ref[...].astype(F32)
        acc[...] += jnp.sum((dhv * xh).reshape(tr // SUBLANES, SUBLANES, D), axis=0)
        dxh = dhv * g_ref[...]
        dx_ref[...] = dres_ref[...] + r * (dxh - xh * jnp.mean(dxh * xh, axis=-1, keepdims=True))

        @pl.when(i == nsteps - 1)
        def _():
            dg_ref[...] = jnp.sum(acc[...], axis=0, keepdims=True)

    row = pl.BlockSpec((tr, D), lambda i: (i, 0))
    vec = pl.BlockSpec((1, D), lambda i: (0, 0))
    return pl.pallas_call(
        body, name=name, out_shape=(jax.ShapeDtypeStruct((L, D), F32), jax.ShapeDtypeStruct((1, D), F32)),
        grid=(nsteps,), in_specs=[row, vec, row, row], out_specs=(row, vec),
        scratch_shapes=[pltpu.VMEM((SUBLANES, D), F32)], compiler_params=_cparams(("arbitrary",)),
    )(x, g.reshape(1, D), dh, dres)


def _loss_head(x, g, tgt):
    L, D = x.shape
    tr = _pick(L, prefs=(512, 256, 128))
    nsteps = L // tr

    def body(x_ref, g_ref, t_ref, loss_ref, dx_ref, dg_ref, acc_g, acc_l):
        i = pl.program_id(0)

        @pl.when(i == 0)
        def _():
            acc_g[...] = jnp.zeros_like(acc_g)
            acc_l[...] = jnp.zeros_like(acc_l)

        xv = x_ref[...]
        gv = g_ref[...]
        r = lax.rsqrt(jnp.mean(xv * xv, axis=-1, keepdims=True) + EPS)
        xh = xv * r
        e = xh * gv - t_ref[...]
        acc_l[...] += jnp.sum((e * e).reshape(tr // SUBLANES, SUBLANES, D), axis=0)
        dy = e * (1.0 / D)
        acc_g[...] += jnp.sum((dy * xh).reshape(tr // SUBLANES, SUBLANES, D), axis=0)
        dxh = dy * gv
        dx_ref[...] = r * (dxh - xh * jnp.mean(dxh * xh, axis=-1, keepdims=True))

        @pl.when(i == nsteps - 1)
        def _():
            dg_ref[...] = jnp.sum(acc_g[...], axis=0, keepdims=True)
            tot = jnp.sum(jnp.sum(acc_l[...], axis=0, keepdims=True), axis=1, keepdims=True) * (0.5 / D)
            loss_ref[...] = jnp.broadcast_to(tot, (SUBLANES, LANES))

    row = pl.BlockSpec((tr, D), lambda i: (i, 0))
    vec = pl.BlockSpec((1, D), lambda i: (0, 0))
    return pl.pallas_call(
        body, name="loss_head",
        out_shape=(jax.ShapeDtypeStruct((SUBLANES, LANES), F32), jax.ShapeDtypeStruct((L, D), F32),
                   jax.ShapeDtypeStruct((1, D), F32)),
        grid=(nsteps,), in_specs=[row, vec, row],
        out_specs=(pl.BlockSpec((SUBLANES, LANES), lambda i: (0, 0)), row, vec),
        scratch_shapes=[pltpu.VMEM((SUBLANES, D), F32), pltpu.VMEM((SUBLANES, D), F32)],
        compiler_params=_cparams(("arbitrary",)),
    )(x, g.reshape(1, D), tgt)


def _sconv_fwd(proj4, conv_w, name):
    _, L, C = proj4.shape
    cb = LANES

    def body(p_ref, w_ref, o_ref):
        xa, ba, ca = p_ref[0], p_ref[1], p_ref[2]
        o_ref[...] = (ba * _conv3(ca * xa, w_ref[...])).astype(BF16)

    return pl.pallas_call(
        body, name=name, out_shape=jax.ShapeDtypeStruct((L, C), BF16), grid=(C // cb,),
        in_specs=[pl.BlockSpec((3, L, cb), lambda j: (0, 0, j)), pl.BlockSpec((3, cb), lambda j: (0, j))],
        out_specs=pl.BlockSpec((L, cb), lambda j: (0, j)), compiler_params=_cparams(("parallel",)),
    )(proj4, conv_w)


def _sconv_bwd(proj4, dmix, conv_w, name):
    _, L, C = proj4.shape
    cb = LANES

    def body(p_ref, d_ref, w_ref, o_ref, dw_ref):
        xa, ba, ca = p_ref[0], p_ref[1], p_ref[2]
        w = w_ref[...]
        dya = d_ref[...]
        q = ca * xa
        cq = _conv3(q, w)
        dcq = dya * ba
        dq = _conv3_t(dcq, w)
        for tap, dwt in enumerate(_conv3_dw(dcq, q)):
            dw_ref[tap:tap + 1, :] = dwt
        o_ref[0] = (dq * ca).astype(BF16)
        o_ref[1] = (dya * cq).astype(BF16)
        o_ref[2] = (dq * xa).astype(BF16)

    return pl.pallas_call(
        body, name=name,
        out_shape=(jax.ShapeDtypeStruct((3, L, C), BF16), jax.ShapeDtypeStruct((3, C), F32)), grid=(C // cb,),
        in_specs=[pl.BlockSpec((3, L, cb), lambda j: (0, 0, j)), pl.BlockSpec((L, cb), lambda j: (0, j)),
                  pl.BlockSpec((3, cb), lambda j: (0, j))],
        out_specs=(pl.BlockSpec((3, L, cb), lambda j: (0, 0, j)), pl.BlockSpec((3, cb), lambda j: (0, j))),
        compiler_params=_cparams(("parallel",)),
    )(proj4, dmix, conv_w)


def _to_scan_order(v):
    L, C = v.shape
    return v.reshape(SCAN_CHUNKS, L // SCAN_CHUNKS, C).transpose(1, 0, 2).reshape(L, C)


def _from_scan_order(v):
    L, C = v.shape
    return v.reshape(L // SCAN_CHUNKS, SCAN_CHUNKS, C).transpose(1, 0, 2).reshape(L, C)


def _s5_prep(log_step, a_re, a_im, b_re, b_im, c_re, c_im):
    G, P = a_re.shape
    H = b_re.shape[-1]
    gs = S5_GROUPS_PER_STEP
    ns = G // gs
    gu = LANES // H
    lam = lax.complex(a_re, a_im)
    step = jnp.exp(log_step)[:, None]
    lam_bar = jnp.exp(lam * step)
    b_bar = ((lam_bar - 1.0) / lam)[..., None] * lax.complex(b_re, b_im)
    lr = jnp.real(lam_bar).reshape(ns, 1, gs * P)
    li = jnp.imag(lam_bar).reshape(ns, 1, gs * P)
    k = np.arange(ns)[:, None, None]
    oh = jnp.asarray((np.arange(gu)[None, :, None] == gs * (k % (gu // gs)) + np.arange(gs)[None, None, :]),
                     F32)
    bre = jnp.einsum('kgl,klph->kghlp', oh, jnp.real(b_bar).reshape(ns, gs, P, H)).reshape(ns, gu * H, gs * P)
    bim = jnp.einsum('kgl,klph->kghlp', oh, jnp.imag(b_bar).reshape(ns, gs, P, H)).reshape(ns, gu * H, gs * P)
    cre = jnp.einsum('kgl,klhp->klpgh', oh, c_re.reshape(ns, gs, H, P)).reshape(ns, gs * P, gu * H)
    cim = jnp.einsum('kgl,klhp->klpgh', oh, c_im.reshape(ns, gs, H, P)).reshape(ns, gs * P, gu * H)
    return lr, li, jnp.concatenate([bre, bim], axis=2), jnp.concatenate([cre, -cim], axis=1)


def _carry_tile(fr, fi, pr, pi, reverse):
    row = lax.broadcasted_iota(jnp.int32, fr.shape, 0)
    cr = jnp.zeros_like(fr)
    ci = jnp.zeros_like(fi)
    sr = jnp.zeros_like(fr[0:1])
    si = jnp.zeros_like(sr)
    order = range(SCAN_CHUNKS - 1, 0, -1) if reverse else range(0, SCAN_CHUNKS - 1)
    for c in order:
        fcr = jnp.sum(jnp.where(row == c, fr, 0.0), axis=0, keepdims=True)
        fci = jnp.sum(jnp.where(row == c, fi, 0.0), axis=0, keepdims=True)
        mr, mi = _cmul(pr, pi, sr, si)
        sr, si = mr + fcr, mi + fci
        nxt = c - 1 if reverse else c + 1
        cr = jnp.where(row == nxt, sr, cr)
        ci = jnp.where(row == nxt, si, ci)
    return cr, ci


def _s5_fwd(u, lr, li, bmat, cmat, d, name):
    L, Du = u.shape
    ns, _, W2 = bmat.shape
    W = W2 // 2
    T = L // SCAN_CHUNKS
    rb = _pick(L, prefs=(512, 256, 128))
    per = (ns * LANES) // Du

    def body(u_ref, lr_ref, li_ref, b_ref, c_ref, d_ref, y_ref, sr_ref, si_ref):
        k = pl.program_id(0)
        for r in range(L // rb):
            rows = pl.ds(r * rb, rb)
            bu = jnp.dot(u_ref[rows, :].astype(BF16), b_ref[...], preferred_element_type=F32)
            sr_ref[rows, :] = bu[:, :W]
            si_ref[rows, :] = bu[:, W:]
        lam_r = jnp.broadcast_to(lr_ref[...], (SUBLANES, W))
        lam_i = jnp.broadcast_to(li_ref[...], (SUBLANES, W))

        def local(t, carry):
            sr, si = carry
            rows = pl.ds(pl.multiple_of(t * SUBLANES, SUBLANES), SUBLANES)
            mr, mi = _cmul(lam_r, lam_i, sr, si)
            sr = mr + sr_ref[rows, :]
            si = mi + si_ref[rows, :]
            sr_ref[rows, :] = sr
            si_ref[rows, :] = si
            return sr, si

        z = jnp.zeros((SUBLANES, W), F32)
        fr, fi = lax.fori_loop(0, T, local, (z, z))
        pr, pi = _cpow(lam_r, lam_i, T)
        cr, ci = _carry_tile(fr, fi, pr[0:1], pi[0:1], reverse=False)

        def fix(t, carry):
            wr, wi = carry
            rows = pl.ds(pl.multiple_of(t * SUBLANES, SUBLANES), SUBLANES)
            ar, ai = _cmul(wr, wi, cr, ci)
            sr_ref[rows, :] += ar
            si_ref[rows, :] += ai
            return _cmul(wr, wi, lam_r, lam_i)

        lax.fori_loop(0, T, fix, (lam_r, lam_i))
        first = (k % per) == 0
        for r in range(L // rb):
            rows = pl.ds(r * rb, rb)
            s = jnp.concatenate([sr_ref[rows, :], si_ref[rows, :]], axis=1).astype(BF16)
            y = jnp.dot(s, c_ref[...], preferred_element_type=F32)

            @pl.when(first)
            def _():
                y_ref[rows, :] = y + d_ref[...] * u_ref[rows, :]

            @pl.when(jnp.logical_not(first))
            def _():
                y_ref[rows, :] += y

    ublk = pl.BlockSpec((L, LANES), lambda k: (0, k // per))
    sblk = pl.BlockSpec((L, W), lambda k: (0, k))
    lam = pl.BlockSpec((None, 1, W), lambda k: (k, 0, 0))
    return pl.pallas_call(
        body, name=name,
        out_shape=(jax.ShapeDtypeStruct((L, Du), F32), jax.ShapeDtypeStruct((L, ns * W), F32),
                   jax.ShapeDtypeStruct((L, ns * W), F32)),
        grid=(ns,),
        in_specs=[ublk, lam, lam, pl.BlockSpec((None, LANES, 2 * W), lambda k: (k, 0, 0)),
                  pl.BlockSpec((None, 2 * W, LANES), lambda k: (k, 0, 0)),
                  pl.BlockSpec((1, LANES), lambda k: (0, k // per))],
        out_specs=(ublk, sblk, sblk), compiler_params=_cparams(("arbitrary",), VMEM_LIMIT_S5),
    )(u, lr, li, bmat.astype(BF16), cmat.astype(BF16), d.reshape(1, Du))


def _s5_bwd(dy, u, s_re, s_im, lr, li, bmat, cmat, d, name):
    L, Du = u.shape
    ns, _, W2 = bmat.shape
    W = W2 // 2
    T = L // SCAN_CHUNKS
    rb = _pick(L, prefs=(512, 256, 128))
    per = (ns * LANES) // Du
    NT = (((1,), (1,)), ((), ()))
    TN = (((0,), (0,)), ((), ()))

    def body(dy_ref, u_ref, sr_ref, si_ref, lr_ref, li_ref, b_ref, c_ref, d_ref,
             du_ref, db_ref, dc_ref, dl_ref, dd_ref, gr_ref, gi_ref):
        k = pl.program_id(0)
        for r in range(L // rb):
            rows = pl.ds(r * rb, rb)
            g = lax.dot_general(dy_ref[rows, :].astype(BF16), c_ref[...], NT, preferred_element_type=F32)
            gr_ref[rows, :] = g[:, :W]
            gi_ref[rows, :] = g[:, W:]
        lam_r = jnp.broadcast_to(lr_ref[...], (SUBLANES, W))
        lam_i = -jnp.broadcast_to(li_ref[...], (SUBLANES, W))

        def local(i, carry):
            gr, gi = carry
            rows = pl.ds(pl.multiple_of((T - 1 - i) * SUBLANES, SUBLANES), SUBLANES)
            mr, mi = _cmul(lam_r, lam_i, gr, gi)
            gr = mr + gr_ref[rows, :]
            gi = mi + gi_ref[rows, :]
            gr_ref[rows, :] = gr
            gi_ref[rows, :] = gi
            return gr, gi

        z = jnp.zeros((SUBLANES, W), F32)
        fr, fi = lax.fori_loop(0, T, local, (z, z))
        pr, pi = _cpow(lam_r, lam_i, T)
        cr, ci = _carry_tile(fr, fi, pr[0:1], pi[0:1], reverse=True)

        def true_g(rows, wr, wi):
            ar, ai = _cmul(wr, wi, cr, ci)
            gr = gr_ref[rows, :] + ar
            gi = gi_ref[rows, :] + ai
            gr_ref[rows, :] = gr
            gi_ref[rows, :] = gi
            return gr, gi

        def fix(i, carry):
            wr, wi, ar_, ai_ = carry
            t = T - 1 - i
            rows = pl.ds(pl.multiple_of(t * SUBLANES, SUBLANES), SUBLANES)
            prev = pl.ds(pl.multiple_of((t - 1) * SUBLANES, SUBLANES), SUBLANES)
            gr, gi = true_g(rows, wr, wi)
            qr, qi = sr_ref[prev, :], si_ref[prev, :]
            ar_ = ar_ + gr * qr + gi * qi
            ai_ = ai_ + gi * qr - gr * qi
            wr, wi = _cmul(wr, wi, lam_r, lam_i)
            return wr, wi, ar_, ai_

        wr, wi, acc_r, acc_i = lax.fori_loop(0, T - 1, fix, (lam_r, lam_i, z, z))
        gr, gi = true_g(pl.ds(0, SUBLANES), wr, wi)
        last = pl.ds((T - 1) * SUBLANES, SUBLANES)
        row = lax.broadcasted_iota(jnp.int32, (SUBLANES, W), 0)
        qr = jnp.where(row >= 1, pltpu.roll(sr_ref[last, :], 1, axis=0), 0.0)
        qi = jnp.where(row >= 1, pltpu.roll(si_ref[last, :], 1, axis=0), 0.0)
        acc_r = acc_r + gr * qr + gi * qi
        acc_i = acc_i + gi * qr - gr * qi
        dl_ref[0:1, :] = jnp.sum(acc_r, axis=0, keepdims=True)
        dl_ref[1:2, :] = jnp.sum(acc_i, axis=0, keepdims=True)

        first = (k % per) == 0
        db = jnp.zeros((LANES, 2 * W), F32)
        dc = jnp.zeros((LANES, 2 * W), F32)
        dd = jnp.zeros((1, LANES), F32)
        for r in range(L // rb):
            rows = pl.ds(r * rb, rb)
            gb = jnp.concatenate([gr_ref[rows, :], gi_ref[rows, :]], axis=1).astype(BF16)
            sb = jnp.concatenate([sr_ref[rows, :], si_ref[rows, :]], axis=1).astype(BF16)
            dyv = dy_ref[rows, :]
            uv = u_ref[rows, :]
            du = lax.dot_general(gb, b_ref[...], NT, preferred_element_type=F32)
            db = db + lax.dot_general(uv.astype(BF16), gb, TN, preferred_element_type=F32)
            dc = dc + lax.dot_general(dyv.astype(BF16), sb, TN, preferred_element_type=F32)
            dd = dd + jnp.sum(dyv * uv, axis=0, keepdims=True)

            @pl.when(first)
            def _():
                du_ref[rows, :] = du + d_ref[...] * dyv

            @pl.when(jnp.logical_not(first))
            def _():
                du_ref[rows, :] += du

        db_ref[...] = db
        dc_ref[...] = dc

        @pl.when(first)
        def _():
            dd_ref[...] = dd

    ublk = pl.BlockSpec((L, LANES), lambda k: (0, k // per))
    sblk = pl.BlockSpec((L, W), lambda k: (0, k))
    lam = pl.BlockSpec((None, 1, W), lambda k: (k, 0, 0))
    vec = pl.BlockSpec((1, LANES), lambda k: (0, k // per))
    mat = pl.BlockSpec((None, LANES, 2 * W), lambda k: (k, 0, 0))
    return pl.pallas_call(
        body, name=name,
        out_shape=(jax.ShapeDtypeStruct((L, Du), F32), jax.ShapeDtypeStruct((ns, LANES, 2 * W), F32),
                   jax.ShapeDtypeStruct((ns, LANES, 2 * W), F32), jax.ShapeDtypeStruct((ns, 2, W), F32),
                   jax.ShapeDtypeStruct((1, Du), F32)),
        grid=(ns,),
        in_specs=[ublk, ublk, sblk, sblk, lam, lam, mat,
                  pl.BlockSpec((None, 2 * W, LANES), lambda k: (k, 0, 0)), vec],
        out_specs=(ublk, mat, mat, pl.BlockSpec((None, 2, W), lambda k: (k, 0, 0)), vec),
        scratch_shapes=[pltpu.VMEM((L, W), F32), pltpu.VMEM((L, W), F32)],
        compiler_params=_cparams(("arbitrary",), VMEM_LIMIT_S5),
    )(dy, u, s_re, s_im, lr, li, bmat.astype(BF16), cmat.astype(BF16), d.reshape(1, Du))


def _glu_fwd(yraw, wmat, bias, name):
    L, C = yraw.shape
    tr = _pick(L, prefs=(512, 256, 128))

    def body(y_ref, w_ref, b_ref, o_ref):
        yg = _gelu(y_ref[...])
        zz = jnp.dot(yg.astype(BF16), w_ref[...], preferred_element_type=F32) + b_ref[...]
        o_ref[...] = (yg * _sigmoid(zz)).astype(BF16)

    return pl.pallas_call(
        body, name=name, out_shape=jax.ShapeDtypeStruct((L, C), BF16), grid=(L // tr,),
        in_specs=[pl.BlockSpec((tr, C), lambda i: (i, 0)), pl.BlockSpec((C, C), lambda i: (0, 0)),
                  pl.BlockSpec((1, C), lambda i: (0, 0))],
        out_specs=pl.BlockSpec((tr, C), lambda i: (i, 0)), compiler_params=_cparams(("parallel",)),
    )(yraw, wmat, bias.reshape(1, C))


def _glu_bwd(yraw, dyb, wmat, bias, name):
    L, C = yraw.shape
    tr = _pick(L, prefs=(512, 256, 128))
    nsteps = L // tr

    def body(y_ref, d_ref, w_ref, b_ref, dy_ref, dw_ref, db_ref, acc_b):
        i = pl.program_id(0)

        @pl.when(i == 0)
        def _():
            dw_ref[...] = jnp.zeros_like(dw_ref)
            acc_b[...] = jnp.zeros_like(acc_b)

        yr = y_ref[...]
        yg = _gelu(yr)
        ygb = yg.astype(BF16)
        sg = _sigmoid(jnp.dot(ygb, w_ref[...], preferred_element_type=F32) + b_ref[...])
        dyb_ = d_ref[...]
        dz = dyb_ * yg * sg * (1.0 - sg)
        dzb = dz.astype(BF16)
        dyg = dyb_ * sg + lax.dot_general(dzb, w_ref[...], (((1,), (1,)), ((), ())), preferred_element_type=F32)
        dw_ref[...] += lax.dot_general(ygb, dzb, (((0,), (0,)), ((), ())), preferred_element_type=F32)
        acc_b[...] += jnp.sum(dz.reshape(tr // SUBLANES, SUBLANES, C), axis=0)
        dy_ref[...] = dyg * _gelu_grad(yr)

        @pl.when(i == nsteps - 1)
        def _():
            db_ref[...] = jnp.sum(acc_b[...], axis=0, keepdims=True)

    row = pl.BlockSpec((tr, C), lambda i: (i, 0))
    return pl.pallas_call(
        body, name=name,
        out_shape=(jax.ShapeDtypeStruct((L, C), F32), jax.ShapeDtypeStruct((C, C), F32),
                   jax.ShapeDtypeStruct((1, C), F32)),
        grid=(nsteps,),
        in_specs=[row, row, pl.BlockSpec((C, C), lambda i: (0, 0)), pl.BlockSpec((1, C), lambda i: (0, 0))],
        out_specs=(row, pl.BlockSpec((C, C), lambda i: (0, 0)), pl.BlockSpec((1, C), lambda i: (0, 0))),
        scratch_shapes=[pltpu.VMEM((SUBLANES, C), F32)], compiler_params=_cparams(("arbitrary",)),
    )(yraw, dyb, wmat, bias.reshape(1, C))


def _pool_counts(L, g):
    t = lax.broadcasted_iota(jnp.int32, (L, LANES), 0).astype(F32) + 1.0
    w = jnp.where(g == 0, 2.0, jnp.where(g == 1, 4.0, jnp.where(g == 2, 8.0, 16.0)))
    return 1.0 / jnp.minimum(t, w)


def _select_window(g, a2, a4, a8, a16):
    return jnp.where(g == 0, a2, jnp.where(g == 1, a4, jnp.where(g == 2, a8, a16)))


def _pooled(z, g):
    a2 = z + _down(z, 1)
    a4 = a2 + _down(a2, 2)
    a8 = a4 + _down(a4, 4)
    a16 = a8 + _down(a8, 8)
    return _select_window(g, a2, a4, a8, a16) * _pool_counts(z.shape[0], g) - z


def _pool_fwd(proj3, pool_w, scale, name):
    _, L, C = proj3.shape
    ng = len(POOL_WINDOWS)
    pg = C // ng
    assert pg == LANES

    def body(z_ref, w_ref, s_ref, o_ref):
        g = pl.program_id(0)
        p = _pooled(z_ref[...], g)
        y = jnp.dot(p.astype(BF16), w_ref[...].astype(BF16), preferred_element_type=F32)
        o_ref[...] = (y * s_ref[...]).astype(BF16)

    return pl.pallas_call(
        body, name=name, out_shape=jax.ShapeDtypeStruct((L, C), BF16), grid=(ng,),
        in_specs=[pl.BlockSpec((None, L, pg), lambda g: (0, 0, g)), pl.BlockSpec((None, pg, pg), lambda g: (g, 0, 0)),
                  pl.BlockSpec((1, pg), lambda g: (0, g))],
        out_specs=pl.BlockSpec((L, pg), lambda g: (0, g)), compiler_params=_cparams(("parallel",)),
    )(proj3, pool_w, scale.reshape(1, C))


def _pool_bwd(proj3, dmix, pool_w, scale, name):
    _, L, C = proj3.shape
    ng = len(POOL_WINDOWS)
    pg = C // ng

    def body(z_ref, d_ref, w_ref, s_ref, dz_ref, dw_ref, ds_ref):
        g = pl.program_id(0)
        p = _pooled(z_ref[...], g)
        pb = p.astype(BF16)
        wb = w_ref[...].astype(BF16)
        pre = jnp.dot(pb, wb, preferred_element_type=F32)
        dyc = d_ref[...]
        ds_ref[...] = jnp.sum(dyc * pre, axis=0, keepdims=True)
        dpre = (dyc * s_ref[...]).astype(BF16)
        dw_ref[...] = lax.dot_general(pb, dpre, (((0,), (0,)), ((), ())), preferred_element_type=F32)
        dp = lax.dot_general(dpre, wb, (((1,), (1,)), ((), ())), preferred_element_type=F32)
        v = dp * _pool_counts(L, g)
        a2 = v + _up(v, 1)
        a4 = a2 + _up(a2, 2)
        a8 = a4 + _up(a4, 4)
        a16 = a8 + _up(a8, 8)
        dz_ref[...] = (_select_window(g, a2, a4, a8, a16) - dp).astype(BF16)

    return pl.pallas_call(
        body, name=name,
        out_shape=(jax.ShapeDtypeStruct((L, C), BF16), jax.ShapeDtypeStruct((ng, pg, pg), F32),
                   jax.ShapeDtypeStruct((1, C), F32)),
        grid=(ng,),
        in_specs=[pl.BlockSpec((None, L, pg), lambda g: (0, 0, g)), pl.BlockSpec((L, pg), lambda g: (0, g)),
                  pl.BlockSpec((None, pg, pg), lambda g: (g, 0, 0)), pl.BlockSpec((1, pg), lambda g: (0, g))],
        out_specs=(pl.BlockSpec((L, pg), lambda g: (0, g)), pl.BlockSpec((None, pg, pg), lambda g: (g, 0, 0)),
                   pl.BlockSpec((1, pg), lambda g: (0, g))),
        compiler_params=_cparams(("parallel",)),
    )(proj3, dmix, pool_w, scale.reshape(1, C))


def _tril_w(w_ref, h):
    r = lax.broadcasted_iota(jnp.int32, (CHUNK, CHUNK), 0)
    c = lax.broadcasted_iota(jnp.int32, (CHUNK, CHUNK), 1)
    return jnp.where(r >= c, w_ref[h], 0.0)


def _sgu_fwd(proj3, norm_g, w, b, name):
    _, L, C = proj3.shape
    nh = w.shape[0]
    dh = C // nh
    assert dh == LANES and w.shape[1] == CHUNK
    tr = _pick(L, prefs=(512, 256, 128))
    bfull = jnp.broadcast_to(b[:, :, None], (nh, CHUNK, dh))

    def body(su_ref, sv_ref, g_ref, w_ref, b_ref, o_ref):
        sv = _gelu(sv_ref[...])
        r = lax.rsqrt(jnp.mean(sv * sv, axis=-1, keepdims=True) + EPS)
        v = (sv * r * g_ref[...]).astype(BF16)
        for h in range(nh):
            wm = _tril_w(w_ref, h).astype(BF16)
            cols = slice(h * dh, (h + 1) * dh)
            for n in range(tr // CHUNK):
                rows = slice(n * CHUNK, (n + 1) * CHUNK)
                mixed = jnp.dot(wm, v[rows, cols], preferred_element_type=F32) + b_ref[h]
                o_ref[rows, cols] = (_gelu(su_ref[rows, cols]) * mixed).astype(BF16)

    full = lambda shp: pl.BlockSpec(shp, lambda i: (0,) * len(shp))
    return pl.pallas_call(
        body, name=name, out_shape=jax.ShapeDtypeStruct((L, C), BF16), grid=(L // tr,),
        in_specs=[pl.BlockSpec((None, tr, C), lambda i: (1, i, 0)), pl.BlockSpec((None, tr, C), lambda i: (2, i, 0)),
                  full((1, C)), full((nh, CHUNK, CHUNK)), full((nh, CHUNK, dh))],
        out_specs=pl.BlockSpec((tr, C), lambda i: (i, 0)), compiler_params=_cparams(("parallel",)),
    )(proj3, proj3, norm_g.reshape(1, C), w, bfull)


def _sgu_bwd(proj3, dmix, norm_g, w, b, name):
    _, L, C = proj3.shape
    nh = w.shape[0]
    dh = C // nh
    tr = _pick(L, prefs=(512, 256, 128))
    nsteps = L // tr
    bfull = jnp.broadcast_to(b[:, :, None], (nh, CHUNK, dh))

    def body(su_ref, sv_ref, d_ref, g_ref, w_ref, b_ref, o_ref, dw_ref, db_ref, dg_ref, dv_ref, acc_g):
        i = pl.program_id(0)

        @pl.when(i == 0)
        def _():
            dw_ref[...] = jnp.zeros_like(dw_ref)
            db_ref[...] = jnp.zeros_like(db_ref)
            acc_g[...] = jnp.zeros_like(acc_g)

        svp = sv_ref[...]
        sv = _gelu(svp)
        r = lax.rsqrt(jnp.mean(sv * sv, axis=-1, keepdims=True) + EPS)
        vh = sv * r
        gv = g_ref[...]
        v = (vh * gv).astype(BF16)
        tri_r = lax.broadcasted_iota(jnp.int32, (CHUNK, CHUNK), 0)
        tri_c = lax.broadcasted_iota(jnp.int32, (CHUNK, CHUNK), 1)
        for h in range(nh):
            wm = _tril_w(w_ref, h).astype(BF16)
            cols = slice(h * dh, (h + 1) * dh)
            dwh = jnp.zeros((CHUNK, CHUNK), F32)
            dbh = jnp.zeros((CHUNK, dh), F32)
            for n in range(tr // CHUNK):
                rows = slice(n * CHUNK, (n + 1) * CHUNK)
                vb = v[rows, cols]
                mixed = jnp.dot(wm, vb, preferred_element_type=F32) + b_ref[h]
                sup = su_ref[rows, cols]
                dyd = d_ref[rows, cols]
                dmx = dyd * _gelu(sup)
                o_ref[0, rows, cols] = (dyd * mixed * _gelu_grad(sup)).astype(BF16)
                dmb = dmx.astype(BF16)
                dwh = dwh + lax.dot_general(dmb, vb, (((1,), (1,)), ((), ())), preferred_element_type=F32)
                dbh = dbh + dmx
                dv_ref[rows, cols] = lax.dot_general(wm, dmb, (((0,), (0,)), ((), ())), preferred_element_type=F32)
            dw_ref[h] += jnp.where(tri_r >= tri_c, dwh, 0.0)
            db_ref[h] += dbh
        dv = dv_ref[...]
        acc_g[...] += jnp.sum((dv * vh).reshape(tr // SUBLANES, SUBLANES, C), axis=0)
        dvg = dv * gv
        dsv = r * (dvg - vh * jnp.mean(dvg * vh, axis=-1, keepdims=True))
        o_ref[1] = (dsv * _gelu_grad(svp)).astype(BF16)

        @pl.when(i == nsteps - 1)
        def _():
            dg_ref[...] = jnp.sum(acc_g[...], axis=0, keepdims=True)

    full = lambda shp: pl.BlockSpec(shp, lambda i: (0,) * len(shp))
    return pl.pallas_call(
        body, name=name,
        out_shape=(jax.ShapeDtypeStruct((2, L, C), BF16), jax.ShapeDtypeStruct((nh, CHUNK, CHUNK), F32),
                   jax.ShapeDtypeStruct((nh, CHUNK, dh), F32), jax.ShapeDtypeStruct((1, C), F32)),
        grid=(nsteps,),
        in_specs=[pl.BlockSpec((None, tr, C), lambda i: (1, i, 0)), pl.BlockSpec((None, tr, C), lambda i: (2, i, 0)),
                  pl.BlockSpec((tr, C), lambda i: (i, 1)), full((1, C)), full((nh, CHUNK, CHUNK)),
                  full((nh, CHUNK, dh))],
        out_specs=(pl.BlockSpec((2, tr, C), lambda i: (0, i, 0)), full((nh, CHUNK, CHUNK)), full((nh, CHUNK, dh)),
                   full((1, C))),
        scratch_shapes=[pltpu.VMEM((tr, C), F32), pltpu.VMEM((SUBLANES, C), F32)],
        compiler_params=_cparams(("arbitrary",)),
    )(proj3, proj3, dmix, norm_g.reshape(1, C), w, bfull)


def _ffn_act_fwd(up3, conv_w, conv_b, name):
    _, L, Fh = up3.shape
    cb = LANES
    w2 = conv_w.reshape(3, 2, Fh).transpose(1, 0, 2)
    b2 = conv_b.reshape(2, 1, Fh)

    def body(u_ref, w_ref, b_ref, o_ref):
        g = _conv3(u_ref[0].astype(F32), w_ref[0]) + b_ref[0]
        v = _conv3(u_ref[1].astype(F32), w_ref[1]) + b_ref[1]
        o_ref[...] = (g * _sigmoid(g) * v).astype(BF16)

    return pl.pallas_call(
        body, name=name, out_shape=jax.ShapeDtypeStruct((L, Fh), BF16), grid=(Fh // cb,),
        in_specs=[pl.BlockSpec((2, L, cb), lambda j: (0, 0, j)), pl.BlockSpec((2, 3, cb), lambda j: (0, 0, j)),
                  pl.BlockSpec((2, 1, cb), lambda j: (0, 0, j))],
        out_specs=pl.BlockSpec((L, cb), lambda j: (0, j)), compiler_params=_cparams(("parallel",)),
    )(up3, w2, b2)


def _ffn_act_bwd(up3, da, conv_w, conv_b, name):
    _, L, Fh = up3.shape
    cb = LANES
    w2 = conv_w.reshape(3, 2, Fh).transpose(1, 0, 2)
    b2 = conv_b.reshape(2, 1, Fh)

    def body(u_ref, d_ref, w_ref, b_ref, o_ref, dw_ref, db_ref):
        ug, uv = u_ref[0].astype(F32), u_ref[1].astype(F32)
        wg, wv = w_ref[0], w_ref[1]
        g = _conv3(ug, wg) + b_ref[0]
        v = _conv3(uv, wv) + b_ref[1]
        sg = _sigmoid(g)
        dav = d_ref[...].astype(F32)
        dg = dav * v * (sg * (1.0 + g * (1.0 - sg)))
        dv = dav * (g * sg)
        o_ref[0] = _conv3_t(dg, wg).astype(BF16)
        o_ref[1] = _conv3_t(dv, wv).astype(BF16)
        for tap, (dwg, dwv) in enumerate(zip(_conv3_dw(dg, ug), _conv3_dw(dv, uv))):
            dw_ref[0, tap:tap + 1, :] = dwg
            dw_ref[1, tap:tap + 1, :] = dwv
        db_ref[0] = jnp.sum(dg, axis=0, keepdims=True)
        db_ref[1] = jnp.sum(dv, axis=0, keepdims=True)

    dup, dw2, db2 = pl.pallas_call(
        body, name=name,
        out_shape=(jax.ShapeDtypeStruct((2, L, Fh), BF16), jax.ShapeDtypeStruct((2, 3, Fh), F32),
                   jax.ShapeDtypeStruct((2, 1, Fh), F32)),
        grid=(Fh // cb,),
        in_specs=[pl.BlockSpec((2, L, cb), lambda j: (0, 0, j)), pl.BlockSpec((L, cb), lambda j: (0, j)),
                  pl.BlockSpec((2, 3, cb), lambda j: (0, 0, j)), pl.BlockSpec((2, 1, cb), lambda j: (0, 0, j))],
        out_specs=(pl.BlockSpec((2, L, cb), lambda j: (0, 0, j)), pl.BlockSpec((2, 3, cb), lambda j: (0, 0, j)),
                   pl.BlockSpec((2, 1, cb), lambda j: (0, 0, j))),
        compiler_params=_cparams(("parallel",)),
    )(up3, da, w2, b2)
    return dup, dw2.transpose(1, 0, 2).reshape(3, 2 * Fh), db2.reshape(2 * Fh)


def _local_step(x, tgt, w, layer_weights, on_layer_grads):
    L, D = x.shape
    depth = w['norm_mix_g'].shape[0]
    saved = []
    for i in range(depth):
        j = i // 2
        wb = layer_weights(i, x)
        s = {'x': x, 'wb': wb}
        h = _rms_fwd(x, w['norm_mix_g'][i], "mix_norm_fwd")
        s['h'] = h
        if i % 2 == 0:
            proj4 = _mm(h, wb['even_w_in'], 'nn', F32, "even_in_fwd", ok=('seg', 4))
            s['proj'] = proj4
            ya = _sconv_fwd(proj4, w['even_conv_w'][j], "sconv_fwd")
            prm = (w['ssm_log_step'][j], w['ssm_a_re'][j], w['ssm_a_im'][j], w['ssm_b_re'][j], w['ssm_b_im'][j],
                   w['ssm_c_re'][j], w['ssm_c_im'][j])
            (lr, li, bmat, cmat), prep_vjp = jax.vjp(_s5_prep, *prm)
            u = _to_scan_order(proj4[3])
            yraw, s_re, s_im = _s5_fwd(u, lr, li, bmat, cmat, w['ssm_d'][j], "s5_fwd")
            yb = _glu_fwd(yraw, wb['ssm_glu_w'], w['ssm_glu_b'][j], "glu_fwd")
            s.update(u=u, yraw=yraw, s_re=s_re, s_im=s_im, s5=(lr, li, bmat, cmat), prep_vjp=prep_vjp)
            mixin = jnp.concatenate([ya, _from_scan_order(yb)], axis=1)
            x = _mm(mixin, wb['even_w_out'], 'nn', F32, "even_out_fwd", res=x)
        else:
            proj3 = _mm(h, wb['odd_w_in'], 'nn', F32, "odd_in_fwd", ok=('seg', 3))
            s['proj'] = proj3
            yc = _pool_fwd(proj3, w['pool_w'][j], w['pool_scale'][j], "pool_fwd")
            yd = _sgu_fwd(proj3, w['sgu_norm_g'][j], w['sgu_w'][j], w['sgu_b'][j], "sgu_fwd")
            mixin = jnp.concatenate([yc, yd], axis=1)
            x = _mm(mixin, wb['odd_w_out'], 'nn', F32, "odd_out_fwd", res=x)
        s['mixin'] = mixin
        s['x1'] = x
        h2 = _rms_fwd(x, w['norm_ffn_g'][i], "ffn_norm_fwd")
        up3 = _mm(h2, wb['ffn_w_up'], 'nn', BF16, "ffn_up_fwd", ok=('seg', 2))
        a = _ffn_act_fwd(up3, w['ffn_conv_w'][i], w['ffn_conv_b'][i], "ffn_act_fwd")
        x = _mm(a, wb['ffn_w_down'], 'nn', F32, "ffn_down_fwd", res=x)
        s.update(h2=h2, up3=up3, a=a)
        saved.append(s)

    loss8, dx, dg_final = _loss_head(x, w['norm_final_g'], tgt)
    gs = {n: [None] * w[n].shape[0] for n in SMALL if n != 'norm_final_g'}
    gs['norm_final_g'] = dg_final.reshape(D)

    for i in reversed(range(depth)):
        j = i // 2
        s = saved[i]
        wb = s['wb']
        gb = {}
        da = _mm(dx, wb['ffn_w_down'], 'nt', BF16, "ffn_down_dgrad")
        gb['ffn_w_down'] = _mm(s['a'].T, dx, 'nn', BF16, "ffn_down_wgrad")
        dup3, dcw, dcb = _ffn_act_bwd(s['up3'], da, w['ffn_conv_w'][i], w['ffn_conv_b'][i], "ffn_act_bwd")
        gs['ffn_conv_w'][i], gs['ffn_conv_b'][i] = dcw, dcb
        gb['ffn_w_up'] = _mm(s['h2'].T, dup3, 'nn', BF16, "ffn_up_wgrad", bk=('seg', 2))
        dh2 = _mm(dup3, wb['ffn_w_up'], 'nt', F32, "ffn_up_dgrad", ak=('seg', 2))
        dx, dg = _rms_bwd(s['x1'], w['norm_ffn_g'][i], dh2, dx, "ffn_norm_bwd")
        gs['norm_ffn_g'][i] = dg.reshape(D)
        if i % 2 == 0:
            dmix = _mm(dx, wb['even_w_out'], 'nt', F32, "even_out_dgrad")
            gb['even_w_out'] = _mm(s['mixin'].T, dx, 'nn', BF16, "even_out_wgrad")
            dpc, dcw = _sconv_bwd(s['proj'], dmix, w['even_conv_w'][j], "sconv_bwd")
            gs['even_conv_w'][j] = dcw
            dyb = _to_scan_order(dmix[:, D // 2:])
            dyraw, dglu_w, dglu_b = _glu_bwd(s['yraw'], dyb, wb['ssm_glu_w'], w['ssm_glu_b'][j], "glu_bwd")
            gb['ssm_glu_w'] = dglu_w.astype(BF16)
            gs['ssm_glu_b'][j] = dglu_b.reshape(-1)
            lr, li, bmat, cmat = s['s5']
            du, dbm, dcm, dlam, dd = _s5_bwd(dyraw, s['u'], s['s_re'], s['s_im'], lr, li, bmat, cmat,
                                            w['ssm_d'][j], "s5_bwd")
            gs['ssm_d'][j] = dd.reshape(-1)
            dcm = jnp.swapaxes(dcm, 1, 2)
            dprm = s['prep_vjp']((dlam[:, 0:1, :], dlam[:, 1:2, :], dbm, dcm))
            for n, gval in zip(('ssm_log_step', 'ssm_a_re', 'ssm_a_im', 'ssm_b_re', 'ssm_b_im', 'ssm_c_re',
                                'ssm_c_im'), dprm):
                gs[n][j] = gval
            dproj = jnp.concatenate([dpc, _from_scan_order(du).astype(BF16)[None]], axis=0)
            gb['even_w_in'] = _mm(s['h'].T, dproj, 'nn', BF16, "even_in_wgrad", bk=('seg', 4))
            dh = _mm(dproj, wb['even_w_in'], 'nt', F32, "even_in_dgrad", ak=('seg', 4))
        else:
            dmix = _mm(dx, wb['odd_w_out'], 'nt', F32, "odd_out_dgrad")
            gb['odd_w_out'] = _mm(s['mixin'].T, dx, 'nn', BF16, "odd_out_wgrad")
            dz, dpw, dps = _pool_bwd(s['proj'], dmix, w['pool_w'][j], w['pool_scale'][j], "pool_bwd")
            gs['pool_w'][j], gs['pool_scale'][j] = dpw, dps.reshape(-1)
            dsuv, dsw, dsb, dsg = _sgu_bwd(s['proj'], dmix, w['sgu_norm_g'][j], w['sgu_w'][j], w['sgu_b'][j],
                                           "sgu_bwd")
            gs['sgu_w'][j], gs['sgu_b'][j], gs['sgu_norm_g'][j] = dsw, jnp.sum(dsb, axis=-1), dsg.reshape(-1)
            dproj = jnp.concatenate([dz[None], dsuv], axis=0)
            gb['odd_w_in'] = _mm(s['h'].T, dproj, 'nn', BF16, "odd_in_wgrad", bk=('seg', 3))
            dh = _mm(dproj, wb['odd_w_in'], 'nt', F32, "odd_in_dgrad", ak=('seg', 3))
        dx, dg = _rms_bwd(s['x'], w['norm_mix_g'][i], dh, dx, "mix_norm_bwd")
        gs['norm_mix_g'][i] = dg.reshape(D)
        on_layer_grads(i, gb)

    gsmall = {n: (v if n == 'norm_final_g' else jnp.stack(v)) for n, v in gs.items()}
    return loss8[0, 0], dx, gsmall


_HBM = pl.BlockSpec(memory_space=pltpu.HBM)
_CHIP_FLIPS = ((0, 0), (1, 0), (0, 1), (1, 1))


def _coords():
    return lax.axis_index("x"), lax.axis_index("y"), lax.axis_index("c")


def _flip(v, f):
    return 1 - v if f else v


def _shard_of(ref, axis, s, width):
    start = pl.multiple_of(s * width, LANES if axis == ref.ndim - 1 else 16) if width % 16 == 0 else s * width
    idx = [slice(None)] * ref.ndim
    idx[axis] = pl.ds(start, width)
    return ref.at[tuple(idx)]


def _gather_over_chips(shards, axes):
    n = len(shards)

    def body(*refs):
        ins, outs = refs[:n], refs[n:2 * n]
        send_sems, recv_sems, loc_sems = refs[2 * n:]
        x, y, c = _coords()

        def place(t, px, py):
            return _shard_of(outs[t], axes[t], 2 * px + py, ins[t].shape[axes[t]])

        def remote(t, f, dst_chip):
            fx, fy = _CHIP_FLIPS[f]
            return pltpu.make_async_remote_copy(
                src_ref=ins[t], dst_ref=place(t, *dst_chip), send_sem=send_sems.at[3 * t + f - 1],
                recv_sem=recv_sems.at[3 * t + f - 1], device_id=(_flip(x, fx), _flip(y, fy), c), device_id_type=MESH)

        local = [pltpu.make_async_copy(ins[t], place(t, x, y), loc_sems.at[t]) for t in range(n)]
        sends = [remote(t, f, (x, y)) for t in range(n) for f in (1, 2, 3)]
        for cp in local + sends:
            cp.start()
        for t in range(n):
            for f in (1, 2, 3):
                fx, fy = _CHIP_FLIPS[f]
                remote(t, f, (_flip(x, fx), _flip(y, fy))).wait_recv()
        for cp in sends:
            cp.wait_send()
        for cp in local:
            cp.wait()

    out_shape = []
    for a, ax in zip(shards, axes):
        shp = list(a.shape)
        shp[ax] *= N_CHIPS
        out_shape.append(jax.ShapeDtypeStruct(tuple(shp), a.dtype))
    return pl.pallas_call(
        body, name="gather_weights", out_shape=tuple(out_shape), in_specs=[_HBM] * n, out_specs=tuple([_HBM] * n),
        scratch_shapes=[pltpu.SemaphoreType.DMA((3 * n,)), pltpu.SemaphoreType.DMA((3 * n,)),
                        pltpu.SemaphoreType.DMA((n,))],
    )(*shards)


_SEM = pl.BlockSpec(memory_space=pltpu.SEMAPHORE)
_ANY = pl.BlockSpec(memory_space=pl.ANY)
_DATAFLOW = pltpu.SideEffectType.DATAFLOW_SIDE_EFFECTING


def _in_hbm(a):
    return pltpu.with_memory_space_constraint(a, pltpu.HBM)


def _model_layer(name, l):
    if name.startswith('ffn'):
        return l
    return 2 * l + 1 if name.startswith('odd') else 2 * l


def _place_own(shards, pairs, axes):
    nt, npair = len(shards), len(pairs)

    def body(*refs):
        ins, outs, sems = refs[:nt], refs[nt:nt + npair], refs[-1]
        x, y, _ = _coords()
        cps = [pltpu.make_async_copy(ins[t].at[l], _shard_of(outs[k], axes[t], 2 * x + y, ins[t].shape[axes[t] + 1]),
                                     sems.at[k]) for k, (t, l) in enumerate(pairs)]
        for cp in cps:
            cp.start()
        for cp in cps:
            cp.wait()

    out_shape = []
    for t, l in pairs:
        shp = list(shards[t].shape[1:])
        shp[axes[t]] *= N_CHIPS
        out_shape.append(jax.ShapeDtypeStruct(tuple(shp), shards[t].dtype))
    return pl.pallas_call(
        body, name="place_own_quarters", out_shape=tuple(out_shape), in_specs=[_HBM] * nt,
        out_specs=tuple([_HBM] * npair), scratch_shapes=[pltpu.SemaphoreType.DMA((npair,))],
    )(*shards)


def _gather_copies(shard_refs, land_refs, sems, pairs, axes, group, g, landing_chip_of):
    x, y, c = _coords()
    out = []
    for j, k in enumerate(group):
        t, l = pairs[k]
        width = shard_refs[t].shape[axes[t] + 1]
        for f in (1, 2, 3):
            fx, fy = _CHIP_FLIPS[f]
            px, py = _flip(x, fx), _flip(y, fy)
            lx, ly = landing_chip_of(px, py)
            out.append(pltpu.make_async_remote_copy(
                src_ref=shard_refs[t].at[l], dst_ref=_shard_of(land_refs[k], axes[t], 2 * lx + ly, width),
                send_sem=sems[2 * g].at[3 * j + f - 1], recv_sem=sems[2 * g + 1].at[3 * j + f - 1],
                device_id=(px, py, c), device_id_type=MESH))
    return out


def _gather_start(shards, lands, pairs, axes, groups):
    nt, npair, ng = len(shards), len(pairs), len(groups)

    def body(*refs):
        shard_refs, land_refs = refs[:nt], refs[nt:nt + npair]
        sems = refs[nt + npair:nt + npair + 2 * ng]
        token = refs[-1]
        x, y, _ = _coords()
        for g, group in enumerate(groups):
            for cp in _gather_copies(shard_refs, land_refs, sems, pairs, axes, group, g, lambda px, py: (x, y)):
                cp.start()
        token[...] = jnp.zeros_like(token)

    sem_shapes = []
    for group in groups:
        sem_shapes += [pltpu.SemaphoreType.DMA((3 * len(group),)), pltpu.SemaphoreType.DMA((3 * len(group),))]
    thru = [pltpu.HBM(a.shape, a.dtype) for a in list(shards) + list(lands)]
    outs = pl.pallas_call(
        body, name="gather_start",
        out_shape=tuple(sem_shapes + thru + [jax.ShapeDtypeStruct((SUBLANES, LANES), F32)]),
        in_specs=[_HBM] * (nt + npair),
        out_specs=tuple([_SEM] * (2 * ng) + [_HBM] * (nt + npair) + [pl.BlockSpec(memory_space=pltpu.VMEM)]),
        input_output_aliases={i: 2 * ng + i for i in range(nt + npair)},
        compiler_params=pltpu.CompilerParams(has_side_effects=_DATAFLOW),
    )(*[_in_hbm(a) for a in list(shards) + list(lands)])
    sems = outs[:2 * ng]
    return sems, list(outs[2 * ng:2 * ng + nt]), list(outs[2 * ng + nt:2 * ng + nt + npair]), outs[-1]


def _gather_wait(g, shards, lands_g, send_sem, recv_sem, after, pairs, axes, group):
    nt, n = len(shards), len(group)

    def body(*refs):
        shard_refs, land_g = refs[:nt], refs[nt:nt + n]
        sems = {2 * g: refs[nt + n], 2 * g + 1: refs[nt + n + 1]}
        land_refs = {k: land_g[j] for j, k in enumerate(group)}
        for cp in _gather_copies(shard_refs, land_refs, sems, pairs, axes, group, g, lambda px, py: (px, py)):
            cp.wait_send()
            cp.wait_recv()

    thru = [pltpu.HBM(a.shape, a.dtype) for a in list(shards) + list(lands_g)]
    outs = pl.pallas_call(
        body, name=f"gather_wait_{g}", out_shape=tuple(thru),
        in_specs=[_HBM] * (nt + n) + [_SEM, _SEM, _ANY], out_specs=tuple([_HBM] * (nt + n)),
        input_output_aliases={i: i for i in range(nt + n)},
        compiler_params=pltpu.CompilerParams(has_side_effects=_DATAFLOW),
    )(*shards, *lands_g, send_sem, recv_sem, after)
    return list(outs[:nt]), list(outs[nt:])


def _scatter_grads(names, grads, axes):
    items = [(t, l) for t in range(len(names)) for l in range(len(grads[t]))]
    arrays = [grads[t][l] for t, l in items]
    ni, nt = len(items), len(names)
    halves = [len(g) // 2 for g in grads]
    base = np.concatenate([[0], np.cumsum(halves)]).astype(int)

    def quarter_shape(t):
        r, c = grads[t][0].shape
        return (r, c // N_CHIPS) if axes[t] == 1 else (r // N_CHIPS, c)

    def body(*refs):
        ins, outs = refs[:ni], refs[ni:ni + nt]
        send_sems, recv_sems, loc_sems = refs[ni + nt:]
        x, y, c = _coords()
        started = []
        for idx, (t, l) in enumerate(items):
            owner, q = l // halves[t], l % halves[t]
            other = c if owner == 0 else 1 - c
            width = quarter_shape(t)[axes[t]]
            for f, (fx, fy) in enumerate(_CHIP_FLIPS):
                px, py = _flip(x, fx), _flip(y, fy)
                src = _shard_of(ins[idx], axes[t], 2 * px + py, width)
                slot = f + 4 * other
                dst = outs[t].at[q, slot]
                rem = pltpu.make_async_remote_copy(
                    src_ref=src, dst_ref=dst, send_sem=send_sems.at[4 * idx + f],
                    recv_sem=recv_sems.at[(int(base[t]) + q) * N_DEV + slot], device_id=(px, py, owner),
                    device_id_type=MESH)
                if f == 0:
                    loc = pltpu.make_async_copy(src, dst, loc_sems.at[idx])
                    pl.when(other == 0)(loc.start)
                    pl.when(other == 1)(rem.start)
                    started.append((other, loc, rem))
                else:
                    rem.start()
                    started.append((None, None, rem))
        for t in range(nt):
            for q in range(halves[t]):
                for slot in range(1, N_DEV):
                    land = outs[t].at[q, slot]
                    pltpu.make_async_remote_copy(
                        src_ref=land, dst_ref=land, send_sem=send_sems.at[0],
                        recv_sem=recv_sems.at[(int(base[t]) + q) * N_DEV + slot], device_id=(x, y, c),
                        device_id_type=MESH).wait_recv()
        for other, loc, rem in started:
            if loc is None:
                rem.wait_send()
            else:
                pl.when(other == 0)(loc.wait)
                pl.when(other == 1)(rem.wait_send)

    out_shape = tuple(jax.ShapeDtypeStruct((halves[t], N_DEV) + quarter_shape(t), BF16) for t in range(nt))
    return pl.pallas_call(
        body, name="scatter_grads", out_shape=out_shape, in_specs=[_HBM] * ni, out_specs=tuple([_HBM] * nt),
        scratch_shapes=[pltpu.SemaphoreType.DMA((4 * ni,)), pltpu.SemaphoreType.DMA((int(base[-1]) * N_DEV,)),
                        pltpu.SemaphoreType.DMA((ni,))],
    )(*arrays)


def _sum_slots(recv, name):
    n, ns, r, c = recv.shape
    tr = _pick(r, prefs=(256, 128, 64, 32, 16))

    def body(i_ref, o_ref):
        acc = i_ref[0].astype(F32)
        for s in range(1, ns):
            acc = acc + i_ref[s].astype(F32)
        o_ref[...] = acc

    return pl.pallas_call(
        body, name=name, out_shape=jax.ShapeDtypeStruct((n, r, c), F32), grid=(n, r // tr),
        in_specs=[pl.BlockSpec((None, ns, tr, c), lambda h, i: (h, 0, i, 0))],
        out_specs=pl.BlockSpec((None, tr, c), lambda h, i: (h, i, 0)),
        compiler_params=_cparams(("parallel", "parallel")),
    )(recv)


def _sum_and_share(recv, name):
    n, ns, r, c = recv.shape
    tr = _pick(r, prefs=(256, 128, 64, 32, 16))
    nr = r // tr
    nsteps = n * nr

    def body(i_ref, o_ref, buf, loc_sems, send_sems, recv_sems):
        h, i = pl.program_id(0), pl.program_id(1)
        step = h * nr + i
        slot = step % 2
        x, y, core = _coords()

        def copies(sl):
            dst = o_ref.at[core * n + h, pl.ds(pl.multiple_of(i * tr, tr), tr), :]
            loc = pltpu.make_async_copy(buf.at[sl], dst, loc_sems.at[sl])
            rem = pltpu.make_async_remote_copy(
                src_ref=buf.at[sl], dst_ref=dst, send_sem=send_sems.at[sl], recv_sem=recv_sems.at[step],
                device_id=(x, y, 1 - core), device_id_type=MESH)
            return loc, rem

        def drain(sl):
            loc, rem = copies(sl)
            loc.wait()
            rem.wait_send()

        pl.when(step >= 2)(lambda: drain(slot))
        acc = i_ref[0].astype(F32)
        for s in range(1, ns):
            acc = acc + i_ref[s].astype(F32)
        buf[slot] = acc
        loc, rem = copies(slot)
        loc.start()
        rem.start()

        @pl.when(step == nsteps - 1)
        def _():
            drain(slot)
            if nsteps > 1:
                drain(1 - slot)
            for hh in range(n):
                for ii in range(nr):
                    land = o_ref.at[(1 - core) * n + hh, pl.ds(ii * tr, tr), :]
                    pltpu.make_async_remote_copy(
                        src_ref=buf.at[0], dst_ref=land, send_sem=send_sems.at[0], recv_sem=recv_sems.at[hh * nr + ii],
                        device_id=(x, y, 1 - core), device_id_type=MESH).wait_recv()

    return pl.pallas_call(
        body, name=name, out_shape=jax.ShapeDtypeStruct((2 * n, r, c), F32), grid=(n, nr),
        in_specs=[pl.BlockSpec((None, ns, tr, c), lambda h, i: (h, 0, i, 0))], out_specs=_HBM,
        scratch_shapes=[pltpu.VMEM((2, tr, c), F32), pltpu.SemaphoreType.DMA((2,)), pltpu.SemaphoreType.DMA((2,)),
                        pltpu.SemaphoreType.DMA((nsteps,))],
        compiler_params=_cparams(("arbitrary", "arbitrary")),
    )(recv)


def _gather_over_devices(pack):
    def body(i_ref, o_ref, send_sems, recv_sems, loc_sem):
        x, y, c = _coords()
        me = 4 * x + 2 * y + c
        masks = [(m >> 2 & 1, m >> 1 & 1, m & 1) for m in range(1, N_DEV)]
        local = pltpu.make_async_copy(i_ref, o_ref.at[me], loc_sem)
        sends = [pltpu.make_async_remote_copy(
            src_ref=i_ref, dst_ref=o_ref.at[me], send_sem=send_sems.at[k], recv_sem=recv_sems.at[me],
            device_id=(_flip(x, fx), _flip(y, fy), _flip(c, fc)), device_id_type=MESH)
            for k, (fx, fy, fc) in enumerate(masks)]
        for cp in [local] + sends:
            cp.start()
        for fx, fy, fc in masks:
            px, py, pc = _flip(x, fx), _flip(y, fy), _flip(c, fc)
            peer = 4 * px + 2 * py + pc
            pltpu.make_async_remote_copy(
                src_ref=i_ref, dst_ref=o_ref.at[peer], send_sem=send_sems.at[0], recv_sem=recv_sems.at[peer],
                device_id=(px, py, pc), device_id_type=MESH).wait_recv()
        for cp in sends:
            cp.wait_send()
        local.wait()

    return pl.pallas_call(
        body, name="gather_small_grads", out_shape=jax.ShapeDtypeStruct((N_DEV,) + pack.shape, pack.dtype),
        in_specs=[_HBM], out_specs=_HBM,
        scratch_shapes=[pltpu.SemaphoreType.DMA((N_DEV - 1,)), pltpu.SemaphoreType.DMA((N_DEV,)),
                        pltpu.SemaphoreType.DMA],
    )(pack)


def _adamw(w, g, m, v, name):
    R, C = w.shape
    tr = _pick(R, prefs=(256, 128, 64, 32, 16, 8))
    bc1 = 1.0 - ADAM_B1 ** ADAM_STEP
    bc2 = 1.0 - ADAM_B2 ** ADAM_STEP

    def body(w_ref, g_ref, m_ref, v_ref, d_ref, mo_ref, vo_ref):
        gv = g_ref[...]
        mn = ADAM_B1 * m_ref[...] + (1.0 - ADAM_B1) * gv
        vn = ADAM_B2 * v_ref[...] + (1.0 - ADAM_B2) * (gv * gv)
        d_ref[...] = -ADAM_LR * ((mn / bc1) / (jnp.sqrt(vn / bc2) + ADAM_EPS) + ADAM_WD * w_ref[...])
        mo_ref[...] = mn
        vo_ref[...] = vn

    blk = pl.BlockSpec((tr, C), lambda i: (i, 0))
    sds = jax.ShapeDtypeStruct((R, C), F32)
    return pl.pallas_call(
        body, name=name, out_shape=(sds, sds, sds), grid=(R // tr,), in_specs=[blk] * 4, out_specs=(blk,) * 3,
        compiler_params=_cparams(("parallel",)),
    )(w, g, m, v)


_PACK_QUANTUM = 256 * LANES


def _pack(arrs):
    flat = jnp.concatenate([a.reshape(-1).astype(F32) for a in arrs])
    flat = jnp.pad(flat, (0, (-flat.shape[0]) % _PACK_QUANTUM))
    return flat.reshape(-1, LANES)


def _unpack(p, shapes):
    flat = p.reshape(-1)
    out, off = [], 0
    for s in shapes:
        n = int(np.prod(s))
        out.append(flat[off:off + n].reshape(s))
        off += n
    return out


def kernel(*args):
    nw = len(WEIGHTS)
    x, tgt = args[0], args[1 + nw]
    w = dict(zip(WEIGHTS, args[1:1 + nw]))
    m = dict(zip(WEIGHTS, args[2 + nw:2 + 2 * nw]))
    v = dict(zip(WEIGHTS, args[2 + 2 * nw:2 + 3 * nw]))
    _, L, D = x.shape
    chip = 2 * lax.axis_index("x") + lax.axis_index("y")

    big = list(BIG)
    small_sh_shapes = [w[n].shape for n in SMALL_SHARDED]
    axes2 = [BIG[n] - 1 for n in big]
    shards = [w[n].astype(BF16) for n in big]
    pairs = [(t, l) for t in range(len(big)) for l in range(shards[t].shape[0])]
    depth = w['norm_mix_g'].shape[0]
    groups = [[k for k, (t, l) in enumerate(pairs) if _model_layer(big[t], l) == i] for i in range(depth)]
    sems, shards_thru, lands, token = _gather_start(shards, _place_own(shards, pairs, axes2), pairs, axes2, groups)
    in_flight = {'shards': shards_thru}

    def layer_weights(i, after):
        sh, landed = _gather_wait(i, in_flight['shards'], [lands[k] for k in groups[i]], sems[2 * i], sems[2 * i + 1],
                                  token if i == 0 else after, pairs, axes2, groups[i])
        in_flight['shards'] = sh
        return {big[pairs[k][0]]: a for k, a in zip(groups[i], landed)}

    gbig = {n: [None] * w[n].shape[0] for n in big}

    def on_layer_grads(i, gb):
        for n, g in gb.items():
            gbig[n][i if n.startswith('ffn') else i // 2] = g

    packed = _gather_over_chips([_pack([w[n] for n in SMALL_SHARDED])[None]], [0])[0]
    per_chip = [_unpack(packed[s], small_sh_shapes) for s in range(N_CHIPS)]
    wl = dict(w)
    for k, n in enumerate(SMALL_SHARDED):
        wl[n] = jnp.concatenate([per_chip[s][k] for s in range(N_CHIPS)], axis=-1)

    loss, dx, gsmall = _local_step(x.reshape(L, D), tgt.reshape(L, D), wl, layer_weights, on_layer_grads)
    loss = lax.psum(loss, ("x", "y", "c"))

    landed = _scatter_grads(big, [gbig[n] for n in big], [BIG[n] - 1 for n in big])
    gshard = {n: _sum_and_share(r, "sum_share_" + n) for n, r in zip(big, landed)}

    small_shapes = [gsmall[n].shape for n in SMALL]
    gpack = _sum_slots(_gather_over_devices(_pack([gsmall[n] for n in SMALL]))[None], "sum_small_grads")[0]
    gs = dict(zip(SMALL, _unpack(gpack, small_shapes)))
    for n in SMALL_SHARDED:
        width = w[n].shape[-1]
        gs[n] = lax.dynamic_slice_in_dim(gs[n], chip * width, width, axis=gs[n].ndim - 1)

    grads, delta, new_m, new_v = {}, {}, {}, {}
    for n in big:
        shp = w[n].shape
        flat = lambda a: a.reshape(shp[0] * shp[1], shp[2])
        g = gshard[n]
        grads[n] = g
        d_, m_, v_ = _adamw(flat(w[n]), flat(g), flat(m[n]), flat(v[n]), "adamw_" + n)
        delta[n], new_m[n], new_v[n] = d_.reshape(shp), m_.reshape(shp), v_.reshape(shp)
    loc_shapes = [w[n].shape for n in SMALL]
    d_, m_, v_ = _adamw(_pack([w[n] for n in SMALL]), _pack([gs[n] for n in SMALL]), _pack([m[n] for n in SMALL]),
                        _pack([v[n] for n in SMALL]), "adamw_small")
    for n, dn, mn, vn in zip(SMALL, _unpack(d_, loc_shapes), _unpack(m_, loc_shapes), _unpack(v_, loc_shapes)):
        grads[n], delta[n], new_m[n], new_v[n] = gs[n], dn, mn, vn

    return (loss, dx.reshape(1, L, D), *[grads[n] for n in WEIGHTS], *[delta[n] for n in WEIGHTS],
            *[new_m[n] for n in WEIGHTS], *[new_v[n] for n in WEIGHTS])
```

```python
import functools
import math

import numpy as np
import jax
import jax.numpy as jnp
from jax import lax
from jax.experimental import pallas as pl
from jax.experimental.pallas import tpu as pltpu

F32 = jnp.float32
BF16 = jnp.bfloat16
MESH = pl.DeviceIdType.MESH

EPS = 1e-6
CHUNK = 128
POOL_WINDOWS = (2, 4, 8, 16)
LANES = 128
SUBLANES = 8
SCAN_CHUNKS = SUBLANES
S5_GROUPS_PER_STEP = 4
MM_TM_CAP, MM_TN_CAP, MM_TK_CAP = 1024, 1408, 1408
MM_TK_WHOLE = 2048
VMEM_LIMIT = 48 * 1024 * 1024
VMEM_LIMIT_S5 = 56 * 1024 * 1024

ADAM_LR, ADAM_B1, ADAM_B2, ADAM_EPS, ADAM_WD, ADAM_STEP = 0.001, 0.9, 0.999, 1e-08, 0.01, 10

WEIGHTS = ['norm_mix_g', 'even_w_in', 'even_conv_w', 'ssm_log_step', 'ssm_a_re', 'ssm_a_im', 'ssm_b_re',
           'ssm_b_im', 'ssm_c_re', 'ssm_c_im', 'ssm_d', 'ssm_glu_w', 'ssm_glu_b', 'even_w_out', 'odd_w_in',
           'pool_w', 'pool_scale', 'sgu_norm_g', 'sgu_w', 'sgu_b', 'odd_w_out', 'norm_ffn_g', 'ffn_w_up',
           'ffn_conv_w', 'ffn_conv_b', 'ffn_w_down', 'norm_final_g']
BIG = {'even_w_in': 2, 'ssm_glu_w': 1, 'even_w_out': 1, 'odd_w_in': 2, 'odd_w_out': 1, 'ffn_w_up': 2,
       'ffn_w_down': 1}
SMALL_SHARDED = ('even_conv_w', 'pool_scale', 'sgu_norm_g', 'ffn_conv_w')
SMALL = [n for n in WEIGHTS if n not in BIG]
N_CHIPS = 4
N_DEV = 8


def _cparams(sem=None, vmem=VMEM_LIMIT):
    kw = dict(vmem_limit_bytes=vmem)
    if sem is not None:
        kw['dimension_semantics'] = sem
    return pltpu.CompilerParams(**kw)


def _pick(n, segs=(), prefs=(1024, 512, 256, 128)):
    for t in prefs:
        if n % t == 0 and all(s % t == 0 for s in segs if s):
            return t
    return n


def _largest_tile(n, segs, cap):
    best = None
    for t in range(LANES, min(n, cap) + 1, LANES):
        if n % t == 0 and all(s % t == 0 for s in segs if s):
            best = t
    return best if best is not None else n


def _ldims(arr, kind):
    if kind is None:
        return arr.shape
    if kind[0] == 'lead':
        return arr.shape[1:]
    return (arr.shape[1], arr.shape[0] * arr.shape[2])


def _segw(arr, kind):
    return arr.shape[2] if (kind is not None and kind[0] == 'seg') else None


def _opspec(arr, kind, br, bc, rfn, cfn):
    if kind is None:
        return pl.BlockSpec((br, bc), lambda i, j, k: (rfn(i, j, k), cfn(i, j, k)))
    if kind[0] == 'lead':
        lead = kind[1]
        return pl.BlockSpec((None, br, bc), lambda i, j, k: (lead, rfn(i, j, k), cfn(i, j, k)))
    per = arr.shape[2] // bc
    return pl.BlockSpec((None, br, bc), lambda i, j, k: (cfn(i, j, k) // per, rfn(i, j, k), cfn(i, j, k) % per))


def _mm(a, b, mode, out_dtype, name, ak=None, bk=None, ok=None, res=None):
    ar, ac = _ldims(a, ak)
    br_, bc_ = _ldims(b, bk)
    if mode == 'nn':
        M, K, N = ar, ac, bc_
        assert br_ == K
    else:
        M, K, N = ar, ac, br_
        assert bc_ == K
    sa, sb = _segw(a, ak), _segw(b, bk)
    so = (N // ok[1]) if ok is not None else None
    tm = _largest_tile(M, [], MM_TM_CAP)
    tn = _largest_tile(N, [sb if mode == 'nn' else None, so], MM_TN_CAP)
    ksegs = [sa, sb if mode == 'nt' else None]
    tk = K if (K <= MM_TK_WHOLE and not any(ksegs)) else _largest_tile(K, ksegs, MM_TK_CAP)
    nk = K // tk
    I = lambda i, j, k: i
    J = lambda i, j, k: j
    Kk = lambda i, j, k: k
    a_spec = _opspec(a, ak, tm, tk, I, Kk)
    if mode == 'nn':
        b_spec = _opspec(b, bk, tk, tn, Kk, J)
        dims = (((1,), (0,)), ((), ()))
    else:
        b_spec = _opspec(b, bk, tn, tk, J, Kk)
        dims = (((1,), (1,)), ((), ()))
    if ok is None:
        out_shape = jax.ShapeDtypeStruct((M, N), out_dtype)
        o_spec = pl.BlockSpec((tm, tn), lambda i, j, k: (i, j))
    else:
        out_shape = jax.ShapeDtypeStruct((ok[1], M, N // ok[1]), out_dtype)
        per = (N // ok[1]) // tn
        o_spec = pl.BlockSpec((None, tm, tn), lambda i, j, k: (j // per, i, j % per))
    has_res = res is not None

    def body(*refs):
        a_ref, b_ref = refs[0], refs[1]
        r_ref = refs[2] if has_res else None
        o_ref = refs[3] if has_res else refs[2]
        prod = lax.dot_general(a_ref[...].astype(BF16), b_ref[...].astype(BF16), dims, preferred_element_type=F32)
        if nk == 1:
            o_ref[...] = (prod + r_ref[...] if has_res else prod).astype(out_dtype)
            return
        acc = refs[-1]
        k = pl.program_id(2)

        @pl.when(k == 0)
        def _():
            acc[...] = prod

        @pl.when(k > 0)
        def _():
            acc[...] += prod

        @pl.when(k == nk - 1)
        def _():
            o = acc[...]
            if has_res:
                o = o + r_ref[...]
            o_ref[...] = o.astype(out_dtype)

    in_specs = [a_spec, b_spec]
    args = [a, b]
    if has_res:
        in_specs.append(pl.BlockSpec((tm, tn), lambda i, j, k: (i, j)))
        args.append(res)
    return pl.pallas_call(
        body, name=name, out_shape=out_shape, grid=(M // tm, N // tn, nk), in_specs=in_specs, out_specs=o_spec,
        scratch_shapes=[pltpu.VMEM((tm, tn), F32)] if nk > 1 else [],
        compiler_params=_cparams(("parallel", "parallel", "arbitrary")),
    )(*args)


_G0 = math.sqrt(2.0 / math.pi)
_G1 = 0.044715


def _gelu(x):
    return 0.5 * x * (1.0 + jnp.tanh(_G0 * (x + _G1 * x * x * x)))


def _gelu_grad(x):
    x2 = x * x
    t = jnp.tanh(_G0 * (x + _G1 * x * x2))
    return 0.5 * (1.0 + t) + 0.5 * x * (1.0 - t * t) * (_G0 * (1.0 + 3.0 * _G1 * x2))


def _sigmoid(x):
    return 1.0 / (1.0 + jnp.exp(-x))


def _down(v, k):
    row = lax.broadcasted_iota(jnp.int32, v.shape, 0)
    return jnp.where(row >= k, pltpu.roll(v, k, axis=0), 0.0)


def _up(v, k):
    n = v.shape[0]
    row = lax.broadcasted_iota(jnp.int32, v.shape, 0)
    return jnp.where(row < n - k, pltpu.roll(v, n - k, axis=0), 0.0)


def _conv3(v, w):
    return w[0:1, :] * _down(v, 2) + w[1:2, :] * _down(v, 1) + w[2:3, :] * v


def _conv3_t(dv, w):
    return w[2:3, :] * dv + w[1:2, :] * _up(dv, 1) + w[0:1, :] * _up(dv, 2)


def _conv3_dw(dv, v):
    return (jnp.sum(dv * _down(v, 2), axis=0, keepdims=True),
            jnp.sum(dv * _down(v, 1), axis=0, keepdims=True),
            jnp.sum(dv * v, axis=0, keepdims=True))


def _cmul(ar, ai, br, bi):
    return ar * br - ai * bi, ar * bi + ai * br


def _cpow(lr, li, n):
    rr = ri = None
    br, bi = lr, li
    while n:
        if n & 1:
            rr, ri = (br, bi) if rr is None else _cmul(rr, ri, br, bi)
        n >>= 1
        if n:
            br, bi = _cmul(br, bi, br, bi)
    return rr, ri


def _rms_fwd(x, g, name):
    L, D = x.shape
    tr = _pick(L, prefs=(512, 256, 128))

    def body(x_ref, g_ref, h_ref):
        xv = x_ref[...]
        r = lax.rsqrt(jnp.mean(xv * xv, axis=-1, keepdims=True) + EPS)
        h_ref[...] = (xv * r * g_ref[...]).astype(BF16)

    return pl.pallas_call(
        body, name=name, out_shape=jax.ShapeDtypeStruct((L, D), BF16), grid=(L // tr,),
        in_specs=[pl.BlockSpec((tr, D), lambda i: (i, 0)), pl.BlockSpec((1, D), lambda i: (0, 0))],
        out_specs=pl.BlockSpec((tr, D), lambda i: (i, 0)), compiler_params=_cparams(("parallel",)),
    )(x, g.reshape(1, D))


def _rms_bwd(x, g, dh, dres, name):
    L, D = x.shape
    tr = _pick(L, prefs=(512, 256, 128))
    nsteps = L // tr

    def body(x_ref, g_ref, dh_ref, dres_ref, dx_ref, dg_ref, acc):
        i = pl.program_id(0)

        @pl.when(i == 0)
        def _():
            acc[...] = jnp.zeros_like(acc)

        xv = x_ref[...]
        r = lax.rsqrt(jnp.mean(xv * xv, axis=-1, keepdims=True) + EPS)
        xh = xv * r
        dhv = dh_ref[...].astype(F32)
        acc[...] += jnp.sum((dhv * xh).reshape(tr // SUBLANES, SUBLANES, D), axis=0)
        dxh = dhv * g_ref[...]
        dx_ref[...] = dres_ref[...] + r * (dxh - xh * jnp.mean(dxh * xh, axis=-1, keepdims=True))

        @pl.when(i == nsteps - 1)
        def _():
            dg_ref[...] = jnp.sum(acc[...], axis=0, keepdims=True)

    row = pl.BlockSpec((tr, D), lambda i: (i, 0))
    vec = pl.BlockSpec((1, D), lambda i: (0, 0))
    return pl.pallas_call(
        body, name=name, out_shape=(jax.ShapeDtypeStruct((L, D), F32), jax.ShapeDtypeStruct((1, D), F32)),
        grid=(nsteps,), in_specs=[row, vec, row, row], out_specs=(row, vec),
        scratch_shapes=[pltpu.VMEM((SUBLANES, D), F32)], compiler_params=_cparams(("arbitrary",)),
    )(x, g.reshape(1, D), dh, dres)


def _loss_head(x, g, tgt):
    L, D = x.shape
    tr = _pick(L, prefs=(512, 256, 128))
    nsteps = L // tr

    def body(x_ref, g_ref, t_ref, loss_ref, dx_ref, dg_ref, acc_g, acc_l):
        i = pl.program_id(0)

        @pl.when(i == 0)
        def _():
            acc_g[...] = jnp.zeros_like(acc_g)
            acc_l[...] = jnp.zeros_like(acc_l)

        xv = x_ref[...]
        gv = g_ref[...]
        r = lax.rsqrt(jnp.mean(xv * xv, axis=-1, keepdims=True) + EPS)
        xh = xv * r
        e = xh * gv - t_ref[...]
        acc_l[...] += jnp.sum((e * e).reshape(tr // SUBLANES, SUBLANES, D), axis=0)
        dy = e * (1.0 / D)
        acc_g[...] += jnp.sum((dy * xh).reshape(tr // SUBLANES, SUBLANES, D), axis=0)
        dxh = dy * gv
        dx_ref[...] = r * (dxh - xh * jnp.mean(dxh * xh, axis=-1, keepdims=True))

        @pl.when(i == nsteps - 1)
        def _():
            dg_ref[...] = jnp.sum(acc_g[...], axis=0, keepdims=True)
            tot = jnp.sum(jnp.sum(acc_l[...], axis=0, keepdims=True), axis=1, keepdims=True) * (0.5 / D)
            loss_ref[...] = jnp.broadcast_to(tot, (SUBLANES, LANES))

    row = pl.BlockSpec((tr, D), lambda i: (i, 0))
    vec = pl.BlockSpec((1, D), lambda i: (0, 0))
    return pl.pallas_call(
        body, name="loss_head",
        out_shape=(jax.ShapeDtypeStruct((SUBLANES, LANES), F32), jax.ShapeDtypeStruct((L, D), F32),
                   jax.ShapeDtypeStruct((1, D), F32)),
        grid=(nsteps,), in_specs=[row, vec, row],
        out_specs=(pl.BlockSpec((SUBLANES, LANES), lambda i: (0, 0)), row, vec),
        scratch_shapes=[pltpu.VMEM((SUBLANES, D), F32), pltpu.VMEM((SUBLANES, D), F32)],
        compiler_params=_cparams(("arbitrary",)),
    )(x, g.reshape(1, D), tgt)


def _sconv_fwd(proj4, conv_w, name):
    _, L, C = proj4.shape
    cb = LANES

    def body(p_ref, w_ref, o_ref):
        xa, ba, ca = p_ref[0], p_ref[1], p_ref[2]
        o_ref[...] = (ba * _conv3(ca * xa, w_ref[...])).astype(BF16)

    return pl.pallas_call(
        body, name=name, out_shape=jax.ShapeDtypeStruct((L, C), BF16), grid=(C // cb,),
        in_specs=[pl.BlockSpec((3, L, cb), lambda j: (0, 0, j)), pl.BlockSpec((3, cb), lambda j: (0, j))],
        out_specs=pl.BlockSpec((L, cb), lambda j: (0, j)), compiler_params=_cparams(("parallel",)),
    )(proj4, conv_w)


def _sconv_bwd(proj4, dmix, conv_w, name):
    _, L, C = proj4.shape
    cb = LANES

    def body(p_ref, d_ref, w_ref, o_ref, dw_ref):
        xa, ba, ca = p_ref[0], p_ref[1], p_ref[2]
        w = w_ref[...]
        dya = d_ref[...]
        q = ca * xa
        cq = _conv3(q, w)
        dcq = dya * ba
        dq = _conv3_t(dcq, w)
        for tap, dwt in enumerate(_conv3_dw(dcq, q)):
            dw_ref[tap:tap + 1, :] = dwt
        o_ref[0] = (dq * ca).astype(BF16)
        o_ref[1] = (dya * cq).astype(BF16)
        o_ref[2] = (dq * xa).astype(BF16)

    return pl.pallas_call(
        body, name=name,
        out_shape=(jax.ShapeDtypeStruct((3, L, C), BF16), jax.ShapeDtypeStruct((3, C), F32)), grid=(C // cb,),
        in_specs=[pl.BlockSpec((3, L, cb), lambda j: (0, 0, j)), pl.BlockSpec((L, cb), lambda j: (0, j)),
                  pl.BlockSpec((3, cb), lambda j: (0, j))],
        out_specs=(pl.BlockSpec((3, L, cb), lambda j: (0, 0, j)), pl.BlockSpec((3, cb), lambda j: (0, j))),
        compiler_params=_cparams(("parallel",)),
    )(proj4, dmix, conv_w)


def _to_scan_order(v):
    L, C = v.shape
    return v.reshape(SCAN_CHUNKS, L // SCAN_CHUNKS, C).transpose(1, 0, 2).reshape(L, C)


def _from_scan_order(v):
    L, C = v.shape
    return v.reshape(L // SCAN_CHUNKS, SCAN_CHUNKS, C).transpose(1, 0, 2).reshape(L, C)


def _s5_prep(log_step, a_re, a_im, b_re, b_im, c_re, c_im):
    G, P = a_re.shape
    H = b_re.shape[-1]
    gs = S5_GROUPS_PER_STEP
    ns = G // gs
    gu = LANES // H
    lam = lax.complex(a_re, a_im)
    step = jnp.exp(log_step)[:, None]
    lam_bar = jnp.exp(lam * step)
    b_bar = ((lam_bar - 1.0) / lam)[..., None] * lax.complex(b_re, b_im)
    lr = jnp.real(lam_bar).reshape(ns, 1, gs * P)
    li = jnp.imag(lam_bar).reshape(ns, 1, gs * P)
    k = np.arange(ns)[:, None, None]
    oh = jnp.asarray((np.arange(gu)[None, :, None] == gs * (k % (gu // gs)) + np.arange(gs)[None, None, :]),
                     F32)
    bre = jnp.einsum('kgl,klph->kghlp', oh, jnp.real(b_bar).reshape(ns, gs, P, H)).reshape(ns, gu * H, gs * P)
    bim = jnp.einsum('kgl,klph->kghlp', oh, jnp.imag(b_bar).reshape(ns, gs, P, H)).reshape(ns, gu * H, gs * P)
    cre = jnp.einsum('kgl,klhp->klpgh', oh, c_re.reshape(ns, gs, H, P)).reshape(ns, gs * P, gu * H)
    cim = jnp.einsum('kgl,klhp->klpgh', oh, c_im.reshape(ns, gs, H, P)).reshape(ns, gs * P, gu * H)
    return lr, li, jnp.concatenate([bre, bim], axis=2), jnp.concatenate([cre, -cim], axis=1)


def _carry_tile(fr, fi, pr, pi, reverse):
    row = lax.broadcasted_iota(jnp.int32, fr.shape, 0)
    cr = jnp.zeros_like(fr)
    ci = jnp.zeros_like(fi)
    sr = jnp.zeros_like(fr[0:1])
    si = jnp.zeros_like(sr)
    order = range(SCAN_CHUNKS - 1, 0, -1) if reverse else range(0, SCAN_CHUNKS - 1)
    for c in order:
        fcr = jnp.sum(jnp.where(row == c, fr, 0.0), axis=0, keepdims=True)
        fci = jnp.sum(jnp.where(row == c, fi, 0.0), axis=0, keepdims=True)
        mr, mi = _cmul(pr, pi, sr, si)
        sr, si = mr + fcr, mi + fci
        nxt = c - 1 if reverse else c + 1
        cr = jnp.where(row == nxt, sr, cr)
        ci = jnp.where(row == nxt, si, ci)
    return cr, ci


def _s5_fwd(u, lr, li, bmat, cmat, d, name):
    L, Du = u.shape
    ns, _, W2 = bmat.shape
    W = W2 // 2
    T = L // SCAN_CHUNKS
    rb = _pick(L, prefs=(512, 256, 128))
    per = (ns * LANES) // Du

    def body(u_ref, lr_ref, li_ref, b_ref, c_ref, d_ref, y_ref, sr_ref, si_ref):
        k = pl.program_id(0)
        for r in range(L // rb):
            rows = pl.ds(r * rb, rb)
            bu = jnp.dot(u_ref[rows, :].astype(BF16), b_ref[...], preferred_element_type=F32)
            sr_ref[rows, :] = bu[:, :W]
            si_ref[rows, :] = bu[:, W:]
        lam_r = jnp.broadcast_to(lr_ref[...], (SUBLANES, W))
        lam_i = jnp.broadcast_to(li_ref[...], (SUBLANES, W))

        def local(t, carry):
            sr, si = carry
            rows = pl.ds(pl.multiple_of(t * SUBLANES, SUBLANES), SUBLANES)
            mr, mi = _cmul(lam_r, lam_i, sr, si)
            sr = mr + sr_ref[rows, :]
            si = mi + si_ref[rows, :]
            sr_ref[rows, :] = sr
            si_ref[rows, :] = si
            return sr, si

        z = jnp.zeros((SUBLANES, W), F32)
        fr, fi = lax.fori_loop(0, T, local, (z, z))
        pr, pi = _cpow(lam_r, lam_i, T)
        cr, ci = _carry_tile(fr, fi, pr[0:1], pi[0:1], reverse=False)

        def fix(t, carry):
            wr, wi = carry
            rows = pl.ds(pl.multiple_of(t * SUBLANES, SUBLANES), SUBLANES)
            ar, ai = _cmul(wr, wi, cr, ci)
            sr_ref[rows, :] += ar
            si_ref[rows, :] += ai
            return _cmul(wr, wi, lam_r, lam_i)

        lax.fori_loop(0, T, fix, (lam_r, lam_i))
        first = (k % per) == 0
        for r in range(L // rb):
            rows = pl.ds(r * rb, rb)
            s = jnp.concatenate([sr_ref[rows, :], si_ref[rows, :]], axis=1).astype(BF16)
            y = jnp.dot(s, c_ref[...], preferred_element_type=F32)

            @pl.when(first)
            def _():
                y_ref[rows, :] = y + d_ref[...] * u_ref[rows, :]

            @pl.when(jnp.logical_not(first))
            def _():
                y_ref[rows, :] += y

    ublk = pl.BlockSpec((L, LANES), lambda k: (0, k // per))
    sblk = pl.BlockSpec((L, W), lambda k: (0, k))
    lam = pl.BlockSpec((None, 1, W), lambda k: (k, 0, 0))
    return pl.pallas_call(
        body, name=name,
        out_shape=(jax.ShapeDtypeStruct((L, Du), F32), jax.ShapeDtypeStruct((L, ns * W), F32),
                   jax.ShapeDtypeStruct((L, ns * W), F32)),
        grid=(ns,),
        in_specs=[ublk, lam, lam, pl.BlockSpec((None, LANES, 2 * W), lambda k: (k, 0, 0)),
                  pl.BlockSpec((None, 2 * W, LANES), lambda k: (k, 0, 0)),
                  pl.BlockSpec((1, LANES), lambda k: (0, k // per))],
        out_specs=(ublk, sblk, sblk), compiler_params=_cparams(("arbitrary",), VMEM_LIMIT_S5),
    )(u, lr, li, bmat.astype(BF16), cmat.astype(BF16), d.reshape(1, Du))


def _s5_bwd(dy, u, s_re, s_im, lr, li, bmat, cmat, d, name):
    L, Du = u.shape
    ns, _, W2 = bmat.shape
    W = W2 // 2
    T = L // SCAN_CHUNKS
    rb = _pick(L, prefs=(512, 256, 128))
    per = (ns * LANES) // Du
    NT = (((1,), (1,)), ((), ()))
    TN = (((0,), (0,)), ((), ()))

    def body(dy_ref, u_ref, sr_ref, si_ref, lr_ref, li_ref, b_ref, c_ref, d_ref,
             du_ref, db_ref, dc_ref, dl_ref, dd_ref, gr_ref, gi_ref):
        k = pl.program_id(0)
        for r in range(L // rb):
            rows = pl.ds(r * rb, rb)
            g = lax.dot_general(dy_ref[rows, :].astype(BF16), c_ref[...], NT, preferred_element_type=F32)
            gr_ref[rows, :] = g[:, :W]
            gi_ref[rows, :] = g[:, W:]
        lam_r = jnp.broadcast_to(lr_ref[...], (SUBLANES, W))
        lam_i = -jnp.broadcast_to(li_ref[...], (SUBLANES, W))

        def local(i, carry):
            gr, gi = carry
            rows = pl.ds(pl.multiple_of((T - 1 - i) * SUBLANES, SUBLANES), SUBLANES)
            mr, mi = _cmul(lam_r, lam_i, gr, gi)
            gr = mr + gr_ref[rows, :]
            gi = mi + gi_ref[rows, :]
            gr_ref[rows, :] = gr
            gi_ref[rows, :] = gi
            return gr, gi

        z = jnp.zeros((SUBLANES, W), F32)
        fr, fi = lax.fori_loop(0, T, local, (z, z))
        pr, pi = _cpow(lam_r, lam_i, T)
        cr, ci = _carry_tile(fr, fi, pr[0:1], pi[0:1], reverse=True)

        def true_g(rows, wr, wi):
            ar, ai = _cmul(wr, wi, cr, ci)
            gr = gr_ref[rows, :] + ar
            gi = gi_ref[rows, :] + ai
            gr_ref[rows, :] = gr
            gi_ref[rows, :] = gi
            return gr, gi

        def fix(i, carry):
            wr, wi, ar_, ai_ = carry
            t = T - 1 - i
            rows = pl.ds(pl.multiple_of(t * SUBLANES, SUBLANES), SUBLANES)
            prev = pl.ds(pl.multiple_of((t - 1) * SUBLANES, SUBLANES), SUBLANES)
            gr, gi = true_g(rows, wr, wi)
            qr, qi = sr_ref[prev, :], si_ref[prev, :]
            ar_ = ar_ + gr * qr + gi * qi
            ai_ = ai_ + gi * qr - gr * qi
            wr, wi = _cmul(wr, wi, lam_r, lam_i)
            return wr, wi, ar_, ai_

        wr, wi, acc_r, acc_i = lax.fori_loop(0, T - 1, fix, (lam_r, lam_i, z, z))
        gr, gi = true_g(pl.ds(0, SUBLANES), wr, wi)
        last = pl.ds((T - 1) * SUBLANES, SUBLANES)
        row = lax.broadcasted_iota(jnp.int32, (SUBLANES, W), 0)
        qr = jnp.where(row >= 1, pltpu.roll(sr_ref[last, :], 1, axis=0), 0.0)
        qi = jnp.where(row >= 1, pltpu.roll(si_ref[last, :], 1, axis=0), 0.0)
        acc_r = acc_r + gr * qr + gi * qi
        acc_i = acc_i + gi * qr - gr * qi
        dl_ref[0:1, :] = jnp.sum(acc_r, axis=0, keepdims=True)
        dl_ref[1:2, :] = jnp.sum(acc_i, axis=0, keepdims=True)

        first = (k % per) == 0
        db = jnp.zeros((LANES, 2 * W), F32)
        dc = jnp.zeros((LANES, 2 * W), F32)
        dd = jnp.zeros((1, LANES), F32)
        for r in range(L // rb):
            rows = pl.ds(r * rb, rb)
            gb = jnp.concatenate([gr_ref[rows, :], gi_ref[rows, :]], axis=1).astype(BF16)
            sb = jnp.concatenate([sr_ref[rows, :], si_ref[rows, :]], axis=1).astype(BF16)
            dyv = dy_ref[rows, :]
            uv = u_ref[rows, :]
            du = lax.dot_general(gb, b_ref[...], NT, preferred_element_type=F32)
            db = db + lax.dot_general(uv.astype(BF16), gb, TN, preferred_element_type=F32)
            dc = dc + lax.dot_general(dyv.astype(BF16), sb, TN, preferred_element_type=F32)
            dd = dd + jnp.sum(dyv * uv, axis=0, keepdims=True)

            @pl.when(first)
            def _():
                du_ref[rows, :] = du + d_ref[...] * dyv

            @pl.when(jnp.logical_not(first))
            def _():
                du_ref[rows, :] += du

        db_ref[...] = db
        dc_ref[...] = dc

        @pl.when(first)
        def _():
            dd_ref[...] = dd

    ublk = pl.BlockSpec((L, LANES), lambda k: (0, k // per))
    sblk = pl.BlockSpec((L, W), lambda k: (0, k))
    lam = pl.BlockSpec((None, 1, W), lambda k: (k, 0, 0))
    vec = pl.BlockSpec((1, LANES), lambda k: (0, k // per))
    mat = pl.BlockSpec((None, LANES, 2 * W), lambda k: (k, 0, 0))
    return pl.pallas_call(
        body, name=name,
        out_shape=(jax.ShapeDtypeStruct((L, Du), F32), jax.ShapeDtypeStruct((ns, LANES, 2 * W), F32),
                   jax.ShapeDtypeStruct((ns, LANES, 2 * W), F32), jax.ShapeDtypeStruct((ns, 2, W), F32),
                   jax.ShapeDtypeStruct((1, Du), F32)),
        grid=(ns,),
        in_specs=[ublk, ublk, sblk, sblk, lam, lam, mat,
                  pl.BlockSpec((None, 2 * W, LANES), lambda k: (k, 0, 0)), vec],
        out_specs=(ublk, mat, mat, pl.BlockSpec((None, 2, W), lambda k: (k, 0, 0)), vec),
        scratch_shapes=[pltpu.VMEM((L, W), F32), pltpu.VMEM((L, W), F32)],
        compiler_params=_cparams(("arbitrary",), VMEM_LIMIT_S5),
    )(dy, u, s_re, s_im, lr, li, bmat.astype(BF16), cmat.astype(BF16), d.reshape(1, Du))


def _glu_fwd(yraw, wmat, bias, name):
    L, C = yraw.shape
    tr = _pick(L, prefs=(512, 256, 128))

    def body(y_ref, w_ref, b_ref, o_ref):
        yg = _gelu(y_ref[...])
        zz = jnp.dot(yg.astype(BF16), w_ref[...], preferred_element_type=F32) + b_ref[...]
        o_ref[...] = (yg * _sigmoid(zz)).astype(BF16)

    return pl.pallas_call(
        body, name=name, out_shape=jax.ShapeDtypeStruct((L, C), BF16), grid=(L // tr,),
        in_specs=[pl.BlockSpec((tr, C), lambda i: (i, 0)), pl.BlockSpec((C, C), lambda i: (0, 0)),
                  pl.BlockSpec((1, C), lambda i: (0, 0))],
        out_specs=pl.BlockSpec((tr, C), lambda i: (i, 0)), compiler_params=_cparams(("parallel",)),
    )(yraw, wmat, bias.reshape(1, C))


def _glu_bwd(yraw, dyb, wmat, bias, name):
    L, C = yraw.shape
    tr = _pick(L, prefs=(512, 256, 128))
    nsteps = L // tr

    def body(y_ref, d_ref, w_ref, b_ref, dy_ref, dw_ref, db_ref, acc_b):
        i = pl.program_id(0)

        @pl.when(i == 0)
        def _():
            dw_ref[...] = jnp.zeros_like(dw_ref)
            acc_b[...] = jnp.zeros_like(acc_b)

        yr = y_ref[...]
        yg = _gelu(yr)
        ygb = yg.astype(BF16)
        sg = _sigmoid(jnp.dot(ygb, w_ref[...], preferred_element_type=F32) + b_ref[...])
        dyb_ = d_ref[...]
        dz = dyb_ * yg * sg * (1.0 - sg)
        dzb = dz.astype(BF16)
        dyg = dyb_ * sg + lax.dot_general(dzb, w_ref[...], (((1,), (1,)), ((), ())), preferred_element_type=F32)
        dw_ref[...] += lax.dot_general(ygb, dzb, (((0,), (0,)), ((), ())), preferred_element_type=F32)
        acc_b[...] += jnp.sum(dz.reshape(tr // SUBLANES, SUBLANES, C), axis=0)
        dy_ref[...] = dyg * _gelu_grad(yr)

        @pl.when(i == nsteps - 1)
        def _():
            db_ref[...] = jnp.sum(acc_b[...], axis=0, keepdims=True)

    row = pl.BlockSpec((tr, C), lambda i: (i, 0))
    return pl.pallas_call(
        body, name=name,
        out_shape=(jax.ShapeDtypeStruct((L, C), F32), jax.ShapeDtypeStruct((C, C), F32),
                   jax.ShapeDtypeStruct((1, C), F32)),
        grid=(nsteps,),
        in_specs=[row, row, pl.BlockSpec((C, C), lambda i: (0, 0)), pl.BlockSpec((1, C), lambda i: (0, 0))],
        out_specs=(row, pl.BlockSpec((C, C), lambda i: (0, 0)), pl.BlockSpec((1, C), lambda i: (0, 0))),
        scratch_shapes=[pltpu.VMEM((SUBLANES, C), F32)], compiler_params=_cparams(("arbitrary",)),
    )(yraw, dyb, wmat, bias.reshape(1, C))


def _pool_counts(L, g):
    t = lax.broadcasted_iota(jnp.int32, (L, LANES), 0).astype(F32) + 1.0
    w = jnp.where(g == 0, 2.0, jnp.where(g == 1, 4.0, jnp.where(g == 2, 8.0, 16.0)))
    return 1.0 / jnp.minimum(t, w)


def _select_window(g, a2, a4, a8, a16):
    return jnp.where(g == 0, a2, jnp.where(g == 1, a4, jnp.where(g == 2, a8, a16)))


def _pooled(z, g):
    a2 = z + _down(z, 1)
    a4 = a2 + _down(a2, 2)
    a8 = a4 + _down(a4, 4)
    a16 = a8 + _down(a8, 8)
    return _select_window(g, a2, a4, a8, a16) * _pool_counts(z.shape[0], g) - z


def _pool_fwd(proj3, pool_w, scale, name):
    _, L, C = proj3.shape
    ng = len(POOL_WINDOWS)
    pg = C // ng
    assert pg == LANES

    def body(z_ref, w_ref, s_ref, o_ref):
        g = pl.program_id(0)
        p = _pooled(z_ref[...], g)
        y = jnp.dot(p.astype(BF16), w_ref[...].astype(BF16), preferred_element_type=F32)
        o_ref[...] = (y * s_ref[...]).astype(BF16)

    return pl.pallas_call(
        body, name=name, out_shape=jax.ShapeDtypeStruct((L, C), BF16), grid=(ng,),
        in_specs=[pl.BlockSpec((None, L, pg), lambda g: (0, 0, g)), pl.BlockSpec((None, pg, pg), lambda g: (g, 0, 0)),
                  pl.BlockSpec((1, pg), lambda g: (0, g))],
        out_specs=pl.BlockSpec((L, pg), lambda g: (0, g)), compiler_params=_cparams(("parallel",)),
    )(proj3, pool_w, scale.reshape(1, C))


def _pool_bwd(proj3, dmix, pool_w, scale, name):
    _, L, C = proj3.shape
    ng = len(POOL_WINDOWS)
    pg = C // ng

    def body(z_ref, d_ref, w_ref, s_ref, dz_ref, dw_ref, ds_ref):
        g = pl.program_id(0)
        p = _pooled(z_ref[...], g)
        pb = p.astype(BF16)
        wb = w_ref[...].astype(BF16)
        pre = jnp.dot(pb, wb, preferred_element_type=F32)
        dyc = d_ref[...]
        ds_ref[...] = jnp.sum(dyc * pre, axis=0, keepdims=True)
        dpre = (dyc * s_ref[...]).astype(BF16)
        dw_ref[...] = lax.dot_general(pb, dpre, (((0,), (0,)), ((), ())), preferred_element_type=F32)
        dp = lax.dot_general(dpre, wb, (((1,), (1,)), ((), ())), preferred_element_type=F32)
        v = dp * _pool_counts(L, g)
        a2 = v + _up(v, 1)
        a4 = a2 + _up(a2, 2)
        a8 = a4 + _up(a4, 4)
        a16 = a8 + _up(a8, 8)
        dz_ref[...] = (_select_window(g, a2, a4, a8, a16) - dp).astype(BF16)

    return pl.pallas_call(
        body, name=name,
        out_shape=(jax.ShapeDtypeStruct((L, C), BF16), jax.ShapeDtypeStruct((ng, pg, pg), F32),
                   jax.ShapeDtypeStruct((1, C), F32)),
        grid=(ng,),
        in_specs=[pl.BlockSpec((None, L, pg), lambda g: (0, 0, g)), pl.BlockSpec((L, pg), lambda g: (0, g)),
                  pl.BlockSpec((None, pg, pg), lambda g: (g, 0, 0)), pl.BlockSpec((1, pg), lambda g: (0, g))],
        out_specs=(pl.BlockSpec((L, pg), lambda g: (0, g)), pl.BlockSpec((None, pg, pg), lambda g: (g, 0, 0)),
                   pl.BlockSpec((1, pg), lambda g: (0, g))),
        compiler_params=_cparams(("parallel",)),
    )(proj3, dmix, pool_w, scale.reshape(1, C))


def _tril_w(w_ref, h):
    r = lax.broadcasted_iota(jnp.int32, (CHUNK, CHUNK), 0)
    c = lax.broadcasted_iota(jnp.int32, (CHUNK, CHUNK), 1)
    return jnp.where(r >= c, w_ref[h], 0.0)


def _sgu_fwd(proj3, norm_g, w, b, name):
    _, L, C = proj3.shape
    nh = w.shape[0]
    dh = C // nh
    assert dh == LANES and w.shape[1] == CHUNK
    tr = _pick(L, prefs=(512, 256, 128))
    bfull = jnp.broadcast_to(b[:, :, None], (nh, CHUNK, dh))

    def body(su_ref, sv_ref, g_ref, w_ref, b_ref, o_ref):
        sv = _gelu(sv_ref[...])
        r = lax.rsqrt(jnp.mean(sv * sv, axis=-1, keepdims=True) + EPS)
        v = (sv * r * g_ref[...]).astype(BF16)
        for h in range(nh):
            wm = _tril_w(w_ref, h).astype(BF16)
            cols = slice(h * dh, (h + 1) * dh)
            for n in range(tr // CHUNK):
                rows = slice(n * CHUNK, (n + 1) * CHUNK)
                mixed = jnp.dot(wm, v[rows, cols], preferred_element_type=F32) + b_ref[h]
                o_ref[rows, cols] = (_gelu(su_ref[rows, cols]) * mixed).astype(BF16)

    full = lambda shp: pl.BlockSpec(shp, lambda i: (0,) * len(shp))
    return pl.pallas_call(
        body, name=name, out_shape=jax.ShapeDtypeStruct((L, C), BF16), grid=(L // tr,),
        in_specs=[pl.BlockSpec((None, tr, C), lambda i: (1, i, 0)), pl.BlockSpec((None, tr, C), lambda i: (2, i, 0)),
                  full((1, C)), full((nh, CHUNK, CHUNK)), full((nh, CHUNK, dh))],
        out_specs=pl.BlockSpec((tr, C), lambda i: (i, 0)), compiler_params=_cparams(("parallel",)),
    )(proj3, proj3, norm_g.reshape(1, C), w, bfull)


def _sgu_bwd(proj3, dmix, norm_g, w, b, name):
    _, L, C = proj3.shape
    nh = w.shape[0]
    dh = C // nh
    tr = _pick(L, prefs=(512, 256, 128))
    nsteps = L // tr
    bfull = jnp.broadcast_to(b[:, :, None], (nh, CHUNK, dh))

    def body(su_ref, sv_ref, d_ref, g_ref, w_ref, b_ref, o_ref, dw_ref, db_ref, dg_ref, dv_ref, acc_g):
        i = pl.program_id(0)

        @pl.when(i == 0)
        def _():
            dw_ref[...] = jnp.zeros_like(dw_ref)
            db_ref[...] = jnp.zeros_like(db_ref)
            acc_g[...] = jnp.zeros_like(acc_g)

        svp = sv_ref[...]
        sv = _gelu(svp)
        r = lax.rsqrt(jnp.mean(sv * sv, axis=-1, keepdims=True) + EPS)
        vh = sv * r
        gv = g_ref[...]
        v = (vh * gv).astype(BF16)
        tri_r = lax.broadcasted_iota(jnp.int32, (CHUNK, CHUNK), 0)
        tri_c = lax.broadcasted_iota(jnp.int32, (CHUNK, CHUNK), 1)
        for h in range(nh):
            wm = _tril_w(w_ref, h).astype(BF16)
            cols = slice(h * dh, (h + 1) * dh)
            dwh = jnp.zeros((CHUNK, CHUNK), F32)
            dbh = jnp.zeros((CHUNK, dh), F32)
            for n in range(tr // CHUNK):
                rows = slice(n * CHUNK, (n + 1) * CHUNK)
                vb = v[rows, cols]
                mixed = jnp.dot(wm, vb, preferred_element_type=F32) + b_ref[h]
                sup = su_ref[rows, cols]
                dyd = d_ref[rows, cols]
                dmx = dyd * _gelu(sup)
                o_ref[0, rows, cols] = (dyd * mixed * _gelu_grad(sup)).astype(BF16)
                dmb = dmx.astype(BF16)
                dwh = dwh + lax.dot_general(dmb, vb, (((1,), (1,)), ((), ())), preferred_element_type=F32)
                dbh = dbh + dmx
                dv_ref[rows, cols] = lax.dot_general(wm, dmb, (((0,), (0,)), ((), ())), preferred_element_type=F32)
            dw_ref[h] += jnp.where(tri_r >= tri_c, dwh, 0.0)
            db_ref[h] += dbh
        dv = dv_ref[...]
        acc_g[...] += jnp.sum((dv * vh).reshape(tr // SUBLANES, SUBLANES, C), axis=0)
        dvg = dv * gv
        dsv = r * (dvg - vh * jnp.mean(dvg * vh, axis=-1, keepdims=True))
        o_ref[1] = (dsv * _gelu_grad(svp)).astype(BF16)

        @pl.when(i == nsteps - 1)
        def _():
            dg_ref[...] = jnp.sum(acc_g[...], axis=0, keepdims=True)

    full = lambda shp: pl.BlockSpec(shp, lambda i: (0,) * len(shp))
    return pl.pallas_call(
        body, name=name,
        out_shape=(jax.ShapeDtypeStruct((2, L, C), BF16), jax.ShapeDtypeStruct((nh, CHUNK, CHUNK), F32),
                   jax.ShapeDtypeStruct((nh, CHUNK, dh), F32), jax.ShapeDtypeStruct((1, C), F32)),
        grid=(nsteps,),
        in_specs=[pl.BlockSpec((None, tr, C), lambda i: (1, i, 0)), pl.BlockSpec((None, tr, C), lambda i: (2, i, 0)),
                  pl.BlockSpec((tr, C), lambda i: (i, 1)), full((1, C)), full((nh, CHUNK, CHUNK)),
                  full((nh, CHUNK, dh))],
        out_specs=(pl.BlockSpec((2, tr, C), lambda i: (0, i, 0)), full((nh, CHUNK, CHUNK)), full((nh, CHUNK, dh)),
                   full((1, C))),
        scratch_shapes=[pltpu.VMEM((tr, C), F32), pltpu.VMEM((SUBLANES, C), F32)],
        compiler_params=_cparams(("arbitrary",)),
    )(proj3, proj3, dmix, norm_g.reshape(1, C), w, bfull)


def _ffn_act_fwd(up3, conv_w, conv_b, name):
    _, L, Fh = up3.shape
    cb = LANES
    w2 = conv_w.reshape(3, 2, Fh).transpose(1, 0, 2)
    b2 = conv_b.reshape(2, 1, Fh)

    def body(u_ref, w_ref, b_ref, o_ref):
        g = _conv3(u_ref[0].astype(F32), w_ref[0]) + b_ref[0]
        v = _conv3(u_ref[1].astype(F32), w_ref[1]) + b_ref[1]
        o_ref[...] = (g * _sigmoid(g) * v).astype(BF16)

    return pl.pallas_call(
        body, name=name, out_shape=jax.ShapeDtypeStruct((L, Fh), BF16), grid=(Fh // cb,),
        in_specs=[pl.BlockSpec((2, L, cb), lambda j: (0, 0, j)), pl.BlockSpec((2, 3, cb), lambda j: (0, 0, j)),
                  pl.BlockSpec((2, 1, cb), lambda j: (0, 0, j))],
        out_specs=pl.BlockSpec((L, cb), lambda j: (0, j)), compiler_params=_cparams(("parallel",)),
    )(up3, w2, b2)


def _ffn_act_bwd(up3, da, conv_w, conv_b, name):
    _, L, Fh = up3.shape
    cb = LANES
    w2 = conv_w.reshape(3, 2, Fh).transpose(1, 0, 2)
    b2 = conv_b.reshape(2, 1, Fh)

    def body(u_ref, d_ref, w_ref, b_ref, o_ref, dw_ref, db_ref):
        ug, uv = u_ref[0].astype(F32), u_ref[1].astype(F32)
        wg, wv = w_ref[0], w_ref[1]
        g = _conv3(ug, wg) + b_ref[0]
        v = _conv3(uv, wv) + b_ref[1]
        sg = _sigmoid(g)
        dav = d_ref[...].astype(F32)
        dg = dav * v * (sg * (1.0 + g * (1.0 - sg)))
        dv = dav * (g * sg)
        o_ref[0] = _conv3_t(dg, wg).astype(BF16)
        o_ref[1] = _conv3_t(dv, wv).astype(BF16)
        for tap, (dwg, dwv) in enumerate(zip(_conv3_dw(dg, ug), _conv3_dw(dv, uv))):
            dw_ref[0, tap:tap + 1, :] = dwg
            dw_ref[1, tap:tap + 1, :] = dwv
        db_ref[0] = jnp.sum(dg, axis=0, keepdims=True)
        db_ref[1] = jnp.sum(dv, axis=0, keepdims=True)

    dup, dw2, db2 = pl.pallas_call(
        body, name=name,
        out_shape=(jax.ShapeDtypeStruct((2, L, Fh), BF16), jax.ShapeDtypeStruct((2, 3, Fh), F32),
                   jax.ShapeDtypeStruct((2, 1, Fh), F32)),
        grid=(Fh // cb,),
        in_specs=[pl.BlockSpec((2, L, cb), lambda j: (0, 0, j)), pl.BlockSpec((L, cb), lambda j: (0, j)),
                  pl.BlockSpec((2, 3, cb), lambda j: (0, 0, j)), pl.BlockSpec((2, 1, cb), lambda j: (0, 0, j))],
        out_specs=(pl.BlockSpec((2, L, cb), lambda j: (0, 0, j)), pl.BlockSpec((2, 3, cb), lambda j: (0, 0, j)),
                   pl.BlockSpec((2, 1, cb), lambda j: (0, 0, j))),
        compiler_params=_cparams(("parallel",)),
    )(up3, da, w2, b2)
    return dup, dw2.transpose(1, 0, 2).reshape(3, 2 * Fh), db2.reshape(2 * Fh)


def _local_step(x, tgt, w, layer_weights, on_layer_grads):
    L, D = x.shape
    depth = w['norm_mix_g'].shape[0]
    saved = []
    for i in range(depth):
        j = i // 2
        wb = layer_weights(i, x)
        s = {'x': x, 'wb': wb}
        h = _rms_fwd(x, w['norm_mix_g'][i], "mix_norm_fwd")
        s['h'] = h
        if i % 2 == 0:
            proj4 = _mm(h, wb['even_w_in'], 'nn', F32, "even_in_fwd", ok=('seg', 4))
            s['proj'] = proj4
            ya = _sconv_fwd(proj4, w['even_conv_w'][j], "sconv_fwd")
            prm = (w['ssm_log_step'][j], w['ssm_a_re'][j], w['ssm_a_im'][j], w['ssm_b_re'][j], w['ssm_b_im'][j],
                   w['ssm_c_re'][j], w['ssm_c_im'][j])
            (lr, li, bmat, cmat), prep_vjp = jax.vjp(_s5_prep, *prm)
            u = _to_scan_order(proj4[3])
            yraw, s_re, s_im = _s5_fwd(u, lr, li, bmat, cmat, w['ssm_d'][j], "s5_fwd")
            yb = _glu_fwd(yraw, wb['ssm_glu_w'], w['ssm_glu_b'][j], "glu_fwd")
            s.update(u=u, yraw=yraw, s_re=s_re, s_im=s_im, s5=(lr, li, bmat, cmat), prep_vjp=prep_vjp)
            mixin = jnp.concatenate([ya, _from_scan_order(yb)], axis=1)
            x = _mm(mixin, wb['even_w_out'], 'nn', F32, "even_out_fwd", res=x)
        else:
            proj3 = _mm(h, wb['odd_w_in'], 'nn', F32, "odd_in_fwd", ok=('seg', 3))
            s['proj'] = proj3
            yc = _pool_fwd(proj3, w['pool_w'][j], w['pool_scale'][j], "pool_fwd")
            yd = _sgu_fwd(proj3, w['sgu_norm_g'][j], w['sgu_w'][j], w['sgu_b'][j], "sgu_fwd")
            mixin = jnp.concatenate([yc, yd], axis=1)
            x = _mm(mixin, wb['odd_w_out'], 'nn', F32, "odd_out_fwd", res=x)
        s['mixin'] = mixin
        s['x1'] = x
        h2 = _rms_fwd(x, w['norm_ffn_g'][i], "ffn_norm_fwd")
        up3 = _mm(h2, wb['ffn_w_up'], 'nn', BF16, "ffn_up_fwd", ok=('seg', 2))
        a = _ffn_act_fwd(up3, w['ffn_conv_w'][i], w['ffn_conv_b'][i], "ffn_act_fwd")
        x = _mm(a, wb['ffn_w_down'], 'nn', F32, "ffn_down_fwd", res=x)
        s.update(h2=h2, up3=up3, a=a)
        saved.append(s)

    loss8, dx, dg_final = _loss_head(x, w['norm_final_g'], tgt)
    gs = {n: [None] * w[n].shape[0] for n in SMALL if n != 'norm_final_g'}
    gs['norm_final_g'] = dg_final.reshape(D)

    for i in reversed(range(depth)):
        j = i // 2
        s = saved[i]
        wb = s['wb']
        gb = {}
        da = _mm(dx, wb['ffn_w_down'], 'nt', BF16, "ffn_down_dgrad")
        gb['ffn_w_down'] = _mm(s['a'].T, dx, 'nn', BF16, "ffn_down_wgrad")
        dup3, dcw, dcb = _ffn_act_bwd(s['up3'], da, w['ffn_conv_w'][i], w['ffn_conv_b'][i], "ffn_act_bwd")
        gs['ffn_conv_w'][i], gs['ffn_conv_b'][i] = dcw, dcb
        gb['ffn_w_up'] = _mm(s['h2'].T, dup3, 'nn', BF16, "ffn_up_wgrad", bk=('seg', 2))
        dh2 = _mm(dup3, wb['ffn_w_up'], 'nt', F32, "ffn_up_dgrad", ak=('seg', 2))
        dx, dg = _rms_bwd(s['x1'], w['norm_ffn_g'][i], dh2, dx, "ffn_norm_bwd")
        gs['norm_ffn_g'][i] = dg.reshape(D)
        if i % 2 == 0:
            dmix = _mm(dx, wb['even_w_out'], 'nt', F32, "even_out_dgrad")
            gb['even_w_out'] = _mm(s['mixin'].T, dx, 'nn', BF16, "even_out_wgrad")
            dpc, dcw = _sconv_bwd(s['proj'], dmix, w['even_conv_w'][j], "sconv_bwd")
            gs['even_conv_w'][j] = dcw
            dyb = _to_scan_order(dmix[:, D // 2:])
            dyraw, dglu_w, dglu_b = _glu_bwd(s['yraw'], dyb, wb['ssm_glu_w'], w['ssm_glu_b'][j], "glu_bwd")
            gb['ssm_glu_w'] = dglu_w.astype(BF16)
            gs['ssm_glu_b'][j] = dglu_b.reshape(-1)
            lr, li, bmat, cmat = s['s5']
            du, dbm, dcm, dlam, dd = _s5_bwd(dyraw, s['u'], s['s_re'], s['s_im'], lr, li, bmat, cmat,
                                            w['ssm_d'][j], "s5_bwd")
            gs['ssm_d'][j] = dd.reshape(-1)
            dcm = jnp.swapaxes(dcm, 1, 2)
            dprm = s['prep_vjp']((dlam[:, 0:1, :], dlam[:, 1:2, :], dbm, dcm))
            for n, gval in zip(('ssm_log_step', 'ssm_a_re', 'ssm_a_im', 'ssm_b_re', 'ssm_b_im', 'ssm_c_re',
                                'ssm_c_im'), dprm):
                gs[n][j] = gval
            dproj = jnp.concatenate([dpc, _from_scan_order(du).astype(BF16)[None]], axis=0)
            gb['even_w_in'] = _mm(s['h'].T, dproj, 'nn', BF16, "even_in_wgrad", bk=('seg', 4))
            dh = _mm(dproj, wb['even_w_in'], 'nt', F32, "even_in_dgrad", ak=('seg', 4))
        else:
            dmix = _mm(dx, wb['odd_w_out'], 'nt', F32, "odd_out_dgrad")
            gb['odd_w_out'] = _mm(s['mixin'].T, dx, 'nn', BF16, "odd_out_wgrad")
            dz, dpw, dps = _pool_bwd(s['proj'], dmix, w['pool_w'][j], w['pool_scale'][j], "pool_bwd")
            gs['pool_w'][j], gs['pool_scale'][j] = dpw, dps.reshape(-1)
            dsuv, dsw, dsb, dsg = _sgu_bwd(s['proj'], dmix, w['sgu_norm_g'][j], w['sgu_w'][j], w['sgu_b'][j],
                                           "sgu_bwd")
            gs['sgu_w'][j], gs['sgu_b'][j], gs['sgu_norm_g'][j] = dsw, jnp.sum(dsb, axis=-1), dsg.reshape(-1)
            dproj = jnp.concatenate([dz[None], dsuv], axis=0)
            gb['odd_w_in'] = _mm(s['h'].T, dproj, 'nn', BF16, "odd_in_wgrad", bk=('seg', 3))
            dh = _mm(dproj, wb['odd_w_in'], 'nt', F32, "odd_in_dgrad", ak=('seg', 3))
        dx, dg = _rms_bwd(s['x'], w['norm_mix_g'][i], dh, dx, "mix_norm_bwd")
        gs['norm_mix_g'][i] = dg.reshape(D)
        on_layer_grads(i, gb)

    gsmall = {n: (v if n == 'norm_final_g' else jnp.stack(v)) for n, v in gs.items()}
    return loss8[0, 0], dx, gsmall


_HBM = pl.BlockSpec(memory_space=pltpu.HBM)
_CHIP_FLIPS = ((0, 0), (1, 0), (0, 1), (1, 1))


def _coords():
    return lax.axis_index("x"), lax.axis_index("y"), lax.axis_index("c")


def _flip(v, f):
    return 1 - v if f else v


def _shard_of(ref, axis, s, width):
    start = pl.multiple_of(s * width, LANES if axis == ref.ndim - 1 else 16) if width % 16 == 0 else s * width
    idx = [slice(None)] * ref.ndim
    idx[axis] = pl.ds(start, width)
    return ref.at[tuple(idx)]


_SEM = pl.BlockSpec(memory_space=pltpu.SEMAPHORE)
_ANY = pl.BlockSpec(memory_space=pl.ANY)
_DATAFLOW = pltpu.SideEffectType.DATAFLOW_SIDE_EFFECTING


def _in_hbm(a):
    return pltpu.with_memory_space_constraint(a, pltpu.HBM)


def _model_layer(name, l):
    if name.startswith('ffn'):
        return l
    return 2 * l + 1 if name.startswith('odd') else 2 * l


def _place_quarter(shard, l, axis, chip):
    _, r, c = shard.shape
    tr = _pick(r, prefs=(512, 256, 128, 64, 32, 16))
    nrb = r // tr

    def body(chip_ref, i_ref, o_ref):
        o_ref[...] = i_ref[...]

    if axis == 1:
        out_shape, o_map = (r, c * N_CHIPS), (lambda i, s: (i, s[0]))
    else:
        out_shape, o_map = (r * N_CHIPS, c), (lambda i, s: (s[0] * nrb + i, 0))
    return pl.pallas_call(
        body, name="place_quarter", out_shape=jax.ShapeDtypeStruct(out_shape, shard.dtype),
        grid_spec=pltpu.PrefetchScalarGridSpec(
            num_scalar_prefetch=1, grid=(nrb,), in_specs=[pl.BlockSpec((None, tr, c), lambda i, s: (l, i, 0))],
            out_specs=pl.BlockSpec((tr, c), o_map)),
        compiler_params=_cparams(("parallel",)),
    )(chip, shard)


def _gather_copies(shard_refs, land_refs, sems, pairs, axes, group, g, landing_chip_of):
    x, y, c = _coords()
    out = []
    for j, k in enumerate(group):
        t, l = pairs[k]
        width = shard_refs[t].shape[axes[t] + 1]
        for f in (1, 2, 3):
            fx, fy = _CHIP_FLIPS[f]
            px, py = _flip(x, fx), _flip(y, fy)
            lx, ly = landing_chip_of(px, py)
            out.append(pltpu.make_async_remote_copy(
                src_ref=shard_refs[t].at[l], dst_ref=_shard_of(land_refs[k], axes[t], 2 * lx + ly, width),
                send_sem=sems[2 * g].at[3 * j + f - 1], recv_sem=sems[2 * g + 1].at[3 * j + f - 1],
                device_id=(px, py, c), device_id_type=MESH))
    return out


def _gather_start(shards, lands, pairs, axes, groups):
    nt, npair, ng = len(shards), len(pairs), len(groups)

    def body(*refs):
        shard_refs, land_refs = refs[:nt], refs[nt:nt + npair]
        sems = refs[nt + npair:nt + npair + 2 * ng]
        token = refs[-1]
        x, y, _ = _coords()
        for g, group in enumerate(groups):
            for cp in _gather_copies(shard_refs, land_refs, sems, pairs, axes, group, g, lambda px, py: (x, y)):
                cp.start()
        token[...] = jnp.zeros_like(token)

    sem_shapes = []
    for group in groups:
        sem_shapes += [pltpu.SemaphoreType.DMA((3 * len(group),)), pltpu.SemaphoreType.DMA((3 * len(group),))]
    thru = [pltpu.HBM(a.shape, a.dtype) for a in list(shards) + list(lands)]
    outs = pl.pallas_call(
        body, name="gather_start",
        out_shape=tuple(sem_shapes + thru + [jax.ShapeDtypeStruct((SUBLANES, LANES), F32)]),
        in_specs=[_HBM] * (nt + npair),
        out_specs=tuple([_SEM] * (2 * ng) + [_HBM] * (nt + npair) + [pl.BlockSpec(memory_space=pltpu.VMEM)]),
        input_output_aliases={i: 2 * ng + i for i in range(nt + npair)},
        compiler_params=pltpu.CompilerParams(has_side_effects=_DATAFLOW),
    )(*[_in_hbm(a) for a in list(shards) + list(lands)])
    sems = outs[:2 * ng]
    return sems, list(outs[2 * ng:2 * ng + nt]), list(outs[2 * ng + nt:2 * ng + nt + npair]), outs[-1]


def _gather_wait(g, shards, lands_g, send_sem, recv_sem, after, pairs, axes, group):
    nt, n = len(shards), len(group)

    def body(*refs):
        shard_refs, land_g = refs[:nt], refs[nt:nt + n]
        sems = {2 * g: refs[nt + n], 2 * g + 1: refs[nt + n + 1]}
        land_refs = {k: land_g[j] for j, k in enumerate(group)}
        for cp in _gather_copies(shard_refs, land_refs, sems, pairs, axes, group, g, lambda px, py: (px, py)):
            cp.wait_send()
            cp.wait_recv()

    thru = [pltpu.HBM(a.shape, a.dtype) for a in list(shards) + list(lands_g)]
    outs = pl.pallas_call(
        body, name=f"gather_wait_{g}", out_shape=tuple(thru),
        in_specs=[_HBM] * (nt + n) + [_SEM, _SEM, _ANY], out_specs=tuple([_HBM] * (nt + n)),
        input_output_aliases={i: i for i in range(nt + n)},
        compiler_params=pltpu.CompilerParams(has_side_effects=_DATAFLOW),
    )(*shards, *lands_g, send_sem, recv_sem, after)
    return list(outs[:nt]), list(outs[nt:])


N_SLOTS = N_DEV - 1


def _scatter_sends(grad_refs, land_refs, send_sem, recv_sem, meta):
    x, y, c = _coords()
    out = []
    for j, (axis, owner, q, width) in enumerate(meta):
        other = c if owner == 0 else 1 - c
        for f, (fx, fy) in enumerate(_CHIP_FLIPS):
            px, py = _flip(x, fx), _flip(y, fy)
            slot = f + 4 * other - 1
            out.append((other if f == 0 else None, pltpu.make_async_remote_copy(
                src_ref=_shard_of(grad_refs[j], axis, 2 * px + py, width), dst_ref=land_refs[j].at[q, slot],
                send_sem=send_sem.at[4 * j + f], recv_sem=recv_sem.at[N_SLOTS * j + slot],
                device_id=(px, py, owner), device_id_type=MESH)))
    return out


def _scatter_start(layer, grads, lands, meta):
    n = len(grads)
    uniq = []
    for a in lands:
        if not any(a is u for u in uniq):
            uniq.append(a)
    which = [next(k for k, u in enumerate(uniq) if u is a) for a in lands]
    nu = len(uniq)

    def body(*refs):
        grad_refs, land_u = refs[:n], refs[n:n + nu]
        send_sem, recv_sem = refs[n + nu], refs[n + nu + 1]
        for other, cp in _scatter_sends(grad_refs, [land_u[k] for k in which], send_sem, recv_sem, meta):
            if other is None:
                cp.start()
            else:
                pl.when(other == 1)(cp.start)

    thru = [pltpu.HBM(a.shape, a.dtype) for a in list(grads) + uniq]
    outs = pl.pallas_call(
        body, name=f"scatter_start_{layer}",
        out_shape=tuple([pltpu.SemaphoreType.DMA((4 * n,)), pltpu.SemaphoreType.DMA((N_SLOTS * n,))] + thru),
        in_specs=[_HBM] * (n + nu), out_specs=tuple([_SEM, _SEM] + [_HBM] * (n + nu)),
        input_output_aliases={i: 2 + i for i in range(n + nu)},
        compiler_params=pltpu.CompilerParams(has_side_effects=_DATAFLOW),
    )(*[_in_hbm(a) for a in list(grads) + uniq])
    new_lands = [outs[2 + n + k] for k in which]
    return outs[0], outs[1], list(outs[2:2 + n]), new_lands, uniq


def _scatter_wait(started, lands):
    nl = len(lands)
    flat_grads = [g for s in started for g in s[2]]
    ng, ns = len(flat_grads), len(started)

    def body(*refs):
        land_refs = refs[:nl]
        grad_refs = refs[nl:nl + ng]
        sem_refs = refs[nl + ng:nl + ng + 2 * ns]
        _, _, c = _coords()
        off = 0
        for k, (_, _, grads, idx, meta) in enumerate(started):
            send_sem, recv_sem = sem_refs[2 * k], sem_refs[2 * k + 1]
            lr = [land_refs[i] for i in idx]
            for other, cp in _scatter_sends(grad_refs[off:off + len(grads)], lr, send_sem, recv_sem, meta):
                if other is None:
                    cp.wait_send()
                else:
                    pl.when(other == 1)(cp.wait_send)
            for j, (axis, owner, q, width) in enumerate(meta):
                mine = (c if owner == 0 else 1 - c) == 0

                @pl.when(mine)
                def _():
                    for slot in range(N_SLOTS):
                        land = lr[j].at[q, slot]
                        pltpu.make_async_remote_copy(
                            src_ref=land, dst_ref=land, send_sem=send_sem.at[0], recv_sem=recv_sem.at[N_SLOTS * j + slot],
                            device_id=_coords(), device_id_type=MESH).wait_recv()
            off += len(grads)

    args = list(lands) + flat_grads
    thru = [pltpu.HBM(a.shape, a.dtype) for a in args]
    sems = [s for st in started for s in st[:2]]
    outs = pl.pallas_call(
        body, name="scatter_wait", out_shape=tuple(thru), in_specs=[_HBM] * (nl + ng) + [_SEM] * (2 * ns),
        out_specs=tuple([_HBM] * (nl + ng)), input_output_aliases={i: i for i in range(nl + ng)},
        compiler_params=pltpu.CompilerParams(has_side_effects=_DATAFLOW),
    )(*args, *sems)
    return list(outs[:nl]), list(outs[nl:])


def _sum_slots(recv, name):
    n, ns, r, c = recv.shape
    tr = _pick(r, prefs=(256, 128, 64, 32, 16))

    def body(i_ref, o_ref):
        acc = i_ref[0].astype(F32)
        for s in range(1, ns):
            acc = acc + i_ref[s].astype(F32)
        o_ref[...] = acc

    return pl.pallas_call(
        body, name=name, out_shape=jax.ShapeDtypeStruct((n, r, c), F32), grid=(n, r // tr),
        in_specs=[pl.BlockSpec((None, ns, tr, c), lambda h, i: (h, 0, i, 0))],
        out_specs=pl.BlockSpec((None, tr, c), lambda h, i: (h, i, 0)),
        compiler_params=_cparams(("parallel", "parallel")),
    )(recv)


def _sum_and_share(recv, layer_grads, axis, chip, name):
    n, ns, r, c = recv.shape
    tr = _pick(r, prefs=(256, 128, 64, 32, 16))
    nr = r // tr
    nsteps = n * nr
    nlay = len(layer_grads)
    own_map = (lambda h, i, s: (i, s[0])) if axis == 1 else (lambda h, i, s: (s[0] * nr + i, 0))

    def body(chip_ref, i_ref, *rest):
        g_refs = rest[:nlay]
        o_ref, buf, loc_sems, send_sems, recv_sems = rest[nlay:]
        h, i = pl.program_id(0), pl.program_id(1)
        step = h * nr + i
        slot = step % 2
        x, y, core = _coords()
        layer = core * n + h
        own = g_refs[0][...]
        for l in range(1, nlay):
            own = jnp.where(layer == l, g_refs[l][...], own)

        def copies(sl):
            dst = o_ref.at[core * n + h, pl.ds(pl.multiple_of(i * tr, tr), tr), :]
            loc = pltpu.make_async_copy(buf.at[sl], dst, loc_sems.at[sl])
            rem = pltpu.make_async_remote_copy(
                src_ref=buf.at[sl], dst_ref=dst, send_sem=send_sems.at[sl], recv_sem=recv_sems.at[step],
                device_id=(x, y, 1 - core), device_id_type=MESH)
            return loc, rem

        def drain(sl):
            loc, rem = copies(sl)
            loc.wait()
            rem.wait_send()

        pl.when(step >= 2)(lambda: drain(slot))
        acc = own.astype(F32)
        for s in range(ns):
            acc = acc + i_ref[s].astype(F32)
        buf[slot] = acc
        loc, rem = copies(slot)
        loc.start()
        rem.start()

        @pl.when(step == nsteps - 1)
        def _():
            drain(slot)
            if nsteps > 1:
                drain(1 - slot)
            for hh in range(n):
                for ii in range(nr):
                    land = o_ref.at[(1 - core) * n + hh, pl.ds(ii * tr, tr), :]
                    pltpu.make_async_remote_copy(
                        src_ref=buf.at[0], dst_ref=land, send_sem=send_sems.at[0], recv_sem=recv_sems.at[hh * nr + ii],
                        device_id=(x, y, 1 - core), device_id_type=MESH).wait_recv()

    return pl.pallas_call(
        body, name=name, out_shape=jax.ShapeDtypeStruct((2 * n, r, c), F32),
        grid_spec=pltpu.PrefetchScalarGridSpec(
            num_scalar_prefetch=1, grid=(n, nr),
            in_specs=[pl.BlockSpec((None, ns, tr, c), lambda h, i, s: (h, 0, i, 0))]
            + [pl.BlockSpec((tr, c), own_map)] * nlay,
            out_specs=_HBM,
            scratch_shapes=[pltpu.VMEM((2, tr, c), F32), pltpu.SemaphoreType.DMA((2,)),
                            pltpu.SemaphoreType.DMA((2,)), pltpu.SemaphoreType.DMA((nsteps,))]),
        compiler_params=_cparams(("arbitrary", "arbitrary")),
    )(chip, recv, *layer_grads)


def _gather_over_devices(pack):
    def body(i_ref, o_ref, send_sems, recv_sems, loc_sem):
        x, y, c = _coords()
        me = 4 * x + 2 * y + c
        masks = [(m >> 2 & 1, m >> 1 & 1, m & 1) for m in range(1, N_DEV)]
        local = pltpu.make_async_copy(i_ref, o_ref.at[me], loc_sem)
        sends = [pltpu.make_async_remote_copy(
            src_ref=i_ref, dst_ref=o_ref.at[me], send_sem=send_sems.at[k], recv_sem=recv_sems.at[me],
            device_id=(_flip(x, fx), _flip(y, fy), _flip(c, fc)), device_id_type=MESH)
            for k, (fx, fy, fc) in enumerate(masks)]
        for cp in [local] + sends:
            cp.start()
        for fx, fy, fc in masks:
            px, py, pc = _flip(x, fx), _flip(y, fy), _flip(c, fc)
            peer = 4 * px + 2 * py + pc
            pltpu.make_async_remote_copy(
                src_ref=i_ref, dst_ref=o_ref.at[peer], send_sem=send_sems.at[0], recv_sem=recv_sems.at[peer],
                device_id=(px, py, pc), device_id_type=MESH).wait_recv()
        for cp in sends:
            cp.wait_send()
        local.wait()

    return pl.pallas_call(
        body, name="gather_small_grads", out_shape=jax.ShapeDtypeStruct((N_DEV,) + pack.shape, pack.dtype),
        in_specs=[_HBM], out_specs=_HBM,
        scratch_shapes=[pltpu.SemaphoreType.DMA((N_DEV - 1,)), pltpu.SemaphoreType.DMA((N_DEV,)),
                        pltpu.SemaphoreType.DMA],
    )(pack)


def _adamw(w, g, m, v, name):
    R, C = w.shape
    tr = _pick(R, prefs=(256, 128, 64, 32, 16, 8))
    bc1 = 1.0 - ADAM_B1 ** ADAM_STEP
    bc2 = 1.0 - ADAM_B2 ** ADAM_STEP

    def body(w_ref, g_ref, m_ref, v_ref, d_ref, mo_ref, vo_ref):
        gv = g_ref[...]
        mn = ADAM_B1 * m_ref[...] + (1.0 - ADAM_B1) * gv
        vn = ADAM_B2 * v_ref[...] + (1.0 - ADAM_B2) * (gv * gv)
        d_ref[...] = -ADAM_LR * ((mn / bc1) / (jnp.sqrt(vn / bc2) + ADAM_EPS) + ADAM_WD * w_ref[...])
        mo_ref[...] = mn
        vo_ref[...] = vn

    blk = pl.BlockSpec((tr, C), lambda i: (i, 0))
    sds = jax.ShapeDtypeStruct((R, C), F32)
    return pl.pallas_call(
        body, name=name, out_shape=(sds, sds, sds), grid=(R // tr,), in_specs=[blk] * 4, out_specs=(blk,) * 3,
        compiler_params=_cparams(("parallel",)),
    )(w, g, m, v)


_PACK_QUANTUM = 256 * LANES


def _pack(arrs):
    flat = jnp.concatenate([a.reshape(-1).astype(F32) for a in arrs])
    flat = jnp.pad(flat, (0, (-flat.shape[0]) % _PACK_QUANTUM))
    return flat.reshape(-1, LANES)


def _unpack(p, shapes):
    flat = p.reshape(-1)
    out, off = [], 0
    for s in shapes:
        n = int(np.prod(s))
        out.append(flat[off:off + n].reshape(s))
        off += n
    return out


def kernel(*args):
    nw = len(WEIGHTS)
    x, tgt = args[0], args[1 + nw]
    w = dict(zip(WEIGHTS, args[1:1 + nw]))
    m = dict(zip(WEIGHTS, args[2 + nw:2 + 2 * nw]))
    v = dict(zip(WEIGHTS, args[2 + 2 * nw:2 + 3 * nw]))
    _, L, D = x.shape
    chip = 2 * lax.axis_index("x") + lax.axis_index("y")

    big = list(BIG)
    small_sh_shapes = [w[n].shape for n in SMALL_SHARDED]
    nbig = len(big)
    chip1 = chip.reshape(1).astype(jnp.int32)
    axes2 = [BIG[n] - 1 for n in big] + [0]
    pack_sh = _pack([w[n] for n in SMALL_SHARDED])
    shards = [w[n].astype(BF16) for n in big] + [pack_sh[None]]
    pairs = [(t, l) for t in range(nbig + 1) for l in range(shards[t].shape[0])]
    depth = w['norm_mix_g'].shape[0]
    layer_of = lambda t, l: 0 if t == nbig else _model_layer(big[t], l)
    groups = [[k for k, (t, l) in enumerate(pairs) if layer_of(t, l) == i] for i in range(depth)]
    placed = [_place_quarter(shards[t], l, axes2[t], chip1) for t, l in pairs]
    sems, shards_thru, lands, token = _gather_start(shards, placed, pairs, axes2, groups)
    in_flight = {'shards': shards_thru}

    def wait_group(i, after):
        sh, landed = _gather_wait(i, in_flight['shards'], [lands[k] for k in groups[i]], sems[2 * i], sems[2 * i + 1],
                                  after, pairs, axes2, groups[i])
        in_flight['shards'] = sh
        return {pairs[k][0]: a for k, a in zip(groups[i], landed)}

    first = wait_group(0, token)
    packed = first.pop(nbig).reshape(N_CHIPS, -1, LANES)
    per_chip = [_unpack(packed[s], small_sh_shapes) for s in range(N_CHIPS)]
    wl = dict(w)
    for k, n in enumerate(SMALL_SHARDED):
        wl[n] = jnp.concatenate([per_chip[s][k] for s in range(N_CHIPS)], axis=-1)

    def layer_weights(i, after):
        got = first if i == 0 else wait_group(i, after)
        return {big[t]: a for t, a in got.items()}

    halves = [w[n].shape[0] // 2 for n in big]

    def quarter(t):
        shp = list(w[big[t]].shape[1:])
        return tuple(shp)

    land_now = [lax.empty((halves[t], N_SLOTS) + quarter(t), BF16) for t in range(nbig)]
    gbig = {n: [None] * w[n].shape[0] for n in big}
    started = []

    def on_layer_grads(i, gb):
        ts = [big.index(n) for n in gb]
        ls = [i if big[t].startswith('ffn') else i // 2 for t in ts]
        meta = [(axes2[t], l // halves[t], l % halves[t], quarter(t)[axes2[t]]) for t, l in zip(ts, ls)]
        send, recv, thru, new_lands, _ = _scatter_start(i, [gb[big[t]] for t in ts], [land_now[t] for t in ts], meta)
        for t, ln in zip(ts, new_lands):
            land_now[t] = ln
        started.append((send, recv, thru, ts, meta, ls))

    loss, dx, gsmall = _local_step(x.reshape(L, D), tgt.reshape(L, D), wl, layer_weights, on_layer_grads)
    loss = lax.psum(loss, ("x", "y", "c"))

    landed, sent = _scatter_wait([s[:5] for s in started], land_now)
    for (t, l), g in zip([(t, l) for s in started for t, l in zip(s[3], s[5])], sent):
        gbig[big[t]][l] = g
    gshard = {n: _sum_and_share(landed[t], gbig[n], axes2[t], chip1, "sum_share_" + n) for t, n in enumerate(big)}

    small_shapes = [gsmall[n].shape for n in SMALL]
    gpack = _sum_slots(_gather_over_devices(_pack([gsmall[n] for n in SMALL]))[None], "sum_small_grads")[0]
    gs = dict(zip(SMALL, _unpack(gpack, small_shapes)))
    for n in SMALL_SHARDED:
        width = w[n].shape[-1]
        gs[n] = lax.dynamic_slice_in_dim(gs[n], chip * width, width, axis=gs[n].ndim - 1)

    grads, delta, new_m, new_v = {}, {}, {}, {}
    for n in big:
        shp = w[n].shape
        flat = lambda a: a.reshape(shp[0] * shp[1], shp[2])
        g = gshard[n]
        grads[n] = g
        d_, m_, v_ = _adamw(flat(w[n]), flat(g), flat(m[n]), flat(v[n]), "adamw_" + n)
        delta[n], new_m[n], new_v[n] = d_.reshape(shp), m_.reshape(shp), v_.reshape(shp)
    loc_shapes = [w[n].shape for n in SMALL]
    d_, m_, v_ = _adamw(_pack([w[n] for n in SMALL]), _pack([gs[n] for n in SMALL]), _pack([m[n] for n in SMALL]),
                        _pack([v[n] for n in SMALL]), "adamw_small")
    for n, dn, mn, vn in zip(SMALL, _unpack(d_, loc_shapes), _unpack(m_, loc_shapes), _unpack(v_, loc_shapes)):
        grads[n], delta[n], new_m[n], new_v[n] = gs[n], dn, mn, vn

    return (loss, dx.reshape(1, L, D), *[grads[n] for n in WEIGHTS], *[delta[n] for n in WEIGHTS],
            *[new_m[n] for n in WEIGHTS], *[new_v[n] for n in WEIGHTS])
```

```python
import functools
import math

import numpy as np
import jax
import jax.numpy as jnp
from jax import lax
from jax.experimental import pallas as pl
from jax.experimental.pallas import tpu as pltpu

F32 = jnp.float32
BF16 = jnp.bfloat16
MESH = pl.DeviceIdType.MESH

EPS = 1e-6
CHUNK = 128
POOL_WINDOWS = (2, 4, 8, 16)
LANES = 128
SUBLANES = 8
SCAN_CHUNKS = SUBLANES
S5_GROUPS_PER_STEP = 4
MM_TM_CAP, MM_TN_CAP, MM_TK_CAP = 1024, 1408, 1408
MM_TK_WHOLE = 2048
VMEM_LIMIT = 48 * 1024 * 1024
VMEM_LIMIT_S5 = 56 * 1024 * 1024

ADAM_LR, ADAM_B1, ADAM_B2, ADAM_EPS, ADAM_WD, ADAM_STEP = 0.001, 0.9, 0.999, 1e-08, 0.01, 10

WEIGHTS = ['norm_mix_g', 'even_w_in', 'even_conv_w', 'ssm_log_step', 'ssm_a_re', 'ssm_a_im', 'ssm_b_re',
           'ssm_b_im', 'ssm_c_re', 'ssm_c_im', 'ssm_d', 'ssm_glu_w', 'ssm_glu_b', 'even_w_out', 'odd_w_in',
           'pool_w', 'pool_scale', 'sgu_norm_g', 'sgu_w', 'sgu_b', 'odd_w_out', 'norm_ffn_g', 'ffn_w_up',
           'ffn_conv_w', 'ffn_conv_b', 'ffn_w_down', 'norm_final_g']
BIG = {'even_w_in': 2, 'ssm_glu_w': 1, 'even_w_out': 1, 'odd_w_in': 2, 'odd_w_out': 1, 'ffn_w_up': 2,
       'ffn_w_down': 1}
SMALL_SHARDED = ('even_conv_w', 'pool_scale', 'sgu_norm_g', 'ffn_conv_w')
SMALL = [n for n in WEIGHTS if n not in BIG]
N_CHIPS = 4
N_DEV = 8


def _cparams(sem=None, vmem=VMEM_LIMIT):
    kw = dict(vmem_limit_bytes=vmem)
    if sem is not None:
        kw['dimension_semantics'] = sem
    return pltpu.CompilerParams(**kw)


def _pick(n, segs=(), prefs=(1024, 512, 256, 128)):
    for t in prefs:
        if n % t == 0 and all(s % t == 0 for s in segs if s):
            return t
    return n


def _largest_tile(n, segs, cap):
    best = None
    for t in range(LANES, min(n, cap) + 1, LANES):
        if n % t == 0 and all(s % t == 0 for s in segs if s):
            best = t
    return best if best is not None else n


def _ldims(arr, kind):
    if kind is None:
        return arr.shape
    if kind[0] == 'lead':
        return arr.shape[1:]
    return (arr.shape[1], arr.shape[0] * arr.shape[2])


def _segw(arr, kind):
    return arr.shape[2] if (kind is not None and kind[0] == 'seg') else None


def _opspec(arr, kind, br, bc, rfn, cfn):
    if kind is None:
        return pl.BlockSpec((br, bc), lambda i, j, k: (rfn(i, j, k), cfn(i, j, k)))
    if kind[0] == 'lead':
        lead = kind[1]
        return pl.BlockSpec((None, br, bc), lambda i, j, k: (lead, rfn(i, j, k), cfn(i, j, k)))
    per = arr.shape[2] // bc
    return pl.BlockSpec((None, br, bc), lambda i, j, k: (cfn(i, j, k) // per, rfn(i, j, k), cfn(i, j, k) % per))


def _mm(a, b, mode, out_dtype, name, ak=None, bk=None, ok=None, res=None, dep=None):
    ar, ac = _ldims(a, ak)
    br_, bc_ = _ldims(b, bk)
    if mode == 'nn':
        M, K, N = ar, ac, bc_
        assert br_ == K
    else:
        M, K, N = ar, ac, br_
        assert bc_ == K
    sa, sb = _segw(a, ak), _segw(b, bk)
    so = (N // ok[1]) if ok is not None else None
    tm = _largest_tile(M, [], MM_TM_CAP)
    tn = _largest_tile(N, [sb if mode == 'nn' else None, so], MM_TN_CAP)
    ksegs = [sa, sb if mode == 'nt' else None]
    tk = K if (K <= MM_TK_WHOLE and not any(ksegs)) else _largest_tile(K, ksegs, MM_TK_CAP)
    nk = K // tk
    I = lambda i, j, k: i
    J = lambda i, j, k: j
    Kk = lambda i, j, k: k
    a_spec = _opspec(a, ak, tm, tk, I, Kk)
    if mode == 'nn':
        b_spec = _opspec(b, bk, tk, tn, Kk, J)
        dims = (((1,), (0,)), ((), ()))
    else:
        b_spec = _opspec(b, bk, tn, tk, J, Kk)
        dims = (((1,), (1,)), ((), ()))
    if ok is None:
        out_shape = jax.ShapeDtypeStruct((M, N), out_dtype)
        o_spec = pl.BlockSpec((tm, tn), lambda i, j, k: (i, j))
    else:
        out_shape = jax.ShapeDtypeStruct((ok[1], M, N // ok[1]), out_dtype)
        per = (N // ok[1]) // tn
        o_spec = pl.BlockSpec((None, tm, tn), lambda i, j, k: (j // per, i, j % per))
    has_res = res is not None

    def body(*refs):
        a_ref, b_ref = refs[0], refs[1]
        r_ref = refs[2] if has_res else None
        o_ref = refs[n_in]
        prod = lax.dot_general(a_ref[...].astype(BF16), b_ref[...].astype(BF16), dims, preferred_element_type=F32)
        if nk == 1:
            o_ref[...] = (prod + r_ref[...] if has_res else prod).astype(out_dtype)
            return
        acc = refs[-1]
        k = pl.program_id(2)

        @pl.when(k == 0)
        def _():
            acc[...] = prod

        @pl.when(k > 0)
        def _():
            acc[...] += prod

        @pl.when(k == nk - 1)
        def _():
            o = acc[...]
            if has_res:
                o = o + r_ref[...]
            o_ref[...] = o.astype(out_dtype)

    in_specs = [a_spec, b_spec]
    args = [a, b]
    if has_res:
        in_specs.append(pl.BlockSpec((tm, tn), lambda i, j, k: (i, j)))
        args.append(res)
    if dep is not None:
        in_specs.append(pl.BlockSpec(memory_space=pl.ANY))
        args.append(dep)
    n_in = len(args)
    return pl.pallas_call(
        body, name=name, out_shape=out_shape, grid=(M // tm, N // tn, nk), in_specs=in_specs, out_specs=o_spec,
        scratch_shapes=[pltpu.VMEM((tm, tn), F32)] if nk > 1 else [],
        compiler_params=_cparams(("parallel", "parallel", "arbitrary")),
    )(*args)


_G0 = math.sqrt(2.0 / math.pi)
_G1 = 0.044715


def _gelu(x):
    return 0.5 * x * (1.0 + jnp.tanh(_G0 * (x + _G1 * x * x * x)))


def _gelu_grad(x):
    x2 = x * x
    t = jnp.tanh(_G0 * (x + _G1 * x * x2))
    return 0.5 * (1.0 + t) + 0.5 * x * (1.0 - t * t) * (_G0 * (1.0 + 3.0 * _G1 * x2))


def _sigmoid(x):
    return 1.0 / (1.0 + jnp.exp(-x))


def _down(v, k):
    row = lax.broadcasted_iota(jnp.int32, v.shape, 0)
    return jnp.where(row >= k, pltpu.roll(v, k, axis=0), 0.0)


def _up(v, k):
    n = v.shape[0]
    row = lax.broadcasted_iota(jnp.int32, v.shape, 0)
    return jnp.where(row < n - k, pltpu.roll(v, n - k, axis=0), 0.0)


def _conv3(v, w):
    return w[0:1, :] * _down(v, 2) + w[1:2, :] * _down(v, 1) + w[2:3, :] * v


def _conv3_t(dv, w):
    return w[2:3, :] * dv + w[1:2, :] * _up(dv, 1) + w[0:1, :] * _up(dv, 2)


def _conv3_dw(dv, v):
    return (jnp.sum(dv * _down(v, 2), axis=0, keepdims=True),
            jnp.sum(dv * _down(v, 1), axis=0, keepdims=True),
            jnp.sum(dv * v, axis=0, keepdims=True))


def _cmul(ar, ai, br, bi):
    return ar * br - ai * bi, ar * bi + ai * br


def _cpow(lr, li, n):
    rr = ri = None
    br, bi = lr, li
    while n:
        if n & 1:
            rr, ri = (br, bi) if rr is None else _cmul(rr, ri, br, bi)
        n >>= 1
        if n:
            br, bi = _cmul(br, bi, br, bi)
    return rr, ri


def _rms_fwd(x, g, name):
    L, D = x.shape
    tr = _pick(L, prefs=(512, 256, 128))

    def body(x_ref, g_ref, h_ref, ht_ref):
        xv = x_ref[...]
        r = lax.rsqrt(jnp.mean(xv * xv, axis=-1, keepdims=True) + EPS)
        h = xv * r * g_ref[...]
        h_ref[...] = h.astype(BF16)
        ht_ref[...] = h.T.astype(BF16)

    return pl.pallas_call(
        body, name=name, out_shape=(jax.ShapeDtypeStruct((L, D), BF16), jax.ShapeDtypeStruct((D, L), BF16)),
        grid=(L // tr,),
        in_specs=[pl.BlockSpec((tr, D), lambda i: (i, 0)), pl.BlockSpec((1, D), lambda i: (0, 0))],
        out_specs=(pl.BlockSpec((tr, D), lambda i: (i, 0)), pl.BlockSpec((D, tr), lambda i: (0, i))),
        compiler_params=_cparams(("parallel",)),
    )(x, g.reshape(1, D))


def _rms_bwd(x, g, dh, dres, name):
    L, D = x.shape
    tr = _pick(L, prefs=(512, 256, 128))
    nsteps = L // tr

    def body(x_ref, g_ref, dh_ref, dres_ref, dx_ref, dg_ref, acc):
        i = pl.program_id(0)

        @pl.when(i == 0)
        def _():
            acc[...] = jnp.zeros_like(acc)

        xv = x_ref[...]
        r = lax.rsqrt(jnp.mean(xv * xv, axis=-1, keepdims=True) + EPS)
        xh = xv * r
        dhv = dh_ref[...].astype(F32)
        acc[...] += jnp.sum((dhv * xh).reshape(tr // SUBLANES, SUBLANES, D), axis=0)
        dxh = dhv * g_ref[...]
        dx_ref[...] = dres_ref[...] + r * (dxh - xh * jnp.mean(dxh * xh, axis=-1, keepdims=True))

        @pl.when(i == nsteps - 1)
        def _():
            dg_ref[...] = jnp.sum(acc[...], axis=0, keepdims=True)

    row = pl.BlockSpec((tr, D), lambda i: (i, 0))
    vec = pl.BlockSpec((1, D), lambda i: (0, 0))
    return pl.pallas_call(
        body, name=name, out_shape=(jax.ShapeDtypeStruct((L, D), F32), jax.ShapeDtypeStruct((1, D), F32)),
        grid=(nsteps,), in_specs=[row, vec, row, row], out_specs=(row, vec),
        scratch_shapes=[pltpu.VMEM((SUBLANES, D), F32)], compiler_params=_cparams(("arbitrary",)),
    )(x, g.reshape(1, D), dh, dres)


def _loss_head(x, g, tgt):
    L, D = x.shape
    tr = _pick(L, prefs=(512, 256, 128))
    nsteps = L // tr

    def body(x_ref, g_ref, t_ref, loss_ref, dx_ref, dg_ref, acc_g, acc_l):
        i = pl.program_id(0)

        @pl.when(i == 0)
        def _():
            acc_g[...] = jnp.zeros_like(acc_g)
            acc_l[...] = jnp.zeros_like(acc_l)

        xv = x_ref[...]
        gv = g_ref[...]
        r = lax.rsqrt(jnp.mean(xv * xv, axis=-1, keepdims=True) + EPS)
        xh = xv * r
        e = xh * gv - t_ref[...]
        acc_l[...] += jnp.sum((e * e).reshape(tr // SUBLANES, SUBLANES, D), axis=0)
        dy = e * (1.0 / D)
        acc_g[...] += jnp.sum((dy * xh).reshape(tr // SUBLANES, SUBLANES, D), axis=0)
        dxh = dy * gv
        dx_ref[...] = r * (dxh - xh * jnp.mean(dxh * xh, axis=-1, keepdims=True))

        @pl.when(i == nsteps - 1)
        def _():
            dg_ref[...] = jnp.sum(acc_g[...], axis=0, keepdims=True)
            tot = jnp.sum(jnp.sum(acc_l[...], axis=0, keepdims=True), axis=1, keepdims=True) * (0.5 / D)
            loss_ref[...] = jnp.broadcast_to(tot, (SUBLANES, LANES))

    row = pl.BlockSpec((tr, D), lambda i: (i, 0))
    vec = pl.BlockSpec((1, D), lambda i: (0, 0))
    return pl.pallas_call(
        body, name="loss_head",
        out_shape=(jax.ShapeDtypeStruct((SUBLANES, LANES), F32), jax.ShapeDtypeStruct((L, D), F32),
                   jax.ShapeDtypeStruct((1, D), F32)),
        grid=(nsteps,), in_specs=[row, vec, row],
        out_specs=(pl.BlockSpec((SUBLANES, LANES), lambda i: (0, 0)), row, vec),
        scratch_shapes=[pltpu.VMEM((SUBLANES, D), F32), pltpu.VMEM((SUBLANES, D), F32)],
        compiler_params=_cparams(("arbitrary",)),
    )(x, g.reshape(1, D), tgt)


def _sconv_fwd(proj4, conv_w, name):
    _, L, C = proj4.shape
    cb = LANES

    def body(p_ref, w_ref, o_ref):
        xa, ba, ca = p_ref[0], p_ref[1], p_ref[2]
        o_ref[...] = (ba * _conv3(ca * xa, w_ref[...])).astype(BF16)

    return pl.pallas_call(
        body, name=name, out_shape=jax.ShapeDtypeStruct((L, C), BF16), grid=(C // cb,),
        in_specs=[pl.BlockSpec((3, L, cb), lambda j: (0, 0, j)), pl.BlockSpec((3, cb), lambda j: (0, j))],
        out_specs=pl.BlockSpec((L, cb), lambda j: (0, j)), compiler_params=_cparams(("parallel",)),
    )(proj4, conv_w)


def _sconv_bwd(proj4, dmix, conv_w, name):
    _, L, C = proj4.shape
    cb = LANES

    def body(p_ref, d_ref, w_ref, o_ref, dw_ref):
        xa, ba, ca = p_ref[0], p_ref[1], p_ref[2]
        w = w_ref[...]
        dya = d_ref[...]
        q = ca * xa
        cq = _conv3(q, w)
        dcq = dya * ba
        dq = _conv3_t(dcq, w)
        for tap, dwt in enumerate(_conv3_dw(dcq, q)):
            dw_ref[tap:tap + 1, :] = dwt
        o_ref[0] = (dq * ca).astype(BF16)
        o_ref[1] = (dya * cq).astype(BF16)
        o_ref[2] = (dq * xa).astype(BF16)

    return pl.pallas_call(
        body, name=name,
        out_shape=(jax.ShapeDtypeStruct((3, L, C), BF16), jax.ShapeDtypeStruct((3, C), F32)), grid=(C // cb,),
        in_specs=[pl.BlockSpec((3, L, cb), lambda j: (0, 0, j)), pl.BlockSpec((L, cb), lambda j: (0, j)),
                  pl.BlockSpec((3, cb), lambda j: (0, j))],
        out_specs=(pl.BlockSpec((3, L, cb), lambda j: (0, 0, j)), pl.BlockSpec((3, cb), lambda j: (0, j))),
        compiler_params=_cparams(("parallel",)),
    )(proj4, dmix, conv_w)


def _to_scan_order(v):
    L, C = v.shape
    return v.reshape(SCAN_CHUNKS, L // SCAN_CHUNKS, C).transpose(1, 0, 2).reshape(L, C)


def _from_scan_order(v):
    L, C = v.shape
    return v.reshape(L // SCAN_CHUNKS, SCAN_CHUNKS, C).transpose(1, 0, 2).reshape(L, C)


def _s5_prep(log_step, a_re, a_im, b_re, b_im, c_re, c_im):
    G, P = a_re.shape
    H = b_re.shape[-1]
    gs = S5_GROUPS_PER_STEP
    ns = G // gs
    gu = LANES // H
    lam = lax.complex(a_re, a_im)
    step = jnp.exp(log_step)[:, None]
    lam_bar = jnp.exp(lam * step)
    b_bar = ((lam_bar - 1.0) / lam)[..., None] * lax.complex(b_re, b_im)
    lr = jnp.real(lam_bar).reshape(ns, 1, gs * P)
    li = jnp.imag(lam_bar).reshape(ns, 1, gs * P)
    k = np.arange(ns)[:, None, None]
    oh = jnp.asarray((np.arange(gu)[None, :, None] == gs * (k % (gu // gs)) + np.arange(gs)[None, None, :]),
                     F32)
    bre = jnp.einsum('kgl,klph->kghlp', oh, jnp.real(b_bar).reshape(ns, gs, P, H)).reshape(ns, gu * H, gs * P)
    bim = jnp.einsum('kgl,klph->kghlp', oh, jnp.imag(b_bar).reshape(ns, gs, P, H)).reshape(ns, gu * H, gs * P)
    cre = jnp.einsum('kgl,klhp->klpgh', oh, c_re.reshape(ns, gs, H, P)).reshape(ns, gs * P, gu * H)
    cim = jnp.einsum('kgl,klhp->klpgh', oh, c_im.reshape(ns, gs, H, P)).reshape(ns, gs * P, gu * H)
    return lr, li, jnp.concatenate([bre, bim], axis=2), jnp.concatenate([cre, -cim], axis=1)


def _carry_tile(fr, fi, pr, pi, reverse):
    row = lax.broadcasted_iota(jnp.int32, fr.shape, 0)
    cr = jnp.zeros_like(fr)
    ci = jnp.zeros_like(fi)
    sr = jnp.zeros_like(fr[0:1])
    si = jnp.zeros_like(sr)
    order = range(SCAN_CHUNKS - 1, 0, -1) if reverse else range(0, SCAN_CHUNKS - 1)
    for c in order:
        fcr = jnp.sum(jnp.where(row == c, fr, 0.0), axis=0, keepdims=True)
        fci = jnp.sum(jnp.where(row == c, fi, 0.0), axis=0, keepdims=True)
        mr, mi = _cmul(pr, pi, sr, si)
        sr, si = mr + fcr, mi + fci
        nxt = c - 1 if reverse else c + 1
        cr = jnp.where(row == nxt, sr, cr)
        ci = jnp.where(row == nxt, si, ci)
    return cr, ci


def _s5_fwd(u, lr, li, bmat, cmat, d, name):
    L, Du = u.shape
    ns, _, W2 = bmat.shape
    W = W2 // 2
    T = L // SCAN_CHUNKS
    rb = _pick(L, prefs=(512, 256, 128))
    per = (ns * LANES) // Du

    def body(u_ref, lr_ref, li_ref, b_ref, c_ref, d_ref, y_ref, sr_ref, si_ref):
        k = pl.program_id(0)
        for r in range(L // rb):
            rows = pl.ds(r * rb, rb)
            bu = jnp.dot(u_ref[rows, :].astype(BF16), b_ref[...], preferred_element_type=F32)
            sr_ref[rows, :] = bu[:, :W]
            si_ref[rows, :] = bu[:, W:]
        lam_r = jnp.broadcast_to(lr_ref[...], (SUBLANES, W))
        lam_i = jnp.broadcast_to(li_ref[...], (SUBLANES, W))

        def local(t, carry):
            sr, si = carry
            rows = pl.ds(pl.multiple_of(t * SUBLANES, SUBLANES), SUBLANES)
            mr, mi = _cmul(lam_r, lam_i, sr, si)
            sr = mr + sr_ref[rows, :]
            si = mi + si_ref[rows, :]
            sr_ref[rows, :] = sr
            si_ref[rows, :] = si
            return sr, si

        z = jnp.zeros((SUBLANES, W), F32)
        fr, fi = lax.fori_loop(0, T, local, (z, z))
        pr, pi = _cpow(lam_r, lam_i, T)
        cr, ci = _carry_tile(fr, fi, pr[0:1], pi[0:1], reverse=False)

        def fix(t, carry):
            wr, wi = carry
            rows = pl.ds(pl.multiple_of(t * SUBLANES, SUBLANES), SUBLANES)
            ar, ai = _cmul(wr, wi, cr, ci)
            sr_ref[rows, :] += ar
            si_ref[rows, :] += ai
            return _cmul(wr, wi, lam_r, lam_i)

        lax.fori_loop(0, T, fix, (lam_r, lam_i))
        first = (k % per) == 0
        for r in range(L // rb):
            rows = pl.ds(r * rb, rb)
            s = jnp.concatenate([sr_ref[rows, :], si_ref[rows, :]], axis=1).astype(BF16)
            y = jnp.dot(s, c_ref[...], preferred_element_type=F32)

            @pl.when(first)
            def _():
                y_ref[rows, :] = y + d_ref[...] * u_ref[rows, :]

            @pl.when(jnp.logical_not(first))
            def _():
                y_ref[rows, :] += y

    ublk = pl.BlockSpec((L, LANES), lambda k: (0, k // per))
    sblk = pl.BlockSpec((L, W), lambda k: (0, k))
    lam = pl.BlockSpec((None, 1, W), lambda k: (k, 0, 0))
    return pl.pallas_call(
        body, name=name,
        out_shape=(jax.ShapeDtypeStruct((L, Du), F32), jax.ShapeDtypeStruct((L, ns * W), F32),
                   jax.ShapeDtypeStruct((L, ns * W), F32)),
        grid=(ns,),
        in_specs=[ublk, lam, lam, pl.BlockSpec((None, LANES, 2 * W), lambda k: (k, 0, 0)),
                  pl.BlockSpec((None, 2 * W, LANES), lambda k: (k, 0, 0)),
                  pl.BlockSpec((1, LANES), lambda k: (0, k // per))],
        out_specs=(ublk, sblk, sblk), compiler_params=_cparams(("arbitrary",), VMEM_LIMIT_S5),
    )(u, lr, li, bmat.astype(BF16), cmat.astype(BF16), d.reshape(1, Du))


def _s5_bwd(dy, u, s_re, s_im, lr, li, bmat, cmat, d, name):
    L, Du = u.shape
    ns, _, W2 = bmat.shape
    W = W2 // 2
    T = L // SCAN_CHUNKS
    rb = _pick(L, prefs=(512, 256, 128))
    per = (ns * LANES) // Du
    NT = (((1,), (1,)), ((), ()))
    TN = (((0,), (0,)), ((), ()))

    def body(dy_ref, u_ref, sr_ref, si_ref, lr_ref, li_ref, b_ref, c_ref, d_ref,
             du_ref, db_ref, dc_ref, dl_ref, dd_ref, gr_ref, gi_ref):
        k = pl.program_id(0)
        for r in range(L // rb):
            rows = pl.ds(r * rb, rb)
            g = lax.dot_general(dy_ref[rows, :].astype(BF16), c_ref[...], NT, preferred_element_type=F32)
            gr_ref[rows, :] = g[:, :W]
            gi_ref[rows, :] = g[:, W:]
        lam_r = jnp.broadcast_to(lr_ref[...], (SUBLANES, W))
        lam_i = -jnp.broadcast_to(li_ref[...], (SUBLANES, W))

        def local(i, carry):
            gr, gi = carry
            rows = pl.ds(pl.multiple_of((T - 1 - i) * SUBLANES, SUBLANES), SUBLANES)
            mr, mi = _cmul(lam_r, lam_i, gr, gi)
            gr = mr + gr_ref[rows, :]
            gi = mi + gi_ref[rows, :]
            gr_ref[rows, :] = gr
            gi_ref[rows, :] = gi
            return gr, gi

        z = jnp.zeros((SUBLANES, W), F32)
        fr, fi = lax.fori_loop(0, T, local, (z, z))
        pr, pi = _cpow(lam_r, lam_i, T)
        cr, ci = _carry_tile(fr, fi, pr[0:1], pi[0:1], reverse=True)

        def true_g(rows, wr, wi):
            ar, ai = _cmul(wr, wi, cr, ci)
            gr = gr_ref[rows, :] + ar
            gi = gi_ref[rows, :] + ai
            gr_ref[rows, :] = gr
            gi_ref[rows, :] = gi
            return gr, gi

        def fix(i, carry):
            wr, wi, ar_, ai_ = carry
            t = T - 1 - i
            rows = pl.ds(pl.multiple_of(t * SUBLANES, SUBLANES), SUBLANES)
            prev = pl.ds(pl.multiple_of((t - 1) * SUBLANES, SUBLANES), SUBLANES)
            gr, gi = true_g(rows, wr, wi)
            qr, qi = sr_ref[prev, :], si_ref[prev, :]
            ar_ = ar_ + gr * qr + gi * qi
            ai_ = ai_ + gi * qr - gr * qi
            wr, wi = _cmul(wr, wi, lam_r, lam_i)
            return wr, wi, ar_, ai_

        wr, wi, acc_r, acc_i = lax.fori_loop(0, T - 1, fix, (lam_r, lam_i, z, z))
        gr, gi = true_g(pl.ds(0, SUBLANES), wr, wi)
        last = pl.ds((T - 1) * SUBLANES, SUBLANES)
        row = lax.broadcasted_iota(jnp.int32, (SUBLANES, W), 0)
        qr = jnp.where(row >= 1, pltpu.roll(sr_ref[last, :], 1, axis=0), 0.0)
        qi = jnp.where(row >= 1, pltpu.roll(si_ref[last, :], 1, axis=0), 0.0)
        acc_r = acc_r + gr * qr + gi * qi
        acc_i = acc_i + gi * qr - gr * qi
        dl_ref[0:1, :] = jnp.sum(acc_r, axis=0, keepdims=True)
        dl_ref[1:2, :] = jnp.sum(acc_i, axis=0, keepdims=True)

        first = (k % per) == 0
        db = jnp.zeros((LANES, 2 * W), F32)
        dc = jnp.zeros((LANES, 2 * W), F32)
        dd = jnp.zeros((1, LANES), F32)
        for r in range(L // rb):
            rows = pl.ds(r * rb, rb)
            gb = jnp.concatenate([gr_ref[rows, :], gi_ref[rows, :]], axis=1).astype(BF16)
            sb = jnp.concatenate([sr_ref[rows, :], si_ref[rows, :]], axis=1).astype(BF16)
            dyv = dy_ref[rows, :]
            uv = u_ref[rows, :]
            du = lax.dot_general(gb, b_ref[...], NT, preferred_element_type=F32)
            db = db + lax.dot_general(uv.astype(BF16), gb, TN, preferred_element_type=F32)
            dc = dc + lax.dot_general(dyv.astype(BF16), sb, TN, preferred_element_type=F32)
            dd = dd + jnp.sum(dyv * uv, axis=0, keepdims=True)

            @pl.when(first)
            def _():
                du_ref[rows, :] = du + d_ref[...] * dyv

            @pl.when(jnp.logical_not(first))
            def _():
                du_ref[rows, :] += du

        db_ref[...] = db
        dc_ref[...] = dc

        @pl.when(first)
        def _():
            dd_ref[...] = dd

    ublk = pl.BlockSpec((L, LANES), lambda k: (0, k // per))
    sblk = pl.BlockSpec((L, W), lambda k: (0, k))
    lam = pl.BlockSpec((None, 1, W), lambda k: (k, 0, 0))
    vec = pl.BlockSpec((1, LANES), lambda k: (0, k // per))
    mat = pl.BlockSpec((None, LANES, 2 * W), lambda k: (k, 0, 0))
    return pl.pallas_call(
        body, name=name,
        out_shape=(jax.ShapeDtypeStruct((L, Du), F32), jax.ShapeDtypeStruct((ns, LANES, 2 * W), F32),
                   jax.ShapeDtypeStruct((ns, LANES, 2 * W), F32), jax.ShapeDtypeStruct((ns, 2, W), F32),
                   jax.ShapeDtypeStruct((1, Du), F32)),
        grid=(ns,),
        in_specs=[ublk, ublk, sblk, sblk, lam, lam, mat,
                  pl.BlockSpec((None, 2 * W, LANES), lambda k: (k, 0, 0)), vec],
        out_specs=(ublk, mat, mat, pl.BlockSpec((None, 2, W), lambda k: (k, 0, 0)), vec),
        scratch_shapes=[pltpu.VMEM((L, W), F32), pltpu.VMEM((L, W), F32)],
        compiler_params=_cparams(("arbitrary",), VMEM_LIMIT_S5),
    )(dy, u, s_re, s_im, lr, li, bmat.astype(BF16), cmat.astype(BF16), d.reshape(1, Du))


def _glu_fwd(yraw, wmat, bias, name):
    L, C = yraw.shape
    tr = _pick(L, prefs=(512, 256, 128))

    def body(y_ref, w_ref, b_ref, o_ref):
        yg = _gelu(y_ref[...])
        zz = jnp.dot(yg.astype(BF16), w_ref[...], preferred_element_type=F32) + b_ref[...]
        o_ref[...] = (yg * _sigmoid(zz)).astype(BF16)

    return pl.pallas_call(
        body, name=name, out_shape=jax.ShapeDtypeStruct((L, C), BF16), grid=(L // tr,),
        in_specs=[pl.BlockSpec((tr, C), lambda i: (i, 0)), pl.BlockSpec((C, C), lambda i: (0, 0)),
                  pl.BlockSpec((1, C), lambda i: (0, 0))],
        out_specs=pl.BlockSpec((tr, C), lambda i: (i, 0)), compiler_params=_cparams(("parallel",)),
    )(yraw, wmat, bias.reshape(1, C))


def _glu_bwd(yraw, dyb, wmat, bias, name):
    L, C = yraw.shape
    tr = _pick(L, prefs=(512, 256, 128))
    nsteps = L // tr

    def body(y_ref, d_ref, w_ref, b_ref, dy_ref, dw_ref, db_ref, acc_b):
        i = pl.program_id(0)

        @pl.when(i == 0)
        def _():
            dw_ref[...] = jnp.zeros_like(dw_ref)
            acc_b[...] = jnp.zeros_like(acc_b)

        yr = y_ref[...]
        yg = _gelu(yr)
        ygb = yg.astype(BF16)
        sg = _sigmoid(jnp.dot(ygb, w_ref[...], preferred_element_type=F32) + b_ref[...])
        dyb_ = d_ref[...]
        dz = dyb_ * yg * sg * (1.0 - sg)
        dzb = dz.astype(BF16)
        dyg = dyb_ * sg + lax.dot_general(dzb, w_ref[...], (((1,), (1,)), ((), ())), preferred_element_type=F32)
        dw_ref[...] += lax.dot_general(ygb, dzb, (((0,), (0,)), ((), ())), preferred_element_type=F32)
        acc_b[...] += jnp.sum(dz.reshape(tr // SUBLANES, SUBLANES, C), axis=0)
        dy_ref[...] = dyg * _gelu_grad(yr)

        @pl.when(i == nsteps - 1)
        def _():
            db_ref[...] = jnp.sum(acc_b[...], axis=0, keepdims=True)

    row = pl.BlockSpec((tr, C), lambda i: (i, 0))
    return pl.pallas_call(
        body, name=name,
        out_shape=(jax.ShapeDtypeStruct((L, C), F32), jax.ShapeDtypeStruct((C, C), F32),
                   jax.ShapeDtypeStruct((1, C), F32)),
        grid=(nsteps,),
        in_specs=[row, row, pl.BlockSpec((C, C), lambda i: (0, 0)), pl.BlockSpec((1, C), lambda i: (0, 0))],
        out_specs=(row, pl.BlockSpec((C, C), lambda i: (0, 0)), pl.BlockSpec((1, C), lambda i: (0, 0))),
        scratch_shapes=[pltpu.VMEM((SUBLANES, C), F32)], compiler_params=_cparams(("arbitrary",)),
    )(yraw, dyb, wmat, bias.reshape(1, C))


def _pool_counts(L, g):
    t = lax.broadcasted_iota(jnp.int32, (L, LANES), 0).astype(F32) + 1.0
    w = jnp.where(g == 0, 2.0, jnp.where(g == 1, 4.0, jnp.where(g == 2, 8.0, 16.0)))
    return 1.0 / jnp.minimum(t, w)


def _select_window(g, a2, a4, a8, a16):
    return jnp.where(g == 0, a2, jnp.where(g == 1, a4, jnp.where(g == 2, a8, a16)))


def _pooled(z, g):
    a2 = z + _down(z, 1)
    a4 = a2 + _down(a2, 2)
    a8 = a4 + _down(a4, 4)
    a16 = a8 + _down(a8, 8)
    return _select_window(g, a2, a4, a8, a16) * _pool_counts(z.shape[0], g) - z


def _pool_fwd(proj3, pool_w, scale, name):
    _, L, C = proj3.shape
    ng = len(POOL_WINDOWS)
    pg = C // ng
    assert pg == LANES

    def body(z_ref, w_ref, s_ref, o_ref):
        g = pl.program_id(0)
        p = _pooled(z_ref[...], g)
        y = jnp.dot(p.astype(BF16), w_ref[...].astype(BF16), preferred_element_type=F32)
        o_ref[...] = (y * s_ref[...]).astype(BF16)

    return pl.pallas_call(
        body, name=name, out_shape=jax.ShapeDtypeStruct((L, C), BF16), grid=(ng,),
        in_specs=[pl.BlockSpec((None, L, pg), lambda g: (0, 0, g)), pl.BlockSpec((None, pg, pg), lambda g: (g, 0, 0)),
                  pl.BlockSpec((1, pg), lambda g: (0, g))],
        out_specs=pl.BlockSpec((L, pg), lambda g: (0, g)), compiler_params=_cparams(("parallel",)),
    )(proj3, pool_w, scale.reshape(1, C))


def _pool_bwd(proj3, dmix, pool_w, scale, name):
    _, L, C = proj3.shape
    ng = len(POOL_WINDOWS)
    pg = C // ng

    def body(z_ref, d_ref, w_ref, s_ref, dz_ref, dw_ref, ds_ref):
        g = pl.program_id(0)
        p = _pooled(z_ref[...], g)
        pb = p.astype(BF16)
        wb = w_ref[...].astype(BF16)
        pre = jnp.dot(pb, wb, preferred_element_type=F32)
        dyc = d_ref[...]
        ds_ref[...] = jnp.sum(dyc * pre, axis=0, keepdims=True)
        dpre = (dyc * s_ref[...]).astype(BF16)
        dw_ref[...] = lax.dot_general(pb, dpre, (((0,), (0,)), ((), ())), preferred_element_type=F32)
        dp = lax.dot_general(dpre, wb, (((1,), (1,)), ((), ())), preferred_element_type=F32)
        v = dp * _pool_counts(L, g)
        a2 = v + _up(v, 1)
        a4 = a2 + _up(a2, 2)
        a8 = a4 + _up(a4, 4)
        a16 = a8 + _up(a8, 8)
        dz_ref[...] = (_select_window(g, a2, a4, a8, a16) - dp).astype(BF16)

    return pl.pallas_call(
        body, name=name,
        out_shape=(jax.ShapeDtypeStruct((L, C), BF16), jax.ShapeDtypeStruct((ng, pg, pg), F32),
                   jax.ShapeDtypeStruct((1, C), F32)),
        grid=(ng,),
        in_specs=[pl.BlockSpec((None, L, pg), lambda g: (0, 0, g)), pl.BlockSpec((L, pg), lambda g: (0, g)),
                  pl.BlockSpec((None, pg, pg), lambda g: (g, 0, 0)), pl.BlockSpec((1, pg), lambda g: (0, g))],
        out_specs=(pl.BlockSpec((L, pg), lambda g: (0, g)), pl.BlockSpec((None, pg, pg), lambda g: (g, 0, 0)),
                   pl.BlockSpec((1, pg), lambda g: (0, g))),
        compiler_params=_cparams(("parallel",)),
    )(proj3, dmix, pool_w, scale.reshape(1, C))


def _tril_w(w_ref, h):
    r = lax.broadcasted_iota(jnp.int32, (CHUNK, CHUNK), 0)
    c = lax.broadcasted_iota(jnp.int32, (CHUNK, CHUNK), 1)
    return jnp.where(r >= c, w_ref[h], 0.0)


def _sgu_fwd(proj3, norm_g, w, b, name):
    _, L, C = proj3.shape
    nh = w.shape[0]
    dh = C // nh
    assert dh == LANES and w.shape[1] == CHUNK
    tr = _pick(L, prefs=(512, 256, 128))
    bfull = jnp.broadcast_to(b[:, :, None], (nh, CHUNK, dh))

    def body(su_ref, sv_ref, g_ref, w_ref, b_ref, o_ref):
        sv = _gelu(sv_ref[...])
        r = lax.rsqrt(jnp.mean(sv * sv, axis=-1, keepdims=True) + EPS)
        v = (sv * r * g_ref[...]).astype(BF16)
        for h in range(nh):
            wm = _tril_w(w_ref, h).astype(BF16)
            cols = slice(h * dh, (h + 1) * dh)
            for n in range(tr // CHUNK):
                rows = slice(n * CHUNK, (n + 1) * CHUNK)
                mixed = jnp.dot(wm, v[rows, cols], preferred_element_type=F32) + b_ref[h]
                o_ref[rows, cols] = (_gelu(su_ref[rows, cols]) * mixed).astype(BF16)

    full = lambda shp: pl.BlockSpec(shp, lambda i: (0,) * len(shp))
    return pl.pallas_call(
        body, name=name, out_shape=jax.ShapeDtypeStruct((L, C), BF16), grid=(L // tr,),
        in_specs=[pl.BlockSpec((None, tr, C), lambda i: (1, i, 0)), pl.BlockSpec((None, tr, C), lambda i: (2, i, 0)),
                  full((1, C)), full((nh, CHUNK, CHUNK)), full((nh, CHUNK, dh))],
        out_specs=pl.BlockSpec((tr, C), lambda i: (i, 0)), compiler_params=_cparams(("parallel",)),
    )(proj3, proj3, norm_g.reshape(1, C), w, bfull)


def _sgu_bwd(proj3, dmix, norm_g, w, b, name):
    _, L, C = proj3.shape
    nh = w.shape[0]
    dh = C // nh
    tr = _pick(L, prefs=(512, 256, 128))
    nsteps = L // tr
    bfull = jnp.broadcast_to(b[:, :, None], (nh, CHUNK, dh))

    def body(su_ref, sv_ref, d_ref, g_ref, w_ref, b_ref, o_ref, dw_ref, db_ref, dg_ref, dv_ref, acc_g):
        i = pl.program_id(0)

        @pl.when(i == 0)
        def _():
            dw_ref[...] = jnp.zeros_like(dw_ref)
            db_ref[...] = jnp.zeros_like(db_ref)
            acc_g[...] = jnp.zeros_like(acc_g)

        svp = sv_ref[...]
        sv = _gelu(svp)
        r = lax.rsqrt(jnp.mean(sv * sv, axis=-1, keepdims=True) + EPS)
        vh = sv * r
        gv = g_ref[...]
        v = (vh * gv).astype(BF16)
        tri_r = lax.broadcasted_iota(jnp.int32, (CHUNK, CHUNK), 0)
        tri_c = lax.broadcasted_iota(jnp.int32, (CHUNK, CHUNK), 1)
        for h in range(nh):
            wm = _tril_w(w_ref, h).astype(BF16)
            cols = slice(h * dh, (h + 1) * dh)
            dwh = jnp.zeros((CHUNK, CHUNK), F32)
            dbh = jnp.zeros((CHUNK, dh), F32)
            for n in range(tr // CHUNK):
                rows = slice(n * CHUNK, (n + 1) * CHUNK)
                vb = v[rows, cols]
                mixed = jnp.dot(wm, vb, preferred_element_type=F32) + b_ref[h]
                sup = su_ref[rows, cols]
                dyd = d_ref[rows, cols]
                dmx = dyd * _gelu(sup)
                o_ref[0, rows, cols] = (dyd * mixed * _gelu_grad(sup)).astype(BF16)
                dmb = dmx.astype(BF16)
                dwh = dwh + lax.dot_general(dmb, vb, (((1,), (1,)), ((), ())), preferred_element_type=F32)
                dbh = dbh + dmx
                dv_ref[rows, cols] = lax.dot_general(wm, dmb, (((0,), (0,)), ((), ())), preferred_element_type=F32)
            dw_ref[h] += jnp.where(tri_r >= tri_c, dwh, 0.0)
            db_ref[h] += dbh
        dv = dv_ref[...]
        acc_g[...] += jnp.sum((dv * vh).reshape(tr // SUBLANES, SUBLANES, C), axis=0)
        dvg = dv * gv
        dsv = r * (dvg - vh * jnp.mean(dvg * vh, axis=-1, keepdims=True))
        o_ref[1] = (dsv * _gelu_grad(svp)).astype(BF16)

        @pl.when(i == nsteps - 1)
        def _():
            dg_ref[...] = jnp.sum(acc_g[...], axis=0, keepdims=True)

    full = lambda shp: pl.BlockSpec(shp, lambda i: (0,) * len(shp))
    return pl.pallas_call(
        body, name=name,
        out_shape=(jax.ShapeDtypeStruct((2, L, C), BF16), jax.ShapeDtypeStruct((nh, CHUNK, CHUNK), F32),
                   jax.ShapeDtypeStruct((nh, CHUNK, dh), F32), jax.ShapeDtypeStruct((1, C), F32)),
        grid=(nsteps,),
        in_specs=[pl.BlockSpec((None, tr, C), lambda i: (1, i, 0)), pl.BlockSpec((None, tr, C), lambda i: (2, i, 0)),
                  pl.BlockSpec((tr, C), lambda i: (i, 1)), full((1, C)), full((nh, CHUNK, CHUNK)),
                  full((nh, CHUNK, dh))],
        out_specs=(pl.BlockSpec((2, tr, C), lambda i: (0, i, 0)), full((nh, CHUNK, CHUNK)), full((nh, CHUNK, dh)),
                   full((1, C))),
        scratch_shapes=[pltpu.VMEM((tr, C), F32), pltpu.VMEM((SUBLANES, C), F32)],
        compiler_params=_cparams(("arbitrary",)),
    )(proj3, proj3, dmix, norm_g.reshape(1, C), w, bfull)


def _ffn_act_fwd(up3, conv_w, conv_b, name):
    _, L, Fh = up3.shape
    cb = LANES
    w2 = conv_w.reshape(3, 2, Fh).transpose(1, 0, 2)
    b2 = conv_b.reshape(2, 1, Fh)

    def body(u_ref, w_ref, b_ref, o_ref, ot_ref):
        g = _conv3(u_ref[0].astype(F32), w_ref[0]) + b_ref[0]
        v = _conv3(u_ref[1].astype(F32), w_ref[1]) + b_ref[1]
        a = g * _sigmoid(g) * v
        o_ref[...] = a.astype(BF16)
        ot_ref[...] = a.T.astype(BF16)

    return pl.pallas_call(
        body, name=name, out_shape=(jax.ShapeDtypeStruct((L, Fh), BF16), jax.ShapeDtypeStruct((Fh, L), BF16)),
        grid=(Fh // cb,),
        in_specs=[pl.BlockSpec((2, L, cb), lambda j: (0, 0, j)), pl.BlockSpec((2, 3, cb), lambda j: (0, 0, j)),
                  pl.BlockSpec((2, 1, cb), lambda j: (0, 0, j))],
        out_specs=(pl.BlockSpec((L, cb), lambda j: (0, j)), pl.BlockSpec((cb, L), lambda j: (j, 0))),
        compiler_params=_cparams(("parallel",)),
    )(up3, w2, b2)


def _ffn_act_bwd(up3, da, conv_w, conv_b, name):
    _, L, Fh = up3.shape
    cb = LANES
    w2 = conv_w.reshape(3, 2, Fh).transpose(1, 0, 2)
    b2 = conv_b.reshape(2, 1, Fh)

    def body(u_ref, d_ref, w_ref, b_ref, o_ref, dw_ref, db_ref):
        ug, uv = u_ref[0].astype(F32), u_ref[1].astype(F32)
        wg, wv = w_ref[0], w_ref[1]
        g = _conv3(ug, wg) + b_ref[0]
        v = _conv3(uv, wv) + b_ref[1]
        sg = _sigmoid(g)
        dav = d_ref[...].astype(F32)
        dg = dav * v * (sg * (1.0 + g * (1.0 - sg)))
        dv = dav * (g * sg)
        o_ref[0] = _conv3_t(dg, wg).astype(BF16)
        o_ref[1] = _conv3_t(dv, wv).astype(BF16)
        for tap, (dwg, dwv) in enumerate(zip(_conv3_dw(dg, ug), _conv3_dw(dv, uv))):
            dw_ref[0, tap:tap + 1, :] = dwg
            dw_ref[1, tap:tap + 1, :] = dwv
        db_ref[0] = jnp.sum(dg, axis=0, keepdims=True)
        db_ref[1] = jnp.sum(dv, axis=0, keepdims=True)

    dup, dw2, db2 = pl.pallas_call(
        body, name=name,
        out_shape=(jax.ShapeDtypeStruct((2, L, Fh), BF16), jax.ShapeDtypeStruct((2, 3, Fh), F32),
                   jax.ShapeDtypeStruct((2, 1, Fh), F32)),
        grid=(Fh // cb,),
        in_specs=[pl.BlockSpec((2, L, cb), lambda j: (0, 0, j)), pl.BlockSpec((L, cb), lambda j: (0, j)),
                  pl.BlockSpec((2, 3, cb), lambda j: (0, 0, j)), pl.BlockSpec((2, 1, cb), lambda j: (0, 0, j))],
        out_specs=(pl.BlockSpec((2, L, cb), lambda j: (0, 0, j)), pl.BlockSpec((2, 3, cb), lambda j: (0, 0, j)),
                   pl.BlockSpec((2, 1, cb), lambda j: (0, 0, j))),
        compiler_params=_cparams(("parallel",)),
    )(up3, da, w2, b2)
    return dup, dw2.transpose(1, 0, 2).reshape(3, 2 * Fh), db2.reshape(2 * Fh)


def _local_step(x, tgt, w, layer_weights, on_layer_grads):
    L, D = x.shape
    depth = w['norm_mix_g'].shape[0]
    saved = []
    for i in range(depth):
        j = i // 2
        wb = dict(layer_weights(2 * i, x))
        s = {'x': x, 'wb': wb}
        h, s['hT'] = _rms_fwd(x, w['norm_mix_g'][i], "mix_norm_fwd")
        if i % 2 == 0:
            proj4 = _mm(h, wb['even_w_in'], 'nn', F32, "even_in_fwd", ok=('seg', 4))
            s['proj'] = proj4
            ya = _sconv_fwd(proj4, w['even_conv_w'][j], "sconv_fwd")
            prm = (w['ssm_log_step'][j], w['ssm_a_re'][j], w['ssm_a_im'][j], w['ssm_b_re'][j], w['ssm_b_im'][j],
                   w['ssm_c_re'][j], w['ssm_c_im'][j])
            (lr, li, bmat, cmat), prep_vjp = jax.vjp(_s5_prep, *prm)
            u = _to_scan_order(proj4[3])
            yraw, s_re, s_im = _s5_fwd(u, lr, li, bmat, cmat, w['ssm_d'][j], "s5_fwd")
            yb = _glu_fwd(yraw, wb['ssm_glu_w'], w['ssm_glu_b'][j], "glu_fwd")
            s.update(u=u, yraw=yraw, s_re=s_re, s_im=s_im, s5=(lr, li, bmat, cmat), prep_vjp=prep_vjp)
            mixin = jnp.concatenate([ya, _from_scan_order(yb)], axis=1)
            x = _mm(mixin, wb['even_w_out'], 'nn', F32, "even_out_fwd", res=x)
        else:
            proj3 = _mm(h, wb['odd_w_in'], 'nn', F32, "odd_in_fwd", ok=('seg', 3))
            s['proj'] = proj3
            yc = _pool_fwd(proj3, w['pool_w'][j], w['pool_scale'][j], "pool_fwd")
            yd = _sgu_fwd(proj3, w['sgu_norm_g'][j], w['sgu_w'][j], w['sgu_b'][j], "sgu_fwd")
            mixin = jnp.concatenate([yc, yd], axis=1)
            x = _mm(mixin, wb['odd_w_out'], 'nn', F32, "odd_out_fwd", res=x)
        s['mixin'] = mixin
        s['x1'] = x
        wb.update(layer_weights(2 * i + 1, x))
        h2, h2t = _rms_fwd(x, w['norm_ffn_g'][i], "ffn_norm_fwd")
        up3 = _mm(h2, wb['ffn_w_up'], 'nn', BF16, "ffn_up_fwd", ok=('seg', 2))
        a, at = _ffn_act_fwd(up3, w['ffn_conv_w'][i], w['ffn_conv_b'][i], "ffn_act_fwd")
        x = _mm(a, wb['ffn_w_down'], 'nn', F32, "ffn_down_fwd", res=x)
        s.update(h2T=h2t, up3=up3, aT=at)
        saved.append(s)

    loss8, dx, dg_final = _loss_head(x, w['norm_final_g'], tgt)
    gs = {n: [None] * w[n].shape[0] for n in SMALL if n != 'norm_final_g'}
    gs['norm_final_g'] = dg_final.reshape(D)

    dep = None
    for i in reversed(range(depth)):
        j = i // 2
        s = saved[i]
        wb = s['wb']
        gb = {}
        da = _mm(dx, wb['ffn_w_down'], 'nt', BF16, "ffn_down_dgrad", dep=dep)
        gb['ffn_w_down'] = _mm(s['aT'], dx, 'nn', BF16, "ffn_down_wgrad")
        dup3, dcw, dcb = _ffn_act_bwd(s['up3'], da, w['ffn_conv_w'][i], w['ffn_conv_b'][i], "ffn_act_bwd")
        gs['ffn_conv_w'][i], gs['ffn_conv_b'][i] = dcw, dcb
        gb['ffn_w_up'] = _mm(s['h2T'], dup3, 'nn', BF16, "ffn_up_wgrad", bk=('seg', 2))
        dh2 = _mm(dup3, wb['ffn_w_up'], 'nt', F32, "ffn_up_dgrad", ak=('seg', 2))
        dx, dg = _rms_bwd(s['x1'], w['norm_ffn_g'][i], dh2, dx, "ffn_norm_bwd")
        gs['norm_ffn_g'][i] = dg.reshape(D)
        dep = on_layer_grads(2 * i + 1, gb)
        gb = {}
        if i % 2 == 0:
            dmix = _mm(dx, wb['even_w_out'], 'nt', F32, "even_out_dgrad", dep=dep)
            gb['even_w_out'] = _mm(s['mixin'].T, dx, 'nn', BF16, "even_out_wgrad")
            dpc, dcw = _sconv_bwd(s['proj'], dmix, w['even_conv_w'][j], "sconv_bwd")
            gs['even_conv_w'][j] = dcw
            dyb = _to_scan_order(dmix[:, D // 2:])
            dyraw, dglu_w, dglu_b = _glu_bwd(s['yraw'], dyb, wb['ssm_glu_w'], w['ssm_glu_b'][j], "glu_bwd")
            gb['ssm_glu_w'] = dglu_w.astype(BF16)
            gs['ssm_glu_b'][j] = dglu_b.reshape(-1)
            lr, li, bmat, cmat = s['s5']
            du, dbm, dcm, dlam, dd = _s5_bwd(dyraw, s['u'], s['s_re'], s['s_im'], lr, li, bmat, cmat,
                                            w['ssm_d'][j], "s5_bwd")
            gs['ssm_d'][j] = dd.reshape(-1)
            dcm = jnp.swapaxes(dcm, 1, 2)
            dprm = s['prep_vjp']((dlam[:, 0:1, :], dlam[:, 1:2, :], dbm, dcm))
            for n, gval in zip(('ssm_log_step', 'ssm_a_re', 'ssm_a_im', 'ssm_b_re', 'ssm_b_im', 'ssm_c_re',
                                'ssm_c_im'), dprm):
                gs[n][j] = gval
            dproj = jnp.concatenate([dpc, _from_scan_order(du).astype(BF16)[None]], axis=0)
            gb['even_w_in'] = _mm(s['hT'], dproj, 'nn', BF16, "even_in_wgrad", bk=('seg', 4))
            dh = _mm(dproj, wb['even_w_in'], 'nt', F32, "even_in_dgrad", ak=('seg', 4))
        else:
            dmix = _mm(dx, wb['odd_w_out'], 'nt', F32, "odd_out_dgrad", dep=dep)
            gb['odd_w_out'] = _mm(s['mixin'].T, dx, 'nn', BF16, "odd_out_wgrad")
            dz, dpw, dps = _pool_bwd(s['proj'], dmix, w['pool_w'][j], w['pool_scale'][j], "pool_bwd")
            gs['pool_w'][j], gs['pool_scale'][j] = dpw, dps.reshape(-1)
            dsuv, dsw, dsb, dsg = _sgu_bwd(s['proj'], dmix, w['sgu_norm_g'][j], w['sgu_w'][j], w['sgu_b'][j],
                                           "sgu_bwd")
            gs['sgu_w'][j], gs['sgu_b'][j], gs['sgu_norm_g'][j] = dsw, jnp.sum(dsb, axis=-1), dsg.reshape(-1)
            dproj = jnp.concatenate([dz[None], dsuv], axis=0)
            gb['odd_w_in'] = _mm(s['hT'], dproj, 'nn', BF16, "odd_in_wgrad", bk=('seg', 3))
            dh = _mm(dproj, wb['odd_w_in'], 'nt', F32, "odd_in_dgrad", ak=('seg', 3))
        dx, dg = _rms_bwd(s['x'], w['norm_mix_g'][i], dh, dx, "mix_norm_bwd")
        gs['norm_mix_g'][i] = dg.reshape(D)
        dep = on_layer_grads(2 * i, gb)

    gsmall = {n: (v if n == 'norm_final_g' else jnp.stack(v)) for n, v in gs.items()}
    return loss8[0, 0], dx, gsmall


_HBM = pl.BlockSpec(memory_space=pltpu.HBM)
_CHIP_FLIPS = ((0, 0), (1, 0), (0, 1), (1, 1))


def _coords():
    return lax.axis_index("x"), lax.axis_index("y"), lax.axis_index("c")


def _flip(v, f):
    return 1 - v if f else v


def _shard_of(ref, axis, s, width):
    start = pl.multiple_of(s * width, LANES if axis == ref.ndim - 1 else 16) if width % 16 == 0 else s * width
    idx = [slice(None)] * ref.ndim
    idx[axis] = pl.ds(start, width)
    return ref.at[tuple(idx)]


_SEM = pl.BlockSpec(memory_space=pltpu.SEMAPHORE)
_ANY = pl.BlockSpec(memory_space=pl.ANY)
_DATAFLOW = pltpu.SideEffectType.DATAFLOW_SIDE_EFFECTING


def _in_hbm(a):
    return pltpu.with_memory_space_constraint(a, pltpu.HBM)


def _model_layer(name, l):
    if name.startswith('ffn'):
        return l
    return 2 * l + 1 if name.startswith('odd') else 2 * l


def _place_quarter(shard, l, axis, chip):
    _, r, c = shard.shape
    tr = _pick(r, prefs=(512, 256, 128, 64, 32, 16))
    nrb = r // tr

    def body(chip_ref, i_ref, o_ref):
        o_ref[...] = i_ref[...]

    if axis == 1:
        out_shape, o_map = (r, c * N_CHIPS), (lambda i, s: (i, s[0]))
    else:
        out_shape, o_map = (r * N_CHIPS, c), (lambda i, s: (s[0] * nrb + i, 0))
    return pl.pallas_call(
        body, name="place_quarter", out_shape=jax.ShapeDtypeStruct(out_shape, shard.dtype),
        grid_spec=pltpu.PrefetchScalarGridSpec(
            num_scalar_prefetch=1, grid=(nrb,), in_specs=[pl.BlockSpec((None, tr, c), lambda i, s: (l, i, 0))],
            out_specs=pl.BlockSpec((tr, c), o_map)),
        compiler_params=_cparams(("parallel",)),
    )(chip, shard)


def _gather_copies(shard_refs, land_refs, sems, pairs, axes, group, g, landing_chip_of):
    x, y, c = _coords()
    out = []
    for j, k in enumerate(group):
        t, l = pairs[k]
        width = shard_refs[t].shape[axes[t] + 1]
        for f in (1, 2, 3):
            fx, fy = _CHIP_FLIPS[f]
            px, py = _flip(x, fx), _flip(y, fy)
            lx, ly = landing_chip_of(px, py)
            out.append(pltpu.make_async_remote_copy(
                src_ref=shard_refs[t].at[l], dst_ref=_shard_of(land_refs[k], axes[t], 2 * lx + ly, width),
                send_sem=sems[2 * g].at[3 * j + f - 1], recv_sem=sems[2 * g + 1].at[3 * j + f - 1],
                device_id=(px, py, c), device_id_type=MESH))
    return out


def _gather_start(shards, lands, pairs, axes, groups):
    nt, npair, ng = len(shards), len(pairs), len(groups)

    def body(*refs):
        shard_refs, land_refs = refs[:nt], refs[nt:nt + npair]
        sems = refs[nt + npair:nt + npair + 2 * ng]
        token = refs[-1]
        x, y, _ = _coords()
        for g, group in enumerate(groups):
            for cp in _gather_copies(shard_refs, land_refs, sems, pairs, axes, group, g, lambda px, py: (x, y)):
                cp.start()
        token[...] = jnp.zeros_like(token)

    sem_shapes = []
    for group in groups:
        sem_shapes += [pltpu.SemaphoreType.DMA((3 * len(group),)), pltpu.SemaphoreType.DMA((3 * len(group),))]
    thru = [pltpu.HBM(a.shape, a.dtype) for a in list(shards) + list(lands)]
    outs = pl.pallas_call(
        body, name="gather_start",
        out_shape=tuple(sem_shapes + thru + [jax.ShapeDtypeStruct((SUBLANES, LANES), F32)]),
        in_specs=[_HBM] * (nt + npair),
        out_specs=tuple([_SEM] * (2 * ng) + [_HBM] * (nt + npair) + [pl.BlockSpec(memory_space=pltpu.VMEM)]),
        input_output_aliases={i: 2 * ng + i for i in range(nt + npair)},
        compiler_params=pltpu.CompilerParams(has_side_effects=_DATAFLOW),
    )(*[_in_hbm(a) for a in list(shards) + list(lands)])
    sems = outs[:2 * ng]
    return sems, list(outs[2 * ng:2 * ng + nt]), list(outs[2 * ng + nt:2 * ng + nt + npair]), outs[-1]


def _gather_wait(g, shards, lands_g, send_sem, recv_sem, after, pairs, axes, group):
    nt, n = len(shards), len(group)

    def body(*refs):
        shard_refs, land_g = refs[:nt], refs[nt:nt + n]
        sems = {2 * g: refs[nt + n], 2 * g + 1: refs[nt + n + 1]}
        land_refs = {k: land_g[j] for j, k in enumerate(group)}
        for cp in _gather_copies(shard_refs, land_refs, sems, pairs, axes, group, g, lambda px, py: (px, py)):
            cp.wait_send()
            cp.wait_recv()

    thru = [pltpu.HBM(a.shape, a.dtype) for a in list(shards) + list(lands_g)]
    outs = pl.pallas_call(
        body, name=f"gather_wait_{g}", out_shape=tuple(thru),
        in_specs=[_HBM] * (nt + n) + [_SEM, _SEM, _ANY], out_specs=tuple([_HBM] * (nt + n)),
        input_output_aliases={i: i for i in range(nt + n)},
        compiler_params=pltpu.CompilerParams(has_side_effects=_DATAFLOW),
    )(*shards, *lands_g, send_sem, recv_sem, after)
    return list(outs[:nt]), list(outs[nt:])


N_SLOTS = N_DEV - 1


def _scatter_sends(grad_refs, land_refs, send_sem, recv_sem, meta):
    x, y, c = _coords()
    out = []
    for j, (axis, owner, q, width) in enumerate(meta):
        other = c if owner == 0 else 1 - c
        for f, (fx, fy) in enumerate(_CHIP_FLIPS):
            px, py = _flip(x, fx), _flip(y, fy)
            slot = f + 4 * other - 1
            out.append((other if f == 0 else None, pltpu.make_async_remote_copy(
                src_ref=_shard_of(grad_refs[j], axis, 2 * px + py, width), dst_ref=land_refs[j].at[q, slot],
                send_sem=send_sem.at[4 * j + f], recv_sem=recv_sem.at[N_SLOTS * j + slot],
                device_id=(px, py, owner), device_id_type=MESH)))
    return out


def _scatter_start(layer, grads, lands, meta):
    n = len(grads)
    uniq = []
    for a in lands:
        if not any(a is u for u in uniq):
            uniq.append(a)
    which = [next(k for k, u in enumerate(uniq) if u is a) for a in lands]
    nu = len(uniq)

    def body(*refs):
        grad_refs, land_u = refs[:n], refs[n:n + nu]
        send_sem, recv_sem = refs[n + nu], refs[n + nu + 1]
        for other, cp in _scatter_sends(grad_refs, [land_u[k] for k in which], send_sem, recv_sem, meta):
            if other is None:
                cp.start()
            else:
                pl.when(other == 1)(cp.start)
        refs[-1][...] = jnp.zeros_like(refs[-1])

    thru = [pltpu.HBM(a.shape, a.dtype) for a in list(grads) + uniq]
    outs = pl.pallas_call(
        body, name=f"scatter_start_{layer}",
        out_shape=tuple([pltpu.SemaphoreType.DMA((4 * n,)), pltpu.SemaphoreType.DMA((N_SLOTS * n,))] + thru
                        + [jax.ShapeDtypeStruct((SUBLANES, LANES), F32)]),
        in_specs=[_HBM] * (n + nu),
        out_specs=tuple([_SEM, _SEM] + [_HBM] * (n + nu) + [pl.BlockSpec(memory_space=pltpu.VMEM)]),
        input_output_aliases={i: 2 + i for i in range(n + nu)},
        compiler_params=pltpu.CompilerParams(has_side_effects=_DATAFLOW),
    )(*[_in_hbm(a) for a in list(grads) + uniq])
    new_lands = [outs[2 + n + k] for k in which]
    return outs[0], outs[1], list(outs[2:2 + n]), new_lands, outs[-1]


def _scatter_wait(started, lands):
    nl = len(lands)
    flat_grads = [g for s in started for g in s[2]]
    ng, ns = len(flat_grads), len(started)

    def body(*refs):
        land_refs = refs[:nl]
        grad_refs = refs[nl:nl + ng]
        sem_refs = refs[nl + ng:nl + ng + 2 * ns]
        _, _, c = _coords()
        off = 0
        for k, (_, _, grads, idx, meta) in enumerate(started):
            send_sem, recv_sem = sem_refs[2 * k], sem_refs[2 * k + 1]
            lr = [land_refs[i] for i in idx]
            for other, cp in _scatter_sends(grad_refs[off:off + len(grads)], lr, send_sem, recv_sem, meta):
                if other is None:
                    cp.wait_send()
                else:
                    pl.when(other == 1)(cp.wait_send)
            for j, (axis, owner, q, width) in enumerate(meta):
                mine = (c if owner == 0 else 1 - c) == 0

                @pl.when(mine)
                def _():
                    for slot in range(N_SLOTS):
                        land = lr[j].at[q, slot]
                        pltpu.make_async_remote_copy(
                            src_ref=land, dst_ref=land, send_sem=send_sem.at[0], recv_sem=recv_sem.at[N_SLOTS * j + slot],
                            device_id=_coords(), device_id_type=MESH).wait_recv()
            off += len(grads)

    args = list(lands) + flat_grads
    thru = [pltpu.HBM(a.shape, a.dtype) for a in args]
    sems = [s for st in started for s in st[:2]]
    outs = pl.pallas_call(
        body, name="scatter_wait", out_shape=tuple(thru), in_specs=[_HBM] * (nl + ng) + [_SEM] * (2 * ns),
        out_specs=tuple([_HBM] * (nl + ng)), input_output_aliases={i: i for i in range(nl + ng)},
        compiler_params=pltpu.CompilerParams(has_side_effects=_DATAFLOW),
    )(*args, *sems)
    return list(outs[:nl]), list(outs[nl:])


def _sum_and_share(recv, layer_grads, axis, chip, name):
    n, ns, r, c = recv.shape
    tr = _pick(r, prefs=(256, 128, 64, 32, 16))
    nr = r // tr
    nsteps = n * nr
    nlay = len(layer_grads)
    own_map = (lambda h, i, s: (i, s[0])) if axis == 1 else (lambda h, i, s: (s[0] * nr + i, 0))

    def body(chip_ref, i_ref, *rest):
        g_refs = rest[:nlay]
        o_ref, buf, loc_sems, send_sems, recv_sems = rest[nlay:]
        h, i = pl.program_id(0), pl.program_id(1)
        step = h * nr + i
        slot = step % 2
        x, y, core = _coords()
        layer = core * n + h
        own = g_refs[0][...]
        for l in range(1, nlay):
            own = jnp.where(layer == l, g_refs[l][...], own)

        def copies(sl):
            dst = o_ref.at[core * n + h, pl.ds(pl.multiple_of(i * tr, tr), tr), :]
            loc = pltpu.make_async_copy(buf.at[sl], dst, loc_sems.at[sl])
            rem = pltpu.make_async_remote_copy(
                src_ref=buf.at[sl], dst_ref=dst, send_sem=send_sems.at[sl], recv_sem=recv_sems.at[step],
                device_id=(x, y, 1 - core), device_id_type=MESH)
            return loc, rem

        def drain(sl):
            loc, rem = copies(sl)
            loc.wait()
            rem.wait_send()

        pl.when(step >= 2)(lambda: drain(slot))
        acc = own.astype(F32)
        for s in range(ns):
            acc = acc + i_ref[s].astype(F32)
        buf[slot] = acc
        loc, rem = copies(slot)
        loc.start()
        rem.start()

        @pl.when(step == nsteps - 1)
        def _():
            drain(slot)
            if nsteps > 1:
                drain(1 - slot)
            for hh in range(n):
                for ii in range(nr):
                    land = o_ref.at[(1 - core) * n + hh, pl.ds(ii * tr, tr), :]
                    pltpu.make_async_remote_copy(
                        src_ref=buf.at[0], dst_ref=land, send_sem=send_sems.at[0], recv_sem=recv_sems.at[hh * nr + ii],
                        device_id=(x, y, 1 - core), device_id_type=MESH).wait_recv()

    return pl.pallas_call(
        body, name=name, out_shape=jax.ShapeDtypeStruct((2 * n, r, c), F32),
        grid_spec=pltpu.PrefetchScalarGridSpec(
            num_scalar_prefetch=1, grid=(n, nr),
            in_specs=[pl.BlockSpec((None, ns, tr, c), lambda h, i, s: (h, 0, i, 0))]
            + [pl.BlockSpec((tr, c), own_map)] * nlay,
            out_specs=_HBM,
            scratch_shapes=[pltpu.VMEM((2, tr, c), F32), pltpu.SemaphoreType.DMA((2,)),
                            pltpu.SemaphoreType.DMA((2,)), pltpu.SemaphoreType.DMA((nsteps,))]),
        compiler_params=_cparams(("arbitrary", "arbitrary")),
    )(chip, recv, *layer_grads)


def _gather_sums_over_chips(part):
    def body(i_ref, o_ref, send_sems, recv_sems):
        x, y, c = _coords()
        o_ref[2 * x + y] = i_ref[...]

        def copy(f, slot_chip):
            fx, fy = _CHIP_FLIPS[f]
            return pltpu.make_async_remote_copy(
                src_ref=i_ref, dst_ref=o_ref.at[2 * slot_chip[0] + slot_chip[1]], send_sem=send_sems.at[f - 1],
                recv_sem=recv_sems.at[f - 1], device_id=(_flip(x, fx), _flip(y, fy), c), device_id_type=MESH)

        sends = [copy(f, (x, y)) for f in (1, 2, 3)]
        for cp in sends:
            cp.start()
        for f in (1, 2, 3):
            fx, fy = _CHIP_FLIPS[f]
            copy(f, (_flip(x, fx), _flip(y, fy))).wait_recv()
        for cp in sends:
            cp.wait_send()

    vmem = pl.BlockSpec(memory_space=pltpu.VMEM)
    return pl.pallas_call(
        body, name="gather_small_sums", out_shape=jax.ShapeDtypeStruct((N_CHIPS,) + part.shape, part.dtype),
        in_specs=[vmem], out_specs=vmem,
        scratch_shapes=[pltpu.SemaphoreType.DMA((3,)), pltpu.SemaphoreType.DMA((3,))],
    )(part)


def _adamw(w, g, m, v, name):
    R, C = w.shape
    tr = _pick(R, prefs=(256, 128, 64, 32, 16, 8))
    bc1 = 1.0 - ADAM_B1 ** ADAM_STEP
    bc2 = 1.0 - ADAM_B2 ** ADAM_STEP

    def body(w_ref, g_ref, m_ref, v_ref, d_ref, mo_ref, vo_ref):
        gv = g_ref[...]
        mn = ADAM_B1 * m_ref[...] + (1.0 - ADAM_B1) * gv
        vn = ADAM_B2 * v_ref[...] + (1.0 - ADAM_B2) * (gv * gv)
        d_ref[...] = -ADAM_LR * ((mn / bc1) / (jnp.sqrt(vn / bc2) + ADAM_EPS) + ADAM_WD * w_ref[...])
        mo_ref[...] = mn
        vo_ref[...] = vn

    blk = pl.BlockSpec((tr, C), lambda i: (i, 0))
    sds = jax.ShapeDtypeStruct((R, C), F32)
    return pl.pallas_call(
        body, name=name, out_shape=(sds, sds, sds), grid=(R // tr,), in_specs=[blk] * 4, out_specs=(blk,) * 3,
        compiler_params=_cparams(("parallel",)),
    )(w, g, m, v)


_PACK_QUANTUM = 256 * LANES


def _pack(arrs):
    flat = jnp.concatenate([a.reshape(-1).astype(F32) for a in arrs])
    flat = jnp.pad(flat, (0, (-flat.shape[0]) % _PACK_QUANTUM))
    return flat.reshape(-1, LANES)


def _unpack(p, shapes):
    flat = p.reshape(-1)
    out, off = [], 0
    for s in shapes:
        n = int(np.prod(s))
        out.append(flat[off:off + n].reshape(s))
        off += n
    return out


def kernel(*args):
    nw = len(WEIGHTS)
    x, tgt = args[0], args[1 + nw]
    w = dict(zip(WEIGHTS, args[1:1 + nw]))
    m = dict(zip(WEIGHTS, args[2 + nw:2 + 2 * nw]))
    v = dict(zip(WEIGHTS, args[2 + 2 * nw:2 + 3 * nw]))
    _, L, D = x.shape
    chip = 2 * lax.axis_index("x") + lax.axis_index("y")

    big = list(BIG)
    small_sh_shapes = [w[n].shape for n in SMALL_SHARDED]
    nbig = len(big)
    chip1 = chip.reshape(1).astype(jnp.int32)
    axes2 = [BIG[n] - 1 for n in big] + [0]
    pack_sh = _pack([w[n] for n in SMALL_SHARDED])
    shards = [w[n].astype(BF16) for n in big] + [pack_sh[None]]
    pairs = [(t, l) for t in range(nbig + 1) for l in range(shards[t].shape[0])]
    depth = w['norm_mix_g'].shape[0]
    part_of = lambda t, l: 0 if t == nbig else 2 * _model_layer(big[t], l) + big[t].startswith('ffn')
    groups = [[k for k, (t, l) in enumerate(pairs) if part_of(t, l) == g] for g in range(2 * depth)]
    placed = [_place_quarter(shards[t], l, axes2[t], chip1) for t, l in pairs]
    sems, shards_thru, lands, token = _gather_start(shards, placed, pairs, axes2, groups)
    in_flight = {'shards': shards_thru}

    def wait_group(i, after):
        sh, landed = _gather_wait(i, in_flight['shards'], [lands[k] for k in groups[i]], sems[2 * i], sems[2 * i + 1],
                                  after, pairs, axes2, groups[i])
        in_flight['shards'] = sh
        return {pairs[k][0]: a for k, a in zip(groups[i], landed)}

    first = wait_group(0, token)
    packed = first.pop(nbig).reshape(N_CHIPS, -1, LANES)
    per_chip = [_unpack(packed[s], small_sh_shapes) for s in range(N_CHIPS)]
    wl = dict(w)
    for k, n in enumerate(SMALL_SHARDED):
        wl[n] = jnp.concatenate([per_chip[s][k] for s in range(N_CHIPS)], axis=-1)

    def layer_weights(i, after):
        got = first if i == 0 else wait_group(i, after)
        return {big[t]: a for t, a in got.items()}

    small_shapes = [(w[n].shape[:-1] + (w[n].shape[-1] * N_CHIPS,)) if n in SMALL_SHARDED else w[n].shape
                    for n in SMALL]
    n_small = sum(int(np.prod(s)) for s in small_shapes)
    pack_rows = -(-n_small // _PACK_QUANTUM) * _PACK_QUANTUM // LANES
    nlayers = [w[n].shape[0] for n in big] + [2]
    halves = [n // 2 for n in nlayers]
    quarters = [tuple(w[n].shape[1:]) for n in big] + [(pack_rows // 2 // N_CHIPS, LANES)]
    wire = [BF16] * nbig + [F32]
    land_now = [lax.empty((halves[t], N_SLOTS) + quarters[t], wire[t]) for t in range(nbig + 1)]
    gparts = [[None] * n for n in nlayers]
    started = []

    def start_scatter(tag, ts, ls, arrays):
        meta = [(axes2[t], l // halves[t], l % halves[t], quarters[t][axes2[t]]) for t, l in zip(ts, ls)]
        send, recv, thru, new_lands, token = _scatter_start(tag, arrays, [land_now[t] for t in ts], meta)
        for t, ln in zip(ts, new_lands):
            land_now[t] = ln
        started.append((send, recv, thru, ts, meta, ls))
        return token

    def on_layer_grads(g, gb):
        ts = [big.index(n) for n in gb]
        return start_scatter(g, ts, [g // 2 if big[t].startswith('ffn') else g // 4 for t in ts],
                             [gb[big[t]] for t in ts])

    loss, dx, gsmall = _local_step(x.reshape(L, D), tgt.reshape(L, D), wl, layer_weights, on_layer_grads)
    loss = lax.psum(loss, ("x", "y", "c"))

    gpack = _pack([gsmall[n] for n in SMALL])
    start_scatter(2 * depth, [nbig, nbig], [0, 1], [gpack[:pack_rows // 2], gpack[pack_rows // 2:]])
    landed, sent = _scatter_wait([s[:5] for s in started], land_now)
    for (t, l), g in zip([(t, l) for s in started for t, l in zip(s[3], s[5])], sent):
        gparts[t][l] = g
    gshard = {n: _sum_and_share(landed[t], gparts[t], axes2[t], chip1, "sum_share_" + n) for t, n in enumerate(big)}
    small_sum = _sum_and_share(landed[nbig], gparts[nbig], 0, chip1, "sum_share_small")
    gpack = _gather_sums_over_chips(small_sum).transpose(1, 0, 2, 3).reshape(pack_rows, LANES)
    gs = dict(zip(SMALL, _unpack(gpack, small_shapes)))
    for n in SMALL_SHARDED:
        width = w[n].shape[-1]
        gs[n] = lax.dynamic_slice_in_dim(gs[n], chip * width, width, axis=gs[n].ndim - 1)

    grads, delta, new_m, new_v = {}, {}, {}, {}
    for n in big:
        shp = w[n].shape
        flat = lambda a: a.reshape(shp[0] * shp[1], shp[2])
        g = gshard[n]
        grads[n] = g
        d_, m_, v_ = _adamw(flat(w[n]), flat(g), flat(m[n]), flat(v[n]), "adamw_" + n)
        delta[n], new_m[n], new_v[n] = d_.reshape(shp), m_.reshape(shp), v_.reshape(shp)
    loc_shapes = [w[n].shape for n in SMALL]
    d_, m_, v_ = _adamw(_pack([w[n] for n in SMALL]), _pack([gs[n] for n in SMALL]), _pack([m[n] for n in SMALL]),
                        _pack([v[n] for n in SMALL]), "adamw_small")
    for n, dn, mn, vn in zip(SMALL, _unpack(d_, loc_shapes), _unpack(m_, loc_shapes), _unpack(v_, loc_shapes)):
        grads[n], delta[n], new_m[n], new_v[n] = gs[n], dn, mn, vn

    return (loss, dx.reshape(1, L, D), *[grads[n] for n in WEIGHTS], *[delta[n] for n in WEIGHTS],
            *[new_m[n] for n in WEIGHTS], *[new_v[n] for n in WEIGHTS])
```

```python
import functools
import math

import numpy as np
import jax
import jax.numpy as jnp
from jax import lax
from jax.experimental import pallas as pl
from jax.experimental.pallas import tpu as pltpu

F32 = jnp.float32
BF16 = jnp.bfloat16
MESH = pl.DeviceIdType.MESH

EPS = 1e-6
CHUNK = 128
POOL_WINDOWS = (2, 4, 8, 16)
LANES = 128
SUBLANES = 8
SCAN_CHUNKS = SUBLANES
S5_GROUPS_PER_STEP = 4
MM_TM_CAP, MM_TN_CAP, MM_TK_CAP = 1408, 1408, 1408
MM_TK_WHOLE = 2048
VMEM_LIMIT = 48 * 1024 * 1024
VMEM_LIMIT_S5 = 56 * 1024 * 1024

ADAM_LR, ADAM_B1, ADAM_B2, ADAM_EPS, ADAM_WD, ADAM_STEP = 0.001, 0.9, 0.999, 1e-08, 0.01, 10

WEIGHTS = ['norm_mix_g', 'even_w_in', 'even_conv_w', 'ssm_log_step', 'ssm_a_re', 'ssm_a_im', 'ssm_b_re',
           'ssm_b_im', 'ssm_c_re', 'ssm_c_im', 'ssm_d', 'ssm_glu_w', 'ssm_glu_b', 'even_w_out', 'odd_w_in',
           'pool_w', 'pool_scale', 'sgu_norm_g', 'sgu_w', 'sgu_b', 'odd_w_out', 'norm_ffn_g', 'ffn_w_up',
           'ffn_conv_w', 'ffn_conv_b', 'ffn_w_down', 'norm_final_g']
BIG = {'even_w_in': 2, 'ssm_glu_w': 1, 'even_w_out': 1, 'odd_w_in': 2, 'odd_w_out': 1, 'ffn_w_up': 2,
       'ffn_w_down': 1}
SMALL_SHARDED = ('even_conv_w', 'pool_scale', 'sgu_norm_g', 'ffn_conv_w')
SMALL = [n for n in WEIGHTS if n not in BIG]
N_CHIPS = 4
N_DEV = 8


def _cparams(sem=None, vmem=VMEM_LIMIT):
    kw = dict(vmem_limit_bytes=vmem)
    if sem is not None:
        kw['dimension_semantics'] = sem
    return pltpu.CompilerParams(**kw)


def _pick(n, segs=(), prefs=(1024, 512, 256, 128)):
    for t in prefs:
        if n % t == 0 and all(s % t == 0 for s in segs if s):
            return t
    return n


def _largest_tile(n, segs, cap):
    best = None
    for t in range(LANES, min(n, cap) + 1, LANES):
        if n % t == 0 and all(s % t == 0 for s in segs if s):
            best = t
    return best if best is not None else n


def _ldims(arr, kind):
    if kind is None:
        return arr.shape
    if kind[0] == 'lead':
        return arr.shape[1:]
    return (arr.shape[1], arr.shape[0] * arr.shape[2])


def _segw(arr, kind):
    return arr.shape[2] if (kind is not None and kind[0] == 'seg') else None


def _opspec(arr, kind, br, bc, rfn, cfn):
    if kind is None:
        return pl.BlockSpec((br, bc), lambda i, j, k: (rfn(i, j, k), cfn(i, j, k)))
    if kind[0] == 'lead':
        lead = kind[1]
        return pl.BlockSpec((None, br, bc), lambda i, j, k: (lead, rfn(i, j, k), cfn(i, j, k)))
    per = arr.shape[2] // bc
    return pl.BlockSpec((None, br, bc), lambda i, j, k: (cfn(i, j, k) // per, rfn(i, j, k), cfn(i, j, k) % per))


def _mm(a, b, mode, out_dtype, name, ak=None, bk=None, ok=None, res=None, dep=None):
    ar, ac = _ldims(a, ak)
    br_, bc_ = _ldims(b, bk)
    if mode == 'nn':
        M, K, N = ar, ac, bc_
        assert br_ == K
    else:
        M, K, N = ar, ac, br_
        assert bc_ == K
    sa, sb = _segw(a, ak), _segw(b, bk)
    so = (N // ok[1]) if ok is not None else None
    tm = _largest_tile(M, [], MM_TM_CAP)
    tn = _largest_tile(N, [sb if mode == 'nn' else None, so], MM_TN_CAP)
    ksegs = [sa, sb if mode == 'nt' else None]
    tk = K if (K <= MM_TK_WHOLE and not any(ksegs)) else _largest_tile(K, ksegs, MM_TK_CAP)
    nk = K // tk
    I = lambda i, j, k: i
    J = lambda i, j, k: j
    Kk = lambda i, j, k: k
    a_spec = _opspec(a, ak, tm, tk, I, Kk)
    if mode == 'nn':
        b_spec = _opspec(b, bk, tk, tn, Kk, J)
        dims = (((1,), (0,)), ((), ()))
    else:
        b_spec = _opspec(b, bk, tn, tk, J, Kk)
        dims = (((1,), (1,)), ((), ()))
    if ok is None:
        out_shape = jax.ShapeDtypeStruct((M, N), out_dtype)
        o_spec = pl.BlockSpec((tm, tn), lambda i, j, k: (i, j))
    else:
        out_shape = jax.ShapeDtypeStruct((ok[1], M, N // ok[1]), out_dtype)
        per = (N // ok[1]) // tn
        o_spec = pl.BlockSpec((None, tm, tn), lambda i, j, k: (j // per, i, j % per))
    has_res = res is not None

    def body(*refs):
        a_ref, b_ref = refs[0], refs[1]
        r_ref = refs[2] if has_res else None
        o_ref = refs[n_in]
        prod = lax.dot_general(a_ref[...].astype(BF16), b_ref[...].astype(BF16), dims, preferred_element_type=F32)
        if nk == 1:
            o_ref[...] = (prod + r_ref[...] if has_res else prod).astype(out_dtype)
            return
        acc = refs[-1]
        k = pl.program_id(2)

        @pl.when(k == 0)
        def _():
            acc[...] = prod

        @pl.when(k > 0)
        def _():
            acc[...] += prod

        @pl.when(k == nk - 1)
        def _():
            o = acc[...]
            if has_res:
                o = o + r_ref[...]
            o_ref[...] = o.astype(out_dtype)

    in_specs = [a_spec, b_spec]
    args = [a, b]
    if has_res:
        in_specs.append(pl.BlockSpec((tm, tn), lambda i, j, k: (i, j)))
        args.append(res)
    if dep is not None:
        in_specs.append(pl.BlockSpec(memory_space=pl.ANY))
        args.append(dep)
    n_in = len(args)
    return pl.pallas_call(
        body, name=name, out_shape=out_shape, grid=(M // tm, N // tn, nk), in_specs=in_specs, out_specs=o_spec,
        scratch_shapes=[pltpu.VMEM((tm, tn), F32)] if nk > 1 else [],
        compiler_params=_cparams(("parallel", "parallel", "arbitrary")),
    )(*args)


_G0 = math.sqrt(2.0 / math.pi)
_G1 = 0.044715


def _gelu(x):
    return 0.5 * x * (1.0 + jnp.tanh(_G0 * (x + _G1 * x * x * x)))


def _gelu_grad(x):
    x2 = x * x
    t = jnp.tanh(_G0 * (x + _G1 * x * x2))
    return 0.5 * (1.0 + t) + 0.5 * x * (1.0 - t * t) * (_G0 * (1.0 + 3.0 * _G1 * x2))


def _sigmoid(x):
    return 1.0 / (1.0 + jnp.exp(-x))


def _down(v, k):
    r = pltpu.roll(v, k, axis=0)
    row = lax.broadcasted_iota(jnp.int32, (SUBLANES, v.shape[1]), 0)
    return jnp.concatenate([jnp.where(row >= k, r[:SUBLANES], 0.0), r[SUBLANES:]], axis=0)


def _up(v, k):
    n = v.shape[0]
    r = pltpu.roll(v, n - k, axis=0)
    row = lax.broadcasted_iota(jnp.int32, (SUBLANES, v.shape[1]), 0)
    return jnp.concatenate([r[:n - SUBLANES], jnp.where(row < SUBLANES - k, r[n - SUBLANES:], 0.0)], axis=0)


def _conv3(v, w):
    return w[0:1, :] * _down(v, 2) + w[1:2, :] * _down(v, 1) + w[2:3, :] * v


def _conv3_t(dv, w):
    return w[2:3, :] * dv + w[1:2, :] * _up(dv, 1) + w[0:1, :] * _up(dv, 2)


def _conv3_dw(dv, v):
    return (jnp.sum(dv * _down(v, 2), axis=0, keepdims=True),
            jnp.sum(dv * _down(v, 1), axis=0, keepdims=True),
            jnp.sum(dv * v, axis=0, keepdims=True))


def _cmul(ar, ai, br, bi):
    return ar * br - ai * bi, ar * bi + ai * br


def _cpow(lr, li, n):
    rr = ri = None
    br, bi = lr, li
    while n:
        if n & 1:
            rr, ri = (br, bi) if rr is None else _cmul(rr, ri, br, bi)
        n >>= 1
        if n:
            br, bi = _cmul(br, bi, br, bi)
    return rr, ri


def _rms_fwd(x, g, name):
    L, D = x.shape
    tr = _pick(L, prefs=(512, 256, 128))

    def body(x_ref, g_ref, h_ref, ht_ref):
        xv = x_ref[...]
        r = lax.rsqrt(jnp.mean(xv * xv, axis=-1, keepdims=True) + EPS)
        h = xv * r * g_ref[...]
        h_ref[...] = h.astype(BF16)
        ht_ref[...] = h.T.astype(BF16)

    return pl.pallas_call(
        body, name=name, out_shape=(jax.ShapeDtypeStruct((L, D), BF16), jax.ShapeDtypeStruct((D, L), BF16)),
        grid=(L // tr,),
        in_specs=[pl.BlockSpec((tr, D), lambda i: (i, 0)), pl.BlockSpec((1, D), lambda i: (0, 0))],
        out_specs=(pl.BlockSpec((tr, D), lambda i: (i, 0)), pl.BlockSpec((D, tr), lambda i: (0, i))),
        compiler_params=_cparams(("parallel",)),
    )(x, g.reshape(1, D))


def _rms_bwd(x, g, dh, dres, name):
    L, D = x.shape
    tr = _pick(L, prefs=(512, 256, 128))
    nsteps = L // tr

    def body(x_ref, g_ref, dh_ref, dres_ref, dx_ref, dg_ref, acc):
        i = pl.program_id(0)

        @pl.when(i == 0)
        def _():
            acc[...] = jnp.zeros_like(acc)

        xv = x_ref[...]
        r = lax.rsqrt(jnp.mean(xv * xv, axis=-1, keepdims=True) + EPS)
        xh = xv * r
        dhv = dh_ref[...].astype(F32)
        acc[...] += jnp.sum((dhv * xh).reshape(tr // SUBLANES, SUBLANES, D), axis=0)
        dxh = dhv * g_ref[...]
        dx_ref[...] = dres_ref[...] + r * (dxh - xh * jnp.mean(dxh * xh, axis=-1, keepdims=True))

        @pl.when(i == nsteps - 1)
        def _():
            dg_ref[...] = jnp.sum(acc[...], axis=0, keepdims=True)

    row = pl.BlockSpec((tr, D), lambda i: (i, 0))
    vec = pl.BlockSpec((1, D), lambda i: (0, 0))
    return pl.pallas_call(
        body, name=name, out_shape=(jax.ShapeDtypeStruct((L, D), F32), jax.ShapeDtypeStruct((1, D), F32)),
        grid=(nsteps,), in_specs=[row, vec, row, row], out_specs=(row, vec),
        scratch_shapes=[pltpu.VMEM((SUBLANES, D), F32)], compiler_params=_cparams(("arbitrary",)),
    )(x, g.reshape(1, D), dh, dres)


def _loss_head(x, g, tgt):
    L, D = x.shape
    tr = _pick(L, prefs=(512, 256, 128))
    nsteps = L // tr

    def body(x_ref, g_ref, t_ref, loss_ref, dx_ref, dg_ref, acc_g, acc_l):
        i = pl.program_id(0)

        @pl.when(i == 0)
        def _():
            acc_g[...] = jnp.zeros_like(acc_g)
            acc_l[...] = jnp.zeros_like(acc_l)

        xv = x_ref[...]
        gv = g_ref[...]
        r = lax.rsqrt(jnp.mean(xv * xv, axis=-1, keepdims=True) + EPS)
        xh = xv * r
        e = xh * gv - t_ref[...]
        acc_l[...] += jnp.sum((e * e).reshape(tr // SUBLANES, SUBLANES, D), axis=0)
        dy = e * (1.0 / D)
        acc_g[...] += jnp.sum((dy * xh).reshape(tr // SUBLANES, SUBLANES, D), axis=0)
        dxh = dy * gv
        dx_ref[...] = r * (dxh - xh * jnp.mean(dxh * xh, axis=-1, keepdims=True))

        @pl.when(i == nsteps - 1)
        def _():
            dg_ref[...] = jnp.sum(acc_g[...], axis=0, keepdims=True)
            tot = jnp.sum(jnp.sum(acc_l[...], axis=0, keepdims=True), axis=1, keepdims=True) * (0.5 / D)
            loss_ref[...] = jnp.broadcast_to(tot, (SUBLANES, LANES))

    row = pl.BlockSpec((tr, D), lambda i: (i, 0))
    vec = pl.BlockSpec((1, D), lambda i: (0, 0))
    return pl.pallas_call(
        body, name="loss_head",
        out_shape=(jax.ShapeDtypeStruct((SUBLANES, LANES), F32), jax.ShapeDtypeStruct((L, D), F32),
                   jax.ShapeDtypeStruct((1, D), F32)),
        grid=(nsteps,), in_specs=[row, vec, row],
        out_specs=(pl.BlockSpec((SUBLANES, LANES), lambda i: (0, 0)), row, vec),
        scratch_shapes=[pltpu.VMEM((SUBLANES, D), F32), pltpu.VMEM((SUBLANES, D), F32)],
        compiler_params=_cparams(("arbitrary",)),
    )(x, g.reshape(1, D), tgt)


def _sconv_fwd(proj4, conv_w, name):
    _, L, C = proj4.shape
    cb = LANES

    def body(p_ref, w_ref, o_ref):
        xa, ba, ca = p_ref[0], p_ref[1], p_ref[2]
        o_ref[...] = (ba * _conv3(ca * xa, w_ref[...])).astype(BF16)

    return pl.pallas_call(
        body, name=name, out_shape=jax.ShapeDtypeStruct((L, C), BF16), grid=(C // cb,),
        in_specs=[pl.BlockSpec((3, L, cb), lambda j: (0, 0, j)), pl.BlockSpec((3, cb), lambda j: (0, j))],
        out_specs=pl.BlockSpec((L, cb), lambda j: (0, j)), compiler_params=_cparams(("parallel",)),
    )(proj4, conv_w)


def _sconv_bwd(proj4, dmix, conv_w, name):
    _, L, C = proj4.shape
    cb = LANES

    def body(p_ref, d_ref, w_ref, o_ref, dw_ref):
        xa, ba, ca = p_ref[0], p_ref[1], p_ref[2]
        w = w_ref[...]
        dya = d_ref[...]
        q = ca * xa
        cq = _conv3(q, w)
        dcq = dya * ba
        dq = _conv3_t(dcq, w)
        for tap, dwt in enumerate(_conv3_dw(dcq, q)):
            dw_ref[tap:tap + 1, :] = dwt
        o_ref[0] = (dq * ca).astype(BF16)
        o_ref[1] = (dya * cq).astype(BF16)
        o_ref[2] = (dq * xa).astype(BF16)

    return pl.pallas_call(
        body, name=name,
        out_shape=(jax.ShapeDtypeStruct((3, L, C), BF16), jax.ShapeDtypeStruct((3, C), F32)), grid=(C // cb,),
        in_specs=[pl.BlockSpec((3, L, cb), lambda j: (0, 0, j)), pl.BlockSpec((L, cb), lambda j: (0, j)),
                  pl.BlockSpec((3, cb), lambda j: (0, j))],
        out_specs=(pl.BlockSpec((3, L, cb), lambda j: (0, 0, j)), pl.BlockSpec((3, cb), lambda j: (0, j))),
        compiler_params=_cparams(("parallel",)),
    )(proj4, dmix, conv_w)


def _to_scan_order(v):
    L, C = v.shape
    return v.reshape(SCAN_CHUNKS, L // SCAN_CHUNKS, C).transpose(1, 0, 2).reshape(L, C)


def _from_scan_order(v):
    L, C = v.shape
    return v.reshape(L // SCAN_CHUNKS, SCAN_CHUNKS, C).transpose(1, 0, 2).reshape(L, C)


def _s5_prep(log_step, a_re, a_im, b_re, b_im, c_re, c_im):
    G, P = a_re.shape
    H = b_re.shape[-1]
    gs = S5_GROUPS_PER_STEP
    ns = G // gs
    gu = LANES // H
    lam = lax.complex(a_re, a_im)
    step = jnp.exp(log_step)[:, None]
    lam_bar = jnp.exp(lam * step)
    b_bar = ((lam_bar - 1.0) / lam)[..., None] * lax.complex(b_re, b_im)
    lr = jnp.real(lam_bar).reshape(ns, 1, gs * P)
    li = jnp.imag(lam_bar).reshape(ns, 1, gs * P)
    k = np.arange(ns)[:, None, None]
    oh = jnp.asarray((np.arange(gu)[None, :, None] == gs * (k % (gu // gs)) + np.arange(gs)[None, None, :]),
                     F32)
    bre = jnp.einsum('kgl,klph->kghlp', oh, jnp.real(b_bar).reshape(ns, gs, P, H)).reshape(ns, gu * H, gs * P)
    bim = jnp.einsum('kgl,klph->kghlp', oh, jnp.imag(b_bar).reshape(ns, gs, P, H)).reshape(ns, gu * H, gs * P)
    cre = jnp.einsum('kgl,klhp->klpgh', oh, c_re.reshape(ns, gs, H, P)).reshape(ns, gs * P, gu * H)
    cim = jnp.einsum('kgl,klhp->klpgh', oh, c_im.reshape(ns, gs, H, P)).reshape(ns, gs * P, gu * H)
    return lr, li, jnp.concatenate([bre, bim], axis=2), jnp.concatenate([cre, -cim], axis=1)


def _carry_tile(fr, fi, pr, pi, reverse):
    row = lax.broadcasted_iota(jnp.int32, fr.shape, 0)
    cr = jnp.zeros_like(fr)
    ci = jnp.zeros_like(fi)
    sr = jnp.zeros_like(fr[0:1])
    si = jnp.zeros_like(sr)
    order = range(SCAN_CHUNKS - 1, 0, -1) if reverse else range(0, SCAN_CHUNKS - 1)
    for c in order:
        fcr = jnp.sum(jnp.where(row == c, fr, 0.0), axis=0, keepdims=True)
        fci = jnp.sum(jnp.where(row == c, fi, 0.0), axis=0, keepdims=True)
        mr, mi = _cmul(pr, pi, sr, si)
        sr, si = mr + fcr, mi + fci
        nxt = c - 1 if reverse else c + 1
        cr = jnp.where(row == nxt, sr, cr)
        ci = jnp.where(row == nxt, si, ci)
    return cr, ci


def _s5_fwd(u, lr, li, bmat, cmat, d, name):
    L, Du = u.shape
    ns, _, W2 = bmat.shape
    W = W2 // 2
    T = L // SCAN_CHUNKS
    rb = _pick(L, prefs=(512, 256, 128))
    per = (ns * LANES) // Du

    def body(u_ref, lr_ref, li_ref, b_ref, c_ref, d_ref, y_ref, sr_ref, si_ref):
        k = pl.program_id(0)
        for r in range(L // rb):
            rows = pl.ds(r * rb, rb)
            bu = jnp.dot(u_ref[rows, :].astype(BF16), b_ref[...], preferred_element_type=F32)
            sr_ref[rows, :] = bu[:, :W]
            si_ref[rows, :] = bu[:, W:]
        lam_r = jnp.broadcast_to(lr_ref[...], (SUBLANES, W))
        lam_i = jnp.broadcast_to(li_ref[...], (SUBLANES, W))

        def local(t, carry):
            sr, si = carry
            rows = pl.ds(pl.multiple_of(t * SUBLANES, SUBLANES), SUBLANES)
            mr, mi = _cmul(lam_r, lam_i, sr, si)
            sr = mr + sr_ref[rows, :]
            si = mi + si_ref[rows, :]
            sr_ref[rows, :] = sr
            si_ref[rows, :] = si
            return sr, si

        z = jnp.zeros((SUBLANES, W), F32)
        fr, fi = lax.fori_loop(0, T, local, (z, z))
        pr, pi = _cpow(lam_r, lam_i, T)
        cr, ci = _carry_tile(fr, fi, pr[0:1], pi[0:1], reverse=False)

        def fix(t, carry):
            wr, wi = carry
            rows = pl.ds(pl.multiple_of(t * SUBLANES, SUBLANES), SUBLANES)
            ar, ai = _cmul(wr, wi, cr, ci)
            sr_ref[rows, :] += ar
            si_ref[rows, :] += ai
            return _cmul(wr, wi, lam_r, lam_i)

        lax.fori_loop(0, T, fix, (lam_r, lam_i))
        first = (k % per) == 0
        for r in range(L // rb):
            rows = pl.ds(r * rb, rb)
            s = jnp.concatenate([sr_ref[rows, :], si_ref[rows, :]], axis=1).astype(BF16)
            y = jnp.dot(s, c_ref[...], preferred_element_type=F32)

            @pl.when(first)
            def _():
                y_ref[rows, :] = y + d_ref[...] * u_ref[rows, :]

            @pl.when(jnp.logical_not(first))
            def _():
                y_ref[rows, :] += y

    ublk = pl.BlockSpec((L, LANES), lambda k: (0, k // per))
    sblk = pl.BlockSpec((L, W), lambda k: (0, k))
    lam = pl.BlockSpec((None, 1, W), lambda k: (k, 0, 0))
    return pl.pallas_call(
        body, name=name,
        out_shape=(jax.ShapeDtypeStruct((L, Du), F32), jax.ShapeDtypeStruct((L, ns * W), F32),
                   jax.ShapeDtypeStruct((L, ns * W), F32)),
        grid=(ns,),
        in_specs=[ublk, lam, lam, pl.BlockSpec((None, LANES, 2 * W), lambda k: (k, 0, 0)),
                  pl.BlockSpec((None, 2 * W, LANES), lambda k: (k, 0, 0)),
                  pl.BlockSpec((1, LANES), lambda k: (0, k // per))],
        out_specs=(ublk, sblk, sblk), compiler_params=_cparams(("arbitrary",), VMEM_LIMIT_S5),
    )(u, lr, li, bmat.astype(BF16), cmat.astype(BF16), d.reshape(1, Du))


def _s5_bwd(dy, u, s_re, s_im, lr, li, bmat, cmat, d, name):
    L, Du = u.shape
    ns, _, W2 = bmat.shape
    W = W2 // 2
    T = L // SCAN_CHUNKS
    rb = _pick(L, prefs=(512, 256, 128))
    per = (ns * LANES) // Du
    NT = (((1,), (1,)), ((), ()))
    TN = (((0,), (0,)), ((), ()))

    def body(dy_ref, u_ref, sr_ref, si_ref, lr_ref, li_ref, b_ref, c_ref, d_ref,
             du_ref, db_ref, dc_ref, dl_ref, dd_ref, gr_ref, gi_ref):
        k = pl.program_id(0)
        for r in range(L // rb):
            rows = pl.ds(r * rb, rb)
            g = lax.dot_general(dy_ref[rows, :].astype(BF16), c_ref[...], NT, preferred_element_type=F32)
            gr_ref[rows, :] = g[:, :W]
            gi_ref[rows, :] = g[:, W:]
        lam_r = jnp.broadcast_to(lr_ref[...], (SUBLANES, W))
        lam_i = -jnp.broadcast_to(li_ref[...], (SUBLANES, W))

        def local(i, carry):
            gr, gi = carry
            rows = pl.ds(pl.multiple_of((T - 1 - i) * SUBLANES, SUBLANES), SUBLANES)
            mr, mi = _cmul(lam_r, lam_i, gr, gi)
            gr = mr + gr_ref[rows, :]
            gi = mi + gi_ref[rows, :]
            gr_ref[rows, :] = gr
            gi_ref[rows, :] = gi
            return gr, gi

        z = jnp.zeros((SUBLANES, W), F32)
        fr, fi = lax.fori_loop(0, T, local, (z, z))
        pr, pi = _cpow(lam_r, lam_i, T)
        cr, ci = _carry_tile(fr, fi, pr[0:1], pi[0:1], reverse=True)

        def true_g(rows, wr, wi):
            ar, ai = _cmul(wr, wi, cr, ci)
            gr = gr_ref[rows, :] + ar
            gi = gi_ref[rows, :] + ai
            gr_ref[rows, :] = gr
            gi_ref[rows, :] = gi
            return gr, gi

        def fix(i, carry):
            wr, wi, ar_, ai_ = carry
            t = T - 1 - i
            rows = pl.ds(pl.multiple_of(t * SUBLANES, SUBLANES), SUBLANES)
            prev = pl.ds(pl.multiple_of((t - 1) * SUBLANES, SUBLANES), SUBLANES)
            gr, gi = true_g(rows, wr, wi)
            qr, qi = sr_ref[prev, :], si_ref[prev, :]
            ar_ = ar_ + gr * qr + gi * qi
            ai_ = ai_ + gi * qr - gr * qi
            wr, wi = _cmul(wr, wi, lam_r, lam_i)
            return wr, wi, ar_, ai_

        wr, wi, acc_r, acc_i = lax.fori_loop(0, T - 1, fix, (lam_r, lam_i, z, z))
        gr, gi = true_g(pl.ds(0, SUBLANES), wr, wi)
        last = pl.ds((T - 1) * SUBLANES, SUBLANES)
        row = lax.broadcasted_iota(jnp.int32, (SUBLANES, W), 0)
        qr = jnp.where(row >= 1, pltpu.roll(sr_ref[last, :], 1, axis=0), 0.0)
        qi = jnp.where(row >= 1, pltpu.roll(si_ref[last, :], 1, axis=0), 0.0)
        acc_r = acc_r + gr * qr + gi * qi
        acc_i = acc_i + gi * qr - gr * qi
        dl_ref[0:1, :] = jnp.sum(acc_r, axis=0, keepdims=True)
        dl_ref[1:2, :] = jnp.sum(acc_i, axis=0, keepdims=True)

        first = (k % per) == 0
        db = jnp.zeros((LANES, 2 * W), F32)
        dc = jnp.zeros((LANES, 2 * W), F32)
        dd = jnp.zeros((1, LANES), F32)
        for r in range(L // rb):
            rows = pl.ds(r * rb, rb)
            gb = jnp.concatenate([gr_ref[rows, :], gi_ref[rows, :]], axis=1).astype(BF16)
            sb = jnp.concatenate([sr_ref[rows, :], si_ref[rows, :]], axis=1).astype(BF16)
            dyv = dy_ref[rows, :]
            uv = u_ref[rows, :]
            du = lax.dot_general(gb, b_ref[...], NT, preferred_element_type=F32)
            db = db + lax.dot_general(uv.astype(BF16), gb, TN, preferred_element_type=F32)
            dc = dc + lax.dot_general(dyv.astype(BF16), sb, TN, preferred_element_type=F32)
            dd = dd + jnp.sum(dyv * uv, axis=0, keepdims=True)

            @pl.when(first)
            def _():
                du_ref[rows, :] = du + d_ref[...] * dyv

            @pl.when(jnp.logical_not(first))
            def _():
                du_ref[rows, :] += du

        db_ref[...] = db
        dc_ref[...] = dc

        @pl.when(first)
        def _():
            dd_ref[...] = dd

    ublk = pl.BlockSpec((L, LANES), lambda k: (0, k // per))
    sblk = pl.BlockSpec((L, W), lambda k: (0, k))
    lam = pl.BlockSpec((None, 1, W), lambda k: (k, 0, 0))
    vec = pl.BlockSpec((1, LANES), lambda k: (0, k // per))
    mat = pl.BlockSpec((None, LANES, 2 * W), lambda k: (k, 0, 0))
    return pl.pallas_call(
        body, name=name,
        out_shape=(jax.ShapeDtypeStruct((L, Du), F32), jax.ShapeDtypeStruct((ns, LANES, 2 * W), F32),
                   jax.ShapeDtypeStruct((ns, LANES, 2 * W), F32), jax.ShapeDtypeStruct((ns, 2, W), F32),
                   jax.ShapeDtypeStruct((1, Du), F32)),
        grid=(ns,),
        in_specs=[ublk, ublk, sblk, sblk, lam, lam, mat,
                  pl.BlockSpec((None, 2 * W, LANES), lambda k: (k, 0, 0)), vec],
        out_specs=(ublk, mat, mat, pl.BlockSpec((None, 2, W), lambda k: (k, 0, 0)), vec),
        scratch_shapes=[pltpu.VMEM((L, W), F32), pltpu.VMEM((L, W), F32)],
        compiler_params=_cparams(("arbitrary",), VMEM_LIMIT_S5),
    )(dy, u, s_re, s_im, lr, li, bmat.astype(BF16), cmat.astype(BF16), d.reshape(1, Du))


def _glu_fwd(yraw, wmat, bias, name):
    L, C = yraw.shape
    tr = _pick(L, prefs=(512, 256, 128))

    def body(y_ref, w_ref, b_ref, o_ref):
        yg = _gelu(y_ref[...])
        zz = jnp.dot(yg.astype(BF16), w_ref[...], preferred_element_type=F32) + b_ref[...]
        o_ref[...] = (yg * _sigmoid(zz)).astype(BF16)

    return pl.pallas_call(
        body, name=name, out_shape=jax.ShapeDtypeStruct((L, C), BF16), grid=(L // tr,),
        in_specs=[pl.BlockSpec((tr, C), lambda i: (i, 0)), pl.BlockSpec((C, C), lambda i: (0, 0)),
                  pl.BlockSpec((1, C), lambda i: (0, 0))],
        out_specs=pl.BlockSpec((tr, C), lambda i: (i, 0)), compiler_params=_cparams(("parallel",)),
    )(yraw, wmat, bias.reshape(1, C))


def _glu_bwd(yraw, dyb, wmat, bias, name):
    L, C = yraw.shape
    tr = _pick(L, prefs=(512, 256, 128))
    nsteps = L // tr

    def body(y_ref, d_ref, w_ref, b_ref, dy_ref, dw_ref, db_ref, acc_b):
        i = pl.program_id(0)

        @pl.when(i == 0)
        def _():
            dw_ref[...] = jnp.zeros_like(dw_ref)
            acc_b[...] = jnp.zeros_like(acc_b)

        yr = y_ref[...]
        yg = _gelu(yr)
        ygb = yg.astype(BF16)
        sg = _sigmoid(jnp.dot(ygb, w_ref[...], preferred_element_type=F32) + b_ref[...])
        dyb_ = d_ref[...]
        dz = dyb_ * yg * sg * (1.0 - sg)
        dzb = dz.astype(BF16)
        dyg = dyb_ * sg + lax.dot_general(dzb, w_ref[...], (((1,), (1,)), ((), ())), preferred_element_type=F32)
        dw_ref[...] += lax.dot_general(ygb, dzb, (((0,), (0,)), ((), ())), preferred_element_type=F32)
        acc_b[...] += jnp.sum(dz.reshape(tr // SUBLANES, SUBLANES, C), axis=0)
        dy_ref[...] = dyg * _gelu_grad(yr)

        @pl.when(i == nsteps - 1)
        def _():
            db_ref[...] = jnp.sum(acc_b[...], axis=0, keepdims=True)

    row = pl.BlockSpec((tr, C), lambda i: (i, 0))
    return pl.pallas_call(
        body, name=name,
        out_shape=(jax.ShapeDtypeStruct((L, C), F32), jax.ShapeDtypeStruct((C, C), F32),
                   jax.ShapeDtypeStruct((1, C), F32)),
        grid=(nsteps,),
        in_specs=[row, row, pl.BlockSpec((C, C), lambda i: (0, 0)), pl.BlockSpec((1, C), lambda i: (0, 0))],
        out_specs=(row, pl.BlockSpec((C, C), lambda i: (0, 0)), pl.BlockSpec((1, C), lambda i: (0, 0))),
        scratch_shapes=[pltpu.VMEM((SUBLANES, C), F32)], compiler_params=_cparams(("arbitrary",)),
    )(yraw, dyb, wmat, bias.reshape(1, C))


def _pool_counts(L, g):
    t = lax.broadcasted_iota(jnp.int32, (L, LANES), 0).astype(F32) + 1.0
    w = jnp.where(g == 0, 2.0, jnp.where(g == 1, 4.0, jnp.where(g == 2, 8.0, 16.0)))
    return 1.0 / jnp.minimum(t, w)


def _select_window(g, a2, a4, a8, a16):
    return jnp.where(g == 0, a2, jnp.where(g == 1, a4, jnp.where(g == 2, a8, a16)))


def _pooled(z, g):
    a2 = z + _down(z, 1)
    a4 = a2 + _down(a2, 2)
    a8 = a4 + _down(a4, 4)
    a16 = a8 + _down(a8, 8)
    return _select_window(g, a2, a4, a8, a16) * _pool_counts(z.shape[0], g) - z


def _pool_fwd(proj3, pool_w, scale, name):
    _, L, C = proj3.shape
    ng = len(POOL_WINDOWS)
    pg = C // ng
    assert pg == LANES

    def body(z_ref, w_ref, s_ref, o_ref):
        g = pl.program_id(0)
        p = _pooled(z_ref[...], g)
        y = jnp.dot(p.astype(BF16), w_ref[...].astype(BF16), preferred_element_type=F32)
        o_ref[...] = (y * s_ref[...]).astype(BF16)

    return pl.pallas_call(
        body, name=name, out_shape=jax.ShapeDtypeStruct((L, C), BF16), grid=(ng,),
        in_specs=[pl.BlockSpec((None, L, pg), lambda g: (0, 0, g)), pl.BlockSpec((None, pg, pg), lambda g: (g, 0, 0)),
                  pl.BlockSpec((1, pg), lambda g: (0, g))],
        out_specs=pl.BlockSpec((L, pg), lambda g: (0, g)), compiler_params=_cparams(("parallel",)),
    )(proj3, pool_w, scale.reshape(1, C))


def _pool_bwd(proj3, dmix, pool_w, scale, name):
    _, L, C = proj3.shape
    ng = len(POOL_WINDOWS)
    pg = C // ng

    def body(z_ref, d_ref, w_ref, s_ref, dz_ref, dw_ref, ds_ref):
        g = pl.program_id(0)
        p = _pooled(z_ref[...], g)
        pb = p.astype(BF16)
        wb = w_ref[...].astype(BF16)
        pre = jnp.dot(pb, wb, preferred_element_type=F32)
        dyc = d_ref[...]
        ds_ref[...] = jnp.sum(dyc * pre, axis=0, keepdims=True)
        dpre = (dyc * s_ref[...]).astype(BF16)
        dw_ref[...] = lax.dot_general(pb, dpre, (((0,), (0,)), ((), ())), preferred_element_type=F32)
        dp = lax.dot_general(dpre, wb, (((1,), (1,)), ((), ())), preferred_element_type=F32)
        v = dp * _pool_counts(L, g)
        a2 = v + _up(v, 1)
        a4 = a2 + _up(a2, 2)
        a8 = a4 + _up(a4, 4)
        a16 = a8 + _up(a8, 8)
        dz_ref[...] = (_select_window(g, a2, a4, a8, a16) - dp).astype(BF16)

    return pl.pallas_call(
        body, name=name,
        out_shape=(jax.ShapeDtypeStruct((L, C), BF16), jax.ShapeDtypeStruct((ng, pg, pg), F32),
                   jax.ShapeDtypeStruct((1, C), F32)),
        grid=(ng,),
        in_specs=[pl.BlockSpec((None, L, pg), lambda g: (0, 0, g)), pl.BlockSpec((L, pg), lambda g: (0, g)),
                  pl.BlockSpec((None, pg, pg), lambda g: (g, 0, 0)), pl.BlockSpec((1, pg), lambda g: (0, g))],
        out_specs=(pl.BlockSpec((L, pg), lambda g: (0, g)), pl.BlockSpec((None, pg, pg), lambda g: (g, 0, 0)),
                   pl.BlockSpec((1, pg), lambda g: (0, g))),
        compiler_params=_cparams(("parallel",)),
    )(proj3, dmix, pool_w, scale.reshape(1, C))


def _tril_w(w_ref, h):
    r = lax.broadcasted_iota(jnp.int32, (CHUNK, CHUNK), 0)
    c = lax.broadcasted_iota(jnp.int32, (CHUNK, CHUNK), 1)
    return jnp.where(r >= c, w_ref[h], 0.0)


def _sgu_fwd(proj3, norm_g, w, b, name):
    _, L, C = proj3.shape
    nh = w.shape[0]
    dh = C // nh
    assert dh == LANES and w.shape[1] == CHUNK
    tr = _pick(L, prefs=(512, 256, 128))
    bfull = jnp.broadcast_to(b[:, :, None], (nh, CHUNK, dh))

    def body(su_ref, sv_ref, g_ref, w_ref, b_ref, o_ref):
        sv = _gelu(sv_ref[...])
        r = lax.rsqrt(jnp.mean(sv * sv, axis=-1, keepdims=True) + EPS)
        v = (sv * r * g_ref[...]).astype(BF16)
        for h in range(nh):
            wm = _tril_w(w_ref, h).astype(BF16)
            cols = slice(h * dh, (h + 1) * dh)
            for n in range(tr // CHUNK):
                rows = slice(n * CHUNK, (n + 1) * CHUNK)
                mixed = jnp.dot(wm, v[rows, cols], preferred_element_type=F32) + b_ref[h]
                o_ref[rows, cols] = (_gelu(su_ref[rows, cols]) * mixed).astype(BF16)

    full = lambda shp: pl.BlockSpec(shp, lambda i: (0,) * len(shp))
    return pl.pallas_call(
        body, name=name, out_shape=jax.ShapeDtypeStruct((L, C), BF16), grid=(L // tr,),
        in_specs=[pl.BlockSpec((None, tr, C), lambda i: (1, i, 0)), pl.BlockSpec((None, tr, C), lambda i: (2, i, 0)),
                  full((1, C)), full((nh, CHUNK, CHUNK)), full((nh, CHUNK, dh))],
        out_specs=pl.BlockSpec((tr, C), lambda i: (i, 0)), compiler_params=_cparams(("parallel",)),
    )(proj3, proj3, norm_g.reshape(1, C), w, bfull)


def _sgu_bwd(proj3, dmix, norm_g, w, b, name):
    _, L, C = proj3.shape
    nh = w.shape[0]
    dh = C // nh
    tr = _pick(L, prefs=(512, 256, 128))
    nsteps = L // tr
    bfull = jnp.broadcast_to(b[:, :, None], (nh, CHUNK, dh))

    def body(su_ref, sv_ref, d_ref, g_ref, w_ref, b_ref, o_ref, dw_ref, db_ref, dg_ref, dv_ref, acc_g):
        i = pl.program_id(0)

        @pl.when(i == 0)
        def _():
            dw_ref[...] = jnp.zeros_like(dw_ref)
            db_ref[...] = jnp.zeros_like(db_ref)
            acc_g[...] = jnp.zeros_like(acc_g)

        svp = sv_ref[...]
        sv = _gelu(svp)
        r = lax.rsqrt(jnp.mean(sv * sv, axis=-1, keepdims=True) + EPS)
        vh = sv * r
        gv = g_ref[...]
        v = (vh * gv).astype(BF16)
        tri_r = lax.broadcasted_iota(jnp.int32, (CHUNK, CHUNK), 0)
        tri_c = lax.broadcasted_iota(jnp.int32, (CHUNK, CHUNK), 1)
        for h in range(nh):
            wm = _tril_w(w_ref, h).astype(BF16)
            cols = slice(h * dh, (h + 1) * dh)
            dwh = jnp.zeros((CHUNK, CHUNK), F32)
            dbh = jnp.zeros((CHUNK, dh), F32)
            for n in range(tr // CHUNK):
                rows = slice(n * CHUNK, (n + 1) * CHUNK)
                vb = v[rows, cols]
                mixed = jnp.dot(wm, vb, preferred_element_type=F32) + b_ref[h]
                sup = su_ref[rows, cols]
                dyd = d_ref[rows, cols]
                dmx = dyd * _gelu(sup)
                o_ref[0, rows, cols] = (dyd * mixed * _gelu_grad(sup)).astype(BF16)
                dmb = dmx.astype(BF16)
                dwh = dwh + lax.dot_general(dmb, vb, (((1,), (1,)), ((), ())), preferred_element_type=F32)
                dbh = dbh + dmx
                dv_ref[rows, cols] = lax.dot_general(wm, dmb, (((0,), (0,)), ((), ())), preferred_element_type=F32)
            dw_ref[h] += jnp.where(tri_r >= tri_c, dwh, 0.0)
            db_ref[h] += dbh
        dv = dv_ref[...]
        acc_g[...] += jnp.sum((dv * vh).reshape(tr // SUBLANES, SUBLANES, C), axis=0)
        dvg = dv * gv
        dsv = r * (dvg - vh * jnp.mean(dvg * vh, axis=-1, keepdims=True))
        o_ref[1] = (dsv * _gelu_grad(svp)).astype(BF16)

        @pl.when(i == nsteps - 1)
        def _():
            dg_ref[...] = jnp.sum(acc_g[...], axis=0, keepdims=True)

    full = lambda shp: pl.BlockSpec(shp, lambda i: (0,) * len(shp))
    return pl.pallas_call(
        body, name=name,
        out_shape=(jax.ShapeDtypeStruct((2, L, C), BF16), jax.ShapeDtypeStruct((nh, CHUNK, CHUNK), F32),
                   jax.ShapeDtypeStruct((nh, CHUNK, dh), F32), jax.ShapeDtypeStruct((1, C), F32)),
        grid=(nsteps,),
        in_specs=[pl.BlockSpec((None, tr, C), lambda i: (1, i, 0)), pl.BlockSpec((None, tr, C), lambda i: (2, i, 0)),
                  pl.BlockSpec((tr, C), lambda i: (i, 1)), full((1, C)), full((nh, CHUNK, CHUNK)),
                  full((nh, CHUNK, dh))],
        out_specs=(pl.BlockSpec((2, tr, C), lambda i: (0, i, 0)), full((nh, CHUNK, CHUNK)), full((nh, CHUNK, dh)),
                   full((1, C))),
        scratch_shapes=[pltpu.VMEM((tr, C), F32), pltpu.VMEM((SUBLANES, C), F32)],
        compiler_params=_cparams(("arbitrary",)),
    )(proj3, proj3, dmix, norm_g.reshape(1, C), w, bfull)


def _ffn_act_fwd(up3, conv_w, conv_b, name):
    _, L, Fh = up3.shape
    cb = LANES
    w2 = conv_w.reshape(3, 2, Fh).transpose(1, 0, 2)
    b2 = conv_b.reshape(2, 1, Fh)

    def body(u_ref, w_ref, b_ref, o_ref, ot_ref):
        g = _conv3(u_ref[0].astype(F32), w_ref[0]) + b_ref[0]
        v = _conv3(u_ref[1].astype(F32), w_ref[1]) + b_ref[1]
        a = g * _sigmoid(g) * v
        o_ref[...] = a.astype(BF16)
        ot_ref[...] = a.T.astype(BF16)

    return pl.pallas_call(
        body, name=name, out_shape=(jax.ShapeDtypeStruct((L, Fh), BF16), jax.ShapeDtypeStruct((Fh, L), BF16)),
        grid=(Fh // cb,),
        in_specs=[pl.BlockSpec((2, L, cb), lambda j: (0, 0, j)), pl.BlockSpec((2, 3, cb), lambda j: (0, 0, j)),
                  pl.BlockSpec((2, 1, cb), lambda j: (0, 0, j))],
        out_specs=(pl.BlockSpec((L, cb), lambda j: (0, j)), pl.BlockSpec((cb, L), lambda j: (j, 0))),
        compiler_params=_cparams(("parallel",)),
    )(up3, w2, b2)


def _ffn_act_bwd(up3, da, conv_w, conv_b, name):
    _, L, Fh = up3.shape
    cb = LANES
    w2 = conv_w.reshape(3, 2, Fh).transpose(1, 0, 2)
    b2 = conv_b.reshape(2, 1, Fh)

    def body(u_ref, d_ref, w_ref, b_ref, o_ref, dw_ref, db_ref):
        ug, uv = u_ref[0].astype(F32), u_ref[1].astype(F32)
        wg, wv = w_ref[0], w_ref[1]
        g = _conv3(ug, wg) + b_ref[0]
        v = _conv3(uv, wv) + b_ref[1]
        sg = _sigmoid(g)
        dav = d_ref[...].astype(F32)
        dg = dav * v * (sg * (1.0 + g * (1.0 - sg)))
        dv = dav * (g * sg)
        o_ref[0] = _conv3_t(dg, wg).astype(BF16)
        o_ref[1] = _conv3_t(dv, wv).astype(BF16)
        for tap, (dwg, dwv) in enumerate(zip(_conv3_dw(dg, ug), _conv3_dw(dv, uv))):
            dw_ref[0, tap:tap + 1, :] = dwg
            dw_ref[1, tap:tap + 1, :] = dwv
        db_ref[0] = jnp.sum(dg, axis=0, keepdims=True)
        db_ref[1] = jnp.sum(dv, axis=0, keepdims=True)

    dup, dw2, db2 = pl.pallas_call(
        body, name=name,
        out_shape=(jax.ShapeDtypeStruct((2, L, Fh), BF16), jax.ShapeDtypeStruct((2, 3, Fh), F32),
                   jax.ShapeDtypeStruct((2, 1, Fh), F32)),
        grid=(Fh // cb,),
        in_specs=[pl.BlockSpec((2, L, cb), lambda j: (0, 0, j)), pl.BlockSpec((L, cb), lambda j: (0, j)),
                  pl.BlockSpec((2, 3, cb), lambda j: (0, 0, j)), pl.BlockSpec((2, 1, cb), lambda j: (0, 0, j))],
        out_specs=(pl.BlockSpec((2, L, cb), lambda j: (0, 0, j)), pl.BlockSpec((2, 3, cb), lambda j: (0, 0, j)),
                   pl.BlockSpec((2, 1, cb), lambda j: (0, 0, j))),
        compiler_params=_cparams(("parallel",)),
    )(up3, da, w2, b2)
    return dup, dw2.transpose(1, 0, 2).reshape(3, 2 * Fh), db2.reshape(2 * Fh)


def _local_step(x, tgt, w, layer_weights, on_layer_grads):
    L, D = x.shape
    depth = w['norm_mix_g'].shape[0]
    saved = []
    for i in range(depth):
        j = i // 2
        wb = dict(layer_weights(2 * i, x))
        s = {'x': x, 'wb': wb}
        h, s['hT'] = _rms_fwd(x, w['norm_mix_g'][i], "mix_norm_fwd")
        if i % 2 == 0:
            proj4 = _mm(h, wb['even_w_in'], 'nn', F32, "even_in_fwd", ok=('seg', 4))
            s['proj'] = proj4
            ya = _sconv_fwd(proj4, w['even_conv_w'][j], "sconv_fwd")
            prm = (w['ssm_log_step'][j], w['ssm_a_re'][j], w['ssm_a_im'][j], w['ssm_b_re'][j], w['ssm_b_im'][j],
                   w['ssm_c_re'][j], w['ssm_c_im'][j])
            (lr, li, bmat, cmat), prep_vjp = jax.vjp(_s5_prep, *prm)
            u = _to_scan_order(proj4[3])
            yraw, s_re, s_im = _s5_fwd(u, lr, li, bmat, cmat, w['ssm_d'][j], "s5_fwd")
            yb = _glu_fwd(yraw, wb['ssm_glu_w'], w['ssm_glu_b'][j], "glu_fwd")
            s.update(u=u, yraw=yraw, s_re=s_re, s_im=s_im, s5=(lr, li, bmat, cmat), prep_vjp=prep_vjp)
            mixin = jnp.concatenate([ya, _from_scan_order(yb)], axis=1)
            x = _mm(mixin, wb['even_w_out'], 'nn', F32, "even_out_fwd", res=x)
        else:
            proj3 = _mm(h, wb['odd_w_in'], 'nn', F32, "odd_in_fwd", ok=('seg', 3))
            s['proj'] = proj3
            yc = _pool_fwd(proj3, w['pool_w'][j], w['pool_scale'][j], "pool_fwd")
            yd = _sgu_fwd(proj3, w['sgu_norm_g'][j], w['sgu_w'][j], w['sgu_b'][j], "sgu_fwd")
            mixin = jnp.concatenate([yc, yd], axis=1)
            x = _mm(mixin, wb['odd_w_out'], 'nn', F32, "odd_out_fwd", res=x)
        s['mixin'] = mixin
        s['x1'] = x
        wb.update(layer_weights(2 * i + 1, x))
        h2, h2t = _rms_fwd(x, w['norm_ffn_g'][i], "ffn_norm_fwd")
        up3 = _mm(h2, wb['ffn_w_up'], 'nn', BF16, "ffn_up_fwd", ok=('seg', 2))
        a, at = _ffn_act_fwd(up3, w['ffn_conv_w'][i], w['ffn_conv_b'][i], "ffn_act_fwd")
        x = _mm(a, wb['ffn_w_down'], 'nn', F32, "ffn_down_fwd", res=x)
        s.update(h2T=h2t, up3=up3, aT=at)
        saved.append(s)

    loss8, dx, dg_final = _loss_head(x, w['norm_final_g'], tgt)
    gs = {n: [None] * w[n].shape[0] for n in SMALL if n != 'norm_final_g'}
    gs['norm_final_g'] = dg_final.reshape(D)

    dep = None
    for i in reversed(range(depth)):
        j = i // 2
        s = saved[i]
        wb = s['wb']
        gb = {}
        da = _mm(dx, wb['ffn_w_down'], 'nt', BF16, "ffn_down_dgrad", dep=dep)
        gb['ffn_w_down'] = _mm(s['aT'], dx, 'nn', BF16, "ffn_down_wgrad")
        dup3, dcw, dcb = _ffn_act_bwd(s['up3'], da, w['ffn_conv_w'][i], w['ffn_conv_b'][i], "ffn_act_bwd")
        gs['ffn_conv_w'][i], gs['ffn_conv_b'][i] = dcw, dcb
        gb['ffn_w_up'] = _mm(s['h2T'], dup3, 'nn', BF16, "ffn_up_wgrad", bk=('seg', 2))
        dh2 = _mm(dup3, wb['ffn_w_up'], 'nt', F32, "ffn_up_dgrad", ak=('seg', 2))
        dx, dg = _rms_bwd(s['x1'], w['norm_ffn_g'][i], dh2, dx, "ffn_norm_bwd")
        gs['norm_ffn_g'][i] = dg.reshape(D)
        dep = on_layer_grads(2 * i + 1, gb)
        gb = {}
        if i % 2 == 0:
            dmix = _mm(dx, wb['even_w_out'], 'nt', F32, "even_out_dgrad", dep=dep)
            gb['even_w_out'] = _mm(s['mixin'].T, dx, 'nn', BF16, "even_out_wgrad")
            dpc, dcw = _sconv_bwd(s['proj'], dmix, w['even_conv_w'][j], "sconv_bwd")
            gs['even_conv_w'][j] = dcw
            dyb = _to_scan_order(dmix[:, D // 2:])
            dyraw, dglu_w, dglu_b = _glu_bwd(s['yraw'], dyb, wb['ssm_glu_w'], w['ssm_glu_b'][j], "glu_bwd")
            gb['ssm_glu_w'] = dglu_w.astype(BF16)
            gs['ssm_glu_b'][j] = dglu_b.reshape(-1)
            lr, li, bmat, cmat = s['s5']
            du, dbm, dcm, dlam, dd = _s5_bwd(dyraw, s['u'], s['s_re'], s['s_im'], lr, li, bmat, cmat,
                                            w['ssm_d'][j], "s5_bwd")
            gs['ssm_d'][j] = dd.reshape(-1)
            dcm = jnp.swapaxes(dcm, 1, 2)
            dprm = s['prep_vjp']((dlam[:, 0:1, :], dlam[:, 1:2, :], dbm, dcm))
            for n, gval in zip(('ssm_log_step', 'ssm_a_re', 'ssm_a_im', 'ssm_b_re', 'ssm_b_im', 'ssm_c_re',
                                'ssm_c_im'), dprm):
                gs[n][j] = gval
            dproj = jnp.concatenate([dpc, _from_scan_order(du).astype(BF16)[None]], axis=0)
            gb['even_w_in'] = _mm(s['hT'], dproj, 'nn', BF16, "even_in_wgrad", bk=('seg', 4))
            dh = _mm(dproj, wb['even_w_in'], 'nt', F32, "even_in_dgrad", ak=('seg', 4))
        else:
            dmix = _mm(dx, wb['odd_w_out'], 'nt', F32, "odd_out_dgrad", dep=dep)
            gb['odd_w_out'] = _mm(s['mixin'].T, dx, 'nn', BF16, "odd_out_wgrad")
            dz, dpw, dps = _pool_bwd(s['proj'], dmix, w['pool_w'][j], w['pool_scale'][j], "pool_bwd")
            gs['pool_w'][j], gs['pool_scale'][j] = dpw, dps.reshape(-1)
            dsuv, dsw, dsb, dsg = _sgu_bwd(s['proj'], dmix, w['sgu_norm_g'][j], w['sgu_w'][j], w['sgu_b'][j],
                                           "sgu_bwd")
            gs['sgu_w'][j], gs['sgu_b'][j], gs['sgu_norm_g'][j] = dsw, jnp.sum(dsb, axis=-1), dsg.reshape(-1)
            dproj = jnp.concatenate([dz[None], dsuv], axis=0)
            gb['odd_w_in'] = _mm(s['hT'], dproj, 'nn', BF16, "odd_in_wgrad", bk=('seg', 3))
            dh = _mm(dproj, wb['odd_w_in'], 'nt', F32, "odd_in_dgrad", ak=('seg', 3))
        dx, dg = _rms_bwd(s['x'], w['norm_mix_g'][i], dh, dx, "mix_norm_bwd")
        gs['norm_mix_g'][i] = dg.reshape(D)
        dep = on_layer_grads(2 * i, gb)

    gsmall = {n: (v if n == 'norm_final_g' else jnp.stack(v)) for n, v in gs.items()}
    return loss8[0, 0], dx, gsmall


_HBM = pl.BlockSpec(memory_space=pltpu.HBM)
_CHIP_FLIPS = ((0, 0), (1, 0), (0, 1), (1, 1))


def _coords():
    return lax.axis_index("x"), lax.axis_index("y"), lax.axis_index("c")


def _flip(v, f):
    return 1 - v if f else v


def _shard_of(ref, axis, s, width):
    start = pl.multiple_of(s * width, LANES if axis == ref.ndim - 1 else 16) if width % 16 == 0 else s * width
    idx = [slice(None)] * ref.ndim
    idx[axis] = pl.ds(start, width)
    return ref.at[tuple(idx)]


_SEM = pl.BlockSpec(memory_space=pltpu.SEMAPHORE)
_ANY = pl.BlockSpec(memory_space=pl.ANY)
_DATAFLOW = pltpu.SideEffectType.DATAFLOW_SIDE_EFFECTING


def _in_hbm(a):
    return pltpu.with_memory_space_constraint(a, pltpu.HBM)


def _model_layer(name, l):
    if name.startswith('ffn'):
        return l
    return 2 * l + 1 if name.startswith('odd') else 2 * l


def _place_quarter(shard, l, axis, chip):
    _, r, c = shard.shape
    tr = _pick(r, prefs=(512, 256, 128, 64, 32, 16))
    nrb = r // tr

    def body(chip_ref, i_ref, o_ref):
        o_ref[...] = i_ref[...]

    if axis == 1:
        out_shape, o_map = (r, c * N_CHIPS), (lambda i, s: (i, s[0]))
    else:
        out_shape, o_map = (r * N_CHIPS, c), (lambda i, s: (s[0] * nrb + i, 0))
    return pl.pallas_call(
        body, name="place_quarter", out_shape=jax.ShapeDtypeStruct(out_shape, shard.dtype),
        grid_spec=pltpu.PrefetchScalarGridSpec(
            num_scalar_prefetch=1, grid=(nrb,), in_specs=[pl.BlockSpec((None, tr, c), lambda i, s: (l, i, 0))],
            out_specs=pl.BlockSpec((tr, c), o_map)),
        compiler_params=_cparams(("parallel",)),
    )(chip, shard)


def _gather_copies(shard_refs, land_refs, sems, pairs, axes, group, g, landing_chip_of):
    x, y, c = _coords()
    out = []
    for j, k in enumerate(group):
        t, l = pairs[k]
        width = shard_refs[t].shape[axes[t] + 1]
        for f in (1, 2, 3):
            fx, fy = _CHIP_FLIPS[f]
            px, py = _flip(x, fx), _flip(y, fy)
            lx, ly = landing_chip_of(px, py)
            out.append(pltpu.make_async_remote_copy(
                src_ref=shard_refs[t].at[l], dst_ref=_shard_of(land_refs[k], axes[t], 2 * lx + ly, width),
                send_sem=sems[2 * g].at[3 * j + f - 1], recv_sem=sems[2 * g + 1].at[3 * j + f - 1],
                device_id=(px, py, c), device_id_type=MESH))
    return out


def _gather_start(shards, lands, pairs, axes, groups):
    nt, npair, ng = len(shards), len(pairs), len(groups)

    def body(*refs):
        shard_refs, land_refs = refs[:nt], refs[nt:nt + npair]
        sems = refs[nt + npair:nt + npair + 2 * ng]
        token = refs[-1]
        x, y, _ = _coords()
        for g, group in enumerate(groups):
            for cp in _gather_copies(shard_refs, land_refs, sems, pairs, axes, group, g, lambda px, py: (x, y)):
                cp.start()
        token[...] = jnp.zeros_like(token)

    sem_shapes = []
    for group in groups:
        sem_shapes += [pltpu.SemaphoreType.DMA((3 * len(group),)), pltpu.SemaphoreType.DMA((3 * len(group),))]
    thru = [pltpu.HBM(a.shape, a.dtype) for a in list(shards) + list(lands)]
    outs = pl.pallas_call(
        body, name="gather_start",
        out_shape=tuple(sem_shapes + thru + [jax.ShapeDtypeStruct((SUBLANES, LANES), F32)]),
        in_specs=[_HBM] * (nt + npair),
        out_specs=tuple([_SEM] * (2 * ng) + [_HBM] * (nt + npair) + [pl.BlockSpec(memory_space=pltpu.VMEM)]),
        input_output_aliases={i: 2 * ng + i for i in range(nt + npair)},
        compiler_params=pltpu.CompilerParams(has_side_effects=_DATAFLOW),
    )(*[_in_hbm(a) for a in list(shards) + list(lands)])
    sems = outs[:2 * ng]
    return sems, list(outs[2 * ng:2 * ng + nt]), list(outs[2 * ng + nt:2 * ng + nt + npair]), outs[-1]


def _gather_wait(g, shards, lands_g, send_sem, recv_sem, after, pairs, axes, group):
    nt, n = len(shards), len(group)

    def body(*refs):
        shard_refs, land_g = refs[:nt], refs[nt:nt + n]
        sems = {2 * g: refs[nt + n], 2 * g + 1: refs[nt + n + 1]}
        land_refs = {k: land_g[j] for j, k in enumerate(group)}
        for cp in _gather_copies(shard_refs, land_refs, sems, pairs, axes, group, g, lambda px, py: (px, py)):
            cp.wait_send()
            cp.wait_recv()

    thru = [pltpu.HBM(a.shape, a.dtype) for a in list(shards) + list(lands_g)]
    outs = pl.pallas_call(
        body, name=f"gather_wait_{g}", out_shape=tuple(thru),
        in_specs=[_HBM] * (nt + n) + [_SEM, _SEM, _ANY], out_specs=tuple([_HBM] * (nt + n)),
        input_output_aliases={i: i for i in range(nt + n)},
        compiler_params=pltpu.CompilerParams(has_side_effects=_DATAFLOW),
    )(*shards, *lands_g, send_sem, recv_sem, after)
    return list(outs[:nt]), list(outs[nt:])


N_SLOTS = N_DEV - 1


def _scatter_sends(grad_refs, land_refs, send_sem, recv_sem, meta):
    x, y, c = _coords()
    out = []
    for j, (axis, owner, q, width) in enumerate(meta):
        other = c if owner == 0 else 1 - c
        for f, (fx, fy) in enumerate(_CHIP_FLIPS):
            px, py = _flip(x, fx), _flip(y, fy)
            slot = f + 4 * other - 1
            out.append((other if f == 0 else None, pltpu.make_async_remote_copy(
                src_ref=_shard_of(grad_refs[j], axis, 2 * px + py, width), dst_ref=land_refs[j].at[q, slot],
                send_sem=send_sem.at[4 * j + f], recv_sem=recv_sem.at[N_SLOTS * j + slot],
                device_id=(px, py, owner), device_id_type=MESH)))
    return out


def _scatter_start(layer, grads, lands, meta):
    n = len(grads)
    uniq = []
    for a in lands:
        if not any(a is u for u in uniq):
            uniq.append(a)
    which = [next(k for k, u in enumerate(uniq) if u is a) for a in lands]
    nu = len(uniq)

    def body(*refs):
        grad_refs, land_u = refs[:n], refs[n:n + nu]
        send_sem, recv_sem = refs[n + nu], refs[n + nu + 1]
        for other, cp in _scatter_sends(grad_refs, [land_u[k] for k in which], send_sem, recv_sem, meta):
            if other is None:
                cp.start()
            else:
                pl.when(other == 1)(cp.start)
        refs[-1][...] = jnp.zeros_like(refs[-1])

    thru = [pltpu.HBM(a.shape, a.dtype) for a in list(grads) + uniq]
    outs = pl.pallas_call(
        body, name=f"scatter_start_{layer}",
        out_shape=tuple([pltpu.SemaphoreType.DMA((4 * n,)), pltpu.SemaphoreType.DMA((N_SLOTS * n,))] + thru
                        + [jax.ShapeDtypeStruct((SUBLANES, LANES), F32)]),
        in_specs=[_HBM] * (n + nu),
        out_specs=tuple([_SEM, _SEM] + [_HBM] * (n + nu) + [pl.BlockSpec(memory_space=pltpu.VMEM)]),
        input_output_aliases={i: 2 + i for i in range(n + nu)},
        compiler_params=pltpu.CompilerParams(has_side_effects=_DATAFLOW),
    )(*[_in_hbm(a) for a in list(grads) + uniq])
    new_lands = [outs[2 + n + k] for k in which]
    return outs[0], outs[1], list(outs[2:2 + n]), new_lands, outs[-1]


def _scatter_wait(started, lands):
    nl = len(lands)
    flat_grads = [g for s in started for g in s[2]]
    ng, ns = len(flat_grads), len(started)

    def body(*refs):
        land_refs = refs[:nl]
        grad_refs = refs[nl:nl + ng]
        sem_refs = refs[nl + ng:nl + ng + 2 * ns]
        _, _, c = _coords()
        off = 0
        for k, (_, _, grads, idx, meta) in enumerate(started):
            send_sem, recv_sem = sem_refs[2 * k], sem_refs[2 * k + 1]
            lr = [land_refs[i] for i in idx]
            for other, cp in _scatter_sends(grad_refs[off:off + len(grads)], lr, send_sem, recv_sem, meta):
                if other is None:
                    cp.wait_send()
                else:
                    pl.when(other == 1)(cp.wait_send)
            for j, (axis, owner, q, width) in enumerate(meta):
                mine = (c if owner == 0 else 1 - c) == 0

                @pl.when(mine)
                def _():
                    for slot in range(N_SLOTS):
                        land = lr[j].at[q, slot]
                        pltpu.make_async_remote_copy(
                            src_ref=land, dst_ref=land, send_sem=send_sem.at[0], recv_sem=recv_sem.at[N_SLOTS * j + slot],
                            device_id=_coords(), device_id_type=MESH).wait_recv()
            off += len(grads)

    args = list(lands) + flat_grads
    thru = [pltpu.HBM(a.shape, a.dtype) for a in args]
    sems = [s for st in started for s in st[:2]]
    outs = pl.pallas_call(
        body, name="scatter_wait", out_shape=tuple(thru), in_specs=[_HBM] * (nl + ng) + [_SEM] * (2 * ns),
        out_specs=tuple([_HBM] * (nl + ng)), input_output_aliases={i: i for i in range(nl + ng)},
        compiler_params=pltpu.CompilerParams(has_side_effects=_DATAFLOW),
    )(*args, *sems)
    return list(outs[:nl]), list(outs[nl:])


def _sum_and_share(recv, layer_grads, axis, chip, name):
    n, ns, r, c = recv.shape
    tr = _pick(r, prefs=(256, 128, 64, 32, 16))
    nr = r // tr
    nsteps = n * nr
    nlay = len(layer_grads)
    own_map = (lambda h, i, s: (i, s[0])) if axis == 1 else (lambda h, i, s: (s[0] * nr + i, 0))

    def body(chip_ref, i_ref, *rest):
        g_refs = rest[:nlay]
        o_ref, buf, loc_sems, send_sems, recv_sems = rest[nlay:]
        h, i = pl.program_id(0), pl.program_id(1)
        step = h * nr + i
        slot = step % 2
        x, y, core = _coords()
        layer = core * n + h
        own = g_refs[0][...]
        for l in range(1, nlay):
            own = jnp.where(layer == l, g_refs[l][...], own)

        def copies(sl):
            dst = o_ref.at[core * n + h, pl.ds(pl.multiple_of(i * tr, tr), tr), :]
            loc = pltpu.make_async_copy(buf.at[sl], dst, loc_sems.at[sl])
            rem = pltpu.make_async_remote_copy(
                src_ref=buf.at[sl], dst_ref=dst, send_sem=send_sems.at[sl], recv_sem=recv_sems.at[step],
                device_id=(x, y, 1 - core), device_id_type=MESH)
            return loc, rem

        def drain(sl):
            loc, rem = copies(sl)
            loc.wait()
            rem.wait_send()

        pl.when(step >= 2)(lambda: drain(slot))
        acc = own.astype(F32)
        for s in range(ns):
            acc = acc + i_ref[s].astype(F32)
        buf[slot] = acc
        loc, rem = copies(slot)
        loc.start()
        rem.start()

        @pl.when(step == nsteps - 1)
        def _():
            drain(slot)
            if nsteps > 1:
                drain(1 - slot)
            for hh in range(n):
                for ii in range(nr):
                    land = o_ref.at[(1 - core) * n + hh, pl.ds(ii * tr, tr), :]
                    pltpu.make_async_remote_copy(
                        src_ref=buf.at[0], dst_ref=land, send_sem=send_sems.at[0], recv_sem=recv_sems.at[hh * nr + ii],
                        device_id=(x, y, 1 - core), device_id_type=MESH).wait_recv()

    return pl.pallas_call(
        body, name=name, out_shape=jax.ShapeDtypeStruct((2 * n, r, c), F32),
        grid_spec=pltpu.PrefetchScalarGridSpec(
            num_scalar_prefetch=1, grid=(n, nr),
            in_specs=[pl.BlockSpec((None, ns, tr, c), lambda h, i, s: (h, 0, i, 0))]
            + [pl.BlockSpec((tr, c), own_map)] * nlay,
            out_specs=_HBM,
            scratch_shapes=[pltpu.VMEM((2, tr, c), F32), pltpu.SemaphoreType.DMA((2,)),
                            pltpu.SemaphoreType.DMA((2,)), pltpu.SemaphoreType.DMA((nsteps,))]),
        compiler_params=_cparams(("arbitrary", "arbitrary")),
    )(chip, recv, *layer_grads)


def _gather_sums_over_chips(part):
    def body(i_ref, o_ref, send_sems, recv_sems):
        x, y, c = _coords()
        o_ref[2 * x + y] = i_ref[...]

        def copy(f, slot_chip):
            fx, fy = _CHIP_FLIPS[f]
            return pltpu.make_async_remote_copy(
                src_ref=i_ref, dst_ref=o_ref.at[2 * slot_chip[0] + slot_chip[1]], send_sem=send_sems.at[f - 1],
                recv_sem=recv_sems.at[f - 1], device_id=(_flip(x, fx), _flip(y, fy), c), device_id_type=MESH)

        sends = [copy(f, (x, y)) for f in (1, 2, 3)]
        for cp in sends:
            cp.start()
        for f in (1, 2, 3):
            fx, fy = _CHIP_FLIPS[f]
            copy(f, (_flip(x, fx), _flip(y, fy))).wait_recv()
        for cp in sends:
            cp.wait_send()

    vmem = pl.BlockSpec(memory_space=pltpu.VMEM)
    return pl.pallas_call(
        body, name="gather_small_sums", out_shape=jax.ShapeDtypeStruct((N_CHIPS,) + part.shape, part.dtype),
        in_specs=[vmem], out_specs=vmem,
        scratch_shapes=[pltpu.SemaphoreType.DMA((3,)), pltpu.SemaphoreType.DMA((3,))],
    )(part)


def _adamw(w, g, m, v, name):
    bc1 = 1.0 - ADAM_B1 ** ADAM_STEP
    bc2 = 1.0 - ADAM_B2 ** ADAM_STEP

    def body(w_ref, g_ref, m_ref, v_ref, d_ref, mo_ref, vo_ref):
        gv = g_ref[...]
        mn = ADAM_B1 * m_ref[...] + (1.0 - ADAM_B1) * gv
        vn = ADAM_B2 * v_ref[...] + (1.0 - ADAM_B2) * (gv * gv)
        d_ref[...] = -ADAM_LR * ((mn / bc1) / (jnp.sqrt(vn / bc2) + ADAM_EPS) + ADAM_WD * w_ref[...])
        mo_ref[...] = mn
        vo_ref[...] = vn

    sds = jax.ShapeDtypeStruct(w.shape, F32)
    if w.ndim == 2 and w.shape[0] % SUBLANES == 0:
        tr = _pick(w.shape[0], prefs=(256, 128, 64, 32, 16, 8))
        grid, blk = (w.shape[0] // tr,), pl.BlockSpec((tr, w.shape[1]), lambda i: (i, 0))
    else:
        nd = w.ndim
        grid, blk = (1,), pl.BlockSpec(w.shape, lambda i: (0,) * nd)
    return pl.pallas_call(
        body, name=name, out_shape=(sds, sds, sds), grid=grid, in_specs=[blk] * 4, out_specs=(blk,) * 3,
        compiler_params=_cparams(("parallel",)),
    )(w, g, m, v)


_PACK_QUANTUM = 256 * LANES


def _pack(arrs):
    flat = jnp.concatenate([a.reshape(-1).astype(F32) for a in arrs])
    flat = jnp.pad(flat, (0, (-flat.shape[0]) % _PACK_QUANTUM))
    return flat.reshape(-1, LANES)


def _unpack(p, shapes):
    flat = p.reshape(-1)
    out, off = [], 0
    for s in shapes:
        n = int(np.prod(s))
        out.append(flat[off:off + n].reshape(s))
        off += n
    return out


def kernel(*args):
    nw = len(WEIGHTS)
    x, tgt = args[0], args[1 + nw]
    w = dict(zip(WEIGHTS, args[1:1 + nw]))
    m = dict(zip(WEIGHTS, args[2 + nw:2 + 2 * nw]))
    v = dict(zip(WEIGHTS, args[2 + 2 * nw:2 + 3 * nw]))
    _, L, D = x.shape
    chip = 2 * lax.axis_index("x") + lax.axis_index("y")

    big = list(BIG)
    small_sh_shapes = [w[n].shape for n in SMALL_SHARDED]
    nbig = len(big)
    chip1 = chip.reshape(1).astype(jnp.int32)
    axes2 = [BIG[n] - 1 for n in big] + [0]
    pack_sh = _pack([w[n] for n in SMALL_SHARDED])
    shards = [w[n].astype(BF16) for n in big] + [pack_sh[None]]
    pairs = [(t, l) for t in range(nbig + 1) for l in range(shards[t].shape[0])]
    depth = w['norm_mix_g'].shape[0]
    part_of = lambda t, l: 0 if t == nbig else 2 * _model_layer(big[t], l) + big[t].startswith('ffn')
    groups = [[k for k, (t, l) in enumerate(pairs) if part_of(t, l) == g] for g in range(2 * depth)]
    placed = [_place_quarter(shards[t], l, axes2[t], chip1) for t, l in pairs]
    sems, shards_thru, lands, token = _gather_start(shards, placed, pairs, axes2, groups)
    in_flight = {'shards': shards_thru}

    def wait_group(i, after):
        sh, landed = _gather_wait(i, in_flight['shards'], [lands[k] for k in groups[i]], sems[2 * i], sems[2 * i + 1],
                                  after, pairs, axes2, groups[i])
        in_flight['shards'] = sh
        return {pairs[k][0]: a for k, a in zip(groups[i], landed)}

    first = wait_group(0, token)
    packed = first.pop(nbig).reshape(N_CHIPS, -1, LANES)
    per_chip = [_unpack(packed[s], small_sh_shapes) for s in range(N_CHIPS)]
    wl = dict(w)
    for k, n in enumerate(SMALL_SHARDED):
        wl[n] = jnp.concatenate([per_chip[s][k] for s in range(N_CHIPS)], axis=-1)

    def layer_weights(i, after):
        got = first if i == 0 else wait_group(i, after)
        return {big[t]: a for t, a in got.items()}

    small_shapes = [(w[n].shape[:-1] + (w[n].shape[-1] * N_CHIPS,)) if n in SMALL_SHARDED else w[n].shape
                    for n in SMALL] + [(1,)]
    n_small = sum(int(np.prod(s)) for s in small_shapes)
    pack_rows = -(-n_small // _PACK_QUANTUM) * _PACK_QUANTUM // LANES
    nlayers = [w[n].shape[0] for n in big] + [2]
    halves = [n // 2 for n in nlayers]
    quarters = [tuple(w[n].shape[1:]) for n in big] + [(pack_rows // 2 // N_CHIPS, LANES)]
    wire = [BF16] * nbig + [F32]
    land_now = [lax.empty((halves[t], N_SLOTS) + quarters[t], wire[t]) for t in range(nbig + 1)]
    gparts = [[None] * n for n in nlayers]
    started = []

    def start_scatter(tag, ts, ls, arrays):
        meta = [(axes2[t], l // halves[t], l % halves[t], quarters[t][axes2[t]]) for t, l in zip(ts, ls)]
        send, recv, thru, new_lands, token = _scatter_start(tag, arrays, [land_now[t] for t in ts], meta)
        for t, ln in zip(ts, new_lands):
            land_now[t] = ln
        started.append((send, recv, thru, ts, meta, ls))
        return token

    def on_layer_grads(g, gb):
        ts = [big.index(n) for n in gb]
        return start_scatter(g, ts, [g // 2 if big[t].startswith('ffn') else g // 4 for t in ts],
                             [gb[big[t]] for t in ts])

    loss, dx, gsmall = _local_step(x.reshape(L, D), tgt.reshape(L, D), wl, layer_weights, on_layer_grads)
    gpack = _pack([gsmall[n] for n in SMALL] + [loss.reshape(1)])
    start_scatter(2 * depth, [nbig, nbig], [0, 1], [gpack[:pack_rows // 2], gpack[pack_rows // 2:]])
    landed, sent = _scatter_wait([s[:5] for s in started], land_now)
    for (t, l), g in zip([(t, l) for s in started for t, l in zip(s[3], s[5])], sent):
        gparts[t][l] = g
    gshard = {n: _sum_and_share(landed[t], gparts[t], axes2[t], chip1, "sum_share_" + n) for t, n in enumerate(big)}
    small_sum = _sum_and_share(landed[nbig], gparts[nbig], 0, chip1, "sum_share_small")
    gpack = _gather_sums_over_chips(small_sum).transpose(1, 0, 2, 3).reshape(pack_rows, LANES)
    gs = dict(zip(SMALL + ['loss'], _unpack(gpack, small_shapes)))
    loss = gs.pop('loss').reshape(())
    for n in SMALL_SHARDED:
        width = w[n].shape[-1]
        gs[n] = lax.dynamic_slice_in_dim(gs[n], chip * width, width, axis=gs[n].ndim - 1)

    grads, delta, new_m, new_v = {}, {}, {}, {}
    for n in big:
        shp = w[n].shape
        flat = lambda a: a.reshape(shp[0] * shp[1], shp[2])
        g = gshard[n]
        grads[n] = g
        d_, m_, v_ = _adamw(flat(w[n]), flat(g), flat(m[n]), flat(v[n]), "adamw_" + n)
        delta[n], new_m[n], new_v[n] = d_.reshape(shp), m_.reshape(shp), v_.reshape(shp)
    for n in SMALL:
        shp = w[n].shape
        as2d = (lambda a: a.reshape(1, -1)) if len(shp) == 1 else (lambda a: a)
        d_, m_, v_ = _adamw(as2d(w[n]), as2d(gs[n]), as2d(m[n]), as2d(v[n]), "adamw_" + n)
        grads[n], delta[n], new_m[n], new_v[n] = gs[n], d_.reshape(shp), m_.reshape(shp), v_.reshape(shp)

    return (loss, dx.reshape(1, L, D), *[grads[n] for n in WEIGHTS], *[delta[n] for n in WEIGHTS],
            *[new_m[n] for n in WEIGHTS], *[new_v[n] for n in WEIGHTS])
```

```python
import functools
import math

import numpy as np
import jax
import jax.numpy as jnp
from jax import lax
from jax.experimental import pallas as pl
from jax.experimental.pallas import tpu as pltpu

F32 = jnp.float32
BF16 = jnp.bfloat16
MESH = pl.DeviceIdType.MESH

EPS = 1e-6
CHUNK = 128
POOL_WINDOWS = (2, 4, 8, 16)
LANES = 128
SUBLANES = 8
SCAN_CHUNKS = SUBLANES
S5_GROUPS_PER_STEP = 4
MM_TM_CAP, MM_TN_CAP, MM_TK_CAP = 1408, 1408, 1408
MM_TK_WHOLE = 2048
VMEM_LIMIT = 48 * 1024 * 1024
VMEM_LIMIT_S5 = 56 * 1024 * 1024

ADAM_LR, ADAM_B1, ADAM_B2, ADAM_EPS, ADAM_WD, ADAM_STEP = 0.001, 0.9, 0.999, 1e-08, 0.01, 10

WEIGHTS = ['norm_mix_g', 'even_w_in', 'even_conv_w', 'ssm_log_step', 'ssm_a_re', 'ssm_a_im', 'ssm_b_re',
           'ssm_b_im', 'ssm_c_re', 'ssm_c_im', 'ssm_d', 'ssm_glu_w', 'ssm_glu_b', 'even_w_out', 'odd_w_in',
           'pool_w', 'pool_scale', 'sgu_norm_g', 'sgu_w', 'sgu_b', 'odd_w_out', 'norm_ffn_g', 'ffn_w_up',
           'ffn_conv_w', 'ffn_conv_b', 'ffn_w_down', 'norm_final_g']
BIG = {'even_w_in': 2, 'ssm_glu_w': 1, 'even_w_out': 1, 'odd_w_in': 2, 'odd_w_out': 1, 'ffn_w_up': 2,
       'ffn_w_down': 1}
SMALL_SHARDED = ('even_conv_w', 'pool_scale', 'sgu_norm_g', 'ffn_conv_w')
SMALL = [n for n in WEIGHTS if n not in BIG]
N_CHIPS = 4
N_DEV = 8


def _cparams(sem=None, vmem=VMEM_LIMIT):
    kw = dict(vmem_limit_bytes=vmem)
    if sem is not None:
        kw['dimension_semantics'] = sem
    return pltpu.CompilerParams(**kw)


def _pick(n, segs=(), prefs=(1024, 512, 256, 128)):
    for t in prefs:
        if n % t == 0 and all(s % t == 0 for s in segs if s):
            return t
    return n


def _largest_tile(n, segs, cap):
    best = None
    for t in range(LANES, min(n, cap) + 1, LANES):
        if n % t == 0 and all(s % t == 0 for s in segs if s):
            best = t
    return best if best is not None else n


def _ldims(arr, kind):
    if kind is None:
        return arr.shape
    if kind[0] == 'lead':
        return arr.shape[1:]
    return (arr.shape[1], arr.shape[0] * arr.shape[2])


def _segw(arr, kind):
    return arr.shape[2] if (kind is not None and kind[0] == 'seg') else None


def _opspec(arr, kind, br, bc, rfn, cfn):
    if kind is None:
        return pl.BlockSpec((br, bc), lambda i, j, k: (rfn(i, j, k), cfn(i, j, k)))
    if kind[0] == 'lead':
        lead = kind[1]
        return pl.BlockSpec((None, br, bc), lambda i, j, k: (lead, rfn(i, j, k), cfn(i, j, k)))
    per = arr.shape[2] // bc
    return pl.BlockSpec((None, br, bc), lambda i, j, k: (cfn(i, j, k) // per, rfn(i, j, k), cfn(i, j, k) % per))


def _mm(a, b, mode, out_dtype, name, ak=None, bk=None, ok=None, res=None, dep=None):
    ar, ac = _ldims(a, ak)
    br_, bc_ = _ldims(b, bk)
    if mode == 'nn':
        M, K, N = ar, ac, bc_
        assert br_ == K
    else:
        M, K, N = ar, ac, br_
        assert bc_ == K
    sa, sb = _segw(a, ak), _segw(b, bk)
    so = (N // ok[1]) if ok is not None else None
    tm = _largest_tile(M, [], MM_TM_CAP)
    tn = _largest_tile(N, [sb if mode == 'nn' else None, so], MM_TN_CAP)
    ksegs = [sa, sb if mode == 'nt' else None]
    tk = K if (K <= MM_TK_WHOLE and not any(ksegs)) else _largest_tile(K, ksegs, MM_TK_CAP)
    nk = K // tk
    I = lambda i, j, k: i
    J = lambda i, j, k: j
    Kk = lambda i, j, k: k
    a_spec = _opspec(a, ak, tm, tk, I, Kk)
    if mode == 'nn':
        b_spec = _opspec(b, bk, tk, tn, Kk, J)
        dims = (((1,), (0,)), ((), ()))
    else:
        b_spec = _opspec(b, bk, tn, tk, J, Kk)
        dims = (((1,), (1,)), ((), ()))
    if ok is None:
        out_shape = jax.ShapeDtypeStruct((M, N), out_dtype)
        o_spec = pl.BlockSpec((tm, tn), lambda i, j, k: (i, j))
    else:
        out_shape = jax.ShapeDtypeStruct((ok[1], M, N // ok[1]), out_dtype)
        per = (N // ok[1]) // tn
        o_spec = pl.BlockSpec((None, tm, tn), lambda i, j, k: (j // per, i, j % per))
    has_res = res is not None

    def body(*refs):
        a_ref, b_ref = refs[0], refs[1]
        r_ref = refs[2] if has_res else None
        o_ref = refs[n_in]
        prod = lax.dot_general(a_ref[...].astype(BF16), b_ref[...].astype(BF16), dims, preferred_element_type=F32)
        if nk == 1:
            o_ref[...] = (prod + r_ref[...] if has_res else prod).astype(out_dtype)
            return
        acc = refs[-1]
        k = pl.program_id(2)

        @pl.when(k == 0)
        def _():
            acc[...] = prod

        @pl.when(k > 0)
        def _():
            acc[...] += prod

        @pl.when(k == nk - 1)
        def _():
            o = acc[...]
            if has_res:
                o = o + r_ref[...]
            o_ref[...] = o.astype(out_dtype)

    in_specs = [a_spec, b_spec]
    args = [a, b]
    if has_res:
        in_specs.append(pl.BlockSpec((tm, tn), lambda i, j, k: (i, j)))
        args.append(res)
    if dep is not None:
        in_specs.append(pl.BlockSpec(memory_space=pl.ANY))
        args.append(dep)
    n_in = len(args)
    return pl.pallas_call(
        body, name=name, out_shape=out_shape, grid=(M // tm, N // tn, nk), in_specs=in_specs, out_specs=o_spec,
        scratch_shapes=[pltpu.VMEM((tm, tn), F32)] if nk > 1 else [],
        compiler_params=_cparams(("parallel", "parallel", "arbitrary")),
    )(*args)


_G0 = math.sqrt(2.0 / math.pi)
_G1 = 0.044715


def _gelu(x):
    return 0.5 * x * (1.0 + jnp.tanh(_G0 * (x + _G1 * x * x * x)))


def _gelu_grad(x):
    x2 = x * x
    t = jnp.tanh(_G0 * (x + _G1 * x * x2))
    return 0.5 * (1.0 + t) + 0.5 * x * (1.0 - t * t) * (_G0 * (1.0 + 3.0 * _G1 * x2))


def _sigmoid(x):
    return 1.0 / (1.0 + jnp.exp(-x))


def _down(v, k):
    r = pltpu.roll(v, k, axis=0)
    row = lax.broadcasted_iota(jnp.int32, (SUBLANES, v.shape[1]), 0)
    return jnp.concatenate([jnp.where(row >= k, r[:SUBLANES], 0.0), r[SUBLANES:]], axis=0)


def _up(v, k):
    n = v.shape[0]
    r = pltpu.roll(v, n - k, axis=0)
    row = lax.broadcasted_iota(jnp.int32, (SUBLANES, v.shape[1]), 0)
    return jnp.concatenate([r[:n - SUBLANES], jnp.where(row < SUBLANES - k, r[n - SUBLANES:], 0.0)], axis=0)


def _conv3(v, w):
    return w[0:1, :] * _down(v, 2) + w[1:2, :] * _down(v, 1) + w[2:3, :] * v


def _conv3_t(dv, w):
    return w[2:3, :] * dv + w[1:2, :] * _up(dv, 1) + w[0:1, :] * _up(dv, 2)


def _conv3_dw(dv, v):
    return (jnp.sum(dv * _down(v, 2), axis=0, keepdims=True),
            jnp.sum(dv * _down(v, 1), axis=0, keepdims=True),
            jnp.sum(dv * v, axis=0, keepdims=True))


def _cmul(ar, ai, br, bi):
    return ar * br - ai * bi, ar * bi + ai * br


def _cpow(lr, li, n):
    rr = ri = None
    br, bi = lr, li
    while n:
        if n & 1:
            rr, ri = (br, bi) if rr is None else _cmul(rr, ri, br, bi)
        n >>= 1
        if n:
            br, bi = _cmul(br, bi, br, bi)
    return rr, ri


NORM_ROWS = 256


def _norm_mm(x, g, b, out_dtype, name, ok=None):
    M, D = x.shape
    N = b.shape[1]
    so = (N // ok[1]) if ok is not None else None
    tm = _largest_tile(M, [], 1024)
    tn = _largest_tile(N, [so], MM_TN_CAP)
    if ok is None:
        out_shape = jax.ShapeDtypeStruct((M, N), out_dtype)
        o_spec = pl.BlockSpec((tm, tn), lambda i, j: (i, j))
    else:
        out_shape = jax.ShapeDtypeStruct((ok[1], M, N // ok[1]), out_dtype)
        per = (N // ok[1]) // tn
        o_spec = pl.BlockSpec((None, tm, tn), lambda i, j: (j // per, i, j % per))

    def body(x_ref, g_ref, b_ref, o_ref, ht_ref, h_scr):
        @pl.when(pl.program_id(1) == 0)
        def _():
            for c in range(tm // NORM_ROWS):
                rows = pl.ds(c * NORM_ROWS, NORM_ROWS)
                xv = x_ref[rows, :]
                h = xv * lax.rsqrt(jnp.mean(xv * xv, axis=-1, keepdims=True) + EPS) * g_ref[...]
                h_scr[rows, :] = h.astype(BF16)
                ht_ref[:, rows] = h.T.astype(BF16)

        o_ref[...] = jnp.dot(h_scr[...], b_ref[...], preferred_element_type=F32).astype(out_dtype)

    return pl.pallas_call(
        body, name=name, out_shape=(out_shape, jax.ShapeDtypeStruct((D, M), BF16)), grid=(M // tm, N // tn),
        in_specs=[pl.BlockSpec((tm, D), lambda i, j: (i, 0)), pl.BlockSpec((1, D), lambda i, j: (0, 0)),
                  pl.BlockSpec((D, tn), lambda i, j: (0, j))],
        out_specs=(o_spec, pl.BlockSpec((D, tm), lambda i, j: (0, i))),
        scratch_shapes=[pltpu.VMEM((tm, D), BF16)], compiler_params=_cparams(("parallel", "arbitrary")),
    )(x, g.reshape(1, D), b)


def _mm_norm_bwd(a, b, x, g, dres, name, ak=None, dep=None):
    M, K = _ldims(a, ak)
    D = b.shape[0]
    assert b.shape[1] == K and x.shape == (M, D)
    sa = _segw(a, ak)
    tm = _largest_tile(M, [], 1024)
    tk = K if (K <= MM_TK_WHOLE and not sa) else _largest_tile(K, [sa], MM_TK_CAP)
    ni, nk = M // tm, K // tk
    a3 = _opspec(a, ak, tm, tk, lambda i, j, k: i, lambda i, j, k: k)
    a_spec = pl.BlockSpec(a3.block_shape, lambda i, k: a3.index_map(i, 0, k))
    n_in = 5 + (dep is not None)

    def body(*refs):
        a_ref, b_ref, x_ref, g_ref, r_ref = refs[:5]
        dx_ref, dg_ref, acc, accg = refs[n_in:]
        i, k = pl.program_id(0), pl.program_id(1)
        prod = lax.dot_general(a_ref[...].astype(BF16), b_ref[...], (((1,), (1,)), ((), ())),
                               preferred_element_type=F32)

        @pl.when(k == 0)
        def _():
            acc[...] = prod

        @pl.when(k > 0)
        def _():
            acc[...] += prod

        @pl.when((i == 0) & (k == 0))
        def _():
            accg[...] = jnp.zeros_like(accg)

        @pl.when(k == nk - 1)
        def _():
            for c in range(tm // NORM_ROWS):
                rows = pl.ds(c * NORM_ROWS, NORM_ROWS)
                xv = x_ref[rows, :]
                r = lax.rsqrt(jnp.mean(xv * xv, axis=-1, keepdims=True) + EPS)
                xh = xv * r
                dhv = acc[rows, :]
                accg[...] += jnp.sum((dhv * xh).reshape(NORM_ROWS // SUBLANES, SUBLANES, D), axis=0)
                dxh = dhv * g_ref[...]
                dx_ref[rows, :] = r_ref[rows, :] + r * (dxh - xh * jnp.mean(dxh * xh, axis=-1, keepdims=True))

        @pl.when((i == ni - 1) & (k == nk - 1))
        def _():
            dg_ref[...] = jnp.sum(accg[...], axis=0, keepdims=True)

    row = pl.BlockSpec((tm, D), lambda i, k: (i, 0))
    vec = pl.BlockSpec((1, D), lambda i, k: (0, 0))
    in_specs = [a_spec, pl.BlockSpec((D, tk), lambda i, k: (0, k)), row, vec, row]
    args = [a, b, x, g.reshape(1, D), dres]
    if dep is not None:
        in_specs.append(pl.BlockSpec(memory_space=pl.ANY))
        args.append(dep)
    return pl.pallas_call(
        body, name=name, out_shape=(jax.ShapeDtypeStruct((M, D), F32), jax.ShapeDtypeStruct((1, D), F32)),
        grid=(ni, nk), in_specs=in_specs, out_specs=(row, vec),
        scratch_shapes=[pltpu.VMEM((tm, D), F32), pltpu.VMEM((SUBLANES, D), F32)],
        compiler_params=_cparams(("arbitrary", "arbitrary"), VMEM_LIMIT_S5),
    )(*args)


def _loss_head(x, g, tgt):
    L, D = x.shape
    tr = _pick(L, prefs=(512, 256, 128))
    nsteps = L // tr

    def body(x_ref, g_ref, t_ref, loss_ref, dx_ref, dg_ref, acc_g, acc_l):
        i = pl.program_id(0)

        @pl.when(i == 0)
        def _():
            acc_g[...] = jnp.zeros_like(acc_g)
            acc_l[...] = jnp.zeros_like(acc_l)

        xv = x_ref[...]
        gv = g_ref[...]
        r = lax.rsqrt(jnp.mean(xv * xv, axis=-1, keepdims=True) + EPS)
        xh = xv * r
        e = xh * gv - t_ref[...]
        acc_l[...] += jnp.sum((e * e).reshape(tr // SUBLANES, SUBLANES, D), axis=0)
        dy = e * (1.0 / D)
        acc_g[...] += jnp.sum((dy * xh).reshape(tr // SUBLANES, SUBLANES, D), axis=0)
        dxh = dy * gv
        dx_ref[...] = r * (dxh - xh * jnp.mean(dxh * xh, axis=-1, keepdims=True))

        @pl.when(i == nsteps - 1)
        def _():
            dg_ref[...] = jnp.sum(acc_g[...], axis=0, keepdims=True)
            tot = jnp.sum(jnp.sum(acc_l[...], axis=0, keepdims=True), axis=1, keepdims=True) * (0.5 / D)
            loss_ref[...] = jnp.broadcast_to(tot, (SUBLANES, LANES))

    row = pl.BlockSpec((tr, D), lambda i: (i, 0))
    vec = pl.BlockSpec((1, D), lambda i: (0, 0))
    return pl.pallas_call(
        body, name="loss_head",
        out_shape=(jax.ShapeDtypeStruct((SUBLANES, LANES), F32), jax.ShapeDtypeStruct((L, D), F32),
                   jax.ShapeDtypeStruct((1, D), F32)),
        grid=(nsteps,), in_specs=[row, vec, row],
        out_specs=(pl.BlockSpec((SUBLANES, LANES), lambda i: (0, 0)), row, vec),
        scratch_shapes=[pltpu.VMEM((SUBLANES, D), F32), pltpu.VMEM((SUBLANES, D), F32)],
        compiler_params=_cparams(("arbitrary",)),
    )(x, g.reshape(1, D), tgt)


def _sconv_fwd(proj4, conv_w, name):
    _, L, C = proj4.shape
    cb = LANES

    def body(p_ref, w_ref, o_ref):
        xa, ba, ca = p_ref[0], p_ref[1], p_ref[2]
        o_ref[...] = (ba * _conv3(ca * xa, w_ref[...])).astype(BF16)

    return pl.pallas_call(
        body, name=name, out_shape=jax.ShapeDtypeStruct((L, C), BF16), grid=(C // cb,),
        in_specs=[pl.BlockSpec((3, L, cb), lambda j: (0, 0, j)), pl.BlockSpec((3, cb), lambda j: (0, j))],
        out_specs=pl.BlockSpec((L, cb), lambda j: (0, j)), compiler_params=_cparams(("parallel",)),
    )(proj4, conv_w)


def _sconv_bwd(proj4, dmix, conv_w, name):
    _, L, C = proj4.shape
    cb = LANES

    def body(p_ref, d_ref, w_ref, o_ref, dw_ref):
        xa, ba, ca = p_ref[0], p_ref[1], p_ref[2]
        w = w_ref[...]
        dya = d_ref[...]
        q = ca * xa
        cq = _conv3(q, w)
        dcq = dya * ba
        dq = _conv3_t(dcq, w)
        for tap, dwt in enumerate(_conv3_dw(dcq, q)):
            dw_ref[tap:tap + 1, :] = dwt
        o_ref[0] = (dq * ca).astype(BF16)
        o_ref[1] = (dya * cq).astype(BF16)
        o_ref[2] = (dq * xa).astype(BF16)

    return pl.pallas_call(
        body, name=name,
        out_shape=(jax.ShapeDtypeStruct((3, L, C), BF16), jax.ShapeDtypeStruct((3, C), F32)), grid=(C // cb,),
        in_specs=[pl.BlockSpec((3, L, cb), lambda j: (0, 0, j)), pl.BlockSpec((L, cb), lambda j: (0, j)),
                  pl.BlockSpec((3, cb), lambda j: (0, j))],
        out_specs=(pl.BlockSpec((3, L, cb), lambda j: (0, 0, j)), pl.BlockSpec((3, cb), lambda j: (0, j))),
        compiler_params=_cparams(("parallel",)),
    )(proj4, dmix, conv_w)


def _to_scan_order(v):
    L, C = v.shape
    return v.reshape(SCAN_CHUNKS, L // SCAN_CHUNKS, C).transpose(1, 0, 2).reshape(L, C)


def _from_scan_order(v):
    L, C = v.shape
    return v.reshape(L // SCAN_CHUNKS, SCAN_CHUNKS, C).transpose(1, 0, 2).reshape(L, C)


def _s5_prep(log_step, a_re, a_im, b_re, b_im, c_re, c_im):
    G, P = a_re.shape
    H = b_re.shape[-1]
    gs = S5_GROUPS_PER_STEP
    ns = G // gs
    gu = LANES // H
    lam = lax.complex(a_re, a_im)
    step = jnp.exp(log_step)[:, None]
    lam_bar = jnp.exp(lam * step)
    b_bar = ((lam_bar - 1.0) / lam)[..., None] * lax.complex(b_re, b_im)
    lr = jnp.real(lam_bar).reshape(ns, 1, gs * P)
    li = jnp.imag(lam_bar).reshape(ns, 1, gs * P)
    k = np.arange(ns)[:, None, None]
    oh = jnp.asarray((np.arange(gu)[None, :, None] == gs * (k % (gu // gs)) + np.arange(gs)[None, None, :]),
                     F32)
    bre = jnp.einsum('kgl,klph->kghlp', oh, jnp.real(b_bar).reshape(ns, gs, P, H)).reshape(ns, gu * H, gs * P)
    bim = jnp.einsum('kgl,klph->kghlp', oh, jnp.imag(b_bar).reshape(ns, gs, P, H)).reshape(ns, gu * H, gs * P)
    cre = jnp.einsum('kgl,klhp->klpgh', oh, c_re.reshape(ns, gs, H, P)).reshape(ns, gs * P, gu * H)
    cim = jnp.einsum('kgl,klhp->klpgh', oh, c_im.reshape(ns, gs, H, P)).reshape(ns, gs * P, gu * H)
    return lr, li, jnp.concatenate([bre, bim], axis=2), jnp.concatenate([cre, -cim], axis=1)


def _carry_tile(fr, fi, pr, pi, reverse):
    row = lax.broadcasted_iota(jnp.int32, fr.shape, 0)
    cr = jnp.zeros_like(fr)
    ci = jnp.zeros_like(fi)
    sr = jnp.zeros_like(fr[0:1])
    si = jnp.zeros_like(sr)
    order = range(SCAN_CHUNKS - 1, 0, -1) if reverse else range(0, SCAN_CHUNKS - 1)
    for c in order:
        fcr = jnp.sum(jnp.where(row == c, fr, 0.0), axis=0, keepdims=True)
        fci = jnp.sum(jnp.where(row == c, fi, 0.0), axis=0, keepdims=True)
        mr, mi = _cmul(pr, pi, sr, si)
        sr, si = mr + fcr, mi + fci
        nxt = c - 1 if reverse else c + 1
        cr = jnp.where(row == nxt, sr, cr)
        ci = jnp.where(row == nxt, si, ci)
    return cr, ci


def _s5_fwd(u, lr, li, bmat, cmat, d, name):
    L, Du = u.shape
    ns, _, W2 = bmat.shape
    W = W2 // 2
    T = L // SCAN_CHUNKS
    rb = _pick(L, prefs=(512, 256, 128))
    per = (ns * LANES) // Du

    def body(u_ref, lr_ref, li_ref, b_ref, c_ref, d_ref, y_ref, sr_ref, si_ref):
        k = pl.program_id(0)
        for r in range(L // rb):
            rows = pl.ds(r * rb, rb)
            bu = jnp.dot(u_ref[rows, :].astype(BF16), b_ref[...], preferred_element_type=F32)
            sr_ref[rows, :] = bu[:, :W]
            si_ref[rows, :] = bu[:, W:]
        lam_r = jnp.broadcast_to(lr_ref[...], (SUBLANES, W))
        lam_i = jnp.broadcast_to(li_ref[...], (SUBLANES, W))

        def local(t, carry):
            sr, si = carry
            rows = pl.ds(pl.multiple_of(t * SUBLANES, SUBLANES), SUBLANES)
            mr, mi = _cmul(lam_r, lam_i, sr, si)
            sr = mr + sr_ref[rows, :]
            si = mi + si_ref[rows, :]
            sr_ref[rows, :] = sr
            si_ref[rows, :] = si
            return sr, si

        z = jnp.zeros((SUBLANES, W), F32)
        fr, fi = lax.fori_loop(0, T, local, (z, z))
        pr, pi = _cpow(lam_r, lam_i, T)
        cr, ci = _carry_tile(fr, fi, pr[0:1], pi[0:1], reverse=False)

        def fix(t, carry):
            wr, wi = carry
            rows = pl.ds(pl.multiple_of(t * SUBLANES, SUBLANES), SUBLANES)
            ar, ai = _cmul(wr, wi, cr, ci)
            sr_ref[rows, :] += ar
            si_ref[rows, :] += ai
            return _cmul(wr, wi, lam_r, lam_i)

        lax.fori_loop(0, T, fix, (lam_r, lam_i))
        first = (k % per) == 0
        for r in range(L // rb):
            rows = pl.ds(r * rb, rb)
            s = jnp.concatenate([sr_ref[rows, :], si_ref[rows, :]], axis=1).astype(BF16)
            y = jnp.dot(s, c_ref[...], preferred_element_type=F32)

            @pl.when(first)
            def _():
                y_ref[rows, :] = y + d_ref[...] * u_ref[rows, :]

            @pl.when(jnp.logical_not(first))
            def _():
                y_ref[rows, :] += y

    ublk = pl.BlockSpec((L, LANES), lambda k: (0, k // per))
    sblk = pl.BlockSpec((L, W), lambda k: (0, k))
    lam = pl.BlockSpec((None, 1, W), lambda k: (k, 0, 0))
    return pl.pallas_call(
        body, name=name,
        out_shape=(jax.ShapeDtypeStruct((L, Du), F32), jax.ShapeDtypeStruct((L, ns * W), F32),
                   jax.ShapeDtypeStruct((L, ns * W), F32)),
        grid=(ns,),
        in_specs=[ublk, lam, lam, pl.BlockSpec((None, LANES, 2 * W), lambda k: (k, 0, 0)),
                  pl.BlockSpec((None, 2 * W, LANES), lambda k: (k, 0, 0)),
                  pl.BlockSpec((1, LANES), lambda k: (0, k // per))],
        out_specs=(ublk, sblk, sblk), compiler_params=_cparams(("arbitrary",), VMEM_LIMIT_S5),
    )(u, lr, li, bmat.astype(BF16), cmat.astype(BF16), d.reshape(1, Du))


def _s5_bwd(dy, u, s_re, s_im, lr, li, bmat, cmat, d, name):
    L, Du = u.shape
    ns, _, W2 = bmat.shape
    W = W2 // 2
    T = L // SCAN_CHUNKS
    rb = _pick(L, prefs=(512, 256, 128))
    per = (ns * LANES) // Du
    NT = (((1,), (1,)), ((), ()))
    TN = (((0,), (0,)), ((), ()))

    def body(dy_ref, u_ref, sr_ref, si_ref, lr_ref, li_ref, b_ref, c_ref, d_ref,
             du_ref, db_ref, dc_ref, dl_ref, dd_ref, gr_ref, gi_ref):
        k = pl.program_id(0)
        for r in range(L // rb):
            rows = pl.ds(r * rb, rb)
            g = lax.dot_general(dy_ref[rows, :].astype(BF16), c_ref[...], NT, preferred_element_type=F32)
            gr_ref[rows, :] = g[:, :W]
            gi_ref[rows, :] = g[:, W:]
        lam_r = jnp.broadcast_to(lr_ref[...], (SUBLANES, W))
        lam_i = -jnp.broadcast_to(li_ref[...], (SUBLANES, W))

        def local(i, carry):
            gr, gi = carry
            rows = pl.ds(pl.multiple_of((T - 1 - i) * SUBLANES, SUBLANES), SUBLANES)
            mr, mi = _cmul(lam_r, lam_i, gr, gi)
            gr = mr + gr_ref[rows, :]
            gi = mi + gi_ref[rows, :]
            gr_ref[rows, :] = gr
            gi_ref[rows, :] = gi
            return gr, gi

        z = jnp.zeros((SUBLANES, W), F32)
        fr, fi = lax.fori_loop(0, T, local, (z, z))
        pr, pi = _cpow(lam_r, lam_i, T)
        cr, ci = _carry_tile(fr, fi, pr[0:1], pi[0:1], reverse=True)

        def true_g(rows, wr, wi):
            ar, ai = _cmul(wr, wi, cr, ci)
            gr = gr_ref[rows, :] + ar
            gi = gi_ref[rows, :] + ai
            gr_ref[rows, :] = gr
            gi_ref[rows, :] = gi
            return gr, gi

        def fix(i, carry):
            wr, wi, ar_, ai_ = carry
            t = T - 1 - i
            rows = pl.ds(pl.multiple_of(t * SUBLANES, SUBLANES), SUBLANES)
            prev = pl.ds(pl.multiple_of((t - 1) * SUBLANES, SUBLANES), SUBLANES)
            gr, gi = true_g(rows, wr, wi)
            qr, qi = sr_ref[prev, :], si_ref[prev, :]
            ar_ = ar_ + gr * qr + gi * qi
            ai_ = ai_ + gi * qr - gr * qi
            wr, wi = _cmul(wr, wi, lam_r, lam_i)
            return wr, wi, ar_, ai_

        wr, wi, acc_r, acc_i = lax.fori_loop(0, T - 1, fix, (lam_r, lam_i, z, z))
        gr, gi = true_g(pl.ds(0, SUBLANES), wr, wi)
        last = pl.ds((T - 1) * SUBLANES, SUBLANES)
        row = lax.broadcasted_iota(jnp.int32, (SUBLANES, W), 0)
        qr = jnp.where(row >= 1, pltpu.roll(sr_ref[last, :], 1, axis=0), 0.0)
        qi = jnp.where(row >= 1, pltpu.roll(si_ref[last, :], 1, axis=0), 0.0)
        acc_r = acc_r + gr * qr + gi * qi
        acc_i = acc_i + gi * qr - gr * qi
        dl_ref[0:1, :] = jnp.sum(acc_r, axis=0, keepdims=True)
        dl_ref[1:2, :] = jnp.sum(acc_i, axis=0, keepdims=True)

        first = (k % per) == 0
        db = jnp.zeros((LANES, 2 * W), F32)
        dc = jnp.zeros((LANES, 2 * W), F32)
        dd = jnp.zeros((1, LANES), F32)
        for r in range(L // rb):
            rows = pl.ds(r * rb, rb)
            gb = jnp.concatenate([gr_ref[rows, :], gi_ref[rows, :]], axis=1).astype(BF16)
            sb = jnp.concatenate([sr_ref[rows, :], si_ref[rows, :]], axis=1).astype(BF16)
            dyv = dy_ref[rows, :]
            uv = u_ref[rows, :]
            du = lax.dot_general(gb, b_ref[...], NT, preferred_element_type=F32)
            db = db + lax.dot_general(uv.astype(BF16), gb, TN, preferred_element_type=F32)
            dc = dc + lax.dot_general(dyv.astype(BF16), sb, TN, preferred_element_type=F32)
            dd = dd + jnp.sum(dyv * uv, axis=0, keepdims=True)

            @pl.when(first)
            def _():
                du_ref[rows, :] = du + d_ref[...] * dyv

            @pl.when(jnp.logical_not(first))
            def _():
                du_ref[rows, :] += du

        db_ref[...] = db
        dc_ref[...] = dc

        @pl.when(first)
        def _():
            dd_ref[...] = dd

    ublk = pl.BlockSpec((L, LANES), lambda k: (0, k // per))
    sblk = pl.BlockSpec((L, W), lambda k: (0, k))
    lam = pl.BlockSpec((None, 1, W), lambda k: (k, 0, 0))
    vec = pl.BlockSpec((1, LANES), lambda k: (0, k // per))
    mat = pl.BlockSpec((None, LANES, 2 * W), lambda k: (k, 0, 0))
    return pl.pallas_call(
        body, name=name,
        out_shape=(jax.ShapeDtypeStruct((L, Du), F32), jax.ShapeDtypeStruct((ns, LANES, 2 * W), F32),
                   jax.ShapeDtypeStruct((ns, LANES, 2 * W), F32), jax.ShapeDtypeStruct((ns, 2, W), F32),
                   jax.ShapeDtypeStruct((1, Du), F32)),
        grid=(ns,),
        in_specs=[ublk, ublk, sblk, sblk, lam, lam, mat,
                  pl.BlockSpec((None, 2 * W, LANES), lambda k: (k, 0, 0)), vec],
        out_specs=(ublk, mat, mat, pl.BlockSpec((None, 2, W), lambda k: (k, 0, 0)), vec),
        scratch_shapes=[pltpu.VMEM((L, W), F32), pltpu.VMEM((L, W), F32)],
        compiler_params=_cparams(("arbitrary",), VMEM_LIMIT_S5),
    )(dy, u, s_re, s_im, lr, li, bmat.astype(BF16), cmat.astype(BF16), d.reshape(1, Du))


def _glu_fwd(yraw, wmat, bias, name):
    L, C = yraw.shape
    tr = _pick(L, prefs=(512, 256, 128))

    def body(y_ref, w_ref, b_ref, o_ref):
        yg = _gelu(y_ref[...])
        zz = jnp.dot(yg.astype(BF16), w_ref[...], preferred_element_type=F32) + b_ref[...]
        o_ref[...] = (yg * _sigmoid(zz)).astype(BF16)

    return pl.pallas_call(
        body, name=name, out_shape=jax.ShapeDtypeStruct((L, C), BF16), grid=(L // tr,),
        in_specs=[pl.BlockSpec((tr, C), lambda i: (i, 0)), pl.BlockSpec((C, C), lambda i: (0, 0)),
                  pl.BlockSpec((1, C), lambda i: (0, 0))],
        out_specs=pl.BlockSpec((tr, C), lambda i: (i, 0)), compiler_params=_cparams(("parallel",)),
    )(yraw, wmat, bias.reshape(1, C))


def _glu_bwd(yraw, dyb, wmat, bias, name):
    L, C = yraw.shape
    tr = _pick(L, prefs=(512, 256, 128))
    nsteps = L // tr

    def body(y_ref, d_ref, w_ref, b_ref, dy_ref, dw_ref, db_ref, acc_b):
        i = pl.program_id(0)

        @pl.when(i == 0)
        def _():
            dw_ref[...] = jnp.zeros_like(dw_ref)
            acc_b[...] = jnp.zeros_like(acc_b)

        yr = y_ref[...]
        yg = _gelu(yr)
        ygb = yg.astype(BF16)
        sg = _sigmoid(jnp.dot(ygb, w_ref[...], preferred_element_type=F32) + b_ref[...])
        dyb_ = d_ref[...]
        dz = dyb_ * yg * sg * (1.0 - sg)
        dzb = dz.astype(BF16)
        dyg = dyb_ * sg + lax.dot_general(dzb, w_ref[...], (((1,), (1,)), ((), ())), preferred_element_type=F32)
        dw_ref[...] += lax.dot_general(ygb, dzb, (((0,), (0,)), ((), ())), preferred_element_type=F32)
        acc_b[...] += jnp.sum(dz.reshape(tr // SUBLANES, SUBLANES, C), axis=0)
        dy_ref[...] = dyg * _gelu_grad(yr)

        @pl.when(i == nsteps - 1)
        def _():
            db_ref[...] = jnp.sum(acc_b[...], axis=0, keepdims=True)

    row = pl.BlockSpec((tr, C), lambda i: (i, 0))
    return pl.pallas_call(
        body, name=name,
        out_shape=(jax.ShapeDtypeStruct((L, C), F32), jax.ShapeDtypeStruct((C, C), F32),
                   jax.ShapeDtypeStruct((1, C), F32)),
        grid=(nsteps,),
        in_specs=[row, row, pl.BlockSpec((C, C), lambda i: (0, 0)), pl.BlockSpec((1, C), lambda i: (0, 0))],
        out_specs=(row, pl.BlockSpec((C, C), lambda i: (0, 0)), pl.BlockSpec((1, C), lambda i: (0, 0))),
        scratch_shapes=[pltpu.VMEM((SUBLANES, C), F32)], compiler_params=_cparams(("arbitrary",)),
    )(yraw, dyb, wmat, bias.reshape(1, C))


def _pool_counts(L, g):
    t = lax.broadcasted_iota(jnp.int32, (L, LANES), 0).astype(F32) + 1.0
    w = jnp.where(g == 0, 2.0, jnp.where(g == 1, 4.0, jnp.where(g == 2, 8.0, 16.0)))
    return 1.0 / jnp.minimum(t, w)


def _select_window(g, a2, a4, a8, a16):
    return jnp.where(g == 0, a2, jnp.where(g == 1, a4, jnp.where(g == 2, a8, a16)))


def _pooled(z, g):
    a2 = z + _down(z, 1)
    a4 = a2 + _down(a2, 2)
    a8 = a4 + _down(a4, 4)
    a16 = a8 + _down(a8, 8)
    return _select_window(g, a2, a4, a8, a16) * _pool_counts(z.shape[0], g) - z


def _pool_fwd(proj3, pool_w, scale, name):
    _, L, C = proj3.shape
    ng = len(POOL_WINDOWS)
    pg = C // ng
    assert pg == LANES

    def body(z_ref, w_ref, s_ref, o_ref):
        g = pl.program_id(0)
        p = _pooled(z_ref[...], g)
        y = jnp.dot(p.astype(BF16), w_ref[...].astype(BF16), preferred_element_type=F32)
        o_ref[...] = (y * s_ref[...]).astype(BF16)

    return pl.pallas_call(
        body, name=name, out_shape=jax.ShapeDtypeStruct((L, C), BF16), grid=(ng,),
        in_specs=[pl.BlockSpec((None, L, pg), lambda g: (0, 0, g)), pl.BlockSpec((None, pg, pg), lambda g: (g, 0, 0)),
                  pl.BlockSpec((1, pg), lambda g: (0, g))],
        out_specs=pl.BlockSpec((L, pg), lambda g: (0, g)), compiler_params=_cparams(("parallel",)),
    )(proj3, pool_w, scale.reshape(1, C))


def _pool_bwd(proj3, dmix, pool_w, scale, name):
    _, L, C = proj3.shape
    ng = len(POOL_WINDOWS)
    pg = C // ng

    def body(z_ref, d_ref, w_ref, s_ref, dz_ref, dw_ref, ds_ref):
        g = pl.program_id(0)
        p = _pooled(z_ref[...], g)
        pb = p.astype(BF16)
        wb = w_ref[...].astype(BF16)
        pre = jnp.dot(pb, wb, preferred_element_type=F32)
        dyc = d_ref[...]
        ds_ref[...] = jnp.sum(dyc * pre, axis=0, keepdims=True)
        dpre = (dyc * s_ref[...]).astype(BF16)
        dw_ref[...] = lax.dot_general(pb, dpre, (((0,), (0,)), ((), ())), preferred_element_type=F32)
        dp = lax.dot_general(dpre, wb, (((1,), (1,)), ((), ())), preferred_element_type=F32)
        v = dp * _pool_counts(L, g)
        a2 = v + _up(v, 1)
        a4 = a2 + _up(a2, 2)
        a8 = a4 + _up(a4, 4)
        a16 = a8 + _up(a8, 8)
        dz_ref[...] = (_select_window(g, a2, a4, a8, a16) - dp).astype(BF16)

    return pl.pallas_call(
        body, name=name,
        out_shape=(jax.ShapeDtypeStruct((L, C), BF16), jax.ShapeDtypeStruct((ng, pg, pg), F32),
                   jax.ShapeDtypeStruct((1, C), F32)),
        grid=(ng,),
        in_specs=[pl.BlockSpec((None, L, pg), lambda g: (0, 0, g)), pl.BlockSpec((L, pg), lambda g: (0, g)),
                  pl.BlockSpec((None, pg, pg), lambda g: (g, 0, 0)), pl.BlockSpec((1, pg), lambda g: (0, g))],
        out_specs=(pl.BlockSpec((L, pg), lambda g: (0, g)), pl.BlockSpec((None, pg, pg), lambda g: (g, 0, 0)),
                   pl.BlockSpec((1, pg), lambda g: (0, g))),
        compiler_params=_cparams(("parallel",)),
    )(proj3, dmix, pool_w, scale.reshape(1, C))


def _tril_w(w_ref, h):
    r = lax.broadcasted_iota(jnp.int32, (CHUNK, CHUNK), 0)
    c = lax.broadcasted_iota(jnp.int32, (CHUNK, CHUNK), 1)
    return jnp.where(r >= c, w_ref[h], 0.0)


def _sgu_fwd(proj3, norm_g, w, b, name):
    _, L, C = proj3.shape
    nh = w.shape[0]
    dh = C // nh
    assert dh == LANES and w.shape[1] == CHUNK
    tr = _pick(L, prefs=(512, 256, 128))
    bfull = jnp.broadcast_to(b[:, :, None], (nh, CHUNK, dh))

    def body(su_ref, sv_ref, g_ref, w_ref, b_ref, o_ref):
        sv = _gelu(sv_ref[...])
        r = lax.rsqrt(jnp.mean(sv * sv, axis=-1, keepdims=True) + EPS)
        v = (sv * r * g_ref[...]).astype(BF16)
        for h in range(nh):
            wm = _tril_w(w_ref, h).astype(BF16)
            cols = slice(h * dh, (h + 1) * dh)
            for n in range(tr // CHUNK):
                rows = slice(n * CHUNK, (n + 1) * CHUNK)
                mixed = jnp.dot(wm, v[rows, cols], preferred_element_type=F32) + b_ref[h]
                o_ref[rows, cols] = (_gelu(su_ref[rows, cols]) * mixed).astype(BF16)

    full = lambda shp: pl.BlockSpec(shp, lambda i: (0,) * len(shp))
    return pl.pallas_call(
        body, name=name, out_shape=jax.ShapeDtypeStruct((L, C), BF16), grid=(L // tr,),
        in_specs=[pl.BlockSpec((None, tr, C), lambda i: (1, i, 0)), pl.BlockSpec((None, tr, C), lambda i: (2, i, 0)),
                  full((1, C)), full((nh, CHUNK, CHUNK)), full((nh, CHUNK, dh))],
        out_specs=pl.BlockSpec((tr, C), lambda i: (i, 0)), compiler_params=_cparams(("parallel",)),
    )(proj3, proj3, norm_g.reshape(1, C), w, bfull)


def _sgu_bwd(proj3, dmix, norm_g, w, b, name):
    _, L, C = proj3.shape
    nh = w.shape[0]
    dh = C // nh
    tr = _pick(L, prefs=(512, 256, 128))
    nsteps = L // tr
    bfull = jnp.broadcast_to(b[:, :, None], (nh, CHUNK, dh))

    def body(su_ref, sv_ref, d_ref, g_ref, w_ref, b_ref, o_ref, dw_ref, db_ref, dg_ref, dv_ref, acc_g):
        i = pl.program_id(0)

        @pl.when(i == 0)
        def _():
            dw_ref[...] = jnp.zeros_like(dw_ref)
            db_ref[...] = jnp.zeros_like(db_ref)
            acc_g[...] = jnp.zeros_like(acc_g)

        svp = sv_ref[...]
        sv = _gelu(svp)
        r = lax.rsqrt(jnp.mean(sv * sv, axis=-1, keepdims=True) + EPS)
        vh = sv * r
        gv = g_ref[...]
        v = (vh * gv).astype(BF16)
        tri_r = lax.broadcasted_iota(jnp.int32, (CHUNK, CHUNK), 0)
        tri_c = lax.broadcasted_iota(jnp.int32, (CHUNK, CHUNK), 1)
        for h in range(nh):
            wm = _tril_w(w_ref, h).astype(BF16)
            cols = slice(h * dh, (h + 1) * dh)
            dwh = jnp.zeros((CHUNK, CHUNK), F32)
            dbh = jnp.zeros((CHUNK, dh), F32)
            for n in range(tr // CHUNK):
                rows = slice(n * CHUNK, (n + 1) * CHUNK)
                vb = v[rows, cols]
                mixed = jnp.dot(wm, vb, preferred_element_type=F32) + b_ref[h]
                sup = su_ref[rows, cols]
                dyd = d_ref[rows, cols]
                dmx = dyd * _gelu(sup)
                o_ref[0, rows, cols] = (dyd * mixed * _gelu_grad(sup)).astype(BF16)
                dmb = dmx.astype(BF16)
                dwh = dwh + lax.dot_general(dmb, vb, (((1,), (1,)), ((), ())), preferred_element_type=F32)
                dbh = dbh + dmx
                dv_ref[rows, cols] = lax.dot_general(wm, dmb, (((0,), (0,)), ((), ())), preferred_element_type=F32)
            dw_ref[h] += jnp.where(tri_r >= tri_c, dwh, 0.0)
            db_ref[h] += dbh
        dv = dv_ref[...]
        acc_g[...] += jnp.sum((dv * vh).reshape(tr // SUBLANES, SUBLANES, C), axis=0)
        dvg = dv * gv
        dsv = r * (dvg - vh * jnp.mean(dvg * vh, axis=-1, keepdims=True))
        o_ref[1] = (dsv * _gelu_grad(svp)).astype(BF16)

        @pl.when(i == nsteps - 1)
        def _():
            dg_ref[...] = jnp.sum(acc_g[...], axis=0, keepdims=True)

    full = lambda shp: pl.BlockSpec(shp, lambda i: (0,) * len(shp))
    return pl.pallas_call(
        body, name=name,
        out_shape=(jax.ShapeDtypeStruct((2, L, C), BF16), jax.ShapeDtypeStruct((nh, CHUNK, CHUNK), F32),
                   jax.ShapeDtypeStruct((nh, CHUNK, dh), F32), jax.ShapeDtypeStruct((1, C), F32)),
        grid=(nsteps,),
        in_specs=[pl.BlockSpec((None, tr, C), lambda i: (1, i, 0)), pl.BlockSpec((None, tr, C), lambda i: (2, i, 0)),
                  pl.BlockSpec((tr, C), lambda i: (i, 1)), full((1, C)), full((nh, CHUNK, CHUNK)),
                  full((nh, CHUNK, dh))],
        out_specs=(pl.BlockSpec((2, tr, C), lambda i: (0, i, 0)), full((nh, CHUNK, CHUNK)), full((nh, CHUNK, dh)),
                   full((1, C))),
        scratch_shapes=[pltpu.VMEM((tr, C), F32), pltpu.VMEM((SUBLANES, C), F32)],
        compiler_params=_cparams(("arbitrary",)),
    )(proj3, proj3, dmix, norm_g.reshape(1, C), w, bfull)


def _ffn_act_fwd(up3, conv_w, conv_b, name):
    _, L, Fh = up3.shape
    cb = LANES
    w2 = conv_w.reshape(3, 2, Fh).transpose(1, 0, 2)
    b2 = conv_b.reshape(2, 1, Fh)

    def body(u_ref, w_ref, b_ref, o_ref, ot_ref):
        g = _conv3(u_ref[0].astype(F32), w_ref[0]) + b_ref[0]
        v = _conv3(u_ref[1].astype(F32), w_ref[1]) + b_ref[1]
        a = g * _sigmoid(g) * v
        o_ref[...] = a.astype(BF16)
        ot_ref[...] = a.T.astype(BF16)

    return pl.pallas_call(
        body, name=name, out_shape=(jax.ShapeDtypeStruct((L, Fh), BF16), jax.ShapeDtypeStruct((Fh, L), BF16)),
        grid=(Fh // cb,),
        in_specs=[pl.BlockSpec((2, L, cb), lambda j: (0, 0, j)), pl.BlockSpec((2, 3, cb), lambda j: (0, 0, j)),
                  pl.BlockSpec((2, 1, cb), lambda j: (0, 0, j))],
        out_specs=(pl.BlockSpec((L, cb), lambda j: (0, j)), pl.BlockSpec((cb, L), lambda j: (j, 0))),
        compiler_params=_cparams(("parallel",)),
    )(up3, w2, b2)


def _ffn_act_bwd(up3, da, conv_w, conv_b, name):
    _, L, Fh = up3.shape
    cb = LANES
    w2 = conv_w.reshape(3, 2, Fh).transpose(1, 0, 2)
    b2 = conv_b.reshape(2, 1, Fh)

    def body(u_ref, d_ref, w_ref, b_ref, o_ref, dw_ref, db_ref):
        ug, uv = u_ref[0].astype(F32), u_ref[1].astype(F32)
        wg, wv = w_ref[0], w_ref[1]
        g = _conv3(ug, wg) + b_ref[0]
        v = _conv3(uv, wv) + b_ref[1]
        sg = _sigmoid(g)
        dav = d_ref[...].astype(F32)
        dg = dav * v * (sg * (1.0 + g * (1.0 - sg)))
        dv = dav * (g * sg)
        o_ref[0] = _conv3_t(dg, wg).astype(BF16)
        o_ref[1] = _conv3_t(dv, wv).astype(BF16)
        for tap, (dwg, dwv) in enumerate(zip(_conv3_dw(dg, ug), _conv3_dw(dv, uv))):
            dw_ref[0, tap:tap + 1, :] = dwg
            dw_ref[1, tap:tap + 1, :] = dwv
        db_ref[0] = jnp.sum(dg, axis=0, keepdims=True)
        db_ref[1] = jnp.sum(dv, axis=0, keepdims=True)

    dup, dw2, db2 = pl.pallas_call(
        body, name=name,
        out_shape=(jax.ShapeDtypeStruct((2, L, Fh), BF16), jax.ShapeDtypeStruct((2, 3, Fh), F32),
                   jax.ShapeDtypeStruct((2, 1, Fh), F32)),
        grid=(Fh // cb,),
        in_specs=[pl.BlockSpec((2, L, cb), lambda j: (0, 0, j)), pl.BlockSpec((L, cb), lambda j: (0, j)),
                  pl.BlockSpec((2, 3, cb), lambda j: (0, 0, j)), pl.BlockSpec((2, 1, cb), lambda j: (0, 0, j))],
        out_specs=(pl.BlockSpec((2, L, cb), lambda j: (0, 0, j)), pl.BlockSpec((2, 3, cb), lambda j: (0, 0, j)),
                   pl.BlockSpec((2, 1, cb), lambda j: (0, 0, j))),
        compiler_params=_cparams(("parallel",)),
    )(up3, da, w2, b2)
    return dup, dw2.transpose(1, 0, 2).reshape(3, 2 * Fh), db2.reshape(2 * Fh)


def _local_step(x, tgt, w, layer_weights, on_layer_grads):
    L, D = x.shape
    depth = w['norm_mix_g'].shape[0]
    saved = []
    for i in range(depth):
        j = i // 2
        wb = dict(layer_weights(2 * i, x))
        s = {'x': x, 'wb': wb}
        if i % 2 == 0:
            proj4, s['hT'] = _norm_mm(x, w['norm_mix_g'][i], wb['even_w_in'], F32, "even_in_fwd", ok=('seg', 4))
            s['proj'] = proj4
            ya = _sconv_fwd(proj4, w['even_conv_w'][j], "sconv_fwd")
            prm = (w['ssm_log_step'][j], w['ssm_a_re'][j], w['ssm_a_im'][j], w['ssm_b_re'][j], w['ssm_b_im'][j],
                   w['ssm_c_re'][j], w['ssm_c_im'][j])
            (lr, li, bmat, cmat), prep_vjp = jax.vjp(_s5_prep, *prm)
            u = _to_scan_order(proj4[3])
            yraw, s_re, s_im = _s5_fwd(u, lr, li, bmat, cmat, w['ssm_d'][j], "s5_fwd")
            yb = _glu_fwd(yraw, wb['ssm_glu_w'], w['ssm_glu_b'][j], "glu_fwd")
            s.update(u=u, yraw=yraw, s_re=s_re, s_im=s_im, s5=(lr, li, bmat, cmat), prep_vjp=prep_vjp)
            mixin = jnp.concatenate([ya, _from_scan_order(yb)], axis=1)
            x = _mm(mixin, wb['even_w_out'], 'nn', F32, "even_out_fwd", res=x)
        else:
            proj3, s['hT'] = _norm_mm(x, w['norm_mix_g'][i], wb['odd_w_in'], F32, "odd_in_fwd", ok=('seg', 3))
            s['proj'] = proj3
            yc = _pool_fwd(proj3, w['pool_w'][j], w['pool_scale'][j], "pool_fwd")
            yd = _sgu_fwd(proj3, w['sgu_norm_g'][j], w['sgu_w'][j], w['sgu_b'][j], "sgu_fwd")
            mixin = jnp.concatenate([yc, yd], axis=1)
            x = _mm(mixin, wb['odd_w_out'], 'nn', F32, "odd_out_fwd", res=x)
        s['mixin'] = mixin
        s['x1'] = x
        wb.update(layer_weights(2 * i + 1, x))
        up3, h2t = _norm_mm(x, w['norm_ffn_g'][i], wb['ffn_w_up'], BF16, "ffn_up_fwd", ok=('seg', 2))
        a, at = _ffn_act_fwd(up3, w['ffn_conv_w'][i], w['ffn_conv_b'][i], "ffn_act_fwd")
        x = _mm(a, wb['ffn_w_down'], 'nn', F32, "ffn_down_fwd", res=x)
        s.update(h2T=h2t, up3=up3, aT=at)
        saved.append(s)

    loss8, dx, dg_final = _loss_head(x, w['norm_final_g'], tgt)
    gs = {n: [None] * w[n].shape[0] for n in SMALL if n != 'norm_final_g'}
    gs['norm_final_g'] = dg_final.reshape(D)

    dep = None
    for i in reversed(range(depth)):
        j = i // 2
        s = saved[i]
        wb = s['wb']
        gb = {}
        da = _mm(dx, wb['ffn_w_down'], 'nt', BF16, "ffn_down_dgrad", dep=dep)
        gb['ffn_w_down'] = _mm(s['aT'], dx, 'nn', BF16, "ffn_down_wgrad")
        dup3, dcw, dcb = _ffn_act_bwd(s['up3'], da, w['ffn_conv_w'][i], w['ffn_conv_b'][i], "ffn_act_bwd")
        gs['ffn_conv_w'][i], gs['ffn_conv_b'][i] = dcw, dcb
        gb['ffn_w_up'] = _mm(s['h2T'], dup3, 'nn', BF16, "ffn_up_wgrad", bk=('seg', 2))
        dep = on_layer_grads(2 * i + 1, gb)
        dx, dg = _mm_norm_bwd(dup3, wb['ffn_w_up'], s['x1'], w['norm_ffn_g'][i], dx, "ffn_up_dgrad",
                              ak=('seg', 2), dep=dep)
        gs['norm_ffn_g'][i] = dg.reshape(D)
        gb = {}
        if i % 2 == 0:
            dmix = _mm(dx, wb['even_w_out'], 'nt', F32, "even_out_dgrad")
            gb['even_w_out'] = _mm(s['mixin'].T, dx, 'nn', BF16, "even_out_wgrad")
            dpc, dcw = _sconv_bwd(s['proj'], dmix, w['even_conv_w'][j], "sconv_bwd")
            gs['even_conv_w'][j] = dcw
            dyb = _to_scan_order(dmix[:, D // 2:])
            dyraw, dglu_w, dglu_b = _glu_bwd(s['yraw'], dyb, wb['ssm_glu_w'], w['ssm_glu_b'][j], "glu_bwd")
            gb['ssm_glu_w'] = dglu_w.astype(BF16)
            gs['ssm_glu_b'][j] = dglu_b.reshape(-1)
            lr, li, bmat, cmat = s['s5']
            du, dbm, dcm, dlam, dd = _s5_bwd(dyraw, s['u'], s['s_re'], s['s_im'], lr, li, bmat, cmat,
                                            w['ssm_d'][j], "s5_bwd")
            gs['ssm_d'][j] = dd.reshape(-1)
            dcm = jnp.swapaxes(dcm, 1, 2)
            dprm = s['prep_vjp']((dlam[:, 0:1, :], dlam[:, 1:2, :], dbm, dcm))
            for n, gval in zip(('ssm_log_step', 'ssm_a_re', 'ssm_a_im', 'ssm_b_re', 'ssm_b_im', 'ssm_c_re',
                                'ssm_c_im'), dprm):
                gs[n][j] = gval
            dproj = jnp.concatenate([dpc, _from_scan_order(du).astype(BF16)[None]], axis=0)
            gb['even_w_in'] = _mm(s['hT'], dproj, 'nn', BF16, "even_in_wgrad", bk=('seg', 4))
            w_in, in_kind, in_name = wb['even_w_in'], ('seg', 4), "even_in_dgrad"
        else:
            dmix = _mm(dx, wb['odd_w_out'], 'nt', F32, "odd_out_dgrad")
            gb['odd_w_out'] = _mm(s['mixin'].T, dx, 'nn', BF16, "odd_out_wgrad")
            dz, dpw, dps = _pool_bwd(s['proj'], dmix, w['pool_w'][j], w['pool_scale'][j], "pool_bwd")
            gs['pool_w'][j], gs['pool_scale'][j] = dpw, dps.reshape(-1)
            dsuv, dsw, dsb, dsg = _sgu_bwd(s['proj'], dmix, w['sgu_norm_g'][j], w['sgu_w'][j], w['sgu_b'][j],
                                           "sgu_bwd")
            gs['sgu_w'][j], gs['sgu_b'][j], gs['sgu_norm_g'][j] = dsw, jnp.sum(dsb, axis=-1), dsg.reshape(-1)
            dproj = jnp.concatenate([dz[None], dsuv], axis=0)
            gb['odd_w_in'] = _mm(s['hT'], dproj, 'nn', BF16, "odd_in_wgrad", bk=('seg', 3))
            w_in, in_kind, in_name = wb['odd_w_in'], ('seg', 3), "odd_in_dgrad"
        dep = on_layer_grads(2 * i, gb)
        dx, dg = _mm_norm_bwd(dproj, w_in, s['x'], w['norm_mix_g'][i], dx, in_name, ak=in_kind, dep=dep)
        gs['norm_mix_g'][i] = dg.reshape(D)

    gsmall = {n: (v if n == 'norm_final_g' else jnp.stack(v)) for n, v in gs.items()}
    return loss8[0, 0], dx, gsmall


_HBM = pl.BlockSpec(memory_space=pltpu.HBM)
_CHIP_FLIPS = ((0, 0), (1, 0), (0, 1), (1, 1))


def _coords():
    return lax.axis_index("x"), lax.axis_index("y"), lax.axis_index("c")


def _flip(v, f):
    return 1 - v if f else v


def _shard_of(ref, axis, s, width):
    start = pl.multiple_of(s * width, LANES if axis == ref.ndim - 1 else 16) if width % 16 == 0 else s * width
    idx = [slice(None)] * ref.ndim
    idx[axis] = pl.ds(start, width)
    return ref.at[tuple(idx)]


_SEM = pl.BlockSpec(memory_space=pltpu.SEMAPHORE)
_ANY = pl.BlockSpec(memory_space=pl.ANY)
_DATAFLOW = pltpu.SideEffectType.DATAFLOW_SIDE_EFFECTING


def _in_hbm(a):
    return pltpu.with_memory_space_constraint(a, pltpu.HBM)


def _model_layer(name, l):
    if name.startswith('ffn'):
        return l
    return 2 * l + 1 if name.startswith('odd') else 2 * l


def _place_quarter(shard, l, axis, chip, dtype, dep=None):
    _, r, c = shard.shape
    tr = _pick(r, prefs=(512, 256, 128, 64, 32, 16))
    nrb = r // tr

    def body(chip_ref, i_ref, *rest):
        rest[-1][...] = i_ref[...].astype(dtype)

    if axis == 1:
        out_shape, o_map = (r, c * N_CHIPS), (lambda i, s: (i, s[0]))
    else:
        out_shape, o_map = (r * N_CHIPS, c), (lambda i, s: (s[0] * nrb + i, 0))
    in_specs = [pl.BlockSpec((None, tr, c), lambda i, s: (l, i, 0))]
    args = [chip, shard]
    if dep is not None:
        in_specs.append(pl.BlockSpec(memory_space=pl.ANY))
        args.append(dep)
    return pl.pallas_call(
        body, name="place_quarter", out_shape=jax.ShapeDtypeStruct(out_shape, dtype),
        grid_spec=pltpu.PrefetchScalarGridSpec(
            num_scalar_prefetch=1, grid=(nrb,), in_specs=in_specs, out_specs=pl.BlockSpec((tr, c), o_map)),
        compiler_params=_cparams(("parallel",)),
    )(*args)


def _gather_copies(land_refs, send_sem, recv_sem, axes, landing_chip_of):
    x, y, c = _coords()
    out = []
    for j, land in enumerate(land_refs):
        width = land.shape[axes[j]] // N_CHIPS
        for f in (1, 2, 3):
            fx, fy = _CHIP_FLIPS[f]
            px, py = _flip(x, fx), _flip(y, fy)
            lx, ly = landing_chip_of(px, py)
            out.append(pltpu.make_async_remote_copy(
                src_ref=_shard_of(land, axes[j], 2 * x + y, width), dst_ref=_shard_of(land, axes[j], 2 * lx + ly, width),
                send_sem=send_sem.at[3 * j + f - 1], recv_sem=recv_sem.at[3 * j + f - 1],
                device_id=(px, py, c), device_id_type=MESH))
    return out


def _gather_start(tag, lands, axes, dep=None):
    n = len(lands)

    def body(*refs):
        land_refs, send_sem, recv_sem = refs[:n], refs[-3], refs[-2]
        x, y, _ = _coords()
        for cp in _gather_copies(land_refs, send_sem, recv_sem, axes, lambda px, py: (x, y)):
            cp.start()
        refs[-1][...] = jnp.zeros_like(refs[-1])

    thru = [pltpu.HBM(a.shape, a.dtype) for a in lands]
    outs = pl.pallas_call(
        body, name=f"gather_start_{tag}",
        out_shape=tuple(thru + [pltpu.SemaphoreType.DMA((3 * n,)), pltpu.SemaphoreType.DMA((3 * n,)),
                                jax.ShapeDtypeStruct((SUBLANES, LANES), F32)]),
        in_specs=[_HBM] * n + ([_ANY] if dep is not None else []),
        out_specs=tuple([_HBM] * n + [_SEM, _SEM, pl.BlockSpec(memory_space=pltpu.VMEM)]),
        input_output_aliases={i: i for i in range(n)},
        compiler_params=pltpu.CompilerParams(has_side_effects=_DATAFLOW),
    )(*[_in_hbm(a) for a in lands], *([dep] if dep is not None else []))
    return list(outs[:n]), outs[n], outs[n + 1], outs[n + 2]


def _gather_wait(tag, lands, send_sem, recv_sem, axes, after):
    n = len(lands)

    def body(*refs):
        for cp in _gather_copies(refs[:n], refs[n], refs[n + 1], axes, lambda px, py: (px, py)):
            cp.wait_send()
            cp.wait_recv()

    outs = pl.pallas_call(
        body, name=f"gather_wait_{tag}", out_shape=tuple(pltpu.HBM(a.shape, a.dtype) for a in lands),
        in_specs=[_HBM] * n + [_SEM, _SEM, _ANY], out_specs=tuple([_HBM] * n),
        input_output_aliases={i: i for i in range(n)},
        compiler_params=pltpu.CompilerParams(has_side_effects=_DATAFLOW),
    )(*lands, send_sem, recv_sem, after)
    return list(outs)


N_SLOTS = N_DEV - 1


def _scatter_sends(grad_refs, land_refs, send_sem, recv_sem, meta):
    x, y, c = _coords()
    out = []
    for j, (axis, owner, q, width) in enumerate(meta):
        other = c if owner == 0 else 1 - c
        for f, (fx, fy) in enumerate(_CHIP_FLIPS):
            px, py = _flip(x, fx), _flip(y, fy)
            slot = f + 4 * other - 1
            out.append((other if f == 0 else None, pltpu.make_async_remote_copy(
                src_ref=_shard_of(grad_refs[j], axis, 2 * px + py, width), dst_ref=land_refs[j].at[q, slot],
                send_sem=send_sem.at[4 * j + f], recv_sem=recv_sem.at[N_SLOTS * j + slot],
                device_id=(px, py, owner), device_id_type=MESH)))
    return out


def _scatter_start(layer, grads, lands, meta):
    n = len(grads)
    uniq = []
    for a in lands:
        if not any(a is u for u in uniq):
            uniq.append(a)
    which = [next(k for k, u in enumerate(uniq) if u is a) for a in lands]
    nu = len(uniq)

    def body(*refs):
        grad_refs, land_u = refs[:n], refs[n:n + nu]
        send_sem, recv_sem = refs[n + nu], refs[n + nu + 1]
        for other, cp in _scatter_sends(grad_refs, [land_u[k] for k in which], send_sem, recv_sem, meta):
            if other is None:
                cp.start()
            else:
                pl.when(other == 1)(cp.start)
        refs[-1][...] = jnp.zeros_like(refs[-1])

    thru = [pltpu.HBM(a.shape, a.dtype) for a in list(grads) + uniq]
    outs = pl.pallas_call(
        body, name=f"scatter_start_{layer}",
        out_shape=tuple([pltpu.SemaphoreType.DMA((4 * n,)), pltpu.SemaphoreType.DMA((N_SLOTS * n,))] + thru
                        + [jax.ShapeDtypeStruct((SUBLANES, LANES), F32)]),
        in_specs=[_HBM] * (n + nu),
        out_specs=tuple([_SEM, _SEM] + [_HBM] * (n + nu) + [pl.BlockSpec(memory_space=pltpu.VMEM)]),
        input_output_aliases={i: 2 + i for i in range(n + nu)},
        compiler_params=pltpu.CompilerParams(has_side_effects=_DATAFLOW),
    )(*[_in_hbm(a) for a in list(grads) + uniq])
    new_lands = [outs[2 + n + k] for k in which]
    return outs[0], outs[1], list(outs[2:2 + n]), new_lands, outs[-1]


def _scatter_wait(started, lands):
    nl = len(lands)
    flat_grads = [g for s in started for g in s[2]]
    ng, ns = len(flat_grads), len(started)

    def body(*refs):
        land_refs = refs[:nl]
        grad_refs = refs[nl:nl + ng]
        sem_refs = refs[nl + ng:nl + ng + 2 * ns]
        _, _, c = _coords()
        off = 0
        for k, (_, _, grads, idx, meta) in enumerate(started):
            send_sem, recv_sem = sem_refs[2 * k], sem_refs[2 * k + 1]
            lr = [land_refs[i] for i in idx]
            for other, cp in _scatter_sends(grad_refs[off:off + len(grads)], lr, send_sem, recv_sem, meta):
                if other is None:
                    cp.wait_send()
                else:
                    pl.when(other == 1)(cp.wait_send)
            for j, (axis, owner, q, width) in enumerate(meta):
                mine = (c if owner == 0 else 1 - c) == 0

                @pl.when(mine)
                def _():
                    for slot in range(N_SLOTS):
                        land = lr[j].at[q, slot]
                        pltpu.make_async_remote_copy(
                            src_ref=land, dst_ref=land, send_sem=send_sem.at[0], recv_sem=recv_sem.at[N_SLOTS * j + slot],
                            device_id=_coords(), device_id_type=MESH).wait_recv()
            off += len(grads)

    args = list(lands) + flat_grads
    thru = [pltpu.HBM(a.shape, a.dtype) for a in args]
    sems = [s for st in started for s in st[:2]]
    outs = pl.pallas_call(
        body, name="scatter_wait", out_shape=tuple(thru), in_specs=[_HBM] * (nl + ng) + [_SEM] * (2 * ns),
        out_specs=tuple([_HBM] * (nl + ng)), input_output_aliases={i: i for i in range(nl + ng)},
        compiler_params=pltpu.CompilerParams(has_side_effects=_DATAFLOW),
    )(*args, *sems)
    return list(outs[:nl]), list(outs[nl:])


def _sum_and_share(recv, layer_grads, axis, chip, name):
    n, ns, r, c = recv.shape
    tr = _pick(r, prefs=(256, 128, 64, 32, 16))
    nr = r // tr
    nsteps = n * nr
    nlay = len(layer_grads)
    own_map = (lambda h, i, s: (i, s[0])) if axis == 1 else (lambda h, i, s: (s[0] * nr + i, 0))

    def body(chip_ref, i_ref, *rest):
        g_refs = rest[:nlay]
        o_ref, buf, loc_sems, send_sems, recv_sems = rest[nlay:]
        h, i = pl.program_id(0), pl.program_id(1)
        step = h * nr + i
        slot = step % 2
        x, y, core = _coords()
        layer = core * n + h
        own = g_refs[0][...]
        for l in range(1, nlay):
            own = jnp.where(layer == l, g_refs[l][...], own)

        def copies(sl):
            dst = o_ref.at[core * n + h, pl.ds(pl.multiple_of(i * tr, tr), tr), :]
            loc = pltpu.make_async_copy(buf.at[sl], dst, loc_sems.at[sl])
            rem = pltpu.make_async_remote_copy(
                src_ref=buf.at[sl], dst_ref=dst, send_sem=send_sems.at[sl], recv_sem=recv_sems.at[step],
                device_id=(x, y, 1 - core), device_id_type=MESH)
            return loc, rem

        def drain(sl):
            loc, rem = copies(sl)
            loc.wait()
            rem.wait_send()

        pl.when(step >= 2)(lambda: drain(slot))
        acc = own.astype(F32)
        for s in range(ns):
            acc = acc + i_ref[s].astype(F32)
        buf[slot] = acc
        loc, rem = copies(slot)
        loc.start()
        rem.start()

        @pl.when(step == nsteps - 1)
        def _():
            drain(slot)
            if nsteps > 1:
                drain(1 - slot)
            for hh in range(n):
                for ii in range(nr):
                    land = o_ref.at[(1 - core) * n + hh, pl.ds(ii * tr, tr), :]
                    pltpu.make_async_remote_copy(
                        src_ref=buf.at[0], dst_ref=land, send_sem=send_sems.at[0], recv_sem=recv_sems.at[hh * nr + ii],
                        device_id=(x, y, 1 - core), device_id_type=MESH).wait_recv()

    return pl.pallas_call(
        body, name=name, out_shape=jax.ShapeDtypeStruct((2 * n, r, c), F32),
        grid_spec=pltpu.PrefetchScalarGridSpec(
            num_scalar_prefetch=1, grid=(n, nr),
            in_specs=[pl.BlockSpec((None, ns, tr, c), lambda h, i, s: (h, 0, i, 0))]
            + [pl.BlockSpec((tr, c), own_map)] * nlay,
            out_specs=_HBM,
            scratch_shapes=[pltpu.VMEM((2, tr, c), F32), pltpu.SemaphoreType.DMA((2,)),
                            pltpu.SemaphoreType.DMA((2,)), pltpu.SemaphoreType.DMA((nsteps,))]),
        compiler_params=_cparams(("arbitrary", "arbitrary")),
    )(chip, recv, *layer_grads)


def _gather_sums_over_chips(part):
    def body(i_ref, o_ref, send_sems, recv_sems):
        x, y, c = _coords()
        o_ref[2 * x + y] = i_ref[...]

        def copy(f, slot_chip):
            fx, fy = _CHIP_FLIPS[f]
            return pltpu.make_async_remote_copy(
                src_ref=i_ref, dst_ref=o_ref.at[2 * slot_chip[0] + slot_chip[1]], send_sem=send_sems.at[f - 1],
                recv_sem=recv_sems.at[f - 1], device_id=(_flip(x, fx), _flip(y, fy), c), device_id_type=MESH)

        sends = [copy(f, (x, y)) for f in (1, 2, 3)]
        for cp in sends:
            cp.start()
        for f in (1, 2, 3):
            fx, fy = _CHIP_FLIPS[f]
            copy(f, (_flip(x, fx), _flip(y, fy))).wait_recv()
        for cp in sends:
            cp.wait_send()

    vmem = pl.BlockSpec(memory_space=pltpu.VMEM)
    return pl.pallas_call(
        body, name="gather_small_sums", out_shape=jax.ShapeDtypeStruct((N_CHIPS,) + part.shape, part.dtype),
        in_specs=[vmem], out_specs=vmem,
        scratch_shapes=[pltpu.SemaphoreType.DMA((3,)), pltpu.SemaphoreType.DMA((3,))],
    )(part)


def _adamw(w, g, m, v, name):
    bc1 = 1.0 - ADAM_B1 ** ADAM_STEP
    bc2 = 1.0 - ADAM_B2 ** ADAM_STEP

    def body(w_ref, g_ref, m_ref, v_ref, d_ref, mo_ref, vo_ref):
        gv = g_ref[...]
        mn = ADAM_B1 * m_ref[...] + (1.0 - ADAM_B1) * gv
        vn = ADAM_B2 * v_ref[...] + (1.0 - ADAM_B2) * (gv * gv)
        d_ref[...] = -ADAM_LR * ((mn / bc1) / (jnp.sqrt(vn / bc2) + ADAM_EPS) + ADAM_WD * w_ref[...])
        mo_ref[...] = mn
        vo_ref[...] = vn

    sds = jax.ShapeDtypeStruct(w.shape, F32)
    if w.ndim == 2 and w.shape[0] % SUBLANES == 0:
        tr = _pick(w.shape[0], prefs=(256, 128, 64, 32, 16, 8))
        grid, blk = (w.shape[0] // tr,), pl.BlockSpec((tr, w.shape[1]), lambda i: (i, 0))
    else:
        nd = w.ndim
        grid, blk = (1,), pl.BlockSpec(w.shape, lambda i: (0,) * nd)
    return pl.pallas_call(
        body, name=name, out_shape=(sds, sds, sds), grid=grid, in_specs=[blk] * 4, out_specs=(blk,) * 3,
        compiler_params=_cparams(("parallel",)),
    )(w, g, m, v)


_PACK_QUANTUM = 256 * LANES


def _pack(arrs):
    flat = jnp.concatenate([a.reshape(-1).astype(F32) for a in arrs])
    flat = jnp.pad(flat, (0, (-flat.shape[0]) % _PACK_QUANTUM))
    return flat.reshape(-1, LANES)


def _unpack(p, shapes):
    flat = p.reshape(-1)
    out, off = [], 0
    for s in shapes:
        n = int(np.prod(s))
        out.append(flat[off:off + n].reshape(s))
        off += n
    return out


def kernel(*args):
    nw = len(WEIGHTS)
    x, tgt = args[0], args[1 + nw]
    w = dict(zip(WEIGHTS, args[1:1 + nw]))
    m = dict(zip(WEIGHTS, args[2 + nw:2 + 2 * nw]))
    v = dict(zip(WEIGHTS, args[2 + 2 * nw:2 + 3 * nw]))
    _, L, D = x.shape
    chip = 2 * lax.axis_index("x") + lax.axis_index("y")

    big = list(BIG)
    small_sh_shapes = [w[n].shape for n in SMALL_SHARDED]
    nbig = len(big)
    chip1 = chip.reshape(1).astype(jnp.int32)
    axes2 = [BIG[n] - 1 for n in big] + [0]
    shards = [w[n] for n in big] + [_pack([w[n] for n in SMALL_SHARDED])[None]]
    pairs = [(t, l) for t in range(nbig + 1) for l in range(shards[t].shape[0])]
    depth = w['norm_mix_g'].shape[0]
    part_of = lambda t, l: 0 if t == nbig else 2 * _model_layer(big[t], l) + big[t].startswith('ffn')
    flying, token = [], None
    for g in range(2 * depth):
        ids = [k for k, (t, l) in enumerate(pairs) if part_of(t, l) == g]
        ts = [pairs[k][0] for k in ids]
        placed = [_place_quarter(shards[t], pairs[k][1], axes2[t], chip1, F32 if t == nbig else BF16, token)
                  for k, t in zip(ids, ts)]
        lands, send, recv, token = _gather_start(g, placed, [axes2[t] for t in ts], token)
        flying.append((ts, lands, send, recv, token))

    def wait_group(g, after):
        ts, lands, send, recv, tok = flying[g]
        landed = _gather_wait(g, lands, send, recv, [axes2[t] for t in ts], tok if after is None else after)
        return dict(zip(ts, landed))

    first = wait_group(0, None)
    packed = first.pop(nbig).reshape(N_CHIPS, -1, LANES)
    per_chip = [_unpack(packed[s], small_sh_shapes) for s in range(N_CHIPS)]
    wl = dict(w)
    for k, n in enumerate(SMALL_SHARDED):
        wl[n] = jnp.concatenate([per_chip[s][k] for s in range(N_CHIPS)], axis=-1)

    def layer_weights(i, after):
        got = first if i == 0 else wait_group(i, after)
        return {big[t]: a for t, a in got.items()}

    small_shapes = [(w[n].shape[:-1] + (w[n].shape[-1] * N_CHIPS,)) if n in SMALL_SHARDED else w[n].shape
                    for n in SMALL] + [(1,)]
    n_small = sum(int(np.prod(s)) for s in small_shapes)
    pack_rows = -(-n_small // _PACK_QUANTUM) * _PACK_QUANTUM // LANES
    nlayers = [w[n].shape[0] for n in big] + [2]
    halves = [n // 2 for n in nlayers]
    quarters = [tuple(w[n].shape[1:]) for n in big] + [(pack_rows // 2 // N_CHIPS, LANES)]
    wire = [BF16] * nbig + [F32]
    land_now = [lax.empty((halves[t], N_SLOTS) + quarters[t], wire[t]) for t in range(nbig + 1)]
    gparts = [[None] * n for n in nlayers]
    started = []

    def start_scatter(tag, ts, ls, arrays):
        meta = [(axes2[t], l // halves[t], l % halves[t], quarters[t][axes2[t]]) for t, l in zip(ts, ls)]
        send, recv, thru, new_lands, token = _scatter_start(tag, arrays, [land_now[t] for t in ts], meta)
        for t, ln in zip(ts, new_lands):
            land_now[t] = ln
        started.append((send, recv, thru, ts, meta, ls))
        return token

    def on_layer_grads(g, gb):
        ts = [big.index(n) for n in gb]
        return start_scatter(g, ts, [g // 2 if big[t].startswith('ffn') else g // 4 for t in ts],
                             [gb[big[t]] for t in ts])

    loss, dx, gsmall = _local_step(x.reshape(L, D), tgt.reshape(L, D), wl, layer_weights, on_layer_grads)
    gpack = _pack([gsmall[n] for n in SMALL] + [loss.reshape(1)])
    start_scatter(2 * depth, [nbig, nbig], [0, 1], [gpack[:pack_rows // 2], gpack[pack_rows // 2:]])
    landed, sent = _scatter_wait([s[:5] for s in started], land_now)
    for (t, l), g in zip([(t, l) for s in started for t, l in zip(s[3], s[5])], sent):
        gparts[t][l] = g
    gshard = {n: _sum_and_share(landed[t], gparts[t], axes2[t], chip1, "sum_share_" + n) for t, n in enumerate(big)}
    small_sum = _sum_and_share(landed[nbig], gparts[nbig], 0, chip1, "sum_share_small")
    gpack = _gather_sums_over_chips(small_sum).transpose(1, 0, 2, 3).reshape(pack_rows, LANES)
    gs = dict(zip(SMALL + ['loss'], _unpack(gpack, small_shapes)))
    loss = gs.pop('loss').reshape(())
    for n in SMALL_SHARDED:
        width = w[n].shape[-1]
        gs[n] = lax.dynamic_slice_in_dim(gs[n], chip * width, width, axis=gs[n].ndim - 1)

    grads, delta, new_m, new_v = {}, {}, {}, {}
    for n in big:
        shp = w[n].shape
        flat = lambda a: a.reshape(shp[0] * shp[1], shp[2])
        g = gshard[n]
        grads[n] = g
        d_, m_, v_ = _adamw(flat(w[n]), flat(g), flat(m[n]), flat(v[n]), "adamw_" + n)
        delta[n], new_m[n], new_v[n] = d_.reshape(shp), m_.reshape(shp), v_.reshape(shp)
    for n in SMALL:
        shp = w[n].shape
        as2d = (lambda a: a.reshape(1, -1)) if len(shp) == 1 else (lambda a: a)
        d_, m_, v_ = _adamw(as2d(w[n]), as2d(gs[n]), as2d(m[n]), as2d(v[n]), "adamw_" + n)
        grads[n], delta[n], new_m[n], new_v[n] = gs[n], d_.reshape(shp), m_.reshape(shp), v_.reshape(shp)

    return (loss, dx.reshape(1, L, D), *[grads[n] for n in WEIGHTS], *[delta[n] for n in WEIGHTS],
            *[new_m[n] for n in WEIGHTS], *[new_v[n] for n in WEIGHTS])
```

```python
import functools
import math

import numpy as np
import jax
import jax.numpy as jnp
from jax import lax
from jax.experimental import pallas as pl
from jax.experimental.pallas import tpu as pltpu

F32 = jnp.float32
BF16 = jnp.bfloat16
MESH = pl.DeviceIdType.MESH

EPS = 1e-6
CHUNK = 128
POOL_WINDOWS = (2, 4, 8, 16)
LANES = 128
SUBLANES = 8
SCAN_CHUNKS = SUBLANES
S5_GROUPS_PER_STEP = 4
MM_TM_CAP, MM_TN_CAP, MM_TK_CAP = 1408, 1408, 2048
MM_TK_WHOLE = 2048
VMEM_LIMIT = 48 * 1024 * 1024
VMEM_LIMIT_S5 = 56 * 1024 * 1024

ADAM_LR, ADAM_B1, ADAM_B2, ADAM_EPS, ADAM_WD, ADAM_STEP = 0.001, 0.9, 0.999, 1e-08, 0.01, 10

WEIGHTS = ['norm_mix_g', 'even_w_in', 'even_conv_w', 'ssm_log_step', 'ssm_a_re', 'ssm_a_im', 'ssm_b_re',
           'ssm_b_im', 'ssm_c_re', 'ssm_c_im', 'ssm_d', 'ssm_glu_w', 'ssm_glu_b', 'even_w_out', 'odd_w_in',
           'pool_w', 'pool_scale', 'sgu_norm_g', 'sgu_w', 'sgu_b', 'odd_w_out', 'norm_ffn_g', 'ffn_w_up',
           'ffn_conv_w', 'ffn_conv_b', 'ffn_w_down', 'norm_final_g']
BIG = {'even_w_in': 2, 'ssm_glu_w': 1, 'even_w_out': 1, 'odd_w_in': 2, 'odd_w_out': 1, 'ffn_w_up': 2,
       'ffn_w_down': 1}
SMALL_SHARDED = ('even_conv_w', 'pool_scale', 'sgu_norm_g', 'ffn_conv_w')
SMALL = [n for n in WEIGHTS if n not in BIG]
N_CHIPS = 4
N_DEV = 8


def _cparams(sem=None, vmem=VMEM_LIMIT):
    kw = dict(vmem_limit_bytes=vmem)
    if sem is not None:
        kw['dimension_semantics'] = sem
    return pltpu.CompilerParams(**kw)


def _pick(n, segs=(), prefs=(1024, 512, 256, 128)):
    for t in prefs:
        if n % t == 0 and all(s % t == 0 for s in segs if s):
            return t
    return n


def _largest_tile(n, segs, cap):
    best = None
    for t in range(LANES, min(n, cap) + 1, LANES):
        if n % t == 0 and all(s % t == 0 for s in segs if s):
            best = t
    return best if best is not None else n


def _ldims(arr, kind):
    if kind is None:
        return arr.shape
    if kind[0] == 'lead':
        return arr.shape[1:]
    return (arr.shape[1], arr.shape[0] * arr.shape[2])


def _segw(arr, kind):
    return arr.shape[2] if (kind is not None and kind[0] == 'seg') else None


def _opspec(arr, kind, br, bc, rfn, cfn):
    if kind is None:
        return pl.BlockSpec((br, bc), lambda i, j, k: (rfn(i, j, k), cfn(i, j, k)))
    if kind[0] == 'lead':
        lead = kind[1]
        return pl.BlockSpec((None, br, bc), lambda i, j, k: (lead, rfn(i, j, k), cfn(i, j, k)))
    per = arr.shape[2] // bc
    return pl.BlockSpec((None, br, bc), lambda i, j, k: (cfn(i, j, k) // per, rfn(i, j, k), cfn(i, j, k) % per))


def _mm(a, b, mode, out_dtype, name, ak=None, bk=None, ok=None, res=None, dep=None):
    ar, ac = _ldims(a, ak)
    br_, bc_ = _ldims(b, bk)
    if mode == 'nn':
        M, K, N = ar, ac, bc_
        assert br_ == K
    else:
        M, K, N = ar, ac, br_
        assert bc_ == K
    sa, sb = _segw(a, ak), _segw(b, bk)
    so = (N // ok[1]) if ok is not None else None
    tm = _largest_tile(M, [], MM_TM_CAP)
    tn = _largest_tile(N, [sb if mode == 'nn' else None, so], MM_TN_CAP)
    ksegs = [sa, sb if mode == 'nt' else None]
    tk = K if (K <= MM_TK_WHOLE and not any(ksegs)) else _largest_tile(K, ksegs, MM_TK_CAP)
    nk = K // tk
    I = lambda i, j, k: i
    J = lambda i, j, k: j
    Kk = lambda i, j, k: k
    a_spec = _opspec(a, ak, tm, tk, I, Kk)
    if mode == 'nn':
        b_spec = _opspec(b, bk, tk, tn, Kk, J)
        dims = (((1,), (0,)), ((), ()))
    else:
        b_spec = _opspec(b, bk, tn, tk, J, Kk)
        dims = (((1,), (1,)), ((), ()))
    if ok is None:
        out_shape = jax.ShapeDtypeStruct((M, N), out_dtype)
        o_spec = pl.BlockSpec((tm, tn), lambda i, j, k: (i, j))
    else:
        out_shape = jax.ShapeDtypeStruct((ok[1], M, N // ok[1]), out_dtype)
        per = (N // ok[1]) // tn
        o_spec = pl.BlockSpec((None, tm, tn), lambda i, j, k: (j // per, i, j % per))
    has_res = res is not None

    def body(*refs):
        a_ref, b_ref = refs[0], refs[1]
        r_ref = refs[2] if has_res else None
        o_ref = refs[n_in]
        prod = lax.dot_general(a_ref[...].astype(BF16), b_ref[...].astype(BF16), dims, preferred_element_type=F32)
        if nk == 1:
            o_ref[...] = (prod + r_ref[...] if has_res else prod).astype(out_dtype)
            return
        acc = refs[-1]
        k = pl.program_id(2)

        @pl.when(k == 0)
        def _():
            acc[...] = prod

        @pl.when(k > 0)
        def _():
            acc[...] += prod

        @pl.when(k == nk - 1)
        def _():
            o = acc[...]
            if has_res:
                o = o + r_ref[...]
            o_ref[...] = o.astype(out_dtype)

    in_specs = [a_spec, b_spec]
    args = [a, b]
    if has_res:
        in_specs.append(pl.BlockSpec((tm, tn), lambda i, j, k: (i, j)))
        args.append(res)
    if dep is not None:
        in_specs.append(pl.BlockSpec(memory_space=pl.ANY))
        args.append(dep)
    n_in = len(args)
    return pl.pallas_call(
        body, name=name, out_shape=out_shape, grid=(M // tm, N // tn, nk), in_specs=in_specs, out_specs=o_spec,
        scratch_shapes=[pltpu.VMEM((tm, tn), F32)] if nk > 1 else [],
        compiler_params=_cparams(("parallel", "parallel", "arbitrary")),
    )(*args)


_G0 = math.sqrt(2.0 / math.pi)
_G1 = 0.044715


def _gelu(x):
    return 0.5 * x * (1.0 + jnp.tanh(_G0 * (x + _G1 * x * x * x)))


def _gelu_grad(x):
    x2 = x * x
    t = jnp.tanh(_G0 * (x + _G1 * x * x2))
    return 0.5 * (1.0 + t) + 0.5 * x * (1.0 - t * t) * (_G0 * (1.0 + 3.0 * _G1 * x2))


def _sigmoid(x):
    return 1.0 / (1.0 + jnp.exp(-x))


def _down(v, k):
    r = pltpu.roll(v, k, axis=0)
    row = lax.broadcasted_iota(jnp.int32, (SUBLANES, v.shape[1]), 0)
    return jnp.concatenate([jnp.where(row >= k, r[:SUBLANES], 0.0), r[SUBLANES:]], axis=0)


def _up(v, k):
    n = v.shape[0]
    r = pltpu.roll(v, n - k, axis=0)
    row = lax.broadcasted_iota(jnp.int32, (SUBLANES, v.shape[1]), 0)
    return jnp.concatenate([r[:n - SUBLANES], jnp.where(row < SUBLANES - k, r[n - SUBLANES:], 0.0)], axis=0)


def _conv3(v, w):
    return w[0:1, :] * _down(v, 2) + w[1:2, :] * _down(v, 1) + w[2:3, :] * v


def _conv3_t(dv, w):
    return w[2:3, :] * dv + w[1:2, :] * _up(dv, 1) + w[0:1, :] * _up(dv, 2)


def _conv3_dw(dv, v):
    return (jnp.sum(dv * _down(v, 2), axis=0, keepdims=True),
            jnp.sum(dv * _down(v, 1), axis=0, keepdims=True),
            jnp.sum(dv * v, axis=0, keepdims=True))


def _cmul(ar, ai, br, bi):
    return ar * br - ai * bi, ar * bi + ai * br


def _cpow(lr, li, n):
    rr = ri = None
    br, bi = lr, li
    while n:
        if n & 1:
            rr, ri = (br, bi) if rr is None else _cmul(rr, ri, br, bi)
        n >>= 1
        if n:
            br, bi = _cmul(br, bi, br, bi)
    return rr, ri


NORM_ROWS = 256


def _norm_mm(x, g, b, out_dtype, name, ok=None):
    M, D = x.shape
    N = b.shape[1]
    so = (N // ok[1]) if ok is not None else None
    tm = _largest_tile(M, [], 1024)
    tn = _largest_tile(N, [so], MM_TN_CAP)
    if ok is None:
        out_shape = jax.ShapeDtypeStruct((M, N), out_dtype)
        o_spec = pl.BlockSpec((tm, tn), lambda i, j: (i, j))
    else:
        out_shape = jax.ShapeDtypeStruct((ok[1], M, N // ok[1]), out_dtype)
        per = (N // ok[1]) // tn
        o_spec = pl.BlockSpec((None, tm, tn), lambda i, j: (j // per, i, j % per))

    def body(x_ref, g_ref, b_ref, o_ref, ht_ref, h_scr):
        @pl.when(pl.program_id(1) == 0)
        def _():
            for c in range(tm // NORM_ROWS):
                rows = pl.ds(c * NORM_ROWS, NORM_ROWS)
                xv = x_ref[rows, :]
                h = xv * lax.rsqrt(jnp.mean(xv * xv, axis=-1, keepdims=True) + EPS) * g_ref[...]
                h_scr[rows, :] = h.astype(BF16)
                ht_ref[:, rows] = h.T.astype(BF16)

        o_ref[...] = jnp.dot(h_scr[...], b_ref[...], preferred_element_type=F32).astype(out_dtype)

    return pl.pallas_call(
        body, name=name, out_shape=(out_shape, jax.ShapeDtypeStruct((D, M), BF16)), grid=(M // tm, N // tn),
        in_specs=[pl.BlockSpec((tm, D), lambda i, j: (i, 0)), pl.BlockSpec((1, D), lambda i, j: (0, 0)),
                  pl.BlockSpec((D, tn), lambda i, j: (0, j))],
        out_specs=(o_spec, pl.BlockSpec((D, tm), lambda i, j: (0, i))),
        scratch_shapes=[pltpu.VMEM((tm, D), BF16)], compiler_params=_cparams(("parallel", "arbitrary")),
    )(x, g.reshape(1, D), b)


def _mm_norm_bwd(a, b, x, g, dres, name, ak=None, dep=None):
    M, K = _ldims(a, ak)
    D = b.shape[0]
    assert b.shape[1] == K and x.shape == (M, D)
    sa = _segw(a, ak)
    tm = _largest_tile(M, [], 1024)
    tk = K if (K <= MM_TK_WHOLE and not sa) else _largest_tile(K, [sa], MM_TK_CAP)
    ni, nk = M // tm, K // tk
    a3 = _opspec(a, ak, tm, tk, lambda i, j, k: i, lambda i, j, k: k)
    a_spec = pl.BlockSpec(a3.block_shape, lambda i, k: a3.index_map(i, 0, k))
    n_in = 5 + (dep is not None)

    def body(*refs):
        a_ref, b_ref, x_ref, g_ref, r_ref = refs[:5]
        dx_ref, dxb_ref, dg_ref, acc, accg = refs[n_in:]
        i, k = pl.program_id(0), pl.program_id(1)
        prod = lax.dot_general(a_ref[...].astype(BF16), b_ref[...], (((1,), (1,)), ((), ())),
                               preferred_element_type=F32)

        @pl.when(k == 0)
        def _():
            acc[...] = prod

        @pl.when(k > 0)
        def _():
            acc[...] += prod

        @pl.when((i == 0) & (k == 0))
        def _():
            accg[...] = jnp.zeros_like(accg)

        @pl.when(k == nk - 1)
        def _():
            for c in range(tm // NORM_ROWS):
                rows = pl.ds(c * NORM_ROWS, NORM_ROWS)
                xv = x_ref[rows, :]
                r = lax.rsqrt(jnp.mean(xv * xv, axis=-1, keepdims=True) + EPS)
                xh = xv * r
                dhv = acc[rows, :]
                accg[...] += jnp.sum((dhv * xh).reshape(NORM_ROWS // SUBLANES, SUBLANES, D), axis=0)
                dxh = dhv * g_ref[...]
                dxv = r_ref[rows, :] + r * (dxh - xh * jnp.mean(dxh * xh, axis=-1, keepdims=True))
                dx_ref[rows, :] = dxv
                dxb_ref[rows, :] = dxv.astype(BF16)

        @pl.when((i == ni - 1) & (k == nk - 1))
        def _():
            dg_ref[...] = jnp.sum(accg[...], axis=0, keepdims=True)

    row = pl.BlockSpec((tm, D), lambda i, k: (i, 0))
    vec = pl.BlockSpec((1, D), lambda i, k: (0, 0))
    in_specs = [a_spec, pl.BlockSpec((D, tk), lambda i, k: (0, k)), row, vec, row]
    args = [a, b, x, g.reshape(1, D), dres]
    if dep is not None:
        in_specs.append(pl.BlockSpec(memory_space=pl.ANY))
        args.append(dep)
    return pl.pallas_call(
        body, name=name,
        out_shape=(jax.ShapeDtypeStruct((M, D), F32), jax.ShapeDtypeStruct((M, D), BF16),
                   jax.ShapeDtypeStruct((1, D), F32)),
        grid=(ni, nk), in_specs=in_specs, out_specs=(row, row, vec),
        scratch_shapes=[pltpu.VMEM((tm, D), F32), pltpu.VMEM((SUBLANES, D), F32)],
        compiler_params=_cparams(("arbitrary", "arbitrary"), VMEM_LIMIT_S5),
    )(*args)


def _loss_head(x, g, tgt):
    L, D = x.shape
    tr = _pick(L, prefs=(512, 256, 128))
    nsteps = L // tr

    def body(x_ref, g_ref, t_ref, loss_ref, dx_ref, dxb_ref, dg_ref, acc_g, acc_l):
        i = pl.program_id(0)

        @pl.when(i == 0)
        def _():
            acc_g[...] = jnp.zeros_like(acc_g)
            acc_l[...] = jnp.zeros_like(acc_l)

        xv = x_ref[...]
        gv = g_ref[...]
        r = lax.rsqrt(jnp.mean(xv * xv, axis=-1, keepdims=True) + EPS)
        xh = xv * r
        e = xh * gv - t_ref[...]
        acc_l[...] += jnp.sum((e * e).reshape(tr // SUBLANES, SUBLANES, D), axis=0)
        dy = e * (1.0 / D)
        acc_g[...] += jnp.sum((dy * xh).reshape(tr // SUBLANES, SUBLANES, D), axis=0)
        dxh = dy * gv
        dxv = r * (dxh - xh * jnp.mean(dxh * xh, axis=-1, keepdims=True))
        dx_ref[...] = dxv
        dxb_ref[...] = dxv.astype(BF16)

        @pl.when(i == nsteps - 1)
        def _():
            dg_ref[...] = jnp.sum(acc_g[...], axis=0, keepdims=True)
            tot = jnp.sum(jnp.sum(acc_l[...], axis=0, keepdims=True), axis=1, keepdims=True) * (0.5 / D)
            loss_ref[...] = jnp.broadcast_to(tot, (SUBLANES, LANES))

    row = pl.BlockSpec((tr, D), lambda i: (i, 0))
    vec = pl.BlockSpec((1, D), lambda i: (0, 0))
    return pl.pallas_call(
        body, name="loss_head",
        out_shape=(jax.ShapeDtypeStruct((SUBLANES, LANES), F32), jax.ShapeDtypeStruct((L, D), F32),
                   jax.ShapeDtypeStruct((L, D), BF16), jax.ShapeDtypeStruct((1, D), F32)),
        grid=(nsteps,), in_specs=[row, vec, row],
        out_specs=(pl.BlockSpec((SUBLANES, LANES), lambda i: (0, 0)), row, row, vec),
        scratch_shapes=[pltpu.VMEM((SUBLANES, D), F32), pltpu.VMEM((SUBLANES, D), F32)],
        compiler_params=_cparams(("arbitrary",)),
    )(x, g.reshape(1, D), tgt)


def _sconv_fwd(proj4, conv_w, name):
    _, L, C = proj4.shape
    cb = LANES

    def body(p_ref, w_ref, o_ref):
        xa, ba, ca = p_ref[0], p_ref[1], p_ref[2]
        o_ref[...] = (ba * _conv3(ca * xa, w_ref[...])).astype(BF16)

    return pl.pallas_call(
        body, name=name, out_shape=jax.ShapeDtypeStruct((L, C), BF16), grid=(C // cb,),
        in_specs=[pl.BlockSpec((3, L, cb), lambda j: (0, 0, j)), pl.BlockSpec((3, cb), lambda j: (0, j))],
        out_specs=pl.BlockSpec((L, cb), lambda j: (0, j)), compiler_params=_cparams(("parallel",)),
    )(proj4, conv_w)


def _sconv_bwd(proj4, dmix, conv_w, name):
    _, L, C = proj4.shape
    cb = LANES

    def body(p_ref, d_ref, w_ref, o_ref, dw_ref):
        xa, ba, ca = p_ref[0], p_ref[1], p_ref[2]
        w = w_ref[...]
        dya = d_ref[...]
        q = ca * xa
        cq = _conv3(q, w)
        dcq = dya * ba
        dq = _conv3_t(dcq, w)
        for tap, dwt in enumerate(_conv3_dw(dcq, q)):
            dw_ref[tap:tap + 1, :] = dwt
        o_ref[0] = (dq * ca).astype(BF16)
        o_ref[1] = (dya * cq).astype(BF16)
        o_ref[2] = (dq * xa).astype(BF16)

    return pl.pallas_call(
        body, name=name,
        out_shape=(jax.ShapeDtypeStruct((3, L, C), BF16), jax.ShapeDtypeStruct((3, C), F32)), grid=(C // cb,),
        in_specs=[pl.BlockSpec((3, L, cb), lambda j: (0, 0, j)), pl.BlockSpec((L, cb), lambda j: (0, j)),
                  pl.BlockSpec((3, cb), lambda j: (0, j))],
        out_specs=(pl.BlockSpec((3, L, cb), lambda j: (0, 0, j)), pl.BlockSpec((3, cb), lambda j: (0, j))),
        compiler_params=_cparams(("parallel",)),
    )(proj4, dmix, conv_w)


def _to_scan_order(v):
    L, C = v.shape
    return v.reshape(SCAN_CHUNKS, L // SCAN_CHUNKS, C).transpose(1, 0, 2).reshape(L, C)


def _from_scan_order(v):
    L, C = v.shape
    return v.reshape(L // SCAN_CHUNKS, SCAN_CHUNKS, C).transpose(1, 0, 2).reshape(L, C)


def _s5_prep(log_step, a_re, a_im, b_re, b_im, c_re, c_im):
    G, P = a_re.shape
    H = b_re.shape[-1]
    gs = S5_GROUPS_PER_STEP
    ns = G // gs
    gu = LANES // H
    lam = lax.complex(a_re, a_im)
    step = jnp.exp(log_step)[:, None]
    lam_bar = jnp.exp(lam * step)
    b_bar = ((lam_bar - 1.0) / lam)[..., None] * lax.complex(b_re, b_im)
    lr = jnp.real(lam_bar).reshape(ns, 1, gs * P)
    li = jnp.imag(lam_bar).reshape(ns, 1, gs * P)
    k = np.arange(ns)[:, None, None]
    oh = jnp.asarray((np.arange(gu)[None, :, None] == gs * (k % (gu // gs)) + np.arange(gs)[None, None, :]),
                     F32)
    bre = jnp.einsum('kgl,klph->kghlp', oh, jnp.real(b_bar).reshape(ns, gs, P, H)).reshape(ns, gu * H, gs * P)
    bim = jnp.einsum('kgl,klph->kghlp', oh, jnp.imag(b_bar).reshape(ns, gs, P, H)).reshape(ns, gu * H, gs * P)
    cre = jnp.einsum('kgl,klhp->klpgh', oh, c_re.reshape(ns, gs, H, P)).reshape(ns, gs * P, gu * H)
    cim = jnp.einsum('kgl,klhp->klpgh', oh, c_im.reshape(ns, gs, H, P)).reshape(ns, gs * P, gu * H)
    return lr, li, jnp.concatenate([bre, bim], axis=2), jnp.concatenate([cre, -cim], axis=1)


def _carry_tile(fr, fi, pr, pi, reverse):
    row = lax.broadcasted_iota(jnp.int32, fr.shape, 0)
    cr = jnp.zeros_like(fr)
    ci = jnp.zeros_like(fi)
    sr = jnp.zeros_like(fr[0:1])
    si = jnp.zeros_like(sr)
    order = range(SCAN_CHUNKS - 1, 0, -1) if reverse else range(0, SCAN_CHUNKS - 1)
    for c in order:
        fcr = jnp.sum(jnp.where(row == c, fr, 0.0), axis=0, keepdims=True)
        fci = jnp.sum(jnp.where(row == c, fi, 0.0), axis=0, keepdims=True)
        mr, mi = _cmul(pr, pi, sr, si)
        sr, si = mr + fcr, mi + fci
        nxt = c - 1 if reverse else c + 1
        cr = jnp.where(row == nxt, sr, cr)
        ci = jnp.where(row == nxt, si, ci)
    return cr, ci


def _s5_fwd(u, lr, li, bmat, cmat, d, name):
    L, Du = u.shape
    ns, _, W2 = bmat.shape
    W = W2 // 2
    T = L // SCAN_CHUNKS
    rb = _pick(L, prefs=(512, 256, 128))
    per = (ns * LANES) // Du

    def body(u_ref, lr_ref, li_ref, b_ref, c_ref, d_ref, y_ref, sr_ref, si_ref):
        k = pl.program_id(0)
        for r in range(L // rb):
            rows = pl.ds(r * rb, rb)
            bu = jnp.dot(u_ref[rows, :].astype(BF16), b_ref[...], preferred_element_type=F32)
            sr_ref[rows, :] = bu[:, :W]
            si_ref[rows, :] = bu[:, W:]
        lam_r = jnp.broadcast_to(lr_ref[...], (SUBLANES, W))
        lam_i = jnp.broadcast_to(li_ref[...], (SUBLANES, W))

        def local(t, carry):
            sr, si = carry
            rows = pl.ds(pl.multiple_of(t * SUBLANES, SUBLANES), SUBLANES)
            mr, mi = _cmul(lam_r, lam_i, sr, si)
            sr = mr + sr_ref[rows, :]
            si = mi + si_ref[rows, :]
            sr_ref[rows, :] = sr
            si_ref[rows, :] = si
            return sr, si

        z = jnp.zeros((SUBLANES, W), F32)
        fr, fi = lax.fori_loop(0, T, local, (z, z))
        pr, pi = _cpow(lam_r, lam_i, T)
        cr, ci = _carry_tile(fr, fi, pr[0:1], pi[0:1], reverse=False)

        def fix(t, carry):
            wr, wi = carry
            rows = pl.ds(pl.multiple_of(t * SUBLANES, SUBLANES), SUBLANES)
            ar, ai = _cmul(wr, wi, cr, ci)
            sr_ref[rows, :] += ar
            si_ref[rows, :] += ai
            return _cmul(wr, wi, lam_r, lam_i)

        lax.fori_loop(0, T, fix, (lam_r, lam_i))
        first = (k % per) == 0
        for r in range(L // rb):
            rows = pl.ds(r * rb, rb)
            s = jnp.concatenate([sr_ref[rows, :], si_ref[rows, :]], axis=1).astype(BF16)
            y = jnp.dot(s, c_ref[...], preferred_element_type=F32)

            @pl.when(first)
            def _():
                y_ref[rows, :] = y + d_ref[...] * u_ref[rows, :]

            @pl.when(jnp.logical_not(first))
            def _():
                y_ref[rows, :] += y

    ublk = pl.BlockSpec((L, LANES), lambda k: (0, k // per))
    sblk = pl.BlockSpec((L, W), lambda k: (0, k))
    lam = pl.BlockSpec((None, 1, W), lambda k: (k, 0, 0))
    return pl.pallas_call(
        body, name=name,
        out_shape=(jax.ShapeDtypeStruct((L, Du), F32), jax.ShapeDtypeStruct((L, ns * W), F32),
                   jax.ShapeDtypeStruct((L, ns * W), F32)),
        grid=(ns,),
        in_specs=[ublk, lam, lam, pl.BlockSpec((None, LANES, 2 * W), lambda k: (k, 0, 0)),
                  pl.BlockSpec((None, 2 * W, LANES), lambda k: (k, 0, 0)),
                  pl.BlockSpec((1, LANES), lambda k: (0, k // per))],
        out_specs=(ublk, sblk, sblk), compiler_params=_cparams(("arbitrary",), VMEM_LIMIT_S5),
    )(u, lr, li, bmat.astype(BF16), cmat.astype(BF16), d.reshape(1, Du))


def _s5_bwd(dy, u, s_re, s_im, lr, li, bmat, cmat, d, name):
    L, Du = u.shape
    ns, _, W2 = bmat.shape
    W = W2 // 2
    T = L // SCAN_CHUNKS
    rb = _pick(L, prefs=(512, 256, 128))
    per = (ns * LANES) // Du
    NT = (((1,), (1,)), ((), ()))
    TN = (((0,), (0,)), ((), ()))

    def body(dy_ref, u_ref, sr_ref, si_ref, lr_ref, li_ref, b_ref, c_ref, d_ref,
             du_ref, db_ref, dc_ref, dl_ref, dd_ref, gr_ref, gi_ref):
        k = pl.program_id(0)
        for r in range(L // rb):
            rows = pl.ds(r * rb, rb)
            g = lax.dot_general(dy_ref[rows, :].astype(BF16), c_ref[...], NT, preferred_element_type=F32)
            gr_ref[rows, :] = g[:, :W]
            gi_ref[rows, :] = g[:, W:]
        lam_r = jnp.broadcast_to(lr_ref[...], (SUBLANES, W))
        lam_i = -jnp.broadcast_to(li_ref[...], (SUBLANES, W))

        def local(i, carry):
            gr, gi = carry
            rows = pl.ds(pl.multiple_of((T - 1 - i) * SUBLANES, SUBLANES), SUBLANES)
            mr, mi = _cmul(lam_r, lam_i, gr, gi)
            gr = mr + gr_ref[rows, :]
            gi = mi + gi_ref[rows, :]
            gr_ref[rows, :] = gr
            gi_ref[rows, :] = gi
            return gr, gi

        z = jnp.zeros((SUBLANES, W), F32)
        fr, fi = lax.fori_loop(0, T, local, (z, z))
        pr, pi = _cpow(lam_r, lam_i, T)
        cr, ci = _carry_tile(fr, fi, pr[0:1], pi[0:1], reverse=True)

        def true_g(rows, wr, wi):
            ar, ai = _cmul(wr, wi, cr, ci)
            gr = gr_ref[rows, :] + ar
            gi = gi_ref[rows, :] + ai
            gr_ref[rows, :] = gr
            gi_ref[rows, :] = gi
            return gr, gi

        def fix(i, carry):
            wr, wi, ar_, ai_ = carry
            t = T - 1 - i
            rows = pl.ds(pl.multiple_of(t * SUBLANES, SUBLANES), SUBLANES)
            prev = pl.ds(pl.multiple_of((t - 1) * SUBLANES, SUBLANES), SUBLANES)
            gr, gi = true_g(rows, wr, wi)
            qr, qi = sr_ref[prev, :], si_ref[prev, :]
            ar_ = ar_ + gr * qr + gi * qi
            ai_ = ai_ + gi * qr - gr * qi
            wr, wi = _cmul(wr, wi, lam_r, lam_i)
            return wr, wi, ar_, ai_

        wr, wi, acc_r, acc_i = lax.fori_loop(0, T - 1, fix, (lam_r, lam_i, z, z))
        gr, gi = true_g(pl.ds(0, SUBLANES), wr, wi)
        last = pl.ds((T - 1) * SUBLANES, SUBLANES)
        row = lax.broadcasted_iota(jnp.int32, (SUBLANES, W), 0)
        qr = jnp.where(row >= 1, pltpu.roll(sr_ref[last, :], 1, axis=0), 0.0)
        qi = jnp.where(row >= 1, pltpu.roll(si_ref[last, :], 1, axis=0), 0.0)
        acc_r = acc_r + gr * qr + gi * qi
        acc_i = acc_i + gi * qr - gr * qi
        dl_ref[0:1, :] = jnp.sum(acc_r, axis=0, keepdims=True)
        dl_ref[1:2, :] = jnp.sum(acc_i, axis=0, keepdims=True)

        first = (k % per) == 0
        db = jnp.zeros((LANES, 2 * W), F32)
        dc = jnp.zeros((LANES, 2 * W), F32)
        dd = jnp.zeros((1, LANES), F32)
        for r in range(L // rb):
            rows = pl.ds(r * rb, rb)
            gb = jnp.concatenate([gr_ref[rows, :], gi_ref[rows, :]], axis=1).astype(BF16)
            sb = jnp.concatenate([sr_ref[rows, :], si_ref[rows, :]], axis=1).astype(BF16)
            dyv = dy_ref[rows, :]
            uv = u_ref[rows, :]
            du = lax.dot_general(gb, b_ref[...], NT, preferred_element_type=F32)
            db = db + lax.dot_general(uv.astype(BF16), gb, TN, preferred_element_type=F32)
            dc = dc + lax.dot_general(dyv.astype(BF16), sb, TN, preferred_element_type=F32)
            dd = dd + jnp.sum(dyv * uv, axis=0, keepdims=True)

            @pl.when(first)
            def _():
                du_ref[rows, :] = du + d_ref[...] * dyv

            @pl.when(jnp.logical_not(first))
            def _():
                du_ref[rows, :] += du

        db_ref[...] = db
        dc_ref[...] = dc

        @pl.when(first)
        def _():
            dd_ref[...] = dd

    ublk = pl.BlockSpec((L, LANES), lambda k: (0, k // per))
    sblk = pl.BlockSpec((L, W), lambda k: (0, k))
    lam = pl.BlockSpec((None, 1, W), lambda k: (k, 0, 0))
    vec = pl.BlockSpec((1, LANES), lambda k: (0, k // per))
    mat = pl.BlockSpec((None, LANES, 2 * W), lambda k: (k, 0, 0))
    return pl.pallas_call(
        body, name=name,
        out_shape=(jax.ShapeDtypeStruct((L, Du), F32), jax.ShapeDtypeStruct((ns, LANES, 2 * W), F32),
                   jax.ShapeDtypeStruct((ns, LANES, 2 * W), F32), jax.ShapeDtypeStruct((ns, 2, W), F32),
                   jax.ShapeDtypeStruct((1, Du), F32)),
        grid=(ns,),
        in_specs=[ublk, ublk, sblk, sblk, lam, lam, mat,
                  pl.BlockSpec((None, 2 * W, LANES), lambda k: (k, 0, 0)), vec],
        out_specs=(ublk, mat, mat, pl.BlockSpec((None, 2, W), lambda k: (k, 0, 0)), vec),
        scratch_shapes=[pltpu.VMEM((L, W), F32), pltpu.VMEM((L, W), F32)],
        compiler_params=_cparams(("arbitrary",), VMEM_LIMIT_S5),
    )(dy, u, s_re, s_im, lr, li, bmat.astype(BF16), cmat.astype(BF16), d.reshape(1, Du))


def _glu_fwd(yraw, wmat, bias, name):
    L, C = yraw.shape
    tr = _pick(L, prefs=(512, 256, 128))

    def body(y_ref, w_ref, b_ref, o_ref):
        yg = _gelu(y_ref[...])
        zz = jnp.dot(yg.astype(BF16), w_ref[...], preferred_element_type=F32) + b_ref[...]
        o_ref[...] = (yg * _sigmoid(zz)).astype(BF16)

    return pl.pallas_call(
        body, name=name, out_shape=jax.ShapeDtypeStruct((L, C), BF16), grid=(L // tr,),
        in_specs=[pl.BlockSpec((tr, C), lambda i: (i, 0)), pl.BlockSpec((C, C), lambda i: (0, 0)),
                  pl.BlockSpec((1, C), lambda i: (0, 0))],
        out_specs=pl.BlockSpec((tr, C), lambda i: (i, 0)), compiler_params=_cparams(("parallel",)),
    )(yraw, wmat, bias.reshape(1, C))


def _glu_bwd(yraw, dyb, wmat, bias, name):
    L, C = yraw.shape
    tr = _pick(L, prefs=(512, 256, 128))
    nsteps = L // tr

    def body(y_ref, d_ref, w_ref, b_ref, dy_ref, dw_ref, db_ref, acc_b):
        i = pl.program_id(0)

        @pl.when(i == 0)
        def _():
            dw_ref[...] = jnp.zeros_like(dw_ref)
            acc_b[...] = jnp.zeros_like(acc_b)

        yr = y_ref[...]
        yg = _gelu(yr)
        ygb = yg.astype(BF16)
        sg = _sigmoid(jnp.dot(ygb, w_ref[...], preferred_element_type=F32) + b_ref[...])
        dyb_ = d_ref[...]
        dz = dyb_ * yg * sg * (1.0 - sg)
        dzb = dz.astype(BF16)
        dyg = dyb_ * sg + lax.dot_general(dzb, w_ref[...], (((1,), (1,)), ((), ())), preferred_element_type=F32)
        dw_ref[...] += lax.dot_general(ygb, dzb, (((0,), (0,)), ((), ())), preferred_element_type=F32)
        acc_b[...] += jnp.sum(dz.reshape(tr // SUBLANES, SUBLANES, C), axis=0)
        dy_ref[...] = dyg * _gelu_grad(yr)

        @pl.when(i == nsteps - 1)
        def _():
            db_ref[...] = jnp.sum(acc_b[...], axis=0, keepdims=True)

    row = pl.BlockSpec((tr, C), lambda i: (i, 0))
    return pl.pallas_call(
        body, name=name,
        out_shape=(jax.ShapeDtypeStruct((L, C), F32), jax.ShapeDtypeStruct((C, C), F32),
                   jax.ShapeDtypeStruct((1, C), F32)),
        grid=(nsteps,),
        in_specs=[row, row, pl.BlockSpec((C, C), lambda i: (0, 0)), pl.BlockSpec((1, C), lambda i: (0, 0))],
        out_specs=(row, pl.BlockSpec((C, C), lambda i: (0, 0)), pl.BlockSpec((1, C), lambda i: (0, 0))),
        scratch_shapes=[pltpu.VMEM((SUBLANES, C), F32)], compiler_params=_cparams(("arbitrary",)),
    )(yraw, dyb, wmat, bias.reshape(1, C))


def _pool_counts(L, g):
    t = lax.broadcasted_iota(jnp.int32, (L, LANES), 0).astype(F32) + 1.0
    w = jnp.where(g == 0, 2.0, jnp.where(g == 1, 4.0, jnp.where(g == 2, 8.0, 16.0)))
    return 1.0 / jnp.minimum(t, w)


def _select_window(g, a2, a4, a8, a16):
    return jnp.where(g == 0, a2, jnp.where(g == 1, a4, jnp.where(g == 2, a8, a16)))


def _pooled(z, g):
    a2 = z + _down(z, 1)
    a4 = a2 + _down(a2, 2)
    a8 = a4 + _down(a4, 4)
    a16 = a8 + _down(a8, 8)
    return _select_window(g, a2, a4, a8, a16) * _pool_counts(z.shape[0], g) - z


def _pool_fwd(proj3, pool_w, scale, name):
    _, L, C = proj3.shape
    ng = len(POOL_WINDOWS)
    pg = C // ng
    assert pg == LANES

    def body(z_ref, w_ref, s_ref, o_ref):
        g = pl.program_id(0)
        p = _pooled(z_ref[...], g)
        y = jnp.dot(p.astype(BF16), w_ref[...].astype(BF16), preferred_element_type=F32)
        o_ref[...] = (y * s_ref[...]).astype(BF16)

    return pl.pallas_call(
        body, name=name, out_shape=jax.ShapeDtypeStruct((L, C), BF16), grid=(ng,),
        in_specs=[pl.BlockSpec((None, L, pg), lambda g: (0, 0, g)), pl.BlockSpec((None, pg, pg), lambda g: (g, 0, 0)),
                  pl.BlockSpec((1, pg), lambda g: (0, g))],
        out_specs=pl.BlockSpec((L, pg), lambda g: (0, g)), compiler_params=_cparams(("parallel",)),
    )(proj3, pool_w, scale.reshape(1, C))


def _pool_bwd(proj3, dmix, pool_w, scale, name):
    _, L, C = proj3.shape
    ng = len(POOL_WINDOWS)
    pg = C // ng

    def body(z_ref, d_ref, w_ref, s_ref, dz_ref, dw_ref, ds_ref):
        g = pl.program_id(0)
        p = _pooled(z_ref[...], g)
        pb = p.astype(BF16)
        wb = w_ref[...].astype(BF16)
        pre = jnp.dot(pb, wb, preferred_element_type=F32)
        dyc = d_ref[...]
        ds_ref[...] = jnp.sum(dyc * pre, axis=0, keepdims=True)
        dpre = (dyc * s_ref[...]).astype(BF16)
        dw_ref[...] = lax.dot_general(pb, dpre, (((0,), (0,)), ((), ())), preferred_element_type=F32)
        dp = lax.dot_general(dpre, wb, (((1,), (1,)), ((), ())), preferred_element_type=F32)
        v = dp * _pool_counts(L, g)
        a2 = v + _up(v, 1)
        a4 = a2 + _up(a2, 2)
        a8 = a4 + _up(a4, 4)
        a16 = a8 + _up(a8, 8)
        dz_ref[...] = (_select_window(g, a2, a4, a8, a16) - dp).astype(BF16)

    return pl.pallas_call(
        body, name=name,
        out_shape=(jax.ShapeDtypeStruct((L, C), BF16), jax.ShapeDtypeStruct((ng, pg, pg), F32),
                   jax.ShapeDtypeStruct((1, C), F32)),
        grid=(ng,),
        in_specs=[pl.BlockSpec((None, L, pg), lambda g: (0, 0, g)), pl.BlockSpec((L, pg), lambda g: (0, g)),
                  pl.BlockSpec((None, pg, pg), lambda g: (g, 0, 0)), pl.BlockSpec((1, pg), lambda g: (0, g))],
        out_specs=(pl.BlockSpec((L, pg), lambda g: (0, g)), pl.BlockSpec((None, pg, pg), lambda g: (g, 0, 0)),
                   pl.BlockSpec((1, pg), lambda g: (0, g))),
        compiler_params=_cparams(("parallel",)),
    )(proj3, dmix, pool_w, scale.reshape(1, C))


def _tril_w(w_ref, h):
    r = lax.broadcasted_iota(jnp.int32, (CHUNK, CHUNK), 0)
    c = lax.broadcasted_iota(jnp.int32, (CHUNK, CHUNK), 1)
    return jnp.where(r >= c, w_ref[h], 0.0)


def _sgu_fwd(proj3, norm_g, w, b, name):
    _, L, C = proj3.shape
    nh = w.shape[0]
    dh = C // nh
    assert dh == LANES and w.shape[1] == CHUNK
    tr = _pick(L, prefs=(512, 256, 128))
    bfull = jnp.broadcast_to(b[:, :, None], (nh, CHUNK, dh))

    def body(su_ref, sv_ref, g_ref, w_ref, b_ref, o_ref):
        sv = _gelu(sv_ref[...])
        r = lax.rsqrt(jnp.mean(sv * sv, axis=-1, keepdims=True) + EPS)
        v = (sv * r * g_ref[...]).astype(BF16)
        for h in range(nh):
            wm = _tril_w(w_ref, h).astype(BF16)
            cols = slice(h * dh, (h + 1) * dh)
            for n in range(tr // CHUNK):
                rows = slice(n * CHUNK, (n + 1) * CHUNK)
                mixed = jnp.dot(wm, v[rows, cols], preferred_element_type=F32) + b_ref[h]
                o_ref[rows, cols] = (_gelu(su_ref[rows, cols]) * mixed).astype(BF16)

    full = lambda shp: pl.BlockSpec(shp, lambda i: (0,) * len(shp))
    return pl.pallas_call(
        body, name=name, out_shape=jax.ShapeDtypeStruct((L, C), BF16), grid=(L // tr,),
        in_specs=[pl.BlockSpec((None, tr, C), lambda i: (1, i, 0)), pl.BlockSpec((None, tr, C), lambda i: (2, i, 0)),
                  full((1, C)), full((nh, CHUNK, CHUNK)), full((nh, CHUNK, dh))],
        out_specs=pl.BlockSpec((tr, C), lambda i: (i, 0)), compiler_params=_cparams(("parallel",)),
    )(proj3, proj3, norm_g.reshape(1, C), w, bfull)


def _sgu_bwd(proj3, dmix, norm_g, w, b, name):
    _, L, C = proj3.shape
    nh = w.shape[0]
    dh = C // nh
    tr = _pick(L, prefs=(512, 256, 128))
    nsteps = L // tr
    bfull = jnp.broadcast_to(b[:, :, None], (nh, CHUNK, dh))

    def body(su_ref, sv_ref, d_ref, g_ref, w_ref, b_ref, o_ref, dw_ref, db_ref, dg_ref, dv_ref, acc_g):
        i = pl.program_id(0)

        @pl.when(i == 0)
        def _():
            dw_ref[...] = jnp.zeros_like(dw_ref)
            db_ref[...] = jnp.zeros_like(db_ref)
            acc_g[...] = jnp.zeros_like(acc_g)

        svp = sv_ref[...]
        sv = _gelu(svp)
        r = lax.rsqrt(jnp.mean(sv * sv, axis=-1, keepdims=True) + EPS)
        vh = sv * r
        gv = g_ref[...]
        v = (vh * gv).astype(BF16)
        tri_r = lax.broadcasted_iota(jnp.int32, (CHUNK, CHUNK), 0)
        tri_c = lax.broadcasted_iota(jnp.int32, (CHUNK, CHUNK), 1)
        for h in range(nh):
            wm = _tril_w(w_ref, h).astype(BF16)
            cols = slice(h * dh, (h + 1) * dh)
            dwh = jnp.zeros((CHUNK, CHUNK), F32)
            dbh = jnp.zeros((CHUNK, dh), F32)
            for n in range(tr // CHUNK):
                rows = slice(n * CHUNK, (n + 1) * CHUNK)
                vb = v[rows, cols]
                mixed = jnp.dot(wm, vb, preferred_element_type=F32) + b_ref[h]
                sup = su_ref[rows, cols]
                dyd = d_ref[rows, cols]
                dmx = dyd * _gelu(sup)
                o_ref[0, rows, cols] = (dyd * mixed * _gelu_grad(sup)).astype(BF16)
                dmb = dmx.astype(BF16)
                dwh = dwh + lax.dot_general(dmb, vb, (((1,), (1,)), ((), ())), preferred_element_type=F32)
                dbh = dbh + dmx
                dv_ref[rows, cols] = lax.dot_general(wm, dmb, (((0,), (0,)), ((), ())), preferred_element_type=F32)
            dw_ref[h] += jnp.where(tri_r >= tri_c, dwh, 0.0)
            db_ref[h] += dbh
        dv = dv_ref[...]
        acc_g[...] += jnp.sum((dv * vh).reshape(tr // SUBLANES, SUBLANES, C), axis=0)
        dvg = dv * gv
        dsv = r * (dvg - vh * jnp.mean(dvg * vh, axis=-1, keepdims=True))
        o_ref[1] = (dsv * _gelu_grad(svp)).astype(BF16)

        @pl.when(i == nsteps - 1)
        def _():
            dg_ref[...] = jnp.sum(acc_g[...], axis=0, keepdims=True)

    full = lambda shp: pl.BlockSpec(shp, lambda i: (0,) * len(shp))
    return pl.pallas_call(
        body, name=name,
        out_shape=(jax.ShapeDtypeStruct((2, L, C), BF16), jax.ShapeDtypeStruct((nh, CHUNK, CHUNK), F32),
                   jax.ShapeDtypeStruct((nh, CHUNK, dh), F32), jax.ShapeDtypeStruct((1, C), F32)),
        grid=(nsteps,),
        in_specs=[pl.BlockSpec((None, tr, C), lambda i: (1, i, 0)), pl.BlockSpec((None, tr, C), lambda i: (2, i, 0)),
                  pl.BlockSpec((tr, C), lambda i: (i, 1)), full((1, C)), full((nh, CHUNK, CHUNK)),
                  full((nh, CHUNK, dh))],
        out_specs=(pl.BlockSpec((2, tr, C), lambda i: (0, i, 0)), full((nh, CHUNK, CHUNK)), full((nh, CHUNK, dh)),
                   full((1, C))),
        scratch_shapes=[pltpu.VMEM((tr, C), F32), pltpu.VMEM((SUBLANES, C), F32)],
        compiler_params=_cparams(("arbitrary",)),
    )(proj3, proj3, dmix, norm_g.reshape(1, C), w, bfull)


def _ffn_act_fwd(up3, conv_w, conv_b, name):
    _, L, Fh = up3.shape
    cb = LANES
    w2 = conv_w.reshape(3, 2, Fh).transpose(1, 0, 2)
    b2 = conv_b.reshape(2, 1, Fh)

    def body(u_ref, w_ref, b_ref, o_ref, ot_ref):
        g = _conv3(u_ref[0].astype(F32), w_ref[0]) + b_ref[0]
        v = _conv3(u_ref[1].astype(F32), w_ref[1]) + b_ref[1]
        a = g * _sigmoid(g) * v
        o_ref[...] = a.astype(BF16)
        ot_ref[...] = a.T.astype(BF16)

    return pl.pallas_call(
        body, name=name, out_shape=(jax.ShapeDtypeStruct((L, Fh), BF16), jax.ShapeDtypeStruct((Fh, L), BF16)),
        grid=(Fh // cb,),
        in_specs=[pl.BlockSpec((2, L, cb), lambda j: (0, 0, j)), pl.BlockSpec((2, 3, cb), lambda j: (0, 0, j)),
                  pl.BlockSpec((2, 1, cb), lambda j: (0, 0, j))],
        out_specs=(pl.BlockSpec((L, cb), lambda j: (0, j)), pl.BlockSpec((cb, L), lambda j: (j, 0))),
        compiler_params=_cparams(("parallel",)),
    )(up3, w2, b2)


def _ffn_act_bwd(up3, da, conv_w, conv_b, name):
    _, L, Fh = up3.shape
    cb = LANES
    w2 = conv_w.reshape(3, 2, Fh).transpose(1, 0, 2)
    b2 = conv_b.reshape(2, 1, Fh)

    def body(u_ref, d_ref, w_ref, b_ref, o_ref, dw_ref, db_ref):
        ug, uv = u_ref[0].astype(F32), u_ref[1].astype(F32)
        wg, wv = w_ref[0], w_ref[1]
        g = _conv3(ug, wg) + b_ref[0]
        v = _conv3(uv, wv) + b_ref[1]
        sg = _sigmoid(g)
        dav = d_ref[...].astype(F32)
        dg = dav * v * (sg * (1.0 + g * (1.0 - sg)))
        dv = dav * (g * sg)
        o_ref[0] = _conv3_t(dg, wg).astype(BF16)
        o_ref[1] = _conv3_t(dv, wv).astype(BF16)
        for tap, (dwg, dwv) in enumerate(zip(_conv3_dw(dg, ug), _conv3_dw(dv, uv))):
            dw_ref[0, tap:tap + 1, :] = dwg
            dw_ref[1, tap:tap + 1, :] = dwv
        db_ref[0] = jnp.sum(dg, axis=0, keepdims=True)
        db_ref[1] = jnp.sum(dv, axis=0, keepdims=True)

    dup, dw2, db2 = pl.pallas_call(
        body, name=name,
        out_shape=(jax.ShapeDtypeStruct((2, L, Fh), BF16), jax.ShapeDtypeStruct((2, 3, Fh), F32),
                   jax.ShapeDtypeStruct((2, 1, Fh), F32)),
        grid=(Fh // cb,),
        in_specs=[pl.BlockSpec((2, L, cb), lambda j: (0, 0, j)), pl.BlockSpec((L, cb), lambda j: (0, j)),
                  pl.BlockSpec((2, 3, cb), lambda j: (0, 0, j)), pl.BlockSpec((2, 1, cb), lambda j: (0, 0, j))],
        out_specs=(pl.BlockSpec((2, L, cb), lambda j: (0, 0, j)), pl.BlockSpec((2, 3, cb), lambda j: (0, 0, j)),
                   pl.BlockSpec((2, 1, cb), lambda j: (0, 0, j))),
        compiler_params=_cparams(("parallel",)),
    )(up3, da, w2, b2)
    return dup, dw2.transpose(1, 0, 2).reshape(3, 2 * Fh), db2.reshape(2 * Fh)


def _local_step(x, tgt, w, layer_weights, on_layer_grads):
    L, D = x.shape
    depth = w['norm_mix_g'].shape[0]
    saved = []
    for i in range(depth):
        j = i // 2
        wb = dict(layer_weights(2 * i, x))
        s = {'x': x, 'wb': wb}
        if i % 2 == 0:
            proj4, s['hT'] = _norm_mm(x, w['norm_mix_g'][i], wb['even_w_in'], F32, "even_in_fwd", ok=('seg', 4))
            s['proj'] = proj4
            ya = _sconv_fwd(proj4, w['even_conv_w'][j], "sconv_fwd")
            prm = (w['ssm_log_step'][j], w['ssm_a_re'][j], w['ssm_a_im'][j], w['ssm_b_re'][j], w['ssm_b_im'][j],
                   w['ssm_c_re'][j], w['ssm_c_im'][j])
            (lr, li, bmat, cmat), prep_vjp = jax.vjp(_s5_prep, *prm)
            u = _to_scan_order(proj4[3])
            yraw, s_re, s_im = _s5_fwd(u, lr, li, bmat, cmat, w['ssm_d'][j], "s5_fwd")
            yb = _glu_fwd(yraw, wb['ssm_glu_w'], w['ssm_glu_b'][j], "glu_fwd")
            s.update(u=u, yraw=yraw, s_re=s_re, s_im=s_im, s5=(lr, li, bmat, cmat), prep_vjp=prep_vjp)
            mixin = jnp.concatenate([ya, _from_scan_order(yb)], axis=1)
            x = _mm(mixin, wb['even_w_out'], 'nn', F32, "even_out_fwd", res=x)
        else:
            proj3, s['hT'] = _norm_mm(x, w['norm_mix_g'][i], wb['odd_w_in'], F32, "odd_in_fwd", ok=('seg', 3))
            s['proj'] = proj3
            yc = _pool_fwd(proj3, w['pool_w'][j], w['pool_scale'][j], "pool_fwd")
            yd = _sgu_fwd(proj3, w['sgu_norm_g'][j], w['sgu_w'][j], w['sgu_b'][j], "sgu_fwd")
            mixin = jnp.concatenate([yc, yd], axis=1)
            x = _mm(mixin, wb['odd_w_out'], 'nn', F32, "odd_out_fwd", res=x)
        s['mixin'] = mixin
        s['x1'] = x
        wb.update(layer_weights(2 * i + 1, x))
        up3, h2t = _norm_mm(x, w['norm_ffn_g'][i], wb['ffn_w_up'], BF16, "ffn_up_fwd", ok=('seg', 2))
        a, at = _ffn_act_fwd(up3, w['ffn_conv_w'][i], w['ffn_conv_b'][i], "ffn_act_fwd")
        x = _mm(a, wb['ffn_w_down'], 'nn', F32, "ffn_down_fwd", res=x)
        s.update(h2T=h2t, up3=up3, aT=at)
        saved.append(s)

    loss8, dx, dxb, dg_final = _loss_head(x, w['norm_final_g'], tgt)
    gs = {n: [None] * w[n].shape[0] for n in SMALL if n != 'norm_final_g'}
    gs['norm_final_g'] = dg_final.reshape(D)

    dep = None
    for i in reversed(range(depth)):
        j = i // 2
        s = saved[i]
        wb = s['wb']
        gb = {}
        da = _mm(dxb, wb['ffn_w_down'], 'nt', BF16, "ffn_down_dgrad", dep=dep)
        gb['ffn_w_down'] = _mm(s['aT'], dxb, 'nn', BF16, "ffn_down_wgrad")
        dup3, dcw, dcb = _ffn_act_bwd(s['up3'], da, w['ffn_conv_w'][i], w['ffn_conv_b'][i], "ffn_act_bwd")
        gs['ffn_conv_w'][i], gs['ffn_conv_b'][i] = dcw, dcb
        gb['ffn_w_up'] = _mm(s['h2T'], dup3, 'nn', BF16, "ffn_up_wgrad", bk=('seg', 2))
        dep = on_layer_grads(2 * i + 1, gb)
        dx, dxb, dg = _mm_norm_bwd(dup3, wb['ffn_w_up'], s['x1'], w['norm_ffn_g'][i], dx, "ffn_up_dgrad",
                              ak=('seg', 2), dep=dep)
        gs['norm_ffn_g'][i] = dg.reshape(D)
        gb = {}
        if i % 2 == 0:
            dmix = _mm(dxb, wb['even_w_out'], 'nt', F32, "even_out_dgrad")
            gb['even_w_out'] = _mm(s['mixin'].T, dxb, 'nn', BF16, "even_out_wgrad")
            dpc, dcw = _sconv_bwd(s['proj'], dmix, w['even_conv_w'][j], "sconv_bwd")
            gs['even_conv_w'][j] = dcw
            dyb = _to_scan_order(dmix[:, D // 2:])
            dyraw, dglu_w, dglu_b = _glu_bwd(s['yraw'], dyb, wb['ssm_glu_w'], w['ssm_glu_b'][j], "glu_bwd")
            gb['ssm_glu_w'] = dglu_w.astype(BF16)
            gs['ssm_glu_b'][j] = dglu_b.reshape(-1)
            lr, li, bmat, cmat = s['s5']
            du, dbm, dcm, dlam, dd = _s5_bwd(dyraw, s['u'], s['s_re'], s['s_im'], lr, li, bmat, cmat,
                                            w['ssm_d'][j], "s5_bwd")
            gs['ssm_d'][j] = dd.reshape(-1)
            dcm = jnp.swapaxes(dcm, 1, 2)
            dprm = s['prep_vjp']((dlam[:, 0:1, :], dlam[:, 1:2, :], dbm, dcm))
            for n, gval in zip(('ssm_log_step', 'ssm_a_re', 'ssm_a_im', 'ssm_b_re', 'ssm_b_im', 'ssm_c_re',
                                'ssm_c_im'), dprm):
                gs[n][j] = gval
            dproj = jnp.concatenate([dpc, _from_scan_order(du).astype(BF16)[None]], axis=0)
            gb['even_w_in'] = _mm(s['hT'], dproj, 'nn', BF16, "even_in_wgrad", bk=('seg', 4))
            w_in, in_kind, in_name = wb['even_w_in'], ('seg', 4), "even_in_dgrad"
        else:
            dmix = _mm(dxb, wb['odd_w_out'], 'nt', F32, "odd_out_dgrad")
            gb['odd_w_out'] = _mm(s['mixin'].T, dxb, 'nn', BF16, "odd_out_wgrad")
            dz, dpw, dps = _pool_bwd(s['proj'], dmix, w['pool_w'][j], w['pool_scale'][j], "pool_bwd")
            gs['pool_w'][j], gs['pool_scale'][j] = dpw, dps.reshape(-1)
            dsuv, dsw, dsb, dsg = _sgu_bwd(s['proj'], dmix, w['sgu_norm_g'][j], w['sgu_w'][j], w['sgu_b'][j],
                                           "sgu_bwd")
            gs['sgu_w'][j], gs['sgu_b'][j], gs['sgu_norm_g'][j] = dsw, jnp.sum(dsb, axis=-1), dsg.reshape(-1)
            dproj = jnp.concatenate([dz[None], dsuv], axis=0)
            gb['odd_w_in'] = _mm(s['hT'], dproj, 'nn', BF16, "odd_in_wgrad", bk=('seg', 3))
            w_in, in_kind, in_name = wb['odd_w_in'], ('seg', 3), "odd_in_dgrad"
        dep = on_layer_grads(2 * i, gb)
        dx, dxb, dg = _mm_norm_bwd(dproj, w_in, s['x'], w['norm_mix_g'][i], dx, in_name, ak=in_kind, dep=dep)
        gs['norm_mix_g'][i] = dg.reshape(D)

    gsmall = {n: (v if n == 'norm_final_g' else jnp.stack(v)) for n, v in gs.items()}
    return loss8[0, 0], dx, gsmall


_HBM = pl.BlockSpec(memory_space=pltpu.HBM)
_CHIP_FLIPS = ((0, 0), (1, 0), (0, 1), (1, 1))


def _coords():
    return lax.axis_index("x"), lax.axis_index("y"), lax.axis_index("c")


def _flip(v, f):
    return 1 - v if f else v


def _shard_of(ref, axis, s, width):
    start = pl.multiple_of(s * width, LANES if axis == ref.ndim - 1 else 16) if width % 16 == 0 else s * width
    idx = [slice(None)] * ref.ndim
    idx[axis] = pl.ds(start, width)
    return ref.at[tuple(idx)]


_SEM = pl.BlockSpec(memory_space=pltpu.SEMAPHORE)
_ANY = pl.BlockSpec(memory_space=pl.ANY)
_DATAFLOW = pltpu.SideEffectType.DATAFLOW_SIDE_EFFECTING


def _in_hbm(a):
    return pltpu.with_memory_space_constraint(a, pltpu.HBM)


def _model_layer(name, l):
    if name.startswith('ffn'):
        return l
    return 2 * l + 1 if name.startswith('odd') else 2 * l


def _place_quarter(shard, l, axis, chip, dtype, dep=None):
    _, r, c = shard.shape
    tr = _pick(r, prefs=(512, 256, 128, 64, 32, 16))
    nrb = r // tr

    def body(chip_ref, i_ref, *rest):
        rest[-1][...] = i_ref[...].astype(dtype)

    if axis == 1:
        out_shape, o_map = (r, c * N_CHIPS), (lambda i, s: (i, s[0]))
    else:
        out_shape, o_map = (r * N_CHIPS, c), (lambda i, s: (s[0] * nrb + i, 0))
    in_specs = [pl.BlockSpec((None, tr, c), lambda i, s: (l, i, 0))]
    args = [chip, shard]
    if dep is not None:
        in_specs.append(pl.BlockSpec(memory_space=pl.ANY))
        args.append(dep)
    return pl.pallas_call(
        body, name="place_quarter", out_shape=jax.ShapeDtypeStruct(out_shape, dtype),
        grid_spec=pltpu.PrefetchScalarGridSpec(
            num_scalar_prefetch=1, grid=(nrb,), in_specs=in_specs, out_specs=pl.BlockSpec((tr, c), o_map)),
        compiler_params=_cparams(("parallel",)),
    )(*args)


def _gather_copies(land_refs, send_sem, recv_sem, axes, landing_chip_of):
    x, y, c = _coords()
    out = []
    for j, land in enumerate(land_refs):
        width = land.shape[axes[j]] // N_CHIPS
        for f in (1, 2, 3):
            fx, fy = _CHIP_FLIPS[f]
            px, py = _flip(x, fx), _flip(y, fy)
            lx, ly = landing_chip_of(px, py)
            out.append(pltpu.make_async_remote_copy(
                src_ref=_shard_of(land, axes[j], 2 * x + y, width), dst_ref=_shard_of(land, axes[j], 2 * lx + ly, width),
                send_sem=send_sem.at[3 * j + f - 1], recv_sem=recv_sem.at[3 * j + f - 1],
                device_id=(px, py, c), device_id_type=MESH))
    return out


def _gather_start(tag, lands, axes, dep=None):
    n = len(lands)

    def body(*refs):
        land_refs, send_sem, recv_sem = refs[:n], refs[-3], refs[-2]
        x, y, _ = _coords()
        for cp in _gather_copies(land_refs, send_sem, recv_sem, axes, lambda px, py: (x, y)):
            cp.start()
        refs[-1][...] = jnp.zeros_like(refs[-1])

    thru = [pltpu.HBM(a.shape, a.dtype) for a in lands]
    outs = pl.pallas_call(
        body, name=f"gather_start_{tag}",
        out_shape=tuple(thru + [pltpu.SemaphoreType.DMA((3 * n,)), pltpu.SemaphoreType.DMA((3 * n,)),
                                jax.ShapeDtypeStruct((SUBLANES, LANES), F32)]),
        in_specs=[_HBM] * n + ([_ANY] if dep is not None else []),
        out_specs=tuple([_HBM] * n + [_SEM, _SEM, pl.BlockSpec(memory_space=pltpu.VMEM)]),
        input_output_aliases={i: i for i in range(n)},
        compiler_params=pltpu.CompilerParams(has_side_effects=_DATAFLOW),
    )(*[_in_hbm(a) for a in lands], *([dep] if dep is not None else []))
    return list(outs[:n]), outs[n], outs[n + 1], outs[n + 2]


def _gather_wait(tag, lands, send_sem, recv_sem, axes, after):
    n = len(lands)

    def body(*refs):
        for cp in _gather_copies(refs[:n], refs[n], refs[n + 1], axes, lambda px, py: (px, py)):
            cp.wait_send()
            cp.wait_recv()

    outs = pl.pallas_call(
        body, name=f"gather_wait_{tag}", out_shape=tuple(pltpu.HBM(a.shape, a.dtype) for a in lands),
        in_specs=[_HBM] * n + [_SEM, _SEM, _ANY], out_specs=tuple([_HBM] * n),
        input_output_aliases={i: i for i in range(n)},
        compiler_params=pltpu.CompilerParams(has_side_effects=_DATAFLOW),
    )(*lands, send_sem, recv_sem, after)
    return list(outs)


N_SLOTS = N_DEV - 1


def _scatter_sends(grad_refs, land_refs, send_sem, recv_sem, meta):
    x, y, c = _coords()
    out = []
    for j, (axis, owner, q, width) in enumerate(meta):
        other = c if owner == 0 else 1 - c
        for f, (fx, fy) in enumerate(_CHIP_FLIPS):
            px, py = _flip(x, fx), _flip(y, fy)
            slot = f + 4 * other - 1
            out.append((other if f == 0 else None, pltpu.make_async_remote_copy(
                src_ref=_shard_of(grad_refs[j], axis, 2 * px + py, width), dst_ref=land_refs[j].at[q, slot],
                send_sem=send_sem.at[4 * j + f], recv_sem=recv_sem.at[N_SLOTS * j + slot],
                device_id=(px, py, owner), device_id_type=MESH)))
    return out


def _scatter_start(layer, grads, lands, meta):
    n = len(grads)
    uniq = []
    for a in lands:
        if not any(a is u for u in uniq):
            uniq.append(a)
    which = [next(k for k, u in enumerate(uniq) if u is a) for a in lands]
    nu = len(uniq)

    def body(*refs):
        grad_refs, land_u = refs[:n], refs[n:n + nu]
        send_sem, recv_sem = refs[n + nu], refs[n + nu + 1]
        for other, cp in _scatter_sends(grad_refs, [land_u[k] for k in which], send_sem, recv_sem, meta):
            if other is None:
                cp.start()
            else:
                pl.when(other == 1)(cp.start)
        refs[-1][...] = jnp.zeros_like(refs[-1])

    thru = [pltpu.HBM(a.shape, a.dtype) for a in list(grads) + uniq]
    outs = pl.pallas_call(
        body, name=f"scatter_start_{layer}",
        out_shape=tuple([pltpu.SemaphoreType.DMA((4 * n,)), pltpu.SemaphoreType.DMA((N_SLOTS * n,))] + thru
                        + [jax.ShapeDtypeStruct((SUBLANES, LANES), F32)]),
        in_specs=[_HBM] * (n + nu),
        out_specs=tuple([_SEM, _SEM] + [_HBM] * (n + nu) + [pl.BlockSpec(memory_space=pltpu.VMEM)]),
        input_output_aliases={i: 2 + i for i in range(n + nu)},
        compiler_params=pltpu.CompilerParams(has_side_effects=_DATAFLOW),
    )(*[_in_hbm(a) for a in list(grads) + uniq])
    new_lands = [outs[2 + n + k] for k in which]
    return outs[0], outs[1], list(outs[2:2 + n]), new_lands, outs[-1]


def _scatter_wait(started, lands):
    nl = len(lands)
    flat_grads = [g for s in started for g in s[2]]
    ng, ns = len(flat_grads), len(started)

    def body(*refs):
        land_refs = refs[:nl]
        grad_refs = refs[nl:nl + ng]
        sem_refs = refs[nl + ng:nl + ng + 2 * ns]
        _, _, c = _coords()
        off = 0
        for k, (_, _, grads, idx, meta) in enumerate(started):
            send_sem, recv_sem = sem_refs[2 * k], sem_refs[2 * k + 1]
            lr = [land_refs[i] for i in idx]
            for other, cp in _scatter_sends(grad_refs[off:off + len(grads)], lr, send_sem, recv_sem, meta):
                if other is None:
                    cp.wait_send()
                else:
                    pl.when(other == 1)(cp.wait_send)
            for j, (axis, owner, q, width) in enumerate(meta):
                mine = (c if owner == 0 else 1 - c) == 0

                @pl.when(mine)
                def _():
                    for slot in range(N_SLOTS):
                        land = lr[j].at[q, slot]
                        pltpu.make_async_remote_copy(
                            src_ref=land, dst_ref=land, send_sem=send_sem.at[0], recv_sem=recv_sem.at[N_SLOTS * j + slot],
                            device_id=_coords(), device_id_type=MESH).wait_recv()
            off += len(grads)

    args = list(lands) + flat_grads
    thru = [pltpu.HBM(a.shape, a.dtype) for a in args]
    sems = [s for st in started for s in st[:2]]
    outs = pl.pallas_call(
        body, name="scatter_wait", out_shape=tuple(thru), in_specs=[_HBM] * (nl + ng) + [_SEM] * (2 * ns),
        out_specs=tuple([_HBM] * (nl + ng)), input_output_aliases={i: i for i in range(nl + ng)},
        compiler_params=pltpu.CompilerParams(has_side_effects=_DATAFLOW),
    )(*args, *sems)
    return list(outs[:nl]), list(outs[nl:])


def _sum_and_share(recv, layer_grads, axis, chip, name):
    n, ns, r, c = recv.shape
    tr = _pick(r, prefs=(256, 128, 64, 32, 16))
    nr = r // tr
    nsteps = n * nr
    nlay = len(layer_grads)
    own_map = (lambda h, i, s: (i, s[0])) if axis == 1 else (lambda h, i, s: (s[0] * nr + i, 0))

    def body(chip_ref, i_ref, *rest):
        g_refs = rest[:nlay]
        o_ref, buf, loc_sems, send_sems, recv_sems = rest[nlay:]
        h, i = pl.program_id(0), pl.program_id(1)
        step = h * nr + i
        slot = step % 2
        x, y, core = _coords()
        layer = core * n + h
        own = g_refs[0][...]
        for l in range(1, nlay):
            own = jnp.where(layer == l, g_refs[l][...], own)

        def copies(sl):
            dst = o_ref.at[core * n + h, pl.ds(pl.multiple_of(i * tr, tr), tr), :]
            loc = pltpu.make_async_copy(buf.at[sl], dst, loc_sems.at[sl])
            rem = pltpu.make_async_remote_copy(
                src_ref=buf.at[sl], dst_ref=dst, send_sem=send_sems.at[sl], recv_sem=recv_sems.at[step],
                device_id=(x, y, 1 - core), device_id_type=MESH)
            return loc, rem

        def drain(sl):
            loc, rem = copies(sl)
            loc.wait()
            rem.wait_send()

        pl.when(step >= 2)(lambda: drain(slot))
        acc = own.astype(F32)
        for s in range(ns):
            acc = acc + i_ref[s].astype(F32)
        buf[slot] = acc
        loc, rem = copies(slot)
        loc.start()
        rem.start()

        @pl.when(step == nsteps - 1)
        def _():
            drain(slot)
            if nsteps > 1:
                drain(1 - slot)
            for hh in range(n):
                for ii in range(nr):
                    land = o_ref.at[(1 - core) * n + hh, pl.ds(ii * tr, tr), :]
                    pltpu.make_async_remote_copy(
                        src_ref=buf.at[0], dst_ref=land, send_sem=send_sems.at[0], recv_sem=recv_sems.at[hh * nr + ii],
                        device_id=(x, y, 1 - core), device_id_type=MESH).wait_recv()

    return pl.pallas_call(
        body, name=name, out_shape=jax.ShapeDtypeStruct((2 * n, r, c), F32),
        grid_spec=pltpu.PrefetchScalarGridSpec(
            num_scalar_prefetch=1, grid=(n, nr),
            in_specs=[pl.BlockSpec((None, ns, tr, c), lambda h, i, s: (h, 0, i, 0))]
            + [pl.BlockSpec((tr, c), own_map)] * nlay,
            out_specs=_HBM,
            scratch_shapes=[pltpu.VMEM((2, tr, c), F32), pltpu.SemaphoreType.DMA((2,)),
                            pltpu.SemaphoreType.DMA((2,)), pltpu.SemaphoreType.DMA((nsteps,))]),
        compiler_params=_cparams(("arbitrary", "arbitrary")),
    )(chip, recv, *layer_grads)


def _gather_sums_over_chips(part):
    def body(i_ref, o_ref, send_sems, recv_sems):
        x, y, c = _coords()
        o_ref[2 * x + y] = i_ref[...]

        def copy(f, slot_chip):
            fx, fy = _CHIP_FLIPS[f]
            return pltpu.make_async_remote_copy(
                src_ref=i_ref, dst_ref=o_ref.at[2 * slot_chip[0] + slot_chip[1]], send_sem=send_sems.at[f - 1],
                recv_sem=recv_sems.at[f - 1], device_id=(_flip(x, fx), _flip(y, fy), c), device_id_type=MESH)

        sends = [copy(f, (x, y)) for f in (1, 2, 3)]
        for cp in sends:
            cp.start()
        for f in (1, 2, 3):
            fx, fy = _CHIP_FLIPS[f]
            copy(f, (_flip(x, fx), _flip(y, fy))).wait_recv()
        for cp in sends:
            cp.wait_send()

    vmem = pl.BlockSpec(memory_space=pltpu.VMEM)
    return pl.pallas_call(
        body, name="gather_small_sums", out_shape=jax.ShapeDtypeStruct((N_CHIPS,) + part.shape, part.dtype),
        in_specs=[vmem], out_specs=vmem,
        scratch_shapes=[pltpu.SemaphoreType.DMA((3,)), pltpu.SemaphoreType.DMA((3,))],
    )(part)


def _adamw(w, g, m, v, name):
    bc1 = 1.0 - ADAM_B1 ** ADAM_STEP
    bc2 = 1.0 - ADAM_B2 ** ADAM_STEP

    def body(w_ref, g_ref, m_ref, v_ref, d_ref, mo_ref, vo_ref):
        gv = g_ref[...]
        mn = ADAM_B1 * m_ref[...] + (1.0 - ADAM_B1) * gv
        vn = ADAM_B2 * v_ref[...] + (1.0 - ADAM_B2) * (gv * gv)
        d_ref[...] = -ADAM_LR * ((mn / bc1) / (jnp.sqrt(vn / bc2) + ADAM_EPS) + ADAM_WD * w_ref[...])
        mo_ref[...] = mn
        vo_ref[...] = vn

    sds = jax.ShapeDtypeStruct(w.shape, F32)
    if w.ndim == 2 and w.shape[0] % SUBLANES == 0:
        tr = _pick(w.shape[0], prefs=(256, 128, 64, 32, 16, 8))
        grid, blk = (w.shape[0] // tr,), pl.BlockSpec((tr, w.shape[1]), lambda i: (i, 0))
    else:
        nd = w.ndim
        grid, blk = (1,), pl.BlockSpec(w.shape, lambda i: (0,) * nd)
    return pl.pallas_call(
        body, name=name, out_shape=(sds, sds, sds), grid=grid, in_specs=[blk] * 4, out_specs=(blk,) * 3,
        compiler_params=_cparams(("parallel",)),
    )(w, g, m, v)


_PACK_QUANTUM = 256 * LANES


def _pack(arrs):
    flat = jnp.concatenate([a.reshape(-1).astype(F32) for a in arrs])
    flat = jnp.pad(flat, (0, (-flat.shape[0]) % _PACK_QUANTUM))
    return flat.reshape(-1, LANES)


def _unpack(p, shapes):
    flat = p.reshape(-1)
    out, off = [], 0
    for s in shapes:
        n = int(np.prod(s))
        out.append(flat[off:off + n].reshape(s))
        off += n
    return out


def kernel(*args):
    nw = len(WEIGHTS)
    x, tgt = args[0], args[1 + nw]
    w = dict(zip(WEIGHTS, args[1:1 + nw]))
    m = dict(zip(WEIGHTS, args[2 + nw:2 + 2 * nw]))
    v = dict(zip(WEIGHTS, args[2 + 2 * nw:2 + 3 * nw]))
    _, L, D = x.shape
    chip = 2 * lax.axis_index("x") + lax.axis_index("y")

    big = list(BIG)
    small_sh_shapes = [w[n].shape for n in SMALL_SHARDED]
    nbig = len(big)
    chip1 = chip.reshape(1).astype(jnp.int32)
    axes2 = [BIG[n] - 1 for n in big] + [0]
    shards = [w[n] for n in big] + [_pack([w[n] for n in SMALL_SHARDED])[None]]
    pairs = [(t, l) for t in range(nbig + 1) for l in range(shards[t].shape[0])]
    depth = w['norm_mix_g'].shape[0]
    part_of = lambda t, l: 0 if t == nbig else 2 * _model_layer(big[t], l) + big[t].startswith('ffn')
    flying, token = [], None
    for g in range(2 * depth):
        ids = [k for k, (t, l) in enumerate(pairs) if part_of(t, l) == g]
        ts = [pairs[k][0] for k in ids]
        placed = [_place_quarter(shards[t], pairs[k][1], axes2[t], chip1, F32 if t == nbig else BF16, token)
                  for k, t in zip(ids, ts)]
        lands, send, recv, token = _gather_start(g, placed, [axes2[t] for t in ts], token)
        flying.append((ts, lands, send, recv))

    def wait_group(g, after):
        ts, lands, send, recv = flying[g]
        landed = _gather_wait(g, lands, send, recv, [axes2[t] for t in ts], token if after is None else after)
        return dict(zip(ts, landed))

    first = wait_group(0, None)
    packed = first.pop(nbig).reshape(N_CHIPS, -1, LANES)
    per_chip = [_unpack(packed[s], small_sh_shapes) for s in range(N_CHIPS)]
    wl = dict(w)
    for k, n in enumerate(SMALL_SHARDED):
        wl[n] = jnp.concatenate([per_chip[s][k] for s in range(N_CHIPS)], axis=-1)

    def layer_weights(i, after):
        got = first if i == 0 else wait_group(i, after)
        return {big[t]: a for t, a in got.items()}

    small_shapes = [(w[n].shape[:-1] + (w[n].shape[-1] * N_CHIPS,)) if n in SMALL_SHARDED else w[n].shape
                    for n in SMALL] + [(1,)]
    n_small = sum(int(np.prod(s)) for s in small_shapes)
    pack_rows = -(-n_small // _PACK_QUANTUM) * _PACK_QUANTUM // LANES
    nlayers = [w[n].shape[0] for n in big] + [2]
    halves = [n // 2 for n in nlayers]
    quarters = [tuple(w[n].shape[1:]) for n in big] + [(pack_rows // 2 // N_CHIPS, LANES)]
    wire = [BF16] * nbig + [F32]
    land_now = [lax.empty((halves[t], N_SLOTS) + quarters[t], wire[t]) for t in range(nbig + 1)]
    gparts = [[None] * n for n in nlayers]
    started = []

    def start_scatter(tag, ts, ls, arrays):
        meta = [(axes2[t], l // halves[t], l % halves[t], quarters[t][axes2[t]]) for t, l in zip(ts, ls)]
        send, recv, thru, new_lands, token = _scatter_start(tag, arrays, [land_now[t] for t in ts], meta)
        for t, ln in zip(ts, new_lands):
            land_now[t] = ln
        started.append((send, recv, thru, ts, meta, ls))
        return token

    def on_layer_grads(g, gb):
        ts = [big.index(n) for n in gb]
        return start_scatter(g, ts, [g // 2 if big[t].startswith('ffn') else g // 4 for t in ts],
                             [gb[big[t]] for t in ts])

    loss, dx, gsmall = _local_step(x.reshape(L, D), tgt.reshape(L, D), wl, layer_weights, on_layer_grads)
    gpack = _pack([gsmall[n] for n in SMALL] + [loss.reshape(1)])
    start_scatter(2 * depth, [nbig, nbig], [0, 1], [gpack[:pack_rows // 2], gpack[pack_rows // 2:]])
    landed, sent = _scatter_wait([s[:5] for s in started], land_now)
    for (t, l), g in zip([(t, l) for s in started for t, l in zip(s[3], s[5])], sent):
        gparts[t][l] = g
    gshard = {n: _sum_and_share(landed[t], gparts[t], axes2[t], chip1, "sum_share_" + n) for t, n in enumerate(big)}
    small_sum = _sum_and_share(landed[nbig], gparts[nbig], 0, chip1, "sum_share_small")
    gpack = _gather_sums_over_chips(small_sum).transpose(1, 0, 2, 3).reshape(pack_rows, LANES)
    gs = dict(zip(SMALL + ['loss'], _unpack(gpack, small_shapes)))
    loss = gs.pop('loss').reshape(())
    for n in SMALL_SHARDED:
        width = w[n].shape[-1]
        gs[n] = lax.dynamic_slice_in_dim(gs[n], chip * width, width, axis=gs[n].ndim - 1)

    grads, delta, new_m, new_v = {}, {}, {}, {}
    for n in big:
        shp = w[n].shape
        flat = lambda a: a.reshape(shp[0] * shp[1], shp[2])
        g = gshard[n]
        grads[n] = g
        d_, m_, v_ = _adamw(flat(w[n]), flat(g), flat(m[n]), flat(v[n]), "adamw_" + n)
        delta[n], new_m[n], new_v[n] = d_.reshape(shp), m_.reshape(shp), v_.reshape(shp)
    for n in SMALL:
        shp = w[n].shape
        as2d = (lambda a: a.reshape(1, -1)) if len(shp) == 1 else (lambda a: a)
        d_, m_, v_ = _adamw(as2d(w[n]), as2d(gs[n]), as2d(m[n]), as2d(v[n]), "adamw_" + n)
        grads[n], delta[n], new_m[n], new_v[n] = gs[n], d_.reshape(shp), m_.reshape(shp), v_.reshape(shp)

    return (loss, dx.reshape(1, L, D), *[grads[n] for n in WEIGHTS], *[delta[n] for n in WEIGHTS],
            *[new_m[n] for n in WEIGHTS], *[new_v[n] for n in WEIGHTS])
```

```python
import functools
import math

import numpy as np
import jax
import jax.numpy as jnp
from jax import lax
from jax.experimental import pallas as pl
from jax.experimental.pallas import tpu as pltpu

F32 = jnp.float32
BF16 = jnp.bfloat16
MESH = pl.DeviceIdType.MESH

EPS = 1e-6
CHUNK = 128
POOL_WINDOWS = (2, 4, 8, 16)
LANES = 128
SUBLANES = 8
SCAN_CHUNKS = SUBLANES
S5_GROUPS_PER_STEP = 4
MM_TM_CAP, MM_TN_CAP, MM_TK_CAP = 1408, 1408, 2048
MM_TK_WHOLE = 2048
VMEM_LIMIT = 48 * 1024 * 1024
VMEM_LIMIT_S5 = 56 * 1024 * 1024

ADAM_LR, ADAM_B1, ADAM_B2, ADAM_EPS, ADAM_WD, ADAM_STEP = 0.001, 0.9, 0.999, 1e-08, 0.01, 10

WEIGHTS = ['norm_mix_g', 'even_w_in', 'even_conv_w', 'ssm_log_step', 'ssm_a_re', 'ssm_a_im', 'ssm_b_re',
           'ssm_b_im', 'ssm_c_re', 'ssm_c_im', 'ssm_d', 'ssm_glu_w', 'ssm_glu_b', 'even_w_out', 'odd_w_in',
           'pool_w', 'pool_scale', 'sgu_norm_g', 'sgu_w', 'sgu_b', 'odd_w_out', 'norm_ffn_g', 'ffn_w_up',
           'ffn_conv_w', 'ffn_conv_b', 'ffn_w_down', 'norm_final_g']
BIG = {'even_w_in': 2, 'ssm_glu_w': 1, 'even_w_out': 1, 'odd_w_in': 2, 'odd_w_out': 1, 'ffn_w_up': 2,
       'ffn_w_down': 1}
SMALL_SHARDED = ('even_conv_w', 'pool_scale', 'sgu_norm_g', 'ffn_conv_w')
SMALL = [n for n in WEIGHTS if n not in BIG]
N_CHIPS = 4
N_DEV = 8


def _cparams(sem=None, vmem=VMEM_LIMIT):
    kw = dict(vmem_limit_bytes=vmem)
    if sem is not None:
        kw['dimension_semantics'] = sem
    return pltpu.CompilerParams(**kw)


def _pick(n, segs=(), prefs=(1024, 512, 256, 128)):
    for t in prefs:
        if n % t == 0 and all(s % t == 0 for s in segs if s):
            return t
    return n


def _largest_tile(n, segs, cap):
    best = None
    for t in range(LANES, min(n, cap) + 1, LANES):
        if n % t == 0 and all(s % t == 0 for s in segs if s):
            best = t
    return best if best is not None else n


def _ldims(arr, kind):
    if kind is None:
        return arr.shape
    if kind[0] == 'lead':
        return arr.shape[1:]
    return (arr.shape[1], arr.shape[0] * arr.shape[2])


def _segw(arr, kind):
    return arr.shape[2] if (kind is not None and kind[0] == 'seg') else None


def _opspec(arr, kind, br, bc, rfn, cfn):
    if kind is None:
        return pl.BlockSpec((br, bc), lambda i, j, k: (rfn(i, j, k), cfn(i, j, k)))
    if kind[0] == 'lead':
        lead = kind[1]
        return pl.BlockSpec((None, br, bc), lambda i, j, k: (lead, rfn(i, j, k), cfn(i, j, k)))
    per = arr.shape[2] // bc
    return pl.BlockSpec((None, br, bc), lambda i, j, k: (cfn(i, j, k) // per, rfn(i, j, k), cfn(i, j, k) % per))


def _mm(a, b, mode, out_dtype, name, ak=None, bk=None, ok=None, res=None, dep=None):
    ar, ac = _ldims(a, ak)
    br_, bc_ = _ldims(b, bk)
    if mode == 'nn':
        M, K, N = ar, ac, bc_
        assert br_ == K
    else:
        M, K, N = ar, ac, br_
        assert bc_ == K
    sa, sb = _segw(a, ak), _segw(b, bk)
    so = (N // ok[1]) if ok is not None else None
    tm = _largest_tile(M, [], MM_TM_CAP)
    tn = _largest_tile(N, [sb if mode == 'nn' else None, so], MM_TN_CAP)
    ksegs = [sa, sb if mode == 'nt' else None]
    tk = K if (K <= MM_TK_WHOLE and not any(ksegs)) else _largest_tile(K, ksegs, MM_TK_CAP)
    nk = K // tk
    I = lambda i, j, k: i
    J = lambda i, j, k: j
    Kk = lambda i, j, k: k
    a_spec = _opspec(a, ak, tm, tk, I, Kk)
    if mode == 'nn':
        b_spec = _opspec(b, bk, tk, tn, Kk, J)
        dims = (((1,), (0,)), ((), ()))
    else:
        b_spec = _opspec(b, bk, tn, tk, J, Kk)
        dims = (((1,), (1,)), ((), ()))
    if ok is None:
        out_shape = jax.ShapeDtypeStruct((M, N), out_dtype)
        o_spec = pl.BlockSpec((tm, tn), lambda i, j, k: (i, j))
    else:
        out_shape = jax.ShapeDtypeStruct((ok[1], M, N // ok[1]), out_dtype)
        per = (N // ok[1]) // tn
        o_spec = pl.BlockSpec((None, tm, tn), lambda i, j, k: (j // per, i, j % per))
    has_res = res is not None

    def body(*refs):
        a_ref, b_ref = refs[0], refs[1]
        r_ref = refs[2] if has_res else None
        o_ref = refs[n_in]
        prod = lax.dot_general(a_ref[...].astype(BF16), b_ref[...].astype(BF16), dims, preferred_element_type=F32)
        if nk == 1:
            o_ref[...] = (prod + r_ref[...] if has_res else prod).astype(out_dtype)
            return
        acc = refs[-1]
        k = pl.program_id(2)

        @pl.when(k == 0)
        def _():
            acc[...] = prod

        @pl.when(k > 0)
        def _():
            acc[...] += prod

        @pl.when(k == nk - 1)
        def _():
            o = acc[...]
            if has_res:
                o = o + r_ref[...]
            o_ref[...] = o.astype(out_dtype)

    in_specs = [a_spec, b_spec]
    args = [a, b]
    if has_res:
        in_specs.append(pl.BlockSpec((tm, tn), lambda i, j, k: (i, j)))
        args.append(res)
    if dep is not None:
        in_specs.append(pl.BlockSpec(memory_space=pl.ANY))
        args.append(dep)
    n_in = len(args)
    return pl.pallas_call(
        body, name=name, out_shape=out_shape, grid=(M // tm, N // tn, nk), in_specs=in_specs, out_specs=o_spec,
        scratch_shapes=[pltpu.VMEM((tm, tn), F32)] if nk > 1 else [],
        compiler_params=_cparams(("parallel", "parallel", "arbitrary")),
    )(*args)


_G0 = math.sqrt(2.0 / math.pi)
_G1 = 0.044715


def _gelu(x):
    return 0.5 * x * (1.0 + jnp.tanh(_G0 * (x + _G1 * x * x * x)))


def _gelu_grad(x):
    x2 = x * x
    t = jnp.tanh(_G0 * (x + _G1 * x * x2))
    return 0.5 * (1.0 + t) + 0.5 * x * (1.0 - t * t) * (_G0 * (1.0 + 3.0 * _G1 * x2))


def _sigmoid(x):
    return 1.0 / (1.0 + jnp.exp(-x))


def _down(v, k):
    r = pltpu.roll(v, k, axis=0)
    row = lax.broadcasted_iota(jnp.int32, (SUBLANES, v.shape[1]), 0)
    return jnp.concatenate([jnp.where(row >= k, r[:SUBLANES], 0.0), r[SUBLANES:]], axis=0)


def _up(v, k):
    n = v.shape[0]
    r = pltpu.roll(v, n - k, axis=0)
    row = lax.broadcasted_iota(jnp.int32, (SUBLANES, v.shape[1]), 0)
    return jnp.concatenate([r[:n - SUBLANES], jnp.where(row < SUBLANES - k, r[n - SUBLANES:], 0.0)], axis=0)


def _taps(v):
    return _down(v, 2), _down(v, 1), v


def _conv3(taps, w):
    return w[0:1, :] * taps[0] + w[1:2, :] * taps[1] + w[2:3, :] * taps[2]


def _conv3_t(dv, w):
    return w[2:3, :] * dv + w[1:2, :] * _up(dv, 1) + w[0:1, :] * _up(dv, 2)


def _conv3_dw(dv, taps):
    return tuple(jnp.sum(dv * tp, axis=0, keepdims=True) for tp in taps)


def _cmul(ar, ai, br, bi):
    return ar * br - ai * bi, ar * bi + ai * br


def _cpow(lr, li, n):
    rr = ri = None
    br, bi = lr, li
    while n:
        if n & 1:
            rr, ri = (br, bi) if rr is None else _cmul(rr, ri, br, bi)
        n >>= 1
        if n:
            br, bi = _cmul(br, bi, br, bi)
    return rr, ri


NORM_ROWS = 256


def _norm_mm(x, g, b, out_dtype, name, ok=None):
    M, D = x.shape
    N = b.shape[1]
    so = (N // ok[1]) if ok is not None else None
    tm = _largest_tile(M, [], 1024)
    tn = _largest_tile(N, [so], MM_TN_CAP)
    if ok is None:
        out_shape = jax.ShapeDtypeStruct((M, N), out_dtype)
        o_spec = pl.BlockSpec((tm, tn), lambda i, j: (i, j))
    else:
        out_shape = jax.ShapeDtypeStruct((ok[1], M, N // ok[1]), out_dtype)
        per = (N // ok[1]) // tn
        o_spec = pl.BlockSpec((None, tm, tn), lambda i, j: (j // per, i, j % per))

    def body(x_ref, g_ref, b_ref, o_ref, ht_ref, h_scr):
        @pl.when(pl.program_id(1) == 0)
        def _():
            for c in range(tm // NORM_ROWS):
                rows = pl.ds(c * NORM_ROWS, NORM_ROWS)
                xv = x_ref[rows, :]
                h = xv * lax.rsqrt(jnp.mean(xv * xv, axis=-1, keepdims=True) + EPS) * g_ref[...]
                h_scr[rows, :] = h.astype(BF16)
                ht_ref[:, rows] = h.T.astype(BF16)

        o_ref[...] = jnp.dot(h_scr[...], b_ref[...], preferred_element_type=F32).astype(out_dtype)

    return pl.pallas_call(
        body, name=name, out_shape=(out_shape, jax.ShapeDtypeStruct((D, M), BF16)), grid=(M // tm, N // tn),
        in_specs=[pl.BlockSpec((tm, D), lambda i, j: (i, 0)), pl.BlockSpec((1, D), lambda i, j: (0, 0)),
                  pl.BlockSpec((D, tn), lambda i, j: (0, j))],
        out_specs=(o_spec, pl.BlockSpec((D, tm), lambda i, j: (0, i))),
        scratch_shapes=[pltpu.VMEM((tm, D), BF16)], compiler_params=_cparams(("parallel", "arbitrary")),
    )(x, g.reshape(1, D), b)


def _mm_norm_bwd(a, b, x, g, dres, name, ak=None, dep=None):
    M, K = _ldims(a, ak)
    D = b.shape[0]
    assert b.shape[1] == K and x.shape == (M, D)
    sa = _segw(a, ak)
    tm = _largest_tile(M, [], 1024)
    tk = K if (K <= MM_TK_WHOLE and not sa) else _largest_tile(K, [sa], MM_TK_CAP)
    ni, nk = M // tm, K // tk
    a3 = _opspec(a, ak, tm, tk, lambda i, j, k: i, lambda i, j, k: k)
    a_spec = pl.BlockSpec(a3.block_shape, lambda i, k: a3.index_map(i, 0, k))
    n_in = 5 + (dep is not None)

    def body(*refs):
        a_ref, b_ref, x_ref, g_ref, r_ref = refs[:5]
        dx_ref, dxb_ref, dg_ref, acc, accg = refs[n_in:]
        i, k = pl.program_id(0), pl.program_id(1)
        prod = lax.dot_general(a_ref[...].astype(BF16), b_ref[...], (((1,), (1,)), ((), ())),
                               preferred_element_type=F32)

        @pl.when(k == 0)
        def _():
            acc[...] = prod

        @pl.when(k > 0)
        def _():
            acc[...] += prod

        @pl.when((i == 0) & (k == 0))
        def _():
            accg[...] = jnp.zeros_like(accg)

        @pl.when(k == nk - 1)
        def _():
            for c in range(tm // NORM_ROWS):
                rows = pl.ds(c * NORM_ROWS, NORM_ROWS)
                xv = x_ref[rows, :]
                r = lax.rsqrt(jnp.mean(xv * xv, axis=-1, keepdims=True) + EPS)
                xh = xv * r
                dhv = acc[rows, :]
                accg[...] += jnp.sum((dhv * xh).reshape(NORM_ROWS // SUBLANES, SUBLANES, D), axis=0)
                dxh = dhv * g_ref[...]
                dxv = r_ref[rows, :] + r * (dxh - xh * jnp.mean(dxh * xh, axis=-1, keepdims=True))
                dx_ref[rows, :] = dxv
                dxb_ref[rows, :] = dxv.astype(BF16)

        @pl.when((i == ni - 1) & (k == nk - 1))
        def _():
            dg_ref[...] = jnp.sum(accg[...], axis=0, keepdims=True)

    row = pl.BlockSpec((tm, D), lambda i, k: (i, 0))
    vec = pl.BlockSpec((1, D), lambda i, k: (0, 0))
    in_specs = [a_spec, pl.BlockSpec((D, tk), lambda i, k: (0, k)), row, vec, row]
    args = [a, b, x, g.reshape(1, D), dres]
    if dep is not None:
        in_specs.append(pl.BlockSpec(memory_space=pl.ANY))
        args.append(dep)
    return pl.pallas_call(
        body, name=name,
        out_shape=(jax.ShapeDtypeStruct((M, D), F32), jax.ShapeDtypeStruct((M, D), BF16),
                   jax.ShapeDtypeStruct((1, D), F32)),
        grid=(ni, nk), in_specs=in_specs, out_specs=(row, row, vec),
        scratch_shapes=[pltpu.VMEM((tm, D), F32), pltpu.VMEM((SUBLANES, D), F32)],
        compiler_params=_cparams(("arbitrary", "arbitrary"), VMEM_LIMIT_S5),
    )(*args)


def _loss_head(x, g, tgt):
    L, D = x.shape
    tr = _pick(L, prefs=(512, 256, 128))
    nsteps = L // tr

    def body(x_ref, g_ref, t_ref, loss_ref, dx_ref, dxb_ref, dg_ref, acc_g, acc_l):
        i = pl.program_id(0)

        @pl.when(i == 0)
        def _():
            acc_g[...] = jnp.zeros_like(acc_g)
            acc_l[...] = jnp.zeros_like(acc_l)

        xv = x_ref[...]
        gv = g_ref[...]
        r = lax.rsqrt(jnp.mean(xv * xv, axis=-1, keepdims=True) + EPS)
        xh = xv * r
        e = xh * gv - t_ref[...]
        acc_l[...] += jnp.sum((e * e).reshape(tr // SUBLANES, SUBLANES, D), axis=0)
        dy = e * (1.0 / D)
        acc_g[...] += jnp.sum((dy * xh).reshape(tr // SUBLANES, SUBLANES, D), axis=0)
        dxh = dy * gv
        dxv = r * (dxh - xh * jnp.mean(dxh * xh, axis=-1, keepdims=True))
        dx_ref[...] = dxv
        dxb_ref[...] = dxv.astype(BF16)

        @pl.when(i == nsteps - 1)
        def _():
            dg_ref[...] = jnp.sum(acc_g[...], axis=0, keepdims=True)
            tot = jnp.sum(jnp.sum(acc_l[...], axis=0, keepdims=True), axis=1, keepdims=True) * (0.5 / D)
            loss_ref[...] = jnp.broadcast_to(tot, (SUBLANES, LANES))

    row = pl.BlockSpec((tr, D), lambda i: (i, 0))
    vec = pl.BlockSpec((1, D), lambda i: (0, 0))
    return pl.pallas_call(
        body, name="loss_head",
        out_shape=(jax.ShapeDtypeStruct((SUBLANES, LANES), F32), jax.ShapeDtypeStruct((L, D), F32),
                   jax.ShapeDtypeStruct((L, D), BF16), jax.ShapeDtypeStruct((1, D), F32)),
        grid=(nsteps,), in_specs=[row, vec, row],
        out_specs=(pl.BlockSpec((SUBLANES, LANES), lambda i: (0, 0)), row, row, vec),
        scratch_shapes=[pltpu.VMEM((SUBLANES, D), F32), pltpu.VMEM((SUBLANES, D), F32)],
        compiler_params=_cparams(("arbitrary",)),
    )(x, g.reshape(1, D), tgt)


def _sconv_fwd(proj4, conv_w, name):
    _, L, C = proj4.shape
    cb = LANES

    def body(p_ref, w_ref, o_ref):
        xa, ba, ca = p_ref[0].astype(F32), p_ref[1].astype(F32), p_ref[2].astype(F32)
        o_ref[...] = (ba * _conv3(_taps(ca * xa), w_ref[...])).astype(BF16)

    return pl.pallas_call(
        body, name=name, out_shape=jax.ShapeDtypeStruct((L, C), BF16), grid=(C // cb,),
        in_specs=[pl.BlockSpec((3, L, cb), lambda j: (0, 0, j)), pl.BlockSpec((3, cb), lambda j: (0, j))],
        out_specs=pl.BlockSpec((L, cb), lambda j: (0, j)), compiler_params=_cparams(("parallel",)),
    )(proj4, conv_w)


def _sconv_bwd(proj4, dmix, conv_w, name):
    _, L, C = proj4.shape
    cb = LANES

    def body(p_ref, d_ref, w_ref, o_ref, dw_ref):
        xa, ba, ca = p_ref[0].astype(F32), p_ref[1].astype(F32), p_ref[2].astype(F32)
        w = w_ref[...]
        dya = d_ref[...]
        tq = _taps(ca * xa)
        cq = _conv3(tq, w)
        dcq = dya * ba
        dq = _conv3_t(dcq, w)
        for tap, dwt in enumerate(_conv3_dw(dcq, tq)):
            dw_ref[tap:tap + 1, :] = dwt
        o_ref[0] = (dq * ca).astype(BF16)
        o_ref[1] = (dya * cq).astype(BF16)
        o_ref[2] = (dq * xa).astype(BF16)

    return pl.pallas_call(
        body, name=name,
        out_shape=(jax.ShapeDtypeStruct((3, L, C), BF16), jax.ShapeDtypeStruct((3, C), F32)), grid=(C // cb,),
        in_specs=[pl.BlockSpec((3, L, cb), lambda j: (0, 0, j)), pl.BlockSpec((L, cb), lambda j: (0, j)),
                  pl.BlockSpec((3, cb), lambda j: (0, j))],
        out_specs=(pl.BlockSpec((3, L, cb), lambda j: (0, 0, j)), pl.BlockSpec((3, cb), lambda j: (0, j))),
        compiler_params=_cparams(("parallel",)),
    )(proj4, dmix, conv_w)


def _to_scan_order(v):
    L, C = v.shape
    return v.reshape(SCAN_CHUNKS, L // SCAN_CHUNKS, C).transpose(1, 0, 2).reshape(L, C)


def _from_scan_order(v):
    L, C = v.shape
    return v.reshape(L // SCAN_CHUNKS, SCAN_CHUNKS, C).transpose(1, 0, 2).reshape(L, C)


def _s5_prep(log_step, a_re, a_im, b_re, b_im, c_re, c_im):
    G, P = a_re.shape
    H = b_re.shape[-1]
    gs = S5_GROUPS_PER_STEP
    ns = G // gs
    gu = LANES // H
    lam = lax.complex(a_re, a_im)
    step = jnp.exp(log_step)[:, None]
    lam_bar = jnp.exp(lam * step)
    b_bar = ((lam_bar - 1.0) / lam)[..., None] * lax.complex(b_re, b_im)
    lr = jnp.real(lam_bar).reshape(ns, 1, gs * P)
    li = jnp.imag(lam_bar).reshape(ns, 1, gs * P)
    k = np.arange(ns)[:, None, None]
    oh = jnp.asarray((np.arange(gu)[None, :, None] == gs * (k % (gu // gs)) + np.arange(gs)[None, None, :]),
                     F32)
    bre = jnp.einsum('kgl,klph->kghlp', oh, jnp.real(b_bar).reshape(ns, gs, P, H)).reshape(ns, gu * H, gs * P)
    bim = jnp.einsum('kgl,klph->kghlp', oh, jnp.imag(b_bar).reshape(ns, gs, P, H)).reshape(ns, gu * H, gs * P)
    cre = jnp.einsum('kgl,klhp->klpgh', oh, c_re.reshape(ns, gs, H, P)).reshape(ns, gs * P, gu * H)
    cim = jnp.einsum('kgl,klhp->klpgh', oh, c_im.reshape(ns, gs, H, P)).reshape(ns, gs * P, gu * H)
    return lr, li, jnp.concatenate([bre, bim], axis=2), jnp.concatenate([cre, -cim], axis=1)


def _carry_tile(fr, fi, pr, pi, reverse):
    row = lax.broadcasted_iota(jnp.int32, fr.shape, 0)
    cr = jnp.zeros_like(fr)
    ci = jnp.zeros_like(fi)
    sr = jnp.zeros_like(fr[0:1])
    si = jnp.zeros_like(sr)
    order = range(SCAN_CHUNKS - 1, 0, -1) if reverse else range(0, SCAN_CHUNKS - 1)
    for c in order:
        fcr = jnp.sum(jnp.where(row == c, fr, 0.0), axis=0, keepdims=True)
        fci = jnp.sum(jnp.where(row == c, fi, 0.0), axis=0, keepdims=True)
        mr, mi = _cmul(pr, pi, sr, si)
        sr, si = mr + fcr, mi + fci
        nxt = c - 1 if reverse else c + 1
        cr = jnp.where(row == nxt, sr, cr)
        ci = jnp.where(row == nxt, si, ci)
    return cr, ci


def _s5_fwd(u, lr, li, bmat, cmat, d, name):
    L, Du = u.shape
    ns, _, W2 = bmat.shape
    W = W2 // 2
    T = L // SCAN_CHUNKS
    rb = _pick(L, prefs=(512, 256, 128))
    per = (ns * LANES) // Du

    def body(u_ref, lr_ref, li_ref, b_ref, c_ref, d_ref, y_ref, sr_ref, si_ref):
        k = pl.program_id(0)
        for r in range(L // rb):
            rows = pl.ds(r * rb, rb)
            bu = jnp.dot(u_ref[rows, :].astype(BF16), b_ref[...], preferred_element_type=F32)
            sr_ref[rows, :] = bu[:, :W]
            si_ref[rows, :] = bu[:, W:]
        lam_r = jnp.broadcast_to(lr_ref[...], (SUBLANES, W))
        lam_i = jnp.broadcast_to(li_ref[...], (SUBLANES, W))

        def local(t, carry):
            sr, si = carry
            rows = pl.ds(pl.multiple_of(t * SUBLANES, SUBLANES), SUBLANES)
            mr, mi = _cmul(lam_r, lam_i, sr, si)
            sr = mr + sr_ref[rows, :]
            si = mi + si_ref[rows, :]
            sr_ref[rows, :] = sr
            si_ref[rows, :] = si
            return sr, si

        z = jnp.zeros((SUBLANES, W), F32)
        fr, fi = lax.fori_loop(0, T, local, (z, z))
        pr, pi = _cpow(lam_r, lam_i, T)
        cr, ci = _carry_tile(fr, fi, pr[0:1], pi[0:1], reverse=False)

        def fix(t, carry):
            wr, wi = carry
            rows = pl.ds(pl.multiple_of(t * SUBLANES, SUBLANES), SUBLANES)
            ar, ai = _cmul(wr, wi, cr, ci)
            sr_ref[rows, :] += ar
            si_ref[rows, :] += ai
            return _cmul(wr, wi, lam_r, lam_i)

        lax.fori_loop(0, T, fix, (lam_r, lam_i))
        first = (k % per) == 0
        for r in range(L // rb):
            rows = pl.ds(r * rb, rb)
            s = jnp.concatenate([sr_ref[rows, :], si_ref[rows, :]], axis=1).astype(BF16)
            y = jnp.dot(s, c_ref[...], preferred_element_type=F32)

            @pl.when(first)
            def _():
                y_ref[rows, :] = y + d_ref[...] * u_ref[rows, :]

            @pl.when(jnp.logical_not(first))
            def _():
                y_ref[rows, :] += y

    ublk = pl.BlockSpec((L, LANES), lambda k: (0, k // per))
    sblk = pl.BlockSpec((L, W), lambda k: (0, k))
    lam = pl.BlockSpec((None, 1, W), lambda k: (k, 0, 0))
    return pl.pallas_call(
        body, name=name,
        out_shape=(jax.ShapeDtypeStruct((L, Du), F32), jax.ShapeDtypeStruct((L, ns * W), F32),
                   jax.ShapeDtypeStruct((L, ns * W), F32)),
        grid=(ns,),
        in_specs=[ublk, lam, lam, pl.BlockSpec((None, LANES, 2 * W), lambda k: (k, 0, 0)),
                  pl.BlockSpec((None, 2 * W, LANES), lambda k: (k, 0, 0)),
                  pl.BlockSpec((1, LANES), lambda k: (0, k // per))],
        out_specs=(ublk, sblk, sblk), compiler_params=_cparams(("arbitrary",), VMEM_LIMIT_S5),
    )(u, lr, li, bmat.astype(BF16), cmat.astype(BF16), d.reshape(1, Du))


def _s5_bwd(dy, u, s_re, s_im, lr, li, bmat, cmat, d, name):
    L, Du = u.shape
    ns, _, W2 = bmat.shape
    W = W2 // 2
    T = L // SCAN_CHUNKS
    rb = _pick(L, prefs=(512, 256, 128))
    per = (ns * LANES) // Du
    NT = (((1,), (1,)), ((), ()))
    TN = (((0,), (0,)), ((), ()))

    def body(dy_ref, u_ref, sr_ref, si_ref, lr_ref, li_ref, b_ref, c_ref, d_ref,
             du_ref, db_ref, dc_ref, dl_ref, dd_ref, gr_ref, gi_ref):
        k = pl.program_id(0)
        for r in range(L // rb):
            rows = pl.ds(r * rb, rb)
            g = lax.dot_general(dy_ref[rows, :].astype(BF16), c_ref[...], NT, preferred_element_type=F32)
            gr_ref[rows, :] = g[:, :W]
            gi_ref[rows, :] = g[:, W:]
        lam_r = jnp.broadcast_to(lr_ref[...], (SUBLANES, W))
        lam_i = -jnp.broadcast_to(li_ref[...], (SUBLANES, W))

        def local(i, carry):
            gr, gi = carry
            rows = pl.ds(pl.multiple_of((T - 1 - i) * SUBLANES, SUBLANES), SUBLANES)
            mr, mi = _cmul(lam_r, lam_i, gr, gi)
            gr = mr + gr_ref[rows, :]
            gi = mi + gi_ref[rows, :]
            gr_ref[rows, :] = gr
            gi_ref[rows, :] = gi
            return gr, gi

        z = jnp.zeros((SUBLANES, W), F32)
        fr, fi = lax.fori_loop(0, T, local, (z, z))
        pr, pi = _cpow(lam_r, lam_i, T)
        cr, ci = _carry_tile(fr, fi, pr[0:1], pi[0:1], reverse=True)

        def true_g(rows, wr, wi):
            ar, ai = _cmul(wr, wi, cr, ci)
            gr = gr_ref[rows, :] + ar
            gi = gi_ref[rows, :] + ai
            gr_ref[rows, :] = gr
            gi_ref[rows, :] = gi
            return gr, gi

        def fix(i, carry):
            wr, wi, ar_, ai_ = carry
            t = T - 1 - i
            rows = pl.ds(pl.multiple_of(t * SUBLANES, SUBLANES), SUBLANES)
            prev = pl.ds(pl.multiple_of((t - 1) * SUBLANES, SUBLANES), SUBLANES)
            gr, gi = true_g(rows, wr, wi)
            qr, qi = sr_ref[prev, :], si_ref[prev, :]
            ar_ = ar_ + gr * qr + gi * qi
            ai_ = ai_ + gi * qr - gr * qi
            wr, wi = _cmul(wr, wi, lam_r, lam_i)
            return wr, wi, ar_, ai_

        wr, wi, acc_r, acc_i = lax.fori_loop(0, T - 1, fix, (lam_r, lam_i, z, z))
        gr, gi = true_g(pl.ds(0, SUBLANES), wr, wi)
        last = pl.ds((T - 1) * SUBLANES, SUBLANES)
        row = lax.broadcasted_iota(jnp.int32, (SUBLANES, W), 0)
        qr = jnp.where(row >= 1, pltpu.roll(sr_ref[last, :], 1, axis=0), 0.0)
        qi = jnp.where(row >= 1, pltpu.roll(si_ref[last, :], 1, axis=0), 0.0)
        acc_r = acc_r + gr * qr + gi * qi
        acc_i = acc_i + gi * qr - gr * qi
        dl_ref[0:1, :] = jnp.sum(acc_r, axis=0, keepdims=True)
        dl_ref[1:2, :] = jnp.sum(acc_i, axis=0, keepdims=True)

        first = (k % per) == 0
        db = jnp.zeros((LANES, 2 * W), F32)
        dc = jnp.zeros((LANES, 2 * W), F32)
        dd = jnp.zeros((1, LANES), F32)
        for r in range(L // rb):
            rows = pl.ds(r * rb, rb)
            gb = jnp.concatenate([gr_ref[rows, :], gi_ref[rows, :]], axis=1).astype(BF16)
            sb = jnp.concatenate([sr_ref[rows, :], si_ref[rows, :]], axis=1).astype(BF16)
            dyv = dy_ref[rows, :]
            uv = u_ref[rows, :]
            du = lax.dot_general(gb, b_ref[...], NT, preferred_element_type=F32)
            db = db + lax.dot_general(uv.astype(BF16), gb, TN, preferred_element_type=F32)
            dc = dc + lax.dot_general(dyv.astype(BF16), sb, TN, preferred_element_type=F32)
            dd = dd + jnp.sum(dyv * uv, axis=0, keepdims=True)

            @pl.when(first)
            def _():
                du_ref[rows, :] = du + d_ref[...] * dyv

            @pl.when(jnp.logical_not(first))
            def _():
                du_ref[rows, :] += du

        db_ref[...] = db
        dc_ref[...] = dc

        @pl.when(first)
        def _():
            dd_ref[...] = dd

    ublk = pl.BlockSpec((L, LANES), lambda k: (0, k // per))
    sblk = pl.BlockSpec((L, W), lambda k: (0, k))
    lam = pl.BlockSpec((None, 1, W), lambda k: (k, 0, 0))
    vec = pl.BlockSpec((1, LANES), lambda k: (0, k // per))
    mat = pl.BlockSpec((None, LANES, 2 * W), lambda k: (k, 0, 0))
    return pl.pallas_call(
        body, name=name,
        out_shape=(jax.ShapeDtypeStruct((L, Du), F32), jax.ShapeDtypeStruct((ns, LANES, 2 * W), F32),
                   jax.ShapeDtypeStruct((ns, LANES, 2 * W), F32), jax.ShapeDtypeStruct((ns, 2, W), F32),
                   jax.ShapeDtypeStruct((1, Du), F32)),
        grid=(ns,),
        in_specs=[ublk, ublk, sblk, sblk, lam, lam, mat,
                  pl.BlockSpec((None, 2 * W, LANES), lambda k: (k, 0, 0)), vec],
        out_specs=(ublk, mat, mat, pl.BlockSpec((None, 2, W), lambda k: (k, 0, 0)), vec),
        scratch_shapes=[pltpu.VMEM((L, W), F32), pltpu.VMEM((L, W), F32)],
        compiler_params=_cparams(("arbitrary",), VMEM_LIMIT_S5),
    )(dy, u, s_re, s_im, lr, li, bmat.astype(BF16), cmat.astype(BF16), d.reshape(1, Du))


def _glu_fwd(yraw, wmat, bias, name):
    L, C = yraw.shape
    tr = _pick(L, prefs=(512, 256, 128))

    def body(y_ref, w_ref, b_ref, o_ref):
        yg = _gelu(y_ref[...])
        zz = jnp.dot(yg.astype(BF16), w_ref[...], preferred_element_type=F32) + b_ref[...]
        o_ref[...] = (yg * _sigmoid(zz)).astype(BF16)

    return pl.pallas_call(
        body, name=name, out_shape=jax.ShapeDtypeStruct((L, C), BF16), grid=(L // tr,),
        in_specs=[pl.BlockSpec((tr, C), lambda i: (i, 0)), pl.BlockSpec((C, C), lambda i: (0, 0)),
                  pl.BlockSpec((1, C), lambda i: (0, 0))],
        out_specs=pl.BlockSpec((tr, C), lambda i: (i, 0)), compiler_params=_cparams(("parallel",)),
    )(yraw, wmat, bias.reshape(1, C))


def _glu_bwd(yraw, dyb, wmat, bias, name):
    L, C = yraw.shape
    tr = _pick(L, prefs=(512, 256, 128))
    nsteps = L // tr

    def body(y_ref, d_ref, w_ref, b_ref, dy_ref, dw_ref, db_ref, acc_b):
        i = pl.program_id(0)

        @pl.when(i == 0)
        def _():
            dw_ref[...] = jnp.zeros_like(dw_ref)
            acc_b[...] = jnp.zeros_like(acc_b)

        yr = y_ref[...]
        yg = _gelu(yr)
        ygb = yg.astype(BF16)
        sg = _sigmoid(jnp.dot(ygb, w_ref[...], preferred_element_type=F32) + b_ref[...])
        dyb_ = d_ref[...]
        dz = dyb_ * yg * sg * (1.0 - sg)
        dzb = dz.astype(BF16)
        dyg = dyb_ * sg + lax.dot_general(dzb, w_ref[...], (((1,), (1,)), ((), ())), preferred_element_type=F32)
        dw_ref[...] += lax.dot_general(ygb, dzb, (((0,), (0,)), ((), ())), preferred_element_type=F32)
        acc_b[...] += jnp.sum(dz.reshape(tr // SUBLANES, SUBLANES, C), axis=0)
        dy_ref[...] = dyg * _gelu_grad(yr)

        @pl.when(i == nsteps - 1)
        def _():
            db_ref[...] = jnp.sum(acc_b[...], axis=0, keepdims=True)

    row = pl.BlockSpec((tr, C), lambda i: (i, 0))
    return pl.pallas_call(
        body, name=name,
        out_shape=(jax.ShapeDtypeStruct((L, C), F32), jax.ShapeDtypeStruct((C, C), F32),
                   jax.ShapeDtypeStruct((1, C), F32)),
        grid=(nsteps,),
        in_specs=[row, row, pl.BlockSpec((C, C), lambda i: (0, 0)), pl.BlockSpec((1, C), lambda i: (0, 0))],
        out_specs=(row, pl.BlockSpec((C, C), lambda i: (0, 0)), pl.BlockSpec((1, C), lambda i: (0, 0))),
        scratch_shapes=[pltpu.VMEM((SUBLANES, C), F32)], compiler_params=_cparams(("arbitrary",)),
    )(yraw, dyb, wmat, bias.reshape(1, C))


def _pool_counts(L, g):
    t = lax.broadcasted_iota(jnp.int32, (L, LANES), 0).astype(F32) + 1.0
    w = jnp.where(g == 0, 2.0, jnp.where(g == 1, 4.0, jnp.where(g == 2, 8.0, 16.0)))
    return 1.0 / jnp.minimum(t, w)


def _select_window(g, a2, a4, a8, a16):
    return jnp.where(g == 0, a2, jnp.where(g == 1, a4, jnp.where(g == 2, a8, a16)))


def _pooled(z, g):
    a2 = z + _down(z, 1)
    a4 = a2 + _down(a2, 2)
    a8 = a4 + _down(a4, 4)
    a16 = a8 + _down(a8, 8)
    return _select_window(g, a2, a4, a8, a16) * _pool_counts(z.shape[0], g) - z


def _pool_fwd(proj3, pool_w, scale, name):
    _, L, C = proj3.shape
    ng = len(POOL_WINDOWS)
    pg = C // ng
    assert pg == LANES

    def body(z_ref, w_ref, s_ref, o_ref):
        g = pl.program_id(0)
        p = _pooled(z_ref[...].astype(F32), g)
        y = jnp.dot(p.astype(BF16), w_ref[...].astype(BF16), preferred_element_type=F32)
        o_ref[...] = (y * s_ref[...]).astype(BF16)

    return pl.pallas_call(
        body, name=name, out_shape=jax.ShapeDtypeStruct((L, C), BF16), grid=(ng,),
        in_specs=[pl.BlockSpec((None, L, pg), lambda g: (0, 0, g)), pl.BlockSpec((None, pg, pg), lambda g: (g, 0, 0)),
                  pl.BlockSpec((1, pg), lambda g: (0, g))],
        out_specs=pl.BlockSpec((L, pg), lambda g: (0, g)), compiler_params=_cparams(("parallel",)),
    )(proj3, pool_w, scale.reshape(1, C))


def _pool_bwd(proj3, dmix, pool_w, scale, name):
    _, L, C = proj3.shape
    ng = len(POOL_WINDOWS)
    pg = C // ng

    def body(z_ref, d_ref, w_ref, s_ref, dz_ref, dw_ref, ds_ref):
        g = pl.program_id(0)
        p = _pooled(z_ref[...].astype(F32), g)
        pb = p.astype(BF16)
        wb = w_ref[...].astype(BF16)
        pre = jnp.dot(pb, wb, preferred_element_type=F32)
        dyc = d_ref[...]
        ds_ref[...] = jnp.sum(dyc * pre, axis=0, keepdims=True)
        dpre = (dyc * s_ref[...]).astype(BF16)
        dw_ref[...] = lax.dot_general(pb, dpre, (((0,), (0,)), ((), ())), preferred_element_type=F32)
        dp = lax.dot_general(dpre, wb, (((1,), (1,)), ((), ())), preferred_element_type=F32)
        v = dp * _pool_counts(L, g)
        a2 = v + _up(v, 1)
        a4 = a2 + _up(a2, 2)
        a8 = a4 + _up(a4, 4)
        a16 = a8 + _up(a8, 8)
        dz_ref[...] = (_select_window(g, a2, a4, a8, a16) - dp).astype(BF16)

    return pl.pallas_call(
        body, name=name,
        out_shape=(jax.ShapeDtypeStruct((L, C), BF16), jax.ShapeDtypeStruct((ng, pg, pg), F32),
                   jax.ShapeDtypeStruct((1, C), F32)),
        grid=(ng,),
        in_specs=[pl.BlockSpec((None, L, pg), lambda g: (0, 0, g)), pl.BlockSpec((L, pg), lambda g: (0, g)),
                  pl.BlockSpec((None, pg, pg), lambda g: (g, 0, 0)), pl.BlockSpec((1, pg), lambda g: (0, g))],
        out_specs=(pl.BlockSpec((L, pg), lambda g: (0, g)), pl.BlockSpec((None, pg, pg), lambda g: (g, 0, 0)),
                   pl.BlockSpec((1, pg), lambda g: (0, g))),
        compiler_params=_cparams(("parallel",)),
    )(proj3, dmix, pool_w, scale.reshape(1, C))


def _tril_w(w_ref, h):
    r = lax.broadcasted_iota(jnp.int32, (CHUNK, CHUNK), 0)
    c = lax.broadcasted_iota(jnp.int32, (CHUNK, CHUNK), 1)
    return jnp.where(r >= c, w_ref[h], 0.0)


def _sgu_fwd(proj3, norm_g, w, b, name):
    _, L, C = proj3.shape
    nh = w.shape[0]
    dh = C // nh
    assert dh == LANES and w.shape[1] == CHUNK
    tr = _pick(L, prefs=(512, 256, 128))
    bfull = jnp.broadcast_to(b[:, :, None], (nh, CHUNK, dh))

    def body(su_ref, sv_ref, g_ref, w_ref, b_ref, o_ref):
        sv = _gelu(sv_ref[...].astype(F32))
        r = lax.rsqrt(jnp.mean(sv * sv, axis=-1, keepdims=True) + EPS)
        v = (sv * r * g_ref[...]).astype(BF16)
        for h in range(nh):
            wm = _tril_w(w_ref, h).astype(BF16)
            cols = slice(h * dh, (h + 1) * dh)
            for n in range(tr // CHUNK):
                rows = slice(n * CHUNK, (n + 1) * CHUNK)
                mixed = jnp.dot(wm, v[rows, cols], preferred_element_type=F32) + b_ref[h]
                o_ref[rows, cols] = (_gelu(su_ref[rows, cols].astype(F32)) * mixed).astype(BF16)

    full = lambda shp: pl.BlockSpec(shp, lambda i: (0,) * len(shp))
    return pl.pallas_call(
        body, name=name, out_shape=jax.ShapeDtypeStruct((L, C), BF16), grid=(L // tr,),
        in_specs=[pl.BlockSpec((None, tr, C), lambda i: (1, i, 0)), pl.BlockSpec((None, tr, C), lambda i: (2, i, 0)),
                  full((1, C)), full((nh, CHUNK, CHUNK)), full((nh, CHUNK, dh))],
        out_specs=pl.BlockSpec((tr, C), lambda i: (i, 0)), compiler_params=_cparams(("parallel",)),
    )(proj3, proj3, norm_g.reshape(1, C), w, bfull)


def _sgu_bwd(proj3, dmix, norm_g, w, b, name):
    _, L, C = proj3.shape
    nh = w.shape[0]
    dh = C // nh
    tr = _pick(L, prefs=(512, 256, 128))
    nsteps = L // tr
    bfull = jnp.broadcast_to(b[:, :, None], (nh, CHUNK, dh))

    def body(su_ref, sv_ref, d_ref, g_ref, w_ref, b_ref, o_ref, dw_ref, db_ref, dg_ref, dv_ref, acc_g):
        i = pl.program_id(0)

        @pl.when(i == 0)
        def _():
            dw_ref[...] = jnp.zeros_like(dw_ref)
            db_ref[...] = jnp.zeros_like(db_ref)
            acc_g[...] = jnp.zeros_like(acc_g)

        svp = sv_ref[...].astype(F32)
        sv = _gelu(svp)
        r = lax.rsqrt(jnp.mean(sv * sv, axis=-1, keepdims=True) + EPS)
        vh = sv * r
        gv = g_ref[...]
        v = (vh * gv).astype(BF16)
        tri_r = lax.broadcasted_iota(jnp.int32, (CHUNK, CHUNK), 0)
        tri_c = lax.broadcasted_iota(jnp.int32, (CHUNK, CHUNK), 1)
        for h in range(nh):
            wm = _tril_w(w_ref, h).astype(BF16)
            cols = slice(h * dh, (h + 1) * dh)
            dwh = jnp.zeros((CHUNK, CHUNK), F32)
            dbh = jnp.zeros((CHUNK, dh), F32)
            for n in range(tr // CHUNK):
                rows = slice(n * CHUNK, (n + 1) * CHUNK)
                vb = v[rows, cols]
                mixed = jnp.dot(wm, vb, preferred_element_type=F32) + b_ref[h]
                sup = su_ref[rows, cols].astype(F32)
                dyd = d_ref[rows, cols]
                dmx = dyd * _gelu(sup)
                o_ref[0, rows, cols] = (dyd * mixed * _gelu_grad(sup)).astype(BF16)
                dmb = dmx.astype(BF16)
                dwh = dwh + lax.dot_general(dmb, vb, (((1,), (1,)), ((), ())), preferred_element_type=F32)
                dbh = dbh + dmx
                dv_ref[rows, cols] = lax.dot_general(wm, dmb, (((0,), (0,)), ((), ())), preferred_element_type=F32)
            dw_ref[h] += jnp.where(tri_r >= tri_c, dwh, 0.0)
            db_ref[h] += dbh
        dv = dv_ref[...]
        acc_g[...] += jnp.sum((dv * vh).reshape(tr // SUBLANES, SUBLANES, C), axis=0)
        dvg = dv * gv
        dsv = r * (dvg - vh * jnp.mean(dvg * vh, axis=-1, keepdims=True))
        o_ref[1] = (dsv * _gelu_grad(svp)).astype(BF16)

        @pl.when(i == nsteps - 1)
        def _():
            dg_ref[...] = jnp.sum(acc_g[...], axis=0, keepdims=True)

    full = lambda shp: pl.BlockSpec(shp, lambda i: (0,) * len(shp))
    return pl.pallas_call(
        body, name=name,
        out_shape=(jax.ShapeDtypeStruct((2, L, C), BF16), jax.ShapeDtypeStruct((nh, CHUNK, CHUNK), F32),
                   jax.ShapeDtypeStruct((nh, CHUNK, dh), F32), jax.ShapeDtypeStruct((1, C), F32)),
        grid=(nsteps,),
        in_specs=[pl.BlockSpec((None, tr, C), lambda i: (1, i, 0)), pl.BlockSpec((None, tr, C), lambda i: (2, i, 0)),
                  pl.BlockSpec((tr, C), lambda i: (i, 1)), full((1, C)), full((nh, CHUNK, CHUNK)),
                  full((nh, CHUNK, dh))],
        out_specs=(pl.BlockSpec((2, tr, C), lambda i: (0, i, 0)), full((nh, CHUNK, CHUNK)), full((nh, CHUNK, dh)),
                   full((1, C))),
        scratch_shapes=[pltpu.VMEM((tr, C), F32), pltpu.VMEM((SUBLANES, C), F32)],
        compiler_params=_cparams(("arbitrary",)),
    )(proj3, proj3, dmix, norm_g.reshape(1, C), w, bfull)


def _ffn_act_fwd(up3, conv_w, conv_b, name):
    _, L, Fh = up3.shape
    cb = LANES
    w2 = conv_w.reshape(3, 2, Fh).transpose(1, 0, 2)
    b2 = conv_b.reshape(2, 1, Fh)

    def body(u_ref, w_ref, b_ref, o_ref, ot_ref):
        g = _conv3(_taps(u_ref[0].astype(F32)), w_ref[0]) + b_ref[0]
        v = _conv3(_taps(u_ref[1].astype(F32)), w_ref[1]) + b_ref[1]
        a = g * _sigmoid(g) * v
        o_ref[...] = a.astype(BF16)
        ot_ref[...] = a.T.astype(BF16)

    return pl.pallas_call(
        body, name=name, out_shape=(jax.ShapeDtypeStruct((L, Fh), BF16), jax.ShapeDtypeStruct((Fh, L), BF16)),
        grid=(Fh // cb,),
        in_specs=[pl.BlockSpec((2, L, cb), lambda j: (0, 0, j)), pl.BlockSpec((2, 3, cb), lambda j: (0, 0, j)),
                  pl.BlockSpec((2, 1, cb), lambda j: (0, 0, j))],
        out_specs=(pl.BlockSpec((L, cb), lambda j: (0, j)), pl.BlockSpec((cb, L), lambda j: (j, 0))),
        compiler_params=_cparams(("parallel",)),
    )(up3, w2, b2)


def _ffn_act_bwd(up3, da, conv_w, conv_b, name):
    _, L, Fh = up3.shape
    cb = LANES
    w2 = conv_w.reshape(3, 2, Fh).transpose(1, 0, 2)
    b2 = conv_b.reshape(2, 1, Fh)

    def body(u_ref, d_ref, w_ref, b_ref, o_ref, dw_ref, db_ref):
        tg, tv = _taps(u_ref[0].astype(F32)), _taps(u_ref[1].astype(F32))
        wg, wv = w_ref[0], w_ref[1]
        g = _conv3(tg, wg) + b_ref[0]
        v = _conv3(tv, wv) + b_ref[1]
        sg = _sigmoid(g)
        dav = d_ref[...].astype(F32)
        dg = dav * v * (sg * (1.0 + g * (1.0 - sg)))
        dv = dav * (g * sg)
        o_ref[0] = _conv3_t(dg, wg).astype(BF16)
        o_ref[1] = _conv3_t(dv, wv).astype(BF16)
        for tap, (dwg, dwv) in enumerate(zip(_conv3_dw(dg, tg), _conv3_dw(dv, tv))):
            dw_ref[0, tap:tap + 1, :] = dwg
            dw_ref[1, tap:tap + 1, :] = dwv
        db_ref[0] = jnp.sum(dg, axis=0, keepdims=True)
        db_ref[1] = jnp.sum(dv, axis=0, keepdims=True)

    dup, dw2, db2 = pl.pallas_call(
        body, name=name,
        out_shape=(jax.ShapeDtypeStruct((2, L, Fh), BF16), jax.ShapeDtypeStruct((2, 3, Fh), F32),
                   jax.ShapeDtypeStruct((2, 1, Fh), F32)),
        grid=(Fh // cb,),
        in_specs=[pl.BlockSpec((2, L, cb), lambda j: (0, 0, j)), pl.BlockSpec((L, cb), lambda j: (0, j)),
                  pl.BlockSpec((2, 3, cb), lambda j: (0, 0, j)), pl.BlockSpec((2, 1, cb), lambda j: (0, 0, j))],
        out_specs=(pl.BlockSpec((2, L, cb), lambda j: (0, 0, j)), pl.BlockSpec((2, 3, cb), lambda j: (0, 0, j)),
                   pl.BlockSpec((2, 1, cb), lambda j: (0, 0, j))),
        compiler_params=_cparams(("parallel",)),
    )(up3, da, w2, b2)
    return dup, dw2.transpose(1, 0, 2).reshape(3, 2 * Fh), db2.reshape(2 * Fh)


def _local_step(x, tgt, w, layer_weights, on_layer_grads):
    L, D = x.shape
    depth = w['norm_mix_g'].shape[0]
    saved = []
    for i in range(depth):
        j = i // 2
        wb = dict(layer_weights(2 * i, x))
        s = {'x': x, 'wb': wb}
        if i % 2 == 0:
            proj4, s['hT'] = _norm_mm(x, w['norm_mix_g'][i], wb['even_w_in'], BF16, "even_in_fwd", ok=('seg', 4))
            s['proj'] = proj4
            ya = _sconv_fwd(proj4, w['even_conv_w'][j], "sconv_fwd")
            prm = (w['ssm_log_step'][j], w['ssm_a_re'][j], w['ssm_a_im'][j], w['ssm_b_re'][j], w['ssm_b_im'][j],
                   w['ssm_c_re'][j], w['ssm_c_im'][j])
            (lr, li, bmat, cmat), prep_vjp = jax.vjp(_s5_prep, *prm)
            u = _to_scan_order(proj4[3])
            yraw, s_re, s_im = _s5_fwd(u, lr, li, bmat, cmat, w['ssm_d'][j], "s5_fwd")
            yb = _glu_fwd(yraw, wb['ssm_glu_w'], w['ssm_glu_b'][j], "glu_fwd")
            s.update(u=u, yraw=yraw, s_re=s_re, s_im=s_im, s5=(lr, li, bmat, cmat), prep_vjp=prep_vjp)
            mixin = jnp.concatenate([ya, _from_scan_order(yb)], axis=1)
            x = _mm(mixin, wb['even_w_out'], 'nn', F32, "even_out_fwd", res=x)
        else:
            proj3, s['hT'] = _norm_mm(x, w['norm_mix_g'][i], wb['odd_w_in'], BF16, "odd_in_fwd", ok=('seg', 3))
            s['proj'] = proj3
            yc = _pool_fwd(proj3, w['pool_w'][j], w['pool_scale'][j], "pool_fwd")
            yd = _sgu_fwd(proj3, w['sgu_norm_g'][j], w['sgu_w'][j], w['sgu_b'][j], "sgu_fwd")
            mixin = jnp.concatenate([yc, yd], axis=1)
            x = _mm(mixin, wb['odd_w_out'], 'nn', F32, "odd_out_fwd", res=x)
        s['mixin'] = mixin
        s['x1'] = x
        wb.update(layer_weights(2 * i + 1, x))
        up3, h2t = _norm_mm(x, w['norm_ffn_g'][i], wb['ffn_w_up'], BF16, "ffn_up_fwd", ok=('seg', 2))
        a, at = _ffn_act_fwd(up3, w['ffn_conv_w'][i], w['ffn_conv_b'][i], "ffn_act_fwd")
        x = _mm(a, wb['ffn_w_down'], 'nn', F32, "ffn_down_fwd", res=x)
        s.update(h2T=h2t, up3=up3, aT=at)
        saved.append(s)

    loss8, dx, dxb, dg_final = _loss_head(x, w['norm_final_g'], tgt)
    gs = {n: [None] * w[n].shape[0] for n in SMALL if n != 'norm_final_g'}
    gs['norm_final_g'] = dg_final.reshape(D)

    dep = None
    for i in reversed(range(depth)):
        j = i // 2
        s = saved[i]
        wb = s['wb']
        gb = {}
        da = _mm(dxb, wb['ffn_w_down'], 'nt', BF16, "ffn_down_dgrad", dep=dep)
        gb['ffn_w_down'] = _mm(s['aT'], dxb, 'nn', BF16, "ffn_down_wgrad")
        dup3, dcw, dcb = _ffn_act_bwd(s['up3'], da, w['ffn_conv_w'][i], w['ffn_conv_b'][i], "ffn_act_bwd")
        gs['ffn_conv_w'][i], gs['ffn_conv_b'][i] = dcw, dcb
        gb['ffn_w_up'] = _mm(s['h2T'], dup3, 'nn', BF16, "ffn_up_wgrad", bk=('seg', 2))
        dep = on_layer_grads(2 * i + 1, gb)
        dx, dxb, dg = _mm_norm_bwd(dup3, wb['ffn_w_up'], s['x1'], w['norm_ffn_g'][i], dx, "ffn_up_dgrad",
                              ak=('seg', 2), dep=dep)
        gs['norm_ffn_g'][i] = dg.reshape(D)
        gb = {}
        if i % 2 == 0:
            dmix = _mm(dxb, wb['even_w_out'], 'nt', F32, "even_out_dgrad")
            gb['even_w_out'] = _mm(s['mixin'].T, dxb, 'nn', BF16, "even_out_wgrad")
            dpc, dcw = _sconv_bwd(s['proj'], dmix, w['even_conv_w'][j], "sconv_bwd")
            gs['even_conv_w'][j] = dcw
            dyb = _to_scan_order(dmix[:, D // 2:])
            dyraw, dglu_w, dglu_b = _glu_bwd(s['yraw'], dyb, wb['ssm_glu_w'], w['ssm_glu_b'][j], "glu_bwd")
            gb['ssm_glu_w'] = dglu_w.astype(BF16)
            gs['ssm_glu_b'][j] = dglu_b.reshape(-1)
            lr, li, bmat, cmat = s['s5']
            du, dbm, dcm, dlam, dd = _s5_bwd(dyraw, s['u'], s['s_re'], s['s_im'], lr, li, bmat, cmat,
                                            w['ssm_d'][j], "s5_bwd")
            gs['ssm_d'][j] = dd.reshape(-1)
            dcm = jnp.swapaxes(dcm, 1, 2)
            dprm = s['prep_vjp']((dlam[:, 0:1, :], dlam[:, 1:2, :], dbm, dcm))
            for n, gval in zip(('ssm_log_step', 'ssm_a_re', 'ssm_a_im', 'ssm_b_re', 'ssm_b_im', 'ssm_c_re',
                                'ssm_c_im'), dprm):
                gs[n][j] = gval
            dproj = jnp.concatenate([dpc, _from_scan_order(du).astype(BF16)[None]], axis=0)
            gb['even_w_in'] = _mm(s['hT'], dproj, 'nn', BF16, "even_in_wgrad", bk=('seg', 4))
            w_in, in_kind, in_name = wb['even_w_in'], ('seg', 4), "even_in_dgrad"
        else:
            dmix = _mm(dxb, wb['odd_w_out'], 'nt', F32, "odd_out_dgrad")
            gb['odd_w_out'] = _mm(s['mixin'].T, dxb, 'nn', BF16, "odd_out_wgrad")
            dz, dpw, dps = _pool_bwd(s['proj'], dmix, w['pool_w'][j], w['pool_scale'][j], "pool_bwd")
            gs['pool_w'][j], gs['pool_scale'][j] = dpw, dps.reshape(-1)
            dsuv, dsw, dsb, dsg = _sgu_bwd(s['proj'], dmix, w['sgu_norm_g'][j], w['sgu_w'][j], w['sgu_b'][j],
                                           "sgu_bwd")
            gs['sgu_w'][j], gs['sgu_b'][j], gs['sgu_norm_g'][j] = dsw, jnp.sum(dsb, axis=-1), dsg.reshape(-1)
            dproj = jnp.concatenate([dz[None], dsuv], axis=0)
            gb['odd_w_in'] = _mm(s['hT'], dproj, 'nn', BF16, "odd_in_wgrad", bk=('seg', 3))
            w_in, in_kind, in_name = wb['odd_w_in'], ('seg', 3), "odd_in_dgrad"
        dep = on_layer_grads(2 * i, gb)
        dx, dxb, dg = _mm_norm_bwd(dproj, w_in, s['x'], w['norm_mix_g'][i], dx, in_name, ak=in_kind, dep=dep)
        gs['norm_mix_g'][i] = dg.reshape(D)

    gsmall = {n: (v if n == 'norm_final_g' else jnp.stack(v)) for n, v in gs.items()}
    return loss8[0, 0], dx, gsmall


_HBM = pl.BlockSpec(memory_space=pltpu.HBM)
_CHIP_FLIPS = ((0, 0), (1, 0), (0, 1), (1, 1))


def _coords():
    return lax.axis_index("x"), lax.axis_index("y"), lax.axis_index("c")


def _flip(v, f):
    return 1 - v if f else v


def _shard_of(ref, axis, s, width):
    start = pl.multiple_of(s * width, LANES if axis == ref.ndim - 1 else 16) if width % 16 == 0 else s * width
    idx = [slice(None)] * ref.ndim
    idx[axis] = pl.ds(start, width)
    return ref.at[tuple(idx)]


_SEM = pl.BlockSpec(memory_space=pltpu.SEMAPHORE)
_ANY = pl.BlockSpec(memory_space=pl.ANY)
_DATAFLOW = pltpu.SideEffectType.DATAFLOW_SIDE_EFFECTING


def _in_hbm(a):
    return pltpu.with_memory_space_constraint(a, pltpu.HBM)


def _model_layer(name, l):
    if name.startswith('ffn'):
        return l
    return 2 * l + 1 if name.startswith('odd') else 2 * l


def _place_quarter(shard, l, axis, chip, dtype, dep=None):
    _, r, c = shard.shape
    tr = _pick(r, prefs=(512, 256, 128, 64, 32, 16))
    nrb = r // tr

    def body(chip_ref, i_ref, *rest):
        rest[-1][...] = i_ref[...].astype(dtype)

    if axis == 1:
        out_shape, o_map = (r, c * N_CHIPS), (lambda i, s: (i, s[0]))
    else:
        out_shape, o_map = (r * N_CHIPS, c), (lambda i, s: (s[0] * nrb + i, 0))
    in_specs = [pl.BlockSpec((None, tr, c), lambda i, s: (l, i, 0))]
    args = [chip, shard]
    if dep is not None:
        in_specs.append(pl.BlockSpec(memory_space=pl.ANY))
        args.append(dep)
    return pl.pallas_call(
        body, name="place_quarter", out_shape=jax.ShapeDtypeStruct(out_shape, dtype),
        grid_spec=pltpu.PrefetchScalarGridSpec(
            num_scalar_prefetch=1, grid=(nrb,), in_specs=in_specs, out_specs=pl.BlockSpec((tr, c), o_map)),
        compiler_params=_cparams(("parallel",)),
    )(*args)


def _gather_copies(land_refs, send_sem, recv_sem, axes, landing_chip_of):
    x, y, c = _coords()
    out = []
    for j, land in enumerate(land_refs):
        width = land.shape[axes[j]] // N_CHIPS
        for f in (1, 2, 3):
            fx, fy = _CHIP_FLIPS[f]
            px, py = _flip(x, fx), _flip(y, fy)
            lx, ly = landing_chip_of(px, py)
            out.append(pltpu.make_async_remote_copy(
                src_ref=_shard_of(land, axes[j], 2 * x + y, width), dst_ref=_shard_of(land, axes[j], 2 * lx + ly, width),
                send_sem=send_sem.at[3 * j + f - 1], recv_sem=recv_sem.at[3 * j + f - 1],
                device_id=(px, py, c), device_id_type=MESH))
    return out


def _gather_start(tag, lands, axes, dep=None):
    n = len(lands)

    def body(*refs):
        land_refs, send_sem, recv_sem = refs[:n], refs[-3], refs[-2]
        x, y, _ = _coords()
        for cp in _gather_copies(land_refs, send_sem, recv_sem, axes, lambda px, py: (x, y)):
            cp.start()
        refs[-1][...] = jnp.zeros_like(refs[-1])

    thru = [pltpu.HBM(a.shape, a.dtype) for a in lands]
    outs = pl.pallas_call(
        body, name=f"gather_start_{tag}",
        out_shape=tuple(thru + [pltpu.SemaphoreType.DMA((3 * n,)), pltpu.SemaphoreType.DMA((3 * n,)),
                                jax.ShapeDtypeStruct((SUBLANES, LANES), F32)]),
        in_specs=[_HBM] * n + ([_ANY] if dep is not None else []),
        out_specs=tuple([_HBM] * n + [_SEM, _SEM, pl.BlockSpec(memory_space=pltpu.VMEM)]),
        input_output_aliases={i: i for i in range(n)},
        compiler_params=pltpu.CompilerParams(has_side_effects=_DATAFLOW),
    )(*[_in_hbm(a) for a in lands], *([dep] if dep is not None else []))
    return list(outs[:n]), outs[n], outs[n + 1], outs[n + 2]


def _gather_wait(tag, lands, send_sem, recv_sem, axes, after):
    n = len(lands)

    def body(*refs):
        for cp in _gather_copies(refs[:n], refs[n], refs[n + 1], axes, lambda px, py: (px, py)):
            cp.wait_send()
            cp.wait_recv()

    outs = pl.pallas_call(
        body, name=f"gather_wait_{tag}", out_shape=tuple(pltpu.HBM(a.shape, a.dtype) for a in lands),
        in_specs=[_HBM] * n + [_SEM, _SEM, _ANY], out_specs=tuple([_HBM] * n),
        input_output_aliases={i: i for i in range(n)},
        compiler_params=pltpu.CompilerParams(has_side_effects=_DATAFLOW),
    )(*lands, send_sem, recv_sem, after)
    return list(outs)


N_SLOTS = N_DEV - 1


def _scatter_sends(grad_refs, land_refs, send_sem, recv_sem, meta):
    x, y, c = _coords()
    out = []
    for j, (axis, owner, q, width) in enumerate(meta):
        other = c if owner == 0 else 1 - c
        for f, (fx, fy) in enumerate(_CHIP_FLIPS):
            px, py = _flip(x, fx), _flip(y, fy)
            slot = f + 4 * other - 1
            out.append((other if f == 0 else None, pltpu.make_async_remote_copy(
                src_ref=_shard_of(grad_refs[j], axis, 2 * px + py, width), dst_ref=land_refs[j].at[q, slot],
                send_sem=send_sem.at[4 * j + f], recv_sem=recv_sem.at[N_SLOTS * j + slot],
                device_id=(px, py, owner), device_id_type=MESH)))
    return out


def _scatter_start(layer, grads, lands, meta):
    n = len(grads)
    uniq = []
    for a in lands:
        if not any(a is u for u in uniq):
            uniq.append(a)
    which = [next(k for k, u in enumerate(uniq) if u is a) for a in lands]
    nu = len(uniq)

    def body(*refs):
        grad_refs, land_u = refs[:n], refs[n:n + nu]
        send_sem, recv_sem = refs[n + nu], refs[n + nu + 1]
        for other, cp in _scatter_sends(grad_refs, [land_u[k] for k in which], send_sem, recv_sem, meta):
            if other is None:
                cp.start()
            else:
                pl.when(other == 1)(cp.start)
        refs[-1][...] = jnp.zeros_like(refs[-1])

    thru = [pltpu.HBM(a.shape, a.dtype) for a in list(grads) + uniq]
    outs = pl.pallas_call(
        body, name=f"scatter_start_{layer}",
        out_shape=tuple([pltpu.SemaphoreType.DMA((4 * n,)), pltpu.SemaphoreType.DMA((N_SLOTS * n,))] + thru
                        + [jax.ShapeDtypeStruct((SUBLANES, LANES), F32)]),
        in_specs=[_HBM] * (n + nu),
        out_specs=tuple([_SEM, _SEM] + [_HBM] * (n + nu) + [pl.BlockSpec(memory_space=pltpu.VMEM)]),
        input_output_aliases={i: 2 + i for i in range(n + nu)},
        compiler_params=pltpu.CompilerParams(has_side_effects=_DATAFLOW),
    )(*[_in_hbm(a) for a in list(grads) + uniq])
    new_lands = [outs[2 + n + k] for k in which]
    return outs[0], outs[1], list(outs[2:2 + n]), new_lands, outs[-1]


def _scatter_wait(started, lands):
    nl = len(lands)
    flat_grads = [g for s in started for g in s[2]]
    ng, ns = len(flat_grads), len(started)

    def body(*refs):
        land_refs = refs[:nl]
        grad_refs = refs[nl:nl + ng]
        sem_refs = refs[nl + ng:nl + ng + 2 * ns]
        _, _, c = _coords()
        off = 0
        for k, (_, _, grads, idx, meta) in enumerate(started):
            send_sem, recv_sem = sem_refs[2 * k], sem_refs[2 * k + 1]
            lr = [land_refs[i] for i in idx]
            for other, cp in _scatter_sends(grad_refs[off:off + len(grads)], lr, send_sem, recv_sem, meta):
                if other is None:
                    cp.wait_send()
                else:
                    pl.when(other == 1)(cp.wait_send)
            for j, (axis, owner, q, width) in enumerate(meta):
                mine = (c if owner == 0 else 1 - c) == 0

                @pl.when(mine)
                def _():
                    for slot in range(N_SLOTS):
                        land = lr[j].at[q, slot]
                        pltpu.make_async_remote_copy(
                            src_ref=land, dst_ref=land, send_sem=send_sem.at[0], recv_sem=recv_sem.at[N_SLOTS * j + slot],
                            device_id=_coords(), device_id_type=MESH).wait_recv()
            off += len(grads)

    args = list(lands) + flat_grads
    thru = [pltpu.HBM(a.shape, a.dtype) for a in args]
    sems = [s for st in started for s in st[:2]]
    outs = pl.pallas_call(
        body, name="scatter_wait", out_shape=tuple(thru), in_specs=[_HBM] * (nl + ng) + [_SEM] * (2 * ns),
        out_specs=tuple([_HBM] * (nl + ng)), input_output_aliases={i: i for i in range(nl + ng)},
        compiler_params=pltpu.CompilerParams(has_side_effects=_DATAFLOW),
    )(*args, *sems)
    return list(outs[:nl]), list(outs[nl:])


def _sum_and_share(recv, layer_grads, axis, chip, name):
    n, ns, r, c = recv.shape
    tr = _pick(r, prefs=(256, 128, 64, 32, 16))
    nr = r // tr
    nsteps = n * nr
    nlay = len(layer_grads)
    own_map = (lambda h, i, s: (i, s[0])) if axis == 1 else (lambda h, i, s: (s[0] * nr + i, 0))

    def body(chip_ref, i_ref, *rest):
        g_refs = rest[:nlay]
        o_ref, buf, loc_sems, send_sems, recv_sems = rest[nlay:]
        h, i = pl.program_id(0), pl.program_id(1)
        step = h * nr + i
        slot = step % 2
        x, y, core = _coords()
        layer = core * n + h
        own = g_refs[0][...]
        for l in range(1, nlay):
            own = jnp.where(layer == l, g_refs[l][...], own)

        def copies(sl):
            dst = o_ref.at[core * n + h, pl.ds(pl.multiple_of(i * tr, tr), tr), :]
            loc = pltpu.make_async_copy(buf.at[sl], dst, loc_sems.at[sl])
            rem = pltpu.make_async_remote_copy(
                src_ref=buf.at[sl], dst_ref=dst, send_sem=send_sems.at[sl], recv_sem=recv_sems.at[step],
                device_id=(x, y, 1 - core), device_id_type=MESH)
            return loc, rem

        def drain(sl):
            loc, rem = copies(sl)
            loc.wait()
            rem.wait_send()

        pl.when(step >= 2)(lambda: drain(slot))
        acc = own.astype(F32)
        for s in range(ns):
            acc = acc + i_ref[s].astype(F32)
        buf[slot] = acc
        loc, rem = copies(slot)
        loc.start()
        rem.start()

        @pl.when(step == nsteps - 1)
        def _():
            drain(slot)
            if nsteps > 1:
                drain(1 - slot)
            for hh in range(n):
                for ii in range(nr):
                    land = o_ref.at[(1 - core) * n + hh, pl.ds(ii * tr, tr), :]
                    pltpu.make_async_remote_copy(
                        src_ref=buf.at[0], dst_ref=land, send_sem=send_sems.at[0], recv_sem=recv_sems.at[hh * nr + ii],
                        device_id=(x, y, 1 - core), device_id_type=MESH).wait_recv()

    return pl.pallas_call(
        body, name=name, out_shape=jax.ShapeDtypeStruct((2 * n, r, c), F32),
        grid_spec=pltpu.PrefetchScalarGridSpec(
            num_scalar_prefetch=1, grid=(n, nr),
            in_specs=[pl.BlockSpec((None, ns, tr, c), lambda h, i, s: (h, 0, i, 0))]
            + [pl.BlockSpec((tr, c), own_map)] * nlay,
            out_specs=_HBM,
            scratch_shapes=[pltpu.VMEM((2, tr, c), F32), pltpu.SemaphoreType.DMA((2,)),
                            pltpu.SemaphoreType.DMA((2,)), pltpu.SemaphoreType.DMA((nsteps,))]),
        compiler_params=_cparams(("arbitrary", "arbitrary")),
    )(chip, recv, *layer_grads)


def _gather_sums_over_chips(part):
    def body(i_ref, o_ref, send_sems, recv_sems):
        x, y, c = _coords()
        o_ref[2 * x + y] = i_ref[...]

        def copy(f, slot_chip):
            fx, fy = _CHIP_FLIPS[f]
            return pltpu.make_async_remote_copy(
                src_ref=i_ref, dst_ref=o_ref.at[2 * slot_chip[0] + slot_chip[1]], send_sem=send_sems.at[f - 1],
                recv_sem=recv_sems.at[f - 1], device_id=(_flip(x, fx), _flip(y, fy), c), device_id_type=MESH)

        sends = [copy(f, (x, y)) for f in (1, 2, 3)]
        for cp in sends:
            cp.start()
        for f in (1, 2, 3):
            fx, fy = _CHIP_FLIPS[f]
            copy(f, (_flip(x, fx), _flip(y, fy))).wait_recv()
        for cp in sends:
            cp.wait_send()

    vmem = pl.BlockSpec(memory_space=pltpu.VMEM)
    return pl.pallas_call(
        body, name="gather_small_sums", out_shape=jax.ShapeDtypeStruct((N_CHIPS,) + part.shape, part.dtype),
        in_specs=[vmem], out_specs=vmem,
        scratch_shapes=[pltpu.SemaphoreType.DMA((3,)), pltpu.SemaphoreType.DMA((3,))],
    )(part)


def _adamw(w, g, m, v, name):
    bc1 = 1.0 - ADAM_B1 ** ADAM_STEP
    bc2 = 1.0 - ADAM_B2 ** ADAM_STEP

    def body(w_ref, g_ref, m_ref, v_ref, d_ref, mo_ref, vo_ref):
        gv = g_ref[...]
        mn = ADAM_B1 * m_ref[...] + (1.0 - ADAM_B1) * gv
        vn = ADAM_B2 * v_ref[...] + (1.0 - ADAM_B2) * (gv * gv)
        d_ref[...] = -ADAM_LR * ((mn / bc1) / (jnp.sqrt(vn / bc2) + ADAM_EPS) + ADAM_WD * w_ref[...])
        mo_ref[...] = mn
        vo_ref[...] = vn

    sds = jax.ShapeDtypeStruct(w.shape, F32)
    if w.ndim == 2 and w.shape[0] % SUBLANES == 0:
        tr = _pick(w.shape[0], prefs=(256, 128, 64, 32, 16, 8))
        grid, blk = (w.shape[0] // tr,), pl.BlockSpec((tr, w.shape[1]), lambda i: (i, 0))
    else:
        nd = w.ndim
        grid, blk = (1,), pl.BlockSpec(w.shape, lambda i: (0,) * nd)
    return pl.pallas_call(
        body, name=name, out_shape=(sds, sds, sds), grid=grid, in_specs=[blk] * 4, out_specs=(blk,) * 3,
        compiler_params=_cparams(("parallel",)),
    )(w, g, m, v)


_PACK_QUANTUM = 256 * LANES


def _pack(arrs):
    flat = jnp.concatenate([a.reshape(-1).astype(F32) for a in arrs])
    flat = jnp.pad(flat, (0, (-flat.shape[0]) % _PACK_QUANTUM))
    return flat.reshape(-1, LANES)


def _unpack(p, shapes):
    flat = p.reshape(-1)
    out, off = [], 0
    for s in shapes:
        n = int(np.prod(s))
        out.append(flat[off:off + n].reshape(s))
        off += n
    return out


def kernel(*args):
    nw = len(WEIGHTS)
    x, tgt = args[0], args[1 + nw]
    w = dict(zip(WEIGHTS, args[1:1 + nw]))
    m = dict(zip(WEIGHTS, args[2 + nw:2 + 2 * nw]))
    v = dict(zip(WEIGHTS, args[2 + 2 * nw:2 + 3 * nw]))
    _, L, D = x.shape
    chip = 2 * lax.axis_index("x") + lax.axis_index("y")

    big = list(BIG)
    small_sh_shapes = [w[n].shape for n in SMALL_SHARDED]
    nbig = len(big)
    chip1 = chip.reshape(1).astype(jnp.int32)
    axes2 = [BIG[n] - 1 for n in big] + [0]
    shards = [w[n] for n in big] + [_pack([w[n] for n in SMALL_SHARDED])[None]]
    pairs = [(t, l) for t in range(nbig + 1) for l in range(shards[t].shape[0])]
    depth = w['norm_mix_g'].shape[0]
    part_of = lambda t, l: 0 if t == nbig else 2 * _model_layer(big[t], l) + big[t].startswith('ffn')
    flying, token = [], None
    for g in range(2 * depth):
        ids = [k for k, (t, l) in enumerate(pairs) if part_of(t, l) == g]
        ts = [pairs[k][0] for k in ids]
        placed = [_place_quarter(shards[t], pairs[k][1], axes2[t], chip1, F32 if t == nbig else BF16, token)
                  for k, t in zip(ids, ts)]
        lands, send, recv, token = _gather_start(g, placed, [axes2[t] for t in ts], token)
        flying.append((ts, lands, send, recv))

    def wait_group(g, after):
        ts, lands, send, recv = flying[g]
        landed = _gather_wait(g, lands, send, recv, [axes2[t] for t in ts], token if after is None else after)
        return dict(zip(ts, landed))

    first = wait_group(0, None)
    packed = first.pop(nbig).reshape(N_CHIPS, -1, LANES)
    per_chip = [_unpack(packed[s], small_sh_shapes) for s in range(N_CHIPS)]
    wl = dict(w)
    for k, n in enumerate(SMALL_SHARDED):
        wl[n] = jnp.concatenate([per_chip[s][k] for s in range(N_CHIPS)], axis=-1)

    def layer_weights(i, after):
        got = first if i == 0 else wait_group(i, after)
        return {big[t]: a for t, a in got.items()}

    small_shapes = [(w[n].shape[:-1] + (w[n].shape[-1] * N_CHIPS,)) if n in SMALL_SHARDED else w[n].shape
                    for n in SMALL] + [(1,)]
    n_small = sum(int(np.prod(s)) for s in small_shapes)
    pack_rows = -(-n_small // _PACK_QUANTUM) * _PACK_QUANTUM // LANES
    nlayers = [w[n].shape[0] for n in big] + [2]
    halves = [n // 2 for n in nlayers]
    quarters = [tuple(w[n].shape[1:]) for n in big] + [(pack_rows // 2 // N_CHIPS, LANES)]
    wire = [BF16] * nbig + [F32]
    land_now = [lax.empty((halves[t], N_SLOTS) + quarters[t], wire[t]) for t in range(nbig + 1)]
    gparts = [[None] * n for n in nlayers]
    started = []

    def start_scatter(tag, ts, ls, arrays):
        meta = [(axes2[t], l // halves[t], l % halves[t], quarters[t][axes2[t]]) for t, l in zip(ts, ls)]
        send, recv, thru, new_lands, token = _scatter_start(tag, arrays, [land_now[t] for t in ts], meta)
        for t, ln in zip(ts, new_lands):
            land_now[t] = ln
        started.append((send, recv, thru, ts, meta, ls))
        return token

    def on_layer_grads(g, gb):
        ts = [big.index(n) for n in gb]
        return start_scatter(g, ts, [g // 2 if big[t].startswith('ffn') else g // 4 for t in ts],
                             [gb[big[t]] for t in ts])

    loss, dx, gsmall = _local_step(x.reshape(L, D), tgt.reshape(L, D), wl, layer_weights, on_layer_grads)
    gpack = _pack([gsmall[n] for n in SMALL] + [loss.reshape(1)])
    start_scatter(2 * depth, [nbig, nbig], [0, 1], [gpack[:pack_rows // 2], gpack[pack_rows // 2:]])
    landed, sent = _scatter_wait([s[:5] for s in started], land_now)
    for (t, l), g in zip([(t, l) for s in started for t, l in zip(s[3], s[5])], sent):
        gparts[t][l] = g
    gshard = {n: _sum_and_share(landed[t], gparts[t], axes2[t], chip1, "sum_share_" + n) for t, n in enumerate(big)}
    small_sum = _sum_and_share(landed[nbig], gparts[nbig], 0, chip1, "sum_share_small")
    gpack = _gather_sums_over_chips(small_sum).transpose(1, 0, 2, 3).reshape(pack_rows, LANES)
    gs = dict(zip(SMALL + ['loss'], _unpack(gpack, small_shapes)))
    loss = gs.pop('loss').reshape(())
    for n in SMALL_SHARDED:
        width = w[n].shape[-1]
        gs[n] = lax.dynamic_slice_in_dim(gs[n], chip * width, width, axis=gs[n].ndim - 1)

    grads, delta, new_m, new_v = {}, {}, {}, {}
    for n in big:
        shp = w[n].shape
        flat = lambda a: a.reshape(shp[0] * shp[1], shp[2])
        g = gshard[n]
        grads[n] = g
        d_, m_, v_ = _adamw(flat(w[n]), flat(g), flat(m[n]), flat(v[n]), "adamw_" + n)
        delta[n], new_m[n], new_v[n] = d_.reshape(shp), m_.reshape(shp), v_.reshape(shp)
    for n in SMALL:
        shp = w[n].shape
        as2d = (lambda a: a.reshape(1, -1)) if len(shp) == 1 else (lambda a: a)
        d_, m_, v_ = _adamw(as2d(w[n]), as2d(gs[n]), as2d(m[n]), as2d(v[n]), "adamw_" + n)
        grads[n], delta[n], new_m[n], new_v[n] = gs[n], d_.reshape(shp), m_.reshape(shp), v_.reshape(shp)

    return (loss, dx.reshape(1, L, D), *[grads[n] for n in WEIGHTS], *[delta[n] for n in WEIGHTS],
            *[new_m[n] for n in WEIGHTS], *[new_v[n] for n in WEIGHTS])
```

```python
import functools
import math

import numpy as np
import jax
import jax.numpy as jnp
from jax import lax
from jax.experimental import pallas as pl
from jax.experimental.pallas import tpu as pltpu

F32 = jnp.float32
BF16 = jnp.bfloat16
MESH = pl.DeviceIdType.MESH

EPS = 1e-6
CHUNK = 128
POOL_WINDOWS = (2, 4, 8, 16)
LANES = 128
SUBLANES = 8
SCAN_CHUNKS = SUBLANES
S5_GROUPS_PER_STEP = 4
MM_TM_CAP, MM_TN_CAP, MM_TK_CAP = 1408, 1408, 2048
MM_TK_WHOLE = 2048
VMEM_LIMIT = 48 * 1024 * 1024
VMEM_LIMIT_S5 = 56 * 1024 * 1024

ADAM_LR, ADAM_B1, ADAM_B2, ADAM_EPS, ADAM_WD, ADAM_STEP = 0.001, 0.9, 0.999, 1e-08, 0.01, 10

WEIGHTS = ['norm_mix_g', 'even_w_in', 'even_conv_w', 'ssm_log_step', 'ssm_a_re', 'ssm_a_im', 'ssm_b_re',
           'ssm_b_im', 'ssm_c_re', 'ssm_c_im', 'ssm_d', 'ssm_glu_w', 'ssm_glu_b', 'even_w_out', 'odd_w_in',
           'pool_w', 'pool_scale', 'sgu_norm_g', 'sgu_w', 'sgu_b', 'odd_w_out', 'norm_ffn_g', 'ffn_w_up',
           'ffn_conv_w', 'ffn_conv_b', 'ffn_w_down', 'norm_final_g']
BIG = {'even_w_in': 2, 'ssm_glu_w': 1, 'even_w_out': 1, 'odd_w_in': 2, 'odd_w_out': 1, 'ffn_w_up': 2,
       'ffn_w_down': 1}
SMALL_SHARDED = ('even_conv_w', 'pool_scale', 'sgu_norm_g', 'ffn_conv_w')
SMALL = [n for n in WEIGHTS if n not in BIG]
N_CHIPS = 4
N_DEV = 8


def _cparams(sem=None, vmem=VMEM_LIMIT):
    kw = dict(vmem_limit_bytes=vmem)
    if sem is not None:
        kw['dimension_semantics'] = sem
    return pltpu.CompilerParams(**kw)


def _pick(n, segs=(), prefs=(1024, 512, 256, 128)):
    for t in prefs:
        if n % t == 0 and all(s % t == 0 for s in segs if s):
            return t
    return n


def _largest_tile(n, segs, cap):
    best = None
    for t in range(LANES, min(n, cap) + 1, LANES):
        if n % t == 0 and all(s % t == 0 for s in segs if s):
            best = t
    return best if best is not None else n


def _ldims(arr, kind):
    if kind is None:
        return arr.shape
    if kind[0] == 'lead':
        return arr.shape[1:]
    return (arr.shape[1], arr.shape[0] * arr.shape[2])


def _segw(arr, kind):
    return arr.shape[2] if (kind is not None and kind[0] == 'seg') else None


def _opspec(arr, kind, br, bc, rfn, cfn):
    if kind is None:
        return pl.BlockSpec((br, bc), lambda i, j, k: (rfn(i, j, k), cfn(i, j, k)))
    if kind[0] == 'lead':
        lead = kind[1]
        return pl.BlockSpec((None, br, bc), lambda i, j, k: (lead, rfn(i, j, k), cfn(i, j, k)))
    per = arr.shape[2] // bc
    return pl.BlockSpec((None, br, bc), lambda i, j, k: (cfn(i, j, k) // per, rfn(i, j, k), cfn(i, j, k) % per))


def _mm(a, b, mode, out_dtype, name, ak=None, bk=None, ok=None, res=None, dep=None):
    ar, ac = _ldims(a, ak)
    br_, bc_ = _ldims(b, bk)
    if mode == 'nn':
        M, K, N = ar, ac, bc_
        assert br_ == K
    else:
        M, K, N = ar, ac, br_
        assert bc_ == K
    sa, sb = _segw(a, ak), _segw(b, bk)
    so = (N // ok[1]) if ok is not None else None
    tm = _largest_tile(M, [], MM_TM_CAP)
    tn = _largest_tile(N, [sb if mode == 'nn' else None, so], MM_TN_CAP)
    ksegs = [sa, sb if mode == 'nt' else None]
    tk = K if (K <= MM_TK_WHOLE and not any(ksegs)) else _largest_tile(K, ksegs, MM_TK_CAP)
    nk = K // tk
    I = lambda i, j, k: i
    J = lambda i, j, k: j
    Kk = lambda i, j, k: k
    a_spec = _opspec(a, ak, tm, tk, I, Kk)
    if mode == 'nn':
        b_spec = _opspec(b, bk, tk, tn, Kk, J)
        dims = (((1,), (0,)), ((), ()))
    else:
        b_spec = _opspec(b, bk, tn, tk, J, Kk)
        dims = (((1,), (1,)), ((), ()))
    if ok is None:
        out_shape = jax.ShapeDtypeStruct((M, N), out_dtype)
        o_spec = pl.BlockSpec((tm, tn), lambda i, j, k: (i, j))
    else:
        out_shape = jax.ShapeDtypeStruct((ok[1], M, N // ok[1]), out_dtype)
        per = (N // ok[1]) // tn
        o_spec = pl.BlockSpec((None, tm, tn), lambda i, j, k: (j // per, i, j % per))
    has_res = res is not None

    def body(*refs):
        a_ref, b_ref = refs[0], refs[1]
        r_ref = refs[2] if has_res else None
        o_ref = refs[n_in]
        prod = lax.dot_general(a_ref[...].astype(BF16), b_ref[...].astype(BF16), dims, preferred_element_type=F32)
        if nk == 1:
            o_ref[...] = (prod + r_ref[...] if has_res else prod).astype(out_dtype)
            return
        acc = refs[-1]
        k = pl.program_id(2)

        @pl.when(k == 0)
        def _():
            acc[...] = prod

        @pl.when(k > 0)
        def _():
            acc[...] += prod

        @pl.when(k == nk - 1)
        def _():
            o = acc[...]
            if has_res:
                o = o + r_ref[...]
            o_ref[...] = o.astype(out_dtype)

    in_specs = [a_spec, b_spec]
    args = [a, b]
    if has_res:
        in_specs.append(pl.BlockSpec((tm, tn), lambda i, j, k: (i, j)))
        args.append(res)
    if dep is not None:
        in_specs.append(pl.BlockSpec(memory_space=pl.ANY))
        args.append(dep)
    n_in = len(args)
    return pl.pallas_call(
        body, name=name, out_shape=out_shape, grid=(M // tm, N // tn, nk), in_specs=in_specs, out_specs=o_spec,
        scratch_shapes=[pltpu.VMEM((tm, tn), F32)] if nk > 1 else [],
        compiler_params=_cparams(("parallel", "parallel", "arbitrary")),
    )(*args)


_G0 = math.sqrt(2.0 / math.pi)
_G1 = 0.044715


def _gelu(x):
    return 0.5 * x * (1.0 + jnp.tanh(_G0 * (x + _G1 * x * x * x)))


def _gelu_grad(x):
    x2 = x * x
    t = jnp.tanh(_G0 * (x + _G1 * x * x2))
    return 0.5 * (1.0 + t) + 0.5 * x * (1.0 - t * t) * (_G0 * (1.0 + 3.0 * _G1 * x2))


def _sigmoid(x):
    return 1.0 / (1.0 + jnp.exp(-x))


def _down(v, k):
    r = pltpu.roll(v, k, axis=0)
    row = lax.broadcasted_iota(jnp.int32, (SUBLANES, v.shape[1]), 0)
    return jnp.concatenate([jnp.where(row >= k, r[:SUBLANES], 0.0), r[SUBLANES:]], axis=0)


def _up(v, k):
    n = v.shape[0]
    r = pltpu.roll(v, n - k, axis=0)
    row = lax.broadcasted_iota(jnp.int32, (SUBLANES, v.shape[1]), 0)
    return jnp.concatenate([r[:n - SUBLANES], jnp.where(row < SUBLANES - k, r[n - SUBLANES:], 0.0)], axis=0)


def _taps(v):
    return _down(v, 2), _down(v, 1), v


def _conv3(taps, w):
    return w[0:1, :] * taps[0] + w[1:2, :] * taps[1] + w[2:3, :] * taps[2]


def _conv3_t(dv, w):
    return w[2:3, :] * dv + w[1:2, :] * _up(dv, 1) + w[0:1, :] * _up(dv, 2)


def _conv3_dw(dv, taps):
    return tuple(jnp.sum(dv * tp, axis=0, keepdims=True) for tp in taps)


def _cmul(ar, ai, br, bi):
    return ar * br - ai * bi, ar * bi + ai * br


def _cpow(lr, li, n):
    rr = ri = None
    br, bi = lr, li
    while n:
        if n & 1:
            rr, ri = (br, bi) if rr is None else _cmul(rr, ri, br, bi)
        n >>= 1
        if n:
            br, bi = _cmul(br, bi, br, bi)
    return rr, ri


NORM_ROWS = 256


def _norm_mm(x, g, b, out_dtype, name, ok=None):
    M, D = x.shape
    N = b.shape[1]
    so = (N // ok[1]) if ok is not None else None
    tm = _largest_tile(M, [], 1024)
    tn = _largest_tile(N, [so], MM_TN_CAP)
    if ok is None:
        out_shape = jax.ShapeDtypeStruct((M, N), out_dtype)
        o_spec = pl.BlockSpec((tm, tn), lambda i, j: (i, j))
    else:
        out_shape = jax.ShapeDtypeStruct((ok[1], M, N // ok[1]), out_dtype)
        per = (N // ok[1]) // tn
        o_spec = pl.BlockSpec((None, tm, tn), lambda i, j: (j // per, i, j % per))

    def body(x_ref, g_ref, b_ref, o_ref, ht_ref, h_scr):
        @pl.when(pl.program_id(1) == 0)
        def _():
            for c in range(tm // NORM_ROWS):
                rows = pl.ds(c * NORM_ROWS, NORM_ROWS)
                xv = x_ref[rows, :]
                h = xv * lax.rsqrt(jnp.mean(xv * xv, axis=-1, keepdims=True) + EPS) * g_ref[...]
                h_scr[rows, :] = h.astype(BF16)
                ht_ref[:, rows] = h.T.astype(BF16)

        o_ref[...] = jnp.dot(h_scr[...], b_ref[...], preferred_element_type=F32).astype(out_dtype)

    return pl.pallas_call(
        body, name=name, out_shape=(out_shape, jax.ShapeDtypeStruct((D, M), BF16)), grid=(M // tm, N // tn),
        in_specs=[pl.BlockSpec((tm, D), lambda i, j: (i, 0)), pl.BlockSpec((1, D), lambda i, j: (0, 0)),
                  pl.BlockSpec((D, tn), lambda i, j: (0, j))],
        out_specs=(o_spec, pl.BlockSpec((D, tm), lambda i, j: (0, i))),
        scratch_shapes=[pltpu.VMEM((tm, D), BF16)], compiler_params=_cparams(("parallel", "arbitrary")),
    )(x, g.reshape(1, D), b)


def _mm_norm_bwd(a, b, x, g, dres, name, ak=None, dep=None):
    M, K = _ldims(a, ak)
    D = b.shape[0]
    assert b.shape[1] == K and x.shape == (M, D)
    sa = _segw(a, ak)
    tm = _largest_tile(M, [], 1024)
    tk = K if (K <= MM_TK_WHOLE and not sa) else _largest_tile(K, [sa], MM_TK_CAP)
    ni, nk = M // tm, K // tk
    a3 = _opspec(a, ak, tm, tk, lambda i, j, k: i, lambda i, j, k: k)
    a_spec = pl.BlockSpec(a3.block_shape, lambda i, k: a3.index_map(i, 0, k))
    n_in = 5 + (dep is not None)

    def body(*refs):
        a_ref, b_ref, x_ref, g_ref, r_ref = refs[:5]
        dx_ref, dxb_ref, dg_ref, acc, accg = refs[n_in:]
        i, k = pl.program_id(0), pl.program_id(1)
        prod = lax.dot_general(a_ref[...].astype(BF16), b_ref[...], (((1,), (1,)), ((), ())),
                               preferred_element_type=F32)

        @pl.when(k == 0)
        def _():
            acc[...] = prod

        @pl.when(k > 0)
        def _():
            acc[...] += prod

        @pl.when((i == 0) & (k == 0))
        def _():
            accg[...] = jnp.zeros_like(accg)

        @pl.when(k == nk - 1)
        def _():
            for c in range(tm // NORM_ROWS):
                rows = pl.ds(c * NORM_ROWS, NORM_ROWS)
                xv = x_ref[rows, :]
                r = lax.rsqrt(jnp.mean(xv * xv, axis=-1, keepdims=True) + EPS)
                xh = xv * r
                dhv = acc[rows, :]
                accg[...] += jnp.sum((dhv * xh).reshape(NORM_ROWS // SUBLANES, SUBLANES, D), axis=0)
                dxh = dhv * g_ref[...]
                dxv = r_ref[rows, :] + r * (dxh - xh * jnp.mean(dxh * xh, axis=-1, keepdims=True))
                dx_ref[rows, :] = dxv
                dxb_ref[rows, :] = dxv.astype(BF16)

        @pl.when((i == ni - 1) & (k == nk - 1))
        def _():
            dg_ref[...] = jnp.sum(accg[...], axis=0, keepdims=True)

    row = pl.BlockSpec((tm, D), lambda i, k: (i, 0))
    vec = pl.BlockSpec((1, D), lambda i, k: (0, 0))
    in_specs = [a_spec, pl.BlockSpec((D, tk), lambda i, k: (0, k)), row, vec, row]
    args = [a, b, x, g.reshape(1, D), dres]
    if dep is not None:
        in_specs.append(pl.BlockSpec(memory_space=pl.ANY))
        args.append(dep)
    return pl.pallas_call(
        body, name=name,
        out_shape=(jax.ShapeDtypeStruct((M, D), F32), jax.ShapeDtypeStruct((M, D), BF16),
                   jax.ShapeDtypeStruct((1, D), F32)),
        grid=(ni, nk), in_specs=in_specs, out_specs=(row, row, vec),
        scratch_shapes=[pltpu.VMEM((tm, D), F32), pltpu.VMEM((SUBLANES, D), F32)],
        compiler_params=_cparams(("arbitrary", "arbitrary"), VMEM_LIMIT_S5),
    )(*args)


def _loss_head(x, g, tgt):
    L, D = x.shape
    tr = _pick(L, prefs=(512, 256, 128))
    nsteps = L // tr

    def body(x_ref, g_ref, t_ref, loss_ref, dx_ref, dxb_ref, dg_ref, acc_g, acc_l):
        i = pl.program_id(0)

        @pl.when(i == 0)
        def _():
            acc_g[...] = jnp.zeros_like(acc_g)
            acc_l[...] = jnp.zeros_like(acc_l)

        xv = x_ref[...]
        gv = g_ref[...]
        r = lax.rsqrt(jnp.mean(xv * xv, axis=-1, keepdims=True) + EPS)
        xh = xv * r
        e = xh * gv - t_ref[...]
        acc_l[...] += jnp.sum((e * e).reshape(tr // SUBLANES, SUBLANES, D), axis=0)
        dy = e * (1.0 / D)
        acc_g[...] += jnp.sum((dy * xh).reshape(tr // SUBLANES, SUBLANES, D), axis=0)
        dxh = dy * gv
        dxv = r * (dxh - xh * jnp.mean(dxh * xh, axis=-1, keepdims=True))
        dx_ref[...] = dxv
        dxb_ref[...] = dxv.astype(BF16)

        @pl.when(i == nsteps - 1)
        def _():
            dg_ref[...] = jnp.sum(acc_g[...], axis=0, keepdims=True)
            tot = jnp.sum(jnp.sum(acc_l[...], axis=0, keepdims=True), axis=1, keepdims=True) * (0.5 / D)
            loss_ref[...] = jnp.broadcast_to(tot, (SUBLANES, LANES))

    row = pl.BlockSpec((tr, D), lambda i: (i, 0))
    vec = pl.BlockSpec((1, D), lambda i: (0, 0))
    return pl.pallas_call(
        body, name="loss_head",
        out_shape=(jax.ShapeDtypeStruct((SUBLANES, LANES), F32), jax.ShapeDtypeStruct((L, D), F32),
                   jax.ShapeDtypeStruct((L, D), BF16), jax.ShapeDtypeStruct((1, D), F32)),
        grid=(nsteps,), in_specs=[row, vec, row],
        out_specs=(pl.BlockSpec((SUBLANES, LANES), lambda i: (0, 0)), row, row, vec),
        scratch_shapes=[pltpu.VMEM((SUBLANES, D), F32), pltpu.VMEM((SUBLANES, D), F32)],
        compiler_params=_cparams(("arbitrary",)),
    )(x, g.reshape(1, D), tgt)


def _sconv_fwd(proj4, conv_w, name):
    _, L, C = proj4.shape
    cb = LANES

    def body(p_ref, w_ref, o_ref):
        xa, ba, ca = p_ref[0].astype(F32), p_ref[1].astype(F32), p_ref[2].astype(F32)
        o_ref[...] = (ba * _conv3(_taps(ca * xa), w_ref[...])).astype(BF16)

    return pl.pallas_call(
        body, name=name, out_shape=jax.ShapeDtypeStruct((L, C), BF16), grid=(C // cb,),
        in_specs=[pl.BlockSpec((3, L, cb), lambda j: (0, 0, j)), pl.BlockSpec((3, cb), lambda j: (0, j))],
        out_specs=pl.BlockSpec((L, cb), lambda j: (0, j)), compiler_params=_cparams(("parallel",)),
    )(proj4, conv_w)


def _sconv_bwd(proj4, dmix, conv_w, name):
    _, L, C = proj4.shape
    cb = LANES

    def body(p_ref, d_ref, w_ref, o_ref, dw_ref):
        xa, ba, ca = p_ref[0].astype(F32), p_ref[1].astype(F32), p_ref[2].astype(F32)
        w = w_ref[...]
        dya = d_ref[...]
        tq = _taps(ca * xa)
        cq = _conv3(tq, w)
        dcq = dya * ba
        dq = _conv3_t(dcq, w)
        for tap, dwt in enumerate(_conv3_dw(dcq, tq)):
            dw_ref[tap:tap + 1, :] = dwt
        o_ref[0] = (dq * ca).astype(BF16)
        o_ref[1] = (dya * cq).astype(BF16)
        o_ref[2] = (dq * xa).astype(BF16)

    return pl.pallas_call(
        body, name=name,
        out_shape=(jax.ShapeDtypeStruct((3, L, C), BF16), jax.ShapeDtypeStruct((3, C), F32)), grid=(C // cb,),
        in_specs=[pl.BlockSpec((3, L, cb), lambda j: (0, 0, j)), pl.BlockSpec((L, cb), lambda j: (0, j)),
                  pl.BlockSpec((3, cb), lambda j: (0, j))],
        out_specs=(pl.BlockSpec((3, L, cb), lambda j: (0, 0, j)), pl.BlockSpec((3, cb), lambda j: (0, j))),
        compiler_params=_cparams(("parallel",)),
    )(proj4, dmix, conv_w)


def _to_scan_order(v):
    L, C = v.shape
    return v.reshape(SCAN_CHUNKS, L // SCAN_CHUNKS, C).transpose(1, 0, 2).reshape(L, C)


def _from_scan_order(v):
    L, C = v.shape
    return v.reshape(L // SCAN_CHUNKS, SCAN_CHUNKS, C).transpose(1, 0, 2).reshape(L, C)


def _s5_prep(log_step, a_re, a_im, b_re, b_im, c_re, c_im):
    G, P = a_re.shape
    H = b_re.shape[-1]
    gs = S5_GROUPS_PER_STEP
    ns = G // gs
    gu = LANES // H
    lam = lax.complex(a_re, a_im)
    step = jnp.exp(log_step)[:, None]
    lam_bar = jnp.exp(lam * step)
    b_bar = ((lam_bar - 1.0) / lam)[..., None] * lax.complex(b_re, b_im)
    lr = jnp.real(lam_bar).reshape(ns, 1, gs * P)
    li = jnp.imag(lam_bar).reshape(ns, 1, gs * P)
    k = np.arange(ns)[:, None, None]
    oh = jnp.asarray((np.arange(gu)[None, :, None] == gs * (k % (gu // gs)) + np.arange(gs)[None, None, :]),
                     F32)
    bre = jnp.einsum('kgl,klph->kghlp', oh, jnp.real(b_bar).reshape(ns, gs, P, H)).reshape(ns, gu * H, gs * P)
    bim = jnp.einsum('kgl,klph->kghlp', oh, jnp.imag(b_bar).reshape(ns, gs, P, H)).reshape(ns, gu * H, gs * P)
    cre = jnp.einsum('kgl,klhp->klpgh', oh, c_re.reshape(ns, gs, H, P)).reshape(ns, gs * P, gu * H)
    cim = jnp.einsum('kgl,klhp->klpgh', oh, c_im.reshape(ns, gs, H, P)).reshape(ns, gs * P, gu * H)
    return lr, li, jnp.concatenate([bre, bim], axis=2), jnp.concatenate([cre, -cim], axis=1)


def _carry_tile(fr, fi, pr, pi, reverse):
    row = lax.broadcasted_iota(jnp.int32, fr.shape, 0)
    cr = jnp.zeros_like(fr)
    ci = jnp.zeros_like(fi)
    sr = jnp.zeros_like(fr[0:1])
    si = jnp.zeros_like(sr)
    order = range(SCAN_CHUNKS - 1, 0, -1) if reverse else range(0, SCAN_CHUNKS - 1)
    for c in order:
        fcr = jnp.sum(jnp.where(row == c, fr, 0.0), axis=0, keepdims=True)
        fci = jnp.sum(jnp.where(row == c, fi, 0.0), axis=0, keepdims=True)
        mr, mi = _cmul(pr, pi, sr, si)
        sr, si = mr + fcr, mi + fci
        nxt = c - 1 if reverse else c + 1
        cr = jnp.where(row == nxt, sr, cr)
        ci = jnp.where(row == nxt, si, ci)
    return cr, ci


def _s5_fwd(u, lr, li, bmat, cmat, d, name):
    L, Du = u.shape
    ns, _, W2 = bmat.shape
    W = W2 // 2
    T = L // SCAN_CHUNKS
    rb = _pick(L, prefs=(512, 256, 128))
    per = (ns * LANES) // Du

    def body(u_ref, lr_ref, li_ref, b_ref, c_ref, d_ref, y_ref, sr_ref, si_ref):
        k = pl.program_id(0)
        for r in range(L // rb):
            rows = pl.ds(r * rb, rb)
            bu = jnp.dot(u_ref[rows, :].astype(BF16), b_ref[...], preferred_element_type=F32)
            sr_ref[rows, :] = bu[:, :W]
            si_ref[rows, :] = bu[:, W:]
        lam_r = jnp.broadcast_to(lr_ref[...], (SUBLANES, W))
        lam_i = jnp.broadcast_to(li_ref[...], (SUBLANES, W))

        def local(t, carry):
            sr, si = carry
            rows = pl.ds(pl.multiple_of(t * SUBLANES, SUBLANES), SUBLANES)
            mr, mi = _cmul(lam_r, lam_i, sr, si)
            sr = mr + sr_ref[rows, :]
            si = mi + si_ref[rows, :]
            sr_ref[rows, :] = sr
            si_ref[rows, :] = si
            return sr, si

        z = jnp.zeros((SUBLANES, W), F32)
        fr, fi = lax.fori_loop(0, T, local, (z, z))
        pr, pi = _cpow(lam_r, lam_i, T)
        cr, ci = _carry_tile(fr, fi, pr[0:1], pi[0:1], reverse=False)

        def fix(t, carry):
            wr, wi = carry
            rows = pl.ds(pl.multiple_of(t * SUBLANES, SUBLANES), SUBLANES)
            ar, ai = _cmul(wr, wi, cr, ci)
            sr_ref[rows, :] += ar
            si_ref[rows, :] += ai
            return _cmul(wr, wi, lam_r, lam_i)

        lax.fori_loop(0, T, fix, (lam_r, lam_i))
        first = (k % per) == 0
        for r in range(L // rb):
            rows = pl.ds(r * rb, rb)
            s = jnp.concatenate([sr_ref[rows, :], si_ref[rows, :]], axis=1).astype(BF16)
            y = jnp.dot(s, c_ref[...], preferred_element_type=F32)

            @pl.when(first)
            def _():
                y_ref[rows, :] = y + d_ref[...] * u_ref[rows, :]

            @pl.when(jnp.logical_not(first))
            def _():
                y_ref[rows, :] += y

    ublk = pl.BlockSpec((L, LANES), lambda k: (0, k // per))
    sblk = pl.BlockSpec((L, W), lambda k: (0, k))
    lam = pl.BlockSpec((None, 1, W), lambda k: (k, 0, 0))
    return pl.pallas_call(
        body, name=name,
        out_shape=(jax.ShapeDtypeStruct((L, Du), F32), jax.ShapeDtypeStruct((L, ns * W), F32),
                   jax.ShapeDtypeStruct((L, ns * W), F32)),
        grid=(ns,),
        in_specs=[ublk, lam, lam, pl.BlockSpec((None, LANES, 2 * W), lambda k: (k, 0, 0)),
                  pl.BlockSpec((None, 2 * W, LANES), lambda k: (k, 0, 0)),
                  pl.BlockSpec((1, LANES), lambda k: (0, k // per))],
        out_specs=(ublk, sblk, sblk), compiler_params=_cparams(("arbitrary",), VMEM_LIMIT_S5),
    )(u, lr, li, bmat.astype(BF16), cmat.astype(BF16), d.reshape(1, Du))


def _s5_bwd(dy, u, s_re, s_im, lr, li, bmat, cmat, d, name):
    L, Du = u.shape
    ns, _, W2 = bmat.shape
    W = W2 // 2
    T = L // SCAN_CHUNKS
    rb = _pick(L, prefs=(512, 256, 128))
    per = (ns * LANES) // Du
    NT = (((1,), (1,)), ((), ()))
    TN = (((0,), (0,)), ((), ()))

    def body(dy_ref, u_ref, sr_ref, si_ref, lr_ref, li_ref, b_ref, c_ref, d_ref,
             du_ref, db_ref, dc_ref, dl_ref, dd_ref, gr_ref, gi_ref):
        k = pl.program_id(0)
        for r in range(L // rb):
            rows = pl.ds(r * rb, rb)
            g = lax.dot_general(dy_ref[rows, :].astype(BF16), c_ref[...], NT, preferred_element_type=F32)
            gr_ref[rows, :] = g[:, :W]
            gi_ref[rows, :] = g[:, W:]
        lam_r = jnp.broadcast_to(lr_ref[...], (SUBLANES, W))
        lam_i = -jnp.broadcast_to(li_ref[...], (SUBLANES, W))

        def local(i, carry):
            gr, gi = carry
            rows = pl.ds(pl.multiple_of((T - 1 - i) * SUBLANES, SUBLANES), SUBLANES)
            mr, mi = _cmul(lam_r, lam_i, gr, gi)
            gr = mr + gr_ref[rows, :]
            gi = mi + gi_ref[rows, :]
            gr_ref[rows, :] = gr
            gi_ref[rows, :] = gi
            return gr, gi

        z = jnp.zeros((SUBLANES, W), F32)
        fr, fi = lax.fori_loop(0, T, local, (z, z))
        pr, pi = _cpow(lam_r, lam_i, T)
        cr, ci = _carry_tile(fr, fi, pr[0:1], pi[0:1], reverse=True)

        def true_g(rows, wr, wi):
            ar, ai = _cmul(wr, wi, cr, ci)
            gr = gr_ref[rows, :] + ar
            gi = gi_ref[rows, :] + ai
            gr_ref[rows, :] = gr
            gi_ref[rows, :] = gi
            return gr, gi

        def fix(i, carry):
            wr, wi, ar_, ai_ = carry
            t = T - 1 - i
            rows = pl.ds(pl.multiple_of(t * SUBLANES, SUBLANES), SUBLANES)
            prev = pl.ds(pl.multiple_of((t - 1) * SUBLANES, SUBLANES), SUBLANES)
            gr, gi = true_g(rows, wr, wi)
            qr, qi = sr_ref[prev, :], si_ref[prev, :]
            ar_ = ar_ + gr * qr + gi * qi
            ai_ = ai_ + gi * qr - gr * qi
            wr, wi = _cmul(wr, wi, lam_r, lam_i)
            return wr, wi, ar_, ai_

        wr, wi, acc_r, acc_i = lax.fori_loop(0, T - 1, fix, (lam_r, lam_i, z, z))
        gr, gi = true_g(pl.ds(0, SUBLANES), wr, wi)
        last = pl.ds((T - 1) * SUBLANES, SUBLANES)
        row = lax.broadcasted_iota(jnp.int32, (SUBLANES, W), 0)
        qr = jnp.where(row >= 1, pltpu.roll(sr_ref[last, :], 1, axis=0), 0.0)
        qi = jnp.where(row >= 1, pltpu.roll(si_ref[last, :], 1, axis=0), 0.0)
        acc_r = acc_r + gr * qr + gi * qi
        acc_i = acc_i + gi * qr - gr * qi
        dl_ref[0:1, :] = jnp.sum(acc_r, axis=0, keepdims=True)
        dl_ref[1:2, :] = jnp.sum(acc_i, axis=0, keepdims=True)

        first = (k % per) == 0
        db = jnp.zeros((LANES, 2 * W), F32)
        dc = jnp.zeros((LANES, 2 * W), F32)
        dd = jnp.zeros((1, LANES), F32)
        for r in range(L // rb):
            rows = pl.ds(r * rb, rb)
            gb = jnp.concatenate([gr_ref[rows, :], gi_ref[rows, :]], axis=1).astype(BF16)
            sb = jnp.concatenate([sr_ref[rows, :], si_ref[rows, :]], axis=1).astype(BF16)
            dyv = dy_ref[rows, :]
            uv = u_ref[rows, :]
            du = lax.dot_general(gb, b_ref[...], NT, preferred_element_type=F32)
            db = db + lax.dot_general(uv.astype(BF16), gb, TN, preferred_element_type=F32)
            dc = dc + lax.dot_general(dyv.astype(BF16), sb, TN, preferred_element_type=F32)
            dd = dd + jnp.sum(dyv * uv, axis=0, keepdims=True)

            @pl.when(first)
            def _():
                du_ref[rows, :] = du + d_ref[...] * dyv

            @pl.when(jnp.logical_not(first))
            def _():
                du_ref[rows, :] += du

        db_ref[...] = db
        dc_ref[...] = dc

        @pl.when(first)
        def _():
            dd_ref[...] = dd

    ublk = pl.BlockSpec((L, LANES), lambda k: (0, k // per))
    sblk = pl.BlockSpec((L, W), lambda k: (0, k))
    lam = pl.BlockSpec((None, 1, W), lambda k: (k, 0, 0))
    vec = pl.BlockSpec((1, LANES), lambda k: (0, k // per))
    mat = pl.BlockSpec((None, LANES, 2 * W), lambda k: (k, 0, 0))
    return pl.pallas_call(
        body, name=name,
        out_shape=(jax.ShapeDtypeStruct((L, Du), F32), jax.ShapeDtypeStruct((ns, LANES, 2 * W), F32),
                   jax.ShapeDtypeStruct((ns, LANES, 2 * W), F32), jax.ShapeDtypeStruct((ns, 2, W), F32),
                   jax.ShapeDtypeStruct((1, Du), F32)),
        grid=(ns,),
        in_specs=[ublk, ublk, sblk, sblk, lam, lam, mat,
                  pl.BlockSpec((None, 2 * W, LANES), lambda k: (k, 0, 0)), vec],
        out_specs=(ublk, mat, mat, pl.BlockSpec((None, 2, W), lambda k: (k, 0, 0)), vec),
        scratch_shapes=[pltpu.VMEM((L, W), F32), pltpu.VMEM((L, W), F32)],
        compiler_params=_cparams(("arbitrary",), VMEM_LIMIT_S5),
    )(dy, u, s_re, s_im, lr, li, bmat.astype(BF16), cmat.astype(BF16), d.reshape(1, Du))


def _glu_fwd(yraw, wmat, bias, name):
    L, C = yraw.shape
    tr = _pick(L, prefs=(512, 256, 128))

    def body(y_ref, w_ref, b_ref, o_ref):
        yg = _gelu(y_ref[...])
        zz = jnp.dot(yg.astype(BF16), w_ref[...], preferred_element_type=F32) + b_ref[...]
        o_ref[...] = (yg * _sigmoid(zz)).astype(BF16)

    return pl.pallas_call(
        body, name=name, out_shape=jax.ShapeDtypeStruct((L, C), BF16), grid=(L // tr,),
        in_specs=[pl.BlockSpec((tr, C), lambda i: (i, 0)), pl.BlockSpec((C, C), lambda i: (0, 0)),
                  pl.BlockSpec((1, C), lambda i: (0, 0))],
        out_specs=pl.BlockSpec((tr, C), lambda i: (i, 0)), compiler_params=_cparams(("parallel",)),
    )(yraw, wmat, bias.reshape(1, C))


def _glu_bwd(yraw, dyb, wmat, bias, name):
    L, C = yraw.shape
    tr = _pick(L, prefs=(512, 256, 128))
    nsteps = L // tr

    def body(y_ref, d_ref, w_ref, b_ref, dy_ref, dw_ref, db_ref, acc_b):
        i = pl.program_id(0)

        @pl.when(i == 0)
        def _():
            dw_ref[...] = jnp.zeros_like(dw_ref)
            acc_b[...] = jnp.zeros_like(acc_b)

        yr = y_ref[...]
        yg = _gelu(yr)
        ygb = yg.astype(BF16)
        sg = _sigmoid(jnp.dot(ygb, w_ref[...], preferred_element_type=F32) + b_ref[...])
        dyb_ = d_ref[...]
        dz = dyb_ * yg * sg * (1.0 - sg)
        dzb = dz.astype(BF16)
        dyg = dyb_ * sg + lax.dot_general(dzb, w_ref[...], (((1,), (1,)), ((), ())), preferred_element_type=F32)
        dw_ref[...] += lax.dot_general(ygb, dzb, (((0,), (0,)), ((), ())), preferred_element_type=F32)
        acc_b[...] += jnp.sum(dz.reshape(tr // SUBLANES, SUBLANES, C), axis=0)
        dy_ref[...] = dyg * _gelu_grad(yr)

        @pl.when(i == nsteps - 1)
        def _():
            db_ref[...] = jnp.sum(acc_b[...], axis=0, keepdims=True)

    row = pl.BlockSpec((tr, C), lambda i: (i, 0))
    return pl.pallas_call(
        body, name=name,
        out_shape=(jax.ShapeDtypeStruct((L, C), F32), jax.ShapeDtypeStruct((C, C), F32),
                   jax.ShapeDtypeStruct((1, C), F32)),
        grid=(nsteps,),
        in_specs=[row, row, pl.BlockSpec((C, C), lambda i: (0, 0)), pl.BlockSpec((1, C), lambda i: (0, 0))],
        out_specs=(row, pl.BlockSpec((C, C), lambda i: (0, 0)), pl.BlockSpec((1, C), lambda i: (0, 0))),
        scratch_shapes=[pltpu.VMEM((SUBLANES, C), F32)], compiler_params=_cparams(("arbitrary",)),
    )(yraw, dyb, wmat, bias.reshape(1, C))


def _pool_counts(L, g):
    t = lax.broadcasted_iota(jnp.int32, (L, LANES), 0).astype(F32) + 1.0
    w = jnp.where(g == 0, 2.0, jnp.where(g == 1, 4.0, jnp.where(g == 2, 8.0, 16.0)))
    return 1.0 / jnp.minimum(t, w)


def _select_window(g, a2, a4, a8, a16):
    return jnp.where(g == 0, a2, jnp.where(g == 1, a4, jnp.where(g == 2, a8, a16)))


def _pooled(z, g):
    a2 = z + _down(z, 1)
    a4 = a2 + _down(a2, 2)
    a8 = a4 + _down(a4, 4)
    a16 = a8 + _down(a8, 8)
    return _select_window(g, a2, a4, a8, a16) * _pool_counts(z.shape[0], g) - z


def _transpose_on_mxu(yb):
    c = yb.shape[1]
    eye = lax.broadcasted_iota(jnp.int32, (c, c), 0) == lax.broadcasted_iota(jnp.int32, (c, c), 1)
    return lax.dot_general(eye.astype(BF16), yb, (((1,), (1,)), ((), ())), preferred_element_type=F32).astype(BF16)


def _pool_fwd(proj3, pool_w, scale, name):
    _, L, C = proj3.shape
    ng = len(POOL_WINDOWS)
    pg = C // ng
    assert pg == LANES

    def body(z_ref, w_ref, s_ref, o_ref, ot_ref):
        g = pl.program_id(0)
        p = _pooled(z_ref[...].astype(F32), g)
        y = jnp.dot(p.astype(BF16), w_ref[...].astype(BF16), preferred_element_type=F32)
        yb = (y * s_ref[...]).astype(BF16)
        o_ref[...] = yb
        ot_ref[...] = _transpose_on_mxu(yb)

    return pl.pallas_call(
        body, name=name, out_shape=(jax.ShapeDtypeStruct((L, 2 * C), BF16), jax.ShapeDtypeStruct((2 * C, L), BF16)),
        grid=(ng,),
        in_specs=[pl.BlockSpec((None, L, pg), lambda g: (0, 0, g)), pl.BlockSpec((None, pg, pg), lambda g: (g, 0, 0)),
                  pl.BlockSpec((1, pg), lambda g: (0, g))],
        out_specs=(pl.BlockSpec((L, pg), lambda g: (0, g)), pl.BlockSpec((pg, L), lambda g: (g, 0))),
        compiler_params=_cparams(("parallel",)),
    )(proj3, pool_w, scale.reshape(1, C))


def _pool_bwd(proj3, dmix, pool_w, scale, name):
    _, L, C = proj3.shape
    ng = len(POOL_WINDOWS)
    pg = C // ng

    def body(z_ref, d_ref, w_ref, s_ref, dz_ref, dw_ref, ds_ref):
        g = pl.program_id(0)
        p = _pooled(z_ref[...].astype(F32), g)
        pb = p.astype(BF16)
        wb = w_ref[...].astype(BF16)
        pre = jnp.dot(pb, wb, preferred_element_type=F32)
        dyc = d_ref[...]
        ds_ref[...] = jnp.sum(dyc * pre, axis=0, keepdims=True)
        dpre = (dyc * s_ref[...]).astype(BF16)
        dw_ref[...] = lax.dot_general(pb, dpre, (((0,), (0,)), ((), ())), preferred_element_type=F32)
        dp = lax.dot_general(dpre, wb, (((1,), (1,)), ((), ())), preferred_element_type=F32)
        v = dp * _pool_counts(L, g)
        a2 = v + _up(v, 1)
        a4 = a2 + _up(a2, 2)
        a8 = a4 + _up(a4, 4)
        a16 = a8 + _up(a8, 8)
        dz_ref[...] = (_select_window(g, a2, a4, a8, a16) - dp).astype(BF16)

    return pl.pallas_call(
        body, name=name,
        out_shape=(jax.ShapeDtypeStruct((L, C), BF16), jax.ShapeDtypeStruct((ng, pg, pg), F32),
                   jax.ShapeDtypeStruct((1, C), F32)),
        grid=(ng,),
        in_specs=[pl.BlockSpec((None, L, pg), lambda g: (0, 0, g)), pl.BlockSpec((L, pg), lambda g: (0, g)),
                  pl.BlockSpec((None, pg, pg), lambda g: (g, 0, 0)), pl.BlockSpec((1, pg), lambda g: (0, g))],
        out_specs=(pl.BlockSpec((L, pg), lambda g: (0, g)), pl.BlockSpec((None, pg, pg), lambda g: (g, 0, 0)),
                   pl.BlockSpec((1, pg), lambda g: (0, g))),
        compiler_params=_cparams(("parallel",)),
    )(proj3, dmix, pool_w, scale.reshape(1, C))


def _tril_w(w_ref, h):
    r = lax.broadcasted_iota(jnp.int32, (CHUNK, CHUNK), 0)
    c = lax.broadcasted_iota(jnp.int32, (CHUNK, CHUNK), 1)
    return jnp.where(r >= c, w_ref[h], 0.0)


def _sgu_fwd(proj3, norm_g, w, b, mixin, mixin_t, name):
    _, L, C = proj3.shape
    nh = w.shape[0]
    dh = C // nh
    assert dh == LANES and w.shape[1] == CHUNK
    tr = _pick(L, prefs=(512, 256, 128))
    bfull = jnp.broadcast_to(b[:, :, None], (nh, CHUNK, dh))

    def body(su_ref, sv_ref, g_ref, w_ref, b_ref, m_in, mt_in, o_ref, ot_ref):
        sv = _gelu(sv_ref[...].astype(F32))
        r = lax.rsqrt(jnp.mean(sv * sv, axis=-1, keepdims=True) + EPS)
        v = (sv * r * g_ref[...]).astype(BF16)
        for h in range(nh):
            wm = _tril_w(w_ref, h).astype(BF16)
            cols = slice(h * dh, (h + 1) * dh)
            for n in range(tr // CHUNK):
                rows = slice(n * CHUNK, (n + 1) * CHUNK)
                mixed = jnp.dot(wm, v[rows, cols], preferred_element_type=F32) + b_ref[h]
                o_ref[rows, cols] = (_gelu(su_ref[rows, cols].astype(F32)) * mixed).astype(BF16)
        ot_ref[...] = _transpose_on_mxu(o_ref[...])

    full = lambda shp: pl.BlockSpec(shp, lambda i: (0,) * len(shp))
    anywhere = pl.BlockSpec(memory_space=pl.ANY)
    return pl.pallas_call(
        body, name=name, out_shape=(jax.ShapeDtypeStruct(mixin.shape, BF16), jax.ShapeDtypeStruct(mixin_t.shape, BF16)),
        grid=(L // tr,),
        in_specs=[pl.BlockSpec((None, tr, C), lambda i: (1, i, 0)), pl.BlockSpec((None, tr, C), lambda i: (2, i, 0)),
                  full((1, C)), full((nh, CHUNK, CHUNK)), full((nh, CHUNK, dh)), anywhere, anywhere],
        out_specs=(pl.BlockSpec((tr, C), lambda i: (i, 1)), pl.BlockSpec((C, tr), lambda i: (1, i))),
        input_output_aliases={5: 0, 6: 1}, compiler_params=_cparams(("parallel",)),
    )(proj3, proj3, norm_g.reshape(1, C), w, bfull, mixin, mixin_t)


def _sgu_bwd(proj3, dmix, norm_g, w, b, name):
    _, L, C = proj3.shape
    nh = w.shape[0]
    dh = C // nh
    tr = _pick(L, prefs=(512, 256, 128))
    nsteps = L // tr
    bfull = jnp.broadcast_to(b[:, :, None], (nh, CHUNK, dh))

    def body(su_ref, sv_ref, d_ref, g_ref, w_ref, b_ref, o_ref, dw_ref, db_ref, dg_ref, dv_ref, acc_g):
        i = pl.program_id(0)

        @pl.when(i == 0)
        def _():
            dw_ref[...] = jnp.zeros_like(dw_ref)
            db_ref[...] = jnp.zeros_like(db_ref)
            acc_g[...] = jnp.zeros_like(acc_g)

        svp = sv_ref[...].astype(F32)
        sv = _gelu(svp)
        r = lax.rsqrt(jnp.mean(sv * sv, axis=-1, keepdims=True) + EPS)
        vh = sv * r
        gv = g_ref[...]
        v = (vh * gv).astype(BF16)
        tri_r = lax.broadcasted_iota(jnp.int32, (CHUNK, CHUNK), 0)
        tri_c = lax.broadcasted_iota(jnp.int32, (CHUNK, CHUNK), 1)
        for h in range(nh):
            wm = _tril_w(w_ref, h).astype(BF16)
            cols = slice(h * dh, (h + 1) * dh)
            dwh = jnp.zeros((CHUNK, CHUNK), F32)
            dbh = jnp.zeros((CHUNK, dh), F32)
            for n in range(tr // CHUNK):
                rows = slice(n * CHUNK, (n + 1) * CHUNK)
                vb = v[rows, cols]
                mixed = jnp.dot(wm, vb, preferred_element_type=F32) + b_ref[h]
                sup = su_ref[rows, cols].astype(F32)
                dyd = d_ref[rows, cols]
                dmx = dyd * _gelu(sup)
                o_ref[0, rows, cols] = (dyd * mixed * _gelu_grad(sup)).astype(BF16)
                dmb = dmx.astype(BF16)
                dwh = dwh + lax.dot_general(dmb, vb, (((1,), (1,)), ((), ())), preferred_element_type=F32)
                dbh = dbh + dmx
                dv_ref[rows, cols] = lax.dot_general(wm, dmb, (((0,), (0,)), ((), ())), preferred_element_type=F32)
            dw_ref[h] += jnp.where(tri_r >= tri_c, dwh, 0.0)
            db_ref[h] += dbh
        dv = dv_ref[...]
        acc_g[...] += jnp.sum((dv * vh).reshape(tr // SUBLANES, SUBLANES, C), axis=0)
        dvg = dv * gv
        dsv = r * (dvg - vh * jnp.mean(dvg * vh, axis=-1, keepdims=True))
        o_ref[1] = (dsv * _gelu_grad(svp)).astype(BF16)

        @pl.when(i == nsteps - 1)
        def _():
            dg_ref[...] = jnp.sum(acc_g[...], axis=0, keepdims=True)

    full = lambda shp: pl.BlockSpec(shp, lambda i: (0,) * len(shp))
    return pl.pallas_call(
        body, name=name,
        out_shape=(jax.ShapeDtypeStruct((2, L, C), BF16), jax.ShapeDtypeStruct((nh, CHUNK, CHUNK), F32),
                   jax.ShapeDtypeStruct((nh, CHUNK, dh), F32), jax.ShapeDtypeStruct((1, C), F32)),
        grid=(nsteps,),
        in_specs=[pl.BlockSpec((None, tr, C), lambda i: (1, i, 0)), pl.BlockSpec((None, tr, C), lambda i: (2, i, 0)),
                  pl.BlockSpec((tr, C), lambda i: (i, 1)), full((1, C)), full((nh, CHUNK, CHUNK)),
                  full((nh, CHUNK, dh))],
        out_specs=(pl.BlockSpec((2, tr, C), lambda i: (0, i, 0)), full((nh, CHUNK, CHUNK)), full((nh, CHUNK, dh)),
                   full((1, C))),
        scratch_shapes=[pltpu.VMEM((tr, C), F32), pltpu.VMEM((SUBLANES, C), F32)],
        compiler_params=_cparams(("arbitrary",)),
    )(proj3, proj3, dmix, norm_g.reshape(1, C), w, bfull)


def _ffn_act_fwd(up3, conv_w, conv_b, name):
    _, L, Fh = up3.shape
    cb = LANES
    w2 = conv_w.reshape(3, 2, Fh).transpose(1, 0, 2)
    b2 = conv_b.reshape(2, 1, Fh)

    def body(u_ref, w_ref, b_ref, o_ref, ot_ref, gv_ref):
        g = _conv3(_taps(u_ref[0].astype(F32)), w_ref[0]) + b_ref[0]
        v = _conv3(_taps(u_ref[1].astype(F32)), w_ref[1]) + b_ref[1]
        gv_ref[0] = g.astype(BF16)
        gv_ref[1] = v.astype(BF16)
        ab = (g * _sigmoid(g) * v).astype(BF16)
        o_ref[...] = ab
        ot_ref[...] = _transpose_on_mxu(ab)

    blk3 = pl.BlockSpec((2, L, cb), lambda j: (0, 0, j))
    return pl.pallas_call(
        body, name=name,
        out_shape=(jax.ShapeDtypeStruct((L, Fh), BF16), jax.ShapeDtypeStruct((Fh, L), BF16),
                   jax.ShapeDtypeStruct((2, L, Fh), BF16)),
        grid=(Fh // cb,),
        in_specs=[blk3, pl.BlockSpec((2, 3, cb), lambda j: (0, 0, j)), pl.BlockSpec((2, 1, cb), lambda j: (0, 0, j))],
        out_specs=(pl.BlockSpec((L, cb), lambda j: (0, j)), pl.BlockSpec((cb, L), lambda j: (j, 0)), blk3),
        compiler_params=_cparams(("parallel",)),
    )(up3, w2, b2)


def _ffn_act_bwd(up3, gv3, da, conv_w, name):
    _, L, Fh = up3.shape
    cb = LANES
    w2 = conv_w.reshape(3, 2, Fh).transpose(1, 0, 2)

    def body(u_ref, gv_ref, d_ref, w_ref, o_ref, dw_ref, db_ref):
        tg, tv = _taps(u_ref[0].astype(F32)), _taps(u_ref[1].astype(F32))
        wg, wv = w_ref[0], w_ref[1]
        g = gv_ref[0].astype(F32)
        v = gv_ref[1].astype(F32)
        sg = _sigmoid(g)
        dav = d_ref[...].astype(F32)
        dg = dav * v * (sg * (1.0 + g * (1.0 - sg)))
        dv = dav * (g * sg)
        o_ref[0] = _conv3_t(dg, wg).astype(BF16)
        o_ref[1] = _conv3_t(dv, wv).astype(BF16)
        for tap, (dwg, dwv) in enumerate(zip(_conv3_dw(dg, tg), _conv3_dw(dv, tv))):
            dw_ref[0, tap:tap + 1, :] = dwg
            dw_ref[1, tap:tap + 1, :] = dwv
        db_ref[0] = jnp.sum(dg, axis=0, keepdims=True)
        db_ref[1] = jnp.sum(dv, axis=0, keepdims=True)

    dup, dw2, db2 = pl.pallas_call(
        body, name=name,
        out_shape=(jax.ShapeDtypeStruct((2, L, Fh), BF16), jax.ShapeDtypeStruct((2, 3, Fh), F32),
                   jax.ShapeDtypeStruct((2, 1, Fh), F32)),
        grid=(Fh // cb,),
        in_specs=[pl.BlockSpec((2, L, cb), lambda j: (0, 0, j)), pl.BlockSpec((2, L, cb), lambda j: (0, 0, j)),
                  pl.BlockSpec((L, cb), lambda j: (0, j)), pl.BlockSpec((2, 3, cb), lambda j: (0, 0, j))],
        out_specs=(pl.BlockSpec((2, L, cb), lambda j: (0, 0, j)), pl.BlockSpec((2, 3, cb), lambda j: (0, 0, j)),
                   pl.BlockSpec((2, 1, cb), lambda j: (0, 0, j))),
        compiler_params=_cparams(("parallel",)),
    )(up3, gv3, da, w2)
    return dup, dw2.transpose(1, 0, 2).reshape(3, 2 * Fh), db2.reshape(2 * Fh)


def _local_step(x, tgt, w, layer_weights, on_layer_grads):
    L, D = x.shape
    depth = w['norm_mix_g'].shape[0]
    saved = []
    for i in range(depth):
        j = i // 2
        wb = dict(layer_weights(2 * i, x))
        s = {'x': x, 'wb': wb}
        if i % 2 == 0:
            proj4, s['hT'] = _norm_mm(x, w['norm_mix_g'][i], wb['even_w_in'], BF16, "even_in_fwd", ok=('seg', 4))
            s['proj'] = proj4
            ya = _sconv_fwd(proj4, w['even_conv_w'][j], "sconv_fwd")
            prm = (w['ssm_log_step'][j], w['ssm_a_re'][j], w['ssm_a_im'][j], w['ssm_b_re'][j], w['ssm_b_im'][j],
                   w['ssm_c_re'][j], w['ssm_c_im'][j])
            (lr, li, bmat, cmat), prep_vjp = jax.vjp(_s5_prep, *prm)
            u = _to_scan_order(proj4[3])
            yraw, s_re, s_im = _s5_fwd(u, lr, li, bmat, cmat, w['ssm_d'][j], "s5_fwd")
            yb = _glu_fwd(yraw, wb['ssm_glu_w'], w['ssm_glu_b'][j], "glu_fwd")
            s.update(u=u, yraw=yraw, s_re=s_re, s_im=s_im, s5=(lr, li, bmat, cmat), prep_vjp=prep_vjp)
            mixin = jnp.concatenate([ya, _from_scan_order(yb)], axis=1)
            s['mixinT'] = mixin.T
            x = _mm(mixin, wb['even_w_out'], 'nn', F32, "even_out_fwd", res=x)
        else:
            proj3, s['hT'] = _norm_mm(x, w['norm_mix_g'][i], wb['odd_w_in'], BF16, "odd_in_fwd", ok=('seg', 3))
            s['proj'] = proj3
            mixin, mixin_t = _pool_fwd(proj3, w['pool_w'][j], w['pool_scale'][j], "pool_fwd")
            mixin, s['mixinT'] = _sgu_fwd(proj3, w['sgu_norm_g'][j], w['sgu_w'][j], w['sgu_b'][j], mixin, mixin_t,
                                          "sgu_fwd")
            x = _mm(mixin, wb['odd_w_out'], 'nn', F32, "odd_out_fwd", res=x)
        s['x1'] = x
        wb.update(layer_weights(2 * i + 1, x))
        up3, h2t = _norm_mm(x, w['norm_ffn_g'][i], wb['ffn_w_up'], BF16, "ffn_up_fwd", ok=('seg', 2))
        a, at, gv3 = _ffn_act_fwd(up3, w['ffn_conv_w'][i], w['ffn_conv_b'][i], "ffn_act_fwd")
        x = _mm(a, wb['ffn_w_down'], 'nn', F32, "ffn_down_fwd", res=x)
        s.update(h2T=h2t, up3=up3, aT=at, gv3=gv3)
        saved.append(s)

    loss8, dx, dxb, dg_final = _loss_head(x, w['norm_final_g'], tgt)
    gs = {n: [None] * w[n].shape[0] for n in SMALL if n != 'norm_final_g'}
    gs['norm_final_g'] = dg_final.reshape(D)

    dep = None
    for i in reversed(range(depth)):
        j = i // 2
        s = saved[i]
        wb = s['wb']
        gb = {}
        da = _mm(dxb, wb['ffn_w_down'], 'nt', BF16, "ffn_down_dgrad", dep=dep)
        gb['ffn_w_down'] = _mm(s['aT'], dxb, 'nn', BF16, "ffn_down_wgrad")
        dup3, dcw, dcb = _ffn_act_bwd(s['up3'], s['gv3'], da, w['ffn_conv_w'][i], "ffn_act_bwd")
        gs['ffn_conv_w'][i], gs['ffn_conv_b'][i] = dcw, dcb
        gb['ffn_w_up'] = _mm(s['h2T'], dup3, 'nn', BF16, "ffn_up_wgrad", bk=('seg', 2))
        dep = on_layer_grads(2 * i + 1, gb)
        dx, dxb, dg = _mm_norm_bwd(dup3, wb['ffn_w_up'], s['x1'], w['norm_ffn_g'][i], dx, "ffn_up_dgrad",
                              ak=('seg', 2), dep=dep)
        gs['norm_ffn_g'][i] = dg.reshape(D)
        gb = {}
        if i % 2 == 0:
            dmix = _mm(dxb, wb['even_w_out'], 'nt', F32, "even_out_dgrad")
            gb['even_w_out'] = _mm(s['mixinT'], dxb, 'nn', BF16, "even_out_wgrad")
            dpc, dcw = _sconv_bwd(s['proj'], dmix, w['even_conv_w'][j], "sconv_bwd")
            gs['even_conv_w'][j] = dcw
            dyb = _to_scan_order(dmix[:, D // 2:])
            dyraw, dglu_w, dglu_b = _glu_bwd(s['yraw'], dyb, wb['ssm_glu_w'], w['ssm_glu_b'][j], "glu_bwd")
            gb['ssm_glu_w'] = dglu_w.astype(BF16)
            gs['ssm_glu_b'][j] = dglu_b.reshape(-1)
            lr, li, bmat, cmat = s['s5']
            du, dbm, dcm, dlam, dd = _s5_bwd(dyraw, s['u'], s['s_re'], s['s_im'], lr, li, bmat, cmat,
                                            w['ssm_d'][j], "s5_bwd")
            gs['ssm_d'][j] = dd.reshape(-1)
            dcm = jnp.swapaxes(dcm, 1, 2)
            dprm = s['prep_vjp']((dlam[:, 0:1, :], dlam[:, 1:2, :], dbm, dcm))
            for n, gval in zip(('ssm_log_step', 'ssm_a_re', 'ssm_a_im', 'ssm_b_re', 'ssm_b_im', 'ssm_c_re',
                                'ssm_c_im'), dprm):
                gs[n][j] = gval
            dproj = jnp.concatenate([dpc, _from_scan_order(du).astype(BF16)[None]], axis=0)
            gb['even_w_in'] = _mm(s['hT'], dproj, 'nn', BF16, "even_in_wgrad", bk=('seg', 4))
            w_in, in_kind, in_name = wb['even_w_in'], ('seg', 4), "even_in_dgrad"
        else:
            dmix = _mm(dxb, wb['odd_w_out'], 'nt', F32, "odd_out_dgrad")
            gb['odd_w_out'] = _mm(s['mixinT'], dxb, 'nn', BF16, "odd_out_wgrad")
            dz, dpw, dps = _pool_bwd(s['proj'], dmix, w['pool_w'][j], w['pool_scale'][j], "pool_bwd")
            gs['pool_w'][j], gs['pool_scale'][j] = dpw, dps.reshape(-1)
            dsuv, dsw, dsb, dsg = _sgu_bwd(s['proj'], dmix, w['sgu_norm_g'][j], w['sgu_w'][j], w['sgu_b'][j],
                                           "sgu_bwd")
            gs['sgu_w'][j], gs['sgu_b'][j], gs['sgu_norm_g'][j] = dsw, jnp.sum(dsb, axis=-1), dsg.reshape(-1)
            dproj = jnp.concatenate([dz[None], dsuv], axis=0)
            gb['odd_w_in'] = _mm(s['hT'], dproj, 'nn', BF16, "odd_in_wgrad", bk=('seg', 3))
            w_in, in_kind, in_name = wb['odd_w_in'], ('seg', 3), "odd_in_dgrad"
        dep = on_layer_grads(2 * i, gb)
        dx, dxb, dg = _mm_norm_bwd(dproj, w_in, s['x'], w['norm_mix_g'][i], dx, in_name, ak=in_kind, dep=dep)
        gs['norm_mix_g'][i] = dg.reshape(D)

    gsmall = {n: (v if n == 'norm_final_g' else jnp.stack(v)) for n, v in gs.items()}
    return loss8[0, 0], dx, gsmall


_HBM = pl.BlockSpec(memory_space=pltpu.HBM)
_CHIP_FLIPS = ((0, 0), (1, 0), (0, 1), (1, 1))


def _coords():
    return lax.axis_index("x"), lax.axis_index("y"), lax.axis_index("c")


def _flip(v, f):
    return 1 - v if f else v


def _shard_of(ref, axis, s, width):
    start = pl.multiple_of(s * width, LANES if axis == ref.ndim - 1 else 16) if width % 16 == 0 else s * width
    idx = [slice(None)] * ref.ndim
    idx[axis] = pl.ds(start, width)
    return ref.at[tuple(idx)]


_SEM = pl.BlockSpec(memory_space=pltpu.SEMAPHORE)
_ANY = pl.BlockSpec(memory_space=pl.ANY)
_DATAFLOW = pltpu.SideEffectType.DATAFLOW_SIDE_EFFECTING


def _in_hbm(a):
    return pltpu.with_memory_space_constraint(a, pltpu.HBM)


def _model_layer(name, l):
    if name.startswith('ffn'):
        return l
    return 2 * l + 1 if name.startswith('odd') else 2 * l


def _place_quarter(shard, l, axis, chip, dtype, dep=None):
    _, r, c = shard.shape
    tr = _pick(r, prefs=(512, 256, 128, 64, 32, 16))
    nrb = r // tr

    def body(chip_ref, i_ref, *rest):
        rest[-1][...] = i_ref[...].astype(dtype)

    if axis == 1:
        out_shape, o_map = (r, c * N_CHIPS), (lambda i, s: (i, s[0]))
    else:
        out_shape, o_map = (r * N_CHIPS, c), (lambda i, s: (s[0] * nrb + i, 0))
    in_specs = [pl.BlockSpec((None, tr, c), lambda i, s: (l, i, 0))]
    args = [chip, shard]
    if dep is not None:
        in_specs.append(pl.BlockSpec(memory_space=pl.ANY))
        args.append(dep)
    return pl.pallas_call(
        body, name="place_quarter", out_shape=jax.ShapeDtypeStruct(out_shape, dtype),
        grid_spec=pltpu.PrefetchScalarGridSpec(
            num_scalar_prefetch=1, grid=(nrb,), in_specs=in_specs, out_specs=pl.BlockSpec((tr, c), o_map)),
        compiler_params=_cparams(("parallel",)),
    )(*args)


def _gather_copies(land_refs, send_sem, recv_sem, axes, landing_chip_of):
    x, y, c = _coords()
    out = []
    for j, land in enumerate(land_refs):
        width = land.shape[axes[j]] // N_CHIPS
        for f in (1, 2, 3):
            fx, fy = _CHIP_FLIPS[f]
            px, py = _flip(x, fx), _flip(y, fy)
            lx, ly = landing_chip_of(px, py)
            out.append(pltpu.make_async_remote_copy(
                src_ref=_shard_of(land, axes[j], 2 * x + y, width), dst_ref=_shard_of(land, axes[j], 2 * lx + ly, width),
                send_sem=send_sem.at[3 * j + f - 1], recv_sem=recv_sem.at[3 * j + f - 1],
                device_id=(px, py, c), device_id_type=MESH))
    return out


def _gather_start(tag, lands, axes, dep=None):
    n = len(lands)

    def body(*refs):
        land_refs, send_sem, recv_sem = refs[:n], refs[-3], refs[-2]
        x, y, _ = _coords()
        for cp in _gather_copies(land_refs, send_sem, recv_sem, axes, lambda px, py: (x, y)):
            cp.start()
        refs[-1][...] = jnp.zeros_like(refs[-1])

    thru = [pltpu.HBM(a.shape, a.dtype) for a in lands]
    outs = pl.pallas_call(
        body, name=f"gather_start_{tag}",
        out_shape=tuple(thru + [pltpu.SemaphoreType.DMA((3 * n,)), pltpu.SemaphoreType.DMA((3 * n,)),
                                jax.ShapeDtypeStruct((SUBLANES, LANES), F32)]),
        in_specs=[_HBM] * n + ([_ANY] if dep is not None else []),
        out_specs=tuple([_HBM] * n + [_SEM, _SEM, pl.BlockSpec(memory_space=pltpu.VMEM)]),
        input_output_aliases={i: i for i in range(n)},
        compiler_params=pltpu.CompilerParams(has_side_effects=_DATAFLOW),
    )(*[_in_hbm(a) for a in lands], *([dep] if dep is not None else []))
    return list(outs[:n]), outs[n], outs[n + 1], outs[n + 2]


def _gather_wait(tag, lands, send_sem, recv_sem, axes, after):
    n = len(lands)

    def body(*refs):
        for cp in _gather_copies(refs[:n], refs[n], refs[n + 1], axes, lambda px, py: (px, py)):
            cp.wait_send()
            cp.wait_recv()

    outs = pl.pallas_call(
        body, name=f"gather_wait_{tag}", out_shape=tuple(pltpu.HBM(a.shape, a.dtype) for a in lands),
        in_specs=[_HBM] * n + [_SEM, _SEM, _ANY], out_specs=tuple([_HBM] * n),
        input_output_aliases={i: i for i in range(n)},
        compiler_params=pltpu.CompilerParams(has_side_effects=_DATAFLOW),
    )(*lands, send_sem, recv_sem, after)
    return list(outs)


N_SLOTS = N_DEV - 1


def _scatter_sends(grad_refs, land_refs, send_sem, recv_sem, meta):
    x, y, c = _coords()
    out = []
    for j, (axis, owner, q, width) in enumerate(meta):
        other = c if owner == 0 else 1 - c
        for f, (fx, fy) in enumerate(_CHIP_FLIPS):
            px, py = _flip(x, fx), _flip(y, fy)
            slot = f + 4 * other - 1
            out.append((other if f == 0 else None, pltpu.make_async_remote_copy(
                src_ref=_shard_of(grad_refs[j], axis, 2 * px + py, width), dst_ref=land_refs[j].at[q, slot],
                send_sem=send_sem.at[4 * j + f], recv_sem=recv_sem.at[N_SLOTS * j + slot],
                device_id=(px, py, owner), device_id_type=MESH)))
    return out


def _scatter_start(layer, grads, lands, meta):
    n = len(grads)
    uniq = []
    for a in lands:
        if not any(a is u for u in uniq):
            uniq.append(a)
    which = [next(k for k, u in enumerate(uniq) if u is a) for a in lands]
    nu = len(uniq)

    def body(*refs):
        grad_refs, land_u = refs[:n], refs[n:n + nu]
        send_sem, recv_sem = refs[n + nu], refs[n + nu + 1]
        for other, cp in _scatter_sends(grad_refs, [land_u[k] for k in which], send_sem, recv_sem, meta):
            if other is None:
                cp.start()
            else:
                pl.when(other == 1)(cp.start)
        refs[-1][...] = jnp.zeros_like(refs[-1])

    thru = [pltpu.HBM(a.shape, a.dtype) for a in list(grads) + uniq]
    outs = pl.pallas_call(
        body, name=f"scatter_start_{layer}",
        out_shape=tuple([pltpu.SemaphoreType.DMA((4 * n,)), pltpu.SemaphoreType.DMA((N_SLOTS * n,))] + thru
                        + [jax.ShapeDtypeStruct((SUBLANES, LANES), F32)]),
        in_specs=[_HBM] * (n + nu),
        out_specs=tuple([_SEM, _SEM] + [_HBM] * (n + nu) + [pl.BlockSpec(memory_space=pltpu.VMEM)]),
        input_output_aliases={i: 2 + i for i in range(n + nu)},
        compiler_params=pltpu.CompilerParams(has_side_effects=_DATAFLOW),
    )(*[_in_hbm(a) for a in list(grads) + uniq])
    new_lands = [outs[2 + n + k] for k in which]
    return outs[0], outs[1], list(outs[2:2 + n]), new_lands, outs[-1]


def _scatter_wait(started, lands):
    nl = len(lands)
    flat_grads = [g for s in started for g in s[2]]
    ng, ns = len(flat_grads), len(started)

    def body(*refs):
        land_refs = refs[:nl]
        grad_refs = refs[nl:nl + ng]
        sem_refs = refs[nl + ng:nl + ng + 2 * ns]
        _, _, c = _coords()
        off = 0
        for k, (_, _, grads, idx, meta) in enumerate(started):
            send_sem, recv_sem = sem_refs[2 * k], sem_refs[2 * k + 1]
            lr = [land_refs[i] for i in idx]
            for other, cp in _scatter_sends(grad_refs[off:off + len(grads)], lr, send_sem, recv_sem, meta):
                if other is None:
                    cp.wait_send()
                else:
                    pl.when(other == 1)(cp.wait_send)
            for j, (axis, owner, q, width) in enumerate(meta):
                mine = (c if owner == 0 else 1 - c) == 0

                @pl.when(mine)
                def _():
                    for slot in range(N_SLOTS):
                        land = lr[j].at[q, slot]
                        pltpu.make_async_remote_copy(
                            src_ref=land, dst_ref=land, send_sem=send_sem.at[0], recv_sem=recv_sem.at[N_SLOTS * j + slot],
                            device_id=_coords(), device_id_type=MESH).wait_recv()
            off += len(grads)

    args = list(lands) + flat_grads
    thru = [pltpu.HBM(a.shape, a.dtype) for a in args]
    sems = [s for st in started for s in st[:2]]
    outs = pl.pallas_call(
        body, name="scatter_wait", out_shape=tuple(thru), in_specs=[_HBM] * (nl + ng) + [_SEM] * (2 * ns),
        out_specs=tuple([_HBM] * (nl + ng)), input_output_aliases={i: i for i in range(nl + ng)},
        compiler_params=pltpu.CompilerParams(has_side_effects=_DATAFLOW),
    )(*args, *sems)
    return list(outs[:nl]), list(outs[nl:])


def _sum_and_share(recv, layer_grads, axis, chip, name):
    n, ns, r, c = recv.shape
    tr = _pick(r, prefs=(256, 128, 64, 32, 16))
    nr = r // tr
    nsteps = n * nr
    nlay = len(layer_grads)
    own_map = (lambda h, i, s: (i, s[0])) if axis == 1 else (lambda h, i, s: (s[0] * nr + i, 0))

    def body(chip_ref, i_ref, *rest):
        g_refs = rest[:nlay]
        o_ref, buf, loc_sems, send_sems, recv_sems = rest[nlay:]
        h, i = pl.program_id(0), pl.program_id(1)
        step = h * nr + i
        slot = step % 2
        x, y, core = _coords()
        layer = core * n + h
        own = g_refs[0][...]
        for l in range(1, nlay):
            own = jnp.where(layer == l, g_refs[l][...], own)

        def copies(sl):
            dst = o_ref.at[core * n + h, pl.ds(pl.multiple_of(i * tr, tr), tr), :]
            loc = pltpu.make_async_copy(buf.at[sl], dst, loc_sems.at[sl])
            rem = pltpu.make_async_remote_copy(
                src_ref=buf.at[sl], dst_ref=dst, send_sem=send_sems.at[sl], recv_sem=recv_sems.at[step],
                device_id=(x, y, 1 - core), device_id_type=MESH)
            return loc, rem

        def drain(sl):
            loc, rem = copies(sl)
            loc.wait()
            rem.wait_send()

        pl.when(step >= 2)(lambda: drain(slot))
        acc = own.astype(F32)
        for s in range(ns):
            acc = acc + i_ref[s].astype(F32)
        buf[slot] = acc
        loc, rem = copies(slot)
        loc.start()
        rem.start()

        @pl.when(step == nsteps - 1)
        def _():
            drain(slot)
            if nsteps > 1:
                drain(1 - slot)
            for hh in range(n):
                for ii in range(nr):
                    land = o_ref.at[(1 - core) * n + hh, pl.ds(ii * tr, tr), :]
                    pltpu.make_async_remote_copy(
                        src_ref=buf.at[0], dst_ref=land, send_sem=send_sems.at[0], recv_sem=recv_sems.at[hh * nr + ii],
                        device_id=(x, y, 1 - core), device_id_type=MESH).wait_recv()

    return pl.pallas_call(
        body, name=name, out_shape=jax.ShapeDtypeStruct((2 * n, r, c), F32),
        grid_spec=pltpu.PrefetchScalarGridSpec(
            num_scalar_prefetch=1, grid=(n, nr),
            in_specs=[pl.BlockSpec((None, ns, tr, c), lambda h, i, s: (h, 0, i, 0))]
            + [pl.BlockSpec((tr, c), own_map)] * nlay,
            out_specs=_HBM,
            scratch_shapes=[pltpu.VMEM((2, tr, c), F32), pltpu.SemaphoreType.DMA((2,)),
                            pltpu.SemaphoreType.DMA((2,)), pltpu.SemaphoreType.DMA((nsteps,))]),
        compiler_params=_cparams(("arbitrary", "arbitrary")),
    )(chip, recv, *layer_grads)


def _gather_sums_over_chips(part):
    def body(i_ref, o_ref, send_sems, recv_sems):
        x, y, c = _coords()
        o_ref[2 * x + y] = i_ref[...]

        def copy(f, slot_chip):
            fx, fy = _CHIP_FLIPS[f]
            return pltpu.make_async_remote_copy(
                src_ref=i_ref, dst_ref=o_ref.at[2 * slot_chip[0] + slot_chip[1]], send_sem=send_sems.at[f - 1],
                recv_sem=recv_sems.at[f - 1], device_id=(_flip(x, fx), _flip(y, fy), c), device_id_type=MESH)

        sends = [copy(f, (x, y)) for f in (1, 2, 3)]
        for cp in sends:
            cp.start()
        for f in (1, 2, 3):
            fx, fy = _CHIP_FLIPS[f]
            copy(f, (_flip(x, fx), _flip(y, fy))).wait_recv()
        for cp in sends:
            cp.wait_send()

    vmem = pl.BlockSpec(memory_space=pltpu.VMEM)
    return pl.pallas_call(
        body, name="gather_small_sums", out_shape=jax.ShapeDtypeStruct((N_CHIPS,) + part.shape, part.dtype),
        in_specs=[vmem], out_specs=vmem,
        scratch_shapes=[pltpu.SemaphoreType.DMA((3,)), pltpu.SemaphoreType.DMA((3,))],
    )(part)


def _adamw(w, g, m, v, name):
    bc1 = 1.0 - ADAM_B1 ** ADAM_STEP
    bc2 = 1.0 - ADAM_B2 ** ADAM_STEP

    def body(w_ref, g_ref, m_ref, v_ref, d_ref, mo_ref, vo_ref):
        gv = g_ref[...]
        mn = ADAM_B1 * m_ref[...] + (1.0 - ADAM_B1) * gv
        vn = ADAM_B2 * v_ref[...] + (1.0 - ADAM_B2) * (gv * gv)
        d_ref[...] = -ADAM_LR * ((mn / bc1) / (jnp.sqrt(vn / bc2) + ADAM_EPS) + ADAM_WD * w_ref[...])
        mo_ref[...] = mn
        vo_ref[...] = vn

    sds = jax.ShapeDtypeStruct(w.shape, F32)
    if w.ndim == 2 and w.shape[0] % SUBLANES == 0:
        tr = _pick(w.shape[0], prefs=(256, 128, 64, 32, 16, 8))
        grid, blk = (w.shape[0] // tr,), pl.BlockSpec((tr, w.shape[1]), lambda i: (i, 0))
    else:
        nd = w.ndim
        grid, blk = (1,), pl.BlockSpec(w.shape, lambda i: (0,) * nd)
    return pl.pallas_call(
        body, name=name, out_shape=(sds, sds, sds), grid=grid, in_specs=[blk] * 4, out_specs=(blk,) * 3,
        compiler_params=_cparams(("parallel",)),
    )(w, g, m, v)


_PACK_QUANTUM = 256 * LANES


def _pack(arrs):
    flat = jnp.concatenate([a.reshape(-1).astype(F32) for a in arrs])
    flat = jnp.pad(flat, (0, (-flat.shape[0]) % _PACK_QUANTUM))
    return flat.reshape(-1, LANES)


def _unpack(p, shapes):
    flat = p.reshape(-1)
    out, off = [], 0
    for s in shapes:
        n = int(np.prod(s))
        out.append(flat[off:off + n].reshape(s))
        off += n
    return out


def kernel(*args):
    nw = len(WEIGHTS)
    x, tgt = args[0], args[1 + nw]
    w = dict(zip(WEIGHTS, args[1:1 + nw]))
    m = dict(zip(WEIGHTS, args[2 + nw:2 + 2 * nw]))
    v = dict(zip(WEIGHTS, args[2 + 2 * nw:2 + 3 * nw]))
    _, L, D = x.shape
    chip = 2 * lax.axis_index("x") + lax.axis_index("y")

    big = list(BIG)
    small_sh_shapes = [w[n].shape for n in SMALL_SHARDED]
    nbig = len(big)
    chip1 = chip.reshape(1).astype(jnp.int32)
    axes2 = [BIG[n] - 1 for n in big] + [0]
    shards = [w[n] for n in big] + [_pack([w[n] for n in SMALL_SHARDED])[None]]
    pairs = [(t, l) for t in range(nbig + 1) for l in range(shards[t].shape[0])]
    depth = w['norm_mix_g'].shape[0]
    part_of = lambda t, l: 0 if t == nbig else 2 * _model_layer(big[t], l) + big[t].startswith('ffn')
    flying, token = [], None
    for g in range(2 * depth):
        ids = [k for k, (t, l) in enumerate(pairs) if part_of(t, l) == g]
        ts = [pairs[k][0] for k in ids]
        placed = [_place_quarter(shards[t], pairs[k][1], axes2[t], chip1, F32 if t == nbig else BF16, token)
                  for k, t in zip(ids, ts)]
        lands, send, recv, token = _gather_start(g, placed, [axes2[t] for t in ts], token)
        flying.append((ts, lands, send, recv))

    def wait_group(g, after):
        ts, lands, send, recv = flying[g]
        landed = _gather_wait(g, lands, send, recv, [axes2[t] for t in ts], token if after is None else after)
        return dict(zip(ts, landed))

    first = wait_group(0, None)
    packed = first.pop(nbig).reshape(N_CHIPS, -1, LANES)
    per_chip = [_unpack(packed[s], small_sh_shapes) for s in range(N_CHIPS)]
    wl = dict(w)
    for k, n in enumerate(SMALL_SHARDED):
        wl[n] = jnp.concatenate([per_chip[s][k] for s in range(N_CHIPS)], axis=-1)

    def layer_weights(i, after):
        got = first if i == 0 else wait_group(i, after)
        return {big[t]: a for t, a in got.items()}

    small_shapes = [(w[n].shape[:-1] + (w[n].shape[-1] * N_CHIPS,)) if n in SMALL_SHARDED else w[n].shape
                    for n in SMALL] + [(1,)]
    n_small = sum(int(np.prod(s)) for s in small_shapes)
    pack_rows = -(-n_small // _PACK_QUANTUM) * _PACK_QUANTUM // LANES
    nlayers = [w[n].shape[0] for n in big] + [2]
    halves = [n // 2 for n in nlayers]
    quarters = [tuple(w[n].shape[1:]) for n in big] + [(pack_rows // 2 // N_CHIPS, LANES)]
    wire = [BF16] * nbig + [F32]
    land_now = [lax.empty((halves[t], N_SLOTS) + quarters[t], wire[t]) for t in range(nbig + 1)]
    gparts = [[None] * n for n in nlayers]
    started = []

    def start_scatter(tag, ts, ls, arrays):
        meta = [(axes2[t], l // halves[t], l % halves[t], quarters[t][axes2[t]]) for t, l in zip(ts, ls)]
        send, recv, thru, new_lands, token = _scatter_start(tag, arrays, [land_now[t] for t in ts], meta)
        for t, ln in zip(ts, new_lands):
            land_now[t] = ln
        started.append((send, recv, thru, ts, meta, ls))
        return token

    def on_layer_grads(g, gb):
        ts = [big.index(n) for n in gb]
        return start_scatter(g, ts, [g // 2 if big[t].startswith('ffn') else g // 4 for t in ts],
                             [gb[big[t]] for t in ts])

    loss, dx, gsmall = _local_step(x.reshape(L, D), tgt.reshape(L, D), wl, layer_weights, on_layer_grads)
    gpack = _pack([gsmall[n] for n in SMALL] + [loss.reshape(1)])
    start_scatter(2 * depth, [nbig, nbig], [0, 1], [gpack[:pack_rows // 2], gpack[pack_rows // 2:]])
    landed, sent = _scatter_wait([s[:5] for s in started], land_now)
    for (t, l), g in zip([(t, l) for s in started for t, l in zip(s[3], s[5])], sent):
        gparts[t][l] = g
    gshard = {n: _sum_and_share(landed[t], gparts[t], axes2[t], chip1, "sum_share_" + n) for t, n in enumerate(big)}
    small_sum = _sum_and_share(landed[nbig], gparts[nbig], 0, chip1, "sum_share_small")
    gpack = _gather_sums_over_chips(small_sum).transpose(1, 0, 2, 3).reshape(pack_rows, LANES)
    gs = dict(zip(SMALL + ['loss'], _unpack(gpack, small_shapes)))
    loss = gs.pop('loss').reshape(())
    for n in SMALL_SHARDED:
        width = w[n].shape[-1]
        gs[n] = lax.dynamic_slice_in_dim(gs[n], chip * width, width, axis=gs[n].ndim - 1)

    grads, delta, new_m, new_v = {}, {}, {}, {}
    for n in big:
        shp = w[n].shape
        flat = lambda a: a.reshape(shp[0] * shp[1], shp[2])
        g = gshard[n]
        grads[n] = g
        d_, m_, v_ = _adamw(flat(w[n]), flat(g), flat(m[n]), flat(v[n]), "adamw_" + n)
        delta[n], new_m[n], new_v[n] = d_.reshape(shp), m_.reshape(shp), v_.reshape(shp)
    for n in SMALL:
        shp = w[n].shape
        as2d = (lambda a: a.reshape(1, -1)) if len(shp) == 1 else (lambda a: a)
        d_, m_, v_ = _adamw(as2d(w[n]), as2d(gs[n]), as2d(m[n]), as2d(v[n]), "adamw_" + n)
        grads[n], delta[n], new_m[n], new_v[n] = gs[n], d_.reshape(shp), m_.reshape(shp), v_.reshape(shp)

    return (loss, dx.reshape(1, L, D), *[grads[n] for n in WEIGHTS], *[delta[n] for n in WEIGHTS],
            *[new_m[n] for n in WEIGHTS], *[new_v[n] for n in WEIGHTS])
```

```python
import functools
import math

import numpy as np
import jax
import jax.numpy as jnp
from jax import lax
from jax.experimental import pallas as pl
from jax.experimental.pallas import tpu as pltpu

F32 = jnp.float32
BF16 = jnp.bfloat16
MESH = pl.DeviceIdType.MESH

EPS = 1e-6
CHUNK = 128
POOL_WINDOWS = (2, 4, 8, 16)
LANES = 128
SUBLANES = 8
SCAN_CHUNKS = SUBLANES
S5_GROUPS_PER_STEP = 4
MM_TM_CAP, MM_TN_CAP, MM_TK_CAP = 1408, 1408, 2048
MM_TK_WHOLE = 2048
VMEM_LIMIT = 48 * 1024 * 1024
VMEM_LIMIT_S5 = 56 * 1024 * 1024

ADAM_LR, ADAM_B1, ADAM_B2, ADAM_EPS, ADAM_WD, ADAM_STEP = 0.001, 0.9, 0.999, 1e-08, 0.01, 10

WEIGHTS = ['norm_mix_g', 'even_w_in', 'even_conv_w', 'ssm_log_step', 'ssm_a_re', 'ssm_a_im', 'ssm_b_re',
           'ssm_b_im', 'ssm_c_re', 'ssm_c_im', 'ssm_d', 'ssm_glu_w', 'ssm_glu_b', 'even_w_out', 'odd_w_in',
           'pool_w', 'pool_scale', 'sgu_norm_g', 'sgu_w', 'sgu_b', 'odd_w_out', 'norm_ffn_g', 'ffn_w_up',
           'ffn_conv_w', 'ffn_conv_b', 'ffn_w_down', 'norm_final_g']
BIG = {'even_w_in': 2, 'ssm_glu_w': 1, 'even_w_out': 1, 'odd_w_in': 2, 'odd_w_out': 1, 'ffn_w_up': 2,
       'ffn_w_down': 1}
SMALL_SHARDED = ('even_conv_w', 'pool_scale', 'sgu_norm_g', 'ffn_conv_w')
SMALL = [n for n in WEIGHTS if n not in BIG]
N_CHIPS = 4
N_DEV = 8


def _cparams(sem=None, vmem=VMEM_LIMIT):
    kw = dict(vmem_limit_bytes=vmem)
    if sem is not None:
        kw['dimension_semantics'] = sem
    return pltpu.CompilerParams(**kw)


def _pick(n, segs=(), prefs=(1024, 512, 256, 128)):
    for t in prefs:
        if n % t == 0 and all(s % t == 0 for s in segs if s):
            return t
    return n


def _largest_tile(n, segs, cap):
    best = None
    for t in range(LANES, min(n, cap) + 1, LANES):
        if n % t == 0 and all(s % t == 0 for s in segs if s):
            best = t
    return best if best is not None else n


def _ldims(arr, kind):
    if kind is None:
        return arr.shape
    if kind[0] == 'lead':
        return arr.shape[1:]
    return (arr.shape[1], arr.shape[0] * arr.shape[2])


def _segw(arr, kind):
    return arr.shape[2] if (kind is not None and kind[0] == 'seg') else None


def _opspec(arr, kind, br, bc, rfn, cfn):
    if kind is None:
        return pl.BlockSpec((br, bc), lambda i, j, k: (rfn(i, j, k), cfn(i, j, k)))
    if kind[0] == 'lead':
        lead = kind[1]
        return pl.BlockSpec((None, br, bc), lambda i, j, k: (lead, rfn(i, j, k), cfn(i, j, k)))
    per = arr.shape[2] // bc
    return pl.BlockSpec((None, br, bc), lambda i, j, k: (cfn(i, j, k) // per, rfn(i, j, k), cfn(i, j, k) % per))


def _mm(a, b, mode, out_dtype, name, ak=None, bk=None, ok=None, res=None, dep=None):
    ar, ac = _ldims(a, ak)
    br_, bc_ = _ldims(b, bk)
    if mode == 'nn':
        M, K, N = ar, ac, bc_
        assert br_ == K
    else:
        M, K, N = ar, ac, br_
        assert bc_ == K
    sa, sb = _segw(a, ak), _segw(b, bk)
    so = (N // ok[1]) if ok is not None else None
    tm = _largest_tile(M, [], MM_TM_CAP)
    tn = _largest_tile(N, [sb if mode == 'nn' else None, so], MM_TN_CAP)
    ksegs = [sa, sb if mode == 'nt' else None]
    tk = K if (K <= MM_TK_WHOLE and not any(ksegs)) else _largest_tile(K, ksegs, MM_TK_CAP)
    nk = K // tk
    I = lambda i, j, k: i
    J = lambda i, j, k: j
    Kk = lambda i, j, k: k
    a_spec = _opspec(a, ak, tm, tk, I, Kk)
    if mode == 'nn':
        b_spec = _opspec(b, bk, tk, tn, Kk, J)
        dims = (((1,), (0,)), ((), ()))
    else:
        b_spec = _opspec(b, bk, tn, tk, J, Kk)
        dims = (((1,), (1,)), ((), ()))
    if ok is None:
        out_shape = jax.ShapeDtypeStruct((M, N), out_dtype)
        o_spec = pl.BlockSpec((tm, tn), lambda i, j, k: (i, j))
    else:
        out_shape = jax.ShapeDtypeStruct((ok[1], M, N // ok[1]), out_dtype)
        per = (N // ok[1]) // tn
        o_spec = pl.BlockSpec((None, tm, tn), lambda i, j, k: (j // per, i, j % per))
    has_res = res is not None

    def body(*refs):
        a_ref, b_ref = refs[0], refs[1]
        r_ref = refs[2] if has_res else None
        o_ref = refs[n_in]
        prod = lax.dot_general(a_ref[...].astype(BF16), b_ref[...].astype(BF16), dims, preferred_element_type=F32)
        if nk == 1:
            o_ref[...] = (prod + r_ref[...] if has_res else prod).astype(out_dtype)
            return
        acc = refs[-1]
        k = pl.program_id(2)

        @pl.when(k == 0)
        def _():
            acc[...] = prod

        @pl.when(k > 0)
        def _():
            acc[...] += prod

        @pl.when(k == nk - 1)
        def _():
            o = acc[...]
            if has_res:
                o = o + r_ref[...]
            o_ref[...] = o.astype(out_dtype)

    in_specs = [a_spec, b_spec]
    args = [a, b]
    if has_res:
        in_specs.append(pl.BlockSpec((tm, tn), lambda i, j, k: (i, j)))
        args.append(res)
    if dep is not None:
        in_specs.append(pl.BlockSpec(memory_space=pl.ANY))
        args.append(dep)
    n_in = len(args)
    return pl.pallas_call(
        body, name=name, out_shape=out_shape, grid=(M // tm, N // tn, nk), in_specs=in_specs, out_specs=o_spec,
        scratch_shapes=[pltpu.VMEM((tm, tn), F32)] if nk > 1 else [],
        compiler_params=_cparams(("parallel", "parallel", "arbitrary")),
    )(*args)


_G0 = math.sqrt(2.0 / math.pi)
_G1 = 0.044715


def _gelu(x):
    return 0.5 * x * (1.0 + jnp.tanh(_G0 * (x + _G1 * x * x * x)))


def _gelu_grad(x):
    x2 = x * x
    t = jnp.tanh(_G0 * (x + _G1 * x * x2))
    return 0.5 * (1.0 + t) + 0.5 * x * (1.0 - t * t) * (_G0 * (1.0 + 3.0 * _G1 * x2))


def _sigmoid(x):
    return 1.0 / (1.0 + jnp.exp(-x))


def _down(v, k):
    r = pltpu.roll(v, k, axis=0)
    row = lax.broadcasted_iota(jnp.int32, (SUBLANES, v.shape[1]), 0)
    return jnp.concatenate([jnp.where(row >= k, r[:SUBLANES], 0.0), r[SUBLANES:]], axis=0)


def _up(v, k):
    n = v.shape[0]
    r = pltpu.roll(v, n - k, axis=0)
    row = lax.broadcasted_iota(jnp.int32, (SUBLANES, v.shape[1]), 0)
    return jnp.concatenate([r[:n - SUBLANES], jnp.where(row < SUBLANES - k, r[n - SUBLANES:], 0.0)], axis=0)


def _taps(v):
    return _down(v, 2), _down(v, 1), v


def _conv3(taps, w):
    return w[0:1, :] * taps[0] + w[1:2, :] * taps[1] + w[2:3, :] * taps[2]


def _conv3_t(dv, w):
    return w[2:3, :] * dv + w[1:2, :] * _up(dv, 1) + w[0:1, :] * _up(dv, 2)


def _conv3_dw(dv, taps):
    return tuple(jnp.sum(dv * tp, axis=0, keepdims=True) for tp in taps)


def _cmul(ar, ai, br, bi):
    return ar * br - ai * bi, ar * bi + ai * br


def _cpow(lr, li, n):
    rr = ri = None
    br, bi = lr, li
    while n:
        if n & 1:
            rr, ri = (br, bi) if rr is None else _cmul(rr, ri, br, bi)
        n >>= 1
        if n:
            br, bi = _cmul(br, bi, br, bi)
    return rr, ri


NORM_ROWS = 256


def _norm_mm(x, g, b, out_dtype, name, ok=None):
    M, D = x.shape
    N = b.shape[1]
    so = (N // ok[1]) if ok is not None else None
    tm = _largest_tile(M, [], 1024)
    tn = _largest_tile(N, [so], MM_TN_CAP)
    if ok is None:
        out_shape = jax.ShapeDtypeStruct((M, N), out_dtype)
        o_spec = pl.BlockSpec((tm, tn), lambda i, j: (i, j))
    else:
        out_shape = jax.ShapeDtypeStruct((ok[1], M, N // ok[1]), out_dtype)
        per = (N // ok[1]) // tn
        o_spec = pl.BlockSpec((None, tm, tn), lambda i, j: (j // per, i, j % per))

    def body(x_ref, g_ref, b_ref, o_ref, ht_ref, h_scr):
        @pl.when(pl.program_id(1) == 0)
        def _():
            for c in range(tm // NORM_ROWS):
                rows = pl.ds(c * NORM_ROWS, NORM_ROWS)
                xv = x_ref[rows, :]
                h = xv * lax.rsqrt(jnp.mean(xv * xv, axis=-1, keepdims=True) + EPS) * g_ref[...]
                h_scr[rows, :] = h.astype(BF16)
                ht_ref[:, rows] = h.T.astype(BF16)

        o_ref[...] = jnp.dot(h_scr[...], b_ref[...], preferred_element_type=F32).astype(out_dtype)

    return pl.pallas_call(
        body, name=name, out_shape=(out_shape, jax.ShapeDtypeStruct((D, M), BF16)), grid=(M // tm, N // tn),
        in_specs=[pl.BlockSpec((tm, D), lambda i, j: (i, 0)), pl.BlockSpec((1, D), lambda i, j: (0, 0)),
                  pl.BlockSpec((D, tn), lambda i, j: (0, j))],
        out_specs=(o_spec, pl.BlockSpec((D, tm), lambda i, j: (0, i))),
        scratch_shapes=[pltpu.VMEM((tm, D), BF16)], compiler_params=_cparams(("parallel", "arbitrary")),
    )(x, g.reshape(1, D), b)


def _mm_norm_bwd(a, b, x, g, dres, name, ak=None, dep=None):
    M, K = _ldims(a, ak)
    D = b.shape[0]
    assert b.shape[1] == K and x.shape == (M, D)
    sa = _segw(a, ak)
    tm = _largest_tile(M, [], 1024)
    tk = K if (K <= MM_TK_WHOLE and not sa) else _largest_tile(K, [sa], MM_TK_CAP)
    ni, nk = M // tm, K // tk
    a3 = _opspec(a, ak, tm, tk, lambda i, j, k: i, lambda i, j, k: k)
    a_spec = pl.BlockSpec(a3.block_shape, lambda i, k: a3.index_map(i, 0, k))
    n_in = 5 + (dep is not None)

    def body(*refs):
        a_ref, b_ref, x_ref, g_ref, r_ref = refs[:5]
        dx_ref, dxb_ref, dg_ref, acc, accg = refs[n_in:]
        i, k = pl.program_id(0), pl.program_id(1)
        prod = lax.dot_general(a_ref[...].astype(BF16), b_ref[...], (((1,), (1,)), ((), ())),
                               preferred_element_type=F32)

        @pl.when(k == 0)
        def _():
            acc[...] = prod

        @pl.when(k > 0)
        def _():
            acc[...] += prod

        @pl.when((i == 0) & (k == 0))
        def _():
            accg[...] = jnp.zeros_like(accg)

        @pl.when(k == nk - 1)
        def _():
            for c in range(tm // NORM_ROWS):
                rows = pl.ds(c * NORM_ROWS, NORM_ROWS)
                xv = x_ref[rows, :]
                r = lax.rsqrt(jnp.mean(xv * xv, axis=-1, keepdims=True) + EPS)
                xh = xv * r
                dhv = acc[rows, :]
                accg[...] += jnp.sum((dhv * xh).reshape(NORM_ROWS // SUBLANES, SUBLANES, D), axis=0)
                dxh = dhv * g_ref[...]
                dxv = r_ref[rows, :] + r * (dxh - xh * jnp.mean(dxh * xh, axis=-1, keepdims=True))
                dx_ref[rows, :] = dxv
                dxb_ref[rows, :] = dxv.astype(BF16)

        @pl.when((i == ni - 1) & (k == nk - 1))
        def _():
            dg_ref[...] = jnp.sum(accg[...], axis=0, keepdims=True)

    row = pl.BlockSpec((tm, D), lambda i, k: (i, 0))
    vec = pl.BlockSpec((1, D), lambda i, k: (0, 0))
    in_specs = [a_spec, pl.BlockSpec((D, tk), lambda i, k: (0, k)), row, vec, row]
    args = [a, b, x, g.reshape(1, D), dres]
    if dep is not None:
        in_specs.append(pl.BlockSpec(memory_space=pl.ANY))
        args.append(dep)
    return pl.pallas_call(
        body, name=name,
        out_shape=(jax.ShapeDtypeStruct((M, D), F32), jax.ShapeDtypeStruct((M, D), BF16),
                   jax.ShapeDtypeStruct((1, D), F32)),
        grid=(ni, nk), in_specs=in_specs, out_specs=(row, row, vec),
        scratch_shapes=[pltpu.VMEM((tm, D), F32), pltpu.VMEM((SUBLANES, D), F32)],
        compiler_params=_cparams(("arbitrary", "arbitrary"), VMEM_LIMIT_S5),
    )(*args)


def _loss_head(x, g, tgt):
    L, D = x.shape
    tr = _pick(L, prefs=(512, 256, 128))
    nsteps = L // tr

    def body(x_ref, g_ref, t_ref, loss_ref, dx_ref, dxb_ref, dg_ref, acc_g, acc_l):
        i = pl.program_id(0)

        @pl.when(i == 0)
        def _():
            acc_g[...] = jnp.zeros_like(acc_g)
            acc_l[...] = jnp.zeros_like(acc_l)

        xv = x_ref[...]
        gv = g_ref[...]
        r = lax.rsqrt(jnp.mean(xv * xv, axis=-1, keepdims=True) + EPS)
        xh = xv * r
        e = xh * gv - t_ref[...]
        acc_l[...] += jnp.sum((e * e).reshape(tr // SUBLANES, SUBLANES, D), axis=0)
        dy = e * (1.0 / D)
        acc_g[...] += jnp.sum((dy * xh).reshape(tr // SUBLANES, SUBLANES, D), axis=0)
        dxh = dy * gv
        dxv = r * (dxh - xh * jnp.mean(dxh * xh, axis=-1, keepdims=True))
        dx_ref[...] = dxv
        dxb_ref[...] = dxv.astype(BF16)

        @pl.when(i == nsteps - 1)
        def _():
            dg_ref[...] = jnp.sum(acc_g[...], axis=0, keepdims=True)
            tot = jnp.sum(jnp.sum(acc_l[...], axis=0, keepdims=True), axis=1, keepdims=True) * (0.5 / D)
            loss_ref[...] = jnp.broadcast_to(tot, (SUBLANES, LANES))

    row = pl.BlockSpec((tr, D), lambda i: (i, 0))
    vec = pl.BlockSpec((1, D), lambda i: (0, 0))
    return pl.pallas_call(
        body, name="loss_head",
        out_shape=(jax.ShapeDtypeStruct((SUBLANES, LANES), F32), jax.ShapeDtypeStruct((L, D), F32),
                   jax.ShapeDtypeStruct((L, D), BF16), jax.ShapeDtypeStruct((1, D), F32)),
        grid=(nsteps,), in_specs=[row, vec, row],
        out_specs=(pl.BlockSpec((SUBLANES, LANES), lambda i: (0, 0)), row, row, vec),
        scratch_shapes=[pltpu.VMEM((SUBLANES, D), F32), pltpu.VMEM((SUBLANES, D), F32)],
        compiler_params=_cparams(("arbitrary",)),
    )(x, g.reshape(1, D), tgt)


def _sconv_fwd(proj4, conv_w, name):
    _, L, C = proj4.shape
    cb = LANES

    def body(p_ref, w_ref, o_ref):
        xa, ba, ca = p_ref[0].astype(F32), p_ref[1].astype(F32), p_ref[2].astype(F32)
        o_ref[...] = (ba * _conv3(_taps(ca * xa), w_ref[...])).astype(BF16)

    return pl.pallas_call(
        body, name=name, out_shape=jax.ShapeDtypeStruct((L, 2 * C), BF16), grid=(C // cb,),
        in_specs=[pl.BlockSpec((3, L, cb), lambda j: (0, 0, j)), pl.BlockSpec((3, cb), lambda j: (0, j))],
        out_specs=pl.BlockSpec((L, cb), lambda j: (0, j)), compiler_params=_cparams(("parallel",)),
    )(proj4, conv_w)


def _sconv_bwd(proj4, dmix, conv_w, name):
    _, L, C = proj4.shape
    cb = LANES

    def body(p_ref, d_ref, w_ref, o_ref, dw_ref):
        xa, ba, ca = p_ref[0].astype(F32), p_ref[1].astype(F32), p_ref[2].astype(F32)
        w = w_ref[...]
        dya = d_ref[...]
        tq = _taps(ca * xa)
        cq = _conv3(tq, w)
        dcq = dya * ba
        dq = _conv3_t(dcq, w)
        for tap, dwt in enumerate(_conv3_dw(dcq, tq)):
            dw_ref[tap:tap + 1, :] = dwt
        o_ref[0] = (dq * ca).astype(BF16)
        o_ref[1] = (dya * cq).astype(BF16)
        o_ref[2] = (dq * xa).astype(BF16)

    return pl.pallas_call(
        body, name=name,
        out_shape=(jax.ShapeDtypeStruct((4, L, C), BF16), jax.ShapeDtypeStruct((3, C), F32)), grid=(C // cb,),
        in_specs=[pl.BlockSpec((3, L, cb), lambda j: (0, 0, j)), pl.BlockSpec((L, cb), lambda j: (0, j)),
                  pl.BlockSpec((3, cb), lambda j: (0, j))],
        out_specs=(pl.BlockSpec((3, L, cb), lambda j: (0, 0, j)), pl.BlockSpec((3, cb), lambda j: (0, j))),
        compiler_params=_cparams(("parallel",)),
    )(proj4, dmix, conv_w)


def _s5_prep(log_step, a_re, a_im, b_re, b_im, c_re, c_im):
    G, P = a_re.shape
    H = b_re.shape[-1]
    gs = S5_GROUPS_PER_STEP
    ns = G // gs
    gu = LANES // H
    lam = lax.complex(a_re, a_im)
    step = jnp.exp(log_step)[:, None]
    lam_bar = jnp.exp(lam * step)
    b_bar = ((lam_bar - 1.0) / lam)[..., None] * lax.complex(b_re, b_im)
    lr = jnp.real(lam_bar).reshape(ns, 1, gs * P)
    li = jnp.imag(lam_bar).reshape(ns, 1, gs * P)
    k = np.arange(ns)[:, None, None]
    oh = jnp.asarray((np.arange(gu)[None, :, None] == gs * (k % (gu // gs)) + np.arange(gs)[None, None, :]),
                     F32)
    bre = jnp.einsum('kgl,klph->kghlp', oh, jnp.real(b_bar).reshape(ns, gs, P, H)).reshape(ns, gu * H, gs * P)
    bim = jnp.einsum('kgl,klph->kghlp', oh, jnp.imag(b_bar).reshape(ns, gs, P, H)).reshape(ns, gu * H, gs * P)
    cre = jnp.einsum('kgl,klhp->klpgh', oh, c_re.reshape(ns, gs, H, P)).reshape(ns, gs * P, gu * H)
    cim = jnp.einsum('kgl,klhp->klpgh', oh, c_im.reshape(ns, gs, H, P)).reshape(ns, gs * P, gu * H)
    return lr, li, jnp.concatenate([bre, bim], axis=2), jnp.concatenate([cre, -cim], axis=1)


def _carry_tile(fr, fi, pr, pi, reverse):
    row = lax.broadcasted_iota(jnp.int32, fr.shape, 0)
    cr = jnp.zeros_like(fr)
    ci = jnp.zeros_like(fi)
    sr = jnp.zeros_like(fr[0:1])
    si = jnp.zeros_like(sr)
    order = range(SCAN_CHUNKS - 1, 0, -1) if reverse else range(0, SCAN_CHUNKS - 1)
    for c in order:
        fcr = jnp.sum(jnp.where(row == c, fr, 0.0), axis=0, keepdims=True)
        fci = jnp.sum(jnp.where(row == c, fi, 0.0), axis=0, keepdims=True)
        mr, mi = _cmul(pr, pi, sr, si)
        sr, si = mr + fcr, mi + fci
        nxt = c - 1 if reverse else c + 1
        cr = jnp.where(row == nxt, sr, cr)
        ci = jnp.where(row == nxt, si, ci)
    return cr, ci


def _scan_order_into(dst_ref, src_ref, T):
    for c in range(SCAN_CHUNKS):
        dst_ref[pl.ds(c, T, stride=SCAN_CHUNKS), :] = src_ref[pl.ds(c * T, T), :].astype(F32)


def _s5_fwd(proj4, lr, li, bmat, cmat, d, name):
    _, L, Du = proj4.shape
    ns, _, W2 = bmat.shape
    W = W2 // 2
    T = L // SCAN_CHUNKS
    rb = _pick(L, prefs=(512, 256, 128))
    per = (ns * LANES) // Du

    def body(ut_ref, lr_ref, li_ref, b_ref, c_ref, d_ref, y_ref, sr_ref, si_ref, u_ref):
        k = pl.program_id(0)
        _scan_order_into(u_ref, ut_ref, T)
        for r in range(L // rb):
            rows = pl.ds(r * rb, rb)
            bu = jnp.dot(u_ref[rows, :].astype(BF16), b_ref[...], preferred_element_type=F32)
            sr_ref[rows, :] = bu[:, :W]
            si_ref[rows, :] = bu[:, W:]
        lam_r = jnp.broadcast_to(lr_ref[...], (SUBLANES, W))
        lam_i = jnp.broadcast_to(li_ref[...], (SUBLANES, W))

        def local(t, carry):
            sr, si = carry
            rows = pl.ds(pl.multiple_of(t * SUBLANES, SUBLANES), SUBLANES)
            mr, mi = _cmul(lam_r, lam_i, sr, si)
            sr = mr + sr_ref[rows, :]
            si = mi + si_ref[rows, :]
            sr_ref[rows, :] = sr
            si_ref[rows, :] = si
            return sr, si

        z = jnp.zeros((SUBLANES, W), F32)
        fr, fi = lax.fori_loop(0, T, local, (z, z))
        pr, pi = _cpow(lam_r, lam_i, T)
        cr, ci = _carry_tile(fr, fi, pr[0:1], pi[0:1], reverse=False)

        def fix(t, carry):
            wr, wi = carry
            rows = pl.ds(pl.multiple_of(t * SUBLANES, SUBLANES), SUBLANES)
            ar, ai = _cmul(wr, wi, cr, ci)
            sr_ref[rows, :] += ar
            si_ref[rows, :] += ai
            return _cmul(wr, wi, lam_r, lam_i)

        lax.fori_loop(0, T, fix, (lam_r, lam_i))
        first = (k % per) == 0
        for r in range(L // rb):
            rows = pl.ds(r * rb, rb)
            s = jnp.concatenate([sr_ref[rows, :], si_ref[rows, :]], axis=1).astype(BF16)
            y = jnp.dot(s, c_ref[...], preferred_element_type=F32)

            @pl.when(first)
            def _():
                y_ref[rows, :] = y + d_ref[...] * u_ref[rows, :]

            @pl.when(jnp.logical_not(first))
            def _():
                y_ref[rows, :] += y

    ublk = pl.BlockSpec((L, LANES), lambda k: (0, k // per))
    sblk = pl.BlockSpec((L, W), lambda k: (0, k))
    lam = pl.BlockSpec((None, 1, W), lambda k: (k, 0, 0))
    return pl.pallas_call(
        body, name=name,
        out_shape=(jax.ShapeDtypeStruct((L, Du), F32), jax.ShapeDtypeStruct((L, ns * W), F32),
                   jax.ShapeDtypeStruct((L, ns * W), F32)),
        grid=(ns,),
        in_specs=[pl.BlockSpec((None, L, LANES), lambda k: (3, 0, k // per)), lam, lam,
                  pl.BlockSpec((None, LANES, 2 * W), lambda k: (k, 0, 0)),
                  pl.BlockSpec((None, 2 * W, LANES), lambda k: (k, 0, 0)),
                  pl.BlockSpec((1, LANES), lambda k: (0, k // per))],
        out_specs=(ublk, sblk, sblk), scratch_shapes=[pltpu.VMEM((L, LANES), F32)],
        compiler_params=_cparams(("arbitrary",), VMEM_LIMIT_S5),
    )(proj4, lr, li, bmat.astype(BF16), cmat.astype(BF16), d.reshape(1, Du))


def _s5_bwd(dy, proj4, dproj, s_re, s_im, lr, li, bmat, cmat, d, name):
    _, L, Du = proj4.shape
    ns, _, W2 = bmat.shape
    W = W2 // 2
    T = L // SCAN_CHUNKS
    rb = _pick(L, prefs=(512, 256, 128))
    per = (ns * LANES) // Du
    NT = (((1,), (1,)), ((), ()))
    TN = (((0,), (0,)), ((), ()))

    def body(dy_ref, ut_ref, dp_in, sr_ref, si_ref, lr_ref, li_ref, b_ref, c_ref, d_ref,
             dut_ref, db_ref, dc_ref, dl_ref, dd_ref, gr_ref, gi_ref, u_ref, du_ref):
        k = pl.program_id(0)
        _scan_order_into(u_ref, ut_ref, T)
        for r in range(L // rb):
            rows = pl.ds(r * rb, rb)
            g = lax.dot_general(dy_ref[rows, :].astype(BF16), c_ref[...], NT, preferred_element_type=F32)
            gr_ref[rows, :] = g[:, :W]
            gi_ref[rows, :] = g[:, W:]
        lam_r = jnp.broadcast_to(lr_ref[...], (SUBLANES, W))
        lam_i = -jnp.broadcast_to(li_ref[...], (SUBLANES, W))

        def local(i, carry):
            gr, gi = carry
            rows = pl.ds(pl.multiple_of((T - 1 - i) * SUBLANES, SUBLANES), SUBLANES)
            mr, mi = _cmul(lam_r, lam_i, gr, gi)
            gr = mr + gr_ref[rows, :]
            gi = mi + gi_ref[rows, :]
            gr_ref[rows, :] = gr
            gi_ref[rows, :] = gi
            return gr, gi

        z = jnp.zeros((SUBLANES, W), F32)
        fr, fi = lax.fori_loop(0, T, local, (z, z))
        pr, pi = _cpow(lam_r, lam_i, T)
        cr, ci = _carry_tile(fr, fi, pr[0:1], pi[0:1], reverse=True)

        def true_g(rows, wr, wi):
            ar, ai = _cmul(wr, wi, cr, ci)
            gr = gr_ref[rows, :] + ar
            gi = gi_ref[rows, :] + ai
            gr_ref[rows, :] = gr
            gi_ref[rows, :] = gi
            return gr, gi

        def fix(i, carry):
            wr, wi, ar_, ai_ = carry
            t = T - 1 - i
            rows = pl.ds(pl.multiple_of(t * SUBLANES, SUBLANES), SUBLANES)
            prev = pl.ds(pl.multiple_of((t - 1) * SUBLANES, SUBLANES), SUBLANES)
            gr, gi = true_g(rows, wr, wi)
            qr, qi = sr_ref[prev, :], si_ref[prev, :]
            ar_ = ar_ + gr * qr + gi * qi
            ai_ = ai_ + gi * qr - gr * qi
            wr, wi = _cmul(wr, wi, lam_r, lam_i)
            return wr, wi, ar_, ai_

        wr, wi, acc_r, acc_i = lax.fori_loop(0, T - 1, fix, (lam_r, lam_i, z, z))
        gr, gi = true_g(pl.ds(0, SUBLANES), wr, wi)
        last = pl.ds((T - 1) * SUBLANES, SUBLANES)
        row = lax.broadcasted_iota(jnp.int32, (SUBLANES, W), 0)
        qr = jnp.where(row >= 1, pltpu.roll(sr_ref[last, :], 1, axis=0), 0.0)
        qi = jnp.where(row >= 1, pltpu.roll(si_ref[last, :], 1, axis=0), 0.0)
        acc_r = acc_r + gr * qr + gi * qi
        acc_i = acc_i + gi * qr - gr * qi
        dl_ref[0:1, :] = jnp.sum(acc_r, axis=0, keepdims=True)
        dl_ref[1:2, :] = jnp.sum(acc_i, axis=0, keepdims=True)

        first = (k % per) == 0
        db = jnp.zeros((LANES, 2 * W), F32)
        dc = jnp.zeros((LANES, 2 * W), F32)
        dd = jnp.zeros((1, LANES), F32)
        for r in range(L // rb):
            rows = pl.ds(r * rb, rb)
            gb = jnp.concatenate([gr_ref[rows, :], gi_ref[rows, :]], axis=1).astype(BF16)
            sb = jnp.concatenate([sr_ref[rows, :], si_ref[rows, :]], axis=1).astype(BF16)
            dyv = dy_ref[rows, :]
            uv = u_ref[rows, :]
            du = lax.dot_general(gb, b_ref[...], NT, preferred_element_type=F32)
            db = db + lax.dot_general(uv.astype(BF16), gb, TN, preferred_element_type=F32)
            dc = dc + lax.dot_general(dyv.astype(BF16), sb, TN, preferred_element_type=F32)
            dd = dd + jnp.sum(dyv * uv, axis=0, keepdims=True)

            @pl.when(first)
            def _():
                du_ref[rows, :] = du + d_ref[...] * dyv

            @pl.when(jnp.logical_not(first))
            def _():
                du_ref[rows, :] += du

        db_ref[...] = db
        dc_ref[...] = dc

        @pl.when(first)
        def _():
            dd_ref[...] = dd

        @pl.when((k % per) == per - 1)
        def _():
            for c in range(SCAN_CHUNKS):
                dut_ref[pl.ds(c * T, T), :] = du_ref[pl.ds(c, T, stride=SCAN_CHUNKS), :].astype(BF16)

    ublk = pl.BlockSpec((L, LANES), lambda k: (0, k // per))
    uslab = pl.BlockSpec((None, L, LANES), lambda k: (3, 0, k // per))
    sblk = pl.BlockSpec((L, W), lambda k: (0, k))
    lam = pl.BlockSpec((None, 1, W), lambda k: (k, 0, 0))
    vec = pl.BlockSpec((1, LANES), lambda k: (0, k // per))
    mat = pl.BlockSpec((None, LANES, 2 * W), lambda k: (k, 0, 0))
    return pl.pallas_call(
        body, name=name,
        out_shape=(jax.ShapeDtypeStruct(dproj.shape, dproj.dtype), jax.ShapeDtypeStruct((ns, LANES, 2 * W), F32),
                   jax.ShapeDtypeStruct((ns, LANES, 2 * W), F32), jax.ShapeDtypeStruct((ns, 2, W), F32),
                   jax.ShapeDtypeStruct((1, Du), F32)),
        grid=(ns,),
        in_specs=[ublk, uslab, pl.BlockSpec(memory_space=pl.ANY), sblk, sblk, lam, lam, mat,
                  pl.BlockSpec((None, 2 * W, LANES), lambda k: (k, 0, 0)), vec],
        out_specs=(uslab, mat, mat, pl.BlockSpec((None, 2, W), lambda k: (k, 0, 0)), vec),
        scratch_shapes=[pltpu.VMEM((L, W), F32), pltpu.VMEM((L, W), F32), pltpu.VMEM((L, LANES), F32),
                        pltpu.VMEM((L, LANES), F32)],
        input_output_aliases={2: 0}, compiler_params=_cparams(("arbitrary",), VMEM_LIMIT_S5),
    )(dy, proj4, dproj, s_re, s_im, lr, li, bmat.astype(BF16), cmat.astype(BF16), d.reshape(1, Du))


def _glu_fwd(yraw, wmat, bias, mixin, name):
    L, C = yraw.shape
    tr = _pick(L, prefs=(512, 256, 128))
    tb = tr // SCAN_CHUNKS
    nl = C // LANES

    def body(y_ref, w_ref, b_ref, m_in, o_ref, scr):
        yg = _gelu(y_ref[...])
        zz = jnp.dot(yg.astype(BF16), w_ref[...], preferred_element_type=F32) + b_ref[...]
        yb = yg * _sigmoid(zz)
        for k in range(nl):
            scr[k] = yb[:, k * LANES:(k + 1) * LANES]
        for c in range(SCAN_CHUNKS):
            for k in range(nl):
                o_ref[c, :, k * LANES:(k + 1) * LANES] = scr[k, pl.ds(c, tb, stride=SCAN_CHUNKS), :].astype(BF16)

    out = pl.pallas_call(
        body, name=name, out_shape=jax.ShapeDtypeStruct((SCAN_CHUNKS, L // SCAN_CHUNKS, 2 * C), BF16),
        grid=(L // tr,),
        in_specs=[pl.BlockSpec((tr, C), lambda i: (i, 0)), pl.BlockSpec((C, C), lambda i: (0, 0)),
                  pl.BlockSpec((1, C), lambda i: (0, 0)), pl.BlockSpec(memory_space=pl.ANY)],
        out_specs=pl.BlockSpec((SCAN_CHUNKS, tb, C), lambda i: (0, i, 1)),
        scratch_shapes=[pltpu.VMEM((nl, tr, LANES), F32)], input_output_aliases={3: 0},
        compiler_params=_cparams(("parallel",)),
    )(yraw, wmat, bias.reshape(1, C), mixin.reshape(SCAN_CHUNKS, L // SCAN_CHUNKS, 2 * C))
    return out.reshape(L, 2 * C)


def _glu_bwd(yraw, dmix, wmat, bias, name):
    L, C = yraw.shape
    tr = _pick(L, prefs=(512, 256, 128))
    nsteps = L // tr
    tb = tr // SCAN_CHUNKS
    nl = C // LANES

    def body(y_ref, d_ref, w_ref, b_ref, dy_ref, dw_ref, db_ref, acc_b, scr):
        i = pl.program_id(0)

        @pl.when(i == 0)
        def _():
            dw_ref[...] = jnp.zeros_like(dw_ref)
            acc_b[...] = jnp.zeros_like(acc_b)

        for c in range(SCAN_CHUNKS):
            for k in range(nl):
                scr[k, pl.ds(c, tb, stride=SCAN_CHUNKS), :] = d_ref[c, :, k * LANES:(k + 1) * LANES]
        yr = y_ref[...]
        yg = _gelu(yr)
        ygb = yg.astype(BF16)
        sg = _sigmoid(jnp.dot(ygb, w_ref[...], preferred_element_type=F32) + b_ref[...])
        dyb_ = jnp.concatenate([scr[k] for k in range(nl)], axis=1)
        dz = dyb_ * yg * sg * (1.0 - sg)
        dzb = dz.astype(BF16)
        dyg = dyb_ * sg + lax.dot_general(dzb, w_ref[...], (((1,), (1,)), ((), ())), preferred_element_type=F32)
        dw_ref[...] += lax.dot_general(ygb, dzb, (((0,), (0,)), ((), ())), preferred_element_type=F32)
        acc_b[...] += jnp.sum(dz.reshape(tr // SUBLANES, SUBLANES, C), axis=0)
        dy_ref[...] = dyg * _gelu_grad(yr)

        @pl.when(i == nsteps - 1)
        def _():
            db_ref[...] = jnp.sum(acc_b[...], axis=0, keepdims=True)

    row = pl.BlockSpec((tr, C), lambda i: (i, 0))
    return pl.pallas_call(
        body, name=name,
        out_shape=(jax.ShapeDtypeStruct((L, C), F32), jax.ShapeDtypeStruct((C, C), F32),
                   jax.ShapeDtypeStruct((1, C), F32)),
        grid=(nsteps,),
        in_specs=[row, pl.BlockSpec((SCAN_CHUNKS, tb, C), lambda i: (0, i, 1)), pl.BlockSpec((C, C), lambda i: (0, 0)),
                  pl.BlockSpec((1, C), lambda i: (0, 0))],
        out_specs=(row, pl.BlockSpec((C, C), lambda i: (0, 0)), pl.BlockSpec((1, C), lambda i: (0, 0))),
        scratch_shapes=[pltpu.VMEM((SUBLANES, C), F32), pltpu.VMEM((nl, tr, LANES), F32)],
        compiler_params=_cparams(("arbitrary",)),
    )(yraw, dmix.reshape(SCAN_CHUNKS, L // SCAN_CHUNKS, 2 * C), wmat, bias.reshape(1, C))


def _pool_counts(L, g):
    t = lax.broadcasted_iota(jnp.int32, (L, LANES), 0).astype(F32) + 1.0
    w = jnp.where(g == 0, 2.0, jnp.where(g == 1, 4.0, jnp.where(g == 2, 8.0, 16.0)))
    return 1.0 / jnp.minimum(t, w)


def _select_window(g, a2, a4, a8, a16):
    return jnp.where(g == 0, a2, jnp.where(g == 1, a4, jnp.where(g == 2, a8, a16)))


def _pooled(z, g):
    a2 = z + _down(z, 1)
    a4 = a2 + _down(a2, 2)
    a8 = a4 + _down(a4, 4)
    a16 = a8 + _down(a8, 8)
    return _select_window(g, a2, a4, a8, a16) * _pool_counts(z.shape[0], g) - z


def _transpose_on_mxu(yb):
    c = yb.shape[1]
    eye = lax.broadcasted_iota(jnp.int32, (c, c), 0) == lax.broadcasted_iota(jnp.int32, (c, c), 1)
    return lax.dot_general(eye.astype(BF16), yb, (((1,), (1,)), ((), ())), preferred_element_type=F32).astype(BF16)


def _pool_fwd(proj3, pool_w, scale, name):
    _, L, C = proj3.shape
    ng = len(POOL_WINDOWS)
    pg = C // ng
    assert pg == LANES

    def body(z_ref, w_ref, s_ref, o_ref, ot_ref):
        g = pl.program_id(0)
        p = _pooled(z_ref[...].astype(F32), g)
        y = jnp.dot(p.astype(BF16), w_ref[...].astype(BF16), preferred_element_type=F32)
        yb = (y * s_ref[...]).astype(BF16)
        o_ref[...] = yb
        ot_ref[...] = _transpose_on_mxu(yb)

    return pl.pallas_call(
        body, name=name, out_shape=(jax.ShapeDtypeStruct((L, 2 * C), BF16), jax.ShapeDtypeStruct((2 * C, L), BF16)),
        grid=(ng,),
        in_specs=[pl.BlockSpec((None, L, pg), lambda g: (0, 0, g)), pl.BlockSpec((None, pg, pg), lambda g: (g, 0, 0)),
                  pl.BlockSpec((1, pg), lambda g: (0, g))],
        out_specs=(pl.BlockSpec((L, pg), lambda g: (0, g)), pl.BlockSpec((pg, L), lambda g: (g, 0))),
        compiler_params=_cparams(("parallel",)),
    )(proj3, pool_w, scale.reshape(1, C))


def _pool_bwd(proj3, dmix, pool_w, scale, name):
    _, L, C = proj3.shape
    ng = len(POOL_WINDOWS)
    pg = C // ng

    def body(z_ref, d_ref, w_ref, s_ref, dz_ref, dw_ref, ds_ref):
        g = pl.program_id(0)
        p = _pooled(z_ref[...].astype(F32), g)
        pb = p.astype(BF16)
        wb = w_ref[...].astype(BF16)
        pre = jnp.dot(pb, wb, preferred_element_type=F32)
        dyc = d_ref[...]
        ds_ref[...] = jnp.sum(dyc * pre, axis=0, keepdims=True)
        dpre = (dyc * s_ref[...]).astype(BF16)
        dw_ref[...] = lax.dot_general(pb, dpre, (((0,), (0,)), ((), ())), preferred_element_type=F32)
        dp = lax.dot_general(dpre, wb, (((1,), (1,)), ((), ())), preferred_element_type=F32)
        v = dp * _pool_counts(L, g)
        a2 = v + _up(v, 1)
        a4 = a2 + _up(a2, 2)
        a8 = a4 + _up(a4, 4)
        a16 = a8 + _up(a8, 8)
        dz_ref[...] = (_select_window(g, a2, a4, a8, a16) - dp).astype(BF16)

    return pl.pallas_call(
        body, name=name,
        out_shape=(jax.ShapeDtypeStruct((L, C), BF16), jax.ShapeDtypeStruct((ng, pg, pg), F32),
                   jax.ShapeDtypeStruct((1, C), F32)),
        grid=(ng,),
        in_specs=[pl.BlockSpec((None, L, pg), lambda g: (0, 0, g)), pl.BlockSpec((L, pg), lambda g: (0, g)),
                  pl.BlockSpec((None, pg, pg), lambda g: (g, 0, 0)), pl.BlockSpec((1, pg), lambda g: (0, g))],
        out_specs=(pl.BlockSpec((L, pg), lambda g: (0, g)), pl.BlockSpec((None, pg, pg), lambda g: (g, 0, 0)),
                   pl.BlockSpec((1, pg), lambda g: (0, g))),
        compiler_params=_cparams(("parallel",)),
    )(proj3, dmix, pool_w, scale.reshape(1, C))


def _tril_w(w_ref, h):
    r = lax.broadcasted_iota(jnp.int32, (CHUNK, CHUNK), 0)
    c = lax.broadcasted_iota(jnp.int32, (CHUNK, CHUNK), 1)
    return jnp.where(r >= c, w_ref[h], 0.0)


def _sgu_fwd(proj3, norm_g, w, b, mixin, mixin_t, name):
    _, L, C = proj3.shape
    nh = w.shape[0]
    dh = C // nh
    assert dh == LANES and w.shape[1] == CHUNK
    tr = _pick(L, prefs=(512, 256, 128))
    bfull = jnp.broadcast_to(b[:, :, None], (nh, CHUNK, dh))

    def body(su_ref, sv_ref, g_ref, w_ref, b_ref, m_in, mt_in, o_ref, ot_ref):
        sv = _gelu(sv_ref[...].astype(F32))
        r = lax.rsqrt(jnp.mean(sv * sv, axis=-1, keepdims=True) + EPS)
        v = (sv * r * g_ref[...]).astype(BF16)
        for h in range(nh):
            wm = _tril_w(w_ref, h).astype(BF16)
            cols = slice(h * dh, (h + 1) * dh)
            for n in range(tr // CHUNK):
                rows = slice(n * CHUNK, (n + 1) * CHUNK)
                mixed = jnp.dot(wm, v[rows, cols], preferred_element_type=F32) + b_ref[h]
                o_ref[rows, cols] = (_gelu(su_ref[rows, cols].astype(F32)) * mixed).astype(BF16)
        ot_ref[...] = _transpose_on_mxu(o_ref[...])

    full = lambda shp: pl.BlockSpec(shp, lambda i: (0,) * len(shp))
    anywhere = pl.BlockSpec(memory_space=pl.ANY)
    return pl.pallas_call(
        body, name=name, out_shape=(jax.ShapeDtypeStruct(mixin.shape, BF16), jax.ShapeDtypeStruct(mixin_t.shape, BF16)),
        grid=(L // tr,),
        in_specs=[pl.BlockSpec((None, tr, C), lambda i: (1, i, 0)), pl.BlockSpec((None, tr, C), lambda i: (2, i, 0)),
                  full((1, C)), full((nh, CHUNK, CHUNK)), full((nh, CHUNK, dh)), anywhere, anywhere],
        out_specs=(pl.BlockSpec((tr, C), lambda i: (i, 1)), pl.BlockSpec((C, tr), lambda i: (1, i))),
        input_output_aliases={5: 0, 6: 1}, compiler_params=_cparams(("parallel",)),
    )(proj3, proj3, norm_g.reshape(1, C), w, bfull, mixin, mixin_t)


def _sgu_bwd(proj3, dmix, norm_g, w, b, name):
    _, L, C = proj3.shape
    nh = w.shape[0]
    dh = C // nh
    tr = _pick(L, prefs=(512, 256, 128))
    nsteps = L // tr
    bfull = jnp.broadcast_to(b[:, :, None], (nh, CHUNK, dh))

    def body(su_ref, sv_ref, d_ref, g_ref, w_ref, b_ref, o_ref, dw_ref, db_ref, dg_ref, dv_ref, acc_g):
        i = pl.program_id(0)

        @pl.when(i == 0)
        def _():
            dw_ref[...] = jnp.zeros_like(dw_ref)
            db_ref[...] = jnp.zeros_like(db_ref)
            acc_g[...] = jnp.zeros_like(acc_g)

        svp = sv_ref[...].astype(F32)
        sv = _gelu(svp)
        r = lax.rsqrt(jnp.mean(sv * sv, axis=-1, keepdims=True) + EPS)
        vh = sv * r
        gv = g_ref[...]
        v = (vh * gv).astype(BF16)
        tri_r = lax.broadcasted_iota(jnp.int32, (CHUNK, CHUNK), 0)
        tri_c = lax.broadcasted_iota(jnp.int32, (CHUNK, CHUNK), 1)
        for h in range(nh):
            wm = _tril_w(w_ref, h).astype(BF16)
            cols = slice(h * dh, (h + 1) * dh)
            dwh = jnp.zeros((CHUNK, CHUNK), F32)
            dbh = jnp.zeros((CHUNK, dh), F32)
            for n in range(tr // CHUNK):
                rows = slice(n * CHUNK, (n + 1) * CHUNK)
                vb = v[rows, cols]
                mixed = jnp.dot(wm, vb, preferred_element_type=F32) + b_ref[h]
                sup = su_ref[rows, cols].astype(F32)
                dyd = d_ref[rows, cols]
                dmx = dyd * _gelu(sup)
                o_ref[0, rows, cols] = (dyd * mixed * _gelu_grad(sup)).astype(BF16)
                dmb = dmx.astype(BF16)
                dwh = dwh + lax.dot_general(dmb, vb, (((1,), (1,)), ((), ())), preferred_element_type=F32)
                dbh = dbh + dmx
                dv_ref[rows, cols] = lax.dot_general(wm, dmb, (((0,), (0,)), ((), ())), preferred_element_type=F32)
            dw_ref[h] += jnp.where(tri_r >= tri_c, dwh, 0.0)
            db_ref[h] += dbh
        dv = dv_ref[...]
        acc_g[...] += jnp.sum((dv * vh).reshape(tr // SUBLANES, SUBLANES, C), axis=0)
        dvg = dv * gv
        dsv = r * (dvg - vh * jnp.mean(dvg * vh, axis=-1, keepdims=True))
        o_ref[1] = (dsv * _gelu_grad(svp)).astype(BF16)

        @pl.when(i == nsteps - 1)
        def _():
            dg_ref[...] = jnp.sum(acc_g[...], axis=0, keepdims=True)

    full = lambda shp: pl.BlockSpec(shp, lambda i: (0,) * len(shp))
    return pl.pallas_call(
        body, name=name,
        out_shape=(jax.ShapeDtypeStruct((2, L, C), BF16), jax.ShapeDtypeStruct((nh, CHUNK, CHUNK), F32),
                   jax.ShapeDtypeStruct((nh, CHUNK, dh), F32), jax.ShapeDtypeStruct((1, C), F32)),
        grid=(nsteps,),
        in_specs=[pl.BlockSpec((None, tr, C), lambda i: (1, i, 0)), pl.BlockSpec((None, tr, C), lambda i: (2, i, 0)),
                  pl.BlockSpec((tr, C), lambda i: (i, 1)), full((1, C)), full((nh, CHUNK, CHUNK)),
                  full((nh, CHUNK, dh))],
        out_specs=(pl.BlockSpec((2, tr, C), lambda i: (0, i, 0)), full((nh, CHUNK, CHUNK)), full((nh, CHUNK, dh)),
                   full((1, C))),
        scratch_shapes=[pltpu.VMEM((tr, C), F32), pltpu.VMEM((SUBLANES, C), F32)],
        compiler_params=_cparams(("arbitrary",)),
    )(proj3, proj3, dmix, norm_g.reshape(1, C), w, bfull)


def _ffn_act_fwd(up3, conv_w, conv_b, name):
    _, L, Fh = up3.shape
    cb = LANES
    w2 = conv_w.reshape(3, 2, Fh).transpose(1, 0, 2)
    b2 = conv_b.reshape(2, 1, Fh)

    def body(u_ref, w_ref, b_ref, o_ref, ot_ref, gv_ref):
        g = _conv3(_taps(u_ref[0].astype(F32)), w_ref[0]) + b_ref[0]
        v = _conv3(_taps(u_ref[1].astype(F32)), w_ref[1]) + b_ref[1]
        gv_ref[0] = g.astype(BF16)
        gv_ref[1] = v.astype(BF16)
        ab = (g * _sigmoid(g) * v).astype(BF16)
        o_ref[...] = ab
        ot_ref[...] = _transpose_on_mxu(ab)

    blk3 = pl.BlockSpec((2, L, cb), lambda j: (0, 0, j))
    return pl.pallas_call(
        body, name=name,
        out_shape=(jax.ShapeDtypeStruct((L, Fh), BF16), jax.ShapeDtypeStruct((Fh, L), BF16),
                   jax.ShapeDtypeStruct((2, L, Fh), BF16)),
        grid=(Fh // cb,),
        in_specs=[blk3, pl.BlockSpec((2, 3, cb), lambda j: (0, 0, j)), pl.BlockSpec((2, 1, cb), lambda j: (0, 0, j))],
        out_specs=(pl.BlockSpec((L, cb), lambda j: (0, j)), pl.BlockSpec((cb, L), lambda j: (j, 0)), blk3),
        compiler_params=_cparams(("parallel",)),
    )(up3, w2, b2)


def _ffn_act_bwd(up3, gv3, da, conv_w, name):
    _, L, Fh = up3.shape
    cb = LANES
    w2 = conv_w.reshape(3, 2, Fh).transpose(1, 0, 2)

    def body(u_ref, gv_ref, d_ref, w_ref, o_ref, dw_ref, db_ref):
        tg, tv = _taps(u_ref[0].astype(F32)), _taps(u_ref[1].astype(F32))
        wg, wv = w_ref[0], w_ref[1]
        g = gv_ref[0].astype(F32)
        v = gv_ref[1].astype(F32)
        sg = _sigmoid(g)
        dav = d_ref[...].astype(F32)
        dg = dav * v * (sg * (1.0 + g * (1.0 - sg)))
        dv = dav * (g * sg)
        o_ref[0] = _conv3_t(dg, wg).astype(BF16)
        o_ref[1] = _conv3_t(dv, wv).astype(BF16)
        for tap, (dwg, dwv) in enumerate(zip(_conv3_dw(dg, tg), _conv3_dw(dv, tv))):
            dw_ref[0, tap:tap + 1, :] = dwg
            dw_ref[1, tap:tap + 1, :] = dwv
        db_ref[0] = jnp.sum(dg, axis=0, keepdims=True)
        db_ref[1] = jnp.sum(dv, axis=0, keepdims=True)

    dup, dw2, db2 = pl.pallas_call(
        body, name=name,
        out_shape=(jax.ShapeDtypeStruct((2, L, Fh), BF16), jax.ShapeDtypeStruct((2, 3, Fh), F32),
                   jax.ShapeDtypeStruct((2, 1, Fh), F32)),
        grid=(Fh // cb,),
        in_specs=[pl.BlockSpec((2, L, cb), lambda j: (0, 0, j)), pl.BlockSpec((2, L, cb), lambda j: (0, 0, j)),
                  pl.BlockSpec((L, cb), lambda j: (0, j)), pl.BlockSpec((2, 3, cb), lambda j: (0, 0, j))],
        out_specs=(pl.BlockSpec((2, L, cb), lambda j: (0, 0, j)), pl.BlockSpec((2, 3, cb), lambda j: (0, 0, j)),
                   pl.BlockSpec((2, 1, cb), lambda j: (0, 0, j))),
        compiler_params=_cparams(("parallel",)),
    )(up3, gv3, da, w2)
    return dup, dw2.transpose(1, 0, 2).reshape(3, 2 * Fh), db2.reshape(2 * Fh)


def _local_step(x, tgt, w, layer_weights, on_layer_grads):
    L, D = x.shape
    depth = w['norm_mix_g'].shape[0]
    saved = []
    for i in range(depth):
        j = i // 2
        wb = dict(layer_weights(2 * i, x))
        s = {'x': x, 'wb': wb}
        if i % 2 == 0:
            proj4, s['hT'] = _norm_mm(x, w['norm_mix_g'][i], wb['even_w_in'], BF16, "even_in_fwd", ok=('seg', 4))
            s['proj'] = proj4
            mixin = _sconv_fwd(proj4, w['even_conv_w'][j], "sconv_fwd")
            prm = (w['ssm_log_step'][j], w['ssm_a_re'][j], w['ssm_a_im'][j], w['ssm_b_re'][j], w['ssm_b_im'][j],
                   w['ssm_c_re'][j], w['ssm_c_im'][j])
            (lr, li, bmat, cmat), prep_vjp = jax.vjp(_s5_prep, *prm)
            yraw, s_re, s_im = _s5_fwd(proj4, lr, li, bmat, cmat, w['ssm_d'][j], "s5_fwd")
            mixin = _glu_fwd(yraw, wb['ssm_glu_w'], w['ssm_glu_b'][j], mixin, "glu_fwd")
            s.update(yraw=yraw, s_re=s_re, s_im=s_im, s5=(lr, li, bmat, cmat), prep_vjp=prep_vjp)
            s['mixinT'] = mixin.T
            x = _mm(mixin, wb['even_w_out'], 'nn', F32, "even_out_fwd", res=x)
        else:
            proj3, s['hT'] = _norm_mm(x, w['norm_mix_g'][i], wb['odd_w_in'], BF16, "odd_in_fwd", ok=('seg', 3))
            s['proj'] = proj3
            mixin, mixin_t = _pool_fwd(proj3, w['pool_w'][j], w['pool_scale'][j], "pool_fwd")
            mixin, s['mixinT'] = _sgu_fwd(proj3, w['sgu_norm_g'][j], w['sgu_w'][j], w['sgu_b'][j], mixin, mixin_t,
                                          "sgu_fwd")
            x = _mm(mixin, wb['odd_w_out'], 'nn', F32, "odd_out_fwd", res=x)
        s['x1'] = x
        wb.update(layer_weights(2 * i + 1, x))
        up3, h2t = _norm_mm(x, w['norm_ffn_g'][i], wb['ffn_w_up'], BF16, "ffn_up_fwd", ok=('seg', 2))
        a, at, gv3 = _ffn_act_fwd(up3, w['ffn_conv_w'][i], w['ffn_conv_b'][i], "ffn_act_fwd")
        x = _mm(a, wb['ffn_w_down'], 'nn', F32, "ffn_down_fwd", res=x)
        s.update(h2T=h2t, up3=up3, aT=at, gv3=gv3)
        saved.append(s)

    loss8, dx, dxb, dg_final = _loss_head(x, w['norm_final_g'], tgt)
    gs = {n: [None] * w[n].shape[0] for n in SMALL if n != 'norm_final_g'}
    gs['norm_final_g'] = dg_final.reshape(D)

    dep = None
    for i in reversed(range(depth)):
        j = i // 2
        s = saved[i]
        wb = s['wb']
        gb = {}
        da = _mm(dxb, wb['ffn_w_down'], 'nt', BF16, "ffn_down_dgrad", dep=dep)
        gb['ffn_w_down'] = _mm(s['aT'], dxb, 'nn', BF16, "ffn_down_wgrad")
        dup3, dcw, dcb = _ffn_act_bwd(s['up3'], s['gv3'], da, w['ffn_conv_w'][i], "ffn_act_bwd")
        gs['ffn_conv_w'][i], gs['ffn_conv_b'][i] = dcw, dcb
        gb['ffn_w_up'] = _mm(s['h2T'], dup3, 'nn', BF16, "ffn_up_wgrad", bk=('seg', 2))
        dep = on_layer_grads(2 * i + 1, gb)
        dx, dxb, dg = _mm_norm_bwd(dup3, wb['ffn_w_up'], s['x1'], w['norm_ffn_g'][i], dx, "ffn_up_dgrad",
                              ak=('seg', 2), dep=dep)
        gs['norm_ffn_g'][i] = dg.reshape(D)
        gb = {}
        if i % 2 == 0:
            dmix = _mm(dxb, wb['even_w_out'], 'nt', F32, "even_out_dgrad")
            gb['even_w_out'] = _mm(s['mixinT'], dxb, 'nn', BF16, "even_out_wgrad")
            dproj, dcw = _sconv_bwd(s['proj'], dmix, w['even_conv_w'][j], "sconv_bwd")
            gs['even_conv_w'][j] = dcw
            dyraw, dglu_w, dglu_b = _glu_bwd(s['yraw'], dmix, wb['ssm_glu_w'], w['ssm_glu_b'][j], "glu_bwd")
            gb['ssm_glu_w'] = dglu_w.astype(BF16)
            gs['ssm_glu_b'][j] = dglu_b.reshape(-1)
            lr, li, bmat, cmat = s['s5']
            dproj, dbm, dcm, dlam, dd = _s5_bwd(dyraw, s['proj'], dproj, s['s_re'], s['s_im'], lr, li, bmat, cmat,
                                               w['ssm_d'][j], "s5_bwd")
            gs['ssm_d'][j] = dd.reshape(-1)
            dcm = jnp.swapaxes(dcm, 1, 2)
            dprm = s['prep_vjp']((dlam[:, 0:1, :], dlam[:, 1:2, :], dbm, dcm))
            for n, gval in zip(('ssm_log_step', 'ssm_a_re', 'ssm_a_im', 'ssm_b_re', 'ssm_b_im', 'ssm_c_re',
                                'ssm_c_im'), dprm):
                gs[n][j] = gval
            gb['even_w_in'] = _mm(s['hT'], dproj, 'nn', BF16, "even_in_wgrad", bk=('seg', 4))
            w_in, in_kind, in_name = wb['even_w_in'], ('seg', 4), "even_in_dgrad"
        else:
            dmix = _mm(dxb, wb['odd_w_out'], 'nt', F32, "odd_out_dgrad")
            gb['odd_w_out'] = _mm(s['mixinT'], dxb, 'nn', BF16, "odd_out_wgrad")
            dz, dpw, dps = _pool_bwd(s['proj'], dmix, w['pool_w'][j], w['pool_scale'][j], "pool_bwd")
            gs['pool_w'][j], gs['pool_scale'][j] = dpw, dps.reshape(-1)
            dsuv, dsw, dsb, dsg = _sgu_bwd(s['proj'], dmix, w['sgu_norm_g'][j], w['sgu_w'][j], w['sgu_b'][j],
                                           "sgu_bwd")
            gs['sgu_w'][j], gs['sgu_b'][j], gs['sgu_norm_g'][j] = dsw, jnp.sum(dsb, axis=-1), dsg.reshape(-1)
            dproj = jnp.concatenate([dz[None], dsuv], axis=0)
            gb['odd_w_in'] = _mm(s['hT'], dproj, 'nn', BF16, "odd_in_wgrad", bk=('seg', 3))
            w_in, in_kind, in_name = wb['odd_w_in'], ('seg', 3), "odd_in_dgrad"
        dep = on_layer_grads(2 * i, gb)
        dx, dxb, dg = _mm_norm_bwd(dproj, w_in, s['x'], w['norm_mix_g'][i], dx, in_name, ak=in_kind, dep=dep)
        gs['norm_mix_g'][i] = dg.reshape(D)

    gsmall = {n: (v if n == 'norm_final_g' else jnp.stack(v)) for n, v in gs.items()}
    return loss8[0, 0], dx, gsmall


_HBM = pl.BlockSpec(memory_space=pltpu.HBM)
_CHIP_FLIPS = ((0, 0), (1, 0), (0, 1), (1, 1))


def _coords():
    return lax.axis_index("x"), lax.axis_index("y"), lax.axis_index("c")


def _flip(v, f):
    return 1 - v if f else v


def _shard_of(ref, axis, s, width):
    start = pl.multiple_of(s * width, LANES if axis == ref.ndim - 1 else 16) if width % 16 == 0 else s * width
    idx = [slice(None)] * ref.ndim
    idx[axis] = pl.ds(start, width)
    return ref.at[tuple(idx)]


_SEM = pl.BlockSpec(memory_space=pltpu.SEMAPHORE)
_ANY = pl.BlockSpec(memory_space=pl.ANY)
_DATAFLOW = pltpu.SideEffectType.DATAFLOW_SIDE_EFFECTING


def _in_hbm(a):
    return pltpu.with_memory_space_constraint(a, pltpu.HBM)


def _model_layer(name, l):
    if name.startswith('ffn'):
        return l
    return 2 * l + 1 if name.startswith('odd') else 2 * l


def _place_quarter(shard, l, axis, chip, dtype, dep=None):
    _, r, c = shard.shape
    tr = _pick(r, prefs=(512, 256, 128, 64, 32, 16))
    nrb = r // tr

    def body(chip_ref, i_ref, *rest):
        rest[-1][...] = i_ref[...].astype(dtype)

    if axis == 1:
        out_shape, o_map = (r, c * N_CHIPS), (lambda i, s: (i, s[0]))
    else:
        out_shape, o_map = (r * N_CHIPS, c), (lambda i, s: (s[0] * nrb + i, 0))
    in_specs = [pl.BlockSpec((None, tr, c), lambda i, s: (l, i, 0))]
    args = [chip, shard]
    if dep is not None:
        in_specs.append(pl.BlockSpec(memory_space=pl.ANY))
        args.append(dep)
    return pl.pallas_call(
        body, name="place_quarter", out_shape=jax.ShapeDtypeStruct(out_shape, dtype),
        grid_spec=pltpu.PrefetchScalarGridSpec(
            num_scalar_prefetch=1, grid=(nrb,), in_specs=in_specs, out_specs=pl.BlockSpec((tr, c), o_map)),
        compiler_params=_cparams(("parallel",)),
    )(*args)


def _gather_copies(land_refs, send_sem, recv_sem, axes, landing_chip_of):
    x, y, c = _coords()
    out = []
    for j, land in enumerate(land_refs):
        width = land.shape[axes[j]] // N_CHIPS
        for f in (1, 2, 3):
            fx, fy = _CHIP_FLIPS[f]
            px, py = _flip(x, fx), _flip(y, fy)
            lx, ly = landing_chip_of(px, py)
            out.append(pltpu.make_async_remote_copy(
                src_ref=_shard_of(land, axes[j], 2 * x + y, width), dst_ref=_shard_of(land, axes[j], 2 * lx + ly, width),
                send_sem=send_sem.at[3 * j + f - 1], recv_sem=recv_sem.at[3 * j + f - 1],
                device_id=(px, py, c), device_id_type=MESH))
    return out


def _gather_start(tag, lands, axes, dep=None):
    n = len(lands)

    def body(*refs):
        land_refs, send_sem, recv_sem = refs[:n], refs[-3], refs[-2]
        x, y, _ = _coords()
        for cp in _gather_copies(land_refs, send_sem, recv_sem, axes, lambda px, py: (x, y)):
            cp.start()
        refs[-1][...] = jnp.zeros_like(refs[-1])

    thru = [pltpu.HBM(a.shape, a.dtype) for a in lands]
    outs = pl.pallas_call(
        body, name=f"gather_start_{tag}",
        out_shape=tuple(thru + [pltpu.SemaphoreType.DMA((3 * n,)), pltpu.SemaphoreType.DMA((3 * n,)),
                                jax.ShapeDtypeStruct((SUBLANES, LANES), F32)]),
        in_specs=[_HBM] * n + ([_ANY] if dep is not None else []),
        out_specs=tuple([_HBM] * n + [_SEM, _SEM, pl.BlockSpec(memory_space=pltpu.VMEM)]),
        input_output_aliases={i: i for i in range(n)},
        compiler_params=pltpu.CompilerParams(has_side_effects=_DATAFLOW),
    )(*[_in_hbm(a) for a in lands], *([dep] if dep is not None else []))
    return list(outs[:n]), outs[n], outs[n + 1], outs[n + 2]


def _gather_wait(tag, lands, send_sem, recv_sem, axes, after):
    n = len(lands)

    def body(*refs):
        for cp in _gather_copies(refs[:n], refs[n], refs[n + 1], axes, lambda px, py: (px, py)):
            cp.wait_send()
            cp.wait_recv()

    outs = pl.pallas_call(
        body, name=f"gather_wait_{tag}", out_shape=tuple(pltpu.HBM(a.shape, a.dtype) for a in lands),
        in_specs=[_HBM] * n + [_SEM, _SEM, _ANY], out_specs=tuple([_HBM] * n),
        input_output_aliases={i: i for i in range(n)},
        compiler_params=pltpu.CompilerParams(has_side_effects=_DATAFLOW),
    )(*lands, send_sem, recv_sem, after)
    return list(outs)


N_SLOTS = N_DEV - 1


def _scatter_sends(grad_refs, land_refs, send_sem, recv_sem, meta):
    x, y, c = _coords()
    out = []
    for j, (axis, owner, q, width) in enumerate(meta):
        other = c if owner == 0 else 1 - c
        for f, (fx, fy) in enumerate(_CHIP_FLIPS):
            px, py = _flip(x, fx), _flip(y, fy)
            slot = f + 4 * other - 1
            out.append((other if f == 0 else None, pltpu.make_async_remote_copy(
                src_ref=_shard_of(grad_refs[j], axis, 2 * px + py, width), dst_ref=land_refs[j].at[q, slot],
                send_sem=send_sem.at[4 * j + f], recv_sem=recv_sem.at[N_SLOTS * j + slot],
                device_id=(px, py, owner), device_id_type=MESH)))
    return out


def _scatter_start(layer, grads, lands, meta):
    n = len(grads)
    uniq = []
    for a in lands:
        if not any(a is u for u in uniq):
            uniq.append(a)
    which = [next(k for k, u in enumerate(uniq) if u is a) for a in lands]
    nu = len(uniq)

    def body(*refs):
        grad_refs, land_u = refs[:n], refs[n:n + nu]
        send_sem, recv_sem = refs[n + nu], refs[n + nu + 1]
        for other, cp in _scatter_sends(grad_refs, [land_u[k] for k in which], send_sem, recv_sem, meta):
            if other is None:
                cp.start()
            else:
                pl.when(other == 1)(cp.start)
        refs[-1][...] = jnp.zeros_like(refs[-1])

    thru = [pltpu.HBM(a.shape, a.dtype) for a in list(grads) + uniq]
    outs = pl.pallas_call(
        body, name=f"scatter_start_{layer}",
        out_shape=tuple([pltpu.SemaphoreType.DMA((4 * n,)), pltpu.SemaphoreType.DMA((N_SLOTS * n,))] + thru
                        + [jax.ShapeDtypeStruct((SUBLANES, LANES), F32)]),
        in_specs=[_HBM] * (n + nu),
        out_specs=tuple([_SEM, _SEM] + [_HBM] * (n + nu) + [pl.BlockSpec(memory_space=pltpu.VMEM)]),
        input_output_aliases={i: 2 + i for i in range(n + nu)},
        compiler_params=pltpu.CompilerParams(has_side_effects=_DATAFLOW),
    )(*[_in_hbm(a) for a in list(grads) + uniq])
    new_lands = [outs[2 + n + k] for k in which]
    return outs[0], outs[1], list(outs[2:2 + n]), new_lands, outs[-1]


def _scatter_wait(started, lands):
    nl = len(lands)
    flat_grads = [g for s in started for g in s[2]]
    ng, ns = len(flat_grads), len(started)

    def body(*refs):
        land_refs = refs[:nl]
        grad_refs = refs[nl:nl + ng]
        sem_refs = refs[nl + ng:nl + ng + 2 * ns]
        _, _, c = _coords()
        off = 0
        for k, (_, _, grads, idx, meta) in enumerate(started):
            send_sem, recv_sem = sem_refs[2 * k], sem_refs[2 * k + 1]
            lr = [land_refs[i] for i in idx]
            for other, cp in _scatter_sends(grad_refs[off:off + len(grads)], lr, send_sem, recv_sem, meta):
                if other is None:
                    cp.wait_send()
                else:
                    pl.when(other == 1)(cp.wait_send)
            for j, (axis, owner, q, width) in enumerate(meta):
                mine = (c if owner == 0 else 1 - c) == 0

                @pl.when(mine)
                def _():
                    for slot in range(N_SLOTS):
                        land = lr[j].at[q, slot]
                        pltpu.make_async_remote_copy(
                            src_ref=land, dst_ref=land, send_sem=send_sem.at[0], recv_sem=recv_sem.at[N_SLOTS * j + slot],
                            device_id=_coords(), device_id_type=MESH).wait_recv()
            off += len(grads)

    args = list(lands) + flat_grads
    thru = [pltpu.HBM(a.shape, a.dtype) for a in args]
    sems = [s for st in started for s in st[:2]]
    outs = pl.pallas_call(
        body, name="scatter_wait", out_shape=tuple(thru), in_specs=[_HBM] * (nl + ng) + [_SEM] * (2 * ns),
        out_specs=tuple([_HBM] * (nl + ng)), input_output_aliases={i: i for i in range(nl + ng)},
        compiler_params=pltpu.CompilerParams(has_side_effects=_DATAFLOW),
    )(*args, *sems)
    return list(outs[:nl]), list(outs[nl:])


def _sum_and_share(recv, layer_grads, axis, chip, name):
    n, ns, r, c = recv.shape
    tr = _pick(r, prefs=(256, 128, 64, 32, 16))
    nr = r // tr
    nsteps = n * nr
    nlay = len(layer_grads)
    own_map = (lambda h, i, s: (i, s[0])) if axis == 1 else (lambda h, i, s: (s[0] * nr + i, 0))

    def body(chip_ref, i_ref, *rest):
        g_refs = rest[:nlay]
        o_ref, buf, loc_sems, send_sems, recv_sems = rest[nlay:]
        h, i = pl.program_id(0), pl.program_id(1)
        step = h * nr + i
        slot = step % 2
        x, y, core = _coords()
        layer = core * n + h
        own = g_refs[0][...]
        for l in range(1, nlay):
            own = jnp.where(layer == l, g_refs[l][...], own)

        def copies(sl):
            dst = o_ref.at[core * n + h, pl.ds(pl.multiple_of(i * tr, tr), tr), :]
            loc = pltpu.make_async_copy(buf.at[sl], dst, loc_sems.at[sl])
            rem = pltpu.make_async_remote_copy(
                src_ref=buf.at[sl], dst_ref=dst, send_sem=send_sems.at[sl], recv_sem=recv_sems.at[step],
                device_id=(x, y, 1 - core), device_id_type=MESH)
            return loc, rem

        def drain(sl):
            loc, rem = copies(sl)
            loc.wait()
            rem.wait_send()

        pl.when(step >= 2)(lambda: drain(slot))
        acc = own.astype(F32)
        for s in range(ns):
            acc = acc + i_ref[s].astype(F32)
        buf[slot] = acc
        loc, rem = copies(slot)
        loc.start()
        rem.start()

        @pl.when(step == nsteps - 1)
        def _():
            drain(slot)
            if nsteps > 1:
                drain(1 - slot)
            for hh in range(n):
                for ii in range(nr):
                    land = o_ref.at[(1 - core) * n + hh, pl.ds(ii * tr, tr), :]
                    pltpu.make_async_remote_copy(
                        src_ref=buf.at[0], dst_ref=land, send_sem=send_sems.at[0], recv_sem=recv_sems.at[hh * nr + ii],
                        device_id=(x, y, 1 - core), device_id_type=MESH).wait_recv()

    return pl.pallas_call(
        body, name=name, out_shape=jax.ShapeDtypeStruct((2 * n, r, c), F32),
        grid_spec=pltpu.PrefetchScalarGridSpec(
            num_scalar_prefetch=1, grid=(n, nr),
            in_specs=[pl.BlockSpec((None, ns, tr, c), lambda h, i, s: (h, 0, i, 0))]
            + [pl.BlockSpec((tr, c), own_map)] * nlay,
            out_specs=_HBM,
            scratch_shapes=[pltpu.VMEM((2, tr, c), F32), pltpu.SemaphoreType.DMA((2,)),
                            pltpu.SemaphoreType.DMA((2,)), pltpu.SemaphoreType.DMA((nsteps,))]),
        compiler_params=_cparams(("arbitrary", "arbitrary")),
    )(chip, recv, *layer_grads)


def _gather_sums_over_chips(part):
    def body(i_ref, o_ref, send_sems, recv_sems):
        x, y, c = _coords()
        o_ref[2 * x + y] = i_ref[...]

        def copy(f, slot_chip):
            fx, fy = _CHIP_FLIPS[f]
            return pltpu.make_async_remote_copy(
                src_ref=i_ref, dst_ref=o_ref.at[2 * slot_chip[0] + slot_chip[1]], send_sem=send_sems.at[f - 1],
                recv_sem=recv_sems.at[f - 1], device_id=(_flip(x, fx), _flip(y, fy), c), device_id_type=MESH)

        sends = [copy(f, (x, y)) for f in (1, 2, 3)]
        for cp in sends:
            cp.start()
        for f in (1, 2, 3):
            fx, fy = _CHIP_FLIPS[f]
            copy(f, (_flip(x, fx), _flip(y, fy))).wait_recv()
        for cp in sends:
            cp.wait_send()

    vmem = pl.BlockSpec(memory_space=pltpu.VMEM)
    return pl.pallas_call(
        body, name="gather_small_sums", out_shape=jax.ShapeDtypeStruct((N_CHIPS,) + part.shape, part.dtype),
        in_specs=[vmem], out_specs=vmem,
        scratch_shapes=[pltpu.SemaphoreType.DMA((3,)), pltpu.SemaphoreType.DMA((3,))],
    )(part)


def _adamw(w, g, m, v, name):
    bc1 = 1.0 - ADAM_B1 ** ADAM_STEP
    bc2 = 1.0 - ADAM_B2 ** ADAM_STEP

    def body(w_ref, g_ref, m_ref, v_ref, d_ref, mo_ref, vo_ref):
        gv = g_ref[...]
        mn = ADAM_B1 * m_ref[...] + (1.0 - ADAM_B1) * gv
        vn = ADAM_B2 * v_ref[...] + (1.0 - ADAM_B2) * (gv * gv)
        d_ref[...] = -ADAM_LR * ((mn / bc1) / (jnp.sqrt(vn / bc2) + ADAM_EPS) + ADAM_WD * w_ref[...])
        mo_ref[...] = mn
        vo_ref[...] = vn

    sds = jax.ShapeDtypeStruct(w.shape, F32)
    if w.ndim == 2 and w.shape[0] % SUBLANES == 0:
        tr = _pick(w.shape[0], prefs=(256, 128, 64, 32, 16, 8))
        grid, blk = (w.shape[0] // tr,), pl.BlockSpec((tr, w.shape[1]), lambda i: (i, 0))
    else:
        nd = w.ndim
        grid, blk = (1,), pl.BlockSpec(w.shape, lambda i: (0,) * nd)
    return pl.pallas_call(
        body, name=name, out_shape=(sds, sds, sds), grid=grid, in_specs=[blk] * 4, out_specs=(blk,) * 3,
        compiler_params=_cparams(("parallel",)),
    )(w, g, m, v)


_PACK_QUANTUM = 256 * LANES


def _pack(arrs):
    flat = jnp.concatenate([a.reshape(-1).astype(F32) for a in arrs])
    flat = jnp.pad(flat, (0, (-flat.shape[0]) % _PACK_QUANTUM))
    return flat.reshape(-1, LANES)


def _unpack(p, shapes):
    flat = p.reshape(-1)
    out, off = [], 0
    for s in shapes:
        n = int(np.prod(s))
        out.append(flat[off:off + n].reshape(s))
        off += n
    return out


def kernel(*args):
    nw = len(WEIGHTS)
    x, tgt = args[0], args[1 + nw]
    w = dict(zip(WEIGHTS, args[1:1 + nw]))
    m = dict(zip(WEIGHTS, args[2 + nw:2 + 2 * nw]))
    v = dict(zip(WEIGHTS, args[2 + 2 * nw:2 + 3 * nw]))
    _, L, D = x.shape
    chip = 2 * lax.axis_index("x") + lax.axis_index("y")

    big = list(BIG)
    small_sh_shapes = [w[n].shape for n in SMALL_SHARDED]
    nbig = len(big)
    chip1 = chip.reshape(1).astype(jnp.int32)
    axes2 = [BIG[n] - 1 for n in big] + [0]
    shards = [w[n] for n in big] + [_pack([w[n] for n in SMALL_SHARDED])[None]]
    pairs = [(t, l) for t in range(nbig + 1) for l in range(shards[t].shape[0])]
    depth = w['norm_mix_g'].shape[0]
    part_of = lambda t, l: 0 if t == nbig else 2 * _model_layer(big[t], l) + big[t].startswith('ffn')
    flying, token = [], None
    for g in range(2 * depth):
        ids = [k for k, (t, l) in enumerate(pairs) if part_of(t, l) == g]
        ts = [pairs[k][0] for k in ids]
        placed = [_place_quarter(shards[t], pairs[k][1], axes2[t], chip1, F32 if t == nbig else BF16, token)
                  for k, t in zip(ids, ts)]
        lands, send, recv, token = _gather_start(g, placed, [axes2[t] for t in ts], token)
        flying.append((ts, lands, send, recv))

    def wait_group(g, after):
        ts, lands, send, recv = flying[g]
        landed = _gather_wait(g, lands, send, recv, [axes2[t] for t in ts], token if after is None else after)
        return dict(zip(ts, landed))

    first = wait_group(0, None)
    packed = first.pop(nbig).reshape(N_CHIPS, -1, LANES)
    per_chip = [_unpack(packed[s], small_sh_shapes) for s in range(N_CHIPS)]
    wl = dict(w)
    for k, n in enumerate(SMALL_SHARDED):
        wl[n] = jnp.concatenate([per_chip[s][k] for s in range(N_CHIPS)], axis=-1)

    def layer_weights(i, after):
        got = first if i == 0 else wait_group(i, after)
        return {big[t]: a for t, a in got.items()}

    small_shapes = [(w[n].shape[:-1] + (w[n].shape[-1] * N_CHIPS,)) if n in SMALL_SHARDED else w[n].shape
                    for n in SMALL] + [(1,)]
    n_small = sum(int(np.prod(s)) for s in small_shapes)
    pack_rows = -(-n_small // _PACK_QUANTUM) * _PACK_QUANTUM // LANES
    nlayers = [w[n].shape[0] for n in big] + [2]
    halves = [n // 2 for n in nlayers]
    quarters = [tuple(w[n].shape[1:]) for n in big] + [(pack_rows // 2 // N_CHIPS, LANES)]
    wire = [BF16] * nbig + [F32]
    land_now = [lax.empty((halves[t], N_SLOTS) + quarters[t], wire[t]) for t in range(nbig + 1)]
    gparts = [[None] * n for n in nlayers]
    started = []

    def start_scatter(tag, ts, ls, arrays):
        meta = [(axes2[t], l // halves[t], l % halves[t], quarters[t][axes2[t]]) for t, l in zip(ts, ls)]
        send, recv, thru, new_lands, token = _scatter_start(tag, arrays, [land_now[t] for t in ts], meta)
        for t, ln in zip(ts, new_lands):
            land_now[t] = ln
        started.append((send, recv, thru, ts, meta, ls))
        return token

    def on_layer_grads(g, gb):
        ts = [big.index(n) for n in gb]
        return start_scatter(g, ts, [g // 2 if big[t].startswith('ffn') else g // 4 for t in ts],
                             [gb[big[t]] for t in ts])

    loss, dx, gsmall = _local_step(x.reshape(L, D), tgt.reshape(L, D), wl, layer_weights, on_layer_grads)
    gpack = _pack([gsmall[n] for n in SMALL] + [loss.reshape(1)])
    start_scatter(2 * depth, [nbig, nbig], [0, 1], [gpack[:pack_rows // 2], gpack[pack_rows // 2:]])
    landed, sent = _scatter_wait([s[:5] for s in started], land_now)
    for (t, l), g in zip([(t, l) for s in started for t, l in zip(s[3], s[5])], sent):
        gparts[t][l] = g
    gshard = {n: _sum_and_share(landed[t], gparts[t], axes2[t], chip1, "sum_share_" + n) for t, n in enumerate(big)}
    small_sum = _sum_and_share(landed[nbig], gparts[nbig], 0, chip1, "sum_share_small")
    gpack = _gather_sums_over_chips(small_sum).transpose(1, 0, 2, 3).reshape(pack_rows, LANES)
    gs = dict(zip(SMALL + ['loss'], _unpack(gpack, small_shapes)))
    loss = gs.pop('loss').reshape(())
    for n in SMALL_SHARDED:
        width = w[n].shape[-1]
        gs[n] = lax.dynamic_slice_in_dim(gs[n], chip * width, width, axis=gs[n].ndim - 1)

    grads, delta, new_m, new_v = {}, {}, {}, {}
    for n in big:
        shp = w[n].shape
        flat = lambda a: a.reshape(shp[0] * shp[1], shp[2])
        g = gshard[n]
        grads[n] = g
        d_, m_, v_ = _adamw(flat(w[n]), flat(g), flat(m[n]), flat(v[n]), "adamw_" + n)
        delta[n], new_m[n], new_v[n] = d_.reshape(shp), m_.reshape(shp), v_.reshape(shp)
    for n in SMALL:
        shp = w[n].shape
        as2d = (lambda a: a.reshape(1, -1)) if len(shp) == 1 else (lambda a: a)
        d_, m_, v_ = _adamw(as2d(w[n]), as2d(gs[n]), as2d(m[n]), as2d(v[n]), "adamw_" + n)
        grads[n], delta[n], new_m[n], new_v[n] = gs[n], d_.reshape(shp), m_.reshape(shp), v_.reshape(shp)

    return (loss, dx.reshape(1, L, D), *[grads[n] for n in WEIGHTS], *[delta[n] for n in WEIGHTS],
            *[new_m[n] for n in WEIGHTS], *[new_v[n] for n in WEIGHTS])
```

```python
import functools
import math

import numpy as np
import jax
import jax.numpy as jnp
from jax import lax
from jax.experimental import pallas as pl
from jax.experimental.pallas import tpu as pltpu

F32 = jnp.float32
BF16 = jnp.bfloat16
MESH = pl.DeviceIdType.MESH

EPS = 1e-6
CHUNK = 128
POOL_WINDOWS = (2, 4, 8, 16)
LANES = 128
SUBLANES = 8
SCAN_CHUNKS = SUBLANES
S5_GROUPS_PER_STEP = 4
MM_TM_CAP, MM_TN_CAP, MM_TK_CAP = 1408, 1408, 2048
MM_TK_WHOLE = 2048
VMEM_LIMIT = 48 * 1024 * 1024
VMEM_LIMIT_S5 = 56 * 1024 * 1024

ADAM_LR, ADAM_B1, ADAM_B2, ADAM_EPS, ADAM_WD, ADAM_STEP = 0.001, 0.9, 0.999, 1e-08, 0.01, 10

WEIGHTS = ['norm_mix_g', 'even_w_in', 'even_conv_w', 'ssm_log_step', 'ssm_a_re', 'ssm_a_im', 'ssm_b_re',
           'ssm_b_im', 'ssm_c_re', 'ssm_c_im', 'ssm_d', 'ssm_glu_w', 'ssm_glu_b', 'even_w_out', 'odd_w_in',
           'pool_w', 'pool_scale', 'sgu_norm_g', 'sgu_w', 'sgu_b', 'odd_w_out', 'norm_ffn_g', 'ffn_w_up',
           'ffn_conv_w', 'ffn_conv_b', 'ffn_w_down', 'norm_final_g']
BIG = {'even_w_in': 2, 'ssm_glu_w': 1, 'even_w_out': 1, 'odd_w_in': 2, 'odd_w_out': 1, 'ffn_w_up': 2,
       'ffn_w_down': 1}
SMALL_SHARDED = ('even_conv_w', 'pool_scale', 'sgu_norm_g', 'ffn_conv_w')
SMALL = [n for n in WEIGHTS if n not in BIG]
N_CHIPS = 4
N_DEV = 8


def _cparams(sem=None, vmem=VMEM_LIMIT):
    kw = dict(vmem_limit_bytes=vmem)
    if sem is not None:
        kw['dimension_semantics'] = sem
    return pltpu.CompilerParams(**kw)


def _pick(n, segs=(), prefs=(1024, 512, 256, 128)):
    for t in prefs:
        if n % t == 0 and all(s % t == 0 for s in segs if s):
            return t
    return n


def _largest_tile(n, segs, cap):
    best = None
    for t in range(LANES, min(n, cap) + 1, LANES):
        if n % t == 0 and all(s % t == 0 for s in segs if s):
            best = t
    return best if best is not None else n


def _ldims(arr, kind):
    if kind is None:
        return arr.shape
    if kind[0] == 'lead':
        return arr.shape[1:]
    return (arr.shape[1], arr.shape[0] * arr.shape[2])


def _segw(arr, kind):
    return arr.shape[2] if (kind is not None and kind[0] == 'seg') else None


def _opspec(arr, kind, br, bc, rfn, cfn):
    if kind is None:
        return pl.BlockSpec((br, bc), lambda i, j, k: (rfn(i, j, k), cfn(i, j, k)))
    if kind[0] == 'lead':
        lead = kind[1]
        return pl.BlockSpec((None, br, bc), lambda i, j, k: (lead, rfn(i, j, k), cfn(i, j, k)))
    per = arr.shape[2] // bc
    return pl.BlockSpec((None, br, bc), lambda i, j, k: (cfn(i, j, k) // per, rfn(i, j, k), cfn(i, j, k) % per))


def _mm(a, b, mode, out_dtype, name, ak=None, bk=None, ok=None, res=None, dep=None):
    ar, ac = _ldims(a, ak)
    br_, bc_ = _ldims(b, bk)
    if mode == 'nn':
        M, K, N = ar, ac, bc_
        assert br_ == K
    else:
        M, K, N = ar, ac, br_
        assert bc_ == K
    sa, sb = _segw(a, ak), _segw(b, bk)
    so = (N // ok[1]) if ok is not None else None
    tm = _largest_tile(M, [], MM_TM_CAP)
    tn = _largest_tile(N, [sb if mode == 'nn' else None, so], MM_TN_CAP)
    ksegs = [sa, sb if mode == 'nt' else None]
    tk = K if (K <= MM_TK_WHOLE and not any(ksegs)) else _largest_tile(K, ksegs, MM_TK_CAP)
    nk = K // tk
    I = lambda i, j, k: i
    J = lambda i, j, k: j
    Kk = lambda i, j, k: k
    a_spec = _opspec(a, ak, tm, tk, I, Kk)
    if mode == 'nn':
        b_spec = _opspec(b, bk, tk, tn, Kk, J)
        dims = (((1,), (0,)), ((), ()))
    else:
        b_spec = _opspec(b, bk, tn, tk, J, Kk)
        dims = (((1,), (1,)), ((), ()))
    if ok is None:
        out_shape = jax.ShapeDtypeStruct((M, N), out_dtype)
        o_spec = pl.BlockSpec((tm, tn), lambda i, j, k: (i, j))
    else:
        out_shape = jax.ShapeDtypeStruct((ok[1], M, N // ok[1]), out_dtype)
        per = (N // ok[1]) // tn
        o_spec = pl.BlockSpec((None, tm, tn), lambda i, j, k: (j // per, i, j % per))
    has_res = res is not None

    def body(*refs):
        a_ref, b_ref = refs[0], refs[1]
        r_ref = refs[2] if has_res else None
        o_ref = refs[n_in]
        prod = lax.dot_general(a_ref[...].astype(BF16), b_ref[...].astype(BF16), dims, preferred_element_type=F32)
        if nk == 1:
            o_ref[...] = (prod + r_ref[...] if has_res else prod).astype(out_dtype)
            return
        acc = refs[-1]
        k = pl.program_id(2)

        @pl.when(k == 0)
        def _():
            acc[...] = prod

        @pl.when(k > 0)
        def _():
            acc[...] += prod

        @pl.when(k == nk - 1)
        def _():
            o = acc[...]
            if has_res:
                o = o + r_ref[...]
            o_ref[...] = o.astype(out_dtype)

    in_specs = [a_spec, b_spec]
    args = [a, b]
    if has_res:
        in_specs.append(pl.BlockSpec((tm, tn), lambda i, j, k: (i, j)))
        args.append(res)
    if dep is not None:
        in_specs.append(pl.BlockSpec(memory_space=pl.ANY))
        args.append(dep)
    n_in = len(args)
    return pl.pallas_call(
        body, name=name, out_shape=out_shape, grid=(M // tm, N // tn, nk), in_specs=in_specs, out_specs=o_spec,
        scratch_shapes=[pltpu.VMEM((tm, tn), F32)] if nk > 1 else [],
        compiler_params=_cparams(("parallel", "parallel", "arbitrary")),
    )(*args)


_G0 = math.sqrt(2.0 / math.pi)
_G1 = 0.044715


def _gelu(x):
    return 0.5 * x * (1.0 + jnp.tanh(_G0 * (x + _G1 * x * x * x)))


def _gelu_grad(x):
    x2 = x * x
    t = jnp.tanh(_G0 * (x + _G1 * x * x2))
    return 0.5 * (1.0 + t) + 0.5 * x * (1.0 - t * t) * (_G0 * (1.0 + 3.0 * _G1 * x2))


def _sigmoid(x):
    return 1.0 / (1.0 + jnp.exp(-x))


def _down(v, k):
    r = pltpu.roll(v, k, axis=0)
    row = lax.broadcasted_iota(jnp.int32, (SUBLANES, v.shape[1]), 0)
    return jnp.concatenate([jnp.where(row >= k, r[:SUBLANES], 0.0), r[SUBLANES:]], axis=0)


def _up(v, k):
    n = v.shape[0]
    r = pltpu.roll(v, n - k, axis=0)
    row = lax.broadcasted_iota(jnp.int32, (SUBLANES, v.shape[1]), 0)
    return jnp.concatenate([r[:n - SUBLANES], jnp.where(row < SUBLANES - k, r[n - SUBLANES:], 0.0)], axis=0)


def _taps(v):
    return _down(v, 2), _down(v, 1), v


def _conv3(taps, w):
    return w[0:1, :] * taps[0] + w[1:2, :] * taps[1] + w[2:3, :] * taps[2]


def _conv3_t(dv, w):
    return w[2:3, :] * dv + w[1:2, :] * _up(dv, 1) + w[0:1, :] * _up(dv, 2)


def _conv3_dw(dv, taps):
    return tuple(jnp.sum(dv * tp, axis=0, keepdims=True) for tp in taps)


def _cmul(ar, ai, br, bi):
    return ar * br - ai * bi, ar * bi + ai * br


def _cpow(lr, li, n):
    rr = ri = None
    br, bi = lr, li
    while n:
        if n & 1:
            rr, ri = (br, bi) if rr is None else _cmul(rr, ri, br, bi)
        n >>= 1
        if n:
            br, bi = _cmul(br, bi, br, bi)
    return rr, ri


NORM_ROWS = 256


def _norm_mm(x, g, b, out_dtype, name, ok=None):
    M, D = x.shape
    N = b.shape[1]
    so = (N // ok[1]) if ok is not None else None
    tm = _largest_tile(M, [], 1024)
    tn = _largest_tile(N, [so], MM_TN_CAP)
    if ok is None:
        out_shape = jax.ShapeDtypeStruct((M, N), out_dtype)
        o_spec = pl.BlockSpec((tm, tn), lambda i, j: (i, j))
    else:
        out_shape = jax.ShapeDtypeStruct((ok[1], M, N // ok[1]), out_dtype)
        per = (N // ok[1]) // tn
        o_spec = pl.BlockSpec((None, tm, tn), lambda i, j: (j // per, i, j % per))

    def body(x_ref, g_ref, b_ref, o_ref, ht_ref, h_scr):
        @pl.when(pl.program_id(1) == 0)
        def _():
            for c in range(tm // NORM_ROWS):
                rows = pl.ds(c * NORM_ROWS, NORM_ROWS)
                xv = x_ref[rows, :]
                h = xv * lax.rsqrt(jnp.mean(xv * xv, axis=-1, keepdims=True) + EPS) * g_ref[...]
                h_scr[rows, :] = h.astype(BF16)
                ht_ref[:, rows] = h.T.astype(BF16)

        o_ref[...] = jnp.dot(h_scr[...], b_ref[...], preferred_element_type=F32).astype(out_dtype)

    return pl.pallas_call(
        body, name=name, out_shape=(out_shape, jax.ShapeDtypeStruct((D, M), BF16)), grid=(M // tm, N // tn),
        in_specs=[pl.BlockSpec((tm, D), lambda i, j: (i, 0)), pl.BlockSpec((1, D), lambda i, j: (0, 0)),
                  pl.BlockSpec((D, tn), lambda i, j: (0, j))],
        out_specs=(o_spec, pl.BlockSpec((D, tm), lambda i, j: (0, i))),
        scratch_shapes=[pltpu.VMEM((tm, D), BF16)], compiler_params=_cparams(("parallel", "arbitrary")),
    )(x, g.reshape(1, D), b)


def _mm_norm_bwd(a, b, x, g, dres, name, ak=None, dep=None):
    M, K = _ldims(a, ak)
    D = b.shape[0]
    assert b.shape[1] == K and x.shape == (M, D)
    sa = _segw(a, ak)
    tm = _largest_tile(M, [], 1024)
    tk = K if (K <= MM_TK_WHOLE and not sa) else _largest_tile(K, [sa], MM_TK_CAP)
    ni, nk = M // tm, K // tk
    a3 = _opspec(a, ak, tm, tk, lambda i, j, k: i, lambda i, j, k: k)
    a_spec = pl.BlockSpec(a3.block_shape, lambda i, k: a3.index_map(i, 0, k))
    n_in = 5 + (dep is not None)

    def body(*refs):
        a_ref, b_ref, x_ref, g_ref, r_ref = refs[:5]
        dx_ref, dxb_ref, dg_ref, acc, accg = refs[n_in:]
        i, k = pl.program_id(0), pl.program_id(1)
        prod = lax.dot_general(a_ref[...].astype(BF16), b_ref[...], (((1,), (1,)), ((), ())),
                               preferred_element_type=F32)

        @pl.when(k == 0)
        def _():
            acc[...] = prod

        @pl.when(k > 0)
        def _():
            acc[...] += prod

        @pl.when((i == 0) & (k == 0))
        def _():
            accg[...] = jnp.zeros_like(accg)

        @pl.when(k == nk - 1)
        def _():
            for c in range(tm // NORM_ROWS):
                rows = pl.ds(c * NORM_ROWS, NORM_ROWS)
                xv = x_ref[rows, :]
                r = lax.rsqrt(jnp.mean(xv * xv, axis=-1, keepdims=True) + EPS)
                xh = xv * r
                dhv = acc[rows, :]
                accg[...] += jnp.sum((dhv * xh).reshape(NORM_ROWS // SUBLANES, SUBLANES, D), axis=0)
                dxh = dhv * g_ref[...]
                dxv = r_ref[rows, :] + r * (dxh - xh * jnp.mean(dxh * xh, axis=-1, keepdims=True))
                dx_ref[rows, :] = dxv
                dxb_ref[rows, :] = dxv.astype(BF16)

        @pl.when((i == ni - 1) & (k == nk - 1))
        def _():
            dg_ref[...] = jnp.sum(accg[...], axis=0, keepdims=True)

    row = pl.BlockSpec((tm, D), lambda i, k: (i, 0))
    vec = pl.BlockSpec((1, D), lambda i, k: (0, 0))
    in_specs = [a_spec, pl.BlockSpec((D, tk), lambda i, k: (0, k)), row, vec, row]
    args = [a, b, x, g.reshape(1, D), dres]
    if dep is not None:
        in_specs.append(pl.BlockSpec(memory_space=pl.ANY))
        args.append(dep)
    return pl.pallas_call(
        body, name=name,
        out_shape=(jax.ShapeDtypeStruct((M, D), F32), jax.ShapeDtypeStruct((M, D), BF16),
                   jax.ShapeDtypeStruct((1, D), F32)),
        grid=(ni, nk), in_specs=in_specs, out_specs=(row, row, vec),
        scratch_shapes=[pltpu.VMEM((tm, D), F32), pltpu.VMEM((SUBLANES, D), F32)],
        compiler_params=_cparams(("arbitrary", "arbitrary"), VMEM_LIMIT_S5),
    )(*args)


def _loss_head(x, g, tgt):
    L, D = x.shape
    tr = _pick(L, prefs=(512, 256, 128))
    nsteps = L // tr

    def body(x_ref, g_ref, t_ref, loss_ref, dx_ref, dxb_ref, dg_ref, acc_g, acc_l):
        i = pl.program_id(0)

        @pl.when(i == 0)
        def _():
            acc_g[...] = jnp.zeros_like(acc_g)
            acc_l[...] = jnp.zeros_like(acc_l)

        xv = x_ref[...]
        gv = g_ref[...]
        r = lax.rsqrt(jnp.mean(xv * xv, axis=-1, keepdims=True) + EPS)
        xh = xv * r
        e = xh * gv - t_ref[...]
        acc_l[...] += jnp.sum((e * e).reshape(tr // SUBLANES, SUBLANES, D), axis=0)
        dy = e * (1.0 / D)
        acc_g[...] += jnp.sum((dy * xh).reshape(tr // SUBLANES, SUBLANES, D), axis=0)
        dxh = dy * gv
        dxv = r * (dxh - xh * jnp.mean(dxh * xh, axis=-1, keepdims=True))
        dx_ref[...] = dxv
        dxb_ref[...] = dxv.astype(BF16)

        @pl.when(i == nsteps - 1)
        def _():
            dg_ref[...] = jnp.sum(acc_g[...], axis=0, keepdims=True)
            tot = jnp.sum(jnp.sum(acc_l[...], axis=0, keepdims=True), axis=1, keepdims=True) * (0.5 / D)
            loss_ref[...] = jnp.broadcast_to(tot, (SUBLANES, LANES))

    row = pl.BlockSpec((tr, D), lambda i: (i, 0))
    vec = pl.BlockSpec((1, D), lambda i: (0, 0))
    return pl.pallas_call(
        body, name="loss_head",
        out_shape=(jax.ShapeDtypeStruct((SUBLANES, LANES), F32), jax.ShapeDtypeStruct((L, D), F32),
                   jax.ShapeDtypeStruct((L, D), BF16), jax.ShapeDtypeStruct((1, D), F32)),
        grid=(nsteps,), in_specs=[row, vec, row],
        out_specs=(pl.BlockSpec((SUBLANES, LANES), lambda i: (0, 0)), row, row, vec),
        scratch_shapes=[pltpu.VMEM((SUBLANES, D), F32), pltpu.VMEM((SUBLANES, D), F32)],
        compiler_params=_cparams(("arbitrary",)),
    )(x, g.reshape(1, D), tgt)


def _sconv_fwd(proj4, conv_w, name):
    _, L, C = proj4.shape
    cb = LANES

    def body(p_ref, w_ref, o_ref):
        xa, ba, ca = p_ref[0].astype(F32), p_ref[1].astype(F32), p_ref[2].astype(F32)
        o_ref[...] = (ba * _conv3(_taps(ca * xa), w_ref[...])).astype(BF16)

    return pl.pallas_call(
        body, name=name, out_shape=jax.ShapeDtypeStruct((L, 2 * C), BF16), grid=(C // cb,),
        in_specs=[pl.BlockSpec((3, L, cb), lambda j: (0, 0, j)), pl.BlockSpec((3, cb), lambda j: (0, j))],
        out_specs=pl.BlockSpec((L, cb), lambda j: (0, j)), compiler_params=_cparams(("parallel",)),
    )(proj4, conv_w)


def _sconv_bwd(proj4, dmix, conv_w, name):
    _, L, C = proj4.shape
    cb = LANES

    def body(p_ref, d_ref, w_ref, o_ref, dw_ref):
        xa, ba, ca = p_ref[0].astype(F32), p_ref[1].astype(F32), p_ref[2].astype(F32)
        w = w_ref[...]
        dya = d_ref[...]
        tq = _taps(ca * xa)
        cq = _conv3(tq, w)
        dcq = dya * ba
        dq = _conv3_t(dcq, w)
        for tap, dwt in enumerate(_conv3_dw(dcq, tq)):
            dw_ref[tap:tap + 1, :] = dwt
        o_ref[0] = (dq * ca).astype(BF16)
        o_ref[1] = (dya * cq).astype(BF16)
        o_ref[2] = (dq * xa).astype(BF16)

    return pl.pallas_call(
        body, name=name,
        out_shape=(jax.ShapeDtypeStruct((4, L, C), BF16), jax.ShapeDtypeStruct((3, C), F32)), grid=(C // cb,),
        in_specs=[pl.BlockSpec((3, L, cb), lambda j: (0, 0, j)), pl.BlockSpec((L, cb), lambda j: (0, j)),
                  pl.BlockSpec((3, cb), lambda j: (0, j))],
        out_specs=(pl.BlockSpec((3, L, cb), lambda j: (0, 0, j)), pl.BlockSpec((3, cb), lambda j: (0, j))),
        compiler_params=_cparams(("parallel",)),
    )(proj4, dmix, conv_w)


def _s5_prep(log_step, a_re, a_im, b_re, b_im, c_re, c_im):
    G, P = a_re.shape
    H = b_re.shape[-1]
    gs = S5_GROUPS_PER_STEP
    ns = G // gs
    gu = LANES // H
    lam = lax.complex(a_re, a_im)
    step = jnp.exp(log_step)[:, None]
    lam_bar = jnp.exp(lam * step)
    b_bar = ((lam_bar - 1.0) / lam)[..., None] * lax.complex(b_re, b_im)
    lr = jnp.real(lam_bar).reshape(ns, 1, gs * P)
    li = jnp.imag(lam_bar).reshape(ns, 1, gs * P)
    k = np.arange(ns)[:, None, None]
    oh = jnp.asarray((np.arange(gu)[None, :, None] == gs * (k % (gu // gs)) + np.arange(gs)[None, None, :]),
                     F32)
    bre = jnp.einsum('kgl,klph->kghlp', oh, jnp.real(b_bar).reshape(ns, gs, P, H)).reshape(ns, gu * H, gs * P)
    bim = jnp.einsum('kgl,klph->kghlp', oh, jnp.imag(b_bar).reshape(ns, gs, P, H)).reshape(ns, gu * H, gs * P)
    cre = jnp.einsum('kgl,klhp->klpgh', oh, c_re.reshape(ns, gs, H, P)).reshape(ns, gs * P, gu * H)
    cim = jnp.einsum('kgl,klhp->klpgh', oh, c_im.reshape(ns, gs, H, P)).reshape(ns, gs * P, gu * H)
    return lr, li, jnp.concatenate([bre, bim], axis=2), jnp.concatenate([cre, -cim], axis=1)


def _carry_tile(fr, fi, pr, pi, reverse):
    row = lax.broadcasted_iota(jnp.int32, fr.shape, 0)
    cr = jnp.zeros_like(fr)
    ci = jnp.zeros_like(fi)
    sr = jnp.zeros_like(fr[0:1])
    si = jnp.zeros_like(sr)
    order = range(SCAN_CHUNKS - 1, 0, -1) if reverse else range(0, SCAN_CHUNKS - 1)
    for c in order:
        fcr = jnp.sum(jnp.where(row == c, fr, 0.0), axis=0, keepdims=True)
        fci = jnp.sum(jnp.where(row == c, fi, 0.0), axis=0, keepdims=True)
        mr, mi = _cmul(pr, pi, sr, si)
        sr, si = mr + fcr, mi + fci
        nxt = c - 1 if reverse else c + 1
        cr = jnp.where(row == nxt, sr, cr)
        ci = jnp.where(row == nxt, si, ci)
    return cr, ci


def _scan_order_into(dst_ref, src_ref, T):
    for c in range(SCAN_CHUNKS):
        dst_ref[pl.ds(c, T, stride=SCAN_CHUNKS), :] = src_ref[pl.ds(c * T, T), :].astype(F32)


def _s5_fwd(proj4, lr, li, bmat, cmat, d, name):
    _, L, Du = proj4.shape
    ns, _, W2 = bmat.shape
    W = W2 // 2
    T = L // SCAN_CHUNKS
    rb = _pick(L, prefs=(512, 256, 128))
    per = (ns * LANES) // Du

    def body(ut_ref, lr_ref, li_ref, b_ref, c_ref, d_ref, y_ref, sr_ref, si_ref, u_ref):
        k = pl.program_id(0)
        _scan_order_into(u_ref, ut_ref, T)
        for r in range(L // rb):
            rows = pl.ds(r * rb, rb)
            bu = jnp.dot(u_ref[rows, :].astype(BF16), b_ref[...], preferred_element_type=F32)
            sr_ref[rows, :] = bu[:, :W]
            si_ref[rows, :] = bu[:, W:]
        lam_r = jnp.broadcast_to(lr_ref[...], (SUBLANES, W))
        lam_i = jnp.broadcast_to(li_ref[...], (SUBLANES, W))

        def local(t, carry):
            sr, si = carry
            rows = pl.ds(pl.multiple_of(t * SUBLANES, SUBLANES), SUBLANES)
            mr, mi = _cmul(lam_r, lam_i, sr, si)
            sr = mr + sr_ref[rows, :]
            si = mi + si_ref[rows, :]
            sr_ref[rows, :] = sr
            si_ref[rows, :] = si
            return sr, si

        z = jnp.zeros((SUBLANES, W), F32)
        fr, fi = lax.fori_loop(0, T, local, (z, z))
        pr, pi = _cpow(lam_r, lam_i, T)
        cr, ci = _carry_tile(fr, fi, pr[0:1], pi[0:1], reverse=False)

        def fix(t, carry):
            wr, wi = carry
            rows = pl.ds(pl.multiple_of(t * SUBLANES, SUBLANES), SUBLANES)
            ar, ai = _cmul(wr, wi, cr, ci)
            sr_ref[rows, :] += ar
            si_ref[rows, :] += ai
            return _cmul(wr, wi, lam_r, lam_i)

        lax.fori_loop(0, T, fix, (lam_r, lam_i))
        first = (k % per) == 0
        for r in range(L // rb):
            rows = pl.ds(r * rb, rb)
            s = jnp.concatenate([sr_ref[rows, :], si_ref[rows, :]], axis=1).astype(BF16)
            y = jnp.dot(s, c_ref[...], preferred_element_type=F32)

            @pl.when(first)
            def _():
                y_ref[rows, :] = y + d_ref[...] * u_ref[rows, :]

            @pl.when(jnp.logical_not(first))
            def _():
                y_ref[rows, :] += y

    ublk = pl.BlockSpec((L, LANES), lambda k: (0, k // per))
    sblk = pl.BlockSpec((L, W), lambda k: (0, k))
    lam = pl.BlockSpec((None, 1, W), lambda k: (k, 0, 0))
    return pl.pallas_call(
        body, name=name,
        out_shape=(jax.ShapeDtypeStruct((L, Du), F32), jax.ShapeDtypeStruct((L, ns * W), F32),
                   jax.ShapeDtypeStruct((L, ns * W), F32)),
        grid=(ns,),
        in_specs=[pl.BlockSpec((None, L, LANES), lambda k: (3, 0, k // per)), lam, lam,
                  pl.BlockSpec((None, LANES, 2 * W), lambda k: (k, 0, 0)),
                  pl.BlockSpec((None, 2 * W, LANES), lambda k: (k, 0, 0)),
                  pl.BlockSpec((1, LANES), lambda k: (0, k // per))],
        out_specs=(ublk, sblk, sblk), scratch_shapes=[pltpu.VMEM((L, LANES), F32)],
        compiler_params=_cparams(("arbitrary",), VMEM_LIMIT_S5),
    )(proj4, lr, li, bmat.astype(BF16), cmat.astype(BF16), d.reshape(1, Du))


def _s5_bwd(dy, proj4, dproj, s_re, s_im, lr, li, bmat, cmat, d, name):
    _, L, Du = proj4.shape
    ns, _, W2 = bmat.shape
    W = W2 // 2
    T = L // SCAN_CHUNKS
    rb = _pick(L, prefs=(512, 256, 128))
    per = (ns * LANES) // Du
    NT = (((1,), (1,)), ((), ()))
    TN = (((0,), (0,)), ((), ()))

    def body(dy_ref, ut_ref, dp_in, sr_ref, si_ref, lr_ref, li_ref, b_ref, c_ref, d_ref,
             dut_ref, db_ref, dc_ref, dl_ref, dd_ref, gr_ref, gi_ref, u_ref, du_ref):
        k = pl.program_id(0)
        _scan_order_into(u_ref, ut_ref, T)
        for r in range(L // rb):
            rows = pl.ds(r * rb, rb)
            g = lax.dot_general(dy_ref[rows, :].astype(BF16), c_ref[...], NT, preferred_element_type=F32)
            gr_ref[rows, :] = g[:, :W]
            gi_ref[rows, :] = g[:, W:]
        lam_r = jnp.broadcast_to(lr_ref[...], (SUBLANES, W))
        lam_i = -jnp.broadcast_to(li_ref[...], (SUBLANES, W))

        def local(i, carry):
            gr, gi = carry
            rows = pl.ds(pl.multiple_of((T - 1 - i) * SUBLANES, SUBLANES), SUBLANES)
            mr, mi = _cmul(lam_r, lam_i, gr, gi)
            gr = mr + gr_ref[rows, :]
            gi = mi + gi_ref[rows, :]
            gr_ref[rows, :] = gr
            gi_ref[rows, :] = gi
            return gr, gi

        z = jnp.zeros((SUBLANES, W), F32)
        fr, fi = lax.fori_loop(0, T, local, (z, z))
        pr, pi = _cpow(lam_r, lam_i, T)
        cr, ci = _carry_tile(fr, fi, pr[0:1], pi[0:1], reverse=True)

        def true_g(rows, wr, wi):
            ar, ai = _cmul(wr, wi, cr, ci)
            gr = gr_ref[rows, :] + ar
            gi = gi_ref[rows, :] + ai
            gr_ref[rows, :] = gr
            gi_ref[rows, :] = gi
            return gr, gi

        def fix(i, carry):
            wr, wi, ar_, ai_ = carry
            t = T - 1 - i
            rows = pl.ds(pl.multiple_of(t * SUBLANES, SUBLANES), SUBLANES)
            prev = pl.ds(pl.multiple_of((t - 1) * SUBLANES, SUBLANES), SUBLANES)
            gr, gi = true_g(rows, wr, wi)
            qr, qi = sr_ref[prev, :], si_ref[prev, :]
            ar_ = ar_ + gr * qr + gi * qi
            ai_ = ai_ + gi * qr - gr * qi
            wr, wi = _cmul(wr, wi, lam_r, lam_i)
            return wr, wi, ar_, ai_

        wr, wi, acc_r, acc_i = lax.fori_loop(0, T - 1, fix, (lam_r, lam_i, z, z))
        gr, gi = true_g(pl.ds(0, SUBLANES), wr, wi)
        last = pl.ds((T - 1) * SUBLANES, SUBLANES)
        row = lax.broadcasted_iota(jnp.int32, (SUBLANES, W), 0)
        qr = jnp.where(row >= 1, pltpu.roll(sr_ref[last, :], 1, axis=0), 0.0)
        qi = jnp.where(row >= 1, pltpu.roll(si_ref[last, :], 1, axis=0), 0.0)
        acc_r = acc_r + gr * qr + gi * qi
        acc_i = acc_i + gi * qr - gr * qi
        dl_ref[0:1, :] = jnp.sum(acc_r, axis=0, keepdims=True)
        dl_ref[1:2, :] = jnp.sum(acc_i, axis=0, keepdims=True)

        first = (k % per) == 0
        db = jnp.zeros((LANES, 2 * W), F32)
        dc = jnp.zeros((LANES, 2 * W), F32)
        dd = jnp.zeros((1, LANES), F32)
        for r in range(L // rb):
            rows = pl.ds(r * rb, rb)
            gb = jnp.concatenate([gr_ref[rows, :], gi_ref[rows, :]], axis=1).astype(BF16)
            sb = jnp.concatenate([sr_ref[rows, :], si_ref[rows, :]], axis=1).astype(BF16)
            dyv = dy_ref[rows, :]
            uv = u_ref[rows, :]
            du = lax.dot_general(gb, b_ref[...], NT, preferred_element_type=F32)
            db = db + lax.dot_general(uv.astype(BF16), gb, TN, preferred_element_type=F32)
            dc = dc + lax.dot_general(dyv.astype(BF16), sb, TN, preferred_element_type=F32)
            dd = dd + jnp.sum(dyv * uv, axis=0, keepdims=True)

            @pl.when(first)
            def _():
                du_ref[rows, :] = du + d_ref[...] * dyv

            @pl.when(jnp.logical_not(first))
            def _():
                du_ref[rows, :] += du

        db_ref[...] = db
        dc_ref[...] = dc

        @pl.when(first)
        def _():
            dd_ref[...] = dd

        @pl.when((k % per) == per - 1)
        def _():
            for c in range(SCAN_CHUNKS):
                dut_ref[pl.ds(c * T, T), :] = du_ref[pl.ds(c, T, stride=SCAN_CHUNKS), :].astype(BF16)

    ublk = pl.BlockSpec((L, LANES), lambda k: (0, k // per))
    uslab = pl.BlockSpec((None, L, LANES), lambda k: (3, 0, k // per))
    sblk = pl.BlockSpec((L, W), lambda k: (0, k))
    lam = pl.BlockSpec((None, 1, W), lambda k: (k, 0, 0))
    vec = pl.BlockSpec((1, LANES), lambda k: (0, k // per))
    mat = pl.BlockSpec((None, LANES, 2 * W), lambda k: (k, 0, 0))
    return pl.pallas_call(
        body, name=name,
        out_shape=(jax.ShapeDtypeStruct(dproj.shape, dproj.dtype), jax.ShapeDtypeStruct((ns, LANES, 2 * W), F32),
                   jax.ShapeDtypeStruct((ns, LANES, 2 * W), F32), jax.ShapeDtypeStruct((ns, 2, W), F32),
                   jax.ShapeDtypeStruct((1, Du), F32)),
        grid=(ns,),
        in_specs=[ublk, uslab, pl.BlockSpec(memory_space=pl.ANY), sblk, sblk, lam, lam, mat,
                  pl.BlockSpec((None, 2 * W, LANES), lambda k: (k, 0, 0)), vec],
        out_specs=(uslab, mat, mat, pl.BlockSpec((None, 2, W), lambda k: (k, 0, 0)), vec),
        scratch_shapes=[pltpu.VMEM((L, W), F32), pltpu.VMEM((L, W), F32), pltpu.VMEM((L, LANES), F32),
                        pltpu.VMEM((L, LANES), F32)],
        input_output_aliases={2: 0}, compiler_params=_cparams(("arbitrary",), VMEM_LIMIT_S5),
    )(dy, proj4, dproj, s_re, s_im, lr, li, bmat.astype(BF16), cmat.astype(BF16), d.reshape(1, Du))


def _glu_fwd(yraw, wmat, bias, mixin, name):
    L, C = yraw.shape
    tr = _pick(L, prefs=(512, 256, 128))
    tb = tr // SCAN_CHUNKS
    nl = C // LANES

    def body(y_ref, w_ref, b_ref, m_in, o_ref, scr):
        yg = _gelu(y_ref[...])
        zz = jnp.dot(yg.astype(BF16), w_ref[...], preferred_element_type=F32) + b_ref[...]
        yb = yg * _sigmoid(zz)
        for k in range(nl):
            scr[k] = yb[:, k * LANES:(k + 1) * LANES]
        for c in range(SCAN_CHUNKS):
            for k in range(nl):
                o_ref[c, :, k * LANES:(k + 1) * LANES] = scr[k, pl.ds(c, tb, stride=SCAN_CHUNKS), :].astype(BF16)

    out = pl.pallas_call(
        body, name=name, out_shape=jax.ShapeDtypeStruct((SCAN_CHUNKS, L // SCAN_CHUNKS, 2 * C), BF16),
        grid=(L // tr,),
        in_specs=[pl.BlockSpec((tr, C), lambda i: (i, 0)), pl.BlockSpec((C, C), lambda i: (0, 0)),
                  pl.BlockSpec((1, C), lambda i: (0, 0)), pl.BlockSpec(memory_space=pl.ANY)],
        out_specs=pl.BlockSpec((SCAN_CHUNKS, tb, C), lambda i: (0, i, 1)),
        scratch_shapes=[pltpu.VMEM((nl, tr, LANES), F32)], input_output_aliases={3: 0},
        compiler_params=_cparams(("parallel",)),
    )(yraw, wmat, bias.reshape(1, C), mixin.reshape(SCAN_CHUNKS, L // SCAN_CHUNKS, 2 * C))
    return out.reshape(L, 2 * C)


def _glu_bwd(yraw, dmix, wmat, bias, name):
    L, C = yraw.shape
    tr = _pick(L, prefs=(512, 256, 128))
    nsteps = L // tr
    tb = tr // SCAN_CHUNKS
    nl = C // LANES

    def body(y_ref, d_ref, w_ref, b_ref, dy_ref, dw_ref, db_ref, acc_b, scr):
        i = pl.program_id(0)

        @pl.when(i == 0)
        def _():
            dw_ref[...] = jnp.zeros_like(dw_ref)
            acc_b[...] = jnp.zeros_like(acc_b)

        for c in range(SCAN_CHUNKS):
            for k in range(nl):
                scr[k, pl.ds(c, tb, stride=SCAN_CHUNKS), :] = d_ref[c, :, k * LANES:(k + 1) * LANES]
        yr = y_ref[...]
        yg = _gelu(yr)
        ygb = yg.astype(BF16)
        sg = _sigmoid(jnp.dot(ygb, w_ref[...], preferred_element_type=F32) + b_ref[...])
        dyb_ = jnp.concatenate([scr[k] for k in range(nl)], axis=1)
        dz = dyb_ * yg * sg * (1.0 - sg)
        dzb = dz.astype(BF16)
        dyg = dyb_ * sg + lax.dot_general(dzb, w_ref[...], (((1,), (1,)), ((), ())), preferred_element_type=F32)
        dw_ref[...] += lax.dot_general(ygb, dzb, (((0,), (0,)), ((), ())), preferred_element_type=F32)
        acc_b[...] += jnp.sum(dz.reshape(tr // SUBLANES, SUBLANES, C), axis=0)
        dy_ref[...] = dyg * _gelu_grad(yr)

        @pl.when(i == nsteps - 1)
        def _():
            db_ref[...] = jnp.sum(acc_b[...], axis=0, keepdims=True)

    row = pl.BlockSpec((tr, C), lambda i: (i, 0))
    return pl.pallas_call(
        body, name=name,
        out_shape=(jax.ShapeDtypeStruct((L, C), F32), jax.ShapeDtypeStruct((C, C), F32),
                   jax.ShapeDtypeStruct((1, C), F32)),
        grid=(nsteps,),
        in_specs=[row, pl.BlockSpec((SCAN_CHUNKS, tb, C), lambda i: (0, i, 1)), pl.BlockSpec((C, C), lambda i: (0, 0)),
                  pl.BlockSpec((1, C), lambda i: (0, 0))],
        out_specs=(row, pl.BlockSpec((C, C), lambda i: (0, 0)), pl.BlockSpec((1, C), lambda i: (0, 0))),
        scratch_shapes=[pltpu.VMEM((SUBLANES, C), F32), pltpu.VMEM((nl, tr, LANES), F32)],
        compiler_params=_cparams(("arbitrary",)),
    )(yraw, dmix.reshape(SCAN_CHUNKS, L // SCAN_CHUNKS, 2 * C), wmat, bias.reshape(1, C))


def _pool_counts(L, g):
    t = lax.broadcasted_iota(jnp.int32, (L, LANES), 0).astype(F32) + 1.0
    w = jnp.where(g == 0, 2.0, jnp.where(g == 1, 4.0, jnp.where(g == 2, 8.0, 16.0)))
    return 1.0 / jnp.minimum(t, w)


def _select_window(g, a2, a4, a8, a16):
    return jnp.where(g == 0, a2, jnp.where(g == 1, a4, jnp.where(g == 2, a8, a16)))


def _pooled(z, g):
    a2 = z + _down(z, 1)
    a4 = a2 + _down(a2, 2)
    a8 = a4 + _down(a4, 4)
    a16 = a8 + _down(a8, 8)
    return _select_window(g, a2, a4, a8, a16) * _pool_counts(z.shape[0], g) - z


def _transpose_on_mxu(yb):
    c = yb.shape[1]
    eye = lax.broadcasted_iota(jnp.int32, (c, c), 0) == lax.broadcasted_iota(jnp.int32, (c, c), 1)
    return lax.dot_general(eye.astype(BF16), yb, (((1,), (1,)), ((), ())), preferred_element_type=F32).astype(BF16)


def _pool_fwd(proj3, pool_w, scale, name):
    _, L, C = proj3.shape
    ng = len(POOL_WINDOWS)
    pg = C // ng
    assert pg == LANES

    def body(z_ref, w_ref, s_ref, o_ref, ot_ref):
        g = pl.program_id(0)
        p = _pooled(z_ref[...].astype(F32), g)
        y = jnp.dot(p.astype(BF16), w_ref[...].astype(BF16), preferred_element_type=F32)
        yb = (y * s_ref[...]).astype(BF16)
        o_ref[...] = yb
        ot_ref[...] = _transpose_on_mxu(yb)

    return pl.pallas_call(
        body, name=name, out_shape=(jax.ShapeDtypeStruct((L, 2 * C), BF16), jax.ShapeDtypeStruct((2 * C, L), BF16)),
        grid=(ng,),
        in_specs=[pl.BlockSpec((None, L, pg), lambda g: (0, 0, g)), pl.BlockSpec((None, pg, pg), lambda g: (g, 0, 0)),
                  pl.BlockSpec((1, pg), lambda g: (0, g))],
        out_specs=(pl.BlockSpec((L, pg), lambda g: (0, g)), pl.BlockSpec((pg, L), lambda g: (g, 0))),
        compiler_params=_cparams(("parallel",)),
    )(proj3, pool_w, scale.reshape(1, C))


def _pool_bwd(proj3, dmix, pool_w, scale, name):
    _, L, C = proj3.shape
    ng = len(POOL_WINDOWS)
    pg = C // ng

    def body(z_ref, d_ref, w_ref, s_ref, dz_ref, dw_ref, ds_ref):
        g = pl.program_id(0)
        p = _pooled(z_ref[...].astype(F32), g)
        pb = p.astype(BF16)
        wb = w_ref[...].astype(BF16)
        pre = jnp.dot(pb, wb, preferred_element_type=F32)
        dyc = d_ref[...]
        ds_ref[...] = jnp.sum(dyc * pre, axis=0, keepdims=True)
        dpre = (dyc * s_ref[...]).astype(BF16)
        dw_ref[...] = lax.dot_general(pb, dpre, (((0,), (0,)), ((), ())), preferred_element_type=F32)
        dp = lax.dot_general(dpre, wb, (((1,), (1,)), ((), ())), preferred_element_type=F32)
        v = dp * _pool_counts(L, g)
        a2 = v + _up(v, 1)
        a4 = a2 + _up(a2, 2)
        a8 = a4 + _up(a4, 4)
        a16 = a8 + _up(a8, 8)
        dz_ref[...] = (_select_window(g, a2, a4, a8, a16) - dp).astype(BF16)

    return pl.pallas_call(
        body, name=name,
        out_shape=(jax.ShapeDtypeStruct((L, C), BF16), jax.ShapeDtypeStruct((ng, pg, pg), F32),
                   jax.ShapeDtypeStruct((1, C), F32)),
        grid=(ng,),
        in_specs=[pl.BlockSpec((None, L, pg), lambda g: (0, 0, g)), pl.BlockSpec((L, pg), lambda g: (0, g)),
                  pl.BlockSpec((None, pg, pg), lambda g: (g, 0, 0)), pl.BlockSpec((1, pg), lambda g: (0, g))],
        out_specs=(pl.BlockSpec((L, pg), lambda g: (0, g)), pl.BlockSpec((None, pg, pg), lambda g: (g, 0, 0)),
                   pl.BlockSpec((1, pg), lambda g: (0, g))),
        compiler_params=_cparams(("parallel",)),
    )(proj3, dmix, pool_w, scale.reshape(1, C))


def _tril_w(w_ref, h):
    r = lax.broadcasted_iota(jnp.int32, (CHUNK, CHUNK), 0)
    c = lax.broadcasted_iota(jnp.int32, (CHUNK, CHUNK), 1)
    return jnp.where(r >= c, w_ref[h], 0.0)


def _sgu_fwd(proj3, norm_g, w, b, mixin, mixin_t, name):
    _, L, C = proj3.shape
    nh = w.shape[0]
    dh = C // nh
    assert dh == LANES and w.shape[1] == CHUNK
    tr = _pick(L, prefs=(512, 256, 128))
    bfull = jnp.broadcast_to(b[:, :, None], (nh, CHUNK, dh))

    def body(su_ref, sv_ref, g_ref, w_ref, b_ref, m_in, mt_in, o_ref, ot_ref):
        sv = _gelu(sv_ref[...].astype(F32))
        r = lax.rsqrt(jnp.mean(sv * sv, axis=-1, keepdims=True) + EPS)
        v = (sv * r * g_ref[...]).astype(BF16)
        for h in range(nh):
            wm = _tril_w(w_ref, h).astype(BF16)
            cols = slice(h * dh, (h + 1) * dh)
            for n in range(tr // CHUNK):
                rows = slice(n * CHUNK, (n + 1) * CHUNK)
                mixed = jnp.dot(wm, v[rows, cols], preferred_element_type=F32) + b_ref[h]
                o_ref[rows, cols] = (_gelu(su_ref[rows, cols].astype(F32)) * mixed).astype(BF16)
        ot_ref[...] = _transpose_on_mxu(o_ref[...])

    full = lambda shp: pl.BlockSpec(shp, lambda i: (0,) * len(shp))
    anywhere = pl.BlockSpec(memory_space=pl.ANY)
    return pl.pallas_call(
        body, name=name, out_shape=(jax.ShapeDtypeStruct(mixin.shape, BF16), jax.ShapeDtypeStruct(mixin_t.shape, BF16)),
        grid=(L // tr,),
        in_specs=[pl.BlockSpec((None, tr, C), lambda i: (1, i, 0)), pl.BlockSpec((None, tr, C), lambda i: (2, i, 0)),
                  full((1, C)), full((nh, CHUNK, CHUNK)), full((nh, CHUNK, dh)), anywhere, anywhere],
        out_specs=(pl.BlockSpec((tr, C), lambda i: (i, 1)), pl.BlockSpec((C, tr), lambda i: (1, i))),
        input_output_aliases={5: 0, 6: 1}, compiler_params=_cparams(("parallel",)),
    )(proj3, proj3, norm_g.reshape(1, C), w, bfull, mixin, mixin_t)


def _sgu_bwd(proj3, dmix, dz, norm_g, w, b, name):
    _, L, C = proj3.shape
    nh = w.shape[0]
    dh = C // nh
    tr = _pick(L, prefs=(512, 256, 128))
    nsteps = L // tr
    bfull = jnp.broadcast_to(b[:, :, None], (nh, CHUNK, dh))

    def body(su_ref, sv_ref, d_ref, dz_ref, g_ref, w_ref, b_ref, o_ref, dw_ref, db_ref, dg_ref, dv_ref, acc_g):
        i = pl.program_id(0)
        o_ref[0] = dz_ref[...]

        @pl.when(i == 0)
        def _():
            dw_ref[...] = jnp.zeros_like(dw_ref)
            db_ref[...] = jnp.zeros_like(db_ref)
            acc_g[...] = jnp.zeros_like(acc_g)

        svp = sv_ref[...].astype(F32)
        sv = _gelu(svp)
        r = lax.rsqrt(jnp.mean(sv * sv, axis=-1, keepdims=True) + EPS)
        vh = sv * r
        gv = g_ref[...]
        v = (vh * gv).astype(BF16)
        tri_r = lax.broadcasted_iota(jnp.int32, (CHUNK, CHUNK), 0)
        tri_c = lax.broadcasted_iota(jnp.int32, (CHUNK, CHUNK), 1)
        for h in range(nh):
            wm = _tril_w(w_ref, h).astype(BF16)
            cols = slice(h * dh, (h + 1) * dh)
            dwh = jnp.zeros((CHUNK, CHUNK), F32)
            dbh = jnp.zeros((CHUNK, dh), F32)
            for n in range(tr // CHUNK):
                rows = slice(n * CHUNK, (n + 1) * CHUNK)
                vb = v[rows, cols]
                mixed = jnp.dot(wm, vb, preferred_element_type=F32) + b_ref[h]
                sup = su_ref[rows, cols].astype(F32)
                dyd = d_ref[rows, cols]
                dmx = dyd * _gelu(sup)
                o_ref[1, rows, cols] = (dyd * mixed * _gelu_grad(sup)).astype(BF16)
                dmb = dmx.astype(BF16)
                dwh = dwh + lax.dot_general(dmb, vb, (((1,), (1,)), ((), ())), preferred_element_type=F32)
                dbh = dbh + dmx
                dv_ref[rows, cols] = lax.dot_general(wm, dmb, (((0,), (0,)), ((), ())), preferred_element_type=F32)
            dw_ref[h] += jnp.where(tri_r >= tri_c, dwh, 0.0)
            db_ref[h] += dbh
        dv = dv_ref[...]
        acc_g[...] += jnp.sum((dv * vh).reshape(tr // SUBLANES, SUBLANES, C), axis=0)
        dvg = dv * gv
        dsv = r * (dvg - vh * jnp.mean(dvg * vh, axis=-1, keepdims=True))
        o_ref[2] = (dsv * _gelu_grad(svp)).astype(BF16)

        @pl.when(i == nsteps - 1)
        def _():
            dg_ref[...] = jnp.sum(acc_g[...], axis=0, keepdims=True)

    full = lambda shp: pl.BlockSpec(shp, lambda i: (0,) * len(shp))
    return pl.pallas_call(
        body, name=name,
        out_shape=(jax.ShapeDtypeStruct((3, L, C), BF16), jax.ShapeDtypeStruct((nh, CHUNK, CHUNK), F32),
                   jax.ShapeDtypeStruct((nh, CHUNK, dh), F32), jax.ShapeDtypeStruct((1, C), F32)),
        grid=(nsteps,),
        in_specs=[pl.BlockSpec((None, tr, C), lambda i: (1, i, 0)), pl.BlockSpec((None, tr, C), lambda i: (2, i, 0)),
                  pl.BlockSpec((tr, C), lambda i: (i, 1)), pl.BlockSpec((tr, C), lambda i: (i, 0)), full((1, C)),
                  full((nh, CHUNK, CHUNK)), full((nh, CHUNK, dh))],
        out_specs=(pl.BlockSpec((3, tr, C), lambda i: (0, i, 0)), full((nh, CHUNK, CHUNK)), full((nh, CHUNK, dh)),
                   full((1, C))),
        scratch_shapes=[pltpu.VMEM((tr, C), F32), pltpu.VMEM((SUBLANES, C), F32)],
        compiler_params=_cparams(("arbitrary",)),
    )(proj3, proj3, dmix, dz, norm_g.reshape(1, C), w, bfull)


def _ffn_act_fwd(up3, conv_w, conv_b, name):
    _, L, Fh = up3.shape
    cb = LANES
    w2 = conv_w.reshape(3, 2, Fh).transpose(1, 0, 2)
    b2 = conv_b.reshape(2, 1, Fh)

    def body(u_ref, w_ref, b_ref, o_ref, ot_ref, gv_ref):
        g = _conv3(_taps(u_ref[0].astype(F32)), w_ref[0]) + b_ref[0]
        v = _conv3(_taps(u_ref[1].astype(F32)), w_ref[1]) + b_ref[1]
        gv_ref[0] = g.astype(BF16)
        gv_ref[1] = v.astype(BF16)
        ab = (g * _sigmoid(g) * v).astype(BF16)
        o_ref[...] = ab
        ot_ref[...] = _transpose_on_mxu(ab)

    blk3 = pl.BlockSpec((2, L, cb), lambda j: (0, 0, j))
    return pl.pallas_call(
        body, name=name,
        out_shape=(jax.ShapeDtypeStruct((L, Fh), BF16), jax.ShapeDtypeStruct((Fh, L), BF16),
                   jax.ShapeDtypeStruct((2, L, Fh), BF16)),
        grid=(Fh // cb,),
        in_specs=[blk3, pl.BlockSpec((2, 3, cb), lambda j: (0, 0, j)), pl.BlockSpec((2, 1, cb), lambda j: (0, 0, j))],
        out_specs=(pl.BlockSpec((L, cb), lambda j: (0, j)), pl.BlockSpec((cb, L), lambda j: (j, 0)), blk3),
        compiler_params=_cparams(("parallel",)),
    )(up3, w2, b2)


def _ffn_act_bwd(up3, gv3, da, conv_w, name):
    _, L, Fh = up3.shape
    cb = LANES
    w2 = conv_w.reshape(3, 2, Fh).transpose(1, 0, 2)

    def body(u_ref, gv_ref, d_ref, w_ref, o_ref, dw_ref, db_ref):
        tg, tv = _taps(u_ref[0].astype(F32)), _taps(u_ref[1].astype(F32))
        wg, wv = w_ref[0], w_ref[1]
        g = gv_ref[0].astype(F32)
        v = gv_ref[1].astype(F32)
        sg = _sigmoid(g)
        dav = d_ref[...].astype(F32)
        dg = dav * v * (sg * (1.0 + g * (1.0 - sg)))
        dv = dav * (g * sg)
        o_ref[0] = _conv3_t(dg, wg).astype(BF16)
        o_ref[1] = _conv3_t(dv, wv).astype(BF16)
        for tap, (dwg, dwv) in enumerate(zip(_conv3_dw(dg, tg), _conv3_dw(dv, tv))):
            dw_ref[0, tap:tap + 1, :] = dwg
            dw_ref[1, tap:tap + 1, :] = dwv
        db_ref[0] = jnp.sum(dg, axis=0, keepdims=True)
        db_ref[1] = jnp.sum(dv, axis=0, keepdims=True)

    dup, dw2, db2 = pl.pallas_call(
        body, name=name,
        out_shape=(jax.ShapeDtypeStruct((2, L, Fh), BF16), jax.ShapeDtypeStruct((2, 3, Fh), F32),
                   jax.ShapeDtypeStruct((2, 1, Fh), F32)),
        grid=(Fh // cb,),
        in_specs=[pl.BlockSpec((2, L, cb), lambda j: (0, 0, j)), pl.BlockSpec((2, L, cb), lambda j: (0, 0, j)),
                  pl.BlockSpec((L, cb), lambda j: (0, j)), pl.BlockSpec((2, 3, cb), lambda j: (0, 0, j))],
        out_specs=(pl.BlockSpec((2, L, cb), lambda j: (0, 0, j)), pl.BlockSpec((2, 3, cb), lambda j: (0, 0, j)),
                   pl.BlockSpec((2, 1, cb), lambda j: (0, 0, j))),
        compiler_params=_cparams(("parallel",)),
    )(up3, gv3, da, w2)
    return dup, dw2.transpose(1, 0, 2).reshape(3, 2 * Fh), db2.reshape(2 * Fh)


def _local_step(x, tgt, w, layer_weights, on_layer_grads):
    L, D = x.shape
    depth = w['norm_mix_g'].shape[0]
    saved = []
    for i in range(depth):
        j = i // 2
        wb = dict(layer_weights(2 * i, x))
        s = {'x': x, 'wb': wb}
        if i % 2 == 0:
            proj4, s['hT'] = _norm_mm(x, w['norm_mix_g'][i], wb['even_w_in'], BF16, "even_in_fwd", ok=('seg', 4))
            s['proj'] = proj4
            mixin = _sconv_fwd(proj4, w['even_conv_w'][j], "sconv_fwd")
            prm = (w['ssm_log_step'][j], w['ssm_a_re'][j], w['ssm_a_im'][j], w['ssm_b_re'][j], w['ssm_b_im'][j],
                   w['ssm_c_re'][j], w['ssm_c_im'][j])
            (lr, li, bmat, cmat), prep_vjp = jax.vjp(_s5_prep, *prm)
            yraw, s_re, s_im = _s5_fwd(proj4, lr, li, bmat, cmat, w['ssm_d'][j], "s5_fwd")
            mixin = _glu_fwd(yraw, wb['ssm_glu_w'], w['ssm_glu_b'][j], mixin, "glu_fwd")
            s.update(yraw=yraw, s_re=s_re, s_im=s_im, s5=(lr, li, bmat, cmat), prep_vjp=prep_vjp)
            s['mixinT'] = mixin.T
            x = _mm(mixin, wb['even_w_out'], 'nn', F32, "even_out_fwd", res=x)
        else:
            proj3, s['hT'] = _norm_mm(x, w['norm_mix_g'][i], wb['odd_w_in'], BF16, "odd_in_fwd", ok=('seg', 3))
            s['proj'] = proj3
            mixin, mixin_t = _pool_fwd(proj3, w['pool_w'][j], w['pool_scale'][j], "pool_fwd")
            mixin, s['mixinT'] = _sgu_fwd(proj3, w['sgu_norm_g'][j], w['sgu_w'][j], w['sgu_b'][j], mixin, mixin_t,
                                          "sgu_fwd")
            x = _mm(mixin, wb['odd_w_out'], 'nn', F32, "odd_out_fwd", res=x)
        s['x1'] = x
        wb.update(layer_weights(2 * i + 1, x))
        up3, h2t = _norm_mm(x, w['norm_ffn_g'][i], wb['ffn_w_up'], BF16, "ffn_up_fwd", ok=('seg', 2))
        a, at, gv3 = _ffn_act_fwd(up3, w['ffn_conv_w'][i], w['ffn_conv_b'][i], "ffn_act_fwd")
        x = _mm(a, wb['ffn_w_down'], 'nn', F32, "ffn_down_fwd", res=x)
        s.update(h2T=h2t, up3=up3, aT=at, gv3=gv3)
        saved.append(s)

    loss8, dx, dxb, dg_final = _loss_head(x, w['norm_final_g'], tgt)
    gs = {n: [None] * w[n].shape[0] for n in SMALL if n != 'norm_final_g'}
    gs['norm_final_g'] = dg_final.reshape(D)

    dep = None
    for i in reversed(range(depth)):
        j = i // 2
        s = saved[i]
        wb = s['wb']
        gb = {}
        da = _mm(dxb, wb['ffn_w_down'], 'nt', BF16, "ffn_down_dgrad", dep=dep)
        gb['ffn_w_down'] = _mm(s['aT'], dxb, 'nn', BF16, "ffn_down_wgrad")
        dup3, dcw, dcb = _ffn_act_bwd(s['up3'], s['gv3'], da, w['ffn_conv_w'][i], "ffn_act_bwd")
        gs['ffn_conv_w'][i], gs['ffn_conv_b'][i] = dcw, dcb
        gb['ffn_w_up'] = _mm(s['h2T'], dup3, 'nn', BF16, "ffn_up_wgrad", bk=('seg', 2))
        dep = on_layer_grads(2 * i + 1, gb)
        dx, dxb, dg = _mm_norm_bwd(dup3, wb['ffn_w_up'], s['x1'], w['norm_ffn_g'][i], dx, "ffn_up_dgrad",
                              ak=('seg', 2), dep=dep)
        gs['norm_ffn_g'][i] = dg.reshape(D)
        gb = {}
        if i % 2 == 0:
            dmix = _mm(dxb, wb['even_w_out'], 'nt', F32, "even_out_dgrad")
            gb['even_w_out'] = _mm(s['mixinT'], dxb, 'nn', BF16, "even_out_wgrad")
            dproj, dcw = _sconv_bwd(s['proj'], dmix, w['even_conv_w'][j], "sconv_bwd")
            gs['even_conv_w'][j] = dcw
            dyraw, dglu_w, dglu_b = _glu_bwd(s['yraw'], dmix, wb['ssm_glu_w'], w['ssm_glu_b'][j], "glu_bwd")
            gb['ssm_glu_w'] = dglu_w.astype(BF16)
            gs['ssm_glu_b'][j] = dglu_b.reshape(-1)
            lr, li, bmat, cmat = s['s5']
            dproj, dbm, dcm, dlam, dd = _s5_bwd(dyraw, s['proj'], dproj, s['s_re'], s['s_im'], lr, li, bmat, cmat,
                                               w['ssm_d'][j], "s5_bwd")
            gs['ssm_d'][j] = dd.reshape(-1)
            dcm = jnp.swapaxes(dcm, 1, 2)
            dprm = s['prep_vjp']((dlam[:, 0:1, :], dlam[:, 1:2, :], dbm, dcm))
            for n, gval in zip(('ssm_log_step', 'ssm_a_re', 'ssm_a_im', 'ssm_b_re', 'ssm_b_im', 'ssm_c_re',
                                'ssm_c_im'), dprm):
                gs[n][j] = gval
            gb['even_w_in'] = _mm(s['hT'], dproj, 'nn', BF16, "even_in_wgrad", bk=('seg', 4))
            w_in, in_kind, in_name = wb['even_w_in'], ('seg', 4), "even_in_dgrad"
        else:
            dmix = _mm(dxb, wb['odd_w_out'], 'nt', F32, "odd_out_dgrad")
            gb['odd_w_out'] = _mm(s['mixinT'], dxb, 'nn', BF16, "odd_out_wgrad")
            dz, dpw, dps = _pool_bwd(s['proj'], dmix, w['pool_w'][j], w['pool_scale'][j], "pool_bwd")
            gs['pool_w'][j], gs['pool_scale'][j] = dpw, dps.reshape(-1)
            dproj, dsw, dsb, dsg = _sgu_bwd(s['proj'], dmix, dz, w['sgu_norm_g'][j], w['sgu_w'][j], w['sgu_b'][j],
                                            "sgu_bwd")
            gs['sgu_w'][j], gs['sgu_b'][j], gs['sgu_norm_g'][j] = dsw, jnp.sum(dsb, axis=-1), dsg.reshape(-1)
            gb['odd_w_in'] = _mm(s['hT'], dproj, 'nn', BF16, "odd_in_wgrad", bk=('seg', 3))
            w_in, in_kind, in_name = wb['odd_w_in'], ('seg', 3), "odd_in_dgrad"
        dep = on_layer_grads(2 * i, gb)
        dx, dxb, dg = _mm_norm_bwd(dproj, w_in, s['x'], w['norm_mix_g'][i], dx, in_name, ak=in_kind, dep=dep)
        gs['norm_mix_g'][i] = dg.reshape(D)

    gsmall = {n: (v if n == 'norm_final_g' else jnp.stack(v)) for n, v in gs.items()}
    return loss8[0, 0], dx, gsmall


_HBM = pl.BlockSpec(memory_space=pltpu.HBM)
_CHIP_FLIPS = ((0, 0), (1, 0), (0, 1), (1, 1))


def _coords():
    return lax.axis_index("x"), lax.axis_index("y"), lax.axis_index("c")


def _flip(v, f):
    return 1 - v if f else v


def _shard_of(ref, axis, s, width):
    start = pl.multiple_of(s * width, LANES if axis == ref.ndim - 1 else 16) if width % 16 == 0 else s * width
    idx = [slice(None)] * ref.ndim
    idx[axis] = pl.ds(start, width)
    return ref.at[tuple(idx)]


_SEM = pl.BlockSpec(memory_space=pltpu.SEMAPHORE)
_ANY = pl.BlockSpec(memory_space=pl.ANY)
_DATAFLOW = pltpu.SideEffectType.DATAFLOW_SIDE_EFFECTING


def _in_hbm(a):
    return pltpu.with_memory_space_constraint(a, pltpu.HBM)


def _model_layer(name, l):
    if name.startswith('ffn'):
        return l
    return 2 * l + 1 if name.startswith('odd') else 2 * l


def _place_quarter(shard, l, axis, chip, dtype, dep=None):
    _, r, c = shard.shape
    tr = _pick(r, prefs=(512, 256, 128, 64, 32, 16))
    nrb = r // tr

    def body(chip_ref, i_ref, *rest):
        rest[-1][...] = i_ref[...].astype(dtype)

    if axis == 1:
        out_shape, o_map = (r, c * N_CHIPS), (lambda i, s: (i, s[0]))
    else:
        out_shape, o_map = (r * N_CHIPS, c), (lambda i, s: (s[0] * nrb + i, 0))
    in_specs = [pl.BlockSpec((None, tr, c), lambda i, s: (l, i, 0))]
    args = [chip, shard]
    if dep is not None:
        in_specs.append(pl.BlockSpec(memory_space=pl.ANY))
        args.append(dep)
    return pl.pallas_call(
        body, name="place_quarter", out_shape=jax.ShapeDtypeStruct(out_shape, dtype),
        grid_spec=pltpu.PrefetchScalarGridSpec(
            num_scalar_prefetch=1, grid=(nrb,), in_specs=in_specs, out_specs=pl.BlockSpec((tr, c), o_map)),
        compiler_params=_cparams(("parallel",)),
    )(*args)


def _gather_copies(land_refs, send_sem, recv_sem, axes, landing_chip_of):
    x, y, c = _coords()
    out = []
    for j, land in enumerate(land_refs):
        width = land.shape[axes[j]] // N_CHIPS
        for f in (1, 2, 3):
            fx, fy = _CHIP_FLIPS[f]
            px, py = _flip(x, fx), _flip(y, fy)
            lx, ly = landing_chip_of(px, py)
            out.append(pltpu.make_async_remote_copy(
                src_ref=_shard_of(land, axes[j], 2 * x + y, width), dst_ref=_shard_of(land, axes[j], 2 * lx + ly, width),
                send_sem=send_sem.at[3 * j + f - 1], recv_sem=recv_sem.at[3 * j + f - 1],
                device_id=(px, py, c), device_id_type=MESH))
    return out


def _gather_start(tag, lands, axes, dep=None):
    n = len(lands)

    def body(*refs):
        land_refs, send_sem, recv_sem = refs[:n], refs[-3], refs[-2]
        x, y, _ = _coords()
        for cp in _gather_copies(land_refs, send_sem, recv_sem, axes, lambda px, py: (x, y)):
            cp.start()
        refs[-1][...] = jnp.zeros_like(refs[-1])

    thru = [pltpu.HBM(a.shape, a.dtype) for a in lands]
    outs = pl.pallas_call(
        body, name=f"gather_start_{tag}",
        out_shape=tuple(thru + [pltpu.SemaphoreType.DMA((3 * n,)), pltpu.SemaphoreType.DMA((3 * n,)),
                                jax.ShapeDtypeStruct((SUBLANES, LANES), F32)]),
        in_specs=[_HBM] * n + ([_ANY] if dep is not None else []),
        out_specs=tuple([_HBM] * n + [_SEM, _SEM, pl.BlockSpec(memory_space=pltpu.VMEM)]),
        input_output_aliases={i: i for i in range(n)},
        compiler_params=pltpu.CompilerParams(has_side_effects=_DATAFLOW),
    )(*[_in_hbm(a) for a in lands], *([dep] if dep is not None else []))
    return list(outs[:n]), outs[n], outs[n + 1], outs[n + 2]


def _gather_wait(tag, lands, send_sem, recv_sem, axes, after):
    n = len(lands)

    def body(*refs):
        for cp in _gather_copies(refs[:n], refs[n], refs[n + 1], axes, lambda px, py: (px, py)):
            cp.wait_send()
            cp.wait_recv()

    outs = pl.pallas_call(
        body, name=f"gather_wait_{tag}", out_shape=tuple(pltpu.HBM(a.shape, a.dtype) for a in lands),
        in_specs=[_HBM] * n + [_SEM, _SEM, _ANY], out_specs=tuple([_HBM] * n),
        input_output_aliases={i: i for i in range(n)},
        compiler_params=pltpu.CompilerParams(has_side_effects=_DATAFLOW),
    )(*lands, send_sem, recv_sem, after)
    return list(outs)


N_SLOTS = N_DEV - 1


def _scatter_sends(grad_refs, land_refs, send_sem, recv_sem, meta):
    x, y, c = _coords()
    out = []
    for j, (axis, owner, q, width) in enumerate(meta):
        other = c if owner == 0 else 1 - c
        for f, (fx, fy) in enumerate(_CHIP_FLIPS):
            px, py = _flip(x, fx), _flip(y, fy)
            slot = f + 4 * other - 1
            out.append((other if f == 0 else None, pltpu.make_async_remote_copy(
                src_ref=_shard_of(grad_refs[j], axis, 2 * px + py, width), dst_ref=land_refs[j].at[q, slot],
                send_sem=send_sem.at[4 * j + f], recv_sem=recv_sem.at[N_SLOTS * j + slot],
                device_id=(px, py, owner), device_id_type=MESH)))
    return out


def _scatter_start(layer, grads, lands, meta):
    n = len(grads)
    uniq = []
    for a in lands:
        if not any(a is u for u in uniq):
            uniq.append(a)
    which = [next(k for k, u in enumerate(uniq) if u is a) for a in lands]
    nu = len(uniq)

    def body(*refs):
        grad_refs, land_u = refs[:n], refs[n:n + nu]
        send_sem, recv_sem = refs[n + nu], refs[n + nu + 1]
        for other, cp in _scatter_sends(grad_refs, [land_u[k] for k in which], send_sem, recv_sem, meta):
            if other is None:
                cp.start()
            else:
                pl.when(other == 1)(cp.start)
        refs[-1][...] = jnp.zeros_like(refs[-1])

    thru = [pltpu.HBM(a.shape, a.dtype) for a in list(grads) + uniq]
    outs = pl.pallas_call(
        body, name=f"scatter_start_{layer}",
        out_shape=tuple([pltpu.SemaphoreType.DMA((4 * n,)), pltpu.SemaphoreType.DMA((N_SLOTS * n,))] + thru
                        + [jax.ShapeDtypeStruct((SUBLANES, LANES), F32)]),
        in_specs=[_HBM] * (n + nu),
        out_specs=tuple([_SEM, _SEM] + [_HBM] * (n + nu) + [pl.BlockSpec(memory_space=pltpu.VMEM)]),
        input_output_aliases={i: 2 + i for i in range(n + nu)},
        compiler_params=pltpu.CompilerParams(has_side_effects=_DATAFLOW),
    )(*[_in_hbm(a) for a in list(grads) + uniq])
    new_lands = [outs[2 + n + k] for k in which]
    return outs[0], outs[1], list(outs[2:2 + n]), new_lands, outs[-1]


def _scatter_wait(started, lands):
    nl = len(lands)
    flat_grads = [g for s in started for g in s[2]]
    ng, ns = len(flat_grads), len(started)

    def body(*refs):
        land_refs = refs[:nl]
        grad_refs = refs[nl:nl + ng]
        sem_refs = refs[nl + ng:nl + ng + 2 * ns]
        _, _, c = _coords()
        off = 0
        for k, (_, _, grads, idx, meta) in enumerate(started):
            send_sem, recv_sem = sem_refs[2 * k], sem_refs[2 * k + 1]
            lr = [land_refs[i] for i in idx]
            for other, cp in _scatter_sends(grad_refs[off:off + len(grads)], lr, send_sem, recv_sem, meta):
                if other is None:
                    cp.wait_send()
                else:
                    pl.when(other == 1)(cp.wait_send)
            for j, (axis, owner, q, width) in enumerate(meta):
                mine = (c if owner == 0 else 1 - c) == 0

                @pl.when(mine)
                def _():
                    for slot in range(N_SLOTS):
                        land = lr[j].at[q, slot]
                        pltpu.make_async_remote_copy(
                            src_ref=land, dst_ref=land, send_sem=send_sem.at[0], recv_sem=recv_sem.at[N_SLOTS * j + slot],
                            device_id=_coords(), device_id_type=MESH).wait_recv()
            off += len(grads)

    args = list(lands) + flat_grads
    thru = [pltpu.HBM(a.shape, a.dtype) for a in args]
    sems = [s for st in started for s in st[:2]]
    outs = pl.pallas_call(
        body, name="scatter_wait", out_shape=tuple(thru), in_specs=[_HBM] * (nl + ng) + [_SEM] * (2 * ns),
        out_specs=tuple([_HBM] * (nl + ng)), input_output_aliases={i: i for i in range(nl + ng)},
        compiler_params=pltpu.CompilerParams(has_side_effects=_DATAFLOW),
    )(*args, *sems)
    return list(outs[:nl]), list(outs[nl:])


def _sum_and_share(recv, layer_grads, axis, chip, name):
    n, ns, r, c = recv.shape
    tr = _pick(r, prefs=(256, 128, 64, 32, 16))
    nr = r // tr
    nsteps = n * nr
    nlay = len(layer_grads)
    own_map = (lambda h, i, s: (i, s[0])) if axis == 1 else (lambda h, i, s: (s[0] * nr + i, 0))

    def body(chip_ref, i_ref, *rest):
        g_refs = rest[:nlay]
        o_ref, buf, loc_sems, send_sems, recv_sems = rest[nlay:]
        h, i = pl.program_id(0), pl.program_id(1)
        step = h * nr + i
        slot = step % 2
        x, y, core = _coords()
        layer = core * n + h
        own = g_refs[0][...]
        for l in range(1, nlay):
            own = jnp.where(layer == l, g_refs[l][...], own)

        def copies(sl):
            dst = o_ref.at[core * n + h, pl.ds(pl.multiple_of(i * tr, tr), tr), :]
            loc = pltpu.make_async_copy(buf.at[sl], dst, loc_sems.at[sl])
            rem = pltpu.make_async_remote_copy(
                src_ref=buf.at[sl], dst_ref=dst, send_sem=send_sems.at[sl], recv_sem=recv_sems.at[step],
                device_id=(x, y, 1 - core), device_id_type=MESH)
            return loc, rem

        def drain(sl):
            loc, rem = copies(sl)
            loc.wait()
            rem.wait_send()

        pl.when(step >= 2)(lambda: drain(slot))
        acc = own.astype(F32)
        for s in range(ns):
            acc = acc + i_ref[s].astype(F32)
        buf[slot] = acc
        loc, rem = copies(slot)
        loc.start()
        rem.start()

        @pl.when(step == nsteps - 1)
        def _():
            drain(slot)
            if nsteps > 1:
                drain(1 - slot)
            for hh in range(n):
                for ii in range(nr):
                    land = o_ref.at[(1 - core) * n + hh, pl.ds(ii * tr, tr), :]
                    pltpu.make_async_remote_copy(
                        src_ref=buf.at[0], dst_ref=land, send_sem=send_sems.at[0], recv_sem=recv_sems.at[hh * nr + ii],
                        device_id=(x, y, 1 - core), device_id_type=MESH).wait_recv()

    return pl.pallas_call(
        body, name=name, out_shape=jax.ShapeDtypeStruct((2 * n, r, c), F32),
        grid_spec=pltpu.PrefetchScalarGridSpec(
            num_scalar_prefetch=1, grid=(n, nr),
            in_specs=[pl.BlockSpec((None, ns, tr, c), lambda h, i, s: (h, 0, i, 0))]
            + [pl.BlockSpec((tr, c), own_map)] * nlay,
            out_specs=_HBM,
            scratch_shapes=[pltpu.VMEM((2, tr, c), F32), pltpu.SemaphoreType.DMA((2,)),
                            pltpu.SemaphoreType.DMA((2,)), pltpu.SemaphoreType.DMA((nsteps,))]),
        compiler_params=_cparams(("arbitrary", "arbitrary")),
    )(chip, recv, *layer_grads)


def _gather_sums_over_chips(part):
    def body(i_ref, o_ref, send_sems, recv_sems):
        x, y, c = _coords()
        o_ref[2 * x + y] = i_ref[...]

        def copy(f, slot_chip):
            fx, fy = _CHIP_FLIPS[f]
            return pltpu.make_async_remote_copy(
                src_ref=i_ref, dst_ref=o_ref.at[2 * slot_chip[0] + slot_chip[1]], send_sem=send_sems.at[f - 1],
                recv_sem=recv_sems.at[f - 1], device_id=(_flip(x, fx), _flip(y, fy), c), device_id_type=MESH)

        sends = [copy(f, (x, y)) for f in (1, 2, 3)]
        for cp in sends:
            cp.start()
        for f in (1, 2, 3):
            fx, fy = _CHIP_FLIPS[f]
            copy(f, (_flip(x, fx), _flip(y, fy))).wait_recv()
        for cp in sends:
            cp.wait_send()

    vmem = pl.BlockSpec(memory_space=pltpu.VMEM)
    return pl.pallas_call(
        body, name="gather_small_sums", out_shape=jax.ShapeDtypeStruct((N_CHIPS,) + part.shape, part.dtype),
        in_specs=[vmem], out_specs=vmem,
        scratch_shapes=[pltpu.SemaphoreType.DMA((3,)), pltpu.SemaphoreType.DMA((3,))],
    )(part)


def _adamw_update(w_ref, g_ref, m_ref, v_ref, d_ref, mo_ref, vo_ref):
    bc1 = 1.0 - ADAM_B1 ** ADAM_STEP
    bc2 = 1.0 - ADAM_B2 ** ADAM_STEP
    gv = g_ref[...]
    mn = ADAM_B1 * m_ref[...] + (1.0 - ADAM_B1) * gv
    vn = ADAM_B2 * v_ref[...] + (1.0 - ADAM_B2) * (gv * gv)
    d_ref[...] = -ADAM_LR * ((mn / bc1) / (jnp.sqrt(vn / bc2) + ADAM_EPS) + ADAM_WD * w_ref[...])
    mo_ref[...] = mn
    vo_ref[...] = vn


def _adamw(w, g, m, v, name):
    def body(*refs):
        _adamw_update(*refs)

    tr = _pick(w.shape[0], prefs=(256, 128, 64, 32, 16, 8))
    blk = pl.BlockSpec((tr, w.shape[1]), lambda i: (i, 0))
    sds = jax.ShapeDtypeStruct(w.shape, F32)
    return pl.pallas_call(
        body, name=name, out_shape=(sds, sds, sds), grid=(w.shape[0] // tr,), in_specs=[blk] * 4,
        out_specs=(blk,) * 3, compiler_params=_cparams(("parallel",)),
    )(w, g, m, v)


def _adamw_many(tensors, name, by_layer=False):
    n = len(tensors)

    def body(*refs):
        for t in range(n):
            _adamw_update(*refs[4 * t:4 * t + 4], *refs[4 * n + 3 * t:4 * n + 3 * t + 3])

    def spec(a):
        nd = a.ndim
        if by_layer:
            return pl.BlockSpec((1,) + a.shape[1:], lambda i: (i,) + (0,) * (nd - 1))
        return pl.BlockSpec(a.shape, lambda i: (0,) * nd)

    steps = tensors[0][0].shape[0] if by_layer else 1
    outs = pl.pallas_call(
        body, name=name, out_shape=tuple(jax.ShapeDtypeStruct(t[0].shape, F32) for t in tensors for _ in range(3)),
        grid=(steps,), in_specs=[spec(a) for t in tensors for a in t],
        out_specs=tuple(spec(t[0]) for t in tensors for _ in range(3)), compiler_params=_cparams(("parallel",)),
    )(*[a for t in tensors for a in t])
    return [tuple(outs[3 * t:3 * t + 3]) for t in range(n)]


_PACK_QUANTUM = 256 * LANES


def _pack(arrs):
    flat = jnp.concatenate([a.reshape(-1).astype(F32) for a in arrs])
    flat = jnp.pad(flat, (0, (-flat.shape[0]) % _PACK_QUANTUM))
    return flat.reshape(-1, LANES)


def _unpack(p, shapes):
    flat = p.reshape(-1)
    out, off = [], 0
    for s in shapes:
        n = int(np.prod(s))
        out.append(flat[off:off + n].reshape(s))
        off += n
    return out


def kernel(*args):
    nw = len(WEIGHTS)
    x, tgt = args[0], args[1 + nw]
    w = dict(zip(WEIGHTS, args[1:1 + nw]))
    m = dict(zip(WEIGHTS, args[2 + nw:2 + 2 * nw]))
    v = dict(zip(WEIGHTS, args[2 + 2 * nw:2 + 3 * nw]))
    _, L, D = x.shape
    chip = 2 * lax.axis_index("x") + lax.axis_index("y")

    big = list(BIG)
    small_sh_shapes = [w[n].shape for n in SMALL_SHARDED]
    nbig = len(big)
    chip1 = chip.reshape(1).astype(jnp.int32)
    axes2 = [BIG[n] - 1 for n in big] + [0]
    shards = [w[n] for n in big] + [_pack([w[n] for n in SMALL_SHARDED])[None]]
    pairs = [(t, l) for t in range(nbig + 1) for l in range(shards[t].shape[0])]
    depth = w['norm_mix_g'].shape[0]
    part_of = lambda t, l: 0 if t == nbig else 2 * _model_layer(big[t], l) + big[t].startswith('ffn')
    flying, token = [], None
    for g in range(2 * depth):
        ids = [k for k, (t, l) in enumerate(pairs) if part_of(t, l) == g]
        ts = [pairs[k][0] for k in ids]
        placed = [_place_quarter(shards[t], pairs[k][1], axes2[t], chip1, F32 if t == nbig else BF16, token)
                  for k, t in zip(ids, ts)]
        lands, send, recv, token = _gather_start(g, placed, [axes2[t] for t in ts], token)
        flying.append((ts, lands, send, recv))

    def wait_group(g, after):
        ts, lands, send, recv = flying[g]
        landed = _gather_wait(g, lands, send, recv, [axes2[t] for t in ts], token if after is None else after)
        return dict(zip(ts, landed))

    first = wait_group(0, None)
    packed = first.pop(nbig).reshape(N_CHIPS, -1, LANES)
    per_chip = [_unpack(packed[s], small_sh_shapes) for s in range(N_CHIPS)]
    wl = dict(w)
    for k, n in enumerate(SMALL_SHARDED):
        wl[n] = jnp.concatenate([per_chip[s][k] for s in range(N_CHIPS)], axis=-1)

    def layer_weights(i, after):
        got = first if i == 0 else wait_group(i, after)
        return {big[t]: a for t, a in got.items()}

    small_shapes = [(w[n].shape[:-1] + (w[n].shape[-1] * N_CHIPS,)) if n in SMALL_SHARDED else w[n].shape
                    for n in SMALL] + [(1,)]
    n_small = sum(int(np.prod(s)) for s in small_shapes)
    pack_rows = -(-n_small // _PACK_QUANTUM) * _PACK_QUANTUM // LANES
    nlayers = [w[n].shape[0] for n in big] + [2]
    halves = [n // 2 for n in nlayers]
    quarters = [tuple(w[n].shape[1:]) for n in big] + [(pack_rows // 2 // N_CHIPS, LANES)]
    wire = [BF16] * nbig + [F32]
    land_now = [lax.empty((halves[t], N_SLOTS) + quarters[t], wire[t]) for t in range(nbig + 1)]
    gparts = [[None] * n for n in nlayers]
    started = []

    def start_scatter(tag, ts, ls, arrays):
        meta = [(axes2[t], l // halves[t], l % halves[t], quarters[t][axes2[t]]) for t, l in zip(ts, ls)]
        send, recv, thru, new_lands, token = _scatter_start(tag, arrays, [land_now[t] for t in ts], meta)
        for t, ln in zip(ts, new_lands):
            land_now[t] = ln
        started.append((send, recv, thru, ts, meta, ls))
        return token

    def on_layer_grads(g, gb):
        ts = [big.index(n) for n in gb]
        return start_scatter(g, ts, [g // 2 if big[t].startswith('ffn') else g // 4 for t in ts],
                             [gb[big[t]] for t in ts])

    loss, dx, gsmall = _local_step(x.reshape(L, D), tgt.reshape(L, D), wl, layer_weights, on_layer_grads)
    gpack = _pack([gsmall[n] for n in SMALL] + [loss.reshape(1)])
    start_scatter(2 * depth, [nbig, nbig], [0, 1], [gpack[:pack_rows // 2], gpack[pack_rows // 2:]])
    landed, sent = _scatter_wait([s[:5] for s in started], land_now)
    for (t, l), g in zip([(t, l) for s in started for t, l in zip(s[3], s[5])], sent):
        gparts[t][l] = g
    gshard = {n: _sum_and_share(landed[t], gparts[t], axes2[t], chip1, "sum_share_" + n) for t, n in enumerate(big)}
    small_sum = _sum_and_share(landed[nbig], gparts[nbig], 0, chip1, "sum_share_small")
    gpack = _gather_sums_over_chips(small_sum).transpose(1, 0, 2, 3).reshape(pack_rows, LANES)
    gs = dict(zip(SMALL + ['loss'], _unpack(gpack, small_shapes)))
    loss = gs.pop('loss').reshape(())
    for n in SMALL_SHARDED:
        width = w[n].shape[-1]
        gs[n] = lax.dynamic_slice_in_dim(gs[n], chip * width, width, axis=gs[n].ndim - 1)

    grads, delta, new_m, new_v = {}, {}, {}, {}
    for n in big:
        shp = w[n].shape
        flat = lambda a: a.reshape(shp[0] * shp[1], shp[2])
        g = gshard[n]
        grads[n] = g
        d_, m_, v_ = _adamw(flat(w[n]), flat(g), flat(m[n]), flat(v[n]), "adamw_" + n)
        delta[n], new_m[n], new_v[n] = d_.reshape(shp), m_.reshape(shp), v_.reshape(shp)
    sparse = [n for n in SMALL if w[n].ndim == 4 and w[n].shape[-1] < LANES // 2]
    for names, by_layer in ((sparse, True), ([n for n in SMALL if n not in sparse], False)):
        as2d = lambda a: a.reshape(1, -1) if a.ndim == 1 else a
        res = _adamw_many([(as2d(w[n]), as2d(gs[n]), as2d(m[n]), as2d(v[n])) for n in names],
                          "adamw_small_by_layer" if by_layer else "adamw_small", by_layer)
        for n, (d_, m_, v_) in zip(names, res):
            shp = w[n].shape
            grads[n], delta[n], new_m[n], new_v[n] = gs[n], d_.reshape(shp), m_.reshape(shp), v_.reshape(shp)

    return (loss, dx.reshape(1, L, D), *[grads[n] for n in WEIGHTS], *[delta[n] for n in WEIGHTS],
            *[new_m[n] for n in WEIGHTS], *[new_v[n] for n in WEIGHTS])
```

```python
import functools
import math

import numpy as np
import jax
import jax.numpy as jnp
from jax import lax
from jax.experimental import pallas as pl
from jax.experimental.pallas import tpu as pltpu

F32 = jnp.float32
BF16 = jnp.bfloat16
MESH = pl.DeviceIdType.MESH

EPS = 1e-6
CHUNK = 128
POOL_WINDOWS = (2, 4, 8, 16)
LANES = 128
SUBLANES = 8
SCAN_CHUNKS = SUBLANES
S5_GROUPS_PER_STEP = 4
MM_TM_CAP, MM_TN_CAP, MM_TK_CAP = 1408, 1408, 2048
MM_TK_WHOLE = 2048
VMEM_LIMIT = 48 * 1024 * 1024
VMEM_LIMIT_S5 = 56 * 1024 * 1024

ADAM_LR, ADAM_B1, ADAM_B2, ADAM_EPS, ADAM_WD, ADAM_STEP = 0.001, 0.9, 0.999, 1e-08, 0.01, 10

WEIGHTS = ['norm_mix_g', 'even_w_in', 'even_conv_w', 'ssm_log_step', 'ssm_a_re', 'ssm_a_im', 'ssm_b_re',
           'ssm_b_im', 'ssm_c_re', 'ssm_c_im', 'ssm_d', 'ssm_glu_w', 'ssm_glu_b', 'even_w_out', 'odd_w_in',
           'pool_w', 'pool_scale', 'sgu_norm_g', 'sgu_w', 'sgu_b', 'odd_w_out', 'norm_ffn_g', 'ffn_w_up',
           'ffn_conv_w', 'ffn_conv_b', 'ffn_w_down', 'norm_final_g']
BIG = {'even_w_in': 2, 'ssm_glu_w': 1, 'even_w_out': 1, 'odd_w_in': 2, 'odd_w_out': 1, 'ffn_w_up': 2,
       'ffn_w_down': 1}
SMALL_SHARDED = ('even_conv_w', 'pool_scale', 'sgu_norm_g', 'ffn_conv_w')
SMALL = [n for n in WEIGHTS if n not in BIG]
N_CHIPS = 4
N_DEV = 8


def _cparams(sem=None, vmem=VMEM_LIMIT):
    kw = dict(vmem_limit_bytes=vmem)
    if sem is not None:
        kw['dimension_semantics'] = sem
    return pltpu.CompilerParams(**kw)


def _pick(n, segs=(), prefs=(1024, 512, 256, 128)):
    for t in prefs:
        if n % t == 0 and all(s % t == 0 for s in segs if s):
            return t
    return n


def _largest_tile(n, segs, cap):
    best = None
    for t in range(LANES, min(n, cap) + 1, LANES):
        if n % t == 0 and all(s % t == 0 for s in segs if s):
            best = t
    return best if best is not None else n


def _ldims(arr, kind):
    if kind is None:
        return arr.shape
    if kind[0] == 'lead':
        return arr.shape[1:]
    return (arr.shape[1], arr.shape[0] * arr.shape[2])


def _segw(arr, kind):
    return arr.shape[2] if (kind is not None and kind[0] == 'seg') else None


def _opspec(arr, kind, br, bc, rfn, cfn):
    if kind is None:
        return pl.BlockSpec((br, bc), lambda i, j, k: (rfn(i, j, k), cfn(i, j, k)))
    if kind[0] == 'lead':
        lead = kind[1]
        return pl.BlockSpec((None, br, bc), lambda i, j, k: (lead, rfn(i, j, k), cfn(i, j, k)))
    per = arr.shape[2] // bc
    return pl.BlockSpec((None, br, bc), lambda i, j, k: (cfn(i, j, k) // per, rfn(i, j, k), cfn(i, j, k) % per))


def _mm(a, b, mode, out_dtype, name, ak=None, bk=None, ok=None, res=None, dep=None):
    ar, ac = _ldims(a, ak)
    br_, bc_ = _ldims(b, bk)
    if mode == 'nn':
        M, K, N = ar, ac, bc_
        assert br_ == K
    else:
        M, K, N = ar, ac, br_
        assert bc_ == K
    sa, sb = _segw(a, ak), _segw(b, bk)
    so = (N // ok[1]) if ok is not None else None
    tm = _largest_tile(M, [], MM_TM_CAP)
    tn = _largest_tile(N, [sb if mode == 'nn' else None, so], MM_TN_CAP)
    ksegs = [sa, sb if mode == 'nt' else None]
    tk = K if (K <= MM_TK_WHOLE and not any(ksegs)) else _largest_tile(K, ksegs, MM_TK_CAP)
    nk = K // tk
    I = lambda i, j, k: i
    J = lambda i, j, k: j
    Kk = lambda i, j, k: k
    a_spec = _opspec(a, ak, tm, tk, I, Kk)
    if mode == 'nn':
        b_spec = _opspec(b, bk, tk, tn, Kk, J)
        dims = (((1,), (0,)), ((), ()))
    else:
        b_spec = _opspec(b, bk, tn, tk, J, Kk)
        dims = (((1,), (1,)), ((), ()))
    if ok is None:
        out_shape = jax.ShapeDtypeStruct((M, N), out_dtype)
        o_spec = pl.BlockSpec((tm, tn), lambda i, j, k: (i, j))
    else:
        out_shape = jax.ShapeDtypeStruct((ok[1], M, N // ok[1]), out_dtype)
        per = (N // ok[1]) // tn
        o_spec = pl.BlockSpec((None, tm, tn), lambda i, j, k: (j // per, i, j % per))
    has_res = res is not None

    def body(*refs):
        a_ref, b_ref = refs[0], refs[1]
        r_ref = refs[2] if has_res else None
        o_ref = refs[n_in]
        prod = lax.dot_general(a_ref[...].astype(BF16), b_ref[...].astype(BF16), dims, preferred_element_type=F32)
        if nk == 1:
            o_ref[...] = (prod + r_ref[...] if has_res else prod).astype(out_dtype)
            return
        acc = refs[-1]
        k = pl.program_id(2)

        @pl.when(k == 0)
        def _():
            acc[...] = prod

        @pl.when(k > 0)
        def _():
            acc[...] += prod

        @pl.when(k == nk - 1)
        def _():
            o = acc[...]
            if has_res:
                o = o + r_ref[...]
            o_ref[...] = o.astype(out_dtype)

    in_specs = [a_spec, b_spec]
    args = [a, b]
    if has_res:
        in_specs.append(pl.BlockSpec((tm, tn), lambda i, j, k: (i, j)))
        args.append(res)
    if dep is not None:
        in_specs.append(pl.BlockSpec(memory_space=pl.ANY))
        args.append(dep)
    n_in = len(args)
    return pl.pallas_call(
        body, name=name, out_shape=out_shape, grid=(M // tm, N // tn, nk), in_specs=in_specs, out_specs=o_spec,
        scratch_shapes=[pltpu.VMEM((tm, tn), F32)] if nk > 1 else [],
        compiler_params=_cparams(("parallel", "parallel", "arbitrary")),
    )(*args)


_G0 = math.sqrt(2.0 / math.pi)
_G1 = 0.044715


def _gelu(x):
    return 0.5 * x * (1.0 + jnp.tanh(_G0 * (x + _G1 * x * x * x)))


def _gelu_grad(x):
    x2 = x * x
    t = jnp.tanh(_G0 * (x + _G1 * x * x2))
    return 0.5 * (1.0 + t) + 0.5 * x * (1.0 - t * t) * (_G0 * (1.0 + 3.0 * _G1 * x2))


def _sigmoid(x):
    return 1.0 / (1.0 + jnp.exp(-x))


def _down(v, k):
    r = pltpu.roll(v, k, axis=0)
    row = lax.broadcasted_iota(jnp.int32, (SUBLANES, v.shape[1]), 0)
    return jnp.concatenate([jnp.where(row >= k, r[:SUBLANES], 0.0), r[SUBLANES:]], axis=0)


def _up(v, k):
    n = v.shape[0]
    r = pltpu.roll(v, n - k, axis=0)
    row = lax.broadcasted_iota(jnp.int32, (SUBLANES, v.shape[1]), 0)
    return jnp.concatenate([r[:n - SUBLANES], jnp.where(row < SUBLANES - k, r[n - SUBLANES:], 0.0)], axis=0)


def _taps(v):
    return _down(v, 2), _down(v, 1), v


def _conv3(taps, w):
    return w[0:1, :] * taps[0] + w[1:2, :] * taps[1] + w[2:3, :] * taps[2]


def _conv3_t(dv, w):
    return w[2:3, :] * dv + w[1:2, :] * _up(dv, 1) + w[0:1, :] * _up(dv, 2)


def _conv3_dw(dv, taps):
    return tuple(jnp.sum(dv * tp, axis=0, keepdims=True) for tp in taps)


def _cmul(ar, ai, br, bi):
    return ar * br - ai * bi, ar * bi + ai * br


def _cpow(lr, li, n):
    rr = ri = None
    br, bi = lr, li
    while n:
        if n & 1:
            rr, ri = (br, bi) if rr is None else _cmul(rr, ri, br, bi)
        n >>= 1
        if n:
            br, bi = _cmul(br, bi, br, bi)
    return rr, ri


NORM_ROWS = 256


def _norm_mm(x, g, b, out_dtype, name, ok=None):
    M, D = x.shape
    N = b.shape[1]
    so = (N // ok[1]) if ok is not None else None
    tm = _largest_tile(M, [], 1024)
    tn = _largest_tile(N, [so], MM_TN_CAP)
    if ok is None:
        out_shape = jax.ShapeDtypeStruct((M, N), out_dtype)
        o_spec = pl.BlockSpec((tm, tn), lambda i, j: (i, j))
    else:
        out_shape = jax.ShapeDtypeStruct((ok[1], M, N // ok[1]), out_dtype)
        per = (N // ok[1]) // tn
        o_spec = pl.BlockSpec((None, tm, tn), lambda i, j: (j // per, i, j % per))

    def body(x_ref, g_ref, b_ref, o_ref, ht_ref, h_scr):
        @pl.when(pl.program_id(1) == 0)
        def _():
            for c in range(tm // NORM_ROWS):
                rows = pl.ds(c * NORM_ROWS, NORM_ROWS)
                xv = x_ref[rows, :]
                h = xv * lax.rsqrt(jnp.mean(xv * xv, axis=-1, keepdims=True) + EPS) * g_ref[...]
                h_scr[rows, :] = h.astype(BF16)
                ht_ref[:, rows] = h.T.astype(BF16)

        o_ref[...] = jnp.dot(h_scr[...], b_ref[...], preferred_element_type=F32).astype(out_dtype)

    return pl.pallas_call(
        body, name=name, out_shape=(out_shape, jax.ShapeDtypeStruct((D, M), BF16)), grid=(M // tm, N // tn),
        in_specs=[pl.BlockSpec((tm, D), lambda i, j: (i, 0)), pl.BlockSpec((1, D), lambda i, j: (0, 0)),
                  pl.BlockSpec((D, tn), lambda i, j: (0, j))],
        out_specs=(o_spec, pl.BlockSpec((D, tm), lambda i, j: (0, i))),
        scratch_shapes=[pltpu.VMEM((tm, D), BF16)], compiler_params=_cparams(("parallel", "arbitrary")),
    )(x, g.reshape(1, D), b)


def _mm_norm_bwd(a, b, x, g, dres, name, ak=None, dep=None):
    M, K = _ldims(a, ak)
    D = b.shape[0]
    assert b.shape[1] == K and x.shape == (M, D)
    sa = _segw(a, ak)
    tm = _largest_tile(M, [], 1024)
    tk = K if (K <= MM_TK_WHOLE and not sa) else _largest_tile(K, [sa], MM_TK_CAP)
    ni, nk = M // tm, K // tk
    a3 = _opspec(a, ak, tm, tk, lambda i, j, k: i, lambda i, j, k: k)
    a_spec = pl.BlockSpec(a3.block_shape, lambda i, k: a3.index_map(i, 0, k))
    n_in = 5 + (dep is not None)

    def body(*refs):
        a_ref, b_ref, x_ref, g_ref, r_ref = refs[:5]
        dx_ref, dxb_ref, dg_ref, acc, accg = refs[n_in:]
        i, k = pl.program_id(0), pl.program_id(1)
        prod = lax.dot_general(a_ref[...].astype(BF16), b_ref[...], (((1,), (1,)), ((), ())),
                               preferred_element_type=F32)

        @pl.when(k == 0)
        def _():
            acc[...] = prod

        @pl.when(k > 0)
        def _():
            acc[...] += prod

        @pl.when((i == 0) & (k == 0))
        def _():
            accg[...] = jnp.zeros_like(accg)

        @pl.when(k == nk - 1)
        def _():
            for c in range(tm // NORM_ROWS):
                rows = pl.ds(c * NORM_ROWS, NORM_ROWS)
                xv = x_ref[rows, :]
                r = lax.rsqrt(jnp.mean(xv * xv, axis=-1, keepdims=True) + EPS)
                xh = xv * r
                dhv = acc[rows, :]
                accg[...] += jnp.sum((dhv * xh).reshape(NORM_ROWS // SUBLANES, SUBLANES, D), axis=0)
                dxh = dhv * g_ref[...]
                dxv = r_ref[rows, :] + r * (dxh - xh * jnp.mean(dxh * xh, axis=-1, keepdims=True))
                dx_ref[rows, :] = dxv
                dxb_ref[rows, :] = dxv.astype(BF16)

        @pl.when((i == ni - 1) & (k == nk - 1))
        def _():
            dg_ref[...] = jnp.sum(accg[...], axis=0, keepdims=True)

    row = pl.BlockSpec((tm, D), lambda i, k: (i, 0))
    vec = pl.BlockSpec((1, D), lambda i, k: (0, 0))
    in_specs = [a_spec, pl.BlockSpec((D, tk), lambda i, k: (0, k)), row, vec, row]
    args = [a, b, x, g.reshape(1, D), dres]
    if dep is not None:
        in_specs.append(pl.BlockSpec(memory_space=pl.ANY))
        args.append(dep)
    return pl.pallas_call(
        body, name=name,
        out_shape=(jax.ShapeDtypeStruct((M, D), F32), jax.ShapeDtypeStruct((M, D), BF16),
                   jax.ShapeDtypeStruct((1, D), F32)),
        grid=(ni, nk), in_specs=in_specs, out_specs=(row, row, vec),
        scratch_shapes=[pltpu.VMEM((tm, D), F32), pltpu.VMEM((SUBLANES, D), F32)],
        compiler_params=_cparams(("arbitrary", "arbitrary"), VMEM_LIMIT_S5),
    )(*args)


def _loss_head(x, g, tgt):
    L, D = x.shape
    tr = _pick(L, prefs=(512, 256, 128))
    nsteps = L // tr

    def body(x_ref, g_ref, t_ref, loss_ref, dx_ref, dxb_ref, dg_ref, acc_g, acc_l):
        i = pl.program_id(0)

        @pl.when(i == 0)
        def _():
            acc_g[...] = jnp.zeros_like(acc_g)
            acc_l[...] = jnp.zeros_like(acc_l)

        xv = x_ref[...]
        gv = g_ref[...]
        r = lax.rsqrt(jnp.mean(xv * xv, axis=-1, keepdims=True) + EPS)
        xh = xv * r
        e = xh * gv - t_ref[...]
        acc_l[...] += jnp.sum((e * e).reshape(tr // SUBLANES, SUBLANES, D), axis=0)
        dy = e * (1.0 / D)
        acc_g[...] += jnp.sum((dy * xh).reshape(tr // SUBLANES, SUBLANES, D), axis=0)
        dxh = dy * gv
        dxv = r * (dxh - xh * jnp.mean(dxh * xh, axis=-1, keepdims=True))
        dx_ref[...] = dxv
        dxb_ref[...] = dxv.astype(BF16)

        @pl.when(i == nsteps - 1)
        def _():
            dg_ref[...] = jnp.sum(acc_g[...], axis=0, keepdims=True)
            tot = jnp.sum(jnp.sum(acc_l[...], axis=0, keepdims=True), axis=1, keepdims=True) * (0.5 / D)
            loss_ref[...] = jnp.broadcast_to(tot, (SUBLANES, LANES))

    row = pl.BlockSpec((tr, D), lambda i: (i, 0))
    vec = pl.BlockSpec((1, D), lambda i: (0, 0))
    return pl.pallas_call(
        body, name="loss_head",
        out_shape=(jax.ShapeDtypeStruct((SUBLANES, LANES), F32), jax.ShapeDtypeStruct((L, D), F32),
                   jax.ShapeDtypeStruct((L, D), BF16), jax.ShapeDtypeStruct((1, D), F32)),
        grid=(nsteps,), in_specs=[row, vec, row],
        out_specs=(pl.BlockSpec((SUBLANES, LANES), lambda i: (0, 0)), row, row, vec),
        scratch_shapes=[pltpu.VMEM((SUBLANES, D), F32), pltpu.VMEM((SUBLANES, D), F32)],
        compiler_params=_cparams(("arbitrary",)),
    )(x, g.reshape(1, D), tgt)


def _sconv_fwd(proj4, conv_w, name):
    _, L, C = proj4.shape
    cb = LANES

    def body(p_ref, w_ref, o_ref):
        xa, ba, ca = p_ref[0].astype(F32), p_ref[1].astype(F32), p_ref[2].astype(F32)
        o_ref[...] = (ba * _conv3(_taps(ca * xa), w_ref[...])).astype(BF16)

    return pl.pallas_call(
        body, name=name, out_shape=jax.ShapeDtypeStruct((L, 2 * C), BF16), grid=(C // cb,),
        in_specs=[pl.BlockSpec((3, L, cb), lambda j: (0, 0, j)), pl.BlockSpec((3, cb), lambda j: (0, j))],
        out_specs=pl.BlockSpec((L, cb), lambda j: (0, j)), compiler_params=_cparams(("parallel",)),
    )(proj4, conv_w)


def _sconv_bwd(proj4, dmix, conv_w, name):
    _, L, C = proj4.shape
    cb = LANES

    def body(p_ref, d_ref, w_ref, o_ref, dw_ref):
        xa, ba, ca = p_ref[0].astype(F32), p_ref[1].astype(F32), p_ref[2].astype(F32)
        w = w_ref[...]
        dya = d_ref[...]
        tq = _taps(ca * xa)
        cq = _conv3(tq, w)
        dcq = dya * ba
        dq = _conv3_t(dcq, w)
        for tap, dwt in enumerate(_conv3_dw(dcq, tq)):
            dw_ref[tap:tap + 1, :] = dwt
        o_ref[0] = (dq * ca).astype(BF16)
        o_ref[1] = (dya * cq).astype(BF16)
        o_ref[2] = (dq * xa).astype(BF16)

    return pl.pallas_call(
        body, name=name,
        out_shape=(jax.ShapeDtypeStruct((4, L, C), BF16), jax.ShapeDtypeStruct((3, C), F32)), grid=(C // cb,),
        in_specs=[pl.BlockSpec((3, L, cb), lambda j: (0, 0, j)), pl.BlockSpec((L, cb), lambda j: (0, j)),
                  pl.BlockSpec((3, cb), lambda j: (0, j))],
        out_specs=(pl.BlockSpec((3, L, cb), lambda j: (0, 0, j)), pl.BlockSpec((3, cb), lambda j: (0, j))),
        compiler_params=_cparams(("parallel",)),
    )(proj4, dmix, conv_w)


def _s5_prep(log_step, a_re, a_im, b_re, b_im, c_re, c_im):
    G, P = a_re.shape
    H = b_re.shape[-1]
    gs = S5_GROUPS_PER_STEP
    ns = G // gs
    gu = LANES // H
    lam = lax.complex(a_re, a_im)
    step = jnp.exp(log_step)[:, None]
    lam_bar = jnp.exp(lam * step)
    b_bar = ((lam_bar - 1.0) / lam)[..., None] * lax.complex(b_re, b_im)
    lr = jnp.real(lam_bar).reshape(ns, 1, gs * P)
    li = jnp.imag(lam_bar).reshape(ns, 1, gs * P)
    k = np.arange(ns)[:, None, None]
    oh = jnp.asarray((np.arange(gu)[None, :, None] == gs * (k % (gu // gs)) + np.arange(gs)[None, None, :]),
                     F32)
    bre = jnp.einsum('kgl,klph->kghlp', oh, jnp.real(b_bar).reshape(ns, gs, P, H)).reshape(ns, gu * H, gs * P)
    bim = jnp.einsum('kgl,klph->kghlp', oh, jnp.imag(b_bar).reshape(ns, gs, P, H)).reshape(ns, gu * H, gs * P)
    cre = jnp.einsum('kgl,klhp->klpgh', oh, c_re.reshape(ns, gs, H, P)).reshape(ns, gs * P, gu * H)
    cim = jnp.einsum('kgl,klhp->klpgh', oh, c_im.reshape(ns, gs, H, P)).reshape(ns, gs * P, gu * H)
    return lr, li, jnp.concatenate([bre, bim], axis=2), jnp.concatenate([cre, -cim], axis=1)


def _carry_tile(fr, fi, pr, pi, reverse):
    row = lax.broadcasted_iota(jnp.int32, fr.shape, 0)
    cr = jnp.zeros_like(fr)
    ci = jnp.zeros_like(fi)
    sr = jnp.zeros_like(fr[0:1])
    si = jnp.zeros_like(sr)
    order = range(SCAN_CHUNKS - 1, 0, -1) if reverse else range(0, SCAN_CHUNKS - 1)
    for c in order:
        fcr = jnp.sum(jnp.where(row == c, fr, 0.0), axis=0, keepdims=True)
        fci = jnp.sum(jnp.where(row == c, fi, 0.0), axis=0, keepdims=True)
        mr, mi = _cmul(pr, pi, sr, si)
        sr, si = mr + fcr, mi + fci
        nxt = c - 1 if reverse else c + 1
        cr = jnp.where(row == nxt, sr, cr)
        ci = jnp.where(row == nxt, si, ci)
    return cr, ci


def _scan_order_into(dst_ref, src_ref, T):
    for c in range(SCAN_CHUNKS):
        dst_ref[pl.ds(c, T, stride=SCAN_CHUNKS), :] = src_ref[pl.ds(c * T, T), :].astype(F32)


def _s5_fwd(proj4, lr, li, bmat, cmat, d, name):
    _, L, Du = proj4.shape
    ns, _, W2 = bmat.shape
    W = W2 // 2
    T = L // SCAN_CHUNKS
    rb = _pick(L, prefs=(512, 256, 128))
    per = (ns * LANES) // Du

    def body(ut_ref, lr_ref, li_ref, b_ref, c_ref, d_ref, y_ref, sr_ref, si_ref, u_ref):
        k = pl.program_id(0)
        _scan_order_into(u_ref, ut_ref, T)
        for r in range(L // rb):
            rows = pl.ds(r * rb, rb)
            bu = jnp.dot(u_ref[rows, :].astype(BF16), b_ref[...], preferred_element_type=F32)
            sr_ref[rows, :] = bu[:, :W]
            si_ref[rows, :] = bu[:, W:]
        lam_r = jnp.broadcast_to(lr_ref[...], (SUBLANES, W))
        lam_i = jnp.broadcast_to(li_ref[...], (SUBLANES, W))

        def local(t, carry):
            sr, si = carry
            rows = pl.ds(pl.multiple_of(t * SUBLANES, SUBLANES), SUBLANES)
            mr, mi = _cmul(lam_r, lam_i, sr, si)
            sr = mr + sr_ref[rows, :]
            si = mi + si_ref[rows, :]
            sr_ref[rows, :] = sr
            si_ref[rows, :] = si
            return sr, si

        z = jnp.zeros((SUBLANES, W), F32)
        fr, fi = lax.fori_loop(0, T, local, (z, z))
        pr, pi = _cpow(lam_r, lam_i, T)
        cr, ci = _carry_tile(fr, fi, pr[0:1], pi[0:1], reverse=False)

        def fix(t, carry):
            wr, wi = carry
            rows = pl.ds(pl.multiple_of(t * SUBLANES, SUBLANES), SUBLANES)
            ar, ai = _cmul(wr, wi, cr, ci)
            sr_ref[rows, :] += ar
            si_ref[rows, :] += ai
            return _cmul(wr, wi, lam_r, lam_i)

        lax.fori_loop(0, T, fix, (lam_r, lam_i))
        first = (k % per) == 0
        for r in range(L // rb):
            rows = pl.ds(r * rb, rb)
            s = jnp.concatenate([sr_ref[rows, :], si_ref[rows, :]], axis=1).astype(BF16)
            y = jnp.dot(s, c_ref[...], preferred_element_type=F32)

            @pl.when(first)
            def _():
                y_ref[rows, :] = y + d_ref[...] * u_ref[rows, :]

            @pl.when(jnp.logical_not(first))
            def _():
                y_ref[rows, :] += y

    ublk = pl.BlockSpec((L, LANES), lambda k: (0, k // per))
    sblk = pl.BlockSpec((L, W), lambda k: (0, k))
    lam = pl.BlockSpec((None, 1, W), lambda k: (k, 0, 0))
    return pl.pallas_call(
        body, name=name,
        out_shape=(jax.ShapeDtypeStruct((L, Du), F32), jax.ShapeDtypeStruct((L, ns * W), F32),
                   jax.ShapeDtypeStruct((L, ns * W), F32)),
        grid=(ns,),
        in_specs=[pl.BlockSpec((None, L, LANES), lambda k: (3, 0, k // per)), lam, lam,
                  pl.BlockSpec((None, LANES, 2 * W), lambda k: (k, 0, 0)),
                  pl.BlockSpec((None, 2 * W, LANES), lambda k: (k, 0, 0)),
                  pl.BlockSpec((1, LANES), lambda k: (0, k // per))],
        out_specs=(ublk, sblk, sblk), scratch_shapes=[pltpu.VMEM((L, LANES), F32)],
        compiler_params=_cparams(("arbitrary",), VMEM_LIMIT_S5),
    )(proj4, lr, li, bmat.astype(BF16), cmat.astype(BF16), d.reshape(1, Du))


def _s5_bwd(dy, proj4, dproj, s_re, s_im, lr, li, bmat, cmat, d, name):
    _, L, Du = proj4.shape
    ns, _, W2 = bmat.shape
    W = W2 // 2
    T = L // SCAN_CHUNKS
    rb = _pick(L, prefs=(512, 256, 128))
    per = (ns * LANES) // Du
    NT = (((1,), (1,)), ((), ()))
    TN = (((0,), (0,)), ((), ()))

    def body(dy_ref, ut_ref, dp_in, sr_ref, si_ref, lr_ref, li_ref, b_ref, c_ref, d_ref,
             dut_ref, db_ref, dc_ref, dl_ref, dd_ref, gr_ref, gi_ref, u_ref, du_ref):
        k = pl.program_id(0)
        _scan_order_into(u_ref, ut_ref, T)
        for r in range(L // rb):
            rows = pl.ds(r * rb, rb)
            g = lax.dot_general(dy_ref[rows, :].astype(BF16), c_ref[...], NT, preferred_element_type=F32)
            gr_ref[rows, :] = g[:, :W]
            gi_ref[rows, :] = g[:, W:]
        lam_r = jnp.broadcast_to(lr_ref[...], (SUBLANES, W))
        lam_i = -jnp.broadcast_to(li_ref[...], (SUBLANES, W))

        def local(i, carry):
            gr, gi = carry
            rows = pl.ds(pl.multiple_of((T - 1 - i) * SUBLANES, SUBLANES), SUBLANES)
            mr, mi = _cmul(lam_r, lam_i, gr, gi)
            gr = mr + gr_ref[rows, :]
            gi = mi + gi_ref[rows, :]
            gr_ref[rows, :] = gr
            gi_ref[rows, :] = gi
            return gr, gi

        z = jnp.zeros((SUBLANES, W), F32)
        fr, fi = lax.fori_loop(0, T, local, (z, z))
        pr, pi = _cpow(lam_r, lam_i, T)
        cr, ci = _carry_tile(fr, fi, pr[0:1], pi[0:1], reverse=True)

        def true_g(rows, wr, wi):
            ar, ai = _cmul(wr, wi, cr, ci)
            gr = gr_ref[rows, :] + ar
            gi = gi_ref[rows, :] + ai
            gr_ref[rows, :] = gr
            gi_ref[rows, :] = gi
            return gr, gi

        def fix(i, carry):
            wr, wi, ar_, ai_ = carry
            t = T - 1 - i
            rows = pl.ds(pl.multiple_of(t * SUBLANES, SUBLANES), SUBLANES)
            prev = pl.ds(pl.multiple_of((t - 1) * SUBLANES, SUBLANES), SUBLANES)
            gr, gi = true_g(rows, wr, wi)
            qr, qi = sr_ref[prev, :], si_ref[prev, :]
            ar_ = ar_ + gr * qr + gi * qi
            ai_ = ai_ + gi * qr - gr * qi
            wr, wi = _cmul(wr, wi, lam_r, lam_i)
            return wr, wi, ar_, ai_

        wr, wi, acc_r, acc_i = lax.fori_loop(0, T - 1, fix, (lam_r, lam_i, z, z))
        gr, gi = true_g(pl.ds(0, SUBLANES), wr, wi)
        last = pl.ds((T - 1) * SUBLANES, SUBLANES)
        row = lax.broadcasted_iota(jnp.int32, (SUBLANES, W), 0)
        qr = jnp.where(row >= 1, pltpu.roll(sr_ref[last, :], 1, axis=0), 0.0)
        qi = jnp.where(row >= 1, pltpu.roll(si_ref[last, :], 1, axis=0), 0.0)
        acc_r = acc_r + gr * qr + gi * qi
        acc_i = acc_i + gi * qr - gr * qi
        dl_ref[0:1, :] = jnp.sum(acc_r, axis=0, keepdims=True)
        dl_ref[1:2, :] = jnp.sum(acc_i, axis=0, keepdims=True)

        first = (k % per) == 0
        db = jnp.zeros((LANES, 2 * W), F32)
        dc = jnp.zeros((LANES, 2 * W), F32)
        dd = jnp.zeros((1, LANES), F32)
        for r in range(L // rb):
            rows = pl.ds(r * rb, rb)
            gb = jnp.concatenate([gr_ref[rows, :], gi_ref[rows, :]], axis=1).astype(BF16)
            sb = jnp.concatenate([sr_ref[rows, :], si_ref[rows, :]], axis=1).astype(BF16)
            dyv = dy_ref[rows, :]
            uv = u_ref[rows, :]
            du = lax.dot_general(gb, b_ref[...], NT, preferred_element_type=F32)
            db = db + lax.dot_general(uv.astype(BF16), gb, TN, preferred_element_type=F32)
            dc = dc + lax.dot_general(dyv.astype(BF16), sb, TN, preferred_element_type=F32)
            dd = dd + jnp.sum(dyv * uv, axis=0, keepdims=True)

            @pl.when(first)
            def _():
                du_ref[rows, :] = du + d_ref[...] * dyv

            @pl.when(jnp.logical_not(first))
            def _():
                du_ref[rows, :] += du

        db_ref[...] = db
        dc_ref[...] = dc

        @pl.when(first)
        def _():
            dd_ref[...] = dd

        @pl.when((k % per) == per - 1)
        def _():
            for c in range(SCAN_CHUNKS):
                dut_ref[pl.ds(c * T, T), :] = du_ref[pl.ds(c, T, stride=SCAN_CHUNKS), :].astype(BF16)

    ublk = pl.BlockSpec((L, LANES), lambda k: (0, k // per))
    uslab = pl.BlockSpec((None, L, LANES), lambda k: (3, 0, k // per))
    sblk = pl.BlockSpec((L, W), lambda k: (0, k))
    lam = pl.BlockSpec((None, 1, W), lambda k: (k, 0, 0))
    vec = pl.BlockSpec((1, LANES), lambda k: (0, k // per))
    mat = pl.BlockSpec((None, LANES, 2 * W), lambda k: (k, 0, 0))
    return pl.pallas_call(
        body, name=name,
        out_shape=(jax.ShapeDtypeStruct(dproj.shape, dproj.dtype), jax.ShapeDtypeStruct((ns, LANES, 2 * W), F32),
                   jax.ShapeDtypeStruct((ns, LANES, 2 * W), F32), jax.ShapeDtypeStruct((ns, 2, W), F32),
                   jax.ShapeDtypeStruct((1, Du), F32)),
        grid=(ns,),
        in_specs=[ublk, uslab, pl.BlockSpec(memory_space=pl.ANY), sblk, sblk, lam, lam, mat,
                  pl.BlockSpec((None, 2 * W, LANES), lambda k: (k, 0, 0)), vec],
        out_specs=(uslab, mat, mat, pl.BlockSpec((None, 2, W), lambda k: (k, 0, 0)), vec),
        scratch_shapes=[pltpu.VMEM((L, W), F32), pltpu.VMEM((L, W), F32), pltpu.VMEM((L, LANES), F32),
                        pltpu.VMEM((L, LANES), F32)],
        input_output_aliases={2: 0}, compiler_params=_cparams(("arbitrary",), VMEM_LIMIT_S5),
    )(dy, proj4, dproj, s_re, s_im, lr, li, bmat.astype(BF16), cmat.astype(BF16), d.reshape(1, Du))


def _glu_fwd(yraw, wmat, bias, mixin, name):
    L, C = yraw.shape
    tr = _pick(L, prefs=(512, 256, 128))
    tb = tr // SCAN_CHUNKS
    nl = C // LANES

    def body(y_ref, w_ref, b_ref, m_in, o_ref, scr):
        yg = _gelu(y_ref[...])
        zz = jnp.dot(yg.astype(BF16), w_ref[...], preferred_element_type=F32) + b_ref[...]
        yb = yg * _sigmoid(zz)
        for k in range(nl):
            scr[k] = yb[:, k * LANES:(k + 1) * LANES]
        for c in range(SCAN_CHUNKS):
            for k in range(nl):
                o_ref[c, :, k * LANES:(k + 1) * LANES] = scr[k, pl.ds(c, tb, stride=SCAN_CHUNKS), :].astype(BF16)

    out = pl.pallas_call(
        body, name=name, out_shape=jax.ShapeDtypeStruct((SCAN_CHUNKS, L // SCAN_CHUNKS, 2 * C), BF16),
        grid=(L // tr,),
        in_specs=[pl.BlockSpec((tr, C), lambda i: (i, 0)), pl.BlockSpec((C, C), lambda i: (0, 0)),
                  pl.BlockSpec((1, C), lambda i: (0, 0)), pl.BlockSpec(memory_space=pl.ANY)],
        out_specs=pl.BlockSpec((SCAN_CHUNKS, tb, C), lambda i: (0, i, 1)),
        scratch_shapes=[pltpu.VMEM((nl, tr, LANES), F32)], input_output_aliases={3: 0},
        compiler_params=_cparams(("parallel",)),
    )(yraw, wmat, bias.reshape(1, C), mixin.reshape(SCAN_CHUNKS, L // SCAN_CHUNKS, 2 * C))
    return out.reshape(L, 2 * C)


def _glu_bwd(yraw, dmix, wmat, bias, name):
    L, C = yraw.shape
    tr = _pick(L, prefs=(512, 256, 128))
    nsteps = L // tr
    tb = tr // SCAN_CHUNKS
    nl = C // LANES

    def body(y_ref, d_ref, w_ref, b_ref, dy_ref, dw_ref, db_ref, acc_b, scr):
        i = pl.program_id(0)

        @pl.when(i == 0)
        def _():
            dw_ref[...] = jnp.zeros_like(dw_ref)
            acc_b[...] = jnp.zeros_like(acc_b)

        for c in range(SCAN_CHUNKS):
            for k in range(nl):
                scr[k, pl.ds(c, tb, stride=SCAN_CHUNKS), :] = d_ref[c, :, k * LANES:(k + 1) * LANES]
        yr = y_ref[...]
        yg = _gelu(yr)
        ygb = yg.astype(BF16)
        sg = _sigmoid(jnp.dot(ygb, w_ref[...], preferred_element_type=F32) + b_ref[...])
        dyb_ = jnp.concatenate([scr[k] for k in range(nl)], axis=1)
        dz = dyb_ * yg * sg * (1.0 - sg)
        dzb = dz.astype(BF16)
        dyg = dyb_ * sg + lax.dot_general(dzb, w_ref[...], (((1,), (1,)), ((), ())), preferred_element_type=F32)
        dw_ref[...] += lax.dot_general(ygb, dzb, (((0,), (0,)), ((), ())), preferred_element_type=F32)
        acc_b[...] += jnp.sum(dz.reshape(tr // SUBLANES, SUBLANES, C), axis=0)
        dy_ref[...] = dyg * _gelu_grad(yr)

        @pl.when(i == nsteps - 1)
        def _():
            db_ref[...] = jnp.sum(acc_b[...], axis=0, keepdims=True)

    row = pl.BlockSpec((tr, C), lambda i: (i, 0))
    return pl.pallas_call(
        body, name=name,
        out_shape=(jax.ShapeDtypeStruct((L, C), F32), jax.ShapeDtypeStruct((C, C), F32),
                   jax.ShapeDtypeStruct((1, C), F32)),
        grid=(nsteps,),
        in_specs=[row, pl.BlockSpec((SCAN_CHUNKS, tb, C), lambda i: (0, i, 1)), pl.BlockSpec((C, C), lambda i: (0, 0)),
                  pl.BlockSpec((1, C), lambda i: (0, 0))],
        out_specs=(row, pl.BlockSpec((C, C), lambda i: (0, 0)), pl.BlockSpec((1, C), lambda i: (0, 0))),
        scratch_shapes=[pltpu.VMEM((SUBLANES, C), F32), pltpu.VMEM((nl, tr, LANES), F32)],
        compiler_params=_cparams(("arbitrary",)),
    )(yraw, dmix.reshape(SCAN_CHUNKS, L // SCAN_CHUNKS, 2 * C), wmat, bias.reshape(1, C))


def _pool_counts(L, g):
    t = lax.broadcasted_iota(jnp.int32, (L, LANES), 0).astype(F32) + 1.0
    w = jnp.where(g == 0, 2.0, jnp.where(g == 1, 4.0, jnp.where(g == 2, 8.0, 16.0)))
    return 1.0 / jnp.minimum(t, w)


def _select_window(g, a2, a4, a8, a16):
    return jnp.where(g == 0, a2, jnp.where(g == 1, a4, jnp.where(g == 2, a8, a16)))


def _pooled(z, g):
    a2 = z + _down(z, 1)
    a4 = a2 + _down(a2, 2)
    a8 = a4 + _down(a4, 4)
    a16 = a8 + _down(a8, 8)
    return _select_window(g, a2, a4, a8, a16) * _pool_counts(z.shape[0], g) - z


def _transpose_on_mxu(yb):
    c = yb.shape[1]
    eye = lax.broadcasted_iota(jnp.int32, (c, c), 0) == lax.broadcasted_iota(jnp.int32, (c, c), 1)
    return lax.dot_general(eye.astype(BF16), yb, (((1,), (1,)), ((), ())), preferred_element_type=F32).astype(BF16)


def _pool_fwd(proj3, pool_w, scale, name):
    _, L, C = proj3.shape
    ng = len(POOL_WINDOWS)
    pg = C // ng
    assert pg == LANES

    def body(z_ref, w_ref, s_ref, o_ref, ot_ref):
        g = pl.program_id(0)
        p = _pooled(z_ref[...].astype(F32), g)
        y = jnp.dot(p.astype(BF16), w_ref[...].astype(BF16), preferred_element_type=F32)
        yb = (y * s_ref[...]).astype(BF16)
        o_ref[...] = yb
        ot_ref[...] = _transpose_on_mxu(yb)

    return pl.pallas_call(
        body, name=name, out_shape=(jax.ShapeDtypeStruct((L, 2 * C), BF16), jax.ShapeDtypeStruct((2 * C, L), BF16)),
        grid=(ng,),
        in_specs=[pl.BlockSpec((None, L, pg), lambda g: (0, 0, g)), pl.BlockSpec((None, pg, pg), lambda g: (g, 0, 0)),
                  pl.BlockSpec((1, pg), lambda g: (0, g))],
        out_specs=(pl.BlockSpec((L, pg), lambda g: (0, g)), pl.BlockSpec((pg, L), lambda g: (g, 0))),
        compiler_params=_cparams(("parallel",)),
    )(proj3, pool_w, scale.reshape(1, C))


def _pool_bwd(proj3, dmix, pool_w, scale, name):
    _, L, C = proj3.shape
    ng = len(POOL_WINDOWS)
    pg = C // ng

    def body(z_ref, d_ref, w_ref, s_ref, dz_ref, dw_ref, ds_ref):
        g = pl.program_id(0)
        p = _pooled(z_ref[...].astype(F32), g)
        pb = p.astype(BF16)
        wb = w_ref[...].astype(BF16)
        pre = jnp.dot(pb, wb, preferred_element_type=F32)
        dyc = d_ref[...]
        ds_ref[...] = jnp.sum(dyc * pre, axis=0, keepdims=True)
        dpre = (dyc * s_ref[...]).astype(BF16)
        dw_ref[...] = lax.dot_general(pb, dpre, (((0,), (0,)), ((), ())), preferred_element_type=F32)
        dp = lax.dot_general(dpre, wb, (((1,), (1,)), ((), ())), preferred_element_type=F32)
        v = dp * _pool_counts(L, g)
        a2 = v + _up(v, 1)
        a4 = a2 + _up(a2, 2)
        a8 = a4 + _up(a4, 4)
        a16 = a8 + _up(a8, 8)
        dz_ref[...] = (_select_window(g, a2, a4, a8, a16) - dp).astype(BF16)

    return pl.pallas_call(
        body, name=name,
        out_shape=(jax.ShapeDtypeStruct((L, C), BF16), jax.ShapeDtypeStruct((ng, pg, pg), F32),
                   jax.ShapeDtypeStruct((1, C), F32)),
        grid=(ng,),
        in_specs=[pl.BlockSpec((None, L, pg), lambda g: (0, 0, g)), pl.BlockSpec((L, pg), lambda g: (0, g)),
                  pl.BlockSpec((None, pg, pg), lambda g: (g, 0, 0)), pl.BlockSpec((1, pg), lambda g: (0, g))],
        out_specs=(pl.BlockSpec((L, pg), lambda g: (0, g)), pl.BlockSpec((None, pg, pg), lambda g: (g, 0, 0)),
                   pl.BlockSpec((1, pg), lambda g: (0, g))),
        compiler_params=_cparams(("parallel",)),
    )(proj3, dmix, pool_w, scale.reshape(1, C))


def _tril_w(w_ref, h):
    r = lax.broadcasted_iota(jnp.int32, (CHUNK, CHUNK), 0)
    c = lax.broadcasted_iota(jnp.int32, (CHUNK, CHUNK), 1)
    return jnp.where(r >= c, w_ref[h], 0.0)


def _sgu_fwd(proj3, norm_g, w, b, mixin, mixin_t, name):
    _, L, C = proj3.shape
    nh = w.shape[0]
    dh = C // nh
    assert dh == LANES and w.shape[1] == CHUNK
    tr = _pick(L, prefs=(512, 256, 128))
    bfull = jnp.broadcast_to(b[:, :, None], (nh, CHUNK, dh))

    def body(su_ref, sv_ref, g_ref, w_ref, b_ref, m_in, mt_in, o_ref, ot_ref):
        sv = _gelu(sv_ref[...].astype(F32))
        r = lax.rsqrt(jnp.mean(sv * sv, axis=-1, keepdims=True) + EPS)
        v = (sv * r * g_ref[...]).astype(BF16)
        for h in range(nh):
            wm = _tril_w(w_ref, h).astype(BF16)
            cols = slice(h * dh, (h + 1) * dh)
            for n in range(tr // CHUNK):
                rows = slice(n * CHUNK, (n + 1) * CHUNK)
                mixed = jnp.dot(wm, v[rows, cols], preferred_element_type=F32) + b_ref[h]
                o_ref[rows, cols] = (_gelu(su_ref[rows, cols].astype(F32)) * mixed).astype(BF16)
        ot_ref[...] = _transpose_on_mxu(o_ref[...])

    full = lambda shp: pl.BlockSpec(shp, lambda i: (0,) * len(shp))
    anywhere = pl.BlockSpec(memory_space=pl.ANY)
    return pl.pallas_call(
        body, name=name, out_shape=(jax.ShapeDtypeStruct(mixin.shape, BF16), jax.ShapeDtypeStruct(mixin_t.shape, BF16)),
        grid=(L // tr,),
        in_specs=[pl.BlockSpec((None, tr, C), lambda i: (1, i, 0)), pl.BlockSpec((None, tr, C), lambda i: (2, i, 0)),
                  full((1, C)), full((nh, CHUNK, CHUNK)), full((nh, CHUNK, dh)), anywhere, anywhere],
        out_specs=(pl.BlockSpec((tr, C), lambda i: (i, 1)), pl.BlockSpec((C, tr), lambda i: (1, i))),
        input_output_aliases={5: 0, 6: 1}, compiler_params=_cparams(("parallel",)),
    )(proj3, proj3, norm_g.reshape(1, C), w, bfull, mixin, mixin_t)


def _sgu_bwd(proj3, dmix, dz, norm_g, w, b, name):
    _, L, C = proj3.shape
    nh = w.shape[0]
    dh = C // nh
    tr = _pick(L, prefs=(512, 256, 128))
    nsteps = L // tr
    bfull = jnp.broadcast_to(b[:, :, None], (nh, CHUNK, dh))

    def body(su_ref, sv_ref, d_ref, dz_ref, g_ref, w_ref, b_ref, o_ref, dw_ref, db_ref, dg_ref, dv_ref, acc_g):
        i = pl.program_id(0)
        o_ref[0] = dz_ref[...]

        @pl.when(i == 0)
        def _():
            dw_ref[...] = jnp.zeros_like(dw_ref)
            db_ref[...] = jnp.zeros_like(db_ref)
            acc_g[...] = jnp.zeros_like(acc_g)

        svp = sv_ref[...].astype(F32)
        sv = _gelu(svp)
        r = lax.rsqrt(jnp.mean(sv * sv, axis=-1, keepdims=True) + EPS)
        vh = sv * r
        gv = g_ref[...]
        v = (vh * gv).astype(BF16)
        tri_r = lax.broadcasted_iota(jnp.int32, (CHUNK, CHUNK), 0)
        tri_c = lax.broadcasted_iota(jnp.int32, (CHUNK, CHUNK), 1)
        for h in range(nh):
            wm = _tril_w(w_ref, h).astype(BF16)
            cols = slice(h * dh, (h + 1) * dh)
            dwh = jnp.zeros((CHUNK, CHUNK), F32)
            dbh = jnp.zeros((CHUNK, dh), F32)
            for n in range(tr // CHUNK):
                rows = slice(n * CHUNK, (n + 1) * CHUNK)
                vb = v[rows, cols]
                mixed = jnp.dot(wm, vb, preferred_element_type=F32) + b_ref[h]
                sup = su_ref[rows, cols].astype(F32)
                dyd = d_ref[rows, cols]
                dmx = dyd * _gelu(sup)
                o_ref[1, rows, cols] = (dyd * mixed * _gelu_grad(sup)).astype(BF16)
                dmb = dmx.astype(BF16)
                dwh = dwh + lax.dot_general(dmb, vb, (((1,), (1,)), ((), ())), preferred_element_type=F32)
                dbh = dbh + dmx
                dv_ref[rows, cols] = lax.dot_general(wm, dmb, (((0,), (0,)), ((), ())), preferred_element_type=F32)
            dw_ref[h] += jnp.where(tri_r >= tri_c, dwh, 0.0)
            db_ref[h] += dbh
        dv = dv_ref[...]
        acc_g[...] += jnp.sum((dv * vh).reshape(tr // SUBLANES, SUBLANES, C), axis=0)
        dvg = dv * gv
        dsv = r * (dvg - vh * jnp.mean(dvg * vh, axis=-1, keepdims=True))
        o_ref[2] = (dsv * _gelu_grad(svp)).astype(BF16)

        @pl.when(i == nsteps - 1)
        def _():
            dg_ref[...] = jnp.sum(acc_g[...], axis=0, keepdims=True)

    full = lambda shp: pl.BlockSpec(shp, lambda i: (0,) * len(shp))
    return pl.pallas_call(
        body, name=name,
        out_shape=(jax.ShapeDtypeStruct((3, L, C), BF16), jax.ShapeDtypeStruct((nh, CHUNK, CHUNK), F32),
                   jax.ShapeDtypeStruct((nh, CHUNK, dh), F32), jax.ShapeDtypeStruct((1, C), F32)),
        grid=(nsteps,),
        in_specs=[pl.BlockSpec((None, tr, C), lambda i: (1, i, 0)), pl.BlockSpec((None, tr, C), lambda i: (2, i, 0)),
                  pl.BlockSpec((tr, C), lambda i: (i, 1)), pl.BlockSpec((tr, C), lambda i: (i, 0)), full((1, C)),
                  full((nh, CHUNK, CHUNK)), full((nh, CHUNK, dh))],
        out_specs=(pl.BlockSpec((3, tr, C), lambda i: (0, i, 0)), full((nh, CHUNK, CHUNK)), full((nh, CHUNK, dh)),
                   full((1, C))),
        scratch_shapes=[pltpu.VMEM((tr, C), F32), pltpu.VMEM((SUBLANES, C), F32)],
        compiler_params=_cparams(("arbitrary",)),
    )(proj3, proj3, dmix, dz, norm_g.reshape(1, C), w, bfull)


def _ffn_act_fwd(up3, conv_w, conv_b, name):
    _, L, Fh = up3.shape
    cb = LANES
    w2 = conv_w.reshape(3, 2, Fh).transpose(1, 0, 2)
    b2 = conv_b.reshape(2, 1, Fh)

    def body(u_ref, w_ref, b_ref, o_ref, ot_ref, gv_ref):
        g = _conv3(_taps(u_ref[0].astype(F32)), w_ref[0]) + b_ref[0]
        v = _conv3(_taps(u_ref[1].astype(F32)), w_ref[1]) + b_ref[1]
        gv_ref[0] = g.astype(BF16)
        gv_ref[1] = v.astype(BF16)
        ab = (g * _sigmoid(g) * v).astype(BF16)
        o_ref[...] = ab
        ot_ref[...] = _transpose_on_mxu(ab)

    blk3 = pl.BlockSpec((2, L, cb), lambda j: (0, 0, j))
    return pl.pallas_call(
        body, name=name,
        out_shape=(jax.ShapeDtypeStruct((L, Fh), BF16), jax.ShapeDtypeStruct((Fh, L), BF16),
                   jax.ShapeDtypeStruct((2, L, Fh), BF16)),
        grid=(Fh // cb,),
        in_specs=[blk3, pl.BlockSpec((2, 3, cb), lambda j: (0, 0, j)), pl.BlockSpec((2, 1, cb), lambda j: (0, 0, j))],
        out_specs=(pl.BlockSpec((L, cb), lambda j: (0, j)), pl.BlockSpec((cb, L), lambda j: (j, 0)), blk3),
        compiler_params=_cparams(("parallel",)),
    )(up3, w2, b2)


def _ffn_act_bwd(up3, gv3, da, conv_w, h2t, name):
    _, L, Fh = up3.shape
    D = h2t.shape[0]
    cb = LANES
    nb = Fh // cb
    w2 = conv_w.reshape(3, 2, Fh).transpose(1, 0, 2)

    def body(u_ref, gv_ref, d_ref, w_ref, h_ref, o_ref, dw_ref, db_ref, wg_ref, wv_ref, scr):
        j = pl.program_id(0)

        @pl.when(j == 0)
        def _():
            scr[1] = jnp.zeros((2, L, cb), BF16)

        prev = scr.at[(j + 1) % 2]
        wg_ref[...] = jnp.dot(h_ref[...], prev[0], preferred_element_type=F32).astype(BF16)
        wv_ref[...] = jnp.dot(h_ref[...], prev[1], preferred_element_type=F32).astype(BF16)
        tg, tv = _taps(u_ref[0].astype(F32)), _taps(u_ref[1].astype(F32))
        wg, wv = w_ref[0], w_ref[1]
        g = gv_ref[0].astype(F32)
        v = gv_ref[1].astype(F32)
        sg = _sigmoid(g)
        dav = d_ref[...].astype(F32)
        dg = dav * v * (sg * (1.0 + g * (1.0 - sg)))
        dv = dav * (g * sg)
        dug = _conv3_t(dg, wg).astype(BF16)
        duv = _conv3_t(dv, wv).astype(BF16)
        o_ref[0] = dug
        o_ref[1] = duv
        cur = scr.at[j % 2]
        cur[0] = dug
        cur[1] = duv
        for tap, (dwg, dwv) in enumerate(zip(_conv3_dw(dg, tg), _conv3_dw(dv, tv))):
            dw_ref[0, tap:tap + 1, :] = dwg
            dw_ref[1, tap:tap + 1, :] = dwv
        db_ref[0] = jnp.sum(dg, axis=0, keepdims=True)
        db_ref[1] = jnp.sum(dv, axis=0, keepdims=True)

    here = lambda j: jnp.minimum(j, nb - 1)
    before = lambda j: jnp.maximum(j - 1, 0)
    blk3 = pl.BlockSpec((2, L, cb), lambda j: (0, 0, here(j)))
    dup, dw2, db2, dwg, dwv = pl.pallas_call(
        body, name=name,
        out_shape=(jax.ShapeDtypeStruct((2, L, Fh), BF16), jax.ShapeDtypeStruct((2, 3, Fh), F32),
                   jax.ShapeDtypeStruct((2, 1, Fh), F32), jax.ShapeDtypeStruct((D, Fh), BF16),
                   jax.ShapeDtypeStruct((D, Fh), BF16)),
        grid=(nb + 1,),
        in_specs=[blk3, blk3, pl.BlockSpec((L, cb), lambda j: (0, here(j))),
                  pl.BlockSpec((2, 3, cb), lambda j: (0, 0, here(j))), pl.BlockSpec((D, L), lambda j: (0, 0))],
        out_specs=(blk3, pl.BlockSpec((2, 3, cb), lambda j: (0, 0, here(j))),
                   pl.BlockSpec((2, 1, cb), lambda j: (0, 0, here(j))),
                   pl.BlockSpec((D, cb), lambda j: (0, before(j))), pl.BlockSpec((D, cb), lambda j: (0, before(j)))),
        scratch_shapes=[pltpu.VMEM((2, 2, L, cb), BF16)],
        compiler_params=_cparams(("arbitrary",), VMEM_LIMIT_S5),
    )(up3, gv3, da, w2, h2t)
    return dup, dw2.transpose(1, 0, 2).reshape(3, 2 * Fh), db2.reshape(2 * Fh), jnp.concatenate([dwg, dwv], axis=1)


def _local_step(x, tgt, w, layer_weights, on_layer_grads):
    L, D = x.shape
    depth = w['norm_mix_g'].shape[0]
    saved = []
    for i in range(depth):
        j = i // 2
        wb = dict(layer_weights(2 * i, x))
        s = {'x': x, 'wb': wb}
        if i % 2 == 0:
            proj4, s['hT'] = _norm_mm(x, w['norm_mix_g'][i], wb['even_w_in'], BF16, "even_in_fwd", ok=('seg', 4))
            s['proj'] = proj4
            mixin = _sconv_fwd(proj4, w['even_conv_w'][j], "sconv_fwd")
            prm = (w['ssm_log_step'][j], w['ssm_a_re'][j], w['ssm_a_im'][j], w['ssm_b_re'][j], w['ssm_b_im'][j],
                   w['ssm_c_re'][j], w['ssm_c_im'][j])
            (lr, li, bmat, cmat), prep_vjp = jax.vjp(_s5_prep, *prm)
            yraw, s_re, s_im = _s5_fwd(proj4, lr, li, bmat, cmat, w['ssm_d'][j], "s5_fwd")
            mixin = _glu_fwd(yraw, wb['ssm_glu_w'], w['ssm_glu_b'][j], mixin, "glu_fwd")
            s.update(yraw=yraw, s_re=s_re, s_im=s_im, s5=(lr, li, bmat, cmat), prep_vjp=prep_vjp)
            s['mixinT'] = mixin.T
            x = _mm(mixin, wb['even_w_out'], 'nn', F32, "even_out_fwd", res=x)
        else:
            proj3, s['hT'] = _norm_mm(x, w['norm_mix_g'][i], wb['odd_w_in'], BF16, "odd_in_fwd", ok=('seg', 3))
            s['proj'] = proj3
            mixin, mixin_t = _pool_fwd(proj3, w['pool_w'][j], w['pool_scale'][j], "pool_fwd")
            mixin, s['mixinT'] = _sgu_fwd(proj3, w['sgu_norm_g'][j], w['sgu_w'][j], w['sgu_b'][j], mixin, mixin_t,
                                          "sgu_fwd")
            x = _mm(mixin, wb['odd_w_out'], 'nn', F32, "odd_out_fwd", res=x)
        s['x1'] = x
        wb.update(layer_weights(2 * i + 1, x))
        up3, h2t = _norm_mm(x, w['norm_ffn_g'][i], wb['ffn_w_up'], BF16, "ffn_up_fwd", ok=('seg', 2))
        a, at, gv3 = _ffn_act_fwd(up3, w['ffn_conv_w'][i], w['ffn_conv_b'][i], "ffn_act_fwd")
        x = _mm(a, wb['ffn_w_down'], 'nn', F32, "ffn_down_fwd", res=x)
        s.update(h2T=h2t, up3=up3, aT=at, gv3=gv3)
        saved.append(s)

    loss8, dx, dxb, dg_final = _loss_head(x, w['norm_final_g'], tgt)
    gs = {n: [None] * w[n].shape[0] for n in SMALL if n != 'norm_final_g'}
    gs['norm_final_g'] = dg_final.reshape(D)

    dep = None
    for i in reversed(range(depth)):
        j = i // 2
        s = saved[i]
        wb = s['wb']
        gb = {}
        da = _mm(dxb, wb['ffn_w_down'], 'nt', BF16, "ffn_down_dgrad", dep=dep)
        gb['ffn_w_down'] = _mm(s['aT'], dxb, 'nn', BF16, "ffn_down_wgrad")
        dup3, dcw, dcb, gb['ffn_w_up'] = _ffn_act_bwd(s['up3'], s['gv3'], da, w['ffn_conv_w'][i], s['h2T'],
                                                      "ffn_act_bwd")
        gs['ffn_conv_w'][i], gs['ffn_conv_b'][i] = dcw, dcb
        dep = on_layer_grads(2 * i + 1, gb)
        dx, dxb, dg = _mm_norm_bwd(dup3, wb['ffn_w_up'], s['x1'], w['norm_ffn_g'][i], dx, "ffn_up_dgrad",
                              ak=('seg', 2), dep=dep)
        gs['norm_ffn_g'][i] = dg.reshape(D)
        gb = {}
        if i % 2 == 0:
            dmix = _mm(dxb, wb['even_w_out'], 'nt', F32, "even_out_dgrad")
            gb['even_w_out'] = _mm(s['mixinT'], dxb, 'nn', BF16, "even_out_wgrad")
            dproj, dcw = _sconv_bwd(s['proj'], dmix, w['even_conv_w'][j], "sconv_bwd")
            gs['even_conv_w'][j] = dcw
            dyraw, dglu_w, dglu_b = _glu_bwd(s['yraw'], dmix, wb['ssm_glu_w'], w['ssm_glu_b'][j], "glu_bwd")
            gb['ssm_glu_w'] = dglu_w.astype(BF16)
            gs['ssm_glu_b'][j] = dglu_b.reshape(-1)
            lr, li, bmat, cmat = s['s5']
            dproj, dbm, dcm, dlam, dd = _s5_bwd(dyraw, s['proj'], dproj, s['s_re'], s['s_im'], lr, li, bmat, cmat,
                                               w['ssm_d'][j], "s5_bwd")
            gs['ssm_d'][j] = dd.reshape(-1)
            dcm = jnp.swapaxes(dcm, 1, 2)
            dprm = s['prep_vjp']((dlam[:, 0:1, :], dlam[:, 1:2, :], dbm, dcm))
            for n, gval in zip(('ssm_log_step', 'ssm_a_re', 'ssm_a_im', 'ssm_b_re', 'ssm_b_im', 'ssm_c_re',
                                'ssm_c_im'), dprm):
                gs[n][j] = gval
            gb['even_w_in'] = _mm(s['hT'], dproj, 'nn', BF16, "even_in_wgrad", bk=('seg', 4))
            w_in, in_kind, in_name = wb['even_w_in'], ('seg', 4), "even_in_dgrad"
        else:
            dmix = _mm(dxb, wb['odd_w_out'], 'nt', F32, "odd_out_dgrad")
            gb['odd_w_out'] = _mm(s['mixinT'], dxb, 'nn', BF16, "odd_out_wgrad")
            dz, dpw, dps = _pool_bwd(s['proj'], dmix, w['pool_w'][j], w['pool_scale'][j], "pool_bwd")
            gs['pool_w'][j], gs['pool_scale'][j] = dpw, dps.reshape(-1)
            dproj, dsw, dsb, dsg = _sgu_bwd(s['proj'], dmix, dz, w['sgu_norm_g'][j], w['sgu_w'][j], w['sgu_b'][j],
                                            "sgu_bwd")
            gs['sgu_w'][j], gs['sgu_b'][j], gs['sgu_norm_g'][j] = dsw, jnp.sum(dsb, axis=-1), dsg.reshape(-1)
            gb['odd_w_in'] = _mm(s['hT'], dproj, 'nn', BF16, "odd_in_wgrad", bk=('seg', 3))
            w_in, in_kind, in_name = wb['odd_w_in'], ('seg', 3), "odd_in_dgrad"
        dep = on_layer_grads(2 * i, gb)
        dx, dxb, dg = _mm_norm_bwd(dproj, w_in, s['x'], w['norm_mix_g'][i], dx, in_name, ak=in_kind, dep=dep)
        gs['norm_mix_g'][i] = dg.reshape(D)

    gsmall = {n: (v if n == 'norm_final_g' else jnp.stack(v)) for n, v in gs.items()}
    return loss8[0, 0], dx, gsmall


_HBM = pl.BlockSpec(memory_space=pltpu.HBM)
_CHIP_FLIPS = ((0, 0), (1, 0), (0, 1), (1, 1))


def _coords():
    return lax.axis_index("x"), lax.axis_index("y"), lax.axis_index("c")


def _flip(v, f):
    return 1 - v if f else v


def _shard_of(ref, axis, s, width):
    start = pl.multiple_of(s * width, LANES if axis == ref.ndim - 1 else 16) if width % 16 == 0 else s * width
    idx = [slice(None)] * ref.ndim
    idx[axis] = pl.ds(start, width)
    return ref.at[tuple(idx)]


_SEM = pl.BlockSpec(memory_space=pltpu.SEMAPHORE)
_ANY = pl.BlockSpec(memory_space=pl.ANY)
_DATAFLOW = pltpu.SideEffectType.DATAFLOW_SIDE_EFFECTING


def _in_hbm(a):
    return pltpu.with_memory_space_constraint(a, pltpu.HBM)


def _model_layer(name, l):
    if name.startswith('ffn'):
        return l
    return 2 * l + 1 if name.startswith('odd') else 2 * l


def _place_quarter(shard, l, axis, chip, dtype, dep=None):
    _, r, c = shard.shape
    tr = _pick(r, prefs=(512, 256, 128, 64, 32, 16))
    nrb = r // tr

    def body(chip_ref, i_ref, *rest):
        rest[-1][...] = i_ref[...].astype(dtype)

    if axis == 1:
        out_shape, o_map = (r, c * N_CHIPS), (lambda i, s: (i, s[0]))
    else:
        out_shape, o_map = (r * N_CHIPS, c), (lambda i, s: (s[0] * nrb + i, 0))
    in_specs = [pl.BlockSpec((None, tr, c), lambda i, s: (l, i, 0))]
    args = [chip, shard]
    if dep is not None:
        in_specs.append(pl.BlockSpec(memory_space=pl.ANY))
        args.append(dep)
    return pl.pallas_call(
        body, name="place_quarter", out_shape=jax.ShapeDtypeStruct(out_shape, dtype),
        grid_spec=pltpu.PrefetchScalarGridSpec(
            num_scalar_prefetch=1, grid=(nrb,), in_specs=in_specs, out_specs=pl.BlockSpec((tr, c), o_map)),
        compiler_params=_cparams(("parallel",)),
    )(*args)


def _gather_copies(land_refs, send_sem, recv_sem, axes, landing_chip_of):
    x, y, c = _coords()
    out = []
    for j, land in enumerate(land_refs):
        width = land.shape[axes[j]] // N_CHIPS
        for f in (1, 2, 3):
            fx, fy = _CHIP_FLIPS[f]
            px, py = _flip(x, fx), _flip(y, fy)
            lx, ly = landing_chip_of(px, py)
            out.append(pltpu.make_async_remote_copy(
                src_ref=_shard_of(land, axes[j], 2 * x + y, width), dst_ref=_shard_of(land, axes[j], 2 * lx + ly, width),
                send_sem=send_sem.at[3 * j + f - 1], recv_sem=recv_sem.at[3 * j + f - 1],
                device_id=(px, py, c), device_id_type=MESH))
    return out


def _gather_start(tag, lands, axes, dep=None):
    n = len(lands)

    def body(*refs):
        land_refs, send_sem, recv_sem = refs[:n], refs[-3], refs[-2]
        x, y, _ = _coords()
        for cp in _gather_copies(land_refs, send_sem, recv_sem, axes, lambda px, py: (x, y)):
            cp.start()
        refs[-1][...] = jnp.zeros_like(refs[-1])

    thru = [pltpu.HBM(a.shape, a.dtype) for a in lands]
    outs = pl.pallas_call(
        body, name=f"gather_start_{tag}",
        out_shape=tuple(thru + [pltpu.SemaphoreType.DMA((3 * n,)), pltpu.SemaphoreType.DMA((3 * n,)),
                                jax.ShapeDtypeStruct((SUBLANES, LANES), F32)]),
        in_specs=[_HBM] * n + ([_ANY] if dep is not None else []),
        out_specs=tuple([_HBM] * n + [_SEM, _SEM, pl.BlockSpec(memory_space=pltpu.VMEM)]),
        input_output_aliases={i: i for i in range(n)},
        compiler_params=pltpu.CompilerParams(has_side_effects=_DATAFLOW),
    )(*[_in_hbm(a) for a in lands], *([dep] if dep is not None else []))
    return list(outs[:n]), outs[n], outs[n + 1], outs[n + 2]


def _gather_wait(tag, lands, send_sem, recv_sem, axes, after):
    n = len(lands)

    def body(*refs):
        for cp in _gather_copies(refs[:n], refs[n], refs[n + 1], axes, lambda px, py: (px, py)):
            cp.wait_send()
            cp.wait_recv()

    outs = pl.pallas_call(
        body, name=f"gather_wait_{tag}", out_shape=tuple(pltpu.HBM(a.shape, a.dtype) for a in lands),
        in_specs=[_HBM] * n + [_SEM, _SEM, _ANY], out_specs=tuple([_HBM] * n),
        input_output_aliases={i: i for i in range(n)},
        compiler_params=pltpu.CompilerParams(has_side_effects=_DATAFLOW),
    )(*lands, send_sem, recv_sem, after)
    return list(outs)


N_SLOTS = N_DEV - 1


def _scatter_sends(grad_refs, land_refs, send_sem, recv_sem, meta):
    x, y, c = _coords()
    out = []
    for j, (axis, owner, q, width) in enumerate(meta):
        other = c if owner == 0 else 1 - c
        for f, (fx, fy) in enumerate(_CHIP_FLIPS):
            px, py = _flip(x, fx), _flip(y, fy)
            slot = f + 4 * other - 1
            out.append((other if f == 0 else None, pltpu.make_async_remote_copy(
                src_ref=_shard_of(grad_refs[j], axis, 2 * px + py, width), dst_ref=land_refs[j].at[q, slot],
                send_sem=send_sem.at[4 * j + f], recv_sem=recv_sem.at[N_SLOTS * j + slot],
                device_id=(px, py, owner), device_id_type=MESH)))
    return out


def _scatter_start(layer, grads, lands, meta):
    n = len(grads)
    uniq = []
    for a in lands:
        if not any(a is u for u in uniq):
            uniq.append(a)
    which = [next(k for k, u in enumerate(uniq) if u is a) for a in lands]
    nu = len(uniq)

    def body(*refs):
        grad_refs, land_u = refs[:n], refs[n:n + nu]
        send_sem, recv_sem = refs[n + nu], refs[n + nu + 1]
        for other, cp in _scatter_sends(grad_refs, [land_u[k] for k in which], send_sem, recv_sem, meta):
            if other is None:
                cp.start()
            else:
                pl.when(other == 1)(cp.start)
        refs[-1][...] = jnp.zeros_like(refs[-1])

    thru = [pltpu.HBM(a.shape, a.dtype) for a in list(grads) + uniq]
    outs = pl.pallas_call(
        body, name=f"scatter_start_{layer}",
        out_shape=tuple([pltpu.SemaphoreType.DMA((4 * n,)), pltpu.SemaphoreType.DMA((N_SLOTS * n,))] + thru
                        + [jax.ShapeDtypeStruct((SUBLANES, LANES), F32)]),
        in_specs=[_HBM] * (n + nu),
        out_specs=tuple([_SEM, _SEM] + [_HBM] * (n + nu) + [pl.BlockSpec(memory_space=pltpu.VMEM)]),
        input_output_aliases={i: 2 + i for i in range(n + nu)},
        compiler_params=pltpu.CompilerParams(has_side_effects=_DATAFLOW),
    )(*[_in_hbm(a) for a in list(grads) + uniq])
    new_lands = [outs[2 + n + k] for k in which]
    return outs[0], outs[1], list(outs[2:2 + n]), new_lands, outs[-1]


def _scatter_wait(started, lands):
    nl = len(lands)
    flat_grads = [g for s in started for g in s[2]]
    ng, ns = len(flat_grads), len(started)

    def body(*refs):
        land_refs = refs[:nl]
        grad_refs = refs[nl:nl + ng]
        sem_refs = refs[nl + ng:nl + ng + 2 * ns]
        _, _, c = _coords()
        off = 0
        for k, (_, _, grads, idx, meta) in enumerate(started):
            send_sem, recv_sem = sem_refs[2 * k], sem_refs[2 * k + 1]
            lr = [land_refs[i] for i in idx]
            for other, cp in _scatter_sends(grad_refs[off:off + len(grads)], lr, send_sem, recv_sem, meta):
                if other is None:
                    cp.wait_send()
                else:
                    pl.when(other == 1)(cp.wait_send)
            for j, (axis, owner, q, width) in enumerate(meta):
                mine = (c if owner == 0 else 1 - c) == 0

                @pl.when(mine)
                def _():
                    for slot in range(N_SLOTS):
                        land = lr[j].at[q, slot]
                        pltpu.make_async_remote_copy(
                            src_ref=land, dst_ref=land, send_sem=send_sem.at[0], recv_sem=recv_sem.at[N_SLOTS * j + slot],
                            device_id=_coords(), device_id_type=MESH).wait_recv()
            off += len(grads)

    args = list(lands) + flat_grads
    thru = [pltpu.HBM(a.shape, a.dtype) for a in args]
    sems = [s for st in started for s in st[:2]]
    outs = pl.pallas_call(
        body, name="scatter_wait", out_shape=tuple(thru), in_specs=[_HBM] * (nl + ng) + [_SEM] * (2 * ns),
        out_specs=tuple([_HBM] * (nl + ng)), input_output_aliases={i: i for i in range(nl + ng)},
        compiler_params=pltpu.CompilerParams(has_side_effects=_DATAFLOW),
    )(*args, *sems)
    return list(outs[:nl]), list(outs[nl:])


def _sum_and_share(recv, layer_grads, axis, chip, name):
    n, ns, r, c = recv.shape
    tr = _pick(r, prefs=(256, 128, 64, 32, 16))
    nr = r // tr
    nsteps = n * nr
    nlay = len(layer_grads)
    own_map = (lambda h, i, s: (i, s[0])) if axis == 1 else (lambda h, i, s: (s[0] * nr + i, 0))

    def body(chip_ref, i_ref, *rest):
        g_refs = rest[:nlay]
        o_ref, buf, loc_sems, send_sems, recv_sems = rest[nlay:]
        h, i = pl.program_id(0), pl.program_id(1)
        step = h * nr + i
        slot = step % 2
        x, y, core = _coords()
        layer = core * n + h
        own = g_refs[0][...]
        for l in range(1, nlay):
            own = jnp.where(layer == l, g_refs[l][...], own)

        def copies(sl):
            dst = o_ref.at[core * n + h, pl.ds(pl.multiple_of(i * tr, tr), tr), :]
            loc = pltpu.make_async_copy(buf.at[sl], dst, loc_sems.at[sl])
            rem = pltpu.make_async_remote_copy(
                src_ref=buf.at[sl], dst_ref=dst, send_sem=send_sems.at[sl], recv_sem=recv_sems.at[step],
                device_id=(x, y, 1 - core), device_id_type=MESH)
            return loc, rem

        def drain(sl):
            loc, rem = copies(sl)
            loc.wait()
            rem.wait_send()

        pl.when(step >= 2)(lambda: drain(slot))
        acc = own.astype(F32)
        for s in range(ns):
            acc = acc + i_ref[s].astype(F32)
        buf[slot] = acc
        loc, rem = copies(slot)
        loc.start()
        rem.start()

        @pl.when(step == nsteps - 1)
        def _():
            drain(slot)
            if nsteps > 1:
                drain(1 - slot)
            for hh in range(n):
                for ii in range(nr):
                    land = o_ref.at[(1 - core) * n + hh, pl.ds(ii * tr, tr), :]
                    pltpu.make_async_remote_copy(
                        src_ref=buf.at[0], dst_ref=land, send_sem=send_sems.at[0], recv_sem=recv_sems.at[hh * nr + ii],
                        device_id=(x, y, 1 - core), device_id_type=MESH).wait_recv()

    return pl.pallas_call(
        body, name=name, out_shape=jax.ShapeDtypeStruct((2 * n, r, c), F32),
        grid_spec=pltpu.PrefetchScalarGridSpec(
            num_scalar_prefetch=1, grid=(n, nr),
            in_specs=[pl.BlockSpec((None, ns, tr, c), lambda h, i, s: (h, 0, i, 0))]
            + [pl.BlockSpec((tr, c), own_map)] * nlay,
            out_specs=_HBM,
            scratch_shapes=[pltpu.VMEM((2, tr, c), F32), pltpu.SemaphoreType.DMA((2,)),
                            pltpu.SemaphoreType.DMA((2,)), pltpu.SemaphoreType.DMA((nsteps,))]),
        compiler_params=_cparams(("arbitrary", "arbitrary")),
    )(chip, recv, *layer_grads)


def _gather_sums_over_chips(part):
    def body(i_ref, o_ref, send_sems, recv_sems):
        x, y, c = _coords()
        o_ref[2 * x + y] = i_ref[...]

        def copy(f, slot_chip):
            fx, fy = _CHIP_FLIPS[f]
            return pltpu.make_async_remote_copy(
                src_ref=i_ref, dst_ref=o_ref.at[2 * slot_chip[0] + slot_chip[1]], send_sem=send_sems.at[f - 1],
                recv_sem=recv_sems.at[f - 1], device_id=(_flip(x, fx), _flip(y, fy), c), device_id_type=MESH)

        sends = [copy(f, (x, y)) for f in (1, 2, 3)]
        for cp in sends:
            cp.start()
        for f in (1, 2, 3):
            fx, fy = _CHIP_FLIPS[f]
            copy(f, (_flip(x, fx), _flip(y, fy))).wait_recv()
        for cp in sends:
            cp.wait_send()

    vmem = pl.BlockSpec(memory_space=pltpu.VMEM)
    return pl.pallas_call(
        body, name="gather_small_sums", out_shape=jax.ShapeDtypeStruct((N_CHIPS,) + part.shape, part.dtype),
        in_specs=[vmem], out_specs=vmem,
        scratch_shapes=[pltpu.SemaphoreType.DMA((3,)), pltpu.SemaphoreType.DMA((3,))],
    )(part)


def _adamw_update(w_ref, g_ref, m_ref, v_ref, d_ref, mo_ref, vo_ref):
    bc1 = 1.0 - ADAM_B1 ** ADAM_STEP
    bc2 = 1.0 - ADAM_B2 ** ADAM_STEP
    gv = g_ref[...]
    mn = ADAM_B1 * m_ref[...] + (1.0 - ADAM_B1) * gv
    vn = ADAM_B2 * v_ref[...] + (1.0 - ADAM_B2) * (gv * gv)
    d_ref[...] = -ADAM_LR * ((mn / bc1) / (jnp.sqrt(vn / bc2) + ADAM_EPS) + ADAM_WD * w_ref[...])
    mo_ref[...] = mn
    vo_ref[...] = vn


def _adamw(w, g, m, v, name):
    def body(*refs):
        _adamw_update(*refs)

    tr = _pick(w.shape[0], prefs=(256, 128, 64, 32, 16, 8))
    blk = pl.BlockSpec((tr, w.shape[1]), lambda i: (i, 0))
    sds = jax.ShapeDtypeStruct(w.shape, F32)
    return pl.pallas_call(
        body, name=name, out_shape=(sds, sds, sds), grid=(w.shape[0] // tr,), in_specs=[blk] * 4,
        out_specs=(blk,) * 3, compiler_params=_cparams(("parallel",)),
    )(w, g, m, v)


def _adamw_many(tensors, name, by_layer=False):
    n = len(tensors)

    def body(*refs):
        for t in range(n):
            _adamw_update(*refs[4 * t:4 * t + 4], *refs[4 * n + 3 * t:4 * n + 3 * t + 3])

    def spec(a):
        nd = a.ndim
        if by_layer:
            return pl.BlockSpec((1,) + a.shape[1:], lambda i: (i,) + (0,) * (nd - 1))
        return pl.BlockSpec(a.shape, lambda i: (0,) * nd)

    steps = tensors[0][0].shape[0] if by_layer else 1
    outs = pl.pallas_call(
        body, name=name, out_shape=tuple(jax.ShapeDtypeStruct(t[0].shape, F32) for t in tensors for _ in range(3)),
        grid=(steps,), in_specs=[spec(a) for t in tensors for a in t],
        out_specs=tuple(spec(t[0]) for t in tensors for _ in range(3)), compiler_params=_cparams(("parallel",)),
    )(*[a for t in tensors for a in t])
    return [tuple(outs[3 * t:3 * t + 3]) for t in range(n)]


_PACK_QUANTUM = 256 * LANES


def _pack(arrs):
    flat = jnp.concatenate([a.reshape(-1).astype(F32) for a in arrs])
    flat = jnp.pad(flat, (0, (-flat.shape[0]) % _PACK_QUANTUM))
    return flat.reshape(-1, LANES)


def _unpack(p, shapes):
    flat = p.reshape(-1)
    out, off = [], 0
    for s in shapes:
        n = int(np.prod(s))
        out.append(flat[off:off + n].reshape(s))
        off += n
    return out


def kernel(*args):
    nw = len(WEIGHTS)
    x, tgt = args[0], args[1 + nw]
    w = dict(zip(WEIGHTS, args[1:1 + nw]))
    m = dict(zip(WEIGHTS, args[2 + nw:2 + 2 * nw]))
    v = dict(zip(WEIGHTS, args[2 + 2 * nw:2 + 3 * nw]))
    _, L, D = x.shape
    chip = 2 * lax.axis_index("x") + lax.axis_index("y")

    big = list(BIG)
    small_sh_shapes = [w[n].shape for n in SMALL_SHARDED]
    nbig = len(big)
    chip1 = chip.reshape(1).astype(jnp.int32)
    axes2 = [BIG[n] - 1 for n in big] + [0]
    shards = [w[n] for n in big] + [_pack([w[n] for n in SMALL_SHARDED])[None]]
    pairs = [(t, l) for t in range(nbig + 1) for l in range(shards[t].shape[0])]
    depth = w['norm_mix_g'].shape[0]
    part_of = lambda t, l: 0 if t == nbig else 2 * _model_layer(big[t], l) + big[t].startswith('ffn')
    flying, token = [], None
    for g in range(2 * depth):
        ids = [k for k, (t, l) in enumerate(pairs) if part_of(t, l) == g]
        ts = [pairs[k][0] for k in ids]
        placed = [_place_quarter(shards[t], pairs[k][1], axes2[t], chip1, F32 if t == nbig else BF16, token)
                  for k, t in zip(ids, ts)]
        lands, send, recv, token = _gather_start(g, placed, [axes2[t] for t in ts], token)
        flying.append((ts, lands, send, recv))

    def wait_group(g, after):
        ts, lands, send, recv = flying[g]
        landed = _gather_wait(g, lands, send, recv, [axes2[t] for t in ts], token if after is None else after)
        return dict(zip(ts, landed))

    first = wait_group(0, None)
    packed = first.pop(nbig).reshape(N_CHIPS, -1, LANES)
    per_chip = [_unpack(packed[s], small_sh_shapes) for s in range(N_CHIPS)]
    wl = dict(w)
    for k, n in enumerate(SMALL_SHARDED):
        wl[n] = jnp.concatenate([per_chip[s][k] for s in range(N_CHIPS)], axis=-1)

    def layer_weights(i, after):
        got = first if i == 0 else wait_group(i, after)
        return {big[t]: a for t, a in got.items()}

    small_shapes = [(w[n].shape[:-1] + (w[n].shape[-1] * N_CHIPS,)) if n in SMALL_SHARDED else w[n].shape
                    for n in SMALL] + [(1,)]
    n_small = sum(int(np.prod(s)) for s in small_shapes)
    pack_rows = -(-n_small // _PACK_QUANTUM) * _PACK_QUANTUM // LANES
    nlayers = [w[n].shape[0] for n in big] + [2]
    halves = [n // 2 for n in nlayers]
    quarters = [tuple(w[n].shape[1:]) for n in big] + [(pack_rows // 2 // N_CHIPS, LANES)]
    wire = [BF16] * nbig + [F32]
    land_now = [lax.empty((halves[t], N_SLOTS) + quarters[t], wire[t]) for t in range(nbig + 1)]
    gparts = [[None] * n for n in nlayers]
    started = []

    def start_scatter(tag, ts, ls, arrays):
        meta = [(axes2[t], l // halves[t], l % halves[t], quarters[t][axes2[t]]) for t, l in zip(ts, ls)]
        send, recv, thru, new_lands, token = _scatter_start(tag, arrays, [land_now[t] for t in ts], meta)
        for t, ln in zip(ts, new_lands):
            land_now[t] = ln
        started.append((send, recv, thru, ts, meta, ls))
        return token

    def on_layer_grads(g, gb):
        ts = [big.index(n) for n in gb]
        return start_scatter(g, ts, [g // 2 if big[t].startswith('ffn') else g // 4 for t in ts],
                             [gb[big[t]] for t in ts])

    loss, dx, gsmall = _local_step(x.reshape(L, D), tgt.reshape(L, D), wl, layer_weights, on_layer_grads)
    gpack = _pack([gsmall[n] for n in SMALL] + [loss.reshape(1)])
    start_scatter(2 * depth, [nbig, nbig], [0, 1], [gpack[:pack_rows // 2], gpack[pack_rows // 2:]])
    landed, sent = _scatter_wait([s[:5] for s in started], land_now)
    for (t, l), g in zip([(t, l) for s in started for t, l in zip(s[3], s[5])], sent):
        gparts[t][l] = g
    gshard = {n: _sum_and_share(landed[t], gparts[t], axes2[t], chip1, "sum_share_" + n) for t, n in enumerate(big)}
    small_sum = _sum_and_share(landed[nbig], gparts[nbig], 0, chip1, "sum_share_small")
    gpack = _gather_sums_over_chips(small_sum).transpose(1, 0, 2, 3).reshape(pack_rows, LANES)
    gs = dict(zip(SMALL + ['loss'], _unpack(gpack, small_shapes)))
    loss = gs.pop('loss').reshape(())
    for n in SMALL_SHARDED:
        width = w[n].shape[-1]
        gs[n] = lax.dynamic_slice_in_dim(gs[n], chip * width, width, axis=gs[n].ndim - 1)

    grads, delta, new_m, new_v = {}, {}, {}, {}
    for n in big:
        shp = w[n].shape
        flat = lambda a: a.reshape(shp[0] * shp[1], shp[2])
        g = gshard[n]
        grads[n] = g
        d_, m_, v_ = _adamw(flat(w[n]), flat(g), flat(m[n]), flat(v[n]), "adamw_" + n)
        delta[n], new_m[n], new_v[n] = d_.reshape(shp), m_.reshape(shp), v_.reshape(shp)
    sparse = [n for n in SMALL if w[n].ndim == 4 and w[n].shape[-1] < LANES // 2]
    for names, by_layer in ((sparse, True), ([n for n in SMALL if n not in sparse], False)):
        as2d = lambda a: a.reshape(1, -1) if a.ndim == 1 else a
        res = _adamw_many([(as2d(w[n]), as2d(gs[n]), as2d(m[n]), as2d(v[n])) for n in names],
                          "adamw_small_by_layer" if by_layer else "adamw_small", by_layer)
        for n, (d_, m_, v_) in zip(names, res):
            shp = w[n].shape
            grads[n], delta[n], new_m[n], new_v[n] = gs[n], d_.reshape(shp), m_.reshape(shp), v_.reshape(shp)

    return (loss, dx.reshape(1, L, D), *[grads[n] for n in WEIGHTS], *[delta[n] for n in WEIGHTS],
            *[new_m[n] for n in WEIGHTS], *[new_v[n] for n in WEIGHTS])
```

```python
import functools
import math

import numpy as np
import jax
import jax.numpy as jnp
from jax import lax
from jax.experimental import pallas as pl
from jax.experimental.pallas import tpu as pltpu

F32 = jnp.float32
BF16 = jnp.bfloat16
MESH = pl.DeviceIdType.MESH

EPS = 1e-6
CHUNK = 128
POOL_WINDOWS = (2, 4, 8, 16)
LANES = 128
SUBLANES = 8
SCAN_CHUNKS = SUBLANES
S5_GROUPS_PER_STEP = 4
MM_TM_CAP, MM_TN_CAP, MM_TK_CAP = 1408, 1408, 2048
MM_TK_WHOLE = 2048
VMEM_LIMIT = 48 * 1024 * 1024
VMEM_LIMIT_S5 = 56 * 1024 * 1024

ADAM_LR, ADAM_B1, ADAM_B2, ADAM_EPS, ADAM_WD, ADAM_STEP = 0.001, 0.9, 0.999, 1e-08, 0.01, 10

WEIGHTS = ['norm_mix_g', 'even_w_in', 'even_conv_w', 'ssm_log_step', 'ssm_a_re', 'ssm_a_im', 'ssm_b_re',
           'ssm_b_im', 'ssm_c_re', 'ssm_c_im', 'ssm_d', 'ssm_glu_w', 'ssm_glu_b', 'even_w_out', 'odd_w_in',
           'pool_w', 'pool_scale', 'sgu_norm_g', 'sgu_w', 'sgu_b', 'odd_w_out', 'norm_ffn_g', 'ffn_w_up',
           'ffn_conv_w', 'ffn_conv_b', 'ffn_w_down', 'norm_final_g']
BIG = {'even_w_in': 2, 'ssm_glu_w': 1, 'even_w_out': 1, 'odd_w_in': 2, 'odd_w_out': 1, 'ffn_w_up': 2,
       'ffn_w_down': 1}
SMALL_SHARDED = ('even_conv_w', 'pool_scale', 'sgu_norm_g', 'ffn_conv_w')
SMALL = [n for n in WEIGHTS if n not in BIG]
N_CHIPS = 4
N_DEV = 8


def _cparams(sem=None, vmem=VMEM_LIMIT):
    kw = dict(vmem_limit_bytes=vmem)
    if sem is not None:
        kw['dimension_semantics'] = sem
    return pltpu.CompilerParams(**kw)


def _pick(n, segs=(), prefs=(1024, 512, 256, 128)):
    for t in prefs:
        if n % t == 0 and all(s % t == 0 for s in segs if s):
            return t
    return n


def _largest_tile(n, segs, cap):
    best = None
    for t in range(LANES, min(n, cap) + 1, LANES):
        if n % t == 0 and all(s % t == 0 for s in segs if s):
            best = t
    return best if best is not None else n


def _ldims(arr, kind):
    if kind is None:
        return arr.shape
    if kind[0] == 'lead':
        return arr.shape[1:]
    return (arr.shape[1], arr.shape[0] * arr.shape[2])


def _segw(arr, kind):
    return arr.shape[2] if (kind is not None and kind[0] == 'seg') else None


def _opspec(arr, kind, br, bc, rfn, cfn):
    if kind is None:
        return pl.BlockSpec((br, bc), lambda i, j, k: (rfn(i, j, k), cfn(i, j, k)))
    if kind[0] == 'lead':
        lead = kind[1]
        return pl.BlockSpec((None, br, bc), lambda i, j, k: (lead, rfn(i, j, k), cfn(i, j, k)))
    per = arr.shape[2] // bc
    return pl.BlockSpec((None, br, bc), lambda i, j, k: (cfn(i, j, k) // per, rfn(i, j, k), cfn(i, j, k) % per))


def _mm(a, b, mode, out_dtype, name, ak=None, bk=None, ok=None, res=None, dep=None):
    ar, ac = _ldims(a, ak)
    br_, bc_ = _ldims(b, bk)
    if mode == 'nn':
        M, K, N = ar, ac, bc_
        assert br_ == K
    else:
        M, K, N = ar, ac, br_
        assert bc_ == K
    sa, sb = _segw(a, ak), _segw(b, bk)
    so = (N // ok[1]) if ok is not None else None
    tm = _largest_tile(M, [], MM_TM_CAP)
    tn = _largest_tile(N, [sb if mode == 'nn' else None, so], MM_TN_CAP)
    ksegs = [sa, sb if mode == 'nt' else None]
    tk = K if (K <= MM_TK_WHOLE and not any(ksegs)) else _largest_tile(K, ksegs, MM_TK_CAP)
    nk = K // tk
    I = lambda i, j, k: i
    J = lambda i, j, k: j
    Kk = lambda i, j, k: k
    a_spec = _opspec(a, ak, tm, tk, I, Kk)
    if mode == 'nn':
        b_spec = _opspec(b, bk, tk, tn, Kk, J)
        dims = (((1,), (0,)), ((), ()))
    else:
        b_spec = _opspec(b, bk, tn, tk, J, Kk)
        dims = (((1,), (1,)), ((), ()))
    if ok is None:
        out_shape = jax.ShapeDtypeStruct((M, N), out_dtype)
        o_spec = pl.BlockSpec((tm, tn), lambda i, j, k: (i, j))
    else:
        out_shape = jax.ShapeDtypeStruct((ok[1], M, N // ok[1]), out_dtype)
        per = (N // ok[1]) // tn
        o_spec = pl.BlockSpec((None, tm, tn), lambda i, j, k: (j // per, i, j % per))
    has_res = res is not None

    def body(*refs):
        a_ref, b_ref = refs[0], refs[1]
        r_ref = refs[2] if has_res else None
        o_ref = refs[n_in]
        prod = lax.dot_general(a_ref[...].astype(BF16), b_ref[...].astype(BF16), dims, preferred_element_type=F32)
        if nk == 1:
            o_ref[...] = (prod + r_ref[...] if has_res else prod).astype(out_dtype)
            return
        acc = refs[-1]
        k = pl.program_id(2)

        @pl.when(k == 0)
        def _():
            acc[...] = prod

        @pl.when(k > 0)
        def _():
            acc[...] += prod

        @pl.when(k == nk - 1)
        def _():
            o = acc[...]
            if has_res:
                o = o + r_ref[...]
            o_ref[...] = o.astype(out_dtype)

    in_specs = [a_spec, b_spec]
    args = [a, b]
    if has_res:
        in_specs.append(pl.BlockSpec((tm, tn), lambda i, j, k: (i, j)))
        args.append(res)
    if dep is not None:
        in_specs.append(pl.BlockSpec(memory_space=pl.ANY))
        args.append(dep)
    n_in = len(args)
    return pl.pallas_call(
        body, name=name, out_shape=out_shape, grid=(M // tm, N // tn, nk), in_specs=in_specs, out_specs=o_spec,
        scratch_shapes=[pltpu.VMEM((tm, tn), F32)] if nk > 1 else [],
        compiler_params=_cparams(("parallel", "parallel", "arbitrary")),
    )(*args)


_G0 = math.sqrt(2.0 / math.pi)
_G1 = 0.044715


def _gelu(x):
    return 0.5 * x * (1.0 + jnp.tanh(_G0 * (x + _G1 * x * x * x)))


def _gelu_grad(x):
    x2 = x * x
    t = jnp.tanh(_G0 * (x + _G1 * x * x2))
    return 0.5 * (1.0 + t) + 0.5 * x * (1.0 - t * t) * (_G0 * (1.0 + 3.0 * _G1 * x2))


def _sigmoid(x):
    return 1.0 / (1.0 + jnp.exp(-x))


def _down(v, k):
    r = pltpu.roll(v, k, axis=0)
    row = lax.broadcasted_iota(jnp.int32, (SUBLANES, v.shape[1]), 0)
    return jnp.concatenate([jnp.where(row >= k, r[:SUBLANES], 0.0), r[SUBLANES:]], axis=0)


def _up(v, k):
    n = v.shape[0]
    r = pltpu.roll(v, n - k, axis=0)
    row = lax.broadcasted_iota(jnp.int32, (SUBLANES, v.shape[1]), 0)
    return jnp.concatenate([r[:n - SUBLANES], jnp.where(row < SUBLANES - k, r[n - SUBLANES:], 0.0)], axis=0)


def _taps(v):
    return _down(v, 2), _down(v, 1), v


def _conv3(taps, w):
    return w[0:1, :] * taps[0] + w[1:2, :] * taps[1] + w[2:3, :] * taps[2]


def _conv3_t(dv, w):
    return w[2:3, :] * dv + w[1:2, :] * _up(dv, 1) + w[0:1, :] * _up(dv, 2)


def _conv3_dw(dv, taps):
    return tuple(jnp.sum(dv * tp, axis=0, keepdims=True) for tp in taps)


def _cmul(ar, ai, br, bi):
    return ar * br - ai * bi, ar * bi + ai * br


def _cpow(lr, li, n):
    rr = ri = None
    br, bi = lr, li
    while n:
        if n & 1:
            rr, ri = (br, bi) if rr is None else _cmul(rr, ri, br, bi)
        n >>= 1
        if n:
            br, bi = _cmul(br, bi, br, bi)
    return rr, ri


NORM_ROWS = 256


def _norm_mm(x, g, b, out_dtype, name, ok=None):
    M, D = x.shape
    N = b.shape[1]
    so = (N // ok[1]) if ok is not None else None
    tm = _largest_tile(M, [], 1024)
    tn = _largest_tile(N, [so], MM_TN_CAP)
    if ok is None:
        out_shape = jax.ShapeDtypeStruct((M, N), out_dtype)
        o_spec = pl.BlockSpec((tm, tn), lambda i, j: (i, j))
    else:
        out_shape = jax.ShapeDtypeStruct((ok[1], M, N // ok[1]), out_dtype)
        per = (N // ok[1]) // tn
        o_spec = pl.BlockSpec((None, tm, tn), lambda i, j: (j // per, i, j % per))

    def body(x_ref, g_ref, b_ref, o_ref, ht_ref, h_scr):
        @pl.when(pl.program_id(1) == 0)
        def _():
            for c in range(tm // NORM_ROWS):
                rows = pl.ds(c * NORM_ROWS, NORM_ROWS)
                xv = x_ref[rows, :]
                h = xv * lax.rsqrt(jnp.mean(xv * xv, axis=-1, keepdims=True) + EPS) * g_ref[...]
                h_scr[rows, :] = h.astype(BF16)
                ht_ref[:, rows] = h.T.astype(BF16)

        o_ref[...] = jnp.dot(h_scr[...], b_ref[...], preferred_element_type=F32).astype(out_dtype)

    return pl.pallas_call(
        body, name=name, out_shape=(out_shape, jax.ShapeDtypeStruct((D, M), BF16)), grid=(M // tm, N // tn),
        in_specs=[pl.BlockSpec((tm, D), lambda i, j: (i, 0)), pl.BlockSpec((1, D), lambda i, j: (0, 0)),
                  pl.BlockSpec((D, tn), lambda i, j: (0, j))],
        out_specs=(o_spec, pl.BlockSpec((D, tm), lambda i, j: (0, i))),
        scratch_shapes=[pltpu.VMEM((tm, D), BF16)], compiler_params=_cparams(("parallel", "arbitrary")),
    )(x, g.reshape(1, D), b)


def _mm_norm_bwd(a, b, x, g, dres, name, ak=None, dep=None):
    M, K = _ldims(a, ak)
    D = b.shape[0]
    assert b.shape[1] == K and x.shape == (M, D)
    sa = _segw(a, ak)
    tm = _largest_tile(M, [], 1024)
    tk = K if (K <= MM_TK_WHOLE and not sa) else _largest_tile(K, [sa], MM_TK_CAP)
    ni, nk = M // tm, K // tk
    a3 = _opspec(a, ak, tm, tk, lambda i, j, k: i, lambda i, j, k: k)
    a_spec = pl.BlockSpec(a3.block_shape, lambda i, k: a3.index_map(i, 0, k))
    n_in = 5 + (dep is not None)

    def body(*refs):
        a_ref, b_ref, x_ref, g_ref, r_ref = refs[:5]
        dx_ref, dxb_ref, dg_ref, acc, accg = refs[n_in:]
        i, k = pl.program_id(0), pl.program_id(1)
        prod = lax.dot_general(a_ref[...].astype(BF16), b_ref[...], (((1,), (1,)), ((), ())),
                               preferred_element_type=F32)

        @pl.when(k == 0)
        def _():
            acc[...] = prod

        @pl.when(k > 0)
        def _():
            acc[...] += prod

        @pl.when((i == 0) & (k == 0))
        def _():
            accg[...] = jnp.zeros_like(accg)

        @pl.when(k == nk - 1)
        def _():
            for c in range(tm // NORM_ROWS):
                rows = pl.ds(c * NORM_ROWS, NORM_ROWS)
                xv = x_ref[rows, :]
                r = lax.rsqrt(jnp.mean(xv * xv, axis=-1, keepdims=True) + EPS)
                xh = xv * r
                dhv = acc[rows, :]
                accg[...] += jnp.sum((dhv * xh).reshape(NORM_ROWS // SUBLANES, SUBLANES, D), axis=0)
                dxh = dhv * g_ref[...]
                dxv = r_ref[rows, :] + r * (dxh - xh * jnp.mean(dxh * xh, axis=-1, keepdims=True))
                dx_ref[rows, :] = dxv
                dxb_ref[rows, :] = dxv.astype(BF16)

        @pl.when((i == ni - 1) & (k == nk - 1))
        def _():
            dg_ref[...] = jnp.sum(accg[...], axis=0, keepdims=True)

    row = pl.BlockSpec((tm, D), lambda i, k: (i, 0))
    vec = pl.BlockSpec((1, D), lambda i, k: (0, 0))
    in_specs = [a_spec, pl.BlockSpec((D, tk), lambda i, k: (0, k)), row, vec, row]
    args = [a, b, x, g.reshape(1, D), dres]
    if dep is not None:
        in_specs.append(pl.BlockSpec(memory_space=pl.ANY))
        args.append(dep)
    return pl.pallas_call(
        body, name=name,
        out_shape=(jax.ShapeDtypeStruct((M, D), F32), jax.ShapeDtypeStruct((M, D), BF16),
                   jax.ShapeDtypeStruct((1, D), F32)),
        grid=(ni, nk), in_specs=in_specs, out_specs=(row, row, vec),
        scratch_shapes=[pltpu.VMEM((tm, D), F32), pltpu.VMEM((SUBLANES, D), F32)],
        compiler_params=_cparams(("arbitrary", "arbitrary"), VMEM_LIMIT_S5),
    )(*args)


def _loss_head(x, g, tgt):
    L, D = x.shape
    tr = _pick(L, prefs=(512, 256, 128))
    nsteps = L // tr

    def body(x_ref, g_ref, t_ref, loss_ref, dx_ref, dxb_ref, dg_ref, acc_g, acc_l):
        i = pl.program_id(0)

        @pl.when(i == 0)
        def _():
            acc_g[...] = jnp.zeros_like(acc_g)
            acc_l[...] = jnp.zeros_like(acc_l)

        xv = x_ref[...]
        gv = g_ref[...]
        r = lax.rsqrt(jnp.mean(xv * xv, axis=-1, keepdims=True) + EPS)
        xh = xv * r
        e = xh * gv - t_ref[...]
        acc_l[...] += jnp.sum((e * e).reshape(tr // SUBLANES, SUBLANES, D), axis=0)
        dy = e * (1.0 / D)
        acc_g[...] += jnp.sum((dy * xh).reshape(tr // SUBLANES, SUBLANES, D), axis=0)
        dxh = dy * gv
        dxv = r * (dxh - xh * jnp.mean(dxh * xh, axis=-1, keepdims=True))
        dx_ref[...] = dxv
        dxb_ref[...] = dxv.astype(BF16)

        @pl.when(i == nsteps - 1)
        def _():
            dg_ref[...] = jnp.sum(acc_g[...], axis=0, keepdims=True)
            tot = jnp.sum(jnp.sum(acc_l[...], axis=0, keepdims=True), axis=1, keepdims=True) * (0.5 / D)
            loss_ref[...] = jnp.broadcast_to(tot, (SUBLANES, LANES))

    row = pl.BlockSpec((tr, D), lambda i: (i, 0))
    vec = pl.BlockSpec((1, D), lambda i: (0, 0))
    return pl.pallas_call(
        body, name="loss_head",
        out_shape=(jax.ShapeDtypeStruct((SUBLANES, LANES), F32), jax.ShapeDtypeStruct((L, D), F32),
                   jax.ShapeDtypeStruct((L, D), BF16), jax.ShapeDtypeStruct((1, D), F32)),
        grid=(nsteps,), in_specs=[row, vec, row],
        out_specs=(pl.BlockSpec((SUBLANES, LANES), lambda i: (0, 0)), row, row, vec),
        scratch_shapes=[pltpu.VMEM((SUBLANES, D), F32), pltpu.VMEM((SUBLANES, D), F32)],
        compiler_params=_cparams(("arbitrary",)),
    )(x, g.reshape(1, D), tgt)


def _sconv_fwd(proj4, conv_w, name):
    _, L, C = proj4.shape
    cb = LANES

    def body(p_ref, w_ref, o_ref):
        xa, ba, ca = p_ref[0].astype(F32), p_ref[1].astype(F32), p_ref[2].astype(F32)
        o_ref[...] = (ba * _conv3(_taps(ca * xa), w_ref[...])).astype(BF16)

    return pl.pallas_call(
        body, name=name, out_shape=jax.ShapeDtypeStruct((L, 2 * C), BF16), grid=(C // cb,),
        in_specs=[pl.BlockSpec((3, L, cb), lambda j: (0, 0, j)), pl.BlockSpec((3, cb), lambda j: (0, j))],
        out_specs=pl.BlockSpec((L, cb), lambda j: (0, j)), compiler_params=_cparams(("parallel",)),
    )(proj4, conv_w)


def _sconv_bwd(proj4, dmix, conv_w, name):
    _, L, C = proj4.shape
    cb = LANES

    def body(p_ref, d_ref, w_ref, o_ref, dw_ref):
        xa, ba, ca = p_ref[0].astype(F32), p_ref[1].astype(F32), p_ref[2].astype(F32)
        w = w_ref[...]
        dya = d_ref[...]
        tq = _taps(ca * xa)
        cq = _conv3(tq, w)
        dcq = dya * ba
        dq = _conv3_t(dcq, w)
        for tap, dwt in enumerate(_conv3_dw(dcq, tq)):
            dw_ref[tap:tap + 1, :] = dwt
        o_ref[0] = (dq * ca).astype(BF16)
        o_ref[1] = (dya * cq).astype(BF16)
        o_ref[2] = (dq * xa).astype(BF16)

    return pl.pallas_call(
        body, name=name,
        out_shape=(jax.ShapeDtypeStruct((4, L, C), BF16), jax.ShapeDtypeStruct((3, C), F32)), grid=(C // cb,),
        in_specs=[pl.BlockSpec((3, L, cb), lambda j: (0, 0, j)), pl.BlockSpec((L, cb), lambda j: (0, j)),
                  pl.BlockSpec((3, cb), lambda j: (0, j))],
        out_specs=(pl.BlockSpec((3, L, cb), lambda j: (0, 0, j)), pl.BlockSpec((3, cb), lambda j: (0, j))),
        compiler_params=_cparams(("parallel",)),
    )(proj4, dmix, conv_w)


def _s5_prep(log_step, a_re, a_im, b_re, b_im, c_re, c_im):
    G, P = a_re.shape
    H = b_re.shape[-1]
    gs = S5_GROUPS_PER_STEP
    ns = G // gs
    gu = LANES // H
    lam = lax.complex(a_re, a_im)
    step = jnp.exp(log_step)[:, None]
    lam_bar = jnp.exp(lam * step)
    b_bar = ((lam_bar - 1.0) / lam)[..., None] * lax.complex(b_re, b_im)
    lr = jnp.real(lam_bar).reshape(ns, 1, gs * P)
    li = jnp.imag(lam_bar).reshape(ns, 1, gs * P)
    k = np.arange(ns)[:, None, None]
    oh = jnp.asarray((np.arange(gu)[None, :, None] == gs * (k % (gu // gs)) + np.arange(gs)[None, None, :]),
                     F32)
    bre = jnp.einsum('kgl,klph->kghlp', oh, jnp.real(b_bar).reshape(ns, gs, P, H)).reshape(ns, gu * H, gs * P)
    bim = jnp.einsum('kgl,klph->kghlp', oh, jnp.imag(b_bar).reshape(ns, gs, P, H)).reshape(ns, gu * H, gs * P)
    cre = jnp.einsum('kgl,klhp->klpgh', oh, c_re.reshape(ns, gs, H, P)).reshape(ns, gs * P, gu * H)
    cim = jnp.einsum('kgl,klhp->klpgh', oh, c_im.reshape(ns, gs, H, P)).reshape(ns, gs * P, gu * H)
    return lr, li, jnp.concatenate([bre, bim], axis=2), jnp.concatenate([cre, -cim], axis=1)


def _carry_tile(fr, fi, pr, pi, reverse):
    row = lax.broadcasted_iota(jnp.int32, fr.shape, 0)
    cr = jnp.zeros_like(fr)
    ci = jnp.zeros_like(fi)
    sr = jnp.zeros_like(fr[0:1])
    si = jnp.zeros_like(sr)
    order = range(SCAN_CHUNKS - 1, 0, -1) if reverse else range(0, SCAN_CHUNKS - 1)
    for c in order:
        fcr = jnp.sum(jnp.where(row == c, fr, 0.0), axis=0, keepdims=True)
        fci = jnp.sum(jnp.where(row == c, fi, 0.0), axis=0, keepdims=True)
        mr, mi = _cmul(pr, pi, sr, si)
        sr, si = mr + fcr, mi + fci
        nxt = c - 1 if reverse else c + 1
        cr = jnp.where(row == nxt, sr, cr)
        ci = jnp.where(row == nxt, si, ci)
    return cr, ci


def _scan_order_into(dst_ref, src_ref, T):
    for c in range(SCAN_CHUNKS):
        dst_ref[pl.ds(c, T, stride=SCAN_CHUNKS), :] = src_ref[pl.ds(c * T, T), :].astype(F32)


def _s5_fwd(proj4, lr, li, bmat, cmat, d, name):
    _, L, Du = proj4.shape
    ns, _, W2 = bmat.shape
    W = W2 // 2
    T = L // SCAN_CHUNKS
    rb = _pick(L, prefs=(512, 256, 128))
    per = (ns * LANES) // Du

    def body(ut_ref, lr_ref, li_ref, b_ref, c_ref, d_ref, y_ref, sr_ref, si_ref, u_ref):
        k = pl.program_id(0)
        _scan_order_into(u_ref, ut_ref, T)
        for r in range(L // rb):
            rows = pl.ds(r * rb, rb)
            bu = jnp.dot(u_ref[rows, :].astype(BF16), b_ref[...], preferred_element_type=F32)
            sr_ref[rows, :] = bu[:, :W]
            si_ref[rows, :] = bu[:, W:]
        lam_r = jnp.broadcast_to(lr_ref[...], (SUBLANES, W))
        lam_i = jnp.broadcast_to(li_ref[...], (SUBLANES, W))

        def local(t, carry):
            sr, si = carry
            rows = pl.ds(pl.multiple_of(t * SUBLANES, SUBLANES), SUBLANES)
            mr, mi = _cmul(lam_r, lam_i, sr, si)
            sr = mr + sr_ref[rows, :]
            si = mi + si_ref[rows, :]
            sr_ref[rows, :] = sr
            si_ref[rows, :] = si
            return sr, si

        z = jnp.zeros((SUBLANES, W), F32)
        fr, fi = lax.fori_loop(0, T, local, (z, z))
        pr, pi = _cpow(lam_r, lam_i, T)
        cr, ci = _carry_tile(fr, fi, pr[0:1], pi[0:1], reverse=False)

        def fix(t, carry):
            wr, wi = carry
            rows = pl.ds(pl.multiple_of(t * SUBLANES, SUBLANES), SUBLANES)
            ar, ai = _cmul(wr, wi, cr, ci)
            sr_ref[rows, :] += ar
            si_ref[rows, :] += ai
            return _cmul(wr, wi, lam_r, lam_i)

        lax.fori_loop(0, T, fix, (lam_r, lam_i))
        first = (k % per) == 0
        for r in range(L // rb):
            rows = pl.ds(r * rb, rb)
            s = jnp.concatenate([sr_ref[rows, :], si_ref[rows, :]], axis=1).astype(BF16)
            y = jnp.dot(s, c_ref[...], preferred_element_type=F32)

            @pl.when(first)
            def _():
                y_ref[rows, :] = y + d_ref[...] * u_ref[rows, :]

            @pl.when(jnp.logical_not(first))
            def _():
                y_ref[rows, :] += y

    ublk = pl.BlockSpec((L, LANES), lambda k: (0, k // per))
    sblk = pl.BlockSpec((L, W), lambda k: (0, k))
    lam = pl.BlockSpec((None, 1, W), lambda k: (k, 0, 0))
    return pl.pallas_call(
        body, name=name,
        out_shape=(jax.ShapeDtypeStruct((L, Du), F32), jax.ShapeDtypeStruct((L, ns * W), F32),
                   jax.ShapeDtypeStruct((L, ns * W), F32)),
        grid=(ns,),
        in_specs=[pl.BlockSpec((None, L, LANES), lambda k: (3, 0, k // per)), lam, lam,
                  pl.BlockSpec((None, LANES, 2 * W), lambda k: (k, 0, 0)),
                  pl.BlockSpec((None, 2 * W, LANES), lambda k: (k, 0, 0)),
                  pl.BlockSpec((1, LANES), lambda k: (0, k // per))],
        out_specs=(ublk, sblk, sblk), scratch_shapes=[pltpu.VMEM((L, LANES), F32)],
        compiler_params=_cparams(("arbitrary",), VMEM_LIMIT_S5),
    )(proj4, lr, li, bmat.astype(BF16), cmat.astype(BF16), d.reshape(1, Du))


def _s5_bwd(dy, proj4, dproj, s_re, s_im, lr, li, bmat, cmat, d, name):
    _, L, Du = proj4.shape
    ns, _, W2 = bmat.shape
    W = W2 // 2
    T = L // SCAN_CHUNKS
    rb = _pick(L, prefs=(512, 256, 128))
    per = (ns * LANES) // Du
    NT = (((1,), (1,)), ((), ()))
    TN = (((0,), (0,)), ((), ()))

    def body(dy_ref, ut_ref, dp_in, sr_ref, si_ref, lr_ref, li_ref, b_ref, c_ref, d_ref,
             dut_ref, db_ref, dc_ref, dl_ref, dd_ref, gr_ref, gi_ref, u_ref, du_ref):
        k = pl.program_id(0)
        _scan_order_into(u_ref, ut_ref, T)
        for r in range(L // rb):
            rows = pl.ds(r * rb, rb)
            g = lax.dot_general(dy_ref[rows, :].astype(BF16), c_ref[...], NT, preferred_element_type=F32)
            gr_ref[rows, :] = g[:, :W]
            gi_ref[rows, :] = g[:, W:]
        lam_r = jnp.broadcast_to(lr_ref[...], (SUBLANES, W))
        lam_i = -jnp.broadcast_to(li_ref[...], (SUBLANES, W))

        def local(i, carry):
            gr, gi = carry
            rows = pl.ds(pl.multiple_of((T - 1 - i) * SUBLANES, SUBLANES), SUBLANES)
            mr, mi = _cmul(lam_r, lam_i, gr, gi)
            gr = mr + gr_ref[rows, :]
            gi = mi + gi_ref[rows, :]
            gr_ref[rows, :] = gr
            gi_ref[rows, :] = gi
            return gr, gi

        z = jnp.zeros((SUBLANES, W), F32)
        fr, fi = lax.fori_loop(0, T, local, (z, z))
        pr, pi = _cpow(lam_r, lam_i, T)
        cr, ci = _carry_tile(fr, fi, pr[0:1], pi[0:1], reverse=True)

        def true_g(rows, wr, wi):
            ar, ai = _cmul(wr, wi, cr, ci)
            gr = gr_ref[rows, :] + ar
            gi = gi_ref[rows, :] + ai
            gr_ref[rows, :] = gr
            gi_ref[rows, :] = gi
            return gr, gi

        def fix(i, carry):
            wr, wi, ar_, ai_ = carry
            t = T - 1 - i
            rows = pl.ds(pl.multiple_of(t * SUBLANES, SUBLANES), SUBLANES)
            prev = pl.ds(pl.multiple_of((t - 1) * SUBLANES, SUBLANES), SUBLANES)
            gr, gi = true_g(rows, wr, wi)
            qr, qi = sr_ref[prev, :], si_ref[prev, :]
            ar_ = ar_ + gr * qr + gi * qi
            ai_ = ai_ + gi * qr - gr * qi
            wr, wi = _cmul(wr, wi, lam_r, lam_i)
            return wr, wi, ar_, ai_

        wr, wi, acc_r, acc_i = lax.fori_loop(0, T - 1, fix, (lam_r, lam_i, z, z))
        gr, gi = true_g(pl.ds(0, SUBLANES), wr, wi)
        last = pl.ds((T - 1) * SUBLANES, SUBLANES)
        row = lax.broadcasted_iota(jnp.int32, (SUBLANES, W), 0)
        qr = jnp.where(row >= 1, pltpu.roll(sr_ref[last, :], 1, axis=0), 0.0)
        qi = jnp.where(row >= 1, pltpu.roll(si_ref[last, :], 1, axis=0), 0.0)
        acc_r = acc_r + gr * qr + gi * qi
        acc_i = acc_i + gi * qr - gr * qi
        dl_ref[0:1, :] = jnp.sum(acc_r, axis=0, keepdims=True)
        dl_ref[1:2, :] = jnp.sum(acc_i, axis=0, keepdims=True)

        first = (k % per) == 0
        db = jnp.zeros((LANES, 2 * W), F32)
        dc = jnp.zeros((LANES, 2 * W), F32)
        dd = jnp.zeros((1, LANES), F32)
        for r in range(L // rb):
            rows = pl.ds(r * rb, rb)
            gb = jnp.concatenate([gr_ref[rows, :], gi_ref[rows, :]], axis=1).astype(BF16)
            sb = jnp.concatenate([sr_ref[rows, :], si_ref[rows, :]], axis=1).astype(BF16)
            dyv = dy_ref[rows, :]
            uv = u_ref[rows, :]
            du = lax.dot_general(gb, b_ref[...], NT, preferred_element_type=F32)
            db = db + lax.dot_general(uv.astype(BF16), gb, TN, preferred_element_type=F32)
            dc = dc + lax.dot_general(dyv.astype(BF16), sb, TN, preferred_element_type=F32)
            dd = dd + jnp.sum(dyv * uv, axis=0, keepdims=True)

            @pl.when(first)
            def _():
                du_ref[rows, :] = du + d_ref[...] * dyv

            @pl.when(jnp.logical_not(first))
            def _():
                du_ref[rows, :] += du

        db_ref[...] = db
        dc_ref[...] = dc

        @pl.when(first)
        def _():
            dd_ref[...] = dd

        @pl.when((k % per) == per - 1)
        def _():
            for c in range(SCAN_CHUNKS):
                dut_ref[pl.ds(c * T, T), :] = du_ref[pl.ds(c, T, stride=SCAN_CHUNKS), :].astype(BF16)

    ublk = pl.BlockSpec((L, LANES), lambda k: (0, k // per))
    uslab = pl.BlockSpec((None, L, LANES), lambda k: (3, 0, k // per))
    sblk = pl.BlockSpec((L, W), lambda k: (0, k))
    lam = pl.BlockSpec((None, 1, W), lambda k: (k, 0, 0))
    vec = pl.BlockSpec((1, LANES), lambda k: (0, k // per))
    mat = pl.BlockSpec((None, LANES, 2 * W), lambda k: (k, 0, 0))
    return pl.pallas_call(
        body, name=name,
        out_shape=(jax.ShapeDtypeStruct(dproj.shape, dproj.dtype), jax.ShapeDtypeStruct((ns, LANES, 2 * W), F32),
                   jax.ShapeDtypeStruct((ns, LANES, 2 * W), F32), jax.ShapeDtypeStruct((ns, 2, W), F32),
                   jax.ShapeDtypeStruct((1, Du), F32)),
        grid=(ns,),
        in_specs=[ublk, uslab, pl.BlockSpec(memory_space=pl.ANY), sblk, sblk, lam, lam, mat,
                  pl.BlockSpec((None, 2 * W, LANES), lambda k: (k, 0, 0)), vec],
        out_specs=(uslab, mat, mat, pl.BlockSpec((None, 2, W), lambda k: (k, 0, 0)), vec),
        scratch_shapes=[pltpu.VMEM((L, W), F32), pltpu.VMEM((L, W), F32), pltpu.VMEM((L, LANES), F32),
                        pltpu.VMEM((L, LANES), F32)],
        input_output_aliases={2: 0}, compiler_params=_cparams(("arbitrary",), VMEM_LIMIT_S5),
    )(dy, proj4, dproj, s_re, s_im, lr, li, bmat.astype(BF16), cmat.astype(BF16), d.reshape(1, Du))


def _glu_fwd(yraw, wmat, bias, mixin, name):
    L, C = yraw.shape
    tr = _pick(L, prefs=(512, 256, 128))
    tb = tr // SCAN_CHUNKS
    nl = C // LANES

    def body(y_ref, w_ref, b_ref, m_in, o_ref, scr):
        yg = _gelu(y_ref[...])
        zz = jnp.dot(yg.astype(BF16), w_ref[...], preferred_element_type=F32) + b_ref[...]
        yb = yg * _sigmoid(zz)
        for k in range(nl):
            scr[k] = yb[:, k * LANES:(k + 1) * LANES]
        for c in range(SCAN_CHUNKS):
            for k in range(nl):
                o_ref[c, :, k * LANES:(k + 1) * LANES] = scr[k, pl.ds(c, tb, stride=SCAN_CHUNKS), :].astype(BF16)

    out = pl.pallas_call(
        body, name=name, out_shape=jax.ShapeDtypeStruct((SCAN_CHUNKS, L // SCAN_CHUNKS, 2 * C), BF16),
        grid=(L // tr,),
        in_specs=[pl.BlockSpec((tr, C), lambda i: (i, 0)), pl.BlockSpec((C, C), lambda i: (0, 0)),
                  pl.BlockSpec((1, C), lambda i: (0, 0)), pl.BlockSpec(memory_space=pl.ANY)],
        out_specs=pl.BlockSpec((SCAN_CHUNKS, tb, C), lambda i: (0, i, 1)),
        scratch_shapes=[pltpu.VMEM((nl, tr, LANES), F32)], input_output_aliases={3: 0},
        compiler_params=_cparams(("parallel",)),
    )(yraw, wmat, bias.reshape(1, C), mixin.reshape(SCAN_CHUNKS, L // SCAN_CHUNKS, 2 * C))
    return out.reshape(L, 2 * C)


def _glu_bwd(yraw, dmix, wmat, bias, name):
    L, C = yraw.shape
    tr = _pick(L, prefs=(512, 256, 128))
    nsteps = L // tr
    tb = tr // SCAN_CHUNKS
    nl = C // LANES

    def body(y_ref, d_ref, w_ref, b_ref, dy_ref, dw_ref, db_ref, acc_b, scr):
        i = pl.program_id(0)

        @pl.when(i == 0)
        def _():
            dw_ref[...] = jnp.zeros_like(dw_ref)
            acc_b[...] = jnp.zeros_like(acc_b)

        for c in range(SCAN_CHUNKS):
            for k in range(nl):
                scr[k, pl.ds(c, tb, stride=SCAN_CHUNKS), :] = d_ref[c, :, k * LANES:(k + 1) * LANES]
        yr = y_ref[...]
        yg = _gelu(yr)
        ygb = yg.astype(BF16)
        sg = _sigmoid(jnp.dot(ygb, w_ref[...], preferred_element_type=F32) + b_ref[...])
        dyb_ = jnp.concatenate([scr[k] for k in range(nl)], axis=1)
        dz = dyb_ * yg * sg * (1.0 - sg)
        dzb = dz.astype(BF16)
        dyg = dyb_ * sg + lax.dot_general(dzb, w_ref[...], (((1,), (1,)), ((), ())), preferred_element_type=F32)
        dw_ref[...] += lax.dot_general(ygb, dzb, (((0,), (0,)), ((), ())), preferred_element_type=F32)
        acc_b[...] += jnp.sum(dz.reshape(tr // SUBLANES, SUBLANES, C), axis=0)
        dy_ref[...] = dyg * _gelu_grad(yr)

        @pl.when(i == nsteps - 1)
        def _():
            db_ref[...] = jnp.sum(acc_b[...], axis=0, keepdims=True)

    row = pl.BlockSpec((tr, C), lambda i: (i, 0))
    return pl.pallas_call(
        body, name=name,
        out_shape=(jax.ShapeDtypeStruct((L, C), F32), jax.ShapeDtypeStruct((C, C), F32),
                   jax.ShapeDtypeStruct((1, C), F32)),
        grid=(nsteps,),
        in_specs=[row, pl.BlockSpec((SCAN_CHUNKS, tb, C), lambda i: (0, i, 1)), pl.BlockSpec((C, C), lambda i: (0, 0)),
                  pl.BlockSpec((1, C), lambda i: (0, 0))],
        out_specs=(row, pl.BlockSpec((C, C), lambda i: (0, 0)), pl.BlockSpec((1, C), lambda i: (0, 0))),
        scratch_shapes=[pltpu.VMEM((SUBLANES, C), F32), pltpu.VMEM((nl, tr, LANES), F32)],
        compiler_params=_cparams(("arbitrary",)),
    )(yraw, dmix.reshape(SCAN_CHUNKS, L // SCAN_CHUNKS, 2 * C), wmat, bias.reshape(1, C))


def _pool_counts(L, g):
    t = lax.broadcasted_iota(jnp.int32, (L, LANES), 0).astype(F32) + 1.0
    w = jnp.where(g == 0, 2.0, jnp.where(g == 1, 4.0, jnp.where(g == 2, 8.0, 16.0)))
    return 1.0 / jnp.minimum(t, w)


def _select_window(g, a2, a4, a8, a16):
    return jnp.where(g == 0, a2, jnp.where(g == 1, a4, jnp.where(g == 2, a8, a16)))


def _pooled(z, g):
    a2 = z + _down(z, 1)
    a4 = a2 + _down(a2, 2)
    a8 = a4 + _down(a4, 4)
    a16 = a8 + _down(a8, 8)
    return _select_window(g, a2, a4, a8, a16) * _pool_counts(z.shape[0], g) - z


def _transpose_on_mxu(yb):
    c = yb.shape[1]
    eye = lax.broadcasted_iota(jnp.int32, (c, c), 0) == lax.broadcasted_iota(jnp.int32, (c, c), 1)
    return lax.dot_general(eye.astype(BF16), yb, (((1,), (1,)), ((), ())), preferred_element_type=F32).astype(BF16)


def _pool_fwd(proj3, pool_w, scale, name):
    _, L, C = proj3.shape
    ng = len(POOL_WINDOWS)
    pg = C // ng
    assert pg == LANES

    def body(z_ref, w_ref, s_ref, o_ref, ot_ref):
        g = pl.program_id(0)
        p = _pooled(z_ref[...].astype(F32), g)
        y = jnp.dot(p.astype(BF16), w_ref[...].astype(BF16), preferred_element_type=F32)
        yb = (y * s_ref[...]).astype(BF16)
        o_ref[...] = yb
        ot_ref[...] = _transpose_on_mxu(yb)

    return pl.pallas_call(
        body, name=name, out_shape=(jax.ShapeDtypeStruct((L, 2 * C), BF16), jax.ShapeDtypeStruct((2 * C, L), BF16)),
        grid=(ng,),
        in_specs=[pl.BlockSpec((None, L, pg), lambda g: (0, 0, g)), pl.BlockSpec((None, pg, pg), lambda g: (g, 0, 0)),
                  pl.BlockSpec((1, pg), lambda g: (0, g))],
        out_specs=(pl.BlockSpec((L, pg), lambda g: (0, g)), pl.BlockSpec((pg, L), lambda g: (g, 0))),
        compiler_params=_cparams(("parallel",)),
    )(proj3, pool_w, scale.reshape(1, C))


def _pool_bwd(proj3, dmix, pool_w, scale, name):
    _, L, C = proj3.shape
    ng = len(POOL_WINDOWS)
    pg = C // ng

    def body(z_ref, d_ref, w_ref, s_ref, dz_ref, dw_ref, ds_ref):
        g = pl.program_id(0)
        p = _pooled(z_ref[...].astype(F32), g)
        pb = p.astype(BF16)
        wb = w_ref[...].astype(BF16)
        pre = jnp.dot(pb, wb, preferred_element_type=F32)
        dyc = d_ref[...]
        ds_ref[...] = jnp.sum(dyc * pre, axis=0, keepdims=True)
        dpre = (dyc * s_ref[...]).astype(BF16)
        dw_ref[...] = lax.dot_general(pb, dpre, (((0,), (0,)), ((), ())), preferred_element_type=F32)
        dp = lax.dot_general(dpre, wb, (((1,), (1,)), ((), ())), preferred_element_type=F32)
        v = dp * _pool_counts(L, g)
        a2 = v + _up(v, 1)
        a4 = a2 + _up(a2, 2)
        a8 = a4 + _up(a4, 4)
        a16 = a8 + _up(a8, 8)
        dz_ref[...] = (_select_window(g, a2, a4, a8, a16) - dp).astype(BF16)

    return pl.pallas_call(
        body, name=name,
        out_shape=(jax.ShapeDtypeStruct((L, C), BF16), jax.ShapeDtypeStruct((ng, pg, pg), F32),
                   jax.ShapeDtypeStruct((1, C), F32)),
        grid=(ng,),
        in_specs=[pl.BlockSpec((None, L, pg), lambda g: (0, 0, g)), pl.BlockSpec((L, pg), lambda g: (0, g)),
                  pl.BlockSpec((None, pg, pg), lambda g: (g, 0, 0)), pl.BlockSpec((1, pg), lambda g: (0, g))],
        out_specs=(pl.BlockSpec((L, pg), lambda g: (0, g)), pl.BlockSpec((None, pg, pg), lambda g: (g, 0, 0)),
                   pl.BlockSpec((1, pg), lambda g: (0, g))),
        compiler_params=_cparams(("parallel",)),
    )(proj3, dmix, pool_w, scale.reshape(1, C))


def _tril_w(w_ref, h):
    r = lax.broadcasted_iota(jnp.int32, (CHUNK, CHUNK), 0)
    c = lax.broadcasted_iota(jnp.int32, (CHUNK, CHUNK), 1)
    return jnp.where(r >= c, w_ref[h], 0.0)


def _sgu_fwd(proj3, norm_g, w, b, mixin, mixin_t, name):
    _, L, C = proj3.shape
    nh = w.shape[0]
    dh = C // nh
    assert dh == LANES and w.shape[1] == CHUNK
    tr = _pick(L, prefs=(512, 256, 128))
    bfull = jnp.broadcast_to(b[:, :, None], (nh, CHUNK, dh))

    def body(su_ref, sv_ref, g_ref, w_ref, b_ref, m_in, mt_in, o_ref, ot_ref):
        sv = _gelu(sv_ref[...].astype(F32))
        r = lax.rsqrt(jnp.mean(sv * sv, axis=-1, keepdims=True) + EPS)
        v = (sv * r * g_ref[...]).astype(BF16)
        for h in range(nh):
            wm = _tril_w(w_ref, h).astype(BF16)
            cols = slice(h * dh, (h + 1) * dh)
            for n in range(tr // CHUNK):
                rows = slice(n * CHUNK, (n + 1) * CHUNK)
                mixed = jnp.dot(wm, v[rows, cols], preferred_element_type=F32) + b_ref[h]
                o_ref[rows, cols] = (_gelu(su_ref[rows, cols].astype(F32)) * mixed).astype(BF16)
        ot_ref[...] = _transpose_on_mxu(o_ref[...])

    full = lambda shp: pl.BlockSpec(shp, lambda i: (0,) * len(shp))
    anywhere = pl.BlockSpec(memory_space=pl.ANY)
    return pl.pallas_call(
        body, name=name, out_shape=(jax.ShapeDtypeStruct(mixin.shape, BF16), jax.ShapeDtypeStruct(mixin_t.shape, BF16)),
        grid=(L // tr,),
        in_specs=[pl.BlockSpec((None, tr, C), lambda i: (1, i, 0)), pl.BlockSpec((None, tr, C), lambda i: (2, i, 0)),
                  full((1, C)), full((nh, CHUNK, CHUNK)), full((nh, CHUNK, dh)), anywhere, anywhere],
        out_specs=(pl.BlockSpec((tr, C), lambda i: (i, 1)), pl.BlockSpec((C, tr), lambda i: (1, i))),
        input_output_aliases={5: 0, 6: 1}, compiler_params=_cparams(("parallel",)),
    )(proj3, proj3, norm_g.reshape(1, C), w, bfull, mixin, mixin_t)


def _sgu_bwd(proj3, dmix, dz, norm_g, w, b, name):
    _, L, C = proj3.shape
    nh = w.shape[0]
    dh = C // nh
    tr = _pick(L, prefs=(512, 256, 128))
    nsteps = L // tr
    bfull = jnp.broadcast_to(b[:, :, None], (nh, CHUNK, dh))

    def body(su_ref, sv_ref, d_ref, dz_ref, g_ref, w_ref, b_ref, o_ref, dw_ref, db_ref, dg_ref, dv_ref, acc_g):
        i = pl.program_id(0)
        o_ref[0] = dz_ref[...]

        @pl.when(i == 0)
        def _():
            dw_ref[...] = jnp.zeros_like(dw_ref)
            db_ref[...] = jnp.zeros_like(db_ref)
            acc_g[...] = jnp.zeros_like(acc_g)

        svp = sv_ref[...].astype(F32)
        sv = _gelu(svp)
        r = lax.rsqrt(jnp.mean(sv * sv, axis=-1, keepdims=True) + EPS)
        vh = sv * r
        gv = g_ref[...]
        v = (vh * gv).astype(BF16)
        tri_r = lax.broadcasted_iota(jnp.int32, (CHUNK, CHUNK), 0)
        tri_c = lax.broadcasted_iota(jnp.int32, (CHUNK, CHUNK), 1)
        for h in range(nh):
            wm = _tril_w(w_ref, h).astype(BF16)
            cols = slice(h * dh, (h + 1) * dh)
            dwh = jnp.zeros((CHUNK, CHUNK), F32)
            dbh = jnp.zeros((CHUNK, dh), F32)
            for n in range(tr // CHUNK):
                rows = slice(n * CHUNK, (n + 1) * CHUNK)
                vb = v[rows, cols]
                mixed = jnp.dot(wm, vb, preferred_element_type=F32) + b_ref[h]
                sup = su_ref[rows, cols].astype(F32)
                dyd = d_ref[rows, cols]
                dmx = dyd * _gelu(sup)
                o_ref[1, rows, cols] = (dyd * mixed * _gelu_grad(sup)).astype(BF16)
                dmb = dmx.astype(BF16)
                dwh = dwh + lax.dot_general(dmb, vb, (((1,), (1,)), ((), ())), preferred_element_type=F32)
                dbh = dbh + dmx
                dv_ref[rows, cols] = lax.dot_general(wm, dmb, (((0,), (0,)), ((), ())), preferred_element_type=F32)
            dw_ref[h] += jnp.where(tri_r >= tri_c, dwh, 0.0)
            db_ref[h] += dbh
        dv = dv_ref[...]
        acc_g[...] += jnp.sum((dv * vh).reshape(tr // SUBLANES, SUBLANES, C), axis=0)
        dvg = dv * gv
        dsv = r * (dvg - vh * jnp.mean(dvg * vh, axis=-1, keepdims=True))
        o_ref[2] = (dsv * _gelu_grad(svp)).astype(BF16)

        @pl.when(i == nsteps - 1)
        def _():
            dg_ref[...] = jnp.sum(acc_g[...], axis=0, keepdims=True)

    full = lambda shp: pl.BlockSpec(shp, lambda i: (0,) * len(shp))
    return pl.pallas_call(
        body, name=name,
        out_shape=(jax.ShapeDtypeStruct((3, L, C), BF16), jax.ShapeDtypeStruct((nh, CHUNK, CHUNK), F32),
                   jax.ShapeDtypeStruct((nh, CHUNK, dh), F32), jax.ShapeDtypeStruct((1, C), F32)),
        grid=(nsteps,),
        in_specs=[pl.BlockSpec((None, tr, C), lambda i: (1, i, 0)), pl.BlockSpec((None, tr, C), lambda i: (2, i, 0)),
                  pl.BlockSpec((tr, C), lambda i: (i, 1)), pl.BlockSpec((tr, C), lambda i: (i, 0)), full((1, C)),
                  full((nh, CHUNK, CHUNK)), full((nh, CHUNK, dh))],
        out_specs=(pl.BlockSpec((3, tr, C), lambda i: (0, i, 0)), full((nh, CHUNK, CHUNK)), full((nh, CHUNK, dh)),
                   full((1, C))),
        scratch_shapes=[pltpu.VMEM((tr, C), F32), pltpu.VMEM((SUBLANES, C), F32)],
        compiler_params=_cparams(("arbitrary",)),
    )(proj3, proj3, dmix, dz, norm_g.reshape(1, C), w, bfull)


def _ffn_act_fwd(up3, conv_w, conv_b, name):
    _, L, Fh = up3.shape
    cb = LANES
    w2 = conv_w.reshape(3, 2, Fh).transpose(1, 0, 2)
    b2 = conv_b.reshape(2, 1, Fh)

    def body(u_ref, w_ref, b_ref, o_ref, ot_ref, gv_ref):
        g = _conv3(_taps(u_ref[0].astype(F32)), w_ref[0]) + b_ref[0]
        v = _conv3(_taps(u_ref[1].astype(F32)), w_ref[1]) + b_ref[1]
        gv_ref[0] = g.astype(BF16)
        gv_ref[1] = v.astype(BF16)
        ab = (g * _sigmoid(g) * v).astype(BF16)
        o_ref[...] = ab
        ot_ref[...] = _transpose_on_mxu(ab)

    blk3 = pl.BlockSpec((2, L, cb), lambda j: (0, 0, j))
    return pl.pallas_call(
        body, name=name,
        out_shape=(jax.ShapeDtypeStruct((L, Fh), BF16), jax.ShapeDtypeStruct((Fh, L), BF16),
                   jax.ShapeDtypeStruct((2, L, Fh), BF16)),
        grid=(Fh // cb,),
        in_specs=[blk3, pl.BlockSpec((2, 3, cb), lambda j: (0, 0, j)), pl.BlockSpec((2, 1, cb), lambda j: (0, 0, j))],
        out_specs=(pl.BlockSpec((L, cb), lambda j: (0, j)), pl.BlockSpec((cb, L), lambda j: (j, 0)), blk3),
        compiler_params=_cparams(("parallel",)),
    )(up3, w2, b2)


def _ffn_act_bwd(up3, gv3, da, conv_w, h2t, at, dxb, name):
    _, L, Fh = up3.shape
    D = h2t.shape[0]
    cb = LANES
    nb = Fh // cb
    w2 = conv_w.reshape(3, 2, Fh).transpose(1, 0, 2)

    def body(u_ref, gv_ref, d_ref, w_ref, h_ref, at_ref, dx_ref, o_ref, dw_ref, db_ref, wg_ref, wv_ref, wd_ref, scr):
        j = pl.program_id(0)

        @pl.when(j == 0)
        def _():
            scr[1] = jnp.zeros((2, L, cb), BF16)

        wd_ref[...] = jnp.dot(at_ref[...], dx_ref[...], preferred_element_type=F32).astype(BF16)
        prev = scr.at[(j + 1) % 2]
        wg_ref[...] = jnp.dot(h_ref[...], prev[0], preferred_element_type=F32).astype(BF16)
        wv_ref[...] = jnp.dot(h_ref[...], prev[1], preferred_element_type=F32).astype(BF16)
        tg, tv = _taps(u_ref[0].astype(F32)), _taps(u_ref[1].astype(F32))
        wg, wv = w_ref[0], w_ref[1]
        g = gv_ref[0].astype(F32)
        v = gv_ref[1].astype(F32)
        sg = _sigmoid(g)
        dav = d_ref[...].astype(F32)
        dg = dav * v * (sg * (1.0 + g * (1.0 - sg)))
        dv = dav * (g * sg)
        dug = _conv3_t(dg, wg).astype(BF16)
        duv = _conv3_t(dv, wv).astype(BF16)
        o_ref[0] = dug
        o_ref[1] = duv
        cur = scr.at[j % 2]
        cur[0] = dug
        cur[1] = duv
        for tap, (dwg, dwv) in enumerate(zip(_conv3_dw(dg, tg), _conv3_dw(dv, tv))):
            dw_ref[0, tap:tap + 1, :] = dwg
            dw_ref[1, tap:tap + 1, :] = dwv
        db_ref[0] = jnp.sum(dg, axis=0, keepdims=True)
        db_ref[1] = jnp.sum(dv, axis=0, keepdims=True)

    here = lambda j: jnp.minimum(j, nb - 1)
    before = lambda j: jnp.maximum(j - 1, 0)
    blk3 = pl.BlockSpec((2, L, cb), lambda j: (0, 0, here(j)))
    dup, dw2, db2, dwg, dwv, dwd = pl.pallas_call(
        body, name=name,
        out_shape=(jax.ShapeDtypeStruct((2, L, Fh), BF16), jax.ShapeDtypeStruct((2, 3, Fh), F32),
                   jax.ShapeDtypeStruct((2, 1, Fh), F32), jax.ShapeDtypeStruct((D, Fh), BF16),
                   jax.ShapeDtypeStruct((D, Fh), BF16), jax.ShapeDtypeStruct((Fh, D), BF16)),
        grid=(nb + 1,),
        in_specs=[blk3, blk3, pl.BlockSpec((L, cb), lambda j: (0, here(j))),
                  pl.BlockSpec((2, 3, cb), lambda j: (0, 0, here(j))), pl.BlockSpec((D, L), lambda j: (0, 0)),
                  pl.BlockSpec((cb, L), lambda j: (here(j), 0)), pl.BlockSpec((L, D), lambda j: (0, 0))],
        out_specs=(blk3, pl.BlockSpec((2, 3, cb), lambda j: (0, 0, here(j))),
                   pl.BlockSpec((2, 1, cb), lambda j: (0, 0, here(j))),
                   pl.BlockSpec((D, cb), lambda j: (0, before(j))), pl.BlockSpec((D, cb), lambda j: (0, before(j))),
                   pl.BlockSpec((cb, D), lambda j: (here(j), 0))),
        scratch_shapes=[pltpu.VMEM((2, 2, L, cb), BF16)],
        compiler_params=_cparams(("arbitrary",), VMEM_LIMIT_S5),
    )(up3, gv3, da, w2, h2t, at, dxb)
    return (dup, dw2.transpose(1, 0, 2).reshape(3, 2 * Fh), db2.reshape(2 * Fh), jnp.concatenate([dwg, dwv], axis=1),
            dwd)


def _local_step(x, tgt, w, layer_weights, on_layer_grads):
    L, D = x.shape
    depth = w['norm_mix_g'].shape[0]
    saved = []
    for i in range(depth):
        j = i // 2
        wb = dict(layer_weights(2 * i, x))
        s = {'x': x, 'wb': wb}
        if i % 2 == 0:
            proj4, s['hT'] = _norm_mm(x, w['norm_mix_g'][i], wb['even_w_in'], BF16, "even_in_fwd", ok=('seg', 4))
            s['proj'] = proj4
            mixin = _sconv_fwd(proj4, w['even_conv_w'][j], "sconv_fwd")
            prm = (w['ssm_log_step'][j], w['ssm_a_re'][j], w['ssm_a_im'][j], w['ssm_b_re'][j], w['ssm_b_im'][j],
                   w['ssm_c_re'][j], w['ssm_c_im'][j])
            (lr, li, bmat, cmat), prep_vjp = jax.vjp(_s5_prep, *prm)
            yraw, s_re, s_im = _s5_fwd(proj4, lr, li, bmat, cmat, w['ssm_d'][j], "s5_fwd")
            mixin = _glu_fwd(yraw, wb['ssm_glu_w'], w['ssm_glu_b'][j], mixin, "glu_fwd")
            s.update(yraw=yraw, s_re=s_re, s_im=s_im, s5=(lr, li, bmat, cmat), prep_vjp=prep_vjp)
            s['mixinT'] = mixin.T
            x = _mm(mixin, wb['even_w_out'], 'nn', F32, "even_out_fwd", res=x)
        else:
            proj3, s['hT'] = _norm_mm(x, w['norm_mix_g'][i], wb['odd_w_in'], BF16, "odd_in_fwd", ok=('seg', 3))
            s['proj'] = proj3
            mixin, mixin_t = _pool_fwd(proj3, w['pool_w'][j], w['pool_scale'][j], "pool_fwd")
            mixin, s['mixinT'] = _sgu_fwd(proj3, w['sgu_norm_g'][j], w['sgu_w'][j], w['sgu_b'][j], mixin, mixin_t,
                                          "sgu_fwd")
            x = _mm(mixin, wb['odd_w_out'], 'nn', F32, "odd_out_fwd", res=x)
        s['x1'] = x
        wb.update(layer_weights(2 * i + 1, x))
        up3, h2t = _norm_mm(x, w['norm_ffn_g'][i], wb['ffn_w_up'], BF16, "ffn_up_fwd", ok=('seg', 2))
        a, at, gv3 = _ffn_act_fwd(up3, w['ffn_conv_w'][i], w['ffn_conv_b'][i], "ffn_act_fwd")
        x = _mm(a, wb['ffn_w_down'], 'nn', F32, "ffn_down_fwd", res=x)
        s.update(h2T=h2t, up3=up3, aT=at, gv3=gv3)
        saved.append(s)

    loss8, dx, dxb, dg_final = _loss_head(x, w['norm_final_g'], tgt)
    gs = {n: [None] * w[n].shape[0] for n in SMALL if n != 'norm_final_g'}
    gs['norm_final_g'] = dg_final.reshape(D)

    dep = None
    for i in reversed(range(depth)):
        j = i // 2
        s = saved[i]
        wb = s['wb']
        gb = {}
        da = _mm(dxb, wb['ffn_w_down'], 'nt', BF16, "ffn_down_dgrad", dep=dep)
        dup3, dcw, dcb, gb['ffn_w_up'], gb['ffn_w_down'] = _ffn_act_bwd(
            s['up3'], s['gv3'], da, w['ffn_conv_w'][i], s['h2T'], s['aT'], dxb, "ffn_act_bwd")
        gs['ffn_conv_w'][i], gs['ffn_conv_b'][i] = dcw, dcb
        dep = on_layer_grads(2 * i + 1, gb)
        dx, dxb, dg = _mm_norm_bwd(dup3, wb['ffn_w_up'], s['x1'], w['norm_ffn_g'][i], dx, "ffn_up_dgrad",
                              ak=('seg', 2), dep=dep)
        gs['norm_ffn_g'][i] = dg.reshape(D)
        gb = {}
        if i % 2 == 0:
            dmix = _mm(dxb, wb['even_w_out'], 'nt', F32, "even_out_dgrad")
            gb['even_w_out'] = _mm(s['mixinT'], dxb, 'nn', BF16, "even_out_wgrad")
            dproj, dcw = _sconv_bwd(s['proj'], dmix, w['even_conv_w'][j], "sconv_bwd")
            gs['even_conv_w'][j] = dcw
            dyraw, dglu_w, dglu_b = _glu_bwd(s['yraw'], dmix, wb['ssm_glu_w'], w['ssm_glu_b'][j], "glu_bwd")
            gb['ssm_glu_w'] = dglu_w.astype(BF16)
            gs['ssm_glu_b'][j] = dglu_b.reshape(-1)
            lr, li, bmat, cmat = s['s5']
            dproj, dbm, dcm, dlam, dd = _s5_bwd(dyraw, s['proj'], dproj, s['s_re'], s['s_im'], lr, li, bmat, cmat,
                                               w['ssm_d'][j], "s5_bwd")
            gs['ssm_d'][j] = dd.reshape(-1)
            dcm = jnp.swapaxes(dcm, 1, 2)
            dprm = s['prep_vjp']((dlam[:, 0:1, :], dlam[:, 1:2, :], dbm, dcm))
            for n, gval in zip(('ssm_log_step', 'ssm_a_re', 'ssm_a_im', 'ssm_b_re', 'ssm_b_im', 'ssm_c_re',
                                'ssm_c_im'), dprm):
                gs[n][j] = gval
            gb['even_w_in'] = _mm(s['hT'], dproj, 'nn', BF16, "even_in_wgrad", bk=('seg', 4))
            w_in, in_kind, in_name = wb['even_w_in'], ('seg', 4), "even_in_dgrad"
        else:
            dmix = _mm(dxb, wb['odd_w_out'], 'nt', F32, "odd_out_dgrad")
            gb['odd_w_out'] = _mm(s['mixinT'], dxb, 'nn', BF16, "odd_out_wgrad")
            dz, dpw, dps = _pool_bwd(s['proj'], dmix, w['pool_w'][j], w['pool_scale'][j], "pool_bwd")
            gs['pool_w'][j], gs['pool_scale'][j] = dpw, dps.reshape(-1)
            dproj, dsw, dsb, dsg = _sgu_bwd(s['proj'], dmix, dz, w['sgu_norm_g'][j], w['sgu_w'][j], w['sgu_b'][j],
                                            "sgu_bwd")
            gs['sgu_w'][j], gs['sgu_b'][j], gs['sgu_norm_g'][j] = dsw, jnp.sum(dsb, axis=-1), dsg.reshape(-1)
            gb['odd_w_in'] = _mm(s['hT'], dproj, 'nn', BF16, "odd_in_wgrad", bk=('seg', 3))
            w_in, in_kind, in_name = wb['odd_w_in'], ('seg', 3), "odd_in_dgrad"
        dep = on_layer_grads(2 * i, gb)
        dx, dxb, dg = _mm_norm_bwd(dproj, w_in, s['x'], w['norm_mix_g'][i], dx, in_name, ak=in_kind, dep=dep)
        gs['norm_mix_g'][i] = dg.reshape(D)

    gsmall = {n: (v if n == 'norm_final_g' else jnp.stack(v)) for n, v in gs.items()}
    return loss8[0, 0], dx, gsmall


_HBM = pl.BlockSpec(memory_space=pltpu.HBM)
_CHIP_FLIPS = ((0, 0), (1, 0), (0, 1), (1, 1))


def _coords():
    return lax.axis_index("x"), lax.axis_index("y"), lax.axis_index("c")


def _flip(v, f):
    return 1 - v if f else v


def _shard_of(ref, axis, s, width):
    start = pl.multiple_of(s * width, LANES if axis == ref.ndim - 1 else 16) if width % 16 == 0 else s * width
    idx = [slice(None)] * ref.ndim
    idx[axis] = pl.ds(start, width)
    return ref.at[tuple(idx)]


_SEM = pl.BlockSpec(memory_space=pltpu.SEMAPHORE)
_ANY = pl.BlockSpec(memory_space=pl.ANY)
_DATAFLOW = pltpu.SideEffectType.DATAFLOW_SIDE_EFFECTING


def _in_hbm(a):
    return pltpu.with_memory_space_constraint(a, pltpu.HBM)


def _model_layer(name, l):
    if name.startswith('ffn'):
        return l
    return 2 * l + 1 if name.startswith('odd') else 2 * l


def _place_quarter(shard, l, axis, chip, dtype, dep=None):
    _, r, c = shard.shape
    tr = _pick(r, prefs=(512, 256, 128, 64, 32, 16))
    nrb = r // tr

    def body(chip_ref, i_ref, *rest):
        rest[-1][...] = i_ref[...].astype(dtype)

    if axis == 1:
        out_shape, o_map = (r, c * N_CHIPS), (lambda i, s: (i, s[0]))
    else:
        out_shape, o_map = (r * N_CHIPS, c), (lambda i, s: (s[0] * nrb + i, 0))
    in_specs = [pl.BlockSpec((None, tr, c), lambda i, s: (l, i, 0))]
    args = [chip, shard]
    if dep is not None:
        in_specs.append(pl.BlockSpec(memory_space=pl.ANY))
        args.append(dep)
    return pl.pallas_call(
        body, name="place_quarter", out_shape=jax.ShapeDtypeStruct(out_shape, dtype),
        grid_spec=pltpu.PrefetchScalarGridSpec(
            num_scalar_prefetch=1, grid=(nrb,), in_specs=in_specs, out_specs=pl.BlockSpec((tr, c), o_map)),
        compiler_params=_cparams(("parallel",)),
    )(*args)


def _gather_copies(land_refs, send_sem, recv_sem, axes, landing_chip_of):
    x, y, c = _coords()
    out = []
    for j, land in enumerate(land_refs):
        width = land.shape[axes[j]] // N_CHIPS
        for f in (1, 2, 3):
            fx, fy = _CHIP_FLIPS[f]
            px, py = _flip(x, fx), _flip(y, fy)
            lx, ly = landing_chip_of(px, py)
            out.append(pltpu.make_async_remote_copy(
                src_ref=_shard_of(land, axes[j], 2 * x + y, width), dst_ref=_shard_of(land, axes[j], 2 * lx + ly, width),
                send_sem=send_sem.at[3 * j + f - 1], recv_sem=recv_sem.at[3 * j + f - 1],
                device_id=(px, py, c), device_id_type=MESH))
    return out


def _gather_start(tag, lands, axes, dep=None):
    n = len(lands)

    def body(*refs):
        land_refs, send_sem, recv_sem = refs[:n], refs[-3], refs[-2]
        x, y, _ = _coords()
        for cp in _gather_copies(land_refs, send_sem, recv_sem, axes, lambda px, py: (x, y)):
            cp.start()
        refs[-1][...] = jnp.zeros_like(refs[-1])

    thru = [pltpu.HBM(a.shape, a.dtype) for a in lands]
    outs = pl.pallas_call(
        body, name=f"gather_start_{tag}",
        out_shape=tuple(thru + [pltpu.SemaphoreType.DMA((3 * n,)), pltpu.SemaphoreType.DMA((3 * n,)),
                                jax.ShapeDtypeStruct((SUBLANES, LANES), F32)]),
        in_specs=[_HBM] * n + ([_ANY] if dep is not None else []),
        out_specs=tuple([_HBM] * n + [_SEM, _SEM, pl.BlockSpec(memory_space=pltpu.VMEM)]),
        input_output_aliases={i: i for i in range(n)},
        compiler_params=pltpu.CompilerParams(has_side_effects=_DATAFLOW),
    )(*[_in_hbm(a) for a in lands], *([dep] if dep is not None else []))
    return list(outs[:n]), outs[n], outs[n + 1], outs[n + 2]


def _gather_wait(tag, lands, send_sem, recv_sem, axes, after):
    n = len(lands)

    def body(*refs):
        for cp in _gather_copies(refs[:n], refs[n], refs[n + 1], axes, lambda px, py: (px, py)):
            cp.wait_send()
            cp.wait_recv()

    outs = pl.pallas_call(
        body, name=f"gather_wait_{tag}", out_shape=tuple(pltpu.HBM(a.shape, a.dtype) for a in lands),
        in_specs=[_HBM] * n + [_SEM, _SEM, _ANY], out_specs=tuple([_HBM] * n),
        input_output_aliases={i: i for i in range(n)},
        compiler_params=pltpu.CompilerParams(has_side_effects=_DATAFLOW),
    )(*lands, send_sem, recv_sem, after)
    return list(outs)


N_SLOTS = N_DEV - 1


def _scatter_sends(grad_refs, land_refs, send_sem, recv_sem, meta):
    x, y, c = _coords()
    out = []
    for j, (axis, owner, q, width) in enumerate(meta):
        other = c if owner == 0 else 1 - c
        for f, (fx, fy) in enumerate(_CHIP_FLIPS):
            px, py = _flip(x, fx), _flip(y, fy)
            slot = f + 4 * other - 1
            out.append((other if f == 0 else None, pltpu.make_async_remote_copy(
                src_ref=_shard_of(grad_refs[j], axis, 2 * px + py, width), dst_ref=land_refs[j].at[q, slot],
                send_sem=send_sem.at[4 * j + f], recv_sem=recv_sem.at[N_SLOTS * j + slot],
                device_id=(px, py, owner), device_id_type=MESH)))
    return out


def _scatter_start(layer, grads, lands, meta):
    n = len(grads)
    uniq = []
    for a in lands:
        if not any(a is u for u in uniq):
            uniq.append(a)
    which = [next(k for k, u in enumerate(uniq) if u is a) for a in lands]
    nu = len(uniq)

    def body(*refs):
        grad_refs, land_u = refs[:n], refs[n:n + nu]
        send_sem, recv_sem = refs[n + nu], refs[n + nu + 1]
        for other, cp in _scatter_sends(grad_refs, [land_u[k] for k in which], send_sem, recv_sem, meta):
            if other is None:
                cp.start()
            else:
                pl.when(other == 1)(cp.start)
        refs[-1][...] = jnp.zeros_like(refs[-1])

    thru = [pltpu.HBM(a.shape, a.dtype) for a in list(grads) + uniq]
    outs = pl.pallas_call(
        body, name=f"scatter_start_{layer}",
        out_shape=tuple([pltpu.SemaphoreType.DMA((4 * n,)), pltpu.SemaphoreType.DMA((N_SLOTS * n,))] + thru
                        + [jax.ShapeDtypeStruct((SUBLANES, LANES), F32)]),
        in_specs=[_HBM] * (n + nu),
        out_specs=tuple([_SEM, _SEM] + [_HBM] * (n + nu) + [pl.BlockSpec(memory_space=pltpu.VMEM)]),
        input_output_aliases={i: 2 + i for i in range(n + nu)},
        compiler_params=pltpu.CompilerParams(has_side_effects=_DATAFLOW),
    )(*[_in_hbm(a) for a in list(grads) + uniq])
    new_lands = [outs[2 + n + k] for k in which]
    return outs[0], outs[1], list(outs[2:2 + n]), new_lands, outs[-1]


def _scatter_wait(started, lands):
    nl = len(lands)
    flat_grads = [g for s in started for g in s[2]]
    ng, ns = len(flat_grads), len(started)

    def body(*refs):
        land_refs = refs[:nl]
        grad_refs = refs[nl:nl + ng]
        sem_refs = refs[nl + ng:nl + ng + 2 * ns]
        _, _, c = _coords()
        off = 0
        for k, (_, _, grads, idx, meta) in enumerate(started):
            send_sem, recv_sem = sem_refs[2 * k], sem_refs[2 * k + 1]
            lr = [land_refs[i] for i in idx]
            for other, cp in _scatter_sends(grad_refs[off:off + len(grads)], lr, send_sem, recv_sem, meta):
                if other is None:
                    cp.wait_send()
                else:
                    pl.when(other == 1)(cp.wait_send)
            for j, (axis, owner, q, width) in enumerate(meta):
                mine = (c if owner == 0 else 1 - c) == 0

                @pl.when(mine)
                def _():
                    for slot in range(N_SLOTS):
                        land = lr[j].at[q, slot]
                        pltpu.make_async_remote_copy(
                            src_ref=land, dst_ref=land, send_sem=send_sem.at[0], recv_sem=recv_sem.at[N_SLOTS * j + slot],
                            device_id=_coords(), device_id_type=MESH).wait_recv()
            off += len(grads)

    args = list(lands) + flat_grads
    thru = [pltpu.HBM(a.shape, a.dtype) for a in args]
    sems = [s for st in started for s in st[:2]]
    outs = pl.pallas_call(
        body, name="scatter_wait", out_shape=tuple(thru), in_specs=[_HBM] * (nl + ng) + [_SEM] * (2 * ns),
        out_specs=tuple([_HBM] * (nl + ng)), input_output_aliases={i: i for i in range(nl + ng)},
        compiler_params=pltpu.CompilerParams(has_side_effects=_DATAFLOW),
    )(*args, *sems)
    return list(outs[:nl]), list(outs[nl:])


def _sum_and_share(recv, layer_grads, axis, chip, name):
    n, ns, r, c = recv.shape
    tr = _pick(r, prefs=(256, 128, 64, 32, 16))
    nr = r // tr
    nsteps = n * nr
    nlay = len(layer_grads)
    own_map = (lambda h, i, s: (i, s[0])) if axis == 1 else (lambda h, i, s: (s[0] * nr + i, 0))

    def body(chip_ref, i_ref, *rest):
        g_refs = rest[:nlay]
        o_ref, buf, loc_sems, send_sems, recv_sems = rest[nlay:]
        h, i = pl.program_id(0), pl.program_id(1)
        step = h * nr + i
        slot = step % 2
        x, y, core = _coords()
        layer = core * n + h
        own = g_refs[0][...]
        for l in range(1, nlay):
            own = jnp.where(layer == l, g_refs[l][...], own)

        def copies(sl):
            dst = o_ref.at[core * n + h, pl.ds(pl.multiple_of(i * tr, tr), tr), :]
            loc = pltpu.make_async_copy(buf.at[sl], dst, loc_sems.at[sl])
            rem = pltpu.make_async_remote_copy(
                src_ref=buf.at[sl], dst_ref=dst, send_sem=send_sems.at[sl], recv_sem=recv_sems.at[step],
                device_id=(x, y, 1 - core), device_id_type=MESH)
            return loc, rem

        def drain(sl):
            loc, rem = copies(sl)
            loc.wait()
            rem.wait_send()

        pl.when(step >= 2)(lambda: drain(slot))
        acc = own.astype(F32)
        for s in range(ns):
            acc = acc + i_ref[s].astype(F32)
        buf[slot] = acc
        loc, rem = copies(slot)
        loc.start()
        rem.start()

        @pl.when(step == nsteps - 1)
        def _():
            drain(slot)
            if nsteps > 1:
                drain(1 - slot)
            for hh in range(n):
                for ii in range(nr):
                    land = o_ref.at[(1 - core) * n + hh, pl.ds(ii * tr, tr), :]
                    pltpu.make_async_remote_copy(
                        src_ref=buf.at[0], dst_ref=land, send_sem=send_sems.at[0], recv_sem=recv_sems.at[hh * nr + ii],
                        device_id=(x, y, 1 - core), device_id_type=MESH).wait_recv()

    return pl.pallas_call(
        body, name=name, out_shape=jax.ShapeDtypeStruct((2 * n, r, c), F32),
        grid_spec=pltpu.PrefetchScalarGridSpec(
            num_scalar_prefetch=1, grid=(n, nr),
            in_specs=[pl.BlockSpec((None, ns, tr, c), lambda h, i, s: (h, 0, i, 0))]
            + [pl.BlockSpec((tr, c), own_map)] * nlay,
            out_specs=_HBM,
            scratch_shapes=[pltpu.VMEM((2, tr, c), F32), pltpu.SemaphoreType.DMA((2,)),
                            pltpu.SemaphoreType.DMA((2,)), pltpu.SemaphoreType.DMA((nsteps,))]),
        compiler_params=_cparams(("arbitrary", "arbitrary")),
    )(chip, recv, *layer_grads)


def _gather_sums_over_chips(part):
    def body(i_ref, o_ref, send_sems, recv_sems):
        x, y, c = _coords()
        o_ref[2 * x + y] = i_ref[...]

        def copy(f, slot_chip):
            fx, fy = _CHIP_FLIPS[f]
            return pltpu.make_async_remote_copy(
                src_ref=i_ref, dst_ref=o_ref.at[2 * slot_chip[0] + slot_chip[1]], send_sem=send_sems.at[f - 1],
                recv_sem=recv_sems.at[f - 1], device_id=(_flip(x, fx), _flip(y, fy), c), device_id_type=MESH)

        sends = [copy(f, (x, y)) for f in (1, 2, 3)]
        for cp in sends:
            cp.start()
        for f in (1, 2, 3):
            fx, fy = _CHIP_FLIPS[f]
            copy(f, (_flip(x, fx), _flip(y, fy))).wait_recv()
        for cp in sends:
            cp.wait_send()

    vmem = pl.BlockSpec(memory_space=pltpu.VMEM)
    return pl.pallas_call(
        body, name="gather_small_sums", out_shape=jax.ShapeDtypeStruct((N_CHIPS,) + part.shape, part.dtype),
        in_specs=[vmem], out_specs=vmem,
        scratch_shapes=[pltpu.SemaphoreType.DMA((3,)), pltpu.SemaphoreType.DMA((3,))],
    )(part)


def _adamw_update(w_ref, g_ref, m_ref, v_ref, d_ref, mo_ref, vo_ref):
    bc1 = 1.0 - ADAM_B1 ** ADAM_STEP
    bc2 = 1.0 - ADAM_B2 ** ADAM_STEP
    gv = g_ref[...]
    mn = ADAM_B1 * m_ref[...] + (1.0 - ADAM_B1) * gv
    vn = ADAM_B2 * v_ref[...] + (1.0 - ADAM_B2) * (gv * gv)
    d_ref[...] = -ADAM_LR * ((mn / bc1) / (jnp.sqrt(vn / bc2) + ADAM_EPS) + ADAM_WD * w_ref[...])
    mo_ref[...] = mn
    vo_ref[...] = vn


def _adamw(w, g, m, v, name):
    def body(*refs):
        _adamw_update(*refs)

    tr = _pick(w.shape[0], prefs=(256, 128, 64, 32, 16, 8))
    blk = pl.BlockSpec((tr, w.shape[1]), lambda i: (i, 0))
    sds = jax.ShapeDtypeStruct(w.shape, F32)
    return pl.pallas_call(
        body, name=name, out_shape=(sds, sds, sds), grid=(w.shape[0] // tr,), in_specs=[blk] * 4,
        out_specs=(blk,) * 3, compiler_params=_cparams(("parallel",)),
    )(w, g, m, v)


def _adamw_many(tensors, name, by_layer=False):
    n = len(tensors)

    def body(*refs):
        for t in range(n):
            _adamw_update(*refs[4 * t:4 * t + 4], *refs[4 * n + 3 * t:4 * n + 3 * t + 3])

    def spec(a):
        nd = a.ndim
        if by_layer:
            return pl.BlockSpec((1,) + a.shape[1:], lambda i: (i,) + (0,) * (nd - 1))
        return pl.BlockSpec(a.shape, lambda i: (0,) * nd)

    steps = tensors[0][0].shape[0] if by_layer else 1
    outs = pl.pallas_call(
        body, name=name, out_shape=tuple(jax.ShapeDtypeStruct(t[0].shape, F32) for t in tensors for _ in range(3)),
        grid=(steps,), in_specs=[spec(a) for t in tensors for a in t],
        out_specs=tuple(spec(t[0]) for t in tensors for _ in range(3)), compiler_params=_cparams(("parallel",)),
    )(*[a for t in tensors for a in t])
    return [tuple(outs[3 * t:3 * t + 3]) for t in range(n)]


_PACK_QUANTUM = 256 * LANES


def _pack(arrs):
    flat = jnp.concatenate([a.reshape(-1).astype(F32) for a in arrs])
    flat = jnp.pad(flat, (0, (-flat.shape[0]) % _PACK_QUANTUM))
    return flat.reshape(-1, LANES)


def _unpack(p, shapes):
    flat = p.reshape(-1)
    out, off = [], 0
    for s in shapes:
        n = int(np.prod(s))
        out.append(flat[off:off + n].reshape(s))
        off += n
    return out


def kernel(*args):
    nw = len(WEIGHTS)
    x, tgt = args[0], args[1 + nw]
    w = dict(zip(WEIGHTS, args[1:1 + nw]))
    m = dict(zip(WEIGHTS, args[2 + nw:2 + 2 * nw]))
    v = dict(zip(WEIGHTS, args[2 + 2 * nw:2 + 3 * nw]))
    _, L, D = x.shape
    chip = 2 * lax.axis_index("x") + lax.axis_index("y")

    big = list(BIG)
    small_sh_shapes = [w[n].shape for n in SMALL_SHARDED]
    nbig = len(big)
    chip1 = chip.reshape(1).astype(jnp.int32)
    axes2 = [BIG[n] - 1 for n in big] + [0]
    shards = [w[n] for n in big] + [_pack([w[n] for n in SMALL_SHARDED])[None]]
    pairs = [(t, l) for t in range(nbig + 1) for l in range(shards[t].shape[0])]
    depth = w['norm_mix_g'].shape[0]
    part_of = lambda t, l: 0 if t == nbig else 2 * _model_layer(big[t], l) + big[t].startswith('ffn')
    flying, token = [], None
    for g in range(2 * depth):
        ids = [k for k, (t, l) in enumerate(pairs) if part_of(t, l) == g]
        ts = [pairs[k][0] for k in ids]
        placed = [_place_quarter(shards[t], pairs[k][1], axes2[t], chip1, F32 if t == nbig else BF16, token)
                  for k, t in zip(ids, ts)]
        lands, send, recv, token = _gather_start(g, placed, [axes2[t] for t in ts], token)
        flying.append((ts, lands, send, recv))

    def wait_group(g, after):
        ts, lands, send, recv = flying[g]
        landed = _gather_wait(g, lands, send, recv, [axes2[t] for t in ts], token if after is None else after)
        return dict(zip(ts, landed))

    first = wait_group(0, None)
    packed = first.pop(nbig).reshape(N_CHIPS, -1, LANES)
    per_chip = [_unpack(packed[s], small_sh_shapes) for s in range(N_CHIPS)]
    wl = dict(w)
    for k, n in enumerate(SMALL_SHARDED):
        wl[n] = jnp.concatenate([per_chip[s][k] for s in range(N_CHIPS)], axis=-1)

    def layer_weights(i, after):
        got = first if i == 0 else wait_group(i, after)
        return {big[t]: a for t, a in got.items()}

    small_shapes = [(w[n].shape[:-1] + (w[n].shape[-1] * N_CHIPS,)) if n in SMALL_SHARDED else w[n].shape
                    for n in SMALL] + [(1,)]
    n_small = sum(int(np.prod(s)) for s in small_shapes)
    pack_rows = -(-n_small // _PACK_QUANTUM) * _PACK_QUANTUM // LANES
    nlayers = [w[n].shape[0] for n in big] + [2]
    halves = [n // 2 for n in nlayers]
    quarters = [tuple(w[n].shape[1:]) for n in big] + [(pack_rows // 2 // N_CHIPS, LANES)]
    wire = [BF16] * nbig + [F32]
    land_now = [lax.empty((halves[t], N_SLOTS) + quarters[t], wire[t]) for t in range(nbig + 1)]
    gparts = [[None] * n for n in nlayers]
    started = []

    def start_scatter(tag, ts, ls, arrays):
        meta = [(axes2[t], l // halves[t], l % halves[t], quarters[t][axes2[t]]) for t, l in zip(ts, ls)]
        send, recv, thru, new_lands, token = _scatter_start(tag, arrays, [land_now[t] for t in ts], meta)
        for t, ln in zip(ts, new_lands):
            land_now[t] = ln
        started.append((send, recv, thru, ts, meta, ls))
        return token

    def on_layer_grads(g, gb):
        ts = [big.index(n) for n in gb]
        return start_scatter(g, ts, [g // 2 if big[t].startswith('ffn') else g // 4 for t in ts],
                             [gb[big[t]] for t in ts])

    loss, dx, gsmall = _local_step(x.reshape(L, D), tgt.reshape(L, D), wl, layer_weights, on_layer_grads)
    gpack = _pack([gsmall[n] for n in SMALL] + [loss.reshape(1)])
    start_scatter(2 * depth, [nbig, nbig], [0, 1], [gpack[:pack_rows // 2], gpack[pack_rows // 2:]])
    landed, sent = _scatter_wait([s[:5] for s in started], land_now)
    for (t, l), g in zip([(t, l) for s in started for t, l in zip(s[3], s[5])], sent):
        gparts[t][l] = g
    gshard = {n: _sum_and_share(landed[t], gparts[t], axes2[t], chip1, "sum_share_" + n) for t, n in enumerate(big)}
    small_sum = _sum_and_share(landed[nbig], gparts[nbig], 0, chip1, "sum_share_small")
    gpack = _gather_sums_over_chips(small_sum).transpose(1, 0, 2, 3).reshape(pack_rows, LANES)
    gs = dict(zip(SMALL + ['loss'], _unpack(gpack, small_shapes)))
    loss = gs.pop('loss').reshape(())
    for n in SMALL_SHARDED:
        width = w[n].shape[-1]
        gs[n] = lax.dynamic_slice_in_dim(gs[n], chip * width, width, axis=gs[n].ndim - 1)

    grads, delta, new_m, new_v = {}, {}, {}, {}
    for n in big:
        shp = w[n].shape
        flat = lambda a: a.reshape(shp[0] * shp[1], shp[2])
        g = gshard[n]
        grads[n] = g
        d_, m_, v_ = _adamw(flat(w[n]), flat(g), flat(m[n]), flat(v[n]), "adamw_" + n)
        delta[n], new_m[n], new_v[n] = d_.reshape(shp), m_.reshape(shp), v_.reshape(shp)
    sparse = [n for n in SMALL if w[n].ndim == 4 and w[n].shape[-1] < LANES // 2]
    for names, by_layer in ((sparse, True), ([n for n in SMALL if n not in sparse], False)):
        as2d = lambda a: a.reshape(1, -1) if a.ndim == 1 else a
        res = _adamw_many([(as2d(w[n]), as2d(gs[n]), as2d(m[n]), as2d(v[n])) for n in names],
                          "adamw_small_by_layer" if by_layer else "adamw_small", by_layer)
        for n, (d_, m_, v_) in zip(names, res):
            shp = w[n].shape
            grads[n], delta[n], new_m[n], new_v[n] = gs[n], d_.reshape(shp), m_.reshape(shp), v_.reshape(shp)

    return (loss, dx.reshape(1, L, D), *[grads[n] for n in WEIGHTS], *[delta[n] for n in WEIGHTS],
            *[new_m[n] for n in WEIGHTS], *[new_v[n] for n in WEIGHTS])
```

```python
import functools
import math

import numpy as np
import jax
import jax.numpy as jnp
from jax import lax
from jax.experimental import pallas as pl
from jax.experimental.pallas import tpu as pltpu

F32 = jnp.float32
BF16 = jnp.bfloat16
MESH = pl.DeviceIdType.MESH

EPS = 1e-6
CHUNK = 128
POOL_WINDOWS = (2, 4, 8, 16)
LANES = 128
SUBLANES = 8
SCAN_CHUNKS = SUBLANES
S5_GROUPS_PER_STEP = 4
MM_TM_CAP, MM_TN_CAP, MM_TK_CAP = 1408, 1408, 2048
MM_TK_WHOLE = 2048
VMEM_LIMIT = 48 * 1024 * 1024
VMEM_LIMIT_S5 = 56 * 1024 * 1024

ADAM_LR, ADAM_B1, ADAM_B2, ADAM_EPS, ADAM_WD, ADAM_STEP = 0.001, 0.9, 0.999, 1e-08, 0.01, 10

WEIGHTS = ['norm_mix_g', 'even_w_in', 'even_conv_w', 'ssm_log_step', 'ssm_a_re', 'ssm_a_im', 'ssm_b_re',
           'ssm_b_im', 'ssm_c_re', 'ssm_c_im', 'ssm_d', 'ssm_glu_w', 'ssm_glu_b', 'even_w_out', 'odd_w_in',
           'pool_w', 'pool_scale', 'sgu_norm_g', 'sgu_w', 'sgu_b', 'odd_w_out', 'norm_ffn_g', 'ffn_w_up',
           'ffn_conv_w', 'ffn_conv_b', 'ffn_w_down', 'norm_final_g']
BIG = {'even_w_in': 2, 'ssm_glu_w': 1, 'even_w_out': 1, 'odd_w_in': 2, 'odd_w_out': 1, 'ffn_w_up': 2,
       'ffn_w_down': 1}
SMALL_SHARDED = ('even_conv_w', 'pool_scale', 'sgu_norm_g', 'ffn_conv_w')
SMALL = [n for n in WEIGHTS if n not in BIG]
N_CHIPS = 4
N_DEV = 8


def _cparams(sem=None, vmem=VMEM_LIMIT):
    kw = dict(vmem_limit_bytes=vmem)
    if sem is not None:
        kw['dimension_semantics'] = sem
    return pltpu.CompilerParams(**kw)


def _pick(n, segs=(), prefs=(1024, 512, 256, 128)):
    for t in prefs:
        if n % t == 0 and all(s % t == 0 for s in segs if s):
            return t
    return n


def _largest_tile(n, segs, cap):
    best = None
    for t in range(LANES, min(n, cap) + 1, LANES):
        if n % t == 0 and all(s % t == 0 for s in segs if s):
            best = t
    return best if best is not None else n


def _ldims(arr, kind):
    if kind is None:
        return arr.shape
    if kind[0] == 'lead':
        return arr.shape[1:]
    return (arr.shape[1], arr.shape[0] * arr.shape[2])


def _segw(arr, kind):
    return arr.shape[2] if (kind is not None and kind[0] == 'seg') else None


def _opspec(arr, kind, br, bc, rfn, cfn):
    if kind is None:
        return pl.BlockSpec((br, bc), lambda i, j, k: (rfn(i, j, k), cfn(i, j, k)))
    if kind[0] == 'lead':
        lead = kind[1]
        return pl.BlockSpec((None, br, bc), lambda i, j, k: (lead, rfn(i, j, k), cfn(i, j, k)))
    per = arr.shape[2] // bc
    return pl.BlockSpec((None, br, bc), lambda i, j, k: (cfn(i, j, k) // per, rfn(i, j, k), cfn(i, j, k) % per))


def _mm(a, b, mode, out_dtype, name, ak=None, bk=None, ok=None, res=None, dep=None):
    ar, ac = _ldims(a, ak)
    br_, bc_ = _ldims(b, bk)
    if mode == 'nn':
        M, K, N = ar, ac, bc_
        assert br_ == K
    else:
        M, K, N = ar, ac, br_
        assert bc_ == K
    sa, sb = _segw(a, ak), _segw(b, bk)
    so = (N // ok[1]) if ok is not None else None
    tm = _largest_tile(M, [], MM_TM_CAP)
    tn = _largest_tile(N, [sb if mode == 'nn' else None, so], MM_TN_CAP)
    ksegs = [sa, sb if mode == 'nt' else None]
    tk = K if (K <= MM_TK_WHOLE and not any(ksegs)) else _largest_tile(K, ksegs, MM_TK_CAP)
    nk = K // tk
    I = lambda i, j, k: i
    J = lambda i, j, k: j
    Kk = lambda i, j, k: k
    a_spec = _opspec(a, ak, tm, tk, I, Kk)
    if mode == 'nn':
        b_spec = _opspec(b, bk, tk, tn, Kk, J)
        dims = (((1,), (0,)), ((), ()))
    else:
        b_spec = _opspec(b, bk, tn, tk, J, Kk)
        dims = (((1,), (1,)), ((), ()))
    if ok is None:
        out_shape = jax.ShapeDtypeStruct((M, N), out_dtype)
        o_spec = pl.BlockSpec((tm, tn), lambda i, j, k: (i, j))
    else:
        out_shape = jax.ShapeDtypeStruct((ok[1], M, N // ok[1]), out_dtype)
        per = (N // ok[1]) // tn
        o_spec = pl.BlockSpec((None, tm, tn), lambda i, j, k: (j // per, i, j % per))
    has_res = res is not None

    def body(*refs):
        a_ref, b_ref = refs[0], refs[1]
        r_ref = refs[2] if has_res else None
        o_ref = refs[n_in]
        prod = lax.dot_general(a_ref[...].astype(BF16), b_ref[...].astype(BF16), dims, preferred_element_type=F32)
        if nk == 1:
            o_ref[...] = (prod + r_ref[...] if has_res else prod).astype(out_dtype)
            return
        acc = refs[-1]
        k = pl.program_id(2)

        @pl.when(k == 0)
        def _():
            acc[...] = prod

        @pl.when(k > 0)
        def _():
            acc[...] += prod

        @pl.when(k == nk - 1)
        def _():
            o = acc[...]
            if has_res:
                o = o + r_ref[...]
            o_ref[...] = o.astype(out_dtype)

    in_specs = [a_spec, b_spec]
    args = [a, b]
    if has_res:
        in_specs.append(pl.BlockSpec((tm, tn), lambda i, j, k: (i, j)))
        args.append(res)
    if dep is not None:
        in_specs.append(pl.BlockSpec(memory_space=pl.ANY))
        args.append(dep)
    n_in = len(args)
    return pl.pallas_call(
        body, name=name, out_shape=out_shape, grid=(M // tm, N // tn, nk), in_specs=in_specs, out_specs=o_spec,
        scratch_shapes=[pltpu.VMEM((tm, tn), F32)] if nk > 1 else [],
        compiler_params=_cparams(("parallel", "parallel", "arbitrary")),
    )(*args)


_G0 = math.sqrt(2.0 / math.pi)
_G1 = 0.044715


def _gelu(x):
    return 0.5 * x * (1.0 + jnp.tanh(_G0 * (x + _G1 * x * x * x)))


def _gelu_grad(x):
    x2 = x * x
    t = jnp.tanh(_G0 * (x + _G1 * x * x2))
    return 0.5 * (1.0 + t) + 0.5 * x * (1.0 - t * t) * (_G0 * (1.0 + 3.0 * _G1 * x2))


def _sigmoid(x):
    return 1.0 / (1.0 + jnp.exp(-x))


def _down(v, k):
    r = pltpu.roll(v, k, axis=0)
    row = lax.broadcasted_iota(jnp.int32, (SUBLANES, v.shape[1]), 0)
    return jnp.concatenate([jnp.where(row >= k, r[:SUBLANES], 0.0), r[SUBLANES:]], axis=0)


def _up(v, k):
    n = v.shape[0]
    r = pltpu.roll(v, n - k, axis=0)
    row = lax.broadcasted_iota(jnp.int32, (SUBLANES, v.shape[1]), 0)
    return jnp.concatenate([r[:n - SUBLANES], jnp.where(row < SUBLANES - k, r[n - SUBLANES:], 0.0)], axis=0)


def _taps(v):
    return _down(v, 2), _down(v, 1), v


def _conv3(taps, w):
    return w[0:1, :] * taps[0] + w[1:2, :] * taps[1] + w[2:3, :] * taps[2]


def _conv3_t(dv, w):
    return w[2:3, :] * dv + w[1:2, :] * _up(dv, 1) + w[0:1, :] * _up(dv, 2)


def _conv3_dw(dv, taps):
    return tuple(jnp.sum(dv * tp, axis=0, keepdims=True) for tp in taps)


def _cmul(ar, ai, br, bi):
    return ar * br - ai * bi, ar * bi + ai * br


def _cpow(lr, li, n):
    rr = ri = None
    br, bi = lr, li
    while n:
        if n & 1:
            rr, ri = (br, bi) if rr is None else _cmul(rr, ri, br, bi)
        n >>= 1
        if n:
            br, bi = _cmul(br, bi, br, bi)
    return rr, ri


NORM_ROWS = 256


def _norm_mm(x, g, b, out_dtype, name, ok=None):
    M, D = x.shape
    N = b.shape[1]
    so = (N // ok[1]) if ok is not None else None
    tm = _largest_tile(M, [], 1024)
    tn = _largest_tile(N, [so], MM_TN_CAP)
    if ok is None:
        out_shape = jax.ShapeDtypeStruct((M, N), out_dtype)
        o_spec = pl.BlockSpec((tm, tn), lambda i, j: (i, j))
    else:
        out_shape = jax.ShapeDtypeStruct((ok[1], M, N // ok[1]), out_dtype)
        per = (N // ok[1]) // tn
        o_spec = pl.BlockSpec((None, tm, tn), lambda i, j: (j // per, i, j % per))

    def body(x_ref, g_ref, b_ref, o_ref, ht_ref, h_scr):
        @pl.when(pl.program_id(1) == 0)
        def _():
            for c in range(tm // NORM_ROWS):
                rows = pl.ds(c * NORM_ROWS, NORM_ROWS)
                xv = x_ref[rows, :]
                h = xv * lax.rsqrt(jnp.mean(xv * xv, axis=-1, keepdims=True) + EPS) * g_ref[...]
                h_scr[rows, :] = h.astype(BF16)
                ht_ref[:, rows] = h.T.astype(BF16)

        o_ref[...] = jnp.dot(h_scr[...], b_ref[...], preferred_element_type=F32).astype(out_dtype)

    return pl.pallas_call(
        body, name=name, out_shape=(out_shape, jax.ShapeDtypeStruct((D, M), BF16)), grid=(M // tm, N // tn),
        in_specs=[pl.BlockSpec((tm, D), lambda i, j: (i, 0)), pl.BlockSpec((1, D), lambda i, j: (0, 0)),
                  pl.BlockSpec((D, tn), lambda i, j: (0, j))],
        out_specs=(o_spec, pl.BlockSpec((D, tm), lambda i, j: (0, i))),
        scratch_shapes=[pltpu.VMEM((tm, D), BF16)], compiler_params=_cparams(("parallel", "arbitrary")),
    )(x, g.reshape(1, D), b)


def _mm_norm_bwd(a, b, x, g, dres, name, ak=None, dep=None):
    M, K = _ldims(a, ak)
    D = b.shape[0]
    assert b.shape[1] == K and x.shape == (M, D)
    sa = _segw(a, ak)
    tm = _largest_tile(M, [], 1024)
    whole_segs = bool(sa) and K <= MM_TK_WHOLE
    tk = K if (K <= MM_TK_WHOLE) else _largest_tile(K, [sa], MM_TK_CAP)
    ni, nk = M // tm, K // tk
    if whole_segs:
        a_spec = pl.BlockSpec((a.shape[0], tm, sa), lambda i, k: (0, i, 0))
    else:
        a3 = _opspec(a, ak, tm, tk, lambda i, j, k: i, lambda i, j, k: k)
        a_spec = pl.BlockSpec(a3.block_shape, lambda i, k: a3.index_map(i, 0, k))
    n_in = 5 + (dep is not None)

    def body(*refs):
        a_ref, b_ref, x_ref, g_ref, r_ref = refs[:5]
        dx_ref, dxb_ref, dg_ref, acc, accg = refs[n_in:]
        i, k = pl.program_id(0), pl.program_id(1)
        av = jnp.concatenate([a_ref[s] for s in range(a.shape[0])], axis=1) if whole_segs else a_ref[...]
        prod = lax.dot_general(av.astype(BF16), b_ref[...], (((1,), (1,)), ((), ())), preferred_element_type=F32)

        @pl.when(k == 0)
        def _():
            acc[...] = prod

        @pl.when(k > 0)
        def _():
            acc[...] += prod

        @pl.when((i == 0) & (k == 0))
        def _():
            accg[...] = jnp.zeros_like(accg)

        @pl.when(k == nk - 1)
        def _():
            for c in range(tm // NORM_ROWS):
                rows = pl.ds(c * NORM_ROWS, NORM_ROWS)
                xv = x_ref[rows, :]
                r = lax.rsqrt(jnp.mean(xv * xv, axis=-1, keepdims=True) + EPS)
                xh = xv * r
                dhv = acc[rows, :]
                accg[...] += jnp.sum((dhv * xh).reshape(NORM_ROWS // SUBLANES, SUBLANES, D), axis=0)
                dxh = dhv * g_ref[...]
                dxv = r_ref[rows, :] + r * (dxh - xh * jnp.mean(dxh * xh, axis=-1, keepdims=True))
                dx_ref[rows, :] = dxv
                dxb_ref[rows, :] = dxv.astype(BF16)

        @pl.when((i == ni - 1) & (k == nk - 1))
        def _():
            dg_ref[...] = jnp.sum(accg[...], axis=0, keepdims=True)

    row = pl.BlockSpec((tm, D), lambda i, k: (i, 0))
    vec = pl.BlockSpec((1, D), lambda i, k: (0, 0))
    in_specs = [a_spec, pl.BlockSpec((D, tk), lambda i, k: (0, k)), row, vec, row]
    args = [a, b, x, g.reshape(1, D), dres]
    if dep is not None:
        in_specs.append(pl.BlockSpec(memory_space=pl.ANY))
        args.append(dep)
    return pl.pallas_call(
        body, name=name,
        out_shape=(jax.ShapeDtypeStruct((M, D), F32), jax.ShapeDtypeStruct((M, D), BF16),
                   jax.ShapeDtypeStruct((1, D), F32)),
        grid=(ni, nk), in_specs=in_specs, out_specs=(row, row, vec),
        scratch_shapes=[pltpu.VMEM((tm, D), F32), pltpu.VMEM((SUBLANES, D), F32)],
        compiler_params=_cparams(("arbitrary", "arbitrary"), VMEM_LIMIT_S5),
    )(*args)


def _loss_head(x, g, tgt):
    L, D = x.shape
    tr = _pick(L, prefs=(512, 256, 128))
    nsteps = L // tr

    def body(x_ref, g_ref, t_ref, loss_ref, dx_ref, dxb_ref, dg_ref, acc_g, acc_l):
        i = pl.program_id(0)

        @pl.when(i == 0)
        def _():
            acc_g[...] = jnp.zeros_like(acc_g)
            acc_l[...] = jnp.zeros_like(acc_l)

        xv = x_ref[...]
        gv = g_ref[...]
        r = lax.rsqrt(jnp.mean(xv * xv, axis=-1, keepdims=True) + EPS)
        xh = xv * r
        e = xh * gv - t_ref[...]
        acc_l[...] += jnp.sum((e * e).reshape(tr // SUBLANES, SUBLANES, D), axis=0)
        dy = e * (1.0 / D)
        acc_g[...] += jnp.sum((dy * xh).reshape(tr // SUBLANES, SUBLANES, D), axis=0)
        dxh = dy * gv
        dxv = r * (dxh - xh * jnp.mean(dxh * xh, axis=-1, keepdims=True))
        dx_ref[...] = dxv
        dxb_ref[...] = dxv.astype(BF16)

        @pl.when(i == nsteps - 1)
        def _():
            dg_ref[...] = jnp.sum(acc_g[...], axis=0, keepdims=True)
            tot = jnp.sum(jnp.sum(acc_l[...], axis=0, keepdims=True), axis=1, keepdims=True) * (0.5 / D)
            loss_ref[...] = jnp.broadcast_to(tot, (SUBLANES, LANES))

    row = pl.BlockSpec((tr, D), lambda i: (i, 0))
    vec = pl.BlockSpec((1, D), lambda i: (0, 0))
    return pl.pallas_call(
        body, name="loss_head",
        out_shape=(jax.ShapeDtypeStruct((SUBLANES, LANES), F32), jax.ShapeDtypeStruct((L, D), F32),
                   jax.ShapeDtypeStruct((L, D), BF16), jax.ShapeDtypeStruct((1, D), F32)),
        grid=(nsteps,), in_specs=[row, vec, row],
        out_specs=(pl.BlockSpec((SUBLANES, LANES), lambda i: (0, 0)), row, row, vec),
        scratch_shapes=[pltpu.VMEM((SUBLANES, D), F32), pltpu.VMEM((SUBLANES, D), F32)],
        compiler_params=_cparams(("arbitrary",)),
    )(x, g.reshape(1, D), tgt)


def _sconv_fwd(proj4, conv_w, name):
    _, L, C = proj4.shape
    cb = LANES

    def body(p_ref, w_ref, o_ref):
        xa, ba, ca = p_ref[0].astype(F32), p_ref[1].astype(F32), p_ref[2].astype(F32)
        o_ref[...] = (ba * _conv3(_taps(ca * xa), w_ref[...])).astype(BF16)

    return pl.pallas_call(
        body, name=name, out_shape=jax.ShapeDtypeStruct((L, 2 * C), BF16), grid=(C // cb,),
        in_specs=[pl.BlockSpec((3, L, cb), lambda j: (0, 0, j)), pl.BlockSpec((3, cb), lambda j: (0, j))],
        out_specs=pl.BlockSpec((L, cb), lambda j: (0, j)), compiler_params=_cparams(("parallel",)),
    )(proj4, conv_w)


def _sconv_bwd(proj4, dmix, conv_w, name):
    _, L, C = proj4.shape
    cb = LANES

    def body(p_ref, d_ref, w_ref, o_ref, dw_ref):
        xa, ba, ca = p_ref[0].astype(F32), p_ref[1].astype(F32), p_ref[2].astype(F32)
        w = w_ref[...]
        dya = d_ref[...]
        tq = _taps(ca * xa)
        cq = _conv3(tq, w)
        dcq = dya * ba
        dq = _conv3_t(dcq, w)
        for tap, dwt in enumerate(_conv3_dw(dcq, tq)):
            dw_ref[tap:tap + 1, :] = dwt
        o_ref[0] = (dq * ca).astype(BF16)
        o_ref[1] = (dya * cq).astype(BF16)
        o_ref[2] = (dq * xa).astype(BF16)

    return pl.pallas_call(
        body, name=name,
        out_shape=(jax.ShapeDtypeStruct((4, L, C), BF16), jax.ShapeDtypeStruct((3, C), F32)), grid=(C // cb,),
        in_specs=[pl.BlockSpec((3, L, cb), lambda j: (0, 0, j)), pl.BlockSpec((L, cb), lambda j: (0, j)),
                  pl.BlockSpec((3, cb), lambda j: (0, j))],
        out_specs=(pl.BlockSpec((3, L, cb), lambda j: (0, 0, j)), pl.BlockSpec((3, cb), lambda j: (0, j))),
        compiler_params=_cparams(("parallel",)),
    )(proj4, dmix, conv_w)


def _s5_prep(log_step, a_re, a_im, b_re, b_im, c_re, c_im):
    G, P = a_re.shape
    H = b_re.shape[-1]
    gs = S5_GROUPS_PER_STEP
    ns = G // gs
    gu = LANES // H
    lam = lax.complex(a_re, a_im)
    step = jnp.exp(log_step)[:, None]
    lam_bar = jnp.exp(lam * step)
    b_bar = ((lam_bar - 1.0) / lam)[..., None] * lax.complex(b_re, b_im)
    lr = jnp.real(lam_bar).reshape(ns, 1, gs * P)
    li = jnp.imag(lam_bar).reshape(ns, 1, gs * P)
    k = np.arange(ns)[:, None, None]
    oh = jnp.asarray((np.arange(gu)[None, :, None] == gs * (k % (gu // gs)) + np.arange(gs)[None, None, :]),
                     F32)
    bre = jnp.einsum('kgl,klph->kghlp', oh, jnp.real(b_bar).reshape(ns, gs, P, H)).reshape(ns, gu * H, gs * P)
    bim = jnp.einsum('kgl,klph->kghlp', oh, jnp.imag(b_bar).reshape(ns, gs, P, H)).reshape(ns, gu * H, gs * P)
    cre = jnp.einsum('kgl,klhp->klpgh', oh, c_re.reshape(ns, gs, H, P)).reshape(ns, gs * P, gu * H)
    cim = jnp.einsum('kgl,klhp->klpgh', oh, c_im.reshape(ns, gs, H, P)).reshape(ns, gs * P, gu * H)
    return lr, li, jnp.concatenate([bre, bim], axis=2), jnp.concatenate([cre, -cim], axis=1)


def _carry_tile(fr, fi, pr, pi, reverse):
    row = lax.broadcasted_iota(jnp.int32, fr.shape, 0)
    cr = jnp.zeros_like(fr)
    ci = jnp.zeros_like(fi)
    sr = jnp.zeros_like(fr[0:1])
    si = jnp.zeros_like(sr)
    order = range(SCAN_CHUNKS - 1, 0, -1) if reverse else range(0, SCAN_CHUNKS - 1)
    for c in order:
        fcr = jnp.sum(jnp.where(row == c, fr, 0.0), axis=0, keepdims=True)
        fci = jnp.sum(jnp.where(row == c, fi, 0.0), axis=0, keepdims=True)
        mr, mi = _cmul(pr, pi, sr, si)
        sr, si = mr + fcr, mi + fci
        nxt = c - 1 if reverse else c + 1
        cr = jnp.where(row == nxt, sr, cr)
        ci = jnp.where(row == nxt, si, ci)
    return cr, ci


def _scan_order_into(dst_ref, src_ref, T):
    for c in range(SCAN_CHUNKS):
        dst_ref[pl.ds(c, T, stride=SCAN_CHUNKS), :] = src_ref[pl.ds(c * T, T), :].astype(F32)


def _s5_fwd(proj4, lr, li, bmat, cmat, d, name):
    _, L, Du = proj4.shape
    ns, _, W2 = bmat.shape
    W = W2 // 2
    T = L // SCAN_CHUNKS
    rb = _pick(L, prefs=(512, 256, 128))
    per = (ns * LANES) // Du

    def body(ut_ref, lr_ref, li_ref, b_ref, c_ref, d_ref, y_ref, sr_ref, si_ref, u_ref):
        k = pl.program_id(0)
        _scan_order_into(u_ref, ut_ref, T)
        for r in range(L // rb):
            rows = pl.ds(r * rb, rb)
            bu = jnp.dot(u_ref[rows, :].astype(BF16), b_ref[...], preferred_element_type=F32)
            sr_ref[rows, :] = bu[:, :W]
            si_ref[rows, :] = bu[:, W:]
        lam_r = jnp.broadcast_to(lr_ref[...], (SUBLANES, W))
        lam_i = jnp.broadcast_to(li_ref[...], (SUBLANES, W))

        def local(t, carry):
            sr, si = carry
            rows = pl.ds(pl.multiple_of(t * SUBLANES, SUBLANES), SUBLANES)
            mr, mi = _cmul(lam_r, lam_i, sr, si)
            sr = mr + sr_ref[rows, :]
            si = mi + si_ref[rows, :]
            sr_ref[rows, :] = sr
            si_ref[rows, :] = si
            return sr, si

        z = jnp.zeros((SUBLANES, W), F32)
        fr, fi = lax.fori_loop(0, T, local, (z, z))
        pr, pi = _cpow(lam_r, lam_i, T)
        cr, ci = _carry_tile(fr, fi, pr[0:1], pi[0:1], reverse=False)

        def fix(t, carry):
            wr, wi = carry
            rows = pl.ds(pl.multiple_of(t * SUBLANES, SUBLANES), SUBLANES)
            ar, ai = _cmul(wr, wi, cr, ci)
            sr_ref[rows, :] += ar
            si_ref[rows, :] += ai
            return _cmul(wr, wi, lam_r, lam_i)

        lax.fori_loop(0, T, fix, (lam_r, lam_i))
        first = (k % per) == 0
        for r in range(L // rb):
            rows = pl.ds(r * rb, rb)
            s = jnp.concatenate([sr_ref[rows, :], si_ref[rows, :]], axis=1).astype(BF16)
            y = jnp.dot(s, c_ref[...], preferred_element_type=F32)

            @pl.when(first)
            def _():
                y_ref[rows, :] = y + d_ref[...] * u_ref[rows, :]

            @pl.when(jnp.logical_not(first))
            def _():
                y_ref[rows, :] += y

    ublk = pl.BlockSpec((L, LANES), lambda k: (0, k // per))
    sblk = pl.BlockSpec((L, W), lambda k: (0, k))
    lam = pl.BlockSpec((None, 1, W), lambda k: (k, 0, 0))
    return pl.pallas_call(
        body, name=name,
        out_shape=(jax.ShapeDtypeStruct((L, Du), F32), jax.ShapeDtypeStruct((L, ns * W), F32),
                   jax.ShapeDtypeStruct((L, ns * W), F32)),
        grid=(ns,),
        in_specs=[pl.BlockSpec((None, L, LANES), lambda k: (3, 0, k // per)), lam, lam,
                  pl.BlockSpec((None, LANES, 2 * W), lambda k: (k, 0, 0)),
                  pl.BlockSpec((None, 2 * W, LANES), lambda k: (k, 0, 0)),
                  pl.BlockSpec((1, LANES), lambda k: (0, k // per))],
        out_specs=(ublk, sblk, sblk), scratch_shapes=[pltpu.VMEM((L, LANES), F32)],
        compiler_params=_cparams(("arbitrary",), VMEM_LIMIT_S5),
    )(proj4, lr, li, bmat.astype(BF16), cmat.astype(BF16), d.reshape(1, Du))


def _s5_bwd(dy, proj4, dproj, s_re, s_im, lr, li, bmat, cmat, d, name):
    _, L, Du = proj4.shape
    ns, _, W2 = bmat.shape
    W = W2 // 2
    T = L // SCAN_CHUNKS
    rb = _pick(L, prefs=(512, 256, 128))
    per = (ns * LANES) // Du
    NT = (((1,), (1,)), ((), ()))
    TN = (((0,), (0,)), ((), ()))

    def body(dy_ref, ut_ref, dp_in, sr_ref, si_ref, lr_ref, li_ref, b_ref, c_ref, d_ref,
             dut_ref, db_ref, dc_ref, dl_ref, dd_ref, gr_ref, gi_ref, u_ref, du_ref):
        k = pl.program_id(0)
        _scan_order_into(u_ref, ut_ref, T)
        for r in range(L // rb):
            rows = pl.ds(r * rb, rb)
            g = lax.dot_general(dy_ref[rows, :].astype(BF16), c_ref[...], NT, preferred_element_type=F32)
            gr_ref[rows, :] = g[:, :W]
            gi_ref[rows, :] = g[:, W:]
        lam_r = jnp.broadcast_to(lr_ref[...], (SUBLANES, W))
        lam_i = -jnp.broadcast_to(li_ref[...], (SUBLANES, W))

        def local(i, carry):
            gr, gi = carry
            rows = pl.ds(pl.multiple_of((T - 1 - i) * SUBLANES, SUBLANES), SUBLANES)
            mr, mi = _cmul(lam_r, lam_i, gr, gi)
            gr = mr + gr_ref[rows, :]
            gi = mi + gi_ref[rows, :]
            gr_ref[rows, :] = gr
            gi_ref[rows, :] = gi
            return gr, gi

        z = jnp.zeros((SUBLANES, W), F32)
        fr, fi = lax.fori_loop(0, T, local, (z, z))
        pr, pi = _cpow(lam_r, lam_i, T)
        cr, ci = _carry_tile(fr, fi, pr[0:1], pi[0:1], reverse=True)

        def true_g(rows, wr, wi):
            ar, ai = _cmul(wr, wi, cr, ci)
            gr = gr_ref[rows, :] + ar
            gi = gi_ref[rows, :] + ai
            gr_ref[rows, :] = gr
            gi_ref[rows, :] = gi
            return gr, gi

        def fix(i, carry):
            wr, wi, ar_, ai_ = carry
            t = T - 1 - i
            rows = pl.ds(pl.multiple_of(t * SUBLANES, SUBLANES), SUBLANES)
            prev = pl.ds(pl.multiple_of((t - 1) * SUBLANES, SUBLANES), SUBLANES)
            gr, gi = true_g(rows, wr, wi)
            qr, qi = sr_ref[prev, :], si_ref[prev, :]
            ar_ = ar_ + gr * qr + gi * qi
            ai_ = ai_ + gi * qr - gr * qi
            wr, wi = _cmul(wr, wi, lam_r, lam_i)
            return wr, wi, ar_, ai_

        wr, wi, acc_r, acc_i = lax.fori_loop(0, T - 1, fix, (lam_r, lam_i, z, z))
        gr, gi = true_g(pl.ds(0, SUBLANES), wr, wi)
        last = pl.ds((T - 1) * SUBLANES, SUBLANES)
        row = lax.broadcasted_iota(jnp.int32, (SUBLANES, W), 0)
        qr = jnp.where(row >= 1, pltpu.roll(sr_ref[last, :], 1, axis=0), 0.0)
        qi = jnp.where(row >= 1, pltpu.roll(si_ref[last, :], 1, axis=0), 0.0)
        acc_r = acc_r + gr * qr + gi * qi
        acc_i = acc_i + gi * qr - gr * qi
        dl_ref[0:1, :] = jnp.sum(acc_r, axis=0, keepdims=True)
        dl_ref[1:2, :] = jnp.sum(acc_i, axis=0, keepdims=True)

        first = (k % per) == 0
        db = jnp.zeros((LANES, 2 * W), F32)
        dc = jnp.zeros((LANES, 2 * W), F32)
        dd = jnp.zeros((1, LANES), F32)
        for r in range(L // rb):
            rows = pl.ds(r * rb, rb)
            gb = jnp.concatenate([gr_ref[rows, :], gi_ref[rows, :]], axis=1).astype(BF16)
            sb = jnp.concatenate([sr_ref[rows, :], si_ref[rows, :]], axis=1).astype(BF16)
            dyv = dy_ref[rows, :]
            uv = u_ref[rows, :]
            du = lax.dot_general(gb, b_ref[...], NT, preferred_element_type=F32)
            db = db + lax.dot_general(uv.astype(BF16), gb, TN, preferred_element_type=F32)
            dc = dc + lax.dot_general(dyv.astype(BF16), sb, TN, preferred_element_type=F32)
            dd = dd + jnp.sum(dyv * uv, axis=0, keepdims=True)

            @pl.when(first)
            def _():
                du_ref[rows, :] = du + d_ref[...] * dyv

            @pl.when(jnp.logical_not(first))
            def _():
                du_ref[rows, :] += du

        db_ref[...] = db
        dc_ref[...] = dc

        @pl.when(first)
        def _():
            dd_ref[...] = dd

        @pl.when((k % per) == per - 1)
        def _():
            for c in range(SCAN_CHUNKS):
                dut_ref[pl.ds(c * T, T), :] = du_ref[pl.ds(c, T, stride=SCAN_CHUNKS), :].astype(BF16)

    ublk = pl.BlockSpec((L, LANES), lambda k: (0, k // per))
    uslab = pl.BlockSpec((None, L, LANES), lambda k: (3, 0, k // per))
    sblk = pl.BlockSpec((L, W), lambda k: (0, k))
    lam = pl.BlockSpec((None, 1, W), lambda k: (k, 0, 0))
    vec = pl.BlockSpec((1, LANES), lambda k: (0, k // per))
    mat = pl.BlockSpec((None, LANES, 2 * W), lambda k: (k, 0, 0))
    return pl.pallas_call(
        body, name=name,
        out_shape=(jax.ShapeDtypeStruct(dproj.shape, dproj.dtype), jax.ShapeDtypeStruct((ns, LANES, 2 * W), F32),
                   jax.ShapeDtypeStruct((ns, LANES, 2 * W), F32), jax.ShapeDtypeStruct((ns, 2, W), F32),
                   jax.ShapeDtypeStruct((1, Du), F32)),
        grid=(ns,),
        in_specs=[ublk, uslab, pl.BlockSpec(memory_space=pl.ANY), sblk, sblk, lam, lam, mat,
                  pl.BlockSpec((None, 2 * W, LANES), lambda k: (k, 0, 0)), vec],
        out_specs=(uslab, mat, mat, pl.BlockSpec((None, 2, W), lambda k: (k, 0, 0)), vec),
        scratch_shapes=[pltpu.VMEM((L, W), F32), pltpu.VMEM((L, W), F32), pltpu.VMEM((L, LANES), F32),
                        pltpu.VMEM((L, LANES), F32)],
        input_output_aliases={2: 0}, compiler_params=_cparams(("arbitrary",), VMEM_LIMIT_S5),
    )(dy, proj4, dproj, s_re, s_im, lr, li, bmat.astype(BF16), cmat.astype(BF16), d.reshape(1, Du))


def _glu_fwd(yraw, wmat, bias, mixin, name):
    L, C = yraw.shape
    tr = _pick(L, prefs=(512, 256, 128))
    tb = tr // SCAN_CHUNKS
    nl = C // LANES

    def body(y_ref, w_ref, b_ref, m_in, o_ref, scr):
        yg = _gelu(y_ref[...])
        zz = jnp.dot(yg.astype(BF16), w_ref[...], preferred_element_type=F32) + b_ref[...]
        yb = yg * _sigmoid(zz)
        for k in range(nl):
            scr[k] = yb[:, k * LANES:(k + 1) * LANES]
        for c in range(SCAN_CHUNKS):
            for k in range(nl):
                o_ref[c, :, k * LANES:(k + 1) * LANES] = scr[k, pl.ds(c, tb, stride=SCAN_CHUNKS), :].astype(BF16)

    out = pl.pallas_call(
        body, name=name, out_shape=jax.ShapeDtypeStruct((SCAN_CHUNKS, L // SCAN_CHUNKS, 2 * C), BF16),
        grid=(L // tr,),
        in_specs=[pl.BlockSpec((tr, C), lambda i: (i, 0)), pl.BlockSpec((C, C), lambda i: (0, 0)),
                  pl.BlockSpec((1, C), lambda i: (0, 0)), pl.BlockSpec(memory_space=pl.ANY)],
        out_specs=pl.BlockSpec((SCAN_CHUNKS, tb, C), lambda i: (0, i, 1)),
        scratch_shapes=[pltpu.VMEM((nl, tr, LANES), F32)], input_output_aliases={3: 0},
        compiler_params=_cparams(("parallel",)),
    )(yraw, wmat, bias.reshape(1, C), mixin.reshape(SCAN_CHUNKS, L // SCAN_CHUNKS, 2 * C))
    return out.reshape(L, 2 * C)


def _glu_bwd(yraw, dmix, wmat, bias, name):
    L, C = yraw.shape
    tr = _pick(L, prefs=(512, 256, 128))
    nsteps = L // tr
    tb = tr // SCAN_CHUNKS
    nl = C // LANES

    def body(y_ref, d_ref, w_ref, b_ref, dy_ref, dw_ref, db_ref, acc_b, scr):
        i = pl.program_id(0)

        @pl.when(i == 0)
        def _():
            dw_ref[...] = jnp.zeros_like(dw_ref)
            acc_b[...] = jnp.zeros_like(acc_b)

        for c in range(SCAN_CHUNKS):
            for k in range(nl):
                scr[k, pl.ds(c, tb, stride=SCAN_CHUNKS), :] = d_ref[c, :, k * LANES:(k + 1) * LANES]
        yr = y_ref[...]
        yg = _gelu(yr)
        ygb = yg.astype(BF16)
        sg = _sigmoid(jnp.dot(ygb, w_ref[...], preferred_element_type=F32) + b_ref[...])
        dyb_ = jnp.concatenate([scr[k] for k in range(nl)], axis=1)
        dz = dyb_ * yg * sg * (1.0 - sg)
        dzb = dz.astype(BF16)
        dyg = dyb_ * sg + lax.dot_general(dzb, w_ref[...], (((1,), (1,)), ((), ())), preferred_element_type=F32)
        dw_ref[...] += lax.dot_general(ygb, dzb, (((0,), (0,)), ((), ())), preferred_element_type=F32)
        acc_b[...] += jnp.sum(dz.reshape(tr // SUBLANES, SUBLANES, C), axis=0)
        dy_ref[...] = dyg * _gelu_grad(yr)

        @pl.when(i == nsteps - 1)
        def _():
            db_ref[...] = jnp.sum(acc_b[...], axis=0, keepdims=True)

    row = pl.BlockSpec((tr, C), lambda i: (i, 0))
    return pl.pallas_call(
        body, name=name,
        out_shape=(jax.ShapeDtypeStruct((L, C), F32), jax.ShapeDtypeStruct((C, C), F32),
                   jax.ShapeDtypeStruct((1, C), F32)),
        grid=(nsteps,),
        in_specs=[row, pl.BlockSpec((SCAN_CHUNKS, tb, C), lambda i: (0, i, 1)), pl.BlockSpec((C, C), lambda i: (0, 0)),
                  pl.BlockSpec((1, C), lambda i: (0, 0))],
        out_specs=(row, pl.BlockSpec((C, C), lambda i: (0, 0)), pl.BlockSpec((1, C), lambda i: (0, 0))),
        scratch_shapes=[pltpu.VMEM((SUBLANES, C), F32), pltpu.VMEM((nl, tr, LANES), F32)],
        compiler_params=_cparams(("arbitrary",)),
    )(yraw, dmix.reshape(SCAN_CHUNKS, L // SCAN_CHUNKS, 2 * C), wmat, bias.reshape(1, C))


def _pool_counts(L, g):
    t = lax.broadcasted_iota(jnp.int32, (L, LANES), 0).astype(F32) + 1.0
    w = jnp.where(g == 0, 2.0, jnp.where(g == 1, 4.0, jnp.where(g == 2, 8.0, 16.0)))
    return 1.0 / jnp.minimum(t, w)


def _select_window(g, a2, a4, a8, a16):
    return jnp.where(g == 0, a2, jnp.where(g == 1, a4, jnp.where(g == 2, a8, a16)))


def _pooled(z, g):
    a2 = z + _down(z, 1)
    a4 = a2 + _down(a2, 2)
    a8 = a4 + _down(a4, 4)
    a16 = a8 + _down(a8, 8)
    return _select_window(g, a2, a4, a8, a16) * _pool_counts(z.shape[0], g) - z


def _transpose_on_mxu(yb):
    c = yb.shape[1]
    eye = lax.broadcasted_iota(jnp.int32, (c, c), 0) == lax.broadcasted_iota(jnp.int32, (c, c), 1)
    return lax.dot_general(eye.astype(BF16), yb, (((1,), (1,)), ((), ())), preferred_element_type=F32).astype(BF16)


def _pool_fwd(proj3, pool_w, scale, name):
    _, L, C = proj3.shape
    ng = len(POOL_WINDOWS)
    pg = C // ng
    assert pg == LANES

    def body(z_ref, w_ref, s_ref, o_ref, ot_ref):
        g = pl.program_id(0)
        p = _pooled(z_ref[...].astype(F32), g)
        y = jnp.dot(p.astype(BF16), w_ref[...].astype(BF16), preferred_element_type=F32)
        yb = (y * s_ref[...]).astype(BF16)
        o_ref[...] = yb
        ot_ref[...] = _transpose_on_mxu(yb)

    return pl.pallas_call(
        body, name=name, out_shape=(jax.ShapeDtypeStruct((L, 2 * C), BF16), jax.ShapeDtypeStruct((2 * C, L), BF16)),
        grid=(ng,),
        in_specs=[pl.BlockSpec((None, L, pg), lambda g: (0, 0, g)), pl.BlockSpec((None, pg, pg), lambda g: (g, 0, 0)),
                  pl.BlockSpec((1, pg), lambda g: (0, g))],
        out_specs=(pl.BlockSpec((L, pg), lambda g: (0, g)), pl.BlockSpec((pg, L), lambda g: (g, 0))),
        compiler_params=_cparams(("parallel",)),
    )(proj3, pool_w, scale.reshape(1, C))


def _pool_bwd(proj3, dmix, pool_w, scale, name):
    _, L, C = proj3.shape
    ng = len(POOL_WINDOWS)
    pg = C // ng

    def body(z_ref, d_ref, w_ref, s_ref, dz_ref, dw_ref, ds_ref):
        g = pl.program_id(0)
        p = _pooled(z_ref[...].astype(F32), g)
        pb = p.astype(BF16)
        wb = w_ref[...].astype(BF16)
        pre = jnp.dot(pb, wb, preferred_element_type=F32)
        dyc = d_ref[...]
        ds_ref[...] = jnp.sum(dyc * pre, axis=0, keepdims=True)
        dpre = (dyc * s_ref[...]).astype(BF16)
        dw_ref[...] = lax.dot_general(pb, dpre, (((0,), (0,)), ((), ())), preferred_element_type=F32)
        dp = lax.dot_general(dpre, wb, (((1,), (1,)), ((), ())), preferred_element_type=F32)
        v = dp * _pool_counts(L, g)
        a2 = v + _up(v, 1)
        a4 = a2 + _up(a2, 2)
        a8 = a4 + _up(a4, 4)
        a16 = a8 + _up(a8, 8)
        dz_ref[...] = (_select_window(g, a2, a4, a8, a16) - dp).astype(BF16)

    return pl.pallas_call(
        body, name=name,
        out_shape=(jax.ShapeDtypeStruct((L, C), BF16), jax.ShapeDtypeStruct((ng, pg, pg), F32),
                   jax.ShapeDtypeStruct((1, C), F32)),
        grid=(ng,),
        in_specs=[pl.BlockSpec((None, L, pg), lambda g: (0, 0, g)), pl.BlockSpec((L, pg), lambda g: (0, g)),
                  pl.BlockSpec((None, pg, pg), lambda g: (g, 0, 0)), pl.BlockSpec((1, pg), lambda g: (0, g))],
        out_specs=(pl.BlockSpec((L, pg), lambda g: (0, g)), pl.BlockSpec((None, pg, pg), lambda g: (g, 0, 0)),
                   pl.BlockSpec((1, pg), lambda g: (0, g))),
        compiler_params=_cparams(("parallel",)),
    )(proj3, dmix, pool_w, scale.reshape(1, C))


def _tril_w(w_ref, h):
    r = lax.broadcasted_iota(jnp.int32, (CHUNK, CHUNK), 0)
    c = lax.broadcasted_iota(jnp.int32, (CHUNK, CHUNK), 1)
    return jnp.where(r >= c, w_ref[h], 0.0)


def _sgu_fwd(proj3, norm_g, w, b, mixin, mixin_t, name):
    _, L, C = proj3.shape
    nh = w.shape[0]
    dh = C // nh
    assert dh == LANES and w.shape[1] == CHUNK
    tr = _pick(L, prefs=(512, 256, 128))
    bfull = jnp.broadcast_to(b[:, :, None], (nh, CHUNK, dh))

    def body(su_ref, sv_ref, g_ref, w_ref, b_ref, m_in, mt_in, o_ref, ot_ref):
        sv = _gelu(sv_ref[...].astype(F32))
        r = lax.rsqrt(jnp.mean(sv * sv, axis=-1, keepdims=True) + EPS)
        v = (sv * r * g_ref[...]).astype(BF16)
        for h in range(nh):
            wm = _tril_w(w_ref, h).astype(BF16)
            cols = slice(h * dh, (h + 1) * dh)
            for n in range(tr // CHUNK):
                rows = slice(n * CHUNK, (n + 1) * CHUNK)
                mixed = jnp.dot(wm, v[rows, cols], preferred_element_type=F32) + b_ref[h]
                o_ref[rows, cols] = (_gelu(su_ref[rows, cols].astype(F32)) * mixed).astype(BF16)
        ot_ref[...] = _transpose_on_mxu(o_ref[...])

    full = lambda shp: pl.BlockSpec(shp, lambda i: (0,) * len(shp))
    anywhere = pl.BlockSpec(memory_space=pl.ANY)
    return pl.pallas_call(
        body, name=name, out_shape=(jax.ShapeDtypeStruct(mixin.shape, BF16), jax.ShapeDtypeStruct(mixin_t.shape, BF16)),
        grid=(L // tr,),
        in_specs=[pl.BlockSpec((None, tr, C), lambda i: (1, i, 0)), pl.BlockSpec((None, tr, C), lambda i: (2, i, 0)),
                  full((1, C)), full((nh, CHUNK, CHUNK)), full((nh, CHUNK, dh)), anywhere, anywhere],
        out_specs=(pl.BlockSpec((tr, C), lambda i: (i, 1)), pl.BlockSpec((C, tr), lambda i: (1, i))),
        input_output_aliases={5: 0, 6: 1}, compiler_params=_cparams(("parallel",)),
    )(proj3, proj3, norm_g.reshape(1, C), w, bfull, mixin, mixin_t)


def _sgu_bwd(proj3, dmix, dz, norm_g, w, b, name):
    _, L, C = proj3.shape
    nh = w.shape[0]
    dh = C // nh
    tr = _pick(L, prefs=(512, 256, 128))
    nsteps = L // tr
    bfull = jnp.broadcast_to(b[:, :, None], (nh, CHUNK, dh))

    def body(su_ref, sv_ref, d_ref, dz_ref, g_ref, w_ref, b_ref, o_ref, dw_ref, db_ref, dg_ref, dv_ref, acc_g):
        i = pl.program_id(0)
        o_ref[0] = dz_ref[...]

        @pl.when(i == 0)
        def _():
            dw_ref[...] = jnp.zeros_like(dw_ref)
            db_ref[...] = jnp.zeros_like(db_ref)
            acc_g[...] = jnp.zeros_like(acc_g)

        svp = sv_ref[...].astype(F32)
        sv = _gelu(svp)
        r = lax.rsqrt(jnp.mean(sv * sv, axis=-1, keepdims=True) + EPS)
        vh = sv * r
        gv = g_ref[...]
        v = (vh * gv).astype(BF16)
        tri_r = lax.broadcasted_iota(jnp.int32, (CHUNK, CHUNK), 0)
        tri_c = lax.broadcasted_iota(jnp.int32, (CHUNK, CHUNK), 1)
        for h in range(nh):
            wm = _tril_w(w_ref, h).astype(BF16)
            cols = slice(h * dh, (h + 1) * dh)
            dwh = jnp.zeros((CHUNK, CHUNK), F32)
            dbh = jnp.zeros((CHUNK, dh), F32)
            for n in range(tr // CHUNK):
                rows = slice(n * CHUNK, (n + 1) * CHUNK)
                vb = v[rows, cols]
                mixed = jnp.dot(wm, vb, preferred_element_type=F32) + b_ref[h]
                sup = su_ref[rows, cols].astype(F32)
                dyd = d_ref[rows, cols]
                dmx = dyd * _gelu(sup)
                o_ref[1, rows, cols] = (dyd * mixed * _gelu_grad(sup)).astype(BF16)
                dmb = dmx.astype(BF16)
                dwh = dwh + lax.dot_general(dmb, vb, (((1,), (1,)), ((), ())), preferred_element_type=F32)
                dbh = dbh + dmx
                dv_ref[rows, cols] = lax.dot_general(wm, dmb, (((0,), (0,)), ((), ())), preferred_element_type=F32)
            dw_ref[h] += jnp.where(tri_r >= tri_c, dwh, 0.0)
            db_ref[h] += dbh
        dv = dv_ref[...]
        acc_g[...] += jnp.sum((dv * vh).reshape(tr // SUBLANES, SUBLANES, C), axis=0)
        dvg = dv * gv
        dsv = r * (dvg - vh * jnp.mean(dvg * vh, axis=-1, keepdims=True))
        o_ref[2] = (dsv * _gelu_grad(svp)).astype(BF16)

        @pl.when(i == nsteps - 1)
        def _():
            dg_ref[...] = jnp.sum(acc_g[...], axis=0, keepdims=True)

    full = lambda shp: pl.BlockSpec(shp, lambda i: (0,) * len(shp))
    return pl.pallas_call(
        body, name=name,
        out_shape=(jax.ShapeDtypeStruct((3, L, C), BF16), jax.ShapeDtypeStruct((nh, CHUNK, CHUNK), F32),
                   jax.ShapeDtypeStruct((nh, CHUNK, dh), F32), jax.ShapeDtypeStruct((1, C), F32)),
        grid=(nsteps,),
        in_specs=[pl.BlockSpec((None, tr, C), lambda i: (1, i, 0)), pl.BlockSpec((None, tr, C), lambda i: (2, i, 0)),
                  pl.BlockSpec((tr, C), lambda i: (i, 1)), pl.BlockSpec((tr, C), lambda i: (i, 0)), full((1, C)),
                  full((nh, CHUNK, CHUNK)), full((nh, CHUNK, dh))],
        out_specs=(pl.BlockSpec((3, tr, C), lambda i: (0, i, 0)), full((nh, CHUNK, CHUNK)), full((nh, CHUNK, dh)),
                   full((1, C))),
        scratch_shapes=[pltpu.VMEM((tr, C), F32), pltpu.VMEM((SUBLANES, C), F32)],
        compiler_params=_cparams(("arbitrary",)),
    )(proj3, proj3, dmix, dz, norm_g.reshape(1, C), w, bfull)


def _ffn_act_fwd(up3, conv_w, conv_b, name):
    _, L, Fh = up3.shape
    cb = LANES
    w2 = conv_w.reshape(3, 2, Fh).transpose(1, 0, 2)
    b2 = conv_b.reshape(2, 1, Fh)

    def body(u_ref, w_ref, b_ref, o_ref, ot_ref, gv_ref):
        g = _conv3(_taps(u_ref[0].astype(F32)), w_ref[0]) + b_ref[0]
        v = _conv3(_taps(u_ref[1].astype(F32)), w_ref[1]) + b_ref[1]
        gv_ref[0] = g.astype(BF16)
        gv_ref[1] = v.astype(BF16)
        ab = (g * _sigmoid(g) * v).astype(BF16)
        o_ref[...] = ab
        ot_ref[...] = _transpose_on_mxu(ab)

    blk3 = pl.BlockSpec((2, L, cb), lambda j: (0, 0, j))
    return pl.pallas_call(
        body, name=name,
        out_shape=(jax.ShapeDtypeStruct((L, Fh), BF16), jax.ShapeDtypeStruct((Fh, L), BF16),
                   jax.ShapeDtypeStruct((2, L, Fh), BF16)),
        grid=(Fh // cb,),
        in_specs=[blk3, pl.BlockSpec((2, 3, cb), lambda j: (0, 0, j)), pl.BlockSpec((2, 1, cb), lambda j: (0, 0, j))],
        out_specs=(pl.BlockSpec((L, cb), lambda j: (0, j)), pl.BlockSpec((cb, L), lambda j: (j, 0)), blk3),
        compiler_params=_cparams(("parallel",)),
    )(up3, w2, b2)


def _ffn_act_bwd(up3, gv3, da, conv_w, h2t, name):
    _, L, Fh = up3.shape
    D = h2t.shape[0]
    cb = LANES
    nb = Fh // cb
    w2 = conv_w.reshape(3, 2, Fh).transpose(1, 0, 2)

    def body(u_ref, gv_ref, d_ref, w_ref, h_ref, o_ref, dw_ref, db_ref, wg_ref, wv_ref, scr):
        j = pl.program_id(0)

        @pl.when(j == 0)
        def _():
            scr[1] = jnp.zeros((2, L, cb), BF16)

        prev = scr.at[(j + 1) % 2]
        wg_ref[...] = jnp.dot(h_ref[...], prev[0], preferred_element_type=F32).astype(BF16)
        wv_ref[...] = jnp.dot(h_ref[...], prev[1], preferred_element_type=F32).astype(BF16)
        tg, tv = _taps(u_ref[0].astype(F32)), _taps(u_ref[1].astype(F32))
        wg, wv = w_ref[0], w_ref[1]
        g = gv_ref[0].astype(F32)
        v = gv_ref[1].astype(F32)
        sg = _sigmoid(g)
        dav = d_ref[...].astype(F32)
        dg = dav * v * (sg * (1.0 + g * (1.0 - sg)))
        dv = dav * (g * sg)
        dug = _conv3_t(dg, wg).astype(BF16)
        duv = _conv3_t(dv, wv).astype(BF16)
        o_ref[0] = dug
        o_ref[1] = duv
        cur = scr.at[j % 2]
        cur[0] = dug
        cur[1] = duv
        for tap, (dwg, dwv) in enumerate(zip(_conv3_dw(dg, tg), _conv3_dw(dv, tv))):
            dw_ref[0, tap:tap + 1, :] = dwg
            dw_ref[1, tap:tap + 1, :] = dwv
        db_ref[0] = jnp.sum(dg, axis=0, keepdims=True)
        db_ref[1] = jnp.sum(dv, axis=0, keepdims=True)

    here = lambda j: jnp.minimum(j, nb - 1)
    before = lambda j: jnp.maximum(j - 1, 0)
    blk3 = pl.BlockSpec((2, L, cb), lambda j: (0, 0, here(j)))
    dup, dw2, db2, dwg, dwv = pl.pallas_call(
        body, name=name,
        out_shape=(jax.ShapeDtypeStruct((2, L, Fh), BF16), jax.ShapeDtypeStruct((2, 3, Fh), F32),
                   jax.ShapeDtypeStruct((2, 1, Fh), F32), jax.ShapeDtypeStruct((D, Fh), BF16),
                   jax.ShapeDtypeStruct((D, Fh), BF16)),
        grid=(nb + 1,),
        in_specs=[blk3, blk3, pl.BlockSpec((L, cb), lambda j: (0, here(j))),
                  pl.BlockSpec((2, 3, cb), lambda j: (0, 0, here(j))), pl.BlockSpec((D, L), lambda j: (0, 0))],
        out_specs=(blk3, pl.BlockSpec((2, 3, cb), lambda j: (0, 0, here(j))),
                   pl.BlockSpec((2, 1, cb), lambda j: (0, 0, here(j))),
                   pl.BlockSpec((D, cb), lambda j: (0, before(j))), pl.BlockSpec((D, cb), lambda j: (0, before(j)))),
        scratch_shapes=[pltpu.VMEM((2, 2, L, cb), BF16)],
        compiler_params=_cparams(("arbitrary",), VMEM_LIMIT_S5),
    )(up3, gv3, da, w2, h2t)
    return dup, dw2.transpose(1, 0, 2).reshape(3, 2 * Fh), db2.reshape(2 * Fh), jnp.concatenate([dwg, dwv], axis=1)


def _local_step(x, tgt, w, layer_weights, on_layer_grads):
    L, D = x.shape
    depth = w['norm_mix_g'].shape[0]
    saved = []
    for i in range(depth):
        j = i // 2
        wb = dict(layer_weights(2 * i, x))
        s = {'x': x, 'wb': wb}
        if i % 2 == 0:
            proj4, s['hT'] = _norm_mm(x, w['norm_mix_g'][i], wb['even_w_in'], BF16, "even_in_fwd", ok=('seg', 4))
            s['proj'] = proj4
            mixin = _sconv_fwd(proj4, w['even_conv_w'][j], "sconv_fwd")
            prm = (w['ssm_log_step'][j], w['ssm_a_re'][j], w['ssm_a_im'][j], w['ssm_b_re'][j], w['ssm_b_im'][j],
                   w['ssm_c_re'][j], w['ssm_c_im'][j])
            (lr, li, bmat, cmat), prep_vjp = jax.vjp(_s5_prep, *prm)
            yraw, s_re, s_im = _s5_fwd(proj4, lr, li, bmat, cmat, w['ssm_d'][j], "s5_fwd")
            mixin = _glu_fwd(yraw, wb['ssm_glu_w'], w['ssm_glu_b'][j], mixin, "glu_fwd")
            s.update(yraw=yraw, s_re=s_re, s_im=s_im, s5=(lr, li, bmat, cmat), prep_vjp=prep_vjp)
            s['mixinT'] = mixin.T
            x = _mm(mixin, wb['even_w_out'], 'nn', F32, "even_out_fwd", res=x)
        else:
            proj3, s['hT'] = _norm_mm(x, w['norm_mix_g'][i], wb['odd_w_in'], BF16, "odd_in_fwd", ok=('seg', 3))
            s['proj'] = proj3
            mixin, mixin_t = _pool_fwd(proj3, w['pool_w'][j], w['pool_scale'][j], "pool_fwd")
            mixin, s['mixinT'] = _sgu_fwd(proj3, w['sgu_norm_g'][j], w['sgu_w'][j], w['sgu_b'][j], mixin, mixin_t,
                                          "sgu_fwd")
            x = _mm(mixin, wb['odd_w_out'], 'nn', F32, "odd_out_fwd", res=x)
        s['x1'] = x
        wb.update(layer_weights(2 * i + 1, x))
        up3, h2t = _norm_mm(x, w['norm_ffn_g'][i], wb['ffn_w_up'], BF16, "ffn_up_fwd", ok=('seg', 2))
        a, at, gv3 = _ffn_act_fwd(up3, w['ffn_conv_w'][i], w['ffn_conv_b'][i], "ffn_act_fwd")
        x = _mm(a, wb['ffn_w_down'], 'nn', F32, "ffn_down_fwd", res=x)
        s.update(h2T=h2t, up3=up3, aT=at, gv3=gv3)
        saved.append(s)

    loss8, dx, dxb, dg_final = _loss_head(x, w['norm_final_g'], tgt)
    gs = {n: [None] * w[n].shape[0] for n in SMALL if n != 'norm_final_g'}
    gs['norm_final_g'] = dg_final.reshape(D)

    dep = None
    for i in reversed(range(depth)):
        j = i // 2
        s = saved[i]
        wb = s['wb']
        gb = {}
        da = _mm(dxb, wb['ffn_w_down'], 'nt', BF16, "ffn_down_dgrad", dep=dep)
        gb['ffn_w_down'] = _mm(s['aT'], dxb, 'nn', BF16, "ffn_down_wgrad")
        dup3, dcw, dcb, gb['ffn_w_up'] = _ffn_act_bwd(s['up3'], s['gv3'], da, w['ffn_conv_w'][i], s['h2T'],
                                                      "ffn_act_bwd")
        gs['ffn_conv_w'][i], gs['ffn_conv_b'][i] = dcw, dcb
        dep = on_layer_grads(2 * i + 1, gb)
        dx, dxb, dg = _mm_norm_bwd(dup3, wb['ffn_w_up'], s['x1'], w['norm_ffn_g'][i], dx, "ffn_up_dgrad",
                              ak=('seg', 2), dep=dep)
        gs['norm_ffn_g'][i] = dg.reshape(D)
        gb = {}
        if i % 2 == 0:
            dmix = _mm(dxb, wb['even_w_out'], 'nt', F32, "even_out_dgrad")
            gb['even_w_out'] = _mm(s['mixinT'], dxb, 'nn', BF16, "even_out_wgrad")
            dproj, dcw = _sconv_bwd(s['proj'], dmix, w['even_conv_w'][j], "sconv_bwd")
            gs['even_conv_w'][j] = dcw
            dyraw, dglu_w, dglu_b = _glu_bwd(s['yraw'], dmix, wb['ssm_glu_w'], w['ssm_glu_b'][j], "glu_bwd")
            gb['ssm_glu_w'] = dglu_w.astype(BF16)
            gs['ssm_glu_b'][j] = dglu_b.reshape(-1)
            lr, li, bmat, cmat = s['s5']
            dproj, dbm, dcm, dlam, dd = _s5_bwd(dyraw, s['proj'], dproj, s['s_re'], s['s_im'], lr, li, bmat, cmat,
                                               w['ssm_d'][j], "s5_bwd")
            gs['ssm_d'][j] = dd.reshape(-1)
            dcm = jnp.swapaxes(dcm, 1, 2)
            dprm = s['prep_vjp']((dlam[:, 0:1, :], dlam[:, 1:2, :], dbm, dcm))
            for n, gval in zip(('ssm_log_step', 'ssm_a_re', 'ssm_a_im', 'ssm_b_re', 'ssm_b_im', 'ssm_c_re',
                                'ssm_c_im'), dprm):
                gs[n][j] = gval
            gb['even_w_in'] = _mm(s['hT'], dproj, 'nn', BF16, "even_in_wgrad", bk=('seg', 4))
            w_in, in_kind, in_name = wb['even_w_in'], ('seg', 4), "even_in_dgrad"
        else:
            dmix = _mm(dxb, wb['odd_w_out'], 'nt', F32, "odd_out_dgrad")
            gb['odd_w_out'] = _mm(s['mixinT'], dxb, 'nn', BF16, "odd_out_wgrad")
            dz, dpw, dps = _pool_bwd(s['proj'], dmix, w['pool_w'][j], w['pool_scale'][j], "pool_bwd")
            gs['pool_w'][j], gs['pool_scale'][j] = dpw, dps.reshape(-1)
            dproj, dsw, dsb, dsg = _sgu_bwd(s['proj'], dmix, dz, w['sgu_norm_g'][j], w['sgu_w'][j], w['sgu_b'][j],
                                            "sgu_bwd")
            gs['sgu_w'][j], gs['sgu_b'][j], gs['sgu_norm_g'][j] = dsw, jnp.sum(dsb, axis=-1), dsg.reshape(-1)
            gb['odd_w_in'] = _mm(s['hT'], dproj, 'nn', BF16, "odd_in_wgrad", bk=('seg', 3))
            w_in, in_kind, in_name = wb['odd_w_in'], ('seg', 3), "odd_in_dgrad"
        dep = on_layer_grads(2 * i, gb)
        dx, dxb, dg = _mm_norm_bwd(dproj, w_in, s['x'], w['norm_mix_g'][i], dx, in_name, ak=in_kind, dep=dep)
        gs['norm_mix_g'][i] = dg.reshape(D)

    gsmall = {n: (v if n == 'norm_final_g' else jnp.stack(v)) for n, v in gs.items()}
    return loss8[0, 0], dx, gsmall


_HBM = pl.BlockSpec(memory_space=pltpu.HBM)
_CHIP_FLIPS = ((0, 0), (1, 0), (0, 1), (1, 1))


def _coords():
    return lax.axis_index("x"), lax.axis_index("y"), lax.axis_index("c")


def _flip(v, f):
    return 1 - v if f else v


def _shard_of(ref, axis, s, width):
    start = pl.multiple_of(s * width, LANES if axis == ref.ndim - 1 else 16) if width % 16 == 0 else s * width
    idx = [slice(None)] * ref.ndim
    idx[axis] = pl.ds(start, width)
    return ref.at[tuple(idx)]


_SEM = pl.BlockSpec(memory_space=pltpu.SEMAPHORE)
_ANY = pl.BlockSpec(memory_space=pl.ANY)
_DATAFLOW = pltpu.SideEffectType.DATAFLOW_SIDE_EFFECTING


def _in_hbm(a):
    return pltpu.with_memory_space_constraint(a, pltpu.HBM)


def _model_layer(name, l):
    if name.startswith('ffn'):
        return l
    return 2 * l + 1 if name.startswith('odd') else 2 * l


def _place_quarter(shard, l, axis, chip, dtype, dep=None):
    _, r, c = shard.shape
    tr = _pick(r, prefs=(512, 256, 128, 64, 32, 16))
    nrb = r // tr

    def body(chip_ref, i_ref, *rest):
        rest[-1][...] = i_ref[...].astype(dtype)

    if axis == 1:
        out_shape, o_map = (r, c * N_CHIPS), (lambda i, s: (i, s[0]))
    else:
        out_shape, o_map = (r * N_CHIPS, c), (lambda i, s: (s[0] * nrb + i, 0))
    in_specs = [pl.BlockSpec((None, tr, c), lambda i, s: (l, i, 0))]
    args = [chip, shard]
    if dep is not None:
        in_specs.append(pl.BlockSpec(memory_space=pl.ANY))
        args.append(dep)
    return pl.pallas_call(
        body, name="place_quarter", out_shape=jax.ShapeDtypeStruct(out_shape, dtype),
        grid_spec=pltpu.PrefetchScalarGridSpec(
            num_scalar_prefetch=1, grid=(nrb,), in_specs=in_specs, out_specs=pl.BlockSpec((tr, c), o_map)),
        compiler_params=_cparams(("parallel",)),
    )(*args)


def _gather_copies(land_refs, send_sem, recv_sem, axes, landing_chip_of, first=0):
    x, y, c = _coords()
    out = []
    for j, land in enumerate(land_refs):
        width = land.shape[axes[j]] // N_CHIPS
        for f in (1, 2, 3):
            fx, fy = _CHIP_FLIPS[f]
            px, py = _flip(x, fx), _flip(y, fy)
            lx, ly = landing_chip_of(px, py)
            out.append(pltpu.make_async_remote_copy(
                src_ref=_shard_of(land, axes[j], 2 * x + y, width), dst_ref=_shard_of(land, axes[j], 2 * lx + ly, width),
                send_sem=send_sem.at[3 * (first + j) + f - 1], recv_sem=recv_sem.at[3 * (first + j) + f - 1],
                device_id=(px, py, c), device_id_type=MESH))
    return out


def _gather_start(tag, lands, axes, dep=None):
    n = len(lands)

    def body(*refs):
        land_refs, send_sem, recv_sem = refs[:n], refs[-3], refs[-2]
        x, y, _ = _coords()
        for cp in _gather_copies(land_refs, send_sem, recv_sem, axes, lambda px, py: (x, y)):
            cp.start()
        refs[-1][...] = jnp.zeros_like(refs[-1])

    thru = [pltpu.HBM(a.shape, a.dtype) for a in lands]
    outs = pl.pallas_call(
        body, name=f"gather_start_{tag}",
        out_shape=tuple(thru + [pltpu.SemaphoreType.DMA((3 * n,)), pltpu.SemaphoreType.DMA((3 * n,)),
                                jax.ShapeDtypeStruct((SUBLANES, LANES), F32)]),
        in_specs=[_HBM] * n + ([_ANY] if dep is not None else []),
        out_specs=tuple([_HBM] * n + [_SEM, _SEM, pl.BlockSpec(memory_space=pltpu.VMEM)]),
        input_output_aliases={i: i for i in range(n)},
        compiler_params=pltpu.CompilerParams(has_side_effects=_DATAFLOW),
    )(*[_in_hbm(a) for a in lands], *([dep] if dep is not None else []))
    return list(outs[:n]), outs[n], outs[n + 1], outs[n + 2]


def _gather_wait(tag, lands, send_sem, recv_sem, axes, after, first=0):
    n = len(lands)

    def body(*refs):
        for cp in _gather_copies(refs[:n], refs[n], refs[n + 1], axes, lambda px, py: (px, py), first):
            cp.wait_send()
            cp.wait_recv()

    outs = pl.pallas_call(
        body, name=f"gather_wait_{tag}", out_shape=tuple(pltpu.HBM(a.shape, a.dtype) for a in lands),
        in_specs=[_HBM] * n + [_SEM, _SEM, _ANY], out_specs=tuple([_HBM] * n),
        input_output_aliases={i: i for i in range(n)},
        compiler_params=pltpu.CompilerParams(has_side_effects=_DATAFLOW),
    )(*lands, send_sem, recv_sem, after)
    return list(outs)


N_SLOTS = N_DEV - 1


def _scatter_sends(grad_refs, land_refs, send_sem, recv_sem, meta):
    x, y, c = _coords()
    out = []
    for j, (axis, owner, q, width) in enumerate(meta):
        other = c if owner == 0 else 1 - c
        for f, (fx, fy) in enumerate(_CHIP_FLIPS):
            px, py = _flip(x, fx), _flip(y, fy)
            slot = f + 4 * other - 1
            out.append((other if f == 0 else None, pltpu.make_async_remote_copy(
                src_ref=_shard_of(grad_refs[j], axis, 2 * px + py, width), dst_ref=land_refs[j].at[q, slot],
                send_sem=send_sem.at[4 * j + f], recv_sem=recv_sem.at[N_SLOTS * j + slot],
                device_id=(px, py, owner), device_id_type=MESH)))
    return out


def _scatter_start(layer, grads, lands, meta):
    n = len(grads)
    uniq = []
    for a in lands:
        if not any(a is u for u in uniq):
            uniq.append(a)
    which = [next(k for k, u in enumerate(uniq) if u is a) for a in lands]
    nu = len(uniq)

    def body(*refs):
        grad_refs, land_u = refs[:n], refs[n:n + nu]
        send_sem, recv_sem = refs[n + nu], refs[n + nu + 1]
        for other, cp in _scatter_sends(grad_refs, [land_u[k] for k in which], send_sem, recv_sem, meta):
            if other is None:
                cp.start()
            else:
                pl.when(other == 1)(cp.start)
        refs[-1][...] = jnp.zeros_like(refs[-1])

    thru = [pltpu.HBM(a.shape, a.dtype) for a in list(grads) + uniq]
    outs = pl.pallas_call(
        body, name=f"scatter_start_{layer}",
        out_shape=tuple([pltpu.SemaphoreType.DMA((4 * n,)), pltpu.SemaphoreType.DMA((N_SLOTS * n,))] + thru
                        + [jax.ShapeDtypeStruct((SUBLANES, LANES), F32)]),
        in_specs=[_HBM] * (n + nu),
        out_specs=tuple([_SEM, _SEM] + [_HBM] * (n + nu) + [pl.BlockSpec(memory_space=pltpu.VMEM)]),
        input_output_aliases={i: 2 + i for i in range(n + nu)},
        compiler_params=pltpu.CompilerParams(has_side_effects=_DATAFLOW),
    )(*[_in_hbm(a) for a in list(grads) + uniq])
    new_lands = [outs[2 + n + k] for k in which]
    return outs[0], outs[1], list(outs[2:2 + n]), new_lands, outs[-1]


def _scatter_wait(started, lands):
    nl = len(lands)
    flat_grads = [g for s in started for g in s[2]]
    ng, ns = len(flat_grads), len(started)

    def body(*refs):
        land_refs = refs[:nl]
        grad_refs = refs[nl:nl + ng]
        sem_refs = refs[nl + ng:nl + ng + 2 * ns]
        _, _, c = _coords()
        off = 0
        for k, (_, _, grads, idx, meta) in enumerate(started):
            send_sem, recv_sem = sem_refs[2 * k], sem_refs[2 * k + 1]
            lr = [land_refs[i] for i in idx]
            for other, cp in _scatter_sends(grad_refs[off:off + len(grads)], lr, send_sem, recv_sem, meta):
                if other is None:
                    cp.wait_send()
                else:
                    pl.when(other == 1)(cp.wait_send)
            for j, (axis, owner, q, width) in enumerate(meta):
                mine = (c if owner == 0 else 1 - c) == 0

                @pl.when(mine)
                def _():
                    for slot in range(N_SLOTS):
                        land = lr[j].at[q, slot]
                        pltpu.make_async_remote_copy(
                            src_ref=land, dst_ref=land, send_sem=send_sem.at[0], recv_sem=recv_sem.at[N_SLOTS * j + slot],
                            device_id=_coords(), device_id_type=MESH).wait_recv()
            off += len(grads)

    args = list(lands) + flat_grads
    thru = [pltpu.HBM(a.shape, a.dtype) for a in args]
    sems = [s for st in started for s in st[:2]]
    outs = pl.pallas_call(
        body, name="scatter_wait", out_shape=tuple(thru), in_specs=[_HBM] * (nl + ng) + [_SEM] * (2 * ns),
        out_specs=tuple([_HBM] * (nl + ng)), input_output_aliases={i: i for i in range(nl + ng)},
        compiler_params=pltpu.CompilerParams(has_side_effects=_DATAFLOW),
    )(*args, *sems)
    return list(outs[:nl]), list(outs[nl:])


def _sum_and_share(recv, layer_grads, axis, chip, name):
    n, ns, r, c = recv.shape
    tr = _pick(r, prefs=(256, 128, 64, 32, 16))
    nr = r // tr
    nsteps = n * nr
    nlay = len(layer_grads)
    own_map = (lambda h, i, s: (i, s[0])) if axis == 1 else (lambda h, i, s: (s[0] * nr + i, 0))

    def body(chip_ref, i_ref, *rest):
        g_refs = rest[:nlay]
        o_ref, buf, loc_sems, send_sems, recv_sems = rest[nlay:]
        h, i = pl.program_id(0), pl.program_id(1)
        step = h * nr + i
        slot = step % 2
        x, y, core = _coords()
        layer = core * n + h
        own = g_refs[0][...]
        for l in range(1, nlay):
            own = jnp.where(layer == l, g_refs[l][...], own)

        def copies(sl):
            dst = o_ref.at[core * n + h, pl.ds(pl.multiple_of(i * tr, tr), tr), :]
            loc = pltpu.make_async_copy(buf.at[sl], dst, loc_sems.at[sl])
            rem = pltpu.make_async_remote_copy(
                src_ref=buf.at[sl], dst_ref=dst, send_sem=send_sems.at[sl], recv_sem=recv_sems.at[step],
                device_id=(x, y, 1 - core), device_id_type=MESH)
            return loc, rem

        def drain(sl):
            loc, rem = copies(sl)
            loc.wait()
            rem.wait_send()

        pl.when(step >= 2)(lambda: drain(slot))
        acc = own.astype(F32)
        for s in range(ns):
            acc = acc + i_ref[s].astype(F32)
        buf[slot] = acc
        loc, rem = copies(slot)
        loc.start()
        rem.start()

        @pl.when(step == nsteps - 1)
        def _():
            drain(slot)
            if nsteps > 1:
                drain(1 - slot)
            for hh in range(n):
                for ii in range(nr):
                    land = o_ref.at[(1 - core) * n + hh, pl.ds(ii * tr, tr), :]
                    pltpu.make_async_remote_copy(
                        src_ref=buf.at[0], dst_ref=land, send_sem=send_sems.at[0], recv_sem=recv_sems.at[hh * nr + ii],
                        device_id=(x, y, 1 - core), device_id_type=MESH).wait_recv()

    return pl.pallas_call(
        body, name=name, out_shape=jax.ShapeDtypeStruct((2 * n, r, c), F32),
        grid_spec=pltpu.PrefetchScalarGridSpec(
            num_scalar_prefetch=1, grid=(n, nr),
            in_specs=[pl.BlockSpec((None, ns, tr, c), lambda h, i, s: (h, 0, i, 0))]
            + [pl.BlockSpec((tr, c), own_map)] * nlay,
            out_specs=_HBM,
            scratch_shapes=[pltpu.VMEM((2, tr, c), F32), pltpu.SemaphoreType.DMA((2,)),
                            pltpu.SemaphoreType.DMA((2,)), pltpu.SemaphoreType.DMA((nsteps,))]),
        compiler_params=_cparams(("arbitrary", "arbitrary")),
    )(chip, recv, *layer_grads)


def _gather_sums_over_chips(part):
    def body(i_ref, o_ref, send_sems, recv_sems):
        x, y, c = _coords()
        o_ref[2 * x + y] = i_ref[...]

        def copy(f, slot_chip):
            fx, fy = _CHIP_FLIPS[f]
            return pltpu.make_async_remote_copy(
                src_ref=i_ref, dst_ref=o_ref.at[2 * slot_chip[0] + slot_chip[1]], send_sem=send_sems.at[f - 1],
                recv_sem=recv_sems.at[f - 1], device_id=(_flip(x, fx), _flip(y, fy), c), device_id_type=MESH)

        sends = [copy(f, (x, y)) for f in (1, 2, 3)]
        for cp in sends:
            cp.start()
        for f in (1, 2, 3):
            fx, fy = _CHIP_FLIPS[f]
            copy(f, (_flip(x, fx), _flip(y, fy))).wait_recv()
        for cp in sends:
            cp.wait_send()

    vmem = pl.BlockSpec(memory_space=pltpu.VMEM)
    return pl.pallas_call(
        body, name="gather_small_sums", out_shape=jax.ShapeDtypeStruct((N_CHIPS,) + part.shape, part.dtype),
        in_specs=[vmem], out_specs=vmem,
        scratch_shapes=[pltpu.SemaphoreType.DMA((3,)), pltpu.SemaphoreType.DMA((3,))],
    )(part)


def _adamw_update(w_ref, g_ref, m_ref, v_ref, d_ref, mo_ref, vo_ref):
    bc1 = 1.0 - ADAM_B1 ** ADAM_STEP
    bc2 = 1.0 - ADAM_B2 ** ADAM_STEP
    gv = g_ref[...]
    mn = ADAM_B1 * m_ref[...] + (1.0 - ADAM_B1) * gv
    vn = ADAM_B2 * v_ref[...] + (1.0 - ADAM_B2) * (gv * gv)
    d_ref[...] = -ADAM_LR * ((mn / bc1) / (jnp.sqrt(vn / bc2) + ADAM_EPS) + ADAM_WD * w_ref[...])
    mo_ref[...] = mn
    vo_ref[...] = vn


def _adamw(w, g, m, v, name):
    def body(*refs):
        _adamw_update(*refs)

    tr = _pick(w.shape[0], prefs=(256, 128, 64, 32, 16, 8))
    blk = pl.BlockSpec((tr, w.shape[1]), lambda i: (i, 0))
    sds = jax.ShapeDtypeStruct(w.shape, F32)
    return pl.pallas_call(
        body, name=name, out_shape=(sds, sds, sds), grid=(w.shape[0] // tr,), in_specs=[blk] * 4,
        out_specs=(blk,) * 3, compiler_params=_cparams(("parallel",)),
    )(w, g, m, v)


def _adamw_many(tensors, name, by_layer=False):
    n = len(tensors)

    def body(*refs):
        for t in range(n):
            _adamw_update(*refs[4 * t:4 * t + 4], *refs[4 * n + 3 * t:4 * n + 3 * t + 3])

    def spec(a):
        nd = a.ndim
        if by_layer:
            return pl.BlockSpec((1,) + a.shape[1:], lambda i: (i,) + (0,) * (nd - 1))
        return pl.BlockSpec(a.shape, lambda i: (0,) * nd)

    steps = tensors[0][0].shape[0] if by_layer else 1
    outs = pl.pallas_call(
        body, name=name, out_shape=tuple(jax.ShapeDtypeStruct(t[0].shape, F32) for t in tensors for _ in range(3)),
        grid=(steps,), in_specs=[spec(a) for t in tensors for a in t],
        out_specs=tuple(spec(t[0]) for t in tensors for _ in range(3)), compiler_params=_cparams(("parallel",)),
    )(*[a for t in tensors for a in t])
    return [tuple(outs[3 * t:3 * t + 3]) for t in range(n)]


_PACK_QUANTUM = 256 * LANES


def _pack(arrs):
    flat = jnp.concatenate([a.reshape(-1).astype(F32) for a in arrs])
    flat = jnp.pad(flat, (0, (-flat.shape[0]) % _PACK_QUANTUM))
    return flat.reshape(-1, LANES)


def _unpack(p, shapes):
    flat = p.reshape(-1)
    out, off = [], 0
    for s in shapes:
        n = int(np.prod(s))
        out.append(flat[off:off + n].reshape(s))
        off += n
    return out


def kernel(*args):
    nw = len(WEIGHTS)
    x, tgt = args[0], args[1 + nw]
    w = dict(zip(WEIGHTS, args[1:1 + nw]))
    m = dict(zip(WEIGHTS, args[2 + nw:2 + 2 * nw]))
    v = dict(zip(WEIGHTS, args[2 + 2 * nw:2 + 3 * nw]))
    _, L, D = x.shape
    chip = 2 * lax.axis_index("x") + lax.axis_index("y")

    big = list(BIG)
    small_sh_shapes = [w[n].shape for n in SMALL_SHARDED]
    nbig = len(big)
    chip1 = chip.reshape(1).astype(jnp.int32)
    axes2 = [BIG[n] - 1 for n in big] + [0]
    shards = [w[n] for n in big] + [_pack([w[n] for n in SMALL_SHARDED])[None]]
    pairs = [(t, l) for t in range(nbig + 1) for l in range(shards[t].shape[0])]
    depth = w['norm_mix_g'].shape[0]
    part_of = lambda t, l: 0 if t == nbig else 2 * _model_layer(big[t], l) + big[t].startswith('ffn')
    flying, token = {}, None
    for tag, gset in enumerate(([0], list(range(1, 2 * depth)))):
        ids = [k for g in gset for k, (t, l) in enumerate(pairs) if part_of(t, l) == g]
        ts = [pairs[k][0] for k in ids]
        placed = [_place_quarter(shards[t], pairs[k][1], axes2[t], chip1, F32 if t == nbig else BF16, token)
                  for k, t in zip(ids, ts)]
        lands, send, recv, token = _gather_start(tag, placed, [axes2[t] for t in ts], token)
        first = 0
        for g in gset:
            n = sum(1 for t, l in pairs if part_of(t, l) == g)
            flying[g] = (ts[first:first + n], lands[first:first + n], send, recv, first)
            first += n

    def wait_group(g, after):
        ts, lands, send, recv, first = flying[g]
        landed = _gather_wait(g, lands, send, recv, [axes2[t] for t in ts], token if after is None else after, first)
        return dict(zip(ts, landed))

    first = wait_group(0, None)
    packed = first.pop(nbig).reshape(N_CHIPS, -1, LANES)
    per_chip = [_unpack(packed[s], small_sh_shapes) for s in range(N_CHIPS)]
    wl = dict(w)
    for k, n in enumerate(SMALL_SHARDED):
        wl[n] = jnp.concatenate([per_chip[s][k] for s in range(N_CHIPS)], axis=-1)

    def layer_weights(i, after):
        got = first if i == 0 else wait_group(i, after)
        return {big[t]: a for t, a in got.items()}

    small_shapes = [(w[n].shape[:-1] + (w[n].shape[-1] * N_CHIPS,)) if n in SMALL_SHARDED else w[n].shape
                    for n in SMALL] + [(1,)]
    n_small = sum(int(np.prod(s)) for s in small_shapes)
    pack_rows = -(-n_small // _PACK_QUANTUM) * _PACK_QUANTUM // LANES
    nlayers = [w[n].shape[0] for n in big] + [2]
    halves = [n // 2 for n in nlayers]
    quarters = [tuple(w[n].shape[1:]) for n in big] + [(pack_rows // 2 // N_CHIPS, LANES)]
    wire = [BF16] * nbig + [F32]
    land_now = [lax.empty((halves[t], N_SLOTS) + quarters[t], wire[t]) for t in range(nbig + 1)]
    gparts = [[None] * n for n in nlayers]
    started = []

    def start_scatter(tag, ts, ls, arrays):
        meta = [(axes2[t], l // halves[t], l % halves[t], quarters[t][axes2[t]]) for t, l in zip(ts, ls)]
        send, recv, thru, new_lands, token = _scatter_start(tag, arrays, [land_now[t] for t in ts], meta)
        for t, ln in zip(ts, new_lands):
            land_now[t] = ln
        started.append((send, recv, thru, ts, meta, ls))
        return token

    def on_layer_grads(g, gb):
        ts = [big.index(n) for n in gb]
        return start_scatter(g, ts, [g // 2 if big[t].startswith('ffn') else g // 4 for t in ts],
                             [gb[big[t]] for t in ts])

    loss, dx, gsmall = _local_step(x.reshape(L, D), tgt.reshape(L, D), wl, layer_weights, on_layer_grads)
    gpack = _pack([gsmall[n] for n in SMALL] + [loss.reshape(1)])
    start_scatter(2 * depth, [nbig, nbig], [0, 1], [gpack[:pack_rows // 2], gpack[pack_rows // 2:]])
    landed, sent = _scatter_wait([s[:5] for s in started], land_now)
    for (t, l), g in zip([(t, l) for s in started for t, l in zip(s[3], s[5])], sent):
        gparts[t][l] = g
    gshard = {n: _sum_and_share(landed[t], gparts[t], axes2[t], chip1, "sum_share_" + n) for t, n in enumerate(big)}
    small_sum = _sum_and_share(landed[nbig], gparts[nbig], 0, chip1, "sum_share_small")
    gpack = _gather_sums_over_chips(small_sum).transpose(1, 0, 2, 3).reshape(pack_rows, LANES)
    gs = dict(zip(SMALL + ['loss'], _unpack(gpack, small_shapes)))
    loss = gs.pop('loss').reshape(())
    for n in SMALL_SHARDED:
        width = w[n].shape[-1]
        gs[n] = lax.dynamic_slice_in_dim(gs[n], chip * width, width, axis=gs[n].ndim - 1)

    grads, delta, new_m, new_v = {}, {}, {}, {}
    for n in big:
        shp = w[n].shape
        flat = lambda a: a.reshape(shp[0] * shp[1], shp[2])
        g = gshard[n]
        grads[n] = g
        d_, m_, v_ = _adamw(flat(w[n]), flat(g), flat(m[n]), flat(v[n]), "adamw_" + n)
        delta[n], new_m[n], new_v[n] = d_.reshape(shp), m_.reshape(shp), v_.reshape(shp)
    sparse = [n for n in SMALL if w[n].ndim == 4 and w[n].shape[-1] < LANES // 2]
    for names, by_layer in ((sparse, True), ([n for n in SMALL if n not in sparse], False)):
        as2d = lambda a: a.reshape(1, -1) if a.ndim == 1 else a
        res = _adamw_many([(as2d(w[n]), as2d(gs[n]), as2d(m[n]), as2d(v[n])) for n in names],
                          "adamw_small_by_layer" if by_layer else "adamw_small", by_layer)
        for n, (d_, m_, v_) in zip(names, res):
            shp = w[n].shape
            grads[n], delta[n], new_m[n], new_v[n] = gs[n], d_.reshape(shp), m_.reshape(shp), v_.reshape(shp)

    return (loss, dx.reshape(1, L, D), *[grads[n] for n in WEIGHTS], *[delta[n] for n in WEIGHTS],
            *[new_m[n] for n in WEIGHTS], *[new_v[n] for n in WEIGHTS])
```

```python
import math

import numpy as np
import jax
import jax.numpy as jnp
from jax import lax
from jax.experimental import pallas as pl
from jax.experimental.pallas import tpu as pltpu

F32 = jnp.float32
BF16 = jnp.bfloat16
MESH = pl.DeviceIdType.MESH

EPS = 1e-6
CHUNK = 128
POOL_WINDOWS = (2, 4, 8, 16)
LANES = 128
SUBLANES = 8
SCAN_CHUNKS = SUBLANES
S5_GROUPS_PER_STEP = 4
MM_TM_CAP, MM_TN_CAP, MM_TK_CAP = 1408, 1408, 2048
MM_TK_WHOLE = 2048
VMEM_LIMIT = 48 * 1024 * 1024
VMEM_LIMIT_S5 = 56 * 1024 * 1024

ADAM_LR, ADAM_B1, ADAM_B2, ADAM_EPS, ADAM_WD, ADAM_STEP = 0.001, 0.9, 0.999, 1e-08, 0.01, 10

WEIGHTS = ['norm_mix_g', 'even_w_in', 'even_conv_w', 'ssm_log_step', 'ssm_a_re', 'ssm_a_im', 'ssm_b_re',
           'ssm_b_im', 'ssm_c_re', 'ssm_c_im', 'ssm_d', 'ssm_glu_w', 'ssm_glu_b', 'even_w_out', 'odd_w_in',
           'pool_w', 'pool_scale', 'sgu_norm_g', 'sgu_w', 'sgu_b', 'odd_w_out', 'norm_ffn_g', 'ffn_w_up',
           'ffn_conv_w', 'ffn_conv_b', 'ffn_w_down', 'norm_final_g']
BIG = {'even_w_in': 2, 'ssm_glu_w': 1, 'even_w_out': 1, 'odd_w_in': 2, 'odd_w_out': 1, 'ffn_w_up': 2,
       'ffn_w_down': 1}
SMALL_SHARDED = ('even_conv_w', 'pool_scale', 'sgu_norm_g', 'ffn_conv_w')
SMALL = [n for n in WEIGHTS if n not in BIG]
N_CHIPS = 4
N_DEV = 8


def _cparams(sem=None, vmem=VMEM_LIMIT):
    kw = dict(vmem_limit_bytes=vmem)
    if sem is not None:
        kw['dimension_semantics'] = sem
    return pltpu.CompilerParams(**kw)


def _hbm(*arrays):
    return [pltpu.with_memory_space_constraint(a, pltpu.HBM) for a in arrays]


def _pick(n, segs=(), prefs=(1024, 512, 256, 128)):
    for t in prefs:
        if n % t == 0 and all(s % t == 0 for s in segs if s):
            return t
    return n


def _largest_tile(n, segs, cap):
    best = None
    for t in range(LANES, min(n, cap) + 1, LANES):
        if n % t == 0 and all(s % t == 0 for s in segs if s):
            best = t
    return best if best is not None else n


def _ldims(arr, kind):
    if kind is None:
        return arr.shape
    if kind[0] == 'lead':
        return arr.shape[1:]
    return (arr.shape[1], arr.shape[0] * arr.shape[2])


def _segw(arr, kind):
    return arr.shape[2] if (kind is not None and kind[0] == 'seg') else None


def _opspec(arr, kind, br, bc, rfn, cfn):
    if kind is None:
        return pl.BlockSpec((br, bc), lambda i, j, k: (rfn(i, j, k), cfn(i, j, k)))
    if kind[0] == 'lead':
        lead = kind[1]
        return pl.BlockSpec((None, br, bc), lambda i, j, k: (lead, rfn(i, j, k), cfn(i, j, k)))
    per = arr.shape[2] // bc
    return pl.BlockSpec((None, br, bc), lambda i, j, k: (cfn(i, j, k) // per, rfn(i, j, k), cfn(i, j, k) % per))


def _mm(a, b, mode, out_dtype, name, ak=None, bk=None, ok=None, res=None, dep=None):
    ar, ac = _ldims(a, ak)
    br_, bc_ = _ldims(b, bk)
    if mode == 'nn':
        M, K, N = ar, ac, bc_
        assert br_ == K
    else:
        M, K, N = ar, ac, br_
        assert bc_ == K
    sa, sb = _segw(a, ak), _segw(b, bk)
    so = (N // ok[1]) if ok is not None else None
    tm = _largest_tile(M, [], MM_TM_CAP)
    tn = _largest_tile(N, [sb if mode == 'nn' else None, so], MM_TN_CAP)
    ksegs = [sa, sb if mode == 'nt' else None]
    tk = K if (K <= MM_TK_WHOLE and not any(ksegs)) else _largest_tile(K, ksegs, MM_TK_CAP)
    nk = K // tk
    I = lambda i, j, k: i
    J = lambda i, j, k: j
    Kk = lambda i, j, k: k
    a_spec = _opspec(a, ak, tm, tk, I, Kk)
    if mode == 'nn':
        b_spec = _opspec(b, bk, tk, tn, Kk, J)
        dims = (((1,), (0,)), ((), ()))
    else:
        b_spec = _opspec(b, bk, tn, tk, J, Kk)
        dims = (((1,), (1,)), ((), ()))
    if ok is None:
        out_shape = jax.ShapeDtypeStruct((M, N), out_dtype)
        o_spec = pl.BlockSpec((tm, tn), lambda i, j, k: (i, j))
    else:
        out_shape = jax.ShapeDtypeStruct((ok[1], M, N // ok[1]), out_dtype)
        per = (N // ok[1]) // tn
        o_spec = pl.BlockSpec((None, tm, tn), lambda i, j, k: (j // per, i, j % per))
    has_res = res is not None

    def body(*refs):
        a_ref, b_ref = refs[0], refs[1]
        r_ref = refs[2] if has_res else None
        o_ref = refs[n_in]
        prod = lax.dot_general(a_ref[...].astype(BF16), b_ref[...].astype(BF16), dims, preferred_element_type=F32)
        if nk == 1:
            o_ref[...] = (prod + r_ref[...] if has_res else prod).astype(out_dtype)
            return
        acc = refs[-1]
        k = pl.program_id(2)

        @pl.when(k == 0)
        def _():
            acc[...] = prod

        @pl.when(k > 0)
        def _():
            acc[...] += prod

        @pl.when(k == nk - 1)
        def _():
            o = acc[...]
            if has_res:
                o = o + r_ref[...]
            o_ref[...] = o.astype(out_dtype)

    in_specs = [a_spec, b_spec]
    args = [a, b]
    if has_res:
        in_specs.append(pl.BlockSpec((tm, tn), lambda i, j, k: (i, j)))
        args.append(res)
    args = _hbm(*args)
    if dep is not None:
        in_specs.append(pl.BlockSpec(memory_space=pl.ANY))
        args.append(dep)
    n_in = len(args)
    return pl.pallas_call(
        body, name=name, out_shape=out_shape, grid=(M // tm, N // tn, nk), in_specs=in_specs, out_specs=o_spec,
        scratch_shapes=[pltpu.VMEM((tm, tn), F32)] if nk > 1 else [],
        compiler_params=_cparams(("parallel", "parallel", "arbitrary")),
    )(*args)


_G0 = math.sqrt(2.0 / math.pi)
_G1 = 0.044715


def _gelu(x):
    return 0.5 * x * (1.0 + jnp.tanh(_G0 * (x + _G1 * x * x * x)))


def _gelu_grad(x):
    x2 = x * x
    t = jnp.tanh(_G0 * (x + _G1 * x * x2))
    return 0.5 * (1.0 + t) + 0.5 * x * (1.0 - t * t) * (_G0 * (1.0 + 3.0 * _G1 * x2))


def _sigmoid(x):
    return 1.0 / (1.0 + jnp.exp(-x))


def _down(v, k):
    r = pltpu.roll(v, k, axis=0)
    row = lax.broadcasted_iota(jnp.int32, (SUBLANES, v.shape[1]), 0)
    return jnp.concatenate([jnp.where(row >= k, r[:SUBLANES], 0.0), r[SUBLANES:]], axis=0)


def _up(v, k):
    n = v.shape[0]
    r = pltpu.roll(v, n - k, axis=0)
    row = lax.broadcasted_iota(jnp.int32, (SUBLANES, v.shape[1]), 0)
    return jnp.concatenate([r[:n - SUBLANES], jnp.where(row < SUBLANES - k, r[n - SUBLANES:], 0.0)], axis=0)


def _taps(v):
    return _down(v, 2), _down(v, 1), v


def _conv3(taps, w):
    return w[0:1, :] * taps[0] + w[1:2, :] * taps[1] + w[2:3, :] * taps[2]


def _conv3_t(dv, w):
    return w[2:3, :] * dv + w[1:2, :] * _up(dv, 1) + w[0:1, :] * _up(dv, 2)


def _conv3_dw(dv, taps):
    return tuple(jnp.sum(dv * tp, axis=0, keepdims=True) for tp in taps)


def _cmul(ar, ai, br, bi):
    return ar * br - ai * bi, ar * bi + ai * br


def _cpow(lr, li, n):
    rr = ri = None
    br, bi = lr, li
    while n:
        if n & 1:
            rr, ri = (br, bi) if rr is None else _cmul(rr, ri, br, bi)
        n >>= 1
        if n:
            br, bi = _cmul(br, bi, br, bi)
    return rr, ri


NORM_ROWS = 256


def _norm_mm(x, g, b, out_dtype, name, ok=None):
    M, D = x.shape
    N = b.shape[1]
    so = (N // ok[1]) if ok is not None else None
    tm = _largest_tile(M, [], 1024)
    tn = _largest_tile(N, [so], MM_TN_CAP)
    if ok is None:
        out_shape = jax.ShapeDtypeStruct((M, N), out_dtype)
        o_spec = pl.BlockSpec((tm, tn), lambda i, j: (i, j))
    else:
        out_shape = jax.ShapeDtypeStruct((ok[1], M, N // ok[1]), out_dtype)
        per = (N // ok[1]) // tn
        o_spec = pl.BlockSpec((None, tm, tn), lambda i, j: (j // per, i, j % per))

    def body(x_ref, g_ref, b_ref, o_ref, ht_ref, h_scr):
        @pl.when(pl.program_id(1) == 0)
        def _():
            for c in range(tm // NORM_ROWS):
                rows = pl.ds(c * NORM_ROWS, NORM_ROWS)
                xv = x_ref[rows, :]
                h = xv * lax.rsqrt(jnp.mean(xv * xv, axis=-1, keepdims=True) + EPS) * g_ref[...]
                h_scr[rows, :] = h.astype(BF16)
                ht_ref[:, rows] = h.T.astype(BF16)

        o_ref[...] = jnp.dot(h_scr[...], b_ref[...], preferred_element_type=F32).astype(out_dtype)

    return pl.pallas_call(
        body, name=name, out_shape=(out_shape, jax.ShapeDtypeStruct((D, M), BF16)), grid=(M // tm, N // tn),
        in_specs=[pl.BlockSpec((tm, D), lambda i, j: (i, 0)), pl.BlockSpec((1, D), lambda i, j: (0, 0)),
                  pl.BlockSpec((D, tn), lambda i, j: (0, j))],
        out_specs=(o_spec, pl.BlockSpec((D, tm), lambda i, j: (0, i))),
        scratch_shapes=[pltpu.VMEM((tm, D), BF16)], compiler_params=_cparams(("parallel", "arbitrary")),
    )(*_hbm(x, g.reshape(1, D), b))


def _mm_norm_bwd(a, b, x, g, dres, name, ak=None, dep=None):
    M, K = _ldims(a, ak)
    D = b.shape[0]
    assert b.shape[1] == K and x.shape == (M, D)
    sa = _segw(a, ak)
    tm = _largest_tile(M, [], 1024)
    whole_segs = bool(sa) and K <= MM_TK_WHOLE
    tk = K if (K <= MM_TK_WHOLE) else _largest_tile(K, [sa], MM_TK_CAP)
    ni, nk = M // tm, K // tk
    if whole_segs:
        a_spec = pl.BlockSpec((a.shape[0], tm, sa), lambda i, k: (0, i, 0))
    else:
        a3 = _opspec(a, ak, tm, tk, lambda i, j, k: i, lambda i, j, k: k)
        a_spec = pl.BlockSpec(a3.block_shape, lambda i, k: a3.index_map(i, 0, k))
    n_in = 5 + (dep is not None)

    def body(*refs):
        a_ref, b_ref, x_ref, g_ref, r_ref = refs[:5]
        dx_ref, dxb_ref, dg_ref, acc, accg = refs[n_in:]
        i, k = pl.program_id(0), pl.program_id(1)
        av = jnp.concatenate([a_ref[s] for s in range(a.shape[0])], axis=1) if whole_segs else a_ref[...]
        prod = lax.dot_general(av.astype(BF16), b_ref[...], (((1,), (1,)), ((), ())), preferred_element_type=F32)

        @pl.when(k == 0)
        def _():
            acc[...] = prod

        @pl.when(k > 0)
        def _():
            acc[...] += prod

        @pl.when((i == 0) & (k == 0))
        def _():
            accg[...] = jnp.zeros_like(accg)

        @pl.when(k == nk - 1)
        def _():
            for c in range(tm // NORM_ROWS):
                rows = pl.ds(c * NORM_ROWS, NORM_ROWS)
                xv = x_ref[rows, :]
                r = lax.rsqrt(jnp.mean(xv * xv, axis=-1, keepdims=True) + EPS)
                xh = xv * r
                dhv = acc[rows, :]
                accg[...] += jnp.sum((dhv * xh).reshape(NORM_ROWS // SUBLANES, SUBLANES, D), axis=0)
                dxh = dhv * g_ref[...]
                dxv = r_ref[rows, :] + r * (dxh - xh * jnp.mean(dxh * xh, axis=-1, keepdims=True))
                dx_ref[rows, :] = dxv
                dxb_ref[rows, :] = dxv.astype(BF16)

        @pl.when((i == ni - 1) & (k == nk - 1))
        def _():
            dg_ref[...] = jnp.sum(accg[...], axis=0, keepdims=True)

    row = pl.BlockSpec((tm, D), lambda i, k: (i, 0))
    vec = pl.BlockSpec((1, D), lambda i, k: (0, 0))
    in_specs = [a_spec, pl.BlockSpec((D, tk), lambda i, k: (0, k)), row, vec, row]
    args = _hbm(a, b, x, g.reshape(1, D), dres)
    if dep is not None:
        in_specs.append(pl.BlockSpec(memory_space=pl.ANY))
        args.append(dep)
    return pl.pallas_call(
        body, name=name,
        out_shape=(jax.ShapeDtypeStruct((M, D), F32), jax.ShapeDtypeStruct((M, D), BF16),
                   jax.ShapeDtypeStruct((1, D), F32)),
        grid=(ni, nk), in_specs=in_specs, out_specs=(row, row, vec),
        scratch_shapes=[pltpu.VMEM((tm, D), F32), pltpu.VMEM((SUBLANES, D), F32)],
        compiler_params=_cparams(("arbitrary", "arbitrary"), VMEM_LIMIT_S5),
    )(*args)


def _loss_head(x, g, tgt):
    L, D = x.shape
    tr = _pick(L, prefs=(512, 256, 128))
    nsteps = L // tr

    def body(x_ref, g_ref, t_ref, loss_ref, dx_ref, dxb_ref, dg_ref, acc_g, acc_l):
        i = pl.program_id(0)

        @pl.when(i == 0)
        def _():
            acc_g[...] = jnp.zeros_like(acc_g)
            acc_l[...] = jnp.zeros_like(acc_l)

        xv = x_ref[...]
        gv = g_ref[...]
        r = lax.rsqrt(jnp.mean(xv * xv, axis=-1, keepdims=True) + EPS)
        xh = xv * r
        e = xh * gv - t_ref[...]
        acc_l[...] += jnp.sum((e * e).reshape(tr // SUBLANES, SUBLANES, D), axis=0)
        dy = e * (1.0 / D)
        acc_g[...] += jnp.sum((dy * xh).reshape(tr // SUBLANES, SUBLANES, D), axis=0)
        dxh = dy * gv
        dxv = r * (dxh - xh * jnp.mean(dxh * xh, axis=-1, keepdims=True))
        dx_ref[...] = dxv
        dxb_ref[...] = dxv.astype(BF16)

        @pl.when(i == nsteps - 1)
        def _():
            dg_ref[...] = jnp.sum(acc_g[...], axis=0, keepdims=True)
            tot = jnp.sum(jnp.sum(acc_l[...], axis=0, keepdims=True), axis=1, keepdims=True) * (0.5 / D)
            loss_ref[...] = jnp.broadcast_to(tot, (SUBLANES, LANES))

    row = pl.BlockSpec((tr, D), lambda i: (i, 0))
    vec = pl.BlockSpec((1, D), lambda i: (0, 0))
    return pl.pallas_call(
        body, name="loss_head",
        out_shape=(jax.ShapeDtypeStruct((SUBLANES, LANES), F32), jax.ShapeDtypeStruct((L, D), F32),
                   jax.ShapeDtypeStruct((L, D), BF16), jax.ShapeDtypeStruct((1, D), F32)),
        grid=(nsteps,), in_specs=[row, vec, row],
        out_specs=(pl.BlockSpec((SUBLANES, LANES), lambda i: (0, 0)), row, row, vec),
        scratch_shapes=[pltpu.VMEM((SUBLANES, D), F32), pltpu.VMEM((SUBLANES, D), F32)],
        compiler_params=_cparams(("arbitrary",)),
    )(*_hbm(x, g.reshape(1, D), tgt))


def _sconv_fwd(proj4, conv_w, name):
    _, L, C = proj4.shape
    cb = LANES

    def body(p_ref, w_ref, o_ref):
        xa, ba, ca = p_ref[0].astype(F32), p_ref[1].astype(F32), p_ref[2].astype(F32)
        o_ref[...] = (ba * _conv3(_taps(ca * xa), w_ref[...])).astype(BF16)

    return pl.pallas_call(
        body, name=name, out_shape=jax.ShapeDtypeStruct((L, 2 * C), BF16), grid=(C // cb,),
        in_specs=[pl.BlockSpec((3, L, cb), lambda j: (0, 0, j)), pl.BlockSpec((3, cb), lambda j: (0, j))],
        out_specs=pl.BlockSpec((L, cb), lambda j: (0, j)), compiler_params=_cparams(("parallel",)),
    )(*_hbm(proj4, conv_w))


def _sconv_bwd(proj4, dmix, conv_w, name):
    _, L, C = proj4.shape
    cb = LANES

    def body(p_ref, d_ref, w_ref, o_ref, dw_ref):
        xa, ba, ca = p_ref[0].astype(F32), p_ref[1].astype(F32), p_ref[2].astype(F32)
        w = w_ref[...]
        dya = d_ref[...]
        tq = _taps(ca * xa)
        cq = _conv3(tq, w)
        dcq = dya * ba
        dq = _conv3_t(dcq, w)
        for tap, dwt in enumerate(_conv3_dw(dcq, tq)):
            dw_ref[tap:tap + 1, :] = dwt
        o_ref[0] = (dq * ca).astype(BF16)
        o_ref[1] = (dya * cq).astype(BF16)
        o_ref[2] = (dq * xa).astype(BF16)

    return pl.pallas_call(
        body, name=name,
        out_shape=(jax.ShapeDtypeStruct((4, L, C), BF16), jax.ShapeDtypeStruct((3, C), F32)), grid=(C // cb,),
        in_specs=[pl.BlockSpec((3, L, cb), lambda j: (0, 0, j)), pl.BlockSpec((L, cb), lambda j: (0, j)),
                  pl.BlockSpec((3, cb), lambda j: (0, j))],
        out_specs=(pl.BlockSpec((3, L, cb), lambda j: (0, 0, j)), pl.BlockSpec((3, cb), lambda j: (0, j))),
        compiler_params=_cparams(("parallel",)),
    )(*_hbm(proj4, dmix, conv_w))


def _s5_prep(log_step, a_re, a_im, b_re, b_im, c_re, c_im):
    G, P = a_re.shape
    H = b_re.shape[-1]
    gs = S5_GROUPS_PER_STEP
    ns = G // gs
    gu = LANES // H
    lam = lax.complex(a_re, a_im)
    step = jnp.exp(log_step)[:, None]
    lam_bar = jnp.exp(lam * step)
    b_bar = ((lam_bar - 1.0) / lam)[..., None] * lax.complex(b_re, b_im)
    lr = jnp.real(lam_bar).reshape(ns, 1, gs * P)
    li = jnp.imag(lam_bar).reshape(ns, 1, gs * P)
    k = np.arange(ns)[:, None, None]
    oh = jnp.asarray((np.arange(gu)[None, :, None] == gs * (k % (gu // gs)) + np.arange(gs)[None, None, :]),
                     F32)
    bre = jnp.einsum('kgl,klph->kghlp', oh, jnp.real(b_bar).reshape(ns, gs, P, H)).reshape(ns, gu * H, gs * P)
    bim = jnp.einsum('kgl,klph->kghlp', oh, jnp.imag(b_bar).reshape(ns, gs, P, H)).reshape(ns, gu * H, gs * P)
    cre = jnp.einsum('kgl,klhp->klpgh', oh, c_re.reshape(ns, gs, H, P)).reshape(ns, gs * P, gu * H)
    cim = jnp.einsum('kgl,klhp->klpgh', oh, c_im.reshape(ns, gs, H, P)).reshape(ns, gs * P, gu * H)
    return lr, li, jnp.concatenate([bre, bim], axis=2), jnp.concatenate([cre, -cim], axis=1)


def _carry_tile(fr, fi, pr, pi, reverse):
    row = lax.broadcasted_iota(jnp.int32, fr.shape, 0)
    cr = jnp.zeros_like(fr)
    ci = jnp.zeros_like(fi)
    sr = jnp.zeros_like(fr[0:1])
    si = jnp.zeros_like(sr)
    order = range(SCAN_CHUNKS - 1, 0, -1) if reverse else range(0, SCAN_CHUNKS - 1)
    for c in order:
        fcr = jnp.sum(jnp.where(row == c, fr, 0.0), axis=0, keepdims=True)
        fci = jnp.sum(jnp.where(row == c, fi, 0.0), axis=0, keepdims=True)
        mr, mi = _cmul(pr, pi, sr, si)
        sr, si = mr + fcr, mi + fci
        nxt = c - 1 if reverse else c + 1
        cr = jnp.where(row == nxt, sr, cr)
        ci = jnp.where(row == nxt, si, ci)
    return cr, ci


def _scan_order_into(dst_ref, src_ref, T):
    for c in range(SCAN_CHUNKS):
        dst_ref[pl.ds(c, T, stride=SCAN_CHUNKS), :] = src_ref[pl.ds(c * T, T), :].astype(F32)


def _s5_fwd(proj4, lr, li, bmat, cmat, d, name):
    _, L, Du = proj4.shape
    ns, _, W2 = bmat.shape
    W = W2 // 2
    T = L // SCAN_CHUNKS
    rb = _pick(L, prefs=(512, 256, 128))
    per = (ns * LANES) // Du

    def body(ut_ref, lr_ref, li_ref, b_ref, c_ref, d_ref, y_ref, sr_ref, si_ref, u_ref):
        k = pl.program_id(0)
        _scan_order_into(u_ref, ut_ref, T)
        for r in range(L // rb):
            rows = pl.ds(r * rb, rb)
            bu = jnp.dot(u_ref[rows, :].astype(BF16), b_ref[...], preferred_element_type=F32)
            sr_ref[rows, :] = bu[:, :W]
            si_ref[rows, :] = bu[:, W:]
        lam_r = jnp.broadcast_to(lr_ref[...], (SUBLANES, W))
        lam_i = jnp.broadcast_to(li_ref[...], (SUBLANES, W))

        def local(t, carry):
            sr, si = carry
            rows = pl.ds(pl.multiple_of(t * SUBLANES, SUBLANES), SUBLANES)
            mr, mi = _cmul(lam_r, lam_i, sr, si)
            sr = mr + sr_ref[rows, :]
            si = mi + si_ref[rows, :]
            sr_ref[rows, :] = sr
            si_ref[rows, :] = si
            return sr, si

        z = jnp.zeros((SUBLANES, W), F32)
        fr, fi = lax.fori_loop(0, T, local, (z, z))
        pr, pi = _cpow(lam_r, lam_i, T)
        cr, ci = _carry_tile(fr, fi, pr[0:1], pi[0:1], reverse=False)

        def fix(t, carry):
            wr, wi = carry
            rows = pl.ds(pl.multiple_of(t * SUBLANES, SUBLANES), SUBLANES)
            ar, ai = _cmul(wr, wi, cr, ci)
            sr_ref[rows, :] += ar
            si_ref[rows, :] += ai
            return _cmul(wr, wi, lam_r, lam_i)

        lax.fori_loop(0, T, fix, (lam_r, lam_i))
        first = (k % per) == 0
        for r in range(L // rb):
            rows = pl.ds(r * rb, rb)
            s = jnp.concatenate([sr_ref[rows, :], si_ref[rows, :]], axis=1).astype(BF16)
            y = jnp.dot(s, c_ref[...], preferred_element_type=F32)

            @pl.when(first)
            def _():
                y_ref[rows, :] = y + d_ref[...] * u_ref[rows, :]

            @pl.when(jnp.logical_not(first))
            def _():
                y_ref[rows, :] += y

    ublk = pl.BlockSpec((L, LANES), lambda k: (0, k // per))
    sblk = pl.BlockSpec((L, W), lambda k: (0, k))
    lam = pl.BlockSpec((None, 1, W), lambda k: (k, 0, 0))
    return pl.pallas_call(
        body, name=name,
        out_shape=(jax.ShapeDtypeStruct((L, Du), F32), jax.ShapeDtypeStruct((L, ns * W), F32),
                   jax.ShapeDtypeStruct((L, ns * W), F32)),
        grid=(ns,),
        in_specs=[pl.BlockSpec((None, L, LANES), lambda k: (3, 0, k // per)), lam, lam,
                  pl.BlockSpec((None, LANES, 2 * W), lambda k: (k, 0, 0)),
                  pl.BlockSpec((None, 2 * W, LANES), lambda k: (k, 0, 0)),
                  pl.BlockSpec((1, LANES), lambda k: (0, k // per))],
        out_specs=(ublk, sblk, sblk), scratch_shapes=[pltpu.VMEM((L, LANES), F32)],
        compiler_params=_cparams(("arbitrary",), VMEM_LIMIT_S5),
    )(*_hbm(proj4, lr, li, bmat.astype(BF16), cmat.astype(BF16), d.reshape(1, Du)))


def _s5_bwd(dy, proj4, dproj, s_re, s_im, lr, li, bmat, cmat, d, name):
    _, L, Du = proj4.shape
    ns, _, W2 = bmat.shape
    W = W2 // 2
    T = L // SCAN_CHUNKS
    rb = _pick(L, prefs=(512, 256, 128))
    per = (ns * LANES) // Du
    NT = (((1,), (1,)), ((), ()))
    TN = (((0,), (0,)), ((), ()))

    def body(dy_ref, ut_ref, dp_in, sr_ref, si_ref, lr_ref, li_ref, b_ref, c_ref, d_ref,
             dut_ref, db_ref, dc_ref, dl_ref, dd_ref, gr_ref, gi_ref, u_ref, du_ref):
        k = pl.program_id(0)
        _scan_order_into(u_ref, ut_ref, T)
        for r in range(L // rb):
            rows = pl.ds(r * rb, rb)
            g = lax.dot_general(dy_ref[rows, :].astype(BF16), c_ref[...], NT, preferred_element_type=F32)
            gr_ref[rows, :] = g[:, :W]
            gi_ref[rows, :] = g[:, W:]
        lam_r = jnp.broadcast_to(lr_ref[...], (SUBLANES, W))
        lam_i = -jnp.broadcast_to(li_ref[...], (SUBLANES, W))

        def local(i, carry):
            gr, gi = carry
            rows = pl.ds(pl.multiple_of((T - 1 - i) * SUBLANES, SUBLANES), SUBLANES)
            mr, mi = _cmul(lam_r, lam_i, gr, gi)
            gr = mr + gr_ref[rows, :]
            gi = mi + gi_ref[rows, :]
            gr_ref[rows, :] = gr
            gi_ref[rows, :] = gi
            return gr, gi

        z = jnp.zeros((SUBLANES, W), F32)
        fr, fi = lax.fori_loop(0, T, local, (z, z))
        pr, pi = _cpow(lam_r, lam_i, T)
        cr, ci = _carry_tile(fr, fi, pr[0:1], pi[0:1], reverse=True)

        def true_g(rows, wr, wi):
            ar, ai = _cmul(wr, wi, cr, ci)
            gr = gr_ref[rows, :] + ar
            gi = gi_ref[rows, :] + ai
            gr_ref[rows, :] = gr
            gi_ref[rows, :] = gi
            return gr, gi

        def fix(i, carry):
            wr, wi, ar_, ai_ = carry
            t = T - 1 - i
            rows = pl.ds(pl.multiple_of(t * SUBLANES, SUBLANES), SUBLANES)
            prev = pl.ds(pl.multiple_of((t - 1) * SUBLANES, SUBLANES), SUBLANES)
            gr, gi = true_g(rows, wr, wi)
            qr, qi = sr_ref[prev, :], si_ref[prev, :]
            ar_ = ar_ + gr * qr + gi * qi
            ai_ = ai_ + gi * qr - gr * qi
            wr, wi = _cmul(wr, wi, lam_r, lam_i)
            return wr, wi, ar_, ai_

        wr, wi, acc_r, acc_i = lax.fori_loop(0, T - 1, fix, (lam_r, lam_i, z, z))
        gr, gi = true_g(pl.ds(0, SUBLANES), wr, wi)
        last = pl.ds((T - 1) * SUBLANES, SUBLANES)
        row = lax.broadcasted_iota(jnp.int32, (SUBLANES, W), 0)
        qr = jnp.where(row >= 1, pltpu.roll(sr_ref[last, :], 1, axis=0), 0.0)
        qi = jnp.where(row >= 1, pltpu.roll(si_ref[last, :], 1, axis=0), 0.0)
        acc_r = acc_r + gr * qr + gi * qi
        acc_i = acc_i + gi * qr - gr * qi
        dl_ref[0:1, :] = jnp.sum(acc_r, axis=0, keepdims=True)
        dl_ref[1:2, :] = jnp.sum(acc_i, axis=0, keepdims=True)

        first = (k % per) == 0
        db = jnp.zeros((LANES, 2 * W), F32)
        dc = jnp.zeros((LANES, 2 * W), F32)
        dd = jnp.zeros((1, LANES), F32)
        for r in range(L // rb):
            rows = pl.ds(r * rb, rb)
            gb = jnp.concatenate([gr_ref[rows, :], gi_ref[rows, :]], axis=1).astype(BF16)
            sb = jnp.concatenate([sr_ref[rows, :], si_ref[rows, :]], axis=1).astype(BF16)
            dyv = dy_ref[rows, :]
            uv = u_ref[rows, :]
            du = lax.dot_general(gb, b_ref[...], NT, preferred_element_type=F32)
            db = db + lax.dot_general(uv.astype(BF16), gb, TN, preferred_element_type=F32)
            dc = dc + lax.dot_general(dyv.astype(BF16), sb, TN, preferred_element_type=F32)
            dd = dd + jnp.sum(dyv * uv, axis=0, keepdims=True)

            @pl.when(first)
            def _():
                du_ref[rows, :] = du + d_ref[...] * dyv

            @pl.when(jnp.logical_not(first))
            def _():
                du_ref[rows, :] += du

        db_ref[...] = db
        dc_ref[...] = dc

        @pl.when(first)
        def _():
            dd_ref[...] = dd

        @pl.when((k % per) == per - 1)
        def _():
            for c in range(SCAN_CHUNKS):
                dut_ref[pl.ds(c * T, T), :] = du_ref[pl.ds(c, T, stride=SCAN_CHUNKS), :].astype(BF16)

    ublk = pl.BlockSpec((L, LANES), lambda k: (0, k // per))
    uslab = pl.BlockSpec((None, L, LANES), lambda k: (3, 0, k // per))
    sblk = pl.BlockSpec((L, W), lambda k: (0, k))
    lam = pl.BlockSpec((None, 1, W), lambda k: (k, 0, 0))
    vec = pl.BlockSpec((1, LANES), lambda k: (0, k // per))
    mat = pl.BlockSpec((None, LANES, 2 * W), lambda k: (k, 0, 0))
    return pl.pallas_call(
        body, name=name,
        out_shape=(jax.ShapeDtypeStruct(dproj.shape, dproj.dtype), jax.ShapeDtypeStruct((ns, LANES, 2 * W), F32),
                   jax.ShapeDtypeStruct((ns, LANES, 2 * W), F32), jax.ShapeDtypeStruct((ns, 2, W), F32),
                   jax.ShapeDtypeStruct((1, Du), F32)),
        grid=(ns,),
        in_specs=[ublk, uslab, pl.BlockSpec(memory_space=pl.ANY), sblk, sblk, lam, lam, mat,
                  pl.BlockSpec((None, 2 * W, LANES), lambda k: (k, 0, 0)), vec],
        out_specs=(uslab, mat, mat, pl.BlockSpec((None, 2, W), lambda k: (k, 0, 0)), vec),
        scratch_shapes=[pltpu.VMEM((L, W), F32), pltpu.VMEM((L, W), F32), pltpu.VMEM((L, LANES), F32),
                        pltpu.VMEM((L, LANES), F32)],
        input_output_aliases={2: 0}, compiler_params=_cparams(("arbitrary",), VMEM_LIMIT_S5),
    )(*_hbm(dy, proj4, dproj, s_re, s_im, lr, li, bmat.astype(BF16), cmat.astype(BF16), d.reshape(1, Du)))


def _glu_fwd(yraw, wmat, bias, mixin, name):
    L, C = yraw.shape
    tr = _pick(L, prefs=(512, 256, 128))
    tb = tr // SCAN_CHUNKS
    nl = C // LANES

    def body(y_ref, w_ref, b_ref, m_in, o_ref, scr):
        yg = _gelu(y_ref[...])
        zz = jnp.dot(yg.astype(BF16), w_ref[...], preferred_element_type=F32) + b_ref[...]
        yb = yg * _sigmoid(zz)
        for k in range(nl):
            scr[k] = yb[:, k * LANES:(k + 1) * LANES]
        for c in range(SCAN_CHUNKS):
            for k in range(nl):
                o_ref[c, :, k * LANES:(k + 1) * LANES] = scr[k, pl.ds(c, tb, stride=SCAN_CHUNKS), :].astype(BF16)

    out = pl.pallas_call(
        body, name=name, out_shape=jax.ShapeDtypeStruct((SCAN_CHUNKS, L // SCAN_CHUNKS, 2 * C), BF16),
        grid=(L // tr,),
        in_specs=[pl.BlockSpec((tr, C), lambda i: (i, 0)), pl.BlockSpec((C, C), lambda i: (0, 0)),
                  pl.BlockSpec((1, C), lambda i: (0, 0)), pl.BlockSpec(memory_space=pl.ANY)],
        out_specs=pl.BlockSpec((SCAN_CHUNKS, tb, C), lambda i: (0, i, 1)),
        scratch_shapes=[pltpu.VMEM((nl, tr, LANES), F32)], input_output_aliases={3: 0},
        compiler_params=_cparams(("parallel",)),
    )(*_hbm(yraw, wmat, bias.reshape(1, C), mixin.reshape(SCAN_CHUNKS, L // SCAN_CHUNKS, 2 * C)))
    return out.reshape(L, 2 * C)


def _glu_bwd(yraw, dmix, wmat, bias, name):
    L, C = yraw.shape
    tr = _pick(L, prefs=(512, 256, 128))
    nsteps = L // tr
    tb = tr // SCAN_CHUNKS
    nl = C // LANES

    def body(y_ref, d_ref, w_ref, b_ref, dy_ref, dw_ref, db_ref, acc_b, scr):
        i = pl.program_id(0)

        @pl.when(i == 0)
        def _():
            dw_ref[...] = jnp.zeros_like(dw_ref)
            acc_b[...] = jnp.zeros_like(acc_b)

        for c in range(SCAN_CHUNKS):
            for k in range(nl):
                scr[k, pl.ds(c, tb, stride=SCAN_CHUNKS), :] = d_ref[c, :, k * LANES:(k + 1) * LANES]
        yr = y_ref[...]
        yg = _gelu(yr)
        ygb = yg.astype(BF16)
        sg = _sigmoid(jnp.dot(ygb, w_ref[...], preferred_element_type=F32) + b_ref[...])
        dyb_ = jnp.concatenate([scr[k] for k in range(nl)], axis=1)
        dz = dyb_ * yg * sg * (1.0 - sg)
        dzb = dz.astype(BF16)
        dyg = dyb_ * sg + lax.dot_general(dzb, w_ref[...], (((1,), (1,)), ((), ())), preferred_element_type=F32)
        dw_ref[...] += lax.dot_general(ygb, dzb, (((0,), (0,)), ((), ())), preferred_element_type=F32)
        acc_b[...] += jnp.sum(dz.reshape(tr // SUBLANES, SUBLANES, C), axis=0)
        dy_ref[...] = dyg * _gelu_grad(yr)

        @pl.when(i == nsteps - 1)
        def _():
            db_ref[...] = jnp.sum(acc_b[...], axis=0, keepdims=True)

    row = pl.BlockSpec((tr, C), lambda i: (i, 0))
    return pl.pallas_call(
        body, name=name,
        out_shape=(jax.ShapeDtypeStruct((L, C), F32), jax.ShapeDtypeStruct((C, C), F32),
                   jax.ShapeDtypeStruct((1, C), F32)),
        grid=(nsteps,),
        in_specs=[row, pl.BlockSpec((SCAN_CHUNKS, tb, C), lambda i: (0, i, 1)), pl.BlockSpec((C, C), lambda i: (0, 0)),
                  pl.BlockSpec((1, C), lambda i: (0, 0))],
        out_specs=(row, pl.BlockSpec((C, C), lambda i: (0, 0)), pl.BlockSpec((1, C), lambda i: (0, 0))),
        scratch_shapes=[pltpu.VMEM((SUBLANES, C), F32), pltpu.VMEM((nl, tr, LANES), F32)],
        compiler_params=_cparams(("arbitrary",)),
    )(*_hbm(yraw, dmix.reshape(SCAN_CHUNKS, L // SCAN_CHUNKS, 2 * C), wmat, bias.reshape(1, C)))


def _pool_counts(L, g):
    t = lax.broadcasted_iota(jnp.int32, (L, LANES), 0).astype(F32) + 1.0
    w = jnp.where(g == 0, 2.0, jnp.where(g == 1, 4.0, jnp.where(g == 2, 8.0, 16.0)))
    return 1.0 / jnp.minimum(t, w)


def _select_window(g, a2, a4, a8, a16):
    return jnp.where(g == 0, a2, jnp.where(g == 1, a4, jnp.where(g == 2, a8, a16)))


def _pooled(z, g):
    a2 = z + _down(z, 1)
    a4 = a2 + _down(a2, 2)
    a8 = a4 + _down(a4, 4)
    a16 = a8 + _down(a8, 8)
    return _select_window(g, a2, a4, a8, a16) * _pool_counts(z.shape[0], g) - z


def _transpose_on_mxu(yb):
    c = yb.shape[1]
    eye = lax.broadcasted_iota(jnp.int32, (c, c), 0) == lax.broadcasted_iota(jnp.int32, (c, c), 1)
    return lax.dot_general(eye.astype(BF16), yb, (((1,), (1,)), ((), ())), preferred_element_type=F32).astype(BF16)


def _pool_fwd(proj3, pool_w, scale, name):
    _, L, C = proj3.shape
    ng = len(POOL_WINDOWS)
    pg = C // ng
    assert pg == LANES

    def body(z_ref, w_ref, s_ref, o_ref, ot_ref):
        g = pl.program_id(0)
        p = _pooled(z_ref[...].astype(F32), g)
        y = jnp.dot(p.astype(BF16), w_ref[...].astype(BF16), preferred_element_type=F32)
        yb = (y * s_ref[...]).astype(BF16)
        o_ref[...] = yb
        ot_ref[...] = _transpose_on_mxu(yb)

    return pl.pallas_call(
        body, name=name, out_shape=(jax.ShapeDtypeStruct((L, 2 * C), BF16), jax.ShapeDtypeStruct((2 * C, L), BF16)),
        grid=(ng,),
        in_specs=[pl.BlockSpec((None, L, pg), lambda g: (0, 0, g)), pl.BlockSpec((None, pg, pg), lambda g: (g, 0, 0)),
                  pl.BlockSpec((1, pg), lambda g: (0, g))],
        out_specs=(pl.BlockSpec((L, pg), lambda g: (0, g)), pl.BlockSpec((pg, L), lambda g: (g, 0))),
        compiler_params=_cparams(("parallel",)),
    )(*_hbm(proj3, pool_w, scale.reshape(1, C)))


def _pool_bwd(proj3, dmix, pool_w, scale, name):
    _, L, C = proj3.shape
    ng = len(POOL_WINDOWS)
    pg = C // ng

    def body(z_ref, d_ref, w_ref, s_ref, dz_ref, dw_ref, ds_ref):
        g = pl.program_id(0)
        p = _pooled(z_ref[...].astype(F32), g)
        pb = p.astype(BF16)
        wb = w_ref[...].astype(BF16)
        pre = jnp.dot(pb, wb, preferred_element_type=F32)
        dyc = d_ref[...]
        ds_ref[...] = jnp.sum(dyc * pre, axis=0, keepdims=True)
        dpre = (dyc * s_ref[...]).astype(BF16)
        dw_ref[...] = lax.dot_general(pb, dpre, (((0,), (0,)), ((), ())), preferred_element_type=F32)
        dp = lax.dot_general(dpre, wb, (((1,), (1,)), ((), ())), preferred_element_type=F32)
        v = dp * _pool_counts(L, g)
        a2 = v + _up(v, 1)
        a4 = a2 + _up(a2, 2)
        a8 = a4 + _up(a4, 4)
        a16 = a8 + _up(a8, 8)
        dz_ref[...] = (_select_window(g, a2, a4, a8, a16) - dp).astype(BF16)

    return pl.pallas_call(
        body, name=name,
        out_shape=(jax.ShapeDtypeStruct((L, C), BF16), jax.ShapeDtypeStruct((ng, pg, pg), F32),
                   jax.ShapeDtypeStruct((1, C), F32)),
        grid=(ng,),
        in_specs=[pl.BlockSpec((None, L, pg), lambda g: (0, 0, g)), pl.BlockSpec((L, pg), lambda g: (0, g)),
                  pl.BlockSpec((None, pg, pg), lambda g: (g, 0, 0)), pl.BlockSpec((1, pg), lambda g: (0, g))],
        out_specs=(pl.BlockSpec((L, pg), lambda g: (0, g)), pl.BlockSpec((None, pg, pg), lambda g: (g, 0, 0)),
                   pl.BlockSpec((1, pg), lambda g: (0, g))),
        compiler_params=_cparams(("parallel",)),
    )(*_hbm(proj3, dmix, pool_w, scale.reshape(1, C)))


def _tril_w(w_ref, h):
    r = lax.broadcasted_iota(jnp.int32, (CHUNK, CHUNK), 0)
    c = lax.broadcasted_iota(jnp.int32, (CHUNK, CHUNK), 1)
    return jnp.where(r >= c, w_ref[h], 0.0)


def _sgu_fwd(proj3, norm_g, w, b, mixin, mixin_t, name):
    _, L, C = proj3.shape
    nh = w.shape[0]
    dh = C // nh
    assert dh == LANES and w.shape[1] == CHUNK
    tr = _pick(L, prefs=(512, 256, 128))
    bfull = jnp.broadcast_to(b[:, :, None], (nh, CHUNK, dh))

    def body(su_ref, sv_ref, g_ref, w_ref, b_ref, m_in, mt_in, o_ref, ot_ref):
        sv = _gelu(sv_ref[...].astype(F32))
        r = lax.rsqrt(jnp.mean(sv * sv, axis=-1, keepdims=True) + EPS)
        v = (sv * r * g_ref[...]).astype(BF16)
        for h in range(nh):
            wm = _tril_w(w_ref, h).astype(BF16)
            cols = slice(h * dh, (h + 1) * dh)
            for n in range(tr // CHUNK):
                rows = slice(n * CHUNK, (n + 1) * CHUNK)
                mixed = jnp.dot(wm, v[rows, cols], preferred_element_type=F32) + b_ref[h]
                o_ref[rows, cols] = (_gelu(su_ref[rows, cols].astype(F32)) * mixed).astype(BF16)
        ot_ref[...] = _transpose_on_mxu(o_ref[...])

    full = lambda shp: pl.BlockSpec(shp, lambda i: (0,) * len(shp))
    anywhere = pl.BlockSpec(memory_space=pl.ANY)
    return pl.pallas_call(
        body, name=name, out_shape=(jax.ShapeDtypeStruct(mixin.shape, BF16), jax.ShapeDtypeStruct(mixin_t.shape, BF16)),
        grid=(L // tr,),
        in_specs=[pl.BlockSpec((None, tr, C), lambda i: (1, i, 0)), pl.BlockSpec((None, tr, C), lambda i: (2, i, 0)),
                  full((1, C)), full((nh, CHUNK, CHUNK)), full((nh, CHUNK, dh)), anywhere, anywhere],
        out_specs=(pl.BlockSpec((tr, C), lambda i: (i, 1)), pl.BlockSpec((C, tr), lambda i: (1, i))),
        input_output_aliases={5: 0, 6: 1}, compiler_params=_cparams(("parallel",)),
    )(*_hbm(proj3, proj3, norm_g.reshape(1, C), w, bfull, mixin, mixin_t))


def _sgu_bwd(proj3, dmix, dz, norm_g, w, b, name):
    _, L, C = proj3.shape
    nh = w.shape[0]
    dh = C // nh
    tr = _pick(L, prefs=(512, 256, 128))
    nsteps = L // tr
    bfull = jnp.broadcast_to(b[:, :, None], (nh, CHUNK, dh))

    def body(su_ref, sv_ref, d_ref, dz_ref, g_ref, w_ref, b_ref, o_ref, dw_ref, db_ref, dg_ref, dv_ref, acc_g):
        i = pl.program_id(0)
        o_ref[0] = dz_ref[...]

        @pl.when(i == 0)
        def _():
            dw_ref[...] = jnp.zeros_like(dw_ref)
            db_ref[...] = jnp.zeros_like(db_ref)
            acc_g[...] = jnp.zeros_like(acc_g)

        svp = sv_ref[...].astype(F32)
        sv = _gelu(svp)
        r = lax.rsqrt(jnp.mean(sv * sv, axis=-1, keepdims=True) + EPS)
        vh = sv * r
        gv = g_ref[...]
        v = (vh * gv).astype(BF16)
        tri_r = lax.broadcasted_iota(jnp.int32, (CHUNK, CHUNK), 0)
        tri_c = lax.broadcasted_iota(jnp.int32, (CHUNK, CHUNK), 1)
        for h in range(nh):
            wm = _tril_w(w_ref, h).astype(BF16)
            cols = slice(h * dh, (h + 1) * dh)
            dwh = jnp.zeros((CHUNK, CHUNK), F32)
            dbh = jnp.zeros((CHUNK, dh), F32)
            for n in range(tr // CHUNK):
                rows = slice(n * CHUNK, (n + 1) * CHUNK)
                vb = v[rows, cols]
                mixed = jnp.dot(wm, vb, preferred_element_type=F32) + b_ref[h]
                sup = su_ref[rows, cols].astype(F32)
                dyd = d_ref[rows, cols]
                dmx = dyd * _gelu(sup)
                o_ref[1, rows, cols] = (dyd * mixed * _gelu_grad(sup)).astype(BF16)
                dmb = dmx.astype(BF16)
                dwh = dwh + lax.dot_general(dmb, vb, (((1,), (1,)), ((), ())), preferred_element_type=F32)
                dbh = dbh + dmx
                dv_ref[rows, cols] = lax.dot_general(wm, dmb, (((0,), (0,)), ((), ())), preferred_element_type=F32)
            dw_ref[h] += jnp.where(tri_r >= tri_c, dwh, 0.0)
            db_ref[h] += dbh
        dv = dv_ref[...]
        acc_g[...] += jnp.sum((dv * vh).reshape(tr // SUBLANES, SUBLANES, C), axis=0)
        dvg = dv * gv
        dsv = r * (dvg - vh * jnp.mean(dvg * vh, axis=-1, keepdims=True))
        o_ref[2] = (dsv * _gelu_grad(svp)).astype(BF16)

        @pl.when(i == nsteps - 1)
        def _():
            dg_ref[...] = jnp.sum(acc_g[...], axis=0, keepdims=True)

    full = lambda shp: pl.BlockSpec(shp, lambda i: (0,) * len(shp))
    return pl.pallas_call(
        body, name=name,
        out_shape=(jax.ShapeDtypeStruct((3, L, C), BF16), jax.ShapeDtypeStruct((nh, CHUNK, CHUNK), F32),
                   jax.ShapeDtypeStruct((nh, CHUNK, dh), F32), jax.ShapeDtypeStruct((1, C), F32)),
        grid=(nsteps,),
        in_specs=[pl.BlockSpec((None, tr, C), lambda i: (1, i, 0)), pl.BlockSpec((None, tr, C), lambda i: (2, i, 0)),
                  pl.BlockSpec((tr, C), lambda i: (i, 1)), pl.BlockSpec((tr, C), lambda i: (i, 0)), full((1, C)),
                  full((nh, CHUNK, CHUNK)), full((nh, CHUNK, dh))],
        out_specs=(pl.BlockSpec((3, tr, C), lambda i: (0, i, 0)), full((nh, CHUNK, CHUNK)), full((nh, CHUNK, dh)),
                   full((1, C))),
        scratch_shapes=[pltpu.VMEM((tr, C), F32), pltpu.VMEM((SUBLANES, C), F32)],
        compiler_params=_cparams(("arbitrary",)),
    )(*_hbm(proj3, proj3, dmix, dz, norm_g.reshape(1, C), w, bfull))


def _ffn_act_fwd(up3, conv_w, conv_b, name):
    _, L, Fh = up3.shape
    cb = LANES
    w2 = conv_w.reshape(3, 2, Fh).transpose(1, 0, 2)
    b2 = conv_b.reshape(2, 1, Fh)

    def body(u_ref, w_ref, b_ref, o_ref, ot_ref, gv_ref):
        g = _conv3(_taps(u_ref[0].astype(F32)), w_ref[0]) + b_ref[0]
        v = _conv3(_taps(u_ref[1].astype(F32)), w_ref[1]) + b_ref[1]
        gv_ref[0] = g.astype(BF16)
        gv_ref[1] = v.astype(BF16)
        ab = (g * _sigmoid(g) * v).astype(BF16)
        o_ref[...] = ab
        ot_ref[...] = _transpose_on_mxu(ab)

    blk3 = pl.BlockSpec((2, L, cb), lambda j: (0, 0, j))
    return pl.pallas_call(
        body, name=name,
        out_shape=(jax.ShapeDtypeStruct((L, Fh), BF16), jax.ShapeDtypeStruct((Fh, L), BF16),
                   jax.ShapeDtypeStruct((2, L, Fh), BF16)),
        grid=(Fh // cb,),
        in_specs=[blk3, pl.BlockSpec((2, 3, cb), lambda j: (0, 0, j)), pl.BlockSpec((2, 1, cb), lambda j: (0, 0, j))],
        out_specs=(pl.BlockSpec((L, cb), lambda j: (0, j)), pl.BlockSpec((cb, L), lambda j: (j, 0)), blk3),
        compiler_params=_cparams(("parallel",)),
    )(*_hbm(up3, w2, b2))


def _ffn_act_bwd(up3, gv3, da, conv_w, h2t, name):
    _, L, Fh = up3.shape
    D = h2t.shape[0]
    cb = LANES
    nb = Fh // cb
    w2 = conv_w.reshape(3, 2, Fh).transpose(1, 0, 2)

    def body(u_ref, gv_ref, d_ref, w_ref, h_ref, o_ref, dw_ref, db_ref, wg_ref, wv_ref, scr):
        j = pl.program_id(0)

        @pl.when(j == 0)
        def _():
            scr[1] = jnp.zeros((2, L, cb), BF16)

        prev = scr.at[(j + 1) % 2]
        wg_ref[...] = jnp.dot(h_ref[...], prev[0], preferred_element_type=F32).astype(BF16)
        wv_ref[...] = jnp.dot(h_ref[...], prev[1], preferred_element_type=F32).astype(BF16)
        tg, tv = _taps(u_ref[0].astype(F32)), _taps(u_ref[1].astype(F32))
        wg, wv = w_ref[0], w_ref[1]
        g = gv_ref[0].astype(F32)
        v = gv_ref[1].astype(F32)
        sg = _sigmoid(g)
        dav = d_ref[...].astype(F32)
        dg = dav * v * (sg * (1.0 + g * (1.0 - sg)))
        dv = dav * (g * sg)
        dug = _conv3_t(dg, wg).astype(BF16)
        duv = _conv3_t(dv, wv).astype(BF16)
        o_ref[0] = dug
        o_ref[1] = duv
        cur = scr.at[j % 2]
        cur[0] = dug
        cur[1] = duv
        for tap, (dwg, dwv) in enumerate(zip(_conv3_dw(dg, tg), _conv3_dw(dv, tv))):
            dw_ref[0, tap:tap + 1, :] = dwg
            dw_ref[1, tap:tap + 1, :] = dwv
        db_ref[0] = jnp.sum(dg, axis=0, keepdims=True)
        db_ref[1] = jnp.sum(dv, axis=0, keepdims=True)

    here = lambda j: jnp.minimum(j, nb - 1)
    before = lambda j: jnp.maximum(j - 1, 0)
    blk3 = pl.BlockSpec((2, L, cb), lambda j: (0, 0, here(j)))
    dup, dw2, db2, dwg, dwv = pl.pallas_call(
        body, name=name,
        out_shape=(jax.ShapeDtypeStruct((2, L, Fh), BF16), jax.ShapeDtypeStruct((2, 3, Fh), F32),
                   jax.ShapeDtypeStruct((2, 1, Fh), F32), jax.ShapeDtypeStruct((D, Fh), BF16),
                   jax.ShapeDtypeStruct((D, Fh), BF16)),
        grid=(nb + 1,),
        in_specs=[blk3, blk3, pl.BlockSpec((L, cb), lambda j: (0, here(j))),
                  pl.BlockSpec((2, 3, cb), lambda j: (0, 0, here(j))), pl.BlockSpec((D, L), lambda j: (0, 0))],
        out_specs=(blk3, pl.BlockSpec((2, 3, cb), lambda j: (0, 0, here(j))),
                   pl.BlockSpec((2, 1, cb), lambda j: (0, 0, here(j))),
                   pl.BlockSpec((D, cb), lambda j: (0, before(j))), pl.BlockSpec((D, cb), lambda j: (0, before(j)))),
        scratch_shapes=[pltpu.VMEM((2, 2, L, cb), BF16)],
        compiler_params=_cparams(("arbitrary",), VMEM_LIMIT_S5),
    )(*_hbm(up3, gv3, da, w2, h2t))
    return dup, dw2.transpose(1, 0, 2).reshape(3, 2 * Fh), db2.reshape(2 * Fh), jnp.concatenate([dwg, dwv], axis=1)


def _local_step(x, tgt, w, layer_weights, on_layer_grads):
    L, D = x.shape
    depth = w['norm_mix_g'].shape[0]
    saved = []
    for i in range(depth):
        j = i // 2
        wb = dict(layer_weights(2 * i, x))
        s = {'x': x, 'wb': wb}
        if i % 2 == 0:
            proj4, s['hT'] = _norm_mm(x, w['norm_mix_g'][i], wb['even_w_in'], BF16, "even_in_fwd", ok=('seg', 4))
            s['proj'] = proj4
            mixin = _sconv_fwd(proj4, w['even_conv_w'][j], "sconv_fwd")
            prm = (w['ssm_log_step'][j], w['ssm_a_re'][j], w['ssm_a_im'][j], w['ssm_b_re'][j], w['ssm_b_im'][j],
                   w['ssm_c_re'][j], w['ssm_c_im'][j])
            (lr, li, bmat, cmat), prep_vjp = jax.vjp(_s5_prep, *prm)
            yraw, s_re, s_im = _s5_fwd(proj4, lr, li, bmat, cmat, w['ssm_d'][j], "s5_fwd")
            mixin = _glu_fwd(yraw, wb['ssm_glu_w'], w['ssm_glu_b'][j], mixin, "glu_fwd")
            s.update(yraw=yraw, s_re=s_re, s_im=s_im, s5=(lr, li, bmat, cmat), prep_vjp=prep_vjp)
            s['mixinT'] = mixin.T
            x = _mm(mixin, wb['even_w_out'], 'nn', F32, "even_out_fwd", res=x)
        else:
            proj3, s['hT'] = _norm_mm(x, w['norm_mix_g'][i], wb['odd_w_in'], BF16, "odd_in_fwd", ok=('seg', 3))
            s['proj'] = proj3
            mixin, mixin_t = _pool_fwd(proj3, w['pool_w'][j], w['pool_scale'][j], "pool_fwd")
            mixin, s['mixinT'] = _sgu_fwd(proj3, w['sgu_norm_g'][j], w['sgu_w'][j], w['sgu_b'][j], mixin, mixin_t,
                                          "sgu_fwd")
            x = _mm(mixin, wb['odd_w_out'], 'nn', F32, "odd_out_fwd", res=x)
        s['x1'] = x
        wb.update(layer_weights(2 * i + 1, x))
        up3, h2t = _norm_mm(x, w['norm_ffn_g'][i], wb['ffn_w_up'], BF16, "ffn_up_fwd", ok=('seg', 2))
        a, at, gv3 = _ffn_act_fwd(up3, w['ffn_conv_w'][i], w['ffn_conv_b'][i], "ffn_act_fwd")
        x = _mm(a, wb['ffn_w_down'], 'nn', F32, "ffn_down_fwd", res=x)
        s.update(h2T=h2t, up3=up3, aT=at, gv3=gv3)
        saved.append(s)

    loss8, dx, dxb, dg_final = _loss_head(x, w['norm_final_g'], tgt)
    gs = {n: [None] * w[n].shape[0] for n in SMALL if n != 'norm_final_g'}
    gs['norm_final_g'] = dg_final.reshape(D)

    dep = None
    for i in reversed(range(depth)):
        j = i // 2
        s = saved[i]
        wb = s['wb']
        gb = {}
        da = _mm(dxb, wb['ffn_w_down'], 'nt', BF16, "ffn_down_dgrad", dep=dep)
        gb['ffn_w_down'] = _mm(s['aT'], dxb, 'nn', BF16, "ffn_down_wgrad")
        dup3, dcw, dcb, gb['ffn_w_up'] = _ffn_act_bwd(s['up3'], s['gv3'], da, w['ffn_conv_w'][i], s['h2T'],
                                                      "ffn_act_bwd")
        gs['ffn_conv_w'][i], gs['ffn_conv_b'][i] = dcw, dcb
        dep = on_layer_grads(2 * i + 1, gb)
        dx, dxb, dg = _mm_norm_bwd(dup3, wb['ffn_w_up'], s['x1'], w['norm_ffn_g'][i], dx, "ffn_up_dgrad",
                              ak=('seg', 2), dep=dep)
        gs['norm_ffn_g'][i] = dg.reshape(D)
        gb = {}
        if i % 2 == 0:
            dmix = _mm(dxb, wb['even_w_out'], 'nt', F32, "even_out_dgrad")
            gb['even_w_out'] = _mm(s['mixinT'], dxb, 'nn', BF16, "even_out_wgrad")
            dproj, dcw = _sconv_bwd(s['proj'], dmix, w['even_conv_w'][j], "sconv_bwd")
            gs['even_conv_w'][j] = dcw
            dyraw, dglu_w, dglu_b = _glu_bwd(s['yraw'], dmix, wb['ssm_glu_w'], w['ssm_glu_b'][j], "glu_bwd")
            gb['ssm_glu_w'] = dglu_w.astype(BF16)
            gs['ssm_glu_b'][j] = dglu_b.reshape(-1)
            lr, li, bmat, cmat = s['s5']
            dproj, dbm, dcm, dlam, dd = _s5_bwd(dyraw, s['proj'], dproj, s['s_re'], s['s_im'], lr, li, bmat, cmat,
                                               w['ssm_d'][j], "s5_bwd")
            gs['ssm_d'][j] = dd.reshape(-1)
            dcm = jnp.swapaxes(dcm, 1, 2)
            dprm = s['prep_vjp']((dlam[:, 0:1, :], dlam[:, 1:2, :], dbm, dcm))
            for n, gval in zip(('ssm_log_step', 'ssm_a_re', 'ssm_a_im', 'ssm_b_re', 'ssm_b_im', 'ssm_c_re',
                                'ssm_c_im'), dprm):
                gs[n][j] = gval
            gb['even_w_in'] = _mm(s['hT'], dproj, 'nn', BF16, "even_in_wgrad", bk=('seg', 4))
            w_in, in_kind, in_name = wb['even_w_in'], ('seg', 4), "even_in_dgrad"
        else:
            dmix = _mm(dxb, wb['odd_w_out'], 'nt', F32, "odd_out_dgrad")
            gb['odd_w_out'] = _mm(s['mixinT'], dxb, 'nn', BF16, "odd_out_wgrad")
            dz, dpw, dps = _pool_bwd(s['proj'], dmix, w['pool_w'][j], w['pool_scale'][j], "pool_bwd")
            gs['pool_w'][j], gs['pool_scale'][j] = dpw, dps.reshape(-1)
            dproj, dsw, dsb, dsg = _sgu_bwd(s['proj'], dmix, dz, w['sgu_norm_g'][j], w['sgu_w'][j], w['sgu_b'][j],
                                            "sgu_bwd")
            gs['sgu_w'][j], gs['sgu_b'][j], gs['sgu_norm_g'][j] = dsw, jnp.sum(dsb, axis=-1), dsg.reshape(-1)
            gb['odd_w_in'] = _mm(s['hT'], dproj, 'nn', BF16, "odd_in_wgrad", bk=('seg', 3))
            w_in, in_kind, in_name = wb['odd_w_in'], ('seg', 3), "odd_in_dgrad"
        dep = on_layer_grads(2 * i, gb)
        dx, dxb, dg = _mm_norm_bwd(dproj, w_in, s['x'], w['norm_mix_g'][i], dx, in_name, ak=in_kind, dep=dep)
        gs['norm_mix_g'][i] = dg.reshape(D)

    gsmall = {n: (v if n == 'norm_final_g' else jnp.stack(v)) for n, v in gs.items()}
    return loss8[0, 0], dx, gsmall


_HBM = pl.BlockSpec(memory_space=pltpu.HBM)
_CHIP_FLIPS = ((0, 0), (1, 0), (0, 1), (1, 1))


def _coords():
    return lax.axis_index("x"), lax.axis_index("y"), lax.axis_index("c")


def _flip(v, f):
    return 1 - v if f else v


def _shard_of(ref, axis, s, width):
    start = pl.multiple_of(s * width, LANES if axis == ref.ndim - 1 else 16) if width % 16 == 0 else s * width
    idx = [slice(None)] * ref.ndim
    idx[axis] = pl.ds(start, width)
    return ref.at[tuple(idx)]


_SEM = pl.BlockSpec(memory_space=pltpu.SEMAPHORE)
_ANY = pl.BlockSpec(memory_space=pl.ANY)
_DATAFLOW = pltpu.SideEffectType.DATAFLOW_SIDE_EFFECTING


def _in_hbm(a):
    return pltpu.with_memory_space_constraint(a, pltpu.HBM)


def _model_layer(name, l):
    if name.startswith('ffn'):
        return l
    return 2 * l + 1 if name.startswith('odd') else 2 * l


def _place_quarter(shard, l, axis, chip, dtype, dep=None):
    _, r, c = shard.shape
    tr = _pick(r, prefs=(512, 256, 128, 64, 32, 16))
    nrb = r // tr

    def body(chip_ref, i_ref, *rest):
        rest[-1][...] = i_ref[...].astype(dtype)

    if axis == 1:
        out_shape, o_map = (r, c * N_CHIPS), (lambda i, s: (i, s[0]))
    else:
        out_shape, o_map = (r * N_CHIPS, c), (lambda i, s: (s[0] * nrb + i, 0))
    in_specs = [pl.BlockSpec((None, tr, c), lambda i, s: (l, i, 0))]
    args = [chip, *_hbm(shard)]
    if dep is not None:
        in_specs.append(pl.BlockSpec(memory_space=pl.ANY))
        args.append(dep)
    return pl.pallas_call(
        body, name="place_quarter", out_shape=jax.ShapeDtypeStruct(out_shape, dtype),
        grid_spec=pltpu.PrefetchScalarGridSpec(
            num_scalar_prefetch=1, grid=(nrb,), in_specs=in_specs, out_specs=pl.BlockSpec((tr, c), o_map)),
        compiler_params=_cparams(("parallel",)),
    )(*args)


def _gather_copies(land_refs, send_sem, recv_sem, axes, landing_chip_of, first=0):
    x, y, c = _coords()
    out = []
    for j, land in enumerate(land_refs):
        width = land.shape[axes[j]] // N_CHIPS
        for f in (1, 2, 3):
            fx, fy = _CHIP_FLIPS[f]
            px, py = _flip(x, fx), _flip(y, fy)
            lx, ly = landing_chip_of(px, py)
            out.append(pltpu.make_async_remote_copy(
                src_ref=_shard_of(land, axes[j], 2 * x + y, width), dst_ref=_shard_of(land, axes[j], 2 * lx + ly, width),
                send_sem=send_sem.at[3 * (first + j) + f - 1], recv_sem=recv_sem.at[3 * (first + j) + f - 1],
                device_id=(px, py, c), device_id_type=MESH))
    return out


def _gather_start(tag, lands, axes, dep=None):
    n = len(lands)

    def body(*refs):
        land_refs, send_sem, recv_sem = refs[:n], refs[-3], refs[-2]
        x, y, _ = _coords()
        for cp in _gather_copies(land_refs, send_sem, recv_sem, axes, lambda px, py: (x, y)):
            cp.start()
        refs[-1][...] = jnp.zeros_like(refs[-1])

    thru = [pltpu.HBM(a.shape, a.dtype) for a in lands]
    outs = pl.pallas_call(
        body, name=f"gather_start_{tag}",
        out_shape=tuple(thru + [pltpu.SemaphoreType.DMA((3 * n,)), pltpu.SemaphoreType.DMA((3 * n,)),
                                jax.ShapeDtypeStruct((SUBLANES, LANES), F32)]),
        in_specs=[_HBM] * n + ([_ANY] if dep is not None else []),
        out_specs=tuple([_HBM] * n + [_SEM, _SEM, pl.BlockSpec(memory_space=pltpu.VMEM)]),
        input_output_aliases={i: i for i in range(n)},
        compiler_params=pltpu.CompilerParams(has_side_effects=_DATAFLOW),
    )(*[_in_hbm(a) for a in lands], *([dep] if dep is not None else []))
    return list(outs[:n]), outs[n], outs[n + 1], outs[n + 2]


def _gather_wait(tag, lands, send_sem, recv_sem, axes, after, first=0):
    n = len(lands)

    def body(*refs):
        for cp in _gather_copies(refs[:n], refs[n], refs[n + 1], axes, lambda px, py: (px, py), first):
            cp.wait_send()
            cp.wait_recv()

    outs = pl.pallas_call(
        body, name=f"gather_wait_{tag}", out_shape=tuple(pltpu.HBM(a.shape, a.dtype) for a in lands),
        in_specs=[_HBM] * n + [_SEM, _SEM, _ANY], out_specs=tuple([_HBM] * n),
        input_output_aliases={i: i for i in range(n)},
        compiler_params=pltpu.CompilerParams(has_side_effects=_DATAFLOW),
    )(*lands, send_sem, recv_sem, after)
    return list(outs)


N_SLOTS = N_DEV - 1


def _scatter_sends(grad_refs, land_refs, send_sem, recv_sem, meta):
    x, y, c = _coords()
    out = []
    for j, (axis, owner, q, width) in enumerate(meta):
        other = c if owner == 0 else 1 - c
        for f, (fx, fy) in enumerate(_CHIP_FLIPS):
            px, py = _flip(x, fx), _flip(y, fy)
            slot = f + 4 * other - 1
            out.append((other if f == 0 else None, pltpu.make_async_remote_copy(
                src_ref=_shard_of(grad_refs[j], axis, 2 * px + py, width), dst_ref=land_refs[j].at[q, slot],
                send_sem=send_sem.at[4 * j + f], recv_sem=recv_sem.at[N_SLOTS * j + slot],
                device_id=(px, py, owner), device_id_type=MESH)))
    return out


def _scatter_start(layer, grads, lands, meta):
    n = len(grads)
    uniq = []
    for a in lands:
        if not any(a is u for u in uniq):
            uniq.append(a)
    which = [next(k for k, u in enumerate(uniq) if u is a) for a in lands]
    nu = len(uniq)

    def body(*refs):
        grad_refs, land_u = refs[:n], refs[n:n + nu]
        send_sem, recv_sem = refs[n + nu], refs[n + nu + 1]
        for other, cp in _scatter_sends(grad_refs, [land_u[k] for k in which], send_sem, recv_sem, meta):
            if other is None:
                cp.start()
            else:
                pl.when(other == 1)(cp.start)
        refs[-1][...] = jnp.zeros_like(refs[-1])

    thru = [pltpu.HBM(a.shape, a.dtype) for a in list(grads) + uniq]
    outs = pl.pallas_call(
        body, name=f"scatter_start_{layer}",
        out_shape=tuple([pltpu.SemaphoreType.DMA((4 * n,)), pltpu.SemaphoreType.DMA((N_SLOTS * n,))] + thru
                        + [jax.ShapeDtypeStruct((SUBLANES, LANES), F32)]),
        in_specs=[_HBM] * (n + nu),
        out_specs=tuple([_SEM, _SEM] + [_HBM] * (n + nu) + [pl.BlockSpec(memory_space=pltpu.VMEM)]),
        input_output_aliases={i: 2 + i for i in range(n + nu)},
        compiler_params=pltpu.CompilerParams(has_side_effects=_DATAFLOW),
    )(*[_in_hbm(a) for a in list(grads) + uniq])
    new_lands = [outs[2 + n + k] for k in which]
    return outs[0], outs[1], list(outs[2:2 + n]), new_lands, outs[-1]


def _scatter_wait(started, lands):
    nl = len(lands)
    flat_grads = [g for s in started for g in s[2]]
    ng, ns = len(flat_grads), len(started)

    def body(*refs):
        land_refs = refs[:nl]
        grad_refs = refs[nl:nl + ng]
        sem_refs = refs[nl + ng:nl + ng + 2 * ns]
        _, _, c = _coords()
        off = 0
        for k, (_, _, grads, idx, meta) in enumerate(started):
            send_sem, recv_sem = sem_refs[2 * k], sem_refs[2 * k + 1]
            lr = [land_refs[i] for i in idx]
            for other, cp in _scatter_sends(grad_refs[off:off + len(grads)], lr, send_sem, recv_sem, meta):
                if other is None:
                    cp.wait_send()
                else:
                    pl.when(other == 1)(cp.wait_send)
            for j, (axis, owner, q, width) in enumerate(meta):
                mine = (c if owner == 0 else 1 - c) == 0

                @pl.when(mine)
                def _():
                    for slot in range(N_SLOTS):
                        land = lr[j].at[q, slot]
                        pltpu.make_async_remote_copy(
                            src_ref=land, dst_ref=land, send_sem=send_sem.at[0], recv_sem=recv_sem.at[N_SLOTS * j + slot],
                            device_id=_coords(), device_id_type=MESH).wait_recv()
            off += len(grads)

    args = list(lands) + flat_grads
    thru = [pltpu.HBM(a.shape, a.dtype) for a in args]
    sems = [s for st in started for s in st[:2]]
    outs = pl.pallas_call(
        body, name="scatter_wait", out_shape=tuple(thru), in_specs=[_HBM] * (nl + ng) + [_SEM] * (2 * ns),
        out_specs=tuple([_HBM] * (nl + ng)), input_output_aliases={i: i for i in range(nl + ng)},
        compiler_params=pltpu.CompilerParams(has_side_effects=_DATAFLOW),
    )(*args, *sems)
    return list(outs[:nl]), list(outs[nl:])


def _sum_and_share(recv, layer_grads, axis, chip, name):
    n, ns, r, c = recv.shape
    tr = _pick(r, prefs=(256, 128, 64, 32, 16))
    nr = r // tr
    nsteps = n * nr
    nlay = len(layer_grads)
    own_map = (lambda h, i, s: (i, s[0])) if axis == 1 else (lambda h, i, s: (s[0] * nr + i, 0))

    def body(chip_ref, i_ref, *rest):
        g_refs = rest[:nlay]
        o_ref, buf, loc_sems, send_sems, recv_sems = rest[nlay:]
        h, i = pl.program_id(0), pl.program_id(1)
        step = h * nr + i
        slot = step % 2
        x, y, core = _coords()
        layer = core * n + h
        own = g_refs[0][...]
        for l in range(1, nlay):
            own = jnp.where(layer == l, g_refs[l][...], own)

        def copies(sl):
            dst = o_ref.at[core * n + h, pl.ds(pl.multiple_of(i * tr, tr), tr), :]
            loc = pltpu.make_async_copy(buf.at[sl], dst, loc_sems.at[sl])
            rem = pltpu.make_async_remote_copy(
                src_ref=buf.at[sl], dst_ref=dst, send_sem=send_sems.at[sl], recv_sem=recv_sems.at[step],
                device_id=(x, y, 1 - core), device_id_type=MESH)
            return loc, rem

        def drain(sl):
            loc, rem = copies(sl)
            loc.wait()
            rem.wait_send()

        pl.when(step >= 2)(lambda: drain(slot))
        acc = own.astype(F32)
        for s in range(ns):
            acc = acc + i_ref[s].astype(F32)
        buf[slot] = acc
        loc, rem = copies(slot)
        loc.start()
        rem.start()

        @pl.when(step == nsteps - 1)
        def _():
            drain(slot)
            if nsteps > 1:
                drain(1 - slot)
            for hh in range(n):
                for ii in range(nr):
                    land = o_ref.at[(1 - core) * n + hh, pl.ds(ii * tr, tr), :]
                    pltpu.make_async_remote_copy(
                        src_ref=buf.at[0], dst_ref=land, send_sem=send_sems.at[0], recv_sem=recv_sems.at[hh * nr + ii],
                        device_id=(x, y, 1 - core), device_id_type=MESH).wait_recv()

    return pl.pallas_call(
        body, name=name, out_shape=jax.ShapeDtypeStruct((2 * n, r, c), F32),
        grid_spec=pltpu.PrefetchScalarGridSpec(
            num_scalar_prefetch=1, grid=(n, nr),
            in_specs=[pl.BlockSpec((None, ns, tr, c), lambda h, i, s: (h, 0, i, 0))]
            + [pl.BlockSpec((tr, c), own_map)] * nlay,
            out_specs=_HBM,
            scratch_shapes=[pltpu.VMEM((2, tr, c), F32), pltpu.SemaphoreType.DMA((2,)),
                            pltpu.SemaphoreType.DMA((2,)), pltpu.SemaphoreType.DMA((nsteps,))]),
        compiler_params=_cparams(("arbitrary", "arbitrary")),
    )(chip, *_hbm(recv, *layer_grads))


def _gather_sums_over_chips(part):
    def body(i_ref, o_ref, send_sems, recv_sems):
        x, y, c = _coords()
        o_ref[2 * x + y] = i_ref[...]

        def copy(f, slot_chip):
            fx, fy = _CHIP_FLIPS[f]
            return pltpu.make_async_remote_copy(
                src_ref=i_ref, dst_ref=o_ref.at[2 * slot_chip[0] + slot_chip[1]], send_sem=send_sems.at[f - 1],
                recv_sem=recv_sems.at[f - 1], device_id=(_flip(x, fx), _flip(y, fy), c), device_id_type=MESH)

        sends = [copy(f, (x, y)) for f in (1, 2, 3)]
        for cp in sends:
            cp.start()
        for f in (1, 2, 3):
            fx, fy = _CHIP_FLIPS[f]
            copy(f, (_flip(x, fx), _flip(y, fy))).wait_recv()
        for cp in sends:
            cp.wait_send()

    vmem = pl.BlockSpec(memory_space=pltpu.VMEM)
    return pl.pallas_call(
        body, name="gather_small_sums", out_shape=jax.ShapeDtypeStruct((N_CHIPS,) + part.shape, part.dtype),
        in_specs=[vmem], out_specs=vmem,
        scratch_shapes=[pltpu.SemaphoreType.DMA((3,)), pltpu.SemaphoreType.DMA((3,))],
    )(part)


def _adamw_update(w_ref, g_ref, m_ref, v_ref, d_ref, mo_ref, vo_ref):
    bc1 = 1.0 - ADAM_B1 ** ADAM_STEP
    bc2 = 1.0 - ADAM_B2 ** ADAM_STEP
    gv = g_ref[...]
    mn = ADAM_B1 * m_ref[...] + (1.0 - ADAM_B1) * gv
    vn = ADAM_B2 * v_ref[...] + (1.0 - ADAM_B2) * (gv * gv)
    d_ref[...] = -ADAM_LR * ((mn / bc1) / (jnp.sqrt(vn / bc2) + ADAM_EPS) + ADAM_WD * w_ref[...])
    mo_ref[...] = mn
    vo_ref[...] = vn


def _adamw(w, g, m, v, name):
    def body(*refs):
        _adamw_update(*refs)

    tr = _pick(w.shape[0], prefs=(256, 128, 64, 32, 16, 8))
    blk = pl.BlockSpec((tr, w.shape[1]), lambda i: (i, 0))
    sds = jax.ShapeDtypeStruct(w.shape, F32)
    return pl.pallas_call(
        body, name=name, out_shape=(sds, sds, sds), grid=(w.shape[0] // tr,), in_specs=[blk] * 4,
        out_specs=(blk,) * 3, compiler_params=_cparams(("parallel",)),
    )(*_hbm(w, g, m, v))


def _adamw_many(tensors, name, by_layer=False):
    n = len(tensors)

    def body(*refs):
        for t in range(n):
            _adamw_update(*refs[4 * t:4 * t + 4], *refs[4 * n + 3 * t:4 * n + 3 * t + 3])

    def spec(a):
        nd = a.ndim
        if by_layer:
            return pl.BlockSpec((1,) + a.shape[1:], lambda i: (i,) + (0,) * (nd - 1))
        return pl.BlockSpec(a.shape, lambda i: (0,) * nd)

    steps = tensors[0][0].shape[0] if by_layer else 1
    outs = pl.pallas_call(
        body, name=name, out_shape=tuple(jax.ShapeDtypeStruct(t[0].shape, F32) for t in tensors for _ in range(3)),
        grid=(steps,), in_specs=[spec(a) for t in tensors for a in t],
        out_specs=tuple(spec(t[0]) for t in tensors for _ in range(3)), compiler_params=_cparams(("parallel",)),
    )(*_hbm(*[a for t in tensors for a in t]))
    return [tuple(outs[3 * t:3 * t + 3]) for t in range(n)]


_PACK_QUANTUM = 256 * LANES


def _pack(arrs):
    flat = jnp.concatenate([a.reshape(-1).astype(F32) for a in arrs])
    flat = jnp.pad(flat, (0, (-flat.shape[0]) % _PACK_QUANTUM))
    return flat.reshape(-1, LANES)


def _unpack(p, shapes):
    flat = p.reshape(-1)
    out, off = [], 0
    for s in shapes:
        n = int(np.prod(s))
        out.append(flat[off:off + n].reshape(s))
        off += n
    return out


def kernel(*args):
    nw = len(WEIGHTS)
    x, tgt = args[0], args[1 + nw]
    w = dict(zip(WEIGHTS, args[1:1 + nw]))
    m = dict(zip(WEIGHTS, args[2 + nw:2 + 2 * nw]))
    v = dict(zip(WEIGHTS, args[2 + 2 * nw:2 + 3 * nw]))
    _, L, D = x.shape
    chip = 2 * lax.axis_index("x") + lax.axis_index("y")

    big = list(BIG)
    small_sh_shapes = [w[n].shape for n in SMALL_SHARDED]
    nbig = len(big)
    chip1 = chip.reshape(1).astype(jnp.int32)
    axes2 = [BIG[n] - 1 for n in big] + [0]
    shards = [w[n] for n in big] + [_pack([w[n] for n in SMALL_SHARDED])[None]]
    pairs = [(t, l) for t in range(nbig + 1) for l in range(shards[t].shape[0])]
    depth = w['norm_mix_g'].shape[0]
    part_of = lambda t, l: 0 if t == nbig else 2 * _model_layer(big[t], l) + big[t].startswith('ffn')
    flying, token = {}, None
    for tag, gset in enumerate(([0], list(range(1, 2 * depth)))):
        ids = [k for g in gset for k, (t, l) in enumerate(pairs) if part_of(t, l) == g]
        ts = [pairs[k][0] for k in ids]
        placed = [_place_quarter(shards[t], pairs[k][1], axes2[t], chip1, F32 if t == nbig else BF16, token)
                  for k, t in zip(ids, ts)]
        lands, send, recv, token = _gather_start(tag, placed, [axes2[t] for t in ts], token)
        first = 0
        for g in gset:
            n = sum(1 for t, l in pairs if part_of(t, l) == g)
            flying[g] = (ts[first:first + n], lands[first:first + n], send, recv, first)
            first += n

    def wait_group(g, after):
        ts, lands, send, recv, first = flying[g]
        landed = _gather_wait(g, lands, send, recv, [axes2[t] for t in ts], token if after is None else after, first)
        return dict(zip(ts, landed))

    first = wait_group(0, None)
    packed = first.pop(nbig).reshape(N_CHIPS, -1, LANES)
    per_chip = [_unpack(packed[s], small_sh_shapes) for s in range(N_CHIPS)]
    wl = dict(w)
    for k, n in enumerate(SMALL_SHARDED):
        wl[n] = jnp.concatenate([per_chip[s][k] for s in range(N_CHIPS)], axis=-1)

    def layer_weights(i, after):
        got = first if i == 0 else wait_group(i, after)
        return {big[t]: a for t, a in got.items()}

    small_shapes = [(w[n].shape[:-1] + (w[n].shape[-1] * N_CHIPS,)) if n in SMALL_SHARDED else w[n].shape
                    for n in SMALL] + [(1,)]
    n_small = sum(int(np.prod(s)) for s in small_shapes)
    pack_rows = -(-n_small // _PACK_QUANTUM) * _PACK_QUANTUM // LANES
    nlayers = [w[n].shape[0] for n in big] + [2]
    halves = [n // 2 for n in nlayers]
    quarters = [tuple(w[n].shape[1:]) for n in big] + [(pack_rows // 2 // N_CHIPS, LANES)]
    wire = [BF16] * nbig + [F32]
    land_now = [lax.empty((halves[t], N_SLOTS) + quarters[t], wire[t]) for t in range(nbig + 1)]
    gparts = [[None] * n for n in nlayers]
    started = []

    def start_scatter(tag, ts, ls, arrays):
        meta = [(axes2[t], l // halves[t], l % halves[t], quarters[t][axes2[t]]) for t, l in zip(ts, ls)]
        send, recv, thru, new_lands, token = _scatter_start(tag, arrays, [land_now[t] for t in ts], meta)
        for t, ln in zip(ts, new_lands):
            land_now[t] = ln
        started.append((send, recv, thru, ts, meta, ls))
        return token

    def on_layer_grads(g, gb):
        ts = [big.index(n) for n in gb]
        return start_scatter(g, ts, [g // 2 if big[t].startswith('ffn') else g // 4 for t in ts],
                             [gb[big[t]] for t in ts])

    loss, dx, gsmall = _local_step(x.reshape(L, D), tgt.reshape(L, D), wl, layer_weights, on_layer_grads)
    gpack = _pack([gsmall[n] for n in SMALL] + [loss.reshape(1)])
    start_scatter(2 * depth, [nbig, nbig], [0, 1], [gpack[:pack_rows // 2], gpack[pack_rows // 2:]])
    landed, sent = _scatter_wait([s[:5] for s in started], land_now)
    for (t, l), g in zip([(t, l) for s in started for t, l in zip(s[3], s[5])], sent):
        gparts[t][l] = g
    gshard = {n: _sum_and_share(landed[t], gparts[t], axes2[t], chip1, "sum_share_" + n) for t, n in enumerate(big)}
    small_sum = _sum_and_share(landed[nbig], gparts[nbig], 0, chip1, "sum_share_small")
    gpack = _gather_sums_over_chips(small_sum).transpose(1, 0, 2, 3).reshape(pack_rows, LANES)
    gs = dict(zip(SMALL + ['loss'], _unpack(gpack, small_shapes)))
    loss = gs.pop('loss').reshape(())
    for n in SMALL_SHARDED:
        width = w[n].shape[-1]
        gs[n] = lax.dynamic_slice_in_dim(gs[n], chip * width, width, axis=gs[n].ndim - 1)

    grads, delta, new_m, new_v = {}, {}, {}, {}
    for n in big:
        shp = w[n].shape
        flat = lambda a: a.reshape(shp[0] * shp[1], shp[2])
        g = gshard[n]
        grads[n] = g
        d_, m_, v_ = _adamw(flat(w[n]), flat(g), flat(m[n]), flat(v[n]), "adamw_" + n)
        delta[n], new_m[n], new_v[n] = d_.reshape(shp), m_.reshape(shp), v_.reshape(shp)
    sparse = [n for n in SMALL if w[n].ndim == 4 and w[n].shape[-1] < LANES // 2]
    for names, by_layer in ((sparse, True), ([n for n in SMALL if n not in sparse], False)):
        as2d = lambda a: a.reshape(1, -1) if a.ndim == 1 else a
        res = _adamw_many([(as2d(w[n]), as2d(gs[n]), as2d(m[n]), as2d(v[n])) for n in names],
                          "adamw_small_by_layer" if by_layer else "adamw_small", by_layer)
        for n, (d_, m_, v_) in zip(names, res):
            shp = w[n].shape
            grads[n], delta[n], new_m[n], new_v[n] = gs[n], d_.reshape(shp), m_.reshape(shp), v_.reshape(shp)

    return (loss, dx.reshape(1, L, D), *[grads[n] for n in WEIGHTS], *[delta[n] for n in WEIGHTS],
            *[new_m[n] for n in WEIGHTS], *[new_v[n] for n in WEIGHTS])
```

```python
import math

import numpy as np
import jax
import jax.numpy as jnp
from jax import lax
from jax.experimental import pallas as pl
from jax.experimental.pallas import tpu as pltpu

F32 = jnp.float32
BF16 = jnp.bfloat16
MESH = pl.DeviceIdType.MESH

EPS = 1e-6
CHUNK = 128
POOL_WINDOWS = (2, 4, 8, 16)
LANES = 128
SUBLANES = 8
SCAN_CHUNKS = SUBLANES
S5_GROUPS_PER_STEP = 4
MM_TM_CAP, MM_TN_CAP, MM_TK_CAP = 1408, 1408, 2048
MM_TK_WHOLE = 2048
VMEM_LIMIT = 48 * 1024 * 1024
VMEM_LIMIT_S5 = 56 * 1024 * 1024

ADAM_LR, ADAM_B1, ADAM_B2, ADAM_EPS, ADAM_WD, ADAM_STEP = 0.001, 0.9, 0.999, 1e-08, 0.01, 10

WEIGHTS = ['norm_mix_g', 'even_w_in', 'even_conv_w', 'ssm_log_step', 'ssm_a_re', 'ssm_a_im', 'ssm_b_re',
           'ssm_b_im', 'ssm_c_re', 'ssm_c_im', 'ssm_d', 'ssm_glu_w', 'ssm_glu_b', 'even_w_out', 'odd_w_in',
           'pool_w', 'pool_scale', 'sgu_norm_g', 'sgu_w', 'sgu_b', 'odd_w_out', 'norm_ffn_g', 'ffn_w_up',
           'ffn_conv_w', 'ffn_conv_b', 'ffn_w_down', 'norm_final_g']
BIG = {'even_w_in': 2, 'ssm_glu_w': 1, 'even_w_out': 1, 'odd_w_in': 2, 'odd_w_out': 1, 'ffn_w_up': 2,
       'ffn_w_down': 1}
SMALL_SHARDED = ('even_conv_w', 'pool_scale', 'sgu_norm_g', 'ffn_conv_w')
SMALL = [n for n in WEIGHTS if n not in BIG]
N_CHIPS = 4
N_DEV = 8


def _cparams(sem=None, vmem=VMEM_LIMIT):
    kw = dict(vmem_limit_bytes=vmem)
    if sem is not None:
        kw['dimension_semantics'] = sem
    return pltpu.CompilerParams(**kw)


def _pick(n, segs=(), prefs=(1024, 512, 256, 128)):
    for t in prefs:
        if n % t == 0 and all(s % t == 0 for s in segs if s):
            return t
    return n


def _largest_tile(n, segs, cap):
    best = None
    for t in range(LANES, min(n, cap) + 1, LANES):
        if n % t == 0 and all(s % t == 0 for s in segs if s):
            best = t
    return best if best is not None else n


def _ldims(arr, kind):
    if kind is None:
        return arr.shape
    if kind[0] == 'lead':
        return arr.shape[1:]
    return (arr.shape[1], arr.shape[0] * arr.shape[2])


def _segw(arr, kind):
    return arr.shape[2] if (kind is not None and kind[0] == 'seg') else None


def _opspec(arr, kind, br, bc, rfn, cfn):
    if kind is None:
        return pl.BlockSpec((br, bc), lambda i, j, k: (rfn(i, j, k), cfn(i, j, k)))
    if kind[0] == 'lead':
        lead = kind[1]
        return pl.BlockSpec((None, br, bc), lambda i, j, k: (lead, rfn(i, j, k), cfn(i, j, k)))
    per = arr.shape[2] // bc
    return pl.BlockSpec((None, br, bc), lambda i, j, k: (cfn(i, j, k) // per, rfn(i, j, k), cfn(i, j, k) % per))


def _mm(a, b, mode, out_dtype, name, ak=None, bk=None, ok=None, res=None, dep=None):
    ar, ac = _ldims(a, ak)
    br_, bc_ = _ldims(b, bk)
    if mode == 'nn':
        M, K, N = ar, ac, bc_
        assert br_ == K
    else:
        M, K, N = ar, ac, br_
        assert bc_ == K
    sa, sb = _segw(a, ak), _segw(b, bk)
    so = (N // ok[1]) if ok is not None else None
    tm = _largest_tile(M, [], MM_TM_CAP)
    tn = _largest_tile(N, [sb if mode == 'nn' else None, so], MM_TN_CAP)
    ksegs = [sa, sb if mode == 'nt' else None]
    tk = K if (K <= MM_TK_WHOLE and not any(ksegs)) else _largest_tile(K, ksegs, MM_TK_CAP)
    nk = K // tk
    I = lambda i, j, k: i
    J = lambda i, j, k: j
    Kk = lambda i, j, k: k
    a_spec = _opspec(a, ak, tm, tk, I, Kk)
    if mode == 'nn':
        b_spec = _opspec(b, bk, tk, tn, Kk, J)
        dims = (((1,), (0,)), ((), ()))
    else:
        b_spec = _opspec(b, bk, tn, tk, J, Kk)
        dims = (((1,), (1,)), ((), ()))
    if ok is None:
        out_shape = jax.ShapeDtypeStruct((M, N), out_dtype)
        o_spec = pl.BlockSpec((tm, tn), lambda i, j, k: (i, j))
    else:
        out_shape = jax.ShapeDtypeStruct((ok[1], M, N // ok[1]), out_dtype)
        per = (N // ok[1]) // tn
        o_spec = pl.BlockSpec((None, tm, tn), lambda i, j, k: (j // per, i, j % per))
    has_res = res is not None

    def body(*refs):
        a_ref, b_ref = refs[0], refs[1]
        r_ref = refs[2] if has_res else None
        o_ref = refs[n_in]
        prod = lax.dot_general(a_ref[...].astype(BF16), b_ref[...].astype(BF16), dims, preferred_element_type=F32)
        if nk == 1:
            o_ref[...] = (prod + r_ref[...] if has_res else prod).astype(out_dtype)
            return
        acc = refs[-1]
        k = pl.program_id(2)

        @pl.when(k == 0)
        def _():
            acc[...] = prod

        @pl.when(k > 0)
        def _():
            acc[...] += prod

        @pl.when(k == nk - 1)
        def _():
            o = acc[...]
            if has_res:
                o = o + r_ref[...]
            o_ref[...] = o.astype(out_dtype)

    in_specs = [a_spec, b_spec]
    args = [a, b]
    if has_res:
        in_specs.append(pl.BlockSpec((tm, tn), lambda i, j, k: (i, j)))
        args.append(res)
    if dep is not None:
        in_specs.append(pl.BlockSpec(memory_space=pl.ANY))
        args.append(dep)
    n_in = len(args)
    return pl.pallas_call(
        body, name=name, out_shape=out_shape, grid=(M // tm, N // tn, nk), in_specs=in_specs, out_specs=o_spec,
        scratch_shapes=[pltpu.VMEM((tm, tn), F32)] if nk > 1 else [],
        compiler_params=_cparams(("parallel", "parallel", "arbitrary")),
    )(*args)


_G0 = math.sqrt(2.0 / math.pi)
_G1 = 0.044715


def _gelu(x):
    return 0.5 * x * (1.0 + jnp.tanh(_G0 * (x + _G1 * x * x * x)))


def _gelu_grad(x):
    x2 = x * x
    t = jnp.tanh(_G0 * (x + _G1 * x * x2))
    return 0.5 * (1.0 + t) + 0.5 * x * (1.0 - t * t) * (_G0 * (1.0 + 3.0 * _G1 * x2))


def _sigmoid(x):
    return 1.0 / (1.0 + jnp.exp(-x))


def _down(v, k):
    r = pltpu.roll(v, k, axis=0)
    row = lax.broadcasted_iota(jnp.int32, (SUBLANES, v.shape[1]), 0)
    return jnp.concatenate([jnp.where(row >= k, r[:SUBLANES], 0.0), r[SUBLANES:]], axis=0)


def _up(v, k):
    n = v.shape[0]
    r = pltpu.roll(v, n - k, axis=0)
    row = lax.broadcasted_iota(jnp.int32, (SUBLANES, v.shape[1]), 0)
    return jnp.concatenate([r[:n - SUBLANES], jnp.where(row < SUBLANES - k, r[n - SUBLANES:], 0.0)], axis=0)


def _taps(v):
    return _down(v, 2), _down(v, 1), v


def _conv3(taps, w):
    return w[0:1, :] * taps[0] + w[1:2, :] * taps[1] + w[2:3, :] * taps[2]


def _conv3_t(dv, w):
    return w[2:3, :] * dv + w[1:2, :] * _up(dv, 1) + w[0:1, :] * _up(dv, 2)


def _conv3_dw(dv, taps):
    return tuple(jnp.sum(dv * tp, axis=0, keepdims=True) for tp in taps)


def _cmul(ar, ai, br, bi):
    return ar * br - ai * bi, ar * bi + ai * br


def _cpow(lr, li, n):
    rr = ri = None
    br, bi = lr, li
    while n:
        if n & 1:
            rr, ri = (br, bi) if rr is None else _cmul(rr, ri, br, bi)
        n >>= 1
        if n:
            br, bi = _cmul(br, bi, br, bi)
    return rr, ri


NORM_ROWS = 256


def _norm_mm(x, g, b, out_dtype, name, ok=None):
    M, D = x.shape
    N = b.shape[1]
    so = (N // ok[1]) if ok is not None else None
    tm = _largest_tile(M, [], 1024)
    tn = _largest_tile(N, [so], MM_TN_CAP)
    if ok is None:
        out_shape = jax.ShapeDtypeStruct((M, N), out_dtype)
        o_spec = pl.BlockSpec((tm, tn), lambda i, j: (i, j))
    else:
        out_shape = jax.ShapeDtypeStruct((ok[1], M, N // ok[1]), out_dtype)
        per = (N // ok[1]) // tn
        o_spec = pl.BlockSpec((None, tm, tn), lambda i, j: (j // per, i, j % per))

    def body(x_ref, g_ref, b_ref, o_ref, ht_ref, h_scr):
        @pl.when(pl.program_id(1) == 0)
        def _():
            for c in range(tm // NORM_ROWS):
                rows = pl.ds(c * NORM_ROWS, NORM_ROWS)
                xv = x_ref[rows, :]
                h = xv * lax.rsqrt(jnp.mean(xv * xv, axis=-1, keepdims=True) + EPS) * g_ref[...]
                h_scr[rows, :] = h.astype(BF16)
                ht_ref[:, rows] = h.T.astype(BF16)

        o_ref[...] = jnp.dot(h_scr[...], b_ref[...], preferred_element_type=F32).astype(out_dtype)

    return pl.pallas_call(
        body, name=name, out_shape=(out_shape, jax.ShapeDtypeStruct((D, M), BF16)), grid=(M // tm, N // tn),
        in_specs=[pl.BlockSpec((tm, D), lambda i, j: (i, 0)), pl.BlockSpec((1, D), lambda i, j: (0, 0)),
                  pl.BlockSpec((D, tn), lambda i, j: (0, j))],
        out_specs=(o_spec, pl.BlockSpec((D, tm), lambda i, j: (0, i))),
        scratch_shapes=[pltpu.VMEM((tm, D), BF16)], compiler_params=_cparams(("parallel", "arbitrary")),
    )(x, g.reshape(1, D), b)


def _mm_norm_bwd(a, b, x, g, dres, name, ak=None, dep=None):
    M, K = _ldims(a, ak)
    D = b.shape[0]
    assert b.shape[1] == K and x.shape == (M, D)
    sa = _segw(a, ak)
    tm = _largest_tile(M, [], 1024)
    whole_segs = bool(sa) and K <= MM_TK_WHOLE
    tk = K if (K <= MM_TK_WHOLE) else _largest_tile(K, [sa], MM_TK_CAP)
    ni, nk = M // tm, K // tk
    if whole_segs:
        a_spec = pl.BlockSpec((a.shape[0], tm, sa), lambda i, k: (0, i, 0))
    else:
        a3 = _opspec(a, ak, tm, tk, lambda i, j, k: i, lambda i, j, k: k)
        a_spec = pl.BlockSpec(a3.block_shape, lambda i, k: a3.index_map(i, 0, k))
    n_in = 5 + (dep is not None)

    def body(*refs):
        a_ref, b_ref, x_ref, g_ref, r_ref = refs[:5]
        dx_ref, dxb_ref, dg_ref, acc, accg = refs[n_in:]
        i, k = pl.program_id(0), pl.program_id(1)
        av = jnp.concatenate([a_ref[s] for s in range(a.shape[0])], axis=1) if whole_segs else a_ref[...]
        prod = lax.dot_general(av.astype(BF16), b_ref[...], (((1,), (1,)), ((), ())), preferred_element_type=F32)

        @pl.when(k == 0)
        def _():
            acc[...] = prod

        @pl.when(k > 0)
        def _():
            acc[...] += prod

        @pl.when((i == 0) & (k == 0))
        def _():
            accg[...] = jnp.zeros_like(accg)

        @pl.when(k == nk - 1)
        def _():
            for c in range(tm // NORM_ROWS):
                rows = pl.ds(c * NORM_ROWS, NORM_ROWS)
                xv = x_ref[rows, :]
                r = lax.rsqrt(jnp.mean(xv * xv, axis=-1, keepdims=True) + EPS)
                xh = xv * r
                dhv = acc[rows, :]
                accg[...] += jnp.sum((dhv * xh).reshape(NORM_ROWS // SUBLANES, SUBLANES, D), axis=0)
                dxh = dhv * g_ref[...]
                dxv = r_ref[rows, :] + r * (dxh - xh * jnp.mean(dxh * xh, axis=-1, keepdims=True))
                dx_ref[rows, :] = dxv
                dxb_ref[rows, :] = dxv.astype(BF16)

        @pl.when((i == ni - 1) & (k == nk - 1))
        def _():
            dg_ref[...] = jnp.sum(accg[...], axis=0, keepdims=True)

    row = pl.BlockSpec((tm, D), lambda i, k: (i, 0))
    vec = pl.BlockSpec((1, D), lambda i, k: (0, 0))
    in_specs = [a_spec, pl.BlockSpec((D, tk), lambda i, k: (0, k)), row, vec, row]
    args = [a, b, x, g.reshape(1, D), dres]
    if dep is not None:
        in_specs.append(pl.BlockSpec(memory_space=pl.ANY))
        args.append(dep)
    return pl.pallas_call(
        body, name=name,
        out_shape=(jax.ShapeDtypeStruct((M, D), F32), jax.ShapeDtypeStruct((M, D), BF16),
                   jax.ShapeDtypeStruct((1, D), F32)),
        grid=(ni, nk), in_specs=in_specs, out_specs=(row, row, vec),
        scratch_shapes=[pltpu.VMEM((tm, D), F32), pltpu.VMEM((SUBLANES, D), F32)],
        compiler_params=_cparams(("arbitrary", "arbitrary"), VMEM_LIMIT_S5),
    )(*args)


def _loss_head(x, g, tgt):
    L, D = x.shape
    tr = _pick(L, prefs=(512, 256, 128))
    nsteps = L // tr

    def body(x_ref, g_ref, t_ref, loss_ref, dx_ref, dxb_ref, dg_ref, acc_g, acc_l):
        i = pl.program_id(0)

        @pl.when(i == 0)
        def _():
            acc_g[...] = jnp.zeros_like(acc_g)
            acc_l[...] = jnp.zeros_like(acc_l)

        xv = x_ref[...]
        gv = g_ref[...]
        r = lax.rsqrt(jnp.mean(xv * xv, axis=-1, keepdims=True) + EPS)
        xh = xv * r
        e = xh * gv - t_ref[...]
        acc_l[...] += jnp.sum((e * e).reshape(tr // SUBLANES, SUBLANES, D), axis=0)
        dy = e * (1.0 / D)
        acc_g[...] += jnp.sum((dy * xh).reshape(tr // SUBLANES, SUBLANES, D), axis=0)
        dxh = dy * gv
        dxv = r * (dxh - xh * jnp.mean(dxh * xh, axis=-1, keepdims=True))
        dx_ref[...] = dxv
        dxb_ref[...] = dxv.astype(BF16)

        @pl.when(i == nsteps - 1)
        def _():
            dg_ref[...] = jnp.sum(acc_g[...], axis=0, keepdims=True)
            tot = jnp.sum(jnp.sum(acc_l[...], axis=0, keepdims=True), axis=1, keepdims=True) * (0.5 / D)
            loss_ref[...] = jnp.broadcast_to(tot, (SUBLANES, LANES))

    row = pl.BlockSpec((tr, D), lambda i: (i, 0))
    vec = pl.BlockSpec((1, D), lambda i: (0, 0))
    return pl.pallas_call(
        body, name="loss_head",
        out_shape=(jax.ShapeDtypeStruct((SUBLANES, LANES), F32), jax.ShapeDtypeStruct((L, D), F32),
                   jax.ShapeDtypeStruct((L, D), BF16), jax.ShapeDtypeStruct((1, D), F32)),
        grid=(nsteps,), in_specs=[row, vec, row],
        out_specs=(pl.BlockSpec((SUBLANES, LANES), lambda i: (0, 0)), row, row, vec),
        scratch_shapes=[pltpu.VMEM((SUBLANES, D), F32), pltpu.VMEM((SUBLANES, D), F32)],
        compiler_params=_cparams(("arbitrary",)),
    )(x, g.reshape(1, D), tgt)


def _sconv_fwd(proj4, conv_w, name):
    _, L, C = proj4.shape
    cb = LANES

    def body(p_ref, w_ref, o_ref):
        xa, ba, ca = p_ref[0].astype(F32), p_ref[1].astype(F32), p_ref[2].astype(F32)
        o_ref[...] = (ba * _conv3(_taps(ca * xa), w_ref[...])).astype(BF16)

    return pl.pallas_call(
        body, name=name, out_shape=jax.ShapeDtypeStruct((L, 2 * C), BF16), grid=(C // cb,),
        in_specs=[pl.BlockSpec((3, L, cb), lambda j: (0, 0, j)), pl.BlockSpec((3, cb), lambda j: (0, j))],
        out_specs=pl.BlockSpec((L, cb), lambda j: (0, j)), compiler_params=_cparams(("parallel",)),
    )(proj4, conv_w)


def _sconv_bwd(proj4, dmix, conv_w, name):
    _, L, C = proj4.shape
    cb = LANES

    def body(p_ref, d_ref, w_ref, o_ref, dw_ref):
        xa, ba, ca = p_ref[0].astype(F32), p_ref[1].astype(F32), p_ref[2].astype(F32)
        w = w_ref[...]
        dya = d_ref[...]
        tq = _taps(ca * xa)
        cq = _conv3(tq, w)
        dcq = dya * ba
        dq = _conv3_t(dcq, w)
        for tap, dwt in enumerate(_conv3_dw(dcq, tq)):
            dw_ref[tap:tap + 1, :] = dwt
        o_ref[0] = (dq * ca).astype(BF16)
        o_ref[1] = (dya * cq).astype(BF16)
        o_ref[2] = (dq * xa).astype(BF16)

    return pl.pallas_call(
        body, name=name,
        out_shape=(jax.ShapeDtypeStruct((4, L, C), BF16), jax.ShapeDtypeStruct((3, C), F32)), grid=(C // cb,),
        in_specs=[pl.BlockSpec((3, L, cb), lambda j: (0, 0, j)), pl.BlockSpec((L, cb), lambda j: (0, j)),
                  pl.BlockSpec((3, cb), lambda j: (0, j))],
        out_specs=(pl.BlockSpec((3, L, cb), lambda j: (0, 0, j)), pl.BlockSpec((3, cb), lambda j: (0, j))),
        compiler_params=_cparams(("parallel",)),
    )(proj4, dmix, conv_w)


def _s5_prep(log_step, a_re, a_im, b_re, b_im, c_re, c_im):
    G, P = a_re.shape
    H = b_re.shape[-1]
    gs = S5_GROUPS_PER_STEP
    ns = G // gs
    gu = LANES // H
    lam = lax.complex(a_re, a_im)
    step = jnp.exp(log_step)[:, None]
    lam_bar = jnp.exp(lam * step)
    b_bar = ((lam_bar - 1.0) / lam)[..., None] * lax.complex(b_re, b_im)
    lr = jnp.real(lam_bar).reshape(ns, 1, gs * P)
    li = jnp.imag(lam_bar).reshape(ns, 1, gs * P)
    k = np.arange(ns)[:, None, None]
    oh = jnp.asarray((np.arange(gu)[None, :, None] == gs * (k % (gu // gs)) + np.arange(gs)[None, None, :]),
                     F32)
    bre = jnp.einsum('kgl,klph->kghlp', oh, jnp.real(b_bar).reshape(ns, gs, P, H)).reshape(ns, gu * H, gs * P)
    bim = jnp.einsum('kgl,klph->kghlp', oh, jnp.imag(b_bar).reshape(ns, gs, P, H)).reshape(ns, gu * H, gs * P)
    cre = jnp.einsum('kgl,klhp->klpgh', oh, c_re.reshape(ns, gs, H, P)).reshape(ns, gs * P, gu * H)
    cim = jnp.einsum('kgl,klhp->klpgh', oh, c_im.reshape(ns, gs, H, P)).reshape(ns, gs * P, gu * H)
    return lr, li, jnp.concatenate([bre, bim], axis=2), jnp.concatenate([cre, -cim], axis=1)


def _carry_tile(fr, fi, pr, pi, reverse):
    row = lax.broadcasted_iota(jnp.int32, fr.shape, 0)
    cr = jnp.zeros_like(fr)
    ci = jnp.zeros_like(fi)
    sr = jnp.zeros_like(fr[0:1])
    si = jnp.zeros_like(sr)
    order = range(SCAN_CHUNKS - 1, 0, -1) if reverse else range(0, SCAN_CHUNKS - 1)
    for c in order:
        fcr = jnp.sum(jnp.where(row == c, fr, 0.0), axis=0, keepdims=True)
        fci = jnp.sum(jnp.where(row == c, fi, 0.0), axis=0, keepdims=True)
        mr, mi = _cmul(pr, pi, sr, si)
        sr, si = mr + fcr, mi + fci
        nxt = c - 1 if reverse else c + 1
        cr = jnp.where(row == nxt, sr, cr)
        ci = jnp.where(row == nxt, si, ci)
    return cr, ci


def _scan_order_into(dst_ref, src_ref, T):
    for c in range(SCAN_CHUNKS):
        dst_ref[pl.ds(c, T, stride=SCAN_CHUNKS), :] = src_ref[pl.ds(c * T, T), :].astype(F32)


def _s5_fwd(proj4, lr, li, bmat, cmat, d, name):
    _, L, Du = proj4.shape
    ns, _, W2 = bmat.shape
    W = W2 // 2
    T = L // SCAN_CHUNKS
    rb = _pick(L, prefs=(512, 256, 128))
    per = (ns * LANES) // Du

    def body(ut_ref, lr_ref, li_ref, b_ref, c_ref, d_ref, y_ref, sr_ref, si_ref, u_ref):
        k = pl.program_id(0)
        _scan_order_into(u_ref, ut_ref, T)
        for r in range(L // rb):
            rows = pl.ds(r * rb, rb)
            bu = jnp.dot(u_ref[rows, :].astype(BF16), b_ref[...], preferred_element_type=F32)
            sr_ref[rows, :] = bu[:, :W]
            si_ref[rows, :] = bu[:, W:]
        lam_r = jnp.broadcast_to(lr_ref[...], (SUBLANES, W))
        lam_i = jnp.broadcast_to(li_ref[...], (SUBLANES, W))

        def local(t, carry):
            sr, si = carry
            rows = pl.ds(pl.multiple_of(t * SUBLANES, SUBLANES), SUBLANES)
            mr, mi = _cmul(lam_r, lam_i, sr, si)
            sr = mr + sr_ref[rows, :]
            si = mi + si_ref[rows, :]
            sr_ref[rows, :] = sr
            si_ref[rows, :] = si
            return sr, si

        z = jnp.zeros((SUBLANES, W), F32)
        fr, fi = lax.fori_loop(0, T, local, (z, z))
        pr, pi = _cpow(lam_r, lam_i, T)
        cr, ci = _carry_tile(fr, fi, pr[0:1], pi[0:1], reverse=False)

        def fix(t, carry):
            wr, wi = carry
            rows = pl.ds(pl.multiple_of(t * SUBLANES, SUBLANES), SUBLANES)
            ar, ai = _cmul(wr, wi, cr, ci)
            sr_ref[rows, :] += ar
            si_ref[rows, :] += ai
            return _cmul(wr, wi, lam_r, lam_i)

        lax.fori_loop(0, T, fix, (lam_r, lam_i))
        first = (k % per) == 0
        for r in range(L // rb):
            rows = pl.ds(r * rb, rb)
            s = jnp.concatenate([sr_ref[rows, :], si_ref[rows, :]], axis=1).astype(BF16)
            y = jnp.dot(s, c_ref[...], preferred_element_type=F32)

            @pl.when(first)
            def _():
                y_ref[rows, :] = y + d_ref[...] * u_ref[rows, :]

            @pl.when(jnp.logical_not(first))
            def _():
                y_ref[rows, :] += y

    ublk = pl.BlockSpec((L, LANES), lambda k: (0, k // per))
    sblk = pl.BlockSpec((L, W), lambda k: (0, k))
    lam = pl.BlockSpec((None, 1, W), lambda k: (k, 0, 0))
    return pl.pallas_call(
        body, name=name,
        out_shape=(jax.ShapeDtypeStruct((L, Du), F32), jax.ShapeDtypeStruct((L, ns * W), F32),
                   jax.ShapeDtypeStruct((L, ns * W), F32)),
        grid=(ns,),
        in_specs=[pl.BlockSpec((None, L, LANES), lambda k: (3, 0, k // per)), lam, lam,
                  pl.BlockSpec((None, LANES, 2 * W), lambda k: (k, 0, 0)),
                  pl.BlockSpec((None, 2 * W, LANES), lambda k: (k, 0, 0)),
                  pl.BlockSpec((1, LANES), lambda k: (0, k // per))],
        out_specs=(ublk, sblk, sblk), scratch_shapes=[pltpu.VMEM((L, LANES), F32)],
        compiler_params=_cparams(("arbitrary",), VMEM_LIMIT_S5),
    )(proj4, lr, li, bmat.astype(BF16), cmat.astype(BF16), d.reshape(1, Du))


def _s5_bwd(dy, proj4, dproj, s_re, s_im, lr, li, bmat, cmat, d, name):
    _, L, Du = proj4.shape
    ns, _, W2 = bmat.shape
    W = W2 // 2
    T = L // SCAN_CHUNKS
    rb = _pick(L, prefs=(512, 256, 128))
    per = (ns * LANES) // Du
    NT = (((1,), (1,)), ((), ()))
    TN = (((0,), (0,)), ((), ()))

    def body(dy_ref, ut_ref, dp_in, sr_ref, si_ref, lr_ref, li_ref, b_ref, c_ref, d_ref,
             dut_ref, db_ref, dc_ref, dl_ref, dd_ref, gr_ref, gi_ref, u_ref, du_ref):
        k = pl.program_id(0)
        _scan_order_into(u_ref, ut_ref, T)
        for r in range(L // rb):
            rows = pl.ds(r * rb, rb)
            g = lax.dot_general(dy_ref[rows, :].astype(BF16), c_ref[...], NT, preferred_element_type=F32)
            gr_ref[rows, :] = g[:, :W]
            gi_ref[rows, :] = g[:, W:]
        lam_r = jnp.broadcast_to(lr_ref[...], (SUBLANES, W))
        lam_i = -jnp.broadcast_to(li_ref[...], (SUBLANES, W))

        def local(i, carry):
            gr, gi = carry
            rows = pl.ds(pl.multiple_of((T - 1 - i) * SUBLANES, SUBLANES), SUBLANES)
            mr, mi = _cmul(lam_r, lam_i, gr, gi)
            gr = mr + gr_ref[rows, :]
            gi = mi + gi_ref[rows, :]
            gr_ref[rows, :] = gr
            gi_ref[rows, :] = gi
            return gr, gi

        z = jnp.zeros((SUBLANES, W), F32)
        fr, fi = lax.fori_loop(0, T, local, (z, z))
        pr, pi = _cpow(lam_r, lam_i, T)
        cr, ci = _carry_tile(fr, fi, pr[0:1], pi[0:1], reverse=True)

        def true_g(rows, wr, wi):
            ar, ai = _cmul(wr, wi, cr, ci)
            gr = gr_ref[rows, :] + ar
            gi = gi_ref[rows, :] + ai
            gr_ref[rows, :] = gr
            gi_ref[rows, :] = gi
            return gr, gi

        def fix(i, carry):
            wr, wi, ar_, ai_ = carry
            t = T - 1 - i
            rows = pl.ds(pl.multiple_of(t * SUBLANES, SUBLANES), SUBLANES)
            prev = pl.ds(pl.multiple_of((t - 1) * SUBLANES, SUBLANES), SUBLANES)
            gr, gi = true_g(rows, wr, wi)
            qr, qi = sr_ref[prev, :], si_ref[prev, :]
            ar_ = ar_ + gr * qr + gi * qi
            ai_ = ai_ + gi * qr - gr * qi
            wr, wi = _cmul(wr, wi, lam_r, lam_i)
            return wr, wi, ar_, ai_

        wr, wi, acc_r, acc_i = lax.fori_loop(0, T - 1, fix, (lam_r, lam_i, z, z))
        gr, gi = true_g(pl.ds(0, SUBLANES), wr, wi)
        last = pl.ds((T - 1) * SUBLANES, SUBLANES)
        row = lax.broadcasted_iota(jnp.int32, (SUBLANES, W), 0)
        qr = jnp.where(row >= 1, pltpu.roll(sr_ref[last, :], 1, axis=0), 0.0)
        qi = jnp.where(row >= 1, pltpu.roll(si_ref[last, :], 1, axis=0), 0.0)
        acc_r = acc_r + gr * qr + gi * qi
        acc_i = acc_i + gi * qr - gr * qi
        dl_ref[0:1, :] = jnp.sum(acc_r, axis=0, keepdims=True)
        dl_ref[1:2, :] = jnp.sum(acc_i, axis=0, keepdims=True)

        first = (k % per) == 0
        db = jnp.zeros((LANES, 2 * W), F32)
        dc = jnp.zeros((LANES, 2 * W), F32)
        dd = jnp.zeros((1, LANES), F32)
        for r in range(L // rb):
            rows = pl.ds(r * rb, rb)
            gb = jnp.concatenate([gr_ref[rows, :], gi_ref[rows, :]], axis=1).astype(BF16)
            sb = jnp.concatenate([sr_ref[rows, :], si_ref[rows, :]], axis=1).astype(BF16)
            dyv = dy_ref[rows, :]
            uv = u_ref[rows, :]
            du = lax.dot_general(gb, b_ref[...], NT, preferred_element_type=F32)
            db = db + lax.dot_general(uv.astype(BF16), gb, TN, preferred_element_type=F32)
            dc = dc + lax.dot_general(dyv.astype(BF16), sb, TN, preferred_element_type=F32)
            dd = dd + jnp.sum(dyv * uv, axis=0, keepdims=True)

            @pl.when(first)
            def _():
                du_ref[rows, :] = du + d_ref[...] * dyv

            @pl.when(jnp.logical_not(first))
            def _():
                du_ref[rows, :] += du

        db_ref[...] = db
        dc_ref[...] = dc

        @pl.when(first)
        def _():
            dd_ref[...] = dd

        @pl.when((k % per) == per - 1)
        def _():
            for c in range(SCAN_CHUNKS):
                dut_ref[pl.ds(c * T, T), :] = du_ref[pl.ds(c, T, stride=SCAN_CHUNKS), :].astype(BF16)

    ublk = pl.BlockSpec((L, LANES), lambda k: (0, k // per))
    uslab = pl.BlockSpec((None, L, LANES), lambda k: (3, 0, k // per))
    sblk = pl.BlockSpec((L, W), lambda k: (0, k))
    lam = pl.BlockSpec((None, 1, W), lambda k: (k, 0, 0))
    vec = pl.BlockSpec((1, LANES), lambda k: (0, k // per))
    mat = pl.BlockSpec((None, LANES, 2 * W), lambda k: (k, 0, 0))
    return pl.pallas_call(
        body, name=name,
        out_shape=(jax.ShapeDtypeStruct(dproj.shape, dproj.dtype), jax.ShapeDtypeStruct((ns, LANES, 2 * W), F32),
                   jax.ShapeDtypeStruct((ns, LANES, 2 * W), F32), jax.ShapeDtypeStruct((ns, 2, W), F32),
                   jax.ShapeDtypeStruct((1, Du), F32)),
        grid=(ns,),
        in_specs=[ublk, uslab, pl.BlockSpec(memory_space=pl.ANY), sblk, sblk, lam, lam, mat,
                  pl.BlockSpec((None, 2 * W, LANES), lambda k: (k, 0, 0)), vec],
        out_specs=(uslab, mat, mat, pl.BlockSpec((None, 2, W), lambda k: (k, 0, 0)), vec),
        scratch_shapes=[pltpu.VMEM((L, W), F32), pltpu.VMEM((L, W), F32), pltpu.VMEM((L, LANES), F32),
                        pltpu.VMEM((L, LANES), F32)],
        input_output_aliases={2: 0}, compiler_params=_cparams(("arbitrary",), VMEM_LIMIT_S5),
    )(dy, proj4, dproj, s_re, s_im, lr, li, bmat.astype(BF16), cmat.astype(BF16), d.reshape(1, Du))


def _glu_fwd(yraw, wmat, bias, mixin, name):
    L, C = yraw.shape
    tr = _pick(L, prefs=(512, 256, 128))
    tb = tr // SCAN_CHUNKS
    nl = C // LANES

    def body(y_ref, w_ref, b_ref, m_in, o_ref, scr):
        yg = _gelu(y_ref[...])
        zz = jnp.dot(yg.astype(BF16), w_ref[...], preferred_element_type=F32) + b_ref[...]
        yb = yg * _sigmoid(zz)
        for k in range(nl):
            scr[k] = yb[:, k * LANES:(k + 1) * LANES]
        for c in range(SCAN_CHUNKS):
            for k in range(nl):
                o_ref[c, :, k * LANES:(k + 1) * LANES] = scr[k, pl.ds(c, tb, stride=SCAN_CHUNKS), :].astype(BF16)

    out = pl.pallas_call(
        body, name=name, out_shape=jax.ShapeDtypeStruct((SCAN_CHUNKS, L // SCAN_CHUNKS, 2 * C), BF16),
        grid=(L // tr,),
        in_specs=[pl.BlockSpec((tr, C), lambda i: (i, 0)), pl.BlockSpec((C, C), lambda i: (0, 0)),
                  pl.BlockSpec((1, C), lambda i: (0, 0)), pl.BlockSpec(memory_space=pl.ANY)],
        out_specs=pl.BlockSpec((SCAN_CHUNKS, tb, C), lambda i: (0, i, 1)),
        scratch_shapes=[pltpu.VMEM((nl, tr, LANES), F32)], input_output_aliases={3: 0},
        compiler_params=_cparams(("parallel",)),
    )(yraw, wmat, bias.reshape(1, C), mixin.reshape(SCAN_CHUNKS, L // SCAN_CHUNKS, 2 * C))
    return out.reshape(L, 2 * C)


def _glu_bwd(yraw, dmix, wmat, bias, name):
    L, C = yraw.shape
    tr = _pick(L, prefs=(512, 256, 128))
    nsteps = L // tr
    tb = tr // SCAN_CHUNKS
    nl = C // LANES

    def body(y_ref, d_ref, w_ref, b_ref, dy_ref, dw_ref, db_ref, acc_b, scr):
        i = pl.program_id(0)

        @pl.when(i == 0)
        def _():
            dw_ref[...] = jnp.zeros_like(dw_ref)
            acc_b[...] = jnp.zeros_like(acc_b)

        for c in range(SCAN_CHUNKS):
            for k in range(nl):
                scr[k, pl.ds(c, tb, stride=SCAN_CHUNKS), :] = d_ref[c, :, k * LANES:(k + 1) * LANES]
        yr = y_ref[...]
        yg = _gelu(yr)
        ygb = yg.astype(BF16)
        sg = _sigmoid(jnp.dot(ygb, w_ref[...], preferred_element_type=F32) + b_ref[...])
        dyb_ = jnp.concatenate([scr[k] for k in range(nl)], axis=1)
        dz = dyb_ * yg * sg * (1.0 - sg)
        dzb = dz.astype(BF16)
        dyg = dyb_ * sg + lax.dot_general(dzb, w_ref[...], (((1,), (1,)), ((), ())), preferred_element_type=F32)
        dw_ref[...] += lax.dot_general(ygb, dzb, (((0,), (0,)), ((), ())), preferred_element_type=F32)
        acc_b[...] += jnp.sum(dz.reshape(tr // SUBLANES, SUBLANES, C), axis=0)
        dy_ref[...] = dyg * _gelu_grad(yr)

        @pl.when(i == nsteps - 1)
        def _():
            db_ref[...] = jnp.sum(acc_b[...], axis=0, keepdims=True)

    row = pl.BlockSpec((tr, C), lambda i: (i, 0))
    return pl.pallas_call(
        body, name=name,
        out_shape=(jax.ShapeDtypeStruct((L, C), F32), jax.ShapeDtypeStruct((C, C), F32),
                   jax.ShapeDtypeStruct((1, C), F32)),
        grid=(nsteps,),
        in_specs=[row, pl.BlockSpec((SCAN_CHUNKS, tb, C), lambda i: (0, i, 1)), pl.BlockSpec((C, C), lambda i: (0, 0)),
                  pl.BlockSpec((1, C), lambda i: (0, 0))],
        out_specs=(row, pl.BlockSpec((C, C), lambda i: (0, 0)), pl.BlockSpec((1, C), lambda i: (0, 0))),
        scratch_shapes=[pltpu.VMEM((SUBLANES, C), F32), pltpu.VMEM((nl, tr, LANES), F32)],
        compiler_params=_cparams(("arbitrary",)),
    )(yraw, dmix.reshape(SCAN_CHUNKS, L // SCAN_CHUNKS, 2 * C), wmat, bias.reshape(1, C))


def _pool_counts(L, g):
    t = lax.broadcasted_iota(jnp.int32, (L, LANES), 0).astype(F32) + 1.0
    w = jnp.where(g == 0, 2.0, jnp.where(g == 1, 4.0, jnp.where(g == 2, 8.0, 16.0)))
    return 1.0 / jnp.minimum(t, w)


def _select_window(g, a2, a4, a8, a16):
    return jnp.where(g == 0, a2, jnp.where(g == 1, a4, jnp.where(g == 2, a8, a16)))


def _pooled(z, g):
    a2 = z + _down(z, 1)
    a4 = a2 + _down(a2, 2)
    a8 = a4 + _down(a4, 4)
    a16 = a8 + _down(a8, 8)
    return _select_window(g, a2, a4, a8, a16) * _pool_counts(z.shape[0], g) - z


def _transpose_on_mxu(yb):
    c = yb.shape[1]
    eye = lax.broadcasted_iota(jnp.int32, (c, c), 0) == lax.broadcasted_iota(jnp.int32, (c, c), 1)
    return lax.dot_general(eye.astype(BF16), yb, (((1,), (1,)), ((), ())), preferred_element_type=F32).astype(BF16)


def _pool_fwd(proj3, pool_w, scale, name):
    _, L, C = proj3.shape
    ng = len(POOL_WINDOWS)
    pg = C // ng
    assert pg == LANES

    def body(z_ref, w_ref, s_ref, o_ref, ot_ref):
        g = pl.program_id(0)
        p = _pooled(z_ref[...].astype(F32), g)
        y = jnp.dot(p.astype(BF16), w_ref[...].astype(BF16), preferred_element_type=F32)
        yb = (y * s_ref[...]).astype(BF16)
        o_ref[...] = yb
        ot_ref[...] = _transpose_on_mxu(yb)

    return pl.pallas_call(
        body, name=name, out_shape=(jax.ShapeDtypeStruct((L, 2 * C), BF16), jax.ShapeDtypeStruct((2 * C, L), BF16)),
        grid=(ng,),
        in_specs=[pl.BlockSpec((None, L, pg), lambda g: (0, 0, g)), pl.BlockSpec((None, pg, pg), lambda g: (g, 0, 0)),
                  pl.BlockSpec((1, pg), lambda g: (0, g))],
        out_specs=(pl.BlockSpec((L, pg), lambda g: (0, g)), pl.BlockSpec((pg, L), lambda g: (g, 0))),
        compiler_params=_cparams(("parallel",)),
    )(proj3, pool_w, scale.reshape(1, C))


def _pool_bwd(proj3, dmix, pool_w, scale, name):
    _, L, C = proj3.shape
    ng = len(POOL_WINDOWS)
    pg = C // ng

    def body(z_ref, d_ref, w_ref, s_ref, dz_ref, dw_ref, ds_ref):
        g = pl.program_id(0)
        p = _pooled(z_ref[...].astype(F32), g)
        pb = p.astype(BF16)
        wb = w_ref[...].astype(BF16)
        pre = jnp.dot(pb, wb, preferred_element_type=F32)
        dyc = d_ref[...]
        ds_ref[...] = jnp.sum(dyc * pre, axis=0, keepdims=True)
        dpre = (dyc * s_ref[...]).astype(BF16)
        dw_ref[...] = lax.dot_general(pb, dpre, (((0,), (0,)), ((), ())), preferred_element_type=F32)
        dp = lax.dot_general(dpre, wb, (((1,), (1,)), ((), ())), preferred_element_type=F32)
        v = dp * _pool_counts(L, g)
        a2 = v + _up(v, 1)
        a4 = a2 + _up(a2, 2)
        a8 = a4 + _up(a4, 4)
        a16 = a8 + _up(a8, 8)
        dz_ref[...] = (_select_window(g, a2, a4, a8, a16) - dp).astype(BF16)

    return pl.pallas_call(
        body, name=name,
        out_shape=(jax.ShapeDtypeStruct((L, C), BF16), jax.ShapeDtypeStruct((ng, pg, pg), F32),
                   jax.ShapeDtypeStruct((1, C), F32)),
        grid=(ng,),
        in_specs=[pl.BlockSpec((None, L, pg), lambda g: (0, 0, g)), pl.BlockSpec((L, pg), lambda g: (0, g)),
                  pl.BlockSpec((None, pg, pg), lambda g: (g, 0, 0)), pl.BlockSpec((1, pg), lambda g: (0, g))],
        out_specs=(pl.BlockSpec((L, pg), lambda g: (0, g)), pl.BlockSpec((None, pg, pg), lambda g: (g, 0, 0)),
                   pl.BlockSpec((1, pg), lambda g: (0, g))),
        compiler_params=_cparams(("parallel",)),
    )(proj3, dmix, pool_w, scale.reshape(1, C))


def _tril_w(w_ref, h):
    r = lax.broadcasted_iota(jnp.int32, (CHUNK, CHUNK), 0)
    c = lax.broadcasted_iota(jnp.int32, (CHUNK, CHUNK), 1)
    return jnp.where(r >= c, w_ref[h], 0.0)


def _sgu_fwd(proj3, norm_g, w, b, mixin, mixin_t, name):
    _, L, C = proj3.shape
    nh = w.shape[0]
    dh = C // nh
    assert dh == LANES and w.shape[1] == CHUNK
    tr = _pick(L, prefs=(512, 256, 128))
    bfull = jnp.broadcast_to(b[:, :, None], (nh, CHUNK, dh))

    def body(su_ref, sv_ref, g_ref, w_ref, b_ref, m_in, mt_in, o_ref, ot_ref):
        sv = _gelu(sv_ref[...].astype(F32))
        r = lax.rsqrt(jnp.mean(sv * sv, axis=-1, keepdims=True) + EPS)
        v = (sv * r * g_ref[...]).astype(BF16)
        for h in range(nh):
            wm = _tril_w(w_ref, h).astype(BF16)
            cols = slice(h * dh, (h + 1) * dh)
            for n in range(tr // CHUNK):
                rows = slice(n * CHUNK, (n + 1) * CHUNK)
                mixed = jnp.dot(wm, v[rows, cols], preferred_element_type=F32) + b_ref[h]
                o_ref[rows, cols] = (_gelu(su_ref[rows, cols].astype(F32)) * mixed).astype(BF16)
        ot_ref[...] = _transpose_on_mxu(o_ref[...])

    full = lambda shp: pl.BlockSpec(shp, lambda i: (0,) * len(shp))
    anywhere = pl.BlockSpec(memory_space=pl.ANY)
    return pl.pallas_call(
        body, name=name, out_shape=(jax.ShapeDtypeStruct(mixin.shape, BF16), jax.ShapeDtypeStruct(mixin_t.shape, BF16)),
        grid=(L // tr,),
        in_specs=[pl.BlockSpec((None, tr, C), lambda i: (1, i, 0)), pl.BlockSpec((None, tr, C), lambda i: (2, i, 0)),
                  full((1, C)), full((nh, CHUNK, CHUNK)), full((nh, CHUNK, dh)), anywhere, anywhere],
        out_specs=(pl.BlockSpec((tr, C), lambda i: (i, 1)), pl.BlockSpec((C, tr), lambda i: (1, i))),
        input_output_aliases={5: 0, 6: 1}, compiler_params=_cparams(("parallel",)),
    )(proj3, proj3, norm_g.reshape(1, C), w, bfull, mixin, mixin_t)


def _sgu_bwd(proj3, dmix, dz, norm_g, w, b, name):
    _, L, C = proj3.shape
    nh = w.shape[0]
    dh = C // nh
    tr = _pick(L, prefs=(512, 256, 128))
    nsteps = L // tr
    bfull = jnp.broadcast_to(b[:, :, None], (nh, CHUNK, dh))

    def body(su_ref, sv_ref, d_ref, dz_ref, g_ref, w_ref, b_ref, o_ref, dw_ref, db_ref, dg_ref, dv_ref, acc_g):
        i = pl.program_id(0)
        o_ref[0] = dz_ref[...]

        @pl.when(i == 0)
        def _():
            dw_ref[...] = jnp.zeros_like(dw_ref)
            db_ref[...] = jnp.zeros_like(db_ref)
            acc_g[...] = jnp.zeros_like(acc_g)

        svp = sv_ref[...].astype(F32)
        sv = _gelu(svp)
        r = lax.rsqrt(jnp.mean(sv * sv, axis=-1, keepdims=True) + EPS)
        vh = sv * r
        gv = g_ref[...]
        v = (vh * gv).astype(BF16)
        tri_r = lax.broadcasted_iota(jnp.int32, (CHUNK, CHUNK), 0)
        tri_c = lax.broadcasted_iota(jnp.int32, (CHUNK, CHUNK), 1)
        for h in range(nh):
            wm = _tril_w(w_ref, h).astype(BF16)
            cols = slice(h * dh, (h + 1) * dh)
            dwh = jnp.zeros((CHUNK, CHUNK), F32)
            dbh = jnp.zeros((CHUNK, dh), F32)
            for n in range(tr // CHUNK):
                rows = slice(n * CHUNK, (n + 1) * CHUNK)
                vb = v[rows, cols]
                mixed = jnp.dot(wm, vb, preferred_element_type=F32) + b_ref[h]
                sup = su_ref[rows, cols].astype(F32)
                dyd = d_ref[rows, cols]
                dmx = dyd * _gelu(sup)
                o_ref[1, rows, cols] = (dyd * mixed * _gelu_grad(sup)).astype(BF16)
                dmb = dmx.astype(BF16)
                dwh = dwh + lax.dot_general(dmb, vb, (((1,), (1,)), ((), ())), preferred_element_type=F32)
                dbh = dbh + dmx
                dv_ref[rows, cols] = lax.dot_general(wm, dmb, (((0,), (0,)), ((), ())), preferred_element_type=F32)
            dw_ref[h] += jnp.where(tri_r >= tri_c, dwh, 0.0)
            db_ref[h] += dbh
        dv = dv_ref[...]
        acc_g[...] += jnp.sum((dv * vh).reshape(tr // SUBLANES, SUBLANES, C), axis=0)
        dvg = dv * gv
        dsv = r * (dvg - vh * jnp.mean(dvg * vh, axis=-1, keepdims=True))
        o_ref[2] = (dsv * _gelu_grad(svp)).astype(BF16)

        @pl.when(i == nsteps - 1)
        def _():
            dg_ref[...] = jnp.sum(acc_g[...], axis=0, keepdims=True)

    full = lambda shp: pl.BlockSpec(shp, lambda i: (0,) * len(shp))
    return pl.pallas_call(
        body, name=name,
        out_shape=(jax.ShapeDtypeStruct((3, L, C), BF16), jax.ShapeDtypeStruct((nh, CHUNK, CHUNK), F32),
                   jax.ShapeDtypeStruct((nh, CHUNK, dh), F32), jax.ShapeDtypeStruct((1, C), F32)),
        grid=(nsteps,),
        in_specs=[pl.BlockSpec((None, tr, C), lambda i: (1, i, 0)), pl.BlockSpec((None, tr, C), lambda i: (2, i, 0)),
                  pl.BlockSpec((tr, C), lambda i: (i, 1)), pl.BlockSpec((tr, C), lambda i: (i, 0)), full((1, C)),
                  full((nh, CHUNK, CHUNK)), full((nh, CHUNK, dh))],
        out_specs=(pl.BlockSpec((3, tr, C), lambda i: (0, i, 0)), full((nh, CHUNK, CHUNK)), full((nh, CHUNK, dh)),
                   full((1, C))),
        scratch_shapes=[pltpu.VMEM((tr, C), F32), pltpu.VMEM((SUBLANES, C), F32)],
        compiler_params=_cparams(("arbitrary",)),
    )(proj3, proj3, dmix, dz, norm_g.reshape(1, C), w, bfull)


def _ffn_act_fwd(up3, conv_w, conv_b, name):
    _, L, Fh = up3.shape
    cb = LANES
    w2 = conv_w.reshape(3, 2, Fh).transpose(1, 0, 2)
    b2 = conv_b.reshape(2, 1, Fh)

    def body(u_ref, w_ref, b_ref, o_ref, ot_ref, gv_ref):
        g = _conv3(_taps(u_ref[0].astype(F32)), w_ref[0]) + b_ref[0]
        v = _conv3(_taps(u_ref[1].astype(F32)), w_ref[1]) + b_ref[1]
        gv_ref[0] = g.astype(BF16)
        gv_ref[1] = v.astype(BF16)
        ab = (g * _sigmoid(g) * v).astype(BF16)
        o_ref[...] = ab
        ot_ref[...] = _transpose_on_mxu(ab)

    blk3 = pl.BlockSpec((2, L, cb), lambda j: (0, 0, j))
    return pl.pallas_call(
        body, name=name,
        out_shape=(jax.ShapeDtypeStruct((L, Fh), BF16), jax.ShapeDtypeStruct((Fh, L), BF16),
                   jax.ShapeDtypeStruct((2, L, Fh), BF16)),
        grid=(Fh // cb,),
        in_specs=[blk3, pl.BlockSpec((2, 3, cb), lambda j: (0, 0, j)), pl.BlockSpec((2, 1, cb), lambda j: (0, 0, j))],
        out_specs=(pl.BlockSpec((L, cb), lambda j: (0, j)), pl.BlockSpec((cb, L), lambda j: (j, 0)), blk3),
        compiler_params=_cparams(("parallel",)),
    )(up3, w2, b2)


def _ffn_act_bwd(up3, gv3, da, conv_w, h2t, name):
    _, L, Fh = up3.shape
    D = h2t.shape[0]
    cb = LANES
    nb = Fh // cb
    w2 = conv_w.reshape(3, 2, Fh).transpose(1, 0, 2)

    def body(u_ref, gv_ref, d_ref, w_ref, h_ref, o_ref, dw_ref, db_ref, wg_ref, wv_ref, scr):
        j = pl.program_id(0)

        @pl.when(j == 0)
        def _():
            scr[1] = jnp.zeros((2, L, cb), BF16)

        prev = scr.at[(j + 1) % 2]
        wg_ref[...] = jnp.dot(h_ref[...], prev[0], preferred_element_type=F32).astype(BF16)
        wv_ref[...] = jnp.dot(h_ref[...], prev[1], preferred_element_type=F32).astype(BF16)
        tg, tv = _taps(u_ref[0].astype(F32)), _taps(u_ref[1].astype(F32))
        wg, wv = w_ref[0], w_ref[1]
        g = gv_ref[0].astype(F32)
        v = gv_ref[1].astype(F32)
        sg = _sigmoid(g)
        dav = d_ref[...].astype(F32)
        dg = dav * v * (sg * (1.0 + g * (1.0 - sg)))
        dv = dav * (g * sg)
        dug = _conv3_t(dg, wg).astype(BF16)
        duv = _conv3_t(dv, wv).astype(BF16)
        o_ref[0] = dug
        o_ref[1] = duv
        cur = scr.at[j % 2]
        cur[0] = dug
        cur[1] = duv
        for tap, (dwg, dwv) in enumerate(zip(_conv3_dw(dg, tg), _conv3_dw(dv, tv))):
            dw_ref[0, tap:tap + 1, :] = dwg
            dw_ref[1, tap:tap + 1, :] = dwv
        db_ref[0] = jnp.sum(dg, axis=0, keepdims=True)
        db_ref[1] = jnp.sum(dv, axis=0, keepdims=True)

    here = lambda j: jnp.minimum(j, nb - 1)
    before = lambda j: jnp.maximum(j - 1, 0)
    blk3 = pl.BlockSpec((2, L, cb), lambda j: (0, 0, here(j)))
    dup, dw2, db2, dwg, dwv = pl.pallas_call(
        body, name=name,
        out_shape=(jax.ShapeDtypeStruct((2, L, Fh), BF16), jax.ShapeDtypeStruct((2, 3, Fh), F32),
                   jax.ShapeDtypeStruct((2, 1, Fh), F32), jax.ShapeDtypeStruct((D, Fh), BF16),
                   jax.ShapeDtypeStruct((D, Fh), BF16)),
        grid=(nb + 1,),
        in_specs=[blk3, blk3, pl.BlockSpec((L, cb), lambda j: (0, here(j))),
                  pl.BlockSpec((2, 3, cb), lambda j: (0, 0, here(j))), pl.BlockSpec((D, L), lambda j: (0, 0))],
        out_specs=(blk3, pl.BlockSpec((2, 3, cb), lambda j: (0, 0, here(j))),
                   pl.BlockSpec((2, 1, cb), lambda j: (0, 0, here(j))),
                   pl.BlockSpec((D, cb), lambda j: (0, before(j))), pl.BlockSpec((D, cb), lambda j: (0, before(j)))),
        scratch_shapes=[pltpu.VMEM((2, 2, L, cb), BF16)],
        compiler_params=_cparams(("arbitrary",), VMEM_LIMIT_S5),
    )(up3, gv3, da, w2, h2t)
    return dup, dw2.transpose(1, 0, 2).reshape(3, 2 * Fh), db2.reshape(2 * Fh), jnp.concatenate([dwg, dwv], axis=1)


def _local_step(x, tgt, w, layer_weights, on_layer_grads):
    L, D = x.shape
    depth = w['norm_mix_g'].shape[0]
    saved = []
    for i in range(depth):
        j = i // 2
        wb = dict(layer_weights(2 * i, x))
        s = {'x': x, 'wb': wb}
        if i % 2 == 0:
            proj4, s['hT'] = _norm_mm(x, w['norm_mix_g'][i], wb['even_w_in'], BF16, "even_in_fwd", ok=('seg', 4))
            s['proj'] = proj4
            mixin = _sconv_fwd(proj4, w['even_conv_w'][j], "sconv_fwd")
            prm = (w['ssm_log_step'][j], w['ssm_a_re'][j], w['ssm_a_im'][j], w['ssm_b_re'][j], w['ssm_b_im'][j],
                   w['ssm_c_re'][j], w['ssm_c_im'][j])
            (lr, li, bmat, cmat), prep_vjp = jax.vjp(_s5_prep, *prm)
            yraw, s_re, s_im = _s5_fwd(proj4, lr, li, bmat, cmat, w['ssm_d'][j], "s5_fwd")
            mixin = _glu_fwd(yraw, wb['ssm_glu_w'], w['ssm_glu_b'][j], mixin, "glu_fwd")
            s.update(yraw=yraw, s_re=s_re, s_im=s_im, s5=(lr, li, bmat, cmat), prep_vjp=prep_vjp)
            s['mixinT'] = mixin.T
            x = _mm(mixin, wb['even_w_out'], 'nn', F32, "even_out_fwd", res=x)
        else:
            proj3, s['hT'] = _norm_mm(x, w['norm_mix_g'][i], wb['odd_w_in'], BF16, "odd_in_fwd", ok=('seg', 3))
            s['proj'] = proj3
            mixin, mixin_t = _pool_fwd(proj3, w['pool_w'][j], w['pool_scale'][j], "pool_fwd")
            mixin, s['mixinT'] = _sgu_fwd(proj3, w['sgu_norm_g'][j], w['sgu_w'][j], w['sgu_b'][j], mixin, mixin_t,
                                          "sgu_fwd")
            x = _mm(mixin, wb['odd_w_out'], 'nn', F32, "odd_out_fwd", res=x)
        s['x1'] = x
        wb.update(layer_weights(2 * i + 1, x))
        up3, h2t = _norm_mm(x, w['norm_ffn_g'][i], wb['ffn_w_up'], BF16, "ffn_up_fwd", ok=('seg', 2))
        a, at, gv3 = _ffn_act_fwd(up3, w['ffn_conv_w'][i], w['ffn_conv_b'][i], "ffn_act_fwd")
        x = _mm(a, wb['ffn_w_down'], 'nn', F32, "ffn_down_fwd", res=x)
        s.update(h2T=h2t, up3=up3, aT=at, gv3=gv3)
        saved.append(s)

    loss8, dx, dxb, dg_final = _loss_head(x, w['norm_final_g'], tgt)
    gs = {n: [None] * w[n].shape[0] for n in SMALL if n != 'norm_final_g'}
    gs['norm_final_g'] = dg_final.reshape(D)

    dep = None
    for i in reversed(range(depth)):
        j = i // 2
        s = saved[i]
        wb = s['wb']
        gb = {}
        da = _mm(dxb, wb['ffn_w_down'], 'nt', BF16, "ffn_down_dgrad", dep=dep)
        gb['ffn_w_down'] = _mm(s['aT'], dxb, 'nn', BF16, "ffn_down_wgrad")
        dup3, dcw, dcb, gb['ffn_w_up'] = _ffn_act_bwd(s['up3'], s['gv3'], da, w['ffn_conv_w'][i], s['h2T'],
                                                      "ffn_act_bwd")
        gs['ffn_conv_w'][i], gs['ffn_conv_b'][i] = dcw, dcb
        dep = on_layer_grads(2 * i + 1, gb)
        dx, dxb, dg = _mm_norm_bwd(dup3, wb['ffn_w_up'], s['x1'], w['norm_ffn_g'][i], dx, "ffn_up_dgrad",
                              ak=('seg', 2), dep=dep)
        gs['norm_ffn_g'][i] = dg.reshape(D)
        gb = {}
        if i % 2 == 0:
            dmix = _mm(dxb, wb['even_w_out'], 'nt', F32, "even_out_dgrad")
            gb['even_w_out'] = _mm(s['mixinT'], dxb, 'nn', BF16, "even_out_wgrad")
            dproj, dcw = _sconv_bwd(s['proj'], dmix, w['even_conv_w'][j], "sconv_bwd")
            gs['even_conv_w'][j] = dcw
            dyraw, dglu_w, dglu_b = _glu_bwd(s['yraw'], dmix, wb['ssm_glu_w'], w['ssm_glu_b'][j], "glu_bwd")
            gb['ssm_glu_w'] = dglu_w.astype(BF16)
            gs['ssm_glu_b'][j] = dglu_b.reshape(-1)
            lr, li, bmat, cmat = s['s5']
            dproj, dbm, dcm, dlam, dd = _s5_bwd(dyraw, s['proj'], dproj, s['s_re'], s['s_im'], lr, li, bmat, cmat,
                                               w['ssm_d'][j], "s5_bwd")
            gs['ssm_d'][j] = dd.reshape(-1)
            dcm = jnp.swapaxes(dcm, 1, 2)
            dprm = s['prep_vjp']((dlam[:, 0:1, :], dlam[:, 1:2, :], dbm, dcm))
            for n, gval in zip(('ssm_log_step', 'ssm_a_re', 'ssm_a_im', 'ssm_b_re', 'ssm_b_im', 'ssm_c_re',
                                'ssm_c_im'), dprm):
                gs[n][j] = gval
            gb['even_w_in'] = _mm(s['hT'], dproj, 'nn', BF16, "even_in_wgrad", bk=('seg', 4))
            w_in, in_kind, in_name = wb['even_w_in'], ('seg', 4), "even_in_dgrad"
        else:
            dmix = _mm(dxb, wb['odd_w_out'], 'nt', F32, "odd_out_dgrad")
            gb['odd_w_out'] = _mm(s['mixinT'], dxb, 'nn', BF16, "odd_out_wgrad")
            dz, dpw, dps = _pool_bwd(s['proj'], dmix, w['pool_w'][j], w['pool_scale'][j], "pool_bwd")
            gs['pool_w'][j], gs['pool_scale'][j] = dpw, dps.reshape(-1)
            dproj, dsw, dsb, dsg = _sgu_bwd(s['proj'], dmix, dz, w['sgu_norm_g'][j], w['sgu_w'][j], w['sgu_b'][j],
                                            "sgu_bwd")
            gs['sgu_w'][j], gs['sgu_b'][j], gs['sgu_norm_g'][j] = dsw, jnp.sum(dsb, axis=-1), dsg.reshape(-1)
            gb['odd_w_in'] = _mm(s['hT'], dproj, 'nn', BF16, "odd_in_wgrad", bk=('seg', 3))
            w_in, in_kind, in_name = wb['odd_w_in'], ('seg', 3), "odd_in_dgrad"
        dep = on_layer_grads(2 * i, gb)
        dx, dxb, dg = _mm_norm_bwd(dproj, w_in, s['x'], w['norm_mix_g'][i], dx, in_name, ak=in_kind, dep=dep)
        gs['norm_mix_g'][i] = dg.reshape(D)

    gsmall = {n: (v if n == 'norm_final_g' else jnp.stack(v)) for n, v in gs.items()}
    return loss8[0, 0], dx, gsmall


_HBM = pl.BlockSpec(memory_space=pltpu.HBM)
_CHIP_FLIPS = ((0, 0), (1, 0), (0, 1), (1, 1))


def _coords():
    return lax.axis_index("x"), lax.axis_index("y"), lax.axis_index("c")


def _flip(v, f):
    return 1 - v if f else v


def _shard_of(ref, axis, s, width):
    start = pl.multiple_of(s * width, LANES if axis == ref.ndim - 1 else 16) if width % 16 == 0 else s * width
    idx = [slice(None)] * ref.ndim
    idx[axis] = pl.ds(start, width)
    return ref.at[tuple(idx)]


_SEM = pl.BlockSpec(memory_space=pltpu.SEMAPHORE)
_ANY = pl.BlockSpec(memory_space=pl.ANY)
_DATAFLOW = pltpu.SideEffectType.DATAFLOW_SIDE_EFFECTING


def _in_hbm(a):
    return pltpu.with_memory_space_constraint(a, pltpu.HBM)


def _model_layer(name, l):
    if name.startswith('ffn'):
        return l
    return 2 * l + 1 if name.startswith('odd') else 2 * l


def _place_quarter(shard, l, axis, chip, dtype, dep=None):
    _, r, c = shard.shape
    tr = _pick(r, prefs=(512, 256, 128, 64, 32, 16))
    nrb = r // tr

    def body(chip_ref, i_ref, *rest):
        rest[-1][...] = i_ref[...].astype(dtype)

    if axis == 1:
        out_shape, o_map = (r, c * N_CHIPS), (lambda i, s: (i, s[0]))
    else:
        out_shape, o_map = (r * N_CHIPS, c), (lambda i, s: (s[0] * nrb + i, 0))
    in_specs = [pl.BlockSpec((None, tr, c), lambda i, s: (l, i, 0))]
    args = [chip, shard]
    if dep is not None:
        in_specs.append(pl.BlockSpec(memory_space=pl.ANY))
        args.append(dep)
    return pl.pallas_call(
        body, name="place_quarter", out_shape=jax.ShapeDtypeStruct(out_shape, dtype),
        grid_spec=pltpu.PrefetchScalarGridSpec(
            num_scalar_prefetch=1, grid=(nrb,), in_specs=in_specs, out_specs=pl.BlockSpec((tr, c), o_map)),
        compiler_params=_cparams(("parallel",)),
    )(*args)


def _gather_copies(land_refs, send_sem, recv_sem, axes, landing_chip_of, first=0):
    x, y, c = _coords()
    out = []
    for j, land in enumerate(land_refs):
        width = land.shape[axes[j]] // N_CHIPS
        for f in (1, 2, 3):
            fx, fy = _CHIP_FLIPS[f]
            px, py = _flip(x, fx), _flip(y, fy)
            lx, ly = landing_chip_of(px, py)
            out.append(pltpu.make_async_remote_copy(
                src_ref=_shard_of(land, axes[j], 2 * x + y, width), dst_ref=_shard_of(land, axes[j], 2 * lx + ly, width),
                send_sem=send_sem.at[3 * (first + j) + f - 1], recv_sem=recv_sem.at[3 * (first + j) + f - 1],
                device_id=(px, py, c), device_id_type=MESH))
    return out


def _gather_start(tag, lands, axes, dep=None):
    n = len(lands)

    def body(*refs):
        land_refs, send_sem, recv_sem = refs[:n], refs[-3], refs[-2]
        x, y, _ = _coords()
        for cp in _gather_copies(land_refs, send_sem, recv_sem, axes, lambda px, py: (x, y)):
            cp.start()
        refs[-1][...] = jnp.zeros_like(refs[-1])

    thru = [pltpu.HBM(a.shape, a.dtype) for a in lands]
    outs = pl.pallas_call(
        body, name=f"gather_start_{tag}",
        out_shape=tuple(thru + [pltpu.SemaphoreType.DMA((3 * n,)), pltpu.SemaphoreType.DMA((3 * n,)),
                                jax.ShapeDtypeStruct((SUBLANES, LANES), F32)]),
        in_specs=[_HBM] * n + ([_ANY] if dep is not None else []),
        out_specs=tuple([_HBM] * n + [_SEM, _SEM, pl.BlockSpec(memory_space=pltpu.VMEM)]),
        input_output_aliases={i: i for i in range(n)},
        compiler_params=pltpu.CompilerParams(has_side_effects=_DATAFLOW),
    )(*[_in_hbm(a) for a in lands], *([dep] if dep is not None else []))
    return list(outs[:n]), outs[n], outs[n + 1], outs[n + 2]


def _gather_wait(tag, lands, send_sem, recv_sem, axes, after, first=0):
    n = len(lands)

    def body(*refs):
        for cp in _gather_copies(refs[:n], refs[n], refs[n + 1], axes, lambda px, py: (px, py), first):
            cp.wait_send()
            cp.wait_recv()

    outs = pl.pallas_call(
        body, name=f"gather_wait_{tag}", out_shape=tuple(pltpu.HBM(a.shape, a.dtype) for a in lands),
        in_specs=[_HBM] * n + [_SEM, _SEM, _ANY], out_specs=tuple([_HBM] * n),
        input_output_aliases={i: i for i in range(n)},
        compiler_params=pltpu.CompilerParams(has_side_effects=_DATAFLOW),
    )(*lands, send_sem, recv_sem, after)
    return list(outs)


N_SLOTS = N_DEV - 1


def _scatter_sends(grad_refs, land_refs, send_sem, recv_sem, meta):
    x, y, c = _coords()
    out = []
    for j, (axis, owner, q, width) in enumerate(meta):
        other = c if owner == 0 else 1 - c
        for f, (fx, fy) in enumerate(_CHIP_FLIPS):
            px, py = _flip(x, fx), _flip(y, fy)
            slot = f + 4 * other - 1
            out.append((other if f == 0 else None, pltpu.make_async_remote_copy(
                src_ref=_shard_of(grad_refs[j], axis, 2 * px + py, width), dst_ref=land_refs[j].at[q, slot],
                send_sem=send_sem.at[4 * j + f], recv_sem=recv_sem.at[N_SLOTS * j + slot],
                device_id=(px, py, owner), device_id_type=MESH)))
    return out


def _scatter_start(layer, grads, lands, meta):
    n = len(grads)
    uniq = []
    for a in lands:
        if not any(a is u for u in uniq):
            uniq.append(a)
    which = [next(k for k, u in enumerate(uniq) if u is a) for a in lands]
    nu = len(uniq)

    def body(*refs):
        grad_refs, land_u = refs[:n], refs[n:n + nu]
        send_sem, recv_sem = refs[n + nu], refs[n + nu + 1]
        for other, cp in _scatter_sends(grad_refs, [land_u[k] for k in which], send_sem, recv_sem, meta):
            if other is None:
                cp.start()
            else:
                pl.when(other == 1)(cp.start)
        refs[-1][...] = jnp.zeros_like(refs[-1])

    thru = [pltpu.HBM(a.shape, a.dtype) for a in list(grads) + uniq]
    outs = pl.pallas_call(
        body, name=f"scatter_start_{layer}",
        out_shape=tuple([pltpu.SemaphoreType.DMA((4 * n,)), pltpu.SemaphoreType.DMA((N_SLOTS * n,))] + thru
                        + [jax.ShapeDtypeStruct((SUBLANES, LANES), F32)]),
        in_specs=[_HBM] * (n + nu),
        out_specs=tuple([_SEM, _SEM] + [_HBM] * (n + nu) + [pl.BlockSpec(memory_space=pltpu.VMEM)]),
        input_output_aliases={i: 2 + i for i in range(n + nu)},
        compiler_params=pltpu.CompilerParams(has_side_effects=_DATAFLOW),
    )(*[_in_hbm(a) for a in list(grads) + uniq])
    new_lands = [outs[2 + n + k] for k in which]
    return outs[0], outs[1], list(outs[2:2 + n]), new_lands, outs[-1]


def _scatter_wait(started, lands):
    nl = len(lands)
    flat_grads = [g for s in started for g in s[2]]
    ng, ns = len(flat_grads), len(started)

    def body(*refs):
        land_refs = refs[:nl]
        grad_refs = refs[nl:nl + ng]
        sem_refs = refs[nl + ng:nl + ng + 2 * ns]
        _, _, c = _coords()
        off = 0
        for k, (_, _, grads, idx, meta) in enumerate(started):
            send_sem, recv_sem = sem_refs[2 * k], sem_refs[2 * k + 1]
            lr = [land_refs[i] for i in idx]
            for other, cp in _scatter_sends(grad_refs[off:off + len(grads)], lr, send_sem, recv_sem, meta):
                if other is None:
                    cp.wait_send()
                else:
                    pl.when(other == 1)(cp.wait_send)
            for j, (axis, owner, q, width) in enumerate(meta):
                mine = (c if owner == 0 else 1 - c) == 0

                @pl.when(mine)
                def _():
                    for slot in range(N_SLOTS):
                        land = lr[j].at[q, slot]
                        pltpu.make_async_remote_copy(
                            src_ref=land, dst_ref=land, send_sem=send_sem.at[0], recv_sem=recv_sem.at[N_SLOTS * j + slot],
                            device_id=_coords(), device_id_type=MESH).wait_recv()
            off += len(grads)

    args = list(lands) + flat_grads
    thru = [pltpu.HBM(a.shape, a.dtype) for a in args]
    sems = [s for st in started for s in st[:2]]
    outs = pl.pallas_call(
        body, name="scatter_wait", out_shape=tuple(thru), in_specs=[_HBM] * (nl + ng) + [_SEM] * (2 * ns),
        out_specs=tuple([_HBM] * (nl + ng)), input_output_aliases={i: i for i in range(nl + ng)},
        compiler_params=pltpu.CompilerParams(has_side_effects=_DATAFLOW),
    )(*args, *sems)
    return list(outs[:nl]), list(outs[nl:])


def _sum_and_share(recv, layer_grads, axis, chip, name, dep=None):
    n, ns, r, c = recv.shape
    tr = _pick(r, prefs=(256, 128, 64, 32, 16))
    nr = r // tr
    nsteps = n * nr
    nlay = len(layer_grads)
    own_map = (lambda h, i, s: (i, s[0])) if axis == 1 else (lambda h, i, s: (s[0] * nr + i, 0))

    def body(chip_ref, i_ref, *rest):
        g_refs = rest[:nlay]
        o_ref, buf, loc_sems, send_sems, recv_sems = rest[nlay + (dep is not None):]
        h, i = pl.program_id(0), pl.program_id(1)
        step = h * nr + i
        slot = step % 2
        x, y, core = _coords()
        layer = core * n + h
        own = g_refs[0][...]
        for l in range(1, nlay):
            own = jnp.where(layer == l, g_refs[l][...], own)

        def copies(sl):
            dst = o_ref.at[core * n + h, pl.ds(pl.multiple_of(i * tr, tr), tr), :]
            loc = pltpu.make_async_copy(buf.at[sl], dst, loc_sems.at[sl])
            rem = pltpu.make_async_remote_copy(
                src_ref=buf.at[sl], dst_ref=dst, send_sem=send_sems.at[sl], recv_sem=recv_sems.at[step],
                device_id=(x, y, 1 - core), device_id_type=MESH)
            return loc, rem

        def drain(sl):
            loc, rem = copies(sl)
            loc.wait()
            rem.wait_send()

        pl.when(step >= 2)(lambda: drain(slot))
        acc = own.astype(F32)
        for s in range(ns):
            acc = acc + i_ref[s].astype(F32)
        buf[slot] = acc
        loc, rem = copies(slot)
        loc.start()
        rem.start()

        @pl.when(step == nsteps - 1)
        def _():
            drain(slot)
            if nsteps > 1:
                drain(1 - slot)
            for hh in range(n):
                for ii in range(nr):
                    land = o_ref.at[(1 - core) * n + hh, pl.ds(ii * tr, tr), :]
                    pltpu.make_async_remote_copy(
                        src_ref=buf.at[0], dst_ref=land, send_sem=send_sems.at[0], recv_sem=recv_sems.at[hh * nr + ii],
                        device_id=(x, y, 1 - core), device_id_type=MESH).wait_recv()

    return pl.pallas_call(
        body, name=name, out_shape=jax.ShapeDtypeStruct((2 * n, r, c), F32),
        grid_spec=pltpu.PrefetchScalarGridSpec(
            num_scalar_prefetch=1, grid=(n, nr),
            in_specs=[pl.BlockSpec((None, ns, tr, c), lambda h, i, s: (h, 0, i, 0))]
            + [pl.BlockSpec((tr, c), own_map)] * nlay + ([pl.BlockSpec(memory_space=pl.ANY)] if dep is not None else []),
            out_specs=_HBM,
            scratch_shapes=[pltpu.VMEM((2, tr, c), F32), pltpu.SemaphoreType.DMA((2,)),
                            pltpu.SemaphoreType.DMA((2,)), pltpu.SemaphoreType.DMA((nsteps,))]),
        compiler_params=_cparams(("arbitrary", "arbitrary")),
    )(chip, recv, *layer_grads, *([dep] if dep is not None else []))


def _adamw_update(w_ref, g_ref, m_ref, v_ref, d_ref, mo_ref, vo_ref):
    bc1 = 1.0 - ADAM_B1 ** ADAM_STEP
    bc2 = 1.0 - ADAM_B2 ** ADAM_STEP
    gv = g_ref[...]
    mn = ADAM_B1 * m_ref[...] + (1.0 - ADAM_B1) * gv
    vn = ADAM_B2 * v_ref[...] + (1.0 - ADAM_B2) * (gv * gv)
    d_ref[...] = -ADAM_LR * ((mn / bc1) / (jnp.sqrt(vn / bc2) + ADAM_EPS) + ADAM_WD * w_ref[...])
    mo_ref[...] = mn
    vo_ref[...] = vn


def _adamw(w, g, m, v, name):
    def body(*refs):
        _adamw_update(*refs)

    tr = _pick(w.shape[0], prefs=(256, 128, 64, 32, 16, 8))
    blk = pl.BlockSpec((tr, w.shape[1]), lambda i: (i, 0))
    sds = jax.ShapeDtypeStruct(w.shape, F32)
    return pl.pallas_call(
        body, name=name, out_shape=(sds, sds, sds), grid=(w.shape[0] // tr,), in_specs=[blk] * 4,
        out_specs=(blk,) * 3, compiler_params=_cparams(("parallel",)),
    )(w, g, m, v)


def _adamw_many(tensors, name, by_layer=False):
    n = len(tensors)

    def body(*refs):
        for t in range(n):
            _adamw_update(*refs[4 * t:4 * t + 4], *refs[4 * n + 3 * t:4 * n + 3 * t + 3])

    def spec(a):
        nd = a.ndim
        if by_layer:
            return pl.BlockSpec((1,) + a.shape[1:], lambda i: (i,) + (0,) * (nd - 1))
        return pl.BlockSpec(a.shape, lambda i: (0,) * nd)

    steps = tensors[0][0].shape[0] if by_layer else 1
    outs = pl.pallas_call(
        body, name=name, out_shape=tuple(jax.ShapeDtypeStruct(t[0].shape, F32) for t in tensors for _ in range(3)),
        grid=(steps,), in_specs=[spec(a) for t in tensors for a in t],
        out_specs=tuple(spec(t[0]) for t in tensors for _ in range(3)), compiler_params=_cparams(("parallel",)),
    )(*[a for t in tensors for a in t])
    return [tuple(outs[3 * t:3 * t + 3]) for t in range(n)]


_PACK_QUANTUM = 256 * LANES


def _pack(arrs):
    flat = jnp.concatenate([a.reshape(-1).astype(F32) for a in arrs])
    flat = jnp.pad(flat, (0, (-flat.shape[0]) % _PACK_QUANTUM))
    return flat.reshape(-1, LANES)


def _unpack(p, shapes):
    flat = p.reshape(-1)
    out, off = [], 0
    for s in shapes:
        n = int(np.prod(s))
        out.append(flat[off:off + n].reshape(s))
        off += n
    return out


def kernel(*args):
    nw = len(WEIGHTS)
    x, tgt = args[0], args[1 + nw]
    w = dict(zip(WEIGHTS, args[1:1 + nw]))
    m = dict(zip(WEIGHTS, args[2 + nw:2 + 2 * nw]))
    v = dict(zip(WEIGHTS, args[2 + 2 * nw:2 + 3 * nw]))
    _, L, D = x.shape
    chip = 2 * lax.axis_index("x") + lax.axis_index("y")

    big = list(BIG)
    small_sh_shapes = [w[n].shape for n in SMALL_SHARDED]
    nbig = len(big)
    chip1 = chip.reshape(1).astype(jnp.int32)
    axes2 = [BIG[n] - 1 for n in big] + [0]
    shards = [w[n] for n in big] + [_pack([w[n] for n in SMALL_SHARDED])[None]]
    pairs = [(t, l) for t in range(nbig + 1) for l in range(shards[t].shape[0])]
    depth = w['norm_mix_g'].shape[0]
    part_of = lambda t, l: 0 if t == nbig else 2 * _model_layer(big[t], l) + big[t].startswith('ffn')
    flying, token = {}, None
    for tag, gset in enumerate(([0], list(range(1, 2 * depth)))):
        ids = [k for g in gset for k, (t, l) in enumerate(pairs) if part_of(t, l) == g]
        ts = [pairs[k][0] for k in ids]
        placed = [_place_quarter(shards[t], pairs[k][1], axes2[t], chip1, F32 if t == nbig else BF16, token)
                  for k, t in zip(ids, ts)]
        lands, send, recv, token = _gather_start(tag, placed, [axes2[t] for t in ts], token)
        first = 0
        for g in gset:
            n = sum(1 for t, l in pairs if part_of(t, l) == g)
            flying[g] = (ts[first:first + n], lands[first:first + n], send, recv, first)
            first += n

    def wait_group(g, after):
        ts, lands, send, recv, first = flying[g]
        landed = _gather_wait(g, lands, send, recv, [axes2[t] for t in ts], token if after is None else after, first)
        return dict(zip(ts, landed))

    first = wait_group(0, None)
    packed = first.pop(nbig).reshape(N_CHIPS, -1, LANES)
    per_chip = [_unpack(packed[s], small_sh_shapes) for s in range(N_CHIPS)]
    wl = dict(w)
    for k, n in enumerate(SMALL_SHARDED):
        wl[n] = jnp.concatenate([per_chip[s][k] for s in range(N_CHIPS)], axis=-1)

    def layer_weights(i, after):
        got = first if i == 0 else wait_group(i, after)
        return {big[t]: a for t, a in got.items()}

    small_shapes = [(w[n].shape[:-1] + (w[n].shape[-1] * N_CHIPS,)) if n in SMALL_SHARDED else w[n].shape
                    for n in SMALL] + [(1,)]
    n_small = sum(int(np.prod(s)) for s in small_shapes)
    pack_rows = -(-n_small // _PACK_QUANTUM) * _PACK_QUANTUM // LANES
    nlayers = [w[n].shape[0] for n in big] + [2]
    halves = [n // 2 for n in nlayers]
    quarters = [tuple(w[n].shape[1:]) for n in big] + [(pack_rows // 2 // N_CHIPS, LANES)]
    wire = [BF16] * nbig + [F32]
    land_now = [lax.empty((halves[t], N_SLOTS) + quarters[t], wire[t]) for t in range(nbig + 1)]
    gparts = [[None] * n for n in nlayers]
    started = []

    def start_scatter(tag, ts, ls, arrays):
        meta = [(axes2[t], l // halves[t], l % halves[t], quarters[t][axes2[t]]) for t, l in zip(ts, ls)]
        send, recv, thru, new_lands, token = _scatter_start(tag, arrays, [land_now[t] for t in ts], meta)
        for t, ln in zip(ts, new_lands):
            land_now[t] = ln
        started.append((send, recv, thru, ts, meta, ls))
        return token

    def on_layer_grads(g, gb):
        ts = [big.index(n) for n in gb]
        return start_scatter(g, ts, [g // 2 if big[t].startswith('ffn') else g // 4 for t in ts],
                             [gb[big[t]] for t in ts])

    loss, dx, gsmall = _local_step(x.reshape(L, D), tgt.reshape(L, D), wl, layer_weights, on_layer_grads)
    gpack = _pack([gsmall[n] for n in SMALL] + [loss.reshape(1)])
    start_scatter(2 * depth, [nbig, nbig], [0, 1], [gpack[:pack_rows // 2], gpack[pack_rows // 2:]])
    landed, sent = _scatter_wait([s[:5] for s in started], land_now)
    for (t, l), g in zip([(t, l) for s in started for t, l in zip(s[3], s[5])], sent):
        gparts[t][l] = g
    small_sum = _sum_and_share(landed[nbig], gparts[nbig], 0, chip1, "sum_share_small")
    quarter_rows = small_sum.shape[0] * small_sum.shape[1]
    placed = _place_quarter(small_sum.reshape(1, quarter_rows, LANES), 0, 0, chip1, F32)
    flying_small, send, recv, token = _gather_start("small", [placed], [0])
    gshard = {n: _sum_and_share(landed[t], gparts[t], axes2[t], chip1, "sum_share_" + n, token)
              for t, n in enumerate(big)}
    small_all = _gather_wait("small", flying_small, send, recv, [0], gshard[big[-1]])[0]
    gpack = small_all.reshape(N_CHIPS, 2, quarter_rows // 2, LANES).transpose(1, 0, 2, 3).reshape(pack_rows, LANES)
    gs = dict(zip(SMALL + ['loss'], _unpack(gpack, small_shapes)))
    loss = gs.pop('loss').reshape(())
    for n in SMALL_SHARDED:
        width = w[n].shape[-1]
        gs[n] = lax.dynamic_slice_in_dim(gs[n], chip * width, width, axis=gs[n].ndim - 1)

    grads, delta, new_m, new_v = {}, {}, {}, {}
    for n in big:
        shp = w[n].shape
        flat = lambda a: a.reshape(shp[0] * shp[1], shp[2])
        g = gshard[n]
        grads[n] = g
        d_, m_, v_ = _adamw(flat(w[n]), flat(g), flat(m[n]), flat(v[n]), "adamw_" + n)
        delta[n], new_m[n], new_v[n] = d_.reshape(shp), m_.reshape(shp), v_.reshape(shp)
    sparse = [n for n in SMALL if w[n].ndim == 4 and w[n].shape[-1] < LANES // 2]
    for names, by_layer in ((sparse, True), ([n for n in SMALL if n not in sparse], False)):
        as2d = lambda a: a.reshape(1, -1) if a.ndim == 1 else a
        res = _adamw_many([(as2d(w[n]), as2d(gs[n]), as2d(m[n]), as2d(v[n])) for n in names],
                          "adamw_small_by_layer" if by_layer else "adamw_small", by_layer)
        for n, (d_, m_, v_) in zip(names, res):
            shp = w[n].shape
            grads[n], delta[n], new_m[n], new_v[n] = gs[n], d_.reshape(shp), m_.reshape(shp), v_.reshape(shp)

    return (loss, dx.reshape(1, L, D), *[grads[n] for n in WEIGHTS], *[delta[n] for n in WEIGHTS],
            *[new_m[n] for n in WEIGHTS], *[new_v[n] for n in WEIGHTS])
```

```python
import math

import numpy as np
import jax
import jax.numpy as jnp
from jax import lax
from jax.experimental import pallas as pl
from jax.experimental.pallas import tpu as pltpu

F32 = jnp.float32
BF16 = jnp.bfloat16
MESH = pl.DeviceIdType.MESH

EPS = 1e-6
CHUNK = 128
POOL_WINDOWS = (2, 4, 8, 16)
LANES = 128
SUBLANES = 8
SCAN_CHUNKS = SUBLANES
S5_GROUPS_PER_STEP = 4
MM_TM_CAP, MM_TN_CAP, MM_TK_CAP = 1408, 1408, 2048
MM_TK_WHOLE = 2048
VMEM_LIMIT = 48 * 1024 * 1024
VMEM_LIMIT_S5 = 56 * 1024 * 1024

ADAM_LR, ADAM_B1, ADAM_B2, ADAM_EPS, ADAM_WD, ADAM_STEP = 0.001, 0.9, 0.999, 1e-08, 0.01, 10

WEIGHTS = ['norm_mix_g', 'even_w_in', 'even_conv_w', 'ssm_log_step', 'ssm_a_re', 'ssm_a_im', 'ssm_b_re',
           'ssm_b_im', 'ssm_c_re', 'ssm_c_im', 'ssm_d', 'ssm_glu_w', 'ssm_glu_b', 'even_w_out', 'odd_w_in',
           'pool_w', 'pool_scale', 'sgu_norm_g', 'sgu_w', 'sgu_b', 'odd_w_out', 'norm_ffn_g', 'ffn_w_up',
           'ffn_conv_w', 'ffn_conv_b', 'ffn_w_down', 'norm_final_g']
BIG = {'even_w_in': 2, 'ssm_glu_w': 1, 'even_w_out': 1, 'odd_w_in': 2, 'odd_w_out': 1, 'ffn_w_up': 2,
       'ffn_w_down': 1}
SMALL_SHARDED = ('even_conv_w', 'pool_scale', 'sgu_norm_g', 'ffn_conv_w')
SMALL = [n for n in WEIGHTS if n not in BIG]
N_CHIPS = 4
N_DEV = 8


def _cparams(sem=None, vmem=VMEM_LIMIT):
    kw = dict(vmem_limit_bytes=vmem)
    if sem is not None:
        kw['dimension_semantics'] = sem
    return pltpu.CompilerParams(**kw)


def _pick(n, segs=(), prefs=(1024, 512, 256, 128)):
    for t in prefs:
        if n % t == 0 and all(s % t == 0 for s in segs if s):
            return t
    return n


def _largest_tile(n, segs, cap):
    best = None
    for t in range(LANES, min(n, cap) + 1, LANES):
        if n % t == 0 and all(s % t == 0 for s in segs if s):
            best = t
    return best if best is not None else n


def _ldims(arr, kind):
    if kind is None:
        return arr.shape
    if kind[0] == 'lead':
        return arr.shape[1:]
    return (arr.shape[1], arr.shape[0] * arr.shape[2])


def _segw(arr, kind):
    return arr.shape[2] if (kind is not None and kind[0] == 'seg') else None


def _opspec(arr, kind, br, bc, rfn, cfn):
    if kind is None:
        return pl.BlockSpec((br, bc), lambda i, j, k: (rfn(i, j, k), cfn(i, j, k)))
    if kind[0] == 'lead':
        lead = kind[1]
        return pl.BlockSpec((None, br, bc), lambda i, j, k: (lead, rfn(i, j, k), cfn(i, j, k)))
    per = arr.shape[2] // bc
    return pl.BlockSpec((None, br, bc), lambda i, j, k: (cfn(i, j, k) // per, rfn(i, j, k), cfn(i, j, k) % per))


def _mm(a, b, mode, out_dtype, name, ak=None, bk=None, ok=None, res=None, dep=None):
    ar, ac = _ldims(a, ak)
    br_, bc_ = _ldims(b, bk)
    if mode == 'nn':
        M, K, N = ar, ac, bc_
        assert br_ == K
    else:
        M, K, N = ar, ac, br_
        assert bc_ == K
    sa, sb = _segw(a, ak), _segw(b, bk)
    so = (N // ok[1]) if ok is not None else None
    tm = _largest_tile(M, [], MM_TM_CAP)
    tn = _largest_tile(N, [sb if mode == 'nn' else None, so], MM_TN_CAP)
    ksegs = [sa, sb if mode == 'nt' else None]
    tk = K if (K <= MM_TK_WHOLE and not any(ksegs)) else _largest_tile(K, ksegs, MM_TK_CAP)
    nk = K // tk
    I = lambda i, j, k: i
    J = lambda i, j, k: j
    Kk = lambda i, j, k: k
    a_spec = _opspec(a, ak, tm, tk, I, Kk)
    if mode == 'nn':
        b_spec = _opspec(b, bk, tk, tn, Kk, J)
        dims = (((1,), (0,)), ((), ()))
    else:
        b_spec = _opspec(b, bk, tn, tk, J, Kk)
        dims = (((1,), (1,)), ((), ()))
    if ok is None:
        out_shape = jax.ShapeDtypeStruct((M, N), out_dtype)
        o_spec = pl.BlockSpec((tm, tn), lambda i, j, k: (i, j))
    else:
        out_shape = jax.ShapeDtypeStruct((ok[1], M, N // ok[1]), out_dtype)
        per = (N // ok[1]) // tn
        o_spec = pl.BlockSpec((None, tm, tn), lambda i, j, k: (j // per, i, j % per))
    has_res = res is not None

    def body(*refs):
        a_ref, b_ref = refs[0], refs[1]
        r_ref = refs[2] if has_res else None
        o_ref = refs[n_in]
        prod = lax.dot_general(a_ref[...].astype(BF16), b_ref[...].astype(BF16), dims, preferred_element_type=F32)
        if nk == 1:
            o_ref[...] = (prod + r_ref[...] if has_res else prod).astype(out_dtype)
            return
        acc = refs[-1]
        k = pl.program_id(2)

        @pl.when(k == 0)
        def _():
            acc[...] = prod

        @pl.when(k > 0)
        def _():
            acc[...] += prod

        @pl.when(k == nk - 1)
        def _():
            o = acc[...]
            if has_res:
                o = o + r_ref[...]
            o_ref[...] = o.astype(out_dtype)

    in_specs = [a_spec, b_spec]
    args = [a, b]
    if has_res:
        in_specs.append(pl.BlockSpec((tm, tn), lambda i, j, k: (i, j)))
        args.append(res)
    if dep is not None:
        in_specs.append(pl.BlockSpec(memory_space=pl.ANY))
        args.append(dep)
    n_in = len(args)
    return pl.pallas_call(
        body, name=name, out_shape=out_shape, grid=(M // tm, N // tn, nk), in_specs=in_specs, out_specs=o_spec,
        scratch_shapes=[pltpu.VMEM((tm, tn), F32)] if nk > 1 else [],
        compiler_params=_cparams(("parallel", "parallel", "arbitrary")),
    )(*args)


_G0 = math.sqrt(2.0 / math.pi)
_G1 = 0.044715


def _gelu(x):
    return 0.5 * x * (1.0 + jnp.tanh(_G0 * (x + _G1 * x * x * x)))


def _gelu_grad(x):
    x2 = x * x
    t = jnp.tanh(_G0 * (x + _G1 * x * x2))
    return 0.5 * (1.0 + t) + 0.5 * x * (1.0 - t * t) * (_G0 * (1.0 + 3.0 * _G1 * x2))


def _sigmoid(x):
    return 1.0 / (1.0 + jnp.exp(-x))


def _down(v, k):
    r = pltpu.roll(v, k, axis=0)
    row = lax.broadcasted_iota(jnp.int32, (SUBLANES, v.shape[1]), 0)
    return jnp.concatenate([jnp.where(row >= k, r[:SUBLANES], 0.0), r[SUBLANES:]], axis=0)


def _up(v, k):
    n = v.shape[0]
    r = pltpu.roll(v, n - k, axis=0)
    row = lax.broadcasted_iota(jnp.int32, (SUBLANES, v.shape[1]), 0)
    return jnp.concatenate([r[:n - SUBLANES], jnp.where(row < SUBLANES - k, r[n - SUBLANES:], 0.0)], axis=0)


def _taps(v):
    return _down(v, 2), _down(v, 1), v


def _conv3(taps, w):
    return w[0:1, :] * taps[0] + w[1:2, :] * taps[1] + w[2:3, :] * taps[2]


def _conv3_t(dv, w):
    return w[2:3, :] * dv + w[1:2, :] * _up(dv, 1) + w[0:1, :] * _up(dv, 2)


def _conv3_dw(dv, taps):
    return tuple(jnp.sum(dv * tp, axis=0, keepdims=True) for tp in taps)


def _cmul(ar, ai, br, bi):
    return ar * br - ai * bi, ar * bi + ai * br


def _cpow(lr, li, n):
    rr = ri = None
    br, bi = lr, li
    while n:
        if n & 1:
            rr, ri = (br, bi) if rr is None else _cmul(rr, ri, br, bi)
        n >>= 1
        if n:
            br, bi = _cmul(br, bi, br, bi)
    return rr, ri


NORM_ROWS = 256


def _norm_mm(x, g, b, out_dtype, name, ok=None):
    M, D = x.shape
    N = b.shape[1]
    so = (N // ok[1]) if ok is not None else None
    tm = _largest_tile(M, [], 1024)
    tn = _largest_tile(N, [so], MM_TN_CAP)
    if ok is None:
        out_shape = jax.ShapeDtypeStruct((M, N), out_dtype)
        o_spec = pl.BlockSpec((tm, tn), lambda i, j: (i, j))
    else:
        out_shape = jax.ShapeDtypeStruct((ok[1], M, N // ok[1]), out_dtype)
        per = (N // ok[1]) // tn
        o_spec = pl.BlockSpec((None, tm, tn), lambda i, j: (j // per, i, j % per))

    def body(x_ref, g_ref, b_ref, o_ref, ht_ref, h_scr):
        @pl.when(pl.program_id(1) == 0)
        def _():
            for c in range(tm // NORM_ROWS):
                rows = pl.ds(c * NORM_ROWS, NORM_ROWS)
                xv = x_ref[rows, :]
                h = xv * lax.rsqrt(jnp.mean(xv * xv, axis=-1, keepdims=True) + EPS) * g_ref[...]
                h_scr[rows, :] = h.astype(BF16)
                ht_ref[:, rows] = h.T.astype(BF16)

        o_ref[...] = jnp.dot(h_scr[...], b_ref[...], preferred_element_type=F32).astype(out_dtype)

    return pl.pallas_call(
        body, name=name, out_shape=(out_shape, jax.ShapeDtypeStruct((D, M), BF16)), grid=(M // tm, N // tn),
        in_specs=[pl.BlockSpec((tm, D), lambda i, j: (i, 0)), pl.BlockSpec((1, D), lambda i, j: (0, 0)),
                  pl.BlockSpec((D, tn), lambda i, j: (0, j))],
        out_specs=(o_spec, pl.BlockSpec((D, tm), lambda i, j: (0, i))),
        scratch_shapes=[pltpu.VMEM((tm, D), BF16)], compiler_params=_cparams(("parallel", "arbitrary")),
    )(x, g.reshape(1, D), b)


def _mm_norm_bwd(a, b, x, g, dres, name, ak=None, dep=None):
    M, K = _ldims(a, ak)
    D = b.shape[0]
    assert b.shape[1] == K and x.shape == (M, D)
    sa = _segw(a, ak)
    tm = _largest_tile(M, [], 1024)
    whole_segs = bool(sa) and K <= MM_TK_WHOLE
    tk = K if (K <= MM_TK_WHOLE) else _largest_tile(K, [sa], MM_TK_CAP)
    ni, nk = M // tm, K // tk
    if whole_segs:
        a_spec = pl.BlockSpec((a.shape[0], tm, sa), lambda i, k: (0, i, 0))
    else:
        a3 = _opspec(a, ak, tm, tk, lambda i, j, k: i, lambda i, j, k: k)
        a_spec = pl.BlockSpec(a3.block_shape, lambda i, k: a3.index_map(i, 0, k))
    n_in = 5 + (dep is not None)

    def body(*refs):
        a_ref, b_ref, x_ref, g_ref, r_ref = refs[:5]
        dx_ref, dxb_ref, dg_ref, acc, accg = refs[n_in:]
        i, k = pl.program_id(0), pl.program_id(1)
        av = jnp.concatenate([a_ref[s] for s in range(a.shape[0])], axis=1) if whole_segs else a_ref[...]
        prod = lax.dot_general(av.astype(BF16), b_ref[...], (((1,), (1,)), ((), ())), preferred_element_type=F32)

        @pl.when(k == 0)
        def _():
            acc[...] = prod

        @pl.when(k > 0)
        def _():
            acc[...] += prod

        @pl.when((i == 0) & (k == 0))
        def _():
            accg[...] = jnp.zeros_like(accg)

        @pl.when(k == nk - 1)
        def _():
            for c in range(tm // NORM_ROWS):
                rows = pl.ds(c * NORM_ROWS, NORM_ROWS)
                xv = x_ref[rows, :]
                r = lax.rsqrt(jnp.mean(xv * xv, axis=-1, keepdims=True) + EPS)
                xh = xv * r
                dhv = acc[rows, :]
                accg[...] += jnp.sum((dhv * xh).reshape(NORM_ROWS // SUBLANES, SUBLANES, D), axis=0)
                dxh = dhv * g_ref[...]
                dxv = r_ref[rows, :] + r * (dxh - xh * jnp.mean(dxh * xh, axis=-1, keepdims=True))
                dx_ref[rows, :] = dxv
                dxb_ref[rows, :] = dxv.astype(BF16)

        @pl.when((i == ni - 1) & (k == nk - 1))
        def _():
            dg_ref[...] = jnp.sum(accg[...], axis=0, keepdims=True)

    row = pl.BlockSpec((tm, D), lambda i, k: (i, 0))
    vec = pl.BlockSpec((1, D), lambda i, k: (0, 0))
    in_specs = [a_spec, pl.BlockSpec((D, tk), lambda i, k: (0, k)), row, vec, row]
    args = [a, b, x, g.reshape(1, D), dres]
    if dep is not None:
        in_specs.append(pl.BlockSpec(memory_space=pl.ANY))
        args.append(dep)
    return pl.pallas_call(
        body, name=name,
        out_shape=(jax.ShapeDtypeStruct((M, D), F32), jax.ShapeDtypeStruct((M, D), BF16),
                   jax.ShapeDtypeStruct((1, D), F32)),
        grid=(ni, nk), in_specs=in_specs, out_specs=(row, row, vec),
        scratch_shapes=[pltpu.VMEM((tm, D), F32), pltpu.VMEM((SUBLANES, D), F32)],
        compiler_params=_cparams(("arbitrary", "arbitrary"), VMEM_LIMIT_S5),
    )(*args)


def _loss_head(x, g, tgt):
    L, D = x.shape
    tr = _pick(L, prefs=(512, 256, 128))
    nsteps = L // tr

    def body(x_ref, g_ref, t_ref, loss_ref, dx_ref, dxb_ref, dg_ref, acc_g, acc_l):
        i = pl.program_id(0)

        @pl.when(i == 0)
        def _():
            acc_g[...] = jnp.zeros_like(acc_g)
            acc_l[...] = jnp.zeros_like(acc_l)

        xv = x_ref[...]
        gv = g_ref[...]
        r = lax.rsqrt(jnp.mean(xv * xv, axis=-1, keepdims=True) + EPS)
        xh = xv * r
        e = xh * gv - t_ref[...]
        acc_l[...] += jnp.sum((e * e).reshape(tr // SUBLANES, SUBLANES, D), axis=0)
        dy = e * (1.0 / D)
        acc_g[...] += jnp.sum((dy * xh).reshape(tr // SUBLANES, SUBLANES, D), axis=0)
        dxh = dy * gv
        dxv = r * (dxh - xh * jnp.mean(dxh * xh, axis=-1, keepdims=True))
        dx_ref[...] = dxv
        dxb_ref[...] = dxv.astype(BF16)

        @pl.when(i == nsteps - 1)
        def _():
            dg_ref[...] = jnp.sum(acc_g[...], axis=0, keepdims=True)
            tot = jnp.sum(jnp.sum(acc_l[...], axis=0, keepdims=True), axis=1, keepdims=True) * (0.5 / D)
            loss_ref[...] = jnp.broadcast_to(tot, (SUBLANES, LANES))

    row = pl.BlockSpec((tr, D), lambda i: (i, 0))
    vec = pl.BlockSpec((1, D), lambda i: (0, 0))
    return pl.pallas_call(
        body, name="loss_head",
        out_shape=(jax.ShapeDtypeStruct((SUBLANES, LANES), F32), jax.ShapeDtypeStruct((L, D), F32),
                   jax.ShapeDtypeStruct((L, D), BF16), jax.ShapeDtypeStruct((1, D), F32)),
        grid=(nsteps,), in_specs=[row, vec, row],
        out_specs=(pl.BlockSpec((SUBLANES, LANES), lambda i: (0, 0)), row, row, vec),
        scratch_shapes=[pltpu.VMEM((SUBLANES, D), F32), pltpu.VMEM((SUBLANES, D), F32)],
        compiler_params=_cparams(("arbitrary",)),
    )(x, g.reshape(1, D), tgt)


def _sconv_fwd(proj4, conv_w, name):
    _, L, C = proj4.shape
    cb = LANES

    def body(p_ref, w_ref, o_ref):
        xa, ba, ca = p_ref[0].astype(F32), p_ref[1].astype(F32), p_ref[2].astype(F32)
        o_ref[...] = (ba * _conv3(_taps(ca * xa), w_ref[...])).astype(BF16)

    return pl.pallas_call(
        body, name=name, out_shape=jax.ShapeDtypeStruct((L, 2 * C), BF16), grid=(C // cb,),
        in_specs=[pl.BlockSpec((3, L, cb), lambda j: (0, 0, j)), pl.BlockSpec((3, cb), lambda j: (0, j))],
        out_specs=pl.BlockSpec((L, cb), lambda j: (0, j)), compiler_params=_cparams(("parallel",)),
    )(proj4, conv_w)


def _sconv_bwd(proj4, dmix, conv_w, name):
    _, L, C = proj4.shape
    cb = LANES

    def body(p_ref, d_ref, w_ref, o_ref, dw_ref):
        xa, ba, ca = p_ref[0].astype(F32), p_ref[1].astype(F32), p_ref[2].astype(F32)
        w = w_ref[...]
        dya = d_ref[...]
        tq = _taps(ca * xa)
        cq = _conv3(tq, w)
        dcq = dya * ba
        dq = _conv3_t(dcq, w)
        for tap, dwt in enumerate(_conv3_dw(dcq, tq)):
            dw_ref[tap:tap + 1, :] = dwt
        o_ref[0] = (dq * ca).astype(BF16)
        o_ref[1] = (dya * cq).astype(BF16)
        o_ref[2] = (dq * xa).astype(BF16)

    return pl.pallas_call(
        body, name=name,
        out_shape=(jax.ShapeDtypeStruct((4, L, C), BF16), jax.ShapeDtypeStruct((3, C), F32)), grid=(C // cb,),
        in_specs=[pl.BlockSpec((3, L, cb), lambda j: (0, 0, j)), pl.BlockSpec((L, cb), lambda j: (0, j)),
                  pl.BlockSpec((3, cb), lambda j: (0, j))],
        out_specs=(pl.BlockSpec((3, L, cb), lambda j: (0, 0, j)), pl.BlockSpec((3, cb), lambda j: (0, j))),
        compiler_params=_cparams(("parallel",)),
    )(proj4, dmix, conv_w)


def _s5_prep(log_step, a_re, a_im, b_re, b_im, c_re, c_im):
    G, P = a_re.shape
    H = b_re.shape[-1]
    gs = S5_GROUPS_PER_STEP
    ns = G // gs
    gu = LANES // H
    lam = lax.complex(a_re, a_im)
    step = jnp.exp(log_step)[:, None]
    lam_bar = jnp.exp(lam * step)
    b_bar = ((lam_bar - 1.0) / lam)[..., None] * lax.complex(b_re, b_im)
    lr = jnp.real(lam_bar).reshape(ns, 1, gs * P)
    li = jnp.imag(lam_bar).reshape(ns, 1, gs * P)
    k = np.arange(ns)[:, None, None]
    oh = jnp.asarray((np.arange(gu)[None, :, None] == gs * (k % (gu // gs)) + np.arange(gs)[None, None, :]),
                     F32)
    bre = jnp.einsum('kgl,klph->kghlp', oh, jnp.real(b_bar).reshape(ns, gs, P, H)).reshape(ns, gu * H, gs * P)
    bim = jnp.einsum('kgl,klph->kghlp', oh, jnp.imag(b_bar).reshape(ns, gs, P, H)).reshape(ns, gu * H, gs * P)
    cre = jnp.einsum('kgl,klhp->klpgh', oh, c_re.reshape(ns, gs, H, P)).reshape(ns, gs * P, gu * H)
    cim = jnp.einsum('kgl,klhp->klpgh', oh, c_im.reshape(ns, gs, H, P)).reshape(ns, gs * P, gu * H)
    return lr, li, jnp.concatenate([bre, bim], axis=2), jnp.concatenate([cre, -cim], axis=1)


def _carry_tile(fr, fi, pr, pi, reverse):
    row = lax.broadcasted_iota(jnp.int32, fr.shape, 0)
    cr = jnp.zeros_like(fr)
    ci = jnp.zeros_like(fi)
    sr = jnp.zeros_like(fr[0:1])
    si = jnp.zeros_like(sr)
    order = range(SCAN_CHUNKS - 1, 0, -1) if reverse else range(0, SCAN_CHUNKS - 1)
    for c in order:
        fcr = jnp.sum(jnp.where(row == c, fr, 0.0), axis=0, keepdims=True)
        fci = jnp.sum(jnp.where(row == c, fi, 0.0), axis=0, keepdims=True)
        mr, mi = _cmul(pr, pi, sr, si)
        sr, si = mr + fcr, mi + fci
        nxt = c - 1 if reverse else c + 1
        cr = jnp.where(row == nxt, sr, cr)
        ci = jnp.where(row == nxt, si, ci)
    return cr, ci


def _scan_order_into(dst_ref, src_ref, T):
    for c in range(SCAN_CHUNKS):
        dst_ref[pl.ds(c, T, stride=SCAN_CHUNKS), :] = src_ref[pl.ds(c * T, T), :].astype(F32)


def _s5_fwd(proj4, lr, li, bmat, cmat, d, name):
    _, L, Du = proj4.shape
    ns, _, W2 = bmat.shape
    W = W2 // 2
    T = L // SCAN_CHUNKS
    rb = _pick(L, prefs=(512, 256, 128))
    per = (ns * LANES) // Du

    def body(ut_ref, lr_ref, li_ref, b_ref, c_ref, d_ref, y_ref, sr_ref, si_ref, u_ref):
        k = pl.program_id(0)
        _scan_order_into(u_ref, ut_ref, T)
        for r in range(L // rb):
            rows = pl.ds(r * rb, rb)
            bu = jnp.dot(u_ref[rows, :].astype(BF16), b_ref[...], preferred_element_type=F32)
            sr_ref[rows, :] = bu[:, :W]
            si_ref[rows, :] = bu[:, W:]
        lam_r = jnp.broadcast_to(lr_ref[...], (SUBLANES, W))
        lam_i = jnp.broadcast_to(li_ref[...], (SUBLANES, W))

        def local(t, carry):
            sr, si = carry
            rows = pl.ds(pl.multiple_of(t * SUBLANES, SUBLANES), SUBLANES)
            mr, mi = _cmul(lam_r, lam_i, sr, si)
            sr = mr + sr_ref[rows, :]
            si = mi + si_ref[rows, :]
            sr_ref[rows, :] = sr
            si_ref[rows, :] = si
            return sr, si

        z = jnp.zeros((SUBLANES, W), F32)
        fr, fi = lax.fori_loop(0, T, local, (z, z))
        pr, pi = _cpow(lam_r, lam_i, T)
        cr, ci = _carry_tile(fr, fi, pr[0:1], pi[0:1], reverse=False)

        def fix(t, carry):
            wr, wi = carry
            rows = pl.ds(pl.multiple_of(t * SUBLANES, SUBLANES), SUBLANES)
            ar, ai = _cmul(wr, wi, cr, ci)
            sr_ref[rows, :] += ar
            si_ref[rows, :] += ai
            return _cmul(wr, wi, lam_r, lam_i)

        lax.fori_loop(0, T, fix, (lam_r, lam_i))
        first = (k % per) == 0
        for r in range(L // rb):
            rows = pl.ds(r * rb, rb)
            s = jnp.concatenate([sr_ref[rows, :], si_ref[rows, :]], axis=1).astype(BF16)
            y = jnp.dot(s, c_ref[...], preferred_element_type=F32)

            @pl.when(first)
            def _():
                y_ref[rows, :] = y + d_ref[...] * u_ref[rows, :]

            @pl.when(jnp.logical_not(first))
            def _():
                y_ref[rows, :] += y

    ublk = pl.BlockSpec((L, LANES), lambda k: (0, k // per))
    sblk = pl.BlockSpec((L, W), lambda k: (0, k))
    lam = pl.BlockSpec((None, 1, W), lambda k: (k, 0, 0))
    return pl.pallas_call(
        body, name=name,
        out_shape=(jax.ShapeDtypeStruct((L, Du), F32), jax.ShapeDtypeStruct((L, ns * W), F32),
                   jax.ShapeDtypeStruct((L, ns * W), F32)),
        grid=(ns,),
        in_specs=[pl.BlockSpec((None, L, LANES), lambda k: (3, 0, k // per)), lam, lam,
                  pl.BlockSpec((None, LANES, 2 * W), lambda k: (k, 0, 0)),
                  pl.BlockSpec((None, 2 * W, LANES), lambda k: (k, 0, 0)),
                  pl.BlockSpec((1, LANES), lambda k: (0, k // per))],
        out_specs=(ublk, sblk, sblk), scratch_shapes=[pltpu.VMEM((L, LANES), F32)],
        compiler_params=_cparams(("arbitrary",), VMEM_LIMIT_S5),
    )(proj4, lr, li, bmat.astype(BF16), cmat.astype(BF16), d.reshape(1, Du))


def _s5_bwd(dy, proj4, dproj, s_re, s_im, lr, li, bmat, cmat, d, name):
    _, L, Du = proj4.shape
    ns, _, W2 = bmat.shape
    W = W2 // 2
    T = L // SCAN_CHUNKS
    rb = _pick(L, prefs=(512, 256, 128))
    per = (ns * LANES) // Du
    NT = (((1,), (1,)), ((), ()))
    TN = (((0,), (0,)), ((), ()))

    def body(dy_ref, ut_ref, dp_in, sr_ref, si_ref, lr_ref, li_ref, b_ref, c_ref, d_ref,
             dut_ref, db_ref, dc_ref, dl_ref, dd_ref, gr_ref, gi_ref, u_ref, du_ref):
        k = pl.program_id(0)
        _scan_order_into(u_ref, ut_ref, T)
        for r in range(L // rb):
            rows = pl.ds(r * rb, rb)
            g = lax.dot_general(dy_ref[rows, :].astype(BF16), c_ref[...], NT, preferred_element_type=F32)
            gr_ref[rows, :] = g[:, :W]
            gi_ref[rows, :] = g[:, W:]
        lam_r = jnp.broadcast_to(lr_ref[...], (SUBLANES, W))
        lam_i = -jnp.broadcast_to(li_ref[...], (SUBLANES, W))

        def local(i, carry):
            gr, gi = carry
            rows = pl.ds(pl.multiple_of((T - 1 - i) * SUBLANES, SUBLANES), SUBLANES)
            mr, mi = _cmul(lam_r, lam_i, gr, gi)
            gr = mr + gr_ref[rows, :]
            gi = mi + gi_ref[rows, :]
            gr_ref[rows, :] = gr
            gi_ref[rows, :] = gi
            return gr, gi

        z = jnp.zeros((SUBLANES, W), F32)
        fr, fi = lax.fori_loop(0, T, local, (z, z))
        pr, pi = _cpow(lam_r, lam_i, T)
        cr, ci = _carry_tile(fr, fi, pr[0:1], pi[0:1], reverse=True)

        def true_g(rows, wr, wi):
            ar, ai = _cmul(wr, wi, cr, ci)
            gr = gr_ref[rows, :] + ar
            gi = gi_ref[rows, :] + ai
            gr_ref[rows, :] = gr
            gi_ref[rows, :] = gi
            return gr, gi

        def fix(i, carry):
            wr, wi, ar_, ai_ = carry
            t = T - 1 - i
            rows = pl.ds(pl.multiple_of(t * SUBLANES, SUBLANES), SUBLANES)
            prev = pl.ds(pl.multiple_of((t - 1) * SUBLANES, SUBLANES), SUBLANES)
            gr, gi = true_g(rows, wr, wi)
            qr, qi = sr_ref[prev, :], si_ref[prev, :]
            ar_ = ar_ + gr * qr + gi * qi
            ai_ = ai_ + gi * qr - gr * qi
            wr, wi = _cmul(wr, wi, lam_r, lam_i)
            return wr, wi, ar_, ai_

        wr, wi, acc_r, acc_i = lax.fori_loop(0, T - 1, fix, (lam_r, lam_i, z, z))
        gr, gi = true_g(pl.ds(0, SUBLANES), wr, wi)
        last = pl.ds((T - 1) * SUBLANES, SUBLANES)
        row = lax.broadcasted_iota(jnp.int32, (SUBLANES, W), 0)
        qr = jnp.where(row >= 1, pltpu.roll(sr_ref[last, :], 1, axis=0), 0.0)
        qi = jnp.where(row >= 1, pltpu.roll(si_ref[last, :], 1, axis=0), 0.0)
        acc_r = acc_r + gr * qr + gi * qi
        acc_i = acc_i + gi * qr - gr * qi
        dl_ref[0:1, :] = jnp.sum(acc_r, axis=0, keepdims=True)
        dl_ref[1:2, :] = jnp.sum(acc_i, axis=0, keepdims=True)

        first = (k % per) == 0
        db = jnp.zeros((LANES, 2 * W), F32)
        dc = jnp.zeros((LANES, 2 * W), F32)
        dd = jnp.zeros((1, LANES), F32)
        for r in range(L // rb):
            rows = pl.ds(r * rb, rb)
            gb = jnp.concatenate([gr_ref[rows, :], gi_ref[rows, :]], axis=1).astype(BF16)
            sb = jnp.concatenate([sr_ref[rows, :], si_ref[rows, :]], axis=1).astype(BF16)
            dyv = dy_ref[rows, :]
            uv = u_ref[rows, :]
            du = lax.dot_general(gb, b_ref[...], NT, preferred_element_type=F32)
            db = db + lax.dot_general(uv.astype(BF16), gb, TN, preferred_element_type=F32)
            dc = dc + lax.dot_general(dyv.astype(BF16), sb, TN, preferred_element_type=F32)
            dd = dd + jnp.sum(dyv * uv, axis=0, keepdims=True)

            @pl.when(first)
            def _():
                du_ref[rows, :] = du + d_ref[...] * dyv

            @pl.when(jnp.logical_not(first))
            def _():
                du_ref[rows, :] += du

        db_ref[...] = db
        dc_ref[...] = dc

        @pl.when(first)
        def _():
            dd_ref[...] = dd

        @pl.when((k % per) == per - 1)
        def _():
            for c in range(SCAN_CHUNKS):
                dut_ref[pl.ds(c * T, T), :] = du_ref[pl.ds(c, T, stride=SCAN_CHUNKS), :].astype(BF16)

    ublk = pl.BlockSpec((L, LANES), lambda k: (0, k // per))
    uslab = pl.BlockSpec((None, L, LANES), lambda k: (3, 0, k // per))
    sblk = pl.BlockSpec((L, W), lambda k: (0, k))
    lam = pl.BlockSpec((None, 1, W), lambda k: (k, 0, 0))
    vec = pl.BlockSpec((1, LANES), lambda k: (0, k // per))
    mat = pl.BlockSpec((None, LANES, 2 * W), lambda k: (k, 0, 0))
    return pl.pallas_call(
        body, name=name,
        out_shape=(jax.ShapeDtypeStruct(dproj.shape, dproj.dtype), jax.ShapeDtypeStruct((ns, LANES, 2 * W), F32),
                   jax.ShapeDtypeStruct((ns, LANES, 2 * W), F32), jax.ShapeDtypeStruct((ns, 2, W), F32),
                   jax.ShapeDtypeStruct((1, Du), F32)),
        grid=(ns,),
        in_specs=[ublk, uslab, pl.BlockSpec(memory_space=pl.ANY), sblk, sblk, lam, lam, mat,
                  pl.BlockSpec((None, 2 * W, LANES), lambda k: (k, 0, 0)), vec],
        out_specs=(uslab, mat, mat, pl.BlockSpec((None, 2, W), lambda k: (k, 0, 0)), vec),
        scratch_shapes=[pltpu.VMEM((L, W), F32), pltpu.VMEM((L, W), F32), pltpu.VMEM((L, LANES), F32),
                        pltpu.VMEM((L, LANES), F32)],
        input_output_aliases={2: 0}, compiler_params=_cparams(("arbitrary",), VMEM_LIMIT_S5),
    )(dy, proj4, dproj, s_re, s_im, lr, li, bmat.astype(BF16), cmat.astype(BF16), d.reshape(1, Du))


def _glu_fwd(yraw, wmat, bias, mixin, name):
    L, C = yraw.shape
    tr = _pick(L, prefs=(512, 256, 128))
    tb = tr // SCAN_CHUNKS
    nl = C // LANES

    def body(y_ref, w_ref, b_ref, m_in, o_ref, scr):
        yg = _gelu(y_ref[...])
        zz = jnp.dot(yg.astype(BF16), w_ref[...], preferred_element_type=F32) + b_ref[...]
        yb = yg * _sigmoid(zz)
        for k in range(nl):
            scr[k] = yb[:, k * LANES:(k + 1) * LANES]
        for c in range(SCAN_CHUNKS):
            for k in range(nl):
                o_ref[c, :, k * LANES:(k + 1) * LANES] = scr[k, pl.ds(c, tb, stride=SCAN_CHUNKS), :].astype(BF16)

    out = pl.pallas_call(
        body, name=name, out_shape=jax.ShapeDtypeStruct((SCAN_CHUNKS, L // SCAN_CHUNKS, 2 * C), BF16),
        grid=(L // tr,),
        in_specs=[pl.BlockSpec((tr, C), lambda i: (i, 0)), pl.BlockSpec((C, C), lambda i: (0, 0)),
                  pl.BlockSpec((1, C), lambda i: (0, 0)), pl.BlockSpec(memory_space=pl.ANY)],
        out_specs=pl.BlockSpec((SCAN_CHUNKS, tb, C), lambda i: (0, i, 1)),
        scratch_shapes=[pltpu.VMEM((nl, tr, LANES), F32)], input_output_aliases={3: 0},
        compiler_params=_cparams(("parallel",)),
    )(yraw, wmat, bias.reshape(1, C), mixin.reshape(SCAN_CHUNKS, L // SCAN_CHUNKS, 2 * C))
    return out.reshape(L, 2 * C)


def _glu_bwd(yraw, dmix, wmat, bias, name):
    L, C = yraw.shape
    tr = _pick(L, prefs=(512, 256, 128))
    nsteps = L // tr
    tb = tr // SCAN_CHUNKS
    nl = C // LANES

    def body(y_ref, d_ref, w_ref, b_ref, dy_ref, dw_ref, db_ref, acc_b, scr):
        i = pl.program_id(0)

        @pl.when(i == 0)
        def _():
            dw_ref[...] = jnp.zeros_like(dw_ref)
            acc_b[...] = jnp.zeros_like(acc_b)

        for c in range(SCAN_CHUNKS):
            for k in range(nl):
                scr[k, pl.ds(c, tb, stride=SCAN_CHUNKS), :] = d_ref[c, :, k * LANES:(k + 1) * LANES]
        yr = y_ref[...]
        yg = _gelu(yr)
        ygb = yg.astype(BF16)
        sg = _sigmoid(jnp.dot(ygb, w_ref[...], preferred_element_type=F32) + b_ref[...])
        dyb_ = jnp.concatenate([scr[k] for k in range(nl)], axis=1)
        dz = dyb_ * yg * sg * (1.0 - sg)
        dzb = dz.astype(BF16)
        dyg = dyb_ * sg + lax.dot_general(dzb, w_ref[...], (((1,), (1,)), ((), ())), preferred_element_type=F32)
        dw_ref[...] += lax.dot_general(ygb, dzb, (((0,), (0,)), ((), ())), preferred_element_type=F32)
        acc_b[...] += jnp.sum(dz.reshape(tr // SUBLANES, SUBLANES, C), axis=0)
        dy_ref[...] = dyg * _gelu_grad(yr)

        @pl.when(i == nsteps - 1)
        def _():
            db_ref[...] = jnp.sum(acc_b[...], axis=0, keepdims=True)

    row = pl.BlockSpec((tr, C), lambda i: (i, 0))
    return pl.pallas_call(
        body, name=name,
        out_shape=(jax.ShapeDtypeStruct((L, C), F32), jax.ShapeDtypeStruct((C, C), F32),
                   jax.ShapeDtypeStruct((1, C), F32)),
        grid=(nsteps,),
        in_specs=[row, pl.BlockSpec((SCAN_CHUNKS, tb, C), lambda i: (0, i, 1)), pl.BlockSpec((C, C), lambda i: (0, 0)),
                  pl.BlockSpec((1, C), lambda i: (0, 0))],
        out_specs=(row, pl.BlockSpec((C, C), lambda i: (0, 0)), pl.BlockSpec((1, C), lambda i: (0, 0))),
        scratch_shapes=[pltpu.VMEM((SUBLANES, C), F32), pltpu.VMEM((nl, tr, LANES), F32)],
        compiler_params=_cparams(("arbitrary",)),
    )(yraw, dmix.reshape(SCAN_CHUNKS, L // SCAN_CHUNKS, 2 * C), wmat, bias.reshape(1, C))


def _pool_counts(L, g):
    t = lax.broadcasted_iota(jnp.int32, (L, LANES), 0).astype(F32) + 1.0
    w = jnp.where(g == 0, 2.0, jnp.where(g == 1, 4.0, jnp.where(g == 2, 8.0, 16.0)))
    return 1.0 / jnp.minimum(t, w)


def _select_window(g, a2, a4, a8, a16):
    return jnp.where(g == 0, a2, jnp.where(g == 1, a4, jnp.where(g == 2, a8, a16)))


def _pooled(z, g):
    a2 = z + _down(z, 1)
    a4 = a2 + _down(a2, 2)
    a8 = a4 + _down(a4, 4)
    a16 = a8 + _down(a8, 8)
    return _select_window(g, a2, a4, a8, a16) * _pool_counts(z.shape[0], g) - z


def _transpose_on_mxu(yb):
    c = yb.shape[1]
    eye = lax.broadcasted_iota(jnp.int32, (c, c), 0) == lax.broadcasted_iota(jnp.int32, (c, c), 1)
    return lax.dot_general(eye.astype(BF16), yb, (((1,), (1,)), ((), ())), preferred_element_type=F32).astype(BF16)


def _pool_fwd(proj3, pool_w, scale, name):
    _, L, C = proj3.shape
    ng = len(POOL_WINDOWS)
    pg = C // ng
    assert pg == LANES

    def body(z_ref, w_ref, s_ref, o_ref, ot_ref):
        g = pl.program_id(0)
        p = _pooled(z_ref[...].astype(F32), g)
        y = jnp.dot(p.astype(BF16), w_ref[...].astype(BF16), preferred_element_type=F32)
        yb = (y * s_ref[...]).astype(BF16)
        o_ref[...] = yb
        ot_ref[...] = _transpose_on_mxu(yb)

    return pl.pallas_call(
        body, name=name, out_shape=(jax.ShapeDtypeStruct((L, 2 * C), BF16), jax.ShapeDtypeStruct((2 * C, L), BF16)),
        grid=(ng,),
        in_specs=[pl.BlockSpec((None, L, pg), lambda g: (0, 0, g)), pl.BlockSpec((None, pg, pg), lambda g: (g, 0, 0)),
                  pl.BlockSpec((1, pg), lambda g: (0, g))],
        out_specs=(pl.BlockSpec((L, pg), lambda g: (0, g)), pl.BlockSpec((pg, L), lambda g: (g, 0))),
        compiler_params=_cparams(("parallel",)),
    )(proj3, pool_w, scale.reshape(1, C))


def _pool_bwd(proj3, dmix, pool_w, scale, name):
    _, L, C = proj3.shape
    ng = len(POOL_WINDOWS)
    pg = C // ng

    def body(z_ref, d_ref, w_ref, s_ref, dz_ref, dw_ref, ds_ref):
        g = pl.program_id(0)
        p = _pooled(z_ref[...].astype(F32), g)
        pb = p.astype(BF16)
        wb = w_ref[...].astype(BF16)
        pre = jnp.dot(pb, wb, preferred_element_type=F32)
        dyc = d_ref[...]
        ds_ref[...] = jnp.sum(dyc * pre, axis=0, keepdims=True)
        dpre = (dyc * s_ref[...]).astype(BF16)
        dw_ref[...] = lax.dot_general(pb, dpre, (((0,), (0,)), ((), ())), preferred_element_type=F32)
        dp = lax.dot_general(dpre, wb, (((1,), (1,)), ((), ())), preferred_element_type=F32)
        v = dp * _pool_counts(L, g)
        a2 = v + _up(v, 1)
        a4 = a2 + _up(a2, 2)
        a8 = a4 + _up(a4, 4)
        a16 = a8 + _up(a8, 8)
        dz_ref[...] = (_select_window(g, a2, a4, a8, a16) - dp).astype(BF16)

    return pl.pallas_call(
        body, name=name,
        out_shape=(jax.ShapeDtypeStruct((L, C), BF16), jax.ShapeDtypeStruct((ng, pg, pg), F32),
                   jax.ShapeDtypeStruct((1, C), F32)),
        grid=(ng,),
        in_specs=[pl.BlockSpec((None, L, pg), lambda g: (0, 0, g)), pl.BlockSpec((L, pg), lambda g: (0, g)),
                  pl.BlockSpec((None, pg, pg), lambda g: (g, 0, 0)), pl.BlockSpec((1, pg), lambda g: (0, g))],
        out_specs=(pl.BlockSpec((L, pg), lambda g: (0, g)), pl.BlockSpec((None, pg, pg), lambda g: (g, 0, 0)),
                   pl.BlockSpec((1, pg), lambda g: (0, g))),
        compiler_params=_cparams(("parallel",)),
    )(proj3, dmix, pool_w, scale.reshape(1, C))


def _tril_w(w_ref, h):
    r = lax.broadcasted_iota(jnp.int32, (CHUNK, CHUNK), 0)
    c = lax.broadcasted_iota(jnp.int32, (CHUNK, CHUNK), 1)
    return jnp.where(r >= c, w_ref[h], 0.0)


def _sgu_fwd(proj3, norm_g, w, b, mixin, mixin_t, name):
    _, L, C = proj3.shape
    nh = w.shape[0]
    dh = C // nh
    assert dh == LANES and w.shape[1] == CHUNK
    tr = _pick(L, prefs=(512, 256, 128))
    bfull = jnp.broadcast_to(b[:, :, None], (nh, CHUNK, dh))

    def body(su_ref, sv_ref, g_ref, w_ref, b_ref, m_in, mt_in, o_ref, ot_ref):
        sv = _gelu(sv_ref[...].astype(F32))
        r = lax.rsqrt(jnp.mean(sv * sv, axis=-1, keepdims=True) + EPS)
        v = (sv * r * g_ref[...]).astype(BF16)
        for h in range(nh):
            wm = _tril_w(w_ref, h).astype(BF16)
            cols = slice(h * dh, (h + 1) * dh)
            for n in range(tr // CHUNK):
                rows = slice(n * CHUNK, (n + 1) * CHUNK)
                mixed = jnp.dot(wm, v[rows, cols], preferred_element_type=F32) + b_ref[h]
                o_ref[rows, cols] = (_gelu(su_ref[rows, cols].astype(F32)) * mixed).astype(BF16)
        ot_ref[...] = _transpose_on_mxu(o_ref[...])

    full = lambda shp: pl.BlockSpec(shp, lambda i: (0,) * len(shp))
    anywhere = pl.BlockSpec(memory_space=pl.ANY)
    return pl.pallas_call(
        body, name=name, out_shape=(jax.ShapeDtypeStruct(mixin.shape, BF16), jax.ShapeDtypeStruct(mixin_t.shape, BF16)),
        grid=(L // tr,),
        in_specs=[pl.BlockSpec((None, tr, C), lambda i: (1, i, 0)), pl.BlockSpec((None, tr, C), lambda i: (2, i, 0)),
                  full((1, C)), full((nh, CHUNK, CHUNK)), full((nh, CHUNK, dh)), anywhere, anywhere],
        out_specs=(pl.BlockSpec((tr, C), lambda i: (i, 1)), pl.BlockSpec((C, tr), lambda i: (1, i))),
        input_output_aliases={5: 0, 6: 1}, compiler_params=_cparams(("parallel",)),
    )(proj3, proj3, norm_g.reshape(1, C), w, bfull, mixin, mixin_t)


def _sgu_bwd(proj3, dmix, dz, norm_g, w, b, name):
    _, L, C = proj3.shape
    nh = w.shape[0]
    dh = C // nh
    tr = _pick(L, prefs=(512, 256, 128))
    nsteps = L // tr
    bfull = jnp.broadcast_to(b[:, :, None], (nh, CHUNK, dh))

    def body(su_ref, sv_ref, d_ref, dz_ref, g_ref, w_ref, b_ref, o_ref, dw_ref, db_ref, dg_ref, dv_ref, acc_g):
        i = pl.program_id(0)
        o_ref[0] = dz_ref[...]

        @pl.when(i == 0)
        def _():
            dw_ref[...] = jnp.zeros_like(dw_ref)
            db_ref[...] = jnp.zeros_like(db_ref)
            acc_g[...] = jnp.zeros_like(acc_g)

        svp = sv_ref[...].astype(F32)
        sv = _gelu(svp)
        r = lax.rsqrt(jnp.mean(sv * sv, axis=-1, keepdims=True) + EPS)
        vh = sv * r
        gv = g_ref[...]
        v = (vh * gv).astype(BF16)
        tri_r = lax.broadcasted_iota(jnp.int32, (CHUNK, CHUNK), 0)
        tri_c = lax.broadcasted_iota(jnp.int32, (CHUNK, CHUNK), 1)
        for h in range(nh):
            wm = _tril_w(w_ref, h).astype(BF16)
            cols = slice(h * dh, (h + 1) * dh)
            dwh = jnp.zeros((CHUNK, CHUNK), F32)
            dbh = jnp.zeros((CHUNK, dh), F32)
            for n in range(tr // CHUNK):
                rows = slice(n * CHUNK, (n + 1) * CHUNK)
                vb = v[rows, cols]
                mixed = jnp.dot(wm, vb, preferred_element_type=F32) + b_ref[h]
                sup = su_ref[rows, cols].astype(F32)
                dyd = d_ref[rows, cols]
                dmx = dyd * _gelu(sup)
                o_ref[1, rows, cols] = (dyd * mixed * _gelu_grad(sup)).astype(BF16)
                dmb = dmx.astype(BF16)
                dwh = dwh + lax.dot_general(dmb, vb, (((1,), (1,)), ((), ())), preferred_element_type=F32)
                dbh = dbh + dmx
                dv_ref[rows, cols] = lax.dot_general(wm, dmb, (((0,), (0,)), ((), ())), preferred_element_type=F32)
            dw_ref[h] += jnp.where(tri_r >= tri_c, dwh, 0.0)
            db_ref[h] += dbh
        dv = dv_ref[...]
        acc_g[...] += jnp.sum((dv * vh).reshape(tr // SUBLANES, SUBLANES, C), axis=0)
        dvg = dv * gv
        dsv = r * (dvg - vh * jnp.mean(dvg * vh, axis=-1, keepdims=True))
        o_ref[2] = (dsv * _gelu_grad(svp)).astype(BF16)

        @pl.when(i == nsteps - 1)
        def _():
            dg_ref[...] = jnp.sum(acc_g[...], axis=0, keepdims=True)

    full = lambda shp: pl.BlockSpec(shp, lambda i: (0,) * len(shp))
    return pl.pallas_call(
        body, name=name,
        out_shape=(jax.ShapeDtypeStruct((3, L, C), BF16), jax.ShapeDtypeStruct((nh, CHUNK, CHUNK), F32),
                   jax.ShapeDtypeStruct((nh, CHUNK, dh), F32), jax.ShapeDtypeStruct((1, C), F32)),
        grid=(nsteps,),
        in_specs=[pl.BlockSpec((None, tr, C), lambda i: (1, i, 0)), pl.BlockSpec((None, tr, C), lambda i: (2, i, 0)),
                  pl.BlockSpec((tr, C), lambda i: (i, 1)), pl.BlockSpec((tr, C), lambda i: (i, 0)), full((1, C)),
                  full((nh, CHUNK, CHUNK)), full((nh, CHUNK, dh))],
        out_specs=(pl.BlockSpec((3, tr, C), lambda i: (0, i, 0)), full((nh, CHUNK, CHUNK)), full((nh, CHUNK, dh)),
                   full((1, C))),
        scratch_shapes=[pltpu.VMEM((tr, C), F32), pltpu.VMEM((SUBLANES, C), F32)],
        compiler_params=_cparams(("arbitrary",)),
    )(proj3, proj3, dmix, dz, norm_g.reshape(1, C), w, bfull)


def _ffn_act_fwd(up3, conv_w, conv_b, name):
    _, L, Fh = up3.shape
    cb = LANES
    w2 = conv_w.reshape(3, 2, Fh).transpose(1, 0, 2)
    b2 = conv_b.reshape(2, 1, Fh)

    def body(u_ref, w_ref, b_ref, o_ref, ot_ref, gv_ref):
        g = _conv3(_taps(u_ref[0].astype(F32)), w_ref[0]) + b_ref[0]
        v = _conv3(_taps(u_ref[1].astype(F32)), w_ref[1]) + b_ref[1]
        gv_ref[0] = g.astype(BF16)
        gv_ref[1] = v.astype(BF16)
        ab = (g * _sigmoid(g) * v).astype(BF16)
        o_ref[...] = ab
        ot_ref[...] = _transpose_on_mxu(ab)

    blk3 = pl.BlockSpec((2, L, cb), lambda j: (0, 0, j))
    return pl.pallas_call(
        body, name=name,
        out_shape=(jax.ShapeDtypeStruct((L, Fh), BF16), jax.ShapeDtypeStruct((Fh, L), BF16),
                   jax.ShapeDtypeStruct((2, L, Fh), BF16)),
        grid=(Fh // cb,),
        in_specs=[blk3, pl.BlockSpec((2, 3, cb), lambda j: (0, 0, j)), pl.BlockSpec((2, 1, cb), lambda j: (0, 0, j))],
        out_specs=(pl.BlockSpec((L, cb), lambda j: (0, j)), pl.BlockSpec((cb, L), lambda j: (j, 0)), blk3),
        compiler_params=_cparams(("parallel",)),
    )(up3, w2, b2)


def _ffn_act_bwd(up3, gv3, da, conv_w, h2t, name):
    _, L, Fh = up3.shape
    D = h2t.shape[0]
    cb = LANES
    nb = Fh // cb
    w2 = conv_w.reshape(3, 2, Fh).transpose(1, 0, 2)

    def body(u_ref, gv_ref, d_ref, w_ref, h_ref, o_ref, dw_ref, db_ref, wg_ref, wv_ref, scr):
        j = pl.program_id(0)

        @pl.when(j == 0)
        def _():
            scr[1] = jnp.zeros((2, L, cb), BF16)

        prev = scr.at[(j + 1) % 2]
        wg_ref[...] = jnp.dot(h_ref[...], prev[0], preferred_element_type=F32).astype(BF16)
        wv_ref[...] = jnp.dot(h_ref[...], prev[1], preferred_element_type=F32).astype(BF16)
        tg, tv = _taps(u_ref[0].astype(F32)), _taps(u_ref[1].astype(F32))
        wg, wv = w_ref[0], w_ref[1]
        g = gv_ref[0].astype(F32)
        v = gv_ref[1].astype(F32)
        sg = _sigmoid(g)
        dav = d_ref[...].astype(F32)
        dg = dav * v * (sg * (1.0 + g * (1.0 - sg)))
        dv = dav * (g * sg)
        dug = _conv3_t(dg, wg).astype(BF16)
        duv = _conv3_t(dv, wv).astype(BF16)
        o_ref[0] = dug
        o_ref[1] = duv
        cur = scr.at[j % 2]
        cur[0] = dug
        cur[1] = duv
        for tap, (dwg, dwv) in enumerate(zip(_conv3_dw(dg, tg), _conv3_dw(dv, tv))):
            dw_ref[0, tap:tap + 1, :] = dwg
            dw_ref[1, tap:tap + 1, :] = dwv
        db_ref[0] = jnp.sum(dg, axis=0, keepdims=True)
        db_ref[1] = jnp.sum(dv, axis=0, keepdims=True)

    here = lambda j: jnp.minimum(j, nb - 1)
    before = lambda j: jnp.maximum(j - 1, 0)
    blk3 = pl.BlockSpec((2, L, cb), lambda j: (0, 0, here(j)))
    dup, dw2, db2, dwg, dwv = pl.pallas_call(
        body, name=name,
        out_shape=(jax.ShapeDtypeStruct((2, L, Fh), BF16), jax.ShapeDtypeStruct((2, 3, Fh), F32),
                   jax.ShapeDtypeStruct((2, 1, Fh), F32), jax.ShapeDtypeStruct((D, Fh), BF16),
                   jax.ShapeDtypeStruct((D, Fh), BF16)),
        grid=(nb + 1,),
        in_specs=[blk3, blk3, pl.BlockSpec((L, cb), lambda j: (0, here(j))),
                  pl.BlockSpec((2, 3, cb), lambda j: (0, 0, here(j))), pl.BlockSpec((D, L), lambda j: (0, 0))],
        out_specs=(blk3, pl.BlockSpec((2, 3, cb), lambda j: (0, 0, here(j))),
                   pl.BlockSpec((2, 1, cb), lambda j: (0, 0, here(j))),
                   pl.BlockSpec((D, cb), lambda j: (0, before(j))), pl.BlockSpec((D, cb), lambda j: (0, before(j)))),
        scratch_shapes=[pltpu.VMEM((2, 2, L, cb), BF16)],
        compiler_params=_cparams(("arbitrary",), VMEM_LIMIT_S5),
    )(up3, gv3, da, w2, h2t)
    return dup, dw2.transpose(1, 0, 2).reshape(3, 2 * Fh), db2.reshape(2 * Fh), jnp.concatenate([dwg, dwv], axis=1)


def _local_step(x, tgt, w, layer_weights, on_layer_grads):
    L, D = x.shape
    depth = w['norm_mix_g'].shape[0]
    saved = []
    for i in range(depth):
        j = i // 2
        wb = dict(layer_weights(2 * i, x))
        s = {'x': x, 'wb': wb}
        if i % 2 == 0:
            proj4, s['hT'] = _norm_mm(x, w['norm_mix_g'][i], wb['even_w_in'], BF16, "even_in_fwd", ok=('seg', 4))
            s['proj'] = proj4
            mixin = _sconv_fwd(proj4, w['even_conv_w'][j], "sconv_fwd")
            prm = (w['ssm_log_step'][j], w['ssm_a_re'][j], w['ssm_a_im'][j], w['ssm_b_re'][j], w['ssm_b_im'][j],
                   w['ssm_c_re'][j], w['ssm_c_im'][j])
            (lr, li, bmat, cmat), prep_vjp = jax.vjp(_s5_prep, *prm)
            yraw, s_re, s_im = _s5_fwd(proj4, lr, li, bmat, cmat, w['ssm_d'][j], "s5_fwd")
            mixin = _glu_fwd(yraw, wb['ssm_glu_w'], w['ssm_glu_b'][j], mixin, "glu_fwd")
            s.update(yraw=yraw, s_re=s_re, s_im=s_im, s5=(lr, li, bmat, cmat), prep_vjp=prep_vjp)
            s['mixinT'] = mixin.T
            x = _mm(mixin, wb['even_w_out'], 'nn', F32, "even_out_fwd", res=x)
        else:
            proj3, s['hT'] = _norm_mm(x, w['norm_mix_g'][i], wb['odd_w_in'], BF16, "odd_in_fwd", ok=('seg', 3))
            s['proj'] = proj3
            mixin, mixin_t = _pool_fwd(proj3, w['pool_w'][j], w['pool_scale'][j], "pool_fwd")
            mixin, s['mixinT'] = _sgu_fwd(proj3, w['sgu_norm_g'][j], w['sgu_w'][j], w['sgu_b'][j], mixin, mixin_t,
                                          "sgu_fwd")
            x = _mm(mixin, wb['odd_w_out'], 'nn', F32, "odd_out_fwd", res=x)
        s['x1'] = x
        wb.update(layer_weights(2 * i + 1, x))
        up3, h2t = _norm_mm(x, w['norm_ffn_g'][i], wb['ffn_w_up'], BF16, "ffn_up_fwd", ok=('seg', 2))
        a, at, gv3 = _ffn_act_fwd(up3, w['ffn_conv_w'][i], w['ffn_conv_b'][i], "ffn_act_fwd")
        x = _mm(a, wb['ffn_w_down'], 'nn', F32, "ffn_down_fwd", res=x)
        s.update(h2T=h2t, up3=up3, aT=at, gv3=gv3)
        saved.append(s)

    loss8, dx, dxb, dg_final = _loss_head(x, w['norm_final_g'], tgt)
    gs = {n: [None] * w[n].shape[0] for n in SMALL if n != 'norm_final_g'}
    gs['norm_final_g'] = dg_final.reshape(D)

    dep = None
    for i in reversed(range(depth)):
        j = i // 2
        s = saved[i]
        wb = s['wb']
        gb = {}
        da = _mm(dxb, wb['ffn_w_down'], 'nt', BF16, "ffn_down_dgrad", dep=dep)
        gb['ffn_w_down'] = _mm(s['aT'], dxb, 'nn', BF16, "ffn_down_wgrad")
        dup3, dcw, dcb, gb['ffn_w_up'] = _ffn_act_bwd(s['up3'], s['gv3'], da, w['ffn_conv_w'][i], s['h2T'],
                                                      "ffn_act_bwd")
        gs['ffn_conv_w'][i], gs['ffn_conv_b'][i] = dcw, dcb
        dep = on_layer_grads(2 * i + 1, gb)
        dx, dxb, dg = _mm_norm_bwd(dup3, wb['ffn_w_up'], s['x1'], w['norm_ffn_g'][i], dx, "ffn_up_dgrad",
                              ak=('seg', 2), dep=dep)
        gs['norm_ffn_g'][i] = dg.reshape(D)
        gb = {}
        if i % 2 == 0:
            dmix = _mm(dxb, wb['even_w_out'], 'nt', F32, "even_out_dgrad")
            gb['even_w_out'] = _mm(s['mixinT'], dxb, 'nn', BF16, "even_out_wgrad")
            dproj, dcw = _sconv_bwd(s['proj'], dmix, w['even_conv_w'][j], "sconv_bwd")
            gs['even_conv_w'][j] = dcw
            dyraw, dglu_w, dglu_b = _glu_bwd(s['yraw'], dmix, wb['ssm_glu_w'], w['ssm_glu_b'][j], "glu_bwd")
            gb['ssm_glu_w'] = dglu_w.astype(BF16)
            gs['ssm_glu_b'][j] = dglu_b.reshape(-1)
            lr, li, bmat, cmat = s['s5']
            dproj, dbm, dcm, dlam, dd = _s5_bwd(dyraw, s['proj'], dproj, s['s_re'], s['s_im'], lr, li, bmat, cmat,
                                               w['ssm_d'][j], "s5_bwd")
            gs['ssm_d'][j] = dd.reshape(-1)
            dcm = jnp.swapaxes(dcm, 1, 2)
            dprm = s['prep_vjp']((dlam[:, 0:1, :], dlam[:, 1:2, :], dbm, dcm))
            for n, gval in zip(('ssm_log_step', 'ssm_a_re', 'ssm_a_im', 'ssm_b_re', 'ssm_b_im', 'ssm_c_re',
                                'ssm_c_im'), dprm):
                gs[n][j] = gval
            gb['even_w_in'] = _mm(s['hT'], dproj, 'nn', BF16, "even_in_wgrad", bk=('seg', 4))
            w_in, in_kind, in_name = wb['even_w_in'], ('seg', 4), "even_in_dgrad"
        else:
            dmix = _mm(dxb, wb['odd_w_out'], 'nt', F32, "odd_out_dgrad")
            gb['odd_w_out'] = _mm(s['mixinT'], dxb, 'nn', BF16, "odd_out_wgrad")
            dz, dpw, dps = _pool_bwd(s['proj'], dmix, w['pool_w'][j], w['pool_scale'][j], "pool_bwd")
            gs['pool_w'][j], gs['pool_scale'][j] = dpw, dps.reshape(-1)
            dproj, dsw, dsb, dsg = _sgu_bwd(s['proj'], dmix, dz, w['sgu_norm_g'][j], w['sgu_w'][j], w['sgu_b'][j],
                                            "sgu_bwd")
            gs['sgu_w'][j], gs['sgu_b'][j], gs['sgu_norm_g'][j] = dsw, jnp.sum(dsb, axis=-1), dsg.reshape(-1)
            gb['odd_w_in'] = _mm(s['hT'], dproj, 'nn', BF16, "odd_in_wgrad", bk=('seg', 3))
            w_in, in_kind, in_name = wb['odd_w_in'], ('seg', 3), "odd_in_dgrad"
        dep = on_layer_grads(2 * i, gb)
        dx, dxb, dg = _mm_norm_bwd(dproj, w_in, s['x'], w['norm_mix_g'][i], dx, in_name, ak=in_kind, dep=dep)
        gs['norm_mix_g'][i] = dg.reshape(D)

    gsmall = {n: (v if n == 'norm_final_g' else jnp.stack(v)) for n, v in gs.items()}
    return loss8[0, 0], dx, gsmall


_HBM = pl.BlockSpec(memory_space=pltpu.HBM)
_CHIP_FLIPS = ((0, 0), (1, 0), (0, 1), (1, 1))


def _coords():
    return lax.axis_index("x"), lax.axis_index("y"), lax.axis_index("c")


def _flip(v, f):
    return 1 - v if f else v


def _shard_of(ref, axis, s, width):
    start = pl.multiple_of(s * width, LANES if axis == ref.ndim - 1 else 16) if width % 16 == 0 else s * width
    idx = [slice(None)] * ref.ndim
    idx[axis] = pl.ds(start, width)
    return ref.at[tuple(idx)]


_SEM = pl.BlockSpec(memory_space=pltpu.SEMAPHORE)
_ANY = pl.BlockSpec(memory_space=pl.ANY)
_DATAFLOW = pltpu.SideEffectType.DATAFLOW_SIDE_EFFECTING


def _in_hbm(a):
    return pltpu.with_memory_space_constraint(a, pltpu.HBM)


def _model_layer(name, l):
    if name.startswith('ffn'):
        return l
    return 2 * l + 1 if name.startswith('odd') else 2 * l


def _place_quarter(shard, l, axis, chip, dtype, dep=None):
    _, r, c = shard.shape
    tr = _pick(r, prefs=(512, 256, 128, 64, 32, 16))
    nrb = r // tr

    def body(chip_ref, i_ref, *rest):
        rest[-1][...] = i_ref[...].astype(dtype)

    if axis == 1:
        out_shape, o_map = (r, c * N_CHIPS), (lambda i, s: (i, s[0]))
    else:
        out_shape, o_map = (r * N_CHIPS, c), (lambda i, s: (s[0] * nrb + i, 0))
    in_specs = [pl.BlockSpec((None, tr, c), lambda i, s: (l, i, 0))]
    args = [chip, shard]
    if dep is not None:
        in_specs.append(pl.BlockSpec(memory_space=pl.ANY))
        args.append(dep)
    return pl.pallas_call(
        body, name="place_quarter", out_shape=jax.ShapeDtypeStruct(out_shape, dtype),
        grid_spec=pltpu.PrefetchScalarGridSpec(
            num_scalar_prefetch=1, grid=(nrb,), in_specs=in_specs, out_specs=pl.BlockSpec((tr, c), o_map)),
        compiler_params=_cparams(("parallel",)),
    )(*args)


def _gather_copies(land_refs, send_sem, recv_sem, axes, landing_chip_of, first=0):
    x, y, c = _coords()
    out = []
    for j, land in enumerate(land_refs):
        width = land.shape[axes[j]] // N_CHIPS
        for f in (1, 2, 3):
            fx, fy = _CHIP_FLIPS[f]
            px, py = _flip(x, fx), _flip(y, fy)
            lx, ly = landing_chip_of(px, py)
            out.append(pltpu.make_async_remote_copy(
                src_ref=_shard_of(land, axes[j], 2 * x + y, width), dst_ref=_shard_of(land, axes[j], 2 * lx + ly, width),
                send_sem=send_sem.at[3 * (first + j) + f - 1], recv_sem=recv_sem.at[3 * (first + j) + f - 1],
                device_id=(px, py, c), device_id_type=MESH))
    return out


def _gather_start(tag, lands, axes, dep=None):
    n = len(lands)

    def body(*refs):
        land_refs, send_sem, recv_sem = refs[:n], refs[-3], refs[-2]
        x, y, _ = _coords()
        for cp in _gather_copies(land_refs, send_sem, recv_sem, axes, lambda px, py: (x, y)):
            cp.start()
        refs[-1][...] = jnp.zeros_like(refs[-1])

    thru = [pltpu.HBM(a.shape, a.dtype) for a in lands]
    outs = pl.pallas_call(
        body, name=f"gather_start_{tag}",
        out_shape=tuple(thru + [pltpu.SemaphoreType.DMA((3 * n,)), pltpu.SemaphoreType.DMA((3 * n,)),
                                jax.ShapeDtypeStruct((SUBLANES, LANES), F32)]),
        in_specs=[_HBM] * n + ([_ANY] if dep is not None else []),
        out_specs=tuple([_HBM] * n + [_SEM, _SEM, pl.BlockSpec(memory_space=pltpu.VMEM)]),
        input_output_aliases={i: i for i in range(n)},
        compiler_params=pltpu.CompilerParams(has_side_effects=_DATAFLOW),
    )(*[_in_hbm(a) for a in lands], *([dep] if dep is not None else []))
    return list(outs[:n]), outs[n], outs[n + 1], outs[n + 2]


def _gather_wait(tag, lands, send_sem, recv_sem, axes, after, first=0):
    n = len(lands)

    def body(*refs):
        for cp in _gather_copies(refs[:n], refs[n], refs[n + 1], axes, lambda px, py: (px, py), first):
            cp.wait_send()
            cp.wait_recv()

    outs = pl.pallas_call(
        body, name=f"gather_wait_{tag}", out_shape=tuple(pltpu.HBM(a.shape, a.dtype) for a in lands),
        in_specs=[_HBM] * n + [_SEM, _SEM, _ANY], out_specs=tuple([_HBM] * n),
        input_output_aliases={i: i for i in range(n)},
        compiler_params=pltpu.CompilerParams(has_side_effects=_DATAFLOW),
    )(*lands, send_sem, recv_sem, after)
    return list(outs)


N_SLOTS = N_DEV - 1


def _scatter_sends(grad_refs, land_refs, send_sem, recv_sem, meta):
    x, y, c = _coords()
    out = []
    for j, (axis, owner, q, width) in enumerate(meta):
        other = c if owner == 0 else 1 - c
        for f, (fx, fy) in enumerate(_CHIP_FLIPS):
            px, py = _flip(x, fx), _flip(y, fy)
            slot = f + 4 * other - 1
            out.append((other if f == 0 else None, pltpu.make_async_remote_copy(
                src_ref=_shard_of(grad_refs[j], axis, 2 * px + py, width), dst_ref=land_refs[j].at[q, slot],
                send_sem=send_sem.at[4 * j + f], recv_sem=recv_sem.at[N_SLOTS * j + slot],
                device_id=(px, py, owner), device_id_type=MESH)))
    return out


def _scatter_start(layer, grads, lands, meta):
    n = len(grads)
    uniq = []
    for a in lands:
        if not any(a is u for u in uniq):
            uniq.append(a)
    which = [next(k for k, u in enumerate(uniq) if u is a) for a in lands]
    nu = len(uniq)

    def body(*refs):
        grad_refs, land_u = refs[:n], refs[n:n + nu]
        send_sem, recv_sem = refs[n + nu], refs[n + nu + 1]
        for other, cp in _scatter_sends(grad_refs, [land_u[k] for k in which], send_sem, recv_sem, meta):
            if other is None:
                cp.start()
            else:
                pl.when(other == 1)(cp.start)
        refs[-1][...] = jnp.zeros_like(refs[-1])

    thru = [pltpu.HBM(a.shape, a.dtype) for a in list(grads) + uniq]
    outs = pl.pallas_call(
        body, name=f"scatter_start_{layer}",
        out_shape=tuple([pltpu.SemaphoreType.DMA((4 * n,)), pltpu.SemaphoreType.DMA((N_SLOTS * n,))] + thru
                        + [jax.ShapeDtypeStruct((SUBLANES, LANES), F32)]),
        in_specs=[_HBM] * (n + nu),
        out_specs=tuple([_SEM, _SEM] + [_HBM] * (n + nu) + [pl.BlockSpec(memory_space=pltpu.VMEM)]),
        input_output_aliases={i: 2 + i for i in range(n + nu)},
        compiler_params=pltpu.CompilerParams(has_side_effects=_DATAFLOW),
    )(*[_in_hbm(a) for a in list(grads) + uniq])
    new_lands = [outs[2 + n + k] for k in which]
    return outs[0], outs[1], list(outs[2:2 + n]), new_lands, outs[-1]


def _scatter_wait(started, lands):
    nl = len(lands)
    flat_grads = [g for s in started for g in s[2]]
    ng, ns = len(flat_grads), len(started)

    def body(*refs):
        land_refs = refs[:nl]
        grad_refs = refs[nl:nl + ng]
        sem_refs = refs[nl + ng:nl + ng + 2 * ns]
        _, _, c = _coords()
        off = 0
        for k, (_, _, grads, idx, meta) in enumerate(started):
            send_sem, recv_sem = sem_refs[2 * k], sem_refs[2 * k + 1]
            lr = [land_refs[i] for i in idx]
            for other, cp in _scatter_sends(grad_refs[off:off + len(grads)], lr, send_sem, recv_sem, meta):
                if other is None:
                    cp.wait_send()
                else:
                    pl.when(other == 1)(cp.wait_send)
            for j, (axis, owner, q, width) in enumerate(meta):
                mine = (c if owner == 0 else 1 - c) == 0

                @pl.when(mine)
                def _():
                    for slot in range(N_SLOTS):
                        land = lr[j].at[q, slot]
                        pltpu.make_async_remote_copy(
                            src_ref=land, dst_ref=land, send_sem=send_sem.at[0], recv_sem=recv_sem.at[N_SLOTS * j + slot],
                            device_id=_coords(), device_id_type=MESH).wait_recv()
            off += len(grads)

    args = list(lands) + flat_grads
    thru = [pltpu.HBM(a.shape, a.dtype) for a in args]
    sems = [s for st in started for s in st[:2]]
    outs = pl.pallas_call(
        body, name="scatter_wait", out_shape=tuple(thru), in_specs=[_HBM] * (nl + ng) + [_SEM] * (2 * ns),
        out_specs=tuple([_HBM] * (nl + ng)), input_output_aliases={i: i for i in range(nl + ng)},
        compiler_params=pltpu.CompilerParams(has_side_effects=_DATAFLOW),
    )(*args, *sems)
    return list(outs[:nl]), list(outs[nl:])


def _sum_and_share(recv, layer_grads, axis, chip, name, dep=None):
    n, ns, r, c = recv.shape
    tr = _pick(r, prefs=(256, 128, 64, 32, 16))
    nr = r // tr
    nsteps = n * nr
    nlay = len(layer_grads)
    own_map = (lambda h, i, s: (i, s[0])) if axis == 1 else (lambda h, i, s: (s[0] * nr + i, 0))

    def body(chip_ref, i_ref, *rest):
        g_refs = rest[:nlay]
        o_ref, buf, loc_sems, send_sems, recv_sems = rest[nlay + (dep is not None):]
        h, i = pl.program_id(0), pl.program_id(1)
        step = h * nr + i
        slot = step % 2
        x, y, core = _coords()
        layer = core * n + h
        own = g_refs[0][...]
        for l in range(1, nlay):
            own = jnp.where(layer == l, g_refs[l][...], own)

        def copies(sl):
            dst = o_ref.at[core * n + h, pl.ds(pl.multiple_of(i * tr, tr), tr), :]
            loc = pltpu.make_async_copy(buf.at[sl], dst, loc_sems.at[sl])
            rem = pltpu.make_async_remote_copy(
                src_ref=buf.at[sl], dst_ref=dst, send_sem=send_sems.at[sl], recv_sem=recv_sems.at[step],
                device_id=(x, y, 1 - core), device_id_type=MESH)
            return loc, rem

        def drain(sl):
            loc, rem = copies(sl)
            loc.wait()
            rem.wait_send()

        pl.when(step >= 2)(lambda: drain(slot))
        acc = own.astype(F32)
        for s in range(ns):
            acc = acc + i_ref[s].astype(F32)
        buf[slot] = acc
        loc, rem = copies(slot)
        loc.start()
        rem.start()

        @pl.when(step == nsteps - 1)
        def _():
            drain(slot)
            if nsteps > 1:
                drain(1 - slot)
            for hh in range(n):
                for ii in range(nr):
                    land = o_ref.at[(1 - core) * n + hh, pl.ds(ii * tr, tr), :]
                    pltpu.make_async_remote_copy(
                        src_ref=buf.at[0], dst_ref=land, send_sem=send_sems.at[0], recv_sem=recv_sems.at[hh * nr + ii],
                        device_id=(x, y, 1 - core), device_id_type=MESH).wait_recv()

    return pl.pallas_call(
        body, name=name, out_shape=jax.ShapeDtypeStruct((2 * n, r, c), F32),
        grid_spec=pltpu.PrefetchScalarGridSpec(
            num_scalar_prefetch=1, grid=(n, nr),
            in_specs=[pl.BlockSpec((None, ns, tr, c), lambda h, i, s: (h, 0, i, 0))]
            + [pl.BlockSpec((tr, c), own_map)] * nlay + ([pl.BlockSpec(memory_space=pl.ANY)] if dep is not None else []),
            out_specs=_HBM,
            scratch_shapes=[pltpu.VMEM((2, tr, c), F32), pltpu.SemaphoreType.DMA((2,)),
                            pltpu.SemaphoreType.DMA((2,)), pltpu.SemaphoreType.DMA((nsteps,))]),
        compiler_params=_cparams(("arbitrary", "arbitrary")),
    )(chip, recv, *layer_grads, *([dep] if dep is not None else []))


def _adamw_update(w_ref, g_ref, m_ref, v_ref, d_ref, mo_ref, vo_ref):
    bc1 = 1.0 - ADAM_B1 ** ADAM_STEP
    bc2 = 1.0 - ADAM_B2 ** ADAM_STEP
    gv = g_ref[...]
    mn = ADAM_B1 * m_ref[...] + (1.0 - ADAM_B1) * gv
    vn = ADAM_B2 * v_ref[...] + (1.0 - ADAM_B2) * (gv * gv)
    d_ref[...] = -ADAM_LR * ((mn / bc1) / (jnp.sqrt(vn / bc2) + ADAM_EPS) + ADAM_WD * w_ref[...])
    mo_ref[...] = mn
    vo_ref[...] = vn


def _adamw(w, g, m, v, name):
    def body(*refs):
        _adamw_update(*refs)

    tr = _pick(w.shape[0], prefs=(256, 128, 64, 32, 16, 8))
    blk = pl.BlockSpec((tr, w.shape[1]), lambda i: (i, 0))
    sds = jax.ShapeDtypeStruct(w.shape, F32)
    return pl.pallas_call(
        body, name=name, out_shape=(sds, sds, sds), grid=(w.shape[0] // tr,), in_specs=[blk] * 4,
        out_specs=(blk,) * 3, compiler_params=_cparams(("parallel",)),
    )(w, g, m, v)


def _adamw_many(tensors, name, by_layer=False):
    n = len(tensors)

    def body(*refs):
        for t in range(n):
            _adamw_update(*refs[4 * t:4 * t + 4], *refs[4 * n + 3 * t:4 * n + 3 * t + 3])

    def spec(a):
        nd = a.ndim
        if by_layer:
            return pl.BlockSpec((1,) + a.shape[1:], lambda i: (i,) + (0,) * (nd - 1))
        return pl.BlockSpec(a.shape, lambda i: (0,) * nd)

    steps = tensors[0][0].shape[0] if by_layer else 1
    outs = pl.pallas_call(
        body, name=name, out_shape=tuple(jax.ShapeDtypeStruct(t[0].shape, F32) for t in tensors for _ in range(3)),
        grid=(steps,), in_specs=[spec(a) for t in tensors for a in t],
        out_specs=tuple(spec(t[0]) for t in tensors for _ in range(3)), compiler_params=_cparams(("parallel",)),
    )(*[a for t in tensors for a in t])
    return [tuple(outs[3 * t:3 * t + 3]) for t in range(n)]


_PACK_QUANTUM = 256 * LANES


def _pack(arrs):
    flat = jnp.concatenate([a.reshape(-1).astype(F32) for a in arrs])
    flat = jnp.pad(flat, (0, (-flat.shape[0]) % _PACK_QUANTUM))
    return flat.reshape(-1, LANES)


def _unpack(p, shapes):
    flat = p.reshape(-1)
    out, off = [], 0
    for s in shapes:
        n = int(np.prod(s))
        out.append(flat[off:off + n].reshape(s))
        off += n
    return out


def kernel(*args):
    nw = len(WEIGHTS)
    x, tgt = args[0], args[1 + nw]
    w = dict(zip(WEIGHTS, args[1:1 + nw]))
    m = dict(zip(WEIGHTS, args[2 + nw:2 + 2 * nw]))
    v = dict(zip(WEIGHTS, args[2 + 2 * nw:2 + 3 * nw]))
    _, L, D = x.shape
    chip = 2 * lax.axis_index("x") + lax.axis_index("y")

    big = list(BIG)
    small_sh_shapes = [w[n].shape for n in SMALL_SHARDED]
    nbig = len(big)
    chip1 = chip.reshape(1).astype(jnp.int32)
    axes2 = [BIG[n] - 1 for n in big] + [0]
    shards = [w[n] for n in big] + [_pack([w[n] for n in SMALL_SHARDED])[None]]
    pairs = [(t, l) for t in range(nbig + 1) for l in range(shards[t].shape[0])]
    depth = w['norm_mix_g'].shape[0]
    part_of = lambda t, l: 0 if t == nbig else 2 * _model_layer(big[t], l) + big[t].startswith('ffn')
    flying, token = {}, None
    for tag, gset in enumerate(([0], list(range(1, 2 * depth)))):
        ids = [k for g in gset for k, (t, l) in enumerate(pairs) if part_of(t, l) == g]
        ts = [pairs[k][0] for k in ids]
        placed = [_place_quarter(shards[t], pairs[k][1], axes2[t], chip1, F32 if t == nbig else BF16, token)
                  for k, t in zip(ids, ts)]
        lands, send, recv, token = _gather_start(tag, placed, [axes2[t] for t in ts], token)
        first = 0
        for g in gset:
            n = sum(1 for t, l in pairs if part_of(t, l) == g)
            flying[g] = (ts[first:first + n], lands[first:first + n], send, recv, first)
            first += n

    def wait_group(g, after):
        ts, lands, send, recv, first = flying[g]
        landed = _gather_wait(g, lands, send, recv, [axes2[t] for t in ts], token if after is None else after, first)
        return dict(zip(ts, landed))

    first = wait_group(0, None)
    packed = first.pop(nbig).reshape(N_CHIPS, -1, LANES)
    per_chip = [_unpack(packed[s], small_sh_shapes) for s in range(N_CHIPS)]
    wl = dict(w)
    for k, n in enumerate(SMALL_SHARDED):
        wl[n] = jnp.concatenate([per_chip[s][k] for s in range(N_CHIPS)], axis=-1)

    def layer_weights(i, after):
        got = first if i == 0 else wait_group(i, after)
        return {big[t]: a for t, a in got.items()}

    small_shapes = [(w[n].shape[:-1] + (w[n].shape[-1] * N_CHIPS,)) if n in SMALL_SHARDED else w[n].shape
                    for n in SMALL] + [(1,)]
    n_small = sum(int(np.prod(s)) for s in small_shapes)
    pack_rows = -(-n_small // _PACK_QUANTUM) * _PACK_QUANTUM // LANES
    nlayers = [w[n].shape[0] for n in big] + [2]
    halves = [n // 2 for n in nlayers]
    quarters = [tuple(w[n].shape[1:]) for n in big] + [(pack_rows // 2 // N_CHIPS, LANES)]
    wire = [BF16] * nbig + [F32]
    land_now = [lax.empty((halves[t], N_SLOTS) + quarters[t], wire[t]) for t in range(nbig + 1)]
    gparts = [[None] * n for n in nlayers]
    started = []

    def start_scatter(tag, ts, ls, arrays):
        meta = [(axes2[t], l // halves[t], l % halves[t], quarters[t][axes2[t]]) for t, l in zip(ts, ls)]
        send, recv, thru, new_lands, token = _scatter_start(tag, arrays, [land_now[t] for t in ts], meta)
        for t, ln in zip(ts, new_lands):
            land_now[t] = ln
        started.append((send, recv, thru, ts, meta, ls))
        return token

    held = [[], [], []]

    def on_layer_grads(g, gb):
        ts = [big.index(n) for n in gb]
        for keep, new in zip(held, (ts, [g // 2 if big[t].startswith('ffn') else g // 4 for t in ts],
                                    [gb[big[t]] for t in ts])):
            keep.extend(new)
        if g % 2 == 0:
            return None
        token = start_scatter(g, *[list(k) for k in held])
        for keep in held:
            keep.clear()
        return token

    loss, dx, gsmall = _local_step(x.reshape(L, D), tgt.reshape(L, D), wl, layer_weights, on_layer_grads)
    gpack = _pack([gsmall[n] for n in SMALL] + [loss.reshape(1)])
    start_scatter(2 * depth, held[0] + [nbig, nbig], held[1] + [0, 1],
                  held[2] + [gpack[:pack_rows // 2], gpack[pack_rows // 2:]])
    landed, sent = _scatter_wait([s[:5] for s in started], land_now)
    for (t, l), g in zip([(t, l) for s in started for t, l in zip(s[3], s[5])], sent):
        gparts[t][l] = g
    small_sum = _sum_and_share(landed[nbig], gparts[nbig], 0, chip1, "sum_share_small")
    quarter_rows = small_sum.shape[0] * small_sum.shape[1]
    placed = _place_quarter(small_sum.reshape(1, quarter_rows, LANES), 0, 0, chip1, F32)
    flying_small, send, recv, token = _gather_start("small", [placed], [0])
    gshard = {n: _sum_and_share(landed[t], gparts[t], axes2[t], chip1, "sum_share_" + n, token)
              for t, n in enumerate(big)}
    small_all = _gather_wait("small", flying_small, send, recv, [0], gshard[big[-1]])[0]
    gpack = small_all.reshape(N_CHIPS, 2, quarter_rows // 2, LANES).transpose(1, 0, 2, 3).reshape(pack_rows, LANES)
    gs = dict(zip(SMALL + ['loss'], _unpack(gpack, small_shapes)))
    loss = gs.pop('loss').reshape(())
    for n in SMALL_SHARDED:
        width = w[n].shape[-1]
        gs[n] = lax.dynamic_slice_in_dim(gs[n], chip * width, width, axis=gs[n].ndim - 1)

    grads, delta, new_m, new_v = {}, {}, {}, {}
    for n in big:
        shp = w[n].shape
        flat = lambda a: a.reshape(shp[0] * shp[1], shp[2])
        g = gshard[n]
        grads[n] = g
        d_, m_, v_ = _adamw(flat(w[n]), flat(g), flat(m[n]), flat(v[n]), "adamw_" + n)
        delta[n], new_m[n], new_v[n] = d_.reshape(shp), m_.reshape(shp), v_.reshape(shp)
    sparse = [n for n in SMALL if w[n].ndim == 4 and w[n].shape[-1] < LANES // 2]
    for names, by_layer in ((sparse, True), ([n for n in SMALL if n not in sparse], False)):
        as2d = lambda a: a.reshape(1, -1) if a.ndim == 1 else a
        res = _adamw_many([(as2d(w[n]), as2d(gs[n]), as2d(m[n]), as2d(v[n])) for n in names],
                          "adamw_small_by_layer" if by_layer else "adamw_small", by_layer)
        for n, (d_, m_, v_) in zip(names, res):
            shp = w[n].shape
            grads[n], delta[n], new_m[n], new_v[n] = gs[n], d_.reshape(shp), m_.reshape(shp), v_.reshape(shp)

    return (loss, dx.reshape(1, L, D), *[grads[n] for n in WEIGHTS], *[delta[n] for n in WEIGHTS],
            *[new_m[n] for n in WEIGHTS], *[new_v[n] for n in WEIGHTS])
```

```python
import math

import numpy as np
import jax
import jax.numpy as jnp
from jax import lax
from jax.experimental import pallas as pl
from jax.experimental.pallas import tpu as pltpu

F32 = jnp.float32
BF16 = jnp.bfloat16
MESH = pl.DeviceIdType.MESH

EPS = 1e-6
CHUNK = 128
POOL_WINDOWS = (2, 4, 8, 16)
LANES = 128
SUBLANES = 8
SCAN_CHUNKS = SUBLANES
S5_GROUPS_PER_STEP = 4
MM_TM_CAP, MM_TN_CAP, MM_TK_CAP = 1408, 1408, 2048
MM_TK_WHOLE = 2816
VMEM_LIMIT = 48 * 1024 * 1024
VMEM_LIMIT_S5 = 56 * 1024 * 1024

ADAM_LR, ADAM_B1, ADAM_B2, ADAM_EPS, ADAM_WD, ADAM_STEP = 0.001, 0.9, 0.999, 1e-08, 0.01, 10

WEIGHTS = ['norm_mix_g', 'even_w_in', 'even_conv_w', 'ssm_log_step', 'ssm_a_re', 'ssm_a_im', 'ssm_b_re',
           'ssm_b_im', 'ssm_c_re', 'ssm_c_im', 'ssm_d', 'ssm_glu_w', 'ssm_glu_b', 'even_w_out', 'odd_w_in',
           'pool_w', 'pool_scale', 'sgu_norm_g', 'sgu_w', 'sgu_b', 'odd_w_out', 'norm_ffn_g', 'ffn_w_up',
           'ffn_conv_w', 'ffn_conv_b', 'ffn_w_down', 'norm_final_g']
BIG = {'even_w_in': 2, 'ssm_glu_w': 1, 'even_w_out': 1, 'odd_w_in': 2, 'odd_w_out': 1, 'ffn_w_up': 2,
       'ffn_w_down': 1}
SMALL_SHARDED = ('even_conv_w', 'pool_scale', 'sgu_norm_g', 'ffn_conv_w')
SMALL = [n for n in WEIGHTS if n not in BIG]
N_CHIPS = 4
N_DEV = 8


def _cparams(sem=None, vmem=VMEM_LIMIT):
    kw = dict(vmem_limit_bytes=vmem)
    if sem is not None:
        kw['dimension_semantics'] = sem
    return pltpu.CompilerParams(**kw)


def _pick(n, segs=(), prefs=(1024, 512, 256, 128)):
    for t in prefs:
        if n % t == 0 and all(s % t == 0 for s in segs if s):
            return t
    return n


def _largest_tile(n, segs, cap):
    best = None
    for t in range(LANES, min(n, cap) + 1, LANES):
        if n % t == 0 and all(s % t == 0 for s in segs if s):
            best = t
    return best if best is not None else n


def _ldims(arr, kind):
    if kind is None:
        return arr.shape
    if kind[0] == 'lead':
        return arr.shape[1:]
    return (arr.shape[1], arr.shape[0] * arr.shape[2])


def _segw(arr, kind):
    return arr.shape[2] if (kind is not None and kind[0] == 'seg') else None


def _opspec(arr, kind, br, bc, rfn, cfn):
    if kind is None:
        return pl.BlockSpec((br, bc), lambda i, j, k: (rfn(i, j, k), cfn(i, j, k)))
    if kind[0] == 'lead':
        lead = kind[1]
        return pl.BlockSpec((None, br, bc), lambda i, j, k: (lead, rfn(i, j, k), cfn(i, j, k)))
    per = arr.shape[2] // bc
    return pl.BlockSpec((None, br, bc), lambda i, j, k: (cfn(i, j, k) // per, rfn(i, j, k), cfn(i, j, k) % per))


def _mm(a, b, mode, out_dtype, name, ak=None, bk=None, ok=None, res=None, dep=None):
    ar, ac = _ldims(a, ak)
    br_, bc_ = _ldims(b, bk)
    if mode == 'nn':
        M, K, N = ar, ac, bc_
        assert br_ == K
    else:
        M, K, N = ar, ac, br_
        assert bc_ == K
    sa, sb = _segw(a, ak), _segw(b, bk)
    so = (N // ok[1]) if ok is not None else None
    tm = _largest_tile(M, [], MM_TM_CAP)
    tn = _largest_tile(N, [sb if mode == 'nn' else None, so], MM_TN_CAP)
    ksegs = [sa, sb if mode == 'nt' else None]
    tk = K if (K <= MM_TK_WHOLE and not any(ksegs)) else _largest_tile(K, ksegs, MM_TK_CAP)
    nk = K // tk
    I = lambda i, j, k: i
    J = lambda i, j, k: j
    Kk = lambda i, j, k: k
    a_spec = _opspec(a, ak, tm, tk, I, Kk)
    if mode == 'nn':
        b_spec = _opspec(b, bk, tk, tn, Kk, J)
        dims = (((1,), (0,)), ((), ()))
    else:
        b_spec = _opspec(b, bk, tn, tk, J, Kk)
        dims = (((1,), (1,)), ((), ()))
    if ok is None:
        out_shape = jax.ShapeDtypeStruct((M, N), out_dtype)
        o_spec = pl.BlockSpec((tm, tn), lambda i, j, k: (i, j))
    else:
        out_shape = jax.ShapeDtypeStruct((ok[1], M, N // ok[1]), out_dtype)
        per = (N // ok[1]) // tn
        o_spec = pl.BlockSpec((None, tm, tn), lambda i, j, k: (j // per, i, j % per))
    has_res = res is not None

    def body(*refs):
        a_ref, b_ref = refs[0], refs[1]
        r_ref = refs[2] if has_res else None
        o_ref = refs[n_in]
        prod = lax.dot_general(a_ref[...].astype(BF16), b_ref[...].astype(BF16), dims, preferred_element_type=F32)
        if nk == 1:
            o_ref[...] = (prod + r_ref[...] if has_res else prod).astype(out_dtype)
            return
        acc = refs[-1]
        k = pl.program_id(2)

        @pl.when(k == 0)
        def _():
            acc[...] = prod

        @pl.when(k > 0)
        def _():
            acc[...] += prod

        @pl.when(k == nk - 1)
        def _():
            o = acc[...]
            if has_res:
                o = o + r_ref[...]
            o_ref[...] = o.astype(out_dtype)

    in_specs = [a_spec, b_spec]
    args = [a, b]
    if has_res:
        in_specs.append(pl.BlockSpec((tm, tn), lambda i, j, k: (i, j)))
        args.append(res)
    if dep is not None:
        in_specs.append(pl.BlockSpec(memory_space=pl.ANY))
        args.append(dep)
    n_in = len(args)
    return pl.pallas_call(
        body, name=name, out_shape=out_shape, grid=(M // tm, N // tn, nk), in_specs=in_specs, out_specs=o_spec,
        scratch_shapes=[pltpu.VMEM((tm, tn), F32)] if nk > 1 else [],
        compiler_params=_cparams(("parallel", "parallel", "arbitrary")),
    )(*args)


_G0 = math.sqrt(2.0 / math.pi)
_G1 = 0.044715


def _gelu(x):
    return 0.5 * x * (1.0 + jnp.tanh(_G0 * (x + _G1 * x * x * x)))


def _gelu_grad(x):
    x2 = x * x
    t = jnp.tanh(_G0 * (x + _G1 * x * x2))
    return 0.5 * (1.0 + t) + 0.5 * x * (1.0 - t * t) * (_G0 * (1.0 + 3.0 * _G1 * x2))


def _sigmoid(x):
    return 1.0 / (1.0 + jnp.exp(-x))


def _down(v, k):
    r = pltpu.roll(v, k, axis=0)
    row = lax.broadcasted_iota(jnp.int32, (SUBLANES, v.shape[1]), 0)
    return jnp.concatenate([jnp.where(row >= k, r[:SUBLANES], 0.0), r[SUBLANES:]], axis=0)


def _up(v, k):
    n = v.shape[0]
    r = pltpu.roll(v, n - k, axis=0)
    row = lax.broadcasted_iota(jnp.int32, (SUBLANES, v.shape[1]), 0)
    return jnp.concatenate([r[:n - SUBLANES], jnp.where(row < SUBLANES - k, r[n - SUBLANES:], 0.0)], axis=0)


def _taps(v):
    return _down(v, 2), _down(v, 1), v


def _conv3(taps, w):
    return w[0:1, :] * taps[0] + w[1:2, :] * taps[1] + w[2:3, :] * taps[2]


def _conv3_t(dv, w):
    return w[2:3, :] * dv + w[1:2, :] * _up(dv, 1) + w[0:1, :] * _up(dv, 2)


def _conv3_dw(dv, taps):
    return tuple(jnp.sum(dv * tp, axis=0, keepdims=True) for tp in taps)


def _cmul(ar, ai, br, bi):
    return ar * br - ai * bi, ar * bi + ai * br


def _cpow(lr, li, n):
    rr = ri = None
    br, bi = lr, li
    while n:
        if n & 1:
            rr, ri = (br, bi) if rr is None else _cmul(rr, ri, br, bi)
        n >>= 1
        if n:
            br, bi = _cmul(br, bi, br, bi)
    return rr, ri


NORM_ROWS = 256


def _norm_mm(x, g, b, out_dtype, name, ok=None):
    M, D = x.shape
    N = b.shape[1]
    so = (N // ok[1]) if ok is not None else None
    tm = _largest_tile(M, [], 1024)
    tn = _largest_tile(N, [so], MM_TN_CAP)
    if ok is None:
        out_shape = jax.ShapeDtypeStruct((M, N), out_dtype)
        o_spec = pl.BlockSpec((tm, tn), lambda i, j: (i, j))
    else:
        out_shape = jax.ShapeDtypeStruct((ok[1], M, N // ok[1]), out_dtype)
        per = (N // ok[1]) // tn
        o_spec = pl.BlockSpec((None, tm, tn), lambda i, j: (j // per, i, j % per))

    def body(x_ref, g_ref, b_ref, o_ref, ht_ref, h_scr):
        @pl.when(pl.program_id(1) == 0)
        def _():
            for c in range(tm // NORM_ROWS):
                rows = pl.ds(c * NORM_ROWS, NORM_ROWS)
                xv = x_ref[rows, :]
                h = xv * lax.rsqrt(jnp.mean(xv * xv, axis=-1, keepdims=True) + EPS) * g_ref[...]
                h_scr[rows, :] = h.astype(BF16)
                ht_ref[:, rows] = h.T.astype(BF16)

        o_ref[...] = jnp.dot(h_scr[...], b_ref[...], preferred_element_type=F32).astype(out_dtype)

    return pl.pallas_call(
        body, name=name, out_shape=(out_shape, jax.ShapeDtypeStruct((D, M), BF16)), grid=(M // tm, N // tn),
        in_specs=[pl.BlockSpec((tm, D), lambda i, j: (i, 0)), pl.BlockSpec((1, D), lambda i, j: (0, 0)),
                  pl.BlockSpec((D, tn), lambda i, j: (0, j))],
        out_specs=(o_spec, pl.BlockSpec((D, tm), lambda i, j: (0, i))),
        scratch_shapes=[pltpu.VMEM((tm, D), BF16)], compiler_params=_cparams(("parallel", "arbitrary")),
    )(x, g.reshape(1, D), b)


def _mm_norm_bwd(a, b, x, g, dres, name, ak=None, dep=None):
    M, K = _ldims(a, ak)
    D = b.shape[0]
    assert b.shape[1] == K and x.shape == (M, D)
    sa = _segw(a, ak)
    tm = _largest_tile(M, [], 1024)
    whole_segs = bool(sa) and K <= MM_TK_WHOLE
    tk = K if (K <= MM_TK_WHOLE) else _largest_tile(K, [sa], MM_TK_CAP)
    ni, nk = M // tm, K // tk
    if whole_segs:
        a_spec = pl.BlockSpec((a.shape[0], tm, sa), lambda i, k: (0, i, 0))
    else:
        a3 = _opspec(a, ak, tm, tk, lambda i, j, k: i, lambda i, j, k: k)
        a_spec = pl.BlockSpec(a3.block_shape, lambda i, k: a3.index_map(i, 0, k))
    n_in = 5 + (dep is not None)

    def body(*refs):
        a_ref, b_ref, x_ref, g_ref, r_ref = refs[:5]
        dx_ref, dxb_ref, dg_ref, acc, accg = refs[n_in:]
        i, k = pl.program_id(0), pl.program_id(1)
        av = jnp.concatenate([a_ref[s] for s in range(a.shape[0])], axis=1) if whole_segs else a_ref[...]
        prod = lax.dot_general(av.astype(BF16), b_ref[...], (((1,), (1,)), ((), ())), preferred_element_type=F32)

        @pl.when(k == 0)
        def _():
            acc[...] = prod

        @pl.when(k > 0)
        def _():
            acc[...] += prod

        @pl.when((i == 0) & (k == 0))
        def _():
            accg[...] = jnp.zeros_like(accg)

        @pl.when(k == nk - 1)
        def _():
            for c in range(tm // NORM_ROWS):
                rows = pl.ds(c * NORM_ROWS, NORM_ROWS)
                xv = x_ref[rows, :]
                r = lax.rsqrt(jnp.mean(xv * xv, axis=-1, keepdims=True) + EPS)
                xh = xv * r
                dhv = acc[rows, :]
                accg[...] += jnp.sum((dhv * xh).reshape(NORM_ROWS // SUBLANES, SUBLANES, D), axis=0)
                dxh = dhv * g_ref[...]
                dxv = r_ref[rows, :] + r * (dxh - xh * jnp.mean(dxh * xh, axis=-1, keepdims=True))
                dx_ref[rows, :] = dxv
                dxb_ref[rows, :] = dxv.astype(BF16)

        @pl.when((i == ni - 1) & (k == nk - 1))
        def _():
            dg_ref[...] = jnp.sum(accg[...], axis=0, keepdims=True)

    row = pl.BlockSpec((tm, D), lambda i, k: (i, 0))
    vec = pl.BlockSpec((1, D), lambda i, k: (0, 0))
    in_specs = [a_spec, pl.BlockSpec((D, tk), lambda i, k: (0, k)), row, vec, row]
    args = [a, b, x, g.reshape(1, D), dres]
    if dep is not None:
        in_specs.append(pl.BlockSpec(memory_space=pl.ANY))
        args.append(dep)
    return pl.pallas_call(
        body, name=name,
        out_shape=(jax.ShapeDtypeStruct((M, D), F32), jax.ShapeDtypeStruct((M, D), BF16),
                   jax.ShapeDtypeStruct((1, D), F32)),
        grid=(ni, nk), in_specs=in_specs, out_specs=(row, row, vec),
        scratch_shapes=[pltpu.VMEM((tm, D), F32), pltpu.VMEM((SUBLANES, D), F32)],
        compiler_params=_cparams(("arbitrary", "arbitrary"), VMEM_LIMIT_S5),
    )(*args)


def _loss_head(x, g, tgt):
    L, D = x.shape
    tr = _pick(L, prefs=(512, 256, 128))
    nsteps = L // tr

    def body(x_ref, g_ref, t_ref, loss_ref, dx_ref, dxb_ref, dg_ref, acc_g, acc_l):
        i = pl.program_id(0)

        @pl.when(i == 0)
        def _():
            acc_g[...] = jnp.zeros_like(acc_g)
            acc_l[...] = jnp.zeros_like(acc_l)

        xv = x_ref[...]
        gv = g_ref[...]
        r = lax.rsqrt(jnp.mean(xv * xv, axis=-1, keepdims=True) + EPS)
        xh = xv * r
        e = xh * gv - t_ref[...]
        acc_l[...] += jnp.sum((e * e).reshape(tr // SUBLANES, SUBLANES, D), axis=0)
        dy = e * (1.0 / D)
        acc_g[...] += jnp.sum((dy * xh).reshape(tr // SUBLANES, SUBLANES, D), axis=0)
        dxh = dy * gv
        dxv = r * (dxh - xh * jnp.mean(dxh * xh, axis=-1, keepdims=True))
        dx_ref[...] = dxv
        dxb_ref[...] = dxv.astype(BF16)

        @pl.when(i == nsteps - 1)
        def _():
            dg_ref[...] = jnp.sum(acc_g[...], axis=0, keepdims=True)
            tot = jnp.sum(jnp.sum(acc_l[...], axis=0, keepdims=True), axis=1, keepdims=True) * (0.5 / D)
            loss_ref[...] = jnp.broadcast_to(tot, (SUBLANES, LANES))

    row = pl.BlockSpec((tr, D), lambda i: (i, 0))
    vec = pl.BlockSpec((1, D), lambda i: (0, 0))
    return pl.pallas_call(
        body, name="loss_head",
        out_shape=(jax.ShapeDtypeStruct((SUBLANES, LANES), F32), jax.ShapeDtypeStruct((L, D), F32),
                   jax.ShapeDtypeStruct((L, D), BF16), jax.ShapeDtypeStruct((1, D), F32)),
        grid=(nsteps,), in_specs=[row, vec, row],
        out_specs=(pl.BlockSpec((SUBLANES, LANES), lambda i: (0, 0)), row, row, vec),
        scratch_shapes=[pltpu.VMEM((SUBLANES, D), F32), pltpu.VMEM((SUBLANES, D), F32)],
        compiler_params=_cparams(("arbitrary",)),
    )(x, g.reshape(1, D), tgt)


def _sconv_fwd(proj4, conv_w, name):
    _, L, C = proj4.shape
    cb = LANES

    def body(p_ref, w_ref, o_ref):
        xa, ba, ca = p_ref[0].astype(F32), p_ref[1].astype(F32), p_ref[2].astype(F32)
        o_ref[...] = (ba * _conv3(_taps(ca * xa), w_ref[...])).astype(BF16)

    return pl.pallas_call(
        body, name=name, out_shape=jax.ShapeDtypeStruct((L, 2 * C), BF16), grid=(C // cb,),
        in_specs=[pl.BlockSpec((3, L, cb), lambda j: (0, 0, j)), pl.BlockSpec((3, cb), lambda j: (0, j))],
        out_specs=pl.BlockSpec((L, cb), lambda j: (0, j)), compiler_params=_cparams(("parallel",)),
    )(proj4, conv_w)


def _sconv_bwd(proj4, dmix, conv_w, name):
    _, L, C = proj4.shape
    cb = LANES

    def body(p_ref, d_ref, w_ref, o_ref, dw_ref):
        xa, ba, ca = p_ref[0].astype(F32), p_ref[1].astype(F32), p_ref[2].astype(F32)
        w = w_ref[...]
        dya = d_ref[...]
        tq = _taps(ca * xa)
        cq = _conv3(tq, w)
        dcq = dya * ba
        dq = _conv3_t(dcq, w)
        for tap, dwt in enumerate(_conv3_dw(dcq, tq)):
            dw_ref[tap:tap + 1, :] = dwt
        o_ref[0] = (dq * ca).astype(BF16)
        o_ref[1] = (dya * cq).astype(BF16)
        o_ref[2] = (dq * xa).astype(BF16)

    return pl.pallas_call(
        body, name=name,
        out_shape=(jax.ShapeDtypeStruct((4, L, C), BF16), jax.ShapeDtypeStruct((3, C), F32)), grid=(C // cb,),
        in_specs=[pl.BlockSpec((3, L, cb), lambda j: (0, 0, j)), pl.BlockSpec((L, cb), lambda j: (0, j)),
                  pl.BlockSpec((3, cb), lambda j: (0, j))],
        out_specs=(pl.BlockSpec((3, L, cb), lambda j: (0, 0, j)), pl.BlockSpec((3, cb), lambda j: (0, j))),
        compiler_params=_cparams(("parallel",)),
    )(proj4, dmix, conv_w)


def _s5_prep(log_step, a_re, a_im, b_re, b_im, c_re, c_im):
    G, P = a_re.shape
    H = b_re.shape[-1]
    gs = S5_GROUPS_PER_STEP
    ns = G // gs
    gu = LANES // H
    lam = lax.complex(a_re, a_im)
    step = jnp.exp(log_step)[:, None]
    lam_bar = jnp.exp(lam * step)
    b_bar = ((lam_bar - 1.0) / lam)[..., None] * lax.complex(b_re, b_im)
    lr = jnp.real(lam_bar).reshape(ns, 1, gs * P)
    li = jnp.imag(lam_bar).reshape(ns, 1, gs * P)
    k = np.arange(ns)[:, None, None]
    oh = jnp.asarray((np.arange(gu)[None, :, None] == gs * (k % (gu // gs)) + np.arange(gs)[None, None, :]),
                     F32)
    bre = jnp.einsum('kgl,klph->kghlp', oh, jnp.real(b_bar).reshape(ns, gs, P, H)).reshape(ns, gu * H, gs * P)
    bim = jnp.einsum('kgl,klph->kghlp', oh, jnp.imag(b_bar).reshape(ns, gs, P, H)).reshape(ns, gu * H, gs * P)
    cre = jnp.einsum('kgl,klhp->klpgh', oh, c_re.reshape(ns, gs, H, P)).reshape(ns, gs * P, gu * H)
    cim = jnp.einsum('kgl,klhp->klpgh', oh, c_im.reshape(ns, gs, H, P)).reshape(ns, gs * P, gu * H)
    return lr, li, jnp.concatenate([bre, bim], axis=2), jnp.concatenate([cre, -cim], axis=1)


def _carry_tile(fr, fi, pr, pi, reverse):
    row = lax.broadcasted_iota(jnp.int32, fr.shape, 0)
    cr = jnp.zeros_like(fr)
    ci = jnp.zeros_like(fi)
    sr = jnp.zeros_like(fr[0:1])
    si = jnp.zeros_like(sr)
    order = range(SCAN_CHUNKS - 1, 0, -1) if reverse else range(0, SCAN_CHUNKS - 1)
    for c in order:
        fcr = jnp.sum(jnp.where(row == c, fr, 0.0), axis=0, keepdims=True)
        fci = jnp.sum(jnp.where(row == c, fi, 0.0), axis=0, keepdims=True)
        mr, mi = _cmul(pr, pi, sr, si)
        sr, si = mr + fcr, mi + fci
        nxt = c - 1 if reverse else c + 1
        cr = jnp.where(row == nxt, sr, cr)
        ci = jnp.where(row == nxt, si, ci)
    return cr, ci


def _scan_order_into(dst_ref, src_ref, T):
    for c in range(SCAN_CHUNKS):
        dst_ref[pl.ds(c, T, stride=SCAN_CHUNKS), :] = src_ref[pl.ds(c * T, T), :].astype(F32)


def _s5_fwd(proj4, lr, li, bmat, cmat, d, name):
    _, L, Du = proj4.shape
    ns, _, W2 = bmat.shape
    W = W2 // 2
    T = L // SCAN_CHUNKS
    rb = _pick(L, prefs=(512, 256, 128))
    per = (ns * LANES) // Du

    def body(ut_ref, lr_ref, li_ref, b_ref, c_ref, d_ref, y_ref, sr_ref, si_ref, u_ref):
        k = pl.program_id(0)
        _scan_order_into(u_ref, ut_ref, T)
        for r in range(L // rb):
            rows = pl.ds(r * rb, rb)
            bu = jnp.dot(u_ref[rows, :].astype(BF16), b_ref[...], preferred_element_type=F32)
            sr_ref[rows, :] = bu[:, :W]
            si_ref[rows, :] = bu[:, W:]
        lam_r = jnp.broadcast_to(lr_ref[...], (SUBLANES, W))
        lam_i = jnp.broadcast_to(li_ref[...], (SUBLANES, W))

        def local(t, carry):
            sr, si = carry
            rows = pl.ds(pl.multiple_of(t * SUBLANES, SUBLANES), SUBLANES)
            mr, mi = _cmul(lam_r, lam_i, sr, si)
            sr = mr + sr_ref[rows, :]
            si = mi + si_ref[rows, :]
            sr_ref[rows, :] = sr
            si_ref[rows, :] = si
            return sr, si

        z = jnp.zeros((SUBLANES, W), F32)
        fr, fi = lax.fori_loop(0, T, local, (z, z))
        pr, pi = _cpow(lam_r, lam_i, T)
        cr, ci = _carry_tile(fr, fi, pr[0:1], pi[0:1], reverse=False)

        def fix(t, carry):
            wr, wi = carry
            rows = pl.ds(pl.multiple_of(t * SUBLANES, SUBLANES), SUBLANES)
            ar, ai = _cmul(wr, wi, cr, ci)
            sr_ref[rows, :] += ar
            si_ref[rows, :] += ai
            return _cmul(wr, wi, lam_r, lam_i)

        lax.fori_loop(0, T, fix, (lam_r, lam_i))
        first = (k % per) == 0
        for r in range(L // rb):
            rows = pl.ds(r * rb, rb)
            s = jnp.concatenate([sr_ref[rows, :], si_ref[rows, :]], axis=1).astype(BF16)
            y = jnp.dot(s, c_ref[...], preferred_element_type=F32)

            @pl.when(first)
            def _():
                y_ref[rows, :] = y + d_ref[...] * u_ref[rows, :]

            @pl.when(jnp.logical_not(first))
            def _():
                y_ref[rows, :] += y

    ublk = pl.BlockSpec((L, LANES), lambda k: (0, k // per))
    sblk = pl.BlockSpec((L, W), lambda k: (0, k))
    lam = pl.BlockSpec((None, 1, W), lambda k: (k, 0, 0))
    return pl.pallas_call(
        body, name=name,
        out_shape=(jax.ShapeDtypeStruct((L, Du), F32), jax.ShapeDtypeStruct((L, ns * W), F32),
                   jax.ShapeDtypeStruct((L, ns * W), F32)),
        grid=(ns,),
        in_specs=[pl.BlockSpec((None, L, LANES), lambda k: (3, 0, k // per)), lam, lam,
                  pl.BlockSpec((None, LANES, 2 * W), lambda k: (k, 0, 0)),
                  pl.BlockSpec((None, 2 * W, LANES), lambda k: (k, 0, 0)),
                  pl.BlockSpec((1, LANES), lambda k: (0, k // per))],
        out_specs=(ublk, sblk, sblk), scratch_shapes=[pltpu.VMEM((L, LANES), F32)],
        compiler_params=_cparams(("arbitrary",), VMEM_LIMIT_S5),
    )(proj4, lr, li, bmat.astype(BF16), cmat.astype(BF16), d.reshape(1, Du))


def _s5_bwd(dy, proj4, dproj, s_re, s_im, lr, li, bmat, cmat, d, name):
    _, L, Du = proj4.shape
    ns, _, W2 = bmat.shape
    W = W2 // 2
    T = L // SCAN_CHUNKS
    rb = _pick(L, prefs=(512, 256, 128))
    per = (ns * LANES) // Du
    NT = (((1,), (1,)), ((), ()))
    TN = (((0,), (0,)), ((), ()))

    def body(dy_ref, ut_ref, dp_in, sr_ref, si_ref, lr_ref, li_ref, b_ref, c_ref, d_ref,
             dut_ref, db_ref, dc_ref, dl_ref, dd_ref, gr_ref, gi_ref, u_ref, du_ref):
        k = pl.program_id(0)
        _scan_order_into(u_ref, ut_ref, T)
        for r in range(L // rb):
            rows = pl.ds(r * rb, rb)
            g = lax.dot_general(dy_ref[rows, :].astype(BF16), c_ref[...], NT, preferred_element_type=F32)
            gr_ref[rows, :] = g[:, :W]
            gi_ref[rows, :] = g[:, W:]
        lam_r = jnp.broadcast_to(lr_ref[...], (SUBLANES, W))
        lam_i = -jnp.broadcast_to(li_ref[...], (SUBLANES, W))

        def local(i, carry):
            gr, gi = carry
            rows = pl.ds(pl.multiple_of((T - 1 - i) * SUBLANES, SUBLANES), SUBLANES)
            mr, mi = _cmul(lam_r, lam_i, gr, gi)
            gr = mr + gr_ref[rows, :]
            gi = mi + gi_ref[rows, :]
            gr_ref[rows, :] = gr
            gi_ref[rows, :] = gi
            return gr, gi

        z = jnp.zeros((SUBLANES, W), F32)
        fr, fi = lax.fori_loop(0, T, local, (z, z))
        pr, pi = _cpow(lam_r, lam_i, T)
        cr, ci = _carry_tile(fr, fi, pr[0:1], pi[0:1], reverse=True)

        def true_g(rows, wr, wi):
            ar, ai = _cmul(wr, wi, cr, ci)
            gr = gr_ref[rows, :] + ar
            gi = gi_ref[rows, :] + ai
            gr_ref[rows, :] = gr
            gi_ref[rows, :] = gi
            return gr, gi

        def fix(i, carry):
            wr, wi, ar_, ai_ = carry
            t = T - 1 - i
            rows = pl.ds(pl.multiple_of(t * SUBLANES, SUBLANES), SUBLANES)
            prev = pl.ds(pl.multiple_of((t - 1) * SUBLANES, SUBLANES), SUBLANES)
            gr, gi = true_g(rows, wr, wi)
            qr, qi = sr_ref[prev, :], si_ref[prev, :]
            ar_ = ar_ + gr * qr + gi * qi
            ai_ = ai_ + gi * qr - gr * qi
            wr, wi = _cmul(wr, wi, lam_r, lam_i)
            return wr, wi, ar_, ai_

        wr, wi, acc_r, acc_i = lax.fori_loop(0, T - 1, fix, (lam_r, lam_i, z, z))
        gr, gi = true_g(pl.ds(0, SUBLANES), wr, wi)
        last = pl.ds((T - 1) * SUBLANES, SUBLANES)
        row = lax.broadcasted_iota(jnp.int32, (SUBLANES, W), 0)
        qr = jnp.where(row >= 1, pltpu.roll(sr_ref[last, :], 1, axis=0), 0.0)
        qi = jnp.where(row >= 1, pltpu.roll(si_ref[last, :], 1, axis=0), 0.0)
        acc_r = acc_r + gr * qr + gi * qi
        acc_i = acc_i + gi * qr - gr * qi
        dl_ref[0:1, :] = jnp.sum(acc_r, axis=0, keepdims=True)
        dl_ref[1:2, :] = jnp.sum(acc_i, axis=0, keepdims=True)

        first = (k % per) == 0
        db = jnp.zeros((LANES, 2 * W), F32)
        dc = jnp.zeros((LANES, 2 * W), F32)
        dd = jnp.zeros((1, LANES), F32)
        for r in range(L // rb):
            rows = pl.ds(r * rb, rb)
            gb = jnp.concatenate([gr_ref[rows, :], gi_ref[rows, :]], axis=1).astype(BF16)
            sb = jnp.concatenate([sr_ref[rows, :], si_ref[rows, :]], axis=1).astype(BF16)
            dyv = dy_ref[rows, :]
            uv = u_ref[rows, :]
            du = lax.dot_general(gb, b_ref[...], NT, preferred_element_type=F32)
            db = db + lax.dot_general(uv.astype(BF16), gb, TN, preferred_element_type=F32)
            dc = dc + lax.dot_general(dyv.astype(BF16), sb, TN, preferred_element_type=F32)
            dd = dd + jnp.sum(dyv * uv, axis=0, keepdims=True)

            @pl.when(first)
            def _():
                du_ref[rows, :] = du + d_ref[...] * dyv

            @pl.when(jnp.logical_not(first))
            def _():
                du_ref[rows, :] += du

        db_ref[...] = db
        dc_ref[...] = dc

        @pl.when(first)
        def _():
            dd_ref[...] = dd

        @pl.when((k % per) == per - 1)
        def _():
            for c in range(SCAN_CHUNKS):
                dut_ref[pl.ds(c * T, T), :] = du_ref[pl.ds(c, T, stride=SCAN_CHUNKS), :].astype(BF16)

    ublk = pl.BlockSpec((L, LANES), lambda k: (0, k // per))
    uslab = pl.BlockSpec((None, L, LANES), lambda k: (3, 0, k // per))
    sblk = pl.BlockSpec((L, W), lambda k: (0, k))
    lam = pl.BlockSpec((None, 1, W), lambda k: (k, 0, 0))
    vec = pl.BlockSpec((1, LANES), lambda k: (0, k // per))
    mat = pl.BlockSpec((None, LANES, 2 * W), lambda k: (k, 0, 0))
    return pl.pallas_call(
        body, name=name,
        out_shape=(jax.ShapeDtypeStruct(dproj.shape, dproj.dtype), jax.ShapeDtypeStruct((ns, LANES, 2 * W), F32),
                   jax.ShapeDtypeStruct((ns, LANES, 2 * W), F32), jax.ShapeDtypeStruct((ns, 2, W), F32),
                   jax.ShapeDtypeStruct((1, Du), F32)),
        grid=(ns,),
        in_specs=[ublk, uslab, pl.BlockSpec(memory_space=pl.ANY), sblk, sblk, lam, lam, mat,
                  pl.BlockSpec((None, 2 * W, LANES), lambda k: (k, 0, 0)), vec],
        out_specs=(uslab, mat, mat, pl.BlockSpec((None, 2, W), lambda k: (k, 0, 0)), vec),
        scratch_shapes=[pltpu.VMEM((L, W), F32), pltpu.VMEM((L, W), F32), pltpu.VMEM((L, LANES), F32),
                        pltpu.VMEM((L, LANES), F32)],
        input_output_aliases={2: 0}, compiler_params=_cparams(("arbitrary",), VMEM_LIMIT_S5),
    )(dy, proj4, dproj, s_re, s_im, lr, li, bmat.astype(BF16), cmat.astype(BF16), d.reshape(1, Du))


def _glu_fwd(yraw, wmat, bias, mixin, name):
    L, C = yraw.shape
    tr = _pick(L, prefs=(512, 256, 128))
    tb = tr // SCAN_CHUNKS
    nl = C // LANES

    def body(y_ref, w_ref, b_ref, m_in, o_ref, scr):
        yg = _gelu(y_ref[...])
        zz = jnp.dot(yg.astype(BF16), w_ref[...], preferred_element_type=F32) + b_ref[...]
        yb = yg * _sigmoid(zz)
        for k in range(nl):
            scr[k] = yb[:, k * LANES:(k + 1) * LANES]
        for c in range(SCAN_CHUNKS):
            for k in range(nl):
                o_ref[c, :, k * LANES:(k + 1) * LANES] = scr[k, pl.ds(c, tb, stride=SCAN_CHUNKS), :].astype(BF16)

    out = pl.pallas_call(
        body, name=name, out_shape=jax.ShapeDtypeStruct((SCAN_CHUNKS, L // SCAN_CHUNKS, 2 * C), BF16),
        grid=(L // tr,),
        in_specs=[pl.BlockSpec((tr, C), lambda i: (i, 0)), pl.BlockSpec((C, C), lambda i: (0, 0)),
                  pl.BlockSpec((1, C), lambda i: (0, 0)), pl.BlockSpec(memory_space=pl.ANY)],
        out_specs=pl.BlockSpec((SCAN_CHUNKS, tb, C), lambda i: (0, i, 1)),
        scratch_shapes=[pltpu.VMEM((nl, tr, LANES), F32)], input_output_aliases={3: 0},
        compiler_params=_cparams(("parallel",)),
    )(yraw, wmat, bias.reshape(1, C), mixin.reshape(SCAN_CHUNKS, L // SCAN_CHUNKS, 2 * C))
    return out.reshape(L, 2 * C)


def _glu_bwd(yraw, dmix, wmat, bias, name):
    L, C = yraw.shape
    tr = _pick(L, prefs=(512, 256, 128))
    nsteps = L // tr
    tb = tr // SCAN_CHUNKS
    nl = C // LANES

    def body(y_ref, d_ref, w_ref, b_ref, dy_ref, dw_ref, db_ref, acc_b, scr):
        i = pl.program_id(0)

        @pl.when(i == 0)
        def _():
            dw_ref[...] = jnp.zeros_like(dw_ref)
            acc_b[...] = jnp.zeros_like(acc_b)

        for c in range(SCAN_CHUNKS):
            for k in range(nl):
                scr[k, pl.ds(c, tb, stride=SCAN_CHUNKS), :] = d_ref[c, :, k * LANES:(k + 1) * LANES]
        yr = y_ref[...]
        yg = _gelu(yr)
        ygb = yg.astype(BF16)
        sg = _sigmoid(jnp.dot(ygb, w_ref[...], preferred_element_type=F32) + b_ref[...])
        dyb_ = jnp.concatenate([scr[k] for k in range(nl)], axis=1)
        dz = dyb_ * yg * sg * (1.0 - sg)
        dzb = dz.astype(BF16)
        dyg = dyb_ * sg + lax.dot_general(dzb, w_ref[...], (((1,), (1,)), ((), ())), preferred_element_type=F32)
        dw_ref[...] += lax.dot_general(ygb, dzb, (((0,), (0,)), ((), ())), preferred_element_type=F32)
        acc_b[...] += jnp.sum(dz.reshape(tr // SUBLANES, SUBLANES, C), axis=0)
        dy_ref[...] = dyg * _gelu_grad(yr)

        @pl.when(i == nsteps - 1)
        def _():
            db_ref[...] = jnp.sum(acc_b[...], axis=0, keepdims=True)

    row = pl.BlockSpec((tr, C), lambda i: (i, 0))
    return pl.pallas_call(
        body, name=name,
        out_shape=(jax.ShapeDtypeStruct((L, C), F32), jax.ShapeDtypeStruct((C, C), F32),
                   jax.ShapeDtypeStruct((1, C), F32)),
        grid=(nsteps,),
        in_specs=[row, pl.BlockSpec((SCAN_CHUNKS, tb, C), lambda i: (0, i, 1)), pl.BlockSpec((C, C), lambda i: (0, 0)),
                  pl.BlockSpec((1, C), lambda i: (0, 0))],
        out_specs=(row, pl.BlockSpec((C, C), lambda i: (0, 0)), pl.BlockSpec((1, C), lambda i: (0, 0))),
        scratch_shapes=[pltpu.VMEM((SUBLANES, C), F32), pltpu.VMEM((nl, tr, LANES), F32)],
        compiler_params=_cparams(("arbitrary",)),
    )(yraw, dmix.reshape(SCAN_CHUNKS, L // SCAN_CHUNKS, 2 * C), wmat, bias.reshape(1, C))


def _pool_counts(L, g):
    t = lax.broadcasted_iota(jnp.int32, (L, LANES), 0).astype(F32) + 1.0
    w = jnp.where(g == 0, 2.0, jnp.where(g == 1, 4.0, jnp.where(g == 2, 8.0, 16.0)))
    return 1.0 / jnp.minimum(t, w)


def _select_window(g, a2, a4, a8, a16):
    return jnp.where(g == 0, a2, jnp.where(g == 1, a4, jnp.where(g == 2, a8, a16)))


def _pooled(z, g):
    a2 = z + _down(z, 1)
    a4 = a2 + _down(a2, 2)
    a8 = a4 + _down(a4, 4)
    a16 = a8 + _down(a8, 8)
    return _select_window(g, a2, a4, a8, a16) * _pool_counts(z.shape[0], g) - z


def _transpose_on_mxu(yb):
    c = yb.shape[1]
    eye = lax.broadcasted_iota(jnp.int32, (c, c), 0) == lax.broadcasted_iota(jnp.int32, (c, c), 1)
    return lax.dot_general(eye.astype(BF16), yb, (((1,), (1,)), ((), ())), preferred_element_type=F32).astype(BF16)


def _pool_fwd(proj3, pool_w, scale, name):
    _, L, C = proj3.shape
    ng = len(POOL_WINDOWS)
    pg = C // ng
    assert pg == LANES

    def body(z_ref, w_ref, s_ref, o_ref, ot_ref):
        g = pl.program_id(0)
        p = _pooled(z_ref[...].astype(F32), g)
        y = jnp.dot(p.astype(BF16), w_ref[...].astype(BF16), preferred_element_type=F32)
        yb = (y * s_ref[...]).astype(BF16)
        o_ref[...] = yb
        ot_ref[...] = _transpose_on_mxu(yb)

    return pl.pallas_call(
        body, name=name, out_shape=(jax.ShapeDtypeStruct((L, 2 * C), BF16), jax.ShapeDtypeStruct((2 * C, L), BF16)),
        grid=(ng,),
        in_specs=[pl.BlockSpec((None, L, pg), lambda g: (0, 0, g)), pl.BlockSpec((None, pg, pg), lambda g: (g, 0, 0)),
                  pl.BlockSpec((1, pg), lambda g: (0, g))],
        out_specs=(pl.BlockSpec((L, pg), lambda g: (0, g)), pl.BlockSpec((pg, L), lambda g: (g, 0))),
        compiler_params=_cparams(("parallel",)),
    )(proj3, pool_w, scale.reshape(1, C))


def _pool_bwd(proj3, dmix, pool_w, scale, name):
    _, L, C = proj3.shape
    ng = len(POOL_WINDOWS)
    pg = C // ng

    def body(z_ref, d_ref, w_ref, s_ref, dz_ref, dw_ref, ds_ref):
        g = pl.program_id(0)
        p = _pooled(z_ref[...].astype(F32), g)
        pb = p.astype(BF16)
        wb = w_ref[...].astype(BF16)
        pre = jnp.dot(pb, wb, preferred_element_type=F32)
        dyc = d_ref[...]
        ds_ref[...] = jnp.sum(dyc * pre, axis=0, keepdims=True)
        dpre = (dyc * s_ref[...]).astype(BF16)
        dw_ref[...] = lax.dot_general(pb, dpre, (((0,), (0,)), ((), ())), preferred_element_type=F32)
        dp = lax.dot_general(dpre, wb, (((1,), (1,)), ((), ())), preferred_element_type=F32)
        v = dp * _pool_counts(L, g)
        a2 = v + _up(v, 1)
        a4 = a2 + _up(a2, 2)
        a8 = a4 + _up(a4, 4)
        a16 = a8 + _up(a8, 8)
        dz_ref[...] = (_select_window(g, a2, a4, a8, a16) - dp).astype(BF16)

    return pl.pallas_call(
        body, name=name,
        out_shape=(jax.ShapeDtypeStruct((L, C), BF16), jax.ShapeDtypeStruct((ng, pg, pg), F32),
                   jax.ShapeDtypeStruct((1, C), F32)),
        grid=(ng,),
        in_specs=[pl.BlockSpec((None, L, pg), lambda g: (0, 0, g)), pl.BlockSpec((L, pg), lambda g: (0, g)),
                  pl.BlockSpec((None, pg, pg), lambda g: (g, 0, 0)), pl.BlockSpec((1, pg), lambda g: (0, g))],
        out_specs=(pl.BlockSpec((L, pg), lambda g: (0, g)), pl.BlockSpec((None, pg, pg), lambda g: (g, 0, 0)),
                   pl.BlockSpec((1, pg), lambda g: (0, g))),
        compiler_params=_cparams(("parallel",)),
    )(proj3, dmix, pool_w, scale.reshape(1, C))


def _tril_w(w_ref, h):
    r = lax.broadcasted_iota(jnp.int32, (CHUNK, CHUNK), 0)
    c = lax.broadcasted_iota(jnp.int32, (CHUNK, CHUNK), 1)
    return jnp.where(r >= c, w_ref[h], 0.0)


def _sgu_fwd(proj3, norm_g, w, b, mixin, mixin_t, name):
    _, L, C = proj3.shape
    nh = w.shape[0]
    dh = C // nh
    assert dh == LANES and w.shape[1] == CHUNK
    tr = _pick(L, prefs=(512, 256, 128))
    bfull = jnp.broadcast_to(b[:, :, None], (nh, CHUNK, dh))

    def body(su_ref, sv_ref, g_ref, w_ref, b_ref, m_in, mt_in, o_ref, ot_ref):
        sv = _gelu(sv_ref[...].astype(F32))
        r = lax.rsqrt(jnp.mean(sv * sv, axis=-1, keepdims=True) + EPS)
        v = (sv * r * g_ref[...]).astype(BF16)
        for h in range(nh):
            wm = _tril_w(w_ref, h).astype(BF16)
            cols = slice(h * dh, (h + 1) * dh)
            for n in range(tr // CHUNK):
                rows = slice(n * CHUNK, (n + 1) * CHUNK)
                mixed = jnp.dot(wm, v[rows, cols], preferred_element_type=F32) + b_ref[h]
                o_ref[rows, cols] = (_gelu(su_ref[rows, cols].astype(F32)) * mixed).astype(BF16)
        ot_ref[...] = _transpose_on_mxu(o_ref[...])

    full = lambda shp: pl.BlockSpec(shp, lambda i: (0,) * len(shp))
    anywhere = pl.BlockSpec(memory_space=pl.ANY)
    return pl.pallas_call(
        body, name=name, out_shape=(jax.ShapeDtypeStruct(mixin.shape, BF16), jax.ShapeDtypeStruct(mixin_t.shape, BF16)),
        grid=(L // tr,),
        in_specs=[pl.BlockSpec((None, tr, C), lambda i: (1, i, 0)), pl.BlockSpec((None, tr, C), lambda i: (2, i, 0)),
                  full((1, C)), full((nh, CHUNK, CHUNK)), full((nh, CHUNK, dh)), anywhere, anywhere],
        out_specs=(pl.BlockSpec((tr, C), lambda i: (i, 1)), pl.BlockSpec((C, tr), lambda i: (1, i))),
        input_output_aliases={5: 0, 6: 1}, compiler_params=_cparams(("parallel",)),
    )(proj3, proj3, norm_g.reshape(1, C), w, bfull, mixin, mixin_t)


def _sgu_bwd(proj3, dmix, dz, norm_g, w, b, name):
    _, L, C = proj3.shape
    nh = w.shape[0]
    dh = C // nh
    tr = _pick(L, prefs=(512, 256, 128))
    nsteps = L // tr
    bfull = jnp.broadcast_to(b[:, :, None], (nh, CHUNK, dh))

    def body(su_ref, sv_ref, d_ref, dz_ref, g_ref, w_ref, b_ref, o_ref, dw_ref, db_ref, dg_ref, dv_ref, acc_g):
        i = pl.program_id(0)
        o_ref[0] = dz_ref[...]

        @pl.when(i == 0)
        def _():
            dw_ref[...] = jnp.zeros_like(dw_ref)
            db_ref[...] = jnp.zeros_like(db_ref)
            acc_g[...] = jnp.zeros_like(acc_g)

        svp = sv_ref[...].astype(F32)
        sv = _gelu(svp)
        r = lax.rsqrt(jnp.mean(sv * sv, axis=-1, keepdims=True) + EPS)
        vh = sv * r
        gv = g_ref[...]
        v = (vh * gv).astype(BF16)
        tri_r = lax.broadcasted_iota(jnp.int32, (CHUNK, CHUNK), 0)
        tri_c = lax.broadcasted_iota(jnp.int32, (CHUNK, CHUNK), 1)
        for h in range(nh):
            wm = _tril_w(w_ref, h).astype(BF16)
            cols = slice(h * dh, (h + 1) * dh)
            dwh = jnp.zeros((CHUNK, CHUNK), F32)
            dbh = jnp.zeros((CHUNK, dh), F32)
            for n in range(tr // CHUNK):
                rows = slice(n * CHUNK, (n + 1) * CHUNK)
                vb = v[rows, cols]
                mixed = jnp.dot(wm, vb, preferred_element_type=F32) + b_ref[h]
                sup = su_ref[rows, cols].astype(F32)
                dyd = d_ref[rows, cols]
                dmx = dyd * _gelu(sup)
                o_ref[1, rows, cols] = (dyd * mixed * _gelu_grad(sup)).astype(BF16)
                dmb = dmx.astype(BF16)
                dwh = dwh + lax.dot_general(dmb, vb, (((1,), (1,)), ((), ())), preferred_element_type=F32)
                dbh = dbh + dmx
                dv_ref[rows, cols] = lax.dot_general(wm, dmb, (((0,), (0,)), ((), ())), preferred_element_type=F32)
            dw_ref[h] += jnp.where(tri_r >= tri_c, dwh, 0.0)
            db_ref[h] += dbh
        dv = dv_ref[...]
        acc_g[...] += jnp.sum((dv * vh).reshape(tr // SUBLANES, SUBLANES, C), axis=0)
        dvg = dv * gv
        dsv = r * (dvg - vh * jnp.mean(dvg * vh, axis=-1, keepdims=True))
        o_ref[2] = (dsv * _gelu_grad(svp)).astype(BF16)

        @pl.when(i == nsteps - 1)
        def _():
            dg_ref[...] = jnp.sum(acc_g[...], axis=0, keepdims=True)

    full = lambda shp: pl.BlockSpec(shp, lambda i: (0,) * len(shp))
    return pl.pallas_call(
        body, name=name,
        out_shape=(jax.ShapeDtypeStruct((3, L, C), BF16), jax.ShapeDtypeStruct((nh, CHUNK, CHUNK), F32),
                   jax.ShapeDtypeStruct((nh, CHUNK, dh), F32), jax.ShapeDtypeStruct((1, C), F32)),
        grid=(nsteps,),
        in_specs=[pl.BlockSpec((None, tr, C), lambda i: (1, i, 0)), pl.BlockSpec((None, tr, C), lambda i: (2, i, 0)),
                  pl.BlockSpec((tr, C), lambda i: (i, 1)), pl.BlockSpec((tr, C), lambda i: (i, 0)), full((1, C)),
                  full((nh, CHUNK, CHUNK)), full((nh, CHUNK, dh))],
        out_specs=(pl.BlockSpec((3, tr, C), lambda i: (0, i, 0)), full((nh, CHUNK, CHUNK)), full((nh, CHUNK, dh)),
                   full((1, C))),
        scratch_shapes=[pltpu.VMEM((tr, C), F32), pltpu.VMEM((SUBLANES, C), F32)],
        compiler_params=_cparams(("arbitrary",)),
    )(proj3, proj3, dmix, dz, norm_g.reshape(1, C), w, bfull)


def _ffn_act_fwd(up3, conv_w, conv_b, name):
    _, L, Fh = up3.shape
    cb = LANES
    w2 = conv_w.reshape(3, 2, Fh).transpose(1, 0, 2)
    b2 = conv_b.reshape(2, 1, Fh)

    def body(u_ref, w_ref, b_ref, o_ref, ot_ref, gv_ref):
        g = _conv3(_taps(u_ref[0].astype(F32)), w_ref[0]) + b_ref[0]
        v = _conv3(_taps(u_ref[1].astype(F32)), w_ref[1]) + b_ref[1]
        gv_ref[0] = g.astype(BF16)
        gv_ref[1] = v.astype(BF16)
        ab = (g * _sigmoid(g) * v).astype(BF16)
        o_ref[...] = ab
        ot_ref[...] = _transpose_on_mxu(ab)

    blk3 = pl.BlockSpec((2, L, cb), lambda j: (0, 0, j))
    return pl.pallas_call(
        body, name=name,
        out_shape=(jax.ShapeDtypeStruct((L, Fh), BF16), jax.ShapeDtypeStruct((Fh, L), BF16),
                   jax.ShapeDtypeStruct((2, L, Fh), BF16)),
        grid=(Fh // cb,),
        in_specs=[blk3, pl.BlockSpec((2, 3, cb), lambda j: (0, 0, j)), pl.BlockSpec((2, 1, cb), lambda j: (0, 0, j))],
        out_specs=(pl.BlockSpec((L, cb), lambda j: (0, j)), pl.BlockSpec((cb, L), lambda j: (j, 0)), blk3),
        compiler_params=_cparams(("parallel",)),
    )(up3, w2, b2)


def _ffn_act_bwd(up3, gv3, da, conv_w, h2t, name):
    _, L, Fh = up3.shape
    D = h2t.shape[0]
    cb = LANES
    nb = Fh // cb
    w2 = conv_w.reshape(3, 2, Fh).transpose(1, 0, 2)

    def body(u_ref, gv_ref, d_ref, w_ref, h_ref, o_ref, dw_ref, db_ref, wg_ref, wv_ref, scr):
        j = pl.program_id(0)

        @pl.when(j == 0)
        def _():
            scr[1] = jnp.zeros((2, L, cb), BF16)

        prev = scr.at[(j + 1) % 2]
        wg_ref[...] = jnp.dot(h_ref[...], prev[0], preferred_element_type=F32).astype(BF16)
        wv_ref[...] = jnp.dot(h_ref[...], prev[1], preferred_element_type=F32).astype(BF16)
        tg, tv = _taps(u_ref[0].astype(F32)), _taps(u_ref[1].astype(F32))
        wg, wv = w_ref[0], w_ref[1]
        g = gv_ref[0].astype(F32)
        v = gv_ref[1].astype(F32)
        sg = _sigmoid(g)
        dav = d_ref[...].astype(F32)
        dg = dav * v * (sg * (1.0 + g * (1.0 - sg)))
        dv = dav * (g * sg)
        dug = _conv3_t(dg, wg).astype(BF16)
        duv = _conv3_t(dv, wv).astype(BF16)
        o_ref[0] = dug
        o_ref[1] = duv
        cur = scr.at[j % 2]
        cur[0] = dug
        cur[1] = duv
        for tap, (dwg, dwv) in enumerate(zip(_conv3_dw(dg, tg), _conv3_dw(dv, tv))):
            dw_ref[0, tap:tap + 1, :] = dwg
            dw_ref[1, tap:tap + 1, :] = dwv
        db_ref[0] = jnp.sum(dg, axis=0, keepdims=True)
        db_ref[1] = jnp.sum(dv, axis=0, keepdims=True)

    here = lambda j: jnp.minimum(j, nb - 1)
    before = lambda j: jnp.maximum(j - 1, 0)
    blk3 = pl.BlockSpec((2, L, cb), lambda j: (0, 0, here(j)))
    dup, dw2, db2, dwg, dwv = pl.pallas_call(
        body, name=name,
        out_shape=(jax.ShapeDtypeStruct((2, L, Fh), BF16), jax.ShapeDtypeStruct((2, 3, Fh), F32),
                   jax.ShapeDtypeStruct((2, 1, Fh), F32), jax.ShapeDtypeStruct((D, Fh), BF16),
                   jax.ShapeDtypeStruct((D, Fh), BF16)),
        grid=(nb + 1,),
        in_specs=[blk3, blk3, pl.BlockSpec((L, cb), lambda j: (0, here(j))),
                  pl.BlockSpec((2, 3, cb), lambda j: (0, 0, here(j))), pl.BlockSpec((D, L), lambda j: (0, 0))],
        out_specs=(blk3, pl.BlockSpec((2, 3, cb), lambda j: (0, 0, here(j))),
                   pl.BlockSpec((2, 1, cb), lambda j: (0, 0, here(j))),
                   pl.BlockSpec((D, cb), lambda j: (0, before(j))), pl.BlockSpec((D, cb), lambda j: (0, before(j)))),
        scratch_shapes=[pltpu.VMEM((2, 2, L, cb), BF16)],
        compiler_params=_cparams(("arbitrary",), VMEM_LIMIT_S5),
    )(up3, gv3, da, w2, h2t)
    return dup, dw2.transpose(1, 0, 2).reshape(3, 2 * Fh), db2.reshape(2 * Fh), jnp.concatenate([dwg, dwv], axis=1)


def _local_step(x, tgt, w, layer_weights, on_layer_grads):
    L, D = x.shape
    depth = w['norm_mix_g'].shape[0]
    saved = []
    for i in range(depth):
        j = i // 2
        wb = dict(layer_weights(2 * i, x))
        s = {'x': x, 'wb': wb}
        if i % 2 == 0:
            proj4, s['hT'] = _norm_mm(x, w['norm_mix_g'][i], wb['even_w_in'], BF16, "even_in_fwd", ok=('seg', 4))
            s['proj'] = proj4
            mixin = _sconv_fwd(proj4, w['even_conv_w'][j], "sconv_fwd")
            prm = (w['ssm_log_step'][j], w['ssm_a_re'][j], w['ssm_a_im'][j], w['ssm_b_re'][j], w['ssm_b_im'][j],
                   w['ssm_c_re'][j], w['ssm_c_im'][j])
            (lr, li, bmat, cmat), prep_vjp = jax.vjp(_s5_prep, *prm)
            yraw, s_re, s_im = _s5_fwd(proj4, lr, li, bmat, cmat, w['ssm_d'][j], "s5_fwd")
            mixin = _glu_fwd(yraw, wb['ssm_glu_w'], w['ssm_glu_b'][j], mixin, "glu_fwd")
            s.update(yraw=yraw, s_re=s_re, s_im=s_im, s5=(lr, li, bmat, cmat), prep_vjp=prep_vjp)
            s['mixinT'] = mixin.T
            x = _mm(mixin, wb['even_w_out'], 'nn', F32, "even_out_fwd", res=x)
        else:
            proj3, s['hT'] = _norm_mm(x, w['norm_mix_g'][i], wb['odd_w_in'], BF16, "odd_in_fwd", ok=('seg', 3))
            s['proj'] = proj3
            mixin, mixin_t = _pool_fwd(proj3, w['pool_w'][j], w['pool_scale'][j], "pool_fwd")
            mixin, s['mixinT'] = _sgu_fwd(proj3, w['sgu_norm_g'][j], w['sgu_w'][j], w['sgu_b'][j], mixin, mixin_t,
                                          "sgu_fwd")
            x = _mm(mixin, wb['odd_w_out'], 'nn', F32, "odd_out_fwd", res=x)
        s['x1'] = x
        wb.update(layer_weights(2 * i + 1, x))
        up3, h2t = _norm_mm(x, w['norm_ffn_g'][i], wb['ffn_w_up'], BF16, "ffn_up_fwd", ok=('seg', 2))
        a, at, gv3 = _ffn_act_fwd(up3, w['ffn_conv_w'][i], w['ffn_conv_b'][i], "ffn_act_fwd")
        x = _mm(a, wb['ffn_w_down'], 'nn', F32, "ffn_down_fwd", res=x)
        s.update(h2T=h2t, up3=up3, aT=at, gv3=gv3)
        saved.append(s)

    loss8, dx, dxb, dg_final = _loss_head(x, w['norm_final_g'], tgt)
    gs = {n: [None] * w[n].shape[0] for n in SMALL if n != 'norm_final_g'}
    gs['norm_final_g'] = dg_final.reshape(D)

    dep = None
    for i in reversed(range(depth)):
        j = i // 2
        s = saved[i]
        wb = s['wb']
        gb = {}
        da = _mm(dxb, wb['ffn_w_down'], 'nt', BF16, "ffn_down_dgrad", dep=dep)
        gb['ffn_w_down'] = _mm(s['aT'], dxb, 'nn', BF16, "ffn_down_wgrad")
        dup3, dcw, dcb, gb['ffn_w_up'] = _ffn_act_bwd(s['up3'], s['gv3'], da, w['ffn_conv_w'][i], s['h2T'],
                                                      "ffn_act_bwd")
        gs['ffn_conv_w'][i], gs['ffn_conv_b'][i] = dcw, dcb
        dep = on_layer_grads(2 * i + 1, gb)
        dx, dxb, dg = _mm_norm_bwd(dup3, wb['ffn_w_up'], s['x1'], w['norm_ffn_g'][i], dx, "ffn_up_dgrad",
                              ak=('seg', 2), dep=dep)
        gs['norm_ffn_g'][i] = dg.reshape(D)
        gb = {}
        if i % 2 == 0:
            dmix = _mm(dxb, wb['even_w_out'], 'nt', F32, "even_out_dgrad")
            gb['even_w_out'] = _mm(s['mixinT'], dxb, 'nn', BF16, "even_out_wgrad")
            dproj, dcw = _sconv_bwd(s['proj'], dmix, w['even_conv_w'][j], "sconv_bwd")
            gs['even_conv_w'][j] = dcw
            dyraw, dglu_w, dglu_b = _glu_bwd(s['yraw'], dmix, wb['ssm_glu_w'], w['ssm_glu_b'][j], "glu_bwd")
            gb['ssm_glu_w'] = dglu_w.astype(BF16)
            gs['ssm_glu_b'][j] = dglu_b.reshape(-1)
            lr, li, bmat, cmat = s['s5']
            dproj, dbm, dcm, dlam, dd = _s5_bwd(dyraw, s['proj'], dproj, s['s_re'], s['s_im'], lr, li, bmat, cmat,
                                               w['ssm_d'][j], "s5_bwd")
            gs['ssm_d'][j] = dd.reshape(-1)
            dcm = jnp.swapaxes(dcm, 1, 2)
            dprm = s['prep_vjp']((dlam[:, 0:1, :], dlam[:, 1:2, :], dbm, dcm))
            for n, gval in zip(('ssm_log_step', 'ssm_a_re', 'ssm_a_im', 'ssm_b_re', 'ssm_b_im', 'ssm_c_re',
                                'ssm_c_im'), dprm):
                gs[n][j] = gval
            gb['even_w_in'] = _mm(s['hT'], dproj, 'nn', BF16, "even_in_wgrad", bk=('seg', 4))
            w_in, in_kind, in_name = wb['even_w_in'], ('seg', 4), "even_in_dgrad"
        else:
            dmix = _mm(dxb, wb['odd_w_out'], 'nt', F32, "odd_out_dgrad")
            gb['odd_w_out'] = _mm(s['mixinT'], dxb, 'nn', BF16, "odd_out_wgrad")
            dz, dpw, dps = _pool_bwd(s['proj'], dmix, w['pool_w'][j], w['pool_scale'][j], "pool_bwd")
            gs['pool_w'][j], gs['pool_scale'][j] = dpw, dps.reshape(-1)
            dproj, dsw, dsb, dsg = _sgu_bwd(s['proj'], dmix, dz, w['sgu_norm_g'][j], w['sgu_w'][j], w['sgu_b'][j],
                                            "sgu_bwd")
            gs['sgu_w'][j], gs['sgu_b'][j], gs['sgu_norm_g'][j] = dsw, jnp.sum(dsb, axis=-1), dsg.reshape(-1)
            gb['odd_w_in'] = _mm(s['hT'], dproj, 'nn', BF16, "odd_in_wgrad", bk=('seg', 3))
            w_in, in_kind, in_name = wb['odd_w_in'], ('seg', 3), "odd_in_dgrad"
        dep = on_layer_grads(2 * i, gb)
        dx, dxb, dg = _mm_norm_bwd(dproj, w_in, s['x'], w['norm_mix_g'][i], dx, in_name, ak=in_kind, dep=dep)
        gs['norm_mix_g'][i] = dg.reshape(D)

    gsmall = {n: (v if n == 'norm_final_g' else jnp.stack(v)) for n, v in gs.items()}
    return loss8[0, 0], dx, gsmall


_HBM = pl.BlockSpec(memory_space=pltpu.HBM)
_CHIP_FLIPS = ((0, 0), (1, 0), (0, 1), (1, 1))


def _coords():
    return lax.axis_index("x"), lax.axis_index("y"), lax.axis_index("c")


def _flip(v, f):
    return 1 - v if f else v


def _shard_of(ref, axis, s, width):
    start = pl.multiple_of(s * width, LANES if axis == ref.ndim - 1 else 16) if width % 16 == 0 else s * width
    idx = [slice(None)] * ref.ndim
    idx[axis] = pl.ds(start, width)
    return ref.at[tuple(idx)]


_SEM = pl.BlockSpec(memory_space=pltpu.SEMAPHORE)
_ANY = pl.BlockSpec(memory_space=pl.ANY)
_DATAFLOW = pltpu.SideEffectType.DATAFLOW_SIDE_EFFECTING


def _in_hbm(a):
    return pltpu.with_memory_space_constraint(a, pltpu.HBM)


def _model_layer(name, l):
    if name.startswith('ffn'):
        return l
    return 2 * l + 1 if name.startswith('odd') else 2 * l


def _place_quarter(shard, l, axis, chip, dtype, dep=None):
    _, r, c = shard.shape
    tr = _pick(r, prefs=(512, 256, 128, 64, 32, 16))
    nrb = r // tr

    def body(chip_ref, i_ref, *rest):
        rest[-1][...] = i_ref[...].astype(dtype)

    if axis == 1:
        out_shape, o_map = (r, c * N_CHIPS), (lambda i, s: (i, s[0]))
    else:
        out_shape, o_map = (r * N_CHIPS, c), (lambda i, s: (s[0] * nrb + i, 0))
    in_specs = [pl.BlockSpec((None, tr, c), lambda i, s: (l, i, 0))]
    args = [chip, shard]
    if dep is not None:
        in_specs.append(pl.BlockSpec(memory_space=pl.ANY))
        args.append(dep)
    return pl.pallas_call(
        body, name="place_quarter", out_shape=jax.ShapeDtypeStruct(out_shape, dtype),
        grid_spec=pltpu.PrefetchScalarGridSpec(
            num_scalar_prefetch=1, grid=(nrb,), in_specs=in_specs, out_specs=pl.BlockSpec((tr, c), o_map)),
        compiler_params=_cparams(("parallel",)),
    )(*args)


def _gather_copies(land_refs, send_sem, recv_sem, axes, landing_chip_of, first=0):
    x, y, c = _coords()
    out = []
    for j, land in enumerate(land_refs):
        width = land.shape[axes[j]] // N_CHIPS
        for f in (1, 2, 3):
            fx, fy = _CHIP_FLIPS[f]
            px, py = _flip(x, fx), _flip(y, fy)
            lx, ly = landing_chip_of(px, py)
            out.append(pltpu.make_async_remote_copy(
                src_ref=_shard_of(land, axes[j], 2 * x + y, width), dst_ref=_shard_of(land, axes[j], 2 * lx + ly, width),
                send_sem=send_sem.at[3 * (first + j) + f - 1], recv_sem=recv_sem.at[3 * (first + j) + f - 1],
                device_id=(px, py, c), device_id_type=MESH))
    return out


def _gather_start(tag, lands, axes, dep=None):
    n = len(lands)

    def body(*refs):
        land_refs, send_sem, recv_sem = refs[:n], refs[-3], refs[-2]
        x, y, _ = _coords()
        for cp in _gather_copies(land_refs, send_sem, recv_sem, axes, lambda px, py: (x, y)):
            cp.start()
        refs[-1][...] = jnp.zeros_like(refs[-1])

    thru = [pltpu.HBM(a.shape, a.dtype) for a in lands]
    outs = pl.pallas_call(
        body, name=f"gather_start_{tag}",
        out_shape=tuple(thru + [pltpu.SemaphoreType.DMA((3 * n,)), pltpu.SemaphoreType.DMA((3 * n,)),
                                jax.ShapeDtypeStruct((SUBLANES, LANES), F32)]),
        in_specs=[_HBM] * n + ([_ANY] if dep is not None else []),
        out_specs=tuple([_HBM] * n + [_SEM, _SEM, pl.BlockSpec(memory_space=pltpu.VMEM)]),
        input_output_aliases={i: i for i in range(n)},
        compiler_params=pltpu.CompilerParams(has_side_effects=_DATAFLOW),
    )(*[_in_hbm(a) for a in lands], *([dep] if dep is not None else []))
    return list(outs[:n]), outs[n], outs[n + 1], outs[n + 2]


def _gather_wait(tag, lands, send_sem, recv_sem, axes, after, first=0):
    n = len(lands)

    def body(*refs):
        for cp in _gather_copies(refs[:n], refs[n], refs[n + 1], axes, lambda px, py: (px, py), first):
            cp.wait_send()
            cp.wait_recv()

    outs = pl.pallas_call(
        body, name=f"gather_wait_{tag}", out_shape=tuple(pltpu.HBM(a.shape, a.dtype) for a in lands),
        in_specs=[_HBM] * n + [_SEM, _SEM, _ANY], out_specs=tuple([_HBM] * n),
        input_output_aliases={i: i for i in range(n)},
        compiler_params=pltpu.CompilerParams(has_side_effects=_DATAFLOW),
    )(*lands, send_sem, recv_sem, after)
    return list(outs)


N_SLOTS = N_DEV - 1


def _scatter_sends(grad_refs, land_refs, send_sem, recv_sem, meta):
    x, y, c = _coords()
    out = []
    for j, (axis, owner, q, width) in enumerate(meta):
        other = c if owner == 0 else 1 - c
        for f, (fx, fy) in enumerate(_CHIP_FLIPS):
            px, py = _flip(x, fx), _flip(y, fy)
            slot = f + 4 * other - 1
            out.append((other if f == 0 else None, pltpu.make_async_remote_copy(
                src_ref=_shard_of(grad_refs[j], axis, 2 * px + py, width), dst_ref=land_refs[j].at[q, slot],
                send_sem=send_sem.at[4 * j + f], recv_sem=recv_sem.at[N_SLOTS * j + slot],
                device_id=(px, py, owner), device_id_type=MESH)))
    return out


def _scatter_start(layer, grads, lands, meta):
    n = len(grads)
    uniq = []
    for a in lands:
        if not any(a is u for u in uniq):
            uniq.append(a)
    which = [next(k for k, u in enumerate(uniq) if u is a) for a in lands]
    nu = len(uniq)

    def body(*refs):
        grad_refs, land_u = refs[:n], refs[n:n + nu]
        send_sem, recv_sem = refs[n + nu], refs[n + nu + 1]
        for other, cp in _scatter_sends(grad_refs, [land_u[k] for k in which], send_sem, recv_sem, meta):
            if other is None:
                cp.start()
            else:
                pl.when(other == 1)(cp.start)
        refs[-1][...] = jnp.zeros_like(refs[-1])

    thru = [pltpu.HBM(a.shape, a.dtype) for a in list(grads) + uniq]
    outs = pl.pallas_call(
        body, name=f"scatter_start_{layer}",
        out_shape=tuple([pltpu.SemaphoreType.DMA((4 * n,)), pltpu.SemaphoreType.DMA((N_SLOTS * n,))] + thru
                        + [jax.ShapeDtypeStruct((SUBLANES, LANES), F32)]),
        in_specs=[_HBM] * (n + nu),
        out_specs=tuple([_SEM, _SEM] + [_HBM] * (n + nu) + [pl.BlockSpec(memory_space=pltpu.VMEM)]),
        input_output_aliases={i: 2 + i for i in range(n + nu)},
        compiler_params=pltpu.CompilerParams(has_side_effects=_DATAFLOW),
    )(*[_in_hbm(a) for a in list(grads) + uniq])
    new_lands = [outs[2 + n + k] for k in which]
    return outs[0], outs[1], list(outs[2:2 + n]), new_lands, outs[-1]


def _scatter_wait(started, lands):
    nl = len(lands)
    flat_grads = [g for s in started for g in s[2]]
    ng, ns = len(flat_grads), len(started)

    def body(*refs):
        land_refs = refs[:nl]
        grad_refs = refs[nl:nl + ng]
        sem_refs = refs[nl + ng:nl + ng + 2 * ns]
        _, _, c = _coords()
        off = 0
        for k, (_, _, grads, idx, meta) in enumerate(started):
            send_sem, recv_sem = sem_refs[2 * k], sem_refs[2 * k + 1]
            lr = [land_refs[i] for i in idx]
            for other, cp in _scatter_sends(grad_refs[off:off + len(grads)], lr, send_sem, recv_sem, meta):
                if other is None:
                    cp.wait_send()
                else:
                    pl.when(other == 1)(cp.wait_send)
            for j, (axis, owner, q, width) in enumerate(meta):
                mine = (c if owner == 0 else 1 - c) == 0

                @pl.when(mine)
                def _():
                    for slot in range(N_SLOTS):
                        land = lr[j].at[q, slot]
                        pltpu.make_async_remote_copy(
                            src_ref=land, dst_ref=land, send_sem=send_sem.at[0], recv_sem=recv_sem.at[N_SLOTS * j + slot],
                            device_id=_coords(), device_id_type=MESH).wait_recv()
            off += len(grads)

    args = list(lands) + flat_grads
    thru = [pltpu.HBM(a.shape, a.dtype) for a in args]
    sems = [s for st in started for s in st[:2]]
    outs = pl.pallas_call(
        body, name="scatter_wait", out_shape=tuple(thru), in_specs=[_HBM] * (nl + ng) + [_SEM] * (2 * ns),
        out_specs=tuple([_HBM] * (nl + ng)), input_output_aliases={i: i for i in range(nl + ng)},
        compiler_params=pltpu.CompilerParams(has_side_effects=_DATAFLOW),
    )(*args, *sems)
    return list(outs[:nl]), list(outs[nl:])


def _sum_and_share(recv, layer_grads, axis, chip, name, dep=None):
    n, ns, r, c = recv.shape
    tr = _pick(r, prefs=(256, 128, 64, 32, 16))
    nr = r // tr
    nsteps = n * nr
    nlay = len(layer_grads)
    own_map = (lambda h, i, s: (i, s[0])) if axis == 1 else (lambda h, i, s: (s[0] * nr + i, 0))

    def body(chip_ref, i_ref, *rest):
        g_refs = rest[:nlay]
        o_ref, buf, loc_sems, send_sems, recv_sems = rest[nlay + (dep is not None):]
        h, i = pl.program_id(0), pl.program_id(1)
        step = h * nr + i
        slot = step % 2
        x, y, core = _coords()
        layer = core * n + h
        own = g_refs[0][...]
        for l in range(1, nlay):
            own = jnp.where(layer == l, g_refs[l][...], own)

        def copies(sl):
            dst = o_ref.at[core * n + h, pl.ds(pl.multiple_of(i * tr, tr), tr), :]
            loc = pltpu.make_async_copy(buf.at[sl], dst, loc_sems.at[sl])
            rem = pltpu.make_async_remote_copy(
                src_ref=buf.at[sl], dst_ref=dst, send_sem=send_sems.at[sl], recv_sem=recv_sems.at[step],
                device_id=(x, y, 1 - core), device_id_type=MESH)
            return loc, rem

        def drain(sl):
            loc, rem = copies(sl)
            loc.wait()
            rem.wait_send()

        pl.when(step >= 2)(lambda: drain(slot))
        acc = own.astype(F32)
        for s in range(ns):
            acc = acc + i_ref[s].astype(F32)
        buf[slot] = acc
        loc, rem = copies(slot)
        loc.start()
        rem.start()

        @pl.when(step == nsteps - 1)
        def _():
            drain(slot)
            if nsteps > 1:
                drain(1 - slot)
            for hh in range(n):
                for ii in range(nr):
                    land = o_ref.at[(1 - core) * n + hh, pl.ds(ii * tr, tr), :]
                    pltpu.make_async_remote_copy(
                        src_ref=buf.at[0], dst_ref=land, send_sem=send_sems.at[0], recv_sem=recv_sems.at[hh * nr + ii],
                        device_id=(x, y, 1 - core), device_id_type=MESH).wait_recv()

    return pl.pallas_call(
        body, name=name, out_shape=jax.ShapeDtypeStruct((2 * n, r, c), F32),
        grid_spec=pltpu.PrefetchScalarGridSpec(
            num_scalar_prefetch=1, grid=(n, nr),
            in_specs=[pl.BlockSpec((None, ns, tr, c), lambda h, i, s: (h, 0, i, 0))]
            + [pl.BlockSpec((tr, c), own_map)] * nlay + ([pl.BlockSpec(memory_space=pl.ANY)] if dep is not None else []),
            out_specs=_HBM,
            scratch_shapes=[pltpu.VMEM((2, tr, c), F32), pltpu.SemaphoreType.DMA((2,)),
                            pltpu.SemaphoreType.DMA((2,)), pltpu.SemaphoreType.DMA((nsteps,))]),
        compiler_params=_cparams(("arbitrary", "arbitrary")),
    )(chip, recv, *layer_grads, *([dep] if dep is not None else []))


def _adamw_update(w_ref, g_ref, m_ref, v_ref, d_ref, mo_ref, vo_ref):
    bc1 = 1.0 - ADAM_B1 ** ADAM_STEP
    bc2 = 1.0 - ADAM_B2 ** ADAM_STEP
    gv = g_ref[...]
    mn = ADAM_B1 * m_ref[...] + (1.0 - ADAM_B1) * gv
    vn = ADAM_B2 * v_ref[...] + (1.0 - ADAM_B2) * (gv * gv)
    d_ref[...] = -ADAM_LR * ((mn / bc1) / (jnp.sqrt(vn / bc2) + ADAM_EPS) + ADAM_WD * w_ref[...])
    mo_ref[...] = mn
    vo_ref[...] = vn


def _adamw(w, g, m, v, name):
    def body(*refs):
        _adamw_update(*refs)

    tr = _pick(w.shape[0], prefs=(256, 128, 64, 32, 16, 8))
    blk = pl.BlockSpec((tr, w.shape[1]), lambda i: (i, 0))
    sds = jax.ShapeDtypeStruct(w.shape, F32)
    return pl.pallas_call(
        body, name=name, out_shape=(sds, sds, sds), grid=(w.shape[0] // tr,), in_specs=[blk] * 4,
        out_specs=(blk,) * 3, compiler_params=_cparams(("parallel",)),
    )(w, g, m, v)


def _adamw_many(tensors, name, by_layer=False):
    n = len(tensors)

    def body(*refs):
        for t in range(n):
            _adamw_update(*refs[4 * t:4 * t + 4], *refs[4 * n + 3 * t:4 * n + 3 * t + 3])

    def spec(a):
        nd = a.ndim
        if by_layer:
            return pl.BlockSpec((1,) + a.shape[1:], lambda i: (i,) + (0,) * (nd - 1))
        return pl.BlockSpec(a.shape, lambda i: (0,) * nd)

    steps = tensors[0][0].shape[0] if by_layer else 1
    outs = pl.pallas_call(
        body, name=name, out_shape=tuple(jax.ShapeDtypeStruct(t[0].shape, F32) for t in tensors for _ in range(3)),
        grid=(steps,), in_specs=[spec(a) for t in tensors for a in t],
        out_specs=tuple(spec(t[0]) for t in tensors for _ in range(3)), compiler_params=_cparams(("parallel",)),
    )(*[a for t in tensors for a in t])
    return [tuple(outs[3 * t:3 * t + 3]) for t in range(n)]


_PACK_QUANTUM = 256 * LANES


def _pack(arrs):
    flat = jnp.concatenate([a.reshape(-1).astype(F32) for a in arrs])
    flat = jnp.pad(flat, (0, (-flat.shape[0]) % _PACK_QUANTUM))
    return flat.reshape(-1, LANES)


def _unpack(p, shapes):
    flat = p.reshape(-1)
    out, off = [], 0
    for s in shapes:
        n = int(np.prod(s))
        out.append(flat[off:off + n].reshape(s))
        off += n
    return out


def kernel(*args):
    nw = len(WEIGHTS)
    x, tgt = args[0], args[1 + nw]
    w = dict(zip(WEIGHTS, args[1:1 + nw]))
    m = dict(zip(WEIGHTS, args[2 + nw:2 + 2 * nw]))
    v = dict(zip(WEIGHTS, args[2 + 2 * nw:2 + 3 * nw]))
    _, L, D = x.shape
    chip = 2 * lax.axis_index("x") + lax.axis_index("y")

    big = list(BIG)
    small_sh_shapes = [w[n].shape for n in SMALL_SHARDED]
    nbig = len(big)
    chip1 = chip.reshape(1).astype(jnp.int32)
    axes2 = [BIG[n] - 1 for n in big] + [0]
    shards = [w[n] for n in big] + [_pack([w[n] for n in SMALL_SHARDED])[None]]
    pairs = [(t, l) for t in range(nbig + 1) for l in range(shards[t].shape[0])]
    depth = w['norm_mix_g'].shape[0]
    part_of = lambda t, l: 0 if t == nbig else 2 * _model_layer(big[t], l) + big[t].startswith('ffn')
    flying, token = {}, None
    for tag, gset in enumerate(([0], list(range(1, 2 * depth)))):
        ids = [k for g in gset for k, (t, l) in enumerate(pairs) if part_of(t, l) == g]
        ts = [pairs[k][0] for k in ids]
        placed = [_place_quarter(shards[t], pairs[k][1], axes2[t], chip1, F32 if t == nbig else BF16, token)
                  for k, t in zip(ids, ts)]
        lands, send, recv, token = _gather_start(tag, placed, [axes2[t] for t in ts], token)
        first = 0
        for g in gset:
            n = sum(1 for t, l in pairs if part_of(t, l) == g)
            flying[g] = (ts[first:first + n], lands[first:first + n], send, recv, first)
            first += n

    def wait_group(g, after):
        ts, lands, send, recv, first = flying[g]
        landed = _gather_wait(g, lands, send, recv, [axes2[t] for t in ts], token if after is None else after, first)
        return dict(zip(ts, landed))

    first = wait_group(0, None)
    packed = first.pop(nbig).reshape(N_CHIPS, -1, LANES)
    per_chip = [_unpack(packed[s], small_sh_shapes) for s in range(N_CHIPS)]
    wl = dict(w)
    for k, n in enumerate(SMALL_SHARDED):
        wl[n] = jnp.concatenate([per_chip[s][k] for s in range(N_CHIPS)], axis=-1)

    def layer_weights(i, after):
        got = first if i == 0 else wait_group(i, after)
        return {big[t]: a for t, a in got.items()}

    small_shapes = [(w[n].shape[:-1] + (w[n].shape[-1] * N_CHIPS,)) if n in SMALL_SHARDED else w[n].shape
                    for n in SMALL] + [(1,)]
    n_small = sum(int(np.prod(s)) for s in small_shapes)
    pack_rows = -(-n_small // _PACK_QUANTUM) * _PACK_QUANTUM // LANES
    nlayers = [w[n].shape[0] for n in big] + [2]
    halves = [n // 2 for n in nlayers]
    quarters = [tuple(w[n].shape[1:]) for n in big] + [(pack_rows // 2 // N_CHIPS, LANES)]
    wire = [BF16] * nbig + [F32]
    land_now = [lax.empty((halves[t], N_SLOTS) + quarters[t], wire[t]) for t in range(nbig + 1)]
    gparts = [[None] * n for n in nlayers]
    started = []

    def start_scatter(tag, ts, ls, arrays):
        meta = [(axes2[t], l // halves[t], l % halves[t], quarters[t][axes2[t]]) for t, l in zip(ts, ls)]
        send, recv, thru, new_lands, token = _scatter_start(tag, arrays, [land_now[t] for t in ts], meta)
        for t, ln in zip(ts, new_lands):
            land_now[t] = ln
        started.append((send, recv, thru, ts, meta, ls))
        return token

    def on_layer_grads(g, gb):
        ts = [big.index(n) for n in gb]
        return start_scatter(g, ts, [g // 2 if big[t].startswith('ffn') else g // 4 for t in ts],
                             [gb[big[t]] for t in ts])

    loss, dx, gsmall = _local_step(x.reshape(L, D), tgt.reshape(L, D), wl, layer_weights, on_layer_grads)
    gpack = _pack([gsmall[n] for n in SMALL] + [loss.reshape(1)])
    start_scatter(2 * depth, [nbig, nbig], [0, 1], [gpack[:pack_rows // 2], gpack[pack_rows // 2:]])
    landed, sent = _scatter_wait([s[:5] for s in started], land_now)
    for (t, l), g in zip([(t, l) for s in started for t, l in zip(s[3], s[5])], sent):
        gparts[t][l] = g
    small_sum = _sum_and_share(landed[nbig], gparts[nbig], 0, chip1, "sum_share_small")
    quarter_rows = small_sum.shape[0] * small_sum.shape[1]
    placed = _place_quarter(small_sum.reshape(1, quarter_rows, LANES), 0, 0, chip1, F32)
    flying_small, send, recv, token = _gather_start("small", [placed], [0])
    gshard = {n: _sum_and_share(landed[t], gparts[t], axes2[t], chip1, "sum_share_" + n, token)
              for t, n in enumerate(big)}
    small_all = _gather_wait("small", flying_small, send, recv, [0], gshard[big[-1]])[0]
    gpack = small_all.reshape(N_CHIPS, 2, quarter_rows // 2, LANES).transpose(1, 0, 2, 3).reshape(pack_rows, LANES)
    gs = dict(zip(SMALL + ['loss'], _unpack(gpack, small_shapes)))
    loss = gs.pop('loss').reshape(())
    for n in SMALL_SHARDED:
        width = w[n].shape[-1]
        gs[n] = lax.dynamic_slice_in_dim(gs[n], chip * width, width, axis=gs[n].ndim - 1)

    grads, delta, new_m, new_v = {}, {}, {}, {}
    for n in big:
        shp = w[n].shape
        flat = lambda a: a.reshape(shp[0] * shp[1], shp[2])
        g = gshard[n]
        grads[n] = g
        d_, m_, v_ = _adamw(flat(w[n]), flat(g), flat(m[n]), flat(v[n]), "adamw_" + n)
        delta[n], new_m[n], new_v[n] = d_.reshape(shp), m_.reshape(shp), v_.reshape(shp)
    sparse = [n for n in SMALL if w[n].ndim == 4 and w[n].shape[-1] < LANES // 2]
    for names, by_layer in ((sparse, True), ([n for n in SMALL if n not in sparse], False)):
        as2d = lambda a: a.reshape(1, -1) if a.ndim == 1 else a
        res = _adamw_many([(as2d(w[n]), as2d(gs[n]), as2d(m[n]), as2d(v[n])) for n in names],
                          "adamw_small_by_layer" if by_layer else "adamw_small", by_layer)
        for n, (d_, m_, v_) in zip(names, res):
            shp = w[n].shape
            grads[n], delta[n], new_m[n], new_v[n] = gs[n], d_.reshape(shp), m_.reshape(shp), v_.reshape(shp)

    return (loss, dx.reshape(1, L, D), *[grads[n] for n in WEIGHTS], *[delta[n] for n in WEIGHTS],
            *[new_m[n] for n in WEIGHTS], *[new_v[n] for n in WEIGHTS])
```

```python
import math

import numpy as np
import jax
import jax.numpy as jnp
from jax import lax
from jax.experimental import pallas as pl
from jax.experimental.pallas import tpu as pltpu

F32 = jnp.float32
BF16 = jnp.bfloat16
MESH = pl.DeviceIdType.MESH

EPS = 1e-6
CHUNK = 128
POOL_WINDOWS = (2, 4, 8, 16)
LANES = 128
SUBLANES = 8
SCAN_CHUNKS = SUBLANES
S5_GROUPS_PER_STEP = 4
MM_TM_CAP, MM_TN_CAP, MM_TK_CAP = 1408, 1408, 2048
MM_TK_WHOLE = 4096
VMEM_LIMIT = 48 * 1024 * 1024
VMEM_LIMIT_S5 = 56 * 1024 * 1024

ADAM_LR, ADAM_B1, ADAM_B2, ADAM_EPS, ADAM_WD, ADAM_STEP = 0.001, 0.9, 0.999, 1e-08, 0.01, 10

WEIGHTS = ['norm_mix_g', 'even_w_in', 'even_conv_w', 'ssm_log_step', 'ssm_a_re', 'ssm_a_im', 'ssm_b_re',
           'ssm_b_im', 'ssm_c_re', 'ssm_c_im', 'ssm_d', 'ssm_glu_w', 'ssm_glu_b', 'even_w_out', 'odd_w_in',
           'pool_w', 'pool_scale', 'sgu_norm_g', 'sgu_w', 'sgu_b', 'odd_w_out', 'norm_ffn_g', 'ffn_w_up',
           'ffn_conv_w', 'ffn_conv_b', 'ffn_w_down', 'norm_final_g']
BIG = {'even_w_in': 2, 'ssm_glu_w': 1, 'even_w_out': 1, 'odd_w_in': 2, 'odd_w_out': 1, 'ffn_w_up': 2,
       'ffn_w_down': 1}
SMALL_SHARDED = ('even_conv_w', 'pool_scale', 'sgu_norm_g', 'ffn_conv_w')
SMALL = [n for n in WEIGHTS if n not in BIG]
N_CHIPS = 4
N_DEV = 8


def _cparams(sem=None, vmem=VMEM_LIMIT):
    kw = dict(vmem_limit_bytes=vmem)
    if sem is not None:
        kw['dimension_semantics'] = sem
    return pltpu.CompilerParams(**kw)


def _pick(n, segs=(), prefs=(1024, 512, 256, 128)):
    for t in prefs:
        if n % t == 0 and all(s % t == 0 for s in segs if s):
            return t
    return n


def _largest_tile(n, segs, cap):
    best = None
    for t in range(LANES, min(n, cap) + 1, LANES):
        if n % t == 0 and all(s % t == 0 for s in segs if s):
            best = t
    return best if best is not None else n


def _ldims(arr, kind):
    if kind is None:
        return arr.shape
    if kind[0] == 'lead':
        return arr.shape[1:]
    return (arr.shape[1], arr.shape[0] * arr.shape[2])


def _segw(arr, kind):
    return arr.shape[2] if (kind is not None and kind[0] == 'seg') else None


def _opspec(arr, kind, br, bc, rfn, cfn):
    if kind is None:
        return pl.BlockSpec((br, bc), lambda i, j, k: (rfn(i, j, k), cfn(i, j, k)))
    if kind[0] == 'lead':
        lead = kind[1]
        return pl.BlockSpec((None, br, bc), lambda i, j, k: (lead, rfn(i, j, k), cfn(i, j, k)))
    per = arr.shape[2] // bc
    return pl.BlockSpec((None, br, bc), lambda i, j, k: (cfn(i, j, k) // per, rfn(i, j, k), cfn(i, j, k) % per))


def _mm(a, b, mode, out_dtype, name, ak=None, bk=None, ok=None, res=None, dep=None):
    ar, ac = _ldims(a, ak)
    br_, bc_ = _ldims(b, bk)
    if mode == 'nn':
        M, K, N = ar, ac, bc_
        assert br_ == K
    else:
        M, K, N = ar, ac, br_
        assert bc_ == K
    sa, sb = _segw(a, ak), _segw(b, bk)
    so = (N // ok[1]) if ok is not None else None
    tm = _largest_tile(M, [], MM_TM_CAP)
    tn = _largest_tile(N, [sb if mode == 'nn' else None, so], MM_TN_CAP)
    ksegs = [sa, sb if mode == 'nt' else None]
    tk = K if (K <= MM_TK_WHOLE and not any(ksegs)) else _largest_tile(K, ksegs, MM_TK_CAP)
    nk = K // tk
    I = lambda i, j, k: i
    J = lambda i, j, k: j
    Kk = lambda i, j, k: k
    a_spec = _opspec(a, ak, tm, tk, I, Kk)
    if mode == 'nn':
        b_spec = _opspec(b, bk, tk, tn, Kk, J)
        dims = (((1,), (0,)), ((), ()))
    else:
        b_spec = _opspec(b, bk, tn, tk, J, Kk)
        dims = (((1,), (1,)), ((), ()))
    if ok is None:
        out_shape = jax.ShapeDtypeStruct((M, N), out_dtype)
        o_spec = pl.BlockSpec((tm, tn), lambda i, j, k: (i, j))
    else:
        out_shape = jax.ShapeDtypeStruct((ok[1], M, N // ok[1]), out_dtype)
        per = (N // ok[1]) // tn
        o_spec = pl.BlockSpec((None, tm, tn), lambda i, j, k: (j // per, i, j % per))
    has_res = res is not None

    def body(*refs):
        a_ref, b_ref = refs[0], refs[1]
        r_ref = refs[2] if has_res else None
        o_ref = refs[n_in]
        prod = lax.dot_general(a_ref[...].astype(BF16), b_ref[...].astype(BF16), dims, preferred_element_type=F32)
        if nk == 1:
            o_ref[...] = (prod + r_ref[...] if has_res else prod).astype(out_dtype)
            return
        acc = refs[-1]
        k = pl.program_id(2)

        @pl.when(k == 0)
        def _():
            acc[...] = prod

        @pl.when(k > 0)
        def _():
            acc[...] += prod

        @pl.when(k == nk - 1)
        def _():
            o = acc[...]
            if has_res:
                o = o + r_ref[...]
            o_ref[...] = o.astype(out_dtype)

    in_specs = [a_spec, b_spec]
    args = [a, b]
    if has_res:
        in_specs.append(pl.BlockSpec((tm, tn), lambda i, j, k: (i, j)))
        args.append(res)
    if dep is not None:
        in_specs.append(pl.BlockSpec(memory_space=pl.ANY))
        args.append(dep)
    n_in = len(args)
    return pl.pallas_call(
        body, name=name, out_shape=out_shape, grid=(M // tm, N // tn, nk), in_specs=in_specs, out_specs=o_spec,
        scratch_shapes=[pltpu.VMEM((tm, tn), F32)] if nk > 1 else [],
        compiler_params=_cparams(("parallel", "parallel", "arbitrary")),
    )(*args)


_G0 = math.sqrt(2.0 / math.pi)
_G1 = 0.044715


def _gelu(x):
    return 0.5 * x * (1.0 + jnp.tanh(_G0 * (x + _G1 * x * x * x)))


def _gelu_grad(x):
    x2 = x * x
    t = jnp.tanh(_G0 * (x + _G1 * x * x2))
    return 0.5 * (1.0 + t) + 0.5 * x * (1.0 - t * t) * (_G0 * (1.0 + 3.0 * _G1 * x2))


def _sigmoid(x):
    return 1.0 / (1.0 + jnp.exp(-x))


def _down(v, k):
    r = pltpu.roll(v, k, axis=0)
    row = lax.broadcasted_iota(jnp.int32, (SUBLANES, v.shape[1]), 0)
    return jnp.concatenate([jnp.where(row >= k, r[:SUBLANES], 0.0), r[SUBLANES:]], axis=0)


def _up(v, k):
    n = v.shape[0]
    r = pltpu.roll(v, n - k, axis=0)
    row = lax.broadcasted_iota(jnp.int32, (SUBLANES, v.shape[1]), 0)
    return jnp.concatenate([r[:n - SUBLANES], jnp.where(row < SUBLANES - k, r[n - SUBLANES:], 0.0)], axis=0)


def _taps(v):
    return _down(v, 2), _down(v, 1), v


def _conv3(taps, w):
    return w[0:1, :] * taps[0] + w[1:2, :] * taps[1] + w[2:3, :] * taps[2]


def _conv3_t(dv, w):
    return w[2:3, :] * dv + w[1:2, :] * _up(dv, 1) + w[0:1, :] * _up(dv, 2)


def _conv3_dw(dv, taps):
    return tuple(jnp.sum(dv * tp, axis=0, keepdims=True) for tp in taps)


def _cmul(ar, ai, br, bi):
    return ar * br - ai * bi, ar * bi + ai * br


def _cpow(lr, li, n):
    rr = ri = None
    br, bi = lr, li
    while n:
        if n & 1:
            rr, ri = (br, bi) if rr is None else _cmul(rr, ri, br, bi)
        n >>= 1
        if n:
            br, bi = _cmul(br, bi, br, bi)
    return rr, ri


NORM_ROWS = 256


def _norm_mm(x, g, b, out_dtype, name, ok=None):
    M, D = x.shape
    N = b.shape[1]
    so = (N // ok[1]) if ok is not None else None
    tm = _largest_tile(M, [], 1024)
    tn = _largest_tile(N, [so], MM_TN_CAP)
    if ok is None:
        out_shape = jax.ShapeDtypeStruct((M, N), out_dtype)
        o_spec = pl.BlockSpec((tm, tn), lambda i, j: (i, j))
    else:
        out_shape = jax.ShapeDtypeStruct((ok[1], M, N // ok[1]), out_dtype)
        per = (N // ok[1]) // tn
        o_spec = pl.BlockSpec((None, tm, tn), lambda i, j: (j // per, i, j % per))

    def body(x_ref, g_ref, b_ref, o_ref, ht_ref, h_scr):
        @pl.when(pl.program_id(1) == 0)
        def _():
            for c in range(tm // NORM_ROWS):
                rows = pl.ds(c * NORM_ROWS, NORM_ROWS)
                xv = x_ref[rows, :]
                h = xv * lax.rsqrt(jnp.mean(xv * xv, axis=-1, keepdims=True) + EPS) * g_ref[...]
                h_scr[rows, :] = h.astype(BF16)
                ht_ref[:, rows] = h.T.astype(BF16)

        o_ref[...] = jnp.dot(h_scr[...], b_ref[...], preferred_element_type=F32).astype(out_dtype)

    return pl.pallas_call(
        body, name=name, out_shape=(out_shape, jax.ShapeDtypeStruct((D, M), BF16)), grid=(M // tm, N // tn),
        in_specs=[pl.BlockSpec((tm, D), lambda i, j: (i, 0)), pl.BlockSpec((1, D), lambda i, j: (0, 0)),
                  pl.BlockSpec((D, tn), lambda i, j: (0, j))],
        out_specs=(o_spec, pl.BlockSpec((D, tm), lambda i, j: (0, i))),
        scratch_shapes=[pltpu.VMEM((tm, D), BF16)], compiler_params=_cparams(("parallel", "arbitrary")),
    )(x, g.reshape(1, D), b)


def _mm_norm_bwd(a, b, x, g, dres, name, ak=None, dep=None):
    M, K = _ldims(a, ak)
    D = b.shape[0]
    assert b.shape[1] == K and x.shape == (M, D)
    sa = _segw(a, ak)
    tm = _largest_tile(M, [], 1024)
    whole_segs = bool(sa) and K <= MM_TK_WHOLE
    tk = K if (K <= MM_TK_WHOLE) else _largest_tile(K, [sa], MM_TK_CAP)
    ni, nk = M // tm, K // tk
    if whole_segs:
        a_spec = pl.BlockSpec((a.shape[0], tm, sa), lambda i, k: (0, i, 0))
    else:
        a3 = _opspec(a, ak, tm, tk, lambda i, j, k: i, lambda i, j, k: k)
        a_spec = pl.BlockSpec(a3.block_shape, lambda i, k: a3.index_map(i, 0, k))
    n_in = 5 + (dep is not None)

    def body(*refs):
        a_ref, b_ref, x_ref, g_ref, r_ref = refs[:5]
        dx_ref, dxb_ref, dg_ref, acc, accg = refs[n_in:]
        i, k = pl.program_id(0), pl.program_id(1)
        av = jnp.concatenate([a_ref[s] for s in range(a.shape[0])], axis=1) if whole_segs else a_ref[...]
        prod = lax.dot_general(av.astype(BF16), b_ref[...], (((1,), (1,)), ((), ())), preferred_element_type=F32)

        @pl.when(k == 0)
        def _():
            acc[...] = prod

        @pl.when(k > 0)
        def _():
            acc[...] += prod

        @pl.when((i == 0) & (k == 0))
        def _():
            accg[...] = jnp.zeros_like(accg)

        @pl.when(k == nk - 1)
        def _():
            for c in range(tm // NORM_ROWS):
                rows = pl.ds(c * NORM_ROWS, NORM_ROWS)
                xv = x_ref[rows, :]
                r = lax.rsqrt(jnp.mean(xv * xv, axis=-1, keepdims=True) + EPS)
                xh = xv * r
                dhv = acc[rows, :]
                accg[...] += jnp.sum((dhv * xh).reshape(NORM_ROWS // SUBLANES, SUBLANES, D), axis=0)
                dxh = dhv * g_ref[...]
                dxv = r_ref[rows, :] + r * (dxh - xh * jnp.mean(dxh * xh, axis=-1, keepdims=True))
                dx_ref[rows, :] = dxv
                dxb_ref[rows, :] = dxv.astype(BF16)

        @pl.when((i == ni - 1) & (k == nk - 1))
        def _():
            dg_ref[...] = jnp.sum(accg[...], axis=0, keepdims=True)

    row = pl.BlockSpec((tm, D), lambda i, k: (i, 0))
    vec = pl.BlockSpec((1, D), lambda i, k: (0, 0))
    in_specs = [a_spec, pl.BlockSpec((D, tk), lambda i, k: (0, k)), row, vec, row]
    args = [a, b, x, g.reshape(1, D), dres]
    if dep is not None:
        in_specs.append(pl.BlockSpec(memory_space=pl.ANY))
        args.append(dep)
    return pl.pallas_call(
        body, name=name,
        out_shape=(jax.ShapeDtypeStruct((M, D), F32), jax.ShapeDtypeStruct((M, D), BF16),
                   jax.ShapeDtypeStruct((1, D), F32)),
        grid=(ni, nk), in_specs=in_specs, out_specs=(row, row, vec),
        scratch_shapes=[pltpu.VMEM((tm, D), F32), pltpu.VMEM((SUBLANES, D), F32)],
        compiler_params=_cparams(("arbitrary", "arbitrary"), VMEM_LIMIT_S5),
    )(*args)


def _loss_head(x, g, tgt):
    L, D = x.shape
    tr = _pick(L, prefs=(512, 256, 128))
    nsteps = L // tr

    def body(x_ref, g_ref, t_ref, loss_ref, dx_ref, dxb_ref, dg_ref, acc_g, acc_l):
        i = pl.program_id(0)

        @pl.when(i == 0)
        def _():
            acc_g[...] = jnp.zeros_like(acc_g)
            acc_l[...] = jnp.zeros_like(acc_l)

        xv = x_ref[...]
        gv = g_ref[...]
        r = lax.rsqrt(jnp.mean(xv * xv, axis=-1, keepdims=True) + EPS)
        xh = xv * r
        e = xh * gv - t_ref[...]
        acc_l[...] += jnp.sum((e * e).reshape(tr // SUBLANES, SUBLANES, D), axis=0)
        dy = e * (1.0 / D)
        acc_g[...] += jnp.sum((dy * xh).reshape(tr // SUBLANES, SUBLANES, D), axis=0)
        dxh = dy * gv
        dxv = r * (dxh - xh * jnp.mean(dxh * xh, axis=-1, keepdims=True))
        dx_ref[...] = dxv
        dxb_ref[...] = dxv.astype(BF16)

        @pl.when(i == nsteps - 1)
        def _():
            dg_ref[...] = jnp.sum(acc_g[...], axis=0, keepdims=True)
            tot = jnp.sum(jnp.sum(acc_l[...], axis=0, keepdims=True), axis=1, keepdims=True) * (0.5 / D)
            loss_ref[...] = jnp.broadcast_to(tot, (SUBLANES, LANES))

    row = pl.BlockSpec((tr, D), lambda i: (i, 0))
    vec = pl.BlockSpec((1, D), lambda i: (0, 0))
    return pl.pallas_call(
        body, name="loss_head",
        out_shape=(jax.ShapeDtypeStruct((SUBLANES, LANES), F32), jax.ShapeDtypeStruct((L, D), F32),
                   jax.ShapeDtypeStruct((L, D), BF16), jax.ShapeDtypeStruct((1, D), F32)),
        grid=(nsteps,), in_specs=[row, vec, row],
        out_specs=(pl.BlockSpec((SUBLANES, LANES), lambda i: (0, 0)), row, row, vec),
        scratch_shapes=[pltpu.VMEM((SUBLANES, D), F32), pltpu.VMEM((SUBLANES, D), F32)],
        compiler_params=_cparams(("arbitrary",)),
    )(x, g.reshape(1, D), tgt)


def _sconv_fwd(proj4, conv_w, name):
    _, L, C = proj4.shape
    cb = LANES

    def body(p_ref, w_ref, o_ref):
        xa, ba, ca = p_ref[0].astype(F32), p_ref[1].astype(F32), p_ref[2].astype(F32)
        o_ref[...] = (ba * _conv3(_taps(ca * xa), w_ref[...])).astype(BF16)

    return pl.pallas_call(
        body, name=name, out_shape=jax.ShapeDtypeStruct((L, 2 * C), BF16), grid=(C // cb,),
        in_specs=[pl.BlockSpec((3, L, cb), lambda j: (0, 0, j)), pl.BlockSpec((3, cb), lambda j: (0, j))],
        out_specs=pl.BlockSpec((L, cb), lambda j: (0, j)), compiler_params=_cparams(("parallel",)),
    )(proj4, conv_w)


def _sconv_bwd(proj4, dmix, conv_w, name):
    _, L, C = proj4.shape
    cb = LANES

    def body(p_ref, d_ref, w_ref, o_ref, dw_ref):
        xa, ba, ca = p_ref[0].astype(F32), p_ref[1].astype(F32), p_ref[2].astype(F32)
        w = w_ref[...]
        dya = d_ref[...]
        tq = _taps(ca * xa)
        cq = _conv3(tq, w)
        dcq = dya * ba
        dq = _conv3_t(dcq, w)
        for tap, dwt in enumerate(_conv3_dw(dcq, tq)):
            dw_ref[tap:tap + 1, :] = dwt
        o_ref[0] = (dq * ca).astype(BF16)
        o_ref[1] = (dya * cq).astype(BF16)
        o_ref[2] = (dq * xa).astype(BF16)

    return pl.pallas_call(
        body, name=name,
        out_shape=(jax.ShapeDtypeStruct((4, L, C), BF16), jax.ShapeDtypeStruct((3, C), F32)), grid=(C // cb,),
        in_specs=[pl.BlockSpec((3, L, cb), lambda j: (0, 0, j)), pl.BlockSpec((L, cb), lambda j: (0, j)),
                  pl.BlockSpec((3, cb), lambda j: (0, j))],
        out_specs=(pl.BlockSpec((3, L, cb), lambda j: (0, 0, j)), pl.BlockSpec((3, cb), lambda j: (0, j))),
        compiler_params=_cparams(("parallel",)),
    )(proj4, dmix, conv_w)


def _s5_prep(log_step, a_re, a_im, b_re, b_im, c_re, c_im):
    G, P = a_re.shape
    H = b_re.shape[-1]
    gs = S5_GROUPS_PER_STEP
    ns = G // gs
    gu = LANES // H
    lam = lax.complex(a_re, a_im)
    step = jnp.exp(log_step)[:, None]
    lam_bar = jnp.exp(lam * step)
    b_bar = ((lam_bar - 1.0) / lam)[..., None] * lax.complex(b_re, b_im)
    lr = jnp.real(lam_bar).reshape(ns, 1, gs * P)
    li = jnp.imag(lam_bar).reshape(ns, 1, gs * P)
    k = np.arange(ns)[:, None, None]
    oh = jnp.asarray((np.arange(gu)[None, :, None] == gs * (k % (gu // gs)) + np.arange(gs)[None, None, :]),
                     F32)
    bre = jnp.einsum('kgl,klph->kghlp', oh, jnp.real(b_bar).reshape(ns, gs, P, H)).reshape(ns, gu * H, gs * P)
    bim = jnp.einsum('kgl,klph->kghlp', oh, jnp.imag(b_bar).reshape(ns, gs, P, H)).reshape(ns, gu * H, gs * P)
    cre = jnp.einsum('kgl,klhp->klpgh', oh, c_re.reshape(ns, gs, H, P)).reshape(ns, gs * P, gu * H)
    cim = jnp.einsum('kgl,klhp->klpgh', oh, c_im.reshape(ns, gs, H, P)).reshape(ns, gs * P, gu * H)
    return lr, li, jnp.concatenate([bre, bim], axis=2), jnp.concatenate([cre, -cim], axis=1)


def _carry_tile(fr, fi, pr, pi, reverse):
    row = lax.broadcasted_iota(jnp.int32, fr.shape, 0)
    cr = jnp.zeros_like(fr)
    ci = jnp.zeros_like(fi)
    sr = jnp.zeros_like(fr[0:1])
    si = jnp.zeros_like(sr)
    order = range(SCAN_CHUNKS - 1, 0, -1) if reverse else range(0, SCAN_CHUNKS - 1)
    for c in order:
        fcr = jnp.sum(jnp.where(row == c, fr, 0.0), axis=0, keepdims=True)
        fci = jnp.sum(jnp.where(row == c, fi, 0.0), axis=0, keepdims=True)
        mr, mi = _cmul(pr, pi, sr, si)
        sr, si = mr + fcr, mi + fci
        nxt = c - 1 if reverse else c + 1
        cr = jnp.where(row == nxt, sr, cr)
        ci = jnp.where(row == nxt, si, ci)
    return cr, ci


def _scan_order_into(dst_ref, src_ref, T):
    for c in range(SCAN_CHUNKS):
        dst_ref[pl.ds(c, T, stride=SCAN_CHUNKS), :] = src_ref[pl.ds(c * T, T), :].astype(F32)


def _s5_fwd(proj4, lr, li, bmat, cmat, d, name):
    _, L, Du = proj4.shape
    ns, _, W2 = bmat.shape
    W = W2 // 2
    T = L // SCAN_CHUNKS
    rb = _pick(L, prefs=(512, 256, 128))
    per = (ns * LANES) // Du

    def body(ut_ref, lr_ref, li_ref, b_ref, c_ref, d_ref, y_ref, sr_ref, si_ref, u_ref):
        k = pl.program_id(0)
        _scan_order_into(u_ref, ut_ref, T)
        for r in range(L // rb):
            rows = pl.ds(r * rb, rb)
            bu = jnp.dot(u_ref[rows, :].astype(BF16), b_ref[...], preferred_element_type=F32)
            sr_ref[rows, :] = bu[:, :W]
            si_ref[rows, :] = bu[:, W:]
        lam_r = jnp.broadcast_to(lr_ref[...], (SUBLANES, W))
        lam_i = jnp.broadcast_to(li_ref[...], (SUBLANES, W))

        def local(t, carry):
            sr, si = carry
            rows = pl.ds(pl.multiple_of(t * SUBLANES, SUBLANES), SUBLANES)
            mr, mi = _cmul(lam_r, lam_i, sr, si)
            sr = mr + sr_ref[rows, :]
            si = mi + si_ref[rows, :]
            sr_ref[rows, :] = sr
            si_ref[rows, :] = si
            return sr, si

        z = jnp.zeros((SUBLANES, W), F32)
        fr, fi = lax.fori_loop(0, T, local, (z, z))
        pr, pi = _cpow(lam_r, lam_i, T)
        cr, ci = _carry_tile(fr, fi, pr[0:1], pi[0:1], reverse=False)

        def fix(t, carry):
            wr, wi = carry
            rows = pl.ds(pl.multiple_of(t * SUBLANES, SUBLANES), SUBLANES)
            ar, ai = _cmul(wr, wi, cr, ci)
            sr_ref[rows, :] += ar
            si_ref[rows, :] += ai
            return _cmul(wr, wi, lam_r, lam_i)

        lax.fori_loop(0, T, fix, (lam_r, lam_i))
        first = (k % per) == 0
        for r in range(L // rb):
            rows = pl.ds(r * rb, rb)
            s = jnp.concatenate([sr_ref[rows, :], si_ref[rows, :]], axis=1).astype(BF16)
            y = jnp.dot(s, c_ref[...], preferred_element_type=F32)

            @pl.when(first)
            def _():
                y_ref[rows, :] = y + d_ref[...] * u_ref[rows, :]

            @pl.when(jnp.logical_not(first))
            def _():
                y_ref[rows, :] += y

    ublk = pl.BlockSpec((L, LANES), lambda k: (0, k // per))
    sblk = pl.BlockSpec((L, W), lambda k: (0, k))
    lam = pl.BlockSpec((None, 1, W), lambda k: (k, 0, 0))
    return pl.pallas_call(
        body, name=name,
        out_shape=(jax.ShapeDtypeStruct((L, Du), F32), jax.ShapeDtypeStruct((L, ns * W), F32),
                   jax.ShapeDtypeStruct((L, ns * W), F32)),
        grid=(ns,),
        in_specs=[pl.BlockSpec((None, L, LANES), lambda k: (3, 0, k // per)), lam, lam,
                  pl.BlockSpec((None, LANES, 2 * W), lambda k: (k, 0, 0)),
                  pl.BlockSpec((None, 2 * W, LANES), lambda k: (k, 0, 0)),
                  pl.BlockSpec((1, LANES), lambda k: (0, k // per))],
        out_specs=(ublk, sblk, sblk), scratch_shapes=[pltpu.VMEM((L, LANES), F32)],
        compiler_params=_cparams(("arbitrary",), VMEM_LIMIT_S5),
    )(proj4, lr, li, bmat.astype(BF16), cmat.astype(BF16), d.reshape(1, Du))


def _s5_bwd(dy, proj4, dproj, s_re, s_im, lr, li, bmat, cmat, d, name):
    _, L, Du = proj4.shape
    ns, _, W2 = bmat.shape
    W = W2 // 2
    T = L // SCAN_CHUNKS
    rb = _pick(L, prefs=(512, 256, 128))
    per = (ns * LANES) // Du
    NT = (((1,), (1,)), ((), ()))
    TN = (((0,), (0,)), ((), ()))

    def body(dy_ref, ut_ref, dp_in, sr_ref, si_ref, lr_ref, li_ref, b_ref, c_ref, d_ref,
             dut_ref, db_ref, dc_ref, dl_ref, dd_ref, gr_ref, gi_ref, u_ref, du_ref):
        k = pl.program_id(0)
        _scan_order_into(u_ref, ut_ref, T)
        for r in range(L // rb):
            rows = pl.ds(r * rb, rb)
            g = lax.dot_general(dy_ref[rows, :].astype(BF16), c_ref[...], NT, preferred_element_type=F32)
            gr_ref[rows, :] = g[:, :W]
            gi_ref[rows, :] = g[:, W:]
        lam_r = jnp.broadcast_to(lr_ref[...], (SUBLANES, W))
        lam_i = -jnp.broadcast_to(li_ref[...], (SUBLANES, W))

        def local(i, carry):
            gr, gi = carry
            rows = pl.ds(pl.multiple_of((T - 1 - i) * SUBLANES, SUBLANES), SUBLANES)
            mr, mi = _cmul(lam_r, lam_i, gr, gi)
            gr = mr + gr_ref[rows, :]
            gi = mi + gi_ref[rows, :]
            gr_ref[rows, :] = gr
            gi_ref[rows, :] = gi
            return gr, gi

        z = jnp.zeros((SUBLANES, W), F32)
        fr, fi = lax.fori_loop(0, T, local, (z, z))
        pr, pi = _cpow(lam_r, lam_i, T)
        cr, ci = _carry_tile(fr, fi, pr[0:1], pi[0:1], reverse=True)

        def true_g(rows, wr, wi):
            ar, ai = _cmul(wr, wi, cr, ci)
            gr = gr_ref[rows, :] + ar
            gi = gi_ref[rows, :] + ai
            gr_ref[rows, :] = gr
            gi_ref[rows, :] = gi
            return gr, gi

        def fix(i, carry):
            wr, wi, ar_, ai_ = carry
            t = T - 1 - i
            rows = pl.ds(pl.multiple_of(t * SUBLANES, SUBLANES), SUBLANES)
            prev = pl.ds(pl.multiple_of((t - 1) * SUBLANES, SUBLANES), SUBLANES)
            gr, gi = true_g(rows, wr, wi)
            qr, qi = sr_ref[prev, :], si_ref[prev, :]
            ar_ = ar_ + gr * qr + gi * qi
            ai_ = ai_ + gi * qr - gr * qi
            wr, wi = _cmul(wr, wi, lam_r, lam_i)
            return wr, wi, ar_, ai_

        wr, wi, acc_r, acc_i = lax.fori_loop(0, T - 1, fix, (lam_r, lam_i, z, z))
        gr, gi = true_g(pl.ds(0, SUBLANES), wr, wi)
        last = pl.ds((T - 1) * SUBLANES, SUBLANES)
        row = lax.broadcasted_iota(jnp.int32, (SUBLANES, W), 0)
        qr = jnp.where(row >= 1, pltpu.roll(sr_ref[last, :], 1, axis=0), 0.0)
        qi = jnp.where(row >= 1, pltpu.roll(si_ref[last, :], 1, axis=0), 0.0)
        acc_r = acc_r + gr * qr + gi * qi
        acc_i = acc_i + gi * qr - gr * qi
        dl_ref[0:1, :] = jnp.sum(acc_r, axis=0, keepdims=True)
        dl_ref[1:2, :] = jnp.sum(acc_i, axis=0, keepdims=True)

        first = (k % per) == 0
        db = jnp.zeros((LANES, 2 * W), F32)
        dc = jnp.zeros((LANES, 2 * W), F32)
        dd = jnp.zeros((1, LANES), F32)
        for r in range(L // rb):
            rows = pl.ds(r * rb, rb)
            gb = jnp.concatenate([gr_ref[rows, :], gi_ref[rows, :]], axis=1).astype(BF16)
            sb = jnp.concatenate([sr_ref[rows, :], si_ref[rows, :]], axis=1).astype(BF16)
            dyv = dy_ref[rows, :]
            uv = u_ref[rows, :]
            du = lax.dot_general(gb, b_ref[...], NT, preferred_element_type=F32)
            db = db + lax.dot_general(uv.astype(BF16), gb, TN, preferred_element_type=F32)
            dc = dc + lax.dot_general(dyv.astype(BF16), sb, TN, preferred_element_type=F32)
            dd = dd + jnp.sum(dyv * uv, axis=0, keepdims=True)

            @pl.when(first)
            def _():
                du_ref[rows, :] = du + d_ref[...] * dyv

            @pl.when(jnp.logical_not(first))
            def _():
                du_ref[rows, :] += du

        db_ref[...] = db
        dc_ref[...] = dc

        @pl.when(first)
        def _():
            dd_ref[...] = dd

        @pl.when((k % per) == per - 1)
        def _():
            for c in range(SCAN_CHUNKS):
                dut_ref[pl.ds(c * T, T), :] = du_ref[pl.ds(c, T, stride=SCAN_CHUNKS), :].astype(BF16)

    ublk = pl.BlockSpec((L, LANES), lambda k: (0, k // per))
    uslab = pl.BlockSpec((None, L, LANES), lambda k: (3, 0, k // per))
    sblk = pl.BlockSpec((L, W), lambda k: (0, k))
    lam = pl.BlockSpec((None, 1, W), lambda k: (k, 0, 0))
    vec = pl.BlockSpec((1, LANES), lambda k: (0, k // per))
    mat = pl.BlockSpec((None, LANES, 2 * W), lambda k: (k, 0, 0))
    return pl.pallas_call(
        body, name=name,
        out_shape=(jax.ShapeDtypeStruct(dproj.shape, dproj.dtype), jax.ShapeDtypeStruct((ns, LANES, 2 * W), F32),
                   jax.ShapeDtypeStruct((ns, LANES, 2 * W), F32), jax.ShapeDtypeStruct((ns, 2, W), F32),
                   jax.ShapeDtypeStruct((1, Du), F32)),
        grid=(ns,),
        in_specs=[ublk, uslab, pl.BlockSpec(memory_space=pl.ANY), sblk, sblk, lam, lam, mat,
                  pl.BlockSpec((None, 2 * W, LANES), lambda k: (k, 0, 0)), vec],
        out_specs=(uslab, mat, mat, pl.BlockSpec((None, 2, W), lambda k: (k, 0, 0)), vec),
        scratch_shapes=[pltpu.VMEM((L, W), F32), pltpu.VMEM((L, W), F32), pltpu.VMEM((L, LANES), F32),
                        pltpu.VMEM((L, LANES), F32)],
        input_output_aliases={2: 0}, compiler_params=_cparams(("arbitrary",), VMEM_LIMIT_S5),
    )(dy, proj4, dproj, s_re, s_im, lr, li, bmat.astype(BF16), cmat.astype(BF16), d.reshape(1, Du))


def _glu_fwd(yraw, wmat, bias, mixin, name):
    L, C = yraw.shape
    tr = _pick(L, prefs=(512, 256, 128))
    tb = tr // SCAN_CHUNKS
    nl = C // LANES

    def body(y_ref, w_ref, b_ref, m_in, o_ref, scr):
        yg = _gelu(y_ref[...])
        zz = jnp.dot(yg.astype(BF16), w_ref[...], preferred_element_type=F32) + b_ref[...]
        yb = yg * _sigmoid(zz)
        for k in range(nl):
            scr[k] = yb[:, k * LANES:(k + 1) * LANES]
        for c in range(SCAN_CHUNKS):
            for k in range(nl):
                o_ref[c, :, k * LANES:(k + 1) * LANES] = scr[k, pl.ds(c, tb, stride=SCAN_CHUNKS), :].astype(BF16)

    out = pl.pallas_call(
        body, name=name, out_shape=jax.ShapeDtypeStruct((SCAN_CHUNKS, L // SCAN_CHUNKS, 2 * C), BF16),
        grid=(L // tr,),
        in_specs=[pl.BlockSpec((tr, C), lambda i: (i, 0)), pl.BlockSpec((C, C), lambda i: (0, 0)),
                  pl.BlockSpec((1, C), lambda i: (0, 0)), pl.BlockSpec(memory_space=pl.ANY)],
        out_specs=pl.BlockSpec((SCAN_CHUNKS, tb, C), lambda i: (0, i, 1)),
        scratch_shapes=[pltpu.VMEM((nl, tr, LANES), F32)], input_output_aliases={3: 0},
        compiler_params=_cparams(("parallel",)),
    )(yraw, wmat, bias.reshape(1, C), mixin.reshape(SCAN_CHUNKS, L // SCAN_CHUNKS, 2 * C))
    return out.reshape(L, 2 * C)


def _glu_bwd(yraw, dmix, wmat, bias, name):
    L, C = yraw.shape
    tr = _pick(L, prefs=(512, 256, 128))
    nsteps = L // tr
    tb = tr // SCAN_CHUNKS
    nl = C // LANES

    def body(y_ref, d_ref, w_ref, b_ref, dy_ref, dw_ref, db_ref, acc_b, scr):
        i = pl.program_id(0)

        @pl.when(i == 0)
        def _():
            dw_ref[...] = jnp.zeros_like(dw_ref)
            acc_b[...] = jnp.zeros_like(acc_b)

        for c in range(SCAN_CHUNKS):
            for k in range(nl):
                scr[k, pl.ds(c, tb, stride=SCAN_CHUNKS), :] = d_ref[c, :, k * LANES:(k + 1) * LANES]
        yr = y_ref[...]
        yg = _gelu(yr)
        ygb = yg.astype(BF16)
        sg = _sigmoid(jnp.dot(ygb, w_ref[...], preferred_element_type=F32) + b_ref[...])
        dyb_ = jnp.concatenate([scr[k] for k in range(nl)], axis=1)
        dz = dyb_ * yg * sg * (1.0 - sg)
        dzb = dz.astype(BF16)
        dyg = dyb_ * sg + lax.dot_general(dzb, w_ref[...], (((1,), (1,)), ((), ())), preferred_element_type=F32)
        dw_ref[...] += lax.dot_general(ygb, dzb, (((0,), (0,)), ((), ())), preferred_element_type=F32)
        acc_b[...] += jnp.sum(dz.reshape(tr // SUBLANES, SUBLANES, C), axis=0)
        dy_ref[...] = dyg * _gelu_grad(yr)

        @pl.when(i == nsteps - 1)
        def _():
            db_ref[...] = jnp.sum(acc_b[...], axis=0, keepdims=True)

    row = pl.BlockSpec((tr, C), lambda i: (i, 0))
    return pl.pallas_call(
        body, name=name,
        out_shape=(jax.ShapeDtypeStruct((L, C), F32), jax.ShapeDtypeStruct((C, C), F32),
                   jax.ShapeDtypeStruct((1, C), F32)),
        grid=(nsteps,),
        in_specs=[row, pl.BlockSpec((SCAN_CHUNKS, tb, C), lambda i: (0, i, 1)), pl.BlockSpec((C, C), lambda i: (0, 0)),
                  pl.BlockSpec((1, C), lambda i: (0, 0))],
        out_specs=(row, pl.BlockSpec((C, C), lambda i: (0, 0)), pl.BlockSpec((1, C), lambda i: (0, 0))),
        scratch_shapes=[pltpu.VMEM((SUBLANES, C), F32), pltpu.VMEM((nl, tr, LANES), F32)],
        compiler_params=_cparams(("arbitrary",)),
    )(yraw, dmix.reshape(SCAN_CHUNKS, L // SCAN_CHUNKS, 2 * C), wmat, bias.reshape(1, C))


def _pool_counts(L, g):
    t = lax.broadcasted_iota(jnp.int32, (L, LANES), 0).astype(F32) + 1.0
    w = jnp.where(g == 0, 2.0, jnp.where(g == 1, 4.0, jnp.where(g == 2, 8.0, 16.0)))
    return 1.0 / jnp.minimum(t, w)


def _select_window(g, a2, a4, a8, a16):
    return jnp.where(g == 0, a2, jnp.where(g == 1, a4, jnp.where(g == 2, a8, a16)))


def _pooled(z, g):
    a2 = z + _down(z, 1)
    a4 = a2 + _down(a2, 2)
    a8 = a4 + _down(a4, 4)
    a16 = a8 + _down(a8, 8)
    return _select_window(g, a2, a4, a8, a16) * _pool_counts(z.shape[0], g) - z


def _transpose_on_mxu(yb):
    c = yb.shape[1]
    eye = lax.broadcasted_iota(jnp.int32, (c, c), 0) == lax.broadcasted_iota(jnp.int32, (c, c), 1)
    return lax.dot_general(eye.astype(BF16), yb, (((1,), (1,)), ((), ())), preferred_element_type=F32).astype(BF16)


def _pool_fwd(proj3, pool_w, scale, name):
    _, L, C = proj3.shape
    ng = len(POOL_WINDOWS)
    pg = C // ng
    assert pg == LANES

    def body(z_ref, w_ref, s_ref, o_ref, ot_ref):
        g = pl.program_id(0)
        p = _pooled(z_ref[...].astype(F32), g)
        y = jnp.dot(p.astype(BF16), w_ref[...].astype(BF16), preferred_element_type=F32)
        yb = (y * s_ref[...]).astype(BF16)
        o_ref[...] = yb
        ot_ref[...] = _transpose_on_mxu(yb)

    return pl.pallas_call(
        body, name=name, out_shape=(jax.ShapeDtypeStruct((L, 2 * C), BF16), jax.ShapeDtypeStruct((2 * C, L), BF16)),
        grid=(ng,),
        in_specs=[pl.BlockSpec((None, L, pg), lambda g: (0, 0, g)), pl.BlockSpec((None, pg, pg), lambda g: (g, 0, 0)),
                  pl.BlockSpec((1, pg), lambda g: (0, g))],
        out_specs=(pl.BlockSpec((L, pg), lambda g: (0, g)), pl.BlockSpec((pg, L), lambda g: (g, 0))),
        compiler_params=_cparams(("parallel",)),
    )(proj3, pool_w, scale.reshape(1, C))


def _pool_bwd(proj3, dmix, pool_w, scale, name):
    _, L, C = proj3.shape
    ng = len(POOL_WINDOWS)
    pg = C // ng

    def body(z_ref, d_ref, w_ref, s_ref, dz_ref, dw_ref, ds_ref):
        g = pl.program_id(0)
        p = _pooled(z_ref[...].astype(F32), g)
        pb = p.astype(BF16)
        wb = w_ref[...].astype(BF16)
        pre = jnp.dot(pb, wb, preferred_element_type=F32)
        dyc = d_ref[...]
        ds_ref[...] = jnp.sum(dyc * pre, axis=0, keepdims=True)
        dpre = (dyc * s_ref[...]).astype(BF16)
        dw_ref[...] = lax.dot_general(pb, dpre, (((0,), (0,)), ((), ())), preferred_element_type=F32)
        dp = lax.dot_general(dpre, wb, (((1,), (1,)), ((), ())), preferred_element_type=F32)
        v = dp * _pool_counts(L, g)
        a2 = v + _up(v, 1)
        a4 = a2 + _up(a2, 2)
        a8 = a4 + _up(a4, 4)
        a16 = a8 + _up(a8, 8)
        dz_ref[...] = (_select_window(g, a2, a4, a8, a16) - dp).astype(BF16)

    return pl.pallas_call(
        body, name=name,
        out_shape=(jax.ShapeDtypeStruct((L, C), BF16), jax.ShapeDtypeStruct((ng, pg, pg), F32),
                   jax.ShapeDtypeStruct((1, C), F32)),
        grid=(ng,),
        in_specs=[pl.BlockSpec((None, L, pg), lambda g: (0, 0, g)), pl.BlockSpec((L, pg), lambda g: (0, g)),
                  pl.BlockSpec((None, pg, pg), lambda g: (g, 0, 0)), pl.BlockSpec((1, pg), lambda g: (0, g))],
        out_specs=(pl.BlockSpec((L, pg), lambda g: (0, g)), pl.BlockSpec((None, pg, pg), lambda g: (g, 0, 0)),
                   pl.BlockSpec((1, pg), lambda g: (0, g))),
        compiler_params=_cparams(("parallel",)),
    )(proj3, dmix, pool_w, scale.reshape(1, C))


def _tril_w(w_ref, h):
    r = lax.broadcasted_iota(jnp.int32, (CHUNK, CHUNK), 0)
    c = lax.broadcasted_iota(jnp.int32, (CHUNK, CHUNK), 1)
    return jnp.where(r >= c, w_ref[h], 0.0)


def _sgu_fwd(proj3, norm_g, w, b, mixin, mixin_t, name):
    _, L, C = proj3.shape
    nh = w.shape[0]
    dh = C // nh
    assert dh == LANES and w.shape[1] == CHUNK
    tr = _pick(L, prefs=(512, 256, 128))
    bfull = jnp.broadcast_to(b[:, :, None], (nh, CHUNK, dh))

    def body(su_ref, sv_ref, g_ref, w_ref, b_ref, m_in, mt_in, o_ref, ot_ref):
        sv = _gelu(sv_ref[...].astype(F32))
        r = lax.rsqrt(jnp.mean(sv * sv, axis=-1, keepdims=True) + EPS)
        v = (sv * r * g_ref[...]).astype(BF16)
        for h in range(nh):
            wm = _tril_w(w_ref, h).astype(BF16)
            cols = slice(h * dh, (h + 1) * dh)
            for n in range(tr // CHUNK):
                rows = slice(n * CHUNK, (n + 1) * CHUNK)
                mixed = jnp.dot(wm, v[rows, cols], preferred_element_type=F32) + b_ref[h]
                o_ref[rows, cols] = (_gelu(su_ref[rows, cols].astype(F32)) * mixed).astype(BF16)
        ot_ref[...] = _transpose_on_mxu(o_ref[...])

    full = lambda shp: pl.BlockSpec(shp, lambda i: (0,) * len(shp))
    anywhere = pl.BlockSpec(memory_space=pl.ANY)
    return pl.pallas_call(
        body, name=name, out_shape=(jax.ShapeDtypeStruct(mixin.shape, BF16), jax.ShapeDtypeStruct(mixin_t.shape, BF16)),
        grid=(L // tr,),
        in_specs=[pl.BlockSpec((None, tr, C), lambda i: (1, i, 0)), pl.BlockSpec((None, tr, C), lambda i: (2, i, 0)),
                  full((1, C)), full((nh, CHUNK, CHUNK)), full((nh, CHUNK, dh)), anywhere, anywhere],
        out_specs=(pl.BlockSpec((tr, C), lambda i: (i, 1)), pl.BlockSpec((C, tr), lambda i: (1, i))),
        input_output_aliases={5: 0, 6: 1}, compiler_params=_cparams(("parallel",)),
    )(proj3, proj3, norm_g.reshape(1, C), w, bfull, mixin, mixin_t)


def _sgu_bwd(proj3, dmix, dz, norm_g, w, b, name):
    _, L, C = proj3.shape
    nh = w.shape[0]
    dh = C // nh
    tr = _pick(L, prefs=(512, 256, 128))
    nsteps = L // tr
    bfull = jnp.broadcast_to(b[:, :, None], (nh, CHUNK, dh))

    def body(su_ref, sv_ref, d_ref, dz_ref, g_ref, w_ref, b_ref, o_ref, dw_ref, db_ref, dg_ref, dv_ref, acc_g):
        i = pl.program_id(0)
        o_ref[0] = dz_ref[...]

        @pl.when(i == 0)
        def _():
            dw_ref[...] = jnp.zeros_like(dw_ref)
            db_ref[...] = jnp.zeros_like(db_ref)
            acc_g[...] = jnp.zeros_like(acc_g)

        svp = sv_ref[...].astype(F32)
        sv = _gelu(svp)
        r = lax.rsqrt(jnp.mean(sv * sv, axis=-1, keepdims=True) + EPS)
        vh = sv * r
        gv = g_ref[...]
        v = (vh * gv).astype(BF16)
        tri_r = lax.broadcasted_iota(jnp.int32, (CHUNK, CHUNK), 0)
        tri_c = lax.broadcasted_iota(jnp.int32, (CHUNK, CHUNK), 1)
        for h in range(nh):
            wm = _tril_w(w_ref, h).astype(BF16)
            cols = slice(h * dh, (h + 1) * dh)
            dwh = jnp.zeros((CHUNK, CHUNK), F32)
            dbh = jnp.zeros((CHUNK, dh), F32)
            for n in range(tr // CHUNK):
                rows = slice(n * CHUNK, (n + 1) * CHUNK)
                vb = v[rows, cols]
                mixed = jnp.dot(wm, vb, preferred_element_type=F32) + b_ref[h]
                sup = su_ref[rows, cols].astype(F32)
                dyd = d_ref[rows, cols]
                dmx = dyd * _gelu(sup)
                o_ref[1, rows, cols] = (dyd * mixed * _gelu_grad(sup)).astype(BF16)
                dmb = dmx.astype(BF16)
                dwh = dwh + lax.dot_general(dmb, vb, (((1,), (1,)), ((), ())), preferred_element_type=F32)
                dbh = dbh + dmx
                dv_ref[rows, cols] = lax.dot_general(wm, dmb, (((0,), (0,)), ((), ())), preferred_element_type=F32)
            dw_ref[h] += jnp.where(tri_r >= tri_c, dwh, 0.0)
            db_ref[h] += dbh
        dv = dv_ref[...]
        acc_g[...] += jnp.sum((dv * vh).reshape(tr // SUBLANES, SUBLANES, C), axis=0)
        dvg = dv * gv
        dsv = r * (dvg - vh * jnp.mean(dvg * vh, axis=-1, keepdims=True))
        o_ref[2] = (dsv * _gelu_grad(svp)).astype(BF16)

        @pl.when(i == nsteps - 1)
        def _():
            dg_ref[...] = jnp.sum(acc_g[...], axis=0, keepdims=True)

    full = lambda shp: pl.BlockSpec(shp, lambda i: (0,) * len(shp))
    return pl.pallas_call(
        body, name=name,
        out_shape=(jax.ShapeDtypeStruct((3, L, C), BF16), jax.ShapeDtypeStruct((nh, CHUNK, CHUNK), F32),
                   jax.ShapeDtypeStruct((nh, CHUNK, dh), F32), jax.ShapeDtypeStruct((1, C), F32)),
        grid=(nsteps,),
        in_specs=[pl.BlockSpec((None, tr, C), lambda i: (1, i, 0)), pl.BlockSpec((None, tr, C), lambda i: (2, i, 0)),
                  pl.BlockSpec((tr, C), lambda i: (i, 1)), pl.BlockSpec((tr, C), lambda i: (i, 0)), full((1, C)),
                  full((nh, CHUNK, CHUNK)), full((nh, CHUNK, dh))],
        out_specs=(pl.BlockSpec((3, tr, C), lambda i: (0, i, 0)), full((nh, CHUNK, CHUNK)), full((nh, CHUNK, dh)),
                   full((1, C))),
        scratch_shapes=[pltpu.VMEM((tr, C), F32), pltpu.VMEM((SUBLANES, C), F32)],
        compiler_params=_cparams(("arbitrary",)),
    )(proj3, proj3, dmix, dz, norm_g.reshape(1, C), w, bfull)


def _ffn_act_fwd(up3, conv_w, conv_b, name):
    _, L, Fh = up3.shape
    cb = LANES
    w2 = conv_w.reshape(3, 2, Fh).transpose(1, 0, 2)
    b2 = conv_b.reshape(2, 1, Fh)

    def body(u_ref, w_ref, b_ref, o_ref, ot_ref, gv_ref):
        g = _conv3(_taps(u_ref[0].astype(F32)), w_ref[0]) + b_ref[0]
        v = _conv3(_taps(u_ref[1].astype(F32)), w_ref[1]) + b_ref[1]
        gv_ref[0] = g.astype(BF16)
        gv_ref[1] = v.astype(BF16)
        ab = (g * _sigmoid(g) * v).astype(BF16)
        o_ref[...] = ab
        ot_ref[...] = _transpose_on_mxu(ab)

    blk3 = pl.BlockSpec((2, L, cb), lambda j: (0, 0, j))
    return pl.pallas_call(
        body, name=name,
        out_shape=(jax.ShapeDtypeStruct((L, Fh), BF16), jax.ShapeDtypeStruct((Fh, L), BF16),
                   jax.ShapeDtypeStruct((2, L, Fh), BF16)),
        grid=(Fh // cb,),
        in_specs=[blk3, pl.BlockSpec((2, 3, cb), lambda j: (0, 0, j)), pl.BlockSpec((2, 1, cb), lambda j: (0, 0, j))],
        out_specs=(pl.BlockSpec((L, cb), lambda j: (0, j)), pl.BlockSpec((cb, L), lambda j: (j, 0)), blk3),
        compiler_params=_cparams(("parallel",)),
    )(up3, w2, b2)


def _ffn_act_bwd(up3, gv3, da, conv_w, h2t, name):
    _, L, Fh = up3.shape
    D = h2t.shape[0]
    cb = LANES
    nb = Fh // cb
    w2 = conv_w.reshape(3, 2, Fh).transpose(1, 0, 2)

    def body(u_ref, gv_ref, d_ref, w_ref, h_ref, o_ref, dw_ref, db_ref, wg_ref, wv_ref, scr):
        j = pl.program_id(0)

        @pl.when(j == 0)
        def _():
            scr[1] = jnp.zeros((2, L, cb), BF16)

        prev = scr.at[(j + 1) % 2]
        wg_ref[...] = jnp.dot(h_ref[...], prev[0], preferred_element_type=F32).astype(BF16)
        wv_ref[...] = jnp.dot(h_ref[...], prev[1], preferred_element_type=F32).astype(BF16)
        tg, tv = _taps(u_ref[0].astype(F32)), _taps(u_ref[1].astype(F32))
        wg, wv = w_ref[0], w_ref[1]
        g = gv_ref[0].astype(F32)
        v = gv_ref[1].astype(F32)
        sg = _sigmoid(g)
        dav = d_ref[...].astype(F32)
        dg = dav * v * (sg * (1.0 + g * (1.0 - sg)))
        dv = dav * (g * sg)
        dug = _conv3_t(dg, wg).astype(BF16)
        duv = _conv3_t(dv, wv).astype(BF16)
        o_ref[0] = dug
        o_ref[1] = duv
        cur = scr.at[j % 2]
        cur[0] = dug
        cur[1] = duv
        for tap, (dwg, dwv) in enumerate(zip(_conv3_dw(dg, tg), _conv3_dw(dv, tv))):
            dw_ref[0, tap:tap + 1, :] = dwg
            dw_ref[1, tap:tap + 1, :] = dwv
        db_ref[0] = jnp.sum(dg, axis=0, keepdims=True)
        db_ref[1] = jnp.sum(dv, axis=0, keepdims=True)

    here = lambda j: jnp.minimum(j, nb - 1)
    before = lambda j: jnp.maximum(j - 1, 0)
    blk3 = pl.BlockSpec((2, L, cb), lambda j: (0, 0, here(j)))
    dup, dw2, db2, dwg, dwv = pl.pallas_call(
        body, name=name,
        out_shape=(jax.ShapeDtypeStruct((2, L, Fh), BF16), jax.ShapeDtypeStruct((2, 3, Fh), F32),
                   jax.ShapeDtypeStruct((2, 1, Fh), F32), jax.ShapeDtypeStruct((D, Fh), BF16),
                   jax.ShapeDtypeStruct((D, Fh), BF16)),
        grid=(nb + 1,),
        in_specs=[blk3, blk3, pl.BlockSpec((L, cb), lambda j: (0, here(j))),
                  pl.BlockSpec((2, 3, cb), lambda j: (0, 0, here(j))), pl.BlockSpec((D, L), lambda j: (0, 0))],
        out_specs=(blk3, pl.BlockSpec((2, 3, cb), lambda j: (0, 0, here(j))),
                   pl.BlockSpec((2, 1, cb), lambda j: (0, 0, here(j))),
                   pl.BlockSpec((D, cb), lambda j: (0, before(j))), pl.BlockSpec((D, cb), lambda j: (0, before(j)))),
        scratch_shapes=[pltpu.VMEM((2, 2, L, cb), BF16)],
        compiler_params=_cparams(("arbitrary",), VMEM_LIMIT_S5),
    )(up3, gv3, da, w2, h2t)
    return dup, dw2.transpose(1, 0, 2).reshape(3, 2 * Fh), db2.reshape(2 * Fh), jnp.concatenate([dwg, dwv], axis=1)


def _local_step(x, tgt, w, layer_weights, on_layer_grads):
    L, D = x.shape
    depth = w['norm_mix_g'].shape[0]
    saved = []
    for i in range(depth):
        j = i // 2
        wb = dict(layer_weights(2 * i, x))
        s = {'x': x, 'wb': wb}
        if i % 2 == 0:
            proj4, s['hT'] = _norm_mm(x, w['norm_mix_g'][i], wb['even_w_in'], BF16, "even_in_fwd", ok=('seg', 4))
            s['proj'] = proj4
            mixin = _sconv_fwd(proj4, w['even_conv_w'][j], "sconv_fwd")
            prm = (w['ssm_log_step'][j], w['ssm_a_re'][j], w['ssm_a_im'][j], w['ssm_b_re'][j], w['ssm_b_im'][j],
                   w['ssm_c_re'][j], w['ssm_c_im'][j])
            (lr, li, bmat, cmat), prep_vjp = jax.vjp(_s5_prep, *prm)
            yraw, s_re, s_im = _s5_fwd(proj4, lr, li, bmat, cmat, w['ssm_d'][j], "s5_fwd")
            mixin = _glu_fwd(yraw, wb['ssm_glu_w'], w['ssm_glu_b'][j], mixin, "glu_fwd")
            s.update(yraw=yraw, s_re=s_re, s_im=s_im, s5=(lr, li, bmat, cmat), prep_vjp=prep_vjp)
            s['mixinT'] = mixin.T
            x = _mm(mixin, wb['even_w_out'], 'nn', F32, "even_out_fwd", res=x)
        else:
            proj3, s['hT'] = _norm_mm(x, w['norm_mix_g'][i], wb['odd_w_in'], BF16, "odd_in_fwd", ok=('seg', 3))
            s['proj'] = proj3
            mixin, mixin_t = _pool_fwd(proj3, w['pool_w'][j], w['pool_scale'][j], "pool_fwd")
            mixin, s['mixinT'] = _sgu_fwd(proj3, w['sgu_norm_g'][j], w['sgu_w'][j], w['sgu_b'][j], mixin, mixin_t,
                                          "sgu_fwd")
            x = _mm(mixin, wb['odd_w_out'], 'nn', F32, "odd_out_fwd", res=x)
        s['x1'] = x
        wb.update(layer_weights(2 * i + 1, x))
        up3, h2t = _norm_mm(x, w['norm_ffn_g'][i], wb['ffn_w_up'], BF16, "ffn_up_fwd", ok=('seg', 2))
        a, at, gv3 = _ffn_act_fwd(up3, w['ffn_conv_w'][i], w['ffn_conv_b'][i], "ffn_act_fwd")
        x = _mm(a, wb['ffn_w_down'], 'nn', F32, "ffn_down_fwd", res=x)
        s.update(h2T=h2t, up3=up3, aT=at, gv3=gv3)
        saved.append(s)

    loss8, dx, dxb, dg_final = _loss_head(x, w['norm_final_g'], tgt)
    gs = {n: [None] * w[n].shape[0] for n in SMALL if n != 'norm_final_g'}
    gs['norm_final_g'] = dg_final.reshape(D)

    dep = None
    for i in reversed(range(depth)):
        j = i // 2
        s = saved[i]
        wb = s['wb']
        gb = {}
        da = _mm(dxb, wb['ffn_w_down'], 'nt', BF16, "ffn_down_dgrad", dep=dep)
        gb['ffn_w_down'] = _mm(s['aT'], dxb, 'nn', BF16, "ffn_down_wgrad")
        dup3, dcw, dcb, gb['ffn_w_up'] = _ffn_act_bwd(s['up3'], s['gv3'], da, w['ffn_conv_w'][i], s['h2T'],
                                                      "ffn_act_bwd")
        gs['ffn_conv_w'][i], gs['ffn_conv_b'][i] = dcw, dcb
        dep = on_layer_grads(2 * i + 1, gb)
        dx, dxb, dg = _mm_norm_bwd(dup3, wb['ffn_w_up'], s['x1'], w['norm_ffn_g'][i], dx, "ffn_up_dgrad",
                              ak=('seg', 2), dep=dep)
        gs['norm_ffn_g'][i] = dg.reshape(D)
        gb = {}
        if i % 2 == 0:
            dmix = _mm(dxb, wb['even_w_out'], 'nt', F32, "even_out_dgrad")
            gb['even_w_out'] = _mm(s['mixinT'], dxb, 'nn', BF16, "even_out_wgrad")
            dproj, dcw = _sconv_bwd(s['proj'], dmix, w['even_conv_w'][j], "sconv_bwd")
            gs['even_conv_w'][j] = dcw
            dyraw, dglu_w, dglu_b = _glu_bwd(s['yraw'], dmix, wb['ssm_glu_w'], w['ssm_glu_b'][j], "glu_bwd")
            gb['ssm_glu_w'] = dglu_w.astype(BF16)
            gs['ssm_glu_b'][j] = dglu_b.reshape(-1)
            lr, li, bmat, cmat = s['s5']
            dproj, dbm, dcm, dlam, dd = _s5_bwd(dyraw, s['proj'], dproj, s['s_re'], s['s_im'], lr, li, bmat, cmat,
                                               w['ssm_d'][j], "s5_bwd")
            gs['ssm_d'][j] = dd.reshape(-1)
            dcm = jnp.swapaxes(dcm, 1, 2)
            dprm = s['prep_vjp']((dlam[:, 0:1, :], dlam[:, 1:2, :], dbm, dcm))
            for n, gval in zip(('ssm_log_step', 'ssm_a_re', 'ssm_a_im', 'ssm_b_re', 'ssm_b_im', 'ssm_c_re',
                                'ssm_c_im'), dprm):
                gs[n][j] = gval
            gb['even_w_in'] = _mm(s['hT'], dproj, 'nn', BF16, "even_in_wgrad", bk=('seg', 4))
            w_in, in_kind, in_name = wb['even_w_in'], ('seg', 4), "even_in_dgrad"
        else:
            dmix = _mm(dxb, wb['odd_w_out'], 'nt', F32, "odd_out_dgrad")
            gb['odd_w_out'] = _mm(s['mixinT'], dxb, 'nn', BF16, "odd_out_wgrad")
            dz, dpw, dps = _pool_bwd(s['proj'], dmix, w['pool_w'][j], w['pool_scale'][j], "pool_bwd")
            gs['pool_w'][j], gs['pool_scale'][j] = dpw, dps.reshape(-1)
            dproj, dsw, dsb, dsg = _sgu_bwd(s['proj'], dmix, dz, w['sgu_norm_g'][j], w['sgu_w'][j], w['sgu_b'][j],
                                            "sgu_bwd")
            gs['sgu_w'][j], gs['sgu_b'][j], gs['sgu_norm_g'][j] = dsw, jnp.sum(dsb, axis=-1), dsg.reshape(-1)
            gb['odd_w_in'] = _mm(s['hT'], dproj, 'nn', BF16, "odd_in_wgrad", bk=('seg', 3))
            w_in, in_kind, in_name = wb['odd_w_in'], ('seg', 3), "odd_in_dgrad"
        dep = on_layer_grads(2 * i, gb)
        dx, dxb, dg = _mm_norm_bwd(dproj, w_in, s['x'], w['norm_mix_g'][i], dx, in_name, ak=in_kind, dep=dep)
        gs['norm_mix_g'][i] = dg.reshape(D)

    gsmall = {n: (v if n == 'norm_final_g' else jnp.stack(v)) for n, v in gs.items()}
    return loss8[0, 0], dx, gsmall


_HBM = pl.BlockSpec(memory_space=pltpu.HBM)
_CHIP_FLIPS = ((0, 0), (1, 0), (0, 1), (1, 1))


def _coords():
    return lax.axis_index("x"), lax.axis_index("y"), lax.axis_index("c")


def _flip(v, f):
    return 1 - v if f else v


def _shard_of(ref, axis, s, width):
    start = pl.multiple_of(s * width, LANES if axis == ref.ndim - 1 else 16) if width % 16 == 0 else s * width
    idx = [slice(None)] * ref.ndim
    idx[axis] = pl.ds(start, width)
    return ref.at[tuple(idx)]


_SEM = pl.BlockSpec(memory_space=pltpu.SEMAPHORE)
_ANY = pl.BlockSpec(memory_space=pl.ANY)
_DATAFLOW = pltpu.SideEffectType.DATAFLOW_SIDE_EFFECTING


def _in_hbm(a):
    return pltpu.with_memory_space_constraint(a, pltpu.HBM)


def _model_layer(name, l):
    if name.startswith('ffn'):
        return l
    return 2 * l + 1 if name.startswith('odd') else 2 * l


def _place_quarter(shard, l, axis, chip, dtype, dep=None):
    _, r, c = shard.shape
    tr = _pick(r, prefs=(512, 256, 128, 64, 32, 16))
    nrb = r // tr

    def body(chip_ref, i_ref, *rest):
        rest[-1][...] = i_ref[...].astype(dtype)

    if axis == 1:
        out_shape, o_map = (r, c * N_CHIPS), (lambda i, s: (i, s[0]))
    else:
        out_shape, o_map = (r * N_CHIPS, c), (lambda i, s: (s[0] * nrb + i, 0))
    in_specs = [pl.BlockSpec((None, tr, c), lambda i, s: (l, i, 0))]
    args = [chip, shard]
    if dep is not None:
        in_specs.append(pl.BlockSpec(memory_space=pl.ANY))
        args.append(dep)
    return pl.pallas_call(
        body, name="place_quarter", out_shape=jax.ShapeDtypeStruct(out_shape, dtype),
        grid_spec=pltpu.PrefetchScalarGridSpec(
            num_scalar_prefetch=1, grid=(nrb,), in_specs=in_specs, out_specs=pl.BlockSpec((tr, c), o_map)),
        compiler_params=_cparams(("parallel",)),
    )(*args)


def _gather_copies(land_refs, send_sem, recv_sem, axes, landing_chip_of, first=0):
    x, y, c = _coords()
    out = []
    for j, land in enumerate(land_refs):
        width = land.shape[axes[j]] // N_CHIPS
        for f in (1, 2, 3):
            fx, fy = _CHIP_FLIPS[f]
            px, py = _flip(x, fx), _flip(y, fy)
            lx, ly = landing_chip_of(px, py)
            out.append(pltpu.make_async_remote_copy(
                src_ref=_shard_of(land, axes[j], 2 * x + y, width), dst_ref=_shard_of(land, axes[j], 2 * lx + ly, width),
                send_sem=send_sem.at[3 * (first + j) + f - 1], recv_sem=recv_sem.at[3 * (first + j) + f - 1],
                device_id=(px, py, c), device_id_type=MESH))
    return out


def _gather_start(tag, lands, axes, dep=None):
    n = len(lands)

    def body(*refs):
        land_refs, send_sem, recv_sem = refs[:n], refs[-3], refs[-2]
        x, y, _ = _coords()
        for cp in _gather_copies(land_refs, send_sem, recv_sem, axes, lambda px, py: (x, y)):
            cp.start()
        refs[-1][...] = jnp.zeros_like(refs[-1])

    thru = [pltpu.HBM(a.shape, a.dtype) for a in lands]
    outs = pl.pallas_call(
        body, name=f"gather_start_{tag}",
        out_shape=tuple(thru + [pltpu.SemaphoreType.DMA((3 * n,)), pltpu.SemaphoreType.DMA((3 * n,)),
                                jax.ShapeDtypeStruct((SUBLANES, LANES), F32)]),
        in_specs=[_HBM] * n + ([_ANY] if dep is not None else []),
        out_specs=tuple([_HBM] * n + [_SEM, _SEM, pl.BlockSpec(memory_space=pltpu.VMEM)]),
        input_output_aliases={i: i for i in range(n)},
        compiler_params=pltpu.CompilerParams(has_side_effects=_DATAFLOW),
    )(*[_in_hbm(a) for a in lands], *([dep] if dep is not None else []))
    return list(outs[:n]), outs[n], outs[n + 1], outs[n + 2]


def _gather_wait(tag, lands, send_sem, recv_sem, axes, after, first=0):
    n = len(lands)

    def body(*refs):
        for cp in _gather_copies(refs[:n], refs[n], refs[n + 1], axes, lambda px, py: (px, py), first):
            cp.wait_send()
            cp.wait_recv()

    outs = pl.pallas_call(
        body, name=f"gather_wait_{tag}", out_shape=tuple(pltpu.HBM(a.shape, a.dtype) for a in lands),
        in_specs=[_HBM] * n + [_SEM, _SEM, _ANY], out_specs=tuple([_HBM] * n),
        input_output_aliases={i: i for i in range(n)},
        compiler_params=pltpu.CompilerParams(has_side_effects=_DATAFLOW),
    )(*lands, send_sem, recv_sem, after)
    return list(outs)


N_SLOTS = N_DEV - 1


def _scatter_sends(grad_refs, land_refs, send_sem, recv_sem, meta):
    x, y, c = _coords()
    out = []
    for j, (axis, owner, q, width) in enumerate(meta):
        other = c if owner == 0 else 1 - c
        for f, (fx, fy) in enumerate(_CHIP_FLIPS):
            px, py = _flip(x, fx), _flip(y, fy)
            slot = f + 4 * other - 1
            out.append((other if f == 0 else None, pltpu.make_async_remote_copy(
                src_ref=_shard_of(grad_refs[j], axis, 2 * px + py, width), dst_ref=land_refs[j].at[q, slot],
                send_sem=send_sem.at[4 * j + f], recv_sem=recv_sem.at[N_SLOTS * j + slot],
                device_id=(px, py, owner), device_id_type=MESH)))
    return out


def _scatter_start(layer, grads, lands, meta):
    n = len(grads)
    uniq = []
    for a in lands:
        if not any(a is u for u in uniq):
            uniq.append(a)
    which = [next(k for k, u in enumerate(uniq) if u is a) for a in lands]
    nu = len(uniq)

    def body(*refs):
        grad_refs, land_u = refs[:n], refs[n:n + nu]
        send_sem, recv_sem = refs[n + nu], refs[n + nu + 1]
        for other, cp in _scatter_sends(grad_refs, [land_u[k] for k in which], send_sem, recv_sem, meta):
            if other is None:
                cp.start()
            else:
                pl.when(other == 1)(cp.start)
        refs[-1][...] = jnp.zeros_like(refs[-1])

    thru = [pltpu.HBM(a.shape, a.dtype) for a in list(grads) + uniq]
    outs = pl.pallas_call(
        body, name=f"scatter_start_{layer}",
        out_shape=tuple([pltpu.SemaphoreType.DMA((4 * n,)), pltpu.SemaphoreType.DMA((N_SLOTS * n,))] + thru
                        + [jax.ShapeDtypeStruct((SUBLANES, LANES), F32)]),
        in_specs=[_HBM] * (n + nu),
        out_specs=tuple([_SEM, _SEM] + [_HBM] * (n + nu) + [pl.BlockSpec(memory_space=pltpu.VMEM)]),
        input_output_aliases={i: 2 + i for i in range(n + nu)},
        compiler_params=pltpu.CompilerParams(has_side_effects=_DATAFLOW),
    )(*[_in_hbm(a) for a in list(grads) + uniq])
    new_lands = [outs[2 + n + k] for k in which]
    return outs[0], outs[1], list(outs[2:2 + n]), new_lands, outs[-1]


def _scatter_wait(started, lands):
    nl = len(lands)
    flat_grads = [g for s in started for g in s[2]]
    ng, ns = len(flat_grads), len(started)

    def body(*refs):
        land_refs = refs[:nl]
        grad_refs = refs[nl:nl + ng]
        sem_refs = refs[nl + ng:nl + ng + 2 * ns]
        _, _, c = _coords()
        off = 0
        for k, (_, _, grads, idx, meta) in enumerate(started):
            send_sem, recv_sem = sem_refs[2 * k], sem_refs[2 * k + 1]
            lr = [land_refs[i] for i in idx]
            for other, cp in _scatter_sends(grad_refs[off:off + len(grads)], lr, send_sem, recv_sem, meta):
                if other is None:
                    cp.wait_send()
                else:
                    pl.when(other == 1)(cp.wait_send)
            for j, (axis, owner, q, width) in enumerate(meta):
                mine = (c if owner == 0 else 1 - c) == 0

                @pl.when(mine)
                def _():
                    for slot in range(N_SLOTS):
                        land = lr[j].at[q, slot]
                        pltpu.make_async_remote_copy(
                            src_ref=land, dst_ref=land, send_sem=send_sem.at[0], recv_sem=recv_sem.at[N_SLOTS * j + slot],
                            device_id=_coords(), device_id_type=MESH).wait_recv()
            off += len(grads)

    args = list(lands) + flat_grads
    thru = [pltpu.HBM(a.shape, a.dtype) for a in args]
    sems = [s for st in started for s in st[:2]]
    outs = pl.pallas_call(
        body, name="scatter_wait", out_shape=tuple(thru), in_specs=[_HBM] * (nl + ng) + [_SEM] * (2 * ns),
        out_specs=tuple([_HBM] * (nl + ng)), input_output_aliases={i: i for i in range(nl + ng)},
        compiler_params=pltpu.CompilerParams(has_side_effects=_DATAFLOW),
    )(*args, *sems)
    return list(outs[:nl]), list(outs[nl:])


def _sum_and_share(recv, layer_grads, axis, chip, name, dep=None):
    n, ns, r, c = recv.shape
    tr = _pick(r, prefs=(256, 128, 64, 32, 16))
    nr = r // tr
    nsteps = n * nr
    nlay = len(layer_grads)
    own_map = (lambda h, i, s: (i, s[0])) if axis == 1 else (lambda h, i, s: (s[0] * nr + i, 0))

    def body(chip_ref, i_ref, *rest):
        g_refs = rest[:nlay]
        o_ref, buf, loc_sems, send_sems, recv_sems = rest[nlay + (dep is not None):]
        h, i = pl.program_id(0), pl.program_id(1)
        step = h * nr + i
        slot = step % 2
        x, y, core = _coords()
        layer = core * n + h
        own = g_refs[0][...]
        for l in range(1, nlay):
            own = jnp.where(layer == l, g_refs[l][...], own)

        def copies(sl):
            dst = o_ref.at[core * n + h, pl.ds(pl.multiple_of(i * tr, tr), tr), :]
            loc = pltpu.make_async_copy(buf.at[sl], dst, loc_sems.at[sl])
            rem = pltpu.make_async_remote_copy(
                src_ref=buf.at[sl], dst_ref=dst, send_sem=send_sems.at[sl], recv_sem=recv_sems.at[step],
                device_id=(x, y, 1 - core), device_id_type=MESH)
            return loc, rem

        def drain(sl):
            loc, rem = copies(sl)
            loc.wait()
            rem.wait_send()

        pl.when(step >= 2)(lambda: drain(slot))
        acc = own.astype(F32)
        for s in range(ns):
            acc = acc + i_ref[s].astype(F32)
        buf[slot] = acc
        loc, rem = copies(slot)
        loc.start()
        rem.start()

        @pl.when(step == nsteps - 1)
        def _():
            drain(slot)
            if nsteps > 1:
                drain(1 - slot)
            for hh in range(n):
                for ii in range(nr):
                    land = o_ref.at[(1 - core) * n + hh, pl.ds(ii * tr, tr), :]
                    pltpu.make_async_remote_copy(
                        src_ref=buf.at[0], dst_ref=land, send_sem=send_sems.at[0], recv_sem=recv_sems.at[hh * nr + ii],
                        device_id=(x, y, 1 - core), device_id_type=MESH).wait_recv()

    return pl.pallas_call(
        body, name=name, out_shape=jax.ShapeDtypeStruct((2 * n, r, c), F32),
        grid_spec=pltpu.PrefetchScalarGridSpec(
            num_scalar_prefetch=1, grid=(n, nr),
            in_specs=[pl.BlockSpec((None, ns, tr, c), lambda h, i, s: (h, 0, i, 0))]
            + [pl.BlockSpec((tr, c), own_map)] * nlay + ([pl.BlockSpec(memory_space=pl.ANY)] if dep is not None else []),
            out_specs=_HBM,
            scratch_shapes=[pltpu.VMEM((2, tr, c), F32), pltpu.SemaphoreType.DMA((2,)),
                            pltpu.SemaphoreType.DMA((2,)), pltpu.SemaphoreType.DMA((nsteps,))]),
        compiler_params=_cparams(("arbitrary", "arbitrary")),
    )(chip, recv, *layer_grads, *([dep] if dep is not None else []))


def _adamw_update(w_ref, g_ref, m_ref, v_ref, d_ref, mo_ref, vo_ref):
    bc1 = 1.0 - ADAM_B1 ** ADAM_STEP
    bc2 = 1.0 - ADAM_B2 ** ADAM_STEP
    gv = g_ref[...]
    mn = ADAM_B1 * m_ref[...] + (1.0 - ADAM_B1) * gv
    vn = ADAM_B2 * v_ref[...] + (1.0 - ADAM_B2) * (gv * gv)
    d_ref[...] = -ADAM_LR * ((mn / bc1) / (jnp.sqrt(vn / bc2) + ADAM_EPS) + ADAM_WD * w_ref[...])
    mo_ref[...] = mn
    vo_ref[...] = vn


def _adamw(w, g, m, v, name):
    def body(*refs):
        _adamw_update(*refs)

    tr = _pick(w.shape[0], prefs=(256, 128, 64, 32, 16, 8))
    blk = pl.BlockSpec((tr, w.shape[1]), lambda i: (i, 0))
    sds = jax.ShapeDtypeStruct(w.shape, F32)
    return pl.pallas_call(
        body, name=name, out_shape=(sds, sds, sds), grid=(w.shape[0] // tr,), in_specs=[blk] * 4,
        out_specs=(blk,) * 3, compiler_params=_cparams(("parallel",)),
    )(w, g, m, v)


def _adamw_many(tensors, name, by_layer=False):
    n = len(tensors)

    def body(*refs):
        for t in range(n):
            _adamw_update(*refs[4 * t:4 * t + 4], *refs[4 * n + 3 * t:4 * n + 3 * t + 3])

    def spec(a):
        nd = a.ndim
        if by_layer:
            return pl.BlockSpec((1,) + a.shape[1:], lambda i: (i,) + (0,) * (nd - 1))
        return pl.BlockSpec(a.shape, lambda i: (0,) * nd)

    steps = tensors[0][0].shape[0] if by_layer else 1
    outs = pl.pallas_call(
        body, name=name, out_shape=tuple(jax.ShapeDtypeStruct(t[0].shape, F32) for t in tensors for _ in range(3)),
        grid=(steps,), in_specs=[spec(a) for t in tensors for a in t],
        out_specs=tuple(spec(t[0]) for t in tensors for _ in range(3)), compiler_params=_cparams(("parallel",)),
    )(*[a for t in tensors for a in t])
    return [tuple(outs[3 * t:3 * t + 3]) for t in range(n)]


_PACK_QUANTUM = 256 * LANES


def _pack(arrs):
    flat = jnp.concatenate([a.reshape(-1).astype(F32) for a in arrs])
    flat = jnp.pad(flat, (0, (-flat.shape[0]) % _PACK_QUANTUM))
    return flat.reshape(-1, LANES)


def _unpack(p, shapes):
    flat = p.reshape(-1)
    out, off = [], 0
    for s in shapes:
        n = int(np.prod(s))
        out.append(flat[off:off + n].reshape(s))
        off += n
    return out


def kernel(*args):
    nw = len(WEIGHTS)
    x, tgt = args[0], args[1 + nw]
    w = dict(zip(WEIGHTS, args[1:1 + nw]))
    m = dict(zip(WEIGHTS, args[2 + nw:2 + 2 * nw]))
    v = dict(zip(WEIGHTS, args[2 + 2 * nw:2 + 3 * nw]))
    _, L, D = x.shape
    chip = 2 * lax.axis_index("x") + lax.axis_index("y")

    big = list(BIG)
    small_sh_shapes = [w[n].shape for n in SMALL_SHARDED]
    nbig = len(big)
    chip1 = chip.reshape(1).astype(jnp.int32)
    axes2 = [BIG[n] - 1 for n in big] + [0]
    shards = [w[n] for n in big] + [_pack([w[n] for n in SMALL_SHARDED])[None]]
    pairs = [(t, l) for t in range(nbig + 1) for l in range(shards[t].shape[0])]
    depth = w['norm_mix_g'].shape[0]
    part_of = lambda t, l: 0 if t == nbig else 2 * _model_layer(big[t], l) + big[t].startswith('ffn')
    flying, token = {}, None
    for tag, gset in enumerate(([0], list(range(1, 2 * depth)))):
        ids = [k for g in gset for k, (t, l) in enumerate(pairs) if part_of(t, l) == g]
        ts = [pairs[k][0] for k in ids]
        placed = [_place_quarter(shards[t], pairs[k][1], axes2[t], chip1, F32 if t == nbig else BF16, token)
                  for k, t in zip(ids, ts)]
        lands, send, recv, token = _gather_start(tag, placed, [axes2[t] for t in ts], token)
        first = 0
        for g in gset:
            n = sum(1 for t, l in pairs if part_of(t, l) == g)
            flying[g] = (ts[first:first + n], lands[first:first + n], send, recv, first)
            first += n

    def wait_group(g, after):
        ts, lands, send, recv, first = flying[g]
        landed = _gather_wait(g, lands, send, recv, [axes2[t] for t in ts], token if after is None else after, first)
        return dict(zip(ts, landed))

    first = wait_group(0, None)
    packed = first.pop(nbig).reshape(N_CHIPS, -1, LANES)
    per_chip = [_unpack(packed[s], small_sh_shapes) for s in range(N_CHIPS)]
    wl = dict(w)
    for k, n in enumerate(SMALL_SHARDED):
        wl[n] = jnp.concatenate([per_chip[s][k] for s in range(N_CHIPS)], axis=-1)

    def layer_weights(i, after):
        got = first if i == 0 else wait_group(i, after)
        return {big[t]: a for t, a in got.items()}

    small_shapes = [(w[n].shape[:-1] + (w[n].shape[-1] * N_CHIPS,)) if n in SMALL_SHARDED else w[n].shape
                    for n in SMALL] + [(1,)]
    n_small = sum(int(np.prod(s)) for s in small_shapes)
    pack_rows = -(-n_small // _PACK_QUANTUM) * _PACK_QUANTUM // LANES
    nlayers = [w[n].shape[0] for n in big] + [2]
    halves = [n // 2 for n in nlayers]
    quarters = [tuple(w[n].shape[1:]) for n in big] + [(pack_rows // 2 // N_CHIPS, LANES)]
    wire = [BF16] * nbig + [F32]
    land_now = [lax.empty((halves[t], N_SLOTS) + quarters[t], wire[t]) for t in range(nbig + 1)]
    gparts = [[None] * n for n in nlayers]
    started = []

    def start_scatter(tag, ts, ls, arrays):
        meta = [(axes2[t], l // halves[t], l % halves[t], quarters[t][axes2[t]]) for t, l in zip(ts, ls)]
        send, recv, thru, new_lands, token = _scatter_start(tag, arrays, [land_now[t] for t in ts], meta)
        for t, ln in zip(ts, new_lands):
            land_now[t] = ln
        started.append((send, recv, thru, ts, meta, ls))
        return token

    def on_layer_grads(g, gb):
        ts = [big.index(n) for n in gb]
        return start_scatter(g, ts, [g // 2 if big[t].startswith('ffn') else g // 4 for t in ts],
                             [gb[big[t]] for t in ts])

    loss, dx, gsmall = _local_step(x.reshape(L, D), tgt.reshape(L, D), wl, layer_weights, on_layer_grads)
    gpack = _pack([gsmall[n] for n in SMALL] + [loss.reshape(1)])
    start_scatter(2 * depth, [nbig, nbig], [0, 1], [gpack[:pack_rows // 2], gpack[pack_rows // 2:]])
    landed, sent = _scatter_wait([s[:5] for s in started], land_now)
    for (t, l), g in zip([(t, l) for s in started for t, l in zip(s[3], s[5])], sent):
        gparts[t][l] = g
    small_sum = _sum_and_share(landed[nbig], gparts[nbig], 0, chip1, "sum_share_small")
    quarter_rows = small_sum.shape[0] * small_sum.shape[1]
    placed = _place_quarter(small_sum.reshape(1, quarter_rows, LANES), 0, 0, chip1, F32)
    flying_small, send, recv, token = _gather_start("small", [placed], [0])
    gshard = {n: _sum_and_share(landed[t], gparts[t], axes2[t], chip1, "sum_share_" + n, token)
              for t, n in enumerate(big)}
    small_all = _gather_wait("small", flying_small, send, recv, [0], gshard[big[-1]])[0]
    gpack = small_all.reshape(N_CHIPS, 2, quarter_rows // 2, LANES).transpose(1, 0, 2, 3).reshape(pack_rows, LANES)
    gs = dict(zip(SMALL + ['loss'], _unpack(gpack, small_shapes)))
    loss = gs.pop('loss').reshape(())
    for n in SMALL_SHARDED:
        width = w[n].shape[-1]
        gs[n] = lax.dynamic_slice_in_dim(gs[n], chip * width, width, axis=gs[n].ndim - 1)

    grads, delta, new_m, new_v = {}, {}, {}, {}
    for n in big:
        shp = w[n].shape
        flat = lambda a: a.reshape(shp[0] * shp[1], shp[2])
        g = gshard[n]
        grads[n] = g
        d_, m_, v_ = _adamw(flat(w[n]), flat(g), flat(m[n]), flat(v[n]), "adamw_" + n)
        delta[n], new_m[n], new_v[n] = d_.reshape(shp), m_.reshape(shp), v_.reshape(shp)
    sparse = [n for n in SMALL if w[n].ndim == 4 and w[n].shape[-1] < LANES // 2]
    for names, by_layer in ((sparse, True), ([n for n in SMALL if n not in sparse], False)):
        as2d = lambda a: a.reshape(1, -1) if a.ndim == 1 else a
        res = _adamw_many([(as2d(w[n]), as2d(gs[n]), as2d(m[n]), as2d(v[n])) for n in names],
                          "adamw_small_by_layer" if by_layer else "adamw_small", by_layer)
        for n, (d_, m_, v_) in zip(names, res):
            shp = w[n].shape
            grads[n], delta[n], new_m[n], new_v[n] = gs[n], d_.reshape(shp), m_.reshape(shp), v_.reshape(shp)

    return (loss, dx.reshape(1, L, D), *[grads[n] for n in WEIGHTS], *[delta[n] for n in WEIGHTS],
            *[new_m[n] for n in WEIGHTS], *[new_v[n] for n in WEIGHTS])
```

```python
import math

import numpy as np
import jax
import jax.numpy as jnp
from jax import lax
from jax.experimental import pallas as pl
from jax.experimental.pallas import tpu as pltpu

F32 = jnp.float32
BF16 = jnp.bfloat16
MESH = pl.DeviceIdType.MESH

EPS = 1e-6
CHUNK = 128
POOL_WINDOWS = (2, 4, 8, 16)
LANES = 128
SUBLANES = 8
SCAN_CHUNKS = SUBLANES
S5_GROUPS_PER_STEP = 4
MM_TM_CAP, MM_TN_CAP, MM_TK_CAP = 1408, 2816, 2048
MM_TK_WHOLE = 2816
VMEM_LIMIT = 48 * 1024 * 1024
VMEM_LIMIT_S5 = 56 * 1024 * 1024

ADAM_LR, ADAM_B1, ADAM_B2, ADAM_EPS, ADAM_WD, ADAM_STEP = 0.001, 0.9, 0.999, 1e-08, 0.01, 10

WEIGHTS = ['norm_mix_g', 'even_w_in', 'even_conv_w', 'ssm_log_step', 'ssm_a_re', 'ssm_a_im', 'ssm_b_re',
           'ssm_b_im', 'ssm_c_re', 'ssm_c_im', 'ssm_d', 'ssm_glu_w', 'ssm_glu_b', 'even_w_out', 'odd_w_in',
           'pool_w', 'pool_scale', 'sgu_norm_g', 'sgu_w', 'sgu_b', 'odd_w_out', 'norm_ffn_g', 'ffn_w_up',
           'ffn_conv_w', 'ffn_conv_b', 'ffn_w_down', 'norm_final_g']
BIG = {'even_w_in': 2, 'ssm_glu_w': 1, 'even_w_out': 1, 'odd_w_in': 2, 'odd_w_out': 1, 'ffn_w_up': 2,
       'ffn_w_down': 1}
SMALL_SHARDED = ('even_conv_w', 'pool_scale', 'sgu_norm_g', 'ffn_conv_w')
SMALL = [n for n in WEIGHTS if n not in BIG]
N_CHIPS = 4
N_DEV = 8


def _cparams(sem=None, vmem=VMEM_LIMIT):
    kw = dict(vmem_limit_bytes=vmem)
    if sem is not None:
        kw['dimension_semantics'] = sem
    return pltpu.CompilerParams(**kw)


def _pick(n, segs=(), prefs=(1024, 512, 256, 128)):
    for t in prefs:
        if n % t == 0 and all(s % t == 0 for s in segs if s):
            return t
    return n


def _largest_tile(n, segs, cap):
    best = None
    for t in range(LANES, min(n, cap) + 1, LANES):
        if n % t == 0 and all(s % t == 0 for s in segs if s):
            best = t
    return best if best is not None else n


def _ldims(arr, kind):
    if kind is None:
        return arr.shape
    if kind[0] == 'lead':
        return arr.shape[1:]
    return (arr.shape[1], arr.shape[0] * arr.shape[2])


def _segw(arr, kind):
    return arr.shape[2] if (kind is not None and kind[0] == 'seg') else None


def _opspec(arr, kind, br, bc, rfn, cfn):
    if kind is None:
        return pl.BlockSpec((br, bc), lambda i, j, k: (rfn(i, j, k), cfn(i, j, k)))
    if kind[0] == 'lead':
        lead = kind[1]
        return pl.BlockSpec((None, br, bc), lambda i, j, k: (lead, rfn(i, j, k), cfn(i, j, k)))
    per = arr.shape[2] // bc
    return pl.BlockSpec((None, br, bc), lambda i, j, k: (cfn(i, j, k) // per, rfn(i, j, k), cfn(i, j, k) % per))


def _mm(a, b, mode, out_dtype, name, ak=None, bk=None, ok=None, res=None, dep=None):
    ar, ac = _ldims(a, ak)
    br_, bc_ = _ldims(b, bk)
    if mode == 'nn':
        M, K, N = ar, ac, bc_
        assert br_ == K
    else:
        M, K, N = ar, ac, br_
        assert bc_ == K
    sa, sb = _segw(a, ak), _segw(b, bk)
    so = (N // ok[1]) if ok is not None else None
    tm = _largest_tile(M, [], MM_TM_CAP)
    tn = _largest_tile(N, [sb if mode == 'nn' else None, so], MM_TN_CAP)
    ksegs = [sa, sb if mode == 'nt' else None]
    tk = K if (K <= MM_TK_WHOLE and not any(ksegs)) else _largest_tile(K, ksegs, MM_TK_CAP)
    nk = K // tk
    I = lambda i, j, k: i
    J = lambda i, j, k: j
    Kk = lambda i, j, k: k
    a_spec = _opspec(a, ak, tm, tk, I, Kk)
    if mode == 'nn':
        b_spec = _opspec(b, bk, tk, tn, Kk, J)
        dims = (((1,), (0,)), ((), ()))
    else:
        b_spec = _opspec(b, bk, tn, tk, J, Kk)
        dims = (((1,), (1,)), ((), ()))
    if ok is None:
        out_shape = jax.ShapeDtypeStruct((M, N), out_dtype)
        o_spec = pl.BlockSpec((tm, tn), lambda i, j, k: (i, j))
    else:
        out_shape = jax.ShapeDtypeStruct((ok[1], M, N // ok[1]), out_dtype)
        per = (N // ok[1]) // tn
        o_spec = pl.BlockSpec((None, tm, tn), lambda i, j, k: (j // per, i, j % per))
    has_res = res is not None

    def body(*refs):
        a_ref, b_ref = refs[0], refs[1]
        r_ref = refs[2] if has_res else None
        o_ref = refs[n_in]
        prod = lax.dot_general(a_ref[...].astype(BF16), b_ref[...].astype(BF16), dims, preferred_element_type=F32)
        if nk == 1:
            o_ref[...] = (prod + r_ref[...] if has_res else prod).astype(out_dtype)
            return
        acc = refs[-1]
        k = pl.program_id(2)

        @pl.when(k == 0)
        def _():
            acc[...] = prod

        @pl.when(k > 0)
        def _():
            acc[...] += prod

        @pl.when(k == nk - 1)
        def _():
            o = acc[...]
            if has_res:
                o = o + r_ref[...]
            o_ref[...] = o.astype(out_dtype)

    in_specs = [a_spec, b_spec]
    args = [a, b]
    if has_res:
        in_specs.append(pl.BlockSpec((tm, tn), lambda i, j, k: (i, j)))
        args.append(res)
    if dep is not None:
        in_specs.append(pl.BlockSpec(memory_space=pl.ANY))
        args.append(dep)
    n_in = len(args)
    return pl.pallas_call(
        body, name=name, out_shape=out_shape, grid=(M // tm, N // tn, nk), in_specs=in_specs, out_specs=o_spec,
        scratch_shapes=[pltpu.VMEM((tm, tn), F32)] if nk > 1 else [],
        compiler_params=_cparams(("parallel", "parallel", "arbitrary")),
    )(*args)


_G0 = math.sqrt(2.0 / math.pi)
_G1 = 0.044715


def _gelu(x):
    return 0.5 * x * (1.0 + jnp.tanh(_G0 * (x + _G1 * x * x * x)))


def _gelu_grad(x):
    x2 = x * x
    t = jnp.tanh(_G0 * (x + _G1 * x * x2))
    return 0.5 * (1.0 + t) + 0.5 * x * (1.0 - t * t) * (_G0 * (1.0 + 3.0 * _G1 * x2))


def _sigmoid(x):
    return 1.0 / (1.0 + jnp.exp(-x))


def _down(v, k):
    r = pltpu.roll(v, k, axis=0)
    row = lax.broadcasted_iota(jnp.int32, (SUBLANES, v.shape[1]), 0)
    return jnp.concatenate([jnp.where(row >= k, r[:SUBLANES], 0.0), r[SUBLANES:]], axis=0)


def _up(v, k):
    n = v.shape[0]
    r = pltpu.roll(v, n - k, axis=0)
    row = lax.broadcasted_iota(jnp.int32, (SUBLANES, v.shape[1]), 0)
    return jnp.concatenate([r[:n - SUBLANES], jnp.where(row < SUBLANES - k, r[n - SUBLANES:], 0.0)], axis=0)


def _taps(v):
    return _down(v, 2), _down(v, 1), v


def _conv3(taps, w):
    return w[0:1, :] * taps[0] + w[1:2, :] * taps[1] + w[2:3, :] * taps[2]


def _conv3_t(dv, w):
    return w[2:3, :] * dv + w[1:2, :] * _up(dv, 1) + w[0:1, :] * _up(dv, 2)


def _conv3_dw(dv, taps):
    return tuple(jnp.sum(dv * tp, axis=0, keepdims=True) for tp in taps)


def _cmul(ar, ai, br, bi):
    return ar * br - ai * bi, ar * bi + ai * br


def _cpow(lr, li, n):
    rr = ri = None
    br, bi = lr, li
    while n:
        if n & 1:
            rr, ri = (br, bi) if rr is None else _cmul(rr, ri, br, bi)
        n >>= 1
        if n:
            br, bi = _cmul(br, bi, br, bi)
    return rr, ri


NORM_ROWS = 256


def _norm_mm(x, g, b, out_dtype, name, ok=None):
    M, D = x.shape
    N = b.shape[1]
    so = (N // ok[1]) if ok is not None else None
    tm = _largest_tile(M, [], 1024)
    tn = _largest_tile(N, [so], MM_TN_CAP)
    if ok is None:
        out_shape = jax.ShapeDtypeStruct((M, N), out_dtype)
        o_spec = pl.BlockSpec((tm, tn), lambda i, j: (i, j))
    else:
        out_shape = jax.ShapeDtypeStruct((ok[1], M, N // ok[1]), out_dtype)
        per = (N // ok[1]) // tn
        o_spec = pl.BlockSpec((None, tm, tn), lambda i, j: (j // per, i, j % per))

    def body(x_ref, g_ref, b_ref, o_ref, ht_ref, h_scr):
        @pl.when(pl.program_id(1) == 0)
        def _():
            for c in range(tm // NORM_ROWS):
                rows = pl.ds(c * NORM_ROWS, NORM_ROWS)
                xv = x_ref[rows, :]
                h = xv * lax.rsqrt(jnp.mean(xv * xv, axis=-1, keepdims=True) + EPS) * g_ref[...]
                h_scr[rows, :] = h.astype(BF16)
                ht_ref[:, rows] = h.T.astype(BF16)

        o_ref[...] = jnp.dot(h_scr[...], b_ref[...], preferred_element_type=F32).astype(out_dtype)

    return pl.pallas_call(
        body, name=name, out_shape=(out_shape, jax.ShapeDtypeStruct((D, M), BF16)), grid=(M // tm, N // tn),
        in_specs=[pl.BlockSpec((tm, D), lambda i, j: (i, 0)), pl.BlockSpec((1, D), lambda i, j: (0, 0)),
                  pl.BlockSpec((D, tn), lambda i, j: (0, j))],
        out_specs=(o_spec, pl.BlockSpec((D, tm), lambda i, j: (0, i))),
        scratch_shapes=[pltpu.VMEM((tm, D), BF16)], compiler_params=_cparams(("parallel", "arbitrary")),
    )(x, g.reshape(1, D), b)


def _mm_norm_bwd(a, b, x, g, dres, name, ak=None, dep=None):
    M, K = _ldims(a, ak)
    D = b.shape[0]
    assert b.shape[1] == K and x.shape == (M, D)
    sa = _segw(a, ak)
    tm = _largest_tile(M, [], 1024)
    whole_segs = bool(sa) and K <= MM_TK_WHOLE
    tk = K if (K <= MM_TK_WHOLE) else _largest_tile(K, [sa], MM_TK_CAP)
    ni, nk = M // tm, K // tk
    if whole_segs:
        a_spec = pl.BlockSpec((a.shape[0], tm, sa), lambda i, k: (0, i, 0))
    else:
        a3 = _opspec(a, ak, tm, tk, lambda i, j, k: i, lambda i, j, k: k)
        a_spec = pl.BlockSpec(a3.block_shape, lambda i, k: a3.index_map(i, 0, k))
    n_in = 5 + (dep is not None)

    def body(*refs):
        a_ref, b_ref, x_ref, g_ref, r_ref = refs[:5]
        dx_ref, dxb_ref, dg_ref, acc, accg = refs[n_in:]
        i, k = pl.program_id(0), pl.program_id(1)
        av = jnp.concatenate([a_ref[s] for s in range(a.shape[0])], axis=1) if whole_segs else a_ref[...]
        prod = lax.dot_general(av.astype(BF16), b_ref[...], (((1,), (1,)), ((), ())), preferred_element_type=F32)

        @pl.when(k == 0)
        def _():
            acc[...] = prod

        @pl.when(k > 0)
        def _():
            acc[...] += prod

        @pl.when((i == 0) & (k == 0))
        def _():
            accg[...] = jnp.zeros_like(accg)

        @pl.when(k == nk - 1)
        def _():
            for c in range(tm // NORM_ROWS):
                rows = pl.ds(c * NORM_ROWS, NORM_ROWS)
                xv = x_ref[rows, :]
                r = lax.rsqrt(jnp.mean(xv * xv, axis=-1, keepdims=True) + EPS)
                xh = xv * r
                dhv = acc[rows, :]
                accg[...] += jnp.sum((dhv * xh).reshape(NORM_ROWS // SUBLANES, SUBLANES, D), axis=0)
                dxh = dhv * g_ref[...]
                dxv = r_ref[rows, :] + r * (dxh - xh * jnp.mean(dxh * xh, axis=-1, keepdims=True))
                dx_ref[rows, :] = dxv
                dxb_ref[rows, :] = dxv.astype(BF16)

        @pl.when((i == ni - 1) & (k == nk - 1))
        def _():
            dg_ref[...] = jnp.sum(accg[...], axis=0, keepdims=True)

    row = pl.BlockSpec((tm, D), lambda i, k: (i, 0))
    vec = pl.BlockSpec((1, D), lambda i, k: (0, 0))
    in_specs = [a_spec, pl.BlockSpec((D, tk), lambda i, k: (0, k)), row, vec, row]
    args = [a, b, x, g.reshape(1, D), dres]
    if dep is not None:
        in_specs.append(pl.BlockSpec(memory_space=pl.ANY))
        args.append(dep)
    return pl.pallas_call(
        body, name=name,
        out_shape=(jax.ShapeDtypeStruct((M, D), F32), jax.ShapeDtypeStruct((M, D), BF16),
                   jax.ShapeDtypeStruct((1, D), F32)),
        grid=(ni, nk), in_specs=in_specs, out_specs=(row, row, vec),
        scratch_shapes=[pltpu.VMEM((tm, D), F32), pltpu.VMEM((SUBLANES, D), F32)],
        compiler_params=_cparams(("arbitrary", "arbitrary"), VMEM_LIMIT_S5),
    )(*args)


def _loss_head(x, g, tgt):
    L, D = x.shape
    tr = _pick(L, prefs=(512, 256, 128))
    nsteps = L // tr

    def body(x_ref, g_ref, t_ref, loss_ref, dx_ref, dxb_ref, dg_ref, acc_g, acc_l):
        i = pl.program_id(0)

        @pl.when(i == 0)
        def _():
            acc_g[...] = jnp.zeros_like(acc_g)
            acc_l[...] = jnp.zeros_like(acc_l)

        xv = x_ref[...]
        gv = g_ref[...]
        r = lax.rsqrt(jnp.mean(xv * xv, axis=-1, keepdims=True) + EPS)
        xh = xv * r
        e = xh * gv - t_ref[...]
        acc_l[...] += jnp.sum((e * e).reshape(tr // SUBLANES, SUBLANES, D), axis=0)
        dy = e * (1.0 / D)
        acc_g[...] += jnp.sum((dy * xh).reshape(tr // SUBLANES, SUBLANES, D), axis=0)
        dxh = dy * gv
        dxv = r * (dxh - xh * jnp.mean(dxh * xh, axis=-1, keepdims=True))
        dx_ref[...] = dxv
        dxb_ref[...] = dxv.astype(BF16)

        @pl.when(i == nsteps - 1)
        def _():
            dg_ref[...] = jnp.sum(acc_g[...], axis=0, keepdims=True)
            tot = jnp.sum(jnp.sum(acc_l[...], axis=0, keepdims=True), axis=1, keepdims=True) * (0.5 / D)
            loss_ref[...] = jnp.broadcast_to(tot, (SUBLANES, LANES))

    row = pl.BlockSpec((tr, D), lambda i: (i, 0))
    vec = pl.BlockSpec((1, D), lambda i: (0, 0))
    return pl.pallas_call(
        body, name="loss_head",
        out_shape=(jax.ShapeDtypeStruct((SUBLANES, LANES), F32), jax.ShapeDtypeStruct((L, D), F32),
                   jax.ShapeDtypeStruct((L, D), BF16), jax.ShapeDtypeStruct((1, D), F32)),
        grid=(nsteps,), in_specs=[row, vec, row],
        out_specs=(pl.BlockSpec((SUBLANES, LANES), lambda i: (0, 0)), row, row, vec),
        scratch_shapes=[pltpu.VMEM((SUBLANES, D), F32), pltpu.VMEM((SUBLANES, D), F32)],
        compiler_params=_cparams(("arbitrary",)),
    )(x, g.reshape(1, D), tgt)


def _sconv_fwd(proj4, conv_w, name):
    _, L, C = proj4.shape
    cb = LANES

    def body(p_ref, w_ref, o_ref):
        xa, ba, ca = p_ref[0].astype(F32), p_ref[1].astype(F32), p_ref[2].astype(F32)
        o_ref[...] = (ba * _conv3(_taps(ca * xa), w_ref[...])).astype(BF16)

    return pl.pallas_call(
        body, name=name, out_shape=jax.ShapeDtypeStruct((L, 2 * C), BF16), grid=(C // cb,),
        in_specs=[pl.BlockSpec((3, L, cb), lambda j: (0, 0, j)), pl.BlockSpec((3, cb), lambda j: (0, j))],
        out_specs=pl.BlockSpec((L, cb), lambda j: (0, j)), compiler_params=_cparams(("parallel",)),
    )(proj4, conv_w)


def _sconv_bwd(proj4, dmix, conv_w, name):
    _, L, C = proj4.shape
    cb = LANES

    def body(p_ref, d_ref, w_ref, o_ref, dw_ref):
        xa, ba, ca = p_ref[0].astype(F32), p_ref[1].astype(F32), p_ref[2].astype(F32)
        w = w_ref[...]
        dya = d_ref[...]
        tq = _taps(ca * xa)
        cq = _conv3(tq, w)
        dcq = dya * ba
        dq = _conv3_t(dcq, w)
        for tap, dwt in enumerate(_conv3_dw(dcq, tq)):
            dw_ref[tap:tap + 1, :] = dwt
        o_ref[0] = (dq * ca).astype(BF16)
        o_ref[1] = (dya * cq).astype(BF16)
        o_ref[2] = (dq * xa).astype(BF16)

    return pl.pallas_call(
        body, name=name,
        out_shape=(jax.ShapeDtypeStruct((4, L, C), BF16), jax.ShapeDtypeStruct((3, C), F32)), grid=(C // cb,),
        in_specs=[pl.BlockSpec((3, L, cb), lambda j: (0, 0, j)), pl.BlockSpec((L, cb), lambda j: (0, j)),
                  pl.BlockSpec((3, cb), lambda j: (0, j))],
        out_specs=(pl.BlockSpec((3, L, cb), lambda j: (0, 0, j)), pl.BlockSpec((3, cb), lambda j: (0, j))),
        compiler_params=_cparams(("parallel",)),
    )(proj4, dmix, conv_w)


def _s5_prep(log_step, a_re, a_im, b_re, b_im, c_re, c_im):
    G, P = a_re.shape
    H = b_re.shape[-1]
    gs = S5_GROUPS_PER_STEP
    ns = G // gs
    gu = LANES // H
    lam = lax.complex(a_re, a_im)
    step = jnp.exp(log_step)[:, None]
    lam_bar = jnp.exp(lam * step)
    b_bar = ((lam_bar - 1.0) / lam)[..., None] * lax.complex(b_re, b_im)
    lr = jnp.real(lam_bar).reshape(ns, 1, gs * P)
    li = jnp.imag(lam_bar).reshape(ns, 1, gs * P)
    k = np.arange(ns)[:, None, None]
    oh = jnp.asarray((np.arange(gu)[None, :, None] == gs * (k % (gu // gs)) + np.arange(gs)[None, None, :]),
                     F32)
    bre = jnp.einsum('kgl,klph->kghlp', oh, jnp.real(b_bar).reshape(ns, gs, P, H)).reshape(ns, gu * H, gs * P)
    bim = jnp.einsum('kgl,klph->kghlp', oh, jnp.imag(b_bar).reshape(ns, gs, P, H)).reshape(ns, gu * H, gs * P)
    cre = jnp.einsum('kgl,klhp->klpgh', oh, c_re.reshape(ns, gs, H, P)).reshape(ns, gs * P, gu * H)
    cim = jnp.einsum('kgl,klhp->klpgh', oh, c_im.reshape(ns, gs, H, P)).reshape(ns, gs * P, gu * H)
    return lr, li, jnp.concatenate([bre, bim], axis=2), jnp.concatenate([cre, -cim], axis=1)


def _carry_tile(fr, fi, pr, pi, reverse):
    row = lax.broadcasted_iota(jnp.int32, fr.shape, 0)
    cr = jnp.zeros_like(fr)
    ci = jnp.zeros_like(fi)
    sr = jnp.zeros_like(fr[0:1])
    si = jnp.zeros_like(sr)
    order = range(SCAN_CHUNKS - 1, 0, -1) if reverse else range(0, SCAN_CHUNKS - 1)
    for c in order:
        fcr = jnp.sum(jnp.where(row == c, fr, 0.0), axis=0, keepdims=True)
        fci = jnp.sum(jnp.where(row == c, fi, 0.0), axis=0, keepdims=True)
        mr, mi = _cmul(pr, pi, sr, si)
        sr, si = mr + fcr, mi + fci
        nxt = c - 1 if reverse else c + 1
        cr = jnp.where(row == nxt, sr, cr)
        ci = jnp.where(row == nxt, si, ci)
    return cr, ci


def _scan_order_into(dst_ref, src_ref, T):
    for c in range(SCAN_CHUNKS):
        dst_ref[pl.ds(c, T, stride=SCAN_CHUNKS), :] = src_ref[pl.ds(c * T, T), :].astype(F32)


def _s5_fwd(proj4, lr, li, bmat, cmat, d, name):
    _, L, Du = proj4.shape
    ns, _, W2 = bmat.shape
    W = W2 // 2
    T = L // SCAN_CHUNKS
    rb = _pick(L, prefs=(512, 256, 128))
    per = (ns * LANES) // Du

    def body(ut_ref, lr_ref, li_ref, b_ref, c_ref, d_ref, y_ref, sr_ref, si_ref, u_ref):
        k = pl.program_id(0)
        _scan_order_into(u_ref, ut_ref, T)
        for r in range(L // rb):
            rows = pl.ds(r * rb, rb)
            bu = jnp.dot(u_ref[rows, :].astype(BF16), b_ref[...], preferred_element_type=F32)
            sr_ref[rows, :] = bu[:, :W]
            si_ref[rows, :] = bu[:, W:]
        lam_r = jnp.broadcast_to(lr_ref[...], (SUBLANES, W))
        lam_i = jnp.broadcast_to(li_ref[...], (SUBLANES, W))

        def local(t, carry):
            sr, si = carry
            rows = pl.ds(pl.multiple_of(t * SUBLANES, SUBLANES), SUBLANES)
            mr, mi = _cmul(lam_r, lam_i, sr, si)
            sr = mr + sr_ref[rows, :]
            si = mi + si_ref[rows, :]
            sr_ref[rows, :] = sr
            si_ref[rows, :] = si
            return sr, si

        z = jnp.zeros((SUBLANES, W), F32)
        fr, fi = lax.fori_loop(0, T, local, (z, z))
        pr, pi = _cpow(lam_r, lam_i, T)
        cr, ci = _carry_tile(fr, fi, pr[0:1], pi[0:1], reverse=False)

        def fix(t, carry):
            wr, wi = carry
            rows = pl.ds(pl.multiple_of(t * SUBLANES, SUBLANES), SUBLANES)
            ar, ai = _cmul(wr, wi, cr, ci)
            sr_ref[rows, :] += ar
            si_ref[rows, :] += ai
            return _cmul(wr, wi, lam_r, lam_i)

        lax.fori_loop(0, T, fix, (lam_r, lam_i))
        first = (k % per) == 0
        for r in range(L // rb):
            rows = pl.ds(r * rb, rb)
            s = jnp.concatenate([sr_ref[rows, :], si_ref[rows, :]], axis=1).astype(BF16)
            y = jnp.dot(s, c_ref[...], preferred_element_type=F32)

            @pl.when(first)
            def _():
                y_ref[rows, :] = y + d_ref[...] * u_ref[rows, :]

            @pl.when(jnp.logical_not(first))
            def _():
                y_ref[rows, :] += y

    ublk = pl.BlockSpec((L, LANES), lambda k: (0, k // per))
    sblk = pl.BlockSpec((L, W), lambda k: (0, k))
    lam = pl.BlockSpec((None, 1, W), lambda k: (k, 0, 0))
    return pl.pallas_call(
        body, name=name,
        out_shape=(jax.ShapeDtypeStruct((L, Du), F32), jax.ShapeDtypeStruct((L, ns * W), F32),
                   jax.ShapeDtypeStruct((L, ns * W), F32)),
        grid=(ns,),
        in_specs=[pl.BlockSpec((None, L, LANES), lambda k: (3, 0, k // per)), lam, lam,
                  pl.BlockSpec((None, LANES, 2 * W), lambda k: (k, 0, 0)),
                  pl.BlockSpec((None, 2 * W, LANES), lambda k: (k, 0, 0)),
                  pl.BlockSpec((1, LANES), lambda k: (0, k // per))],
        out_specs=(ublk, sblk, sblk), scratch_shapes=[pltpu.VMEM((L, LANES), F32)],
        compiler_params=_cparams(("arbitrary",), VMEM_LIMIT_S5),
    )(proj4, lr, li, bmat.astype(BF16), cmat.astype(BF16), d.reshape(1, Du))


def _s5_bwd(dy, proj4, dproj, s_re, s_im, lr, li, bmat, cmat, d, name):
    _, L, Du = proj4.shape
    ns, _, W2 = bmat.shape
    W = W2 // 2
    T = L // SCAN_CHUNKS
    rb = _pick(L, prefs=(512, 256, 128))
    per = (ns * LANES) // Du
    NT = (((1,), (1,)), ((), ()))
    TN = (((0,), (0,)), ((), ()))

    def body(dy_ref, ut_ref, dp_in, sr_ref, si_ref, lr_ref, li_ref, b_ref, c_ref, d_ref,
             dut_ref, db_ref, dc_ref, dl_ref, dd_ref, gr_ref, gi_ref, u_ref, du_ref):
        k = pl.program_id(0)
        _scan_order_into(u_ref, ut_ref, T)
        for r in range(L // rb):
            rows = pl.ds(r * rb, rb)
            g = lax.dot_general(dy_ref[rows, :].astype(BF16), c_ref[...], NT, preferred_element_type=F32)
            gr_ref[rows, :] = g[:, :W]
            gi_ref[rows, :] = g[:, W:]
        lam_r = jnp.broadcast_to(lr_ref[...], (SUBLANES, W))
        lam_i = -jnp.broadcast_to(li_ref[...], (SUBLANES, W))

        def local(i, carry):
            gr, gi = carry
            rows = pl.ds(pl.multiple_of((T - 1 - i) * SUBLANES, SUBLANES), SUBLANES)
            mr, mi = _cmul(lam_r, lam_i, gr, gi)
            gr = mr + gr_ref[rows, :]
            gi = mi + gi_ref[rows, :]
            gr_ref[rows, :] = gr
            gi_ref[rows, :] = gi
            return gr, gi

        z = jnp.zeros((SUBLANES, W), F32)
        fr, fi = lax.fori_loop(0, T, local, (z, z))
        pr, pi = _cpow(lam_r, lam_i, T)
        cr, ci = _carry_tile(fr, fi, pr[0:1], pi[0:1], reverse=True)

        def true_g(rows, wr, wi):
            ar, ai = _cmul(wr, wi, cr, ci)
            gr = gr_ref[rows, :] + ar
            gi = gi_ref[rows, :] + ai
            gr_ref[rows, :] = gr
            gi_ref[rows, :] = gi
            return gr, gi

        def fix(i, carry):
            wr, wi, ar_, ai_ = carry
            t = T - 1 - i
            rows = pl.ds(pl.multiple_of(t * SUBLANES, SUBLANES), SUBLANES)
            prev = pl.ds(pl.multiple_of((t - 1) * SUBLANES, SUBLANES), SUBLANES)
            gr, gi = true_g(rows, wr, wi)
            qr, qi = sr_ref[prev, :], si_ref[prev, :]
            ar_ = ar_ + gr * qr + gi * qi
            ai_ = ai_ + gi * qr - gr * qi
            wr, wi = _cmul(wr, wi, lam_r, lam_i)
            return wr, wi, ar_, ai_

        wr, wi, acc_r, acc_i = lax.fori_loop(0, T - 1, fix, (lam_r, lam_i, z, z))
        gr, gi = true_g(pl.ds(0, SUBLANES), wr, wi)
        last = pl.ds((T - 1) * SUBLANES, SUBLANES)
        row = lax.broadcasted_iota(jnp.int32, (SUBLANES, W), 0)
        qr = jnp.where(row >= 1, pltpu.roll(sr_ref[last, :], 1, axis=0), 0.0)
        qi = jnp.where(row >= 1, pltpu.roll(si_ref[last, :], 1, axis=0), 0.0)
        acc_r = acc_r + gr * qr + gi * qi
        acc_i = acc_i + gi * qr - gr * qi
        dl_ref[0:1, :] = jnp.sum(acc_r, axis=0, keepdims=True)
        dl_ref[1:2, :] = jnp.sum(acc_i, axis=0, keepdims=True)

        first = (k % per) == 0
        db = jnp.zeros((LANES, 2 * W), F32)
        dc = jnp.zeros((LANES, 2 * W), F32)
        dd = jnp.zeros((1, LANES), F32)
        for r in range(L // rb):
            rows = pl.ds(r * rb, rb)
            gb = jnp.concatenate([gr_ref[rows, :], gi_ref[rows, :]], axis=1).astype(BF16)
            sb = jnp.concatenate([sr_ref[rows, :], si_ref[rows, :]], axis=1).astype(BF16)
            dyv = dy_ref[rows, :]
            uv = u_ref[rows, :]
            du = lax.dot_general(gb, b_ref[...], NT, preferred_element_type=F32)
            db = db + lax.dot_general(uv.astype(BF16), gb, TN, preferred_element_type=F32)
            dc = dc + lax.dot_general(dyv.astype(BF16), sb, TN, preferred_element_type=F32)
            dd = dd + jnp.sum(dyv * uv, axis=0, keepdims=True)

            @pl.when(first)
            def _():
                du_ref[rows, :] = du + d_ref[...] * dyv

            @pl.when(jnp.logical_not(first))
            def _():
                du_ref[rows, :] += du

        db_ref[...] = db
        dc_ref[...] = dc

        @pl.when(first)
        def _():
            dd_ref[...] = dd

        @pl.when((k % per) == per - 1)
        def _():
            for c in range(SCAN_CHUNKS):
                dut_ref[pl.ds(c * T, T), :] = du_ref[pl.ds(c, T, stride=SCAN_CHUNKS), :].astype(BF16)

    ublk = pl.BlockSpec((L, LANES), lambda k: (0, k // per))
    uslab = pl.BlockSpec((None, L, LANES), lambda k: (3, 0, k // per))
    sblk = pl.BlockSpec((L, W), lambda k: (0, k))
    lam = pl.BlockSpec((None, 1, W), lambda k: (k, 0, 0))
    vec = pl.BlockSpec((1, LANES), lambda k: (0, k // per))
    mat = pl.BlockSpec((None, LANES, 2 * W), lambda k: (k, 0, 0))
    return pl.pallas_call(
        body, name=name,
        out_shape=(jax.ShapeDtypeStruct(dproj.shape, dproj.dtype), jax.ShapeDtypeStruct((ns, LANES, 2 * W), F32),
                   jax.ShapeDtypeStruct((ns, LANES, 2 * W), F32), jax.ShapeDtypeStruct((ns, 2, W), F32),
                   jax.ShapeDtypeStruct((1, Du), F32)),
        grid=(ns,),
        in_specs=[ublk, uslab, pl.BlockSpec(memory_space=pl.ANY), sblk, sblk, lam, lam, mat,
                  pl.BlockSpec((None, 2 * W, LANES), lambda k: (k, 0, 0)), vec],
        out_specs=(uslab, mat, mat, pl.BlockSpec((None, 2, W), lambda k: (k, 0, 0)), vec),
        scratch_shapes=[pltpu.VMEM((L, W), F32), pltpu.VMEM((L, W), F32), pltpu.VMEM((L, LANES), F32),
                        pltpu.VMEM((L, LANES), F32)],
        input_output_aliases={2: 0}, compiler_params=_cparams(("arbitrary",), VMEM_LIMIT_S5),
    )(dy, proj4, dproj, s_re, s_im, lr, li, bmat.astype(BF16), cmat.astype(BF16), d.reshape(1, Du))


def _glu_fwd(yraw, wmat, bias, mixin, name):
    L, C = yraw.shape
    tr = _pick(L, prefs=(512, 256, 128))
    tb = tr // SCAN_CHUNKS
    nl = C // LANES

    def body(y_ref, w_ref, b_ref, m_in, o_ref, scr):
        yg = _gelu(y_ref[...])
        zz = jnp.dot(yg.astype(BF16), w_ref[...], preferred_element_type=F32) + b_ref[...]
        yb = yg * _sigmoid(zz)
        for k in range(nl):
            scr[k] = yb[:, k * LANES:(k + 1) * LANES]
        for c in range(SCAN_CHUNKS):
            for k in range(nl):
                o_ref[c, :, k * LANES:(k + 1) * LANES] = scr[k, pl.ds(c, tb, stride=SCAN_CHUNKS), :].astype(BF16)

    out = pl.pallas_call(
        body, name=name, out_shape=jax.ShapeDtypeStruct((SCAN_CHUNKS, L // SCAN_CHUNKS, 2 * C), BF16),
        grid=(L // tr,),
        in_specs=[pl.BlockSpec((tr, C), lambda i: (i, 0)), pl.BlockSpec((C, C), lambda i: (0, 0)),
                  pl.BlockSpec((1, C), lambda i: (0, 0)), pl.BlockSpec(memory_space=pl.ANY)],
        out_specs=pl.BlockSpec((SCAN_CHUNKS, tb, C), lambda i: (0, i, 1)),
        scratch_shapes=[pltpu.VMEM((nl, tr, LANES), F32)], input_output_aliases={3: 0},
        compiler_params=_cparams(("parallel",)),
    )(yraw, wmat, bias.reshape(1, C), mixin.reshape(SCAN_CHUNKS, L // SCAN_CHUNKS, 2 * C))
    return out.reshape(L, 2 * C)


def _glu_bwd(yraw, dmix, wmat, bias, name):
    L, C = yraw.shape
    tr = _pick(L, prefs=(512, 256, 128))
    nsteps = L // tr
    tb = tr // SCAN_CHUNKS
    nl = C // LANES

    def body(y_ref, d_ref, w_ref, b_ref, dy_ref, dw_ref, db_ref, acc_b, scr):
        i = pl.program_id(0)

        @pl.when(i == 0)
        def _():
            dw_ref[...] = jnp.zeros_like(dw_ref)
            acc_b[...] = jnp.zeros_like(acc_b)

        for c in range(SCAN_CHUNKS):
            for k in range(nl):
                scr[k, pl.ds(c, tb, stride=SCAN_CHUNKS), :] = d_ref[c, :, k * LANES:(k + 1) * LANES]
        yr = y_ref[...]
        yg = _gelu(yr)
        ygb = yg.astype(BF16)
        sg = _sigmoid(jnp.dot(ygb, w_ref[...], preferred_element_type=F32) + b_ref[...])
        dyb_ = jnp.concatenate([scr[k] for k in range(nl)], axis=1)
        dz = dyb_ * yg * sg * (1.0 - sg)
        dzb = dz.astype(BF16)
        dyg = dyb_ * sg + lax.dot_general(dzb, w_ref[...], (((1,), (1,)), ((), ())), preferred_element_type=F32)
        dw_ref[...] += lax.dot_general(ygb, dzb, (((0,), (0,)), ((), ())), preferred_element_type=F32)
        acc_b[...] += jnp.sum(dz.reshape(tr // SUBLANES, SUBLANES, C), axis=0)
        dy_ref[...] = dyg * _gelu_grad(yr)

        @pl.when(i == nsteps - 1)
        def _():
            db_ref[...] = jnp.sum(acc_b[...], axis=0, keepdims=True)

    row = pl.BlockSpec((tr, C), lambda i: (i, 0))
    return pl.pallas_call(
        body, name=name,
        out_shape=(jax.ShapeDtypeStruct((L, C), F32), jax.ShapeDtypeStruct((C, C), F32),
                   jax.ShapeDtypeStruct((1, C), F32)),
        grid=(nsteps,),
        in_specs=[row, pl.BlockSpec((SCAN_CHUNKS, tb, C), lambda i: (0, i, 1)), pl.BlockSpec((C, C), lambda i: (0, 0)),
                  pl.BlockSpec((1, C), lambda i: (0, 0))],
        out_specs=(row, pl.BlockSpec((C, C), lambda i: (0, 0)), pl.BlockSpec((1, C), lambda i: (0, 0))),
        scratch_shapes=[pltpu.VMEM((SUBLANES, C), F32), pltpu.VMEM((nl, tr, LANES), F32)],
        compiler_params=_cparams(("arbitrary",)),
    )(yraw, dmix.reshape(SCAN_CHUNKS, L // SCAN_CHUNKS, 2 * C), wmat, bias.reshape(1, C))


def _pool_counts(L, g):
    t = lax.broadcasted_iota(jnp.int32, (L, LANES), 0).astype(F32) + 1.0
    w = jnp.where(g == 0, 2.0, jnp.where(g == 1, 4.0, jnp.where(g == 2, 8.0, 16.0)))
    return 1.0 / jnp.minimum(t, w)


def _select_window(g, a2, a4, a8, a16):
    return jnp.where(g == 0, a2, jnp.where(g == 1, a4, jnp.where(g == 2, a8, a16)))


def _pooled(z, g):
    a2 = z + _down(z, 1)
    a4 = a2 + _down(a2, 2)
    a8 = a4 + _down(a4, 4)
    a16 = a8 + _down(a8, 8)
    return _select_window(g, a2, a4, a8, a16) * _pool_counts(z.shape[0], g) - z


def _transpose_on_mxu(yb):
    c = yb.shape[1]
    eye = lax.broadcasted_iota(jnp.int32, (c, c), 0) == lax.broadcasted_iota(jnp.int32, (c, c), 1)
    return lax.dot_general(eye.astype(BF16), yb, (((1,), (1,)), ((), ())), preferred_element_type=F32).astype(BF16)


def _pool_fwd(proj3, pool_w, scale, name):
    _, L, C = proj3.shape
    ng = len(POOL_WINDOWS)
    pg = C // ng
    assert pg == LANES

    def body(z_ref, w_ref, s_ref, o_ref, ot_ref):
        g = pl.program_id(0)
        p = _pooled(z_ref[...].astype(F32), g)
        y = jnp.dot(p.astype(BF16), w_ref[...].astype(BF16), preferred_element_type=F32)
        yb = (y * s_ref[...]).astype(BF16)
        o_ref[...] = yb
        ot_ref[...] = _transpose_on_mxu(yb)

    return pl.pallas_call(
        body, name=name, out_shape=(jax.ShapeDtypeStruct((L, 2 * C), BF16), jax.ShapeDtypeStruct((2 * C, L), BF16)),
        grid=(ng,),
        in_specs=[pl.BlockSpec((None, L, pg), lambda g: (0, 0, g)), pl.BlockSpec((None, pg, pg), lambda g: (g, 0, 0)),
                  pl.BlockSpec((1, pg), lambda g: (0, g))],
        out_specs=(pl.BlockSpec((L, pg), lambda g: (0, g)), pl.BlockSpec((pg, L), lambda g: (g, 0))),
        compiler_params=_cparams(("parallel",)),
    )(proj3, pool_w, scale.reshape(1, C))


def _pool_bwd(proj3, dmix, pool_w, scale, name):
    _, L, C = proj3.shape
    ng = len(POOL_WINDOWS)
    pg = C // ng

    def body(z_ref, d_ref, w_ref, s_ref, dz_ref, dw_ref, ds_ref):
        g = pl.program_id(0)
        p = _pooled(z_ref[...].astype(F32), g)
        pb = p.astype(BF16)
        wb = w_ref[...].astype(BF16)
        pre = jnp.dot(pb, wb, preferred_element_type=F32)
        dyc = d_ref[...]
        ds_ref[...] = jnp.sum(dyc * pre, axis=0, keepdims=True)
        dpre = (dyc * s_ref[...]).astype(BF16)
        dw_ref[...] = lax.dot_general(pb, dpre, (((0,), (0,)), ((), ())), preferred_element_type=F32)
        dp = lax.dot_general(dpre, wb, (((1,), (1,)), ((), ())), preferred_element_type=F32)
        v = dp * _pool_counts(L, g)
        a2 = v + _up(v, 1)
        a4 = a2 + _up(a2, 2)
        a8 = a4 + _up(a4, 4)
        a16 = a8 + _up(a8, 8)
        dz_ref[...] = (_select_window(g, a2, a4, a8, a16) - dp).astype(BF16)

    return pl.pallas_call(
        body, name=name,
        out_shape=(jax.ShapeDtypeStruct((L, C), BF16), jax.ShapeDtypeStruct((ng, pg, pg), F32),
                   jax.ShapeDtypeStruct((1, C), F32)),
        grid=(ng,),
        in_specs=[pl.BlockSpec((None, L, pg), lambda g: (0, 0, g)), pl.BlockSpec((L, pg), lambda g: (0, g)),
                  pl.BlockSpec((None, pg, pg), lambda g: (g, 0, 0)), pl.BlockSpec((1, pg), lambda g: (0, g))],
        out_specs=(pl.BlockSpec((L, pg), lambda g: (0, g)), pl.BlockSpec((None, pg, pg), lambda g: (g, 0, 0)),
                   pl.BlockSpec((1, pg), lambda g: (0, g))),
        compiler_params=_cparams(("parallel",)),
    )(proj3, dmix, pool_w, scale.reshape(1, C))


def _tril_w(w_ref, h):
    r = lax.broadcasted_iota(jnp.int32, (CHUNK, CHUNK), 0)
    c = lax.broadcasted_iota(jnp.int32, (CHUNK, CHUNK), 1)
    return jnp.where(r >= c, w_ref[h], 0.0)


def _sgu_fwd(proj3, norm_g, w, b, mixin, mixin_t, name):
    _, L, C = proj3.shape
    nh = w.shape[0]
    dh = C // nh
    assert dh == LANES and w.shape[1] == CHUNK
    tr = _pick(L, prefs=(512, 256, 128))
    bfull = jnp.broadcast_to(b[:, :, None], (nh, CHUNK, dh))

    def body(su_ref, sv_ref, g_ref, w_ref, b_ref, m_in, mt_in, o_ref, ot_ref):
        sv = _gelu(sv_ref[...].astype(F32))
        r = lax.rsqrt(jnp.mean(sv * sv, axis=-1, keepdims=True) + EPS)
        v = (sv * r * g_ref[...]).astype(BF16)
        for h in range(nh):
            wm = _tril_w(w_ref, h).astype(BF16)
            cols = slice(h * dh, (h + 1) * dh)
            for n in range(tr // CHUNK):
                rows = slice(n * CHUNK, (n + 1) * CHUNK)
                mixed = jnp.dot(wm, v[rows, cols], preferred_element_type=F32) + b_ref[h]
                o_ref[rows, cols] = (_gelu(su_ref[rows, cols].astype(F32)) * mixed).astype(BF16)
        ot_ref[...] = _transpose_on_mxu(o_ref[...])

    full = lambda shp: pl.BlockSpec(shp, lambda i: (0,) * len(shp))
    anywhere = pl.BlockSpec(memory_space=pl.ANY)
    return pl.pallas_call(
        body, name=name, out_shape=(jax.ShapeDtypeStruct(mixin.shape, BF16), jax.ShapeDtypeStruct(mixin_t.shape, BF16)),
        grid=(L // tr,),
        in_specs=[pl.BlockSpec((None, tr, C), lambda i: (1, i, 0)), pl.BlockSpec((None, tr, C), lambda i: (2, i, 0)),
                  full((1, C)), full((nh, CHUNK, CHUNK)), full((nh, CHUNK, dh)), anywhere, anywhere],
        out_specs=(pl.BlockSpec((tr, C), lambda i: (i, 1)), pl.BlockSpec((C, tr), lambda i: (1, i))),
        input_output_aliases={5: 0, 6: 1}, compiler_params=_cparams(("parallel",)),
    )(proj3, proj3, norm_g.reshape(1, C), w, bfull, mixin, mixin_t)


def _sgu_bwd(proj3, dmix, dz, norm_g, w, b, name):
    _, L, C = proj3.shape
    nh = w.shape[0]
    dh = C // nh
    tr = _pick(L, prefs=(512, 256, 128))
    nsteps = L // tr
    bfull = jnp.broadcast_to(b[:, :, None], (nh, CHUNK, dh))

    def body(su_ref, sv_ref, d_ref, dz_ref, g_ref, w_ref, b_ref, o_ref, dw_ref, db_ref, dg_ref, dv_ref, acc_g):
        i = pl.program_id(0)
        o_ref[0] = dz_ref[...]

        @pl.when(i == 0)
        def _():
            dw_ref[...] = jnp.zeros_like(dw_ref)
            db_ref[...] = jnp.zeros_like(db_ref)
            acc_g[...] = jnp.zeros_like(acc_g)

        svp = sv_ref[...].astype(F32)
        sv = _gelu(svp)
        r = lax.rsqrt(jnp.mean(sv * sv, axis=-1, keepdims=True) + EPS)
        vh = sv * r
        gv = g_ref[...]
        v = (vh * gv).astype(BF16)
        tri_r = lax.broadcasted_iota(jnp.int32, (CHUNK, CHUNK), 0)
        tri_c = lax.broadcasted_iota(jnp.int32, (CHUNK, CHUNK), 1)
        for h in range(nh):
            wm = _tril_w(w_ref, h).astype(BF16)
            cols = slice(h * dh, (h + 1) * dh)
            dwh = jnp.zeros((CHUNK, CHUNK), F32)
            dbh = jnp.zeros((CHUNK, dh), F32)
            for n in range(tr // CHUNK):
                rows = slice(n * CHUNK, (n + 1) * CHUNK)
                vb = v[rows, cols]
                mixed = jnp.dot(wm, vb, preferred_element_type=F32) + b_ref[h]
                sup = su_ref[rows, cols].astype(F32)
                dyd = d_ref[rows, cols]
                dmx = dyd * _gelu(sup)
                o_ref[1, rows, cols] = (dyd * mixed * _gelu_grad(sup)).astype(BF16)
                dmb = dmx.astype(BF16)
                dwh = dwh + lax.dot_general(dmb, vb, (((1,), (1,)), ((), ())), preferred_element_type=F32)
                dbh = dbh + dmx
                dv_ref[rows, cols] = lax.dot_general(wm, dmb, (((0,), (0,)), ((), ())), preferred_element_type=F32)
            dw_ref[h] += jnp.where(tri_r >= tri_c, dwh, 0.0)
            db_ref[h] += dbh
        dv = dv_ref[...]
        acc_g[...] += jnp.sum((dv * vh).reshape(tr // SUBLANES, SUBLANES, C), axis=0)
        dvg = dv * gv
        dsv = r * (dvg - vh * jnp.mean(dvg * vh, axis=-1, keepdims=True))
        o_ref[2] = (dsv * _gelu_grad(svp)).astype(BF16)

        @pl.when(i == nsteps - 1)
        def _():
            dg_ref[...] = jnp.sum(acc_g[...], axis=0, keepdims=True)

    full = lambda shp: pl.BlockSpec(shp, lambda i: (0,) * len(shp))
    return pl.pallas_call(
        body, name=name,
        out_shape=(jax.ShapeDtypeStruct((3, L, C), BF16), jax.ShapeDtypeStruct((nh, CHUNK, CHUNK), F32),
                   jax.ShapeDtypeStruct((nh, CHUNK, dh), F32), jax.ShapeDtypeStruct((1, C), F32)),
        grid=(nsteps,),
        in_specs=[pl.BlockSpec((None, tr, C), lambda i: (1, i, 0)), pl.BlockSpec((None, tr, C), lambda i: (2, i, 0)),
                  pl.BlockSpec((tr, C), lambda i: (i, 1)), pl.BlockSpec((tr, C), lambda i: (i, 0)), full((1, C)),
                  full((nh, CHUNK, CHUNK)), full((nh, CHUNK, dh))],
        out_specs=(pl.BlockSpec((3, tr, C), lambda i: (0, i, 0)), full((nh, CHUNK, CHUNK)), full((nh, CHUNK, dh)),
                   full((1, C))),
        scratch_shapes=[pltpu.VMEM((tr, C), F32), pltpu.VMEM((SUBLANES, C), F32)],
        compiler_params=_cparams(("arbitrary",)),
    )(proj3, proj3, dmix, dz, norm_g.reshape(1, C), w, bfull)


def _ffn_act_fwd(up3, conv_w, conv_b, name):
    _, L, Fh = up3.shape
    cb = LANES
    w2 = conv_w.reshape(3, 2, Fh).transpose(1, 0, 2)
    b2 = conv_b.reshape(2, 1, Fh)

    def body(u_ref, w_ref, b_ref, o_ref, ot_ref, gv_ref):
        g = _conv3(_taps(u_ref[0].astype(F32)), w_ref[0]) + b_ref[0]
        v = _conv3(_taps(u_ref[1].astype(F32)), w_ref[1]) + b_ref[1]
        gv_ref[0] = g.astype(BF16)
        gv_ref[1] = v.astype(BF16)
        ab = (g * _sigmoid(g) * v).astype(BF16)
        o_ref[...] = ab
        ot_ref[...] = _transpose_on_mxu(ab)

    blk3 = pl.BlockSpec((2, L, cb), lambda j: (0, 0, j))
    return pl.pallas_call(
        body, name=name,
        out_shape=(jax.ShapeDtypeStruct((L, Fh), BF16), jax.ShapeDtypeStruct((Fh, L), BF16),
                   jax.ShapeDtypeStruct((2, L, Fh), BF16)),
        grid=(Fh // cb,),
        in_specs=[blk3, pl.BlockSpec((2, 3, cb), lambda j: (0, 0, j)), pl.BlockSpec((2, 1, cb), lambda j: (0, 0, j))],
        out_specs=(pl.BlockSpec((L, cb), lambda j: (0, j)), pl.BlockSpec((cb, L), lambda j: (j, 0)), blk3),
        compiler_params=_cparams(("parallel",)),
    )(up3, w2, b2)


def _ffn_act_bwd(up3, gv3, da, conv_w, h2t, name):
    _, L, Fh = up3.shape
    D = h2t.shape[0]
    cb = LANES
    nb = Fh // cb
    w2 = conv_w.reshape(3, 2, Fh).transpose(1, 0, 2)

    def body(u_ref, gv_ref, d_ref, w_ref, h_ref, o_ref, dw_ref, db_ref, wg_ref, wv_ref, scr):
        j = pl.program_id(0)

        @pl.when(j == 0)
        def _():
            scr[1] = jnp.zeros((2, L, cb), BF16)

        prev = scr.at[(j + 1) % 2]
        wg_ref[...] = jnp.dot(h_ref[...], prev[0], preferred_element_type=F32).astype(BF16)
        wv_ref[...] = jnp.dot(h_ref[...], prev[1], preferred_element_type=F32).astype(BF16)
        tg, tv = _taps(u_ref[0].astype(F32)), _taps(u_ref[1].astype(F32))
        wg, wv = w_ref[0], w_ref[1]
        g = gv_ref[0].astype(F32)
        v = gv_ref[1].astype(F32)
        sg = _sigmoid(g)
        dav = d_ref[...].astype(F32)
        dg = dav * v * (sg * (1.0 + g * (1.0 - sg)))
        dv = dav * (g * sg)
        dug = _conv3_t(dg, wg).astype(BF16)
        duv = _conv3_t(dv, wv).astype(BF16)
        o_ref[0] = dug
        o_ref[1] = duv
        cur = scr.at[j % 2]
        cur[0] = dug
        cur[1] = duv
        for tap, (dwg, dwv) in enumerate(zip(_conv3_dw(dg, tg), _conv3_dw(dv, tv))):
            dw_ref[0, tap:tap + 1, :] = dwg
            dw_ref[1, tap:tap + 1, :] = dwv
        db_ref[0] = jnp.sum(dg, axis=0, keepdims=True)
        db_ref[1] = jnp.sum(dv, axis=0, keepdims=True)

    here = lambda j: jnp.minimum(j, nb - 1)
    before = lambda j: jnp.maximum(j - 1, 0)
    blk3 = pl.BlockSpec((2, L, cb), lambda j: (0, 0, here(j)))
    dup, dw2, db2, dwg, dwv = pl.pallas_call(
        body, name=name,
        out_shape=(jax.ShapeDtypeStruct((2, L, Fh), BF16), jax.ShapeDtypeStruct((2, 3, Fh), F32),
                   jax.ShapeDtypeStruct((2, 1, Fh), F32), jax.ShapeDtypeStruct((D, Fh), BF16),
                   jax.ShapeDtypeStruct((D, Fh), BF16)),
        grid=(nb + 1,),
        in_specs=[blk3, blk3, pl.BlockSpec((L, cb), lambda j: (0, here(j))),
                  pl.BlockSpec((2, 3, cb), lambda j: (0, 0, here(j))), pl.BlockSpec((D, L), lambda j: (0, 0))],
        out_specs=(blk3, pl.BlockSpec((2, 3, cb), lambda j: (0, 0, here(j))),
                   pl.BlockSpec((2, 1, cb), lambda j: (0, 0, here(j))),
                   pl.BlockSpec((D, cb), lambda j: (0, before(j))), pl.BlockSpec((D, cb), lambda j: (0, before(j)))),
        scratch_shapes=[pltpu.VMEM((2, 2, L, cb), BF16)],
        compiler_params=_cparams(("arbitrary",), VMEM_LIMIT_S5),
    )(up3, gv3, da, w2, h2t)
    return dup, dw2.transpose(1, 0, 2).reshape(3, 2 * Fh), db2.reshape(2 * Fh), jnp.concatenate([dwg, dwv], axis=1)


def _local_step(x, tgt, w, layer_weights, on_layer_grads):
    L, D = x.shape
    depth = w['norm_mix_g'].shape[0]
    saved = []
    for i in range(depth):
        j = i // 2
        wb = dict(layer_weights(2 * i, x))
        s = {'x': x, 'wb': wb}
        if i % 2 == 0:
            proj4, s['hT'] = _norm_mm(x, w['norm_mix_g'][i], wb['even_w_in'], BF16, "even_in_fwd", ok=('seg', 4))
            s['proj'] = proj4
            mixin = _sconv_fwd(proj4, w['even_conv_w'][j], "sconv_fwd")
            prm = (w['ssm_log_step'][j], w['ssm_a_re'][j], w['ssm_a_im'][j], w['ssm_b_re'][j], w['ssm_b_im'][j],
                   w['ssm_c_re'][j], w['ssm_c_im'][j])
            (lr, li, bmat, cmat), prep_vjp = jax.vjp(_s5_prep, *prm)
            yraw, s_re, s_im = _s5_fwd(proj4, lr, li, bmat, cmat, w['ssm_d'][j], "s5_fwd")
            mixin = _glu_fwd(yraw, wb['ssm_glu_w'], w['ssm_glu_b'][j], mixin, "glu_fwd")
            s.update(yraw=yraw, s_re=s_re, s_im=s_im, s5=(lr, li, bmat, cmat), prep_vjp=prep_vjp)
            s['mixinT'] = mixin.T
            x = _mm(mixin, wb['even_w_out'], 'nn', F32, "even_out_fwd", res=x)
        else:
            proj3, s['hT'] = _norm_mm(x, w['norm_mix_g'][i], wb['odd_w_in'], BF16, "odd_in_fwd", ok=('seg', 3))
            s['proj'] = proj3
            mixin, mixin_t = _pool_fwd(proj3, w['pool_w'][j], w['pool_scale'][j], "pool_fwd")
            mixin, s['mixinT'] = _sgu_fwd(proj3, w['sgu_norm_g'][j], w['sgu_w'][j], w['sgu_b'][j], mixin, mixin_t,
                                          "sgu_fwd")
            x = _mm(mixin, wb['odd_w_out'], 'nn', F32, "odd_out_fwd", res=x)
        s['x1'] = x
        wb.update(layer_weights(2 * i + 1, x))
        up3, h2t = _norm_mm(x, w['norm_ffn_g'][i], wb['ffn_w_up'], BF16, "ffn_up_fwd", ok=('seg', 2))
        a, at, gv3 = _ffn_act_fwd(up3, w['ffn_conv_w'][i], w['ffn_conv_b'][i], "ffn_act_fwd")
        x = _mm(a, wb['ffn_w_down'], 'nn', F32, "ffn_down_fwd", res=x)
        s.update(h2T=h2t, up3=up3, aT=at, gv3=gv3)
        saved.append(s)

    loss8, dx, dxb, dg_final = _loss_head(x, w['norm_final_g'], tgt)
    gs = {n: [None] * w[n].shape[0] for n in SMALL if n != 'norm_final_g'}
    gs['norm_final_g'] = dg_final.reshape(D)

    dep = None
    for i in reversed(range(depth)):
        j = i // 2
        s = saved[i]
        wb = s['wb']
        gb = {}
        da = _mm(dxb, wb['ffn_w_down'], 'nt', BF16, "ffn_down_dgrad", dep=dep)
        gb['ffn_w_down'] = _mm(s['aT'], dxb, 'nn', BF16, "ffn_down_wgrad")
        dup3, dcw, dcb, gb['ffn_w_up'] = _ffn_act_bwd(s['up3'], s['gv3'], da, w['ffn_conv_w'][i], s['h2T'],
                                                      "ffn_act_bwd")
        gs['ffn_conv_w'][i], gs['ffn_conv_b'][i] = dcw, dcb
        dep = on_layer_grads(2 * i + 1, gb)
        dx, dxb, dg = _mm_norm_bwd(dup3, wb['ffn_w_up'], s['x1'], w['norm_ffn_g'][i], dx, "ffn_up_dgrad",
                              ak=('seg', 2), dep=dep)
        gs['norm_ffn_g'][i] = dg.reshape(D)
        gb = {}
        if i % 2 == 0:
            dmix = _mm(dxb, wb['even_w_out'], 'nt', F32, "even_out_dgrad")
            gb['even_w_out'] = _mm(s['mixinT'], dxb, 'nn', BF16, "even_out_wgrad")
            dproj, dcw = _sconv_bwd(s['proj'], dmix, w['even_conv_w'][j], "sconv_bwd")
            gs['even_conv_w'][j] = dcw
            dyraw, dglu_w, dglu_b = _glu_bwd(s['yraw'], dmix, wb['ssm_glu_w'], w['ssm_glu_b'][j], "glu_bwd")
            gb['ssm_glu_w'] = dglu_w.astype(BF16)
            gs['ssm_glu_b'][j] = dglu_b.reshape(-1)
            lr, li, bmat, cmat = s['s5']
            dproj, dbm, dcm, dlam, dd = _s5_bwd(dyraw, s['proj'], dproj, s['s_re'], s['s_im'], lr, li, bmat, cmat,
                                               w['ssm_d'][j], "s5_bwd")
            gs['ssm_d'][j] = dd.reshape(-1)
            dcm = jnp.swapaxes(dcm, 1, 2)
            dprm = s['prep_vjp']((dlam[:, 0:1, :], dlam[:, 1:2, :], dbm, dcm))
            for n, gval in zip(('ssm_log_step', 'ssm_a_re', 'ssm_a_im', 'ssm_b_re', 'ssm_b_im', 'ssm_c_re',
                                'ssm_c_im'), dprm):
                gs[n][j] = gval
            gb['even_w_in'] = _mm(s['hT'], dproj, 'nn', BF16, "even_in_wgrad", bk=('seg', 4))
            w_in, in_kind, in_name = wb['even_w_in'], ('seg', 4), "even_in_dgrad"
        else:
            dmix = _mm(dxb, wb['odd_w_out'], 'nt', F32, "odd_out_dgrad")
            gb['odd_w_out'] = _mm(s['mixinT'], dxb, 'nn', BF16, "odd_out_wgrad")
            dz, dpw, dps = _pool_bwd(s['proj'], dmix, w['pool_w'][j], w['pool_scale'][j], "pool_bwd")
            gs['pool_w'][j], gs['pool_scale'][j] = dpw, dps.reshape(-1)
            dproj, dsw, dsb, dsg = _sgu_bwd(s['proj'], dmix, dz, w['sgu_norm_g'][j], w['sgu_w'][j], w['sgu_b'][j],
                                            "sgu_bwd")
            gs['sgu_w'][j], gs['sgu_b'][j], gs['sgu_norm_g'][j] = dsw, jnp.sum(dsb, axis=-1), dsg.reshape(-1)
            gb['odd_w_in'] = _mm(s['hT'], dproj, 'nn', BF16, "odd_in_wgrad", bk=('seg', 3))
            w_in, in_kind, in_name = wb['odd_w_in'], ('seg', 3), "odd_in_dgrad"
        dep = on_layer_grads(2 * i, gb)
        dx, dxb, dg = _mm_norm_bwd(dproj, w_in, s['x'], w['norm_mix_g'][i], dx, in_name, ak=in_kind, dep=dep)
        gs['norm_mix_g'][i] = dg.reshape(D)

    gsmall = {n: (v if n == 'norm_final_g' else jnp.stack(v)) for n, v in gs.items()}
    return loss8[0, 0], dx, gsmall


_HBM = pl.BlockSpec(memory_space=pltpu.HBM)
_CHIP_FLIPS = ((0, 0), (1, 0), (0, 1), (1, 1))


def _coords():
    return lax.axis_index("x"), lax.axis_index("y"), lax.axis_index("c")


def _flip(v, f):
    return 1 - v if f else v


def _shard_of(ref, axis, s, width):
    start = pl.multiple_of(s * width, LANES if axis == ref.ndim - 1 else 16) if width % 16 == 0 else s * width
    idx = [slice(None)] * ref.ndim
    idx[axis] = pl.ds(start, width)
    return ref.at[tuple(idx)]


_SEM = pl.BlockSpec(memory_space=pltpu.SEMAPHORE)
_ANY = pl.BlockSpec(memory_space=pl.ANY)
_DATAFLOW = pltpu.SideEffectType.DATAFLOW_SIDE_EFFECTING


def _in_hbm(a):
    return pltpu.with_memory_space_constraint(a, pltpu.HBM)


def _model_layer(name, l):
    if name.startswith('ffn'):
        return l
    return 2 * l + 1 if name.startswith('odd') else 2 * l


def _place_quarter(shard, l, axis, chip, dtype, dep=None):
    _, r, c = shard.shape
    tr = _pick(r, prefs=(512, 256, 128, 64, 32, 16))
    nrb = r // tr

    def body(chip_ref, i_ref, *rest):
        rest[-1][...] = i_ref[...].astype(dtype)

    if axis == 1:
        out_shape, o_map = (r, c * N_CHIPS), (lambda i, s: (i, s[0]))
    else:
        out_shape, o_map = (r * N_CHIPS, c), (lambda i, s: (s[0] * nrb + i, 0))
    in_specs = [pl.BlockSpec((None, tr, c), lambda i, s: (l, i, 0))]
    args = [chip, shard]
    if dep is not None:
        in_specs.append(pl.BlockSpec(memory_space=pl.ANY))
        args.append(dep)
    return pl.pallas_call(
        body, name="place_quarter", out_shape=jax.ShapeDtypeStruct(out_shape, dtype),
        grid_spec=pltpu.PrefetchScalarGridSpec(
            num_scalar_prefetch=1, grid=(nrb,), in_specs=in_specs, out_specs=pl.BlockSpec((tr, c), o_map)),
        compiler_params=_cparams(("parallel",)),
    )(*args)


def _gather_copies(land_refs, send_sem, recv_sem, axes, landing_chip_of, first=0):
    x, y, c = _coords()
    out = []
    for j, land in enumerate(land_refs):
        width = land.shape[axes[j]] // N_CHIPS
        for f in (1, 2, 3):
            fx, fy = _CHIP_FLIPS[f]
            px, py = _flip(x, fx), _flip(y, fy)
            lx, ly = landing_chip_of(px, py)
            out.append(pltpu.make_async_remote_copy(
                src_ref=_shard_of(land, axes[j], 2 * x + y, width), dst_ref=_shard_of(land, axes[j], 2 * lx + ly, width),
                send_sem=send_sem.at[3 * (first + j) + f - 1], recv_sem=recv_sem.at[3 * (first + j) + f - 1],
                device_id=(px, py, c), device_id_type=MESH))
    return out


def _gather_start(tag, lands, axes, dep=None):
    n = len(lands)

    def body(*refs):
        land_refs, send_sem, recv_sem = refs[:n], refs[-3], refs[-2]
        x, y, _ = _coords()
        for cp in _gather_copies(land_refs, send_sem, recv_sem, axes, lambda px, py: (x, y)):
            cp.start()
        refs[-1][...] = jnp.zeros_like(refs[-1])

    thru = [pltpu.HBM(a.shape, a.dtype) for a in lands]
    outs = pl.pallas_call(
        body, name=f"gather_start_{tag}",
        out_shape=tuple(thru + [pltpu.SemaphoreType.DMA((3 * n,)), pltpu.SemaphoreType.DMA((3 * n,)),
                                jax.ShapeDtypeStruct((SUBLANES, LANES), F32)]),
        in_specs=[_HBM] * n + ([_ANY] if dep is not None else []),
        out_specs=tuple([_HBM] * n + [_SEM, _SEM, pl.BlockSpec(memory_space=pltpu.VMEM)]),
        input_output_aliases={i: i for i in range(n)},
        compiler_params=pltpu.CompilerParams(has_side_effects=_DATAFLOW),
    )(*[_in_hbm(a) for a in lands], *([dep] if dep is not None else []))
    return list(outs[:n]), outs[n], outs[n + 1], outs[n + 2]


def _gather_wait(tag, lands, send_sem, recv_sem, axes, after, first=0):
    n = len(lands)

    def body(*refs):
        for cp in _gather_copies(refs[:n], refs[n], refs[n + 1], axes, lambda px, py: (px, py), first):
            cp.wait_send()
            cp.wait_recv()

    outs = pl.pallas_call(
        body, name=f"gather_wait_{tag}", out_shape=tuple(pltpu.HBM(a.shape, a.dtype) for a in lands),
        in_specs=[_HBM] * n + [_SEM, _SEM, _ANY], out_specs=tuple([_HBM] * n),
        input_output_aliases={i: i for i in range(n)},
        compiler_params=pltpu.CompilerParams(has_side_effects=_DATAFLOW),
    )(*lands, send_sem, recv_sem, after)
    return list(outs)


N_SLOTS = N_DEV - 1


def _scatter_sends(grad_refs, land_refs, send_sem, recv_sem, meta):
    x, y, c = _coords()
    out = []
    for j, (axis, owner, q, width) in enumerate(meta):
        other = c if owner == 0 else 1 - c
        for f, (fx, fy) in enumerate(_CHIP_FLIPS):
            px, py = _flip(x, fx), _flip(y, fy)
            slot = f + 4 * other - 1
            out.append((other if f == 0 else None, pltpu.make_async_remote_copy(
                src_ref=_shard_of(grad_refs[j], axis, 2 * px + py, width), dst_ref=land_refs[j].at[q, slot],
                send_sem=send_sem.at[4 * j + f], recv_sem=recv_sem.at[N_SLOTS * j + slot],
                device_id=(px, py, owner), device_id_type=MESH)))
    return out


def _scatter_start(layer, grads, lands, meta):
    n = len(grads)
    uniq = []
    for a in lands:
        if not any(a is u for u in uniq):
            uniq.append(a)
    which = [next(k for k, u in enumerate(uniq) if u is a) for a in lands]
    nu = len(uniq)

    def body(*refs):
        grad_refs, land_u = refs[:n], refs[n:n + nu]
        send_sem, recv_sem = refs[n + nu], refs[n + nu + 1]
        for other, cp in _scatter_sends(grad_refs, [land_u[k] for k in which], send_sem, recv_sem, meta):
            if other is None:
                cp.start()
            else:
                pl.when(other == 1)(cp.start)
        refs[-1][...] = jnp.zeros_like(refs[-1])

    thru = [pltpu.HBM(a.shape, a.dtype) for a in list(grads) + uniq]
    outs = pl.pallas_call(
        body, name=f"scatter_start_{layer}",
        out_shape=tuple([pltpu.SemaphoreType.DMA((4 * n,)), pltpu.SemaphoreType.DMA((N_SLOTS * n,))] + thru
                        + [jax.ShapeDtypeStruct((SUBLANES, LANES), F32)]),
        in_specs=[_HBM] * (n + nu),
        out_specs=tuple([_SEM, _SEM] + [_HBM] * (n + nu) + [pl.BlockSpec(memory_space=pltpu.VMEM)]),
        input_output_aliases={i: 2 + i for i in range(n + nu)},
        compiler_params=pltpu.CompilerParams(has_side_effects=_DATAFLOW),
    )(*[_in_hbm(a) for a in list(grads) + uniq])
    new_lands = [outs[2 + n + k] for k in which]
    return outs[0], outs[1], list(outs[2:2 + n]), new_lands, outs[-1]


def _scatter_wait(started, lands):
    nl = len(lands)
    flat_grads = [g for s in started for g in s[2]]
    ng, ns = len(flat_grads), len(started)

    def body(*refs):
        land_refs = refs[:nl]
        grad_refs = refs[nl:nl + ng]
        sem_refs = refs[nl + ng:nl + ng + 2 * ns]
        _, _, c = _coords()
        off = 0
        for k, (_, _, grads, idx, meta) in enumerate(started):
            send_sem, recv_sem = sem_refs[2 * k], sem_refs[2 * k + 1]
            lr = [land_refs[i] for i in idx]
            for other, cp in _scatter_sends(grad_refs[off:off + len(grads)], lr, send_sem, recv_sem, meta):
                if other is None:
                    cp.wait_send()
                else:
                    pl.when(other == 1)(cp.wait_send)
            for j, (axis, owner, q, width) in enumerate(meta):
                mine = (c if owner == 0 else 1 - c) == 0

                @pl.when(mine)
                def _():
                    for slot in range(N_SLOTS):
                        land = lr[j].at[q, slot]
                        pltpu.make_async_remote_copy(
                            src_ref=land, dst_ref=land, send_sem=send_sem.at[0], recv_sem=recv_sem.at[N_SLOTS * j + slot],
                            device_id=_coords(), device_id_type=MESH).wait_recv()
            off += len(grads)

    args = list(lands) + flat_grads
    thru = [pltpu.HBM(a.shape, a.dtype) for a in args]
    sems = [s for st in started for s in st[:2]]
    outs = pl.pallas_call(
        body, name="scatter_wait", out_shape=tuple(thru), in_specs=[_HBM] * (nl + ng) + [_SEM] * (2 * ns),
        out_specs=tuple([_HBM] * (nl + ng)), input_output_aliases={i: i for i in range(nl + ng)},
        compiler_params=pltpu.CompilerParams(has_side_effects=_DATAFLOW),
    )(*args, *sems)
    return list(outs[:nl]), list(outs[nl:])


def _sum_and_share(recv, layer_grads, axis, chip, name, dep=None):
    n, ns, r, c = recv.shape
    tr = _pick(r, prefs=(256, 128, 64, 32, 16))
    nr = r // tr
    nsteps = n * nr
    nlay = len(layer_grads)
    own_map = (lambda h, i, s: (i, s[0])) if axis == 1 else (lambda h, i, s: (s[0] * nr + i, 0))

    def body(chip_ref, i_ref, *rest):
        g_refs = rest[:nlay]
        o_ref, buf, loc_sems, send_sems, recv_sems = rest[nlay + (dep is not None):]
        h, i = pl.program_id(0), pl.program_id(1)
        step = h * nr + i
        slot = step % 2
        x, y, core = _coords()
        layer = core * n + h
        own = g_refs[0][...]
        for l in range(1, nlay):
            own = jnp.where(layer == l, g_refs[l][...], own)

        def copies(sl):
            dst = o_ref.at[core * n + h, pl.ds(pl.multiple_of(i * tr, tr), tr), :]
            loc = pltpu.make_async_copy(buf.at[sl], dst, loc_sems.at[sl])
            rem = pltpu.make_async_remote_copy(
                src_ref=buf.at[sl], dst_ref=dst, send_sem=send_sems.at[sl], recv_sem=recv_sems.at[step],
                device_id=(x, y, 1 - core), device_id_type=MESH)
            return loc, rem

        def drain(sl):
            loc, rem = copies(sl)
            loc.wait()
            rem.wait_send()

        pl.when(step >= 2)(lambda: drain(slot))
        acc = own.astype(F32)
        for s in range(ns):
            acc = acc + i_ref[s].astype(F32)
        buf[slot] = acc
        loc, rem = copies(slot)
        loc.start()
        rem.start()

        @pl.when(step == nsteps - 1)
        def _():
            drain(slot)
            if nsteps > 1:
                drain(1 - slot)
            for hh in range(n):
                for ii in range(nr):
                    land = o_ref.at[(1 - core) * n + hh, pl.ds(ii * tr, tr), :]
                    pltpu.make_async_remote_copy(
                        src_ref=buf.at[0], dst_ref=land, send_sem=send_sems.at[0], recv_sem=recv_sems.at[hh * nr + ii],
                        device_id=(x, y, 1 - core), device_id_type=MESH).wait_recv()

    return pl.pallas_call(
        body, name=name, out_shape=jax.ShapeDtypeStruct((2 * n, r, c), F32),
        grid_spec=pltpu.PrefetchScalarGridSpec(
            num_scalar_prefetch=1, grid=(n, nr),
            in_specs=[pl.BlockSpec((None, ns, tr, c), lambda h, i, s: (h, 0, i, 0))]
            + [pl.BlockSpec((tr, c), own_map)] * nlay + ([pl.BlockSpec(memory_space=pl.ANY)] if dep is not None else []),
            out_specs=_HBM,
            scratch_shapes=[pltpu.VMEM((2, tr, c), F32), pltpu.SemaphoreType.DMA((2,)),
                            pltpu.SemaphoreType.DMA((2,)), pltpu.SemaphoreType.DMA((nsteps,))]),
        compiler_params=_cparams(("arbitrary", "arbitrary")),
    )(chip, recv, *layer_grads, *([dep] if dep is not None else []))


def _adamw_update(w_ref, g_ref, m_ref, v_ref, d_ref, mo_ref, vo_ref):
    bc1 = 1.0 - ADAM_B1 ** ADAM_STEP
    bc2 = 1.0 - ADAM_B2 ** ADAM_STEP
    gv = g_ref[...]
    mn = ADAM_B1 * m_ref[...] + (1.0 - ADAM_B1) * gv
    vn = ADAM_B2 * v_ref[...] + (1.0 - ADAM_B2) * (gv * gv)
    d_ref[...] = -ADAM_LR * ((mn / bc1) / (jnp.sqrt(vn / bc2) + ADAM_EPS) + ADAM_WD * w_ref[...])
    mo_ref[...] = mn
    vo_ref[...] = vn


def _adamw(w, g, m, v, name):
    def body(*refs):
        _adamw_update(*refs)

    tr = _pick(w.shape[0], prefs=(256, 128, 64, 32, 16, 8))
    blk = pl.BlockSpec((tr, w.shape[1]), lambda i: (i, 0))
    sds = jax.ShapeDtypeStruct(w.shape, F32)
    return pl.pallas_call(
        body, name=name, out_shape=(sds, sds, sds), grid=(w.shape[0] // tr,), in_specs=[blk] * 4,
        out_specs=(blk,) * 3, compiler_params=_cparams(("parallel",)),
    )(w, g, m, v)


def _adamw_many(tensors, name, by_layer=False):
    n = len(tensors)

    def body(*refs):
        for t in range(n):
            _adamw_update(*refs[4 * t:4 * t + 4], *refs[4 * n + 3 * t:4 * n + 3 * t + 3])

    def spec(a):
        nd = a.ndim
        if by_layer:
            return pl.BlockSpec((1,) + a.shape[1:], lambda i: (i,) + (0,) * (nd - 1))
        return pl.BlockSpec(a.shape, lambda i: (0,) * nd)

    steps = tensors[0][0].shape[0] if by_layer else 1
    outs = pl.pallas_call(
        body, name=name, out_shape=tuple(jax.ShapeDtypeStruct(t[0].shape, F32) for t in tensors for _ in range(3)),
        grid=(steps,), in_specs=[spec(a) for t in tensors for a in t],
        out_specs=tuple(spec(t[0]) for t in tensors for _ in range(3)), compiler_params=_cparams(("parallel",)),
    )(*[a for t in tensors for a in t])
    return [tuple(outs[3 * t:3 * t + 3]) for t in range(n)]


_PACK_QUANTUM = 256 * LANES


def _pack(arrs):
    flat = jnp.concatenate([a.reshape(-1).astype(F32) for a in arrs])
    flat = jnp.pad(flat, (0, (-flat.shape[0]) % _PACK_QUANTUM))
    return flat.reshape(-1, LANES)


def _unpack(p, shapes):
    flat = p.reshape(-1)
    out, off = [], 0
    for s in shapes:
        n = int(np.prod(s))
        out.append(flat[off:off + n].reshape(s))
        off += n
    return out


def kernel(*args):
    nw = len(WEIGHTS)
    x, tgt = args[0], args[1 + nw]
    w = dict(zip(WEIGHTS, args[1:1 + nw]))
    m = dict(zip(WEIGHTS, args[2 + nw:2 + 2 * nw]))
    v = dict(zip(WEIGHTS, args[2 + 2 * nw:2 + 3 * nw]))
    _, L, D = x.shape
    chip = 2 * lax.axis_index("x") + lax.axis_index("y")

    big = list(BIG)
    small_sh_shapes = [w[n].shape for n in SMALL_SHARDED]
    nbig = len(big)
    chip1 = chip.reshape(1).astype(jnp.int32)
    axes2 = [BIG[n] - 1 for n in big] + [0]
    shards = [w[n] for n in big] + [_pack([w[n] for n in SMALL_SHARDED])[None]]
    pairs = [(t, l) for t in range(nbig + 1) for l in range(shards[t].shape[0])]
    depth = w['norm_mix_g'].shape[0]
    part_of = lambda t, l: 0 if t == nbig else 2 * _model_layer(big[t], l) + big[t].startswith('ffn')
    flying, token = {}, None
    for tag, gset in enumerate(([0], list(range(1, 2 * depth)))):
        ids = [k for g in gset for k, (t, l) in enumerate(pairs) if part_of(t, l) == g]
        ts = [pairs[k][0] for k in ids]
        placed = [_place_quarter(shards[t], pairs[k][1], axes2[t], chip1, F32 if t == nbig else BF16, token)
                  for k, t in zip(ids, ts)]
        lands, send, recv, token = _gather_start(tag, placed, [axes2[t] for t in ts], token)
        first = 0
        for g in gset:
            n = sum(1 for t, l in pairs if part_of(t, l) == g)
            flying[g] = (ts[first:first + n], lands[first:first + n], send, recv, first)
            first += n

    def wait_group(g, after):
        ts, lands, send, recv, first = flying[g]
        landed = _gather_wait(g, lands, send, recv, [axes2[t] for t in ts], token if after is None else after, first)
        return dict(zip(ts, landed))

    first = wait_group(0, None)
    packed = first.pop(nbig).reshape(N_CHIPS, -1, LANES)
    per_chip = [_unpack(packed[s], small_sh_shapes) for s in range(N_CHIPS)]
    wl = dict(w)
    for k, n in enumerate(SMALL_SHARDED):
        wl[n] = jnp.concatenate([per_chip[s][k] for s in range(N_CHIPS)], axis=-1)

    def layer_weights(i, after):
        got = first if i == 0 else wait_group(i, after)
        return {big[t]: a for t, a in got.items()}

    small_shapes = [(w[n].shape[:-1] + (w[n].shape[-1] * N_CHIPS,)) if n in SMALL_SHARDED else w[n].shape
                    for n in SMALL] + [(1,)]
    n_small = sum(int(np.prod(s)) for s in small_shapes)
    pack_rows = -(-n_small // _PACK_QUANTUM) * _PACK_QUANTUM // LANES
    nlayers = [w[n].shape[0] for n in big] + [2]
    halves = [n // 2 for n in nlayers]
    quarters = [tuple(w[n].shape[1:]) for n in big] + [(pack_rows // 2 // N_CHIPS, LANES)]
    wire = [BF16] * nbig + [F32]
    land_now = [lax.empty((halves[t], N_SLOTS) + quarters[t], wire[t]) for t in range(nbig + 1)]
    gparts = [[None] * n for n in nlayers]
    started = []

    def start_scatter(tag, ts, ls, arrays):
        meta = [(axes2[t], l // halves[t], l % halves[t], quarters[t][axes2[t]]) for t, l in zip(ts, ls)]
        send, recv, thru, new_lands, token = _scatter_start(tag, arrays, [land_now[t] for t in ts], meta)
        for t, ln in zip(ts, new_lands):
            land_now[t] = ln
        started.append((send, recv, thru, ts, meta, ls))
        return token

    def on_layer_grads(g, gb):
        ts = [big.index(n) for n in gb]
        return start_scatter(g, ts, [g // 2 if big[t].startswith('ffn') else g // 4 for t in ts],
                             [gb[big[t]] for t in ts])

    loss, dx, gsmall = _local_step(x.reshape(L, D), tgt.reshape(L, D), wl, layer_weights, on_layer_grads)
    gpack = _pack([gsmall[n] for n in SMALL] + [loss.reshape(1)])
    start_scatter(2 * depth, [nbig, nbig], [0, 1], [gpack[:pack_rows // 2], gpack[pack_rows // 2:]])
    landed, sent = _scatter_wait([s[:5] for s in started], land_now)
    for (t, l), g in zip([(t, l) for s in started for t, l in zip(s[3], s[5])], sent):
        gparts[t][l] = g
    small_sum = _sum_and_share(landed[nbig], gparts[nbig], 0, chip1, "sum_share_small")
    quarter_rows = small_sum.shape[0] * small_sum.shape[1]
    placed = _place_quarter(small_sum.reshape(1, quarter_rows, LANES), 0, 0, chip1, F32)
    flying_small, send, recv, token = _gather_start("small", [placed], [0])
    gshard = {n: _sum_and_share(landed[t], gparts[t], axes2[t], chip1, "sum_share_" + n, token)
              for t, n in enumerate(big)}
    small_all = _gather_wait("small", flying_small, send, recv, [0], gshard[big[-1]])[0]
    gpack = small_all.reshape(N_CHIPS, 2, quarter_rows // 2, LANES).transpose(1, 0, 2, 3).reshape(pack_rows, LANES)
    gs = dict(zip(SMALL + ['loss'], _unpack(gpack, small_shapes)))
    loss = gs.pop('loss').reshape(())
    for n in SMALL_SHARDED:
        width = w[n].shape[-1]
        gs[n] = lax.dynamic_slice_in_dim(gs[n], chip * width, width, axis=gs[n].ndim - 1)

    grads, delta, new_m, new_v = {}, {}, {}, {}
    for n in big:
        shp = w[n].shape
        flat = lambda a: a.reshape(shp[0] * shp[1], shp[2])
        g = gshard[n]
        grads[n] = g
        d_, m_, v_ = _adamw(flat(w[n]), flat(g), flat(m[n]), flat(v[n]), "adamw_" + n)
        delta[n], new_m[n], new_v[n] = d_.reshape(shp), m_.reshape(shp), v_.reshape(shp)
    sparse = [n for n in SMALL if w[n].ndim == 4 and w[n].shape[-1] < LANES // 2]
    for names, by_layer in ((sparse, True), ([n for n in SMALL if n not in sparse], False)):
        as2d = lambda a: a.reshape(1, -1) if a.ndim == 1 else a
        res = _adamw_many([(as2d(w[n]), as2d(gs[n]), as2d(m[n]), as2d(v[n])) for n in names],
                          "adamw_small_by_layer" if by_layer else "adamw_small", by_layer)
        for n, (d_, m_, v_) in zip(names, res):
            shp = w[n].shape
            grads[n], delta[n], new_m[n], new_v[n] = gs[n], d_.reshape(shp), m_.reshape(shp), v_.reshape(shp)

    return (loss, dx.reshape(1, L, D), *[grads[n] for n in WEIGHTS], *[delta[n] for n in WEIGHTS],
            *[new_m[n] for n in WEIGHTS], *[new_v[n] for n in WEIGHTS])
```

```python
import math

import numpy as np
import jax
import jax.numpy as jnp
from jax import lax
from jax.experimental import pallas as pl
from jax.experimental.pallas import tpu as pltpu

F32 = jnp.float32
BF16 = jnp.bfloat16
MESH = pl.DeviceIdType.MESH

EPS = 1e-6
CHUNK = 128
POOL_WINDOWS = (2, 4, 8, 16)
LANES = 128
SUBLANES = 8
SCAN_CHUNKS = SUBLANES
S5_GROUPS_PER_STEP = 4
MM_TM_CAP, MM_TN_CAP, MM_TK_CAP = 1408, 2816, 2048
MM_TK_WHOLE = 2816
VMEM_LIMIT = 48 * 1024 * 1024
VMEM_LIMIT_S5 = 56 * 1024 * 1024

ADAM_LR, ADAM_B1, ADAM_B2, ADAM_EPS, ADAM_WD, ADAM_STEP = 0.001, 0.9, 0.999, 1e-08, 0.01, 10

WEIGHTS = ['norm_mix_g', 'even_w_in', 'even_conv_w', 'ssm_log_step', 'ssm_a_re', 'ssm_a_im', 'ssm_b_re',
           'ssm_b_im', 'ssm_c_re', 'ssm_c_im', 'ssm_d', 'ssm_glu_w', 'ssm_glu_b', 'even_w_out', 'odd_w_in',
           'pool_w', 'pool_scale', 'sgu_norm_g', 'sgu_w', 'sgu_b', 'odd_w_out', 'norm_ffn_g', 'ffn_w_up',
           'ffn_conv_w', 'ffn_conv_b', 'ffn_w_down', 'norm_final_g']
BIG = {'even_w_in': 2, 'ssm_glu_w': 1, 'even_w_out': 1, 'odd_w_in': 2, 'odd_w_out': 1, 'ffn_w_up': 2,
       'ffn_w_down': 1}
SMALL_SHARDED = ('even_conv_w', 'pool_scale', 'sgu_norm_g', 'ffn_conv_w')
SMALL = [n for n in WEIGHTS if n not in BIG]
N_CHIPS = 4
N_DEV = 8


def _cparams(sem=None, vmem=VMEM_LIMIT):
    kw = dict(vmem_limit_bytes=vmem)
    if sem is not None:
        kw['dimension_semantics'] = sem
    return pltpu.CompilerParams(**kw)


def _pick(n, segs=(), prefs=(1024, 512, 256, 128)):
    for t in prefs:
        if n % t == 0 and all(s % t == 0 for s in segs if s):
            return t
    return n


def _largest_tile(n, segs, cap):
    best = None
    for t in range(LANES, min(n, cap) + 1, LANES):
        if n % t == 0 and all(s % t == 0 for s in segs if s):
            best = t
    return best if best is not None else n


def _ldims(arr, kind):
    if kind is None:
        return arr.shape
    if kind[0] == 'lead':
        return arr.shape[1:]
    return (arr.shape[1], arr.shape[0] * arr.shape[2])


def _segw(arr, kind):
    return arr.shape[2] if (kind is not None and kind[0] == 'seg') else None


def _opspec(arr, kind, br, bc, rfn, cfn):
    if kind is None:
        return pl.BlockSpec((br, bc), lambda i, j, k: (rfn(i, j, k), cfn(i, j, k)))
    if kind[0] == 'lead':
        lead = kind[1]
        return pl.BlockSpec((None, br, bc), lambda i, j, k: (lead, rfn(i, j, k), cfn(i, j, k)))
    per = arr.shape[2] // bc
    return pl.BlockSpec((None, br, bc), lambda i, j, k: (cfn(i, j, k) // per, rfn(i, j, k), cfn(i, j, k) % per))


def _mm(a, b, mode, out_dtype, name, ak=None, bk=None, ok=None, res=None, dep=None):
    ar, ac = _ldims(a, ak)
    br_, bc_ = _ldims(b, bk)
    if mode == 'nn':
        M, K, N = ar, ac, bc_
        assert br_ == K
    else:
        M, K, N = ar, ac, br_
        assert bc_ == K
    sa, sb = _segw(a, ak), _segw(b, bk)
    so = (N // ok[1]) if ok is not None else None
    tm = _largest_tile(M, [], MM_TM_CAP)
    tn = _largest_tile(N, [sb if mode == 'nn' else None, so], MM_TN_CAP)
    ksegs = [sa, sb if mode == 'nt' else None]
    tk = K if (K <= MM_TK_WHOLE and not any(ksegs)) else _largest_tile(K, ksegs, MM_TK_CAP)
    nk = K // tk
    I = lambda i, j, k: i
    J = lambda i, j, k: j
    Kk = lambda i, j, k: k
    a_spec = _opspec(a, ak, tm, tk, I, Kk)
    if mode == 'nn':
        b_spec = _opspec(b, bk, tk, tn, Kk, J)
        dims = (((1,), (0,)), ((), ()))
    else:
        b_spec = _opspec(b, bk, tn, tk, J, Kk)
        dims = (((1,), (1,)), ((), ()))
    if ok is None:
        out_shape = jax.ShapeDtypeStruct((M, N), out_dtype)
        o_spec = pl.BlockSpec((tm, tn), lambda i, j, k: (i, j))
    else:
        out_shape = jax.ShapeDtypeStruct((ok[1], M, N // ok[1]), out_dtype)
        per = (N // ok[1]) // tn
        o_spec = pl.BlockSpec((None, tm, tn), lambda i, j, k: (j // per, i, j % per))
    has_res = res is not None

    def body(*refs):
        a_ref, b_ref = refs[0], refs[1]
        r_ref = refs[2] if has_res else None
        o_ref = refs[n_in]
        prod = lax.dot_general(a_ref[...].astype(BF16), b_ref[...].astype(BF16), dims, preferred_element_type=F32)
        if nk == 1:
            o_ref[...] = (prod + r_ref[...] if has_res else prod).astype(out_dtype)
            return
        acc = refs[-1]
        k = pl.program_id(2)

        @pl.when(k == 0)
        def _():
            acc[...] = prod

        @pl.when(k > 0)
        def _():
            acc[...] += prod

        @pl.when(k == nk - 1)
        def _():
            o = acc[...]
            if has_res:
                o = o + r_ref[...]
            o_ref[...] = o.astype(out_dtype)

    in_specs = [a_spec, b_spec]
    args = [a, b]
    if has_res:
        in_specs.append(pl.BlockSpec((tm, tn), lambda i, j, k: (i, j)))
        args.append(res)
    if dep is not None:
        in_specs.append(pl.BlockSpec(memory_space=pl.ANY))
        args.append(dep)
    n_in = len(args)
    return pl.pallas_call(
        body, name=name, out_shape=out_shape, grid=(M // tm, N // tn, nk), in_specs=in_specs, out_specs=o_spec,
        scratch_shapes=[pltpu.VMEM((tm, tn), F32)] if nk > 1 else [],
        compiler_params=_cparams(("parallel", "parallel", "arbitrary")),
    )(*args)


_G0 = math.sqrt(2.0 / math.pi)
_G1 = 0.044715


def _gelu(x):
    return 0.5 * x * (1.0 + jnp.tanh(_G0 * (x + _G1 * x * x * x)))


def _gelu_grad(x):
    x2 = x * x
    t = jnp.tanh(_G0 * (x + _G1 * x * x2))
    return 0.5 * (1.0 + t) + 0.5 * x * (1.0 - t * t) * (_G0 * (1.0 + 3.0 * _G1 * x2))


def _sigmoid(x):
    return 1.0 / (1.0 + jnp.exp(-x))


def _down(v, k):
    r = pltpu.roll(v, k, axis=0)
    row = lax.broadcasted_iota(jnp.int32, (SUBLANES, v.shape[1]), 0)
    return jnp.concatenate([jnp.where(row >= k, r[:SUBLANES], 0.0), r[SUBLANES:]], axis=0)


def _up(v, k):
    n = v.shape[0]
    r = pltpu.roll(v, n - k, axis=0)
    row = lax.broadcasted_iota(jnp.int32, (SUBLANES, v.shape[1]), 0)
    return jnp.concatenate([r[:n - SUBLANES], jnp.where(row < SUBLANES - k, r[n - SUBLANES:], 0.0)], axis=0)


def _taps(v):
    return _down(v, 2), _down(v, 1), v


def _conv3(taps, w):
    return w[0:1, :] * taps[0] + w[1:2, :] * taps[1] + w[2:3, :] * taps[2]


def _conv3_t(dv, w):
    return w[2:3, :] * dv + w[1:2, :] * _up(dv, 1) + w[0:1, :] * _up(dv, 2)


def _conv3_dw(dv, taps):
    return tuple(jnp.sum(dv * tp, axis=0, keepdims=True) for tp in taps)


def _cmul(ar, ai, br, bi):
    return ar * br - ai * bi, ar * bi + ai * br


def _cpow(lr, li, n):
    rr = ri = None
    br, bi = lr, li
    while n:
        if n & 1:
            rr, ri = (br, bi) if rr is None else _cmul(rr, ri, br, bi)
        n >>= 1
        if n:
            br, bi = _cmul(br, bi, br, bi)
    return rr, ri


NORM_ROWS = 256


def _norm_mm(x, g, b, out_dtype, name, ok=None):
    M, D = x.shape
    N = b.shape[1]
    so = (N // ok[1]) if ok is not None else None
    tm = _largest_tile(M, [], 1024)
    tn = _largest_tile(N, [so], MM_TN_CAP)
    if ok is None:
        out_shape = jax.ShapeDtypeStruct((M, N), out_dtype)
        o_spec = pl.BlockSpec((tm, tn), lambda i, j: (i, j))
    else:
        out_shape = jax.ShapeDtypeStruct((ok[1], M, N // ok[1]), out_dtype)
        per = (N // ok[1]) // tn
        o_spec = pl.BlockSpec((None, tm, tn), lambda i, j: (j // per, i, j % per))

    def body(x_ref, g_ref, b_ref, o_ref, ht_ref, h_scr):
        @pl.when(pl.program_id(1) == 0)
        def _():
            for c in range(tm // NORM_ROWS):
                rows = pl.ds(c * NORM_ROWS, NORM_ROWS)
                xv = x_ref[rows, :]
                h = xv * lax.rsqrt(jnp.mean(xv * xv, axis=-1, keepdims=True) + EPS) * g_ref[...]
                h_scr[rows, :] = h.astype(BF16)
                ht_ref[:, rows] = h.T.astype(BF16)

        o_ref[...] = jnp.dot(h_scr[...], b_ref[...], preferred_element_type=F32).astype(out_dtype)

    return pl.pallas_call(
        body, name=name, out_shape=(out_shape, jax.ShapeDtypeStruct((D, M), BF16)), grid=(M // tm, N // tn),
        in_specs=[pl.BlockSpec((tm, D), lambda i, j: (i, 0)), pl.BlockSpec((1, D), lambda i, j: (0, 0)),
                  pl.BlockSpec((D, tn), lambda i, j: (0, j))],
        out_specs=(o_spec, pl.BlockSpec((D, tm), lambda i, j: (0, i))),
        scratch_shapes=[pltpu.VMEM((tm, D), BF16)], compiler_params=_cparams(("parallel", "arbitrary")),
    )(x, g.reshape(1, D), b)


def _mm_norm_bwd(a, b, x, g, dres, name, ak=None, dep=None):
    M, K = _ldims(a, ak)
    D = b.shape[0]
    assert b.shape[1] == K and x.shape == (M, D)
    sa = _segw(a, ak)
    whole_segs = bool(sa) and K <= MM_TK_WHOLE
    tk = K if (K <= MM_TK_WHOLE) else _largest_tile(K, [sa], MM_TK_WHOLE)
    tm = _largest_tile(M, [], 1024 if (tk == K or tk <= MM_TK_CAP) else 512)
    ni, nk = M // tm, K // tk
    if whole_segs:
        a_spec = pl.BlockSpec((a.shape[0], tm, sa), lambda i, k: (0, i, 0))
    else:
        a3 = _opspec(a, ak, tm, tk, lambda i, j, k: i, lambda i, j, k: k)
        a_spec = pl.BlockSpec(a3.block_shape, lambda i, k: a3.index_map(i, 0, k))
    n_in = 5 + (dep is not None)

    def body(*refs):
        a_ref, b_ref, x_ref, g_ref, r_ref = refs[:5]
        dx_ref, dxb_ref, dg_ref, acc, accg = refs[n_in:]
        i, k = pl.program_id(0), pl.program_id(1)
        av = jnp.concatenate([a_ref[s] for s in range(a.shape[0])], axis=1) if whole_segs else a_ref[...]
        prod = lax.dot_general(av.astype(BF16), b_ref[...], (((1,), (1,)), ((), ())), preferred_element_type=F32)

        @pl.when(k == 0)
        def _():
            acc[...] = prod

        @pl.when(k > 0)
        def _():
            acc[...] += prod

        @pl.when((i == 0) & (k == 0))
        def _():
            accg[...] = jnp.zeros_like(accg)

        @pl.when(k == nk - 1)
        def _():
            for c in range(tm // NORM_ROWS):
                rows = pl.ds(c * NORM_ROWS, NORM_ROWS)
                xv = x_ref[rows, :]
                r = lax.rsqrt(jnp.mean(xv * xv, axis=-1, keepdims=True) + EPS)
                xh = xv * r
                dhv = acc[rows, :]
                accg[...] += jnp.sum((dhv * xh).reshape(NORM_ROWS // SUBLANES, SUBLANES, D), axis=0)
                dxh = dhv * g_ref[...]
                dxv = r_ref[rows, :] + r * (dxh - xh * jnp.mean(dxh * xh, axis=-1, keepdims=True))
                dx_ref[rows, :] = dxv
                dxb_ref[rows, :] = dxv.astype(BF16)

        @pl.when((i == ni - 1) & (k == nk - 1))
        def _():
            dg_ref[...] = jnp.sum(accg[...], axis=0, keepdims=True)

    row = pl.BlockSpec((tm, D), lambda i, k: (i, 0))
    vec = pl.BlockSpec((1, D), lambda i, k: (0, 0))
    in_specs = [a_spec, pl.BlockSpec((D, tk), lambda i, k: (0, k)), row, vec, row]
    args = [a, b, x, g.reshape(1, D), dres]
    if dep is not None:
        in_specs.append(pl.BlockSpec(memory_space=pl.ANY))
        args.append(dep)
    return pl.pallas_call(
        body, name=name,
        out_shape=(jax.ShapeDtypeStruct((M, D), F32), jax.ShapeDtypeStruct((M, D), BF16),
                   jax.ShapeDtypeStruct((1, D), F32)),
        grid=(ni, nk), in_specs=in_specs, out_specs=(row, row, vec),
        scratch_shapes=[pltpu.VMEM((tm, D), F32), pltpu.VMEM((SUBLANES, D), F32)],
        compiler_params=_cparams(("arbitrary", "arbitrary"), VMEM_LIMIT_S5),
    )(*args)


def _loss_head(x, g, tgt):
    L, D = x.shape
    tr = _pick(L, prefs=(512, 256, 128))
    nsteps = L // tr

    def body(x_ref, g_ref, t_ref, loss_ref, dx_ref, dxb_ref, dg_ref, acc_g, acc_l):
        i = pl.program_id(0)

        @pl.when(i == 0)
        def _():
            acc_g[...] = jnp.zeros_like(acc_g)
            acc_l[...] = jnp.zeros_like(acc_l)

        xv = x_ref[...]
        gv = g_ref[...]
        r = lax.rsqrt(jnp.mean(xv * xv, axis=-1, keepdims=True) + EPS)
        xh = xv * r
        e = xh * gv - t_ref[...]
        acc_l[...] += jnp.sum((e * e).reshape(tr // SUBLANES, SUBLANES, D), axis=0)
        dy = e * (1.0 / D)
        acc_g[...] += jnp.sum((dy * xh).reshape(tr // SUBLANES, SUBLANES, D), axis=0)
        dxh = dy * gv
        dxv = r * (dxh - xh * jnp.mean(dxh * xh, axis=-1, keepdims=True))
        dx_ref[...] = dxv
        dxb_ref[...] = dxv.astype(BF16)

        @pl.when(i == nsteps - 1)
        def _():
            dg_ref[...] = jnp.sum(acc_g[...], axis=0, keepdims=True)
            tot = jnp.sum(jnp.sum(acc_l[...], axis=0, keepdims=True), axis=1, keepdims=True) * (0.5 / D)
            loss_ref[...] = jnp.broadcast_to(tot, (SUBLANES, LANES))

    row = pl.BlockSpec((tr, D), lambda i: (i, 0))
    vec = pl.BlockSpec((1, D), lambda i: (0, 0))
    return pl.pallas_call(
        body, name="loss_head",
        out_shape=(jax.ShapeDtypeStruct((SUBLANES, LANES), F32), jax.ShapeDtypeStruct((L, D), F32),
                   jax.ShapeDtypeStruct((L, D), BF16), jax.ShapeDtypeStruct((1, D), F32)),
        grid=(nsteps,), in_specs=[row, vec, row],
        out_specs=(pl.BlockSpec((SUBLANES, LANES), lambda i: (0, 0)), row, row, vec),
        scratch_shapes=[pltpu.VMEM((SUBLANES, D), F32), pltpu.VMEM((SUBLANES, D), F32)],
        compiler_params=_cparams(("arbitrary",)),
    )(x, g.reshape(1, D), tgt)


def _sconv_fwd(proj4, conv_w, name):
    _, L, C = proj4.shape
    cb = LANES

    def body(p_ref, w_ref, o_ref):
        xa, ba, ca = p_ref[0].astype(F32), p_ref[1].astype(F32), p_ref[2].astype(F32)
        o_ref[...] = (ba * _conv3(_taps(ca * xa), w_ref[...])).astype(BF16)

    return pl.pallas_call(
        body, name=name, out_shape=jax.ShapeDtypeStruct((L, 2 * C), BF16), grid=(C // cb,),
        in_specs=[pl.BlockSpec((3, L, cb), lambda j: (0, 0, j)), pl.BlockSpec((3, cb), lambda j: (0, j))],
        out_specs=pl.BlockSpec((L, cb), lambda j: (0, j)), compiler_params=_cparams(("parallel",)),
    )(proj4, conv_w)


def _sconv_bwd(proj4, dmix, conv_w, name):
    _, L, C = proj4.shape
    cb = LANES

    def body(p_ref, d_ref, w_ref, o_ref, dw_ref):
        xa, ba, ca = p_ref[0].astype(F32), p_ref[1].astype(F32), p_ref[2].astype(F32)
        w = w_ref[...]
        dya = d_ref[...]
        tq = _taps(ca * xa)
        cq = _conv3(tq, w)
        dcq = dya * ba
        dq = _conv3_t(dcq, w)
        for tap, dwt in enumerate(_conv3_dw(dcq, tq)):
            dw_ref[tap:tap + 1, :] = dwt
        o_ref[0] = (dq * ca).astype(BF16)
        o_ref[1] = (dya * cq).astype(BF16)
        o_ref[2] = (dq * xa).astype(BF16)

    return pl.pallas_call(
        body, name=name,
        out_shape=(jax.ShapeDtypeStruct((4, L, C), BF16), jax.ShapeDtypeStruct((3, C), F32)), grid=(C // cb,),
        in_specs=[pl.BlockSpec((3, L, cb), lambda j: (0, 0, j)), pl.BlockSpec((L, cb), lambda j: (0, j)),
                  pl.BlockSpec((3, cb), lambda j: (0, j))],
        out_specs=(pl.BlockSpec((3, L, cb), lambda j: (0, 0, j)), pl.BlockSpec((3, cb), lambda j: (0, j))),
        compiler_params=_cparams(("parallel",)),
    )(proj4, dmix, conv_w)


def _s5_prep(log_step, a_re, a_im, b_re, b_im, c_re, c_im):
    G, P = a_re.shape
    H = b_re.shape[-1]
    gs = S5_GROUPS_PER_STEP
    ns = G // gs
    gu = LANES // H
    lam = lax.complex(a_re, a_im)
    step = jnp.exp(log_step)[:, None]
    lam_bar = jnp.exp(lam * step)
    b_bar = ((lam_bar - 1.0) / lam)[..., None] * lax.complex(b_re, b_im)
    lr = jnp.real(lam_bar).reshape(ns, 1, gs * P)
    li = jnp.imag(lam_bar).reshape(ns, 1, gs * P)
    k = np.arange(ns)[:, None, None]
    oh = jnp.asarray((np.arange(gu)[None, :, None] == gs * (k % (gu // gs)) + np.arange(gs)[None, None, :]),
                     F32)
    bre = jnp.einsum('kgl,klph->kghlp', oh, jnp.real(b_bar).reshape(ns, gs, P, H)).reshape(ns, gu * H, gs * P)
    bim = jnp.einsum('kgl,klph->kghlp', oh, jnp.imag(b_bar).reshape(ns, gs, P, H)).reshape(ns, gu * H, gs * P)
    cre = jnp.einsum('kgl,klhp->klpgh', oh, c_re.reshape(ns, gs, H, P)).reshape(ns, gs * P, gu * H)
    cim = jnp.einsum('kgl,klhp->klpgh', oh, c_im.reshape(ns, gs, H, P)).reshape(ns, gs * P, gu * H)
    return lr, li, jnp.concatenate([bre, bim], axis=2), jnp.concatenate([cre, -cim], axis=1)


def _carry_tile(fr, fi, pr, pi, reverse):
    row = lax.broadcasted_iota(jnp.int32, fr.shape, 0)
    cr = jnp.zeros_like(fr)
    ci = jnp.zeros_like(fi)
    sr = jnp.zeros_like(fr[0:1])
    si = jnp.zeros_like(sr)
    order = range(SCAN_CHUNKS - 1, 0, -1) if reverse else range(0, SCAN_CHUNKS - 1)
    for c in order:
        fcr = jnp.sum(jnp.where(row == c, fr, 0.0), axis=0, keepdims=True)
        fci = jnp.sum(jnp.where(row == c, fi, 0.0), axis=0, keepdims=True)
        mr, mi = _cmul(pr, pi, sr, si)
        sr, si = mr + fcr, mi + fci
        nxt = c - 1 if reverse else c + 1
        cr = jnp.where(row == nxt, sr, cr)
        ci = jnp.where(row == nxt, si, ci)
    return cr, ci


def _scan_order_into(dst_ref, src_ref, T):
    for c in range(SCAN_CHUNKS):
        dst_ref[pl.ds(c, T, stride=SCAN_CHUNKS), :] = src_ref[pl.ds(c * T, T), :].astype(F32)


def _s5_fwd(proj4, lr, li, bmat, cmat, d, name):
    _, L, Du = proj4.shape
    ns, _, W2 = bmat.shape
    W = W2 // 2
    T = L // SCAN_CHUNKS
    rb = _pick(L, prefs=(512, 256, 128))
    per = (ns * LANES) // Du

    def body(ut_ref, lr_ref, li_ref, b_ref, c_ref, d_ref, y_ref, sr_ref, si_ref, u_ref):
        k = pl.program_id(0)
        _scan_order_into(u_ref, ut_ref, T)
        for r in range(L // rb):
            rows = pl.ds(r * rb, rb)
            bu = jnp.dot(u_ref[rows, :].astype(BF16), b_ref[...], preferred_element_type=F32)
            sr_ref[rows, :] = bu[:, :W]
            si_ref[rows, :] = bu[:, W:]
        lam_r = jnp.broadcast_to(lr_ref[...], (SUBLANES, W))
        lam_i = jnp.broadcast_to(li_ref[...], (SUBLANES, W))

        def local(t, carry):
            sr, si = carry
            rows = pl.ds(pl.multiple_of(t * SUBLANES, SUBLANES), SUBLANES)
            mr, mi = _cmul(lam_r, lam_i, sr, si)
            sr = mr + sr_ref[rows, :]
            si = mi + si_ref[rows, :]
            sr_ref[rows, :] = sr
            si_ref[rows, :] = si
            return sr, si

        z = jnp.zeros((SUBLANES, W), F32)
        fr, fi = lax.fori_loop(0, T, local, (z, z))
        pr, pi = _cpow(lam_r, lam_i, T)
        cr, ci = _carry_tile(fr, fi, pr[0:1], pi[0:1], reverse=False)

        def fix(t, carry):
            wr, wi = carry
            rows = pl.ds(pl.multiple_of(t * SUBLANES, SUBLANES), SUBLANES)
            ar, ai = _cmul(wr, wi, cr, ci)
            sr_ref[rows, :] += ar
            si_ref[rows, :] += ai
            return _cmul(wr, wi, lam_r, lam_i)

        lax.fori_loop(0, T, fix, (lam_r, lam_i))
        first = (k % per) == 0
        for r in range(L // rb):
            rows = pl.ds(r * rb, rb)
            s = jnp.concatenate([sr_ref[rows, :], si_ref[rows, :]], axis=1).astype(BF16)
            y = jnp.dot(s, c_ref[...], preferred_element_type=F32)

            @pl.when(first)
            def _():
                y_ref[rows, :] = y + d_ref[...] * u_ref[rows, :]

            @pl.when(jnp.logical_not(first))
            def _():
                y_ref[rows, :] += y

    ublk = pl.BlockSpec((L, LANES), lambda k: (0, k // per))
    sblk = pl.BlockSpec((L, W), lambda k: (0, k))
    lam = pl.BlockSpec((None, 1, W), lambda k: (k, 0, 0))
    return pl.pallas_call(
        body, name=name,
        out_shape=(jax.ShapeDtypeStruct((L, Du), F32), jax.ShapeDtypeStruct((L, ns * W), F32),
                   jax.ShapeDtypeStruct((L, ns * W), F32)),
        grid=(ns,),
        in_specs=[pl.BlockSpec((None, L, LANES), lambda k: (3, 0, k // per)), lam, lam,
                  pl.BlockSpec((None, LANES, 2 * W), lambda k: (k, 0, 0)),
                  pl.BlockSpec((None, 2 * W, LANES), lambda k: (k, 0, 0)),
                  pl.BlockSpec((1, LANES), lambda k: (0, k // per))],
        out_specs=(ublk, sblk, sblk), scratch_shapes=[pltpu.VMEM((L, LANES), F32)],
        compiler_params=_cparams(("arbitrary",), VMEM_LIMIT_S5),
    )(proj4, lr, li, bmat.astype(BF16), cmat.astype(BF16), d.reshape(1, Du))


def _s5_bwd(dy, proj4, dproj, s_re, s_im, lr, li, bmat, cmat, d, name):
    _, L, Du = proj4.shape
    ns, _, W2 = bmat.shape
    W = W2 // 2
    T = L // SCAN_CHUNKS
    rb = _pick(L, prefs=(512, 256, 128))
    per = (ns * LANES) // Du
    NT = (((1,), (1,)), ((), ()))
    TN = (((0,), (0,)), ((), ()))

    def body(dy_ref, ut_ref, dp_in, sr_ref, si_ref, lr_ref, li_ref, b_ref, c_ref, d_ref,
             dut_ref, db_ref, dc_ref, dl_ref, dd_ref, gr_ref, gi_ref, u_ref, du_ref):
        k = pl.program_id(0)
        _scan_order_into(u_ref, ut_ref, T)
        for r in range(L // rb):
            rows = pl.ds(r * rb, rb)
            g = lax.dot_general(dy_ref[rows, :].astype(BF16), c_ref[...], NT, preferred_element_type=F32)
            gr_ref[rows, :] = g[:, :W]
            gi_ref[rows, :] = g[:, W:]
        lam_r = jnp.broadcast_to(lr_ref[...], (SUBLANES, W))
        lam_i = -jnp.broadcast_to(li_ref[...], (SUBLANES, W))

        def local(i, carry):
            gr, gi = carry
            rows = pl.ds(pl.multiple_of((T - 1 - i) * SUBLANES, SUBLANES), SUBLANES)
            mr, mi = _cmul(lam_r, lam_i, gr, gi)
            gr = mr + gr_ref[rows, :]
            gi = mi + gi_ref[rows, :]
            gr_ref[rows, :] = gr
            gi_ref[rows, :] = gi
            return gr, gi

        z = jnp.zeros((SUBLANES, W), F32)
        fr, fi = lax.fori_loop(0, T, local, (z, z))
        pr, pi = _cpow(lam_r, lam_i, T)
        cr, ci = _carry_tile(fr, fi, pr[0:1], pi[0:1], reverse=True)

        def true_g(rows, wr, wi):
            ar, ai = _cmul(wr, wi, cr, ci)
            gr = gr_ref[rows, :] + ar
            gi = gi_ref[rows, :] + ai
            gr_ref[rows, :] = gr
            gi_ref[rows, :] = gi
            return gr, gi

        def fix(i, carry):
            wr, wi, ar_, ai_ = carry
            t = T - 1 - i
            rows = pl.ds(pl.multiple_of(t * SUBLANES, SUBLANES), SUBLANES)
            prev = pl.ds(pl.multiple_of((t - 1) * SUBLANES, SUBLANES), SUBLANES)
            gr, gi = true_g(rows, wr, wi)
            qr, qi = sr_ref[prev, :], si_ref[prev, :]
            ar_ = ar_ + gr * qr + gi * qi
            ai_ = ai_ + gi * qr - gr * qi
            wr, wi = _cmul(wr, wi, lam_r, lam_i)
            return wr, wi, ar_, ai_

        wr, wi, acc_r, acc_i = lax.fori_loop(0, T - 1, fix, (lam_r, lam_i, z, z))
        gr, gi = true_g(pl.ds(0, SUBLANES), wr, wi)
        last = pl.ds((T - 1) * SUBLANES, SUBLANES)
        row = lax.broadcasted_iota(jnp.int32, (SUBLANES, W), 0)
        qr = jnp.where(row >= 1, pltpu.roll(sr_ref[last, :], 1, axis=0), 0.0)
        qi = jnp.where(row >= 1, pltpu.roll(si_ref[last, :], 1, axis=0), 0.0)
        acc_r = acc_r + gr * qr + gi * qi
        acc_i = acc_i + gi * qr - gr * qi
        dl_ref[0:1, :] = jnp.sum(acc_r, axis=0, keepdims=True)
        dl_ref[1:2, :] = jnp.sum(acc_i, axis=0, keepdims=True)

        first = (k % per) == 0
        db = jnp.zeros((LANES, 2 * W), F32)
        dc = jnp.zeros((LANES, 2 * W), F32)
        dd = jnp.zeros((1, LANES), F32)
        for r in range(L // rb):
            rows = pl.ds(r * rb, rb)
            gb = jnp.concatenate([gr_ref[rows, :], gi_ref[rows, :]], axis=1).astype(BF16)
            sb = jnp.concatenate([sr_ref[rows, :], si_ref[rows, :]], axis=1).astype(BF16)
            dyv = dy_ref[rows, :]
            uv = u_ref[rows, :]
            du = lax.dot_general(gb, b_ref[...], NT, preferred_element_type=F32)
            db = db + lax.dot_general(uv.astype(BF16), gb, TN, preferred_element_type=F32)
            dc = dc + lax.dot_general(dyv.astype(BF16), sb, TN, preferred_element_type=F32)
            dd = dd + jnp.sum(dyv * uv, axis=0, keepdims=True)

            @pl.when(first)
            def _():
                du_ref[rows, :] = du + d_ref[...] * dyv

            @pl.when(jnp.logical_not(first))
            def _():
                du_ref[rows, :] += du

        db_ref[...] = db
        dc_ref[...] = dc

        @pl.when(first)
        def _():
            dd_ref[...] = dd

        @pl.when((k % per) == per - 1)
        def _():
            for c in range(SCAN_CHUNKS):
                dut_ref[pl.ds(c * T, T), :] = du_ref[pl.ds(c, T, stride=SCAN_CHUNKS), :].astype(BF16)

    ublk = pl.BlockSpec((L, LANES), lambda k: (0, k // per))
    uslab = pl.BlockSpec((None, L, LANES), lambda k: (3, 0, k // per))
    sblk = pl.BlockSpec((L, W), lambda k: (0, k))
    lam = pl.BlockSpec((None, 1, W), lambda k: (k, 0, 0))
    vec = pl.BlockSpec((1, LANES), lambda k: (0, k // per))
    mat = pl.BlockSpec((None, LANES, 2 * W), lambda k: (k, 0, 0))
    return pl.pallas_call(
        body, name=name,
        out_shape=(jax.ShapeDtypeStruct(dproj.shape, dproj.dtype), jax.ShapeDtypeStruct((ns, LANES, 2 * W), F32),
                   jax.ShapeDtypeStruct((ns, LANES, 2 * W), F32), jax.ShapeDtypeStruct((ns, 2, W), F32),
                   jax.ShapeDtypeStruct((1, Du), F32)),
        grid=(ns,),
        in_specs=[ublk, uslab, pl.BlockSpec(memory_space=pl.ANY), sblk, sblk, lam, lam, mat,
                  pl.BlockSpec((None, 2 * W, LANES), lambda k: (k, 0, 0)), vec],
        out_specs=(uslab, mat, mat, pl.BlockSpec((None, 2, W), lambda k: (k, 0, 0)), vec),
        scratch_shapes=[pltpu.VMEM((L, W), F32), pltpu.VMEM((L, W), F32), pltpu.VMEM((L, LANES), F32),
                        pltpu.VMEM((L, LANES), F32)],
        input_output_aliases={2: 0}, compiler_params=_cparams(("arbitrary",), VMEM_LIMIT_S5),
    )(dy, proj4, dproj, s_re, s_im, lr, li, bmat.astype(BF16), cmat.astype(BF16), d.reshape(1, Du))


def _glu_fwd(yraw, wmat, bias, mixin, name):
    L, C = yraw.shape
    tr = _pick(L, prefs=(512, 256, 128))
    tb = tr // SCAN_CHUNKS
    nl = C // LANES

    def body(y_ref, w_ref, b_ref, m_in, o_ref, scr):
        yg = _gelu(y_ref[...])
        zz = jnp.dot(yg.astype(BF16), w_ref[...], preferred_element_type=F32) + b_ref[...]
        yb = yg * _sigmoid(zz)
        for k in range(nl):
            scr[k] = yb[:, k * LANES:(k + 1) * LANES]
        for c in range(SCAN_CHUNKS):
            for k in range(nl):
                o_ref[c, :, k * LANES:(k + 1) * LANES] = scr[k, pl.ds(c, tb, stride=SCAN_CHUNKS), :].astype(BF16)

    out = pl.pallas_call(
        body, name=name, out_shape=jax.ShapeDtypeStruct((SCAN_CHUNKS, L // SCAN_CHUNKS, 2 * C), BF16),
        grid=(L // tr,),
        in_specs=[pl.BlockSpec((tr, C), lambda i: (i, 0)), pl.BlockSpec((C, C), lambda i: (0, 0)),
                  pl.BlockSpec((1, C), lambda i: (0, 0)), pl.BlockSpec(memory_space=pl.ANY)],
        out_specs=pl.BlockSpec((SCAN_CHUNKS, tb, C), lambda i: (0, i, 1)),
        scratch_shapes=[pltpu.VMEM((nl, tr, LANES), F32)], input_output_aliases={3: 0},
        compiler_params=_cparams(("parallel",)),
    )(yraw, wmat, bias.reshape(1, C), mixin.reshape(SCAN_CHUNKS, L // SCAN_CHUNKS, 2 * C))
    return out.reshape(L, 2 * C)


def _glu_bwd(yraw, dmix, wmat, bias, name):
    L, C = yraw.shape
    tr = _pick(L, prefs=(512, 256, 128))
    nsteps = L // tr
    tb = tr // SCAN_CHUNKS
    nl = C // LANES

    def body(y_ref, d_ref, w_ref, b_ref, dy_ref, dw_ref, db_ref, acc_b, scr):
        i = pl.program_id(0)

        @pl.when(i == 0)
        def _():
            dw_ref[...] = jnp.zeros_like(dw_ref)
            acc_b[...] = jnp.zeros_like(acc_b)

        for c in range(SCAN_CHUNKS):
            for k in range(nl):
                scr[k, pl.ds(c, tb, stride=SCAN_CHUNKS), :] = d_ref[c, :, k * LANES:(k + 1) * LANES]
        yr = y_ref[...]
        yg = _gelu(yr)
        ygb = yg.astype(BF16)
        sg = _sigmoid(jnp.dot(ygb, w_ref[...], preferred_element_type=F32) + b_ref[...])
        dyb_ = jnp.concatenate([scr[k] for k in range(nl)], axis=1)
        dz = dyb_ * yg * sg * (1.0 - sg)
        dzb = dz.astype(BF16)
        dyg = dyb_ * sg + lax.dot_general(dzb, w_ref[...], (((1,), (1,)), ((), ())), preferred_element_type=F32)
        dw_ref[...] += lax.dot_general(ygb, dzb, (((0,), (0,)), ((), ())), preferred_element_type=F32)
        acc_b[...] += jnp.sum(dz.reshape(tr // SUBLANES, SUBLANES, C), axis=0)
        dy_ref[...] = dyg * _gelu_grad(yr)

        @pl.when(i == nsteps - 1)
        def _():
            db_ref[...] = jnp.sum(acc_b[...], axis=0, keepdims=True)

    row = pl.BlockSpec((tr, C), lambda i: (i, 0))
    return pl.pallas_call(
        body, name=name,
        out_shape=(jax.ShapeDtypeStruct((L, C), F32), jax.ShapeDtypeStruct((C, C), F32),
                   jax.ShapeDtypeStruct((1, C), F32)),
        grid=(nsteps,),
        in_specs=[row, pl.BlockSpec((SCAN_CHUNKS, tb, C), lambda i: (0, i, 1)), pl.BlockSpec((C, C), lambda i: (0, 0)),
                  pl.BlockSpec((1, C), lambda i: (0, 0))],
        out_specs=(row, pl.BlockSpec((C, C), lambda i: (0, 0)), pl.BlockSpec((1, C), lambda i: (0, 0))),
        scratch_shapes=[pltpu.VMEM((SUBLANES, C), F32), pltpu.VMEM((nl, tr, LANES), F32)],
        compiler_params=_cparams(("arbitrary",)),
    )(yraw, dmix.reshape(SCAN_CHUNKS, L // SCAN_CHUNKS, 2 * C), wmat, bias.reshape(1, C))


def _pool_counts(L, g):
    t = lax.broadcasted_iota(jnp.int32, (L, LANES), 0).astype(F32) + 1.0
    w = jnp.where(g == 0, 2.0, jnp.where(g == 1, 4.0, jnp.where(g == 2, 8.0, 16.0)))
    return 1.0 / jnp.minimum(t, w)


def _select_window(g, a2, a4, a8, a16):
    return jnp.where(g == 0, a2, jnp.where(g == 1, a4, jnp.where(g == 2, a8, a16)))


def _pooled(z, g):
    a2 = z + _down(z, 1)
    a4 = a2 + _down(a2, 2)
    a8 = a4 + _down(a4, 4)
    a16 = a8 + _down(a8, 8)
    return _select_window(g, a2, a4, a8, a16) * _pool_counts(z.shape[0], g) - z


def _transpose_on_mxu(yb):
    c = yb.shape[1]
    eye = lax.broadcasted_iota(jnp.int32, (c, c), 0) == lax.broadcasted_iota(jnp.int32, (c, c), 1)
    return lax.dot_general(eye.astype(BF16), yb, (((1,), (1,)), ((), ())), preferred_element_type=F32).astype(BF16)


def _pool_fwd(proj3, pool_w, scale, name):
    _, L, C = proj3.shape
    ng = len(POOL_WINDOWS)
    pg = C // ng
    assert pg == LANES

    def body(z_ref, w_ref, s_ref, o_ref, ot_ref):
        g = pl.program_id(0)
        p = _pooled(z_ref[...].astype(F32), g)
        y = jnp.dot(p.astype(BF16), w_ref[...].astype(BF16), preferred_element_type=F32)
        yb = (y * s_ref[...]).astype(BF16)
        o_ref[...] = yb
        ot_ref[...] = _transpose_on_mxu(yb)

    return pl.pallas_call(
        body, name=name, out_shape=(jax.ShapeDtypeStruct((L, 2 * C), BF16), jax.ShapeDtypeStruct((2 * C, L), BF16)),
        grid=(ng,),
        in_specs=[pl.BlockSpec((None, L, pg), lambda g: (0, 0, g)), pl.BlockSpec((None, pg, pg), lambda g: (g, 0, 0)),
                  pl.BlockSpec((1, pg), lambda g: (0, g))],
        out_specs=(pl.BlockSpec((L, pg), lambda g: (0, g)), pl.BlockSpec((pg, L), lambda g: (g, 0))),
        compiler_params=_cparams(("parallel",)),
    )(proj3, pool_w, scale.reshape(1, C))


def _pool_bwd(proj3, dmix, pool_w, scale, name):
    _, L, C = proj3.shape
    ng = len(POOL_WINDOWS)
    pg = C // ng

    def body(z_ref, d_ref, w_ref, s_ref, dz_ref, dw_ref, ds_ref):
        g = pl.program_id(0)
        p = _pooled(z_ref[...].astype(F32), g)
        pb = p.astype(BF16)
        wb = w_ref[...].astype(BF16)
        pre = jnp.dot(pb, wb, preferred_element_type=F32)
        dyc = d_ref[...]
        ds_ref[...] = jnp.sum(dyc * pre, axis=0, keepdims=True)
        dpre = (dyc * s_ref[...]).astype(BF16)
        dw_ref[...] = lax.dot_general(pb, dpre, (((0,), (0,)), ((), ())), preferred_element_type=F32)
        dp = lax.dot_general(dpre, wb, (((1,), (1,)), ((), ())), preferred_element_type=F32)
        v = dp * _pool_counts(L, g)
        a2 = v + _up(v, 1)
        a4 = a2 + _up(a2, 2)
        a8 = a4 + _up(a4, 4)
        a16 = a8 + _up(a8, 8)
        dz_ref[...] = (_select_window(g, a2, a4, a8, a16) - dp).astype(BF16)

    return pl.pallas_call(
        body, name=name,
        out_shape=(jax.ShapeDtypeStruct((L, C), BF16), jax.ShapeDtypeStruct((ng, pg, pg), F32),
                   jax.ShapeDtypeStruct((1, C), F32)),
        grid=(ng,),
        in_specs=[pl.BlockSpec((None, L, pg), lambda g: (0, 0, g)), pl.BlockSpec((L, pg), lambda g: (0, g)),
                  pl.BlockSpec((None, pg, pg), lambda g: (g, 0, 0)), pl.BlockSpec((1, pg), lambda g: (0, g))],
        out_specs=(pl.BlockSpec((L, pg), lambda g: (0, g)), pl.BlockSpec((None, pg, pg), lambda g: (g, 0, 0)),
                   pl.BlockSpec((1, pg), lambda g: (0, g))),
        compiler_params=_cparams(("parallel",)),
    )(proj3, dmix, pool_w, scale.reshape(1, C))


def _tril_w(w_ref, h):
    r = lax.broadcasted_iota(jnp.int32, (CHUNK, CHUNK), 0)
    c = lax.broadcasted_iota(jnp.int32, (CHUNK, CHUNK), 1)
    return jnp.where(r >= c, w_ref[h], 0.0)


def _sgu_fwd(proj3, norm_g, w, b, mixin, mixin_t, name):
    _, L, C = proj3.shape
    nh = w.shape[0]
    dh = C // nh
    assert dh == LANES and w.shape[1] == CHUNK
    tr = _pick(L, prefs=(512, 256, 128))
    bfull = jnp.broadcast_to(b[:, :, None], (nh, CHUNK, dh))

    def body(su_ref, sv_ref, g_ref, w_ref, b_ref, m_in, mt_in, o_ref, ot_ref):
        sv = _gelu(sv_ref[...].astype(F32))
        r = lax.rsqrt(jnp.mean(sv * sv, axis=-1, keepdims=True) + EPS)
        v = (sv * r * g_ref[...]).astype(BF16)
        for h in range(nh):
            wm = _tril_w(w_ref, h).astype(BF16)
            cols = slice(h * dh, (h + 1) * dh)
            for n in range(tr // CHUNK):
                rows = slice(n * CHUNK, (n + 1) * CHUNK)
                mixed = jnp.dot(wm, v[rows, cols], preferred_element_type=F32) + b_ref[h]
                o_ref[rows, cols] = (_gelu(su_ref[rows, cols].astype(F32)) * mixed).astype(BF16)
        ot_ref[...] = _transpose_on_mxu(o_ref[...])

    full = lambda shp: pl.BlockSpec(shp, lambda i: (0,) * len(shp))
    anywhere = pl.BlockSpec(memory_space=pl.ANY)
    return pl.pallas_call(
        body, name=name, out_shape=(jax.ShapeDtypeStruct(mixin.shape, BF16), jax.ShapeDtypeStruct(mixin_t.shape, BF16)),
        grid=(L // tr,),
        in_specs=[pl.BlockSpec((None, tr, C), lambda i: (1, i, 0)), pl.BlockSpec((None, tr, C), lambda i: (2, i, 0)),
                  full((1, C)), full((nh, CHUNK, CHUNK)), full((nh, CHUNK, dh)), anywhere, anywhere],
        out_specs=(pl.BlockSpec((tr, C), lambda i: (i, 1)), pl.BlockSpec((C, tr), lambda i: (1, i))),
        input_output_aliases={5: 0, 6: 1}, compiler_params=_cparams(("parallel",)),
    )(proj3, proj3, norm_g.reshape(1, C), w, bfull, mixin, mixin_t)


def _sgu_bwd(proj3, dmix, dz, norm_g, w, b, name):
    _, L, C = proj3.shape
    nh = w.shape[0]
    dh = C // nh
    tr = _pick(L, prefs=(512, 256, 128))
    nsteps = L // tr
    bfull = jnp.broadcast_to(b[:, :, None], (nh, CHUNK, dh))

    def body(su_ref, sv_ref, d_ref, dz_ref, g_ref, w_ref, b_ref, o_ref, dw_ref, db_ref, dg_ref, dv_ref, acc_g):
        i = pl.program_id(0)
        o_ref[0] = dz_ref[...]

        @pl.when(i == 0)
        def _():
            dw_ref[...] = jnp.zeros_like(dw_ref)
            db_ref[...] = jnp.zeros_like(db_ref)
            acc_g[...] = jnp.zeros_like(acc_g)

        svp = sv_ref[...].astype(F32)
        sv = _gelu(svp)
        r = lax.rsqrt(jnp.mean(sv * sv, axis=-1, keepdims=True) + EPS)
        vh = sv * r
        gv = g_ref[...]
        v = (vh * gv).astype(BF16)
        tri_r = lax.broadcasted_iota(jnp.int32, (CHUNK, CHUNK), 0)
        tri_c = lax.broadcasted_iota(jnp.int32, (CHUNK, CHUNK), 1)
        for h in range(nh):
            wm = _tril_w(w_ref, h).astype(BF16)
            cols = slice(h * dh, (h + 1) * dh)
            dwh = jnp.zeros((CHUNK, CHUNK), F32)
            dbh = jnp.zeros((CHUNK, dh), F32)
            for n in range(tr // CHUNK):
                rows = slice(n * CHUNK, (n + 1) * CHUNK)
                vb = v[rows, cols]
                mixed = jnp.dot(wm, vb, preferred_element_type=F32) + b_ref[h]
                sup = su_ref[rows, cols].astype(F32)
                dyd = d_ref[rows, cols]
                dmx = dyd * _gelu(sup)
                o_ref[1, rows, cols] = (dyd * mixed * _gelu_grad(sup)).astype(BF16)
                dmb = dmx.astype(BF16)
                dwh = dwh + lax.dot_general(dmb, vb, (((1,), (1,)), ((), ())), preferred_element_type=F32)
                dbh = dbh + dmx
                dv_ref[rows, cols] = lax.dot_general(wm, dmb, (((0,), (0,)), ((), ())), preferred_element_type=F32)
            dw_ref[h] += jnp.where(tri_r >= tri_c, dwh, 0.0)
            db_ref[h] += dbh
        dv = dv_ref[...]
        acc_g[...] += jnp.sum((dv * vh).reshape(tr // SUBLANES, SUBLANES, C), axis=0)
        dvg = dv * gv
        dsv = r * (dvg - vh * jnp.mean(dvg * vh, axis=-1, keepdims=True))
        o_ref[2] = (dsv * _gelu_grad(svp)).astype(BF16)

        @pl.when(i == nsteps - 1)
        def _():
            dg_ref[...] = jnp.sum(acc_g[...], axis=0, keepdims=True)

    full = lambda shp: pl.BlockSpec(shp, lambda i: (0,) * len(shp))
    return pl.pallas_call(
        body, name=name,
        out_shape=(jax.ShapeDtypeStruct((3, L, C), BF16), jax.ShapeDtypeStruct((nh, CHUNK, CHUNK), F32),
                   jax.ShapeDtypeStruct((nh, CHUNK, dh), F32), jax.ShapeDtypeStruct((1, C), F32)),
        grid=(nsteps,),
        in_specs=[pl.BlockSpec((None, tr, C), lambda i: (1, i, 0)), pl.BlockSpec((None, tr, C), lambda i: (2, i, 0)),
                  pl.BlockSpec((tr, C), lambda i: (i, 1)), pl.BlockSpec((tr, C), lambda i: (i, 0)), full((1, C)),
                  full((nh, CHUNK, CHUNK)), full((nh, CHUNK, dh))],
        out_specs=(pl.BlockSpec((3, tr, C), lambda i: (0, i, 0)), full((nh, CHUNK, CHUNK)), full((nh, CHUNK, dh)),
                   full((1, C))),
        scratch_shapes=[pltpu.VMEM((tr, C), F32), pltpu.VMEM((SUBLANES, C), F32)],
        compiler_params=_cparams(("arbitrary",)),
    )(proj3, proj3, dmix, dz, norm_g.reshape(1, C), w, bfull)


def _ffn_act_fwd(up3, conv_w, conv_b, name):
    _, L, Fh = up3.shape
    cb = LANES
    w2 = conv_w.reshape(3, 2, Fh).transpose(1, 0, 2)
    b2 = conv_b.reshape(2, 1, Fh)

    def body(u_ref, w_ref, b_ref, o_ref, ot_ref, gv_ref):
        g = _conv3(_taps(u_ref[0].astype(F32)), w_ref[0]) + b_ref[0]
        v = _conv3(_taps(u_ref[1].astype(F32)), w_ref[1]) + b_ref[1]
        gv_ref[0] = g.astype(BF16)
        gv_ref[1] = v.astype(BF16)
        ab = (g * _sigmoid(g) * v).astype(BF16)
        o_ref[...] = ab
        ot_ref[...] = _transpose_on_mxu(ab)

    blk3 = pl.BlockSpec((2, L, cb), lambda j: (0, 0, j))
    return pl.pallas_call(
        body, name=name,
        out_shape=(jax.ShapeDtypeStruct((L, Fh), BF16), jax.ShapeDtypeStruct((Fh, L), BF16),
                   jax.ShapeDtypeStruct((2, L, Fh), BF16)),
        grid=(Fh // cb,),
        in_specs=[blk3, pl.BlockSpec((2, 3, cb), lambda j: (0, 0, j)), pl.BlockSpec((2, 1, cb), lambda j: (0, 0, j))],
        out_specs=(pl.BlockSpec((L, cb), lambda j: (0, j)), pl.BlockSpec((cb, L), lambda j: (j, 0)), blk3),
        compiler_params=_cparams(("parallel",)),
    )(up3, w2, b2)


def _ffn_act_bwd(up3, gv3, da, conv_w, h2t, name):
    _, L, Fh = up3.shape
    D = h2t.shape[0]
    cb = LANES
    nb = Fh // cb
    w2 = conv_w.reshape(3, 2, Fh).transpose(1, 0, 2)

    def body(u_ref, gv_ref, d_ref, w_ref, h_ref, o_ref, dw_ref, db_ref, wg_ref, wv_ref, scr):
        j = pl.program_id(0)

        @pl.when(j == 0)
        def _():
            scr[1] = jnp.zeros((2, L, cb), BF16)

        prev = scr.at[(j + 1) % 2]
        wg_ref[...] = jnp.dot(h_ref[...], prev[0], preferred_element_type=F32).astype(BF16)
        wv_ref[...] = jnp.dot(h_ref[...], prev[1], preferred_element_type=F32).astype(BF16)
        tg, tv = _taps(u_ref[0].astype(F32)), _taps(u_ref[1].astype(F32))
        wg, wv = w_ref[0], w_ref[1]
        g = gv_ref[0].astype(F32)
        v = gv_ref[1].astype(F32)
        sg = _sigmoid(g)
        dav = d_ref[...].astype(F32)
        dg = dav * v * (sg * (1.0 + g * (1.0 - sg)))
        dv = dav * (g * sg)
        dug = _conv3_t(dg, wg).astype(BF16)
        duv = _conv3_t(dv, wv).astype(BF16)
        o_ref[0] = dug
        o_ref[1] = duv
        cur = scr.at[j % 2]
        cur[0] = dug
        cur[1] = duv
        for tap, (dwg, dwv) in enumerate(zip(_conv3_dw(dg, tg), _conv3_dw(dv, tv))):
            dw_ref[0, tap:tap + 1, :] = dwg
            dw_ref[1, tap:tap + 1, :] = dwv
        db_ref[0] = jnp.sum(dg, axis=0, keepdims=True)
        db_ref[1] = jnp.sum(dv, axis=0, keepdims=True)

    here = lambda j: jnp.minimum(j, nb - 1)
    before = lambda j: jnp.maximum(j - 1, 0)
    blk3 = pl.BlockSpec((2, L, cb), lambda j: (0, 0, here(j)))
    dup, dw2, db2, dwg, dwv = pl.pallas_call(
        body, name=name,
        out_shape=(jax.ShapeDtypeStruct((2, L, Fh), BF16), jax.ShapeDtypeStruct((2, 3, Fh), F32),
                   jax.ShapeDtypeStruct((2, 1, Fh), F32), jax.ShapeDtypeStruct((D, Fh), BF16),
                   jax.ShapeDtypeStruct((D, Fh), BF16)),
        grid=(nb + 1,),
        in_specs=[blk3, blk3, pl.BlockSpec((L, cb), lambda j: (0, here(j))),
                  pl.BlockSpec((2, 3, cb), lambda j: (0, 0, here(j))), pl.BlockSpec((D, L), lambda j: (0, 0))],
        out_specs=(blk3, pl.BlockSpec((2, 3, cb), lambda j: (0, 0, here(j))),
                   pl.BlockSpec((2, 1, cb), lambda j: (0, 0, here(j))),
                   pl.BlockSpec((D, cb), lambda j: (0, before(j))), pl.BlockSpec((D, cb), lambda j: (0, before(j)))),
        scratch_shapes=[pltpu.VMEM((2, 2, L, cb), BF16)],
        compiler_params=_cparams(("arbitrary",), VMEM_LIMIT_S5),
    )(up3, gv3, da, w2, h2t)
    return dup, dw2.transpose(1, 0, 2).reshape(3, 2 * Fh), db2.reshape(2 * Fh), jnp.concatenate([dwg, dwv], axis=1)


def _local_step(x, tgt, w, layer_weights, on_layer_grads):
    L, D = x.shape
    depth = w['norm_mix_g'].shape[0]
    saved = []
    for i in range(depth):
        j = i // 2
        wb = dict(layer_weights(2 * i, x))
        s = {'x': x, 'wb': wb}
        if i % 2 == 0:
            proj4, s['hT'] = _norm_mm(x, w['norm_mix_g'][i], wb['even_w_in'], BF16, "even_in_fwd", ok=('seg', 4))
            s['proj'] = proj4
            mixin = _sconv_fwd(proj4, w['even_conv_w'][j], "sconv_fwd")
            prm = (w['ssm_log_step'][j], w['ssm_a_re'][j], w['ssm_a_im'][j], w['ssm_b_re'][j], w['ssm_b_im'][j],
                   w['ssm_c_re'][j], w['ssm_c_im'][j])
            (lr, li, bmat, cmat), prep_vjp = jax.vjp(_s5_prep, *prm)
            yraw, s_re, s_im = _s5_fwd(proj4, lr, li, bmat, cmat, w['ssm_d'][j], "s5_fwd")
            mixin = _glu_fwd(yraw, wb['ssm_glu_w'], w['ssm_glu_b'][j], mixin, "glu_fwd")
            s.update(yraw=yraw, s_re=s_re, s_im=s_im, s5=(lr, li, bmat, cmat), prep_vjp=prep_vjp)
            s['mixinT'] = mixin.T
            x = _mm(mixin, wb['even_w_out'], 'nn', F32, "even_out_fwd", res=x)
        else:
            proj3, s['hT'] = _norm_mm(x, w['norm_mix_g'][i], wb['odd_w_in'], BF16, "odd_in_fwd", ok=('seg', 3))
            s['proj'] = proj3
            mixin, mixin_t = _pool_fwd(proj3, w['pool_w'][j], w['pool_scale'][j], "pool_fwd")
            mixin, s['mixinT'] = _sgu_fwd(proj3, w['sgu_norm_g'][j], w['sgu_w'][j], w['sgu_b'][j], mixin, mixin_t,
                                          "sgu_fwd")
            x = _mm(mixin, wb['odd_w_out'], 'nn', F32, "odd_out_fwd", res=x)
        s['x1'] = x
        wb.update(layer_weights(2 * i + 1, x))
        up3, h2t = _norm_mm(x, w['norm_ffn_g'][i], wb['ffn_w_up'], BF16, "ffn_up_fwd", ok=('seg', 2))
        a, at, gv3 = _ffn_act_fwd(up3, w['ffn_conv_w'][i], w['ffn_conv_b'][i], "ffn_act_fwd")
        x = _mm(a, wb['ffn_w_down'], 'nn', F32, "ffn_down_fwd", res=x)
        s.update(h2T=h2t, up3=up3, aT=at, gv3=gv3)
        saved.append(s)

    loss8, dx, dxb, dg_final = _loss_head(x, w['norm_final_g'], tgt)
    gs = {n: [None] * w[n].shape[0] for n in SMALL if n != 'norm_final_g'}
    gs['norm_final_g'] = dg_final.reshape(D)

    dep = None
    for i in reversed(range(depth)):
        j = i // 2
        s = saved[i]
        wb = s['wb']
        gb = {}
        da = _mm(dxb, wb['ffn_w_down'], 'nt', BF16, "ffn_down_dgrad", dep=dep)
        gb['ffn_w_down'] = _mm(s['aT'], dxb, 'nn', BF16, "ffn_down_wgrad")
        dup3, dcw, dcb, gb['ffn_w_up'] = _ffn_act_bwd(s['up3'], s['gv3'], da, w['ffn_conv_w'][i], s['h2T'],
                                                      "ffn_act_bwd")
        gs['ffn_conv_w'][i], gs['ffn_conv_b'][i] = dcw, dcb
        dep = on_layer_grads(2 * i + 1, gb)
        dx, dxb, dg = _mm_norm_bwd(dup3, wb['ffn_w_up'], s['x1'], w['norm_ffn_g'][i], dx, "ffn_up_dgrad",
                              ak=('seg', 2), dep=dep)
        gs['norm_ffn_g'][i] = dg.reshape(D)
        gb = {}
        if i % 2 == 0:
            dmix = _mm(dxb, wb['even_w_out'], 'nt', F32, "even_out_dgrad")
            gb['even_w_out'] = _mm(s['mixinT'], dxb, 'nn', BF16, "even_out_wgrad")
            dproj, dcw = _sconv_bwd(s['proj'], dmix, w['even_conv_w'][j], "sconv_bwd")
            gs['even_conv_w'][j] = dcw
            dyraw, dglu_w, dglu_b = _glu_bwd(s['yraw'], dmix, wb['ssm_glu_w'], w['ssm_glu_b'][j], "glu_bwd")
            gb['ssm_glu_w'] = dglu_w.astype(BF16)
            gs['ssm_glu_b'][j] = dglu_b.reshape(-1)
            lr, li, bmat, cmat = s['s5']
            dproj, dbm, dcm, dlam, dd = _s5_bwd(dyraw, s['proj'], dproj, s['s_re'], s['s_im'], lr, li, bmat, cmat,
                                               w['ssm_d'][j], "s5_bwd")
            gs['ssm_d'][j] = dd.reshape(-1)
            dcm = jnp.swapaxes(dcm, 1, 2)
            dprm = s['prep_vjp']((dlam[:, 0:1, :], dlam[:, 1:2, :], dbm, dcm))
            for n, gval in zip(('ssm_log_step', 'ssm_a_re', 'ssm_a_im', 'ssm_b_re', 'ssm_b_im', 'ssm_c_re',
                                'ssm_c_im'), dprm):
                gs[n][j] = gval
            gb['even_w_in'] = _mm(s['hT'], dproj, 'nn', BF16, "even_in_wgrad", bk=('seg', 4))
            w_in, in_kind, in_name = wb['even_w_in'], ('seg', 4), "even_in_dgrad"
        else:
            dmix = _mm(dxb, wb['odd_w_out'], 'nt', F32, "odd_out_dgrad")
            gb['odd_w_out'] = _mm(s['mixinT'], dxb, 'nn', BF16, "odd_out_wgrad")
            dz, dpw, dps = _pool_bwd(s['proj'], dmix, w['pool_w'][j], w['pool_scale'][j], "pool_bwd")
            gs['pool_w'][j], gs['pool_scale'][j] = dpw, dps.reshape(-1)
            dproj, dsw, dsb, dsg = _sgu_bwd(s['proj'], dmix, dz, w['sgu_norm_g'][j], w['sgu_w'][j], w['sgu_b'][j],
                                            "sgu_bwd")
            gs['sgu_w'][j], gs['sgu_b'][j], gs['sgu_norm_g'][j] = dsw, jnp.sum(dsb, axis=-1), dsg.reshape(-1)
            gb['odd_w_in'] = _mm(s['hT'], dproj, 'nn', BF16, "odd_in_wgrad", bk=('seg', 3))
            w_in, in_kind, in_name = wb['odd_w_in'], ('seg', 3), "odd_in_dgrad"
        dep = on_layer_grads(2 * i, gb)
        dx, dxb, dg = _mm_norm_bwd(dproj, w_in, s['x'], w['norm_mix_g'][i], dx, in_name, ak=in_kind, dep=dep)
        gs['norm_mix_g'][i] = dg.reshape(D)

    gsmall = {n: (v if n == 'norm_final_g' else jnp.stack(v)) for n, v in gs.items()}
    return loss8[0, 0], dx, gsmall


_HBM = pl.BlockSpec(memory_space=pltpu.HBM)
_CHIP_FLIPS = ((0, 0), (1, 0), (0, 1), (1, 1))


def _coords():
    return lax.axis_index("x"), lax.axis_index("y"), lax.axis_index("c")


def _flip(v, f):
    return 1 - v if f else v


def _shard_of(ref, axis, s, width):
    start = pl.multiple_of(s * width, LANES if axis == ref.ndim - 1 else 16) if width % 16 == 0 else s * width
    idx = [slice(None)] * ref.ndim
    idx[axis] = pl.ds(start, width)
    return ref.at[tuple(idx)]


_SEM = pl.BlockSpec(memory_space=pltpu.SEMAPHORE)
_ANY = pl.BlockSpec(memory_space=pl.ANY)
_DATAFLOW = pltpu.SideEffectType.DATAFLOW_SIDE_EFFECTING


def _in_hbm(a):
    return pltpu.with_memory_space_constraint(a, pltpu.HBM)


def _model_layer(name, l):
    if name.startswith('ffn'):
        return l
    return 2 * l + 1 if name.startswith('odd') else 2 * l


def _place_quarter(shard, l, axis, chip, dtype, dep=None):
    _, r, c = shard.shape
    tr = _pick(r, prefs=(512, 256, 128, 64, 32, 16))
    nrb = r // tr

    def body(chip_ref, i_ref, *rest):
        rest[-1][...] = i_ref[...].astype(dtype)

    if axis == 1:
        out_shape, o_map = (r, c * N_CHIPS), (lambda i, s: (i, s[0]))
    else:
        out_shape, o_map = (r * N_CHIPS, c), (lambda i, s: (s[0] * nrb + i, 0))
    in_specs = [pl.BlockSpec((None, tr, c), lambda i, s: (l, i, 0))]
    args = [chip, shard]
    if dep is not None:
        in_specs.append(pl.BlockSpec(memory_space=pl.ANY))
        args.append(dep)
    return pl.pallas_call(
        body, name="place_quarter", out_shape=jax.ShapeDtypeStruct(out_shape, dtype),
        grid_spec=pltpu.PrefetchScalarGridSpec(
            num_scalar_prefetch=1, grid=(nrb,), in_specs=in_specs, out_specs=pl.BlockSpec((tr, c), o_map)),
        compiler_params=_cparams(("parallel",)),
    )(*args)


def _gather_copies(land_refs, send_sem, recv_sem, axes, landing_chip_of, first=0):
    x, y, c = _coords()
    out = []
    for j, land in enumerate(land_refs):
        width = land.shape[axes[j]] // N_CHIPS
        for f in (1, 2, 3):
            fx, fy = _CHIP_FLIPS[f]
            px, py = _flip(x, fx), _flip(y, fy)
            lx, ly = landing_chip_of(px, py)
            out.append(pltpu.make_async_remote_copy(
                src_ref=_shard_of(land, axes[j], 2 * x + y, width), dst_ref=_shard_of(land, axes[j], 2 * lx + ly, width),
                send_sem=send_sem.at[3 * (first + j) + f - 1], recv_sem=recv_sem.at[3 * (first + j) + f - 1],
                device_id=(px, py, c), device_id_type=MESH))
    return out


def _gather_start(tag, lands, axes, dep=None):
    n = len(lands)

    def body(*refs):
        land_refs, send_sem, recv_sem = refs[:n], refs[-3], refs[-2]
        x, y, _ = _coords()
        for cp in _gather_copies(land_refs, send_sem, recv_sem, axes, lambda px, py: (x, y)):
            cp.start()
        refs[-1][...] = jnp.zeros_like(refs[-1])

    thru = [pltpu.HBM(a.shape, a.dtype) for a in lands]
    outs = pl.pallas_call(
        body, name=f"gather_start_{tag}",
        out_shape=tuple(thru + [pltpu.SemaphoreType.DMA((3 * n,)), pltpu.SemaphoreType.DMA((3 * n,)),
                                jax.ShapeDtypeStruct((SUBLANES, LANES), F32)]),
        in_specs=[_HBM] * n + ([_ANY] if dep is not None else []),
        out_specs=tuple([_HBM] * n + [_SEM, _SEM, pl.BlockSpec(memory_space=pltpu.VMEM)]),
        input_output_aliases={i: i for i in range(n)},
        compiler_params=pltpu.CompilerParams(has_side_effects=_DATAFLOW),
    )(*[_in_hbm(a) for a in lands], *([dep] if dep is not None else []))
    return list(outs[:n]), outs[n], outs[n + 1], outs[n + 2]


def _gather_wait(tag, lands, send_sem, recv_sem, axes, after, first=0):
    n = len(lands)

    def body(*refs):
        for cp in _gather_copies(refs[:n], refs[n], refs[n + 1], axes, lambda px, py: (px, py), first):
            cp.wait_send()
            cp.wait_recv()

    outs = pl.pallas_call(
        body, name=f"gather_wait_{tag}", out_shape=tuple(pltpu.HBM(a.shape, a.dtype) for a in lands),
        in_specs=[_HBM] * n + [_SEM, _SEM, _ANY], out_specs=tuple([_HBM] * n),
        input_output_aliases={i: i for i in range(n)},
        compiler_params=pltpu.CompilerParams(has_side_effects=_DATAFLOW),
    )(*lands, send_sem, recv_sem, after)
    return list(outs)


N_SLOTS = N_DEV - 1


def _scatter_sends(grad_refs, land_refs, send_sem, recv_sem, meta):
    x, y, c = _coords()
    out = []
    for j, (axis, owner, q, width) in enumerate(meta):
        other = c if owner == 0 else 1 - c
        for f, (fx, fy) in enumerate(_CHIP_FLIPS):
            px, py = _flip(x, fx), _flip(y, fy)
            slot = f + 4 * other - 1
            out.append((other if f == 0 else None, pltpu.make_async_remote_copy(
                src_ref=_shard_of(grad_refs[j], axis, 2 * px + py, width), dst_ref=land_refs[j].at[q, slot],
                send_sem=send_sem.at[4 * j + f], recv_sem=recv_sem.at[N_SLOTS * j + slot],
                device_id=(px, py, owner), device_id_type=MESH)))
    return out


def _scatter_start(layer, grads, lands, meta):
    n = len(grads)
    uniq = []
    for a in lands:
        if not any(a is u for u in uniq):
            uniq.append(a)
    which = [next(k for k, u in enumerate(uniq) if u is a) for a in lands]
    nu = len(uniq)

    def body(*refs):
        grad_refs, land_u = refs[:n], refs[n:n + nu]
        send_sem, recv_sem = refs[n + nu], refs[n + nu + 1]
        for other, cp in _scatter_sends(grad_refs, [land_u[k] for k in which], send_sem, recv_sem, meta):
            if other is None:
                cp.start()
            else:
                pl.when(other == 1)(cp.start)
        refs[-1][...] = jnp.zeros_like(refs[-1])

    thru = [pltpu.HBM(a.shape, a.dtype) for a in list(grads) + uniq]
    outs = pl.pallas_call(
        body, name=f"scatter_start_{layer}",
        out_shape=tuple([pltpu.SemaphoreType.DMA((4 * n,)), pltpu.SemaphoreType.DMA((N_SLOTS * n,))] + thru
                        + [jax.ShapeDtypeStruct((SUBLANES, LANES), F32)]),
        in_specs=[_HBM] * (n + nu),
        out_specs=tuple([_SEM, _SEM] + [_HBM] * (n + nu) + [pl.BlockSpec(memory_space=pltpu.VMEM)]),
        input_output_aliases={i: 2 + i for i in range(n + nu)},
        compiler_params=pltpu.CompilerParams(has_side_effects=_DATAFLOW),
    )(*[_in_hbm(a) for a in list(grads) + uniq])
    new_lands = [outs[2 + n + k] for k in which]
    return outs[0], outs[1], list(outs[2:2 + n]), new_lands, outs[-1]


def _scatter_wait(started, lands):
    nl = len(lands)
    flat_grads = [g for s in started for g in s[2]]
    ng, ns = len(flat_grads), len(started)

    def body(*refs):
        land_refs = refs[:nl]
        grad_refs = refs[nl:nl + ng]
        sem_refs = refs[nl + ng:nl + ng + 2 * ns]
        _, _, c = _coords()
        off = 0
        for k, (_, _, grads, idx, meta) in enumerate(started):
            send_sem, recv_sem = sem_refs[2 * k], sem_refs[2 * k + 1]
            lr = [land_refs[i] for i in idx]
            for other, cp in _scatter_sends(grad_refs[off:off + len(grads)], lr, send_sem, recv_sem, meta):
                if other is None:
                    cp.wait_send()
                else:
                    pl.when(other == 1)(cp.wait_send)
            for j, (axis, owner, q, width) in enumerate(meta):
                mine = (c if owner == 0 else 1 - c) == 0

                @pl.when(mine)
                def _():
                    for slot in range(N_SLOTS):
                        land = lr[j].at[q, slot]
                        pltpu.make_async_remote_copy(
                            src_ref=land, dst_ref=land, send_sem=send_sem.at[0], recv_sem=recv_sem.at[N_SLOTS * j + slot],
                            device_id=_coords(), device_id_type=MESH).wait_recv()
            off += len(grads)

    args = list(lands) + flat_grads
    thru = [pltpu.HBM(a.shape, a.dtype) for a in args]
    sems = [s for st in started for s in st[:2]]
    outs = pl.pallas_call(
        body, name="scatter_wait", out_shape=tuple(thru), in_specs=[_HBM] * (nl + ng) + [_SEM] * (2 * ns),
        out_specs=tuple([_HBM] * (nl + ng)), input_output_aliases={i: i for i in range(nl + ng)},
        compiler_params=pltpu.CompilerParams(has_side_effects=_DATAFLOW),
    )(*args, *sems)
    return list(outs[:nl]), list(outs[nl:])


def _sum_and_share(recv, layer_grads, axis, chip, name, dep=None):
    n, ns, r, c = recv.shape
    tr = _pick(r, prefs=(256, 128, 64, 32, 16))
    nr = r // tr
    nsteps = n * nr
    nlay = len(layer_grads)
    own_map = (lambda h, i, s: (i, s[0])) if axis == 1 else (lambda h, i, s: (s[0] * nr + i, 0))

    def body(chip_ref, i_ref, *rest):
        g_refs = rest[:nlay]
        o_ref, buf, loc_sems, send_sems, recv_sems = rest[nlay + (dep is not None):]
        h, i = pl.program_id(0), pl.program_id(1)
        step = h * nr + i
        slot = step % 2
        x, y, core = _coords()
        layer = core * n + h
        own = g_refs[0][...]
        for l in range(1, nlay):
            own = jnp.where(layer == l, g_refs[l][...], own)

        def copies(sl):
            dst = o_ref.at[core * n + h, pl.ds(pl.multiple_of(i * tr, tr), tr), :]
            loc = pltpu.make_async_copy(buf.at[sl], dst, loc_sems.at[sl])
            rem = pltpu.make_async_remote_copy(
                src_ref=buf.at[sl], dst_ref=dst, send_sem=send_sems.at[sl], recv_sem=recv_sems.at[step],
                device_id=(x, y, 1 - core), device_id_type=MESH)
            return loc, rem

        def drain(sl):
            loc, rem = copies(sl)
            loc.wait()
            rem.wait_send()

        pl.when(step >= 2)(lambda: drain(slot))
        acc = own.astype(F32)
        for s in range(ns):
            acc = acc + i_ref[s].astype(F32)
        buf[slot] = acc
        loc, rem = copies(slot)
        loc.start()
        rem.start()

        @pl.when(step == nsteps - 1)
        def _():
            drain(slot)
            if nsteps > 1:
                drain(1 - slot)
            for hh in range(n):
                for ii in range(nr):
                    land = o_ref.at[(1 - core) * n + hh, pl.ds(ii * tr, tr), :]
                    pltpu.make_async_remote_copy(
                        src_ref=buf.at[0], dst_ref=land, send_sem=send_sems.at[0], recv_sem=recv_sems.at[hh * nr + ii],
                        device_id=(x, y, 1 - core), device_id_type=MESH).wait_recv()

    return pl.pallas_call(
        body, name=name, out_shape=jax.ShapeDtypeStruct((2 * n, r, c), F32),
        grid_spec=pltpu.PrefetchScalarGridSpec(
            num_scalar_prefetch=1, grid=(n, nr),
            in_specs=[pl.BlockSpec((None, ns, tr, c), lambda h, i, s: (h, 0, i, 0))]
            + [pl.BlockSpec((tr, c), own_map)] * nlay + ([pl.BlockSpec(memory_space=pl.ANY)] if dep is not None else []),
            out_specs=_HBM,
            scratch_shapes=[pltpu.VMEM((2, tr, c), F32), pltpu.SemaphoreType.DMA((2,)),
                            pltpu.SemaphoreType.DMA((2,)), pltpu.SemaphoreType.DMA((nsteps,))]),
        compiler_params=_cparams(("arbitrary", "arbitrary")),
    )(chip, recv, *layer_grads, *([dep] if dep is not None else []))


def _adamw_update(w_ref, g_ref, m_ref, v_ref, d_ref, mo_ref, vo_ref):
    bc1 = 1.0 - ADAM_B1 ** ADAM_STEP
    bc2 = 1.0 - ADAM_B2 ** ADAM_STEP
    gv = g_ref[...]
    mn = ADAM_B1 * m_ref[...] + (1.0 - ADAM_B1) * gv
    vn = ADAM_B2 * v_ref[...] + (1.0 - ADAM_B2) * (gv * gv)
    d_ref[...] = -ADAM_LR * ((mn / bc1) / (jnp.sqrt(vn / bc2) + ADAM_EPS) + ADAM_WD * w_ref[...])
    mo_ref[...] = mn
    vo_ref[...] = vn


def _adamw(w, g, m, v, name):
    def body(*refs):
        _adamw_update(*refs)

    tr = _pick(w.shape[0], prefs=(256, 128, 64, 32, 16, 8))
    blk = pl.BlockSpec((tr, w.shape[1]), lambda i: (i, 0))
    sds = jax.ShapeDtypeStruct(w.shape, F32)
    return pl.pallas_call(
        body, name=name, out_shape=(sds, sds, sds), grid=(w.shape[0] // tr,), in_specs=[blk] * 4,
        out_specs=(blk,) * 3, compiler_params=_cparams(("parallel",)),
    )(w, g, m, v)


def _adamw_many(tensors, name, by_layer=False):
    n = len(tensors)

    def body(*refs):
        for t in range(n):
            _adamw_update(*refs[4 * t:4 * t + 4], *refs[4 * n + 3 * t:4 * n + 3 * t + 3])

    def spec(a):
        nd = a.ndim
        if by_layer:
            return pl.BlockSpec((1,) + a.shape[1:], lambda i: (i,) + (0,) * (nd - 1))
        return pl.BlockSpec(a.shape, lambda i: (0,) * nd)

    steps = tensors[0][0].shape[0] if by_layer else 1
    outs = pl.pallas_call(
        body, name=name, out_shape=tuple(jax.ShapeDtypeStruct(t[0].shape, F32) for t in tensors for _ in range(3)),
        grid=(steps,), in_specs=[spec(a) for t in tensors for a in t],
        out_specs=tuple(spec(t[0]) for t in tensors for _ in range(3)), compiler_params=_cparams(("parallel",)),
    )(*[a for t in tensors for a in t])
    return [tuple(outs[3 * t:3 * t + 3]) for t in range(n)]


_PACK_QUANTUM = 256 * LANES


def _pack(arrs):
    flat = jnp.concatenate([a.reshape(-1).astype(F32) for a in arrs])
    flat = jnp.pad(flat, (0, (-flat.shape[0]) % _PACK_QUANTUM))
    return flat.reshape(-1, LANES)


def _unpack(p, shapes):
    flat = p.reshape(-1)
    out, off = [], 0
    for s in shapes:
        n = int(np.prod(s))
        out.append(flat[off:off + n].reshape(s))
        off += n
    return out


def kernel(*args):
    nw = len(WEIGHTS)
    x, tgt = args[0], args[1 + nw]
    w = dict(zip(WEIGHTS, args[1:1 + nw]))
    m = dict(zip(WEIGHTS, args[2 + nw:2 + 2 * nw]))
    v = dict(zip(WEIGHTS, args[2 + 2 * nw:2 + 3 * nw]))
    _, L, D = x.shape
    chip = 2 * lax.axis_index("x") + lax.axis_index("y")

    big = list(BIG)
    small_sh_shapes = [w[n].shape for n in SMALL_SHARDED]
    nbig = len(big)
    chip1 = chip.reshape(1).astype(jnp.int32)
    axes2 = [BIG[n] - 1 for n in big] + [0]
    shards = [w[n] for n in big] + [_pack([w[n] for n in SMALL_SHARDED])[None]]
    pairs = [(t, l) for t in range(nbig + 1) for l in range(shards[t].shape[0])]
    depth = w['norm_mix_g'].shape[0]
    part_of = lambda t, l: 0 if t == nbig else 2 * _model_layer(big[t], l) + big[t].startswith('ffn')
    flying, token = {}, None
    for tag, gset in enumerate(([0], list(range(1, 2 * depth)))):
        ids = [k for g in gset for k, (t, l) in enumerate(pairs) if part_of(t, l) == g]
        ts = [pairs[k][0] for k in ids]
        placed = [_place_quarter(shards[t], pairs[k][1], axes2[t], chip1, F32 if t == nbig else BF16, token)
                  for k, t in zip(ids, ts)]
        lands, send, recv, token = _gather_start(tag, placed, [axes2[t] for t in ts], token)
        first = 0
        for g in gset:
            n = sum(1 for t, l in pairs if part_of(t, l) == g)
            flying[g] = (ts[first:first + n], lands[first:first + n], send, recv, first)
            first += n

    def wait_group(g, after):
        ts, lands, send, recv, first = flying[g]
        landed = _gather_wait(g, lands, send, recv, [axes2[t] for t in ts], token if after is None else after, first)
        return dict(zip(ts, landed))

    first = wait_group(0, None)
    packed = first.pop(nbig).reshape(N_CHIPS, -1, LANES)
    per_chip = [_unpack(packed[s], small_sh_shapes) for s in range(N_CHIPS)]
    wl = dict(w)
    for k, n in enumerate(SMALL_SHARDED):
        wl[n] = jnp.concatenate([per_chip[s][k] for s in range(N_CHIPS)], axis=-1)

    def layer_weights(i, after):
        got = first if i == 0 else wait_group(i, after)
        return {big[t]: a for t, a in got.items()}

    small_shapes = [(w[n].shape[:-1] + (w[n].shape[-1] * N_CHIPS,)) if n in SMALL_SHARDED else w[n].shape
                    for n in SMALL] + [(1,)]
    n_small = sum(int(np.prod(s)) for s in small_shapes)
    pack_rows = -(-n_small // _PACK_QUANTUM) * _PACK_QUANTUM // LANES
    nlayers = [w[n].shape[0] for n in big] + [2]
    halves = [n // 2 for n in nlayers]
    quarters = [tuple(w[n].shape[1:]) for n in big] + [(pack_rows // 2 // N_CHIPS, LANES)]
    wire = [BF16] * nbig + [F32]
    land_now = [lax.empty((halves[t], N_SLOTS) + quarters[t], wire[t]) for t in range(nbig + 1)]
    gparts = [[None] * n for n in nlayers]
    started = []

    def start_scatter(tag, ts, ls, arrays):
        meta = [(axes2[t], l // halves[t], l % halves[t], quarters[t][axes2[t]]) for t, l in zip(ts, ls)]
        send, recv, thru, new_lands, token = _scatter_start(tag, arrays, [land_now[t] for t in ts], meta)
        for t, ln in zip(ts, new_lands):
            land_now[t] = ln
        started.append((send, recv, thru, ts, meta, ls))
        return token

    def on_layer_grads(g, gb):
        ts = [big.index(n) for n in gb]
        return start_scatter(g, ts, [g // 2 if big[t].startswith('ffn') else g // 4 for t in ts],
                             [gb[big[t]] for t in ts])

    loss, dx, gsmall = _local_step(x.reshape(L, D), tgt.reshape(L, D), wl, layer_weights, on_layer_grads)
    gpack = _pack([gsmall[n] for n in SMALL] + [loss.reshape(1)])
    start_scatter(2 * depth, [nbig, nbig], [0, 1], [gpack[:pack_rows // 2], gpack[pack_rows // 2:]])
    landed, sent = _scatter_wait([s[:5] for s in started], land_now)
    for (t, l), g in zip([(t, l) for s in started for t, l in zip(s[3], s[5])], sent):
        gparts[t][l] = g
    small_sum = _sum_and_share(landed[nbig], gparts[nbig], 0, chip1, "sum_share_small")
    quarter_rows = small_sum.shape[0] * small_sum.shape[1]
    placed = _place_quarter(small_sum.reshape(1, quarter_rows, LANES), 0, 0, chip1, F32)
    flying_small, send, recv, token = _gather_start("small", [placed], [0])
    gshard = {n: _sum_and_share(landed[t], gparts[t], axes2[t], chip1, "sum_share_" + n, token)
              for t, n in enumerate(big)}
    small_all = _gather_wait("small", flying_small, send, recv, [0], gshard[big[-1]])[0]
    gpack = small_all.reshape(N_CHIPS, 2, quarter_rows // 2, LANES).transpose(1, 0, 2, 3).reshape(pack_rows, LANES)
    gs = dict(zip(SMALL + ['loss'], _unpack(gpack, small_shapes)))
    loss = gs.pop('loss').reshape(())
    for n in SMALL_SHARDED:
        width = w[n].shape[-1]
        gs[n] = lax.dynamic_slice_in_dim(gs[n], chip * width, width, axis=gs[n].ndim - 1)

    grads, delta, new_m, new_v = {}, {}, {}, {}
    for n in big:
        shp = w[n].shape
        flat = lambda a: a.reshape(shp[0] * shp[1], shp[2])
        g = gshard[n]
        grads[n] = g
        d_, m_, v_ = _adamw(flat(w[n]), flat(g), flat(m[n]), flat(v[n]), "adamw_" + n)
        delta[n], new_m[n], new_v[n] = d_.reshape(shp), m_.reshape(shp), v_.reshape(shp)
    sparse = [n for n in SMALL if w[n].ndim == 4 and w[n].shape[-1] < LANES // 2]
    for names, by_layer in ((sparse, True), ([n for n in SMALL if n not in sparse], False)):
        as2d = lambda a: a.reshape(1, -1) if a.ndim == 1 else a
        res = _adamw_many([(as2d(w[n]), as2d(gs[n]), as2d(m[n]), as2d(v[n])) for n in names],
                          "adamw_small_by_layer" if by_layer else "adamw_small", by_layer)
        for n, (d_, m_, v_) in zip(names, res):
            shp = w[n].shape
            grads[n], delta[n], new_m[n], new_v[n] = gs[n], d_.reshape(shp), m_.reshape(shp), v_.reshape(shp)

    return (loss, dx.reshape(1, L, D), *[grads[n] for n in WEIGHTS], *[delta[n] for n in WEIGHTS],
            *[new_m[n] for n in WEIGHTS], *[new_v[n] for n in WEIGHTS])
```

```python
import math

import numpy as np
import jax
import jax.numpy as jnp
from jax import lax
from jax.experimental import pallas as pl
from jax.experimental.pallas import tpu as pltpu

F32 = jnp.float32
BF16 = jnp.bfloat16
MESH = pl.DeviceIdType.MESH

EPS = 1e-6
CHUNK = 128
POOL_WINDOWS = (2, 4, 8, 16)
LANES = 128
SUBLANES = 8
SCAN_CHUNKS = SUBLANES
S5_GROUPS_PER_STEP = 4
MM_TM_CAP, MM_TN_CAP, MM_TK_CAP = 1408, 2816, 2048
MM_TK_WHOLE = 2816
VMEM_LIMIT = 48 * 1024 * 1024
VMEM_LIMIT_S5 = 56 * 1024 * 1024

ADAM_LR, ADAM_B1, ADAM_B2, ADAM_EPS, ADAM_WD, ADAM_STEP = 0.001, 0.9, 0.999, 1e-08, 0.01, 10

WEIGHTS = ['norm_mix_g', 'even_w_in', 'even_conv_w', 'ssm_log_step', 'ssm_a_re', 'ssm_a_im', 'ssm_b_re',
           'ssm_b_im', 'ssm_c_re', 'ssm_c_im', 'ssm_d', 'ssm_glu_w', 'ssm_glu_b', 'even_w_out', 'odd_w_in',
           'pool_w', 'pool_scale', 'sgu_norm_g', 'sgu_w', 'sgu_b', 'odd_w_out', 'norm_ffn_g', 'ffn_w_up',
           'ffn_conv_w', 'ffn_conv_b', 'ffn_w_down', 'norm_final_g']
BIG = {'even_w_in': 2, 'ssm_glu_w': 1, 'even_w_out': 1, 'odd_w_in': 2, 'odd_w_out': 1, 'ffn_w_up': 2,
       'ffn_w_down': 1}
SMALL_SHARDED = ('even_conv_w', 'pool_scale', 'sgu_norm_g', 'ffn_conv_w')
SMALL = [n for n in WEIGHTS if n not in BIG]
N_CHIPS = 4
N_DEV = 8


def _cparams(sem=None, vmem=VMEM_LIMIT):
    kw = dict(vmem_limit_bytes=vmem)
    if sem is not None:
        kw['dimension_semantics'] = sem
    return pltpu.CompilerParams(**kw)


def _pick(n, segs=(), prefs=(1024, 512, 256, 128)):
    for t in prefs:
        if n % t == 0 and all(s % t == 0 for s in segs if s):
            return t
    return n


def _largest_tile(n, segs, cap):
    best = None
    for t in range(LANES, min(n, cap) + 1, LANES):
        if n % t == 0 and all(s % t == 0 for s in segs if s):
            best = t
    return best if best is not None else n


def _ldims(arr, kind):
    if kind is None:
        return arr.shape
    if kind[0] == 'lead':
        return arr.shape[1:]
    return (arr.shape[1], arr.shape[0] * arr.shape[2])


def _segw(arr, kind):
    return arr.shape[2] if (kind is not None and kind[0] == 'seg') else None


def _opspec(arr, kind, br, bc, rfn, cfn):
    if kind is None:
        return pl.BlockSpec((br, bc), lambda i, j, k: (rfn(i, j, k), cfn(i, j, k)))
    if kind[0] == 'lead':
        lead = kind[1]
        return pl.BlockSpec((None, br, bc), lambda i, j, k: (lead, rfn(i, j, k), cfn(i, j, k)))
    per = arr.shape[2] // bc
    return pl.BlockSpec((None, br, bc), lambda i, j, k: (cfn(i, j, k) // per, rfn(i, j, k), cfn(i, j, k) % per))


def _mm(a, b, mode, out_dtype, name, ak=None, bk=None, ok=None, res=None, dep=None):
    ar, ac = _ldims(a, ak)
    br_, bc_ = _ldims(b, bk)
    if mode == 'nn':
        M, K, N = ar, ac, bc_
        assert br_ == K
    else:
        M, K, N = ar, ac, br_
        assert bc_ == K
    sa, sb = _segw(a, ak), _segw(b, bk)
    so = (N // ok[1]) if ok is not None else None
    tm = _largest_tile(M, [], MM_TM_CAP)
    tn = _largest_tile(N, [sb if mode == 'nn' else None, so], MM_TN_CAP)
    ksegs = [sa, sb if mode == 'nt' else None]
    tk = K if (K <= MM_TK_WHOLE and not any(ksegs)) else _largest_tile(K, ksegs, MM_TK_CAP)
    nk = K // tk
    I = lambda i, j, k: i
    J = lambda i, j, k: j
    Kk = lambda i, j, k: k
    a_spec = _opspec(a, ak, tm, tk, I, Kk)
    if mode == 'nn':
        b_spec = _opspec(b, bk, tk, tn, Kk, J)
        dims = (((1,), (0,)), ((), ()))
    else:
        b_spec = _opspec(b, bk, tn, tk, J, Kk)
        dims = (((1,), (1,)), ((), ()))
    if ok is None:
        out_shape = jax.ShapeDtypeStruct((M, N), out_dtype)
        o_spec = pl.BlockSpec((tm, tn), lambda i, j, k: (i, j))
    else:
        out_shape = jax.ShapeDtypeStruct((ok[1], M, N // ok[1]), out_dtype)
        per = (N // ok[1]) // tn
        o_spec = pl.BlockSpec((None, tm, tn), lambda i, j, k: (j // per, i, j % per))
    has_res = res is not None

    def body(*refs):
        a_ref, b_ref = refs[0], refs[1]
        r_ref = refs[2] if has_res else None
        o_ref = refs[n_in]
        prod = lax.dot_general(a_ref[...].astype(BF16), b_ref[...].astype(BF16), dims, preferred_element_type=F32)
        if nk == 1:
            o_ref[...] = (prod + r_ref[...] if has_res else prod).astype(out_dtype)
            return
        acc = refs[-1]
        k = pl.program_id(2)

        @pl.when(k == 0)
        def _():
            acc[...] = prod

        @pl.when(k > 0)
        def _():
            acc[...] += prod

        @pl.when(k == nk - 1)
        def _():
            o = acc[...]
            if has_res:
                o = o + r_ref[...]
            o_ref[...] = o.astype(out_dtype)

    in_specs = [a_spec, b_spec]
    args = [a, b]
    if has_res:
        in_specs.append(pl.BlockSpec((tm, tn), lambda i, j, k: (i, j)))
        args.append(res)
    if dep is not None:
        in_specs.append(pl.BlockSpec(memory_space=pl.ANY))
        args.append(dep)
    n_in = len(args)
    return pl.pallas_call(
        body, name=name, out_shape=out_shape, grid=(M // tm, N // tn, nk), in_specs=in_specs, out_specs=o_spec,
        scratch_shapes=[pltpu.VMEM((tm, tn), F32)] if nk > 1 else [],
        compiler_params=_cparams(("parallel", "parallel", "arbitrary")),
    )(*args)


_G0 = math.sqrt(2.0 / math.pi)
_G1 = 0.044715


def _gelu(x):
    return 0.5 * x * (1.0 + jnp.tanh(_G0 * (x + _G1 * x * x * x)))


def _gelu_grad(x):
    x2 = x * x
    t = jnp.tanh(_G0 * (x + _G1 * x * x2))
    return 0.5 * (1.0 + t) + 0.5 * x * (1.0 - t * t) * (_G0 * (1.0 + 3.0 * _G1 * x2))


def _sigmoid(x):
    return 1.0 / (1.0 + jnp.exp(-x))


def _down(v, k):
    r = pltpu.roll(v, k, axis=0)
    row = lax.broadcasted_iota(jnp.int32, (SUBLANES, v.shape[1]), 0)
    return jnp.concatenate([jnp.where(row >= k, r[:SUBLANES], 0.0), r[SUBLANES:]], axis=0)


def _up(v, k):
    n = v.shape[0]
    r = pltpu.roll(v, n - k, axis=0)
    row = lax.broadcasted_iota(jnp.int32, (SUBLANES, v.shape[1]), 0)
    return jnp.concatenate([r[:n - SUBLANES], jnp.where(row < SUBLANES - k, r[n - SUBLANES:], 0.0)], axis=0)


def _taps(v):
    return _down(v, 2), _down(v, 1), v


def _conv3(taps, w):
    return w[0:1, :] * taps[0] + w[1:2, :] * taps[1] + w[2:3, :] * taps[2]


def _conv3_t(dv, w):
    return w[2:3, :] * dv + w[1:2, :] * _up(dv, 1) + w[0:1, :] * _up(dv, 2)


def _conv3_dw(dv, taps):
    return tuple(jnp.sum(dv * tp, axis=0, keepdims=True) for tp in taps)


def _cmul(ar, ai, br, bi):
    return ar * br - ai * bi, ar * bi + ai * br


def _cpow(lr, li, n):
    rr = ri = None
    br, bi = lr, li
    while n:
        if n & 1:
            rr, ri = (br, bi) if rr is None else _cmul(rr, ri, br, bi)
        n >>= 1
        if n:
            br, bi = _cmul(br, bi, br, bi)
    return rr, ri


NORM_ROWS = 256


def _norm_mm(x, g, b, out_dtype, name, ok=None):
    M, D = x.shape
    N = b.shape[1]
    so = (N // ok[1]) if ok is not None else None
    tm = _largest_tile(M, [], 1024)
    tn = _largest_tile(N, [so], MM_TN_CAP)
    segs_per_block = 1
    if ok is not None:
        segs_per_block = max(c for c in range(1, ok[1] + 1) if ok[1] % c == 0 and (c == 1 or c * so <= MM_TN_CAP))
    if ok is None:
        out_shape = jax.ShapeDtypeStruct((M, N), out_dtype)
        o_spec = pl.BlockSpec((tm, tn), lambda i, j: (i, j))
    elif segs_per_block > 1:
        tn = segs_per_block * so
        out_shape = jax.ShapeDtypeStruct((ok[1], M, so), out_dtype)
        o_spec = pl.BlockSpec((segs_per_block, tm, so), lambda i, j: (j, i, 0))
    else:
        out_shape = jax.ShapeDtypeStruct((ok[1], M, N // ok[1]), out_dtype)
        per = (N // ok[1]) // tn
        o_spec = pl.BlockSpec((None, tm, tn), lambda i, j: (j // per, i, j % per))

    def body(x_ref, g_ref, b_ref, o_ref, ht_ref, h_scr):
        @pl.when(pl.program_id(1) == 0)
        def _():
            for c in range(tm // NORM_ROWS):
                rows = pl.ds(c * NORM_ROWS, NORM_ROWS)
                xv = x_ref[rows, :]
                h = xv * lax.rsqrt(jnp.mean(xv * xv, axis=-1, keepdims=True) + EPS) * g_ref[...]
                h_scr[rows, :] = h.astype(BF16)
                ht_ref[:, rows] = h.T.astype(BF16)

        prod = jnp.dot(h_scr[...], b_ref[...], preferred_element_type=F32).astype(out_dtype)
        if segs_per_block > 1:
            for s in range(segs_per_block):
                o_ref[s] = prod[:, s * so:(s + 1) * so]
        else:
            o_ref[...] = prod

    return pl.pallas_call(
        body, name=name, out_shape=(out_shape, jax.ShapeDtypeStruct((D, M), BF16)), grid=(M // tm, N // tn),
        in_specs=[pl.BlockSpec((tm, D), lambda i, j: (i, 0)), pl.BlockSpec((1, D), lambda i, j: (0, 0)),
                  pl.BlockSpec((D, tn), lambda i, j: (0, j))],
        out_specs=(o_spec, pl.BlockSpec((D, tm), lambda i, j: (0, i))),
        scratch_shapes=[pltpu.VMEM((tm, D), BF16)], compiler_params=_cparams(("parallel", "arbitrary")),
    )(x, g.reshape(1, D), b)


def _mm_norm_bwd(a, b, x, g, dres, name, ak=None, dep=None):
    M, K = _ldims(a, ak)
    D = b.shape[0]
    assert b.shape[1] == K and x.shape == (M, D)
    sa = _segw(a, ak)
    whole_segs = bool(sa) and K <= MM_TK_WHOLE
    tk = K if (K <= MM_TK_WHOLE) else _largest_tile(K, [sa], MM_TK_WHOLE)
    tm = _largest_tile(M, [], 1024 if (tk == K or tk <= MM_TK_CAP) else 512)
    ni, nk = M // tm, K // tk
    if whole_segs:
        a_spec = pl.BlockSpec((a.shape[0], tm, sa), lambda i, k: (0, i, 0))
    else:
        a3 = _opspec(a, ak, tm, tk, lambda i, j, k: i, lambda i, j, k: k)
        a_spec = pl.BlockSpec(a3.block_shape, lambda i, k: a3.index_map(i, 0, k))
    n_in = 5 + (dep is not None)

    def body(*refs):
        a_ref, b_ref, x_ref, g_ref, r_ref = refs[:5]
        dx_ref, dxb_ref, dg_ref, acc, accg = refs[n_in:]
        i, k = pl.program_id(0), pl.program_id(1)
        av = jnp.concatenate([a_ref[s] for s in range(a.shape[0])], axis=1) if whole_segs else a_ref[...]
        prod = lax.dot_general(av.astype(BF16), b_ref[...], (((1,), (1,)), ((), ())), preferred_element_type=F32)

        @pl.when(k == 0)
        def _():
            acc[...] = prod

        @pl.when(k > 0)
        def _():
            acc[...] += prod

        @pl.when((i == 0) & (k == 0))
        def _():
            accg[...] = jnp.zeros_like(accg)

        @pl.when(k == nk - 1)
        def _():
            for c in range(tm // NORM_ROWS):
                rows = pl.ds(c * NORM_ROWS, NORM_ROWS)
                xv = x_ref[rows, :]
                r = lax.rsqrt(jnp.mean(xv * xv, axis=-1, keepdims=True) + EPS)
                xh = xv * r
                dhv = acc[rows, :]
                accg[...] += jnp.sum((dhv * xh).reshape(NORM_ROWS // SUBLANES, SUBLANES, D), axis=0)
                dxh = dhv * g_ref[...]
                dxv = r_ref[rows, :] + r * (dxh - xh * jnp.mean(dxh * xh, axis=-1, keepdims=True))
                dx_ref[rows, :] = dxv
                dxb_ref[rows, :] = dxv.astype(BF16)

        @pl.when((i == ni - 1) & (k == nk - 1))
        def _():
            dg_ref[...] = jnp.sum(accg[...], axis=0, keepdims=True)

    row = pl.BlockSpec((tm, D), lambda i, k: (i, 0))
    vec = pl.BlockSpec((1, D), lambda i, k: (0, 0))
    in_specs = [a_spec, pl.BlockSpec((D, tk), lambda i, k: (0, k)), row, vec, row]
    args = [a, b, x, g.reshape(1, D), dres]
    if dep is not None:
        in_specs.append(pl.BlockSpec(memory_space=pl.ANY))
        args.append(dep)
    return pl.pallas_call(
        body, name=name,
        out_shape=(jax.ShapeDtypeStruct((M, D), F32), jax.ShapeDtypeStruct((M, D), BF16),
                   jax.ShapeDtypeStruct((1, D), F32)),
        grid=(ni, nk), in_specs=in_specs, out_specs=(row, row, vec),
        scratch_shapes=[pltpu.VMEM((tm, D), F32), pltpu.VMEM((SUBLANES, D), F32)],
        compiler_params=_cparams(("arbitrary", "arbitrary"), VMEM_LIMIT_S5),
    )(*args)


def _loss_head(x, g, tgt):
    L, D = x.shape
    tr = _pick(L, prefs=(512, 256, 128))
    nsteps = L // tr

    def body(x_ref, g_ref, t_ref, loss_ref, dx_ref, dxb_ref, dg_ref, acc_g, acc_l):
        i = pl.program_id(0)

        @pl.when(i == 0)
        def _():
            acc_g[...] = jnp.zeros_like(acc_g)
            acc_l[...] = jnp.zeros_like(acc_l)

        xv = x_ref[...]
        gv = g_ref[...]
        r = lax.rsqrt(jnp.mean(xv * xv, axis=-1, keepdims=True) + EPS)
        xh = xv * r
        e = xh * gv - t_ref[...]
        acc_l[...] += jnp.sum((e * e).reshape(tr // SUBLANES, SUBLANES, D), axis=0)
        dy = e * (1.0 / D)
        acc_g[...] += jnp.sum((dy * xh).reshape(tr // SUBLANES, SUBLANES, D), axis=0)
        dxh = dy * gv
        dxv = r * (dxh - xh * jnp.mean(dxh * xh, axis=-1, keepdims=True))
        dx_ref[...] = dxv
        dxb_ref[...] = dxv.astype(BF16)

        @pl.when(i == nsteps - 1)
        def _():
            dg_ref[...] = jnp.sum(acc_g[...], axis=0, keepdims=True)
            tot = jnp.sum(jnp.sum(acc_l[...], axis=0, keepdims=True), axis=1, keepdims=True) * (0.5 / D)
            loss_ref[...] = jnp.broadcast_to(tot, (SUBLANES, LANES))

    row = pl.BlockSpec((tr, D), lambda i: (i, 0))
    vec = pl.BlockSpec((1, D), lambda i: (0, 0))
    return pl.pallas_call(
        body, name="loss_head",
        out_shape=(jax.ShapeDtypeStruct((SUBLANES, LANES), F32), jax.ShapeDtypeStruct((L, D), F32),
                   jax.ShapeDtypeStruct((L, D), BF16), jax.ShapeDtypeStruct((1, D), F32)),
        grid=(nsteps,), in_specs=[row, vec, row],
        out_specs=(pl.BlockSpec((SUBLANES, LANES), lambda i: (0, 0)), row, row, vec),
        scratch_shapes=[pltpu.VMEM((SUBLANES, D), F32), pltpu.VMEM((SUBLANES, D), F32)],
        compiler_params=_cparams(("arbitrary",)),
    )(x, g.reshape(1, D), tgt)


def _sconv_fwd(proj4, conv_w, name):
    _, L, C = proj4.shape
    cb = LANES

    def body(p_ref, w_ref, o_ref):
        xa, ba, ca = p_ref[0].astype(F32), p_ref[1].astype(F32), p_ref[2].astype(F32)
        o_ref[...] = (ba * _conv3(_taps(ca * xa), w_ref[...])).astype(BF16)

    return pl.pallas_call(
        body, name=name, out_shape=jax.ShapeDtypeStruct((L, 2 * C), BF16), grid=(C // cb,),
        in_specs=[pl.BlockSpec((3, L, cb), lambda j: (0, 0, j)), pl.BlockSpec((3, cb), lambda j: (0, j))],
        out_specs=pl.BlockSpec((L, cb), lambda j: (0, j)), compiler_params=_cparams(("parallel",)),
    )(proj4, conv_w)


def _sconv_bwd(proj4, dmix, conv_w, name):
    _, L, C = proj4.shape
    cb = LANES

    def body(p_ref, d_ref, w_ref, o_ref, dw_ref):
        xa, ba, ca = p_ref[0].astype(F32), p_ref[1].astype(F32), p_ref[2].astype(F32)
        w = w_ref[...]
        dya = d_ref[...]
        tq = _taps(ca * xa)
        cq = _conv3(tq, w)
        dcq = dya * ba
        dq = _conv3_t(dcq, w)
        for tap, dwt in enumerate(_conv3_dw(dcq, tq)):
            dw_ref[tap:tap + 1, :] = dwt
        o_ref[0] = (dq * ca).astype(BF16)
        o_ref[1] = (dya * cq).astype(BF16)
        o_ref[2] = (dq * xa).astype(BF16)

    return pl.pallas_call(
        body, name=name,
        out_shape=(jax.ShapeDtypeStruct((4, L, C), BF16), jax.ShapeDtypeStruct((3, C), F32)), grid=(C // cb,),
        in_specs=[pl.BlockSpec((3, L, cb), lambda j: (0, 0, j)), pl.BlockSpec((L, cb), lambda j: (0, j)),
                  pl.BlockSpec((3, cb), lambda j: (0, j))],
        out_specs=(pl.BlockSpec((3, L, cb), lambda j: (0, 0, j)), pl.BlockSpec((3, cb), lambda j: (0, j))),
        compiler_params=_cparams(("parallel",)),
    )(proj4, dmix, conv_w)


def _s5_prep(log_step, a_re, a_im, b_re, b_im, c_re, c_im):
    G, P = a_re.shape
    H = b_re.shape[-1]
    gs = S5_GROUPS_PER_STEP
    ns = G // gs
    gu = LANES // H
    lam = lax.complex(a_re, a_im)
    step = jnp.exp(log_step)[:, None]
    lam_bar = jnp.exp(lam * step)
    b_bar = ((lam_bar - 1.0) / lam)[..., None] * lax.complex(b_re, b_im)
    lr = jnp.real(lam_bar).reshape(ns, 1, gs * P)
    li = jnp.imag(lam_bar).reshape(ns, 1, gs * P)
    k = np.arange(ns)[:, None, None]
    oh = jnp.asarray((np.arange(gu)[None, :, None] == gs * (k % (gu // gs)) + np.arange(gs)[None, None, :]),
                     F32)
    bre = jnp.einsum('kgl,klph->kghlp', oh, jnp.real(b_bar).reshape(ns, gs, P, H)).reshape(ns, gu * H, gs * P)
    bim = jnp.einsum('kgl,klph->kghlp', oh, jnp.imag(b_bar).reshape(ns, gs, P, H)).reshape(ns, gu * H, gs * P)
    cre = jnp.einsum('kgl,klhp->klpgh', oh, c_re.reshape(ns, gs, H, P)).reshape(ns, gs * P, gu * H)
    cim = jnp.einsum('kgl,klhp->klpgh', oh, c_im.reshape(ns, gs, H, P)).reshape(ns, gs * P, gu * H)
    return lr, li, jnp.concatenate([bre, bim], axis=2), jnp.concatenate([cre, -cim], axis=1)


def _carry_tile(fr, fi, pr, pi, reverse):
    row = lax.broadcasted_iota(jnp.int32, fr.shape, 0)
    cr = jnp.zeros_like(fr)
    ci = jnp.zeros_like(fi)
    sr = jnp.zeros_like(fr[0:1])
    si = jnp.zeros_like(sr)
    order = range(SCAN_CHUNKS - 1, 0, -1) if reverse else range(0, SCAN_CHUNKS - 1)
    for c in order:
        fcr = jnp.sum(jnp.where(row == c, fr, 0.0), axis=0, keepdims=True)
        fci = jnp.sum(jnp.where(row == c, fi, 0.0), axis=0, keepdims=True)
        mr, mi = _cmul(pr, pi, sr, si)
        sr, si = mr + fcr, mi + fci
        nxt = c - 1 if reverse else c + 1
        cr = jnp.where(row == nxt, sr, cr)
        ci = jnp.where(row == nxt, si, ci)
    return cr, ci


def _scan_order_into(dst_ref, src_ref, T):
    for c in range(SCAN_CHUNKS):
        dst_ref[pl.ds(c, T, stride=SCAN_CHUNKS), :] = src_ref[pl.ds(c * T, T), :].astype(F32)


def _s5_fwd(proj4, lr, li, bmat, cmat, d, name):
    _, L, Du = proj4.shape
    ns, _, W2 = bmat.shape
    W = W2 // 2
    T = L // SCAN_CHUNKS
    rb = _pick(L, prefs=(512, 256, 128))
    per = (ns * LANES) // Du

    def body(ut_ref, lr_ref, li_ref, b_ref, c_ref, d_ref, y_ref, sr_ref, si_ref, u_ref):
        k = pl.program_id(0)
        _scan_order_into(u_ref, ut_ref, T)
        for r in range(L // rb):
            rows = pl.ds(r * rb, rb)
            bu = jnp.dot(u_ref[rows, :].astype(BF16), b_ref[...], preferred_element_type=F32)
            sr_ref[rows, :] = bu[:, :W]
            si_ref[rows, :] = bu[:, W:]
        lam_r = jnp.broadcast_to(lr_ref[...], (SUBLANES, W))
        lam_i = jnp.broadcast_to(li_ref[...], (SUBLANES, W))

        def local(t, carry):
            sr, si = carry
            rows = pl.ds(pl.multiple_of(t * SUBLANES, SUBLANES), SUBLANES)
            mr, mi = _cmul(lam_r, lam_i, sr, si)
            sr = mr + sr_ref[rows, :]
            si = mi + si_ref[rows, :]
            sr_ref[rows, :] = sr
            si_ref[rows, :] = si
            return sr, si

        z = jnp.zeros((SUBLANES, W), F32)
        fr, fi = lax.fori_loop(0, T, local, (z, z))
        pr, pi = _cpow(lam_r, lam_i, T)
        cr, ci = _carry_tile(fr, fi, pr[0:1], pi[0:1], reverse=False)

        def fix(t, carry):
            wr, wi = carry
            rows = pl.ds(pl.multiple_of(t * SUBLANES, SUBLANES), SUBLANES)
            ar, ai = _cmul(wr, wi, cr, ci)
            sr_ref[rows, :] += ar
            si_ref[rows, :] += ai
            return _cmul(wr, wi, lam_r, lam_i)

        lax.fori_loop(0, T, fix, (lam_r, lam_i))
        first = (k % per) == 0
        for r in range(L // rb):
            rows = pl.ds(r * rb, rb)
            s = jnp.concatenate([sr_ref[rows, :], si_ref[rows, :]], axis=1).astype(BF16)
            y = jnp.dot(s, c_ref[...], preferred_element_type=F32)

            @pl.when(first)
            def _():
                y_ref[rows, :] = y + d_ref[...] * u_ref[rows, :]

            @pl.when(jnp.logical_not(first))
            def _():
                y_ref[rows, :] += y

    ublk = pl.BlockSpec((L, LANES), lambda k: (0, k // per))
    sblk = pl.BlockSpec((L, W), lambda k: (0, k))
    lam = pl.BlockSpec((None, 1, W), lambda k: (k, 0, 0))
    return pl.pallas_call(
        body, name=name,
        out_shape=(jax.ShapeDtypeStruct((L, Du), F32), jax.ShapeDtypeStruct((L, ns * W), F32),
                   jax.ShapeDtypeStruct((L, ns * W), F32)),
        grid=(ns,),
        in_specs=[pl.BlockSpec((None, L, LANES), lambda k: (3, 0, k // per)), lam, lam,
                  pl.BlockSpec((None, LANES, 2 * W), lambda k: (k, 0, 0)),
                  pl.BlockSpec((None, 2 * W, LANES), lambda k: (k, 0, 0)),
                  pl.BlockSpec((1, LANES), lambda k: (0, k // per))],
        out_specs=(ublk, sblk, sblk), scratch_shapes=[pltpu.VMEM((L, LANES), F32)],
        compiler_params=_cparams(("arbitrary",), VMEM_LIMIT_S5),
    )(proj4, lr, li, bmat.astype(BF16), cmat.astype(BF16), d.reshape(1, Du))


def _s5_bwd(dy, proj4, dproj, s_re, s_im, lr, li, bmat, cmat, d, name):
    _, L, Du = proj4.shape
    ns, _, W2 = bmat.shape
    W = W2 // 2
    T = L // SCAN_CHUNKS
    rb = _pick(L, prefs=(512, 256, 128))
    per = (ns * LANES) // Du
    NT = (((1,), (1,)), ((), ()))
    TN = (((0,), (0,)), ((), ()))

    def body(dy_ref, ut_ref, dp_in, sr_ref, si_ref, lr_ref, li_ref, b_ref, c_ref, d_ref,
             dut_ref, db_ref, dc_ref, dl_ref, dd_ref, gr_ref, gi_ref, u_ref, du_ref):
        k = pl.program_id(0)
        _scan_order_into(u_ref, ut_ref, T)
        for r in range(L // rb):
            rows = pl.ds(r * rb, rb)
            g = lax.dot_general(dy_ref[rows, :].astype(BF16), c_ref[...], NT, preferred_element_type=F32)
            gr_ref[rows, :] = g[:, :W]
            gi_ref[rows, :] = g[:, W:]
        lam_r = jnp.broadcast_to(lr_ref[...], (SUBLANES, W))
        lam_i = -jnp.broadcast_to(li_ref[...], (SUBLANES, W))

        def local(i, carry):
            gr, gi = carry
            rows = pl.ds(pl.multiple_of((T - 1 - i) * SUBLANES, SUBLANES), SUBLANES)
            mr, mi = _cmul(lam_r, lam_i, gr, gi)
            gr = mr + gr_ref[rows, :]
            gi = mi + gi_ref[rows, :]
            gr_ref[rows, :] = gr
            gi_ref[rows, :] = gi
            return gr, gi

        z = jnp.zeros((SUBLANES, W), F32)
        fr, fi = lax.fori_loop(0, T, local, (z, z))
        pr, pi = _cpow(lam_r, lam_i, T)
        cr, ci = _carry_tile(fr, fi, pr[0:1], pi[0:1], reverse=True)

        def true_g(rows, wr, wi):
            ar, ai = _cmul(wr, wi, cr, ci)
            gr = gr_ref[rows, :] + ar
            gi = gi_ref[rows, :] + ai
            gr_ref[rows, :] = gr
            gi_ref[rows, :] = gi
            return gr, gi

        def fix(i, carry):
            wr, wi, ar_, ai_ = carry
            t = T - 1 - i
            rows = pl.ds(pl.multiple_of(t * SUBLANES, SUBLANES), SUBLANES)
            prev = pl.ds(pl.multiple_of((t - 1) * SUBLANES, SUBLANES), SUBLANES)
            gr, gi = true_g(rows, wr, wi)
            qr, qi = sr_ref[prev, :], si_ref[prev, :]
            ar_ = ar_ + gr * qr + gi * qi
            ai_ = ai_ + gi * qr - gr * qi
            wr, wi = _cmul(wr, wi, lam_r, lam_i)
            return wr, wi, ar_, ai_

        wr, wi, acc_r, acc_i = lax.fori_loop(0, T - 1, fix, (lam_r, lam_i, z, z))
        gr, gi = true_g(pl.ds(0, SUBLANES), wr, wi)
        last = pl.ds((T - 1) * SUBLANES, SUBLANES)
        row = lax.broadcasted_iota(jnp.int32, (SUBLANES, W), 0)
        qr = jnp.where(row >= 1, pltpu.roll(sr_ref[last, :], 1, axis=0), 0.0)
        qi = jnp.where(row >= 1, pltpu.roll(si_ref[last, :], 1, axis=0), 0.0)
        acc_r = acc_r + gr * qr + gi * qi
        acc_i = acc_i + gi * qr - gr * qi
        dl_ref[0:1, :] = jnp.sum(acc_r, axis=0, keepdims=True)
        dl_ref[1:2, :] = jnp.sum(acc_i, axis=0, keepdims=True)

        first = (k % per) == 0
        db = jnp.zeros((LANES, 2 * W), F32)
        dc = jnp.zeros((LANES, 2 * W), F32)
        dd = jnp.zeros((1, LANES), F32)
        for r in range(L // rb):
            rows = pl.ds(r * rb, rb)
            gb = jnp.concatenate([gr_ref[rows, :], gi_ref[rows, :]], axis=1).astype(BF16)
            sb = jnp.concatenate([sr_ref[rows, :], si_ref[rows, :]], axis=1).astype(BF16)
            dyv = dy_ref[rows, :]
            uv = u_ref[rows, :]
            du = lax.dot_general(gb, b_ref[...], NT, preferred_element_type=F32)
            db = db + lax.dot_general(uv.astype(BF16), gb, TN, preferred_element_type=F32)
            dc = dc + lax.dot_general(dyv.astype(BF16), sb, TN, preferred_element_type=F32)
            dd = dd + jnp.sum(dyv * uv, axis=0, keepdims=True)

            @pl.when(first)
            def _():
                du_ref[rows, :] = du + d_ref[...] * dyv

            @pl.when(jnp.logical_not(first))
            def _():
                du_ref[rows, :] += du

        db_ref[...] = db
        dc_ref[...] = dc

        @pl.when(first)
        def _():
            dd_ref[...] = dd

        @pl.when((k % per) == per - 1)
        def _():
            for c in range(SCAN_CHUNKS):
                dut_ref[pl.ds(c * T, T), :] = du_ref[pl.ds(c, T, stride=SCAN_CHUNKS), :].astype(BF16)

    ublk = pl.BlockSpec((L, LANES), lambda k: (0, k // per))
    uslab = pl.BlockSpec((None, L, LANES), lambda k: (3, 0, k // per))
    sblk = pl.BlockSpec((L, W), lambda k: (0, k))
    lam = pl.BlockSpec((None, 1, W), lambda k: (k, 0, 0))
    vec = pl.BlockSpec((1, LANES), lambda k: (0, k // per))
    mat = pl.BlockSpec((None, LANES, 2 * W), lambda k: (k, 0, 0))
    return pl.pallas_call(
        body, name=name,
        out_shape=(jax.ShapeDtypeStruct(dproj.shape, dproj.dtype), jax.ShapeDtypeStruct((ns, LANES, 2 * W), F32),
                   jax.ShapeDtypeStruct((ns, LANES, 2 * W), F32), jax.ShapeDtypeStruct((ns, 2, W), F32),
                   jax.ShapeDtypeStruct((1, Du), F32)),
        grid=(ns,),
        in_specs=[ublk, uslab, pl.BlockSpec(memory_space=pl.ANY), sblk, sblk, lam, lam, mat,
                  pl.BlockSpec((None, 2 * W, LANES), lambda k: (k, 0, 0)), vec],
        out_specs=(uslab, mat, mat, pl.BlockSpec((None, 2, W), lambda k: (k, 0, 0)), vec),
        scratch_shapes=[pltpu.VMEM((L, W), F32), pltpu.VMEM((L, W), F32), pltpu.VMEM((L, LANES), F32),
                        pltpu.VMEM((L, LANES), F32)],
        input_output_aliases={2: 0}, compiler_params=_cparams(("arbitrary",), VMEM_LIMIT_S5),
    )(dy, proj4, dproj, s_re, s_im, lr, li, bmat.astype(BF16), cmat.astype(BF16), d.reshape(1, Du))


def _glu_fwd(yraw, wmat, bias, mixin, name):
    L, C = yraw.shape
    tr = _pick(L, prefs=(512, 256, 128))
    tb = tr // SCAN_CHUNKS
    nl = C // LANES

    def body(y_ref, w_ref, b_ref, m_in, o_ref, scr):
        yg = _gelu(y_ref[...])
        zz = jnp.dot(yg.astype(BF16), w_ref[...], preferred_element_type=F32) + b_ref[...]
        yb = yg * _sigmoid(zz)
        for k in range(nl):
            scr[k] = yb[:, k * LANES:(k + 1) * LANES]
        for c in range(SCAN_CHUNKS):
            for k in range(nl):
                o_ref[c, :, k * LANES:(k + 1) * LANES] = scr[k, pl.ds(c, tb, stride=SCAN_CHUNKS), :].astype(BF16)

    out = pl.pallas_call(
        body, name=name, out_shape=jax.ShapeDtypeStruct((SCAN_CHUNKS, L // SCAN_CHUNKS, 2 * C), BF16),
        grid=(L // tr,),
        in_specs=[pl.BlockSpec((tr, C), lambda i: (i, 0)), pl.BlockSpec((C, C), lambda i: (0, 0)),
                  pl.BlockSpec((1, C), lambda i: (0, 0)), pl.BlockSpec(memory_space=pl.ANY)],
        out_specs=pl.BlockSpec((SCAN_CHUNKS, tb, C), lambda i: (0, i, 1)),
        scratch_shapes=[pltpu.VMEM((nl, tr, LANES), F32)], input_output_aliases={3: 0},
        compiler_params=_cparams(("parallel",)),
    )(yraw, wmat, bias.reshape(1, C), mixin.reshape(SCAN_CHUNKS, L // SCAN_CHUNKS, 2 * C))
    return out.reshape(L, 2 * C)


def _glu_bwd(yraw, dmix, wmat, bias, name):
    L, C = yraw.shape
    tr = _pick(L, prefs=(512, 256, 128))
    nsteps = L // tr
    tb = tr // SCAN_CHUNKS
    nl = C // LANES

    def body(y_ref, d_ref, w_ref, b_ref, dy_ref, dw_ref, db_ref, acc_b, scr):
        i = pl.program_id(0)

        @pl.when(i == 0)
        def _():
            dw_ref[...] = jnp.zeros_like(dw_ref)
            acc_b[...] = jnp.zeros_like(acc_b)

        for c in range(SCAN_CHUNKS):
            for k in range(nl):
                scr[k, pl.ds(c, tb, stride=SCAN_CHUNKS), :] = d_ref[c, :, k * LANES:(k + 1) * LANES]
        yr = y_ref[...]
        yg = _gelu(yr)
        ygb = yg.astype(BF16)
        sg = _sigmoid(jnp.dot(ygb, w_ref[...], preferred_element_type=F32) + b_ref[...])
        dyb_ = jnp.concatenate([scr[k] for k in range(nl)], axis=1)
        dz = dyb_ * yg * sg * (1.0 - sg)
        dzb = dz.astype(BF16)
        dyg = dyb_ * sg + lax.dot_general(dzb, w_ref[...], (((1,), (1,)), ((), ())), preferred_element_type=F32)
        dw_ref[...] += lax.dot_general(ygb, dzb, (((0,), (0,)), ((), ())), preferred_element_type=F32)
        acc_b[...] += jnp.sum(dz.reshape(tr // SUBLANES, SUBLANES, C), axis=0)
        dy_ref[...] = dyg * _gelu_grad(yr)

        @pl.when(i == nsteps - 1)
        def _():
            db_ref[...] = jnp.sum(acc_b[...], axis=0, keepdims=True)

    row = pl.BlockSpec((tr, C), lambda i: (i, 0))
    return pl.pallas_call(
        body, name=name,
        out_shape=(jax.ShapeDtypeStruct((L, C), F32), jax.ShapeDtypeStruct((C, C), F32),
                   jax.ShapeDtypeStruct((1, C), F32)),
        grid=(nsteps,),
        in_specs=[row, pl.BlockSpec((SCAN_CHUNKS, tb, C), lambda i: (0, i, 1)), pl.BlockSpec((C, C), lambda i: (0, 0)),
                  pl.BlockSpec((1, C), lambda i: (0, 0))],
        out_specs=(row, pl.BlockSpec((C, C), lambda i: (0, 0)), pl.BlockSpec((1, C), lambda i: (0, 0))),
        scratch_shapes=[pltpu.VMEM((SUBLANES, C), F32), pltpu.VMEM((nl, tr, LANES), F32)],
        compiler_params=_cparams(("arbitrary",)),
    )(yraw, dmix.reshape(SCAN_CHUNKS, L // SCAN_CHUNKS, 2 * C), wmat, bias.reshape(1, C))


def _pool_counts(L, g):
    t = lax.broadcasted_iota(jnp.int32, (L, LANES), 0).astype(F32) + 1.0
    w = jnp.where(g == 0, 2.0, jnp.where(g == 1, 4.0, jnp.where(g == 2, 8.0, 16.0)))
    return 1.0 / jnp.minimum(t, w)


def _select_window(g, a2, a4, a8, a16):
    return jnp.where(g == 0, a2, jnp.where(g == 1, a4, jnp.where(g == 2, a8, a16)))


def _pooled(z, g):
    a2 = z + _down(z, 1)
    a4 = a2 + _down(a2, 2)
    a8 = a4 + _down(a4, 4)
    a16 = a8 + _down(a8, 8)
    return _select_window(g, a2, a4, a8, a16) * _pool_counts(z.shape[0], g) - z


def _transpose_on_mxu(yb):
    c = yb.shape[1]
    eye = lax.broadcasted_iota(jnp.int32, (c, c), 0) == lax.broadcasted_iota(jnp.int32, (c, c), 1)
    return lax.dot_general(eye.astype(BF16), yb, (((1,), (1,)), ((), ())), preferred_element_type=F32).astype(BF16)


def _pool_fwd(proj3, pool_w, scale, name):
    _, L, C = proj3.shape
    ng = len(POOL_WINDOWS)
    pg = C // ng
    assert pg == LANES

    def body(z_ref, w_ref, s_ref, o_ref, ot_ref):
        g = pl.program_id(0)
        p = _pooled(z_ref[...].astype(F32), g)
        y = jnp.dot(p.astype(BF16), w_ref[...].astype(BF16), preferred_element_type=F32)
        yb = (y * s_ref[...]).astype(BF16)
        o_ref[...] = yb
        ot_ref[...] = _transpose_on_mxu(yb)

    return pl.pallas_call(
        body, name=name, out_shape=(jax.ShapeDtypeStruct((L, 2 * C), BF16), jax.ShapeDtypeStruct((2 * C, L), BF16)),
        grid=(ng,),
        in_specs=[pl.BlockSpec((None, L, pg), lambda g: (0, 0, g)), pl.BlockSpec((None, pg, pg), lambda g: (g, 0, 0)),
                  pl.BlockSpec((1, pg), lambda g: (0, g))],
        out_specs=(pl.BlockSpec((L, pg), lambda g: (0, g)), pl.BlockSpec((pg, L), lambda g: (g, 0))),
        compiler_params=_cparams(("parallel",)),
    )(proj3, pool_w, scale.reshape(1, C))


def _pool_bwd(proj3, dmix, pool_w, scale, name):
    _, L, C = proj3.shape
    ng = len(POOL_WINDOWS)
    pg = C // ng

    def body(z_ref, d_ref, w_ref, s_ref, dz_ref, dw_ref, ds_ref):
        g = pl.program_id(0)
        p = _pooled(z_ref[...].astype(F32), g)
        pb = p.astype(BF16)
        wb = w_ref[...].astype(BF16)
        pre = jnp.dot(pb, wb, preferred_element_type=F32)
        dyc = d_ref[...]
        ds_ref[...] = jnp.sum(dyc * pre, axis=0, keepdims=True)
        dpre = (dyc * s_ref[...]).astype(BF16)
        dw_ref[...] = lax.dot_general(pb, dpre, (((0,), (0,)), ((), ())), preferred_element_type=F32)
        dp = lax.dot_general(dpre, wb, (((1,), (1,)), ((), ())), preferred_element_type=F32)
        v = dp * _pool_counts(L, g)
        a2 = v + _up(v, 1)
        a4 = a2 + _up(a2, 2)
        a8 = a4 + _up(a4, 4)
        a16 = a8 + _up(a8, 8)
        dz_ref[...] = (_select_window(g, a2, a4, a8, a16) - dp).astype(BF16)

    return pl.pallas_call(
        body, name=name,
        out_shape=(jax.ShapeDtypeStruct((L, C), BF16), jax.ShapeDtypeStruct((ng, pg, pg), F32),
                   jax.ShapeDtypeStruct((1, C), F32)),
        grid=(ng,),
        in_specs=[pl.BlockSpec((None, L, pg), lambda g: (0, 0, g)), pl.BlockSpec((L, pg), lambda g: (0, g)),
                  pl.BlockSpec((None, pg, pg), lambda g: (g, 0, 0)), pl.BlockSpec((1, pg), lambda g: (0, g))],
        out_specs=(pl.BlockSpec((L, pg), lambda g: (0, g)), pl.BlockSpec((None, pg, pg), lambda g: (g, 0, 0)),
                   pl.BlockSpec((1, pg), lambda g: (0, g))),
        compiler_params=_cparams(("parallel",)),
    )(proj3, dmix, pool_w, scale.reshape(1, C))


def _tril_w(w_ref, h):
    r = lax.broadcasted_iota(jnp.int32, (CHUNK, CHUNK), 0)
    c = lax.broadcasted_iota(jnp.int32, (CHUNK, CHUNK), 1)
    return jnp.where(r >= c, w_ref[h], 0.0)


def _sgu_fwd(proj3, norm_g, w, b, mixin, mixin_t, name):
    _, L, C = proj3.shape
    nh = w.shape[0]
    dh = C // nh
    assert dh == LANES and w.shape[1] == CHUNK
    tr = _pick(L, prefs=(512, 256, 128))
    bfull = jnp.broadcast_to(b[:, :, None], (nh, CHUNK, dh))

    def body(su_ref, sv_ref, g_ref, w_ref, b_ref, m_in, mt_in, o_ref, ot_ref):
        sv = _gelu(sv_ref[...].astype(F32))
        r = lax.rsqrt(jnp.mean(sv * sv, axis=-1, keepdims=True) + EPS)
        v = (sv * r * g_ref[...]).astype(BF16)
        for h in range(nh):
            wm = _tril_w(w_ref, h).astype(BF16)
            cols = slice(h * dh, (h + 1) * dh)
            for n in range(tr // CHUNK):
                rows = slice(n * CHUNK, (n + 1) * CHUNK)
                mixed = jnp.dot(wm, v[rows, cols], preferred_element_type=F32) + b_ref[h]
                o_ref[rows, cols] = (_gelu(su_ref[rows, cols].astype(F32)) * mixed).astype(BF16)
        ot_ref[...] = _transpose_on_mxu(o_ref[...])

    full = lambda shp: pl.BlockSpec(shp, lambda i: (0,) * len(shp))
    anywhere = pl.BlockSpec(memory_space=pl.ANY)
    return pl.pallas_call(
        body, name=name, out_shape=(jax.ShapeDtypeStruct(mixin.shape, BF16), jax.ShapeDtypeStruct(mixin_t.shape, BF16)),
        grid=(L // tr,),
        in_specs=[pl.BlockSpec((None, tr, C), lambda i: (1, i, 0)), pl.BlockSpec((None, tr, C), lambda i: (2, i, 0)),
                  full((1, C)), full((nh, CHUNK, CHUNK)), full((nh, CHUNK, dh)), anywhere, anywhere],
        out_specs=(pl.BlockSpec((tr, C), lambda i: (i, 1)), pl.BlockSpec((C, tr), lambda i: (1, i))),
        input_output_aliases={5: 0, 6: 1}, compiler_params=_cparams(("parallel",)),
    )(proj3, proj3, norm_g.reshape(1, C), w, bfull, mixin, mixin_t)


def _sgu_bwd(proj3, dmix, dz, norm_g, w, b, name):
    _, L, C = proj3.shape
    nh = w.shape[0]
    dh = C // nh
    tr = _pick(L, prefs=(512, 256, 128))
    nsteps = L // tr
    bfull = jnp.broadcast_to(b[:, :, None], (nh, CHUNK, dh))

    def body(su_ref, sv_ref, d_ref, dz_ref, g_ref, w_ref, b_ref, o_ref, dw_ref, db_ref, dg_ref, dv_ref, acc_g):
        i = pl.program_id(0)
        o_ref[0] = dz_ref[...]

        @pl.when(i == 0)
        def _():
            dw_ref[...] = jnp.zeros_like(dw_ref)
            db_ref[...] = jnp.zeros_like(db_ref)
            acc_g[...] = jnp.zeros_like(acc_g)

        svp = sv_ref[...].astype(F32)
        sv = _gelu(svp)
        r = lax.rsqrt(jnp.mean(sv * sv, axis=-1, keepdims=True) + EPS)
        vh = sv * r
        gv = g_ref[...]
        v = (vh * gv).astype(BF16)
        tri_r = lax.broadcasted_iota(jnp.int32, (CHUNK, CHUNK), 0)
        tri_c = lax.broadcasted_iota(jnp.int32, (CHUNK, CHUNK), 1)
        for h in range(nh):
            wm = _tril_w(w_ref, h).astype(BF16)
            cols = slice(h * dh, (h + 1) * dh)
            dwh = jnp.zeros((CHUNK, CHUNK), F32)
            dbh = jnp.zeros((CHUNK, dh), F32)
            for n in range(tr // CHUNK):
                rows = slice(n * CHUNK, (n + 1) * CHUNK)
                vb = v[rows, cols]
                mixed = jnp.dot(wm, vb, preferred_element_type=F32) + b_ref[h]
                sup = su_ref[rows, cols].astype(F32)
                dyd = d_ref[rows, cols]
                dmx = dyd * _gelu(sup)
                o_ref[1, rows, cols] = (dyd * mixed * _gelu_grad(sup)).astype(BF16)
                dmb = dmx.astype(BF16)
                dwh = dwh + lax.dot_general(dmb, vb, (((1,), (1,)), ((), ())), preferred_element_type=F32)
                dbh = dbh + dmx
                dv_ref[rows, cols] = lax.dot_general(wm, dmb, (((0,), (0,)), ((), ())), preferred_element_type=F32)
            dw_ref[h] += jnp.where(tri_r >= tri_c, dwh, 0.0)
            db_ref[h] += dbh
        dv = dv_ref[...]
        acc_g[...] += jnp.sum((dv * vh).reshape(tr // SUBLANES, SUBLANES, C), axis=0)
        dvg = dv * gv
        dsv = r * (dvg - vh * jnp.mean(dvg * vh, axis=-1, keepdims=True))
        o_ref[2] = (dsv * _gelu_grad(svp)).astype(BF16)

        @pl.when(i == nsteps - 1)
        def _():
            dg_ref[...] = jnp.sum(acc_g[...], axis=0, keepdims=True)

    full = lambda shp: pl.BlockSpec(shp, lambda i: (0,) * len(shp))
    return pl.pallas_call(
        body, name=name,
        out_shape=(jax.ShapeDtypeStruct((3, L, C), BF16), jax.ShapeDtypeStruct((nh, CHUNK, CHUNK), F32),
                   jax.ShapeDtypeStruct((nh, CHUNK, dh), F32), jax.ShapeDtypeStruct((1, C), F32)),
        grid=(nsteps,),
        in_specs=[pl.BlockSpec((None, tr, C), lambda i: (1, i, 0)), pl.BlockSpec((None, tr, C), lambda i: (2, i, 0)),
                  pl.BlockSpec((tr, C), lambda i: (i, 1)), pl.BlockSpec((tr, C), lambda i: (i, 0)), full((1, C)),
                  full((nh, CHUNK, CHUNK)), full((nh, CHUNK, dh))],
        out_specs=(pl.BlockSpec((3, tr, C), lambda i: (0, i, 0)), full((nh, CHUNK, CHUNK)), full((nh, CHUNK, dh)),
                   full((1, C))),
        scratch_shapes=[pltpu.VMEM((tr, C), F32), pltpu.VMEM((SUBLANES, C), F32)],
        compiler_params=_cparams(("arbitrary",)),
    )(proj3, proj3, dmix, dz, norm_g.reshape(1, C), w, bfull)


def _ffn_act_fwd(up3, conv_w, conv_b, name):
    _, L, Fh = up3.shape
    cb = LANES
    w2 = conv_w.reshape(3, 2, Fh).transpose(1, 0, 2)
    b2 = conv_b.reshape(2, 1, Fh)

    def body(u_ref, w_ref, b_ref, o_ref, ot_ref, gv_ref):
        g = _conv3(_taps(u_ref[0].astype(F32)), w_ref[0]) + b_ref[0]
        v = _conv3(_taps(u_ref[1].astype(F32)), w_ref[1]) + b_ref[1]
        gv_ref[0] = g.astype(BF16)
        gv_ref[1] = v.astype(BF16)
        ab = (g * _sigmoid(g) * v).astype(BF16)
        o_ref[...] = ab
        ot_ref[...] = _transpose_on_mxu(ab)

    blk3 = pl.BlockSpec((2, L, cb), lambda j: (0, 0, j))
    return pl.pallas_call(
        body, name=name,
        out_shape=(jax.ShapeDtypeStruct((L, Fh), BF16), jax.ShapeDtypeStruct((Fh, L), BF16),
                   jax.ShapeDtypeStruct((2, L, Fh), BF16)),
        grid=(Fh // cb,),
        in_specs=[blk3, pl.BlockSpec((2, 3, cb), lambda j: (0, 0, j)), pl.BlockSpec((2, 1, cb), lambda j: (0, 0, j))],
        out_specs=(pl.BlockSpec((L, cb), lambda j: (0, j)), pl.BlockSpec((cb, L), lambda j: (j, 0)), blk3),
        compiler_params=_cparams(("parallel",)),
    )(up3, w2, b2)


def _ffn_act_bwd(up3, gv3, da, conv_w, h2t, name):
    _, L, Fh = up3.shape
    D = h2t.shape[0]
    cb = LANES
    nb = Fh // cb
    w2 = conv_w.reshape(3, 2, Fh).transpose(1, 0, 2)

    def body(u_ref, gv_ref, d_ref, w_ref, h_ref, o_ref, dw_ref, db_ref, wg_ref, wv_ref, scr):
        j = pl.program_id(0)

        @pl.when(j == 0)
        def _():
            scr[1] = jnp.zeros((2, L, cb), BF16)

        prev = scr.at[(j + 1) % 2]
        wg_ref[...] = jnp.dot(h_ref[...], prev[0], preferred_element_type=F32).astype(BF16)
        wv_ref[...] = jnp.dot(h_ref[...], prev[1], preferred_element_type=F32).astype(BF16)
        tg, tv = _taps(u_ref[0].astype(F32)), _taps(u_ref[1].astype(F32))
        wg, wv = w_ref[0], w_ref[1]
        g = gv_ref[0].astype(F32)
        v = gv_ref[1].astype(F32)
        sg = _sigmoid(g)
        dav = d_ref[...].astype(F32)
        dg = dav * v * (sg * (1.0 + g * (1.0 - sg)))
        dv = dav * (g * sg)
        dug = _conv3_t(dg, wg).astype(BF16)
        duv = _conv3_t(dv, wv).astype(BF16)
        o_ref[0] = dug
        o_ref[1] = duv
        cur = scr.at[j % 2]
        cur[0] = dug
        cur[1] = duv
        for tap, (dwg, dwv) in enumerate(zip(_conv3_dw(dg, tg), _conv3_dw(dv, tv))):
            dw_ref[0, tap:tap + 1, :] = dwg
            dw_ref[1, tap:tap + 1, :] = dwv
        db_ref[0] = jnp.sum(dg, axis=0, keepdims=True)
        db_ref[1] = jnp.sum(dv, axis=0, keepdims=True)

    here = lambda j: jnp.minimum(j, nb - 1)
    before = lambda j: jnp.maximum(j - 1, 0)
    blk3 = pl.BlockSpec((2, L, cb), lambda j: (0, 0, here(j)))
    dup, dw2, db2, dwg, dwv = pl.pallas_call(
        body, name=name,
        out_shape=(jax.ShapeDtypeStruct((2, L, Fh), BF16), jax.ShapeDtypeStruct((2, 3, Fh), F32),
                   jax.ShapeDtypeStruct((2, 1, Fh), F32), jax.ShapeDtypeStruct((D, Fh), BF16),
                   jax.ShapeDtypeStruct((D, Fh), BF16)),
        grid=(nb + 1,),
        in_specs=[blk3, blk3, pl.BlockSpec((L, cb), lambda j: (0, here(j))),
                  pl.BlockSpec((2, 3, cb), lambda j: (0, 0, here(j))), pl.BlockSpec((D, L), lambda j: (0, 0))],
        out_specs=(blk3, pl.BlockSpec((2, 3, cb), lambda j: (0, 0, here(j))),
                   pl.BlockSpec((2, 1, cb), lambda j: (0, 0, here(j))),
                   pl.BlockSpec((D, cb), lambda j: (0, before(j))), pl.BlockSpec((D, cb), lambda j: (0, before(j)))),
        scratch_shapes=[pltpu.VMEM((2, 2, L, cb), BF16)],
        compiler_params=_cparams(("arbitrary",), VMEM_LIMIT_S5),
    )(up3, gv3, da, w2, h2t)
    return dup, dw2.transpose(1, 0, 2).reshape(3, 2 * Fh), db2.reshape(2 * Fh), jnp.concatenate([dwg, dwv], axis=1)


def _local_step(x, tgt, w, layer_weights, on_layer_grads):
    L, D = x.shape
    depth = w['norm_mix_g'].shape[0]
    saved = []
    for i in range(depth):
        j = i // 2
        wb = dict(layer_weights(2 * i, x))
        s = {'x': x, 'wb': wb}
        if i % 2 == 0:
            proj4, s['hT'] = _norm_mm(x, w['norm_mix_g'][i], wb['even_w_in'], BF16, "even_in_fwd", ok=('seg', 4))
            s['proj'] = proj4
            mixin = _sconv_fwd(proj4, w['even_conv_w'][j], "sconv_fwd")
            prm = (w['ssm_log_step'][j], w['ssm_a_re'][j], w['ssm_a_im'][j], w['ssm_b_re'][j], w['ssm_b_im'][j],
                   w['ssm_c_re'][j], w['ssm_c_im'][j])
            (lr, li, bmat, cmat), prep_vjp = jax.vjp(_s5_prep, *prm)
            yraw, s_re, s_im = _s5_fwd(proj4, lr, li, bmat, cmat, w['ssm_d'][j], "s5_fwd")
            mixin = _glu_fwd(yraw, wb['ssm_glu_w'], w['ssm_glu_b'][j], mixin, "glu_fwd")
            s.update(yraw=yraw, s_re=s_re, s_im=s_im, s5=(lr, li, bmat, cmat), prep_vjp=prep_vjp)
            s['mixinT'] = mixin.T
            x = _mm(mixin, wb['even_w_out'], 'nn', F32, "even_out_fwd", res=x)
        else:
            proj3, s['hT'] = _norm_mm(x, w['norm_mix_g'][i], wb['odd_w_in'], BF16, "odd_in_fwd", ok=('seg', 3))
            s['proj'] = proj3
            mixin, mixin_t = _pool_fwd(proj3, w['pool_w'][j], w['pool_scale'][j], "pool_fwd")
            mixin, s['mixinT'] = _sgu_fwd(proj3, w['sgu_norm_g'][j], w['sgu_w'][j], w['sgu_b'][j], mixin, mixin_t,
                                          "sgu_fwd")
            x = _mm(mixin, wb['odd_w_out'], 'nn', F32, "odd_out_fwd", res=x)
        s['x1'] = x
        wb.update(layer_weights(2 * i + 1, x))
        up3, h2t = _norm_mm(x, w['norm_ffn_g'][i], wb['ffn_w_up'], BF16, "ffn_up_fwd", ok=('seg', 2))
        a, at, gv3 = _ffn_act_fwd(up3, w['ffn_conv_w'][i], w['ffn_conv_b'][i], "ffn_act_fwd")
        x = _mm(a, wb['ffn_w_down'], 'nn', F32, "ffn_down_fwd", res=x)
        s.update(h2T=h2t, up3=up3, aT=at, gv3=gv3)
        saved.append(s)

    loss8, dx, dxb, dg_final = _loss_head(x, w['norm_final_g'], tgt)
    gs = {n: [None] * w[n].shape[0] for n in SMALL if n != 'norm_final_g'}
    gs['norm_final_g'] = dg_final.reshape(D)

    dep = None
    for i in reversed(range(depth)):
        j = i // 2
        s = saved[i]
        wb = s['wb']
        gb = {}
        da = _mm(dxb, wb['ffn_w_down'], 'nt', BF16, "ffn_down_dgrad", dep=dep)
        gb['ffn_w_down'] = _mm(s['aT'], dxb, 'nn', BF16, "ffn_down_wgrad")
        dup3, dcw, dcb, gb['ffn_w_up'] = _ffn_act_bwd(s['up3'], s['gv3'], da, w['ffn_conv_w'][i], s['h2T'],
                                                      "ffn_act_bwd")
        gs['ffn_conv_w'][i], gs['ffn_conv_b'][i] = dcw, dcb
        dep = on_layer_grads(2 * i + 1, gb)
        dx, dxb, dg = _mm_norm_bwd(dup3, wb['ffn_w_up'], s['x1'], w['norm_ffn_g'][i], dx, "ffn_up_dgrad",
                              ak=('seg', 2), dep=dep)
        gs['norm_ffn_g'][i] = dg.reshape(D)
        gb = {}
        if i % 2 == 0:
            dmix = _mm(dxb, wb['even_w_out'], 'nt', F32, "even_out_dgrad")
            gb['even_w_out'] = _mm(s['mixinT'], dxb, 'nn', BF16, "even_out_wgrad")
            dproj, dcw = _sconv_bwd(s['proj'], dmix, w['even_conv_w'][j], "sconv_bwd")
            gs['even_conv_w'][j] = dcw
            dyraw, dglu_w, dglu_b = _glu_bwd(s['yraw'], dmix, wb['ssm_glu_w'], w['ssm_glu_b'][j], "glu_bwd")
            gb['ssm_glu_w'] = dglu_w.astype(BF16)
            gs['ssm_glu_b'][j] = dglu_b.reshape(-1)
            lr, li, bmat, cmat = s['s5']
            dproj, dbm, dcm, dlam, dd = _s5_bwd(dyraw, s['proj'], dproj, s['s_re'], s['s_im'], lr, li, bmat, cmat,
                                               w['ssm_d'][j], "s5_bwd")
            gs['ssm_d'][j] = dd.reshape(-1)
            dcm = jnp.swapaxes(dcm, 1, 2)
            dprm = s['prep_vjp']((dlam[:, 0:1, :], dlam[:, 1:2, :], dbm, dcm))
            for n, gval in zip(('ssm_log_step', 'ssm_a_re', 'ssm_a_im', 'ssm_b_re', 'ssm_b_im', 'ssm_c_re',
                                'ssm_c_im'), dprm):
                gs[n][j] = gval
            gb['even_w_in'] = _mm(s['hT'], dproj, 'nn', BF16, "even_in_wgrad", bk=('seg', 4))
            w_in, in_kind, in_name = wb['even_w_in'], ('seg', 4), "even_in_dgrad"
        else:
            dmix = _mm(dxb, wb['odd_w_out'], 'nt', F32, "odd_out_dgrad")
            gb['odd_w_out'] = _mm(s['mixinT'], dxb, 'nn', BF16, "odd_out_wgrad")
            dz, dpw, dps = _pool_bwd(s['proj'], dmix, w['pool_w'][j], w['pool_scale'][j], "pool_bwd")
            gs['pool_w'][j], gs['pool_scale'][j] = dpw, dps.reshape(-1)
            dproj, dsw, dsb, dsg = _sgu_bwd(s['proj'], dmix, dz, w['sgu_norm_g'][j], w['sgu_w'][j], w['sgu_b'][j],
                                            "sgu_bwd")
            gs['sgu_w'][j], gs['sgu_b'][j], gs['sgu_norm_g'][j] = dsw, jnp.sum(dsb, axis=-1), dsg.reshape(-1)
            gb['odd_w_in'] = _mm(s['hT'], dproj, 'nn', BF16, "odd_in_wgrad", bk=('seg', 3))
            w_in, in_kind, in_name = wb['odd_w_in'], ('seg', 3), "odd_in_dgrad"
        dep = on_layer_grads(2 * i, gb)
        dx, dxb, dg = _mm_norm_bwd(dproj, w_in, s['x'], w['norm_mix_g'][i], dx, in_name, ak=in_kind, dep=dep)
        gs['norm_mix_g'][i] = dg.reshape(D)

    gsmall = {n: (v if n == 'norm_final_g' else jnp.stack(v)) for n, v in gs.items()}
    return loss8[0, 0], dx, gsmall


_HBM = pl.BlockSpec(memory_space=pltpu.HBM)
_CHIP_FLIPS = ((0, 0), (1, 0), (0, 1), (1, 1))


def _coords():
    return lax.axis_index("x"), lax.axis_index("y"), lax.axis_index("c")


def _flip(v, f):
    return 1 - v if f else v


def _shard_of(ref, axis, s, width):
    start = pl.multiple_of(s * width, LANES if axis == ref.ndim - 1 else 16) if width % 16 == 0 else s * width
    idx = [slice(None)] * ref.ndim
    idx[axis] = pl.ds(start, width)
    return ref.at[tuple(idx)]


_SEM = pl.BlockSpec(memory_space=pltpu.SEMAPHORE)
_ANY = pl.BlockSpec(memory_space=pl.ANY)
_DATAFLOW = pltpu.SideEffectType.DATAFLOW_SIDE_EFFECTING


def _in_hbm(a):
    return pltpu.with_memory_space_constraint(a, pltpu.HBM)


def _model_layer(name, l):
    if name.startswith('ffn'):
        return l
    return 2 * l + 1 if name.startswith('odd') else 2 * l


def _place_quarter(shard, l, axis, chip, dtype, dep=None):
    _, r, c = shard.shape
    tr = _pick(r, prefs=(512, 256, 128, 64, 32, 16))
    nrb = r // tr

    def body(chip_ref, i_ref, *rest):
        rest[-1][...] = i_ref[...].astype(dtype)

    if axis == 1:
        out_shape, o_map = (r, c * N_CHIPS), (lambda i, s: (i, s[0]))
    else:
        out_shape, o_map = (r * N_CHIPS, c), (lambda i, s: (s[0] * nrb + i, 0))
    in_specs = [pl.BlockSpec((None, tr, c), lambda i, s: (l, i, 0))]
    args = [chip, shard]
    if dep is not None:
        in_specs.append(pl.BlockSpec(memory_space=pl.ANY))
        args.append(dep)
    return pl.pallas_call(
        body, name="place_quarter", out_shape=jax.ShapeDtypeStruct(out_shape, dtype),
        grid_spec=pltpu.PrefetchScalarGridSpec(
            num_scalar_prefetch=1, grid=(nrb,), in_specs=in_specs, out_specs=pl.BlockSpec((tr, c), o_map)),
        compiler_params=_cparams(("parallel",)),
    )(*args)


def _gather_copies(land_refs, send_sem, recv_sem, axes, landing_chip_of, first=0):
    x, y, c = _coords()
    out = []
    for j, land in enumerate(land_refs):
        width = land.shape[axes[j]] // N_CHIPS
        for f in (1, 2, 3):
            fx, fy = _CHIP_FLIPS[f]
            px, py = _flip(x, fx), _flip(y, fy)
            lx, ly = landing_chip_of(px, py)
            out.append(pltpu.make_async_remote_copy(
                src_ref=_shard_of(land, axes[j], 2 * x + y, width), dst_ref=_shard_of(land, axes[j], 2 * lx + ly, width),
                send_sem=send_sem.at[3 * (first + j) + f - 1], recv_sem=recv_sem.at[3 * (first + j) + f - 1],
                device_id=(px, py, c), device_id_type=MESH))
    return out


def _gather_start(tag, lands, axes, dep=None):
    n = len(lands)

    def body(*refs):
        land_refs, send_sem, recv_sem = refs[:n], refs[-3], refs[-2]
        x, y, _ = _coords()
        for cp in _gather_copies(land_refs, send_sem, recv_sem, axes, lambda px, py: (x, y)):
            cp.start()
        refs[-1][...] = jnp.zeros_like(refs[-1])

    thru = [pltpu.HBM(a.shape, a.dtype) for a in lands]
    outs = pl.pallas_call(
        body, name=f"gather_start_{tag}",
        out_shape=tuple(thru + [pltpu.SemaphoreType.DMA((3 * n,)), pltpu.SemaphoreType.DMA((3 * n,)),
                                jax.ShapeDtypeStruct((SUBLANES, LANES), F32)]),
        in_specs=[_HBM] * n + ([_ANY] if dep is not None else []),
        out_specs=tuple([_HBM] * n + [_SEM, _SEM, pl.BlockSpec(memory_space=pltpu.VMEM)]),
        input_output_aliases={i: i for i in range(n)},
        compiler_params=pltpu.CompilerParams(has_side_effects=_DATAFLOW),
    )(*[_in_hbm(a) for a in lands], *([dep] if dep is not None else []))
    return list(outs[:n]), outs[n], outs[n + 1], outs[n + 2]


def _gather_wait(tag, lands, send_sem, recv_sem, axes, after, first=0):
    n = len(lands)

    def body(*refs):
        for cp in _gather_copies(refs[:n], refs[n], refs[n + 1], axes, lambda px, py: (px, py), first):
            cp.wait_send()
            cp.wait_recv()

    outs = pl.pallas_call(
        body, name=f"gather_wait_{tag}", out_shape=tuple(pltpu.HBM(a.shape, a.dtype) for a in lands),
        in_specs=[_HBM] * n + [_SEM, _SEM, _ANY], out_specs=tuple([_HBM] * n),
        input_output_aliases={i: i for i in range(n)},
        compiler_params=pltpu.CompilerParams(has_side_effects=_DATAFLOW),
    )(*lands, send_sem, recv_sem, after)
    return list(outs)


N_SLOTS = N_DEV - 1


def _scatter_sends(grad_refs, land_refs, send_sem, recv_sem, meta):
    x, y, c = _coords()
    out = []
    for j, (axis, owner, q, width) in enumerate(meta):
        other = c if owner == 0 else 1 - c
        for f, (fx, fy) in enumerate(_CHIP_FLIPS):
            px, py = _flip(x, fx), _flip(y, fy)
            slot = f + 4 * other - 1
            out.append((other if f == 0 else None, pltpu.make_async_remote_copy(
                src_ref=_shard_of(grad_refs[j], axis, 2 * px + py, width), dst_ref=land_refs[j].at[q, slot],
                send_sem=send_sem.at[4 * j + f], recv_sem=recv_sem.at[N_SLOTS * j + slot],
                device_id=(px, py, owner), device_id_type=MESH)))
    return out


def _scatter_start(layer, grads, lands, meta):
    n = len(grads)
    uniq = []
    for a in lands:
        if not any(a is u for u in uniq):
            uniq.append(a)
    which = [next(k for k, u in enumerate(uniq) if u is a) for a in lands]
    nu = len(uniq)

    def body(*refs):
        grad_refs, land_u = refs[:n], refs[n:n + nu]
        send_sem, recv_sem = refs[n + nu], refs[n + nu + 1]
        for other, cp in _scatter_sends(grad_refs, [land_u[k] for k in which], send_sem, recv_sem, meta):
            if other is None:
                cp.start()
            else:
                pl.when(other == 1)(cp.start)
        refs[-1][...] = jnp.zeros_like(refs[-1])

    thru = [pltpu.HBM(a.shape, a.dtype) for a in list(grads) + uniq]
    outs = pl.pallas_call(
        body, name=f"scatter_start_{layer}",
        out_shape=tuple([pltpu.SemaphoreType.DMA((4 * n,)), pltpu.SemaphoreType.DMA((N_SLOTS * n,))] + thru
                        + [jax.ShapeDtypeStruct((SUBLANES, LANES), F32)]),
        in_specs=[_HBM] * (n + nu),
        out_specs=tuple([_SEM, _SEM] + [_HBM] * (n + nu) + [pl.BlockSpec(memory_space=pltpu.VMEM)]),
        input_output_aliases={i: 2 + i for i in range(n + nu)},
        compiler_params=pltpu.CompilerParams(has_side_effects=_DATAFLOW),
    )(*[_in_hbm(a) for a in list(grads) + uniq])
    new_lands = [outs[2 + n + k] for k in which]
    return outs[0], outs[1], list(outs[2:2 + n]), new_lands, outs[-1]


def _scatter_wait(started, lands):
    nl = len(lands)
    flat_grads = [g for s in started for g in s[2]]
    ng, ns = len(flat_grads), len(started)

    def body(*refs):
        land_refs = refs[:nl]
        grad_refs = refs[nl:nl + ng]
        sem_refs = refs[nl + ng:nl + ng + 2 * ns]
        _, _, c = _coords()
        off = 0
        for k, (_, _, grads, idx, meta) in enumerate(started):
            send_sem, recv_sem = sem_refs[2 * k], sem_refs[2 * k + 1]
            lr = [land_refs[i] for i in idx]
            for other, cp in _scatter_sends(grad_refs[off:off + len(grads)], lr, send_sem, recv_sem, meta):
                if other is None:
                    cp.wait_send()
                else:
                    pl.when(other == 1)(cp.wait_send)
            for j, (axis, owner, q, width) in enumerate(meta):
                mine = (c if owner == 0 else 1 - c) == 0

                @pl.when(mine)
                def _():
                    for slot in range(N_SLOTS):
                        land = lr[j].at[q, slot]
                        pltpu.make_async_remote_copy(
                            src_ref=land, dst_ref=land, send_sem=send_sem.at[0], recv_sem=recv_sem.at[N_SLOTS * j + slot],
                            device_id=_coords(), device_id_type=MESH).wait_recv()
            off += len(grads)

    args = list(lands) + flat_grads
    thru = [pltpu.HBM(a.shape, a.dtype) for a in args]
    sems = [s for st in started for s in st[:2]]
    outs = pl.pallas_call(
        body, name="scatter_wait", out_shape=tuple(thru), in_specs=[_HBM] * (nl + ng) + [_SEM] * (2 * ns),
        out_specs=tuple([_HBM] * (nl + ng)), input_output_aliases={i: i for i in range(nl + ng)},
        compiler_params=pltpu.CompilerParams(has_side_effects=_DATAFLOW),
    )(*args, *sems)
    return list(outs[:nl]), list(outs[nl:])


def _sum_and_share(recv, layer_grads, axis, chip, name, dep=None):
    n, ns, r, c = recv.shape
    tr = _pick(r, prefs=(256, 128, 64, 32, 16))
    nr = r // tr
    nsteps = n * nr
    nlay = len(layer_grads)
    own_map = (lambda h, i, s: (i, s[0])) if axis == 1 else (lambda h, i, s: (s[0] * nr + i, 0))

    def body(chip_ref, i_ref, *rest):
        g_refs = rest[:nlay]
        o_ref, buf, loc_sems, send_sems, recv_sems = rest[nlay + (dep is not None):]
        h, i = pl.program_id(0), pl.program_id(1)
        step = h * nr + i
        slot = step % 2
        x, y, core = _coords()
        layer = core * n + h
        own = g_refs[0][...]
        for l in range(1, nlay):
            own = jnp.where(layer == l, g_refs[l][...], own)

        def copies(sl):
            dst = o_ref.at[core * n + h, pl.ds(pl.multiple_of(i * tr, tr), tr), :]
            loc = pltpu.make_async_copy(buf.at[sl], dst, loc_sems.at[sl])
            rem = pltpu.make_async_remote_copy(
                src_ref=buf.at[sl], dst_ref=dst, send_sem=send_sems.at[sl], recv_sem=recv_sems.at[step],
                device_id=(x, y, 1 - core), device_id_type=MESH)
            return loc, rem

        def drain(sl):
            loc, rem = copies(sl)
            loc.wait()
            rem.wait_send()

        pl.when(step >= 2)(lambda: drain(slot))
        acc = own.astype(F32)
        for s in range(ns):
            acc = acc + i_ref[s].astype(F32)
        buf[slot] = acc
        loc, rem = copies(slot)
        loc.start()
        rem.start()

        @pl.when(step == nsteps - 1)
        def _():
            drain(slot)
            if nsteps > 1:
                drain(1 - slot)
            for hh in range(n):
                for ii in range(nr):
                    land = o_ref.at[(1 - core) * n + hh, pl.ds(ii * tr, tr), :]
                    pltpu.make_async_remote_copy(
                        src_ref=buf.at[0], dst_ref=land, send_sem=send_sems.at[0], recv_sem=recv_sems.at[hh * nr + ii],
                        device_id=(x, y, 1 - core), device_id_type=MESH).wait_recv()

    return pl.pallas_call(
        body, name=name, out_shape=jax.ShapeDtypeStruct((2 * n, r, c), F32),
        grid_spec=pltpu.PrefetchScalarGridSpec(
            num_scalar_prefetch=1, grid=(n, nr),
            in_specs=[pl.BlockSpec((None, ns, tr, c), lambda h, i, s: (h, 0, i, 0))]
            + [pl.BlockSpec((tr, c), own_map)] * nlay + ([pl.BlockSpec(memory_space=pl.ANY)] if dep is not None else []),
            out_specs=_HBM,
            scratch_shapes=[pltpu.VMEM((2, tr, c), F32), pltpu.SemaphoreType.DMA((2,)),
                            pltpu.SemaphoreType.DMA((2,)), pltpu.SemaphoreType.DMA((nsteps,))]),
        compiler_params=_cparams(("arbitrary", "arbitrary")),
    )(chip, recv, *layer_grads, *([dep] if dep is not None else []))


def _adamw_update(w_ref, g_ref, m_ref, v_ref, d_ref, mo_ref, vo_ref):
    bc1 = 1.0 - ADAM_B1 ** ADAM_STEP
    bc2 = 1.0 - ADAM_B2 ** ADAM_STEP
    gv = g_ref[...]
    mn = ADAM_B1 * m_ref[...] + (1.0 - ADAM_B1) * gv
    vn = ADAM_B2 * v_ref[...] + (1.0 - ADAM_B2) * (gv * gv)
    d_ref[...] = -ADAM_LR * ((mn / bc1) / (jnp.sqrt(vn / bc2) + ADAM_EPS) + ADAM_WD * w_ref[...])
    mo_ref[...] = mn
    vo_ref[...] = vn


def _adamw(w, g, m, v, name):
    def body(*refs):
        _adamw_update(*refs)

    tr = _pick(w.shape[0], prefs=(256, 128, 64, 32, 16, 8))
    blk = pl.BlockSpec((tr, w.shape[1]), lambda i: (i, 0))
    sds = jax.ShapeDtypeStruct(w.shape, F32)
    return pl.pallas_call(
        body, name=name, out_shape=(sds, sds, sds), grid=(w.shape[0] // tr,), in_specs=[blk] * 4,
        out_specs=(blk,) * 3, compiler_params=_cparams(("parallel",)),
    )(w, g, m, v)


def _adamw_many(tensors, name, by_layer=False):
    n = len(tensors)

    def body(*refs):
        for t in range(n):
            _adamw_update(*refs[4 * t:4 * t + 4], *refs[4 * n + 3 * t:4 * n + 3 * t + 3])

    def spec(a):
        nd = a.ndim
        if by_layer:
            return pl.BlockSpec((1,) + a.shape[1:], lambda i: (i,) + (0,) * (nd - 1))
        return pl.BlockSpec(a.shape, lambda i: (0,) * nd)

    steps = tensors[0][0].shape[0] if by_layer else 1
    outs = pl.pallas_call(
        body, name=name, out_shape=tuple(jax.ShapeDtypeStruct(t[0].shape, F32) for t in tensors for _ in range(3)),
        grid=(steps,), in_specs=[spec(a) for t in tensors for a in t],
        out_specs=tuple(spec(t[0]) for t in tensors for _ in range(3)), compiler_params=_cparams(("parallel",)),
    )(*[a for t in tensors for a in t])
    return [tuple(outs[3 * t:3 * t + 3]) for t in range(n)]


_PACK_QUANTUM = 256 * LANES


def _pack(arrs):
    flat = jnp.concatenate([a.reshape(-1).astype(F32) for a in arrs])
    flat = jnp.pad(flat, (0, (-flat.shape[0]) % _PACK_QUANTUM))
    return flat.reshape(-1, LANES)


def _unpack(p, shapes):
    flat = p.reshape(-1)
    out, off = [], 0
    for s in shapes:
        n = int(np.prod(s))
        out.append(flat[off:off + n].reshape(s))
        off += n
    return out


def kernel(*args):
    nw = len(WEIGHTS)
    x, tgt = args[0], args[1 + nw]
    w = dict(zip(WEIGHTS, args[1:1 + nw]))
    m = dict(zip(WEIGHTS, args[2 + nw:2 + 2 * nw]))
    v = dict(zip(WEIGHTS, args[2 + 2 * nw:2 + 3 * nw]))
    _, L, D = x.shape
    chip = 2 * lax.axis_index("x") + lax.axis_index("y")

    big = list(BIG)
    small_sh_shapes = [w[n].shape for n in SMALL_SHARDED]
    nbig = len(big)
    chip1 = chip.reshape(1).astype(jnp.int32)
    axes2 = [BIG[n] - 1 for n in big] + [0]
    shards = [w[n] for n in big] + [_pack([w[n] for n in SMALL_SHARDED])[None]]
    pairs = [(t, l) for t in range(nbig + 1) for l in range(shards[t].shape[0])]
    depth = w['norm_mix_g'].shape[0]
    part_of = lambda t, l: 0 if t == nbig else 2 * _model_layer(big[t], l) + big[t].startswith('ffn')
    flying, token = {}, None
    for tag, gset in enumerate(([0], list(range(1, 2 * depth)))):
        ids = [k for g in gset for k, (t, l) in enumerate(pairs) if part_of(t, l) == g]
        ts = [pairs[k][0] for k in ids]
        placed = [_place_quarter(shards[t], pairs[k][1], axes2[t], chip1, F32 if t == nbig else BF16, token)
                  for k, t in zip(ids, ts)]
        lands, send, recv, token = _gather_start(tag, placed, [axes2[t] for t in ts], token)
        first = 0
        for g in gset:
            n = sum(1 for t, l in pairs if part_of(t, l) == g)
            flying[g] = (ts[first:first + n], lands[first:first + n], send, recv, first)
            first += n

    def wait_group(g, after):
        ts, lands, send, recv, first = flying[g]
        landed = _gather_wait(g, lands, send, recv, [axes2[t] for t in ts], token if after is None else after, first)
        return dict(zip(ts, landed))

    first = wait_group(0, None)
    packed = first.pop(nbig).reshape(N_CHIPS, -1, LANES)
    per_chip = [_unpack(packed[s], small_sh_shapes) for s in range(N_CHIPS)]
    wl = dict(w)
    for k, n in enumerate(SMALL_SHARDED):
        wl[n] = jnp.concatenate([per_chip[s][k] for s in range(N_CHIPS)], axis=-1)

    def layer_weights(i, after):
        got = first if i == 0 else wait_group(i, after)
        return {big[t]: a for t, a in got.items()}

    small_shapes = [(w[n].shape[:-1] + (w[n].shape[-1] * N_CHIPS,)) if n in SMALL_SHARDED else w[n].shape
                    for n in SMALL] + [(1,)]
    n_small = sum(int(np.prod(s)) for s in small_shapes)
    pack_rows = -(-n_small // _PACK_QUANTUM) * _PACK_QUANTUM // LANES
    nlayers = [w[n].shape[0] for n in big] + [2]
    halves = [n // 2 for n in nlayers]
    quarters = [tuple(w[n].shape[1:]) for n in big] + [(pack_rows // 2 // N_CHIPS, LANES)]
    wire = [BF16] * nbig + [F32]
    land_now = [lax.empty((halves[t], N_SLOTS) + quarters[t], wire[t]) for t in range(nbig + 1)]
    gparts = [[None] * n for n in nlayers]
    started = []

    def start_scatter(tag, ts, ls, arrays):
        meta = [(axes2[t], l // halves[t], l % halves[t], quarters[t][axes2[t]]) for t, l in zip(ts, ls)]
        send, recv, thru, new_lands, token = _scatter_start(tag, arrays, [land_now[t] for t in ts], meta)
        for t, ln in zip(ts, new_lands):
            land_now[t] = ln
        started.append((send, recv, thru, ts, meta, ls))
        return token

    def on_layer_grads(g, gb):
        ts = [big.index(n) for n in gb]
        return start_scatter(g, ts, [g // 2 if big[t].startswith('ffn') else g // 4 for t in ts],
                             [gb[big[t]] for t in ts])

    loss, dx, gsmall = _local_step(x.reshape(L, D), tgt.reshape(L, D), wl, layer_weights, on_layer_grads)
    gpack = _pack([gsmall[n] for n in SMALL] + [loss.reshape(1)])
    start_scatter(2 * depth, [nbig, nbig], [0, 1], [gpack[:pack_rows // 2], gpack[pack_rows // 2:]])
    landed, sent = _scatter_wait([s[:5] for s in started], land_now)
    for (t, l), g in zip([(t, l) for s in started for t, l in zip(s[3], s[5])], sent):
        gparts[t][l] = g
    small_sum = _sum_and_share(landed[nbig], gparts[nbig], 0, chip1, "sum_share_small")
    quarter_rows = small_sum.shape[0] * small_sum.shape[1]
    placed = _place_quarter(small_sum.reshape(1, quarter_rows, LANES), 0, 0, chip1, F32)
    flying_small, send, recv, token = _gather_start("small", [placed], [0])
    gshard = {n: _sum_and_share(landed[t], gparts[t], axes2[t], chip1, "sum_share_" + n, token)
              for t, n in enumerate(big)}
    small_all = _gather_wait("small", flying_small, send, recv, [0], gshard[big[-1]])[0]
    gpack = small_all.reshape(N_CHIPS, 2, quarter_rows // 2, LANES).transpose(1, 0, 2, 3).reshape(pack_rows, LANES)
    gs = dict(zip(SMALL + ['loss'], _unpack(gpack, small_shapes)))
    loss = gs.pop('loss').reshape(())
    for n in SMALL_SHARDED:
        width = w[n].shape[-1]
        gs[n] = lax.dynamic_slice_in_dim(gs[n], chip * width, width, axis=gs[n].ndim - 1)

    grads, delta, new_m, new_v = {}, {}, {}, {}
    for n in big:
        shp = w[n].shape
        flat = lambda a: a.reshape(shp[0] * shp[1], shp[2])
        g = gshard[n]
        grads[n] = g
        d_, m_, v_ = _adamw(flat(w[n]), flat(g), flat(m[n]), flat(v[n]), "adamw_" + n)
        delta[n], new_m[n], new_v[n] = d_.reshape(shp), m_.reshape(shp), v_.reshape(shp)
    sparse = [n for n in SMALL if w[n].ndim == 4 and w[n].shape[-1] < LANES // 2]
    for names, by_layer in ((sparse, True), ([n for n in SMALL if n not in sparse], False)):
        as2d = lambda a: a.reshape(1, -1) if a.ndim == 1 else a
        res = _adamw_many([(as2d(w[n]), as2d(gs[n]), as2d(m[n]), as2d(v[n])) for n in names],
                          "adamw_small_by_layer" if by_layer else "adamw_small", by_layer)
        for n, (d_, m_, v_) in zip(names, res):
            shp = w[n].shape
            grads[n], delta[n], new_m[n], new_v[n] = gs[n], d_.reshape(shp), m_.reshape(shp), v_.reshape(shp)

    return (loss, dx.reshape(1, L, D), *[grads[n] for n in WEIGHTS], *[delta[n] for n in WEIGHTS],
            *[new_m[n] for n in WEIGHTS], *[new_v[n] for n in WEIGHTS])
```

```python
import math

import numpy as np
import jax
import jax.numpy as jnp
from jax import lax
from jax.experimental import pallas as pl
from jax.experimental.pallas import tpu as pltpu

F32 = jnp.float32
BF16 = jnp.bfloat16
MESH = pl.DeviceIdType.MESH

EPS = 1e-6
CHUNK = 128
POOL_WINDOWS = (2, 4, 8, 16)
LANES = 128
SUBLANES = 8
SCAN_CHUNKS = SUBLANES
S5_GROUPS_PER_STEP = 4
MM_TM_CAP, MM_TN_CAP, MM_TK_CAP = 1408, 2816, 2048
MM_TK_WHOLE = 2816
VMEM_LIMIT = 48 * 1024 * 1024
VMEM_LIMIT_S5 = 56 * 1024 * 1024

ADAM_LR, ADAM_B1, ADAM_B2, ADAM_EPS, ADAM_WD, ADAM_STEP = 0.001, 0.9, 0.999, 1e-08, 0.01, 10

WEIGHTS = ['norm_mix_g', 'even_w_in', 'even_conv_w', 'ssm_log_step', 'ssm_a_re', 'ssm_a_im', 'ssm_b_re',
           'ssm_b_im', 'ssm_c_re', 'ssm_c_im', 'ssm_d', 'ssm_glu_w', 'ssm_glu_b', 'even_w_out', 'odd_w_in',
           'pool_w', 'pool_scale', 'sgu_norm_g', 'sgu_w', 'sgu_b', 'odd_w_out', 'norm_ffn_g', 'ffn_w_up',
           'ffn_conv_w', 'ffn_conv_b', 'ffn_w_down', 'norm_final_g']
BIG = {'even_w_in': 2, 'ssm_glu_w': 1, 'even_w_out': 1, 'odd_w_in': 2, 'odd_w_out': 1, 'ffn_w_up': 2,
       'ffn_w_down': 1}
SMALL_SHARDED = ('even_conv_w', 'pool_scale', 'sgu_norm_g', 'ffn_conv_w')
SMALL = [n for n in WEIGHTS if n not in BIG]
N_CHIPS = 4
N_DEV = 8


def _cparams(sem=None, vmem=VMEM_LIMIT):
    kw = dict(vmem_limit_bytes=vmem)
    if sem is not None:
        kw['dimension_semantics'] = sem
    return pltpu.CompilerParams(**kw)


def _pick(n, segs=(), prefs=(1024, 512, 256, 128)):
    for t in prefs:
        if n % t == 0 and all(s % t == 0 for s in segs if s):
            return t
    return n


def _largest_tile(n, segs, cap):
    best = None
    for t in range(LANES, min(n, cap) + 1, LANES):
        if n % t == 0 and all(s % t == 0 for s in segs if s):
            best = t
    return best if best is not None else n


def _ldims(arr, kind):
    if kind is None:
        return arr.shape
    if kind[0] == 'lead':
        return arr.shape[1:]
    return (arr.shape[1], arr.shape[0] * arr.shape[2])


def _segw(arr, kind):
    return arr.shape[2] if (kind is not None and kind[0] == 'seg') else None


def _opspec(arr, kind, br, bc, rfn, cfn):
    if kind is None:
        return pl.BlockSpec((br, bc), lambda i, j, k: (rfn(i, j, k), cfn(i, j, k)))
    if kind[0] == 'lead':
        lead = kind[1]
        return pl.BlockSpec((None, br, bc), lambda i, j, k: (lead, rfn(i, j, k), cfn(i, j, k)))
    per = arr.shape[2] // bc
    return pl.BlockSpec((None, br, bc), lambda i, j, k: (cfn(i, j, k) // per, rfn(i, j, k), cfn(i, j, k) % per))


def _mm(a, b, mode, out_dtype, name, ak=None, bk=None, ok=None, res=None, dep=None):
    ar, ac = _ldims(a, ak)
    br_, bc_ = _ldims(b, bk)
    if mode == 'nn':
        M, K, N = ar, ac, bc_
        assert br_ == K
    else:
        M, K, N = ar, ac, br_
        assert bc_ == K
    sa, sb = _segw(a, ak), _segw(b, bk)
    so = (N // ok[1]) if ok is not None else None
    tm = _largest_tile(M, [], MM_TM_CAP)
    tn = _largest_tile(N, [sb if mode == 'nn' else None, so], MM_TN_CAP)
    ksegs = [sa, sb if mode == 'nt' else None]
    tk = K if (K <= MM_TK_WHOLE and not any(ksegs)) else _largest_tile(K, ksegs, MM_TK_CAP)
    nk = K // tk
    I = lambda i, j, k: i
    J = lambda i, j, k: j
    Kk = lambda i, j, k: k
    a_spec = _opspec(a, ak, tm, tk, I, Kk)
    b_segs = b.shape[0] if (mode == 'nn' and sb and ok is None and res is None and N <= MM_TN_CAP) else 0
    if b_segs:
        tn = N
        b_spec = pl.BlockSpec((b_segs, tk, sb), lambda i, j, k: (0, k, 0))
        dims = (((1,), (0,)), ((), ()))
    elif mode == 'nn':
        b_spec = _opspec(b, bk, tk, tn, Kk, J)
        dims = (((1,), (0,)), ((), ()))
    else:
        b_spec = _opspec(b, bk, tn, tk, J, Kk)
        dims = (((1,), (1,)), ((), ()))
    if ok is None:
        out_shape = jax.ShapeDtypeStruct((M, N), out_dtype)
        o_spec = pl.BlockSpec((tm, tn), lambda i, j, k: (i, j))
    else:
        out_shape = jax.ShapeDtypeStruct((ok[1], M, N // ok[1]), out_dtype)
        per = (N // ok[1]) // tn
        o_spec = pl.BlockSpec((None, tm, tn), lambda i, j, k: (j // per, i, j % per))
    has_res = res is not None

    def body(*refs):
        a_ref, b_ref = refs[0], refs[1]
        r_ref = refs[2] if has_res else None
        o_ref = refs[n_in]
        if b_segs:
            av = a_ref[...].astype(BF16)
            k = pl.program_id(2)
            for s in range(b_segs):
                cols = slice(s * sb, (s + 1) * sb)
                part = lax.dot_general(av, b_ref[s].astype(BF16), dims, preferred_element_type=F32)
                if nk == 1:
                    o_ref[:, cols] = part.astype(out_dtype)
                    continue
                acc = refs[-1]

                @pl.when(k == 0)
                def _():
                    acc[:, cols] = part

                @pl.when((k > 0) & (k < nk - 1))
                def _():
                    acc[:, cols] += part

                @pl.when(k == nk - 1)
                def _():
                    o_ref[:, cols] = (acc[:, cols] + part).astype(out_dtype)
            return
        prod = lax.dot_general(a_ref[...].astype(BF16), b_ref[...].astype(BF16), dims, preferred_element_type=F32)
        if nk == 1:
            o_ref[...] = (prod + r_ref[...] if has_res else prod).astype(out_dtype)
            return
        acc = refs[-1]
        k = pl.program_id(2)

        @pl.when(k == 0)
        def _():
            acc[...] = prod

        @pl.when(k > 0)
        def _():
            acc[...] += prod

        @pl.when(k == nk - 1)
        def _():
            o = acc[...]
            if has_res:
                o = o + r_ref[...]
            o_ref[...] = o.astype(out_dtype)

    in_specs = [a_spec, b_spec]
    args = [a, b]
    if has_res:
        in_specs.append(pl.BlockSpec((tm, tn), lambda i, j, k: (i, j)))
        args.append(res)
    if dep is not None:
        in_specs.append(pl.BlockSpec(memory_space=pl.ANY))
        args.append(dep)
    n_in = len(args)
    return pl.pallas_call(
        body, name=name, out_shape=out_shape, grid=(M // tm, N // tn, nk), in_specs=in_specs, out_specs=o_spec,
        scratch_shapes=[pltpu.VMEM((tm, tn), F32)] if nk > 1 else [],
        compiler_params=_cparams(("parallel", "parallel", "arbitrary")),
    )(*args)


_G0 = math.sqrt(2.0 / math.pi)
_G1 = 0.044715


def _gelu(x):
    return 0.5 * x * (1.0 + jnp.tanh(_G0 * (x + _G1 * x * x * x)))


def _gelu_grad(x):
    x2 = x * x
    t = jnp.tanh(_G0 * (x + _G1 * x * x2))
    return 0.5 * (1.0 + t) + 0.5 * x * (1.0 - t * t) * (_G0 * (1.0 + 3.0 * _G1 * x2))


def _sigmoid(x):
    return 1.0 / (1.0 + jnp.exp(-x))


def _down(v, k):
    r = pltpu.roll(v, k, axis=0)
    row = lax.broadcasted_iota(jnp.int32, (SUBLANES, v.shape[1]), 0)
    return jnp.concatenate([jnp.where(row >= k, r[:SUBLANES], 0.0), r[SUBLANES:]], axis=0)


def _up(v, k):
    n = v.shape[0]
    r = pltpu.roll(v, n - k, axis=0)
    row = lax.broadcasted_iota(jnp.int32, (SUBLANES, v.shape[1]), 0)
    return jnp.concatenate([r[:n - SUBLANES], jnp.where(row < SUBLANES - k, r[n - SUBLANES:], 0.0)], axis=0)


def _taps(v):
    return _down(v, 2), _down(v, 1), v


def _conv3(taps, w):
    return w[0:1, :] * taps[0] + w[1:2, :] * taps[1] + w[2:3, :] * taps[2]


def _conv3_t(dv, w):
    return w[2:3, :] * dv + w[1:2, :] * _up(dv, 1) + w[0:1, :] * _up(dv, 2)


def _conv3_dw(dv, taps):
    return tuple(jnp.sum(dv * tp, axis=0, keepdims=True) for tp in taps)


def _cmul(ar, ai, br, bi):
    return ar * br - ai * bi, ar * bi + ai * br


def _cpow(lr, li, n):
    rr = ri = None
    br, bi = lr, li
    while n:
        if n & 1:
            rr, ri = (br, bi) if rr is None else _cmul(rr, ri, br, bi)
        n >>= 1
        if n:
            br, bi = _cmul(br, bi, br, bi)
    return rr, ri


NORM_ROWS = 256


def _norm_mm(x, g, b, out_dtype, name, ok=None):
    M, D = x.shape
    N = b.shape[1]
    so = (N // ok[1]) if ok is not None else None
    tm = _largest_tile(M, [], 1024)
    tn = _largest_tile(N, [so], MM_TN_CAP)
    segs_per_block = 1
    if ok is not None:
        segs_per_block = max(c for c in range(1, ok[1] + 1) if ok[1] % c == 0 and (c == 1 or c * so <= MM_TN_CAP))
    if ok is None:
        out_shape = jax.ShapeDtypeStruct((M, N), out_dtype)
        o_spec = pl.BlockSpec((tm, tn), lambda i, j: (i, j))
    elif segs_per_block > 1:
        tn = segs_per_block * so
        out_shape = jax.ShapeDtypeStruct((ok[1], M, so), out_dtype)
        o_spec = pl.BlockSpec((segs_per_block, tm, so), lambda i, j: (j, i, 0))
    else:
        out_shape = jax.ShapeDtypeStruct((ok[1], M, N // ok[1]), out_dtype)
        per = (N // ok[1]) // tn
        o_spec = pl.BlockSpec((None, tm, tn), lambda i, j: (j // per, i, j % per))

    def body(x_ref, g_ref, b_ref, o_ref, ht_ref, h_scr):
        @pl.when(pl.program_id(1) == 0)
        def _():
            for c in range(tm // NORM_ROWS):
                rows = pl.ds(c * NORM_ROWS, NORM_ROWS)
                xv = x_ref[rows, :]
                h = xv * lax.rsqrt(jnp.mean(xv * xv, axis=-1, keepdims=True) + EPS) * g_ref[...]
                h_scr[rows, :] = h.astype(BF16)
                ht_ref[:, rows] = h.T.astype(BF16)

        prod = jnp.dot(h_scr[...], b_ref[...], preferred_element_type=F32).astype(out_dtype)
        if segs_per_block > 1:
            for s in range(segs_per_block):
                o_ref[s] = prod[:, s * so:(s + 1) * so]
        else:
            o_ref[...] = prod

    return pl.pallas_call(
        body, name=name, out_shape=(out_shape, jax.ShapeDtypeStruct((D, M), BF16)), grid=(M // tm, N // tn),
        in_specs=[pl.BlockSpec((tm, D), lambda i, j: (i, 0)), pl.BlockSpec((1, D), lambda i, j: (0, 0)),
                  pl.BlockSpec((D, tn), lambda i, j: (0, j))],
        out_specs=(o_spec, pl.BlockSpec((D, tm), lambda i, j: (0, i))),
        scratch_shapes=[pltpu.VMEM((tm, D), BF16)], compiler_params=_cparams(("parallel", "arbitrary")),
    )(x, g.reshape(1, D), b)


def _mm_norm_bwd(a, b, x, g, dres, name, ak=None, dep=None):
    M, K = _ldims(a, ak)
    D = b.shape[0]
    assert b.shape[1] == K and x.shape == (M, D)
    sa = _segw(a, ak)
    whole_segs = bool(sa) and K <= MM_TK_WHOLE
    tk = K if (K <= MM_TK_WHOLE) else _largest_tile(K, [sa], MM_TK_WHOLE)
    tm = _largest_tile(M, [], 1024 if (tk == K or tk <= MM_TK_CAP) else 512)
    ni, nk = M // tm, K // tk
    if whole_segs:
        a_spec = pl.BlockSpec((a.shape[0], tm, sa), lambda i, k: (0, i, 0))
    else:
        a3 = _opspec(a, ak, tm, tk, lambda i, j, k: i, lambda i, j, k: k)
        a_spec = pl.BlockSpec(a3.block_shape, lambda i, k: a3.index_map(i, 0, k))
    n_in = 5 + (dep is not None)

    def body(*refs):
        a_ref, b_ref, x_ref, g_ref, r_ref = refs[:5]
        dx_ref, dxb_ref, dg_ref, acc, accg = refs[n_in:]
        i, k = pl.program_id(0), pl.program_id(1)
        av = jnp.concatenate([a_ref[s] for s in range(a.shape[0])], axis=1) if whole_segs else a_ref[...]
        prod = lax.dot_general(av.astype(BF16), b_ref[...], (((1,), (1,)), ((), ())), preferred_element_type=F32)

        @pl.when(k == 0)
        def _():
            acc[...] = prod

        @pl.when(k > 0)
        def _():
            acc[...] += prod

        @pl.when((i == 0) & (k == 0))
        def _():
            accg[...] = jnp.zeros_like(accg)

        @pl.when(k == nk - 1)
        def _():
            for c in range(tm // NORM_ROWS):
                rows = pl.ds(c * NORM_ROWS, NORM_ROWS)
                xv = x_ref[rows, :]
                r = lax.rsqrt(jnp.mean(xv * xv, axis=-1, keepdims=True) + EPS)
                xh = xv * r
                dhv = acc[rows, :]
                accg[...] += jnp.sum((dhv * xh).reshape(NORM_ROWS // SUBLANES, SUBLANES, D), axis=0)
                dxh = dhv * g_ref[...]
                dxv = r_ref[rows, :] + r * (dxh - xh * jnp.mean(dxh * xh, axis=-1, keepdims=True))
                dx_ref[rows, :] = dxv
                dxb_ref[rows, :] = dxv.astype(BF16)

        @pl.when((i == ni - 1) & (k == nk - 1))
        def _():
            dg_ref[...] = jnp.sum(accg[...], axis=0, keepdims=True)

    row = pl.BlockSpec((tm, D), lambda i, k: (i, 0))
    vec = pl.BlockSpec((1, D), lambda i, k: (0, 0))
    in_specs = [a_spec, pl.BlockSpec((D, tk), lambda i, k: (0, k)), row, vec, row]
    args = [a, b, x, g.reshape(1, D), dres]
    if dep is not None:
        in_specs.append(pl.BlockSpec(memory_space=pl.ANY))
        args.append(dep)
    return pl.pallas_call(
        body, name=name,
        out_shape=(jax.ShapeDtypeStruct((M, D), F32), jax.ShapeDtypeStruct((M, D), BF16),
                   jax.ShapeDtypeStruct((1, D), F32)),
        grid=(ni, nk), in_specs=in_specs, out_specs=(row, row, vec),
        scratch_shapes=[pltpu.VMEM((tm, D), F32), pltpu.VMEM((SUBLANES, D), F32)],
        compiler_params=_cparams(("arbitrary", "arbitrary"), VMEM_LIMIT_S5),
    )(*args)


def _loss_head(x, g, tgt):
    L, D = x.shape
    tr = _pick(L, prefs=(512, 256, 128))
    nsteps = L // tr

    def body(x_ref, g_ref, t_ref, loss_ref, dx_ref, dxb_ref, dg_ref, acc_g, acc_l):
        i = pl.program_id(0)

        @pl.when(i == 0)
        def _():
            acc_g[...] = jnp.zeros_like(acc_g)
            acc_l[...] = jnp.zeros_like(acc_l)

        xv = x_ref[...]
        gv = g_ref[...]
        r = lax.rsqrt(jnp.mean(xv * xv, axis=-1, keepdims=True) + EPS)
        xh = xv * r
        e = xh * gv - t_ref[...]
        acc_l[...] += jnp.sum((e * e).reshape(tr // SUBLANES, SUBLANES, D), axis=0)
        dy = e * (1.0 / D)
        acc_g[...] += jnp.sum((dy * xh).reshape(tr // SUBLANES, SUBLANES, D), axis=0)
        dxh = dy * gv
        dxv = r * (dxh - xh * jnp.mean(dxh * xh, axis=-1, keepdims=True))
        dx_ref[...] = dxv
        dxb_ref[...] = dxv.astype(BF16)

        @pl.when(i == nsteps - 1)
        def _():
            dg_ref[...] = jnp.sum(acc_g[...], axis=0, keepdims=True)
            tot = jnp.sum(jnp.sum(acc_l[...], axis=0, keepdims=True), axis=1, keepdims=True) * (0.5 / D)
            loss_ref[...] = jnp.broadcast_to(tot, (SUBLANES, LANES))

    row = pl.BlockSpec((tr, D), lambda i: (i, 0))
    vec = pl.BlockSpec((1, D), lambda i: (0, 0))
    return pl.pallas_call(
        body, name="loss_head",
        out_shape=(jax.ShapeDtypeStruct((SUBLANES, LANES), F32), jax.ShapeDtypeStruct((L, D), F32),
                   jax.ShapeDtypeStruct((L, D), BF16), jax.ShapeDtypeStruct((1, D), F32)),
        grid=(nsteps,), in_specs=[row, vec, row],
        out_specs=(pl.BlockSpec((SUBLANES, LANES), lambda i: (0, 0)), row, row, vec),
        scratch_shapes=[pltpu.VMEM((SUBLANES, D), F32), pltpu.VMEM((SUBLANES, D), F32)],
        compiler_params=_cparams(("arbitrary",)),
    )(x, g.reshape(1, D), tgt)


def _sconv_fwd(proj4, conv_w, name):
    _, L, C = proj4.shape
    cb = LANES

    def body(p_ref, w_ref, o_ref):
        xa, ba, ca = p_ref[0].astype(F32), p_ref[1].astype(F32), p_ref[2].astype(F32)
        o_ref[...] = (ba * _conv3(_taps(ca * xa), w_ref[...])).astype(BF16)

    return pl.pallas_call(
        body, name=name, out_shape=jax.ShapeDtypeStruct((L, 2 * C), BF16), grid=(C // cb,),
        in_specs=[pl.BlockSpec((3, L, cb), lambda j: (0, 0, j)), pl.BlockSpec((3, cb), lambda j: (0, j))],
        out_specs=pl.BlockSpec((L, cb), lambda j: (0, j)), compiler_params=_cparams(("parallel",)),
    )(proj4, conv_w)


def _sconv_bwd(proj4, dmix, conv_w, name):
    _, L, C = proj4.shape
    cb = LANES

    def body(p_ref, d_ref, w_ref, o_ref, dw_ref):
        xa, ba, ca = p_ref[0].astype(F32), p_ref[1].astype(F32), p_ref[2].astype(F32)
        w = w_ref[...]
        dya = d_ref[...]
        tq = _taps(ca * xa)
        cq = _conv3(tq, w)
        dcq = dya * ba
        dq = _conv3_t(dcq, w)
        for tap, dwt in enumerate(_conv3_dw(dcq, tq)):
            dw_ref[tap:tap + 1, :] = dwt
        o_ref[0] = (dq * ca).astype(BF16)
        o_ref[1] = (dya * cq).astype(BF16)
        o_ref[2] = (dq * xa).astype(BF16)

    return pl.pallas_call(
        body, name=name,
        out_shape=(jax.ShapeDtypeStruct((4, L, C), BF16), jax.ShapeDtypeStruct((3, C), F32)), grid=(C // cb,),
        in_specs=[pl.BlockSpec((3, L, cb), lambda j: (0, 0, j)), pl.BlockSpec((L, cb), lambda j: (0, j)),
                  pl.BlockSpec((3, cb), lambda j: (0, j))],
        out_specs=(pl.BlockSpec((3, L, cb), lambda j: (0, 0, j)), pl.BlockSpec((3, cb), lambda j: (0, j))),
        compiler_params=_cparams(("parallel",)),
    )(proj4, dmix, conv_w)


def _s5_prep(log_step, a_re, a_im, b_re, b_im, c_re, c_im):
    G, P = a_re.shape
    H = b_re.shape[-1]
    gs = S5_GROUPS_PER_STEP
    ns = G // gs
    gu = LANES // H
    lam = lax.complex(a_re, a_im)
    step = jnp.exp(log_step)[:, None]
    lam_bar = jnp.exp(lam * step)
    b_bar = ((lam_bar - 1.0) / lam)[..., None] * lax.complex(b_re, b_im)
    lr = jnp.real(lam_bar).reshape(ns, 1, gs * P)
    li = jnp.imag(lam_bar).reshape(ns, 1, gs * P)
    k = np.arange(ns)[:, None, None]
    oh = jnp.asarray((np.arange(gu)[None, :, None] == gs * (k % (gu // gs)) + np.arange(gs)[None, None, :]),
                     F32)
    bre = jnp.einsum('kgl,klph->kghlp', oh, jnp.real(b_bar).reshape(ns, gs, P, H)).reshape(ns, gu * H, gs * P)
    bim = jnp.einsum('kgl,klph->kghlp', oh, jnp.imag(b_bar).reshape(ns, gs, P, H)).reshape(ns, gu * H, gs * P)
    cre = jnp.einsum('kgl,klhp->klpgh', oh, c_re.reshape(ns, gs, H, P)).reshape(ns, gs * P, gu * H)
    cim = jnp.einsum('kgl,klhp->klpgh', oh, c_im.reshape(ns, gs, H, P)).reshape(ns, gs * P, gu * H)
    return lr, li, jnp.concatenate([bre, bim], axis=2), jnp.concatenate([cre, -cim], axis=1)


def _carry_tile(fr, fi, pr, pi, reverse):
    row = lax.broadcasted_iota(jnp.int32, fr.shape, 0)
    cr = jnp.zeros_like(fr)
    ci = jnp.zeros_like(fi)
    sr = jnp.zeros_like(fr[0:1])
    si = jnp.zeros_like(sr)
    order = range(SCAN_CHUNKS - 1, 0, -1) if reverse else range(0, SCAN_CHUNKS - 1)
    for c in order:
        fcr = jnp.sum(jnp.where(row == c, fr, 0.0), axis=0, keepdims=True)
        fci = jnp.sum(jnp.where(row == c, fi, 0.0), axis=0, keepdims=True)
        mr, mi = _cmul(pr, pi, sr, si)
        sr, si = mr + fcr, mi + fci
        nxt = c - 1 if reverse else c + 1
        cr = jnp.where(row == nxt, sr, cr)
        ci = jnp.where(row == nxt, si, ci)
    return cr, ci


def _scan_order_into(dst_ref, src_ref, T):
    for c in range(SCAN_CHUNKS):
        dst_ref[pl.ds(c, T, stride=SCAN_CHUNKS), :] = src_ref[pl.ds(c * T, T), :].astype(F32)


def _s5_fwd(proj4, lr, li, bmat, cmat, d, name):
    _, L, Du = proj4.shape
    ns, _, W2 = bmat.shape
    W = W2 // 2
    T = L // SCAN_CHUNKS
    rb = _pick(L, prefs=(512, 256, 128))
    per = (ns * LANES) // Du

    def body(ut_ref, lr_ref, li_ref, b_ref, c_ref, d_ref, y_ref, sr_ref, si_ref, u_ref):
        k = pl.program_id(0)
        _scan_order_into(u_ref, ut_ref, T)
        for r in range(L // rb):
            rows = pl.ds(r * rb, rb)
            bu = jnp.dot(u_ref[rows, :].astype(BF16), b_ref[...], preferred_element_type=F32)
            sr_ref[rows, :] = bu[:, :W]
            si_ref[rows, :] = bu[:, W:]
        lam_r = jnp.broadcast_to(lr_ref[...], (SUBLANES, W))
        lam_i = jnp.broadcast_to(li_ref[...], (SUBLANES, W))

        def local(t, carry):
            sr, si = carry
            rows = pl.ds(pl.multiple_of(t * SUBLANES, SUBLANES), SUBLANES)
            mr, mi = _cmul(lam_r, lam_i, sr, si)
            sr = mr + sr_ref[rows, :]
            si = mi + si_ref[rows, :]
            sr_ref[rows, :] = sr
            si_ref[rows, :] = si
            return sr, si

        z = jnp.zeros((SUBLANES, W), F32)
        fr, fi = lax.fori_loop(0, T, local, (z, z))
        pr, pi = _cpow(lam_r, lam_i, T)
        cr, ci = _carry_tile(fr, fi, pr[0:1], pi[0:1], reverse=False)

        def fix(t, carry):
            wr, wi = carry
            rows = pl.ds(pl.multiple_of(t * SUBLANES, SUBLANES), SUBLANES)
            ar, ai = _cmul(wr, wi, cr, ci)
            sr_ref[rows, :] += ar
            si_ref[rows, :] += ai
            return _cmul(wr, wi, lam_r, lam_i)

        lax.fori_loop(0, T, fix, (lam_r, lam_i))
        first = (k % per) == 0
        for r in range(L // rb):
            rows = pl.ds(r * rb, rb)
            s = jnp.concatenate([sr_ref[rows, :], si_ref[rows, :]], axis=1).astype(BF16)
            y = jnp.dot(s, c_ref[...], preferred_element_type=F32)

            @pl.when(first)
            def _():
                y_ref[rows, :] = y + d_ref[...] * u_ref[rows, :]

            @pl.when(jnp.logical_not(first))
            def _():
                y_ref[rows, :] += y

    ublk = pl.BlockSpec((L, LANES), lambda k: (0, k // per))
    sblk = pl.BlockSpec((L, W), lambda k: (0, k))
    lam = pl.BlockSpec((None, 1, W), lambda k: (k, 0, 0))
    return pl.pallas_call(
        body, name=name,
        out_shape=(jax.ShapeDtypeStruct((L, Du), F32), jax.ShapeDtypeStruct((L, ns * W), F32),
                   jax.ShapeDtypeStruct((L, ns * W), F32)),
        grid=(ns,),
        in_specs=[pl.BlockSpec((None, L, LANES), lambda k: (3, 0, k // per)), lam, lam,
                  pl.BlockSpec((None, LANES, 2 * W), lambda k: (k, 0, 0)),
                  pl.BlockSpec((None, 2 * W, LANES), lambda k: (k, 0, 0)),
                  pl.BlockSpec((1, LANES), lambda k: (0, k // per))],
        out_specs=(ublk, sblk, sblk), scratch_shapes=[pltpu.VMEM((L, LANES), F32)],
        compiler_params=_cparams(("arbitrary",), VMEM_LIMIT_S5),
    )(proj4, lr, li, bmat.astype(BF16), cmat.astype(BF16), d.reshape(1, Du))


def _s5_bwd(dy, proj4, dproj, s_re, s_im, lr, li, bmat, cmat, d, name):
    _, L, Du = proj4.shape
    ns, _, W2 = bmat.shape
    W = W2 // 2
    T = L // SCAN_CHUNKS
    rb = _pick(L, prefs=(512, 256, 128))
    per = (ns * LANES) // Du
    NT = (((1,), (1,)), ((), ()))
    TN = (((0,), (0,)), ((), ()))

    def body(dy_ref, ut_ref, dp_in, sr_ref, si_ref, lr_ref, li_ref, b_ref, c_ref, d_ref,
             dut_ref, db_ref, dc_ref, dl_ref, dd_ref, gr_ref, gi_ref, u_ref, du_ref):
        k = pl.program_id(0)
        _scan_order_into(u_ref, ut_ref, T)
        for r in range(L // rb):
            rows = pl.ds(r * rb, rb)
            g = lax.dot_general(dy_ref[rows, :].astype(BF16), c_ref[...], NT, preferred_element_type=F32)
            gr_ref[rows, :] = g[:, :W]
            gi_ref[rows, :] = g[:, W:]
        lam_r = jnp.broadcast_to(lr_ref[...], (SUBLANES, W))
        lam_i = -jnp.broadcast_to(li_ref[...], (SUBLANES, W))

        def local(i, carry):
            gr, gi = carry
            rows = pl.ds(pl.multiple_of((T - 1 - i) * SUBLANES, SUBLANES), SUBLANES)
            mr, mi = _cmul(lam_r, lam_i, gr, gi)
            gr = mr + gr_ref[rows, :]
            gi = mi + gi_ref[rows, :]
            gr_ref[rows, :] = gr
            gi_ref[rows, :] = gi
            return gr, gi

        z = jnp.zeros((SUBLANES, W), F32)
        fr, fi = lax.fori_loop(0, T, local, (z, z))
        pr, pi = _cpow(lam_r, lam_i, T)
        cr, ci = _carry_tile(fr, fi, pr[0:1], pi[0:1], reverse=True)

        def true_g(rows, wr, wi):
            ar, ai = _cmul(wr, wi, cr, ci)
            gr = gr_ref[rows, :] + ar
            gi = gi_ref[rows, :] + ai
            gr_ref[rows, :] = gr
            gi_ref[rows, :] = gi
            return gr, gi

        def fix(i, carry):
            wr, wi, ar_, ai_ = carry
            t = T - 1 - i
            rows = pl.ds(pl.multiple_of(t * SUBLANES, SUBLANES), SUBLANES)
            prev = pl.ds(pl.multiple_of((t - 1) * SUBLANES, SUBLANES), SUBLANES)
            gr, gi = true_g(rows, wr, wi)
            qr, qi = sr_ref[prev, :], si_ref[prev, :]
            ar_ = ar_ + gr * qr + gi * qi
            ai_ = ai_ + gi * qr - gr * qi
            wr, wi = _cmul(wr, wi, lam_r, lam_i)
            return wr, wi, ar_, ai_

        wr, wi, acc_r, acc_i = lax.fori_loop(0, T - 1, fix, (lam_r, lam_i, z, z))
        gr, gi = true_g(pl.ds(0, SUBLANES), wr, wi)
        last = pl.ds((T - 1) * SUBLANES, SUBLANES)
        row = lax.broadcasted_iota(jnp.int32, (SUBLANES, W), 0)
        qr = jnp.where(row >= 1, pltpu.roll(sr_ref[last, :], 1, axis=0), 0.0)
        qi = jnp.where(row >= 1, pltpu.roll(si_ref[last, :], 1, axis=0), 0.0)
        acc_r = acc_r + gr * qr + gi * qi
        acc_i = acc_i + gi * qr - gr * qi
        dl_ref[0:1, :] = jnp.sum(acc_r, axis=0, keepdims=True)
        dl_ref[1:2, :] = jnp.sum(acc_i, axis=0, keepdims=True)

        first = (k % per) == 0
        db = jnp.zeros((LANES, 2 * W), F32)
        dc = jnp.zeros((LANES, 2 * W), F32)
        dd = jnp.zeros((1, LANES), F32)
        for r in range(L // rb):
            rows = pl.ds(r * rb, rb)
            gb = jnp.concatenate([gr_ref[rows, :], gi_ref[rows, :]], axis=1).astype(BF16)
            sb = jnp.concatenate([sr_ref[rows, :], si_ref[rows, :]], axis=1).astype(BF16)
            dyv = dy_ref[rows, :]
            uv = u_ref[rows, :]
            du = lax.dot_general(gb, b_ref[...], NT, preferred_element_type=F32)
            db = db + lax.dot_general(uv.astype(BF16), gb, TN, preferred_element_type=F32)
            dc = dc + lax.dot_general(dyv.astype(BF16), sb, TN, preferred_element_type=F32)
            dd = dd + jnp.sum(dyv * uv, axis=0, keepdims=True)

            @pl.when(first)
            def _():
                du_ref[rows, :] = du + d_ref[...] * dyv

            @pl.when(jnp.logical_not(first))
            def _():
                du_ref[rows, :] += du

        db_ref[...] = db
        dc_ref[...] = dc

        @pl.when(first)
        def _():
            dd_ref[...] = dd

        @pl.when((k % per) == per - 1)
        def _():
            for c in range(SCAN_CHUNKS):
                dut_ref[pl.ds(c * T, T), :] = du_ref[pl.ds(c, T, stride=SCAN_CHUNKS), :].astype(BF16)

    ublk = pl.BlockSpec((L, LANES), lambda k: (0, k // per))
    uslab = pl.BlockSpec((None, L, LANES), lambda k: (3, 0, k // per))
    sblk = pl.BlockSpec((L, W), lambda k: (0, k))
    lam = pl.BlockSpec((None, 1, W), lambda k: (k, 0, 0))
    vec = pl.BlockSpec((1, LANES), lambda k: (0, k // per))
    mat = pl.BlockSpec((None, LANES, 2 * W), lambda k: (k, 0, 0))
    return pl.pallas_call(
        body, name=name,
        out_shape=(jax.ShapeDtypeStruct(dproj.shape, dproj.dtype), jax.ShapeDtypeStruct((ns, LANES, 2 * W), F32),
                   jax.ShapeDtypeStruct((ns, LANES, 2 * W), F32), jax.ShapeDtypeStruct((ns, 2, W), F32),
                   jax.ShapeDtypeStruct((1, Du), F32)),
        grid=(ns,),
        in_specs=[ublk, uslab, pl.BlockSpec(memory_space=pl.ANY), sblk, sblk, lam, lam, mat,
                  pl.BlockSpec((None, 2 * W, LANES), lambda k: (k, 0, 0)), vec],
        out_specs=(uslab, mat, mat, pl.BlockSpec((None, 2, W), lambda k: (k, 0, 0)), vec),
        scratch_shapes=[pltpu.VMEM((L, W), F32), pltpu.VMEM((L, W), F32), pltpu.VMEM((L, LANES), F32),
                        pltpu.VMEM((L, LANES), F32)],
        input_output_aliases={2: 0}, compiler_params=_cparams(("arbitrary",), VMEM_LIMIT_S5),
    )(dy, proj4, dproj, s_re, s_im, lr, li, bmat.astype(BF16), cmat.astype(BF16), d.reshape(1, Du))


def _glu_fwd(yraw, wmat, bias, mixin, name):
    L, C = yraw.shape
    tr = _pick(L, prefs=(512, 256, 128))
    tb = tr // SCAN_CHUNKS
    nl = C // LANES

    def body(y_ref, w_ref, b_ref, m_in, o_ref, scr):
        yg = _gelu(y_ref[...])
        zz = jnp.dot(yg.astype(BF16), w_ref[...], preferred_element_type=F32) + b_ref[...]
        yb = yg * _sigmoid(zz)
        for k in range(nl):
            scr[k] = yb[:, k * LANES:(k + 1) * LANES]
        for c in range(SCAN_CHUNKS):
            for k in range(nl):
                o_ref[c, :, k * LANES:(k + 1) * LANES] = scr[k, pl.ds(c, tb, stride=SCAN_CHUNKS), :].astype(BF16)

    out = pl.pallas_call(
        body, name=name, out_shape=jax.ShapeDtypeStruct((SCAN_CHUNKS, L // SCAN_CHUNKS, 2 * C), BF16),
        grid=(L // tr,),
        in_specs=[pl.BlockSpec((tr, C), lambda i: (i, 0)), pl.BlockSpec((C, C), lambda i: (0, 0)),
                  pl.BlockSpec((1, C), lambda i: (0, 0)), pl.BlockSpec(memory_space=pl.ANY)],
        out_specs=pl.BlockSpec((SCAN_CHUNKS, tb, C), lambda i: (0, i, 1)),
        scratch_shapes=[pltpu.VMEM((nl, tr, LANES), F32)], input_output_aliases={3: 0},
        compiler_params=_cparams(("parallel",)),
    )(yraw, wmat, bias.reshape(1, C), mixin.reshape(SCAN_CHUNKS, L // SCAN_CHUNKS, 2 * C))
    return out.reshape(L, 2 * C)


def _glu_bwd(yraw, dmix, wmat, bias, name):
    L, C = yraw.shape
    tr = _pick(L, prefs=(512, 256, 128))
    nsteps = L // tr
    tb = tr // SCAN_CHUNKS
    nl = C // LANES

    def body(y_ref, d_ref, w_ref, b_ref, dy_ref, dw_ref, db_ref, acc_b, scr):
        i = pl.program_id(0)

        @pl.when(i == 0)
        def _():
            dw_ref[...] = jnp.zeros_like(dw_ref)
            acc_b[...] = jnp.zeros_like(acc_b)

        for c in range(SCAN_CHUNKS):
            for k in range(nl):
                scr[k, pl.ds(c, tb, stride=SCAN_CHUNKS), :] = d_ref[c, :, k * LANES:(k + 1) * LANES]
        yr = y_ref[...]
        yg = _gelu(yr)
        ygb = yg.astype(BF16)
        sg = _sigmoid(jnp.dot(ygb, w_ref[...], preferred_element_type=F32) + b_ref[...])
        dyb_ = jnp.concatenate([scr[k] for k in range(nl)], axis=1)
        dz = dyb_ * yg * sg * (1.0 - sg)
        dzb = dz.astype(BF16)
        dyg = dyb_ * sg + lax.dot_general(dzb, w_ref[...], (((1,), (1,)), ((), ())), preferred_element_type=F32)
        dw_ref[...] += lax.dot_general(ygb, dzb, (((0,), (0,)), ((), ())), preferred_element_type=F32)
        acc_b[...] += jnp.sum(dz.reshape(tr // SUBLANES, SUBLANES, C), axis=0)
        dy_ref[...] = dyg * _gelu_grad(yr)

        @pl.when(i == nsteps - 1)
        def _():
            db_ref[...] = jnp.sum(acc_b[...], axis=0, keepdims=True)

    row = pl.BlockSpec((tr, C), lambda i: (i, 0))
    return pl.pallas_call(
        body, name=name,
        out_shape=(jax.ShapeDtypeStruct((L, C), F32), jax.ShapeDtypeStruct((C, C), F32),
                   jax.ShapeDtypeStruct((1, C), F32)),
        grid=(nsteps,),
        in_specs=[row, pl.BlockSpec((SCAN_CHUNKS, tb, C), lambda i: (0, i, 1)), pl.BlockSpec((C, C), lambda i: (0, 0)),
                  pl.BlockSpec((1, C), lambda i: (0, 0))],
        out_specs=(row, pl.BlockSpec((C, C), lambda i: (0, 0)), pl.BlockSpec((1, C), lambda i: (0, 0))),
        scratch_shapes=[pltpu.VMEM((SUBLANES, C), F32), pltpu.VMEM((nl, tr, LANES), F32)],
        compiler_params=_cparams(("arbitrary",)),
    )(yraw, dmix.reshape(SCAN_CHUNKS, L // SCAN_CHUNKS, 2 * C), wmat, bias.reshape(1, C))


def _pool_counts(L, g):
    t = lax.broadcasted_iota(jnp.int32, (L, LANES), 0).astype(F32) + 1.0
    w = jnp.where(g == 0, 2.0, jnp.where(g == 1, 4.0, jnp.where(g == 2, 8.0, 16.0)))
    return 1.0 / jnp.minimum(t, w)


def _select_window(g, a2, a4, a8, a16):
    return jnp.where(g == 0, a2, jnp.where(g == 1, a4, jnp.where(g == 2, a8, a16)))


def _pooled(z, g):
    a2 = z + _down(z, 1)
    a4 = a2 + _down(a2, 2)
    a8 = a4 + _down(a4, 4)
    a16 = a8 + _down(a8, 8)
    return _select_window(g, a2, a4, a8, a16) * _pool_counts(z.shape[0], g) - z


def _transpose_on_mxu(yb):
    c = yb.shape[1]
    eye = lax.broadcasted_iota(jnp.int32, (c, c), 0) == lax.broadcasted_iota(jnp.int32, (c, c), 1)
    return lax.dot_general(eye.astype(BF16), yb, (((1,), (1,)), ((), ())), preferred_element_type=F32).astype(BF16)


def _pool_fwd(proj3, pool_w, scale, name):
    _, L, C = proj3.shape
    ng = len(POOL_WINDOWS)
    pg = C // ng
    assert pg == LANES

    def body(z_ref, w_ref, s_ref, o_ref, ot_ref):
        g = pl.program_id(0)
        p = _pooled(z_ref[...].astype(F32), g)
        y = jnp.dot(p.astype(BF16), w_ref[...].astype(BF16), preferred_element_type=F32)
        yb = (y * s_ref[...]).astype(BF16)
        o_ref[...] = yb
        ot_ref[...] = _transpose_on_mxu(yb)

    return pl.pallas_call(
        body, name=name, out_shape=(jax.ShapeDtypeStruct((L, 2 * C), BF16), jax.ShapeDtypeStruct((2 * C, L), BF16)),
        grid=(ng,),
        in_specs=[pl.BlockSpec((None, L, pg), lambda g: (0, 0, g)), pl.BlockSpec((None, pg, pg), lambda g: (g, 0, 0)),
                  pl.BlockSpec((1, pg), lambda g: (0, g))],
        out_specs=(pl.BlockSpec((L, pg), lambda g: (0, g)), pl.BlockSpec((pg, L), lambda g: (g, 0))),
        compiler_params=_cparams(("parallel",)),
    )(proj3, pool_w, scale.reshape(1, C))


def _pool_bwd(proj3, dmix, pool_w, scale, name):
    _, L, C = proj3.shape
    ng = len(POOL_WINDOWS)
    pg = C // ng

    def body(z_ref, d_ref, w_ref, s_ref, dz_ref, dw_ref, ds_ref):
        g = pl.program_id(0)
        p = _pooled(z_ref[...].astype(F32), g)
        pb = p.astype(BF16)
        wb = w_ref[...].astype(BF16)
        pre = jnp.dot(pb, wb, preferred_element_type=F32)
        dyc = d_ref[...]
        ds_ref[...] = jnp.sum(dyc * pre, axis=0, keepdims=True)
        dpre = (dyc * s_ref[...]).astype(BF16)
        dw_ref[...] = lax.dot_general(pb, dpre, (((0,), (0,)), ((), ())), preferred_element_type=F32)
        dp = lax.dot_general(dpre, wb, (((1,), (1,)), ((), ())), preferred_element_type=F32)
        v = dp * _pool_counts(L, g)
        a2 = v + _up(v, 1)
        a4 = a2 + _up(a2, 2)
        a8 = a4 + _up(a4, 4)
        a16 = a8 + _up(a8, 8)
        dz_ref[...] = (_select_window(g, a2, a4, a8, a16) - dp).astype(BF16)

    return pl.pallas_call(
        body, name=name,
        out_shape=(jax.ShapeDtypeStruct((L, C), BF16), jax.ShapeDtypeStruct((ng, pg, pg), F32),
                   jax.ShapeDtypeStruct((1, C), F32)),
        grid=(ng,),
        in_specs=[pl.BlockSpec((None, L, pg), lambda g: (0, 0, g)), pl.BlockSpec((L, pg), lambda g: (0, g)),
                  pl.BlockSpec((None, pg, pg), lambda g: (g, 0, 0)), pl.BlockSpec((1, pg), lambda g: (0, g))],
        out_specs=(pl.BlockSpec((L, pg), lambda g: (0, g)), pl.BlockSpec((None, pg, pg), lambda g: (g, 0, 0)),
                   pl.BlockSpec((1, pg), lambda g: (0, g))),
        compiler_params=_cparams(("parallel",)),
    )(proj3, dmix, pool_w, scale.reshape(1, C))


def _tril_w(w_ref, h):
    r = lax.broadcasted_iota(jnp.int32, (CHUNK, CHUNK), 0)
    c = lax.broadcasted_iota(jnp.int32, (CHUNK, CHUNK), 1)
    return jnp.where(r >= c, w_ref[h], 0.0)


def _sgu_fwd(proj3, norm_g, w, b, mixin, mixin_t, name):
    _, L, C = proj3.shape
    nh = w.shape[0]
    dh = C // nh
    assert dh == LANES and w.shape[1] == CHUNK
    tr = _pick(L, prefs=(512, 256, 128))
    bfull = jnp.broadcast_to(b[:, :, None], (nh, CHUNK, dh))

    def body(su_ref, sv_ref, g_ref, w_ref, b_ref, m_in, mt_in, o_ref, ot_ref):
        sv = _gelu(sv_ref[...].astype(F32))
        r = lax.rsqrt(jnp.mean(sv * sv, axis=-1, keepdims=True) + EPS)
        v = (sv * r * g_ref[...]).astype(BF16)
        for h in range(nh):
            wm = _tril_w(w_ref, h).astype(BF16)
            cols = slice(h * dh, (h + 1) * dh)
            for n in range(tr // CHUNK):
                rows = slice(n * CHUNK, (n + 1) * CHUNK)
                mixed = jnp.dot(wm, v[rows, cols], preferred_element_type=F32) + b_ref[h]
                o_ref[rows, cols] = (_gelu(su_ref[rows, cols].astype(F32)) * mixed).astype(BF16)
        ot_ref[...] = _transpose_on_mxu(o_ref[...])

    full = lambda shp: pl.BlockSpec(shp, lambda i: (0,) * len(shp))
    anywhere = pl.BlockSpec(memory_space=pl.ANY)
    return pl.pallas_call(
        body, name=name, out_shape=(jax.ShapeDtypeStruct(mixin.shape, BF16), jax.ShapeDtypeStruct(mixin_t.shape, BF16)),
        grid=(L // tr,),
        in_specs=[pl.BlockSpec((None, tr, C), lambda i: (1, i, 0)), pl.BlockSpec((None, tr, C), lambda i: (2, i, 0)),
                  full((1, C)), full((nh, CHUNK, CHUNK)), full((nh, CHUNK, dh)), anywhere, anywhere],
        out_specs=(pl.BlockSpec((tr, C), lambda i: (i, 1)), pl.BlockSpec((C, tr), lambda i: (1, i))),
        input_output_aliases={5: 0, 6: 1}, compiler_params=_cparams(("parallel",)),
    )(proj3, proj3, norm_g.reshape(1, C), w, bfull, mixin, mixin_t)


def _sgu_bwd(proj3, dmix, dz, norm_g, w, b, name):
    _, L, C = proj3.shape
    nh = w.shape[0]
    dh = C // nh
    tr = _pick(L, prefs=(512, 256, 128))
    nsteps = L // tr
    bfull = jnp.broadcast_to(b[:, :, None], (nh, CHUNK, dh))

    def body(su_ref, sv_ref, d_ref, dz_ref, g_ref, w_ref, b_ref, o_ref, dw_ref, db_ref, dg_ref, dv_ref, acc_g):
        i = pl.program_id(0)
        o_ref[0] = dz_ref[...]

        @pl.when(i == 0)
        def _():
            dw_ref[...] = jnp.zeros_like(dw_ref)
            db_ref[...] = jnp.zeros_like(db_ref)
            acc_g[...] = jnp.zeros_like(acc_g)

        svp = sv_ref[...].astype(F32)
        sv = _gelu(svp)
        r = lax.rsqrt(jnp.mean(sv * sv, axis=-1, keepdims=True) + EPS)
        vh = sv * r
        gv = g_ref[...]
        v = (vh * gv).astype(BF16)
        tri_r = lax.broadcasted_iota(jnp.int32, (CHUNK, CHUNK), 0)
        tri_c = lax.broadcasted_iota(jnp.int32, (CHUNK, CHUNK), 1)
        for h in range(nh):
            wm = _tril_w(w_ref, h).astype(BF16)
            cols = slice(h * dh, (h + 1) * dh)
            dwh = jnp.zeros((CHUNK, CHUNK), F32)
            dbh = jnp.zeros((CHUNK, dh), F32)
            for n in range(tr // CHUNK):
                rows = slice(n * CHUNK, (n + 1) * CHUNK)
                vb = v[rows, cols]
                mixed = jnp.dot(wm, vb, preferred_element_type=F32) + b_ref[h]
                sup = su_ref[rows, cols].astype(F32)
                dyd = d_ref[rows, cols]
                dmx = dyd * _gelu(sup)
                o_ref[1, rows, cols] = (dyd * mixed * _gelu_grad(sup)).astype(BF16)
                dmb = dmx.astype(BF16)
                dwh = dwh + lax.dot_general(dmb, vb, (((1,), (1,)), ((), ())), preferred_element_type=F32)
                dbh = dbh + dmx
                dv_ref[rows, cols] = lax.dot_general(wm, dmb, (((0,), (0,)), ((), ())), preferred_element_type=F32)
            dw_ref[h] += jnp.where(tri_r >= tri_c, dwh, 0.0)
            db_ref[h] += dbh
        dv = dv_ref[...]
        acc_g[...] += jnp.sum((dv * vh).reshape(tr // SUBLANES, SUBLANES, C), axis=0)
        dvg = dv * gv
        dsv = r * (dvg - vh * jnp.mean(dvg * vh, axis=-1, keepdims=True))
        o_ref[2] = (dsv * _gelu_grad(svp)).astype(BF16)

        @pl.when(i == nsteps - 1)
        def _():
            dg_ref[...] = jnp.sum(acc_g[...], axis=0, keepdims=True)

    full = lambda shp: pl.BlockSpec(shp, lambda i: (0,) * len(shp))
    return pl.pallas_call(
        body, name=name,
        out_shape=(jax.ShapeDtypeStruct((3, L, C), BF16), jax.ShapeDtypeStruct((nh, CHUNK, CHUNK), F32),
                   jax.ShapeDtypeStruct((nh, CHUNK, dh), F32), jax.ShapeDtypeStruct((1, C), F32)),
        grid=(nsteps,),
        in_specs=[pl.BlockSpec((None, tr, C), lambda i: (1, i, 0)), pl.BlockSpec((None, tr, C), lambda i: (2, i, 0)),
                  pl.BlockSpec((tr, C), lambda i: (i, 1)), pl.BlockSpec((tr, C), lambda i: (i, 0)), full((1, C)),
                  full((nh, CHUNK, CHUNK)), full((nh, CHUNK, dh))],
        out_specs=(pl.BlockSpec((3, tr, C), lambda i: (0, i, 0)), full((nh, CHUNK, CHUNK)), full((nh, CHUNK, dh)),
                   full((1, C))),
        scratch_shapes=[pltpu.VMEM((tr, C), F32), pltpu.VMEM((SUBLANES, C), F32)],
        compiler_params=_cparams(("arbitrary",)),
    )(proj3, proj3, dmix, dz, norm_g.reshape(1, C), w, bfull)


def _ffn_act_fwd(up3, conv_w, conv_b, name):
    _, L, Fh = up3.shape
    cb = LANES
    w2 = conv_w.reshape(3, 2, Fh).transpose(1, 0, 2)
    b2 = conv_b.reshape(2, 1, Fh)

    def body(u_ref, w_ref, b_ref, o_ref, ot_ref, gv_ref):
        g = _conv3(_taps(u_ref[0].astype(F32)), w_ref[0]) + b_ref[0]
        v = _conv3(_taps(u_ref[1].astype(F32)), w_ref[1]) + b_ref[1]
        gv_ref[0] = g.astype(BF16)
        gv_ref[1] = v.astype(BF16)
        ab = (g * _sigmoid(g) * v).astype(BF16)
        o_ref[...] = ab
        ot_ref[...] = _transpose_on_mxu(ab)

    blk3 = pl.BlockSpec((2, L, cb), lambda j: (0, 0, j))
    return pl.pallas_call(
        body, name=name,
        out_shape=(jax.ShapeDtypeStruct((L, Fh), BF16), jax.ShapeDtypeStruct((Fh, L), BF16),
                   jax.ShapeDtypeStruct((2, L, Fh), BF16)),
        grid=(Fh // cb,),
        in_specs=[blk3, pl.BlockSpec((2, 3, cb), lambda j: (0, 0, j)), pl.BlockSpec((2, 1, cb), lambda j: (0, 0, j))],
        out_specs=(pl.BlockSpec((L, cb), lambda j: (0, j)), pl.BlockSpec((cb, L), lambda j: (j, 0)), blk3),
        compiler_params=_cparams(("parallel",)),
    )(up3, w2, b2)


def _ffn_act_bwd(up3, gv3, da, conv_w, h2t, name):
    _, L, Fh = up3.shape
    D = h2t.shape[0]
    cb = LANES
    nb = Fh // cb
    w2 = conv_w.reshape(3, 2, Fh).transpose(1, 0, 2)

    def body(u_ref, gv_ref, d_ref, w_ref, h_ref, o_ref, dw_ref, db_ref, wg_ref, wv_ref, scr):
        j = pl.program_id(0)

        @pl.when(j == 0)
        def _():
            scr[1] = jnp.zeros((2, L, cb), BF16)

        prev = scr.at[(j + 1) % 2]
        wg_ref[...] = jnp.dot(h_ref[...], prev[0], preferred_element_type=F32).astype(BF16)
        wv_ref[...] = jnp.dot(h_ref[...], prev[1], preferred_element_type=F32).astype(BF16)
        tg, tv = _taps(u_ref[0].astype(F32)), _taps(u_ref[1].astype(F32))
        wg, wv = w_ref[0], w_ref[1]
        g = gv_ref[0].astype(F32)
        v = gv_ref[1].astype(F32)
        sg = _sigmoid(g)
        dav = d_ref[...].astype(F32)
        dg = dav * v * (sg * (1.0 + g * (1.0 - sg)))
        dv = dav * (g * sg)
        dug = _conv3_t(dg, wg).astype(BF16)
        duv = _conv3_t(dv, wv).astype(BF16)
        o_ref[0] = dug
        o_ref[1] = duv
        cur = scr.at[j % 2]
        cur[0] = dug
        cur[1] = duv
        for tap, (dwg, dwv) in enumerate(zip(_conv3_dw(dg, tg), _conv3_dw(dv, tv))):
            dw_ref[0, tap:tap + 1, :] = dwg
            dw_ref[1, tap:tap + 1, :] = dwv
        db_ref[0] = jnp.sum(dg, axis=0, keepdims=True)
        db_ref[1] = jnp.sum(dv, axis=0, keepdims=True)

    here = lambda j: jnp.minimum(j, nb - 1)
    before = lambda j: jnp.maximum(j - 1, 0)
    blk3 = pl.BlockSpec((2, L, cb), lambda j: (0, 0, here(j)))
    dup, dw2, db2, dwg, dwv = pl.pallas_call(
        body, name=name,
        out_shape=(jax.ShapeDtypeStruct((2, L, Fh), BF16), jax.ShapeDtypeStruct((2, 3, Fh), F32),
                   jax.ShapeDtypeStruct((2, 1, Fh), F32), jax.ShapeDtypeStruct((D, Fh), BF16),
                   jax.ShapeDtypeStruct((D, Fh), BF16)),
        grid=(nb + 1,),
        in_specs=[blk3, blk3, pl.BlockSpec((L, cb), lambda j: (0, here(j))),
                  pl.BlockSpec((2, 3, cb), lambda j: (0, 0, here(j))), pl.BlockSpec((D, L), lambda j: (0, 0))],
        out_specs=(blk3, pl.BlockSpec((2, 3, cb), lambda j: (0, 0, here(j))),
                   pl.BlockSpec((2, 1, cb), lambda j: (0, 0, here(j))),
                   pl.BlockSpec((D, cb), lambda j: (0, before(j))), pl.BlockSpec((D, cb), lambda j: (0, before(j)))),
        scratch_shapes=[pltpu.VMEM((2, 2, L, cb), BF16)],
        compiler_params=_cparams(("arbitrary",), VMEM_LIMIT_S5),
    )(up3, gv3, da, w2, h2t)
    return dup, dw2.transpose(1, 0, 2).reshape(3, 2 * Fh), db2.reshape(2 * Fh), jnp.concatenate([dwg, dwv], axis=1)


def _local_step(x, tgt, w, layer_weights, on_layer_grads):
    L, D = x.shape
    depth = w['norm_mix_g'].shape[0]
    saved = []
    for i in range(depth):
        j = i // 2
        wb = dict(layer_weights(2 * i, x))
        s = {'x': x, 'wb': wb}
        if i % 2 == 0:
            proj4, s['hT'] = _norm_mm(x, w['norm_mix_g'][i], wb['even_w_in'], BF16, "even_in_fwd", ok=('seg', 4))
            s['proj'] = proj4
            mixin = _sconv_fwd(proj4, w['even_conv_w'][j], "sconv_fwd")
            prm = (w['ssm_log_step'][j], w['ssm_a_re'][j], w['ssm_a_im'][j], w['ssm_b_re'][j], w['ssm_b_im'][j],
                   w['ssm_c_re'][j], w['ssm_c_im'][j])
            (lr, li, bmat, cmat), prep_vjp = jax.vjp(_s5_prep, *prm)
            yraw, s_re, s_im = _s5_fwd(proj4, lr, li, bmat, cmat, w['ssm_d'][j], "s5_fwd")
            mixin = _glu_fwd(yraw, wb['ssm_glu_w'], w['ssm_glu_b'][j], mixin, "glu_fwd")
            s.update(yraw=yraw, s_re=s_re, s_im=s_im, s5=(lr, li, bmat, cmat), prep_vjp=prep_vjp)
            s['mixinT'] = mixin.T
            x = _mm(mixin, wb['even_w_out'], 'nn', F32, "even_out_fwd", res=x)
        else:
            proj3, s['hT'] = _norm_mm(x, w['norm_mix_g'][i], wb['odd_w_in'], BF16, "odd_in_fwd", ok=('seg', 3))
            s['proj'] = proj3
            mixin, mixin_t = _pool_fwd(proj3, w['pool_w'][j], w['pool_scale'][j], "pool_fwd")
            mixin, s['mixinT'] = _sgu_fwd(proj3, w['sgu_norm_g'][j], w['sgu_w'][j], w['sgu_b'][j], mixin, mixin_t,
                                          "sgu_fwd")
            x = _mm(mixin, wb['odd_w_out'], 'nn', F32, "odd_out_fwd", res=x)
        s['x1'] = x
        wb.update(layer_weights(2 * i + 1, x))
        up3, h2t = _norm_mm(x, w['norm_ffn_g'][i], wb['ffn_w_up'], BF16, "ffn_up_fwd", ok=('seg', 2))
        a, at, gv3 = _ffn_act_fwd(up3, w['ffn_conv_w'][i], w['ffn_conv_b'][i], "ffn_act_fwd")
        x = _mm(a, wb['ffn_w_down'], 'nn', F32, "ffn_down_fwd", res=x)
        s.update(h2T=h2t, up3=up3, aT=at, gv3=gv3)
        saved.append(s)

    loss8, dx, dxb, dg_final = _loss_head(x, w['norm_final_g'], tgt)
    gs = {n: [None] * w[n].shape[0] for n in SMALL if n != 'norm_final_g'}
    gs['norm_final_g'] = dg_final.reshape(D)

    dep = None
    for i in reversed(range(depth)):
        j = i // 2
        s = saved[i]
        wb = s['wb']
        gb = {}
        da = _mm(dxb, wb['ffn_w_down'], 'nt', BF16, "ffn_down_dgrad", dep=dep)
        gb['ffn_w_down'] = _mm(s['aT'], dxb, 'nn', BF16, "ffn_down_wgrad")
        dup3, dcw, dcb, gb['ffn_w_up'] = _ffn_act_bwd(s['up3'], s['gv3'], da, w['ffn_conv_w'][i], s['h2T'],
                                                      "ffn_act_bwd")
        gs['ffn_conv_w'][i], gs['ffn_conv_b'][i] = dcw, dcb
        dep = on_layer_grads(2 * i + 1, gb)
        dx, dxb, dg = _mm_norm_bwd(dup3, wb['ffn_w_up'], s['x1'], w['norm_ffn_g'][i], dx, "ffn_up_dgrad",
                              ak=('seg', 2), dep=dep)
        gs['norm_ffn_g'][i] = dg.reshape(D)
        gb = {}
        if i % 2 == 0:
            dmix = _mm(dxb, wb['even_w_out'], 'nt', F32, "even_out_dgrad")
            gb['even_w_out'] = _mm(s['mixinT'], dxb, 'nn', BF16, "even_out_wgrad")
            dproj, dcw = _sconv_bwd(s['proj'], dmix, w['even_conv_w'][j], "sconv_bwd")
            gs['even_conv_w'][j] = dcw
            dyraw, dglu_w, dglu_b = _glu_bwd(s['yraw'], dmix, wb['ssm_glu_w'], w['ssm_glu_b'][j], "glu_bwd")
            gb['ssm_glu_w'] = dglu_w.astype(BF16)
            gs['ssm_glu_b'][j] = dglu_b.reshape(-1)
            lr, li, bmat, cmat = s['s5']
            dproj, dbm, dcm, dlam, dd = _s5_bwd(dyraw, s['proj'], dproj, s['s_re'], s['s_im'], lr, li, bmat, cmat,
                                               w['ssm_d'][j], "s5_bwd")
            gs['ssm_d'][j] = dd.reshape(-1)
            dcm = jnp.swapaxes(dcm, 1, 2)
            dprm = s['prep_vjp']((dlam[:, 0:1, :], dlam[:, 1:2, :], dbm, dcm))
            for n, gval in zip(('ssm_log_step', 'ssm_a_re', 'ssm_a_im', 'ssm_b_re', 'ssm_b_im', 'ssm_c_re',
                                'ssm_c_im'), dprm):
                gs[n][j] = gval
            gb['even_w_in'] = _mm(s['hT'], dproj, 'nn', BF16, "even_in_wgrad", bk=('seg', 4))
            w_in, in_kind, in_name = wb['even_w_in'], ('seg', 4), "even_in_dgrad"
        else:
            dmix = _mm(dxb, wb['odd_w_out'], 'nt', F32, "odd_out_dgrad")
            gb['odd_w_out'] = _mm(s['mixinT'], dxb, 'nn', BF16, "odd_out_wgrad")
            dz, dpw, dps = _pool_bwd(s['proj'], dmix, w['pool_w'][j], w['pool_scale'][j], "pool_bwd")
            gs['pool_w'][j], gs['pool_scale'][j] = dpw, dps.reshape(-1)
            dproj, dsw, dsb, dsg = _sgu_bwd(s['proj'], dmix, dz, w['sgu_norm_g'][j], w['sgu_w'][j], w['sgu_b'][j],
                                            "sgu_bwd")
            gs['sgu_w'][j], gs['sgu_b'][j], gs['sgu_norm_g'][j] = dsw, jnp.sum(dsb, axis=-1), dsg.reshape(-1)
            gb['odd_w_in'] = _mm(s['hT'], dproj, 'nn', BF16, "odd_in_wgrad", bk=('seg', 3))
            w_in, in_kind, in_name = wb['odd_w_in'], ('seg', 3), "odd_in_dgrad"
        dep = on_layer_grads(2 * i, gb)
        dx, dxb, dg = _mm_norm_bwd(dproj, w_in, s['x'], w['norm_mix_g'][i], dx, in_name, ak=in_kind, dep=dep)
        gs['norm_mix_g'][i] = dg.reshape(D)

    gsmall = {n: (v if n == 'norm_final_g' else jnp.stack(v)) for n, v in gs.items()}
    return loss8[0, 0], dx, gsmall


_HBM = pl.BlockSpec(memory_space=pltpu.HBM)
_CHIP_FLIPS = ((0, 0), (1, 0), (0, 1), (1, 1))


def _coords():
    return lax.axis_index("x"), lax.axis_index("y"), lax.axis_index("c")


def _flip(v, f):
    return 1 - v if f else v


def _shard_of(ref, axis, s, width):
    start = pl.multiple_of(s * width, LANES if axis == ref.ndim - 1 else 16) if width % 16 == 0 else s * width
    idx = [slice(None)] * ref.ndim
    idx[axis] = pl.ds(start, width)
    return ref.at[tuple(idx)]


_SEM = pl.BlockSpec(memory_space=pltpu.SEMAPHORE)
_ANY = pl.BlockSpec(memory_space=pl.ANY)
_DATAFLOW = pltpu.SideEffectType.DATAFLOW_SIDE_EFFECTING


def _in_hbm(a):
    return pltpu.with_memory_space_constraint(a, pltpu.HBM)


def _model_layer(name, l):
    if name.startswith('ffn'):
        return l
    return 2 * l + 1 if name.startswith('odd') else 2 * l


def _place_quarter(shard, l, axis, chip, dtype, dep=None):
    _, r, c = shard.shape
    tr = _pick(r, prefs=(512, 256, 128, 64, 32, 16))
    nrb = r // tr

    def body(chip_ref, i_ref, *rest):
        rest[-1][...] = i_ref[...].astype(dtype)

    if axis == 1:
        out_shape, o_map = (r, c * N_CHIPS), (lambda i, s: (i, s[0]))
    else:
        out_shape, o_map = (r * N_CHIPS, c), (lambda i, s: (s[0] * nrb + i, 0))
    in_specs = [pl.BlockSpec((None, tr, c), lambda i, s: (l, i, 0))]
    args = [chip, shard]
    if dep is not None:
        in_specs.append(pl.BlockSpec(memory_space=pl.ANY))
        args.append(dep)
    return pl.pallas_call(
        body, name="place_quarter", out_shape=jax.ShapeDtypeStruct(out_shape, dtype),
        grid_spec=pltpu.PrefetchScalarGridSpec(
            num_scalar_prefetch=1, grid=(nrb,), in_specs=in_specs, out_specs=pl.BlockSpec((tr, c), o_map)),
        compiler_params=_cparams(("parallel",)),
    )(*args)


def _gather_copies(land_refs, send_sem, recv_sem, axes, landing_chip_of, first=0):
    x, y, c = _coords()
    out = []
    for j, land in enumerate(land_refs):
        width = land.shape[axes[j]] // N_CHIPS
        for f in (1, 2, 3):
            fx, fy = _CHIP_FLIPS[f]
            px, py = _flip(x, fx), _flip(y, fy)
            lx, ly = landing_chip_of(px, py)
            out.append(pltpu.make_async_remote_copy(
                src_ref=_shard_of(land, axes[j], 2 * x + y, width), dst_ref=_shard_of(land, axes[j], 2 * lx + ly, width),
                send_sem=send_sem.at[3 * (first + j) + f - 1], recv_sem=recv_sem.at[3 * (first + j) + f - 1],
                device_id=(px, py, c), device_id_type=MESH))
    return out


def _gather_start(tag, lands, axes, dep=None):
    n = len(lands)

    def body(*refs):
        land_refs, send_sem, recv_sem = refs[:n], refs[-3], refs[-2]
        x, y, _ = _coords()
        for cp in _gather_copies(land_refs, send_sem, recv_sem, axes, lambda px, py: (x, y)):
            cp.start()
        refs[-1][...] = jnp.zeros_like(refs[-1])

    thru = [pltpu.HBM(a.shape, a.dtype) for a in lands]
    outs = pl.pallas_call(
        body, name=f"gather_start_{tag}",
        out_shape=tuple(thru + [pltpu.SemaphoreType.DMA((3 * n,)), pltpu.SemaphoreType.DMA((3 * n,)),
                                jax.ShapeDtypeStruct((SUBLANES, LANES), F32)]),
        in_specs=[_HBM] * n + ([_ANY] if dep is not None else []),
        out_specs=tuple([_HBM] * n + [_SEM, _SEM, pl.BlockSpec(memory_space=pltpu.VMEM)]),
        input_output_aliases={i: i for i in range(n)},
        compiler_params=pltpu.CompilerParams(has_side_effects=_DATAFLOW),
    )(*[_in_hbm(a) for a in lands], *([dep] if dep is not None else []))
    return list(outs[:n]), outs[n], outs[n + 1], outs[n + 2]


def _gather_wait(tag, lands, send_sem, recv_sem, axes, after, first=0):
    n = len(lands)

    def body(*refs):
        for cp in _gather_copies(refs[:n], refs[n], refs[n + 1], axes, lambda px, py: (px, py), first):
            cp.wait_send()
            cp.wait_recv()

    outs = pl.pallas_call(
        body, name=f"gather_wait_{tag}", out_shape=tuple(pltpu.HBM(a.shape, a.dtype) for a in lands),
        in_specs=[_HBM] * n + [_SEM, _SEM, _ANY], out_specs=tuple([_HBM] * n),
        input_output_aliases={i: i for i in range(n)},
        compiler_params=pltpu.CompilerParams(has_side_effects=_DATAFLOW),
    )(*lands, send_sem, recv_sem, after)
    return list(outs)


N_SLOTS = N_DEV - 1


def _scatter_sends(grad_refs, land_refs, send_sem, recv_sem, meta):
    x, y, c = _coords()
    out = []
    for j, (axis, owner, q, width) in enumerate(meta):
        other = c if owner == 0 else 1 - c
        for f, (fx, fy) in enumerate(_CHIP_FLIPS):
            px, py = _flip(x, fx), _flip(y, fy)
            slot = f + 4 * other - 1
            out.append((other if f == 0 else None, pltpu.make_async_remote_copy(
                src_ref=_shard_of(grad_refs[j], axis, 2 * px + py, width), dst_ref=land_refs[j].at[q, slot],
                send_sem=send_sem.at[4 * j + f], recv_sem=recv_sem.at[N_SLOTS * j + slot],
                device_id=(px, py, owner), device_id_type=MESH)))
    return out


def _scatter_start(layer, grads, lands, meta):
    n = len(grads)
    uniq = []
    for a in lands:
        if not any(a is u for u in uniq):
            uniq.append(a)
    which = [next(k for k, u in enumerate(uniq) if u is a) for a in lands]
    nu = len(uniq)

    def body(*refs):
        grad_refs, land_u = refs[:n], refs[n:n + nu]
        send_sem, recv_sem = refs[n + nu], refs[n + nu + 1]
        for other, cp in _scatter_sends(grad_refs, [land_u[k] for k in which], send_sem, recv_sem, meta):
            if other is None:
                cp.start()
            else:
                pl.when(other == 1)(cp.start)
        refs[-1][...] = jnp.zeros_like(refs[-1])

    thru = [pltpu.HBM(a.shape, a.dtype) for a in list(grads) + uniq]
    outs = pl.pallas_call(
        body, name=f"scatter_start_{layer}",
        out_shape=tuple([pltpu.SemaphoreType.DMA((4 * n,)), pltpu.SemaphoreType.DMA((N_SLOTS * n,))] + thru
                        + [jax.ShapeDtypeStruct((SUBLANES, LANES), F32)]),
        in_specs=[_HBM] * (n + nu),
        out_specs=tuple([_SEM, _SEM] + [_HBM] * (n + nu) + [pl.BlockSpec(memory_space=pltpu.VMEM)]),
        input_output_aliases={i: 2 + i for i in range(n + nu)},
        compiler_params=pltpu.CompilerParams(has_side_effects=_DATAFLOW),
    )(*[_in_hbm(a) for a in list(grads) + uniq])
    new_lands = [outs[2 + n + k] for k in which]
    return outs[0], outs[1], list(outs[2:2 + n]), new_lands, outs[-1]


def _scatter_wait(started, lands):
    nl = len(lands)
    flat_grads = [g for s in started for g in s[2]]
    ng, ns = len(flat_grads), len(started)

    def body(*refs):
        land_refs = refs[:nl]
        grad_refs = refs[nl:nl + ng]
        sem_refs = refs[nl + ng:nl + ng + 2 * ns]
        _, _, c = _coords()
        off = 0
        for k, (_, _, grads, idx, meta) in enumerate(started):
            send_sem, recv_sem = sem_refs[2 * k], sem_refs[2 * k + 1]
            lr = [land_refs[i] for i in idx]
            for other, cp in _scatter_sends(grad_refs[off:off + len(grads)], lr, send_sem, recv_sem, meta):
                if other is None:
                    cp.wait_send()
                else:
                    pl.when(other == 1)(cp.wait_send)
            for j, (axis, owner, q, width) in enumerate(meta):
                mine = (c if owner == 0 else 1 - c) == 0

                @pl.when(mine)
                def _():
                    for slot in range(N_SLOTS):
                        land = lr[j].at[q, slot]
                        pltpu.make_async_remote_copy(
                            src_ref=land, dst_ref=land, send_sem=send_sem.at[0], recv_sem=recv_sem.at[N_SLOTS * j + slot],
                            device_id=_coords(), device_id_type=MESH).wait_recv()
            off += len(grads)

    args = list(lands) + flat_grads
    thru = [pltpu.HBM(a.shape, a.dtype) for a in args]
    sems = [s for st in started for s in st[:2]]
    outs = pl.pallas_call(
        body, name="scatter_wait", out_shape=tuple(thru), in_specs=[_HBM] * (nl + ng) + [_SEM] * (2 * ns),
        out_specs=tuple([_HBM] * (nl + ng)), input_output_aliases={i: i for i in range(nl + ng)},
        compiler_params=pltpu.CompilerParams(has_side_effects=_DATAFLOW),
    )(*args, *sems)
    return list(outs[:nl]), list(outs[nl:])


def _sum_and_share(recv, layer_grads, axis, chip, name, dep=None):
    n, ns, r, c = recv.shape
    tr = _pick(r, prefs=(256, 128, 64, 32, 16))
    nr = r // tr
    nsteps = n * nr
    nlay = len(layer_grads)
    own_map = (lambda h, i, s: (i, s[0])) if axis == 1 else (lambda h, i, s: (s[0] * nr + i, 0))

    def body(chip_ref, i_ref, *rest):
        g_refs = rest[:nlay]
        o_ref, buf, loc_sems, send_sems, recv_sems = rest[nlay + (dep is not None):]
        h, i = pl.program_id(0), pl.program_id(1)
        step = h * nr + i
        slot = step % 2
        x, y, core = _coords()
        layer = core * n + h
        own = g_refs[0][...]
        for l in range(1, nlay):
            own = jnp.where(layer == l, g_refs[l][...], own)

        def copies(sl):
            dst = o_ref.at[core * n + h, pl.ds(pl.multiple_of(i * tr, tr), tr), :]
            loc = pltpu.make_async_copy(buf.at[sl], dst, loc_sems.at[sl])
            rem = pltpu.make_async_remote_copy(
                src_ref=buf.at[sl], dst_ref=dst, send_sem=send_sems.at[sl], recv_sem=recv_sems.at[step],
                device_id=(x, y, 1 - core), device_id_type=MESH)
            return loc, rem

        def drain(sl):
            loc, rem = copies(sl)
            loc.wait()
            rem.wait_send()

        pl.when(step >= 2)(lambda: drain(slot))
        acc = own.astype(F32)
        for s in range(ns):
            acc = acc + i_ref[s].astype(F32)
        buf[slot] = acc
        loc, rem = copies(slot)
        loc.start()
        rem.start()

        @pl.when(step == nsteps - 1)
        def _():
            drain(slot)
            if nsteps > 1:
                drain(1 - slot)
            for hh in range(n):
                for ii in range(nr):
                    land = o_ref.at[(1 - core) * n + hh, pl.ds(ii * tr, tr), :]
                    pltpu.make_async_remote_copy(
                        src_ref=buf.at[0], dst_ref=land, send_sem=send_sems.at[0], recv_sem=recv_sems.at[hh * nr + ii],
                        device_id=(x, y, 1 - core), device_id_type=MESH).wait_recv()

    return pl.pallas_call(
        body, name=name, out_shape=jax.ShapeDtypeStruct((2 * n, r, c), F32),
        grid_spec=pltpu.PrefetchScalarGridSpec(
            num_scalar_prefetch=1, grid=(n, nr),
            in_specs=[pl.BlockSpec((None, ns, tr, c), lambda h, i, s: (h, 0, i, 0))]
            + [pl.BlockSpec((tr, c), own_map)] * nlay + ([pl.BlockSpec(memory_space=pl.ANY)] if dep is not None else []),
            out_specs=_HBM,
            scratch_shapes=[pltpu.VMEM((2, tr, c), F32), pltpu.SemaphoreType.DMA((2,)),
                            pltpu.SemaphoreType.DMA((2,)), pltpu.SemaphoreType.DMA((nsteps,))]),
        compiler_params=_cparams(("arbitrary", "arbitrary")),
    )(chip, recv, *layer_grads, *([dep] if dep is not None else []))


def _adamw_update(w_ref, g_ref, m_ref, v_ref, d_ref, mo_ref, vo_ref):
    bc1 = 1.0 - ADAM_B1 ** ADAM_STEP
    bc2 = 1.0 - ADAM_B2 ** ADAM_STEP
    gv = g_ref[...]
    mn = ADAM_B1 * m_ref[...] + (1.0 - ADAM_B1) * gv
    vn = ADAM_B2 * v_ref[...] + (1.0 - ADAM_B2) * (gv * gv)
    d_ref[...] = -ADAM_LR * ((mn / bc1) / (jnp.sqrt(vn / bc2) + ADAM_EPS) + ADAM_WD * w_ref[...])
    mo_ref[...] = mn
    vo_ref[...] = vn


def _adamw(w, g, m, v, name):
    def body(*refs):
        _adamw_update(*refs)

    tr = _pick(w.shape[0], prefs=(256, 128, 64, 32, 16, 8))
    blk = pl.BlockSpec((tr, w.shape[1]), lambda i: (i, 0))
    sds = jax.ShapeDtypeStruct(w.shape, F32)
    return pl.pallas_call(
        body, name=name, out_shape=(sds, sds, sds), grid=(w.shape[0] // tr,), in_specs=[blk] * 4,
        out_specs=(blk,) * 3, compiler_params=_cparams(("parallel",)),
    )(w, g, m, v)


def _adamw_many(tensors, name, by_layer=False):
    n = len(tensors)

    def body(*refs):
        for t in range(n):
            _adamw_update(*refs[4 * t:4 * t + 4], *refs[4 * n + 3 * t:4 * n + 3 * t + 3])

    def spec(a):
        nd = a.ndim
        if by_layer:
            return pl.BlockSpec((1,) + a.shape[1:], lambda i: (i,) + (0,) * (nd - 1))
        return pl.BlockSpec(a.shape, lambda i: (0,) * nd)

    steps = tensors[0][0].shape[0] if by_layer else 1
    outs = pl.pallas_call(
        body, name=name, out_shape=tuple(jax.ShapeDtypeStruct(t[0].shape, F32) for t in tensors for _ in range(3)),
        grid=(steps,), in_specs=[spec(a) for t in tensors for a in t],
        out_specs=tuple(spec(t[0]) for t in tensors for _ in range(3)), compiler_params=_cparams(("parallel",)),
    )(*[a for t in tensors for a in t])
    return [tuple(outs[3 * t:3 * t + 3]) for t in range(n)]


_PACK_QUANTUM = 256 * LANES


def _pack(arrs):
    flat = jnp.concatenate([a.reshape(-1).astype(F32) for a in arrs])
    flat = jnp.pad(flat, (0, (-flat.shape[0]) % _PACK_QUANTUM))
    return flat.reshape(-1, LANES)


def _unpack(p, shapes):
    flat = p.reshape(-1)
    out, off = [], 0
    for s in shapes:
        n = int(np.prod(s))
        out.append(flat[off:off + n].reshape(s))
        off += n
    return out


def kernel(*args):
    nw = len(WEIGHTS)
    x, tgt = args[0], args[1 + nw]
    w = dict(zip(WEIGHTS, args[1:1 + nw]))
    m = dict(zip(WEIGHTS, args[2 + nw:2 + 2 * nw]))
    v = dict(zip(WEIGHTS, args[2 + 2 * nw:2 + 3 * nw]))
    _, L, D = x.shape
    chip = 2 * lax.axis_index("x") + lax.axis_index("y")

    big = list(BIG)
    small_sh_shapes = [w[n].shape for n in SMALL_SHARDED]
    nbig = len(big)
    chip1 = chip.reshape(1).astype(jnp.int32)
    axes2 = [BIG[n] - 1 for n in big] + [0]
    shards = [w[n] for n in big] + [_pack([w[n] for n in SMALL_SHARDED])[None]]
    pairs = [(t, l) for t in range(nbig + 1) for l in range(shards[t].shape[0])]
    depth = w['norm_mix_g'].shape[0]
    part_of = lambda t, l: 0 if t == nbig else 2 * _model_layer(big[t], l) + big[t].startswith('ffn')
    flying, token = {}, None
    for tag, gset in enumerate(([0], list(range(1, 2 * depth)))):
        ids = [k for g in gset for k, (t, l) in enumerate(pairs) if part_of(t, l) == g]
        ts = [pairs[k][0] for k in ids]
        placed = [_place_quarter(shards[t], pairs[k][1], axes2[t], chip1, F32 if t == nbig else BF16, token)
                  for k, t in zip(ids, ts)]
        lands, send, recv, token = _gather_start(tag, placed, [axes2[t] for t in ts], token)
        first = 0
        for g in gset:
            n = sum(1 for t, l in pairs if part_of(t, l) == g)
            flying[g] = (ts[first:first + n], lands[first:first + n], send, recv, first)
            first += n

    def wait_group(g, after):
        ts, lands, send, recv, first = flying[g]
        landed = _gather_wait(g, lands, send, recv, [axes2[t] for t in ts], token if after is None else after, first)
        return dict(zip(ts, landed))

    first = wait_group(0, None)
    packed = first.pop(nbig).reshape(N_CHIPS, -1, LANES)
    per_chip = [_unpack(packed[s], small_sh_shapes) for s in range(N_CHIPS)]
    wl = dict(w)
    for k, n in enumerate(SMALL_SHARDED):
        wl[n] = jnp.concatenate([per_chip[s][k] for s in range(N_CHIPS)], axis=-1)

    def layer_weights(i, after):
        got = first if i == 0 else wait_group(i, after)
        return {big[t]: a for t, a in got.items()}

    small_shapes = [(w[n].shape[:-1] + (w[n].shape[-1] * N_CHIPS,)) if n in SMALL_SHARDED else w[n].shape
                    for n in SMALL] + [(1,)]
    n_small = sum(int(np.prod(s)) for s in small_shapes)
    pack_rows = -(-n_small // _PACK_QUANTUM) * _PACK_QUANTUM // LANES
    nlayers = [w[n].shape[0] for n in big] + [2]
    halves = [n // 2 for n in nlayers]
    quarters = [tuple(w[n].shape[1:]) for n in big] + [(pack_rows // 2 // N_CHIPS, LANES)]
    wire = [BF16] * nbig + [F32]
    land_now = [lax.empty((halves[t], N_SLOTS) + quarters[t], wire[t]) for t in range(nbig + 1)]
    gparts = [[None] * n for n in nlayers]
    started = []

    def start_scatter(tag, ts, ls, arrays):
        meta = [(axes2[t], l // halves[t], l % halves[t], quarters[t][axes2[t]]) for t, l in zip(ts, ls)]
        send, recv, thru, new_lands, token = _scatter_start(tag, arrays, [land_now[t] for t in ts], meta)
        for t, ln in zip(ts, new_lands):
            land_now[t] = ln
        started.append((send, recv, thru, ts, meta, ls))
        return token

    def on_layer_grads(g, gb):
        ts = [big.index(n) for n in gb]
        return start_scatter(g, ts, [g // 2 if big[t].startswith('ffn') else g // 4 for t in ts],
                             [gb[big[t]] for t in ts])

    loss, dx, gsmall = _local_step(x.reshape(L, D), tgt.reshape(L, D), wl, layer_weights, on_layer_grads)
    gpack = _pack([gsmall[n] for n in SMALL] + [loss.reshape(1)])
    start_scatter(2 * depth, [nbig, nbig], [0, 1], [gpack[:pack_rows // 2], gpack[pack_rows // 2:]])
    landed, sent = _scatter_wait([s[:5] for s in started], land_now)
    for (t, l), g in zip([(t, l) for s in started for t, l in zip(s[3], s[5])], sent):
        gparts[t][l] = g
    small_sum = _sum_and_share(landed[nbig], gparts[nbig], 0, chip1, "sum_share_small")
    quarter_rows = small_sum.shape[0] * small_sum.shape[1]
    placed = _place_quarter(small_sum.reshape(1, quarter_rows, LANES), 0, 0, chip1, F32)
    flying_small, send, recv, token = _gather_start("small", [placed], [0])
    gshard = {n: _sum_and_share(landed[t], gparts[t], axes2[t], chip1, "sum_share_" + n, token)
              for t, n in enumerate(big)}
    small_all = _gather_wait("small", flying_small, send, recv, [0], gshard[big[-1]])[0]
    gpack = small_all.reshape(N_CHIPS, 2, quarter_rows // 2, LANES).transpose(1, 0, 2, 3).reshape(pack_rows, LANES)
    gs = dict(zip(SMALL + ['loss'], _unpack(gpack, small_shapes)))
    loss = gs.pop('loss').reshape(())
    for n in SMALL_SHARDED:
        width = w[n].shape[-1]
        gs[n] = lax.dynamic_slice_in_dim(gs[n], chip * width, width, axis=gs[n].ndim - 1)

    grads, delta, new_m, new_v = {}, {}, {}, {}
    for n in big:
        shp = w[n].shape
        flat = lambda a: a.reshape(shp[0] * shp[1], shp[2])
        g = gshard[n]
        grads[n] = g
        d_, m_, v_ = _adamw(flat(w[n]), flat(g), flat(m[n]), flat(v[n]), "adamw_" + n)
        delta[n], new_m[n], new_v[n] = d_.reshape(shp), m_.reshape(shp), v_.reshape(shp)
    sparse = [n for n in SMALL if w[n].ndim == 4 and w[n].shape[-1] < LANES // 2]
    for names, by_layer in ((sparse, True), ([n for n in SMALL if n not in sparse], False)):
        as2d = lambda a: a.reshape(1, -1) if a.ndim == 1 else a
        res = _adamw_many([(as2d(w[n]), as2d(gs[n]), as2d(m[n]), as2d(v[n])) for n in names],
                          "adamw_small_by_layer" if by_layer else "adamw_small", by_layer)
        for n, (d_, m_, v_) in zip(names, res):
            shp = w[n].shape
            grads[n], delta[n], new_m[n], new_v[n] = gs[n], d_.reshape(shp), m_.reshape(shp), v_.reshape(shp)

    return (loss, dx.reshape(1, L, D), *[grads[n] for n in WEIGHTS], *[delta[n] for n in WEIGHTS],
            *[new_m[n] for n in WEIGHTS], *[new_v[n] for n in WEIGHTS])
```
